```python
import math
import jax, jax.numpy as jnp
from jax import lax
import numpy as np

D_MODEL = 1024
BATCH = 8
SEQ = 8192
DEPTH = 1

HEAD_DIM = 64
N_Q_HEADS = 8
N_KV_HEADS = 2
Q_PER_KV = N_Q_HEADS // N_KV_HEADS
ATTN_WIDTH = N_Q_HEADS * HEAD_DIM
KV_WIDTH = N_KV_HEADS * HEAD_DIM
WINDOW = 128
ATTN_BLOCK = 128
ROPE_THETA = 10000.0
SSM_HEADS = 8
SSM_HEAD_DIM = 64
SSM_WIDTH = SSM_HEADS * SSM_HEAD_DIM
SSM_GROUPS = 2
HEADS_PER_GROUP = SSM_HEADS // SSM_GROUPS
D_STATE = 128
CONV_WIDTH = 4
CHUNK = 128
CONV_CH = SSM_WIDTH + 2 * SSM_GROUPS * D_STATE
MIX_WIDTH = ATTN_WIDTH + SSM_WIDTH
IN_PROJ = ATTN_WIDTH + 2 * KV_WIDTH + SSM_WIDTH + CONV_CH + SSM_HEADS
D_FF = -(-8 * D_MODEL // (3 * 256)) * 256
N_MOD = 6
EPS = 1e-6

kernel_name = "hymba_swa_sink_ssd_adaln_layer"


def rmsnorm(x, w):
    xf = x.astype(jnp.float32)
    y = xf * lax.rsqrt(jnp.mean(xf * xf, axis=-1, keepdims=True) + EPS)
    return (y * w.astype(jnp.float32)).astype(x.dtype)


def modulate(h, shift, scale):
    return h * (1.0 + scale[:, None, :]) + shift[:, None, :]


def rope(t, positions):
    half = HEAD_DIM // 2
    inv_freq = ROPE_THETA ** (-jnp.arange(half, dtype=jnp.float32) / half)
    ang = positions.astype(jnp.float32)[..., None] * inv_freq
    cos = jnp.cos(ang)[:, :, None, :]
    sin = jnp.sin(ang)[:, :, None, :]
    tf = t.astype(jnp.float32)
    t1, t2 = tf[..., :half], tf[..., half:]
    out = jnp.concatenate([t1 * cos - t2 * sin, t2 * cos + t1 * sin], axis=-1)
    return out.astype(t.dtype)


def sliding_window_attention(q, k, v, sinks):
    b, s = q.shape[0], q.shape[1]
    nb = s // ATTN_BLOCK
    qb = q.reshape(b, nb, ATTN_BLOCK, N_KV_HEADS, Q_PER_KV, HEAD_DIM)

    def band(t):
        tb = t.reshape(b, nb, ATTN_BLOCK, N_KV_HEADS, HEAD_DIM)
        prev = jnp.pad(tb, ((0, 0), (1, 0), (0, 0), (0, 0), (0, 0)))[:, :-1]
        return jnp.concatenate([prev, tb], axis=2)

    kb, vb = band(k), band(v)
    scores = jnp.einsum('bnqhgd,bnkhd->bnhgqk', qb, kb).astype(jnp.float32)
    scores = scores * (1.0 / math.sqrt(HEAD_DIM))
    blk = jnp.arange(nb)[:, None] * ATTN_BLOCK
    qpos = blk + jnp.arange(ATTN_BLOCK)[None, :]
    kpos = blk - ATTN_BLOCK + jnp.arange(2 * ATTN_BLOCK)[None, :]
    diff = qpos[:, :, None] - kpos[:, None, :]
    mask = (diff >= 0) & (diff < WINDOW) & (kpos[:, None, :] >= 0)
    scores = jnp.where(mask[None, :, None, None], scores, -jnp.inf)
    sink = sinks.astype(jnp.float32).reshape(N_KV_HEADS, Q_PER_KV)[None, None, :, :, None, None]
    m = jnp.maximum(jnp.max(scores, axis=-1, keepdims=True), sink)
    p = jnp.exp(scores - m)
    denom = jnp.sum(p, axis=-1, keepdims=True) + jnp.exp(sink - m)
    probs = (p / denom).astype(v.dtype)
    out = jnp.einsum('bnhgqk,bnkhd->bnqhgd', probs, vb)
    return out.reshape(b, s, ATTN_WIDTH)


def causal_depthwise_conv(u, w, bias):
    out = lax.conv_general_dilated(
        u, w[:, None, :].astype(u.dtype), window_strides=(1,),
        padding=[(CONV_WIDTH - 1, 0)], dimension_numbers=('NWC', 'WIO', 'NWC'),
        feature_group_count=u.shape[-1])
    return out + bias.astype(u.dtype)


def ssd_chunked_scan(xs, dt, A, Bm, Cm, d_skip):
    b, s = xs.shape[0], xs.shape[1]
    nc = s // CHUNK
    xf = xs.astype(jnp.float32)
    xdt = (xf * dt[..., None]).reshape(b, nc, CHUNK, SSM_GROUPS, HEADS_PER_GROUP, SSM_HEAD_DIM)
    a = (dt * A).reshape(b, nc, CHUNK, SSM_GROUPS, HEADS_PER_GROUP)
    a_cs = jnp.cumsum(a, axis=2)
    Bc = Bm.astype(jnp.float32).reshape(b, nc, CHUNK, SSM_GROUPS, D_STATE)
    Cc = Cm.astype(jnp.float32).reshape(b, nc, CHUNK, SSM_GROUPS, D_STATE)
    causal = jnp.tril(jnp.ones((CHUNK, CHUNK), dtype=bool))[None, None, :, :, None, None]
    seg = a_cs[:, :, :, None] - a_cs[:, :, None, :]
    decay = jnp.exp(jnp.where(causal, seg, -jnp.inf))
    cb = jnp.einsum('bclgn,bcsgn->bclsg', Cc, Bc)
    y_diag = jnp.einsum('bclsg,bclsgj,bcsgjp->bclgjp', cb, decay, xdt)
    decay_to_end = jnp.exp(a_cs[:, :, -1:] - a_cs)
    states = jnp.einsum('bclgn,bclgj,bclgjp->bcgjpn', Bc, decay_to_end, xdt)
    chunk_decay = jnp.exp(a_cs[:, :, -1])

    def step(h, inp):
        st, dec = inp
        return h * dec[..., None, None] + st, h

    init = jnp.zeros((b, SSM_GROUPS, HEADS_PER_GROUP, SSM_HEAD_DIM, D_STATE), jnp.float32)
    _, prev = lax.scan(step, init, (jnp.moveaxis(states, 1, 0), jnp.moveaxis(chunk_decay, 1, 0)))
    prev = jnp.moveaxis(prev, 0, 1)
    y_off = jnp.einsum('bclgn,bcgjpn,bclgj->bclgjp', Cc, prev, jnp.exp(a_cs))
    y = (y_diag + y_off).reshape(b, s, SSM_HEADS, SSM_HEAD_DIM)
    return y + xf * d_skip.astype(jnp.float32)[:, None]


def hybrid_mixer(h, positions, w_in, conv_w, conv_b, dt_bias, a_log, d_skip, sinks, ssm_norm_w, w_out):
    b, s = h.shape[0], h.shape[1]
    proj = h @ w_in
    o1 = ATTN_WIDTH
    o2 = o1 + KV_WIDTH
    o3 = o2 + KV_WIDTH
    o4 = o3 + SSM_WIDTH
    o5 = o4 + CONV_CH
    q, k, v, z, xbc, dt_raw = jnp.split(proj, [o1, o2, o3, o4, o5], axis=-1)
    q = rope(q.reshape(b, s, N_Q_HEADS, HEAD_DIM), positions)
    k = rope(k.reshape(b, s, N_KV_HEADS, HEAD_DIM), positions)
    v = v.reshape(b, s, N_KV_HEADS, HEAD_DIM)
    attn = sliding_window_attention(q, k, v, sinks)
    xbc = jax.nn.silu(causal_depthwise_conv(xbc, conv_w, conv_b))
    xs, Bm, Cm = jnp.split(xbc, [SSM_WIDTH, SSM_WIDTH + SSM_GROUPS * D_STATE], axis=-1)
    xs = xs.reshape(b, s, SSM_HEADS, SSM_HEAD_DIM)
    Bm = Bm.reshape(b, s, SSM_GROUPS, D_STATE)
    Cm = Cm.reshape(b, s, SSM_GROUPS, D_STATE)
    dt = jax.nn.softplus(dt_raw.astype(jnp.float32) + dt_bias.astype(jnp.float32))
    A = -jnp.exp(a_log.astype(jnp.float32))
    y = ssd_chunked_scan(xs, dt, A, Bm, Cm, d_skip).reshape(b, s, SSM_WIDTH)
    y = y * jax.nn.silu(z.astype(jnp.float32))
    yg = y.reshape(b, s, SSM_GROUPS, SSM_WIDTH // SSM_GROUPS)
    yg = yg * lax.rsqrt(jnp.mean(yg * yg, axis=-1, keepdims=True) + EPS)
    y = (yg.reshape(b, s, SSM_WIDTH) * ssm_norm_w.astype(jnp.float32)).astype(h.dtype)
    return jnp.concatenate([attn.astype(h.dtype), y], axis=-1) @ w_out


def swiglu(h, w_gate_up, w_down):
    gu = h @ w_gate_up
    g, u = jnp.split(gu, 2, axis=-1)
    return (jax.nn.silu(g) * u) @ w_down


def _fwd_setup_inputs(seed: int = 0) -> dict:
    key = jax.random.key(seed)
    ks = jax.random.split(key, 20)
    f32 = jnp.float32
    L = DEPTH
    x = jax.random.normal(ks[0], (BATCH, SEQ, D_MODEL), f32)
    c = jax.random.normal(ks[1], (BATCH, D_MODEL), f32)
    offset = jax.random.randint(ks[2], (BATCH, 1), 0, 4096, dtype=jnp.int32)
    positions = (jnp.arange(SEQ, dtype=jnp.int32)[None, :] + offset).astype(jnp.int32)
    w_ada = jax.random.normal(ks[3], (L, D_MODEL, N_MOD * D_MODEL), f32) * (0.5 * D_MODEL ** -0.5)
    b_ada = 0.01 * jax.random.normal(ks[4], (L, N_MOD * D_MODEL), f32)
    norm1_w = 1.0 + 0.02 * jax.random.normal(ks[5], (L, D_MODEL), f32)
    w_in = jax.random.normal(ks[6], (L, D_MODEL, IN_PROJ), f32) * D_MODEL ** -0.5
    conv_w = jax.random.normal(ks[7], (L, CONV_WIDTH, CONV_CH), f32) * CONV_WIDTH ** -0.5
    conv_b = 0.01 * jax.random.normal(ks[8], (L, CONV_CH), f32)
    dt0 = jnp.exp(jax.random.uniform(ks[9], (L, SSM_HEADS), f32, math.log(1e-3), math.log(1e-1)))
    dt_bias = dt0 + jnp.log(-jnp.expm1(-dt0))
    a_log = jnp.log(jax.random.uniform(ks[10], (L, SSM_HEADS), f32, 1.0, 16.0))
    d_skip = 1.0 + 0.1 * jax.random.normal(ks[11], (L, SSM_HEADS), f32)
    attn_sinks = jax.random.normal(ks[12], (L, N_Q_HEADS), f32)
    ssm_norm_w = 1.0 + 0.02 * jax.random.normal(ks[13], (L, SSM_WIDTH), f32)
    w_out = jax.random.normal(ks[14], (L, MIX_WIDTH, D_MODEL), f32) * MIX_WIDTH ** -0.5
    norm2_w = 1.0 + 0.02 * jax.random.normal(ks[15], (L, D_MODEL), f32)
    w_gate_up = jax.random.normal(ks[16], (L, D_MODEL, 2 * D_FF), f32) * D_MODEL ** -0.5
    w_down = jax.random.normal(ks[17], (L, D_FF, D_MODEL), f32) * D_FF ** -0.5
    final_norm_w = 1.0 + 0.02 * jax.random.normal(ks[18], (D_MODEL,), f32)
    return {"x": x, "c": c, "positions": positions, "w_ada": w_ada, "b_ada": b_ada,
            "norm1_w": norm1_w, "w_in": w_in, "conv_w": conv_w, "conv_b": conv_b,
            "dt_bias": dt_bias, "a_log": a_log, "d_skip": d_skip, "attn_sinks": attn_sinks,
            "ssm_norm_w": ssm_norm_w, "w_out": w_out, "norm2_w": norm2_w,
            "w_gate_up": w_gate_up, "w_down": w_down, "final_norm_w": final_norm_w}


def _fwd_reference(x, c, positions, w_ada, b_ada, norm1_w, w_in, conv_w, conv_b, dt_bias, a_log,
              d_skip, attn_sinks, ssm_norm_w, w_out, norm2_w, w_gate_up, w_down, final_norm_w):
    for layer in range(DEPTH):
        mod = jax.nn.silu(c) @ w_ada[layer] + b_ada[layer]
        shift1, scale1, gate1, shift2, scale2, gate2 = jnp.split(mod, N_MOD, axis=-1)
        h = modulate(rmsnorm(x, norm1_w[layer]), shift1, scale1)
        x = x + gate1[:, None, :] * hybrid_mixer(
            h, positions, w_in[layer], conv_w[layer], conv_b[layer], dt_bias[layer],
            a_log[layer], d_skip[layer], attn_sinks[layer], ssm_norm_w[layer], w_out[layer])
        h = modulate(rmsnorm(x, norm2_w[layer]), shift2, scale2)
        x = x + gate2[:, None, :] * swiglu(h, w_gate_up[layer], w_down[layer])
    return rmsnorm(x, final_norm_w)


import jax as _jax
import jax.numpy as _jnp

TWIN_FORMAT = 'train_step'
FWD_PARAMS = ['x', 'c', 'positions', 'w_ada', 'b_ada', 'norm1_w', 'w_in', 'conv_w', 'conv_b', 'dt_bias', 'a_log', 'd_skip', 'attn_sinks', 'ssm_norm_w', 'w_out', 'norm2_w', 'w_gate_up', 'w_down', 'final_norm_w']
TWIN_WEIGHTS = ['w_ada', 'b_ada', 'norm1_w', 'w_in', 'conv_w', 'conv_b', 'dt_bias', 'a_log', 'd_skip', 'attn_sinks', 'ssm_norm_w', 'w_out', 'norm2_w', 'w_gate_up', 'w_down', 'final_norm_w']
TWIN_DIFF_INPUT = 'x'
TWIN_INPUTS = ['x', 'c', 'positions', 'w_ada', 'b_ada', 'norm1_w', 'w_in', 'conv_w', 'conv_b', 'dt_bias', 'a_log', 'd_skip', 'attn_sinks', 'ssm_norm_w', 'w_out', 'norm2_w', 'w_gate_up', 'w_down', 'final_norm_w', 'loss_target', 'm_w_ada', 'm_b_ada', 'm_norm1_w', 'm_w_in', 'm_conv_w', 'm_conv_b', 'm_dt_bias', 'm_a_log', 'm_d_skip', 'm_attn_sinks', 'm_ssm_norm_w', 'm_w_out', 'm_norm2_w', 'm_w_gate_up', 'm_w_down', 'm_final_norm_w', 'v_w_ada', 'v_b_ada', 'v_norm1_w', 'v_w_in', 'v_conv_w', 'v_conv_b', 'v_dt_bias', 'v_a_log', 'v_d_skip', 'v_attn_sinks', 'v_ssm_norm_w', 'v_w_out', 'v_norm2_w', 'v_w_gate_up', 'v_w_down', 'v_final_norm_w']
TWIN_OUTPUTS = ['loss', 'grad_x', 'grad_w_ada', 'grad_b_ada', 'grad_norm1_w', 'grad_w_in', 'grad_conv_w', 'grad_conv_b', 'grad_dt_bias', 'grad_a_log', 'grad_d_skip', 'grad_attn_sinks', 'grad_ssm_norm_w', 'grad_w_out', 'grad_norm2_w', 'grad_w_gate_up', 'grad_w_down', 'grad_final_norm_w', 'delta_w_ada', 'delta_b_ada', 'delta_norm1_w', 'delta_w_in', 'delta_conv_w', 'delta_conv_b', 'delta_dt_bias', 'delta_a_log', 'delta_d_skip', 'delta_attn_sinks', 'delta_ssm_norm_w', 'delta_w_out', 'delta_norm2_w', 'delta_w_gate_up', 'delta_w_down', 'delta_final_norm_w', 'new_m_w_ada', 'new_m_b_ada', 'new_m_norm1_w', 'new_m_w_in', 'new_m_conv_w', 'new_m_conv_b', 'new_m_dt_bias', 'new_m_a_log', 'new_m_d_skip', 'new_m_attn_sinks', 'new_m_ssm_norm_w', 'new_m_w_out', 'new_m_norm2_w', 'new_m_w_gate_up', 'new_m_w_down', 'new_m_final_norm_w', 'new_v_w_ada', 'new_v_b_ada', 'new_v_norm1_w', 'new_v_w_in', 'new_v_conv_w', 'new_v_conv_b', 'new_v_dt_bias', 'new_v_a_log', 'new_v_d_skip', 'new_v_attn_sinks', 'new_v_ssm_norm_w', 'new_v_w_out', 'new_v_norm2_w', 'new_v_w_gate_up', 'new_v_w_down', 'new_v_final_norm_w']
TWIN_LEAF_KINDS = {'loss': 'loss', 'grad_x': 'grad_x', 'grad_w_ada': 'grad_w', 'grad_b_ada': 'grad_w', 'grad_norm1_w': 'grad_w', 'grad_w_in': 'grad_w', 'grad_conv_w': 'grad_w', 'grad_conv_b': 'grad_w', 'grad_dt_bias': 'grad_w', 'grad_a_log': 'grad_w', 'grad_d_skip': 'grad_w', 'grad_attn_sinks': 'grad_w', 'grad_ssm_norm_w': 'grad_w', 'grad_w_out': 'grad_w', 'grad_norm2_w': 'grad_w', 'grad_w_gate_up': 'grad_w', 'grad_w_down': 'grad_w', 'grad_final_norm_w': 'grad_w', 'delta_w_ada': 'delta_w', 'delta_b_ada': 'delta_w', 'delta_norm1_w': 'delta_w', 'delta_w_in': 'delta_w', 'delta_conv_w': 'delta_w', 'delta_conv_b': 'delta_w', 'delta_dt_bias': 'delta_w', 'delta_a_log': 'delta_w', 'delta_d_skip': 'delta_w', 'delta_attn_sinks': 'delta_w', 'delta_ssm_norm_w': 'delta_w', 'delta_w_out': 'delta_w', 'delta_norm2_w': 'delta_w', 'delta_w_gate_up': 'delta_w', 'delta_w_down': 'delta_w', 'delta_final_norm_w': 'delta_w', 'new_m_w_ada': 'new_m', 'new_m_b_ada': 'new_m', 'new_m_norm1_w': 'new_m', 'new_m_w_in': 'new_m', 'new_m_conv_w': 'new_m', 'new_m_conv_b': 'new_m', 'new_m_dt_bias': 'new_m', 'new_m_a_log': 'new_m', 'new_m_d_skip': 'new_m', 'new_m_attn_sinks': 'new_m', 'new_m_ssm_norm_w': 'new_m', 'new_m_w_out': 'new_m', 'new_m_norm2_w': 'new_m', 'new_m_w_gate_up': 'new_m', 'new_m_w_down': 'new_m', 'new_m_final_norm_w': 'new_m', 'new_v_w_ada': 'new_v', 'new_v_b_ada': 'new_v', 'new_v_norm1_w': 'new_v', 'new_v_w_in': 'new_v', 'new_v_conv_w': 'new_v', 'new_v_conv_b': 'new_v', 'new_v_dt_bias': 'new_v', 'new_v_a_log': 'new_v', 'new_v_d_skip': 'new_v', 'new_v_attn_sinks': 'new_v', 'new_v_ssm_norm_w': 'new_v', 'new_v_w_out': 'new_v', 'new_v_norm2_w': 'new_v', 'new_v_w_gate_up': 'new_v', 'new_v_w_down': 'new_v', 'new_v_final_norm_w': 'new_v'}


def _forward(args):
    return _fwd_reference(*[args[k] for k in FWD_PARAMS])


def _output_shape():
    def fwd():
        inp = _fwd_setup_inputs(0)
        return _fwd_reference(*[inp[k] for k in FWD_PARAMS])
    out = _jax.eval_shape(fwd)
    return out.shape, out.dtype

N_MICROBATCH = 1
ADAM_LR = 0.001
ADAM_B1 = 0.9
ADAM_B2 = 0.999
ADAM_EPS = 1e-08
ADAM_WD = 0.01
ADAM_STEP = 10
PER_EXAMPLE_BATCH_AXIS = {'x': 0, 'c': 0, 'positions': 0, 'loss_target': 0}
SHARED_INPUTS = []
_WEIGHT_DTYPES = {'w_ada': _jnp.float32, 'b_ada': _jnp.float32, 'norm1_w': _jnp.float32, 'w_in': _jnp.float32, 'conv_w': _jnp.float32, 'conv_b': _jnp.float32, 'dt_bias': _jnp.float32, 'a_log': _jnp.float32, 'd_skip': _jnp.float32, 'attn_sinks': _jnp.float32, 'ssm_norm_w': _jnp.float32, 'w_out': _jnp.float32, 'norm2_w': _jnp.float32, 'w_gate_up': _jnp.float32, 'w_down': _jnp.float32, 'final_norm_w': _jnp.float32}
MOMENT_SCALE = {'w_ada': 7.559983e-02, 'b_ada': 1.249351e-01, 'norm1_w': 7.869828e-02, 'w_in': 5.666148e-02, 'conv_w': 5.793975e-02, 'conv_b': 6.927861e-02, 'dt_bias': 1.507335e-01, 'a_log': 2.183354e-01, 'd_skip': 5.072754e-01, 'attn_sinks': 1.755925e-02, 'ssm_norm_w': 7.587392e-02, 'w_out': 5.705797e-02, 'norm2_w': 7.310371e-02, 'w_gate_up': 3.180807e-02, 'w_down': 5.196384e-02, 'final_norm_w': 6.400600e+01}


def _to_microbatches(a, axis):
    t = _jnp.moveaxis(a, axis, 0)
    t = t.reshape((N_MICROBATCH, t.shape[0] // N_MICROBATCH) + t.shape[1:])
    return _jnp.moveaxis(t, 1, axis + 1)


def setup_inputs(seed: int = 0) -> dict:
    inp = _fwd_setup_inputs(seed)
    key = _jax.random.fold_in(_jax.random.key(seed), 7919)
    shape, _ = _output_shape()
    out = dict(inp)
    out["loss_target"] = _jax.random.normal(_jax.random.fold_in(key, 0), shape, _jnp.float32)
    for i, name in enumerate(TWIN_WEIGHTS):
        w = inp[name].astype(_jnp.float32)
        if MOMENT_SCALE is None:
            s = _jnp.sqrt(_jnp.mean(_jnp.square(w)) + 1e-30)
        else:
            s = MOMENT_SCALE[name]
        km, kv = _jax.random.split(_jax.random.fold_in(key, i + 1))
        out[name] = w
        out["m_" + name] = s * _jax.random.normal(km, w.shape, _jnp.float32)
        out["v_" + name] = (s * s) * _jax.random.uniform(kv, w.shape, _jnp.float32, 0.5, 1.5)
    if N_MICROBATCH > 1:
        for name, axis in PER_EXAMPLE_BATCH_AXIS.items():
            out[name] = _to_microbatches(out[name], axis)
    return {'x': out['x'], 'c': out['c'], 'positions': out['positions'], 'w_ada': out['w_ada'], 'b_ada': out['b_ada'], 'norm1_w': out['norm1_w'], 'w_in': out['w_in'], 'conv_w': out['conv_w'], 'conv_b': out['conv_b'], 'dt_bias': out['dt_bias'], 'a_log': out['a_log'], 'd_skip': out['d_skip'], 'attn_sinks': out['attn_sinks'], 'ssm_norm_w': out['ssm_norm_w'], 'w_out': out['w_out'], 'norm2_w': out['norm2_w'], 'w_gate_up': out['w_gate_up'], 'w_down': out['w_down'], 'final_norm_w': out['final_norm_w'], 'loss_target': out['loss_target'], 'm_w_ada': out['m_w_ada'], 'm_b_ada': out['m_b_ada'], 'm_norm1_w': out['m_norm1_w'], 'm_w_in': out['m_w_in'], 'm_conv_w': out['m_conv_w'], 'm_conv_b': out['m_conv_b'], 'm_dt_bias': out['m_dt_bias'], 'm_a_log': out['m_a_log'], 'm_d_skip': out['m_d_skip'], 'm_attn_sinks': out['m_attn_sinks'], 'm_ssm_norm_w': out['m_ssm_norm_w'], 'm_w_out': out['m_w_out'], 'm_norm2_w': out['m_norm2_w'], 'm_w_gate_up': out['m_w_gate_up'], 'm_w_down': out['m_w_down'], 'm_final_norm_w': out['m_final_norm_w'], 'v_w_ada': out['v_w_ada'], 'v_b_ada': out['v_b_ada'], 'v_norm1_w': out['v_norm1_w'], 'v_w_in': out['v_w_in'], 'v_conv_w': out['v_conv_w'], 'v_conv_b': out['v_conv_b'], 'v_dt_bias': out['v_dt_bias'], 'v_a_log': out['v_a_log'], 'v_d_skip': out['v_d_skip'], 'v_attn_sinks': out['v_attn_sinks'], 'v_ssm_norm_w': out['v_ssm_norm_w'], 'v_w_out': out['v_w_out'], 'v_norm2_w': out['v_norm2_w'], 'v_w_gate_up': out['v_w_gate_up'], 'v_w_down': out['v_w_down'], 'v_final_norm_w': out['v_final_norm_w']}


def _loss(weights, diff, rest, loss_target):
    with _jax.named_scope("forward"):
        args = {**rest, TWIN_DIFF_INPUT: diff, **{k: w.astype(_WEIGHT_DTYPES[k]) for k, w in weights.items()}}
        y = _forward(args)
    with _jax.named_scope("loss_head"):
        err = _jnp.square(y.astype(_jnp.float32) - loss_target)
        return 0.5 * _jnp.sum(_jnp.mean(err, axis=-1)) if err.ndim else 0.5 * err


def _adamw(w, g, m, v):
    m = ADAM_B1 * m + (1.0 - ADAM_B1) * g
    v = ADAM_B2 * v + (1.0 - ADAM_B2) * _jnp.square(g)
    m_hat = m / (1.0 - ADAM_B1 ** ADAM_STEP)
    v_hat = v / (1.0 - ADAM_B2 ** ADAM_STEP)
    delta = -ADAM_LR * (m_hat / (_jnp.sqrt(v_hat) + ADAM_EPS) + ADAM_WD * w)
    return delta, m, v


def reference(x, c, positions, w_ada, b_ada, norm1_w, w_in, conv_w, conv_b, dt_bias, a_log, d_skip, attn_sinks, ssm_norm_w, w_out, norm2_w, w_gate_up, w_down, final_norm_w, loss_target, m_w_ada, m_b_ada, m_norm1_w, m_w_in, m_conv_w, m_conv_b, m_dt_bias, m_a_log, m_d_skip, m_attn_sinks, m_ssm_norm_w, m_w_out, m_norm2_w, m_w_gate_up, m_w_down, m_final_norm_w, v_w_ada, v_b_ada, v_norm1_w, v_w_in, v_conv_w, v_conv_b, v_dt_bias, v_a_log, v_d_skip, v_attn_sinks, v_ssm_norm_w, v_w_out, v_norm2_w, v_w_gate_up, v_w_down, v_final_norm_w):
    given = dict(x=x, c=c, positions=positions, w_ada=w_ada, b_ada=b_ada, norm1_w=norm1_w, w_in=w_in, conv_w=conv_w, conv_b=conv_b, dt_bias=dt_bias, a_log=a_log, d_skip=d_skip, attn_sinks=attn_sinks, ssm_norm_w=ssm_norm_w, w_out=w_out, norm2_w=norm2_w, w_gate_up=w_gate_up, w_down=w_down, final_norm_w=final_norm_w, loss_target=loss_target, m_w_ada=m_w_ada, m_b_ada=m_b_ada, m_norm1_w=m_norm1_w, m_w_in=m_w_in, m_conv_w=m_conv_w, m_conv_b=m_conv_b, m_dt_bias=m_dt_bias, m_a_log=m_a_log, m_d_skip=m_d_skip, m_attn_sinks=m_attn_sinks, m_ssm_norm_w=m_ssm_norm_w, m_w_out=m_w_out, m_norm2_w=m_norm2_w, m_w_gate_up=m_w_gate_up, m_w_down=m_w_down, m_final_norm_w=m_final_norm_w, v_w_ada=v_w_ada, v_b_ada=v_b_ada, v_norm1_w=v_norm1_w, v_w_in=v_w_in, v_conv_w=v_conv_w, v_conv_b=v_conv_b, v_dt_bias=v_dt_bias, v_a_log=v_a_log, v_d_skip=v_d_skip, v_attn_sinks=v_attn_sinks, v_ssm_norm_w=v_ssm_norm_w, v_w_out=v_w_out, v_norm2_w=v_norm2_w, v_w_gate_up=v_w_gate_up, v_w_down=v_w_down, v_final_norm_w=v_final_norm_w)
    weights = {n: given[n] for n in TWIN_WEIGHTS}
    shared = {n: given[n] for n in SHARED_INPUTS}
    per_example = {n: given[n] for n in ['x', 'c', 'positions']}
    grad_fn = _jax.value_and_grad(_loss, argnums=(0, 1))

    def one_microbatch(ex, loss_target):
        ex = dict(ex)
        diff = ex.pop(TWIN_DIFF_INPUT)
        return grad_fn(weights, diff, {**shared, **ex}, loss_target)

    if N_MICROBATCH == 1:
        loss, (grad_w, grad_x) = one_microbatch(per_example, given["loss_target"])
    else:
        def body(carry, xs):
            loss_sum, grad_sum = carry
            l_k, (gw_k, gx_k) = one_microbatch(xs[0], xs[1])
            with _jax.named_scope("update"):
                return (loss_sum + l_k, _jax.tree.map(_jnp.add, grad_sum, gw_k)), gx_k

        init = (_jnp.zeros((), _jnp.float32), _jax.tree.map(_jnp.zeros_like, weights))
        (loss, grad_w), grad_x = _jax.lax.scan(body, init, (per_example, given["loss_target"]))
    with _jax.named_scope("update"):
        delta_w, new_m, new_v = {}, {}, {}
        for n in TWIN_WEIGHTS:
            delta_w[n], new_m[n], new_v[n] = _adamw(weights[n], grad_w[n], given["m_" + n], given["v_" + n])
    return (loss, grad_x, *[grad_w[n] for n in TWIN_WEIGHTS], *[delta_w[n] for n in TWIN_WEIGHTS],
            *[new_m[n] for n in TWIN_WEIGHTS], *[new_v[n] for n in TWIN_WEIGHTS])
```

```python
import functools
import math

import jax
import jax.numpy as jnp
from jax import lax
from jax.experimental import pallas as pl
from jax.experimental.pallas import tpu as pltpu

F32 = jnp.float32
BF16 = jnp.bfloat16

N_DEV = 8
D = 1024
HD = 64
NQ = 8
AW = 512
KVW = 128
SW = 512
NST = 128
NH = 8
LB = 128
CONVK = 4
DFF = 2816
N_MOD = 6
IN_PROJ = 2312
INP = 2432
O_Q, O_K, O_V, O_Z, O_XBC, O_DT = 0, 512, 640, 768, 1280, 2304
ZXD = INP - O_Z
EPS = 1e-6
NEG = -1e30
ROPE_THETA = 10000.0
VMEM_LIMIT = 56 * 1024 * 1024

ADAM_LR = 0.001
ADAM_B1 = 0.9
ADAM_B2 = 0.999
ADAM_EPS = 1e-08
ADAM_WD = 0.01
ADAM_STEP = 10

NT_DIMS = (((1,), (1,)), ((), ()))
TN_DIMS = (((0,), (0,)), ((), ()))


def _pcall(body, **kw):
    return pl.pallas_call(body, **kw)


def _sds(shape, dtype):
    return jax.ShapeDtypeStruct(shape, dtype)


def _params(n_grid=1):
    return pltpu.CompilerParams(dimension_semantics=("arbitrary",) * n_grid, vmem_limit_bytes=VMEM_LIMIT)


def _const(shape):
    return pl.BlockSpec(shape, lambda *_: (0,) * len(shape), pipeline_mode=pl.Buffered(1))


def _largest_divisor(n, candidates):
    for cand in candidates:
        if n % cand == 0:
            return cand
    raise ValueError(f"no tile in {candidates} divides {n}")


def _rows(t, w):
    return pl.BlockSpec((t, w), lambda i: (i, 0))


def _dot(a, b):
    return jnp.dot(a, b, preferred_element_type=F32)


def _dot_nt(a, b):
    return lax.dot_general(a, b, NT_DIMS, preferred_element_type=F32)


def _dot_tn(a, b):
    return lax.dot_general(a, b, TN_DIMS, preferred_element_type=F32)


def _sigmoid(v):
    return 1.0 / (1.0 + jnp.exp(-v))


def _softplus(v):
    return jnp.maximum(v, 0.0) + jnp.log1p(jnp.exp(-jnp.abs(v)))


def _rope_sign_mask(shape):
    lane = lax.broadcasted_iota(jnp.int32, shape, 1)
    return (lane % HD) < (HD // 2)


def _rope(t, cs, sn, inverse):
    r_dn = pltpu.roll(t, HD // 2, 1)
    r_up = pltpu.roll(t, LB - HD // 2, 1)
    first = _rope_sign_mask(t.shape)
    if inverse:
        rot = jnp.where(first, r_up, -r_dn)
    else:
        rot = jnp.where(first, -r_up, r_dn)
    return t * cs + rot * sn


def _norm_mod_fwd(xv, nw, shift, scale):
    r = lax.rsqrt(jnp.mean(xv * xv, axis=-1, keepdims=True) + EPS)
    xh = xv * r
    return (xh * nw) * (1.0 + scale) + shift


def _norm_mod_bwd(xv, dh, nw, scale):
    r = lax.rsqrt(jnp.mean(xv * xv, axis=-1, keepdims=True) + EPS)
    xh = xv * r
    xn = xh * nw
    d_shift = jnp.sum(dh, axis=0, keepdims=True)
    d_scale = jnp.sum(dh * xn, axis=0, keepdims=True)
    dxn = dh * (1.0 + scale)
    d_w = jnp.sum(dxn * xh, axis=0, keepdims=True)
    dxh = dxn * nw
    dx = r * (dxh - xh * jnp.mean(dxh * xh, axis=-1, keepdims=True))
    return dx, d_shift, d_scale, d_w


def _inproj_fwd(x, pos, invf, mod8, n1w, w_in):
    s = x.shape[0]
    tt = min(512, s)

    def body(x_ref, pos_ref, invf_ref, mod_ref, nw_ref, w_ref,
             q_ref, k_ref, v_ref, z_ref, xbc_ref, dtr_ref, h1_ref, cos_ref, sin_ref):
        h = _norm_mod_fwd(x_ref[...], nw_ref[...], mod_ref[0:1, :], mod_ref[1:2, :])
        hb = h.astype(BF16)
        h1_ref[...] = hb
        proj = _dot(hb, w_ref[...])
        ang = pos_ref[...].astype(F32) * invf_ref[...]
        cs = jnp.cos(ang)
        sn = jnp.sin(ang)
        cos_ref[...] = cs
        sin_ref[...] = sn
        for a in range(AW // LB):
            q_ref[:, a * LB:(a + 1) * LB] = _rope(proj[:, O_Q + a * LB:O_Q + (a + 1) * LB], cs, sn, False).astype(BF16)
        k_ref[...] = _rope(proj[:, O_K:O_V], cs, sn, False).astype(BF16)
        v_ref[...] = proj[:, O_V:O_Z].astype(BF16)
        z_ref[...] = proj[:, O_Z:O_XBC]
        xbc_ref[...] = proj[:, O_XBC:O_DT]
        dtr_ref[...] = proj[:, O_DT:INP]

    return _pcall(
        body, name="inproj_fwd", grid=(s // tt,),
        in_specs=[_rows(tt, D), _rows(tt, 1), _const((1, LB)), _const((8, D)), _const((1, D)), _const((D, INP))],
        out_specs=[_rows(tt, AW), _rows(tt, KVW), _rows(tt, KVW), _rows(tt, SW), _rows(tt, D), _rows(tt, LB),
                   _rows(tt, D), _rows(tt, LB), _rows(tt, LB)],
        out_shape=[_sds((s, AW), BF16), _sds((s, KVW), BF16), _sds((s, KVW), BF16), _sds((s, SW), F32),
                   _sds((s, D), F32), _sds((s, LB), F32), _sds((s, D), BF16), _sds((s, LB), F32), _sds((s, LB), F32)],
        compiler_params=_params(),
    )(x, pos, invf, mod8, n1w, w_in)


def _attn_mask(n):
    row = lax.broadcasted_iota(jnp.int32, (LB, 2 * LB), 0)
    col = lax.broadcasted_iota(jnp.int32, (LB, 2 * LB), 1)
    diff = row + LB - col
    return (diff >= 0) & (diff < LB) & ((col >= LB) | (n > 0))


def _attn_probs(qh, kg, mask, sink):
    sc = _dot_nt(qh, kg) * (1.0 / math.sqrt(HD))
    sc = jnp.where(mask, sc, NEG)
    m = jnp.maximum(jnp.max(sc, axis=-1, keepdims=True), sink)
    p = jnp.exp(sc - m)
    es = jnp.exp(sink - m)
    denom = jnp.sum(p, axis=-1, keepdims=True) + es
    return p / denom, es / denom


def _attn_fwd(q, k, v, sinks8):
    s = q.shape[0]
    nb = s // LB
    prev = lambda n: (jnp.maximum(n - 1, 0), 0)
    cur = lambda n: (n, 0)

    def body(q_ref, kp_ref, kc_ref, vp_ref, vc_ref, sink_ref, o_ref):
        n = pl.program_id(0)
        kcat = jnp.concatenate([kp_ref[...], kc_ref[...]], axis=0)
        vcat = jnp.concatenate([vp_ref[...], vc_ref[...]], axis=0)
        mask = _attn_mask(n)
        for h in range(NQ):
            g = h // 4
            probs, _ = _attn_probs(q_ref[:, h * HD:(h + 1) * HD], kcat[:, g * HD:(g + 1) * HD], mask,
                                   sink_ref[h:h + 1, 0:1])
            o = _dot(probs.astype(BF16), vcat[:, g * HD:(g + 1) * HD])
            o_ref[:, h * HD:(h + 1) * HD] = o.astype(BF16)

    return _pcall(
        body, name="attn_fwd", grid=(nb,),
        in_specs=[pl.BlockSpec((LB, AW), cur), pl.BlockSpec((LB, KVW), prev), pl.BlockSpec((LB, KVW), cur),
                  pl.BlockSpec((LB, KVW), prev), pl.BlockSpec((LB, KVW), cur), _const((8, LB))],
        out_specs=pl.BlockSpec((LB, AW), cur),
        out_shape=_sds((s, AW), BF16),
        compiler_params=_params(),
    )(q, k, k, v, v, sinks8)


def _cumsum_rows(a, reverse):
    row = lax.broadcasted_iota(jnp.int32, a.shape, 0)
    step = 1
    while step < LB:
        if reverse:
            a = a + jnp.where(row < LB - step, pltpu.roll(a, LB - step, 0), 0.0)
        else:
            a = a + jnp.where(row >= step, pltpu.roll(a, step, 0), 0.0)
        step *= 2
    return a


def _conv_shifts(prev, cur):
    row = lax.broadcasted_iota(jnp.int32, cur.shape, 0)
    out = [cur]
    for j in range(1, CONVK):
        out.append(jnp.where(row < j, pltpu.roll(prev, j, 0), pltpu.roll(cur, j, 0)))
    return out


def _ssd_pre(xp_ref, xc_ref, cw_ref, cb_ref, dtr_ref, sp_ref, n):
    cur = xc_ref[...]
    prev = jnp.where(n > 0, xp_ref[...], 0.0)
    sh = _conv_shifts(prev, cur)
    u = cb_ref[...] + cw_ref[CONVK - 1:CONVK, :] * sh[0]
    for j in range(1, CONVK):
        u = u + cw_ref[CONVK - 1 - j:CONVK - j, :] * sh[j]
    sg_u = _sigmoid(u)
    xc = u * sg_u
    pre = dtr_ref[...] + sp_ref[0:1, :]
    dt = _softplus(pre)
    a_neg = -jnp.exp(sp_ref[1:2, :])
    acs = _cumsum_rows(dt * a_neg, False)
    return sh, u, sg_u, xc, pre, dt, a_neg, acs


def _gated_norm_fwd(y, z, nw):
    sz = z * _sigmoid(z)
    yz = y * sz
    parts = []
    for g in range(2):
        t = yz[:, g * 256:(g + 1) * 256]
        parts.append(t * lax.rsqrt(jnp.mean(t * t, axis=-1, keepdims=True) + EPS))
    return jnp.concatenate(parts, axis=1) * nw


def _ssd_fwd(xbc, conv_w8, conv_b, dtr, ssm_p, z, nw):
    s = xbc.shape[0]
    nc = s // LB
    prev = lambda n: (jnp.maximum(n - 1, 0), 0)
    cur = lambda n: (n, 0)

    def body(xp_ref, xc_ref, cw_ref, cb_ref, dtr_ref, sp_ref, z_ref, nw_ref, yn_ref, y_ref, hs_ref, h_scr):
        n = pl.program_id(0)

        @pl.when(n == 0)
        def _():
            h_scr[...] = jnp.zeros_like(h_scr)

        hs_ref[0] = h_scr[...]
        _, _, _, xc, _, dt, _, acs = _ssd_pre(xp_ref, xc_ref, cw_ref, cb_ref, dtr_ref, sp_ref, n)
        acs_t = jnp.transpose(acs)
        e_all = jnp.exp(acs)
        alast = acs[LB - 1:LB, :]
        dte_all = jnp.exp(alast - acs)
        elast = jnp.exp(alast)
        causal = (lax.broadcasted_iota(jnp.int32, (LB, LB), 0) >= lax.broadcasted_iota(jnp.int32, (LB, LB), 1))
        for g in range(2):
            bg = xc[:, SW + g * NST:SW + (g + 1) * NST]
            cg = xc[:, SW + 2 * NST + g * NST:SW + 2 * NST + (g + 1) * NST]
            bgb = bg.astype(BF16)
            cgb = cg.astype(BF16)
            cb = _dot_nt(cgb, bgb)
            for jj in range(4):
                j = 4 * g + jj
                sl = slice(j * HD, (j + 1) * HD)
                dm = jnp.exp(jnp.where(causal, acs[:, j:j + 1] - acs_t[j:j + 1, :], NEG))
                xs_j = xc[:, sl]
                xdtb = (xs_j * dt[:, j:j + 1]).astype(BF16)
                hj = h_scr[sl, :]
                yd = _dot((cb * dm).astype(BF16), xdtb)
                yo = e_all[:, j:j + 1] * _dot_nt(cgb, hj.astype(BF16))
                bd = (bg * dte_all[:, j:j + 1]).astype(BF16)
                h_scr[sl, :] = hj * elast[:, j:j + 1] + _dot_tn(xdtb, bd)
                y_ref[:, sl] = yd + yo + xs_j * sp_ref[2:3, j:j + 1]
        yn_ref[...] = _gated_norm_fwd(y_ref[...], z_ref[...], nw_ref[...]).astype(BF16)

    return _pcall(
        body, name="ssd_fwd", grid=(nc,),
        in_specs=[pl.BlockSpec((LB, D), prev), pl.BlockSpec((LB, D), cur), _const((8, D)), _const((1, D)),
                  pl.BlockSpec((LB, LB), cur), _const((8, LB)), pl.BlockSpec((LB, SW), cur), _const((1, SW))],
        out_specs=[pl.BlockSpec((LB, SW), cur), pl.BlockSpec((LB, SW), cur),
                   pl.BlockSpec((1, NH * HD, NST), lambda n: (n, 0, 0))],
        out_shape=[_sds((s, SW), BF16), _sds((s, SW), F32), _sds((nc, NH * HD, NST), F32)],
        scratch_shapes=[pltpu.VMEM((NH * HD, NST), F32)],
        compiler_params=_params(),
    )(xbc, xbc, conv_w8, conv_b, dtr, ssm_p, z, nw)


def _outproj_fwd(attn, yn, x, mod8, n2w, w_out):
    s = x.shape[0]
    tt = min(512, s)

    def body(a_ref, y_ref, x_ref, mod_ref, nw_ref, w_ref, x2_ref, h2_ref, mo_ref):
        mix = jnp.concatenate([a_ref[...], y_ref[...]], axis=1)
        mo = _dot(mix, w_ref[...])
        mo_ref[...] = mo.astype(BF16)
        x2 = x_ref[...] + mod_ref[2:3, :] * mo
        x2_ref[...] = x2
        h2_ref[...] = _norm_mod_fwd(x2, nw_ref[...], mod_ref[3:4, :], mod_ref[4:5, :]).astype(BF16)

    return _pcall(
        body, name="outproj_fwd", grid=(s // tt,),
        in_specs=[_rows(tt, AW), _rows(tt, SW), _rows(tt, D), _const((8, D)), _const((1, D)), _const((D, D))],
        out_specs=[_rows(tt, D), _rows(tt, D), _rows(tt, D)],
        out_shape=[_sds((s, D), F32), _sds((s, D), BF16), _sds((s, D), BF16)],
        compiler_params=_params(),
    )(attn, yn, x, mod8, n2w, w_out)


def _ffn_fwd_loss(h2, x2, tgt, mod8, fnw, w_gu, w_down):
    s = x2.shape[0]
    tf = min(256, s)

    def body(h_ref, x2_ref, t_ref, mod_ref, fw_ref, wgu_ref, wd_ref, gu_ref, act_ref, dx3_ref, sm_ref):
        i = pl.program_id(0)

        @pl.when(i == 0)
        def _():
            sm_ref[...] = jnp.zeros_like(sm_ref)

        gu = _dot(h_ref[...], wgu_ref[...])
        gu_ref[...] = gu.astype(BF16)
        g = gu[:, :DFF]
        act = (g * _sigmoid(g) * gu[:, DFF:]).astype(BF16)
        act_ref[...] = act
        ff = _dot(act, wd_ref[...])
        x3 = x2_ref[...] + mod_ref[5:6, :] * ff
        r = lax.rsqrt(jnp.mean(x3 * x3, axis=-1, keepdims=True) + EPS)
        xh = x3 * r
        fw = fw_ref[...]
        err = xh * fw - t_ref[...]
        dy = err * (1.0 / D)
        dxh = dy * fw
        dx3 = r * (dxh - xh * jnp.mean(dxh * xh, axis=-1, keepdims=True))
        dx3_ref[...] = dx3
        sm_ref[0:1, :] += jnp.sum(dx3 * ff, axis=0, keepdims=True)
        sm_ref[1:2, :] += jnp.sum(dy * xh, axis=0, keepdims=True)
        sm_ref[2:3, :] += jnp.sum(err * err, axis=0, keepdims=True)

    return _pcall(
        body, name="ffn_fwd_loss", grid=(s // tf,),
        in_specs=[_rows(tf, D), _rows(tf, D), _rows(tf, D), _const((8, D)), _const((1, D)),
                  _const((D, 2 * DFF)), _const((DFF, D))],
        out_specs=[_rows(tf, 2 * DFF), _rows(tf, DFF), _rows(tf, D), pl.BlockSpec((8, D), lambda i: (0, 0))],
        out_shape=[_sds((s, 2 * DFF), BF16), _sds((s, DFF), BF16), _sds((s, D), F32), _sds((8, D), F32)],
        compiler_params=_params(),
    )(h2, x2, tgt, mod8, fnw, w_gu, w_down)


def _ffn_bwd(dx3, gu, x2, mixout, mod8, n2w, w_gu, w_down, w_out):
    s = x2.shape[0]
    tb = min(256, s)

    def body(dx3_ref, gu_ref, x2_ref, mo_ref, mod_ref, nw_ref, wgu_ref, wd_ref, wo_ref,
             dx2_ref, dff_ref, dgu_ref, dmix_ref, dattn_ref, dyn_ref, sm_ref):
        i = pl.program_id(0)

        @pl.when(i == 0)
        def _():
            sm_ref[...] = jnp.zeros_like(sm_ref)

        dx3 = dx3_ref[...]
        dff = (dx3 * mod_ref[5:6, :]).astype(BF16)
        dff_ref[...] = dff
        dact = _dot_nt(dff, wd_ref[...])
        g = gu_ref[:, :DFF].astype(F32)
        u = gu_ref[:, DFF:].astype(F32)
        sg = _sigmoid(g)
        dgu = jnp.concatenate([dact * u * sg * (1.0 + g * (1.0 - sg)), dact * g * sg], axis=1).astype(BF16)
        dgu_ref[...] = dgu
        dh2 = _dot_nt(dgu, wgu_ref[...])
        dxn, d_shift, d_scale, d_w = _norm_mod_bwd(x2_ref[...], dh2, nw_ref[...], mod_ref[4:5, :])
        dx2 = dx3 + dxn
        dx2_ref[...] = dx2
        sm_ref[0:1, :] += d_shift
        sm_ref[1:2, :] += d_scale
        sm_ref[2:3, :] += d_w
        sm_ref[3:4, :] += jnp.sum(dx2 * mo_ref[...].astype(F32), axis=0, keepdims=True)
        dmix = (dx2 * mod_ref[2:3, :]).astype(BF16)
        dmix_ref[...] = dmix
        dmi = _dot_nt(dmix, wo_ref[...])
        dattn_ref[...] = dmi[:, :AW].astype(BF16)
        dyn_ref[...] = dmi[:, AW:]

    return _pcall(
        body, name="ffn_bwd", grid=(s // tb,),
        in_specs=[_rows(tb, D), _rows(tb, 2 * DFF), _rows(tb, D), _rows(tb, D), _const((8, D)), _const((1, D)),
                  _const((D, 2 * DFF)), _const((DFF, D)), _const((D, D))],
        out_specs=[_rows(tb, D), _rows(tb, D), _rows(tb, 2 * DFF), _rows(tb, D), _rows(tb, AW), _rows(tb, SW),
                   pl.BlockSpec((8, D), lambda i: (0, 0))],
        out_shape=[_sds((s, D), F32), _sds((s, D), BF16), _sds((s, 2 * DFF), BF16), _sds((s, D), BF16),
                   _sds((s, AW), BF16), _sds((s, SW), F32), _sds((8, D), F32)],
        compiler_params=_params(),
    )(dx3, gu, x2, mixout, mod8, n2w, w_gu, w_down, w_out)


def _ssd_bwd(dyn, y, z, xbc, conv_w8, conv_b, dtr, ssm_p, nw, hs):
    s = xbc.shape[0]
    nc = s // LB
    cur = lambda i: (nc - 1 - i, 0)
    prev = lambda i: (jnp.maximum(nc - 2 - i, 0), 0)

    def body(dyn_ref, y_ref, z_ref, xp_ref, xc_ref, cw_ref, cb_ref, dtr_ref, sp_ref, nw_ref, hs_ref,
             dzxd_ref, sm_ref, dh_scr, dun_scr, dxc_scr):
        i = pl.program_id(0)
        n = nc - 1 - i

        @pl.when(i == 0)
        def _():
            dh_scr[...] = jnp.zeros_like(dh_scr)
            dun_scr[...] = jnp.zeros_like(dun_scr)
            sm_ref[...] = jnp.zeros_like(sm_ref)

        sh, u, sg_u, xc, pre, dt, a_neg, acs = _ssd_pre(xp_ref, xc_ref, cw_ref, cb_ref, dtr_ref, sp_ref, n)
        acs_t = jnp.transpose(acs)
        e_all = jnp.exp(acs)
        alast = acs[LB - 1:LB, :]
        dte_all = jnp.exp(alast - acs)
        elast = jnp.exp(alast)
        riota = lax.broadcasted_iota(jnp.int32, (LB, LB), 0)
        liota = lax.broadcasted_iota(jnp.int32, (LB, LB), 1)
        causal = riota >= liota
        lane1 = lax.broadcasted_iota(jnp.int32, (1, LB), 1)

        z = z_ref[...]
        y = y_ref[...]
        sgz = _sigmoid(z)
        sz = z * sgz
        yz = y * sz
        nwv = nw_ref[...]
        dyn_v = dyn_ref[...]
        dyhat = dyn_v * nwv
        yhat_parts, dyz_parts = [], []
        for g in range(2):
            gs = slice(g * 256, (g + 1) * 256)
            t = yz[:, gs]
            rg = lax.rsqrt(jnp.mean(t * t, axis=-1, keepdims=True) + EPS)
            yh = t * rg
            dyh = dyhat[:, gs]
            yhat_parts.append(yh)
            dyz_parts.append(rg * (dyh - yh * jnp.mean(dyh * yh, axis=-1, keepdims=True)))
        yhat = jnp.concatenate(yhat_parts, axis=1)
        dyz = jnp.concatenate(dyz_parts, axis=1)
        sm_ref[5:6, 0:SW] += jnp.sum(dyn_v * yhat, axis=0, keepdims=True)
        dy = dyz * sz
        dzxd_ref[:, 0:SW] = (dyz * y * sgz * (1.0 + z * (1.0 - sgz))).astype(BF16)

        dacs = jnp.zeros((LB, LB), F32)
        dacs_t = jnp.zeros((LB, LB), F32)
        ddt = jnp.zeros((LB, LB), F32)
        dalast = jnp.zeros((1, LB), F32)
        ddsk = jnp.zeros((1, LB), F32)
        for g in range(2):
            bg = xc[:, SW + g * NST:SW + (g + 1) * NST]
            cg = xc[:, SW + 2 * NST + g * NST:SW + 2 * NST + (g + 1) * NST]
            bgb = bg.astype(BF16)
            cgb = cg.astype(BF16)
            cb = _dot_nt(cgb, bgb)
            dcb = jnp.zeros((LB, LB), F32)
            dcg = jnp.zeros((LB, NST), F32)
            dbg = jnp.zeros((LB, NST), F32)
            for jj in range(4):
                j = 4 * g + jj
                sl = slice(j * HD, (j + 1) * HD)
                onl = (liota == j).astype(F32)
                onr = (riota == j).astype(F32)
                on1 = (lane1 == j).astype(F32)
                dm = jnp.exp(jnp.where(causal, acs[:, j:j + 1] - acs_t[j:j + 1, :], NEG))
                xs_j = xc[:, sl]
                dt_j = dt[:, j:j + 1]
                xdtb = (xs_j * dt_j).astype(BF16)
                m = cb * dm
                hj = hs_ref[0, sl, :]
                hb = hj.astype(BF16)
                dy_j = dy[:, sl]
                dyb = dy_j.astype(BF16)
                dxs_j = dy_j * sp_ref[2:3, j:j + 1]
                ddsk = ddsk + on1 * jnp.sum(dy_j * xs_j)
                dm_ = _dot_nt(dyb, xdtb)
                dxdt = _dot_tn(m.astype(BF16), dyb)
                gmat = dm_ * m
                dcb = dcb + dm_ * dm
                dacs = dacs + onl * jnp.sum(gmat, axis=1, keepdims=True)
                dacs_t = dacs_t - onr * jnp.sum(gmat, axis=0, keepdims=True)
                qm = _dot_nt(cgb, hb)
                eac = e_all[:, j:j + 1]
                dqb = (dy_j * eac).astype(BF16)
                dcg = dcg + _dot(dqb, hb)
                dh_prev = _dot_tn(dqb, cgb)
                dacs = dacs + onl * (jnp.sum(dy_j * qm, axis=1, keepdims=True) * eac)
                dhn = dh_scr[sl, :]
                dhnb = dhn.astype(BF16)
                dte_j = dte_all[:, j:j + 1]
                bdb = (bg * dte_j).astype(BF16)
                dxdt = dxdt + _dot_nt(bdb, dhnb)
                dbd = _dot(xdtb, dhnb)
                dbg = dbg + dbd * dte_j
                t1 = jnp.sum(dbd * bg, axis=1, keepdims=True) * dte_j
                dacs = dacs - onl * t1
                el = elast[:, j:j + 1]
                dalast = dalast + on1 * (jnp.sum(t1) + jnp.sum(dhn * hj) * el)
                dh_scr[sl, :] = dhn * el + dh_prev
                dxc_scr[:, sl] = dxs_j + dxdt * dt_j
                ddt = ddt + onl * jnp.sum(dxdt * xs_j, axis=1, keepdims=True)
            dcbb = dcb.astype(BF16)
            dxc_scr[:, SW + g * NST:SW + (g + 1) * NST] = dbg + _dot_tn(dcbb, cgb)
            dxc_scr[:, SW + 2 * NST + g * NST:SW + 2 * NST + (g + 1) * NST] = dcg + _dot(dcbb, bgb)
        dacs = dacs + jnp.transpose(dacs_t) + jnp.where(riota == LB - 1, dalast, 0.0)
        da = _cumsum_rows(dacs, True)
        ddt = ddt + da * a_neg
        da_log = jnp.sum(da * dt, axis=0, keepdims=True) * a_neg
        ddtr = ddt * _sigmoid(pre)
        dzxd_ref[:, SW + D:ZXD] = ddtr.astype(BF16)
        sm_ref[6:7, 0:LB] += jnp.sum(ddtr, axis=0, keepdims=True)
        sm_ref[6:7, LB:2 * LB] += da_log
        sm_ref[6:7, 2 * LB:3 * LB] += ddsk

        du = dxc_scr[...] * (sg_u * (1.0 + u * (1.0 - sg_u)))
        sm_ref[0:1, :] += jnp.sum(du, axis=0, keepdims=True)
        for k in range(CONVK):
            sm_ref[1 + k:2 + k, :] += jnp.sum(du * sh[CONVK - 1 - k], axis=0, keepdims=True)
        dun = dun_scr[...]
        row = lax.broadcasted_iota(jnp.int32, du.shape, 0)
        dxbc = cw_ref[CONVK - 1:CONVK, :] * du
        for j in range(1, CONVK):
            adv = jnp.where(row >= LB - j, pltpu.roll(dun, LB - j, 0), pltpu.roll(du, LB - j, 0))
            dxbc = dxbc + cw_ref[CONVK - 1 - j:CONVK - j, :] * adv
        dun_scr[...] = du
        dzxd_ref[:, SW:SW + D] = dxbc.astype(BF16)

    return _pcall(
        body, name="ssd_bwd", grid=(nc,),
        in_specs=[pl.BlockSpec((LB, SW), cur), pl.BlockSpec((LB, SW), cur), pl.BlockSpec((LB, SW), cur),
                  pl.BlockSpec((LB, D), prev), pl.BlockSpec((LB, D), cur), _const((8, D)), _const((1, D)),
                  pl.BlockSpec((LB, LB), cur), _const((8, LB)), _const((1, SW)),
                  pl.BlockSpec((1, NH * HD, NST), lambda i: (nc - 1 - i, 0, 0))],
        out_specs=[pl.BlockSpec((LB, ZXD), cur), pl.BlockSpec((8, D), lambda i: (0, 0))],
        out_shape=[_sds((s, ZXD), BF16), _sds((8, D), F32)],
        scratch_shapes=[pltpu.VMEM((NH * HD, NST), F32), pltpu.VMEM((LB, D), F32), pltpu.VMEM((LB, D), F32)],
        compiler_params=_params(),
    )(dyn, y, z, xbc, xbc, conv_w8, conv_b, dtr, ssm_p, nw, hs)


def _attn_bwd(q, k, v, o, do, cos, sin, sinks8):
    s = q.shape[0]
    nb = s // LB
    cur = lambda n: (jnp.minimum(n, nb - 1), 0)
    prev = lambda n: (jnp.maximum(jnp.minimum(n, nb - 1) - 1, 0), 0)
    late = lambda n: (jnp.maximum(n - 1, 0), 0)

    def body(q_ref, kp_ref, kc_ref, vp_ref, vc_ref, o_ref, do_ref, cq_ref, sq_ref, cl_ref, sl_ref, sink_ref,
             dq_ref, dkv_ref, ds_ref, ck_scr, cv_scr, dq_scr, dk_scr, dv_scr):
        n = pl.program_id(0)

        @pl.when(n == 0)
        def _():
            ds_ref[...] = jnp.zeros_like(ds_ref)
            ck_scr[...] = jnp.zeros_like(ck_scr)
            cv_scr[...] = jnp.zeros_like(cv_scr)

        @pl.when(n < nb)
        def _():
            kcat = jnp.concatenate([kp_ref[...], kc_ref[...]], axis=0)
            vcat = jnp.concatenate([vp_ref[...], vc_ref[...]], axis=0)
            mask = _attn_mask(n)
            lane1 = lax.broadcasted_iota(jnp.int32, (1, LB), 1)
            for g in range(2):
                kg = kcat[:, g * HD:(g + 1) * HD]
                vg = vcat[:, g * HD:(g + 1) * HD]
                dkg = jnp.zeros((2 * LB, HD), F32)
                dvg = jnp.zeros((2 * LB, HD), F32)
                for hh in range(4):
                    h = 4 * g + hh
                    sl = slice(h * HD, (h + 1) * HD)
                    qh = q_ref[:, sl]
                    probs, psink = _attn_probs(qh, kg, mask, sink_ref[h:h + 1, 0:1])
                    doh = do_ref[:, sl]
                    delta = jnp.sum(doh.astype(F32) * o_ref[:, sl].astype(F32), axis=1, keepdims=True)
                    dsc = probs * (_dot_nt(doh, vg) - delta)
                    ds_ref[h:h + 1, :] += jnp.where(lane1 == 0, -jnp.sum(psink * delta), 0.0)
                    dsb = dsc.astype(BF16)
                    dq_scr[:, sl] = _dot(dsb, kg) * (1.0 / math.sqrt(HD))
                    dkg = dkg + _dot_tn(dsb, qh) * (1.0 / math.sqrt(HD))
                    dvg = dvg + _dot_tn(probs.astype(BF16), doh)
                dk_scr[:, g * HD:(g + 1) * HD] = dkg
                dv_scr[:, g * HD:(g + 1) * HD] = dvg
            cs = cq_ref[...]
            sn = sq_ref[...]
            for a in range(AW // LB):
                dq_ref[:, a * LB:(a + 1) * LB] = _rope(dq_scr[:, a * LB:(a + 1) * LB], cs, sn, True).astype(BF16)
            dkv_ref[:, 0:KVW] = _rope(ck_scr[...] + dk_scr[0:LB, :], cl_ref[...], sl_ref[...], True).astype(BF16)
            dkv_ref[:, KVW:2 * KVW] = (cv_scr[...] + dv_scr[0:LB, :]).astype(BF16)
            ck_scr[...] = dk_scr[LB:2 * LB, :]
            cv_scr[...] = dv_scr[LB:2 * LB, :]

        @pl.when(n == nb)
        def _():
            dkv_ref[:, 0:KVW] = _rope(ck_scr[...], cl_ref[...], sl_ref[...], True).astype(BF16)
            dkv_ref[:, KVW:2 * KVW] = cv_scr[...].astype(BF16)

    return _pcall(
        body, name="attn_bwd", grid=(nb + 1,),
        in_specs=[pl.BlockSpec((LB, AW), cur), pl.BlockSpec((LB, KVW), prev), pl.BlockSpec((LB, KVW), cur),
                  pl.BlockSpec((LB, KVW), prev), pl.BlockSpec((LB, KVW), cur), pl.BlockSpec((LB, AW), cur),
                  pl.BlockSpec((LB, AW), cur), pl.BlockSpec((LB, LB), cur), pl.BlockSpec((LB, LB), cur),
                  pl.BlockSpec((LB, LB), late), pl.BlockSpec((LB, LB), late), _const((8, LB))],
        out_specs=[pl.BlockSpec((LB, AW), cur), pl.BlockSpec((LB, 2 * KVW), late),
                   pl.BlockSpec((8, LB), lambda n: (0, 0))],
        out_shape=[_sds((s, AW), BF16), _sds((s, 2 * KVW), BF16), _sds((8, LB), F32)],
        scratch_shapes=[pltpu.VMEM((LB, KVW), F32), pltpu.VMEM((LB, KVW), F32), pltpu.VMEM((LB, AW), F32),
                        pltpu.VMEM((2 * LB, KVW), F32), pltpu.VMEM((2 * LB, KVW), F32)],
        compiler_params=_params(),
    )(q, k, k, v, v, o, do, cos, sin, cos, sin, sinks8)


def _inproj_bwd(dq, dkv, dzxd, x, dx2, mod8, n1w, w_in):
    s = x.shape[0]
    tt = min(512, s)

    def body(dq_ref, dkv_ref, dz_ref, x_ref, dx2_ref, mod_ref, nw_ref, w_ref, gx_ref, sm_ref):
        i = pl.program_id(0)

        @pl.when(i == 0)
        def _():
            sm_ref[...] = jnp.zeros_like(sm_ref)

        dproj = jnp.concatenate([dq_ref[...], dkv_ref[...], dz_ref[...]], axis=1)
        dh1 = _dot_nt(dproj, w_ref[...])
        dxn, d_shift, d_scale, d_w = _norm_mod_bwd(x_ref[...], dh1, nw_ref[...], mod_ref[1:2, :])
        gx_ref[...] = dx2_ref[...] + dxn
        sm_ref[0:1, :] += d_shift
        sm_ref[1:2, :] += d_scale
        sm_ref[2:3, :] += d_w

    return _pcall(
        body, name="inproj_bwd", grid=(s // tt,),
        in_specs=[_rows(tt, AW), _rows(tt, 2 * KVW), _rows(tt, ZXD), _rows(tt, D), _rows(tt, D),
                  _const((8, D)), _const((1, D)), _const((D, INP))],
        out_specs=[_rows(tt, D), pl.BlockSpec((8, D), lambda i: (0, 0))],
        out_shape=[_sds((s, D), F32), _sds((8, D), F32)],
        compiler_params=_params(),
    )(dq, dkv, dzxd, x, dx2, mod8, n1w, w_in)


def _wgrad(a, b, name):
    s, m = a.shape
    n = b.shape[1]
    tk = min(512, s)
    tm = _largest_divisor(m, (512, 256, 128))
    tn = n if n <= 2048 else _largest_divisor(n, (1408, 1024, 512, 256, 128))
    nk = s // tk

    def body(a_ref, b_ref, o_ref, acc):
        kk = pl.program_id(2)

        @pl.when(kk == 0)
        def _():
            acc[...] = jnp.zeros_like(acc)

        acc[...] += _dot_tn(a_ref[...], b_ref[...])

        @pl.when(kk == nk - 1)
        def _():
            o_ref[...] = acc[...].astype(BF16)

    return _pcall(
        body, name=name, grid=(m // tm, n // tn, nk),
        in_specs=[pl.BlockSpec((tk, tm), lambda i, j, kk: (kk, i)), pl.BlockSpec((tk, tn), lambda i, j, kk: (kk, j))],
        out_specs=pl.BlockSpec((tm, tn), lambda i, j, kk: (i, j)),
        out_shape=_sds((m, n), BF16),
        scratch_shapes=[pltpu.VMEM((tm, tn), F32)],
        compiler_params=_params(3),
    )(a, b)


PACK_ROWS = 24


def _pack_small(sm_f, sm_b, sm_s, sm_i, dsink):
    def body(f_ref, b_ref, s_ref, i_ref, k_ref, o_ref):
        o_ref[...] = jnp.zeros_like(o_ref)
        o_ref[0:2, :] = i_ref[0:2, :]
        o_ref[2:3, :] = b_ref[3:4, :]
        o_ref[3:5, :] = b_ref[0:2, :]
        o_ref[5:6, :] = f_ref[0:1, :]
        o_ref[6:7, :] = i_ref[2:3, :]
        o_ref[7:8, :] = b_ref[2:3, :]
        o_ref[8:9, :] = f_ref[1:2, :]
        o_ref[9:14, :] = s_ref[0:5, :]
        o_ref[14:15, :] = s_ref[5:6, :]
        o_ref[15:16, 0:3 * LB] = s_ref[6:7, 0:3 * LB]
        lane = lax.broadcasted_iota(jnp.int32, (1, LB), 1)
        sk = jnp.zeros((1, LB), F32)
        for h in range(NQ):
            sk = sk + jnp.where(lane == h, k_ref[h:h + 1, 0:1], 0.0)
        o_ref[15:16, 3 * LB:4 * LB] = sk
        o_ref[16:17, :] = f_ref[2:3, :]

    return _pcall(body, name="pack_small", out_shape=_sds((PACK_ROWS, D), F32))(sm_f, sm_b, sm_s, sm_i, dsink)


def _local_step(x, pos, tgt, invf, mod8, n1w, n2w, fnw, w_in, w_out, w_gu, w_down, conv_w8, conv_b, ssm_p, sinks8,
                ssm_nw):
    q, k, v, z, xbc, dtr, h1, cos, sin = _inproj_fwd(x, pos, invf, mod8, n1w, w_in)
    attn = _attn_fwd(q, k, v, sinks8)
    yn, y, hs = _ssd_fwd(xbc, conv_w8, conv_b, dtr, ssm_p, z, ssm_nw)
    x2, h2, mo = _outproj_fwd(attn, yn, x, mod8, n2w, w_out)
    gu, act, dx3, sm_f = _ffn_fwd_loss(h2, x2, tgt, mod8, fnw, w_gu, w_down)
    dx2, dff, dgu, dmix, dattn, dyn, sm_b = _ffn_bwd(dx3, gu, x2, mo, mod8, n2w, w_gu, w_down, w_out)
    dzxd, sm_s = _ssd_bwd(dyn, y, z, xbc, conv_w8, conv_b, dtr, ssm_p, ssm_nw, hs)
    dq, dkv, dsink = _attn_bwd(q, k, v, attn, dattn, cos, sin, sinks8)
    gx, sm_i = _inproj_bwd(dq, dkv, dzxd, x, dx2, mod8, n1w, w_in)
    pack = _pack_small(sm_f, sm_b, sm_s, sm_i, dsink)
    dw_in = (_wgrad(h1, dq, "wgrad_in_q"), _wgrad(h1, dkv, "wgrad_in_kv"), _wgrad(h1, dzxd, "wgrad_in_zxd"))
    dw_out = (_wgrad(attn, dmix, "wgrad_out_attn"), _wgrad(yn, dmix, "wgrad_out_ssd"))
    dw_gu = _wgrad(h2, dgu, "wgrad_gate_up")
    dw_down = _wgrad(act, dff, "wgrad_down")
    return gx, pack, dw_in, dw_out, dw_gu, dw_down


def _exchange(gathers, scatters, name):
    items = list(gathers) + list(scatters)
    n_g = len(gathers)
    n = len(items)
    shapes = [_sds((N_DEV,) + a.shape, a.dtype) for a in gathers] + [_sds(a.shape, a.dtype) for a in scatters]

    def body(*refs):
        ins, outs = refs[:n], refs[n:2 * n]
        send_sems, recv_sems, loc_sems = refs[2 * n:]
        xi, yi, ci = lax.axis_index("x"), lax.axis_index("y"), lax.axis_index("c")
        me = 4 * xi + 2 * yi + ci
        peers = []
        for r in range(1, N_DEV):
            px = 1 - xi if r & 4 else xi
            py = 1 - yi if r & 2 else yi
            pc = 1 - ci if r & 1 else ci
            peers.append(((px, py, pc), 4 * px + 2 * py + pc))

        def remote(t, r):
            dev, peer = peers[r]
            src = ins[t] if t < n_g else ins[t].at[peer]
            return pltpu.make_async_remote_copy(
                src_ref=src, dst_ref=outs[t].at[me], send_sem=send_sems.at[t, r], recv_sem=recv_sems.at[t, r],
                device_id=dev, device_id_type=pl.DeviceIdType.MESH)

        def arrival(t, r):
            dev, peer = peers[r]
            src = ins[t] if t < n_g else ins[t].at[peer]
            return pltpu.make_async_remote_copy(
                src_ref=src, dst_ref=outs[t].at[peer], send_sem=send_sems.at[t, r], recv_sem=recv_sems.at[t, r],
                device_id=dev, device_id_type=pl.DeviceIdType.MESH)

        local = [pltpu.make_async_copy(ins[t] if t < n_g else ins[t].at[me], outs[t].at[me], loc_sems.at[t])
                 for t in range(n)]
        for t in range(n):
            local[t].start()
            for r in range(N_DEV - 1):
                remote(t, r).start()
        for t in range(n):
            for r in range(N_DEV - 1):
                arrival(t, r).wait_recv()
        for t in range(n):
            for r in range(N_DEV - 1):
                remote(t, r).wait_send()
            local[t].wait()

    any_spec = pl.BlockSpec(memory_space=pl.ANY)
    return _pcall(
        body, name=name, in_specs=[any_spec] * n, out_specs=[any_spec] * n, out_shape=shapes,
        scratch_shapes=[pltpu.SemaphoreType.DMA((n, N_DEV - 1)), pltpu.SemaphoreType.DMA((n, N_DEV - 1)),
                        pltpu.SemaphoreType.DMA((n,))],
    )(*items)


def _ada_fwd(c_all, w_cols, b_cols):
    def body(c_ref, w_ref, b_ref, o_ref):
        cv = c_ref[...]
        sc = (cv * _sigmoid(cv)).astype(BF16)
        o_ref[...] = _dot(sc, w_ref[...].astype(BF16)) + b_ref[...]

    return _pcall(body, name="ada_fwd", out_shape=_sds((N_DEV, w_cols.shape[1]), F32),
                  compiler_params=_params(0))(c_all, w_cols, b_cols)


def _adamw(w, g, m, v):
    m2 = ADAM_B1 * m + (1.0 - ADAM_B1) * g
    v2 = ADAM_B2 * v + (1.0 - ADAM_B2) * (g * g)
    m_hat = m2 / (1.0 - ADAM_B1 ** ADAM_STEP)
    v_hat = v2 / (1.0 - ADAM_B2 ** ADAM_STEP)
    delta = -ADAM_LR * (m_hat / (jnp.sqrt(v_hat) + ADAM_EPS) + ADAM_WD * w)
    return delta, m2, v2


def _sum_adamw(parts, w, m, v, name):
    rws, cols = w.shape
    tr = _largest_divisor(rws, (256, 176, 128))

    def body(p_ref, w_ref, m_ref, v_ref, g_ref, d_ref, mo_ref, vo_ref):
        g = p_ref[0].astype(F32)
        for dev in range(1, N_DEV):
            g = g + p_ref[dev].astype(F32)
        g_ref[...] = g
        d_ref[...], mo_ref[...], vo_ref[...] = _adamw(w_ref[...], g, m_ref[...], v_ref[...])

    blk = pl.BlockSpec((tr, cols), lambda i: (i, 0))
    return _pcall(
        body, name=name, grid=(rws // tr,),
        in_specs=[pl.BlockSpec((N_DEV, tr, cols), lambda i: (0, i, 0)), blk, blk, blk],
        out_specs=[blk] * 4, out_shape=[_sds((rws, cols), F32)] * 4, compiler_params=_params(),
    )(parts, w, m, v)


def _wada_adamw(c_all, dmod_cols, w, m, v):
    rws, cols = w.shape
    tr = 256

    def body(c_ref, dm_ref, w_ref, m_ref, v_ref, g_ref, d_ref, mo_ref, vo_ref):
        cv = c_ref[...]
        sc = (cv * _sigmoid(cv)).astype(BF16)
        g = _dot_tn(sc, dm_ref[...].astype(BF16))
        g_ref[...] = g
        d_ref[...], mo_ref[...], vo_ref[...] = _adamw(w_ref[...], g, m_ref[...], v_ref[...])

    blk = pl.BlockSpec((tr, cols), lambda i: (i, 0))
    return _pcall(
        body, name="wada_adamw", grid=(rws // tr,),
        in_specs=[pl.BlockSpec((N_DEV, tr), lambda i: (0, i)), pl.BlockSpec((N_DEV, cols), lambda i: (0, 0)),
                  blk, blk, blk],
        out_specs=[blk] * 4, out_shape=[_sds((rws, cols), F32)] * 4, compiler_params=_params(),
    )(c_all, dmod_cols, w, m, v)


def _small_reduce(packs):
    def body(p_ref, o_ref):
        tot = p_ref[0]
        for dev in range(1, N_DEV):
            tot = tot + p_ref[dev]
        o_ref[...] = tot
        o_ref[16:17, :] = jnp.zeros((1, D), F32) + (0.5 / D) * jnp.sum(tot[16:17, :])

    return _pcall(body, name="small_reduce", out_shape=_sds((PACK_ROWS, D), F32))(packs)


def _adamw_many(ws, gs, ms, vs):
    k = len(ws)

    def body(*refs):
        for i in range(k):
            w_ref, g_ref, m_ref, v_ref = refs[i], refs[k + i], refs[2 * k + i], refs[3 * k + i]
            d_ref, mo_ref, vo_ref = refs[4 * k + i], refs[5 * k + i], refs[6 * k + i]
            d_ref[...], mo_ref[...], vo_ref[...] = _adamw(w_ref[...], g_ref[...], m_ref[...], v_ref[...])

    shp = [_sds(w.shape, F32) for w in ws]
    outs = _pcall(body, name="adamw_small", out_shape=shp * 3)(*ws, *gs, *ms, *vs)
    return outs[:k], outs[k:2 * k], outs[2 * k:]


def kernel(x, c, positions, w_ada, b_ada, norm1_w, w_in, conv_w, conv_b, dt_bias, a_log, d_skip, attn_sinks, ssm_norm_w, w_out, norm2_w, w_gate_up, w_down, final_norm_w, loss_target, m_w_ada, m_b_ada, m_norm1_w, m_w_in, m_conv_w, m_conv_b, m_dt_bias, m_a_log, m_d_skip, m_attn_sinks, m_ssm_norm_w, m_w_out, m_norm2_w, m_w_gate_up, m_w_down, m_final_norm_w, v_w_ada, v_b_ada, v_norm1_w, v_w_in, v_conv_w, v_conv_b, v_dt_bias, v_a_log, v_d_skip, v_attn_sinks, v_ssm_norm_w, v_w_out, v_norm2_w, v_w_gate_up, v_w_down, v_final_norm_w):
    s = x.shape[1]
    me = 4 * lax.axis_index("x") + 2 * lax.axis_index("y") + lax.axis_index("c")
    ada_cols = N_MOD * D // N_DEV

    c8 = jnp.pad(c, ((0, 7), (0, 0)))
    cw8 = jnp.pad(conv_w[0], ((0, 8 - CONVK), (0, 0)))
    g_c, g_in, g_out, g_gu, g_down, g_cw = _exchange(
        [c8, w_in[0].astype(BF16), w_out[0].astype(BF16), w_gate_up[0].astype(BF16), w_down[0].astype(BF16), cw8],
        [], "gather_weights")
    c_all = g_c[:, 0, :]
    w_in_f = jnp.transpose(g_in, (1, 0, 2)).reshape(D, IN_PROJ)
    w_in_f = jnp.pad(w_in_f, ((0, 0), (0, INP - IN_PROJ)))
    w_out_f = g_out.reshape(D, D)
    w_gu_f = jnp.transpose(g_gu, (1, 0, 2)).reshape(D, 2 * DFF)
    w_down_f = g_down.reshape(DFF, D)
    conv_w8 = jnp.transpose(g_cw, (1, 0, 2)).reshape(8, D)

    b_cols = lax.dynamic_slice(b_ada, (0, me * ada_cols), (1, ada_cols))
    (g_mod,) = _exchange([_ada_fwd(c_all, w_ada[0], b_cols)], [], "gather_mod")
    mod = lax.dynamic_index_in_dim(g_mod, me, axis=1, keepdims=False).reshape(N_MOD, D)
    mod8 = jnp.pad(mod, ((0, 8 - N_MOD), (0, 0)))

    half = HD // 2
    inv_freq = ROPE_THETA ** (-jnp.arange(half, dtype=F32) / half)
    invf = jnp.tile(inv_freq, LB // half).reshape(1, LB)
    lanes = lambda a: jnp.pad(a, ((0, 0), (0, LB - a.shape[1])))
    ssm_p = jnp.pad(jnp.concatenate([lanes(dt_bias), lanes(a_log), lanes(d_skip)], axis=0), ((0, 5), (0, 0)))
    sinks8 = jnp.broadcast_to(attn_sinks.reshape(NQ, 1), (NQ, LB))

    gx, pack, dw_in, dw_out, dw_gu, dw_down = _local_step(
        x[0], positions[0].reshape(s, 1), loss_target[0], invf, mod8, norm1_w, norm2_w, final_norm_w.reshape(1, D),
        w_in_f, w_out_f, w_gu_f, w_down_f, conv_w8, conv_b, ssm_p, sinks8, ssm_norm_w)

    dw_in_f = jnp.concatenate([dw_in[0], dw_in[1], dw_in[2][:, :IN_PROJ - O_Z]], axis=1)
    p_in = jnp.transpose(dw_in_f.reshape(D, N_DEV, IN_PROJ // N_DEV), (1, 0, 2))
    p_out = jnp.concatenate(dw_out, axis=0).reshape(N_DEV, D // N_DEV, D)
    p_gu = jnp.transpose(dw_gu.reshape(D, N_DEV, 2 * DFF // N_DEV), (1, 0, 2))
    p_down = dw_down.reshape(N_DEV, DFF // N_DEV, D)
    g_pack, r_in, r_out, r_gu, r_down = _exchange([pack], [p_in, p_out, p_gu, p_down], "exchange_grads")

    tot = _small_reduce(g_pack)
    loss = tot[16, 0]
    dmod_all = g_pack[:, 0:N_MOD, :].reshape(N_DEV, N_MOD * D)
    dmod_cols = lax.dynamic_slice(dmod_all, (0, me * ada_cols), (N_DEV, ada_cols))

    big = {
        "w_ada": _wada_adamw(c_all, dmod_cols, w_ada[0], m_w_ada[0], v_w_ada[0]),
        "w_in": _sum_adamw(r_in, w_in[0], m_w_in[0], v_w_in[0], "adamw_in"),
        "w_out": _sum_adamw(r_out, w_out[0], m_w_out[0], v_w_out[0], "adamw_out"),
        "w_gate_up": _sum_adamw(r_gu, w_gate_up[0], m_w_gate_up[0], v_w_gate_up[0], "adamw_gate_up"),
        "w_down": _sum_adamw(r_down, w_down[0], m_w_down[0], v_w_down[0], "adamw_down"),
    }
    small_names = ["b_ada", "norm1_w", "conv_w", "conv_b", "dt_bias", "a_log", "d_skip", "attn_sinks", "ssm_norm_w",
                   "norm2_w", "final_norm_w"]
    row15 = tot[15:16, :]
    small_g = {
        "b_ada": tot[0:N_MOD, :].reshape(1, N_MOD * D),
        "norm1_w": tot[6:7, :],
        "conv_w": lax.dynamic_slice(tot[10:14, :], (0, me * (D // N_DEV)), (CONVK, D // N_DEV)),
        "conv_b": tot[9:10, :],
        "dt_bias": row15[:, 0:NH],
        "a_log": row15[:, LB:LB + NH],
        "d_skip": row15[:, 2 * LB:2 * LB + NH],
        "attn_sinks": row15[:, 3 * LB:3 * LB + NQ],
        "ssm_norm_w": tot[14:15, 0:SW],
        "norm2_w": tot[7:8, :],
        "final_norm_w": tot[8:9, :],
    }
    small_w = {"b_ada": b_ada, "norm1_w": norm1_w, "conv_w": conv_w[0], "conv_b": conv_b, "dt_bias": dt_bias,
               "a_log": a_log, "d_skip": d_skip, "attn_sinks": attn_sinks, "ssm_norm_w": ssm_norm_w,
               "norm2_w": norm2_w, "final_norm_w": final_norm_w.reshape(1, D)}
    small_m = {"b_ada": m_b_ada, "norm1_w": m_norm1_w, "conv_w": m_conv_w[0], "conv_b": m_conv_b,
               "dt_bias": m_dt_bias, "a_log": m_a_log, "d_skip": m_d_skip, "attn_sinks": m_attn_sinks,
               "ssm_norm_w": m_ssm_norm_w, "norm2_w": m_norm2_w, "final_norm_w": m_final_norm_w.reshape(1, D)}
    small_v = {"b_ada": v_b_ada, "norm1_w": v_norm1_w, "conv_w": v_conv_w[0], "conv_b": v_conv_b,
               "dt_bias": v_dt_bias, "a_log": v_a_log, "d_skip": v_d_skip, "attn_sinks": v_attn_sinks,
               "ssm_norm_w": v_ssm_norm_w, "norm2_w": v_norm2_w, "final_norm_w": v_final_norm_w.reshape(1, D)}
    s_d, s_m, s_v = _adamw_many([small_w[k] for k in small_names], [small_g[k] for k in small_names],
                                [small_m[k] for k in small_names], [small_v[k] for k in small_names])

    order = ["w_ada", "b_ada", "norm1_w", "w_in", "conv_w", "conv_b", "dt_bias", "a_log", "d_skip", "attn_sinks",
             "ssm_norm_w", "w_out", "norm2_w", "w_gate_up", "w_down", "final_norm_w"]
    lead = {"w_ada", "w_in", "conv_w", "w_out", "w_gate_up", "w_down"}
    grads, deltas, new_m, new_v = [], [], [], []
    for name in order:
        if name in big:
            g, d, m2, v2 = big[name]
        else:
            i = small_names.index(name)
            g, d, m2, v2 = small_g[name], s_d[i], s_m[i], s_v[i]
        if name in lead:
            g, d, m2, v2 = g[None], d[None], m2[None], v2[None]
        if name == "final_norm_w":
            g, d, m2, v2 = g.reshape(D), d.reshape(D), m2.reshape(D), v2.reshape(D)
        grads.append(g)
        deltas.append(d)
        new_m.append(m2)
        new_v.append(v2)
    return (loss, gx[None], *grads, *deltas, *new_m, *new_v)
```

```python
import functools
import math

import jax
import jax.numpy as jnp
from jax import lax
from jax.experimental import pallas as pl
from jax.experimental.pallas import tpu as pltpu

F32 = jnp.float32
BF16 = jnp.bfloat16

N_DEV = 8
D = 1024
HD = 64
NQ = 8
AW = 512
KVW = 128
SW = 512
NST = 128
NH = 8
LB = 128
CONVK = 4
DFF = 2816
N_MOD = 6
IN_PROJ = 2312
INP = 2432
O_Q, O_K, O_V, O_Z, O_XBC, O_DT = 0, 512, 640, 768, 1280, 2304
ZXD = INP - O_Z
EPS = 1e-6
NEG = -1e30
ROPE_THETA = 10000.0
VMEM_LIMIT = 56 * 1024 * 1024

ADAM_LR = 0.001
ADAM_B1 = 0.9
ADAM_B2 = 0.999
ADAM_EPS = 1e-08
ADAM_WD = 0.01
ADAM_STEP = 10

NT_DIMS = (((1,), (1,)), ((), ()))
TN_DIMS = (((0,), (0,)), ((), ()))


def _pcall(body, **kw):
    return pl.pallas_call(body, **kw)


def _sds(shape, dtype):
    return jax.ShapeDtypeStruct(shape, dtype)


def _params(n_grid=1):
    return pltpu.CompilerParams(dimension_semantics=("arbitrary",) * n_grid, vmem_limit_bytes=VMEM_LIMIT)


def _const(shape):
    return pl.BlockSpec(shape, lambda *_: (0,) * len(shape), pipeline_mode=pl.Buffered(1))


def _largest_divisor(n, candidates):
    for cand in candidates:
        if n % cand == 0:
            return cand
    raise ValueError(f"no tile in {candidates} divides {n}")


def _rows(t, w):
    return pl.BlockSpec((t, w), lambda i: (i, 0))


def _dot(a, b):
    return jnp.dot(a, b, preferred_element_type=F32)


def _dot_nt(a, b):
    return lax.dot_general(a, b, NT_DIMS, preferred_element_type=F32)


def _dot_tn(a, b):
    return lax.dot_general(a, b, TN_DIMS, preferred_element_type=F32)


def _sigmoid(v):
    return 1.0 / (1.0 + jnp.exp(-v))


def _softplus(v):
    return jnp.maximum(v, 0.0) + jnp.log1p(jnp.exp(-jnp.abs(v)))


def _rope_sign_mask(shape):
    lane = lax.broadcasted_iota(jnp.int32, shape, 1)
    return (lane % HD) < (HD // 2)


def _rope(t, cs, sn, inverse):
    r_dn = pltpu.roll(t, HD // 2, 1)
    r_up = pltpu.roll(t, LB - HD // 2, 1)
    first = _rope_sign_mask(t.shape)
    if inverse:
        rot = jnp.where(first, r_up, -r_dn)
    else:
        rot = jnp.where(first, -r_up, r_dn)
    return t * cs + rot * sn


def _norm_mod_fwd(xv, nw, shift, scale):
    r = lax.rsqrt(jnp.mean(xv * xv, axis=-1, keepdims=True) + EPS)
    xh = xv * r
    return (xh * nw) * (1.0 + scale) + shift


def _norm_mod_bwd(xv, dh, nw, scale):
    r = lax.rsqrt(jnp.mean(xv * xv, axis=-1, keepdims=True) + EPS)
    xh = xv * r
    xn = xh * nw
    d_shift = jnp.sum(dh, axis=0, keepdims=True)
    d_scale = jnp.sum(dh * xn, axis=0, keepdims=True)
    dxn = dh * (1.0 + scale)
    d_w = jnp.sum(dxn * xh, axis=0, keepdims=True)
    dxh = dxn * nw
    dx = r * (dxh - xh * jnp.mean(dxh * xh, axis=-1, keepdims=True))
    return dx, d_shift, d_scale, d_w


def _inproj_fwd(x, pos, invf, mod8, n1w, w_in):
    s = x.shape[0]
    tt = min(512, s)

    def body(x_ref, pos_ref, invf_ref, mod_ref, nw_ref, w_ref,
             q_ref, k_ref, v_ref, z_ref, xbc_ref, dtr_ref, h1_ref, cos_ref, sin_ref):
        h = _norm_mod_fwd(x_ref[...], nw_ref[...], mod_ref[0:1, :], mod_ref[1:2, :])
        hb = h.astype(BF16)
        h1_ref[...] = hb
        proj = _dot(hb, w_ref[...])
        ang = pos_ref[...].astype(F32) * invf_ref[...]
        cs = jnp.cos(ang)
        sn = jnp.sin(ang)
        cos_ref[...] = cs
        sin_ref[...] = sn
        for a in range(AW // LB):
            q_ref[:, a * LB:(a + 1) * LB] = _rope(proj[:, O_Q + a * LB:O_Q + (a + 1) * LB], cs, sn, False).astype(BF16)
        k_ref[...] = _rope(proj[:, O_K:O_V], cs, sn, False).astype(BF16)
        v_ref[...] = proj[:, O_V:O_Z].astype(BF16)
        z_ref[...] = proj[:, O_Z:O_XBC]
        xbc_ref[...] = proj[:, O_XBC:O_DT]
        dtr_ref[...] = proj[:, O_DT:INP]

    return _pcall(
        body, name="inproj_fwd", grid=(s // tt,),
        in_specs=[_rows(tt, D), _rows(tt, 1), _const((1, LB)), _const((8, D)), _const((1, D)), _const((D, INP))],
        out_specs=[_rows(tt, AW), _rows(tt, KVW), _rows(tt, KVW), _rows(tt, SW), _rows(tt, D), _rows(tt, LB),
                   _rows(tt, D), _rows(tt, LB), _rows(tt, LB)],
        out_shape=[_sds((s, AW), BF16), _sds((s, KVW), BF16), _sds((s, KVW), BF16), _sds((s, SW), F32),
                   _sds((s, D), F32), _sds((s, LB), F32), _sds((s, D), BF16), _sds((s, LB), F32), _sds((s, LB), F32)],
        compiler_params=_params(),
    )(x, pos, invf, mod8, n1w, w_in)


def _attn_mask(n):
    row = lax.broadcasted_iota(jnp.int32, (LB, 2 * LB), 0)
    col = lax.broadcasted_iota(jnp.int32, (LB, 2 * LB), 1)
    diff = row + LB - col
    return (diff >= 0) & (diff < LB) & ((col >= LB) | (n > 0))


def _attn_probs(qh, kg, mask, sink):
    sc = _dot_nt(qh, kg) * (1.0 / math.sqrt(HD))
    sc = jnp.where(mask, sc, NEG)
    m = jnp.maximum(jnp.max(sc, axis=-1, keepdims=True), sink)
    p = jnp.exp(sc - m)
    es = jnp.exp(sink - m)
    denom = jnp.sum(p, axis=-1, keepdims=True) + es
    return p / denom, es / denom


def _attn_fwd_block(n, q_ref, kp_ref, kc_ref, vp_ref, vc_ref, sink_ref, o_ref):
    kcat = jnp.concatenate([kp_ref[...], kc_ref[...]], axis=0)
    vcat = jnp.concatenate([vp_ref[...], vc_ref[...]], axis=0)
    mask = _attn_mask(n)
    qv = q_ref[...]
    sinks = sink_ref[...]
    outs = []
    for h in range(NQ):
        g = h // 4
        probs, _ = _attn_probs(qv[:, h * HD:(h + 1) * HD], kcat[:, g * HD:(g + 1) * HD], mask, sinks[h:h + 1, 0:1])
        outs.append(_dot(probs.astype(BF16), vcat[:, g * HD:(g + 1) * HD]))
    o_ref[...] = jnp.concatenate(outs, axis=1).astype(BF16)


def _cumsum_rows(a, reverse):
    row = lax.broadcasted_iota(jnp.int32, a.shape, 0)
    step = 1
    while step < LB:
        if reverse:
            a = a + jnp.where(row < LB - step, pltpu.roll(a, LB - step, 0), 0.0)
        else:
            a = a + jnp.where(row >= step, pltpu.roll(a, step, 0), 0.0)
        step *= 2
    return a


def _conv_shifts(prev, cur):
    row = lax.broadcasted_iota(jnp.int32, cur.shape, 0)
    out = [cur]
    for j in range(1, CONVK):
        out.append(jnp.where(row < j, pltpu.roll(prev, j, 0), pltpu.roll(cur, j, 0)))
    return out


def _ssd_pre(xp_ref, xc_ref, cw_ref, cb_ref, dtr_ref, sp_ref, n):
    cur = xc_ref[...]
    prev = jnp.where(n > 0, xp_ref[...], 0.0)
    sh = _conv_shifts(prev, cur)
    u = cb_ref[...] + cw_ref[CONVK - 1:CONVK, :] * sh[0]
    for j in range(1, CONVK):
        u = u + cw_ref[CONVK - 1 - j:CONVK - j, :] * sh[j]
    sg_u = _sigmoid(u)
    xc = u * sg_u
    pre = dtr_ref[...] + sp_ref[0:1, :]
    dt = _softplus(pre)
    a_neg = -jnp.exp(sp_ref[1:2, :])
    acs = _cumsum_rows(dt * a_neg, False)
    return sh, u, sg_u, xc, pre, dt, a_neg, acs


def _gated_norm_fwd(y, z, nw):
    sz = z * _sigmoid(z)
    yz = y * sz
    parts = []
    for g in range(2):
        t = yz[:, g * 256:(g + 1) * 256]
        parts.append(t * lax.rsqrt(jnp.mean(t * t, axis=-1, keepdims=True) + EPS))
    return jnp.concatenate(parts, axis=1) * nw


def _ssd_fwd_block(n, xp_ref, xc_ref, cw_ref, cb_ref, dtr_ref, sp_ref, z_ref, nw_ref, yn_ref, y_ref, hs_ref, h_scr):
    @pl.when(n == 0)
    def _():
        h_scr[...] = jnp.zeros_like(h_scr)

    hs_ref[0] = h_scr[...]
    _, _, _, xc, _, dt, _, acs = _ssd_pre(xp_ref, xc_ref, cw_ref, cb_ref, dtr_ref, sp_ref, n)
    acs_t = jnp.transpose(acs)
    e_all = jnp.exp(acs)
    alast = acs[LB - 1:LB, :]
    dte_all = jnp.exp(alast - acs)
    elast = jnp.exp(alast)
    causal = (lax.broadcasted_iota(jnp.int32, (LB, LB), 0) >= lax.broadcasted_iota(jnp.int32, (LB, LB), 1))
    for g in range(2):
        bg = xc[:, SW + g * NST:SW + (g + 1) * NST]
        cg = xc[:, SW + 2 * NST + g * NST:SW + 2 * NST + (g + 1) * NST]
        bgb = bg.astype(BF16)
        cgb = cg.astype(BF16)
        cb = _dot_nt(cgb, bgb)
        for jj in range(4):
            j = 4 * g + jj
            sl = slice(j * HD, (j + 1) * HD)
            dm = jnp.exp(jnp.where(causal, acs[:, j:j + 1] - acs_t[j:j + 1, :], NEG))
            xs_j = xc[:, sl]
            xdtb = (xs_j * dt[:, j:j + 1]).astype(BF16)
            hj = h_scr[sl, :]
            yd = _dot((cb * dm).astype(BF16), xdtb)
            yo = e_all[:, j:j + 1] * _dot_nt(cgb, hj.astype(BF16))
            bd = (bg * dte_all[:, j:j + 1]).astype(BF16)
            h_scr[sl, :] = hj * elast[:, j:j + 1] + _dot_tn(xdtb, bd)
            y_ref[:, sl] = yd + yo + xs_j * sp_ref[2:3, j:j + 1]
    yn_ref[...] = _gated_norm_fwd(y_ref[...], z_ref[...], nw_ref[...]).astype(BF16)


def _mixer_fwd(q, k, v, sinks8, xbc, conv_w8, conv_b, dtr, ssm_p, z, nw, gathers):
    s = q.shape[0]
    nb = s // LB
    prev = lambda n: (jnp.maximum(n - 1, 0), 0)
    cur = lambda n: (n, 0)
    items, ex_shapes, n_g = _exchange_items(gathers, [])
    ne = len(items)

    def body(*refs):
        a_in, s_in, ex_in = refs[:6], refs[6:14], refs[14:14 + ne]
        o_ref, yn_ref, y_ref, hs_ref = refs[14 + ne:18 + ne]
        ex_out = refs[18 + ne:18 + 2 * ne]
        h_scr = refs[18 + 2 * ne]
        n = pl.program_id(0)

        @pl.when(n == 0)
        def _():
            _Exchange(n_g, ex_in, ex_out, refs[19 + 2 * ne:]).start()

        _attn_fwd_block(n, *a_in, o_ref)
        _ssd_fwd_block(n, *s_in, yn_ref, y_ref, hs_ref, h_scr)

        @pl.when(n == nb - 1)
        def _():
            _Exchange(n_g, ex_in, ex_out, refs[19 + 2 * ne:]).finish()

    any_spec = pl.BlockSpec(memory_space=pl.ANY)
    outs = _pcall(
        body, name="mixer_fwd", grid=(nb,),
        in_specs=[pl.BlockSpec((LB, AW), cur), pl.BlockSpec((LB, KVW), prev), pl.BlockSpec((LB, KVW), cur),
                  pl.BlockSpec((LB, KVW), prev), pl.BlockSpec((LB, KVW), cur), _const((8, LB)),
                  pl.BlockSpec((LB, D), prev), pl.BlockSpec((LB, D), cur), _const((8, D)), _const((1, D)),
                  pl.BlockSpec((LB, LB), cur), _const((8, LB)), pl.BlockSpec((LB, SW), cur), _const((1, SW))]
        + [any_spec] * ne,
        out_specs=[pl.BlockSpec((LB, AW), cur), pl.BlockSpec((LB, SW), cur), pl.BlockSpec((LB, SW), cur),
                   pl.BlockSpec((1, NH * HD, NST), lambda n: (n, 0, 0))] + [any_spec] * ne,
        out_shape=[_sds((s, AW), BF16), _sds((s, SW), BF16), _sds((s, SW), F32), _sds((nb, NH * HD, NST), F32)]
        + ex_shapes,
        scratch_shapes=[pltpu.VMEM((NH * HD, NST), F32)] + _exchange_sems(ne),
        compiler_params=_params(),
    )(q, k, k, v, v, sinks8, xbc, xbc, conv_w8, conv_b, dtr, ssm_p, z, nw, *items)
    return outs[0], outs[1], outs[2], outs[3], outs[4:]


def _outproj_fwd(attn, yn, x, mod8, n2w, w_out):
    s = x.shape[0]
    tt = min(512, s)

    def body(a_ref, y_ref, x_ref, mod_ref, nw_ref, w_ref, x2_ref, h2_ref, mo_ref):
        mix = jnp.concatenate([a_ref[...], y_ref[...]], axis=1)
        mo = _dot(mix, w_ref[...])
        mo_ref[...] = mo.astype(BF16)
        x2 = x_ref[...] + mod_ref[2:3, :] * mo
        x2_ref[...] = x2
        h2_ref[...] = _norm_mod_fwd(x2, nw_ref[...], mod_ref[3:4, :], mod_ref[4:5, :]).astype(BF16)

    return _pcall(
        body, name="outproj_fwd", grid=(s // tt,),
        in_specs=[_rows(tt, AW), _rows(tt, SW), _rows(tt, D), _const((8, D)), _const((1, D)), _const((D, D))],
        out_specs=[_rows(tt, D), _rows(tt, D), _rows(tt, D)],
        out_shape=[_sds((s, D), F32), _sds((s, D), BF16), _sds((s, D), BF16)],
        compiler_params=_params(),
    )(attn, yn, x, mod8, n2w, w_out)


def _ffn_fwd_loss(h2, x2, tgt, mod8, fnw, w_gu, w_down):
    s = x2.shape[0]
    tf = min(256, s)

    def body(h_ref, x2_ref, t_ref, mod_ref, fw_ref, wgu_ref, wd_ref, gu_ref, act_ref, dx3_ref, sm_ref):
        i = pl.program_id(0)

        @pl.when(i == 0)
        def _():
            sm_ref[...] = jnp.zeros_like(sm_ref)

        gu = _dot(h_ref[...], wgu_ref[...])
        gu_ref[...] = gu.astype(BF16)
        g = gu[:, :DFF]
        act = (g * _sigmoid(g) * gu[:, DFF:]).astype(BF16)
        act_ref[...] = act
        ff = _dot(act, wd_ref[...])
        x3 = x2_ref[...] + mod_ref[5:6, :] * ff
        r = lax.rsqrt(jnp.mean(x3 * x3, axis=-1, keepdims=True) + EPS)
        xh = x3 * r
        fw = fw_ref[...]
        err = xh * fw - t_ref[...]
        dy = err * (1.0 / D)
        dxh = dy * fw
        dx3 = r * (dxh - xh * jnp.mean(dxh * xh, axis=-1, keepdims=True))
        dx3_ref[...] = dx3
        sm_ref[0:1, :] += jnp.sum(dx3 * ff, axis=0, keepdims=True)
        sm_ref[1:2, :] += jnp.sum(dy * xh, axis=0, keepdims=True)
        sm_ref[2:3, :] += jnp.sum(err * err, axis=0, keepdims=True)

    return _pcall(
        body, name="ffn_fwd_loss", grid=(s // tf,),
        in_specs=[_rows(tf, D), _rows(tf, D), _rows(tf, D), _const((8, D)), _const((1, D)),
                  _const((D, 2 * DFF)), _const((DFF, D))],
        out_specs=[_rows(tf, 2 * DFF), _rows(tf, DFF), _rows(tf, D), pl.BlockSpec((8, D), lambda i: (0, 0))],
        out_shape=[_sds((s, 2 * DFF), BF16), _sds((s, DFF), BF16), _sds((s, D), F32), _sds((8, D), F32)],
        compiler_params=_params(),
    )(h2, x2, tgt, mod8, fnw, w_gu, w_down)


def _ffn_bwd(dx3, gu, x2, mixout, mod8, n2w, w_gu, w_down, w_out):
    s = x2.shape[0]
    tb = min(256, s)

    def body(dx3_ref, gu_ref, x2_ref, mo_ref, mod_ref, nw_ref, wgu_ref, wd_ref, wo_ref,
             dx2_ref, dff_ref, dgu_ref, dmix_ref, dattn_ref, dyn_ref, sm_ref):
        i = pl.program_id(0)

        @pl.when(i == 0)
        def _():
            sm_ref[...] = jnp.zeros_like(sm_ref)

        dx3 = dx3_ref[...]
        dff = (dx3 * mod_ref[5:6, :]).astype(BF16)
        dff_ref[...] = dff
        dact = _dot_nt(dff, wd_ref[...])
        g = gu_ref[:, :DFF].astype(F32)
        u = gu_ref[:, DFF:].astype(F32)
        sg = _sigmoid(g)
        dgu = jnp.concatenate([dact * u * sg * (1.0 + g * (1.0 - sg)), dact * g * sg], axis=1).astype(BF16)
        dgu_ref[...] = dgu
        dh2 = _dot_nt(dgu, wgu_ref[...])
        dxn, d_shift, d_scale, d_w = _norm_mod_bwd(x2_ref[...], dh2, nw_ref[...], mod_ref[4:5, :])
        dx2 = dx3 + dxn
        dx2_ref[...] = dx2
        sm_ref[0:1, :] += d_shift
        sm_ref[1:2, :] += d_scale
        sm_ref[2:3, :] += d_w
        sm_ref[3:4, :] += jnp.sum(dx2 * mo_ref[...].astype(F32), axis=0, keepdims=True)
        dmix = (dx2 * mod_ref[2:3, :]).astype(BF16)
        dmix_ref[...] = dmix
        dmi = _dot_nt(dmix, wo_ref[...])
        dattn_ref[...] = dmi[:, :AW].astype(BF16)
        dyn_ref[...] = dmi[:, AW:]

    return _pcall(
        body, name="ffn_bwd", grid=(s // tb,),
        in_specs=[_rows(tb, D), _rows(tb, 2 * DFF), _rows(tb, D), _rows(tb, D), _const((8, D)), _const((1, D)),
                  _const((D, 2 * DFF)), _const((DFF, D)), _const((D, D))],
        out_specs=[_rows(tb, D), _rows(tb, D), _rows(tb, 2 * DFF), _rows(tb, D), _rows(tb, AW), _rows(tb, SW),
                   pl.BlockSpec((8, D), lambda i: (0, 0))],
        out_shape=[_sds((s, D), F32), _sds((s, D), BF16), _sds((s, 2 * DFF), BF16), _sds((s, D), BF16),
                   _sds((s, AW), BF16), _sds((s, SW), F32), _sds((8, D), F32)],
        compiler_params=_params(),
    )(dx3, gu, x2, mixout, mod8, n2w, w_gu, w_down, w_out)


def _ssd_bwd_block(i, n, *refs):
    def run(dyn_ref, y_ref, z_ref, xp_ref, xc_ref, cw_ref, cb_ref, dtr_ref, sp_ref, nw_ref, hs_ref,
            dzxd_ref, sm_ref, dh_scr, dun_scr, dxc_scr):
        @pl.when(i == 0)
        def _():
            dh_scr[...] = jnp.zeros_like(dh_scr)
            dun_scr[...] = jnp.zeros_like(dun_scr)
            sm_ref[...] = jnp.zeros_like(sm_ref)

        sh, u, sg_u, xc, pre, dt, a_neg, acs = _ssd_pre(xp_ref, xc_ref, cw_ref, cb_ref, dtr_ref, sp_ref, n)
        acs_t = jnp.transpose(acs)
        e_all = jnp.exp(acs)
        alast = acs[LB - 1:LB, :]
        dte_all = jnp.exp(alast - acs)
        elast = jnp.exp(alast)
        riota = lax.broadcasted_iota(jnp.int32, (LB, LB), 0)
        liota = lax.broadcasted_iota(jnp.int32, (LB, LB), 1)
        causal = riota >= liota
        lane1 = lax.broadcasted_iota(jnp.int32, (1, LB), 1)

        z = z_ref[...]
        y = y_ref[...]
        sgz = _sigmoid(z)
        sz = z * sgz
        yz = y * sz
        nwv = nw_ref[...]
        dyn_v = dyn_ref[...]
        dyhat = dyn_v * nwv
        yhat_parts, dyz_parts = [], []
        for g in range(2):
            gs = slice(g * 256, (g + 1) * 256)
            t = yz[:, gs]
            rg = lax.rsqrt(jnp.mean(t * t, axis=-1, keepdims=True) + EPS)
            yh = t * rg
            dyh = dyhat[:, gs]
            yhat_parts.append(yh)
            dyz_parts.append(rg * (dyh - yh * jnp.mean(dyh * yh, axis=-1, keepdims=True)))
        yhat = jnp.concatenate(yhat_parts, axis=1)
        dyz = jnp.concatenate(dyz_parts, axis=1)
        sm_ref[5:6, 0:SW] += jnp.sum(dyn_v * yhat, axis=0, keepdims=True)
        dy = dyz * sz
        dzxd_ref[:, 0:SW] = (dyz * y * sgz * (1.0 + z * (1.0 - sgz))).astype(BF16)

        dacs = jnp.zeros((LB, LB), F32)
        dacs_t = jnp.zeros((LB, LB), F32)
        ddt = jnp.zeros((LB, LB), F32)
        dalast = jnp.zeros((1, LB), F32)
        ddsk = jnp.zeros((1, LB), F32)
        for g in range(2):
            bg = xc[:, SW + g * NST:SW + (g + 1) * NST]
            cg = xc[:, SW + 2 * NST + g * NST:SW + 2 * NST + (g + 1) * NST]
            bgb = bg.astype(BF16)
            cgb = cg.astype(BF16)
            cb = _dot_nt(cgb, bgb)
            dcb = jnp.zeros((LB, LB), F32)
            dcg = jnp.zeros((LB, NST), F32)
            dbg = jnp.zeros((LB, NST), F32)
            for jj in range(4):
                j = 4 * g + jj
                sl = slice(j * HD, (j + 1) * HD)
                onl = (liota == j).astype(F32)
                onr = (riota == j).astype(F32)
                on1 = (lane1 == j).astype(F32)
                dm = jnp.exp(jnp.where(causal, acs[:, j:j + 1] - acs_t[j:j + 1, :], NEG))
                xs_j = xc[:, sl]
                dt_j = dt[:, j:j + 1]
                xdtb = (xs_j * dt_j).astype(BF16)
                m = cb * dm
                hj = hs_ref[0, sl, :]
                hb = hj.astype(BF16)
                dy_j = dy[:, sl]
                dyb = dy_j.astype(BF16)
                dxs_j = dy_j * sp_ref[2:3, j:j + 1]
                ddsk = ddsk + on1 * jnp.sum(dy_j * xs_j)
                dm_ = _dot_nt(dyb, xdtb)
                dxdt = _dot_tn(m.astype(BF16), dyb)
                gmat = dm_ * m
                dcb = dcb + dm_ * dm
                dacs = dacs + onl * jnp.sum(gmat, axis=1, keepdims=True)
                dacs_t = dacs_t - onr * jnp.sum(gmat, axis=0, keepdims=True)
                qm = _dot_nt(cgb, hb)
                eac = e_all[:, j:j + 1]
                dqb = (dy_j * eac).astype(BF16)
                dcg = dcg + _dot(dqb, hb)
                dh_prev = _dot_tn(dqb, cgb)
                dacs = dacs + onl * (jnp.sum(dy_j * qm, axis=1, keepdims=True) * eac)
                dhn = dh_scr[sl, :]
                dhnb = dhn.astype(BF16)
                dte_j = dte_all[:, j:j + 1]
                bdb = (bg * dte_j).astype(BF16)
                dxdt = dxdt + _dot_nt(bdb, dhnb)
                dbd = _dot(xdtb, dhnb)
                dbg = dbg + dbd * dte_j
                t1 = jnp.sum(dbd * bg, axis=1, keepdims=True) * dte_j
                dacs = dacs - onl * t1
                el = elast[:, j:j + 1]
                dalast = dalast + on1 * (jnp.sum(t1) + jnp.sum(dhn * hj) * el)
                dh_scr[sl, :] = dhn * el + dh_prev
                dxc_scr[:, sl] = dxs_j + dxdt * dt_j
                ddt = ddt + onl * jnp.sum(dxdt * xs_j, axis=1, keepdims=True)
            dcbb = dcb.astype(BF16)
            dxc_scr[:, SW + g * NST:SW + (g + 1) * NST] = dbg + _dot_tn(dcbb, cgb)
            dxc_scr[:, SW + 2 * NST + g * NST:SW + 2 * NST + (g + 1) * NST] = dcg + _dot(dcbb, bgb)
        dacs = dacs + jnp.transpose(dacs_t) + jnp.where(riota == LB - 1, dalast, 0.0)
        da = _cumsum_rows(dacs, True)
        ddt = ddt + da * a_neg
        da_log = jnp.sum(da * dt, axis=0, keepdims=True) * a_neg
        ddtr = ddt * _sigmoid(pre)
        dzxd_ref[:, SW + D:ZXD] = ddtr.astype(BF16)
        sm_ref[6:7, 0:LB] += jnp.sum(ddtr, axis=0, keepdims=True)
        sm_ref[6:7, LB:2 * LB] += da_log
        sm_ref[6:7, 2 * LB:3 * LB] += ddsk

        du = dxc_scr[...] * (sg_u * (1.0 + u * (1.0 - sg_u)))
        sm_ref[0:1, :] += jnp.sum(du, axis=0, keepdims=True)
        for k in range(CONVK):
            sm_ref[1 + k:2 + k, :] += jnp.sum(du * sh[CONVK - 1 - k], axis=0, keepdims=True)
        dun = dun_scr[...]
        row = lax.broadcasted_iota(jnp.int32, du.shape, 0)
        dxbc = cw_ref[CONVK - 1:CONVK, :] * du
        for j in range(1, CONVK):
            adv = jnp.where(row >= LB - j, pltpu.roll(dun, LB - j, 0), pltpu.roll(du, LB - j, 0))
            dxbc = dxbc + cw_ref[CONVK - 1 - j:CONVK - j, :] * adv
        dun_scr[...] = du
        dzxd_ref[:, SW:SW + D] = dxbc.astype(BF16)

    run(*refs)


def _attn_bwd_block(i, n, q_ref, kp_ref, kc_ref, vp_ref, vc_ref, o_ref, do_ref, cos_ref, sin_ref, sink_ref,
                    dq_ref, dkv_ref, ds_ref, ck_scr, cv_scr, dq_scr, dk_scr, dv_scr):
    @pl.when(i == 0)
    def _():
        ds_ref[...] = jnp.zeros_like(ds_ref)
        ck_scr[...] = jnp.zeros_like(ck_scr)
        cv_scr[...] = jnp.zeros_like(cv_scr)

    kcat = jnp.concatenate([kp_ref[...], kc_ref[...]], axis=0)
    vcat = jnp.concatenate([vp_ref[...], vc_ref[...]], axis=0)
    mask = _attn_mask(n)
    lane1 = lax.broadcasted_iota(jnp.int32, (1, LB), 1)
    for g in range(2):
        kg = kcat[:, g * HD:(g + 1) * HD]
        vg = vcat[:, g * HD:(g + 1) * HD]
        dkg = jnp.zeros((2 * LB, HD), F32)
        dvg = jnp.zeros((2 * LB, HD), F32)
        for hh in range(4):
            h = 4 * g + hh
            sl = slice(h * HD, (h + 1) * HD)
            qh = q_ref[:, sl]
            probs, psink = _attn_probs(qh, kg, mask, sink_ref[h:h + 1, 0:1])
            doh = do_ref[:, sl]
            delta = jnp.sum(doh.astype(F32) * o_ref[:, sl].astype(F32), axis=1, keepdims=True)
            dsc = probs * (_dot_nt(doh, vg) - delta)
            ds_ref[h:h + 1, :] += jnp.where(lane1 == 0, -jnp.sum(psink * delta), 0.0)
            dsb = dsc.astype(BF16)
            dq_scr[:, sl] = _dot(dsb, kg) * (1.0 / math.sqrt(HD))
            dkg = dkg + _dot_tn(dsb, qh) * (1.0 / math.sqrt(HD))
            dvg = dvg + _dot_tn(probs.astype(BF16), doh)
        dk_scr[:, g * HD:(g + 1) * HD] = dkg
        dv_scr[:, g * HD:(g + 1) * HD] = dvg
    cs = cos_ref[...]
    sn = sin_ref[...]
    for a in range(AW // LB):
        dq_ref[:, a * LB:(a + 1) * LB] = _rope(dq_scr[:, a * LB:(a + 1) * LB], cs, sn, True).astype(BF16)
    dkv_ref[:, 0:KVW] = _rope(ck_scr[...] + dk_scr[LB:2 * LB, :], cs, sn, True).astype(BF16)
    dkv_ref[:, KVW:2 * KVW] = (cv_scr[...] + dv_scr[LB:2 * LB, :]).astype(BF16)
    ck_scr[...] = dk_scr[0:LB, :]
    cv_scr[...] = dv_scr[0:LB, :]


def _mixer_bwd(q, k, v, o, do, cos, sin, sinks8, dyn, y, z, xbc, conv_w8, conv_b, dtr, ssm_p, nw, hs, scatters):
    s = q.shape[0]
    nb = s // LB
    cur = lambda i: (nb - 1 - i, 0)
    prev = lambda i: (jnp.maximum(nb - 2 - i, 0), 0)
    n_in = 21
    items, ex_shapes, n_g = _exchange_items([], scatters)
    ne = len(items)

    def body(*refs):
        i = pl.program_id(0)
        n = nb - 1 - i
        a_in, s_in, ex_in = refs[:10], refs[10:n_in], refs[n_in:n_in + ne]
        dq_ref, dkv_ref, ds_ref, dzxd_ref, sm_ref = refs[n_in + ne:n_in + ne + 5]
        ex_out = refs[n_in + ne + 5:n_in + 2 * ne + 5]
        ck_scr, cv_scr, dq_scr, dk_scr, dv_scr, dh_scr, dun_scr, dxc_scr = refs[n_in + 2 * ne + 5:n_in + 2 * ne + 13]
        sems = refs[n_in + 2 * ne + 13:]

        @pl.when(i == 0)
        def _():
            _Exchange(n_g, ex_in, ex_out, sems).start()

        _attn_bwd_block(i, n, *a_in, dq_ref, dkv_ref, ds_ref, ck_scr, cv_scr, dq_scr, dk_scr, dv_scr)
        _ssd_bwd_block(i, n, *s_in, dzxd_ref, sm_ref, dh_scr, dun_scr, dxc_scr)

        @pl.when(i == nb - 1)
        def _():
            _Exchange(n_g, ex_in, ex_out, sems).finish()

    any_spec = pl.BlockSpec(memory_space=pl.ANY)
    outs = _pcall(
        body, name="mixer_bwd", grid=(nb,),
        in_specs=[pl.BlockSpec((LB, AW), cur), pl.BlockSpec((LB, KVW), prev), pl.BlockSpec((LB, KVW), cur),
                  pl.BlockSpec((LB, KVW), prev), pl.BlockSpec((LB, KVW), cur), pl.BlockSpec((LB, AW), cur),
                  pl.BlockSpec((LB, AW), cur), pl.BlockSpec((LB, LB), cur), pl.BlockSpec((LB, LB), cur),
                  _const((8, LB)),
                  pl.BlockSpec((LB, SW), cur), pl.BlockSpec((LB, SW), cur), pl.BlockSpec((LB, SW), cur),
                  pl.BlockSpec((LB, D), prev), pl.BlockSpec((LB, D), cur), _const((8, D)), _const((1, D)),
                  pl.BlockSpec((LB, LB), cur), _const((8, LB)), _const((1, SW)),
                  pl.BlockSpec((1, NH * HD, NST), lambda i: (nb - 1 - i, 0, 0))] + [any_spec] * ne,
        out_specs=[pl.BlockSpec((LB, AW), cur), pl.BlockSpec((LB, 2 * KVW), cur),
                   pl.BlockSpec((8, LB), lambda i: (0, 0)),
                   pl.BlockSpec((LB, ZXD), cur), pl.BlockSpec((8, D), lambda i: (0, 0))] + [any_spec] * ne,
        out_shape=[_sds((s, AW), BF16), _sds((s, 2 * KVW), BF16), _sds((8, LB), F32),
                   _sds((s, ZXD), BF16), _sds((8, D), F32)] + ex_shapes,
        scratch_shapes=[pltpu.VMEM((LB, KVW), F32), pltpu.VMEM((LB, KVW), F32), pltpu.VMEM((LB, AW), F32),
                        pltpu.VMEM((2 * LB, KVW), F32), pltpu.VMEM((2 * LB, KVW), F32),
                        pltpu.VMEM((NH * HD, NST), F32), pltpu.VMEM((LB, D), F32), pltpu.VMEM((LB, D), F32)]
        + _exchange_sems(ne),
        compiler_params=_params(),
    )(q, k, k, v, v, o, do, cos, sin, sinks8, dyn, y, z, xbc, xbc, conv_w8, conv_b, dtr, ssm_p, nw, hs, *items)
    return outs[0], outs[1], outs[2], outs[3], outs[4], outs[5:]


def _inproj_bwd(dq, dkv, dzxd, x, dx2, mod8, n1w, w_in):
    s = x.shape[0]
    tt = min(512, s)

    def body(dq_ref, dkv_ref, dz_ref, x_ref, dx2_ref, mod_ref, nw_ref, w_ref, gx_ref, sm_ref):
        i = pl.program_id(0)

        @pl.when(i == 0)
        def _():
            sm_ref[...] = jnp.zeros_like(sm_ref)

        dproj = jnp.concatenate([dq_ref[...], dkv_ref[...], dz_ref[...]], axis=1)
        dh1 = _dot_nt(dproj, w_ref[...])
        dxn, d_shift, d_scale, d_w = _norm_mod_bwd(x_ref[...], dh1, nw_ref[...], mod_ref[1:2, :])
        gx_ref[...] = dx2_ref[...] + dxn
        sm_ref[0:1, :] += d_shift
        sm_ref[1:2, :] += d_scale
        sm_ref[2:3, :] += d_w

    return _pcall(
        body, name="inproj_bwd", grid=(s // tt,),
        in_specs=[_rows(tt, AW), _rows(tt, 2 * KVW), _rows(tt, ZXD), _rows(tt, D), _rows(tt, D),
                  _const((8, D)), _const((1, D)), _const((D, INP))],
        out_specs=[_rows(tt, D), pl.BlockSpec((8, D), lambda i: (0, 0))],
        out_shape=[_sds((s, D), F32), _sds((8, D), F32)],
        compiler_params=_params(),
    )(dq, dkv, dzxd, x, dx2, mod8, n1w, w_in)


def _wgrad(a, b, name):
    s, m = a.shape
    n = b.shape[1]
    tk = min(1024, s)
    tm = _largest_divisor(m, (1408, 1024, 512, 256, 128))
    tn = n if n <= 2048 else _largest_divisor(n, (1408, 1024, 512, 256, 128))
    nk = s // tk

    def body(a_ref, b_ref, o_ref, acc):
        kk = pl.program_id(2)

        @pl.when(kk == 0)
        def _():
            acc[...] = jnp.zeros_like(acc)

        acc[...] += _dot_tn(a_ref[...], b_ref[...])

        @pl.when(kk == nk - 1)
        def _():
            o_ref[...] = acc[...].astype(BF16)

    return _pcall(
        body, name=name, grid=(m // tm, n // tn, nk),
        in_specs=[pl.BlockSpec((tk, tm), lambda i, j, kk: (kk, i)), pl.BlockSpec((tk, tn), lambda i, j, kk: (kk, j))],
        out_specs=pl.BlockSpec((tm, tn), lambda i, j, kk: (i, j)),
        out_shape=_sds((m, n), BF16),
        scratch_shapes=[pltpu.VMEM((tm, tn), F32)],
        compiler_params=_params(3),
    )(a, b)


PACK_ROWS = 24


def _pack_small(sm_f, sm_b, sm_s, sm_i, dsink):
    def body(f_ref, b_ref, s_ref, i_ref, k_ref, o_ref):
        o_ref[...] = jnp.zeros_like(o_ref)
        o_ref[0:2, :] = i_ref[0:2, :]
        o_ref[2:3, :] = b_ref[3:4, :]
        o_ref[3:5, :] = b_ref[0:2, :]
        o_ref[5:6, :] = f_ref[0:1, :]
        o_ref[6:7, :] = i_ref[2:3, :]
        o_ref[7:8, :] = b_ref[2:3, :]
        o_ref[8:9, :] = f_ref[1:2, :]
        o_ref[9:14, :] = s_ref[0:5, :]
        o_ref[14:15, :] = s_ref[5:6, :]
        o_ref[15:16, 0:3 * LB] = s_ref[6:7, 0:3 * LB]
        lane = lax.broadcasted_iota(jnp.int32, (1, LB), 1)
        sk = jnp.zeros((1, LB), F32)
        for h in range(NQ):
            sk = sk + jnp.where(lane == h, k_ref[h:h + 1, 0:1], 0.0)
        o_ref[15:16, 3 * LB:4 * LB] = sk
        o_ref[16:17, :] = f_ref[2:3, :]

    return _pcall(body, name="pack_small", out_shape=_sds((PACK_ROWS, D), F32))(sm_f, sm_b, sm_s, sm_i, dsink)


def _exchange(gathers, scatters, name):
    items, shapes, n_g = _exchange_items(gathers, scatters)
    n = len(items)

    def body(*refs):
        ex = _Exchange(n_g, refs[:n], refs[n:2 * n], refs[2 * n:])
        ex.start()
        ex.finish()

    any_spec = pl.BlockSpec(memory_space=pl.ANY)
    return _pcall(
        body, name=name, in_specs=[any_spec] * n, out_specs=[any_spec] * n, out_shape=shapes,
        scratch_shapes=_exchange_sems(n),
    )(*items)


def _exchange_items(gathers, scatters):
    items = list(gathers) + list(scatters)
    shapes = [_sds((N_DEV,) + a.shape, a.dtype) for a in gathers] + [_sds(a.shape, a.dtype) for a in scatters]
    return items, shapes, len(gathers)


def _exchange_sems(n):
    return [pltpu.SemaphoreType.DMA((n, N_DEV - 1)), pltpu.SemaphoreType.DMA((n, N_DEV - 1)),
            pltpu.SemaphoreType.DMA((n,))]


class _Exchange:
    def __init__(self, n_g, ins, outs, sems):
        self.n_g, self.ins, self.outs = n_g, ins, outs
        self.send_sems, self.recv_sems, self.loc_sems = sems
        xi, yi, ci = lax.axis_index("x"), lax.axis_index("y"), lax.axis_index("c")
        self.me = 4 * xi + 2 * yi + ci
        self.peers = []
        for r in range(1, N_DEV):
            px = 1 - xi if r & 4 else xi
            py = 1 - yi if r & 2 else yi
            pc = 1 - ci if r & 1 else ci
            self.peers.append(((px, py, pc), 4 * px + 2 * py + pc))

    def _copy(self, t, r, landing):
        dev, peer = self.peers[r]
        src = self.ins[t] if t < self.n_g else self.ins[t].at[peer]
        return pltpu.make_async_remote_copy(
            src_ref=src, dst_ref=self.outs[t].at[landing], send_sem=self.send_sems.at[t, r],
            recv_sem=self.recv_sems.at[t, r], device_id=dev, device_id_type=pl.DeviceIdType.MESH)

    def _local(self, t):
        src = self.ins[t] if t < self.n_g else self.ins[t].at[self.me]
        return pltpu.make_async_copy(src, self.outs[t].at[self.me], self.loc_sems.at[t])

    def start(self):
        for t in range(len(self.ins)):
            self._local(t).start()
            for r in range(N_DEV - 1):
                self._copy(t, r, self.me).start()

    def finish(self):
        n = len(self.ins)
        for t in range(n):
            for r in range(N_DEV - 1):
                self._copy(t, r, self.peers[r][1]).wait_recv()
        for t in range(n):
            for r in range(N_DEV - 1):
                self._copy(t, r, self.me).wait_send()
            self._local(t).wait()


def _ada_fwd(c_all, w_cols, b_cols):
    def body(c_ref, w_ref, b_ref, o_ref):
        cv = c_ref[...]
        sc = (cv * _sigmoid(cv)).astype(BF16)
        o_ref[...] = _dot(sc, w_ref[...].astype(BF16)) + b_ref[...]

    return _pcall(body, name="ada_fwd", out_shape=_sds((N_DEV, w_cols.shape[1]), F32),
                  compiler_params=_params(0))(c_all, w_cols, b_cols)


def _adamw(w, g, m, v):
    m2 = ADAM_B1 * m + (1.0 - ADAM_B1) * g
    v2 = ADAM_B2 * v + (1.0 - ADAM_B2) * (g * g)
    m_hat = m2 / (1.0 - ADAM_B1 ** ADAM_STEP)
    v_hat = v2 / (1.0 - ADAM_B2 ** ADAM_STEP)
    delta = -ADAM_LR * (m_hat / (jnp.sqrt(v_hat) + ADAM_EPS) + ADAM_WD * w)
    return delta, m2, v2


def _sum_adamw(parts, w, m, v, name):
    rws, cols = w.shape
    tr = _largest_divisor(rws, (256, 176, 128))

    def body(p_ref, w_ref, m_ref, v_ref, g_ref, d_ref, mo_ref, vo_ref):
        g = p_ref[0].astype(F32)
        for dev in range(1, N_DEV):
            g = g + p_ref[dev].astype(F32)
        g_ref[...] = g
        d_ref[...], mo_ref[...], vo_ref[...] = _adamw(w_ref[...], g, m_ref[...], v_ref[...])

    blk = pl.BlockSpec((tr, cols), lambda i: (i, 0))
    return _pcall(
        body, name=name, grid=(rws // tr,),
        in_specs=[pl.BlockSpec((N_DEV, tr, cols), lambda i: (0, i, 0)), blk, blk, blk],
        out_specs=[blk] * 4, out_shape=[_sds((rws, cols), F32)] * 4, compiler_params=_params(),
    )(parts, w, m, v)


def _wada_adamw(c_all, dmod_cols, w, m, v):
    rws, cols = w.shape
    tr = 256

    def body(c_ref, dm_ref, w_ref, m_ref, v_ref, g_ref, d_ref, mo_ref, vo_ref):
        cv = c_ref[...]
        sc = (cv * _sigmoid(cv)).astype(BF16)
        g = _dot_tn(sc, dm_ref[...].astype(BF16))
        g_ref[...] = g
        d_ref[...], mo_ref[...], vo_ref[...] = _adamw(w_ref[...], g, m_ref[...], v_ref[...])

    blk = pl.BlockSpec((tr, cols), lambda i: (i, 0))
    return _pcall(
        body, name="wada_adamw", grid=(rws // tr,),
        in_specs=[pl.BlockSpec((N_DEV, tr), lambda i: (0, i)), pl.BlockSpec((N_DEV, cols), lambda i: (0, 0)),
                  blk, blk, blk],
        out_specs=[blk] * 4, out_shape=[_sds((rws, cols), F32)] * 4, compiler_params=_params(),
    )(c_all, dmod_cols, w, m, v)


def _small_reduce(packs):
    def body(p_ref, o_ref):
        tot = p_ref[0]
        for dev in range(1, N_DEV):
            tot = tot + p_ref[dev]
        o_ref[...] = tot
        o_ref[16:17, :] = jnp.zeros((1, D), F32) + (0.5 / D) * jnp.sum(tot[16:17, :])

    return _pcall(body, name="small_reduce", out_shape=_sds((PACK_ROWS, D), F32))(packs)


def _adamw_many(ws, gs, ms, vs):
    k = len(ws)

    def body(*refs):
        for i in range(k):
            w_ref, g_ref, m_ref, v_ref = refs[i], refs[k + i], refs[2 * k + i], refs[3 * k + i]
            d_ref, mo_ref, vo_ref = refs[4 * k + i], refs[5 * k + i], refs[6 * k + i]
            d_ref[...], mo_ref[...], vo_ref[...] = _adamw(w_ref[...], g_ref[...], m_ref[...], v_ref[...])

    shp = [_sds(w.shape, F32) for w in ws]
    outs = _pcall(body, name="adamw_small", out_shape=shp * 3)(*ws, *gs, *ms, *vs)
    return outs[:k], outs[k:2 * k], outs[2 * k:]


def kernel(x, c, positions, w_ada, b_ada, norm1_w, w_in, conv_w, conv_b, dt_bias, a_log, d_skip, attn_sinks, ssm_norm_w, w_out, norm2_w, w_gate_up, w_down, final_norm_w, loss_target, m_w_ada, m_b_ada, m_norm1_w, m_w_in, m_conv_w, m_conv_b, m_dt_bias, m_a_log, m_d_skip, m_attn_sinks, m_ssm_norm_w, m_w_out, m_norm2_w, m_w_gate_up, m_w_down, m_final_norm_w, v_w_ada, v_b_ada, v_norm1_w, v_w_in, v_conv_w, v_conv_b, v_dt_bias, v_a_log, v_d_skip, v_attn_sinks, v_ssm_norm_w, v_w_out, v_norm2_w, v_w_gate_up, v_w_down, v_final_norm_w):
    s = x.shape[1]
    me = 4 * lax.axis_index("x") + 2 * lax.axis_index("y") + lax.axis_index("c")
    ada_cols = N_MOD * D // N_DEV

    c8 = jnp.pad(c, ((0, 7), (0, 0)))
    cw8 = jnp.pad(conv_w[0], ((0, 8 - CONVK), (0, 0)))
    g_c, g_in, g_cw = _exchange([c8, w_in[0].astype(BF16), cw8], [], "gather_in")
    c_all = g_c[:, 0, :]
    w_in_f = jnp.transpose(g_in, (1, 0, 2)).reshape(D, IN_PROJ)
    w_in_f = jnp.pad(w_in_f, ((0, 0), (0, INP - IN_PROJ)))
    conv_w8 = jnp.transpose(g_cw, (1, 0, 2)).reshape(8, D)

    b_cols = lax.dynamic_slice(b_ada, (0, me * ada_cols), (1, ada_cols))
    (g_mod,) = _exchange([_ada_fwd(c_all, w_ada[0], b_cols)], [], "gather_mod")
    mod = lax.dynamic_index_in_dim(g_mod, me, axis=1, keepdims=False).reshape(N_MOD, D)
    mod8 = jnp.pad(mod, ((0, 8 - N_MOD), (0, 0)))

    half = HD // 2
    inv_freq = ROPE_THETA ** (-jnp.arange(half, dtype=F32) / half)
    invf = jnp.tile(inv_freq, LB // half).reshape(1, LB)
    lanes = lambda a: jnp.pad(a, ((0, 0), (0, LB - a.shape[1])))
    ssm_p = jnp.pad(jnp.concatenate([lanes(dt_bias), lanes(a_log), lanes(d_skip)], axis=0), ((0, 5), (0, 0)))
    sinks8 = jnp.broadcast_to(attn_sinks.reshape(NQ, 1), (NQ, LB))

    xs, tgt, fnw = x[0], loss_target[0], final_norm_w.reshape(1, D)

    q, k, v, z, xbc, dtr, h1, cos, sin = _inproj_fwd(xs, positions[0].reshape(s, 1), invf, mod8, norm1_w, w_in_f)
    attn, yn, y, hs, (g_out, g_gu, g_down) = _mixer_fwd(
        q, k, v, sinks8, xbc, conv_w8, conv_b, dtr, ssm_p, z, ssm_norm_w,
        [w_out[0].astype(BF16), w_gate_up[0].astype(BF16), w_down[0].astype(BF16)])
    w_out_f = g_out.reshape(D, D)
    w_gu_f = jnp.transpose(g_gu, (1, 0, 2)).reshape(D, 2 * DFF)
    w_down_f = g_down.reshape(DFF, D)
    x2, h2, mo = _outproj_fwd(attn, yn, xs, mod8, norm2_w, w_out_f)
    gu, act, dx3, sm_f = _ffn_fwd_loss(h2, x2, tgt, mod8, fnw, w_gu_f, w_down_f)

    dx2, dff, dgu, dmix, dattn, dyn, sm_b = _ffn_bwd(dx3, gu, x2, mo, mod8, norm2_w, w_gu_f, w_down_f, w_out_f)
    p_gu = jnp.transpose(_wgrad(h2, dgu, "wgrad_gate_up").reshape(D, N_DEV, 2 * DFF // N_DEV), (1, 0, 2))
    p_down = _wgrad(act, dff, "wgrad_down").reshape(N_DEV, DFF // N_DEV, D)
    dq, dkv, dsink, dzxd, sm_s, (r_gu, r_down) = _mixer_bwd(
        q, k, v, attn, dattn, cos, sin, sinks8, dyn, y, z, xbc, conv_w8, conv_b, dtr, ssm_p, ssm_norm_w, hs,
        [p_gu, p_down])
    gx, sm_i = _inproj_bwd(dq, dkv, dzxd, xs, dx2, mod8, norm1_w, w_in_f)
    pack = _pack_small(sm_f, sm_b, sm_s, sm_i, dsink)
    dw_in_f = jnp.concatenate([_wgrad(h1, dq, "wgrad_in_q"), _wgrad(h1, dkv, "wgrad_in_kv"),
                               _wgrad(h1, dzxd, "wgrad_in_zxd")[:, :IN_PROJ - O_Z]], axis=1)
    p_in = jnp.transpose(dw_in_f.reshape(D, N_DEV, IN_PROJ // N_DEV), (1, 0, 2))
    p_out = jnp.concatenate([_wgrad(attn, dmix, "wgrad_out_attn"), _wgrad(yn, dmix, "wgrad_out_ssd")],
                            axis=0).reshape(N_DEV, D // N_DEV, D)
    g_pack, r_in, r_out = _exchange([pack], [p_in, p_out], "exchange_grads")

    tot = _small_reduce(g_pack)
    loss = tot[16, 0]
    dmod_all = g_pack[:, 0:N_MOD, :].reshape(N_DEV, N_MOD * D)
    dmod_cols = lax.dynamic_slice(dmod_all, (0, me * ada_cols), (N_DEV, ada_cols))

    big = {
        "w_ada": _wada_adamw(c_all, dmod_cols, w_ada[0], m_w_ada[0], v_w_ada[0]),
        "w_in": _sum_adamw(r_in, w_in[0], m_w_in[0], v_w_in[0], "adamw_in"),
        "w_out": _sum_adamw(r_out, w_out[0], m_w_out[0], v_w_out[0], "adamw_out"),
        "w_gate_up": _sum_adamw(r_gu, w_gate_up[0], m_w_gate_up[0], v_w_gate_up[0], "adamw_gate_up"),
        "w_down": _sum_adamw(r_down, w_down[0], m_w_down[0], v_w_down[0], "adamw_down"),
    }
    small_names = ["b_ada", "norm1_w", "conv_w", "conv_b", "dt_bias", "a_log", "d_skip", "attn_sinks", "ssm_norm_w",
                   "norm2_w", "final_norm_w"]
    row15 = tot[15:16, :]
    small_g = {
        "b_ada": tot[0:N_MOD, :].reshape(1, N_MOD * D),
        "norm1_w": tot[6:7, :],
        "conv_w": lax.dynamic_slice(tot[10:14, :], (0, me * (D // N_DEV)), (CONVK, D // N_DEV)),
        "conv_b": tot[9:10, :],
        "dt_bias": row15[:, 0:NH],
        "a_log": row15[:, LB:LB + NH],
        "d_skip": row15[:, 2 * LB:2 * LB + NH],
        "attn_sinks": row15[:, 3 * LB:3 * LB + NQ],
        "ssm_norm_w": tot[14:15, 0:SW],
        "norm2_w": tot[7:8, :],
        "final_norm_w": tot[8:9, :],
    }
    small_w = {"b_ada": b_ada, "norm1_w": norm1_w, "conv_w": conv_w[0], "conv_b": conv_b, "dt_bias": dt_bias,
               "a_log": a_log, "d_skip": d_skip, "attn_sinks": attn_sinks, "ssm_norm_w": ssm_norm_w,
               "norm2_w": norm2_w, "final_norm_w": final_norm_w.reshape(1, D)}
    small_m = {"b_ada": m_b_ada, "norm1_w": m_norm1_w, "conv_w": m_conv_w[0], "conv_b": m_conv_b,
               "dt_bias": m_dt_bias, "a_log": m_a_log, "d_skip": m_d_skip, "attn_sinks": m_attn_sinks,
               "ssm_norm_w": m_ssm_norm_w, "norm2_w": m_norm2_w, "final_norm_w": m_final_norm_w.reshape(1, D)}
    small_v = {"b_ada": v_b_ada, "norm1_w": v_norm1_w, "conv_w": v_conv_w[0], "conv_b": v_conv_b,
               "dt_bias": v_dt_bias, "a_log": v_a_log, "d_skip": v_d_skip, "attn_sinks": v_attn_sinks,
               "ssm_norm_w": v_ssm_norm_w, "norm2_w": v_norm2_w, "final_norm_w": v_final_norm_w.reshape(1, D)}
    s_d, s_m, s_v = _adamw_many([small_w[k] for k in small_names], [small_g[k] for k in small_names],
                                [small_m[k] for k in small_names], [small_v[k] for k in small_names])

    order = ["w_ada", "b_ada", "norm1_w", "w_in", "conv_w", "conv_b", "dt_bias", "a_log", "d_skip", "attn_sinks",
             "ssm_norm_w", "w_out", "norm2_w", "w_gate_up", "w_down", "final_norm_w"]
    lead = {"w_ada", "w_in", "conv_w", "w_out", "w_gate_up", "w_down"}
    grads, deltas, new_m, new_v = [], [], [], []
    for name in order:
        if name in big:
            g, d, m2, v2 = big[name]
        else:
            i = small_names.index(name)
            g, d, m2, v2 = small_g[name], s_d[i], s_m[i], s_v[i]
        if name in lead:
            g, d, m2, v2 = g[None], d[None], m2[None], v2[None]
        if name == "final_norm_w":
            g, d, m2, v2 = g.reshape(D), d.reshape(D), m2.reshape(D), v2.reshape(D)
        grads.append(g)
        deltas.append(d)
        new_m.append(m2)
        new_v.append(v2)
    return (loss, gx[None], *grads, *deltas, *new_m, *new_v)
```

```python
import functools
import math

import jax
import jax.numpy as jnp
from jax import lax
from jax.experimental import pallas as pl
from jax.experimental.pallas import tpu as pltpu

F32 = jnp.float32
BF16 = jnp.bfloat16

N_DEV = 8
D = 1024
HD = 64
NQ = 8
AW = 512
KVW = 128
SW = 512
NST = 128
NH = 8
LB = 128
CONVK = 4
DFF = 2816
N_MOD = 6
IN_PROJ = 2312
INP = 2432
O_Q, O_K, O_V, O_Z, O_XBC, O_DT = 0, 512, 640, 768, 1280, 2304
ZXD = INP - O_Z
EPS = 1e-6
NEG = -1e30
ROPE_THETA = 10000.0
VMEM_LIMIT = 56 * 1024 * 1024

ADAM_LR = 0.001
ADAM_B1 = 0.9
ADAM_B2 = 0.999
ADAM_EPS = 1e-08
ADAM_WD = 0.01
ADAM_STEP = 10

NT_DIMS = (((1,), (1,)), ((), ()))
TN_DIMS = (((0,), (0,)), ((), ()))


def _pcall(body, **kw):
    return pl.pallas_call(body, **kw)


def _sds(shape, dtype):
    return jax.ShapeDtypeStruct(shape, dtype)


def _params(n_grid=1):
    return pltpu.CompilerParams(dimension_semantics=("arbitrary",) * n_grid, vmem_limit_bytes=VMEM_LIMIT)


def _const(shape):
    return pl.BlockSpec(shape, lambda *_: (0,) * len(shape), pipeline_mode=pl.Buffered(1))


def _largest_divisor(n, candidates):
    for cand in candidates:
        if n % cand == 0:
            return cand
    raise ValueError(f"no tile in {candidates} divides {n}")


def _rows(t, w):
    return pl.BlockSpec((t, w), lambda i: (i, 0))


def _dot(a, b):
    return jnp.dot(a, b, preferred_element_type=F32)


def _dot_nt(a, b):
    return lax.dot_general(a, b, NT_DIMS, preferred_element_type=F32)


def _dot_tn(a, b):
    return lax.dot_general(a, b, TN_DIMS, preferred_element_type=F32)


def _sigmoid(v):
    return 1.0 / (1.0 + jnp.exp(-v))


def _softplus(v):
    return jnp.maximum(v, 0.0) + jnp.log1p(jnp.exp(-jnp.abs(v)))


def _rope_sign_mask(shape):
    lane = lax.broadcasted_iota(jnp.int32, shape, 1)
    return (lane % HD) < (HD // 2)


def _rope(t, cs, sn, inverse):
    r_dn = pltpu.roll(t, HD // 2, 1)
    r_up = pltpu.roll(t, LB - HD // 2, 1)
    first = _rope_sign_mask(t.shape)
    if inverse:
        rot = jnp.where(first, r_up, -r_dn)
    else:
        rot = jnp.where(first, -r_up, r_dn)
    return t * cs + rot * sn


def _norm_mod_fwd(xv, nw, shift, scale):
    r = lax.rsqrt(jnp.mean(xv * xv, axis=-1, keepdims=True) + EPS)
    xh = xv * r
    return (xh * nw) * (1.0 + scale) + shift


def _norm_mod_bwd(xv, dh, nw, scale):
    r = lax.rsqrt(jnp.mean(xv * xv, axis=-1, keepdims=True) + EPS)
    xh = xv * r
    xn = xh * nw
    d_shift = jnp.sum(dh, axis=0, keepdims=True)
    d_scale = jnp.sum(dh * xn, axis=0, keepdims=True)
    dxn = dh * (1.0 + scale)
    d_w = jnp.sum(dxn * xh, axis=0, keepdims=True)
    dxh = dxn * nw
    dx = r * (dxh - xh * jnp.mean(dxh * xh, axis=-1, keepdims=True))
    return dx, d_shift, d_scale, d_w


def _inproj_fwd(x, pos, invf, mod8, n1w, w_in):
    s = x.shape[0]
    tt = min(512, s)

    def body(x_ref, pos_ref, invf_ref, mod_ref, nw_ref, w_ref,
             q_ref, k_ref, v_ref, z_ref, xbc_ref, dtr_ref, h1_ref, cos_ref, sin_ref):
        h = _norm_mod_fwd(x_ref[...], nw_ref[...], mod_ref[0:1, :], mod_ref[1:2, :])
        hb = h.astype(BF16)
        h1_ref[...] = hb
        proj = _dot(hb, w_ref[...])
        ang = pos_ref[...].astype(F32) * invf_ref[...]
        cs = jnp.cos(ang)
        sn = jnp.sin(ang)
        cos_ref[...] = cs
        sin_ref[...] = sn
        for a in range(AW // LB):
            q_ref[:, a * LB:(a + 1) * LB] = _rope(proj[:, O_Q + a * LB:O_Q + (a + 1) * LB], cs, sn, False).astype(BF16)
        k_ref[...] = _rope(proj[:, O_K:O_V], cs, sn, False).astype(BF16)
        v_ref[...] = proj[:, O_V:O_Z].astype(BF16)
        z_ref[...] = proj[:, O_Z:O_XBC]
        xbc_ref[...] = proj[:, O_XBC:O_DT]
        dtr_ref[...] = proj[:, O_DT:INP]

    return _pcall(
        body, name="inproj_fwd", grid=(s // tt,),
        in_specs=[_rows(tt, D), _rows(tt, 1), _const((1, LB)), _const((8, D)), _const((1, D)), _const((D, INP))],
        out_specs=[_rows(tt, AW), _rows(tt, KVW), _rows(tt, KVW), _rows(tt, SW), _rows(tt, D), _rows(tt, LB),
                   _rows(tt, D), _rows(tt, LB), _rows(tt, LB)],
        out_shape=[_sds((s, AW), BF16), _sds((s, KVW), BF16), _sds((s, KVW), BF16), _sds((s, SW), F32),
                   _sds((s, D), F32), _sds((s, LB), F32), _sds((s, D), BF16), _sds((s, LB), F32), _sds((s, LB), F32)],
        compiler_params=_params(),
    )(x, pos, invf, mod8, n1w, w_in)


QPG = 4
ATT_SCALE = 1.0 / math.sqrt(HD)


def _stack_heads(val, g):
    return jnp.concatenate([val[:, (QPG * g + hh) * HD:(QPG * g + hh + 1) * HD] for hh in range(QPG)], axis=0)


def _unstack_heads(groups):
    return jnp.concatenate([grp[hh * LB:(hh + 1) * LB, :] for grp in groups for hh in range(QPG)], axis=1)


def _upper_mask():
    row = lax.broadcasted_iota(jnp.int32, (QPG * LB, LB), 0)
    col = lax.broadcasted_iota(jnp.int32, (QPG * LB, LB), 1)
    return col > (row % LB)


def _sink_col(sinks, g):
    return jnp.concatenate([jnp.broadcast_to(sinks[QPG * g + hh:QPG * g + hh + 1, 0:1], (LB, 1))
                            for hh in range(QPG)], axis=0)


def _attn_probs(n, qg, kp, kc, sink, upper):
    sp = _dot_nt(qg, kp) * ATT_SCALE
    sc = _dot_nt(qg, kc) * ATT_SCALE
    comb = jnp.where(upper, jnp.where(n > 0, sp, NEG), sc)
    m = jnp.maximum(jnp.max(comb, axis=-1, keepdims=True), sink)
    p = jnp.exp(comb - m)
    es = jnp.exp(sink - m)
    denom = jnp.sum(p, axis=-1, keepdims=True) + es
    return p / denom, es / denom


def _attn_fwd_block(n, q_ref, kp_ref, kc_ref, vp_ref, vc_ref, sink_ref, o_ref):
    qv, kp, kc, vp, vc = q_ref[...], kp_ref[...], kc_ref[...], vp_ref[...], vc_ref[...]
    sinks = sink_ref[...]
    upper = _upper_mask()
    outs = []
    for g in range(NQ // QPG):
        sl = slice(g * HD, (g + 1) * HD)
        probs, _ = _attn_probs(n, _stack_heads(qv, g), kp[:, sl], kc[:, sl], _sink_col(sinks, g), upper)
        outs.append(_dot(jnp.where(upper, probs, 0.0).astype(BF16), vp[:, sl])
                    + _dot(jnp.where(upper, 0.0, probs).astype(BF16), vc[:, sl]))
    o_ref[...] = _unstack_heads(outs).astype(BF16)


def _cumsum_rows(a, reverse):
    row = lax.broadcasted_iota(jnp.int32, a.shape, 0)
    step = 1
    while step < LB:
        if reverse:
            a = a + jnp.where(row < LB - step, pltpu.roll(a, LB - step, 0), 0.0)
        else:
            a = a + jnp.where(row >= step, pltpu.roll(a, step, 0), 0.0)
        step *= 2
    return a


def _conv_shifts(prev, cur):
    row = lax.broadcasted_iota(jnp.int32, cur.shape, 0)
    out = [cur]
    for j in range(1, CONVK):
        out.append(jnp.where(row < j, pltpu.roll(prev, j, 0), pltpu.roll(cur, j, 0)))
    return out


def _ssd_pre(xp_ref, xc_ref, cw_ref, cb_ref, dtr_ref, sp_ref, n):
    cur = xc_ref[...]
    prev = jnp.where(n > 0, xp_ref[...], 0.0)
    sh = _conv_shifts(prev, cur)
    u = cb_ref[...] + cw_ref[CONVK - 1:CONVK, :] * sh[0]
    for j in range(1, CONVK):
        u = u + cw_ref[CONVK - 1 - j:CONVK - j, :] * sh[j]
    sg_u = _sigmoid(u)
    xc = u * sg_u
    pre = dtr_ref[...] + sp_ref[0:1, :]
    dt = _softplus(pre)
    a_neg = -jnp.exp(sp_ref[1:2, :])
    acs = _cumsum_rows(dt * a_neg, False)
    return sh, u, sg_u, xc, pre, dt, a_neg, acs


def _gated_norm_fwd(y, z, nw):
    sz = z * _sigmoid(z)
    yz = y * sz
    parts = []
    for g in range(2):
        t = yz[:, g * 256:(g + 1) * 256]
        parts.append(t * lax.rsqrt(jnp.mean(t * t, axis=-1, keepdims=True) + EPS))
    return jnp.concatenate(parts, axis=1) * nw


def _ssd_fwd_block(n, xp_ref, xc_ref, cw_ref, cb_ref, dtr_ref, sp_ref, z_ref, nw_ref, yn_ref, y_ref, hs_ref, h_scr):
    @pl.when(n == 0)
    def _():
        h_scr[...] = jnp.zeros_like(h_scr)

    hs_ref[0] = h_scr[...]
    _, _, _, xc, _, dt, _, acs = _ssd_pre(xp_ref, xc_ref, cw_ref, cb_ref, dtr_ref, sp_ref, n)
    acs_t = jnp.transpose(acs)
    e_all = jnp.exp(acs)
    alast = acs[LB - 1:LB, :]
    dte_all = jnp.exp(alast - acs)
    elast = jnp.exp(alast)
    causal = (lax.broadcasted_iota(jnp.int32, (LB, LB), 0) >= lax.broadcasted_iota(jnp.int32, (LB, LB), 1))
    for g in range(2):
        bg = xc[:, SW + g * NST:SW + (g + 1) * NST]
        cg = xc[:, SW + 2 * NST + g * NST:SW + 2 * NST + (g + 1) * NST]
        bgb = bg.astype(BF16)
        cgb = cg.astype(BF16)
        cb = _dot_nt(cgb, bgb)
        for jj in range(4):
            j = 4 * g + jj
            sl = slice(j * HD, (j + 1) * HD)
            dm = jnp.exp(jnp.where(causal, acs[:, j:j + 1] - acs_t[j:j + 1, :], NEG))
            xs_j = xc[:, sl]
            xdtb = (xs_j * dt[:, j:j + 1]).astype(BF16)
            hj = h_scr[sl, :]
            yd = _dot((cb * dm).astype(BF16), xdtb)
            yo = e_all[:, j:j + 1] * _dot_nt(cgb, hj.astype(BF16))
            bd = (bg * dte_all[:, j:j + 1]).astype(BF16)
            h_scr[sl, :] = hj * elast[:, j:j + 1] + _dot_tn(xdtb, bd)
            y_ref[:, sl] = yd + yo + xs_j * sp_ref[2:3, j:j + 1]
    yn_ref[...] = _gated_norm_fwd(y_ref[...], z_ref[...], nw_ref[...]).astype(BF16)


def _mixer_fwd(q, k, v, sinks8, xbc, conv_w8, conv_b, dtr, ssm_p, z, nw, gathers):
    s = q.shape[0]
    nb = s // LB
    prev = lambda n: (jnp.maximum(n - 1, 0), 0)
    cur = lambda n: (n, 0)
    items, ex_shapes, n_g = _exchange_items(gathers, [])
    ne = len(items)

    def body(*refs):
        a_in, s_in, ex_in = refs[:6], refs[6:14], refs[14:14 + ne]
        o_ref, yn_ref, y_ref, hs_ref = refs[14 + ne:18 + ne]
        ex_out = refs[18 + ne:18 + 2 * ne]
        h_scr = refs[18 + 2 * ne]
        n = pl.program_id(0)

        @pl.when(n == 0)
        def _():
            _Exchange(n_g, ex_in, ex_out, refs[19 + 2 * ne:]).start()

        _attn_fwd_block(n, *a_in, o_ref)
        _ssd_fwd_block(n, *s_in, yn_ref, y_ref, hs_ref, h_scr)

        @pl.when(n == nb - 1)
        def _():
            _Exchange(n_g, ex_in, ex_out, refs[19 + 2 * ne:]).finish()

    any_spec = pl.BlockSpec(memory_space=pl.ANY)
    outs = _pcall(
        body, name="mixer_fwd", grid=(nb,),
        in_specs=[pl.BlockSpec((LB, AW), cur), pl.BlockSpec((LB, KVW), prev), pl.BlockSpec((LB, KVW), cur),
                  pl.BlockSpec((LB, KVW), prev), pl.BlockSpec((LB, KVW), cur), _const((8, LB)),
                  pl.BlockSpec((LB, D), prev), pl.BlockSpec((LB, D), cur), _const((8, D)), _const((1, D)),
                  pl.BlockSpec((LB, LB), cur), _const((8, LB)), pl.BlockSpec((LB, SW), cur), _const((1, SW))]
        + [any_spec] * ne,
        out_specs=[pl.BlockSpec((LB, AW), cur), pl.BlockSpec((LB, SW), cur), pl.BlockSpec((LB, SW), cur),
                   pl.BlockSpec((1, NH * HD, NST), lambda n: (n, 0, 0))] + [any_spec] * ne,
        out_shape=[_sds((s, AW), BF16), _sds((s, SW), BF16), _sds((s, SW), F32), _sds((nb, NH * HD, NST), F32)]
        + ex_shapes,
        scratch_shapes=[pltpu.VMEM((NH * HD, NST), F32)] + _exchange_sems(ne),
        compiler_params=_params(),
    )(q, k, k, v, v, sinks8, xbc, xbc, conv_w8, conv_b, dtr, ssm_p, z, nw, *items)
    return outs[0], outs[1], outs[2], outs[3], outs[4:]


def _outproj_fwd(attn, yn, x, mod8, n2w, w_out):
    s = x.shape[0]
    tt = min(512, s)

    def body(a_ref, y_ref, x_ref, mod_ref, nw_ref, w_ref, x2_ref, h2_ref, mo_ref):
        mix = jnp.concatenate([a_ref[...], y_ref[...]], axis=1)
        mo = _dot(mix, w_ref[...])
        mo_ref[...] = mo.astype(BF16)
        x2 = x_ref[...] + mod_ref[2:3, :] * mo
        x2_ref[...] = x2
        h2_ref[...] = _norm_mod_fwd(x2, nw_ref[...], mod_ref[3:4, :], mod_ref[4:5, :]).astype(BF16)

    return _pcall(
        body, name="outproj_fwd", grid=(s // tt,),
        in_specs=[_rows(tt, AW), _rows(tt, SW), _rows(tt, D), _const((8, D)), _const((1, D)), _const((D, D))],
        out_specs=[_rows(tt, D), _rows(tt, D), _rows(tt, D)],
        out_shape=[_sds((s, D), F32), _sds((s, D), BF16), _sds((s, D), BF16)],
        compiler_params=_params(),
    )(attn, yn, x, mod8, n2w, w_out)


def _ffn_fwd_loss(h2, x2, tgt, mod8, fnw, w_gu, w_down):
    s = x2.shape[0]
    tf = min(256, s)

    def body(h_ref, x2_ref, t_ref, mod_ref, fw_ref, wgu_ref, wd_ref, gu_ref, act_ref, dx3_ref, sm_ref):
        i = pl.program_id(0)

        @pl.when(i == 0)
        def _():
            sm_ref[...] = jnp.zeros_like(sm_ref)

        gu = _dot(h_ref[...], wgu_ref[...])
        gu_ref[...] = gu.astype(BF16)
        g = gu[:, :DFF]
        act = (g * _sigmoid(g) * gu[:, DFF:]).astype(BF16)
        act_ref[...] = act
        ff = _dot(act, wd_ref[...])
        x3 = x2_ref[...] + mod_ref[5:6, :] * ff
        r = lax.rsqrt(jnp.mean(x3 * x3, axis=-1, keepdims=True) + EPS)
        xh = x3 * r
        fw = fw_ref[...]
        err = xh * fw - t_ref[...]
        dy = err * (1.0 / D)
        dxh = dy * fw
        dx3 = r * (dxh - xh * jnp.mean(dxh * xh, axis=-1, keepdims=True))
        dx3_ref[...] = dx3
        sm_ref[0:1, :] += jnp.sum(dx3 * ff, axis=0, keepdims=True)
        sm_ref[1:2, :] += jnp.sum(dy * xh, axis=0, keepdims=True)
        sm_ref[2:3, :] += jnp.sum(err * err, axis=0, keepdims=True)

    return _pcall(
        body, name="ffn_fwd_loss", grid=(s // tf,),
        in_specs=[_rows(tf, D), _rows(tf, D), _rows(tf, D), _const((8, D)), _const((1, D)),
                  _const((D, 2 * DFF)), _const((DFF, D))],
        out_specs=[_rows(tf, 2 * DFF), _rows(tf, DFF), _rows(tf, D), pl.BlockSpec((8, D), lambda i: (0, 0))],
        out_shape=[_sds((s, 2 * DFF), BF16), _sds((s, DFF), BF16), _sds((s, D), F32), _sds((8, D), F32)],
        compiler_params=_params(),
    )(h2, x2, tgt, mod8, fnw, w_gu, w_down)


def _ffn_bwd(dx3, gu, x2, mixout, mod8, n2w, w_gu, w_down, w_out):
    s = x2.shape[0]
    tb = min(256, s)

    def body(dx3_ref, gu_ref, x2_ref, mo_ref, mod_ref, nw_ref, wgu_ref, wd_ref, wo_ref,
             dx2_ref, dff_ref, dgu_ref, dmix_ref, dattn_ref, dyn_ref, sm_ref):
        i = pl.program_id(0)

        @pl.when(i == 0)
        def _():
            sm_ref[...] = jnp.zeros_like(sm_ref)

        dx3 = dx3_ref[...]
        dff = (dx3 * mod_ref[5:6, :]).astype(BF16)
        dff_ref[...] = dff
        dact = _dot_nt(dff, wd_ref[...])
        g = gu_ref[:, :DFF].astype(F32)
        u = gu_ref[:, DFF:].astype(F32)
        sg = _sigmoid(g)
        dgu = jnp.concatenate([dact * u * sg * (1.0 + g * (1.0 - sg)), dact * g * sg], axis=1).astype(BF16)
        dgu_ref[...] = dgu
        dh2 = _dot_nt(dgu, wgu_ref[...])
        dxn, d_shift, d_scale, d_w = _norm_mod_bwd(x2_ref[...], dh2, nw_ref[...], mod_ref[4:5, :])
        dx2 = dx3 + dxn
        dx2_ref[...] = dx2
        sm_ref[0:1, :] += d_shift
        sm_ref[1:2, :] += d_scale
        sm_ref[2:3, :] += d_w
        sm_ref[3:4, :] += jnp.sum(dx2 * mo_ref[...].astype(F32), axis=0, keepdims=True)
        dmix = (dx2 * mod_ref[2:3, :]).astype(BF16)
        dmix_ref[...] = dmix
        dmi = _dot_nt(dmix, wo_ref[...])
        dattn_ref[...] = dmi[:, :AW].astype(BF16)
        dyn_ref[...] = dmi[:, AW:]

    return _pcall(
        body, name="ffn_bwd", grid=(s // tb,),
        in_specs=[_rows(tb, D), _rows(tb, 2 * DFF), _rows(tb, D), _rows(tb, D), _const((8, D)), _const((1, D)),
                  _const((D, 2 * DFF)), _const((DFF, D)), _const((D, D))],
        out_specs=[_rows(tb, D), _rows(tb, D), _rows(tb, 2 * DFF), _rows(tb, D), _rows(tb, AW), _rows(tb, SW),
                   pl.BlockSpec((8, D), lambda i: (0, 0))],
        out_shape=[_sds((s, D), F32), _sds((s, D), BF16), _sds((s, 2 * DFF), BF16), _sds((s, D), BF16),
                   _sds((s, AW), BF16), _sds((s, SW), F32), _sds((8, D), F32)],
        compiler_params=_params(),
    )(dx3, gu, x2, mixout, mod8, n2w, w_gu, w_down, w_out)


def _ssd_bwd_block(i, n, *refs):
    def run(dyn_ref, y_ref, z_ref, xp_ref, xc_ref, cw_ref, cb_ref, dtr_ref, sp_ref, nw_ref, hs_ref,
            dzxd_ref, sm_ref, dh_scr, dun_scr, dxc_scr):
        @pl.when(i == 0)
        def _():
            dh_scr[...] = jnp.zeros_like(dh_scr)
            dun_scr[...] = jnp.zeros_like(dun_scr)
            sm_ref[...] = jnp.zeros_like(sm_ref)

        sh, u, sg_u, xc, pre, dt, a_neg, acs = _ssd_pre(xp_ref, xc_ref, cw_ref, cb_ref, dtr_ref, sp_ref, n)
        acs_t = jnp.transpose(acs)
        e_all = jnp.exp(acs)
        alast = acs[LB - 1:LB, :]
        dte_all = jnp.exp(alast - acs)
        elast = jnp.exp(alast)
        riota = lax.broadcasted_iota(jnp.int32, (LB, LB), 0)
        liota = lax.broadcasted_iota(jnp.int32, (LB, LB), 1)
        causal = riota >= liota
        lane1 = lax.broadcasted_iota(jnp.int32, (1, LB), 1)

        z = z_ref[...]
        y = y_ref[...]
        sgz = _sigmoid(z)
        sz = z * sgz
        yz = y * sz
        nwv = nw_ref[...]
        dyn_v = dyn_ref[...]
        dyhat = dyn_v * nwv
        yhat_parts, dyz_parts = [], []
        for g in range(2):
            gs = slice(g * 256, (g + 1) * 256)
            t = yz[:, gs]
            rg = lax.rsqrt(jnp.mean(t * t, axis=-1, keepdims=True) + EPS)
            yh = t * rg
            dyh = dyhat[:, gs]
            yhat_parts.append(yh)
            dyz_parts.append(rg * (dyh - yh * jnp.mean(dyh * yh, axis=-1, keepdims=True)))
        yhat = jnp.concatenate(yhat_parts, axis=1)
        dyz = jnp.concatenate(dyz_parts, axis=1)
        sm_ref[5:6, 0:SW] += jnp.sum(dyn_v * yhat, axis=0, keepdims=True)
        dy = dyz * sz
        dzxd_ref[:, 0:SW] = (dyz * y * sgz * (1.0 + z * (1.0 - sgz))).astype(BF16)

        dacs = jnp.zeros((LB, LB), F32)
        dacs_t = jnp.zeros((LB, LB), F32)
        ddt = jnp.zeros((LB, LB), F32)
        dalast = jnp.zeros((1, LB), F32)
        ddsk = jnp.zeros((1, LB), F32)
        for g in range(2):
            bg = xc[:, SW + g * NST:SW + (g + 1) * NST]
            cg = xc[:, SW + 2 * NST + g * NST:SW + 2 * NST + (g + 1) * NST]
            bgb = bg.astype(BF16)
            cgb = cg.astype(BF16)
            cb = _dot_nt(cgb, bgb)
            dcb = jnp.zeros((LB, LB), F32)
            dcg = jnp.zeros((LB, NST), F32)
            dbg = jnp.zeros((LB, NST), F32)
            for jj in range(4):
                j = 4 * g + jj
                sl = slice(j * HD, (j + 1) * HD)
                onl = (liota == j).astype(F32)
                onr = (riota == j).astype(F32)
                on1 = (lane1 == j).astype(F32)
                dm = jnp.exp(jnp.where(causal, acs[:, j:j + 1] - acs_t[j:j + 1, :], NEG))
                xs_j = xc[:, sl]
                dt_j = dt[:, j:j + 1]
                xdtb = (xs_j * dt_j).astype(BF16)
                m = cb * dm
                hj = hs_ref[0, sl, :]
                hb = hj.astype(BF16)
                dy_j = dy[:, sl]
                dyb = dy_j.astype(BF16)
                dxs_j = dy_j * sp_ref[2:3, j:j + 1]
                ddsk = ddsk + on1 * jnp.sum(dy_j * xs_j)
                dm_ = _dot_nt(dyb, xdtb)
                dxdt = _dot_tn(m.astype(BF16), dyb)
                gmat = dm_ * m
                dcb = dcb + dm_ * dm
                dacs = dacs + onl * jnp.sum(gmat, axis=1, keepdims=True)
                dacs_t = dacs_t - onr * jnp.sum(gmat, axis=0, keepdims=True)
                qm = _dot_nt(cgb, hb)
                eac = e_all[:, j:j + 1]
                dqb = (dy_j * eac).astype(BF16)
                dcg = dcg + _dot(dqb, hb)
                dh_prev = _dot_tn(dqb, cgb)
                dacs = dacs + onl * (jnp.sum(dy_j * qm, axis=1, keepdims=True) * eac)
                dhn = dh_scr[sl, :]
                dhnb = dhn.astype(BF16)
                dte_j = dte_all[:, j:j + 1]
                bdb = (bg * dte_j).astype(BF16)
                dxdt = dxdt + _dot_nt(bdb, dhnb)
                dbd = _dot(xdtb, dhnb)
                dbg = dbg + dbd * dte_j
                t1 = jnp.sum(dbd * bg, axis=1, keepdims=True) * dte_j
                dacs = dacs - onl * t1
                el = elast[:, j:j + 1]
                dalast = dalast + on1 * (jnp.sum(t1) + jnp.sum(dhn * hj) * el)
                dh_scr[sl, :] = dhn * el + dh_prev
                dxc_scr[:, sl] = dxs_j + dxdt * dt_j
                ddt = ddt + onl * jnp.sum(dxdt * xs_j, axis=1, keepdims=True)
            dcbb = dcb.astype(BF16)
            dxc_scr[:, SW + g * NST:SW + (g + 1) * NST] = dbg + _dot_tn(dcbb, cgb)
            dxc_scr[:, SW + 2 * NST + g * NST:SW + 2 * NST + (g + 1) * NST] = dcg + _dot(dcbb, bgb)
        dacs = dacs + jnp.transpose(dacs_t) + jnp.where(riota == LB - 1, dalast, 0.0)
        da = _cumsum_rows(dacs, True)
        ddt = ddt + da * a_neg
        da_log = jnp.sum(da * dt, axis=0, keepdims=True) * a_neg
        ddtr = ddt * _sigmoid(pre)
        dzxd_ref[:, SW + D:ZXD] = ddtr.astype(BF16)
        sm_ref[6:7, 0:LB] += jnp.sum(ddtr, axis=0, keepdims=True)
        sm_ref[6:7, LB:2 * LB] += da_log
        sm_ref[6:7, 2 * LB:3 * LB] += ddsk

        du = dxc_scr[...] * (sg_u * (1.0 + u * (1.0 - sg_u)))
        sm_ref[0:1, :] += jnp.sum(du, axis=0, keepdims=True)
        for k in range(CONVK):
            sm_ref[1 + k:2 + k, :] += jnp.sum(du * sh[CONVK - 1 - k], axis=0, keepdims=True)
        dun = dun_scr[...]
        row = lax.broadcasted_iota(jnp.int32, du.shape, 0)
        dxbc = cw_ref[CONVK - 1:CONVK, :] * du
        for j in range(1, CONVK):
            adv = jnp.where(row >= LB - j, pltpu.roll(dun, LB - j, 0), pltpu.roll(du, LB - j, 0))
            dxbc = dxbc + cw_ref[CONVK - 1 - j:CONVK - j, :] * adv
        dun_scr[...] = du
        dzxd_ref[:, SW:SW + D] = dxbc.astype(BF16)

    run(*refs)


def _attn_bwd_block(i, n, q_ref, kp_ref, kc_ref, vp_ref, vc_ref, o_ref, do_ref, cos_ref, sin_ref, sink_ref,
                    dq_ref, dkv_ref, ds_ref, ck_scr, cv_scr):
    @pl.when(i == 0)
    def _():
        ds_ref[...] = jnp.zeros_like(ds_ref)
        ck_scr[...] = jnp.zeros_like(ck_scr)
        cv_scr[...] = jnp.zeros_like(cv_scr)

    qv, kp, kc, vp, vc = q_ref[...], kp_ref[...], kc_ref[...], vp_ref[...], vc_ref[...]
    ov, dov, sinks = o_ref[...], do_ref[...], sink_ref[...]
    upper = _upper_mask()
    srow = lax.broadcasted_iota(jnp.int32, (8, LB), 0)
    slane = lax.broadcasted_iota(jnp.int32, (8, LB), 1)
    dsink = jnp.zeros((8, LB), F32)
    dq_g, dk_prev, dk_cur, dv_prev, dv_cur = [], [], [], [], []
    for g in range(NQ // QPG):
        sl = slice(g * HD, (g + 1) * HD)
        qg = _stack_heads(qv, g)
        dog = _stack_heads(dov, g)
        probs, psink = _attn_probs(n, qg, kp[:, sl], kc[:, sl], _sink_col(sinks, g), upper)
        delta = jnp.sum(dog.astype(F32) * _stack_heads(ov, g).astype(F32), axis=1, keepdims=True)
        dsc = probs * (jnp.where(upper, _dot_nt(dog, vp[:, sl]), _dot_nt(dog, vc[:, sl])) - delta)
        sink_terms = psink * delta
        for hh in range(QPG):
            dsink = dsink - jnp.where((srow == QPG * g + hh) & (slane == 0),
                                      jnp.sum(sink_terms[hh * LB:(hh + 1) * LB, :]), 0.0)
        ds_p = jnp.where(upper, dsc, 0.0).astype(BF16)
        ds_c = jnp.where(upper, 0.0, dsc).astype(BF16)
        dq_g.append((_dot(ds_p, kp[:, sl]) + _dot(ds_c, kc[:, sl])) * ATT_SCALE)
        dk_prev.append(_dot_tn(ds_p, qg) * ATT_SCALE)
        dk_cur.append(_dot_tn(ds_c, qg) * ATT_SCALE)
        dv_prev.append(_dot_tn(jnp.where(upper, probs, 0.0).astype(BF16), dog))
        dv_cur.append(_dot_tn(jnp.where(upper, 0.0, probs).astype(BF16), dog))
    ds_ref[...] += dsink
    cs = cos_ref[...]
    sn = sin_ref[...]
    dq = _unstack_heads(dq_g)
    for a in range(AW // LB):
        dq_ref[:, a * LB:(a + 1) * LB] = _rope(dq[:, a * LB:(a + 1) * LB], cs, sn, True).astype(BF16)
    dkv_ref[:, 0:KVW] = _rope(ck_scr[...] + jnp.concatenate(dk_cur, axis=1), cs, sn, True).astype(BF16)
    dkv_ref[:, KVW:2 * KVW] = (cv_scr[...] + jnp.concatenate(dv_cur, axis=1)).astype(BF16)
    ck_scr[...] = jnp.concatenate(dk_prev, axis=1)
    cv_scr[...] = jnp.concatenate(dv_prev, axis=1)


def _mixer_bwd(q, k, v, o, do, cos, sin, sinks8, dyn, y, z, xbc, conv_w8, conv_b, dtr, ssm_p, nw, hs, scatters):
    s = q.shape[0]
    nb = s // LB
    cur = lambda i: (nb - 1 - i, 0)
    prev = lambda i: (jnp.maximum(nb - 2 - i, 0), 0)
    n_in = 21
    items, ex_shapes, n_g = _exchange_items([], scatters)
    ne = len(items)

    def body(*refs):
        i = pl.program_id(0)
        n = nb - 1 - i
        a_in, s_in, ex_in = refs[:10], refs[10:n_in], refs[n_in:n_in + ne]
        dq_ref, dkv_ref, ds_ref, dzxd_ref, sm_ref = refs[n_in + ne:n_in + ne + 5]
        ex_out = refs[n_in + ne + 5:n_in + 2 * ne + 5]
        ck_scr, cv_scr, dh_scr, dun_scr, dxc_scr = refs[n_in + 2 * ne + 5:n_in + 2 * ne + 10]
        sems = refs[n_in + 2 * ne + 10:]

        @pl.when(i == 0)
        def _():
            _Exchange(n_g, ex_in, ex_out, sems).start()

        _attn_bwd_block(i, n, *a_in, dq_ref, dkv_ref, ds_ref, ck_scr, cv_scr)
        _ssd_bwd_block(i, n, *s_in, dzxd_ref, sm_ref, dh_scr, dun_scr, dxc_scr)

        @pl.when(i == nb - 1)
        def _():
            _Exchange(n_g, ex_in, ex_out, sems).finish()

    any_spec = pl.BlockSpec(memory_space=pl.ANY)
    outs = _pcall(
        body, name="mixer_bwd", grid=(nb,),
        in_specs=[pl.BlockSpec((LB, AW), cur), pl.BlockSpec((LB, KVW), prev), pl.BlockSpec((LB, KVW), cur),
                  pl.BlockSpec((LB, KVW), prev), pl.BlockSpec((LB, KVW), cur), pl.BlockSpec((LB, AW), cur),
                  pl.BlockSpec((LB, AW), cur), pl.BlockSpec((LB, LB), cur), pl.BlockSpec((LB, LB), cur),
                  _const((8, LB)),
                  pl.BlockSpec((LB, SW), cur), pl.BlockSpec((LB, SW), cur), pl.BlockSpec((LB, SW), cur),
                  pl.BlockSpec((LB, D), prev), pl.BlockSpec((LB, D), cur), _const((8, D)), _const((1, D)),
                  pl.BlockSpec((LB, LB), cur), _const((8, LB)), _const((1, SW)),
                  pl.BlockSpec((1, NH * HD, NST), lambda i: (nb - 1 - i, 0, 0))] + [any_spec] * ne,
        out_specs=[pl.BlockSpec((LB, AW), cur), pl.BlockSpec((LB, 2 * KVW), cur),
                   pl.BlockSpec((8, LB), lambda i: (0, 0)),
                   pl.BlockSpec((LB, ZXD), cur), pl.BlockSpec((8, D), lambda i: (0, 0))] + [any_spec] * ne,
        out_shape=[_sds((s, AW), BF16), _sds((s, 2 * KVW), BF16), _sds((8, LB), F32),
                   _sds((s, ZXD), BF16), _sds((8, D), F32)] + ex_shapes,
        scratch_shapes=[pltpu.VMEM((LB, KVW), F32), pltpu.VMEM((LB, KVW), F32),
                        pltpu.VMEM((NH * HD, NST), F32), pltpu.VMEM((LB, D), F32), pltpu.VMEM((LB, D), F32)]
        + _exchange_sems(ne),
        compiler_params=_params(),
    )(q, k, k, v, v, o, do, cos, sin, sinks8, dyn, y, z, xbc, xbc, conv_w8, conv_b, dtr, ssm_p, nw, hs, *items)
    return outs[0], outs[1], outs[2], outs[3], outs[4], outs[5:]


def _inproj_bwd(dq, dkv, dzxd, x, dx2, mod8, n1w, w_in):
    s = x.shape[0]
    tt = min(512, s)

    def body(dq_ref, dkv_ref, dz_ref, x_ref, dx2_ref, mod_ref, nw_ref, w_ref, gx_ref, sm_ref):
        i = pl.program_id(0)

        @pl.when(i == 0)
        def _():
            sm_ref[...] = jnp.zeros_like(sm_ref)

        dproj = jnp.concatenate([dq_ref[...], dkv_ref[...], dz_ref[...]], axis=1)
        dh1 = _dot_nt(dproj, w_ref[...])
        dxn, d_shift, d_scale, d_w = _norm_mod_bwd(x_ref[...], dh1, nw_ref[...], mod_ref[1:2, :])
        gx_ref[...] = dx2_ref[...] + dxn
        sm_ref[0:1, :] += d_shift
        sm_ref[1:2, :] += d_scale
        sm_ref[2:3, :] += d_w

    return _pcall(
        body, name="inproj_bwd", grid=(s // tt,),
        in_specs=[_rows(tt, AW), _rows(tt, 2 * KVW), _rows(tt, ZXD), _rows(tt, D), _rows(tt, D),
                  _const((8, D)), _const((1, D)), _const((D, INP))],
        out_specs=[_rows(tt, D), pl.BlockSpec((8, D), lambda i: (0, 0))],
        out_shape=[_sds((s, D), F32), _sds((8, D), F32)],
        compiler_params=_params(),
    )(dq, dkv, dzxd, x, dx2, mod8, n1w, w_in)


def _wgrad(a, b, name):
    s, m = a.shape
    n = b.shape[1]
    tk = min(1024, s)
    tm = _largest_divisor(m, (1408, 1024, 512, 256, 128))
    tn = n if n <= 2048 else _largest_divisor(n, (1408, 1024, 512, 256, 128))
    nk = s // tk

    def body(a_ref, b_ref, o_ref, acc):
        kk = pl.program_id(2)

        @pl.when(kk == 0)
        def _():
            acc[...] = jnp.zeros_like(acc)

        acc[...] += _dot_tn(a_ref[...], b_ref[...])

        @pl.when(kk == nk - 1)
        def _():
            o_ref[...] = acc[...].astype(BF16)

    return _pcall(
        body, name=name, grid=(m // tm, n // tn, nk),
        in_specs=[pl.BlockSpec((tk, tm), lambda i, j, kk: (kk, i)), pl.BlockSpec((tk, tn), lambda i, j, kk: (kk, j))],
        out_specs=pl.BlockSpec((tm, tn), lambda i, j, kk: (i, j)),
        out_shape=_sds((m, n), BF16),
        scratch_shapes=[pltpu.VMEM((tm, tn), F32)],
        compiler_params=_params(3),
    )(a, b)


PACK_ROWS = 24


def _pack_small(sm_f, sm_b, sm_s, sm_i, dsink):
    def body(f_ref, b_ref, s_ref, i_ref, k_ref, o_ref):
        o_ref[...] = jnp.zeros_like(o_ref)
        o_ref[0:2, :] = i_ref[0:2, :]
        o_ref[2:3, :] = b_ref[3:4, :]
        o_ref[3:5, :] = b_ref[0:2, :]
        o_ref[5:6, :] = f_ref[0:1, :]
        o_ref[6:7, :] = i_ref[2:3, :]
        o_ref[7:8, :] = b_ref[2:3, :]
        o_ref[8:9, :] = f_ref[1:2, :]
        o_ref[9:14, :] = s_ref[0:5, :]
        o_ref[14:15, :] = s_ref[5:6, :]
        o_ref[15:16, 0:3 * LB] = s_ref[6:7, 0:3 * LB]
        lane = lax.broadcasted_iota(jnp.int32, (1, LB), 1)
        sk = jnp.zeros((1, LB), F32)
        for h in range(NQ):
            sk = sk + jnp.where(lane == h, k_ref[h:h + 1, 0:1], 0.0)
        o_ref[15:16, 3 * LB:4 * LB] = sk
        o_ref[16:17, :] = f_ref[2:3, :]

    return _pcall(body, name="pack_small", out_shape=_sds((PACK_ROWS, D), F32))(sm_f, sm_b, sm_s, sm_i, dsink)


def _exchange(gathers, scatters, name):
    items, shapes, n_g = _exchange_items(gathers, scatters)
    n = len(items)

    def body(*refs):
        ex = _Exchange(n_g, refs[:n], refs[n:2 * n], refs[2 * n:])
        ex.start()
        ex.finish()

    any_spec = pl.BlockSpec(memory_space=pl.ANY)
    return _pcall(
        body, name=name, in_specs=[any_spec] * n, out_specs=[any_spec] * n, out_shape=shapes,
        scratch_shapes=_exchange_sems(n),
    )(*items)


def _exchange_items(gathers, scatters):
    items = list(gathers) + list(scatters)
    shapes = [_sds((N_DEV,) + a.shape, a.dtype) for a in gathers] + [_sds(a.shape, a.dtype) for a in scatters]
    return items, shapes, len(gathers)


def _exchange_sems(n):
    return [pltpu.SemaphoreType.DMA((n, N_DEV - 1)), pltpu.SemaphoreType.DMA((n, N_DEV - 1)),
            pltpu.SemaphoreType.DMA((n,))]


class _Exchange:
    def __init__(self, n_g, ins, outs, sems):
        self.n_g, self.ins, self.outs = n_g, ins, outs
        self.send_sems, self.recv_sems, self.loc_sems = sems
        xi, yi, ci = lax.axis_index("x"), lax.axis_index("y"), lax.axis_index("c")
        self.me = 4 * xi + 2 * yi + ci
        self.peers = []
        for r in range(1, N_DEV):
            px = 1 - xi if r & 4 else xi
            py = 1 - yi if r & 2 else yi
            pc = 1 - ci if r & 1 else ci
            self.peers.append(((px, py, pc), 4 * px + 2 * py + pc))

    def _copy(self, t, r, landing):
        dev, peer = self.peers[r]
        src = self.ins[t] if t < self.n_g else self.ins[t].at[peer]
        return pltpu.make_async_remote_copy(
            src_ref=src, dst_ref=self.outs[t].at[landing], send_sem=self.send_sems.at[t, r],
            recv_sem=self.recv_sems.at[t, r], device_id=dev, device_id_type=pl.DeviceIdType.MESH)

    def _local(self, t):
        src = self.ins[t] if t < self.n_g else self.ins[t].at[self.me]
        return pltpu.make_async_copy(src, self.outs[t].at[self.me], self.loc_sems.at[t])

    def start(self):
        for t in range(len(self.ins)):
            self._local(t).start()
            for r in range(N_DEV - 1):
                self._copy(t, r, self.me).start()

    def finish(self):
        n = len(self.ins)
        for t in range(n):
            for r in range(N_DEV - 1):
                self._copy(t, r, self.peers[r][1]).wait_recv()
        for t in range(n):
            for r in range(N_DEV - 1):
                self._copy(t, r, self.me).wait_send()
            self._local(t).wait()


def _ada_fwd(c_all, w_cols, b_cols):
    def body(c_ref, w_ref, b_ref, o_ref):
        cv = c_ref[...]
        sc = (cv * _sigmoid(cv)).astype(BF16)
        o_ref[...] = _dot(sc, w_ref[...].astype(BF16)) + b_ref[...]

    return _pcall(body, name="ada_fwd", out_shape=_sds((N_DEV, w_cols.shape[1]), F32),
                  compiler_params=_params(0))(c_all, w_cols, b_cols)


def _adamw(w, g, m, v):
    m2 = ADAM_B1 * m + (1.0 - ADAM_B1) * g
    v2 = ADAM_B2 * v + (1.0 - ADAM_B2) * (g * g)
    m_hat = m2 / (1.0 - ADAM_B1 ** ADAM_STEP)
    v_hat = v2 / (1.0 - ADAM_B2 ** ADAM_STEP)
    delta = -ADAM_LR * (m_hat / (jnp.sqrt(v_hat) + ADAM_EPS) + ADAM_WD * w)
    return delta, m2, v2


def _sum_adamw(parts, w, m, v, name):
    rws, cols = w.shape
    tr = _largest_divisor(rws, (256, 176, 128))

    def body(p_ref, w_ref, m_ref, v_ref, g_ref, d_ref, mo_ref, vo_ref):
        g = p_ref[0].astype(F32)
        for dev in range(1, N_DEV):
            g = g + p_ref[dev].astype(F32)
        g_ref[...] = g
        d_ref[...], mo_ref[...], vo_ref[...] = _adamw(w_ref[...], g, m_ref[...], v_ref[...])

    blk = pl.BlockSpec((tr, cols), lambda i: (i, 0))
    return _pcall(
        body, name=name, grid=(rws // tr,),
        in_specs=[pl.BlockSpec((N_DEV, tr, cols), lambda i: (0, i, 0)), blk, blk, blk],
        out_specs=[blk] * 4, out_shape=[_sds((rws, cols), F32)] * 4, compiler_params=_params(),
    )(parts, w, m, v)


def _wada_adamw(c_all, dmod_cols, w, m, v):
    rws, cols = w.shape
    tr = 256

    def body(c_ref, dm_ref, w_ref, m_ref, v_ref, g_ref, d_ref, mo_ref, vo_ref):
        cv = c_ref[...]
        sc = (cv * _sigmoid(cv)).astype(BF16)
        g = _dot_tn(sc, dm_ref[...].astype(BF16))
        g_ref[...] = g
        d_ref[...], mo_ref[...], vo_ref[...] = _adamw(w_ref[...], g, m_ref[...], v_ref[...])

    blk = pl.BlockSpec((tr, cols), lambda i: (i, 0))
    return _pcall(
        body, name="wada_adamw", grid=(rws // tr,),
        in_specs=[pl.BlockSpec((N_DEV, tr), lambda i: (0, i)), pl.BlockSpec((N_DEV, cols), lambda i: (0, 0)),
                  blk, blk, blk],
        out_specs=[blk] * 4, out_shape=[_sds((rws, cols), F32)] * 4, compiler_params=_params(),
    )(c_all, dmod_cols, w, m, v)


def _small_reduce(packs):
    def body(p_ref, o_ref):
        tot = p_ref[0]
        for dev in range(1, N_DEV):
            tot = tot + p_ref[dev]
        o_ref[...] = tot
        o_ref[16:17, :] = jnp.zeros((1, D), F32) + (0.5 / D) * jnp.sum(tot[16:17, :])

    return _pcall(body, name="small_reduce", out_shape=_sds((PACK_ROWS, D), F32))(packs)


def _adamw_many(ws, gs, ms, vs):
    k = len(ws)

    def body(*refs):
        for i in range(k):
            w_ref, g_ref, m_ref, v_ref = refs[i], refs[k + i], refs[2 * k + i], refs[3 * k + i]
            d_ref, mo_ref, vo_ref = refs[4 * k + i], refs[5 * k + i], refs[6 * k + i]
            d_ref[...], mo_ref[...], vo_ref[...] = _adamw(w_ref[...], g_ref[...], m_ref[...], v_ref[...])

    shp = [_sds(w.shape, F32) for w in ws]
    outs = _pcall(body, name="adamw_small", out_shape=shp * 3)(*ws, *gs, *ms, *vs)
    return outs[:k], outs[k:2 * k], outs[2 * k:]


def kernel(x, c, positions, w_ada, b_ada, norm1_w, w_in, conv_w, conv_b, dt_bias, a_log, d_skip, attn_sinks, ssm_norm_w, w_out, norm2_w, w_gate_up, w_down, final_norm_w, loss_target, m_w_ada, m_b_ada, m_norm1_w, m_w_in, m_conv_w, m_conv_b, m_dt_bias, m_a_log, m_d_skip, m_attn_sinks, m_ssm_norm_w, m_w_out, m_norm2_w, m_w_gate_up, m_w_down, m_final_norm_w, v_w_ada, v_b_ada, v_norm1_w, v_w_in, v_conv_w, v_conv_b, v_dt_bias, v_a_log, v_d_skip, v_attn_sinks, v_ssm_norm_w, v_w_out, v_norm2_w, v_w_gate_up, v_w_down, v_final_norm_w):
    s = x.shape[1]
    me = 4 * lax.axis_index("x") + 2 * lax.axis_index("y") + lax.axis_index("c")
    ada_cols = N_MOD * D // N_DEV

    c8 = jnp.pad(c, ((0, 7), (0, 0)))
    cw8 = jnp.pad(conv_w[0], ((0, 8 - CONVK), (0, 0)))
    g_c, g_in, g_cw = _exchange([c8, w_in[0].astype(BF16), cw8], [], "gather_in")
    c_all = g_c[:, 0, :]
    w_in_f = jnp.transpose(g_in, (1, 0, 2)).reshape(D, IN_PROJ)
    w_in_f = jnp.pad(w_in_f, ((0, 0), (0, INP - IN_PROJ)))
    conv_w8 = jnp.transpose(g_cw, (1, 0, 2)).reshape(8, D)

    b_cols = lax.dynamic_slice(b_ada, (0, me * ada_cols), (1, ada_cols))
    (g_mod,) = _exchange([_ada_fwd(c_all, w_ada[0], b_cols)], [], "gather_mod")
    mod = lax.dynamic_index_in_dim(g_mod, me, axis=1, keepdims=False).reshape(N_MOD, D)
    mod8 = jnp.pad(mod, ((0, 8 - N_MOD), (0, 0)))

    half = HD // 2
    inv_freq = ROPE_THETA ** (-jnp.arange(half, dtype=F32) / half)
    invf = jnp.tile(inv_freq, LB // half).reshape(1, LB)
    lanes = lambda a: jnp.pad(a, ((0, 0), (0, LB - a.shape[1])))
    ssm_p = jnp.pad(jnp.concatenate([lanes(dt_bias), lanes(a_log), lanes(d_skip)], axis=0), ((0, 5), (0, 0)))
    sinks8 = jnp.broadcast_to(attn_sinks.reshape(NQ, 1), (NQ, LB))

    xs, tgt, fnw = x[0], loss_target[0], final_norm_w.reshape(1, D)

    q, k, v, z, xbc, dtr, h1, cos, sin = _inproj_fwd(xs, positions[0].reshape(s, 1), invf, mod8, norm1_w, w_in_f)
    attn, yn, y, hs, (g_out, g_gu, g_down) = _mixer_fwd(
        q, k, v, sinks8, xbc, conv_w8, conv_b, dtr, ssm_p, z, ssm_norm_w,
        [w_out[0].astype(BF16), w_gate_up[0].astype(BF16), w_down[0].astype(BF16)])
    w_out_f = g_out.reshape(D, D)
    w_gu_f = jnp.transpose(g_gu, (1, 0, 2)).reshape(D, 2 * DFF)
    w_down_f = g_down.reshape(DFF, D)
    x2, h2, mo = _outproj_fwd(attn, yn, xs, mod8, norm2_w, w_out_f)
    gu, act, dx3, sm_f = _ffn_fwd_loss(h2, x2, tgt, mod8, fnw, w_gu_f, w_down_f)

    dx2, dff, dgu, dmix, dattn, dyn, sm_b = _ffn_bwd(dx3, gu, x2, mo, mod8, norm2_w, w_gu_f, w_down_f, w_out_f)
    p_gu = jnp.transpose(_wgrad(h2, dgu, "wgrad_gate_up").reshape(D, N_DEV, 2 * DFF // N_DEV), (1, 0, 2))
    p_down = _wgrad(act, dff, "wgrad_down").reshape(N_DEV, DFF // N_DEV, D)
    dq, dkv, dsink, dzxd, sm_s, (r_gu, r_down) = _mixer_bwd(
        q, k, v, attn, dattn, cos, sin, sinks8, dyn, y, z, xbc, conv_w8, conv_b, dtr, ssm_p, ssm_norm_w, hs,
        [p_gu, p_down])
    gx, sm_i = _inproj_bwd(dq, dkv, dzxd, xs, dx2, mod8, norm1_w, w_in_f)
    pack = _pack_small(sm_f, sm_b, sm_s, sm_i, dsink)
    dw_in_f = jnp.concatenate([_wgrad(h1, dq, "wgrad_in_q"), _wgrad(h1, dkv, "wgrad_in_kv"),
                               _wgrad(h1, dzxd, "wgrad_in_zxd")[:, :IN_PROJ - O_Z]], axis=1)
    p_in = jnp.transpose(dw_in_f.reshape(D, N_DEV, IN_PROJ // N_DEV), (1, 0, 2))
    p_out = jnp.concatenate([_wgrad(attn, dmix, "wgrad_out_attn"), _wgrad(yn, dmix, "wgrad_out_ssd")],
                            axis=0).reshape(N_DEV, D // N_DEV, D)
    g_pack, r_in, r_out = _exchange([pack], [p_in, p_out], "exchange_grads")

    tot = _small_reduce(g_pack)
    loss = tot[16, 0]
    dmod_all = g_pack[:, 0:N_MOD, :].reshape(N_DEV, N_MOD * D)
    dmod_cols = lax.dynamic_slice(dmod_all, (0, me * ada_cols), (N_DEV, ada_cols))

    big = {
        "w_ada": _wada_adamw(c_all, dmod_cols, w_ada[0], m_w_ada[0], v_w_ada[0]),
        "w_in": _sum_adamw(r_in, w_in[0], m_w_in[0], v_w_in[0], "adamw_in"),
        "w_out": _sum_adamw(r_out, w_out[0], m_w_out[0], v_w_out[0], "adamw_out"),
        "w_gate_up": _sum_adamw(r_gu, w_gate_up[0], m_w_gate_up[0], v_w_gate_up[0], "adamw_gate_up"),
        "w_down": _sum_adamw(r_down, w_down[0], m_w_down[0], v_w_down[0], "adamw_down"),
    }
    small_names = ["b_ada", "norm1_w", "conv_w", "conv_b", "dt_bias", "a_log", "d_skip", "attn_sinks", "ssm_norm_w",
                   "norm2_w", "final_norm_w"]
    row15 = tot[15:16, :]
    small_g = {
        "b_ada": tot[0:N_MOD, :].reshape(1, N_MOD * D),
        "norm1_w": tot[6:7, :],
        "conv_w": lax.dynamic_slice(tot[10:14, :], (0, me * (D // N_DEV)), (CONVK, D // N_DEV)),
        "conv_b": tot[9:10, :],
        "dt_bias": row15[:, 0:NH],
        "a_log": row15[:, LB:LB + NH],
        "d_skip": row15[:, 2 * LB:2 * LB + NH],
        "attn_sinks": row15[:, 3 * LB:3 * LB + NQ],
        "ssm_norm_w": tot[14:15, 0:SW],
        "norm2_w": tot[7:8, :],
        "final_norm_w": tot[8:9, :],
    }
    small_w = {"b_ada": b_ada, "norm1_w": norm1_w, "conv_w": conv_w[0], "conv_b": conv_b, "dt_bias": dt_bias,
               "a_log": a_log, "d_skip": d_skip, "attn_sinks": attn_sinks, "ssm_norm_w": ssm_norm_w,
               "norm2_w": norm2_w, "final_norm_w": final_norm_w.reshape(1, D)}
    small_m = {"b_ada": m_b_ada, "norm1_w": m_norm1_w, "conv_w": m_conv_w[0], "conv_b": m_conv_b,
               "dt_bias": m_dt_bias, "a_log": m_a_log, "d_skip": m_d_skip, "attn_sinks": m_attn_sinks,
               "ssm_norm_w": m_ssm_norm_w, "norm2_w": m_norm2_w, "final_norm_w": m_final_norm_w.reshape(1, D)}
    small_v = {"b_ada": v_b_ada, "norm1_w": v_norm1_w, "conv_w": v_conv_w[0], "conv_b": v_conv_b,
               "dt_bias": v_dt_bias, "a_log": v_a_log, "d_skip": v_d_skip, "attn_sinks": v_attn_sinks,
               "ssm_norm_w": v_ssm_norm_w, "norm2_w": v_norm2_w, "final_norm_w": v_final_norm_w.reshape(1, D)}
    s_d, s_m, s_v = _adamw_many([small_w[k] for k in small_names], [small_g[k] for k in small_names],
                                [small_m[k] for k in small_names], [small_v[k] for k in small_names])

    order = ["w_ada", "b_ada", "norm1_w", "w_in", "conv_w", "conv_b", "dt_bias", "a_log", "d_skip", "attn_sinks",
             "ssm_norm_w", "w_out", "norm2_w", "w_gate_up", "w_down", "final_norm_w"]
    lead = {"w_ada", "w_in", "conv_w", "w_out", "w_gate_up", "w_down"}
    grads, deltas, new_m, new_v = [], [], [], []
    for name in order:
        if name in big:
            g, d, m2, v2 = big[name]
        else:
            i = small_names.index(name)
            g, d, m2, v2 = small_g[name], s_d[i], s_m[i], s_v[i]
        if name in lead:
            g, d, m2, v2 = g[None], d[None], m2[None], v2[None]
        if name == "final_norm_w":
            g, d, m2, v2 = g.reshape(D), d.reshape(D), m2.reshape(D), v2.reshape(D)
        grads.append(g)
        deltas.append(d)
        new_m.append(m2)
        new_v.append(v2)
    return (loss, gx[None], *grads, *deltas, *new_m, *new_v)
```

```python
import functools
import math

import jax
import jax.numpy as jnp
from jax import lax
from jax.experimental import pallas as pl
from jax.experimental.pallas import tpu as pltpu

F32 = jnp.float32
BF16 = jnp.bfloat16

N_DEV = 8
D = 1024
HD = 64
NQ = 8
AW = 512
KVW = 128
SW = 512
NST = 128
NH = 8
LB = 128
CONVK = 4
DFF = 2816
N_MOD = 6
IN_PROJ = 2312
INP = 2432
O_Q, O_K, O_V, O_Z, O_XBC, O_DT = 0, 512, 640, 768, 1280, 2304
ZXD = INP - O_Z
EPS = 1e-6
NEG = -1e30
ROPE_THETA = 10000.0
VMEM_LIMIT = 56 * 1024 * 1024

ADAM_LR = 0.001
ADAM_B1 = 0.9
ADAM_B2 = 0.999
ADAM_EPS = 1e-08
ADAM_WD = 0.01
ADAM_STEP = 10

NT_DIMS = (((1,), (1,)), ((), ()))
TN_DIMS = (((0,), (0,)), ((), ()))


def _pcall(body, **kw):
    return pl.pallas_call(body, **kw)


def _sds(shape, dtype):
    return jax.ShapeDtypeStruct(shape, dtype)


def _params(n_grid=1):
    return pltpu.CompilerParams(dimension_semantics=("arbitrary",) * n_grid, vmem_limit_bytes=VMEM_LIMIT)


def _const(shape):
    return pl.BlockSpec(shape, lambda *_: (0,) * len(shape), pipeline_mode=pl.Buffered(1))


def _largest_divisor(n, candidates):
    for cand in candidates:
        if n % cand == 0:
            return cand
    raise ValueError(f"no tile in {candidates} divides {n}")


def _rows(t, w):
    return pl.BlockSpec((t, w), lambda i: (i, 0))


def _dot(a, b):
    return jnp.dot(a, b, preferred_element_type=F32)


def _dot_nt(a, b):
    return lax.dot_general(a, b, NT_DIMS, preferred_element_type=F32)


def _dot_tn(a, b):
    return lax.dot_general(a, b, TN_DIMS, preferred_element_type=F32)


def _sigmoid(v):
    return 1.0 / (1.0 + jnp.exp(-v))


def _softplus(v):
    return jnp.maximum(v, 0.0) + jnp.log1p(jnp.exp(-jnp.abs(v)))


def _rope_sign_mask(shape):
    lane = lax.broadcasted_iota(jnp.int32, shape, 1)
    return (lane % HD) < (HD // 2)


def _rope(t, cs, sn, inverse):
    r_dn = pltpu.roll(t, HD // 2, 1)
    r_up = pltpu.roll(t, LB - HD // 2, 1)
    first = _rope_sign_mask(t.shape)
    if inverse:
        rot = jnp.where(first, r_up, -r_dn)
    else:
        rot = jnp.where(first, -r_up, r_dn)
    return t * cs + rot * sn


def _norm_mod_fwd(xv, nw, shift, scale):
    r = lax.rsqrt(jnp.mean(xv * xv, axis=-1, keepdims=True) + EPS)
    xh = xv * r
    return (xh * nw) * (1.0 + scale) + shift


def _norm_mod_bwd(xv, dh, nw, scale):
    r = lax.rsqrt(jnp.mean(xv * xv, axis=-1, keepdims=True) + EPS)
    xh = xv * r
    xn = xh * nw
    d_shift = jnp.sum(dh, axis=0, keepdims=True)
    d_scale = jnp.sum(dh * xn, axis=0, keepdims=True)
    dxn = dh * (1.0 + scale)
    d_w = jnp.sum(dxn * xh, axis=0, keepdims=True)
    dxh = dxn * nw
    dx = r * (dxh - xh * jnp.mean(dxh * xh, axis=-1, keepdims=True))
    return dx, d_shift, d_scale, d_w


def _inproj_fwd(x, pos, invf, mod8, n1w, w_in):
    s = x.shape[0]
    tt = min(512, s)

    def body(x_ref, pos_ref, invf_ref, mod_ref, nw_ref, w_ref,
             q_ref, k_ref, v_ref, z_ref, xbc_ref, dtr_ref, h1_ref, cos_ref, sin_ref):
        h = _norm_mod_fwd(x_ref[...], nw_ref[...], mod_ref[0:1, :], mod_ref[1:2, :])
        hb = h.astype(BF16)
        h1_ref[...] = hb
        proj = _dot(hb, w_ref[...])
        ang = pos_ref[...].astype(F32) * invf_ref[...]
        cs = jnp.cos(ang)
        sn = jnp.sin(ang)
        cos_ref[...] = cs
        sin_ref[...] = sn
        for a in range(AW // LB):
            q_ref[:, a * LB:(a + 1) * LB] = _rope(proj[:, O_Q + a * LB:O_Q + (a + 1) * LB], cs, sn, False).astype(BF16)
        k_ref[...] = _rope(proj[:, O_K:O_V], cs, sn, False).astype(BF16)
        v_ref[...] = proj[:, O_V:O_Z].astype(BF16)
        z_ref[...] = proj[:, O_Z:O_XBC]
        xbc_ref[...] = proj[:, O_XBC:O_DT]
        dtr_ref[...] = proj[:, O_DT:INP]

    return _pcall(
        body, name="inproj_fwd", grid=(s // tt,),
        in_specs=[_rows(tt, D), _rows(tt, 1), _const((1, LB)), _const((8, D)), _const((1, D)), _const((D, INP))],
        out_specs=[_rows(tt, AW), _rows(tt, KVW), _rows(tt, KVW), _rows(tt, SW), _rows(tt, D), _rows(tt, LB),
                   _rows(tt, D), _rows(tt, LB), _rows(tt, LB)],
        out_shape=[_sds((s, AW), BF16), _sds((s, KVW), BF16), _sds((s, KVW), BF16), _sds((s, SW), F32),
                   _sds((s, D), F32), _sds((s, LB), F32), _sds((s, D), BF16), _sds((s, LB), F32), _sds((s, LB), F32)],
        compiler_params=_params(),
    )(x, pos, invf, mod8, n1w, w_in)


QPG = 4
ATT_SCALE = 1.0 / math.sqrt(HD)


def _stack_heads(val, g):
    return jnp.concatenate([val[:, (QPG * g + hh) * HD:(QPG * g + hh + 1) * HD] for hh in range(QPG)], axis=0)


def _unstack_heads(groups):
    return jnp.concatenate([grp[hh * LB:(hh + 1) * LB, :] for grp in groups for hh in range(QPG)], axis=1)


def _upper_mask():
    row = lax.broadcasted_iota(jnp.int32, (QPG * LB, LB), 0)
    col = lax.broadcasted_iota(jnp.int32, (QPG * LB, LB), 1)
    return col > (row % LB)


def _sink_col(sinks, g):
    return jnp.concatenate([jnp.broadcast_to(sinks[QPG * g + hh:QPG * g + hh + 1, 0:1], (LB, 1))
                            for hh in range(QPG)], axis=0)


def _attn_probs(n, qg, kp, kc, sink, upper):
    sp = _dot_nt(qg, kp) * ATT_SCALE
    sc = _dot_nt(qg, kc) * ATT_SCALE
    comb = jnp.where(upper, jnp.where(n > 0, sp, NEG), sc)
    m = jnp.maximum(jnp.max(comb, axis=-1, keepdims=True), sink)
    p = jnp.exp(comb - m)
    es = jnp.exp(sink - m)
    denom = jnp.sum(p, axis=-1, keepdims=True) + es
    return p / denom, es / denom


def _attn_fwd_block(n, q_ref, kp_ref, kc_ref, vp_ref, vc_ref, sink_ref, o_ref):
    qv, kp, kc, vp, vc = q_ref[...], kp_ref[...], kc_ref[...], vp_ref[...], vc_ref[...]
    sinks = sink_ref[...]
    upper = _upper_mask()
    outs = []
    for g in range(NQ // QPG):
        sl = slice(g * HD, (g + 1) * HD)
        probs, _ = _attn_probs(n, _stack_heads(qv, g), kp[:, sl], kc[:, sl], _sink_col(sinks, g), upper)
        outs.append(_dot(jnp.where(upper, probs, 0.0).astype(BF16), vp[:, sl])
                    + _dot(jnp.where(upper, 0.0, probs).astype(BF16), vc[:, sl]))
    o_ref[...] = _unstack_heads(outs).astype(BF16)


def _cumsum_rows(a, reverse):
    row = lax.broadcasted_iota(jnp.int32, a.shape, 0)
    step = 1
    while step < LB:
        if reverse:
            a = a + jnp.where(row < LB - step, pltpu.roll(a, LB - step, 0), 0.0)
        else:
            a = a + jnp.where(row >= step, pltpu.roll(a, step, 0), 0.0)
        step *= 2
    return a


SUB = 8


def _conv_shifts(tail, cur):
    row = lax.broadcasted_iota(jnp.int32, tail.shape, 0)
    out = [cur]
    for j in range(1, CONVK):
        rolled = pltpu.roll(cur, j, 0)
        top = jnp.where(row < j, pltpu.roll(tail, j, 0), rolled[0:SUB, :])
        out.append(jnp.concatenate([top, rolled[SUB:, :]], axis=0))
    return out


def _conv_advances(du, head):
    row = lax.broadcasted_iota(jnp.int32, head.shape, 0)
    out = []
    for j in range(1, CONVK):
        rolled = pltpu.roll(du, LB - j, 0)
        bottom = jnp.where(row >= SUB - j, pltpu.roll(head, SUB - j, 0), rolled[LB - SUB:, :])
        out.append(jnp.concatenate([rolled[:LB - SUB, :], bottom], axis=0))
    return out


def _split3(v):
    hi = v.astype(BF16)
    r1 = v - hi.astype(F32)
    mid = r1.astype(BF16)
    return hi, mid, (r1 - mid.astype(F32)).astype(BF16)


def _dot_exact(v, sel):
    hi, mid, lo = _split3(v)
    return _dot(hi, sel) + _dot(mid, sel) + _dot(lo, sel)


def _dot_nt_exact(v, sel):
    hi, mid, lo = _split3(v)
    return _dot_nt(hi, sel) + _dot_nt(mid, sel) + _dot_nt(lo, sel)


def _ssd_pre(xt_ref, xc_ref, cw_ref, cb_ref, dtr_ref, sp_ref, n):
    cur = xc_ref[...]
    tail = jnp.where(n > 0, xt_ref[...], 0.0)
    sh = _conv_shifts(tail, cur)
    u = cb_ref[...] + cw_ref[CONVK - 1:CONVK, :] * sh[0]
    for j in range(1, CONVK):
        u = u + cw_ref[CONVK - 1 - j:CONVK - j, :] * sh[j]
    sg_u = _sigmoid(u)
    xc = u * sg_u
    pre = dtr_ref[...] + sp_ref[0:1, :]
    dt = _softplus(pre)
    a_neg = -jnp.exp(sp_ref[1:2, :])
    acs = _cumsum_rows(dt * a_neg, False)
    return sh, u, sg_u, xc, pre, dt, a_neg, acs


def _gated_norm_fwd(y, z, nw):
    sz = z * _sigmoid(z)
    yz = y * sz
    parts = []
    for g in range(2):
        t = yz[:, g * 256:(g + 1) * 256]
        parts.append(t * lax.rsqrt(jnp.mean(t * t, axis=-1, keepdims=True) + EPS))
    return jnp.concatenate(parts, axis=1) * nw


HPG = 4
GW = HPG * HD


class _SsdChunk:
    def __init__(self, xc, dt, acs, spv, e64, e128):
        self.e64, self.e128 = e64, e128
        self.acs_t = jnp.transpose(acs)
        alast = acs[LB - 1:LB, :]
        self.e_all = jnp.exp(acs)
        self.dte_all = jnp.exp(alast - acs)
        self.elast = jnp.exp(alast)
        wide = _dot_exact(jnp.concatenate([dt, self.e_all, self.dte_all], axis=0), e64)
        self.dt_x, self.e_x, self.dte_x = wide[0:LB], wide[LB:2 * LB], wide[2 * LB:3 * LB]
        self.dsk_x = _dot_exact(spv, e64)[2:3, :]
        ac_x = _dot_exact(acs, e128)
        row = lax.broadcasted_iota(jnp.int32, (HPG * LB, LB), 0)
        col = lax.broadcasted_iota(jnp.int32, (HPG * LB, LB), 1)
        causal = (row % LB) >= col
        lane = lax.broadcasted_iota(jnp.int32, (LB, GW), 1)
        self.head_lanes = [(lane >= hh * HD) & (lane < (hh + 1) * HD) for hh in range(HPG)]
        self.xs, self.xdt, self.b, self.c, self.bb, self.cb16, self.cbm, self.dm_st, self.m_st = ([] for _ in range(9))
        for g in range(2):
            heads = range(HPG * g, HPG * (g + 1))
            ac_st = jnp.concatenate([ac_x[:, j * LB:(j + 1) * LB] for j in heads], axis=0)
            ar_st = jnp.concatenate([jnp.broadcast_to(self.acs_t[j:j + 1, :], (LB, LB)) for j in heads], axis=0)
            dm_st = jnp.exp(jnp.where(causal, ac_st - ar_st, NEG))
            bg = xc[:, SW + g * NST:SW + (g + 1) * NST]
            cg = xc[:, SW + 2 * NST + g * NST:SW + 2 * NST + (g + 1) * NST]
            bgb, cgb = bg.astype(BF16), cg.astype(BF16)
            cbm = _dot_nt(cgb, bgb)
            xs_g = xc[:, g * GW:(g + 1) * GW]
            self.xs.append(xs_g)
            self.xdt.append(xs_g * self.dt_x[:, g * GW:(g + 1) * GW])
            self.b.append(bg)
            self.c.append(cg)
            self.bb.append(bgb)
            self.cb16.append(cgb)
            self.cbm.append(cbm)
            self.dm_st.append(dm_st)
            self.m_st.append(jnp.concatenate([cbm] * HPG, axis=0) * dm_st)

    def elast_rows(self, g):
        return jnp.concatenate([jnp.broadcast_to(self.elast[:, j:j + 1], (HD, NST))
                                for j in range(HPG * g, HPG * (g + 1))], axis=0)

    def diag_blocks(self, stacked):
        out = stacked[(HPG - 1) * LB:HPG * LB, :]
        for hh in range(HPG - 2, -1, -1):
            out = jnp.where(self.head_lanes[hh], stacked[hh * LB:(hh + 1) * LB, :], out)
        return out

    def block_diag(self, v):
        return jnp.concatenate([jnp.where(self.head_lanes[hh], v, 0.0) for hh in range(HPG)], axis=0)


def _ssd_fwd_block(n, xt_ref, xc_ref, cw_ref, cb_ref, dtr_ref, sp_ref, z_ref, nw_ref, e64_ref, e128_ref,
                   yn_ref, y_ref, hs_ref, h_scr):
    @pl.when(n == 0)
    def _():
        h_scr[...] = jnp.zeros_like(h_scr)

    h_all = h_scr[...]
    hs_ref[0] = h_all
    _, _, _, xc, _, dt, _, acs = _ssd_pre(xt_ref, xc_ref, cw_ref, cb_ref, dtr_ref, sp_ref, n)
    ck = _SsdChunk(xc, dt, acs, sp_ref[...], e64_ref[...], e128_ref[...])
    ys, hn = [], []
    for g in range(2):
        gl = slice(g * GW, (g + 1) * GW)
        xdt = ck.xdt[g]
        hg = h_all[gl, :]
        y_diag = ck.diag_blocks(_dot(ck.m_st[g].astype(BF16), xdt.astype(BF16)))
        y_off = ck.e_x[:, gl] * _dot_nt(ck.cb16[g], hg.astype(BF16))
        ys.append(y_diag + y_off + ck.xs[g] * ck.dsk_x[:, gl])
        hn.append(hg * ck.elast_rows(g) + _dot_tn((xdt * ck.dte_x[:, gl]).astype(BF16), ck.bb[g]))
    h_scr[...] = jnp.concatenate(hn, axis=0)
    y = jnp.concatenate(ys, axis=1)
    y_ref[...] = y
    yn_ref[...] = _gated_norm_fwd(y, z_ref[...], nw_ref[...]).astype(BF16)


def _mixer_fwd(q, k, v, sinks8, xbc, conv_w8, conv_b, dtr, ssm_p, z, nw, gathers):
    s = q.shape[0]
    nb = s // LB
    prev = lambda n: (jnp.maximum(n - 1, 0), 0)
    cur = lambda n: (n, 0)
    items, ex_shapes, n_g = _exchange_items(gathers, [])
    ne = len(items)

    n_in = 16
    e64, e128 = _head_expanders()

    def body(*refs):
        a_in, s_in, ex_in = refs[:6], refs[6:n_in], refs[n_in:n_in + ne]
        o_ref, yn_ref, y_ref, hs_ref = refs[n_in + ne:n_in + 4 + ne]
        ex_out = refs[n_in + 4 + ne:n_in + 4 + 2 * ne]
        h_scr = refs[n_in + 4 + 2 * ne]
        sems = refs[n_in + 5 + 2 * ne:]
        n = pl.program_id(0)

        @pl.when(n == 0)
        def _():
            _Exchange(n_g, ex_in, ex_out, sems).start()

        _attn_fwd_block(n, *a_in, o_ref)
        _ssd_fwd_block(n, *s_in, yn_ref, y_ref, hs_ref, h_scr)

        @pl.when(n == nb - 1)
        def _():
            _Exchange(n_g, ex_in, ex_out, sems).finish()

    any_spec = pl.BlockSpec(memory_space=pl.ANY)
    tail = pl.BlockSpec((SUB, D), lambda n: (jnp.maximum(n * (LB // SUB) - 1, 0), 0))
    outs = _pcall(
        body, name="mixer_fwd", grid=(nb,),
        in_specs=[pl.BlockSpec((LB, AW), cur), pl.BlockSpec((LB, KVW), prev), pl.BlockSpec((LB, KVW), cur),
                  pl.BlockSpec((LB, KVW), prev), pl.BlockSpec((LB, KVW), cur), _const((8, LB)),
                  tail, pl.BlockSpec((LB, D), cur), _const((8, D)), _const((1, D)),
                  pl.BlockSpec((LB, LB), cur), _const((8, LB)), pl.BlockSpec((LB, SW), cur), _const((1, SW)),
                  _const(e64.shape), _const(e128.shape)]
        + [any_spec] * ne,
        out_specs=[pl.BlockSpec((LB, AW), cur), pl.BlockSpec((LB, SW), cur), pl.BlockSpec((LB, SW), cur),
                   pl.BlockSpec((1, NH * HD, NST), lambda n: (n, 0, 0))] + [any_spec] * ne,
        out_shape=[_sds((s, AW), BF16), _sds((s, SW), BF16), _sds((s, SW), F32), _sds((nb, NH * HD, NST), F32)]
        + ex_shapes,
        scratch_shapes=[pltpu.VMEM((NH * HD, NST), F32)] + _exchange_sems(ne),
        compiler_params=_params(),
    )(q, k, k, v, v, sinks8, xbc, xbc, conv_w8, conv_b, dtr, ssm_p, z, nw, e64, e128, *items)
    return outs[0], outs[1], outs[2], outs[3], outs[4:]


def _head_expanders():
    j = lax.broadcasted_iota(jnp.int32, (LB, NH * HD), 0)
    e64 = (lax.broadcasted_iota(jnp.int32, (LB, NH * HD), 1) // HD == j).astype(BF16)
    j = lax.broadcasted_iota(jnp.int32, (LB, NH * LB), 0)
    e128 = (lax.broadcasted_iota(jnp.int32, (LB, NH * LB), 1) // LB == j).astype(BF16)
    return e64, e128


def _outproj_fwd(attn, yn, x, mod8, n2w, w_out):
    s = x.shape[0]
    tt = min(512, s)

    def body(a_ref, y_ref, x_ref, mod_ref, nw_ref, w_ref, x2_ref, h2_ref, mo_ref):
        mix = jnp.concatenate([a_ref[...], y_ref[...]], axis=1)
        mo = _dot(mix, w_ref[...])
        mo_ref[...] = mo.astype(BF16)
        x2 = x_ref[...] + mod_ref[2:3, :] * mo
        x2_ref[...] = x2
        h2_ref[...] = _norm_mod_fwd(x2, nw_ref[...], mod_ref[3:4, :], mod_ref[4:5, :]).astype(BF16)

    return _pcall(
        body, name="outproj_fwd", grid=(s // tt,),
        in_specs=[_rows(tt, AW), _rows(tt, SW), _rows(tt, D), _const((8, D)), _const((1, D)), _const((D, D))],
        out_specs=[_rows(tt, D), _rows(tt, D), _rows(tt, D)],
        out_shape=[_sds((s, D), F32), _sds((s, D), BF16), _sds((s, D), BF16)],
        compiler_params=_params(),
    )(attn, yn, x, mod8, n2w, w_out)


def _ffn_fwd_loss(h2, x2, tgt, mod8, fnw, w_gu, w_down):
    s = x2.shape[0]
    tf = min(256, s)

    def body(h_ref, x2_ref, t_ref, mod_ref, fw_ref, wgu_ref, wd_ref, gu_ref, act_ref, dx3_ref, sm_ref):
        i = pl.program_id(0)

        @pl.when(i == 0)
        def _():
            sm_ref[...] = jnp.zeros_like(sm_ref)

        gu = _dot(h_ref[...], wgu_ref[...])
        gu_ref[...] = gu.astype(BF16)
        g = gu[:, :DFF]
        act = (g * _sigmoid(g) * gu[:, DFF:]).astype(BF16)
        act_ref[...] = act
        ff = _dot(act, wd_ref[...])
        x3 = x2_ref[...] + mod_ref[5:6, :] * ff
        r = lax.rsqrt(jnp.mean(x3 * x3, axis=-1, keepdims=True) + EPS)
        xh = x3 * r
        fw = fw_ref[...]
        err = xh * fw - t_ref[...]
        dy = err * (1.0 / D)
        dxh = dy * fw
        dx3 = r * (dxh - xh * jnp.mean(dxh * xh, axis=-1, keepdims=True))
        dx3_ref[...] = dx3
        sm_ref[0:1, :] += jnp.sum(dx3 * ff, axis=0, keepdims=True)
        sm_ref[1:2, :] += jnp.sum(dy * xh, axis=0, keepdims=True)
        sm_ref[2:3, :] += jnp.sum(err * err, axis=0, keepdims=True)

    return _pcall(
        body, name="ffn_fwd_loss", grid=(s // tf,),
        in_specs=[_rows(tf, D), _rows(tf, D), _rows(tf, D), _const((8, D)), _const((1, D)),
                  _const((D, 2 * DFF)), _const((DFF, D))],
        out_specs=[_rows(tf, 2 * DFF), _rows(tf, DFF), _rows(tf, D), pl.BlockSpec((8, D), lambda i: (0, 0))],
        out_shape=[_sds((s, 2 * DFF), BF16), _sds((s, DFF), BF16), _sds((s, D), F32), _sds((8, D), F32)],
        compiler_params=_params(),
    )(h2, x2, tgt, mod8, fnw, w_gu, w_down)


def _ffn_bwd(dx3, gu, x2, mixout, mod8, n2w, w_gu, w_down, w_out):
    s = x2.shape[0]
    tb = min(256, s)

    def body(dx3_ref, gu_ref, x2_ref, mo_ref, mod_ref, nw_ref, wgu_ref, wd_ref, wo_ref,
             dx2_ref, dff_ref, dgu_ref, dmix_ref, dattn_ref, dyn_ref, sm_ref):
        i = pl.program_id(0)

        @pl.when(i == 0)
        def _():
            sm_ref[...] = jnp.zeros_like(sm_ref)

        dx3 = dx3_ref[...]
        dff = (dx3 * mod_ref[5:6, :]).astype(BF16)
        dff_ref[...] = dff
        dact = _dot_nt(dff, wd_ref[...])
        g = gu_ref[:, :DFF].astype(F32)
        u = gu_ref[:, DFF:].astype(F32)
        sg = _sigmoid(g)
        dgu = jnp.concatenate([dact * u * sg * (1.0 + g * (1.0 - sg)), dact * g * sg], axis=1).astype(BF16)
        dgu_ref[...] = dgu
        dh2 = _dot_nt(dgu, wgu_ref[...])
        dxn, d_shift, d_scale, d_w = _norm_mod_bwd(x2_ref[...], dh2, nw_ref[...], mod_ref[4:5, :])
        dx2 = dx3 + dxn
        dx2_ref[...] = dx2
        sm_ref[0:1, :] += d_shift
        sm_ref[1:2, :] += d_scale
        sm_ref[2:3, :] += d_w
        sm_ref[3:4, :] += jnp.sum(dx2 * mo_ref[...].astype(F32), axis=0, keepdims=True)
        dmix = (dx2 * mod_ref[2:3, :]).astype(BF16)
        dmix_ref[...] = dmix
        dmi = _dot_nt(dmix, wo_ref[...])
        dattn_ref[...] = dmi[:, :AW].astype(BF16)
        dyn_ref[...] = dmi[:, AW:]

    return _pcall(
        body, name="ffn_bwd", grid=(s // tb,),
        in_specs=[_rows(tb, D), _rows(tb, 2 * DFF), _rows(tb, D), _rows(tb, D), _const((8, D)), _const((1, D)),
                  _const((D, 2 * DFF)), _const((DFF, D)), _const((D, D))],
        out_specs=[_rows(tb, D), _rows(tb, D), _rows(tb, 2 * DFF), _rows(tb, D), _rows(tb, AW), _rows(tb, SW),
                   pl.BlockSpec((8, D), lambda i: (0, 0))],
        out_shape=[_sds((s, D), F32), _sds((s, D), BF16), _sds((s, 2 * DFF), BF16), _sds((s, D), BF16),
                   _sds((s, AW), BF16), _sds((s, SW), F32), _sds((8, D), F32)],
        compiler_params=_params(),
    )(dx3, gu, x2, mixout, mod8, n2w, w_gu, w_down, w_out)


def _ssd_bwd_block(i, n, *refs):
    def run(dyn_ref, y_ref, z_ref, xt_ref, xc_ref, cw_ref, cb_ref, dtr_ref, sp_ref, nw_ref, hs_ref, e64_ref, e128_ref,
            dzxd_ref, sm_ref, dh_scr, dun_scr):
        @pl.when(i == 0)
        def _():
            dh_scr[...] = jnp.zeros_like(dh_scr)
            dun_scr[...] = jnp.zeros_like(dun_scr)
            sm_ref[...] = jnp.zeros_like(sm_ref)

        sh, u, sg_u, xc, pre, dt, a_neg, acs = _ssd_pre(xt_ref, xc_ref, cw_ref, cb_ref, dtr_ref, sp_ref, n)
        ck = _SsdChunk(xc, dt, acs, sp_ref[...], e64_ref[...], e128_ref[...])
        h_all = hs_ref[0]
        dh_all = dh_scr[...]
        riota = lax.broadcasted_iota(jnp.int32, (LB, LB), 0)
        lane1 = lax.broadcasted_iota(jnp.int32, (1, LB), 1)

        z = z_ref[...]
        y = y_ref[...]
        sgz = _sigmoid(z)
        sz = z * sgz
        yz = y * sz
        nwv = nw_ref[...]
        dyn_v = dyn_ref[...]
        dyhat = dyn_v * nwv
        yhat_parts, dyz_parts = [], []
        for g in range(2):
            gs = slice(g * 256, (g + 1) * 256)
            t = yz[:, gs]
            rg = lax.rsqrt(jnp.mean(t * t, axis=-1, keepdims=True) + EPS)
            yh = t * rg
            dyh = dyhat[:, gs]
            yhat_parts.append(yh)
            dyz_parts.append(rg * (dyh - yh * jnp.mean(dyh * yh, axis=-1, keepdims=True)))
        yhat = jnp.concatenate(yhat_parts, axis=1)
        dyz = jnp.concatenate(dyz_parts, axis=1)
        sm_ref[5:6, 0:SW] += jnp.sum(dyn_v * yhat, axis=0, keepdims=True)
        dy = dyz * sz
        dzxd_ref[:, 0:SW] = (dyz * y * sgz * (1.0 + z * (1.0 - sgz))).astype(BF16)

        cat = lambda parts: jnp.concatenate(parts, axis=1)
        dxs, dbs, dcs, dhp, g_cat, de_x, ddte_x, ddt_x, ddsk_x = ([] for _ in range(9))
        dacs_t = jnp.zeros((LB, LB), F32)
        hsum = jnp.zeros((1, LB), F32)
        for g in range(2):
            gl = slice(g * GW, (g + 1) * GW)
            xs_g, xdt, bgb, cgb = ck.xs[g], ck.xdt[g], ck.bb[g], ck.cb16[g]
            m_st, dm_st = ck.m_st[g], ck.dm_st[g]
            dt_x, e_x, dte_x = ck.dt_x[:, gl], ck.e_x[:, gl], ck.dte_x[:, gl]
            xdtb = xdt.astype(BF16)
            hg, dhn = h_all[gl, :], dh_all[gl, :]
            hb, dhnb = hg.astype(BF16), dhn.astype(BF16)
            dy_g = dy[:, gl]
            ddsk_x.append(jnp.sum(dy_g * xs_g, axis=0, keepdims=True))
            dy_bd = ck.block_diag(dy_g).astype(BF16)
            dm4 = _dot_nt(dy_bd, xdtb)
            dxdt = _dot_tn(m_st.astype(BF16), dy_bd)
            gmat = dm4 * m_st
            dcbm = dm4 * dm_st
            dcb = dcbm[0:LB] + dcbm[LB:2 * LB] + dcbm[2 * LB:3 * LB] + dcbm[3 * LB:4 * LB]
            g_cat.append(cat([gmat[hh * LB:(hh + 1) * LB, :] for hh in range(HPG)]))
            for hh in range(HPG):
                j = HPG * g + hh
                col_sum = jnp.sum(gmat[hh * LB:(hh + 1) * LB, :], axis=0, keepdims=True)
                dacs_t = dacs_t - jnp.where(riota == j, col_sum, 0.0)
                hsl = slice(hh * HD, (hh + 1) * HD)
                hsum = hsum + jnp.where(lane1 == j, jnp.sum(dhn[hsl, :] * hg[hsl, :]), 0.0)
            dchb = (dy_g * e_x).astype(BF16)
            dcg = _dot(dchb, hb)
            dh_prev = _dot_tn(dchb, cgb)
            de_x.append(dy_g * _dot_nt(cgb, hb))
            dxs_s = _dot_nt(bgb, dhnb)
            dbg = _dot((xdt * dte_x).astype(BF16), dhnb)
            dxdt = dxdt + dxs_s * dte_x
            ddte_x.append(dxs_s * xdt)
            dhp.append(dhn * ck.elast_rows(g) + dh_prev)
            dxs.append(dy_g * ck.dsk_x[:, gl] + dxdt * dt_x)
            ddt_x.append(dxdt * xs_g)
            dcbb = dcb.astype(BF16)
            dbs.append(dbg + _dot_tn(dcbb, cgb))
            dcs.append(dcg + _dot(dcbb, bgb))
        dh_scr[...] = jnp.concatenate(dhp, axis=0)
        red = _dot_nt_exact(jnp.concatenate([cat(de_x), cat(ddte_x), cat(ddt_x)], axis=0), ck.e64)
        de_c, ddte_c, ddt_c = red[0:LB], red[LB:2 * LB], red[2 * LB:3 * LB]
        ddsk = _dot_nt_exact(jnp.broadcast_to(cat(ddsk_x), (SUB, NH * HD)), ck.e64)[0:1, :]
        t1 = ddte_c * ck.dte_all
        dalast = jnp.sum(t1, axis=0, keepdims=True) + hsum * ck.elast
        dacs = (_dot_nt_exact(cat(g_cat), ck.e128) + de_c * ck.e_all - t1 + jnp.transpose(dacs_t)
                + jnp.where(riota == LB - 1, dalast, 0.0))
        da = _cumsum_rows(dacs, True)
        ddt = ddt_c + da * a_neg
        da_log = jnp.sum(da * dt, axis=0, keepdims=True) * a_neg
        ddtr = ddt * _sigmoid(pre)
        dzxd_ref[:, SW + D:ZXD] = ddtr.astype(BF16)
        sm_ref[6:7, 0:LB] += jnp.sum(ddtr, axis=0, keepdims=True)
        sm_ref[6:7, LB:2 * LB] += da_log
        sm_ref[6:7, 2 * LB:3 * LB] += ddsk

        du = cat(dxs + dbs + dcs) * (sg_u * (1.0 + u * (1.0 - sg_u)))
        sm_ref[0:1, :] += jnp.sum(du, axis=0, keepdims=True)
        for k in range(CONVK):
            sm_ref[1 + k:2 + k, :] += jnp.sum(du * sh[CONVK - 1 - k], axis=0, keepdims=True)
        adv = _conv_advances(du, dun_scr[...])
        dxbc = cw_ref[CONVK - 1:CONVK, :] * du
        for j in range(1, CONVK):
            dxbc = dxbc + cw_ref[CONVK - 1 - j:CONVK - j, :] * adv[j - 1]
        dun_scr[...] = du[0:SUB, :]
        dzxd_ref[:, SW:SW + D] = dxbc.astype(BF16)

    run(*refs)


def _attn_bwd_block(i, n, q_ref, kp_ref, kc_ref, vp_ref, vc_ref, o_ref, do_ref, cos_ref, sin_ref, sink_ref,
                    dq_ref, dkv_ref, ds_ref, ck_scr, cv_scr):
    @pl.when(i == 0)
    def _():
        ds_ref[...] = jnp.zeros_like(ds_ref)
        ck_scr[...] = jnp.zeros_like(ck_scr)
        cv_scr[...] = jnp.zeros_like(cv_scr)

    qv, kp, kc, vp, vc = q_ref[...], kp_ref[...], kc_ref[...], vp_ref[...], vc_ref[...]
    ov, dov, sinks = o_ref[...], do_ref[...], sink_ref[...]
    upper = _upper_mask()
    srow = lax.broadcasted_iota(jnp.int32, (8, LB), 0)
    slane = lax.broadcasted_iota(jnp.int32, (8, LB), 1)
    dsink = jnp.zeros((8, LB), F32)
    dq_g, dk_prev, dk_cur, dv_prev, dv_cur = [], [], [], [], []
    for g in range(NQ // QPG):
        sl = slice(g * HD, (g + 1) * HD)
        qg = _stack_heads(qv, g)
        dog = _stack_heads(dov, g)
        probs, psink = _attn_probs(n, qg, kp[:, sl], kc[:, sl], _sink_col(sinks, g), upper)
        delta = jnp.sum(dog.astype(F32) * _stack_heads(ov, g).astype(F32), axis=1, keepdims=True)
        dsc = probs * (jnp.where(upper, _dot_nt(dog, vp[:, sl]), _dot_nt(dog, vc[:, sl])) - delta)
        sink_terms = psink * delta
        for hh in range(QPG):
            dsink = dsink - jnp.where((srow == QPG * g + hh) & (slane == 0),
                                      jnp.sum(sink_terms[hh * LB:(hh + 1) * LB, :]), 0.0)
        ds_p = jnp.where(upper, dsc, 0.0).astype(BF16)
        ds_c = jnp.where(upper, 0.0, dsc).astype(BF16)
        dq_g.append((_dot(ds_p, kp[:, sl]) + _dot(ds_c, kc[:, sl])) * ATT_SCALE)
        dk_prev.append(_dot_tn(ds_p, qg) * ATT_SCALE)
        dk_cur.append(_dot_tn(ds_c, qg) * ATT_SCALE)
        dv_prev.append(_dot_tn(jnp.where(upper, probs, 0.0).astype(BF16), dog))
        dv_cur.append(_dot_tn(jnp.where(upper, 0.0, probs).astype(BF16), dog))
    ds_ref[...] += dsink
    cs = cos_ref[...]
    sn = sin_ref[...]
    dq = _unstack_heads(dq_g)
    for a in range(AW // LB):
        dq_ref[:, a * LB:(a + 1) * LB] = _rope(dq[:, a * LB:(a + 1) * LB], cs, sn, True).astype(BF16)
    dkv_ref[:, 0:KVW] = _rope(ck_scr[...] + jnp.concatenate(dk_cur, axis=1), cs, sn, True).astype(BF16)
    dkv_ref[:, KVW:2 * KVW] = (cv_scr[...] + jnp.concatenate(dv_cur, axis=1)).astype(BF16)
    ck_scr[...] = jnp.concatenate(dk_prev, axis=1)
    cv_scr[...] = jnp.concatenate(dv_prev, axis=1)


def _mixer_bwd(q, k, v, o, do, cos, sin, sinks8, dyn, y, z, xbc, conv_w8, conv_b, dtr, ssm_p, nw, hs, scatters):
    s = q.shape[0]
    nb = s // LB
    cur = lambda i: (nb - 1 - i, 0)
    prev = lambda i: (jnp.maximum(nb - 2 - i, 0), 0)
    n_in = 23
    items, ex_shapes, n_g = _exchange_items([], scatters)
    ne = len(items)
    e64, e128 = _head_expanders()

    def body(*refs):
        i = pl.program_id(0)
        n = nb - 1 - i
        a_in, s_in, ex_in = refs[:10], refs[10:n_in], refs[n_in:n_in + ne]
        dq_ref, dkv_ref, ds_ref, dzxd_ref, sm_ref = refs[n_in + ne:n_in + ne + 5]
        ex_out = refs[n_in + ne + 5:n_in + 2 * ne + 5]
        ck_scr, cv_scr, dh_scr, dun_scr = refs[n_in + 2 * ne + 5:n_in + 2 * ne + 9]
        sems = refs[n_in + 2 * ne + 9:]

        @pl.when(i == 0)
        def _():
            _Exchange(n_g, ex_in, ex_out, sems).start()

        _attn_bwd_block(i, n, *a_in, dq_ref, dkv_ref, ds_ref, ck_scr, cv_scr)
        _ssd_bwd_block(i, n, *s_in, dzxd_ref, sm_ref, dh_scr, dun_scr)

        @pl.when(i == nb - 1)
        def _():
            _Exchange(n_g, ex_in, ex_out, sems).finish()

    any_spec = pl.BlockSpec(memory_space=pl.ANY)
    tail = pl.BlockSpec((SUB, D), lambda i: (jnp.maximum((nb - 1 - i) * (LB // SUB) - 1, 0), 0))
    outs = _pcall(
        body, name="mixer_bwd", grid=(nb,),
        in_specs=[pl.BlockSpec((LB, AW), cur), pl.BlockSpec((LB, KVW), prev), pl.BlockSpec((LB, KVW), cur),
                  pl.BlockSpec((LB, KVW), prev), pl.BlockSpec((LB, KVW), cur), pl.BlockSpec((LB, AW), cur),
                  pl.BlockSpec((LB, AW), cur), pl.BlockSpec((LB, LB), cur), pl.BlockSpec((LB, LB), cur),
                  _const((8, LB)),
                  pl.BlockSpec((LB, SW), cur), pl.BlockSpec((LB, SW), cur), pl.BlockSpec((LB, SW), cur),
                  tail, pl.BlockSpec((LB, D), cur), _const((8, D)), _const((1, D)),
                  pl.BlockSpec((LB, LB), cur), _const((8, LB)), _const((1, SW)),
                  pl.BlockSpec((1, NH * HD, NST), lambda i: (nb - 1 - i, 0, 0)),
                  _const(e64.shape), _const(e128.shape)] + [any_spec] * ne,
        out_specs=[pl.BlockSpec((LB, AW), cur), pl.BlockSpec((LB, 2 * KVW), cur),
                   pl.BlockSpec((8, LB), lambda i: (0, 0)),
                   pl.BlockSpec((LB, ZXD), cur), pl.BlockSpec((8, D), lambda i: (0, 0))] + [any_spec] * ne,
        out_shape=[_sds((s, AW), BF16), _sds((s, 2 * KVW), BF16), _sds((8, LB), F32),
                   _sds((s, ZXD), BF16), _sds((8, D), F32)] + ex_shapes,
        scratch_shapes=[pltpu.VMEM((LB, KVW), F32), pltpu.VMEM((LB, KVW), F32),
                        pltpu.VMEM((NH * HD, NST), F32), pltpu.VMEM((SUB, D), F32)]
        + _exchange_sems(ne),
        compiler_params=_params(),
    )(q, k, k, v, v, o, do, cos, sin, sinks8, dyn, y, z, xbc, xbc, conv_w8, conv_b, dtr, ssm_p, nw, hs, e64, e128,
      *items)
    return outs[0], outs[1], outs[2], outs[3], outs[4], outs[5:]


def _inproj_bwd(dq, dkv, dzxd, x, dx2, mod8, n1w, w_in):
    s = x.shape[0]
    tt = min(512, s)

    def body(dq_ref, dkv_ref, dz_ref, x_ref, dx2_ref, mod_ref, nw_ref, w_ref, gx_ref, sm_ref):
        i = pl.program_id(0)

        @pl.when(i == 0)
        def _():
            sm_ref[...] = jnp.zeros_like(sm_ref)

        dproj = jnp.concatenate([dq_ref[...], dkv_ref[...], dz_ref[...]], axis=1)
        dh1 = _dot_nt(dproj, w_ref[...])
        dxn, d_shift, d_scale, d_w = _norm_mod_bwd(x_ref[...], dh1, nw_ref[...], mod_ref[1:2, :])
        gx_ref[...] = dx2_ref[...] + dxn
        sm_ref[0:1, :] += d_shift
        sm_ref[1:2, :] += d_scale
        sm_ref[2:3, :] += d_w

    return _pcall(
        body, name="inproj_bwd", grid=(s // tt,),
        in_specs=[_rows(tt, AW), _rows(tt, 2 * KVW), _rows(tt, ZXD), _rows(tt, D), _rows(tt, D),
                  _const((8, D)), _const((1, D)), _const((D, INP))],
        out_specs=[_rows(tt, D), pl.BlockSpec((8, D), lambda i: (0, 0))],
        out_shape=[_sds((s, D), F32), _sds((8, D), F32)],
        compiler_params=_params(),
    )(dq, dkv, dzxd, x, dx2, mod8, n1w, w_in)


def _wgrad(a, b, name):
    s, m = a.shape
    n = b.shape[1]
    tk = min(1024, s)
    tm = _largest_divisor(m, (1408, 1024, 512, 256, 128))
    tn = n if n <= 2048 else _largest_divisor(n, (1408, 1024, 512, 256, 128))
    nk = s // tk

    def body(a_ref, b_ref, o_ref, acc):
        kk = pl.program_id(2)

        @pl.when(kk == 0)
        def _():
            acc[...] = jnp.zeros_like(acc)

        acc[...] += _dot_tn(a_ref[...], b_ref[...])

        @pl.when(kk == nk - 1)
        def _():
            o_ref[...] = acc[...].astype(BF16)

    return _pcall(
        body, name=name, grid=(m // tm, n // tn, nk),
        in_specs=[pl.BlockSpec((tk, tm), lambda i, j, kk: (kk, i)), pl.BlockSpec((tk, tn), lambda i, j, kk: (kk, j))],
        out_specs=pl.BlockSpec((tm, tn), lambda i, j, kk: (i, j)),
        out_shape=_sds((m, n), BF16),
        scratch_shapes=[pltpu.VMEM((tm, tn), F32)],
        compiler_params=_params(3),
    )(a, b)


PACK_ROWS = 24


def _pack_small(sm_f, sm_b, sm_s, sm_i, dsink):
    def body(f_ref, b_ref, s_ref, i_ref, k_ref, o_ref):
        o_ref[...] = jnp.zeros_like(o_ref)
        o_ref[0:2, :] = i_ref[0:2, :]
        o_ref[2:3, :] = b_ref[3:4, :]
        o_ref[3:5, :] = b_ref[0:2, :]
        o_ref[5:6, :] = f_ref[0:1, :]
        o_ref[6:7, :] = i_ref[2:3, :]
        o_ref[7:8, :] = b_ref[2:3, :]
        o_ref[8:9, :] = f_ref[1:2, :]
        o_ref[9:14, :] = s_ref[0:5, :]
        o_ref[14:15, :] = s_ref[5:6, :]
        o_ref[15:16, 0:3 * LB] = s_ref[6:7, 0:3 * LB]
        lane = lax.broadcasted_iota(jnp.int32, (1, LB), 1)
        sk = jnp.zeros((1, LB), F32)
        for h in range(NQ):
            sk = sk + jnp.where(lane == h, k_ref[h:h + 1, 0:1], 0.0)
        o_ref[15:16, 3 * LB:4 * LB] = sk
        o_ref[16:17, :] = f_ref[2:3, :]

    return _pcall(body, name="pack_small", out_shape=_sds((PACK_ROWS, D), F32))(sm_f, sm_b, sm_s, sm_i, dsink)


def _exchange(gathers, scatters, name):
    items, shapes, n_g = _exchange_items(gathers, scatters)
    n = len(items)

    def body(*refs):
        ex = _Exchange(n_g, refs[:n], refs[n:2 * n], refs[2 * n:])
        ex.start()
        ex.finish()

    any_spec = pl.BlockSpec(memory_space=pl.ANY)
    return _pcall(
        body, name=name, in_specs=[any_spec] * n, out_specs=[any_spec] * n, out_shape=shapes,
        scratch_shapes=_exchange_sems(n),
    )(*items)


def _exchange_items(gathers, scatters):
    items = list(gathers) + list(scatters)
    shapes = [_sds((N_DEV,) + a.shape, a.dtype) for a in gathers] + [_sds(a.shape, a.dtype) for a in scatters]
    return items, shapes, len(gathers)


def _exchange_sems(n):
    return [pltpu.SemaphoreType.DMA((n, N_DEV - 1)), pltpu.SemaphoreType.DMA((n, N_DEV - 1)),
            pltpu.SemaphoreType.DMA((n,))]


class _Exchange:
    def __init__(self, n_g, ins, outs, sems):
        self.n_g, self.ins, self.outs = n_g, ins, outs
        self.send_sems, self.recv_sems, self.loc_sems = sems
        xi, yi, ci = lax.axis_index("x"), lax.axis_index("y"), lax.axis_index("c")
        self.me = 4 * xi + 2 * yi + ci
        self.peers = []
        for r in range(1, N_DEV):
            px = 1 - xi if r & 4 else xi
            py = 1 - yi if r & 2 else yi
            pc = 1 - ci if r & 1 else ci
            self.peers.append(((px, py, pc), 4 * px + 2 * py + pc))

    def _copy(self, t, r, landing):
        dev, peer = self.peers[r]
        src = self.ins[t] if t < self.n_g else self.ins[t].at[peer]
        return pltpu.make_async_remote_copy(
            src_ref=src, dst_ref=self.outs[t].at[landing], send_sem=self.send_sems.at[t, r],
            recv_sem=self.recv_sems.at[t, r], device_id=dev, device_id_type=pl.DeviceIdType.MESH)

    def _local(self, t):
        src = self.ins[t] if t < self.n_g else self.ins[t].at[self.me]
        return pltpu.make_async_copy(src, self.outs[t].at[self.me], self.loc_sems.at[t])

    def start(self):
        for t in range(len(self.ins)):
            self._local(t).start()
            for r in range(N_DEV - 1):
                self._copy(t, r, self.me).start()

    def finish(self):
        n = len(self.ins)
        for t in range(n):
            for r in range(N_DEV - 1):
                self._copy(t, r, self.peers[r][1]).wait_recv()
        for t in range(n):
            for r in range(N_DEV - 1):
                self._copy(t, r, self.me).wait_send()
            self._local(t).wait()


def _ada_fwd(c_all, w_cols, b_cols):
    def body(c_ref, w_ref, b_ref, o_ref):
        cv = c_ref[...]
        sc = (cv * _sigmoid(cv)).astype(BF16)
        o_ref[...] = _dot(sc, w_ref[...].astype(BF16)) + b_ref[...]

    return _pcall(body, name="ada_fwd", out_shape=_sds((N_DEV, w_cols.shape[1]), F32),
                  compiler_params=_params(0))(c_all, w_cols, b_cols)


def _adamw(w, g, m, v):
    m2 = ADAM_B1 * m + (1.0 - ADAM_B1) * g
    v2 = ADAM_B2 * v + (1.0 - ADAM_B2) * (g * g)
    m_hat = m2 / (1.0 - ADAM_B1 ** ADAM_STEP)
    v_hat = v2 / (1.0 - ADAM_B2 ** ADAM_STEP)
    delta = -ADAM_LR * (m_hat / (jnp.sqrt(v_hat) + ADAM_EPS) + ADAM_WD * w)
    return delta, m2, v2


def _sum_adamw(parts, w, m, v, name):
    rws, cols = w.shape
    tr = _largest_divisor(rws, (256, 176, 128))

    def body(p_ref, w_ref, m_ref, v_ref, g_ref, d_ref, mo_ref, vo_ref):
        g = p_ref[0].astype(F32)
        for dev in range(1, N_DEV):
            g = g + p_ref[dev].astype(F32)
        g_ref[...] = g
        d_ref[...], mo_ref[...], vo_ref[...] = _adamw(w_ref[...], g, m_ref[...], v_ref[...])

    blk = pl.BlockSpec((tr, cols), lambda i: (i, 0))
    return _pcall(
        body, name=name, grid=(rws // tr,),
        in_specs=[pl.BlockSpec((N_DEV, tr, cols), lambda i: (0, i, 0)), blk, blk, blk],
        out_specs=[blk] * 4, out_shape=[_sds((rws, cols), F32)] * 4, compiler_params=_params(),
    )(parts, w, m, v)


def _wada_adamw(c_all, dmod_cols, w, m, v):
    rws, cols = w.shape
    tr = 256

    def body(c_ref, dm_ref, w_ref, m_ref, v_ref, g_ref, d_ref, mo_ref, vo_ref):
        cv = c_ref[...]
        sc = (cv * _sigmoid(cv)).astype(BF16)
        g = _dot_tn(sc, dm_ref[...].astype(BF16))
        g_ref[...] = g
        d_ref[...], mo_ref[...], vo_ref[...] = _adamw(w_ref[...], g, m_ref[...], v_ref[...])

    blk = pl.BlockSpec((tr, cols), lambda i: (i, 0))
    return _pcall(
        body, name="wada_adamw", grid=(rws // tr,),
        in_specs=[pl.BlockSpec((N_DEV, tr), lambda i: (0, i)), pl.BlockSpec((N_DEV, cols), lambda i: (0, 0)),
                  blk, blk, blk],
        out_specs=[blk] * 4, out_shape=[_sds((rws, cols), F32)] * 4, compiler_params=_params(),
    )(c_all, dmod_cols, w, m, v)


def _small_reduce(packs):
    def body(p_ref, o_ref):
        tot = p_ref[0]
        for dev in range(1, N_DEV):
            tot = tot + p_ref[dev]
        o_ref[...] = tot
        o_ref[16:17, :] = jnp.zeros((1, D), F32) + (0.5 / D) * jnp.sum(tot[16:17, :])

    return _pcall(body, name="small_reduce", out_shape=_sds((PACK_ROWS, D), F32))(packs)


def _adamw_many(ws, gs, ms, vs):
    k = len(ws)

    def body(*refs):
        for i in range(k):
            w_ref, g_ref, m_ref, v_ref = refs[i], refs[k + i], refs[2 * k + i], refs[3 * k + i]
            d_ref, mo_ref, vo_ref = refs[4 * k + i], refs[5 * k + i], refs[6 * k + i]
            d_ref[...], mo_ref[...], vo_ref[...] = _adamw(w_ref[...], g_ref[...], m_ref[...], v_ref[...])

    shp = [_sds(w.shape, F32) for w in ws]
    outs = _pcall(body, name="adamw_small", out_shape=shp * 3)(*ws, *gs, *ms, *vs)
    return outs[:k], outs[k:2 * k], outs[2 * k:]


def kernel(x, c, positions, w_ada, b_ada, norm1_w, w_in, conv_w, conv_b, dt_bias, a_log, d_skip, attn_sinks, ssm_norm_w, w_out, norm2_w, w_gate_up, w_down, final_norm_w, loss_target, m_w_ada, m_b_ada, m_norm1_w, m_w_in, m_conv_w, m_conv_b, m_dt_bias, m_a_log, m_d_skip, m_attn_sinks, m_ssm_norm_w, m_w_out, m_norm2_w, m_w_gate_up, m_w_down, m_final_norm_w, v_w_ada, v_b_ada, v_norm1_w, v_w_in, v_conv_w, v_conv_b, v_dt_bias, v_a_log, v_d_skip, v_attn_sinks, v_ssm_norm_w, v_w_out, v_norm2_w, v_w_gate_up, v_w_down, v_final_norm_w):
    s = x.shape[1]
    me = 4 * lax.axis_index("x") + 2 * lax.axis_index("y") + lax.axis_index("c")
    ada_cols = N_MOD * D // N_DEV

    c8 = jnp.pad(c, ((0, 7), (0, 0)))
    cw8 = jnp.pad(conv_w[0], ((0, 8 - CONVK), (0, 0)))
    g_c, g_in, g_cw = _exchange([c8, w_in[0].astype(BF16), cw8], [], "gather_in")
    c_all = g_c[:, 0, :]
    w_in_f = jnp.transpose(g_in, (1, 0, 2)).reshape(D, IN_PROJ)
    w_in_f = jnp.pad(w_in_f, ((0, 0), (0, INP - IN_PROJ)))
    conv_w8 = jnp.transpose(g_cw, (1, 0, 2)).reshape(8, D)

    b_cols = lax.dynamic_slice(b_ada, (0, me * ada_cols), (1, ada_cols))
    (g_mod,) = _exchange([_ada_fwd(c_all, w_ada[0], b_cols)], [], "gather_mod")
    mod = lax.dynamic_index_in_dim(g_mod, me, axis=1, keepdims=False).reshape(N_MOD, D)
    mod8 = jnp.pad(mod, ((0, 8 - N_MOD), (0, 0)))

    half = HD // 2
    inv_freq = ROPE_THETA ** (-jnp.arange(half, dtype=F32) / half)
    invf = jnp.tile(inv_freq, LB // half).reshape(1, LB)
    lanes = lambda a: jnp.pad(a, ((0, 0), (0, LB - a.shape[1])))
    ssm_p = jnp.pad(jnp.concatenate([lanes(dt_bias), lanes(a_log), lanes(d_skip)], axis=0), ((0, 5), (0, 0)))
    sinks8 = jnp.broadcast_to(attn_sinks.reshape(NQ, 1), (NQ, LB))

    xs, tgt, fnw = x[0], loss_target[0], final_norm_w.reshape(1, D)

    q, k, v, z, xbc, dtr, h1, cos, sin = _inproj_fwd(xs, positions[0].reshape(s, 1), invf, mod8, norm1_w, w_in_f)
    attn, yn, y, hs, (g_out, g_gu, g_down) = _mixer_fwd(
        q, k, v, sinks8, xbc, conv_w8, conv_b, dtr, ssm_p, z, ssm_norm_w,
        [w_out[0].astype(BF16), w_gate_up[0].astype(BF16), w_down[0].astype(BF16)])
    w_out_f = g_out.reshape(D, D)
    w_gu_f = jnp.transpose(g_gu, (1, 0, 2)).reshape(D, 2 * DFF)
    w_down_f = g_down.reshape(DFF, D)
    x2, h2, mo = _outproj_fwd(attn, yn, xs, mod8, norm2_w, w_out_f)
    gu, act, dx3, sm_f = _ffn_fwd_loss(h2, x2, tgt, mod8, fnw, w_gu_f, w_down_f)

    dx2, dff, dgu, dmix, dattn, dyn, sm_b = _ffn_bwd(dx3, gu, x2, mo, mod8, norm2_w, w_gu_f, w_down_f, w_out_f)
    p_gu = jnp.transpose(_wgrad(h2, dgu, "wgrad_gate_up").reshape(D, N_DEV, 2 * DFF // N_DEV), (1, 0, 2))
    p_down = _wgrad(act, dff, "wgrad_down").reshape(N_DEV, DFF // N_DEV, D)
    dq, dkv, dsink, dzxd, sm_s, (r_gu, r_down) = _mixer_bwd(
        q, k, v, attn, dattn, cos, sin, sinks8, dyn, y, z, xbc, conv_w8, conv_b, dtr, ssm_p, ssm_norm_w, hs,
        [p_gu, p_down])
    gx, sm_i = _inproj_bwd(dq, dkv, dzxd, xs, dx2, mod8, norm1_w, w_in_f)
    pack = _pack_small(sm_f, sm_b, sm_s, sm_i, dsink)
    dw_in_f = jnp.concatenate([_wgrad(h1, dq, "wgrad_in_q"), _wgrad(h1, dkv, "wgrad_in_kv"),
                               _wgrad(h1, dzxd, "wgrad_in_zxd")[:, :IN_PROJ - O_Z]], axis=1)
    p_in = jnp.transpose(dw_in_f.reshape(D, N_DEV, IN_PROJ // N_DEV), (1, 0, 2))
    p_out = jnp.concatenate([_wgrad(attn, dmix, "wgrad_out_attn"), _wgrad(yn, dmix, "wgrad_out_ssd")],
                            axis=0).reshape(N_DEV, D // N_DEV, D)
    g_pack, r_in, r_out = _exchange([pack], [p_in, p_out], "exchange_grads")

    tot = _small_reduce(g_pack)
    loss = tot[16, 0]
    dmod_all = g_pack[:, 0:N_MOD, :].reshape(N_DEV, N_MOD * D)
    dmod_cols = lax.dynamic_slice(dmod_all, (0, me * ada_cols), (N_DEV, ada_cols))

    big = {
        "w_ada": _wada_adamw(c_all, dmod_cols, w_ada[0], m_w_ada[0], v_w_ada[0]),
        "w_in": _sum_adamw(r_in, w_in[0], m_w_in[0], v_w_in[0], "adamw_in"),
        "w_out": _sum_adamw(r_out, w_out[0], m_w_out[0], v_w_out[0], "adamw_out"),
        "w_gate_up": _sum_adamw(r_gu, w_gate_up[0], m_w_gate_up[0], v_w_gate_up[0], "adamw_gate_up"),
        "w_down": _sum_adamw(r_down, w_down[0], m_w_down[0], v_w_down[0], "adamw_down"),
    }
    small_names = ["b_ada", "norm1_w", "conv_w", "conv_b", "dt_bias", "a_log", "d_skip", "attn_sinks", "ssm_norm_w",
                   "norm2_w", "final_norm_w"]
    row15 = tot[15:16, :]
    small_g = {
        "b_ada": tot[0:N_MOD, :].reshape(1, N_MOD * D),
        "norm1_w": tot[6:7, :],
        "conv_w": lax.dynamic_slice(tot[10:14, :], (0, me * (D // N_DEV)), (CONVK, D // N_DEV)),
        "conv_b": tot[9:10, :],
        "dt_bias": row15[:, 0:NH],
        "a_log": row15[:, LB:LB + NH],
        "d_skip": row15[:, 2 * LB:2 * LB + NH],
        "attn_sinks": row15[:, 3 * LB:3 * LB + NQ],
        "ssm_norm_w": tot[14:15, 0:SW],
        "norm2_w": tot[7:8, :],
        "final_norm_w": tot[8:9, :],
    }
    small_w = {"b_ada": b_ada, "norm1_w": norm1_w, "conv_w": conv_w[0], "conv_b": conv_b, "dt_bias": dt_bias,
               "a_log": a_log, "d_skip": d_skip, "attn_sinks": attn_sinks, "ssm_norm_w": ssm_norm_w,
               "norm2_w": norm2_w, "final_norm_w": final_norm_w.reshape(1, D)}
    small_m = {"b_ada": m_b_ada, "norm1_w": m_norm1_w, "conv_w": m_conv_w[0], "conv_b": m_conv_b,
               "dt_bias": m_dt_bias, "a_log": m_a_log, "d_skip": m_d_skip, "attn_sinks": m_attn_sinks,
               "ssm_norm_w": m_ssm_norm_w, "norm2_w": m_norm2_w, "final_norm_w": m_final_norm_w.reshape(1, D)}
    small_v = {"b_ada": v_b_ada, "norm1_w": v_norm1_w, "conv_w": v_conv_w[0], "conv_b": v_conv_b,
               "dt_bias": v_dt_bias, "a_log": v_a_log, "d_skip": v_d_skip, "attn_sinks": v_attn_sinks,
               "ssm_norm_w": v_ssm_norm_w, "norm2_w": v_norm2_w, "final_norm_w": v_final_norm_w.reshape(1, D)}
    s_d, s_m, s_v = _adamw_many([small_w[k] for k in small_names], [small_g[k] for k in small_names],
                                [small_m[k] for k in small_names], [small_v[k] for k in small_names])

    order = ["w_ada", "b_ada", "norm1_w", "w_in", "conv_w", "conv_b", "dt_bias", "a_log", "d_skip", "attn_sinks",
             "ssm_norm_w", "w_out", "norm2_w", "w_gate_up", "w_down", "final_norm_w"]
    lead = {"w_ada", "w_in", "conv_w", "w_out", "w_gate_up", "w_down"}
    grads, deltas, new_m, new_v = [], [], [], []
    for name in order:
        if name in big:
            g, d, m2, v2 = big[name]
        else:
            i = small_names.index(name)
            g, d, m2, v2 = small_g[name], s_d[i], s_m[i], s_v[i]
        if name in lead:
            g, d, m2, v2 = g[None], d[None], m2[None], v2[None]
        if name == "final_norm_w":
            g, d, m2, v2 = g.reshape(D), d.reshape(D), m2.reshape(D), v2.reshape(D)
        grads.append(g)
        deltas.append(d)
        new_m.append(m2)
        new_v.append(v2)
    return (loss, gx[None], *grads, *deltas, *new_m, *new_v)
```

```python
import functools
import math

import jax
import jax.numpy as jnp
from jax import lax
from jax.experimental import pallas as pl
from jax.experimental.pallas import tpu as pltpu

F32 = jnp.float32
BF16 = jnp.bfloat16

N_DEV = 8
D = 1024
HD = 64
NQ = 8
AW = 512
KVW = 128
SW = 512
NST = 128
NH = 8
LB = 128
CONVK = 4
DFF = 2816
N_MOD = 6
IN_PROJ = 2312
INP = 2432
O_Q, O_K, O_V, O_Z, O_XBC, O_DT = 0, 512, 640, 768, 1280, 2304
ZXD = INP - O_Z
EPS = 1e-6
NEG = -1e30
ROPE_THETA = 10000.0
VMEM_LIMIT = 56 * 1024 * 1024

ADAM_LR = 0.001
ADAM_B1 = 0.9
ADAM_B2 = 0.999
ADAM_EPS = 1e-08
ADAM_WD = 0.01
ADAM_STEP = 10

NT_DIMS = (((1,), (1,)), ((), ()))
TN_DIMS = (((0,), (0,)), ((), ()))


def _pcall(body, **kw):
    return pl.pallas_call(body, **kw)


def _sds(shape, dtype):
    return jax.ShapeDtypeStruct(shape, dtype)


def _params(n_grid=1):
    return pltpu.CompilerParams(dimension_semantics=("arbitrary",) * n_grid, vmem_limit_bytes=VMEM_LIMIT)


def _const(shape):
    return pl.BlockSpec(shape, lambda *_: (0,) * len(shape), pipeline_mode=pl.Buffered(1))


def _largest_divisor(n, candidates):
    for cand in candidates:
        if n % cand == 0:
            return cand
    raise ValueError(f"no tile in {candidates} divides {n}")


def _rows(t, w):
    return pl.BlockSpec((t, w), lambda i: (i, 0))


def _dot(a, b):
    return jnp.dot(a, b, preferred_element_type=F32)


def _dot_nt(a, b):
    return lax.dot_general(a, b, NT_DIMS, preferred_element_type=F32)


def _dot_tn(a, b):
    return lax.dot_general(a, b, TN_DIMS, preferred_element_type=F32)


def _sigmoid(v):
    return 1.0 / (1.0 + jnp.exp(-v))


def _softplus(v):
    return jnp.maximum(v, 0.0) + jnp.log1p(jnp.exp(-jnp.abs(v)))


def _rope_sign_mask(shape):
    lane = lax.broadcasted_iota(jnp.int32, shape, 1)
    return (lane % HD) < (HD // 2)


def _rope(t, cs, sn, inverse):
    r_dn = pltpu.roll(t, HD // 2, 1)
    r_up = pltpu.roll(t, LB - HD // 2, 1)
    first = _rope_sign_mask(t.shape)
    if inverse:
        rot = jnp.where(first, r_up, -r_dn)
    else:
        rot = jnp.where(first, -r_up, r_dn)
    return t * cs + rot * sn


def _norm_mod_fwd(xv, nw, shift, scale):
    r = lax.rsqrt(jnp.mean(xv * xv, axis=-1, keepdims=True) + EPS)
    xh = xv * r
    return (xh * nw) * (1.0 + scale) + shift


def _norm_mod_bwd(xv, dh, nw, scale):
    r = lax.rsqrt(jnp.mean(xv * xv, axis=-1, keepdims=True) + EPS)
    xh = xv * r
    xn = xh * nw
    d_shift = jnp.sum(dh, axis=0, keepdims=True)
    d_scale = jnp.sum(dh * xn, axis=0, keepdims=True)
    dxn = dh * (1.0 + scale)
    d_w = jnp.sum(dxn * xh, axis=0, keepdims=True)
    dxh = dxn * nw
    dx = r * (dxh - xh * jnp.mean(dxh * xh, axis=-1, keepdims=True))
    return dx, d_shift, d_scale, d_w


def _inproj_fwd(x, pos, invf, mod8, n1w, w_in):
    s = x.shape[0]
    tt = min(512, s)

    def body(x_ref, pos_ref, invf_ref, mod_ref, nw_ref, w_ref,
             q_ref, k_ref, v_ref, z_ref, xbc_ref, dtr_ref, h1_ref, cos_ref, sin_ref):
        h = _norm_mod_fwd(x_ref[...], nw_ref[...], mod_ref[0:1, :], mod_ref[1:2, :])
        hb = h.astype(BF16)
        h1_ref[...] = hb
        proj = _dot_nt(hb, w_ref[...])
        ang = pos_ref[...].astype(F32) * invf_ref[...]
        cs = jnp.cos(ang)
        sn = jnp.sin(ang)
        cos_ref[...] = cs
        sin_ref[...] = sn
        for a in range(AW // LB):
            q_ref[:, a * LB:(a + 1) * LB] = _rope(proj[:, O_Q + a * LB:O_Q + (a + 1) * LB], cs, sn, False).astype(BF16)
        k_ref[...] = _rope(proj[:, O_K:O_V], cs, sn, False).astype(BF16)
        v_ref[...] = proj[:, O_V:O_Z].astype(BF16)
        z_ref[...] = proj[:, O_Z:O_XBC]
        xbc_ref[...] = proj[:, O_XBC:O_DT]
        dtr_ref[...] = proj[:, O_DT:INP]

    return _pcall(
        body, name="inproj_fwd", grid=(s // tt,),
        in_specs=[_rows(tt, D), _rows(tt, 1), _const((1, LB)), _const((8, D)), _const((1, D)), _const((INP, D))],
        out_specs=[_rows(tt, AW), _rows(tt, KVW), _rows(tt, KVW), _rows(tt, SW), _rows(tt, D), _rows(tt, LB),
                   _rows(tt, D), _rows(tt, LB), _rows(tt, LB)],
        out_shape=[_sds((s, AW), BF16), _sds((s, KVW), BF16), _sds((s, KVW), BF16), _sds((s, SW), F32),
                   _sds((s, D), F32), _sds((s, LB), F32), _sds((s, D), BF16), _sds((s, LB), F32), _sds((s, LB), F32)],
        compiler_params=_params(),
    )(x, pos, invf, mod8, n1w, w_in)


QPG = 4
ATT_SCALE = 1.0 / math.sqrt(HD)


def _stack_heads(val, g):
    return jnp.concatenate([val[:, (QPG * g + hh) * HD:(QPG * g + hh + 1) * HD] for hh in range(QPG)], axis=0)


def _unstack_heads(groups):
    return jnp.concatenate([grp[hh * LB:(hh + 1) * LB, :] for grp in groups for hh in range(QPG)], axis=1)


def _upper_mask():
    row = lax.broadcasted_iota(jnp.int32, (QPG * LB, LB), 0)
    col = lax.broadcasted_iota(jnp.int32, (QPG * LB, LB), 1)
    return col > (row % LB)


def _sink_col(sinks, g):
    return jnp.concatenate([jnp.broadcast_to(sinks[QPG * g + hh:QPG * g + hh + 1, 0:1], (LB, 1))
                            for hh in range(QPG)], axis=0)


def _attn_probs(n, qg, kp, kc, sink, upper):
    sp = _dot_nt(qg, kp) * ATT_SCALE
    sc = _dot_nt(qg, kc) * ATT_SCALE
    comb = jnp.where(upper, jnp.where(n > 0, sp, NEG), sc)
    m = jnp.maximum(jnp.max(comb, axis=-1, keepdims=True), sink)
    p = jnp.exp(comb - m)
    es = jnp.exp(sink - m)
    denom = jnp.sum(p, axis=-1, keepdims=True) + es
    return p / denom, es / denom


def _attn_fwd_block(n, q_ref, kp_ref, kc_ref, vp_ref, vc_ref, sink_ref, o_ref):
    qv, kp, kc, vp, vc = q_ref[...], kp_ref[...], kc_ref[...], vp_ref[...], vc_ref[...]
    sinks = sink_ref[...]
    upper = _upper_mask()
    outs = []
    for g in range(NQ // QPG):
        sl = slice(g * HD, (g + 1) * HD)
        probs, _ = _attn_probs(n, _stack_heads(qv, g), kp[:, sl], kc[:, sl], _sink_col(sinks, g), upper)
        outs.append(_dot(jnp.where(upper, probs, 0.0).astype(BF16), vp[:, sl])
                    + _dot(jnp.where(upper, 0.0, probs).astype(BF16), vc[:, sl]))
    o_ref[...] = _unstack_heads(outs).astype(BF16)


def _cumsum_rows(a, reverse):
    row = lax.broadcasted_iota(jnp.int32, a.shape, 0)
    step = 1
    while step < LB:
        if reverse:
            a = a + jnp.where(row < LB - step, pltpu.roll(a, LB - step, 0), 0.0)
        else:
            a = a + jnp.where(row >= step, pltpu.roll(a, step, 0), 0.0)
        step *= 2
    return a


SUB = 8


def _conv_shifts(tail, cur):
    row = lax.broadcasted_iota(jnp.int32, tail.shape, 0)
    out = [cur]
    for j in range(1, CONVK):
        rolled = pltpu.roll(cur, j, 0)
        top = jnp.where(row < j, pltpu.roll(tail, j, 0), rolled[0:SUB, :])
        out.append(jnp.concatenate([top, rolled[SUB:, :]], axis=0))
    return out


def _conv_advances(du, head):
    row = lax.broadcasted_iota(jnp.int32, head.shape, 0)
    out = []
    for j in range(1, CONVK):
        rolled = pltpu.roll(du, LB - j, 0)
        bottom = jnp.where(row >= SUB - j, pltpu.roll(head, SUB - j, 0), rolled[LB - SUB:, :])
        out.append(jnp.concatenate([rolled[:LB - SUB, :], bottom], axis=0))
    return out


def _split3(v):
    hi = v.astype(BF16)
    r1 = v - hi.astype(F32)
    mid = r1.astype(BF16)
    return hi, mid, (r1 - mid.astype(F32)).astype(BF16)


def _dot_exact(v, sel):
    hi, mid, lo = _split3(v)
    return _dot(hi, sel) + _dot(mid, sel) + _dot(lo, sel)


def _dot_nt_exact(v, sel):
    hi, mid, lo = _split3(v)
    return _dot_nt(hi, sel) + _dot_nt(mid, sel) + _dot_nt(lo, sel)


def _ssd_pre(xt_ref, xc_ref, cw_ref, cb_ref, dtr_ref, sp_ref, n):
    cur = xc_ref[...]
    tail = jnp.where(n > 0, xt_ref[...], 0.0)
    sh = _conv_shifts(tail, cur)
    u = cb_ref[...] + cw_ref[CONVK - 1:CONVK, :] * sh[0]
    for j in range(1, CONVK):
        u = u + cw_ref[CONVK - 1 - j:CONVK - j, :] * sh[j]
    sg_u = _sigmoid(u)
    xc = u * sg_u
    pre = dtr_ref[...] + sp_ref[0:1, :]
    dt = _softplus(pre)
    a_neg = -jnp.exp(sp_ref[1:2, :])
    acs = _cumsum_rows(dt * a_neg, False)
    return sh, u, sg_u, xc, pre, dt, a_neg, acs


def _gated_norm_fwd(y, z, nw):
    sz = z * _sigmoid(z)
    yz = y * sz
    parts = []
    for g in range(2):
        t = yz[:, g * 256:(g + 1) * 256]
        parts.append(t * lax.rsqrt(jnp.mean(t * t, axis=-1, keepdims=True) + EPS))
    return jnp.concatenate(parts, axis=1) * nw


HPG = 4
GW = HPG * HD


class _SsdChunk:
    def __init__(self, xc, dt, acs, spv, e64, e128):
        self.e64, self.e128 = e64, e128
        self.acs_t = jnp.transpose(acs)
        alast = acs[LB - 1:LB, :]
        self.e_all = jnp.exp(acs)
        self.dte_all = jnp.exp(alast - acs)
        self.elast = jnp.exp(alast)
        wide = _dot_exact(jnp.concatenate([dt, self.e_all, self.dte_all], axis=0), e64)
        self.dt_x, self.e_x, self.dte_x = wide[0:LB], wide[LB:2 * LB], wide[2 * LB:3 * LB]
        self.dsk_x = _dot_exact(spv, e64)[2:3, :]
        ac_x = _dot_exact(acs, e128)
        row = lax.broadcasted_iota(jnp.int32, (HPG * LB, LB), 0)
        col = lax.broadcasted_iota(jnp.int32, (HPG * LB, LB), 1)
        causal = (row % LB) >= col
        lane = lax.broadcasted_iota(jnp.int32, (LB, GW), 1)
        self.head_lanes = [(lane >= hh * HD) & (lane < (hh + 1) * HD) for hh in range(HPG)]
        self.xs, self.xdt, self.b, self.c, self.bb, self.cb16, self.cbm, self.dm_st, self.m_st = ([] for _ in range(9))
        for g in range(2):
            heads = range(HPG * g, HPG * (g + 1))
            ac_st = jnp.concatenate([ac_x[:, j * LB:(j + 1) * LB] for j in heads], axis=0)
            ar_st = jnp.concatenate([jnp.broadcast_to(self.acs_t[j:j + 1, :], (LB, LB)) for j in heads], axis=0)
            dm_st = jnp.exp(jnp.where(causal, ac_st - ar_st, NEG))
            bg = xc[:, SW + g * NST:SW + (g + 1) * NST]
            cg = xc[:, SW + 2 * NST + g * NST:SW + 2 * NST + (g + 1) * NST]
            bgb, cgb = bg.astype(BF16), cg.astype(BF16)
            cbm = _dot_nt(cgb, bgb)
            xs_g = xc[:, g * GW:(g + 1) * GW]
            self.xs.append(xs_g)
            self.xdt.append(xs_g * self.dt_x[:, g * GW:(g + 1) * GW])
            self.b.append(bg)
            self.c.append(cg)
            self.bb.append(bgb)
            self.cb16.append(cgb)
            self.cbm.append(cbm)
            self.dm_st.append(dm_st)
            self.m_st.append(jnp.concatenate([cbm] * HPG, axis=0) * dm_st)

    def elast_rows(self, g):
        return jnp.concatenate([jnp.broadcast_to(self.elast[:, j:j + 1], (HD, NST))
                                for j in range(HPG * g, HPG * (g + 1))], axis=0)

    def diag_blocks(self, stacked):
        out = stacked[(HPG - 1) * LB:HPG * LB, :]
        for hh in range(HPG - 2, -1, -1):
            out = jnp.where(self.head_lanes[hh], stacked[hh * LB:(hh + 1) * LB, :], out)
        return out

    def block_diag(self, v):
        return jnp.concatenate([jnp.where(self.head_lanes[hh], v, 0.0) for hh in range(HPG)], axis=0)


def _ssd_fwd_block(n, xt_ref, xc_ref, cw_ref, cb_ref, dtr_ref, sp_ref, z_ref, nw_ref, e64_ref, e128_ref,
                   yn_ref, y_ref, hs_ref, h_scr):
    @pl.when(n == 0)
    def _():
        h_scr[...] = jnp.zeros_like(h_scr)

    h_all = h_scr[...]
    hs_ref[0] = h_all
    _, _, _, xc, _, dt, _, acs = _ssd_pre(xt_ref, xc_ref, cw_ref, cb_ref, dtr_ref, sp_ref, n)
    ck = _SsdChunk(xc, dt, acs, sp_ref[...], e64_ref[...], e128_ref[...])
    ys, hn = [], []
    for g in range(2):
        gl = slice(g * GW, (g + 1) * GW)
        xdt = ck.xdt[g]
        hg = h_all[gl, :]
        y_diag = ck.diag_blocks(_dot(ck.m_st[g].astype(BF16), xdt.astype(BF16)))
        y_off = ck.e_x[:, gl] * _dot_nt(ck.cb16[g], hg.astype(BF16))
        ys.append(y_diag + y_off + ck.xs[g] * ck.dsk_x[:, gl])
        hn.append(hg * ck.elast_rows(g) + _dot_tn((xdt * ck.dte_x[:, gl]).astype(BF16), ck.bb[g]))
    h_scr[...] = jnp.concatenate(hn, axis=0)
    y = jnp.concatenate(ys, axis=1)
    y_ref[...] = y
    yn_ref[...] = _gated_norm_fwd(y, z_ref[...], nw_ref[...]).astype(BF16)


def _mixer_fwd(q, k, v, sinks8, xbc, conv_w8, conv_b, dtr, ssm_p, z, nw, gathers):
    s = q.shape[0]
    nb = s // LB
    prev = lambda n: (jnp.maximum(n - 1, 0), 0)
    cur = lambda n: (n, 0)
    items, ex_shapes, n_g = _exchange_items(gathers, [])
    ne = len(items)

    n_in = 16
    e64, e128 = _head_expanders()

    def body(*refs):
        a_in, s_in, ex_in = refs[:6], refs[6:n_in], refs[n_in:n_in + ne]
        o_ref, yn_ref, y_ref, hs_ref = refs[n_in + ne:n_in + 4 + ne]
        ex_out = refs[n_in + 4 + ne:n_in + 4 + 2 * ne]
        h_scr = refs[n_in + 4 + 2 * ne]
        sems = refs[n_in + 5 + 2 * ne:]
        n = pl.program_id(0)

        @pl.when(n == 0)
        def _():
            _Exchange(n_g, ex_in, ex_out, sems).start()

        _attn_fwd_block(n, *a_in, o_ref)
        _ssd_fwd_block(n, *s_in, yn_ref, y_ref, hs_ref, h_scr)

        @pl.when(n == nb - 1)
        def _():
            _Exchange(n_g, ex_in, ex_out, sems).finish()

    any_spec = pl.BlockSpec(memory_space=pl.ANY)
    tail = pl.BlockSpec((SUB, D), lambda n: (jnp.maximum(n * (LB // SUB) - 1, 0), 0))
    outs = _pcall(
        body, name="mixer_fwd", grid=(nb,),
        in_specs=[pl.BlockSpec((LB, AW), cur), pl.BlockSpec((LB, KVW), prev), pl.BlockSpec((LB, KVW), cur),
                  pl.BlockSpec((LB, KVW), prev), pl.BlockSpec((LB, KVW), cur), _const((8, LB)),
                  tail, pl.BlockSpec((LB, D), cur), _const((8, D)), _const((1, D)),
                  pl.BlockSpec((LB, LB), cur), _const((8, LB)), pl.BlockSpec((LB, SW), cur), _const((1, SW)),
                  _const(e64.shape), _const(e128.shape)]
        + [any_spec] * ne,
        out_specs=[pl.BlockSpec((LB, AW), cur), pl.BlockSpec((LB, SW), cur), pl.BlockSpec((LB, SW), cur),
                   pl.BlockSpec((1, NH * HD, NST), lambda n: (n, 0, 0))] + [any_spec] * ne,
        out_shape=[_sds((s, AW), BF16), _sds((s, SW), BF16), _sds((s, SW), F32), _sds((nb, NH * HD, NST), F32)]
        + ex_shapes,
        scratch_shapes=[pltpu.VMEM((NH * HD, NST), F32)] + _exchange_sems(ne),
        compiler_params=_params(),
    )(q, k, k, v, v, sinks8, xbc, xbc, conv_w8, conv_b, dtr, ssm_p, z, nw, e64, e128, *items)
    return outs[0], outs[1], outs[2], outs[3], outs[4:]


def _head_expanders():
    j = lax.broadcasted_iota(jnp.int32, (LB, NH * HD), 0)
    e64 = (lax.broadcasted_iota(jnp.int32, (LB, NH * HD), 1) // HD == j).astype(BF16)
    j = lax.broadcasted_iota(jnp.int32, (LB, NH * LB), 0)
    e128 = (lax.broadcasted_iota(jnp.int32, (LB, NH * LB), 1) // LB == j).astype(BF16)
    return e64, e128


def _outproj_fwd(attn, yn, x, mod8, n2w, w_out):
    s = x.shape[0]
    tt = min(512, s)

    def body(a_ref, y_ref, x_ref, mod_ref, nw_ref, w_ref, x2_ref, h2_ref, mo_ref):
        mix = jnp.concatenate([a_ref[...], y_ref[...]], axis=1)
        mo = _dot(mix, w_ref[...])
        mo_ref[...] = mo.astype(BF16)
        x2 = x_ref[...] + mod_ref[2:3, :] * mo
        x2_ref[...] = x2
        h2_ref[...] = _norm_mod_fwd(x2, nw_ref[...], mod_ref[3:4, :], mod_ref[4:5, :]).astype(BF16)

    return _pcall(
        body, name="outproj_fwd", grid=(s // tt,),
        in_specs=[_rows(tt, AW), _rows(tt, SW), _rows(tt, D), _const((8, D)), _const((1, D)), _const((D, D))],
        out_specs=[_rows(tt, D), _rows(tt, D), _rows(tt, D)],
        out_shape=[_sds((s, D), F32), _sds((s, D), BF16), _sds((s, D), BF16)],
        compiler_params=_params(),
    )(attn, yn, x, mod8, n2w, w_out)


def _ffn_fwd_loss(h2, x2, tgt, mod8, fnw, w_gu, w_down):
    s = x2.shape[0]
    tf = min(256, s)

    def body(h_ref, x2_ref, t_ref, mod_ref, fw_ref, wgu_ref, wd_ref, gu_ref, act_ref, dx3_ref, sm_ref):
        i = pl.program_id(0)

        @pl.when(i == 0)
        def _():
            sm_ref[...] = jnp.zeros_like(sm_ref)

        gu = _dot_nt(h_ref[...], wgu_ref[...])
        gu_ref[...] = gu.astype(BF16)
        g = gu[:, :DFF]
        act = (g * _sigmoid(g) * gu[:, DFF:]).astype(BF16)
        act_ref[...] = act
        ff = _dot(act, wd_ref[...])
        x3 = x2_ref[...] + mod_ref[5:6, :] * ff
        r = lax.rsqrt(jnp.mean(x3 * x3, axis=-1, keepdims=True) + EPS)
        xh = x3 * r
        fw = fw_ref[...]
        err = xh * fw - t_ref[...]
        dy = err * (1.0 / D)
        dxh = dy * fw
        dx3 = r * (dxh - xh * jnp.mean(dxh * xh, axis=-1, keepdims=True))
        dx3_ref[...] = dx3
        sm_ref[0:1, :] += jnp.sum(dx3 * ff, axis=0, keepdims=True)
        sm_ref[1:2, :] += jnp.sum(dy * xh, axis=0, keepdims=True)
        sm_ref[2:3, :] += jnp.sum(err * err, axis=0, keepdims=True)

    return _pcall(
        body, name="ffn_fwd_loss", grid=(s // tf,),
        in_specs=[_rows(tf, D), _rows(tf, D), _rows(tf, D), _const((8, D)), _const((1, D)),
                  _const((2 * DFF, D)), _const((DFF, D))],
        out_specs=[_rows(tf, 2 * DFF), _rows(tf, DFF), _rows(tf, D), pl.BlockSpec((8, D), lambda i: (0, 0))],
        out_shape=[_sds((s, 2 * DFF), BF16), _sds((s, DFF), BF16), _sds((s, D), F32), _sds((8, D), F32)],
        compiler_params=_params(),
    )(h2, x2, tgt, mod8, fnw, w_gu, w_down)


def _ffn_bwd(dx3, gu, x2, mixout, mod8, n2w, w_gu, w_down, w_out):
    s = x2.shape[0]
    tb = min(256, s)

    def body(dx3_ref, gu_ref, x2_ref, mo_ref, mod_ref, nw_ref, wgu_ref, wd_ref, wo_ref,
             dx2_ref, dff_ref, dgu_ref, dmix_ref, dattn_ref, dyn_ref, sm_ref):
        i = pl.program_id(0)

        @pl.when(i == 0)
        def _():
            sm_ref[...] = jnp.zeros_like(sm_ref)

        dx3 = dx3_ref[...]
        dff = (dx3 * mod_ref[5:6, :]).astype(BF16)
        dff_ref[...] = dff
        dact = _dot_nt(dff, wd_ref[...])
        g = gu_ref[:, :DFF].astype(F32)
        u = gu_ref[:, DFF:].astype(F32)
        sg = _sigmoid(g)
        dgu = jnp.concatenate([dact * u * sg * (1.0 + g * (1.0 - sg)), dact * g * sg], axis=1).astype(BF16)
        dgu_ref[...] = dgu
        dh2 = _dot(dgu, wgu_ref[...])
        dxn, d_shift, d_scale, d_w = _norm_mod_bwd(x2_ref[...], dh2, nw_ref[...], mod_ref[4:5, :])
        dx2 = dx3 + dxn
        dx2_ref[...] = dx2
        sm_ref[0:1, :] += d_shift
        sm_ref[1:2, :] += d_scale
        sm_ref[2:3, :] += d_w
        sm_ref[3:4, :] += jnp.sum(dx2 * mo_ref[...].astype(F32), axis=0, keepdims=True)
        dmix = (dx2 * mod_ref[2:3, :]).astype(BF16)
        dmix_ref[...] = dmix
        dmi = _dot_nt(dmix, wo_ref[...])
        dattn_ref[...] = dmi[:, :AW].astype(BF16)
        dyn_ref[...] = dmi[:, AW:]

    return _pcall(
        body, name="ffn_bwd", grid=(s // tb,),
        in_specs=[_rows(tb, D), _rows(tb, 2 * DFF), _rows(tb, D), _rows(tb, D), _const((8, D)), _const((1, D)),
                  _const((2 * DFF, D)), _const((DFF, D)), _const((D, D))],
        out_specs=[_rows(tb, D), _rows(tb, D), _rows(tb, 2 * DFF), _rows(tb, D), _rows(tb, AW), _rows(tb, SW),
                   pl.BlockSpec((8, D), lambda i: (0, 0))],
        out_shape=[_sds((s, D), F32), _sds((s, D), BF16), _sds((s, 2 * DFF), BF16), _sds((s, D), BF16),
                   _sds((s, AW), BF16), _sds((s, SW), F32), _sds((8, D), F32)],
        compiler_params=_params(),
    )(dx3, gu, x2, mixout, mod8, n2w, w_gu, w_down, w_out)


def _ssd_bwd_block(i, n, *refs):
    def run(dyn_ref, y_ref, z_ref, xt_ref, xc_ref, cw_ref, cb_ref, dtr_ref, sp_ref, nw_ref, hs_ref, e64_ref, e128_ref,
            dzxd_ref, sm_ref, dh_scr, dun_scr):
        @pl.when(i == 0)
        def _():
            dh_scr[...] = jnp.zeros_like(dh_scr)
            dun_scr[...] = jnp.zeros_like(dun_scr)
            sm_ref[...] = jnp.zeros_like(sm_ref)

        sh, u, sg_u, xc, pre, dt, a_neg, acs = _ssd_pre(xt_ref, xc_ref, cw_ref, cb_ref, dtr_ref, sp_ref, n)
        ck = _SsdChunk(xc, dt, acs, sp_ref[...], e64_ref[...], e128_ref[...])
        h_all = hs_ref[0]
        dh_all = dh_scr[...]
        riota = lax.broadcasted_iota(jnp.int32, (LB, LB), 0)
        lane1 = lax.broadcasted_iota(jnp.int32, (1, LB), 1)

        z = z_ref[...]
        y = y_ref[...]
        sgz = _sigmoid(z)
        sz = z * sgz
        yz = y * sz
        nwv = nw_ref[...]
        dyn_v = dyn_ref[...]
        dyhat = dyn_v * nwv
        yhat_parts, dyz_parts = [], []
        for g in range(2):
            gs = slice(g * 256, (g + 1) * 256)
            t = yz[:, gs]
            rg = lax.rsqrt(jnp.mean(t * t, axis=-1, keepdims=True) + EPS)
            yh = t * rg
            dyh = dyhat[:, gs]
            yhat_parts.append(yh)
            dyz_parts.append(rg * (dyh - yh * jnp.mean(dyh * yh, axis=-1, keepdims=True)))
        yhat = jnp.concatenate(yhat_parts, axis=1)
        dyz = jnp.concatenate(dyz_parts, axis=1)
        sm_ref[5:6, 0:SW] += jnp.sum(dyn_v * yhat, axis=0, keepdims=True)
        dy = dyz * sz
        dzxd_ref[:, 0:SW] = (dyz * y * sgz * (1.0 + z * (1.0 - sgz))).astype(BF16)

        cat = lambda parts: jnp.concatenate(parts, axis=1)
        dxs, dbs, dcs, dhp, g_cat, de_x, ddte_x, ddt_x, ddsk_x = ([] for _ in range(9))
        dacs_t = jnp.zeros((LB, LB), F32)
        hsum = jnp.zeros((1, LB), F32)
        for g in range(2):
            gl = slice(g * GW, (g + 1) * GW)
            xs_g, xdt, bgb, cgb = ck.xs[g], ck.xdt[g], ck.bb[g], ck.cb16[g]
            m_st, dm_st = ck.m_st[g], ck.dm_st[g]
            dt_x, e_x, dte_x = ck.dt_x[:, gl], ck.e_x[:, gl], ck.dte_x[:, gl]
            xdtb = xdt.astype(BF16)
            hg, dhn = h_all[gl, :], dh_all[gl, :]
            hb, dhnb = hg.astype(BF16), dhn.astype(BF16)
            dy_g = dy[:, gl]
            ddsk_x.append(jnp.sum(dy_g * xs_g, axis=0, keepdims=True))
            dy_bd = ck.block_diag(dy_g).astype(BF16)
            dm4 = _dot_nt(dy_bd, xdtb)
            dxdt = _dot_tn(m_st.astype(BF16), dy_bd)
            gmat = dm4 * m_st
            dcbm = dm4 * dm_st
            dcb = dcbm[0:LB] + dcbm[LB:2 * LB] + dcbm[2 * LB:3 * LB] + dcbm[3 * LB:4 * LB]
            g_cat.append(cat([gmat[hh * LB:(hh + 1) * LB, :] for hh in range(HPG)]))
            for hh in range(HPG):
                j = HPG * g + hh
                col_sum = jnp.sum(gmat[hh * LB:(hh + 1) * LB, :], axis=0, keepdims=True)
                dacs_t = dacs_t - jnp.where(riota == j, col_sum, 0.0)
                hsl = slice(hh * HD, (hh + 1) * HD)
                hsum = hsum + jnp.where(lane1 == j, jnp.sum(dhn[hsl, :] * hg[hsl, :]), 0.0)
            dchb = (dy_g * e_x).astype(BF16)
            dcg = _dot(dchb, hb)
            dh_prev = _dot_tn(dchb, cgb)
            de_x.append(dy_g * _dot_nt(cgb, hb))
            dxs_s = _dot_nt(bgb, dhnb)
            dbg = _dot((xdt * dte_x).astype(BF16), dhnb)
            dxdt = dxdt + dxs_s * dte_x
            ddte_x.append(dxs_s * xdt)
            dhp.append(dhn * ck.elast_rows(g) + dh_prev)
            dxs.append(dy_g * ck.dsk_x[:, gl] + dxdt * dt_x)
            ddt_x.append(dxdt * xs_g)
            dcbb = dcb.astype(BF16)
            dbs.append(dbg + _dot_tn(dcbb, cgb))
            dcs.append(dcg + _dot(dcbb, bgb))
        dh_scr[...] = jnp.concatenate(dhp, axis=0)
        red = _dot_nt_exact(jnp.concatenate([cat(de_x), cat(ddte_x), cat(ddt_x)], axis=0), ck.e64)
        de_c, ddte_c, ddt_c = red[0:LB], red[LB:2 * LB], red[2 * LB:3 * LB]
        ddsk = _dot_nt_exact(jnp.broadcast_to(cat(ddsk_x), (SUB, NH * HD)), ck.e64)[0:1, :]
        t1 = ddte_c * ck.dte_all
        dalast = jnp.sum(t1, axis=0, keepdims=True) + hsum * ck.elast
        dacs = (_dot_nt_exact(cat(g_cat), ck.e128) + de_c * ck.e_all - t1 + jnp.transpose(dacs_t)
                + jnp.where(riota == LB - 1, dalast, 0.0))
        da = _cumsum_rows(dacs, True)
        ddt = ddt_c + da * a_neg
        da_log = jnp.sum(da * dt, axis=0, keepdims=True) * a_neg
        ddtr = ddt * _sigmoid(pre)
        dzxd_ref[:, SW + D:ZXD] = ddtr.astype(BF16)
        sm_ref[6:7, 0:LB] += jnp.sum(ddtr, axis=0, keepdims=True)
        sm_ref[6:7, LB:2 * LB] += da_log
        sm_ref[6:7, 2 * LB:3 * LB] += ddsk

        du = cat(dxs + dbs + dcs) * (sg_u * (1.0 + u * (1.0 - sg_u)))
        sm_ref[0:1, :] += jnp.sum(du, axis=0, keepdims=True)
        for k in range(CONVK):
            sm_ref[1 + k:2 + k, :] += jnp.sum(du * sh[CONVK - 1 - k], axis=0, keepdims=True)
        adv = _conv_advances(du, dun_scr[...])
        dxbc = cw_ref[CONVK - 1:CONVK, :] * du
        for j in range(1, CONVK):
            dxbc = dxbc + cw_ref[CONVK - 1 - j:CONVK - j, :] * adv[j - 1]
        dun_scr[...] = du[0:SUB, :]
        dzxd_ref[:, SW:SW + D] = dxbc.astype(BF16)

    run(*refs)


def _attn_bwd_block(i, n, q_ref, kp_ref, kc_ref, vp_ref, vc_ref, o_ref, do_ref, cos_ref, sin_ref, sink_ref,
                    dq_ref, dkv_ref, ds_ref, ck_scr, cv_scr):
    @pl.when(i == 0)
    def _():
        ds_ref[...] = jnp.zeros_like(ds_ref)
        ck_scr[...] = jnp.zeros_like(ck_scr)
        cv_scr[...] = jnp.zeros_like(cv_scr)

    qv, kp, kc, vp, vc = q_ref[...], kp_ref[...], kc_ref[...], vp_ref[...], vc_ref[...]
    ov, dov, sinks = o_ref[...], do_ref[...], sink_ref[...]
    upper = _upper_mask()
    srow = lax.broadcasted_iota(jnp.int32, (8, LB), 0)
    slane = lax.broadcasted_iota(jnp.int32, (8, LB), 1)
    dsink = jnp.zeros((8, LB), F32)
    dq_g, dk_prev, dk_cur, dv_prev, dv_cur = [], [], [], [], []
    for g in range(NQ // QPG):
        sl = slice(g * HD, (g + 1) * HD)
        qg = _stack_heads(qv, g)
        dog = _stack_heads(dov, g)
        probs, psink = _attn_probs(n, qg, kp[:, sl], kc[:, sl], _sink_col(sinks, g), upper)
        delta = jnp.sum(dog.astype(F32) * _stack_heads(ov, g).astype(F32), axis=1, keepdims=True)
        dsc = probs * (jnp.where(upper, _dot_nt(dog, vp[:, sl]), _dot_nt(dog, vc[:, sl])) - delta)
        sink_terms = psink * delta
        for hh in range(QPG):
            dsink = dsink - jnp.where((srow == QPG * g + hh) & (slane == 0),
                                      jnp.sum(sink_terms[hh * LB:(hh + 1) * LB, :]), 0.0)
        ds_p = jnp.where(upper, dsc, 0.0).astype(BF16)
        ds_c = jnp.where(upper, 0.0, dsc).astype(BF16)
        dq_g.append((_dot(ds_p, kp[:, sl]) + _dot(ds_c, kc[:, sl])) * ATT_SCALE)
        dk_prev.append(_dot_tn(ds_p, qg) * ATT_SCALE)
        dk_cur.append(_dot_tn(ds_c, qg) * ATT_SCALE)
        dv_prev.append(_dot_tn(jnp.where(upper, probs, 0.0).astype(BF16), dog))
        dv_cur.append(_dot_tn(jnp.where(upper, 0.0, probs).astype(BF16), dog))
    ds_ref[...] += dsink
    cs = cos_ref[...]
    sn = sin_ref[...]
    dq = _unstack_heads(dq_g)
    for a in range(AW // LB):
        dq_ref[:, a * LB:(a + 1) * LB] = _rope(dq[:, a * LB:(a + 1) * LB], cs, sn, True).astype(BF16)
    dkv_ref[:, 0:KVW] = _rope(ck_scr[...] + jnp.concatenate(dk_cur, axis=1), cs, sn, True).astype(BF16)
    dkv_ref[:, KVW:2 * KVW] = (cv_scr[...] + jnp.concatenate(dv_cur, axis=1)).astype(BF16)
    ck_scr[...] = jnp.concatenate(dk_prev, axis=1)
    cv_scr[...] = jnp.concatenate(dv_prev, axis=1)


def _mixer_bwd(q, k, v, o, do, cos, sin, sinks8, dyn, y, z, xbc, conv_w8, conv_b, dtr, ssm_p, nw, hs, scatters):
    s = q.shape[0]
    nb = s // LB
    cur = lambda i: (nb - 1 - i, 0)
    prev = lambda i: (jnp.maximum(nb - 2 - i, 0), 0)
    n_in = 23
    items, ex_shapes, n_g = _exchange_items([], scatters)
    ne = len(items)
    e64, e128 = _head_expanders()

    def body(*refs):
        i = pl.program_id(0)
        n = nb - 1 - i
        a_in, s_in, ex_in = refs[:10], refs[10:n_in], refs[n_in:n_in + ne]
        dp_ref, ds_ref, sm_ref = refs[n_in + ne:n_in + ne + 3]
        dq_ref, dkv_ref, dzxd_ref = dp_ref.at[:, O_Q:O_K], dp_ref.at[:, O_K:O_Z], dp_ref.at[:, O_Z:INP]
        ex_out = refs[n_in + ne + 3:n_in + 2 * ne + 3]
        ck_scr, cv_scr, dh_scr, dun_scr = refs[n_in + 2 * ne + 3:n_in + 2 * ne + 7]
        sems = refs[n_in + 2 * ne + 7:]

        @pl.when(i == 0)
        def _():
            _Exchange(n_g, ex_in, ex_out, sems).start()

        _attn_bwd_block(i, n, *a_in, dq_ref, dkv_ref, ds_ref, ck_scr, cv_scr)
        _ssd_bwd_block(i, n, *s_in, dzxd_ref, sm_ref, dh_scr, dun_scr)

        @pl.when(i == nb - 1)
        def _():
            _Exchange(n_g, ex_in, ex_out, sems).finish()

    any_spec = pl.BlockSpec(memory_space=pl.ANY)
    tail = pl.BlockSpec((SUB, D), lambda i: (jnp.maximum((nb - 1 - i) * (LB // SUB) - 1, 0), 0))
    outs = _pcall(
        body, name="mixer_bwd", grid=(nb,),
        in_specs=[pl.BlockSpec((LB, AW), cur), pl.BlockSpec((LB, KVW), prev), pl.BlockSpec((LB, KVW), cur),
                  pl.BlockSpec((LB, KVW), prev), pl.BlockSpec((LB, KVW), cur), pl.BlockSpec((LB, AW), cur),
                  pl.BlockSpec((LB, AW), cur), pl.BlockSpec((LB, LB), cur), pl.BlockSpec((LB, LB), cur),
                  _const((8, LB)),
                  pl.BlockSpec((LB, SW), cur), pl.BlockSpec((LB, SW), cur), pl.BlockSpec((LB, SW), cur),
                  tail, pl.BlockSpec((LB, D), cur), _const((8, D)), _const((1, D)),
                  pl.BlockSpec((LB, LB), cur), _const((8, LB)), _const((1, SW)),
                  pl.BlockSpec((1, NH * HD, NST), lambda i: (nb - 1 - i, 0, 0)),
                  _const(e64.shape), _const(e128.shape)] + [any_spec] * ne,
        out_specs=[pl.BlockSpec((LB, INP), cur), pl.BlockSpec((8, LB), lambda i: (0, 0)),
                   pl.BlockSpec((8, D), lambda i: (0, 0))] + [any_spec] * ne,
        out_shape=[_sds((s, INP), BF16), _sds((8, LB), F32), _sds((8, D), F32)] + ex_shapes,
        scratch_shapes=[pltpu.VMEM((LB, KVW), F32), pltpu.VMEM((LB, KVW), F32),
                        pltpu.VMEM((NH * HD, NST), F32), pltpu.VMEM((SUB, D), F32)]
        + _exchange_sems(ne),
        compiler_params=_params(),
    )(q, k, k, v, v, o, do, cos, sin, sinks8, dyn, y, z, xbc, xbc, conv_w8, conv_b, dtr, ssm_p, nw, hs, e64, e128,
      *items)
    return outs[0], outs[1], outs[2], outs[3:]


def _inproj_bwd(dproj, x, dx2, mod8, n1w, w_in_t, scatters):
    s = x.shape[0]
    tt = min(512, s)
    nt = s // tt
    items, ex_shapes, n_g = _exchange_items([], scatters)
    ne = len(items)

    def body(*refs):
        dp_ref, x_ref, dx2_ref, mod_ref, nw_ref, w_ref = refs[:6]
        ex_in = refs[6:6 + ne]
        gx_ref, sm_ref = refs[6 + ne:8 + ne]
        ex_out = refs[8 + ne:8 + 2 * ne]
        sems = refs[8 + 2 * ne:]
        i = pl.program_id(0)

        @pl.when(i == 0)
        def _():
            sm_ref[...] = jnp.zeros_like(sm_ref)
            _Exchange(n_g, ex_in, ex_out, sems).start()

        dh1 = _dot(dp_ref[...], w_ref[...])
        dxn, d_shift, d_scale, d_w = _norm_mod_bwd(x_ref[...], dh1, nw_ref[...], mod_ref[1:2, :])
        gx_ref[...] = dx2_ref[...] + dxn
        sm_ref[0:1, :] += d_shift
        sm_ref[1:2, :] += d_scale
        sm_ref[2:3, :] += d_w

        @pl.when(i == nt - 1)
        def _():
            _Exchange(n_g, ex_in, ex_out, sems).finish()

    any_spec = pl.BlockSpec(memory_space=pl.ANY)
    outs = _pcall(
        body, name="inproj_bwd", grid=(nt,),
        in_specs=[_rows(tt, INP), _rows(tt, D), _rows(tt, D), _const((8, D)), _const((1, D)), _const((INP, D))]
        + [any_spec] * ne,
        out_specs=[_rows(tt, D), pl.BlockSpec((8, D), lambda i: (0, 0))] + [any_spec] * ne,
        out_shape=[_sds((s, D), F32), _sds((8, D), F32)] + ex_shapes,
        scratch_shapes=_exchange_sems(ne),
        compiler_params=_params(),
    )(dproj, x, dx2, mod8, n1w, w_in_t, *items)
    return outs[0], outs[1], outs[2:]


def _wgrad(a, b, name):
    s, m = a.shape
    n = b.shape[1]
    tk = min(1024, s)
    wide = (1408, 1024, 512)
    tm = next((t for t in wide if m % t == 0), m)
    tn = n if n <= 2048 else _largest_divisor(n, wide)
    nk = s // tk

    def body(a_ref, b_ref, o_ref, acc):
        kk = pl.program_id(2)

        @pl.when(kk == 0)
        def _():
            acc[...] = jnp.zeros_like(acc)

        acc[...] += _dot_tn(a_ref[...], b_ref[...])

        @pl.when(kk == nk - 1)
        def _():
            o_ref[...] = acc[...].astype(BF16)

    return _pcall(
        body, name=name, grid=(m // tm, n // tn, nk),
        in_specs=[pl.BlockSpec((tk, tm), lambda i, j, kk: (kk, i)), pl.BlockSpec((tk, tn), lambda i, j, kk: (kk, j))],
        out_specs=pl.BlockSpec((tm, tn), lambda i, j, kk: (i, j)),
        out_shape=_sds((m, n), BF16),
        scratch_shapes=[pltpu.VMEM((tm, tn), F32)],
        compiler_params=_params(3),
    )(a, b)


PACK_ROWS = 24


def _pack_small(sm_f, sm_b, sm_s, sm_i, dsink):
    def body(f_ref, b_ref, s_ref, i_ref, k_ref, o_ref):
        o_ref[...] = jnp.zeros_like(o_ref)
        o_ref[0:2, :] = i_ref[0:2, :]
        o_ref[2:3, :] = b_ref[3:4, :]
        o_ref[3:5, :] = b_ref[0:2, :]
        o_ref[5:6, :] = f_ref[0:1, :]
        o_ref[6:7, :] = i_ref[2:3, :]
        o_ref[7:8, :] = b_ref[2:3, :]
        o_ref[8:9, :] = f_ref[1:2, :]
        o_ref[9:14, :] = s_ref[0:5, :]
        o_ref[14:15, :] = s_ref[5:6, :]
        o_ref[15:16, 0:3 * LB] = s_ref[6:7, 0:3 * LB]
        lane = lax.broadcasted_iota(jnp.int32, (1, LB), 1)
        sk = jnp.zeros((1, LB), F32)
        for h in range(NQ):
            sk = sk + jnp.where(lane == h, k_ref[h:h + 1, 0:1], 0.0)
        o_ref[15:16, 3 * LB:4 * LB] = sk
        o_ref[16:17, :] = f_ref[2:3, :]

    return _pcall(body, name="pack_small", out_shape=_sds((PACK_ROWS, D), F32))(sm_f, sm_b, sm_s, sm_i, dsink)


def _exchange(gathers, scatters, name):
    items, shapes, n_g = _exchange_items(gathers, scatters)
    n = len(items)

    def body(*refs):
        ex = _Exchange(n_g, refs[:n], refs[n:2 * n], refs[2 * n:])
        ex.start()
        ex.finish()

    any_spec = pl.BlockSpec(memory_space=pl.ANY)
    return _pcall(
        body, name=name, in_specs=[any_spec] * n, out_specs=[any_spec] * n, out_shape=shapes,
        scratch_shapes=_exchange_sems(n),
    )(*items)


def _exchange_items(gathers, scatters):
    items = list(gathers) + list(scatters)
    shapes = [_sds((N_DEV,) + a.shape, a.dtype) for a in gathers] + [_sds(a.shape, a.dtype) for a in scatters]
    return items, shapes, len(gathers)


def _exchange_sems(n):
    return [pltpu.SemaphoreType.DMA((n, N_DEV - 1)), pltpu.SemaphoreType.DMA((n, N_DEV - 1)),
            pltpu.SemaphoreType.DMA((n,))]


class _Exchange:
    def __init__(self, n_g, ins, outs, sems):
        self.n_g, self.ins, self.outs = n_g, ins, outs
        self.send_sems, self.recv_sems, self.loc_sems = sems
        xi, yi, ci = lax.axis_index("x"), lax.axis_index("y"), lax.axis_index("c")
        self.me = 4 * xi + 2 * yi + ci
        self.peers = []
        for r in range(1, N_DEV):
            px = 1 - xi if r & 4 else xi
            py = 1 - yi if r & 2 else yi
            pc = 1 - ci if r & 1 else ci
            self.peers.append(((px, py, pc), 4 * px + 2 * py + pc))

    def _copy(self, t, r, landing):
        dev, peer = self.peers[r]
        src = self.ins[t] if t < self.n_g else self.ins[t].at[peer]
        return pltpu.make_async_remote_copy(
            src_ref=src, dst_ref=self.outs[t].at[landing], send_sem=self.send_sems.at[t, r],
            recv_sem=self.recv_sems.at[t, r], device_id=dev, device_id_type=pl.DeviceIdType.MESH)

    def _local(self, t):
        src = self.ins[t] if t < self.n_g else self.ins[t].at[self.me]
        return pltpu.make_async_copy(src, self.outs[t].at[self.me], self.loc_sems.at[t])

    def start(self):
        for t in range(len(self.ins)):
            self._local(t).start()
            for r in range(N_DEV - 1):
                self._copy(t, r, self.me).start()

    def finish(self):
        n = len(self.ins)
        for t in range(n):
            for r in range(N_DEV - 1):
                self._copy(t, r, self.peers[r][1]).wait_recv()
        for t in range(n):
            for r in range(N_DEV - 1):
                self._copy(t, r, self.me).wait_send()
            self._local(t).wait()


def _ada_fwd(c_all, w_cols, b_cols):
    def body(c_ref, w_ref, b_ref, o_ref):
        cv = c_ref[...]
        sc = (cv * _sigmoid(cv)).astype(BF16)
        o_ref[...] = _dot(sc, w_ref[...].astype(BF16)) + b_ref[...]

    return _pcall(body, name="ada_fwd", out_shape=_sds((N_DEV, w_cols.shape[1]), F32),
                  compiler_params=_params(0))(c_all, w_cols, b_cols)


def _adamw(w, g, m, v):
    m2 = ADAM_B1 * m + (1.0 - ADAM_B1) * g
    v2 = ADAM_B2 * v + (1.0 - ADAM_B2) * (g * g)
    m_hat = m2 / (1.0 - ADAM_B1 ** ADAM_STEP)
    v_hat = v2 / (1.0 - ADAM_B2 ** ADAM_STEP)
    delta = -ADAM_LR * (m_hat / (jnp.sqrt(v_hat) + ADAM_EPS) + ADAM_WD * w)
    return delta, m2, v2


def _sum_adamw(parts, w, m, v, name):
    rws, cols = w.shape
    tr = next((t for t in (256, 176, 128) if rws % t == 0), rws)

    def body(p_ref, w_ref, m_ref, v_ref, g_ref, d_ref, mo_ref, vo_ref):
        g = p_ref[0].astype(F32)
        for dev in range(1, N_DEV):
            g = g + p_ref[dev].astype(F32)
        g_ref[...] = g
        d_ref[...], mo_ref[...], vo_ref[...] = _adamw(w_ref[...], g, m_ref[...], v_ref[...])

    blk = pl.BlockSpec((tr, cols), lambda i: (i, 0))
    return _pcall(
        body, name=name, grid=(rws // tr,),
        in_specs=[pl.BlockSpec((N_DEV, tr, cols), lambda i: (0, i, 0)), blk, blk, blk],
        out_specs=[blk] * 4, out_shape=[_sds((rws, cols), F32)] * 4, compiler_params=_params(),
    )(parts, w, m, v)


def _wada_adamw(c_all, dmod_cols, w, m, v):
    rws, cols = w.shape
    tr = 256

    def body(c_ref, dm_ref, w_ref, m_ref, v_ref, g_ref, d_ref, mo_ref, vo_ref):
        cv = c_ref[...]
        sc = (cv * _sigmoid(cv)).astype(BF16)
        g = _dot_tn(sc, dm_ref[...].astype(BF16))
        g_ref[...] = g
        d_ref[...], mo_ref[...], vo_ref[...] = _adamw(w_ref[...], g, m_ref[...], v_ref[...])

    blk = pl.BlockSpec((tr, cols), lambda i: (i, 0))
    return _pcall(
        body, name="wada_adamw", grid=(rws // tr,),
        in_specs=[pl.BlockSpec((N_DEV, tr), lambda i: (0, i)), pl.BlockSpec((N_DEV, cols), lambda i: (0, 0)),
                  blk, blk, blk],
        out_specs=[blk] * 4, out_shape=[_sds((rws, cols), F32)] * 4, compiler_params=_params(),
    )(c_all, dmod_cols, w, m, v)


def _small_reduce(packs):
    def body(p_ref, o_ref):
        tot = p_ref[0]
        for dev in range(1, N_DEV):
            tot = tot + p_ref[dev]
        o_ref[...] = tot
        o_ref[16:17, :] = jnp.zeros((1, D), F32) + (0.5 / D) * jnp.sum(tot[16:17, :])

    return _pcall(body, name="small_reduce", out_shape=_sds((PACK_ROWS, D), F32))(packs)


def _adamw_many(ws, gs, ms, vs):
    k = len(ws)

    def body(*refs):
        for i in range(k):
            w_ref, g_ref, m_ref, v_ref = refs[i], refs[k + i], refs[2 * k + i], refs[3 * k + i]
            d_ref, mo_ref, vo_ref = refs[4 * k + i], refs[5 * k + i], refs[6 * k + i]
            d_ref[...], mo_ref[...], vo_ref[...] = _adamw(w_ref[...], g_ref[...], m_ref[...], v_ref[...])

    shp = [_sds(w.shape, F32) for w in ws]
    outs = _pcall(body, name="adamw_small", out_shape=shp * 3)(*ws, *gs, *ms, *vs)
    return outs[:k], outs[k:2 * k], outs[2 * k:]


def kernel(x, c, positions, w_ada, b_ada, norm1_w, w_in, conv_w, conv_b, dt_bias, a_log, d_skip, attn_sinks, ssm_norm_w, w_out, norm2_w, w_gate_up, w_down, final_norm_w, loss_target, m_w_ada, m_b_ada, m_norm1_w, m_w_in, m_conv_w, m_conv_b, m_dt_bias, m_a_log, m_d_skip, m_attn_sinks, m_ssm_norm_w, m_w_out, m_norm2_w, m_w_gate_up, m_w_down, m_final_norm_w, v_w_ada, v_b_ada, v_norm1_w, v_w_in, v_conv_w, v_conv_b, v_dt_bias, v_a_log, v_d_skip, v_attn_sinks, v_ssm_norm_w, v_w_out, v_norm2_w, v_w_gate_up, v_w_down, v_final_norm_w):
    s = x.shape[1]
    me = 4 * lax.axis_index("x") + 2 * lax.axis_index("y") + lax.axis_index("c")
    ada_cols = N_MOD * D // N_DEV

    c8 = jnp.pad(c, ((0, 7), (0, 0)))
    cw8 = jnp.pad(conv_w[0], ((0, 8 - CONVK), (0, 0)))
    w_in_t, m_w_in_t, v_w_in_t = jnp.transpose(w_in[0]), jnp.transpose(m_w_in[0]), jnp.transpose(v_w_in[0])
    w_gu_t, m_w_gu_t, v_w_gu_t = (jnp.transpose(w_gate_up[0]), jnp.transpose(m_w_gate_up[0]),
                                  jnp.transpose(v_w_gate_up[0]))
    g_c, g_in, g_cw = _exchange([c8, w_in_t.astype(BF16), cw8], [], "gather_in")
    c_all = g_c[:, 0, :]
    w_in_f = jnp.pad(g_in.reshape(IN_PROJ, D), ((0, INP - IN_PROJ), (0, 0)))
    conv_w8 = jnp.transpose(g_cw, (1, 0, 2)).reshape(8, D)

    b_cols = lax.dynamic_slice(b_ada, (0, me * ada_cols), (1, ada_cols))
    (g_mod,) = _exchange([_ada_fwd(c_all, w_ada[0], b_cols)], [], "gather_mod")
    mod = lax.dynamic_index_in_dim(g_mod, me, axis=1, keepdims=False).reshape(N_MOD, D)
    mod8 = jnp.pad(mod, ((0, 8 - N_MOD), (0, 0)))

    half = HD // 2
    inv_freq = ROPE_THETA ** (-jnp.arange(half, dtype=F32) / half)
    invf = jnp.tile(inv_freq, LB // half).reshape(1, LB)
    lanes = lambda a: jnp.pad(a, ((0, 0), (0, LB - a.shape[1])))
    ssm_p = jnp.pad(jnp.concatenate([lanes(dt_bias), lanes(a_log), lanes(d_skip)], axis=0), ((0, 5), (0, 0)))
    sinks8 = jnp.broadcast_to(attn_sinks.reshape(NQ, 1), (NQ, LB))

    xs, tgt, fnw = x[0], loss_target[0], final_norm_w.reshape(1, D)

    q, k, v, z, xbc, dtr, h1, cos, sin = _inproj_fwd(xs, positions[0].reshape(s, 1), invf, mod8, norm1_w, w_in_f)
    attn, yn, y, hs, (g_out, g_gu, g_down) = _mixer_fwd(
        q, k, v, sinks8, xbc, conv_w8, conv_b, dtr, ssm_p, z, ssm_norm_w,
        [w_out[0].astype(BF16), w_gu_t.astype(BF16), w_down[0].astype(BF16)])
    w_out_f = g_out.reshape(D, D)
    w_gu_f = g_gu.reshape(2 * DFF, D)
    w_down_f = g_down.reshape(DFF, D)
    x2, h2, mo = _outproj_fwd(attn, yn, xs, mod8, norm2_w, w_out_f)
    gu, act, dx3, sm_f = _ffn_fwd_loss(h2, x2, tgt, mod8, fnw, w_gu_f, w_down_f)

    dx2, dff, dgu, dmix, dattn, dyn, sm_b = _ffn_bwd(dx3, gu, x2, mo, mod8, norm2_w, w_gu_f, w_down_f, w_out_f)
    p_gu = _wgrad(dgu, h2, "wgrad_gate_up").reshape(N_DEV, 2 * DFF // N_DEV, D)
    p_down = _wgrad(act, dff, "wgrad_down").reshape(N_DEV, DFF // N_DEV, D)
    p_out = jnp.concatenate([_wgrad(attn, dmix, "wgrad_out_attn"), _wgrad(yn, dmix, "wgrad_out_ssd")],
                            axis=0).reshape(N_DEV, D // N_DEV, D)
    dproj, dsink, sm_s, (r_gu, r_down, r_out) = _mixer_bwd(
        q, k, v, attn, dattn, cos, sin, sinks8, dyn, y, z, xbc, conv_w8, conv_b, dtr, ssm_p, ssm_norm_w, hs,
        [p_gu, p_down, p_out])
    p_in = _wgrad(dproj, h1, "wgrad_in")[:IN_PROJ].reshape(N_DEV, IN_PROJ // N_DEV, D)
    gx, sm_i, (r_in,) = _inproj_bwd(dproj, xs, dx2, mod8, norm1_w, w_in_f, [p_in])
    (g_pack,) = _exchange([_pack_small(sm_f, sm_b, sm_s, sm_i, dsink)], [], "gather_small")

    tot = _small_reduce(g_pack)
    loss = tot[16, 0]
    dmod_all = g_pack[:, 0:N_MOD, :].reshape(N_DEV, N_MOD * D)
    dmod_cols = lax.dynamic_slice(dmod_all, (0, me * ada_cols), (N_DEV, ada_cols))

    big = {
        "w_ada": _wada_adamw(c_all, dmod_cols, w_ada[0], m_w_ada[0], v_w_ada[0]),
        "w_in": [jnp.transpose(t) for t in _sum_adamw(r_in, w_in_t, m_w_in_t, v_w_in_t, "adamw_in")],
        "w_out": _sum_adamw(r_out, w_out[0], m_w_out[0], v_w_out[0], "adamw_out"),
        "w_gate_up": [jnp.transpose(t) for t in _sum_adamw(r_gu, w_gu_t, m_w_gu_t, v_w_gu_t, "adamw_gate_up")],
        "w_down": _sum_adamw(r_down, w_down[0], m_w_down[0], v_w_down[0], "adamw_down"),
    }
    small_names = ["b_ada", "norm1_w", "conv_w", "conv_b", "dt_bias", "a_log", "d_skip", "attn_sinks", "ssm_norm_w",
                   "norm2_w", "final_norm_w"]
    row15 = tot[15:16, :]
    small_g = {
        "b_ada": tot[0:N_MOD, :].reshape(1, N_MOD * D),
        "norm1_w": tot[6:7, :],
        "conv_w": lax.dynamic_slice(tot[10:14, :], (0, me * (D // N_DEV)), (CONVK, D // N_DEV)),
        "conv_b": tot[9:10, :],
        "dt_bias": row15[:, 0:NH],
        "a_log": row15[:, LB:LB + NH],
        "d_skip": row15[:, 2 * LB:2 * LB + NH],
        "attn_sinks": row15[:, 3 * LB:3 * LB + NQ],
        "ssm_norm_w": tot[14:15, 0:SW],
        "norm2_w": tot[7:8, :],
        "final_norm_w": tot[8:9, :],
    }
    small_w = {"b_ada": b_ada, "norm1_w": norm1_w, "conv_w": conv_w[0], "conv_b": conv_b, "dt_bias": dt_bias,
               "a_log": a_log, "d_skip": d_skip, "attn_sinks": attn_sinks, "ssm_norm_w": ssm_norm_w,
               "norm2_w": norm2_w, "final_norm_w": final_norm_w.reshape(1, D)}
    small_m = {"b_ada": m_b_ada, "norm1_w": m_norm1_w, "conv_w": m_conv_w[0], "conv_b": m_conv_b,
               "dt_bias": m_dt_bias, "a_log": m_a_log, "d_skip": m_d_skip, "attn_sinks": m_attn_sinks,
               "ssm_norm_w": m_ssm_norm_w, "norm2_w": m_norm2_w, "final_norm_w": m_final_norm_w.reshape(1, D)}
    small_v = {"b_ada": v_b_ada, "norm1_w": v_norm1_w, "conv_w": v_conv_w[0], "conv_b": v_conv_b,
               "dt_bias": v_dt_bias, "a_log": v_a_log, "d_skip": v_d_skip, "attn_sinks": v_attn_sinks,
               "ssm_norm_w": v_ssm_norm_w, "norm2_w": v_norm2_w, "final_norm_w": v_final_norm_w.reshape(1, D)}
    s_d, s_m, s_v = _adamw_many([small_w[k] for k in small_names], [small_g[k] for k in small_names],
                                [small_m[k] for k in small_names], [small_v[k] for k in small_names])

    order = ["w_ada", "b_ada", "norm1_w", "w_in", "conv_w", "conv_b", "dt_bias", "a_log", "d_skip", "attn_sinks",
             "ssm_norm_w", "w_out", "norm2_w", "w_gate_up", "w_down", "final_norm_w"]
    lead = {"w_ada", "w_in", "conv_w", "w_out", "w_gate_up", "w_down"}
    grads, deltas, new_m, new_v = [], [], [], []
    for name in order:
        if name in big:
            g, d, m2, v2 = big[name]
        else:
            i = small_names.index(name)
            g, d, m2, v2 = small_g[name], s_d[i], s_m[i], s_v[i]
        if name in lead:
            g, d, m2, v2 = g[None], d[None], m2[None], v2[None]
        if name == "final_norm_w":
            g, d, m2, v2 = g.reshape(D), d.reshape(D), m2.reshape(D), v2.reshape(D)
        grads.append(g)
        deltas.append(d)
        new_m.append(m2)
        new_v.append(v2)
    return (loss, gx[None], *grads, *deltas, *new_m, *new_v)
```

```python
import functools
import math

import jax
import jax.numpy as jnp
from jax import lax
from jax.experimental import pallas as pl
from jax.experimental.pallas import tpu as pltpu

F32 = jnp.float32
BF16 = jnp.bfloat16

N_DEV = 8
D = 1024
HD = 64
NQ = 8
AW = 512
KVW = 128
SW = 512
NST = 128
NH = 8
LB = 128
CONVK = 4
DFF = 2816
N_MOD = 6
IN_PROJ = 2312
INP = 2432
O_Q, O_K, O_V, O_Z, O_XBC, O_DT = 0, 512, 640, 768, 1280, 2304
ZXD = INP - O_Z
EPS = 1e-6
NEG = -1e30
ROPE_THETA = 10000.0
VMEM_LIMIT = 56 * 1024 * 1024

ADAM_LR = 0.001
ADAM_B1 = 0.9
ADAM_B2 = 0.999
ADAM_EPS = 1e-08
ADAM_WD = 0.01
ADAM_STEP = 10

NT_DIMS = (((1,), (1,)), ((), ()))
TN_DIMS = (((0,), (0,)), ((), ()))


def _pcall(body, **kw):
    return pl.pallas_call(body, **kw)


def _sds(shape, dtype):
    return jax.ShapeDtypeStruct(shape, dtype)


def _params(n_grid=1):
    return pltpu.CompilerParams(dimension_semantics=("arbitrary",) * n_grid, vmem_limit_bytes=VMEM_LIMIT)


def _const(shape):
    return pl.BlockSpec(shape, lambda *_: (0,) * len(shape), pipeline_mode=pl.Buffered(1))


def _largest_divisor(n, candidates):
    for cand in candidates:
        if n % cand == 0:
            return cand
    raise ValueError(f"no tile in {candidates} divides {n}")


def _rows(t, w):
    return pl.BlockSpec((t, w), lambda i: (i, 0))


def _dot(a, b):
    return jnp.dot(a, b, preferred_element_type=F32)


def _dot_nt(a, b):
    return lax.dot_general(a, b, NT_DIMS, preferred_element_type=F32)


def _dot_tn(a, b):
    return lax.dot_general(a, b, TN_DIMS, preferred_element_type=F32)


def _sigmoid(v):
    return 1.0 / (1.0 + jnp.exp(-v))


def _softplus(v):
    return jnp.maximum(v, 0.0) + jnp.log1p(jnp.exp(-jnp.abs(v)))


def _rope_sign_mask(shape):
    lane = lax.broadcasted_iota(jnp.int32, shape, 1)
    return (lane % HD) < (HD // 2)


def _rope(t, cs, sn, inverse):
    r_dn = pltpu.roll(t, HD // 2, 1)
    r_up = pltpu.roll(t, LB - HD // 2, 1)
    first = _rope_sign_mask(t.shape)
    if inverse:
        rot = jnp.where(first, r_up, -r_dn)
    else:
        rot = jnp.where(first, -r_up, r_dn)
    return t * cs + rot * sn


def _norm_mod_fwd(xv, nw, shift, scale):
    r = lax.rsqrt(jnp.mean(xv * xv, axis=-1, keepdims=True) + EPS)
    xh = xv * r
    return (xh * nw) * (1.0 + scale) + shift


def _norm_mod_bwd(xv, dh, nw, scale):
    r = lax.rsqrt(jnp.mean(xv * xv, axis=-1, keepdims=True) + EPS)
    xh = xv * r
    xn = xh * nw
    d_shift = jnp.sum(dh, axis=0, keepdims=True)
    d_scale = jnp.sum(dh * xn, axis=0, keepdims=True)
    dxn = dh * (1.0 + scale)
    d_w = jnp.sum(dxn * xh, axis=0, keepdims=True)
    dxh = dxn * nw
    dx = r * (dxh - xh * jnp.mean(dxh * xh, axis=-1, keepdims=True))
    return dx, d_shift, d_scale, d_w


def _inproj_fwd(x, pos, invf, mod8, n1w, w_in):
    s = x.shape[0]
    tt = min(512, s)

    def body(x_ref, pos_ref, invf_ref, mod_ref, nw_ref, w_ref,
             q_ref, k_ref, v_ref, z_ref, xbc_ref, dtr_ref, h1_ref, cos_ref, sin_ref):
        h = _norm_mod_fwd(x_ref[...], nw_ref[...], mod_ref[0:1, :], mod_ref[1:2, :])
        hb = h.astype(BF16)
        h1_ref[...] = hb
        proj = _dot_nt(hb, w_ref[...])
        ang = pos_ref[...].astype(F32) * invf_ref[...]
        cs = jnp.cos(ang)
        sn = jnp.sin(ang)
        cos_ref[...] = cs
        sin_ref[...] = sn
        for a in range(AW // LB):
            q_ref[:, a * LB:(a + 1) * LB] = _rope(proj[:, O_Q + a * LB:O_Q + (a + 1) * LB], cs, sn, False).astype(BF16)
        k_ref[...] = _rope(proj[:, O_K:O_V], cs, sn, False).astype(BF16)
        v_ref[...] = proj[:, O_V:O_Z].astype(BF16)
        z_ref[...] = proj[:, O_Z:O_XBC]
        xbc_ref[...] = proj[:, O_XBC:O_DT]
        dtr_ref[...] = proj[:, O_DT:INP]

    return _pcall(
        body, name="inproj_fwd", grid=(s // tt,),
        in_specs=[_rows(tt, D), _rows(tt, 1), _const((1, LB)), _const((8, D)), _const((1, D)), _const((INP, D))],
        out_specs=[_rows(tt, AW), _rows(tt, KVW), _rows(tt, KVW), _rows(tt, SW), _rows(tt, D), _rows(tt, LB),
                   _rows(tt, D), _rows(tt, LB), _rows(tt, LB)],
        out_shape=[_sds((s, AW), BF16), _sds((s, KVW), BF16), _sds((s, KVW), BF16), _sds((s, SW), F32),
                   _sds((s, D), F32), _sds((s, LB), F32), _sds((s, D), BF16), _sds((s, LB), F32), _sds((s, LB), F32)],
        compiler_params=_params(),
    )(x, pos, invf, mod8, n1w, w_in)


QPG = 4
ATT_SCALE = 1.0 / math.sqrt(HD)


def _stack_heads(val, g):
    return jnp.concatenate([val[:, (QPG * g + hh) * HD:(QPG * g + hh + 1) * HD] for hh in range(QPG)], axis=0)


def _unstack_heads(groups):
    return jnp.concatenate([grp[hh * LB:(hh + 1) * LB, :] for grp in groups for hh in range(QPG)], axis=1)


def _upper_mask():
    row = lax.broadcasted_iota(jnp.int32, (QPG * LB, LB), 0)
    col = lax.broadcasted_iota(jnp.int32, (QPG * LB, LB), 1)
    return col > (row % LB)


def _sink_col(sinks, g):
    return jnp.concatenate([jnp.broadcast_to(sinks[QPG * g + hh:QPG * g + hh + 1, 0:1], (LB, 1))
                            for hh in range(QPG)], axis=0)


def _band(upper, prev_part, cur_part):
    return jnp.where(upper, prev_part, cur_part)


def _attn_probs(n, qg, kcat, sink, upper):
    sp = _dot_nt(qg, kcat[0:LB, :]) * ATT_SCALE
    sc = _dot_nt(qg, kcat[LB:2 * LB, :]) * ATT_SCALE
    comb = _band(upper, jnp.where(n > 0, sp, NEG), sc)
    m = jnp.maximum(jnp.max(comb, axis=-1, keepdims=True), sink)
    p = jnp.exp(comb - m)
    es = jnp.exp(sink - m)
    denom = jnp.sum(p, axis=-1, keepdims=True) + es
    return p / denom, es / denom


def _attn_fwd_block(n, q_ref, kp_ref, kc_ref, vp_ref, vc_ref, sink_ref, o_ref):
    qv = q_ref[...]
    kcat = jnp.concatenate([kp_ref[...], kc_ref[...]], axis=0)
    vcat = jnp.concatenate([vp_ref[...], vc_ref[...]], axis=0)
    sinks = sink_ref[...]
    upper = _upper_mask()
    outs = []
    for g in range(NQ // QPG):
        sl = slice(g * HD, (g + 1) * HD)
        probs, _ = _attn_probs(n, _stack_heads(qv, g), kcat[:, sl], _sink_col(sinks, g), upper)
        outs.append(_dot(jnp.where(upper, probs, 0.0).astype(BF16), vcat[0:LB, sl])
                    + _dot(jnp.where(upper, 0.0, probs).astype(BF16), vcat[LB:2 * LB, sl]))
    o_ref[...] = _unstack_heads(outs).astype(BF16)


def _cumsum_rows(a, reverse):
    row = lax.broadcasted_iota(jnp.int32, a.shape, 0)
    step = 1
    while step < LB:
        if reverse:
            a = a + jnp.where(row < LB - step, pltpu.roll(a, LB - step, 0), 0.0)
        else:
            a = a + jnp.where(row >= step, pltpu.roll(a, step, 0), 0.0)
        step *= 2
    return a


SUB = 8


def _conv_shifts(tail, cur):
    row = lax.broadcasted_iota(jnp.int32, tail.shape, 0)
    out = [cur]
    for j in range(1, CONVK):
        rolled = pltpu.roll(cur, j, 0)
        top = jnp.where(row < j, pltpu.roll(tail, j, 0), rolled[0:SUB, :])
        out.append(jnp.concatenate([top, rolled[SUB:, :]], axis=0))
    return out


def _conv_advances(du, head):
    row = lax.broadcasted_iota(jnp.int32, head.shape, 0)
    out = []
    for j in range(1, CONVK):
        rolled = pltpu.roll(du, LB - j, 0)
        bottom = jnp.where(row >= SUB - j, pltpu.roll(head, SUB - j, 0), rolled[LB - SUB:, :])
        out.append(jnp.concatenate([rolled[:LB - SUB, :], bottom], axis=0))
    return out


def _split(v, terms):
    out = []
    for _ in range(terms - 1):
        t = v.astype(BF16)
        out.append(t)
        v = v - t.astype(F32)
    out.append(v.astype(BF16))
    return out


def _dot_sel(v, sel, terms):
    parts = [_dot(t, sel) for t in _split(v, terms)]
    return functools.reduce(lambda a, b: a + b, parts)


def _dot_nt_sel(v, sel, terms):
    parts = [_dot_nt(t, sel) for t in _split(v, terms)]
    return functools.reduce(lambda a, b: a + b, parts)


def _ssd_pre(xt_ref, xc_ref, cw_ref, cb_ref, dtr_ref, sp_ref, n):
    cur = xc_ref[...]
    tail = jnp.where(n > 0, xt_ref[...], 0.0)
    sh = _conv_shifts(tail, cur)
    u = cb_ref[...] + cw_ref[CONVK - 1:CONVK, :] * sh[0]
    for j in range(1, CONVK):
        u = u + cw_ref[CONVK - 1 - j:CONVK - j, :] * sh[j]
    sg_u = _sigmoid(u)
    xc = u * sg_u
    pre = dtr_ref[...] + sp_ref[0:1, :]
    dt = _softplus(pre)
    a_neg = -jnp.exp(sp_ref[1:2, :])
    acs = _cumsum_rows(dt * a_neg, False)
    return sh, u, sg_u, xc, pre, dt, a_neg, acs


def _gated_norm_fwd(y, z, nw):
    sz = z * _sigmoid(z)
    yz = y * sz
    parts = []
    for g in range(2):
        t = yz[:, g * 256:(g + 1) * 256]
        parts.append(t * lax.rsqrt(jnp.mean(t * t, axis=-1, keepdims=True) + EPS))
    return jnp.concatenate(parts, axis=1) * nw


HPG = 4
GW = HPG * HD


class _SsdChunk:
    def __init__(self, xc, dt, acs, spv, e64, e128):
        self.e64, self.e128 = e64, e128
        self.acs_t = jnp.transpose(acs)
        alast = acs[LB - 1:LB, :]
        self.e_all = jnp.exp(acs)
        self.dte_all = jnp.exp(alast - acs)
        self.elast = jnp.exp(alast)
        wide = _dot_sel(jnp.concatenate([dt, self.e_all, self.dte_all], axis=0), e64, 1)
        self.dt_x, self.e_x, self.dte_x = wide[0:LB], wide[LB:2 * LB], wide[2 * LB:3 * LB]
        self.dsk_x = _dot_sel(spv, e64, 3)[2:3, :]
        ac_x = _dot_sel(acs, e128, 3)
        row = lax.broadcasted_iota(jnp.int32, (HPG * LB, LB), 0)
        col = lax.broadcasted_iota(jnp.int32, (HPG * LB, LB), 1)
        causal = (row % LB) >= col
        lane = lax.broadcasted_iota(jnp.int32, (LB, GW), 1)
        self.head_lanes = [(lane >= hh * HD) & (lane < (hh + 1) * HD) for hh in range(HPG)]
        self.xs, self.xdt, self.b, self.c, self.bb, self.cb16, self.cbm, self.dm_st, self.m_st = ([] for _ in range(9))
        for g in range(2):
            heads = range(HPG * g, HPG * (g + 1))
            ac_st = jnp.concatenate([ac_x[:, j * LB:(j + 1) * LB] for j in heads], axis=0)
            ar_st = jnp.concatenate([jnp.broadcast_to(self.acs_t[j:j + 1, :], (LB, LB)) for j in heads], axis=0)
            dm_st = jnp.exp(jnp.where(causal, ac_st - ar_st, NEG))
            bg = xc[:, SW + g * NST:SW + (g + 1) * NST]
            cg = xc[:, SW + 2 * NST + g * NST:SW + 2 * NST + (g + 1) * NST]
            bgb, cgb = bg.astype(BF16), cg.astype(BF16)
            cbm = _dot_nt(cgb, bgb)
            xs_g = xc[:, g * GW:(g + 1) * GW]
            self.xs.append(xs_g)
            self.xdt.append(xs_g * self.dt_x[:, g * GW:(g + 1) * GW])
            self.b.append(bg)
            self.c.append(cg)
            self.bb.append(bgb)
            self.cb16.append(cgb)
            self.cbm.append(cbm)
            self.dm_st.append(dm_st)
            self.m_st.append(jnp.concatenate([cbm] * HPG, axis=0) * dm_st)

    def elast_rows(self, g):
        return jnp.concatenate([jnp.broadcast_to(self.elast[:, j:j + 1], (HD, NST))
                                for j in range(HPG * g, HPG * (g + 1))], axis=0)

    def diag_blocks(self, stacked):
        out = stacked[(HPG - 1) * LB:HPG * LB, :]
        for hh in range(HPG - 2, -1, -1):
            out = jnp.where(self.head_lanes[hh], stacked[hh * LB:(hh + 1) * LB, :], out)
        return out

    def block_diag(self, v):
        return jnp.concatenate([jnp.where(self.head_lanes[hh], v, 0.0) for hh in range(HPG)], axis=0)


def _ssd_fwd_block(n, xt_ref, xc_ref, cw_ref, cb_ref, dtr_ref, sp_ref, z_ref, nw_ref, e64_ref, e128_ref,
                   yn_ref, y_ref, hs_ref, h_scr):
    @pl.when(n == 0)
    def _():
        h_scr[...] = jnp.zeros_like(h_scr)

    h_all = h_scr[...]
    hs_ref[0] = h_all
    _, _, _, xc, _, dt, _, acs = _ssd_pre(xt_ref, xc_ref, cw_ref, cb_ref, dtr_ref, sp_ref, n)
    ck = _SsdChunk(xc, dt, acs, sp_ref[...], e64_ref[...], e128_ref[...])
    ys, hn = [], []
    for g in range(2):
        gl = slice(g * GW, (g + 1) * GW)
        xdt = ck.xdt[g]
        hg = h_all[gl, :]
        y_diag = ck.diag_blocks(_dot(ck.m_st[g].astype(BF16), xdt.astype(BF16)))
        y_off = ck.e_x[:, gl] * _dot_nt(ck.cb16[g], hg.astype(BF16))
        ys.append(y_diag + y_off + ck.xs[g] * ck.dsk_x[:, gl])
        hn.append(hg * ck.elast_rows(g) + _dot_tn((xdt * ck.dte_x[:, gl]).astype(BF16), ck.bb[g]))
    h_scr[...] = jnp.concatenate(hn, axis=0)
    y = jnp.concatenate(ys, axis=1)
    y_ref[...] = y
    yn_ref[...] = _gated_norm_fwd(y, z_ref[...], nw_ref[...]).astype(BF16)


def _mixer_fwd(q, k, v, sinks8, xbc, conv_w8, conv_b, dtr, ssm_p, z, nw, gathers):
    s = q.shape[0]
    nb = s // LB
    prev = lambda n: (jnp.maximum(n - 1, 0), 0)
    cur = lambda n: (n, 0)
    items, ex_shapes, n_g = _exchange_items(gathers, [])
    ne = len(items)

    n_in = 16
    e64, e128 = _head_expanders()

    def body(*refs):
        a_in, s_in, ex_in = refs[:6], refs[6:n_in], refs[n_in:n_in + ne]
        o_ref, yn_ref, y_ref, hs_ref = refs[n_in + ne:n_in + 4 + ne]
        ex_out = refs[n_in + 4 + ne:n_in + 4 + 2 * ne]
        h_scr = refs[n_in + 4 + 2 * ne]
        sems = refs[n_in + 5 + 2 * ne:]
        n = pl.program_id(0)

        @pl.when(n == 0)
        def _():
            _Exchange(n_g, ex_in, ex_out, sems).start()

        _attn_fwd_block(n, *a_in, o_ref)
        _ssd_fwd_block(n, *s_in, yn_ref, y_ref, hs_ref, h_scr)

        @pl.when(n == nb - 1)
        def _():
            _Exchange(n_g, ex_in, ex_out, sems).finish()

    any_spec = pl.BlockSpec(memory_space=pl.ANY)
    tail = pl.BlockSpec((SUB, D), lambda n: (jnp.maximum(n * (LB // SUB) - 1, 0), 0))
    outs = _pcall(
        body, name="mixer_fwd", grid=(nb,),
        in_specs=[pl.BlockSpec((LB, AW), cur), pl.BlockSpec((LB, KVW), prev), pl.BlockSpec((LB, KVW), cur),
                  pl.BlockSpec((LB, KVW), prev), pl.BlockSpec((LB, KVW), cur), _const((8, LB)),
                  tail, pl.BlockSpec((LB, D), cur), _const((8, D)), _const((1, D)),
                  pl.BlockSpec((LB, LB), cur), _const((8, LB)), pl.BlockSpec((LB, SW), cur), _const((1, SW)),
                  _const(e64.shape), _const(e128.shape)]
        + [any_spec] * ne,
        out_specs=[pl.BlockSpec((LB, AW), cur), pl.BlockSpec((LB, SW), cur), pl.BlockSpec((LB, SW), cur),
                   pl.BlockSpec((1, NH * HD, NST), lambda n: (n, 0, 0))] + [any_spec] * ne,
        out_shape=[_sds((s, AW), BF16), _sds((s, SW), BF16), _sds((s, SW), F32), _sds((nb, NH * HD, NST), F32)]
        + ex_shapes,
        scratch_shapes=[pltpu.VMEM((NH * HD, NST), F32)] + _exchange_sems(ne),
        compiler_params=_params(),
    )(q, k, k, v, v, sinks8, xbc, xbc, conv_w8, conv_b, dtr, ssm_p, z, nw, e64, e128, *items)
    return outs[0], outs[1], outs[2], outs[3], outs[4:]


def _head_expanders():
    j = lax.broadcasted_iota(jnp.int32, (LB, NH * HD), 0)
    e64 = (lax.broadcasted_iota(jnp.int32, (LB, NH * HD), 1) // HD == j).astype(BF16)
    j = lax.broadcasted_iota(jnp.int32, (LB, NH * LB), 0)
    e128 = (lax.broadcasted_iota(jnp.int32, (LB, NH * LB), 1) // LB == j).astype(BF16)
    return e64, e128


def _outproj_ffn_fwd_loss(attn, yn, x, tgt, mod8, n2w, fnw, w_out, w_gu_t, w_down):
    s = x.shape[0]
    tf = min(256, s)

    def body(a_ref, y_ref, x_ref, t_ref, mod_ref, nw_ref, fw_ref, wo_ref, wgu_ref, wd_ref,
             x2_ref, h2_ref, mo_ref, mix_ref, gu_ref, act_ref, dx3_ref, sm_ref):
        i = pl.program_id(0)

        @pl.when(i == 0)
        def _():
            sm_ref[...] = jnp.zeros_like(sm_ref)

        mix = jnp.concatenate([a_ref[...], y_ref[...]], axis=1)
        mix_ref[...] = mix
        mo = _dot(mix, wo_ref[...])
        mo_ref[...] = mo.astype(BF16)
        x2 = x_ref[...] + mod_ref[2:3, :] * mo
        x2_ref[...] = x2
        h2 = _norm_mod_fwd(x2, nw_ref[...], mod_ref[3:4, :], mod_ref[4:5, :]).astype(BF16)
        h2_ref[...] = h2
        gu = _dot_nt(h2, wgu_ref[...])
        gu_ref[...] = gu.astype(BF16)
        g = gu[:, :DFF]
        act = (g * _sigmoid(g) * gu[:, DFF:]).astype(BF16)
        act_ref[...] = act
        ff = _dot(act, wd_ref[...])
        x3 = x2 + mod_ref[5:6, :] * ff
        r = lax.rsqrt(jnp.mean(x3 * x3, axis=-1, keepdims=True) + EPS)
        xh = x3 * r
        fw = fw_ref[...]
        err = xh * fw - t_ref[...]
        dy = err * (1.0 / D)
        dxh = dy * fw
        dx3 = r * (dxh - xh * jnp.mean(dxh * xh, axis=-1, keepdims=True))
        dx3_ref[...] = dx3
        sm_ref[0:1, :] += jnp.sum(dx3 * ff, axis=0, keepdims=True)
        sm_ref[1:2, :] += jnp.sum(dy * xh, axis=0, keepdims=True)
        sm_ref[2:3, :] += jnp.sum(err * err, axis=0, keepdims=True)

    return _pcall(
        body, name="outproj_ffn_fwd_loss", grid=(s // tf,),
        in_specs=[_rows(tf, AW), _rows(tf, SW), _rows(tf, D), _rows(tf, D), _const((8, D)), _const((1, D)),
                  _const((1, D)), _const((D, D)), _const((2 * DFF, D)), _const((DFF, D))],
        out_specs=[_rows(tf, D), _rows(tf, D), _rows(tf, D), _rows(tf, D), _rows(tf, 2 * DFF), _rows(tf, DFF),
                   _rows(tf, D), pl.BlockSpec((8, D), lambda i: (0, 0))],
        out_shape=[_sds((s, D), F32), _sds((s, D), BF16), _sds((s, D), BF16), _sds((s, D), BF16),
                   _sds((s, 2 * DFF), BF16), _sds((s, DFF), BF16), _sds((s, D), F32), _sds((8, D), F32)],
        compiler_params=_params(),
    )(attn, yn, x, tgt, mod8, n2w, fnw, w_out, w_gu_t, w_down)


def _ffn_bwd(dx3, gu, x2, mixout, mod8, n2w, w_gu, w_down, w_out):
    s = x2.shape[0]
    tb = min(256, s)

    def body(dx3_ref, gu_ref, x2_ref, mo_ref, mod_ref, nw_ref, wgu_ref, wd_ref, wo_ref,
             dx2_ref, dff_ref, dgu_ref, dmix_ref, dattn_ref, dyn_ref, sm_ref):
        i = pl.program_id(0)

        @pl.when(i == 0)
        def _():
            sm_ref[...] = jnp.zeros_like(sm_ref)

        dx3 = dx3_ref[...]
        dff = (dx3 * mod_ref[5:6, :]).astype(BF16)
        dff_ref[...] = dff
        dact = _dot_nt(dff, wd_ref[...])
        g = gu_ref[:, :DFF].astype(F32)
        u = gu_ref[:, DFF:].astype(F32)
        sg = _sigmoid(g)
        dgu = jnp.concatenate([dact * u * sg * (1.0 + g * (1.0 - sg)), dact * g * sg], axis=1).astype(BF16)
        dgu_ref[...] = dgu
        dh2 = _dot(dgu, wgu_ref[...])
        dxn, d_shift, d_scale, d_w = _norm_mod_bwd(x2_ref[...], dh2, nw_ref[...], mod_ref[4:5, :])
        dx2 = dx3 + dxn
        dx2_ref[...] = dx2
        sm_ref[0:1, :] += d_shift
        sm_ref[1:2, :] += d_scale
        sm_ref[2:3, :] += d_w
        sm_ref[3:4, :] += jnp.sum(dx2 * mo_ref[...].astype(F32), axis=0, keepdims=True)
        dmix = (dx2 * mod_ref[2:3, :]).astype(BF16)
        dmix_ref[...] = dmix
        dmi = _dot_nt(dmix, wo_ref[...])
        dattn_ref[...] = dmi[:, :AW].astype(BF16)
        dyn_ref[...] = dmi[:, AW:]

    return _pcall(
        body, name="ffn_bwd", grid=(s // tb,),
        in_specs=[_rows(tb, D), _rows(tb, 2 * DFF), _rows(tb, D), _rows(tb, D), _const((8, D)), _const((1, D)),
                  _const((2 * DFF, D)), _const((DFF, D)), _const((D, D))],
        out_specs=[_rows(tb, D), _rows(tb, D), _rows(tb, 2 * DFF), _rows(tb, D), _rows(tb, AW), _rows(tb, SW),
                   pl.BlockSpec((8, D), lambda i: (0, 0))],
        out_shape=[_sds((s, D), F32), _sds((s, D), BF16), _sds((s, 2 * DFF), BF16), _sds((s, D), BF16),
                   _sds((s, AW), BF16), _sds((s, SW), F32), _sds((8, D), F32)],
        compiler_params=_params(),
    )(dx3, gu, x2, mixout, mod8, n2w, w_gu, w_down, w_out)


def _ssd_bwd_block(i, n, *refs):
    def run(dyn_ref, y_ref, z_ref, xt_ref, xc_ref, cw_ref, cb_ref, dtr_ref, sp_ref, nw_ref, hs_ref, e64_ref, e128_ref,
            dzxd_ref, sm_ref, dh_scr, dun_scr):
        @pl.when(i == 0)
        def _():
            dh_scr[...] = jnp.zeros_like(dh_scr)
            dun_scr[...] = jnp.zeros_like(dun_scr)
            sm_ref[...] = jnp.zeros_like(sm_ref)

        sh, u, sg_u, xc, pre, dt, a_neg, acs = _ssd_pre(xt_ref, xc_ref, cw_ref, cb_ref, dtr_ref, sp_ref, n)
        ck = _SsdChunk(xc, dt, acs, sp_ref[...], e64_ref[...], e128_ref[...])
        h_all = hs_ref[0]
        dh_all = dh_scr[...]
        riota = lax.broadcasted_iota(jnp.int32, (LB, LB), 0)
        lane1 = lax.broadcasted_iota(jnp.int32, (1, LB), 1)

        z = z_ref[...]
        y = y_ref[...]
        sgz = _sigmoid(z)
        sz = z * sgz
        yz = y * sz
        nwv = nw_ref[...]
        dyn_v = dyn_ref[...]
        dyhat = dyn_v * nwv
        yhat_parts, dyz_parts = [], []
        for g in range(2):
            gs = slice(g * 256, (g + 1) * 256)
            t = yz[:, gs]
            rg = lax.rsqrt(jnp.mean(t * t, axis=-1, keepdims=True) + EPS)
            yh = t * rg
            dyh = dyhat[:, gs]
            yhat_parts.append(yh)
            dyz_parts.append(rg * (dyh - yh * jnp.mean(dyh * yh, axis=-1, keepdims=True)))
        yhat = jnp.concatenate(yhat_parts, axis=1)
        dyz = jnp.concatenate(dyz_parts, axis=1)
        sm_ref[5:6, 0:SW] += jnp.sum(dyn_v * yhat, axis=0, keepdims=True)
        dy = dyz * sz
        dzxd_ref[:, 0:SW] = (dyz * y * sgz * (1.0 + z * (1.0 - sgz))).astype(BF16)

        cat = lambda parts: jnp.concatenate(parts, axis=1)
        dxs, dbs, dcs, dhp, g_cat, de_x, ddte_x, ddt_x, ddsk_x = ([] for _ in range(9))
        dacs_t = jnp.zeros((LB, LB), F32)
        hsum = jnp.zeros((1, LB), F32)
        for g in range(2):
            gl = slice(g * GW, (g + 1) * GW)
            xs_g, xdt, bgb, cgb = ck.xs[g], ck.xdt[g], ck.bb[g], ck.cb16[g]
            m_st, dm_st = ck.m_st[g], ck.dm_st[g]
            dt_x, e_x, dte_x = ck.dt_x[:, gl], ck.e_x[:, gl], ck.dte_x[:, gl]
            xdtb = xdt.astype(BF16)
            hg, dhn = h_all[gl, :], dh_all[gl, :]
            hb, dhnb = hg.astype(BF16), dhn.astype(BF16)
            dy_g = dy[:, gl]
            ddsk_x.append(jnp.sum(dy_g * xs_g, axis=0, keepdims=True))
            dy_bd = ck.block_diag(dy_g).astype(BF16)
            dm4 = _dot_nt(dy_bd, xdtb)
            dxdt = _dot_tn(m_st.astype(BF16), dy_bd)
            gmat = dm4 * m_st
            dcbm = dm4 * dm_st
            dcb = dcbm[0:LB] + dcbm[LB:2 * LB] + dcbm[2 * LB:3 * LB] + dcbm[3 * LB:4 * LB]
            g_cat.append(cat([gmat[hh * LB:(hh + 1) * LB, :] for hh in range(HPG)]))
            for hh in range(HPG):
                j = HPG * g + hh
                col_sum = jnp.sum(gmat[hh * LB:(hh + 1) * LB, :], axis=0, keepdims=True)
                dacs_t = dacs_t - jnp.where(riota == j, col_sum, 0.0)
                hsl = slice(hh * HD, (hh + 1) * HD)
                hsum = hsum + jnp.where(lane1 == j, jnp.sum(dhn[hsl, :] * hg[hsl, :]), 0.0)
            dchb = (dy_g * e_x).astype(BF16)
            dcg = _dot(dchb, hb)
            dh_prev = _dot_tn(dchb, cgb)
            de_x.append(dy_g * _dot_nt(cgb, hb))
            dxs_s = _dot_nt(bgb, dhnb)
            dbg = _dot((xdt * dte_x).astype(BF16), dhnb)
            dxdt = dxdt + dxs_s * dte_x
            ddte_x.append(dxs_s * xdt)
            dhp.append(dhn * ck.elast_rows(g) + dh_prev)
            dxs.append(dy_g * ck.dsk_x[:, gl] + dxdt * dt_x)
            ddt_x.append(dxdt * xs_g)
            dcbb = dcb.astype(BF16)
            dbs.append(dbg + _dot_tn(dcbb, cgb))
            dcs.append(dcg + _dot(dcbb, bgb))
        dh_scr[...] = jnp.concatenate(dhp, axis=0)
        red = _dot_nt_sel(jnp.concatenate([cat(de_x), cat(ddte_x), cat(ddt_x)], axis=0), ck.e64, 2)
        de_c, ddte_c, ddt_c = red[0:LB], red[LB:2 * LB], red[2 * LB:3 * LB]
        ddsk = _dot_nt_sel(jnp.broadcast_to(cat(ddsk_x), (SUB, NH * HD)), ck.e64, 2)[0:1, :]
        t1 = ddte_c * ck.dte_all
        dalast = jnp.sum(t1, axis=0, keepdims=True) + hsum * ck.elast
        dacs = (_dot_nt_sel(cat(g_cat), ck.e128, 2) + de_c * ck.e_all - t1 + jnp.transpose(dacs_t)
                + jnp.where(riota == LB - 1, dalast, 0.0))
        da = _cumsum_rows(dacs, True)
        ddt = ddt_c + da * a_neg
        da_log = jnp.sum(da * dt, axis=0, keepdims=True) * a_neg
        ddtr = ddt * _sigmoid(pre)
        dzxd_ref[:, SW + D:ZXD] = ddtr.astype(BF16)
        sm_ref[6:7, 0:LB] += jnp.sum(ddtr, axis=0, keepdims=True)
        sm_ref[6:7, LB:2 * LB] += da_log
        sm_ref[6:7, 2 * LB:3 * LB] += ddsk

        du = cat(dxs + dbs + dcs) * (sg_u * (1.0 + u * (1.0 - sg_u)))
        sm_ref[0:1, :] += jnp.sum(du, axis=0, keepdims=True)
        for k in range(CONVK):
            sm_ref[1 + k:2 + k, :] += jnp.sum(du * sh[CONVK - 1 - k], axis=0, keepdims=True)
        adv = _conv_advances(du, dun_scr[...])
        dxbc = cw_ref[CONVK - 1:CONVK, :] * du
        for j in range(1, CONVK):
            dxbc = dxbc + cw_ref[CONVK - 1 - j:CONVK - j, :] * adv[j - 1]
        dun_scr[...] = du[0:SUB, :]
        dzxd_ref[:, SW:SW + D] = dxbc.astype(BF16)

    run(*refs)


def _attn_bwd_block(i, n, q_ref, kp_ref, kc_ref, vp_ref, vc_ref, o_ref, do_ref, cos_ref, sin_ref, sink_ref,
                    dq_ref, dkv_ref, ds_ref, ck_scr, cv_scr):
    @pl.when(i == 0)
    def _():
        ds_ref[...] = jnp.zeros_like(ds_ref)
        ck_scr[...] = jnp.zeros_like(ck_scr)
        cv_scr[...] = jnp.zeros_like(cv_scr)

    qv, ov, dov, sinks = q_ref[...], o_ref[...], do_ref[...], sink_ref[...]
    kcat = jnp.concatenate([kp_ref[...], kc_ref[...]], axis=0)
    vcat = jnp.concatenate([vp_ref[...], vc_ref[...]], axis=0)
    upper = _upper_mask()
    srow = lax.broadcasted_iota(jnp.int32, (8, LB), 0)
    slane = lax.broadcasted_iota(jnp.int32, (8, LB), 1)
    dsink = jnp.zeros((8, LB), F32)
    dq_g, dk_g, dv_g = [], [], []
    for g in range(NQ // QPG):
        sl = slice(g * HD, (g + 1) * HD)
        qg = _stack_heads(qv, g)
        dog = _stack_heads(dov, g)
        probs, psink = _attn_probs(n, qg, kcat[:, sl], _sink_col(sinks, g), upper)
        delta = jnp.sum(dog.astype(F32) * _stack_heads(ov, g).astype(F32), axis=1, keepdims=True)
        dsc = probs * (_band(upper, _dot_nt(dog, vcat[0:LB, sl]), _dot_nt(dog, vcat[LB:2 * LB, sl])) - delta)
        sink_terms = psink * delta
        for hh in range(QPG):
            dsink = dsink - jnp.where((srow == QPG * g + hh) & (slane == 0),
                                      jnp.sum(sink_terms[hh * LB:(hh + 1) * LB, :]), 0.0)
        ds_p = jnp.where(upper, dsc, 0.0).astype(BF16)
        ds_c = jnp.where(upper, 0.0, dsc).astype(BF16)
        dq_g.append((_dot(ds_p, kcat[0:LB, sl]) + _dot(ds_c, kcat[LB:2 * LB, sl])) * ATT_SCALE)
        dk_g.append(jnp.concatenate([_dot_tn(ds_p, qg), _dot_tn(ds_c, qg)], axis=0) * ATT_SCALE)
        dv_g.append(jnp.concatenate([_dot_tn(jnp.where(upper, probs, 0.0).astype(BF16), dog),
                                     _dot_tn(jnp.where(upper, 0.0, probs).astype(BF16), dog)], axis=0))
    ds_ref[...] += dsink
    cs = cos_ref[...]
    sn = sin_ref[...]
    dq = _unstack_heads(dq_g)
    dk2 = jnp.concatenate(dk_g, axis=1)
    dv2 = jnp.concatenate(dv_g, axis=1)
    for a in range(AW // LB):
        dq_ref[:, a * LB:(a + 1) * LB] = _rope(dq[:, a * LB:(a + 1) * LB], cs, sn, True).astype(BF16)
    dkv_ref[:, 0:KVW] = _rope(ck_scr[...] + dk2[LB:2 * LB, :], cs, sn, True).astype(BF16)
    dkv_ref[:, KVW:2 * KVW] = (cv_scr[...] + dv2[LB:2 * LB, :]).astype(BF16)
    ck_scr[...] = dk2[0:LB, :]
    cv_scr[...] = dv2[0:LB, :]


def _mixer_bwd(q, k, v, o, do, cos, sin, sinks8, dyn, y, z, xbc, conv_w8, conv_b, dtr, ssm_p, nw, hs, scatters):
    s = q.shape[0]
    nb = s // LB
    cur = lambda i: (nb - 1 - i, 0)
    prev = lambda i: (jnp.maximum(nb - 2 - i, 0), 0)
    n_in = 23
    items, ex_shapes, n_g = _exchange_items([], scatters)
    ne = len(items)
    e64, e128 = _head_expanders()

    def body(*refs):
        i = pl.program_id(0)
        n = nb - 1 - i
        a_in, s_in, ex_in = refs[:10], refs[10:n_in], refs[n_in:n_in + ne]
        dp_ref, ds_ref, sm_ref = refs[n_in + ne:n_in + ne + 3]
        dq_ref, dkv_ref, dzxd_ref = dp_ref.at[:, O_Q:O_K], dp_ref.at[:, O_K:O_Z], dp_ref.at[:, O_Z:INP]
        ex_out = refs[n_in + ne + 3:n_in + 2 * ne + 3]
        ck_scr, cv_scr, dh_scr, dun_scr = refs[n_in + 2 * ne + 3:n_in + 2 * ne + 7]
        sems = refs[n_in + 2 * ne + 7:]

        @pl.when(i == 0)
        def _():
            _Exchange(n_g, ex_in, ex_out, sems).start()

        _attn_bwd_block(i, n, *a_in, dq_ref, dkv_ref, ds_ref, ck_scr, cv_scr)
        _ssd_bwd_block(i, n, *s_in, dzxd_ref, sm_ref, dh_scr, dun_scr)

        @pl.when(i == nb - 1)
        def _():
            _Exchange(n_g, ex_in, ex_out, sems).finish()

    any_spec = pl.BlockSpec(memory_space=pl.ANY)
    tail = pl.BlockSpec((SUB, D), lambda i: (jnp.maximum((nb - 1 - i) * (LB // SUB) - 1, 0), 0))
    outs = _pcall(
        body, name="mixer_bwd", grid=(nb,),
        in_specs=[pl.BlockSpec((LB, AW), cur), pl.BlockSpec((LB, KVW), prev), pl.BlockSpec((LB, KVW), cur),
                  pl.BlockSpec((LB, KVW), prev), pl.BlockSpec((LB, KVW), cur), pl.BlockSpec((LB, AW), cur),
                  pl.BlockSpec((LB, AW), cur), pl.BlockSpec((LB, LB), cur), pl.BlockSpec((LB, LB), cur),
                  _const((8, LB)),
                  pl.BlockSpec((LB, SW), cur), pl.BlockSpec((LB, SW), cur), pl.BlockSpec((LB, SW), cur),
                  tail, pl.BlockSpec((LB, D), cur), _const((8, D)), _const((1, D)),
                  pl.BlockSpec((LB, LB), cur), _const((8, LB)), _const((1, SW)),
                  pl.BlockSpec((1, NH * HD, NST), lambda i: (nb - 1 - i, 0, 0)),
                  _const(e64.shape), _const(e128.shape)] + [any_spec] * ne,
        out_specs=[pl.BlockSpec((LB, INP), cur), pl.BlockSpec((8, LB), lambda i: (0, 0)),
                   pl.BlockSpec((8, D), lambda i: (0, 0))] + [any_spec] * ne,
        out_shape=[_sds((s, INP), BF16), _sds((8, LB), F32), _sds((8, D), F32)] + ex_shapes,
        scratch_shapes=[pltpu.VMEM((LB, KVW), F32), pltpu.VMEM((LB, KVW), F32),
                        pltpu.VMEM((NH * HD, NST), F32), pltpu.VMEM((SUB, D), F32)]
        + _exchange_sems(ne),
        compiler_params=_params(),
    )(q, k, k, v, v, o, do, cos, sin, sinks8, dyn, y, z, xbc, xbc, conv_w8, conv_b, dtr, ssm_p, nw, hs, e64, e128,
      *items)
    return outs[0], outs[1], outs[2], outs[3:]


def _inproj_bwd(dproj, x, dx2, mod8, n1w, w_in_t, scatters):
    s = x.shape[0]
    tt = min(512, s)
    nt = s // tt
    items, ex_shapes, n_g = _exchange_items([], scatters)
    ne = len(items)

    def body(*refs):
        dp_ref, x_ref, dx2_ref, mod_ref, nw_ref, w_ref = refs[:6]
        ex_in = refs[6:6 + ne]
        gx_ref, sm_ref = refs[6 + ne:8 + ne]
        ex_out = refs[8 + ne:8 + 2 * ne]
        sems = refs[8 + 2 * ne:]
        i = pl.program_id(0)

        @pl.when(i == 0)
        def _():
            sm_ref[...] = jnp.zeros_like(sm_ref)
            _Exchange(n_g, ex_in, ex_out, sems).start()

        dh1 = _dot(dp_ref[...], w_ref[...])
        dxn, d_shift, d_scale, d_w = _norm_mod_bwd(x_ref[...], dh1, nw_ref[...], mod_ref[1:2, :])
        gx_ref[...] = dx2_ref[...] + dxn
        sm_ref[0:1, :] += d_shift
        sm_ref[1:2, :] += d_scale
        sm_ref[2:3, :] += d_w

        @pl.when(i == nt - 1)
        def _():
            _Exchange(n_g, ex_in, ex_out, sems).finish()

    any_spec = pl.BlockSpec(memory_space=pl.ANY)
    outs = _pcall(
        body, name="inproj_bwd", grid=(nt,),
        in_specs=[_rows(tt, INP), _rows(tt, D), _rows(tt, D), _const((8, D)), _const((1, D)), _const((INP, D))]
        + [any_spec] * ne,
        out_specs=[_rows(tt, D), pl.BlockSpec((8, D), lambda i: (0, 0))] + [any_spec] * ne,
        out_shape=[_sds((s, D), F32), _sds((8, D), F32)] + ex_shapes,
        scratch_shapes=_exchange_sems(ne),
        compiler_params=_params(),
    )(dproj, x, dx2, mod8, n1w, w_in_t, *items)
    return outs[0], outs[1], outs[2:]


def _wgrad(a, b, name):
    s, m = a.shape
    n = b.shape[1]
    tk = min(1024, s)
    wide = (1408, 1024, 512)
    tm = next((t for t in wide if m % t == 0), m)
    tn = n if n <= 2048 else _largest_divisor(n, wide)
    nk = s // tk

    def body(a_ref, b_ref, o_ref, acc):
        kk = pl.program_id(2)

        @pl.when(kk == 0)
        def _():
            acc[...] = jnp.zeros_like(acc)

        acc[...] += _dot_tn(a_ref[...], b_ref[...])

        @pl.when(kk == nk - 1)
        def _():
            o_ref[...] = acc[...].astype(BF16)

    return _pcall(
        body, name=name, grid=(m // tm, n // tn, nk),
        in_specs=[pl.BlockSpec((tk, tm), lambda i, j, kk: (kk, i)), pl.BlockSpec((tk, tn), lambda i, j, kk: (kk, j))],
        out_specs=pl.BlockSpec((tm, tn), lambda i, j, kk: (i, j)),
        out_shape=_sds((m, n), BF16),
        scratch_shapes=[pltpu.VMEM((tm, tn), F32)],
        compiler_params=_params(3),
    )(a, b)


PACK_ROWS = 24


def _pack_small(sm_f, sm_b, sm_s, sm_i, dsink):
    def body(f_ref, b_ref, s_ref, i_ref, k_ref, o_ref):
        o_ref[...] = jnp.zeros_like(o_ref)
        o_ref[0:2, :] = i_ref[0:2, :]
        o_ref[2:3, :] = b_ref[3:4, :]
        o_ref[3:5, :] = b_ref[0:2, :]
        o_ref[5:6, :] = f_ref[0:1, :]
        o_ref[6:7, :] = i_ref[2:3, :]
        o_ref[7:8, :] = b_ref[2:3, :]
        o_ref[8:9, :] = f_ref[1:2, :]
        o_ref[9:14, :] = s_ref[0:5, :]
        o_ref[14:15, :] = s_ref[5:6, :]
        o_ref[15:16, 0:3 * LB] = s_ref[6:7, 0:3 * LB]
        lane = lax.broadcasted_iota(jnp.int32, (1, LB), 1)
        sk = jnp.zeros((1, LB), F32)
        for h in range(NQ):
            sk = sk + jnp.where(lane == h, k_ref[h:h + 1, 0:1], 0.0)
        o_ref[15:16, 3 * LB:4 * LB] = sk
        o_ref[16:17, :] = f_ref[2:3, :]

    return _pcall(body, name="pack_small", out_shape=_sds((PACK_ROWS, D), F32))(sm_f, sm_b, sm_s, sm_i, dsink)


def _exchange(gathers, scatters, name):
    items, shapes, n_g = _exchange_items(gathers, scatters)
    n = len(items)

    def body(*refs):
        ex = _Exchange(n_g, refs[:n], refs[n:2 * n], refs[2 * n:])
        ex.start()
        ex.finish()

    any_spec = pl.BlockSpec(memory_space=pl.ANY)
    return _pcall(
        body, name=name, in_specs=[any_spec] * n, out_specs=[any_spec] * n, out_shape=shapes,
        scratch_shapes=_exchange_sems(n),
    )(*items)


def _exchange_items(gathers, scatters):
    items = list(gathers) + list(scatters)
    shapes = [_sds((N_DEV,) + a.shape, a.dtype) for a in gathers] + [_sds(a.shape, a.dtype) for a in scatters]
    return items, shapes, len(gathers)


def _exchange_sems(n):
    return [pltpu.SemaphoreType.DMA((n, N_DEV - 1)), pltpu.SemaphoreType.DMA((n, N_DEV - 1)),
            pltpu.SemaphoreType.DMA((n,))]


class _Exchange:
    def __init__(self, n_g, ins, outs, sems):
        self.n_g, self.ins, self.outs = n_g, ins, outs
        self.send_sems, self.recv_sems, self.loc_sems = sems
        xi, yi, ci = lax.axis_index("x"), lax.axis_index("y"), lax.axis_index("c")
        self.me = 4 * xi + 2 * yi + ci
        self.peers = []
        for r in range(1, N_DEV):
            px = 1 - xi if r & 4 else xi
            py = 1 - yi if r & 2 else yi
            pc = 1 - ci if r & 1 else ci
            self.peers.append(((px, py, pc), 4 * px + 2 * py + pc))

    def _copy(self, t, r, landing):
        dev, peer = self.peers[r]
        src = self.ins[t] if t < self.n_g else self.ins[t].at[peer]
        return pltpu.make_async_remote_copy(
            src_ref=src, dst_ref=self.outs[t].at[landing], send_sem=self.send_sems.at[t, r],
            recv_sem=self.recv_sems.at[t, r], device_id=dev, device_id_type=pl.DeviceIdType.MESH)

    def _local(self, t):
        src = self.ins[t] if t < self.n_g else self.ins[t].at[self.me]
        return pltpu.make_async_copy(src, self.outs[t].at[self.me], self.loc_sems.at[t])

    def start(self):
        for t in range(len(self.ins)):
            self._local(t).start()
            for r in range(N_DEV - 1):
                self._copy(t, r, self.me).start()

    def finish(self):
        n = len(self.ins)
        for t in range(n):
            for r in range(N_DEV - 1):
                self._copy(t, r, self.peers[r][1]).wait_recv()
        for t in range(n):
            for r in range(N_DEV - 1):
                self._copy(t, r, self.me).wait_send()
            self._local(t).wait()


def _ada_fwd(c_all, w_cols, b_cols):
    def body(c_ref, w_ref, b_ref, o_ref):
        cv = c_ref[...]
        sc = (cv * _sigmoid(cv)).astype(BF16)
        o_ref[...] = _dot(sc, w_ref[...].astype(BF16)) + b_ref[...]

    return _pcall(body, name="ada_fwd", out_shape=_sds((N_DEV, w_cols.shape[1]), F32),
                  compiler_params=_params(0))(c_all, w_cols, b_cols)


def _adamw(w, g, m, v):
    m2 = ADAM_B1 * m + (1.0 - ADAM_B1) * g
    v2 = ADAM_B2 * v + (1.0 - ADAM_B2) * (g * g)
    m_hat = m2 / (1.0 - ADAM_B1 ** ADAM_STEP)
    v_hat = v2 / (1.0 - ADAM_B2 ** ADAM_STEP)
    delta = -ADAM_LR * (m_hat / (jnp.sqrt(v_hat) + ADAM_EPS) + ADAM_WD * w)
    return delta, m2, v2


def _sum_adamw(parts, w, m, v, name):
    rws, cols = w.shape
    tr = next((t for t in (256, 176, 128) if rws % t == 0), rws)

    def body(p_ref, w_ref, m_ref, v_ref, g_ref, d_ref, mo_ref, vo_ref):
        g = p_ref[0].astype(F32)
        for dev in range(1, N_DEV):
            g = g + p_ref[dev].astype(F32)
        g_ref[...] = g
        d_ref[...], mo_ref[...], vo_ref[...] = _adamw(w_ref[...], g, m_ref[...], v_ref[...])

    blk = pl.BlockSpec((tr, cols), lambda i: (i, 0))
    return _pcall(
        body, name=name, grid=(rws // tr,),
        in_specs=[pl.BlockSpec((N_DEV, tr, cols), lambda i: (0, i, 0)), blk, blk, blk],
        out_specs=[blk] * 4, out_shape=[_sds((rws, cols), F32)] * 4, compiler_params=_params(),
    )(parts, w, m, v)


def _wada_adamw(c_all, dmod_cols, w, m, v):
    rws, cols = w.shape
    tr = 256

    def body(c_ref, dm_ref, w_ref, m_ref, v_ref, g_ref, d_ref, mo_ref, vo_ref):
        cv = c_ref[...]
        sc = (cv * _sigmoid(cv)).astype(BF16)
        g = _dot_tn(sc, dm_ref[...].astype(BF16))
        g_ref[...] = g
        d_ref[...], mo_ref[...], vo_ref[...] = _adamw(w_ref[...], g, m_ref[...], v_ref[...])

    blk = pl.BlockSpec((tr, cols), lambda i: (i, 0))
    return _pcall(
        body, name="wada_adamw", grid=(rws // tr,),
        in_specs=[pl.BlockSpec((N_DEV, tr), lambda i: (0, i)), pl.BlockSpec((N_DEV, cols), lambda i: (0, 0)),
                  blk, blk, blk],
        out_specs=[blk] * 4, out_shape=[_sds((rws, cols), F32)] * 4, compiler_params=_params(),
    )(c_all, dmod_cols, w, m, v)


def _small_reduce(packs):
    def body(p_ref, o_ref):
        tot = p_ref[0]
        for dev in range(1, N_DEV):
            tot = tot + p_ref[dev]
        o_ref[...] = tot
        o_ref[16:17, :] = jnp.zeros((1, D), F32) + (0.5 / D) * jnp.sum(tot[16:17, :])

    return _pcall(body, name="small_reduce", out_shape=_sds((PACK_ROWS, D), F32))(packs)


def _adamw_many(ws, gs, ms, vs):
    k = len(ws)

    def body(*refs):
        for i in range(k):
            w_ref, g_ref, m_ref, v_ref = refs[i], refs[k + i], refs[2 * k + i], refs[3 * k + i]
            d_ref, mo_ref, vo_ref = refs[4 * k + i], refs[5 * k + i], refs[6 * k + i]
            d_ref[...], mo_ref[...], vo_ref[...] = _adamw(w_ref[...], g_ref[...], m_ref[...], v_ref[...])

    shp = [_sds(w.shape, F32) for w in ws]
    outs = _pcall(body, name="adamw_small", out_shape=shp * 3)(*ws, *gs, *ms, *vs)
    return outs[:k], outs[k:2 * k], outs[2 * k:]


def kernel(x, c, positions, w_ada, b_ada, norm1_w, w_in, conv_w, conv_b, dt_bias, a_log, d_skip, attn_sinks, ssm_norm_w, w_out, norm2_w, w_gate_up, w_down, final_norm_w, loss_target, m_w_ada, m_b_ada, m_norm1_w, m_w_in, m_conv_w, m_conv_b, m_dt_bias, m_a_log, m_d_skip, m_attn_sinks, m_ssm_norm_w, m_w_out, m_norm2_w, m_w_gate_up, m_w_down, m_final_norm_w, v_w_ada, v_b_ada, v_norm1_w, v_w_in, v_conv_w, v_conv_b, v_dt_bias, v_a_log, v_d_skip, v_attn_sinks, v_ssm_norm_w, v_w_out, v_norm2_w, v_w_gate_up, v_w_down, v_final_norm_w):
    s = x.shape[1]
    me = 4 * lax.axis_index("x") + 2 * lax.axis_index("y") + lax.axis_index("c")
    ada_cols = N_MOD * D // N_DEV

    c8 = jnp.pad(c, ((0, 7), (0, 0)))
    cw8 = jnp.pad(conv_w[0], ((0, 8 - CONVK), (0, 0)))
    w_in_t, m_w_in_t, v_w_in_t = jnp.transpose(w_in[0]), jnp.transpose(m_w_in[0]), jnp.transpose(v_w_in[0])
    w_gu_t, m_w_gu_t, v_w_gu_t = (jnp.transpose(w_gate_up[0]), jnp.transpose(m_w_gate_up[0]),
                                  jnp.transpose(v_w_gate_up[0]))
    g_c, g_in, g_cw = _exchange([c8, w_in_t.astype(BF16), cw8], [], "gather_in")
    c_all = g_c[:, 0, :]
    w_in_f = jnp.pad(g_in.reshape(IN_PROJ, D), ((0, INP - IN_PROJ), (0, 0)))
    conv_w8 = jnp.transpose(g_cw, (1, 0, 2)).reshape(8, D)

    b_cols = lax.dynamic_slice(b_ada, (0, me * ada_cols), (1, ada_cols))
    (g_mod,) = _exchange([_ada_fwd(c_all, w_ada[0], b_cols)], [], "gather_mod")
    mod = lax.dynamic_index_in_dim(g_mod, me, axis=1, keepdims=False).reshape(N_MOD, D)
    mod8 = jnp.pad(mod, ((0, 8 - N_MOD), (0, 0)))

    half = HD // 2
    inv_freq = ROPE_THETA ** (-jnp.arange(half, dtype=F32) / half)
    invf = jnp.tile(inv_freq, LB // half).reshape(1, LB)
    lanes = lambda a: jnp.pad(a, ((0, 0), (0, LB - a.shape[1])))
    ssm_p = jnp.pad(jnp.concatenate([lanes(dt_bias), lanes(a_log), lanes(d_skip)], axis=0), ((0, 5), (0, 0)))
    sinks8 = jnp.broadcast_to(attn_sinks.reshape(NQ, 1), (NQ, LB))

    xs, tgt, fnw = x[0], loss_target[0], final_norm_w.reshape(1, D)

    q, k, v, z, xbc, dtr, h1, cos, sin = _inproj_fwd(xs, positions[0].reshape(s, 1), invf, mod8, norm1_w, w_in_f)
    attn, yn, y, hs, (g_out, g_gu, g_down) = _mixer_fwd(
        q, k, v, sinks8, xbc, conv_w8, conv_b, dtr, ssm_p, z, ssm_norm_w,
        [w_out[0].astype(BF16), w_gu_t.astype(BF16), w_down[0].astype(BF16)])
    w_out_f = g_out.reshape(D, D)
    w_gu_f = g_gu.reshape(2 * DFF, D)
    w_down_f = g_down.reshape(DFF, D)
    x2, h2, mo, mix, gu, act, dx3, sm_f = _outproj_ffn_fwd_loss(attn, yn, xs, tgt, mod8, norm2_w, fnw, w_out_f, w_gu_f,
                                                                 w_down_f)

    dx2, dff, dgu, dmix, dattn, dyn, sm_b = _ffn_bwd(dx3, gu, x2, mo, mod8, norm2_w, w_gu_f, w_down_f, w_out_f)
    p_gu = _wgrad(dgu, h2, "wgrad_gate_up").reshape(N_DEV, 2 * DFF // N_DEV, D)
    p_down = _wgrad(act, dff, "wgrad_down").reshape(N_DEV, DFF // N_DEV, D)
    p_out = _wgrad(mix, dmix, "wgrad_out").reshape(N_DEV, D // N_DEV, D)
    dproj, dsink, sm_s, (r_gu, r_down, r_out) = _mixer_bwd(
        q, k, v, attn, dattn, cos, sin, sinks8, dyn, y, z, xbc, conv_w8, conv_b, dtr, ssm_p, ssm_norm_w, hs,
        [p_gu, p_down, p_out])
    p_in = _wgrad(dproj, h1, "wgrad_in")[:IN_PROJ].reshape(N_DEV, IN_PROJ // N_DEV, D)
    gx, sm_i, (r_in,) = _inproj_bwd(dproj, xs, dx2, mod8, norm1_w, w_in_f, [p_in])
    (g_pack,) = _exchange([_pack_small(sm_f, sm_b, sm_s, sm_i, dsink)], [], "gather_small")

    tot = _small_reduce(g_pack)
    loss = tot[16, 0]
    dmod_all = g_pack[:, 0:N_MOD, :].reshape(N_DEV, N_MOD * D)
    dmod_cols = lax.dynamic_slice(dmod_all, (0, me * ada_cols), (N_DEV, ada_cols))

    big = {
        "w_ada": _wada_adamw(c_all, dmod_cols, w_ada[0], m_w_ada[0], v_w_ada[0]),
        "w_in": [jnp.transpose(t) for t in _sum_adamw(r_in, w_in_t, m_w_in_t, v_w_in_t, "adamw_in")],
        "w_out": _sum_adamw(r_out, w_out[0], m_w_out[0], v_w_out[0], "adamw_out"),
        "w_gate_up": [jnp.transpose(t) for t in _sum_adamw(r_gu, w_gu_t, m_w_gu_t, v_w_gu_t, "adamw_gate_up")],
        "w_down": _sum_adamw(r_down, w_down[0], m_w_down[0], v_w_down[0], "adamw_down"),
    }
    small_names = ["b_ada", "norm1_w", "conv_w", "conv_b", "dt_bias", "a_log", "d_skip", "attn_sinks", "ssm_norm_w",
                   "norm2_w", "final_norm_w"]
    row15 = tot[15:16, :]
    small_g = {
        "b_ada": tot[0:N_MOD, :].reshape(1, N_MOD * D),
        "norm1_w": tot[6:7, :],
        "conv_w": lax.dynamic_slice(tot[10:14, :], (0, me * (D // N_DEV)), (CONVK, D // N_DEV)),
        "conv_b": tot[9:10, :],
        "dt_bias": row15[:, 0:NH],
        "a_log": row15[:, LB:LB + NH],
        "d_skip": row15[:, 2 * LB:2 * LB + NH],
        "attn_sinks": row15[:, 3 * LB:3 * LB + NQ],
        "ssm_norm_w": tot[14:15, 0:SW],
        "norm2_w": tot[7:8, :],
        "final_norm_w": tot[8:9, :],
    }
    small_w = {"b_ada": b_ada, "norm1_w": norm1_w, "conv_w": conv_w[0], "conv_b": conv_b, "dt_bias": dt_bias,
               "a_log": a_log, "d_skip": d_skip, "attn_sinks": attn_sinks, "ssm_norm_w": ssm_norm_w,
               "norm2_w": norm2_w, "final_norm_w": final_norm_w.reshape(1, D)}
    small_m = {"b_ada": m_b_ada, "norm1_w": m_norm1_w, "conv_w": m_conv_w[0], "conv_b": m_conv_b,
               "dt_bias": m_dt_bias, "a_log": m_a_log, "d_skip": m_d_skip, "attn_sinks": m_attn_sinks,
               "ssm_norm_w": m_ssm_norm_w, "norm2_w": m_norm2_w, "final_norm_w": m_final_norm_w.reshape(1, D)}
    small_v = {"b_ada": v_b_ada, "norm1_w": v_norm1_w, "conv_w": v_conv_w[0], "conv_b": v_conv_b,
               "dt_bias": v_dt_bias, "a_log": v_a_log, "d_skip": v_d_skip, "attn_sinks": v_attn_sinks,
               "ssm_norm_w": v_ssm_norm_w, "norm2_w": v_norm2_w, "final_norm_w": v_final_norm_w.reshape(1, D)}
    s_d, s_m, s_v = _adamw_many([small_w[k] for k in small_names], [small_g[k] for k in small_names],
                                [small_m[k] for k in small_names], [small_v[k] for k in small_names])

    order = ["w_ada", "b_ada", "norm1_w", "w_in", "conv_w", "conv_b", "dt_bias", "a_log", "d_skip", "attn_sinks",
             "ssm_norm_w", "w_out", "norm2_w", "w_gate_up", "w_down", "final_norm_w"]
    lead = {"w_ada", "w_in", "conv_w", "w_out", "w_gate_up", "w_down"}
    grads, deltas, new_m, new_v = [], [], [], []
    for name in order:
        if name in big:
            g, d, m2, v2 = big[name]
        else:
            i = small_names.index(name)
            g, d, m2, v2 = small_g[name], s_d[i], s_m[i], s_v[i]
        if name in lead:
            g, d, m2, v2 = g[None], d[None], m2[None], v2[None]
        if name == "final_norm_w":
            g, d, m2, v2 = g.reshape(D), d.reshape(D), m2.reshape(D), v2.reshape(D)
        grads.append(g)
        deltas.append(d)
        new_m.append(m2)
        new_v.append(v2)
    return (loss, gx[None], *grads, *deltas, *new_m, *new_v)
```

```python
import functools
import math

import jax
import jax.numpy as jnp
from jax import lax
from jax.experimental import pallas as pl
from jax.experimental.pallas import tpu as pltpu

F32 = jnp.float32
BF16 = jnp.bfloat16

N_DEV = 8
D = 1024
HD = 64
NQ = 8
AW = 512
KVW = 128
SW = 512
NST = 128
NH = 8
LB = 128
CONVK = 4
DFF = 2816
N_MOD = 6
IN_PROJ = 2312
INP = 2432
O_Q, O_K, O_V, O_Z, O_XBC, O_DT = 0, 512, 640, 768, 1280, 2304
ZXD = INP - O_Z
EPS = 1e-6
NEG = -1e30
ROPE_THETA = 10000.0
VMEM_LIMIT = 56 * 1024 * 1024

ADAM_LR = 0.001
ADAM_B1 = 0.9
ADAM_B2 = 0.999
ADAM_EPS = 1e-08
ADAM_WD = 0.01
ADAM_STEP = 10

NT_DIMS = (((1,), (1,)), ((), ()))
TN_DIMS = (((0,), (0,)), ((), ()))


def _pcall(body, **kw):
    return pl.pallas_call(body, **kw)


def _sds(shape, dtype):
    return jax.ShapeDtypeStruct(shape, dtype)


def _params(n_grid=1):
    return pltpu.CompilerParams(dimension_semantics=("arbitrary",) * n_grid, vmem_limit_bytes=VMEM_LIMIT)


def _const(shape):
    return pl.BlockSpec(shape, lambda *_: (0,) * len(shape), pipeline_mode=pl.Buffered(1))


def _largest_divisor(n, candidates):
    for cand in candidates:
        if n % cand == 0:
            return cand
    raise ValueError(f"no tile in {candidates} divides {n}")


def _rows(t, w):
    return pl.BlockSpec((t, w), lambda i: (i, 0))


def _dot(a, b):
    return jnp.dot(a, b, preferred_element_type=F32)


def _dot_nt(a, b):
    return lax.dot_general(a, b, NT_DIMS, preferred_element_type=F32)


def _dot_tn(a, b):
    return lax.dot_general(a, b, TN_DIMS, preferred_element_type=F32)


def _sigmoid(v):
    return 1.0 / (1.0 + jnp.exp(-v))


def _softplus(v):
    return jnp.maximum(v, 0.0) + jnp.log1p(jnp.exp(-jnp.abs(v)))


def _rope_sign_mask(shape):
    lane = lax.broadcasted_iota(jnp.int32, shape, 1)
    return (lane % HD) < (HD // 2)


def _rope(t, cs, sn, inverse):
    r_dn = pltpu.roll(t, HD // 2, 1)
    r_up = pltpu.roll(t, LB - HD // 2, 1)
    first = _rope_sign_mask(t.shape)
    if inverse:
        rot = jnp.where(first, r_up, -r_dn)
    else:
        rot = jnp.where(first, -r_up, r_dn)
    return t * cs + rot * sn


def _norm_mod_fwd(xv, nw, shift, scale):
    r = lax.rsqrt(jnp.mean(xv * xv, axis=-1, keepdims=True) + EPS)
    xh = xv * r
    return (xh * nw) * (1.0 + scale) + shift


def _norm_mod_bwd(xv, dh, nw, scale):
    r = lax.rsqrt(jnp.mean(xv * xv, axis=-1, keepdims=True) + EPS)
    xh = xv * r
    xn = xh * nw
    d_shift = jnp.sum(dh, axis=0, keepdims=True)
    d_scale = jnp.sum(dh * xn, axis=0, keepdims=True)
    dxn = dh * (1.0 + scale)
    d_w = jnp.sum(dxn * xh, axis=0, keepdims=True)
    dxh = dxn * nw
    dx = r * (dxh - xh * jnp.mean(dxh * xh, axis=-1, keepdims=True))
    return dx, d_shift, d_scale, d_w


def _inproj_fwd(x, pos, invf, mod8, n1w, w_in):
    s = x.shape[0]
    tt = min(512, s)

    def body(x_ref, pos_ref, invf_ref, mod_ref, nw_ref, w_ref,
             q_ref, k_ref, v_ref, z_ref, xbc_ref, dtr_ref, h1_ref, cos_ref, sin_ref):
        h = _norm_mod_fwd(x_ref[...], nw_ref[...], mod_ref[0:1, :], mod_ref[1:2, :])
        hb = h.astype(BF16)
        h1_ref[...] = hb
        proj = _dot_nt(hb, w_ref[...])
        ang = pos_ref[...].astype(F32) * invf_ref[...]
        cs = jnp.cos(ang)
        sn = jnp.sin(ang)
        cos_ref[...] = cs
        sin_ref[...] = sn
        for a in range(AW // LB):
            q_ref[:, a * LB:(a + 1) * LB] = _rope(proj[:, O_Q + a * LB:O_Q + (a + 1) * LB], cs, sn, False).astype(BF16)
        k_ref[...] = _rope(proj[:, O_K:O_V], cs, sn, False).astype(BF16)
        v_ref[...] = proj[:, O_V:O_Z].astype(BF16)
        z_ref[...] = proj[:, O_Z:O_XBC]
        xbc_ref[...] = proj[:, O_XBC:O_DT]
        dtr_ref[...] = proj[:, O_DT:INP]

    return _pcall(
        body, name="inproj_fwd", grid=(s // tt,),
        in_specs=[_rows(tt, D), _rows(tt, 1), _const((1, LB)), _const((8, D)), _const((1, D)), _const((INP, D))],
        out_specs=[_rows(tt, AW), _rows(tt, KVW), _rows(tt, KVW), _rows(tt, SW), _rows(tt, D), _rows(tt, LB),
                   _rows(tt, D), _rows(tt, LB), _rows(tt, LB)],
        out_shape=[_sds((s, AW), BF16), _sds((s, KVW), BF16), _sds((s, KVW), BF16), _sds((s, SW), F32),
                   _sds((s, D), F32), _sds((s, LB), F32), _sds((s, D), BF16), _sds((s, LB), F32), _sds((s, LB), F32)],
        compiler_params=_params(),
    )(x, pos, invf, mod8, n1w, w_in)


QPG = 4
ATT_SCALE = 1.0 / math.sqrt(HD)


def _stack_heads(val, g):
    return jnp.concatenate([val[:, (QPG * g + hh) * HD:(QPG * g + hh + 1) * HD] for hh in range(QPG)], axis=0)


def _unstack_heads(groups):
    pieces = [grp[hh * LB:(hh + 1) * LB, :] for grp in groups for hh in range(QPG)]
    return [jnp.concatenate(pieces[2 * a:2 * a + 2], axis=1) for a in range(NQ // 2)]


def _upper_mask():
    row = lax.broadcasted_iota(jnp.int32, (QPG * LB, LB), 0)
    col = lax.broadcasted_iota(jnp.int32, (QPG * LB, LB), 1)
    return col > (row % LB)


def _sink_col(sinks, g):
    return jnp.concatenate([jnp.broadcast_to(sinks[QPG * g + hh:QPG * g + hh + 1, 0:1], (LB, 1))
                            for hh in range(QPG)], axis=0)


def _band(upper, prev_part, cur_part):
    return jnp.where(upper, prev_part, cur_part)


def _attn_scores(n, qg, kcat, upper):
    sp = _dot_nt(qg, kcat[0:LB, :]) * ATT_SCALE
    sc = _dot_nt(qg, kcat[LB:2 * LB, :]) * ATT_SCALE
    return _band(upper, jnp.where(n > 0, sp, NEG), sc)


def _attn_softmax(comb, sink):
    m = jnp.maximum(jnp.max(comb, axis=-1, keepdims=True), sink)
    p = jnp.exp(comb - m)
    es = jnp.exp(sink - m)
    denom = jnp.sum(p, axis=-1, keepdims=True) + es
    return p / denom, es / denom


def _attn_fwd_block(n, q_ref, kp_ref, kc_ref, vp_ref, vc_ref, sink_ref, o_ref):
    qv = q_ref[...]
    kcat = jnp.concatenate([kp_ref[...], kc_ref[...]], axis=0)
    vcat = jnp.concatenate([vp_ref[...], vc_ref[...]], axis=0)
    sinks = sink_ref[...]
    upper = _upper_mask()
    outs = []
    for g in range(NQ // QPG):
        sl = slice(g * HD, (g + 1) * HD)
        probs, _ = _attn_softmax(_attn_scores(n, _stack_heads(qv, g), kcat[:, sl], upper), _sink_col(sinks, g))
        outs.append(_dot(jnp.where(upper, probs, 0.0).astype(BF16), vcat[0:LB, sl])
                    + _dot(jnp.where(upper, 0.0, probs).astype(BF16), vcat[LB:2 * LB, sl]))
    for g, grp in enumerate(outs):
        for hh in range(QPG):
            h = QPG * g + hh
            o_ref[:, h * HD:(h + 1) * HD] = grp[hh * LB:(hh + 1) * LB, :].astype(BF16)


def _cumsum_rows(a, reverse):
    row = lax.broadcasted_iota(jnp.int32, a.shape, 0)
    step = 1
    while step < LB:
        if reverse:
            a = a + jnp.where(row < LB - step, pltpu.roll(a, LB - step, 0), 0.0)
        else:
            a = a + jnp.where(row >= step, pltpu.roll(a, step, 0), 0.0)
        step *= 2
    return a


SUB = 8


def _conv_shifts(tail, cur):
    row = lax.broadcasted_iota(jnp.int32, tail.shape, 0)
    out = [cur]
    for j in range(1, CONVK):
        rolled = pltpu.roll(cur, j, 0)
        top = jnp.where(row < j, pltpu.roll(tail, j, 0), rolled[0:SUB, :])
        out.append(jnp.concatenate([top, rolled[SUB:, :]], axis=0))
    return out


def _conv_advances(du, head):
    row = lax.broadcasted_iota(jnp.int32, head.shape, 0)
    out = []
    for j in range(1, CONVK):
        rolled = pltpu.roll(du, LB - j, 0)
        bottom = jnp.where(row >= SUB - j, pltpu.roll(head, SUB - j, 0), rolled[LB - SUB:, :])
        out.append(jnp.concatenate([rolled[:LB - SUB, :], bottom], axis=0))
    return out


def _split(v, terms):
    out = []
    for _ in range(terms - 1):
        t = v.astype(BF16)
        out.append(t)
        v = v - t.astype(F32)
    out.append(v.astype(BF16))
    return out


def _dot_sel(v, sel, terms):
    parts = [_dot(t, sel) for t in _split(v, terms)]
    return functools.reduce(lambda a, b: a + b, parts)


def _dot_nt_sel(v, sel, terms):
    parts = [_dot_nt(t, sel) for t in _split(v, terms)]
    return functools.reduce(lambda a, b: a + b, parts)


def _ssd_pre(xt_ref, xc_ref, cw_ref, cb_ref, dtr_ref, sp_ref, n):
    cur = xc_ref[...]
    tail = jnp.where(n > 0, xt_ref[...], 0.0)
    sh = _conv_shifts(tail, cur)
    u = cb_ref[...] + cw_ref[CONVK - 1:CONVK, :] * sh[0]
    for j in range(1, CONVK):
        u = u + cw_ref[CONVK - 1 - j:CONVK - j, :] * sh[j]
    sg_u = _sigmoid(u)
    xc = u * sg_u
    pre = dtr_ref[...] + sp_ref[0:1, :]
    dt = _softplus(pre)
    a_neg = -jnp.exp(sp_ref[1:2, :])
    acs = _cumsum_rows(dt * a_neg, False)
    return sh, u, sg_u, xc, pre, dt, a_neg, acs


def _gated_norm_fwd(y, z, nw):
    sz = z * _sigmoid(z)
    yz = y * sz
    parts = []
    for g in range(2):
        t = yz[:, g * 256:(g + 1) * 256]
        parts.append(t * lax.rsqrt(jnp.mean(t * t, axis=-1, keepdims=True) + EPS))
    return jnp.concatenate(parts, axis=1) * nw


HPG = 4
GW = HPG * HD


class _SsdChunk:
    def __init__(self, xc, dt, acs, spv, e64, e128):
        self.e64, self.e128 = e64, e128
        self.acs_t = jnp.transpose(acs)
        alast = acs[LB - 1:LB, :]
        self.e_all = jnp.exp(acs)
        self.dte_all = jnp.exp(alast - acs)
        self.elast = jnp.exp(alast)
        wide = _dot_sel(jnp.concatenate([dt, self.e_all, self.dte_all], axis=0), e64, 1)
        self.dt_x, self.e_x, self.dte_x = wide[0:LB], wide[LB:2 * LB], wide[2 * LB:3 * LB]
        self.dsk_x = _dot_sel(spv, e64, 3)[2:3, :]
        ac_x = _dot_sel(acs, e128, 3)
        row = lax.broadcasted_iota(jnp.int32, (HPG * LB, LB), 0)
        col = lax.broadcasted_iota(jnp.int32, (HPG * LB, LB), 1)
        causal = (row % LB) >= col
        lane = lax.broadcasted_iota(jnp.int32, (LB, GW), 1)
        self.head_lanes = [(lane >= hh * HD) & (lane < (hh + 1) * HD) for hh in range(HPG)]
        self.xs, self.xdt, self.b, self.c, self.bb, self.cb16, self.cbm, self.dm_st, self.m_st = ([] for _ in range(9))
        for g in range(2):
            heads = range(HPG * g, HPG * (g + 1))
            ac_st = jnp.concatenate([ac_x[:, j * LB:(j + 1) * LB] for j in heads], axis=0)
            ar_st = jnp.concatenate([jnp.broadcast_to(self.acs_t[j:j + 1, :], (LB, LB)) for j in heads], axis=0)
            dm_st = jnp.exp(jnp.where(causal, ac_st - ar_st, NEG))
            bg = xc[:, SW + g * NST:SW + (g + 1) * NST]
            cg = xc[:, SW + 2 * NST + g * NST:SW + 2 * NST + (g + 1) * NST]
            bgb, cgb = bg.astype(BF16), cg.astype(BF16)
            cbm = _dot_nt(cgb, bgb)
            xs_g = xc[:, g * GW:(g + 1) * GW]
            self.xs.append(xs_g)
            self.xdt.append(xs_g * self.dt_x[:, g * GW:(g + 1) * GW])
            self.b.append(bg)
            self.c.append(cg)
            self.bb.append(bgb)
            self.cb16.append(cgb)
            self.cbm.append(cbm)
            self.dm_st.append(dm_st)
            self.m_st.append(jnp.concatenate([cbm] * HPG, axis=0) * dm_st)

    def elast_rows(self, g):
        return jnp.concatenate([jnp.broadcast_to(self.elast[:, j:j + 1], (HD, NST))
                                for j in range(HPG * g, HPG * (g + 1))], axis=0)

    def diag_blocks(self, stacked):
        out = stacked[(HPG - 1) * LB:HPG * LB, :]
        for hh in range(HPG - 2, -1, -1):
            out = jnp.where(self.head_lanes[hh], stacked[hh * LB:(hh + 1) * LB, :], out)
        return out

    def block_diag(self, v):
        return jnp.concatenate([jnp.where(self.head_lanes[hh], v, 0.0) for hh in range(HPG)], axis=0)


def _ssd_fwd_block(n, xt_ref, xc_ref, cw_ref, cb_ref, dtr_ref, sp_ref, z_ref, nw_ref, e64_ref, e128_ref,
                   yn_ref, y_ref, hs_ref, h_scr):
    @pl.when(n == 0)
    def _():
        h_scr[...] = jnp.zeros_like(h_scr)

    h_all = h_scr[...]
    hs_ref[0] = h_all
    _, _, _, xc, _, dt, _, acs = _ssd_pre(xt_ref, xc_ref, cw_ref, cb_ref, dtr_ref, sp_ref, n)
    ck = _SsdChunk(xc, dt, acs, sp_ref[...], e64_ref[...], e128_ref[...])
    ys, hn = [], []
    for g in range(2):
        gl = slice(g * GW, (g + 1) * GW)
        xdt = ck.xdt[g]
        hg = h_all[gl, :]
        y_diag = ck.diag_blocks(_dot(ck.m_st[g].astype(BF16), xdt.astype(BF16)))
        y_off = ck.e_x[:, gl] * _dot_nt(ck.cb16[g], hg.astype(BF16))
        ys.append(y_diag + y_off + ck.xs[g] * ck.dsk_x[:, gl])
        hn.append(hg * ck.elast_rows(g) + _dot_tn((xdt * ck.dte_x[:, gl]).astype(BF16), ck.bb[g]))
    h_scr[...] = jnp.concatenate(hn, axis=0)
    y = jnp.concatenate(ys, axis=1)
    y_ref[...] = y
    yn_ref[...] = _gated_norm_fwd(y, z_ref[...], nw_ref[...]).astype(BF16)


def _mixer_fwd(q, k, v, sinks8, xbc, conv_w8, conv_b, dtr, ssm_p, z, nw, gathers):
    s = q.shape[0]
    nb = s // LB
    prev = lambda n: (jnp.maximum(n - 1, 0), 0)
    cur = lambda n: (n, 0)
    items, ex_shapes, n_g = _exchange_items(gathers, [])
    ne = len(items)

    n_in = 16
    e64, e128 = _head_expanders()

    def body(*refs):
        a_in, s_in, ex_in = refs[:6], refs[6:n_in], refs[n_in:n_in + ne]
        o_ref, yn_ref, y_ref, hs_ref = refs[n_in + ne:n_in + 4 + ne]
        ex_out = refs[n_in + 4 + ne:n_in + 4 + 2 * ne]
        h_scr = refs[n_in + 4 + 2 * ne]
        sems = refs[n_in + 5 + 2 * ne:]
        n = pl.program_id(0)

        @pl.when(n == 0)
        def _():
            _Exchange(n_g, ex_in, ex_out, sems).start()

        _attn_fwd_block(n, *a_in, o_ref)
        _ssd_fwd_block(n, *s_in, yn_ref, y_ref, hs_ref, h_scr)

        @pl.when(n == nb - 1)
        def _():
            _Exchange(n_g, ex_in, ex_out, sems).finish()

    any_spec = pl.BlockSpec(memory_space=pl.ANY)
    tail = pl.BlockSpec((SUB, D), lambda n: (jnp.maximum(n * (LB // SUB) - 1, 0), 0))
    outs = _pcall(
        body, name="mixer_fwd", grid=(nb,),
        in_specs=[pl.BlockSpec((LB, AW), cur), pl.BlockSpec((LB, KVW), prev), pl.BlockSpec((LB, KVW), cur),
                  pl.BlockSpec((LB, KVW), prev), pl.BlockSpec((LB, KVW), cur), _const((8, LB)),
                  tail, pl.BlockSpec((LB, D), cur), _const((8, D)), _const((1, D)),
                  pl.BlockSpec((LB, LB), cur), _const((8, LB)), pl.BlockSpec((LB, SW), cur), _const((1, SW)),
                  _const(e64.shape), _const(e128.shape)]
        + [any_spec] * ne,
        out_specs=[pl.BlockSpec((LB, AW), cur), pl.BlockSpec((LB, SW), cur), pl.BlockSpec((LB, SW), cur),
                   pl.BlockSpec((1, NH * HD, NST), lambda n: (n, 0, 0))] + [any_spec] * ne,
        out_shape=[_sds((s, AW), BF16), _sds((s, SW), BF16), _sds((s, SW), F32), _sds((nb, NH * HD, NST), F32)]
        + ex_shapes,
        scratch_shapes=[pltpu.VMEM((NH * HD, NST), F32)] + _exchange_sems(ne),
        compiler_params=_params(),
    )(q, k, k, v, v, sinks8, xbc, xbc, conv_w8, conv_b, dtr, ssm_p, z, nw, e64, e128, *items)
    return outs[0], outs[1], outs[2], outs[3], outs[4:]


def _head_expanders():
    j = lax.broadcasted_iota(jnp.int32, (LB, NH * HD), 0)
    e64 = (lax.broadcasted_iota(jnp.int32, (LB, NH * HD), 1) // HD == j).astype(BF16)
    j = lax.broadcasted_iota(jnp.int32, (LB, NH * LB), 0)
    e128 = (lax.broadcasted_iota(jnp.int32, (LB, NH * LB), 1) // LB == j).astype(BF16)
    return e64, e128


def _outproj_ffn_fwd_loss(attn, yn, x, tgt, mod8, n2w, fnw, w_out, w_gu_t, w_down):
    s = x.shape[0]
    tf = min(256, s)

    def body(a_ref, y_ref, x_ref, t_ref, mod_ref, nw_ref, fw_ref, wo_ref, wgu_ref, wd_ref,
             x2_ref, h2_ref, mo_ref, mix_ref, gu_ref, act_ref, dx3_ref, sm_ref):
        i = pl.program_id(0)

        @pl.when(i == 0)
        def _():
            sm_ref[...] = jnp.zeros_like(sm_ref)

        mix = jnp.concatenate([a_ref[...], y_ref[...]], axis=1)
        mix_ref[...] = mix
        mo = _dot(mix, wo_ref[...])
        mo_ref[...] = mo.astype(BF16)
        x2 = x_ref[...] + mod_ref[2:3, :] * mo
        x2_ref[...] = x2
        h2 = _norm_mod_fwd(x2, nw_ref[...], mod_ref[3:4, :], mod_ref[4:5, :]).astype(BF16)
        h2_ref[...] = h2
        gu = _dot_nt(h2, wgu_ref[...])
        gu_ref[...] = gu.astype(BF16)
        g = gu[:, :DFF]
        act = (g * _sigmoid(g) * gu[:, DFF:]).astype(BF16)
        act_ref[...] = act
        ff = _dot(act, wd_ref[...])
        x3 = x2 + mod_ref[5:6, :] * ff
        r = lax.rsqrt(jnp.mean(x3 * x3, axis=-1, keepdims=True) + EPS)
        xh = x3 * r
        fw = fw_ref[...]
        err = xh * fw - t_ref[...]
        dy = err * (1.0 / D)
        dxh = dy * fw
        dx3 = r * (dxh - xh * jnp.mean(dxh * xh, axis=-1, keepdims=True))
        dx3_ref[...] = dx3
        sm_ref[0:1, :] += jnp.sum(dx3 * ff, axis=0, keepdims=True)
        sm_ref[1:2, :] += jnp.sum(dy * xh, axis=0, keepdims=True)
        sm_ref[2:3, :] += jnp.sum(err * err, axis=0, keepdims=True)

    return _pcall(
        body, name="outproj_ffn_fwd_loss", grid=(s // tf,),
        in_specs=[_rows(tf, AW), _rows(tf, SW), _rows(tf, D), _rows(tf, D), _const((8, D)), _const((1, D)),
                  _const((1, D)), _const((D, D)), _const((2 * DFF, D)), _const((DFF, D))],
        out_specs=[_rows(tf, D), _rows(tf, D), _rows(tf, D), _rows(tf, D), _rows(tf, 2 * DFF), _rows(tf, DFF),
                   _rows(tf, D), pl.BlockSpec((8, D), lambda i: (0, 0))],
        out_shape=[_sds((s, D), F32), _sds((s, D), BF16), _sds((s, D), BF16), _sds((s, D), BF16),
                   _sds((s, 2 * DFF), BF16), _sds((s, DFF), BF16), _sds((s, D), F32), _sds((8, D), F32)],
        compiler_params=_params(),
    )(attn, yn, x, tgt, mod8, n2w, fnw, w_out, w_gu_t, w_down)


def _ffn_bwd(dx3, gu, x2, mixout, mod8, n2w, w_gu, w_down, w_out):
    s = x2.shape[0]
    tb = min(256, s)

    def body(dx3_ref, gu_ref, x2_ref, mo_ref, mod_ref, nw_ref, wgu_ref, wd_ref, wo_ref,
             dx2_ref, dff_ref, dgu_ref, dmix_ref, dattn_ref, dyn_ref, sm_ref):
        i = pl.program_id(0)

        @pl.when(i == 0)
        def _():
            sm_ref[...] = jnp.zeros_like(sm_ref)

        dx3 = dx3_ref[...]
        dff = (dx3 * mod_ref[5:6, :]).astype(BF16)
        dff_ref[...] = dff
        dact = _dot_nt(dff, wd_ref[...])
        g = gu_ref[:, :DFF].astype(F32)
        u = gu_ref[:, DFF:].astype(F32)
        sg = _sigmoid(g)
        dgu = jnp.concatenate([dact * u * sg * (1.0 + g * (1.0 - sg)), dact * g * sg], axis=1).astype(BF16)
        dgu_ref[...] = dgu
        dh2 = _dot(dgu, wgu_ref[...])
        dxn, d_shift, d_scale, d_w = _norm_mod_bwd(x2_ref[...], dh2, nw_ref[...], mod_ref[4:5, :])
        dx2 = dx3 + dxn
        dx2_ref[...] = dx2
        sm_ref[0:1, :] += d_shift
        sm_ref[1:2, :] += d_scale
        sm_ref[2:3, :] += d_w
        sm_ref[3:4, :] += jnp.sum(dx2 * mo_ref[...].astype(F32), axis=0, keepdims=True)
        dmix = (dx2 * mod_ref[2:3, :]).astype(BF16)
        dmix_ref[...] = dmix
        dmi = _dot_nt(dmix, wo_ref[...])
        dattn_ref[...] = dmi[:, :AW].astype(BF16)
        dyn_ref[...] = dmi[:, AW:]

    return _pcall(
        body, name="ffn_bwd", grid=(s // tb,),
        in_specs=[_rows(tb, D), _rows(tb, 2 * DFF), _rows(tb, D), _rows(tb, D), _const((8, D)), _const((1, D)),
                  _const((2 * DFF, D)), _const((DFF, D)), _const((D, D))],
        out_specs=[_rows(tb, D), _rows(tb, D), _rows(tb, 2 * DFF), _rows(tb, D), _rows(tb, AW), _rows(tb, SW),
                   pl.BlockSpec((8, D), lambda i: (0, 0))],
        out_shape=[_sds((s, D), F32), _sds((s, D), BF16), _sds((s, 2 * DFF), BF16), _sds((s, D), BF16),
                   _sds((s, AW), BF16), _sds((s, SW), F32), _sds((8, D), F32)],
        compiler_params=_params(),
    )(dx3, gu, x2, mixout, mod8, n2w, w_gu, w_down, w_out)


def _ssd_bwd_block(i, n, *refs):
    def run(dyn_ref, y_ref, z_ref, xt_ref, xc_ref, cw_ref, cb_ref, dtr_ref, sp_ref, nw_ref, hs_ref, e64_ref, e128_ref,
            dzxd_ref, sm_ref, dh_scr, dun_scr):
        @pl.when(i == 0)
        def _():
            dh_scr[...] = jnp.zeros_like(dh_scr)
            dun_scr[...] = jnp.zeros_like(dun_scr)
            sm_ref[...] = jnp.zeros_like(sm_ref)

        sh, u, sg_u, xc, pre, dt, a_neg, acs = _ssd_pre(xt_ref, xc_ref, cw_ref, cb_ref, dtr_ref, sp_ref, n)
        ck = _SsdChunk(xc, dt, acs, sp_ref[...], e64_ref[...], e128_ref[...])
        h_all = hs_ref[0]
        dh_all = dh_scr[...]
        riota = lax.broadcasted_iota(jnp.int32, (LB, LB), 0)
        lane1 = lax.broadcasted_iota(jnp.int32, (1, LB), 1)

        z = z_ref[...]
        y = y_ref[...]
        sgz = _sigmoid(z)
        sz = z * sgz
        yz = y * sz
        nwv = nw_ref[...]
        dyn_v = dyn_ref[...]
        dyhat = dyn_v * nwv
        yhat_parts, dyz_parts = [], []
        for g in range(2):
            gs = slice(g * 256, (g + 1) * 256)
            t = yz[:, gs]
            rg = lax.rsqrt(jnp.mean(t * t, axis=-1, keepdims=True) + EPS)
            yh = t * rg
            dyh = dyhat[:, gs]
            yhat_parts.append(yh)
            dyz_parts.append(rg * (dyh - yh * jnp.mean(dyh * yh, axis=-1, keepdims=True)))
        yhat = jnp.concatenate(yhat_parts, axis=1)
        dyz = jnp.concatenate(dyz_parts, axis=1)
        sm_ref[5:6, 0:SW] += jnp.sum(dyn_v * yhat, axis=0, keepdims=True)
        dy = dyz * sz
        dzxd_ref[:, 0:SW] = (dyz * y * sgz * (1.0 + z * (1.0 - sgz))).astype(BF16)

        cat = lambda parts: jnp.concatenate(parts, axis=1)
        dxs, dbs, dcs, dhp, g_cat, de_x, ddte_x, ddt_x, ddsk_x = ([] for _ in range(9))
        dacs_t = jnp.zeros((LB, LB), F32)
        hsum = jnp.zeros((1, LB), F32)
        for g in range(2):
            gl = slice(g * GW, (g + 1) * GW)
            xs_g, xdt, bgb, cgb = ck.xs[g], ck.xdt[g], ck.bb[g], ck.cb16[g]
            m_st, dm_st = ck.m_st[g], ck.dm_st[g]
            dt_x, e_x, dte_x = ck.dt_x[:, gl], ck.e_x[:, gl], ck.dte_x[:, gl]
            xdtb = xdt.astype(BF16)
            hg, dhn = h_all[gl, :], dh_all[gl, :]
            hb, dhnb = hg.astype(BF16), dhn.astype(BF16)
            dy_g = dy[:, gl]
            ddsk_x.append(jnp.sum(dy_g * xs_g, axis=0, keepdims=True))
            dy_bd = ck.block_diag(dy_g).astype(BF16)
            dm4 = _dot_nt(dy_bd, xdtb)
            dxdt = _dot_tn(m_st.astype(BF16), dy_bd)
            gmat = dm4 * m_st
            dcbm = dm4 * dm_st
            dcb = dcbm[0:LB] + dcbm[LB:2 * LB] + dcbm[2 * LB:3 * LB] + dcbm[3 * LB:4 * LB]
            g_cat.append(cat([gmat[hh * LB:(hh + 1) * LB, :] for hh in range(HPG)]))
            for hh in range(HPG):
                j = HPG * g + hh
                col_sum = jnp.sum(gmat[hh * LB:(hh + 1) * LB, :], axis=0, keepdims=True)
                dacs_t = dacs_t - jnp.where(riota == j, col_sum, 0.0)
                hsl = slice(hh * HD, (hh + 1) * HD)
                hsum = hsum + jnp.where(lane1 == j, jnp.sum(dhn[hsl, :] * hg[hsl, :]), 0.0)
            dchb = (dy_g * e_x).astype(BF16)
            dcg = _dot(dchb, hb)
            dh_prev = _dot_tn(dchb, cgb)
            de_x.append(dy_g * _dot_nt(cgb, hb))
            dxs_s = _dot_nt(bgb, dhnb)
            dbg = _dot((xdt * dte_x).astype(BF16), dhnb)
            dxdt = dxdt + dxs_s * dte_x
            ddte_x.append(dxs_s * xdt)
            dhp.append(dhn * ck.elast_rows(g) + dh_prev)
            dxs.append(dy_g * ck.dsk_x[:, gl] + dxdt * dt_x)
            ddt_x.append(dxdt * xs_g)
            dcbb = dcb.astype(BF16)
            dbs.append(dbg + _dot_tn(dcbb, cgb))
            dcs.append(dcg + _dot(dcbb, bgb))
        dh_scr[...] = jnp.concatenate(dhp, axis=0)
        red = _dot_nt_sel(jnp.concatenate([cat(de_x), cat(ddte_x), cat(ddt_x)], axis=0), ck.e64, 2)
        de_c, ddte_c, ddt_c = red[0:LB], red[LB:2 * LB], red[2 * LB:3 * LB]
        ddsk = _dot_nt_sel(jnp.broadcast_to(cat(ddsk_x), (SUB, NH * HD)), ck.e64, 2)[0:1, :]
        t1 = ddte_c * ck.dte_all
        dalast = jnp.sum(t1, axis=0, keepdims=True) + hsum * ck.elast
        dacs = (_dot_nt_sel(cat(g_cat), ck.e128, 2) + de_c * ck.e_all - t1 + jnp.transpose(dacs_t)
                + jnp.where(riota == LB - 1, dalast, 0.0))
        da = _cumsum_rows(dacs, True)
        ddt = ddt_c + da * a_neg
        da_log = jnp.sum(da * dt, axis=0, keepdims=True) * a_neg
        ddtr = ddt * _sigmoid(pre)
        dzxd_ref[:, SW + D:ZXD] = ddtr.astype(BF16)
        sm_ref[6:7, 0:LB] += jnp.sum(ddtr, axis=0, keepdims=True)
        sm_ref[6:7, LB:2 * LB] += da_log
        sm_ref[6:7, 2 * LB:3 * LB] += ddsk

        du = cat(dxs + dbs + dcs) * (sg_u * (1.0 + u * (1.0 - sg_u)))
        sm_ref[0:1, :] += jnp.sum(du, axis=0, keepdims=True)
        for k in range(CONVK):
            sm_ref[1 + k:2 + k, :] += jnp.sum(du * sh[CONVK - 1 - k], axis=0, keepdims=True)
        adv = _conv_advances(du, dun_scr[...])
        dxbc = cw_ref[CONVK - 1:CONVK, :] * du
        for j in range(1, CONVK):
            dxbc = dxbc + cw_ref[CONVK - 1 - j:CONVK - j, :] * adv[j - 1]
        dun_scr[...] = du[0:SUB, :]
        dzxd_ref[:, SW:SW + D] = dxbc.astype(BF16)

    run(*refs)


def _attn_bwd_block(i, n, q_ref, kp_ref, kc_ref, vp_ref, vc_ref, o_ref, do_ref, cos_ref, sin_ref, sink_ref,
                    dq_ref, dkv_ref, ds_ref, ck_scr, cv_scr):
    @pl.when(i == 0)
    def _():
        ds_ref[...] = jnp.zeros_like(ds_ref)
        ck_scr[...] = jnp.zeros_like(ck_scr)
        cv_scr[...] = jnp.zeros_like(cv_scr)

    qv, ov, dov, sinks = q_ref[...], o_ref[...], do_ref[...], sink_ref[...]
    kcat = jnp.concatenate([kp_ref[...], kc_ref[...]], axis=0)
    vcat = jnp.concatenate([vp_ref[...], vc_ref[...]], axis=0)
    upper = _upper_mask()
    srow = lax.broadcasted_iota(jnp.int32, (8, LB), 0)
    slane = lax.broadcasted_iota(jnp.int32, (8, LB), 1)
    dsink = jnp.zeros((8, LB), F32)
    dq_g, dk_g, dv_g = [], [], []
    for g in range(NQ // QPG):
        sl = slice(g * HD, (g + 1) * HD)
        qg = _stack_heads(qv, g)
        dog = _stack_heads(dov, g)
        probs, psink = _attn_softmax(_attn_scores(n, qg, kcat[:, sl], upper), _sink_col(sinks, g))
        delta = jnp.sum(dog.astype(F32) * _stack_heads(ov, g).astype(F32), axis=1, keepdims=True)
        dsc = probs * (_band(upper, _dot_nt(dog, vcat[0:LB, sl]), _dot_nt(dog, vcat[LB:2 * LB, sl])) - delta)
        sink_terms = psink * delta
        for hh in range(QPG):
            dsink = dsink - jnp.where((srow == QPG * g + hh) & (slane == 0),
                                      jnp.sum(sink_terms[hh * LB:(hh + 1) * LB, :]), 0.0)
        ds_p = jnp.where(upper, dsc, 0.0).astype(BF16)
        ds_c = jnp.where(upper, 0.0, dsc).astype(BF16)
        dq_g.append((_dot(ds_p, kcat[0:LB, sl]) + _dot(ds_c, kcat[LB:2 * LB, sl])) * ATT_SCALE)
        dk_g.append(jnp.concatenate([_dot_tn(ds_p, qg), _dot_tn(ds_c, qg)], axis=0) * ATT_SCALE)
        dv_g.append(jnp.concatenate([_dot_tn(jnp.where(upper, probs, 0.0).astype(BF16), dog),
                                     _dot_tn(jnp.where(upper, 0.0, probs).astype(BF16), dog)], axis=0))
    ds_ref[...] += dsink
    cs = cos_ref[...]
    sn = sin_ref[...]
    dk2 = jnp.concatenate(dk_g, axis=1)
    dv2 = jnp.concatenate(dv_g, axis=1)
    for a, tile in enumerate(_unstack_heads(dq_g)):
        dq_ref[:, a * LB:(a + 1) * LB] = _rope(tile, cs, sn, True).astype(BF16)
    dkv_ref[:, 0:KVW] = _rope(ck_scr[...] + dk2[LB:2 * LB, :], cs, sn, True).astype(BF16)
    dkv_ref[:, KVW:2 * KVW] = (cv_scr[...] + dv2[LB:2 * LB, :]).astype(BF16)
    ck_scr[...] = dk2[0:LB, :]
    cv_scr[...] = dv2[0:LB, :]


def _mixer_bwd(q, k, v, o, do, cos, sin, sinks8, dyn, y, z, xbc, conv_w8, conv_b, dtr, ssm_p, nw, hs, scatters):
    s = q.shape[0]
    nb = s // LB
    cur = lambda i: (nb - 1 - i, 0)
    prev = lambda i: (jnp.maximum(nb - 2 - i, 0), 0)
    n_in = 23
    items, ex_shapes, n_g = _exchange_items([], scatters)
    ne = len(items)
    e64, e128 = _head_expanders()

    def body(*refs):
        i = pl.program_id(0)
        n = nb - 1 - i
        a_in, s_in, ex_in = refs[:10], refs[10:n_in], refs[n_in:n_in + ne]
        dp_ref, ds_ref, sm_ref = refs[n_in + ne:n_in + ne + 3]
        dq_ref, dkv_ref, dzxd_ref = dp_ref.at[:, O_Q:O_K], dp_ref.at[:, O_K:O_Z], dp_ref.at[:, O_Z:INP]
        ex_out = refs[n_in + ne + 3:n_in + 2 * ne + 3]
        ck_scr, cv_scr, dh_scr, dun_scr = refs[n_in + 2 * ne + 3:n_in + 2 * ne + 7]
        sems = refs[n_in + 2 * ne + 7:]

        @pl.when(i == 0)
        def _():
            _Exchange(n_g, ex_in, ex_out, sems).start()

        _attn_bwd_block(i, n, *a_in, dq_ref, dkv_ref, ds_ref, ck_scr, cv_scr)
        _ssd_bwd_block(i, n, *s_in, dzxd_ref, sm_ref, dh_scr, dun_scr)

        @pl.when(i == nb - 1)
        def _():
            _Exchange(n_g, ex_in, ex_out, sems).finish()

    any_spec = pl.BlockSpec(memory_space=pl.ANY)
    tail = pl.BlockSpec((SUB, D), lambda i: (jnp.maximum((nb - 1 - i) * (LB // SUB) - 1, 0), 0))
    outs = _pcall(
        body, name="mixer_bwd", grid=(nb,),
        in_specs=[pl.BlockSpec((LB, AW), cur), pl.BlockSpec((LB, KVW), prev), pl.BlockSpec((LB, KVW), cur),
                  pl.BlockSpec((LB, KVW), prev), pl.BlockSpec((LB, KVW), cur), pl.BlockSpec((LB, AW), cur),
                  pl.BlockSpec((LB, AW), cur), pl.BlockSpec((LB, LB), cur), pl.BlockSpec((LB, LB), cur),
                  _const((8, LB)),
                  pl.BlockSpec((LB, SW), cur), pl.BlockSpec((LB, SW), cur), pl.BlockSpec((LB, SW), cur),
                  tail, pl.BlockSpec((LB, D), cur), _const((8, D)), _const((1, D)),
                  pl.BlockSpec((LB, LB), cur), _const((8, LB)), _const((1, SW)),
                  pl.BlockSpec((1, NH * HD, NST), lambda i: (nb - 1 - i, 0, 0)),
                  _const(e64.shape), _const(e128.shape)] + [any_spec] * ne,
        out_specs=[pl.BlockSpec((LB, INP), cur), pl.BlockSpec((8, LB), lambda i: (0, 0)),
                   pl.BlockSpec((8, D), lambda i: (0, 0))] + [any_spec] * ne,
        out_shape=[_sds((s, INP), BF16), _sds((8, LB), F32), _sds((8, D), F32)] + ex_shapes,
        scratch_shapes=[pltpu.VMEM((LB, KVW), F32), pltpu.VMEM((LB, KVW), F32),
                        pltpu.VMEM((NH * HD, NST), F32), pltpu.VMEM((SUB, D), F32)]
        + _exchange_sems(ne),
        compiler_params=_params(),
    )(q, k, k, v, v, o, do, cos, sin, sinks8, dyn, y, z, xbc, xbc, conv_w8, conv_b, dtr, ssm_p, nw, hs, e64, e128,
      *items)
    return outs[0], outs[1], outs[2], outs[3:]


def _inproj_bwd(dproj, x, dx2, mod8, n1w, w_in_t, scatters):
    s = x.shape[0]
    tt = min(512, s)
    nt = s // tt
    items, ex_shapes, n_g = _exchange_items([], scatters)
    ne = len(items)

    def body(*refs):
        dp_ref, x_ref, dx2_ref, mod_ref, nw_ref, w_ref = refs[:6]
        ex_in = refs[6:6 + ne]
        gx_ref, sm_ref = refs[6 + ne:8 + ne]
        ex_out = refs[8 + ne:8 + 2 * ne]
        sems = refs[8 + 2 * ne:]
        i = pl.program_id(0)

        @pl.when(i == 0)
        def _():
            sm_ref[...] = jnp.zeros_like(sm_ref)
            _Exchange(n_g, ex_in, ex_out, sems).start()

        dh1 = _dot(dp_ref[...], w_ref[...])
        dxn, d_shift, d_scale, d_w = _norm_mod_bwd(x_ref[...], dh1, nw_ref[...], mod_ref[1:2, :])
        gx_ref[...] = dx2_ref[...] + dxn
        sm_ref[0:1, :] += d_shift
        sm_ref[1:2, :] += d_scale
        sm_ref[2:3, :] += d_w

        @pl.when(i == nt - 1)
        def _():
            _Exchange(n_g, ex_in, ex_out, sems).finish()

    any_spec = pl.BlockSpec(memory_space=pl.ANY)
    outs = _pcall(
        body, name="inproj_bwd", grid=(nt,),
        in_specs=[_rows(tt, INP), _rows(tt, D), _rows(tt, D), _const((8, D)), _const((1, D)), _const((INP, D))]
        + [any_spec] * ne,
        out_specs=[_rows(tt, D), pl.BlockSpec((8, D), lambda i: (0, 0))] + [any_spec] * ne,
        out_shape=[_sds((s, D), F32), _sds((8, D), F32)] + ex_shapes,
        scratch_shapes=_exchange_sems(ne),
        compiler_params=_params(),
    )(dproj, x, dx2, mod8, n1w, w_in_t, *items)
    return outs[0], outs[1], outs[2:]


def _wgrad(a, b, name):
    s, m = a.shape
    n = b.shape[1]
    tk = min(1024, s)
    wide = (1408, 1024, 512)
    tm = next((t for t in wide if m % t == 0), m)
    tn = n if n <= 2048 else _largest_divisor(n, wide)
    nk = s // tk

    def body(a_ref, b_ref, o_ref, acc):
        kk = pl.program_id(2)

        @pl.when(kk == 0)
        def _():
            acc[...] = jnp.zeros_like(acc)

        acc[...] += _dot_tn(a_ref[...], b_ref[...])

        @pl.when(kk == nk - 1)
        def _():
            o_ref[...] = acc[...].astype(BF16)

    return _pcall(
        body, name=name, grid=(m // tm, n // tn, nk),
        in_specs=[pl.BlockSpec((tk, tm), lambda i, j, kk: (kk, i)), pl.BlockSpec((tk, tn), lambda i, j, kk: (kk, j))],
        out_specs=pl.BlockSpec((tm, tn), lambda i, j, kk: (i, j)),
        out_shape=_sds((m, n), BF16),
        scratch_shapes=[pltpu.VMEM((tm, tn), F32)],
        compiler_params=_params(3),
    )(a, b)


PACK_ROWS = 24


def _pack_small(sm_f, sm_b, sm_s, sm_i, dsink):
    def body(f_ref, b_ref, s_ref, i_ref, k_ref, o_ref):
        o_ref[...] = jnp.zeros_like(o_ref)
        o_ref[0:2, :] = i_ref[0:2, :]
        o_ref[2:3, :] = b_ref[3:4, :]
        o_ref[3:5, :] = b_ref[0:2, :]
        o_ref[5:6, :] = f_ref[0:1, :]
        o_ref[6:7, :] = i_ref[2:3, :]
        o_ref[7:8, :] = b_ref[2:3, :]
        o_ref[8:9, :] = f_ref[1:2, :]
        o_ref[9:14, :] = s_ref[0:5, :]
        o_ref[14:15, :] = s_ref[5:6, :]
        o_ref[15:16, 0:3 * LB] = s_ref[6:7, 0:3 * LB]
        lane = lax.broadcasted_iota(jnp.int32, (1, LB), 1)
        sk = jnp.zeros((1, LB), F32)
        for h in range(NQ):
            sk = sk + jnp.where(lane == h, k_ref[h:h + 1, 0:1], 0.0)
        o_ref[15:16, 3 * LB:4 * LB] = sk
        o_ref[16:17, :] = f_ref[2:3, :]

    return _pcall(body, name="pack_small", out_shape=_sds((PACK_ROWS, D), F32))(sm_f, sm_b, sm_s, sm_i, dsink)


def _exchange(gathers, scatters, name, two_level=False):
    items, shapes, n_g = _exchange_items(gathers, scatters)
    n = len(items)
    assert not (two_level and scatters)

    def body(*refs):
        ex = _Exchange(n_g, refs[:n], refs[n:2 * n], refs[2 * n:])
        if two_level:
            ex.gather_two_level()
        else:
            ex.start()
            ex.finish()

    any_spec = pl.BlockSpec(memory_space=pl.ANY)
    return _pcall(
        body, name=name, in_specs=[any_spec] * n, out_specs=[any_spec] * n, out_shape=shapes,
        scratch_shapes=_exchange_sems(n),
    )(*items)


def _exchange_items(gathers, scatters):
    items = list(gathers) + list(scatters)
    shapes = [_sds((N_DEV,) + a.shape, a.dtype) for a in gathers] + [_sds(a.shape, a.dtype) for a in scatters]
    return items, shapes, len(gathers)


def _exchange_sems(n):
    return [pltpu.SemaphoreType.DMA((n, N_DEV - 1)), pltpu.SemaphoreType.DMA((n, N_DEV - 1)),
            pltpu.SemaphoreType.DMA((n,))]


class _Exchange:
    def __init__(self, n_g, ins, outs, sems):
        self.n_g, self.ins, self.outs = n_g, ins, outs
        self.send_sems, self.recv_sems, self.loc_sems = sems
        xi, yi, ci = lax.axis_index("x"), lax.axis_index("y"), lax.axis_index("c")
        self.me = 4 * xi + 2 * yi + ci
        self.peers = []
        for r in range(1, N_DEV):
            px = 1 - xi if r & 4 else xi
            py = 1 - yi if r & 2 else yi
            pc = 1 - ci if r & 1 else ci
            self.peers.append(((px, py, pc), 4 * px + 2 * py + pc))

    def _copy(self, t, r, landing):
        dev, peer = self.peers[r]
        src = self.ins[t] if t < self.n_g else self.ins[t].at[peer]
        return pltpu.make_async_remote_copy(
            src_ref=src, dst_ref=self.outs[t].at[landing], send_sem=self.send_sems.at[t, r],
            recv_sem=self.recv_sems.at[t, r], device_id=dev, device_id_type=pl.DeviceIdType.MESH)

    def _local(self, t):
        src = self.ins[t] if t < self.n_g else self.ins[t].at[self.me]
        return pltpu.make_async_copy(src, self.outs[t].at[self.me], self.loc_sems.at[t])

    def start(self):
        for t in range(len(self.ins)):
            self._local(t).start()
            for r in range(N_DEV - 1):
                self._copy(t, r, self.me).start()

    def finish(self):
        n = len(self.ins)
        for t in range(n):
            for r in range(N_DEV - 1):
                self._copy(t, r, self.peers[r][1]).wait_recv()
        for t in range(n):
            for r in range(N_DEV - 1):
                self._copy(t, r, self.me).wait_send()
            self._local(t).wait()

    def gather_two_level(self):
        n = len(self.ins)
        direct = (0, 1, 3, 5)
        for t in range(n):
            self._local(t).start()
            for r in direct:
                self._copy(t, r, self.me).start()
        sibling = self.peers[0][0]
        relays = []
        for t in range(n):
            for r in direct[1:]:
                peer = self.peers[r][1]
                self._copy(t, r, peer).wait_recv()
                relay = pltpu.make_async_remote_copy(
                    src_ref=self.outs[t].at[peer], dst_ref=self.outs[t].at[peer], send_sem=self.send_sems.at[t, r + 1],
                    recv_sem=self.recv_sems.at[t, r + 1], device_id=sibling, device_id_type=pl.DeviceIdType.MESH)
                relay.start()
                relays.append(relay)
        for t in range(n):
            for r in (0, 2, 4, 6):
                self._copy(t, r, self.peers[r][1]).wait_recv()
        for t in range(n):
            for r in direct:
                self._copy(t, r, self.me).wait_send()
            self._local(t).wait()
        for relay in relays:
            relay.wait_send()


def _ada_fwd(c_all, w_cols, b_cols):
    def body(c_ref, w_ref, b_ref, o_ref):
        cv = c_ref[...]
        sc = (cv * _sigmoid(cv)).astype(BF16)
        o_ref[...] = _dot(sc, w_ref[...].astype(BF16)) + b_ref[...]

    return _pcall(body, name="ada_fwd", out_shape=_sds((N_DEV, w_cols.shape[1]), F32),
                  compiler_params=_params(0))(c_all, w_cols, b_cols)


def _adamw(w, g, m, v):
    m2 = ADAM_B1 * m + (1.0 - ADAM_B1) * g
    v2 = ADAM_B2 * v + (1.0 - ADAM_B2) * (g * g)
    m_hat = m2 / (1.0 - ADAM_B1 ** ADAM_STEP)
    v_hat = v2 / (1.0 - ADAM_B2 ** ADAM_STEP)
    delta = -ADAM_LR * (m_hat / (jnp.sqrt(v_hat) + ADAM_EPS) + ADAM_WD * w)
    return delta, m2, v2


def _sum_adamw(parts, w, m, v, name):
    rws, cols = w.shape
    tr = next((t for t in (256, 176, 128) if rws % t == 0), rws)

    def body(p_ref, w_ref, m_ref, v_ref, g_ref, d_ref, mo_ref, vo_ref):
        g = p_ref[0].astype(F32)
        for dev in range(1, N_DEV):
            g = g + p_ref[dev].astype(F32)
        g_ref[...] = g
        d_ref[...], mo_ref[...], vo_ref[...] = _adamw(w_ref[...], g, m_ref[...], v_ref[...])

    blk = pl.BlockSpec((tr, cols), lambda i: (i, 0))
    return _pcall(
        body, name=name, grid=(rws // tr,),
        in_specs=[pl.BlockSpec((N_DEV, tr, cols), lambda i: (0, i, 0)), blk, blk, blk],
        out_specs=[blk] * 4, out_shape=[_sds((rws, cols), F32)] * 4, compiler_params=_params(),
    )(parts, w, m, v)


def _wada_adamw(c_all, dmod_cols, w, m, v):
    rws, cols = w.shape
    tr = 256

    def body(c_ref, dm_ref, w_ref, m_ref, v_ref, g_ref, d_ref, mo_ref, vo_ref):
        cv = c_ref[...]
        sc = (cv * _sigmoid(cv)).astype(BF16)
        g = _dot_tn(sc, dm_ref[...].astype(BF16))
        g_ref[...] = g
        d_ref[...], mo_ref[...], vo_ref[...] = _adamw(w_ref[...], g, m_ref[...], v_ref[...])

    blk = pl.BlockSpec((tr, cols), lambda i: (i, 0))
    return _pcall(
        body, name="wada_adamw", grid=(rws // tr,),
        in_specs=[pl.BlockSpec((N_DEV, tr), lambda i: (0, i)), pl.BlockSpec((N_DEV, cols), lambda i: (0, 0)),
                  blk, blk, blk],
        out_specs=[blk] * 4, out_shape=[_sds((rws, cols), F32)] * 4, compiler_params=_params(),
    )(c_all, dmod_cols, w, m, v)


def _small_reduce(packs):
    def body(p_ref, o_ref):
        tot = p_ref[0]
        for dev in range(1, N_DEV):
            tot = tot + p_ref[dev]
        o_ref[...] = tot
        o_ref[16:17, :] = jnp.zeros((1, D), F32) + (0.5 / D) * jnp.sum(tot[16:17, :])

    return _pcall(body, name="small_reduce", out_shape=_sds((PACK_ROWS, D), F32))(packs)


def _adamw_many(ws, gs, ms, vs):
    k = len(ws)

    def body(*refs):
        for i in range(k):
            w_ref, g_ref, m_ref, v_ref = refs[i], refs[k + i], refs[2 * k + i], refs[3 * k + i]
            d_ref, mo_ref, vo_ref = refs[4 * k + i], refs[5 * k + i], refs[6 * k + i]
            d_ref[...], mo_ref[...], vo_ref[...] = _adamw(w_ref[...], g_ref[...], m_ref[...], v_ref[...])

    shp = [_sds(w.shape, F32) for w in ws]
    outs = _pcall(body, name="adamw_small", out_shape=shp * 3)(*ws, *gs, *ms, *vs)
    return outs[:k], outs[k:2 * k], outs[2 * k:]


def kernel(x, c, positions, w_ada, b_ada, norm1_w, w_in, conv_w, conv_b, dt_bias, a_log, d_skip, attn_sinks, ssm_norm_w, w_out, norm2_w, w_gate_up, w_down, final_norm_w, loss_target, m_w_ada, m_b_ada, m_norm1_w, m_w_in, m_conv_w, m_conv_b, m_dt_bias, m_a_log, m_d_skip, m_attn_sinks, m_ssm_norm_w, m_w_out, m_norm2_w, m_w_gate_up, m_w_down, m_final_norm_w, v_w_ada, v_b_ada, v_norm1_w, v_w_in, v_conv_w, v_conv_b, v_dt_bias, v_a_log, v_d_skip, v_attn_sinks, v_ssm_norm_w, v_w_out, v_norm2_w, v_w_gate_up, v_w_down, v_final_norm_w):
    s = x.shape[1]
    me = 4 * lax.axis_index("x") + 2 * lax.axis_index("y") + lax.axis_index("c")
    ada_cols = N_MOD * D // N_DEV

    c8 = jnp.pad(c, ((0, 7), (0, 0)))
    cw8 = jnp.pad(conv_w[0], ((0, 8 - CONVK), (0, 0)))
    w_in_t, m_w_in_t, v_w_in_t = jnp.transpose(w_in[0]), jnp.transpose(m_w_in[0]), jnp.transpose(v_w_in[0])
    w_gu_t, m_w_gu_t, v_w_gu_t = (jnp.transpose(w_gate_up[0]), jnp.transpose(m_w_gate_up[0]),
                                  jnp.transpose(v_w_gate_up[0]))
    g_c, g_in, g_cw = _exchange([c8, w_in_t.astype(BF16), cw8], [], "gather_in", two_level=True)
    c_all = g_c[:, 0, :]
    w_in_f = jnp.pad(g_in.reshape(IN_PROJ, D), ((0, INP - IN_PROJ), (0, 0)))
    conv_w8 = jnp.transpose(g_cw, (1, 0, 2)).reshape(8, D)

    b_cols = lax.dynamic_slice(b_ada, (0, me * ada_cols), (1, ada_cols))
    (g_mod,) = _exchange([_ada_fwd(c_all, w_ada[0], b_cols)], [], "gather_mod")
    mod = lax.dynamic_index_in_dim(g_mod, me, axis=1, keepdims=False).reshape(N_MOD, D)
    mod8 = jnp.pad(mod, ((0, 8 - N_MOD), (0, 0)))

    half = HD // 2
    inv_freq = ROPE_THETA ** (-jnp.arange(half, dtype=F32) / half)
    invf = jnp.tile(inv_freq, LB // half).reshape(1, LB)
    lanes = lambda a: jnp.pad(a, ((0, 0), (0, LB - a.shape[1])))
    ssm_p = jnp.pad(jnp.concatenate([lanes(dt_bias), lanes(a_log), lanes(d_skip)], axis=0), ((0, 5), (0, 0)))
    sinks8 = jnp.broadcast_to(attn_sinks.reshape(NQ, 1), (NQ, LB))

    xs, tgt, fnw = x[0], loss_target[0], final_norm_w.reshape(1, D)

    q, k, v, z, xbc, dtr, h1, cos, sin = _inproj_fwd(xs, positions[0].reshape(s, 1), invf, mod8, norm1_w, w_in_f)
    attn, yn, y, hs, (g_out, g_gu, g_down) = _mixer_fwd(
        q, k, v, sinks8, xbc, conv_w8, conv_b, dtr, ssm_p, z, ssm_norm_w,
        [w_out[0].astype(BF16), w_gu_t.astype(BF16), w_down[0].astype(BF16)])
    w_out_f = g_out.reshape(D, D)
    w_gu_f = g_gu.reshape(2 * DFF, D)
    w_down_f = g_down.reshape(DFF, D)
    x2, h2, mo, mix, gu, act, dx3, sm_f = _outproj_ffn_fwd_loss(attn, yn, xs, tgt, mod8, norm2_w, fnw, w_out_f, w_gu_f,
                                                                 w_down_f)

    dx2, dff, dgu, dmix, dattn, dyn, sm_b = _ffn_bwd(dx3, gu, x2, mo, mod8, norm2_w, w_gu_f, w_down_f, w_out_f)
    p_gu = _wgrad(dgu, h2, "wgrad_gate_up").reshape(N_DEV, 2 * DFF // N_DEV, D)
    p_down = _wgrad(act, dff, "wgrad_down").reshape(N_DEV, DFF // N_DEV, D)
    p_out = _wgrad(mix, dmix, "wgrad_out").reshape(N_DEV, D // N_DEV, D)
    dproj, dsink, sm_s, (r_gu, r_down, r_out) = _mixer_bwd(
        q, k, v, attn, dattn, cos, sin, sinks8, dyn, y, z, xbc, conv_w8, conv_b, dtr, ssm_p, ssm_norm_w, hs,
        [p_gu, p_down, p_out])
    p_in = _wgrad(dproj, h1, "wgrad_in")[:IN_PROJ].reshape(N_DEV, IN_PROJ // N_DEV, D)
    gx, sm_i, (r_in,) = _inproj_bwd(dproj, xs, dx2, mod8, norm1_w, w_in_f, [p_in])
    (g_pack,) = _exchange([_pack_small(sm_f, sm_b, sm_s, sm_i, dsink)], [], "gather_small")

    tot = _small_reduce(g_pack)
    loss = tot[16, 0]
    dmod_all = g_pack[:, 0:N_MOD, :].reshape(N_DEV, N_MOD * D)
    dmod_cols = lax.dynamic_slice(dmod_all, (0, me * ada_cols), (N_DEV, ada_cols))

    big = {
        "w_ada": _wada_adamw(c_all, dmod_cols, w_ada[0], m_w_ada[0], v_w_ada[0]),
        "w_in": [jnp.transpose(t) for t in _sum_adamw(r_in, w_in_t, m_w_in_t, v_w_in_t, "adamw_in")],
        "w_out": _sum_adamw(r_out, w_out[0], m_w_out[0], v_w_out[0], "adamw_out"),
        "w_gate_up": [jnp.transpose(t) for t in _sum_adamw(r_gu, w_gu_t, m_w_gu_t, v_w_gu_t, "adamw_gate_up")],
        "w_down": _sum_adamw(r_down, w_down[0], m_w_down[0], v_w_down[0], "adamw_down"),
    }
    small_names = ["b_ada", "norm1_w", "conv_w", "conv_b", "dt_bias", "a_log", "d_skip", "attn_sinks", "ssm_norm_w",
                   "norm2_w", "final_norm_w"]
    row15 = tot[15:16, :]
    small_g = {
        "b_ada": tot[0:N_MOD, :].reshape(1, N_MOD * D),
        "norm1_w": tot[6:7, :],
        "conv_w": lax.dynamic_slice(tot[10:14, :], (0, me * (D // N_DEV)), (CONVK, D // N_DEV)),
        "conv_b": tot[9:10, :],
        "dt_bias": row15[:, 0:NH],
        "a_log": row15[:, LB:LB + NH],
        "d_skip": row15[:, 2 * LB:2 * LB + NH],
        "attn_sinks": row15[:, 3 * LB:3 * LB + NQ],
        "ssm_norm_w": tot[14:15, 0:SW],
        "norm2_w": tot[7:8, :],
        "final_norm_w": tot[8:9, :],
    }
    small_w = {"b_ada": b_ada, "norm1_w": norm1_w, "conv_w": conv_w[0], "conv_b": conv_b, "dt_bias": dt_bias,
               "a_log": a_log, "d_skip": d_skip, "attn_sinks": attn_sinks, "ssm_norm_w": ssm_norm_w,
               "norm2_w": norm2_w, "final_norm_w": final_norm_w.reshape(1, D)}
    small_m = {"b_ada": m_b_ada, "norm1_w": m_norm1_w, "conv_w": m_conv_w[0], "conv_b": m_conv_b,
               "dt_bias": m_dt_bias, "a_log": m_a_log, "d_skip": m_d_skip, "attn_sinks": m_attn_sinks,
               "ssm_norm_w": m_ssm_norm_w, "norm2_w": m_norm2_w, "final_norm_w": m_final_norm_w.reshape(1, D)}
    small_v = {"b_ada": v_b_ada, "norm1_w": v_norm1_w, "conv_w": v_conv_w[0], "conv_b": v_conv_b,
               "dt_bias": v_dt_bias, "a_log": v_a_log, "d_skip": v_d_skip, "attn_sinks": v_attn_sinks,
               "ssm_norm_w": v_ssm_norm_w, "norm2_w": v_norm2_w, "final_norm_w": v_final_norm_w.reshape(1, D)}
    s_d, s_m, s_v = _adamw_many([small_w[k] for k in small_names], [small_g[k] for k in small_names],
                                [small_m[k] for k in small_names], [small_v[k] for k in small_names])

    order = ["w_ada", "b_ada", "norm1_w", "w_in", "conv_w", "conv_b", "dt_bias", "a_log", "d_skip", "attn_sinks",
             "ssm_norm_w", "w_out", "norm2_w", "w_gate_up", "w_down", "final_norm_w"]
    lead = {"w_ada", "w_in", "conv_w", "w_out", "w_gate_up", "w_down"}
    grads, deltas, new_m, new_v = [], [], [], []
    for name in order:
        if name in big:
            g, d, m2, v2 = big[name]
        else:
            i = small_names.index(name)
            g, d, m2, v2 = small_g[name], s_d[i], s_m[i], s_v[i]
        if name in lead:
            g, d, m2, v2 = g[None], d[None], m2[None], v2[None]
        if name == "final_norm_w":
            g, d, m2, v2 = g.reshape(D), d.reshape(D), m2.reshape(D), v2.reshape(D)
        grads.append(g)
        deltas.append(d)
        new_m.append(m2)
        new_v.append(v2)
    return (loss, gx[None], *grads, *deltas, *new_m, *new_v)
```

```python
import functools
import math

import jax
import jax.numpy as jnp
from jax import lax
from jax.experimental import pallas as pl
from jax.experimental.pallas import tpu as pltpu

F32 = jnp.float32
BF16 = jnp.bfloat16

N_DEV = 8
D = 1024
HD = 64
NQ = 8
AW = 512
KVW = 128
SW = 512
NST = 128
NH = 8
LB = 128
CONVK = 4
DFF = 2816
N_MOD = 6
IN_PROJ = 2312
INP = 2432
O_Q, O_K, O_V, O_Z, O_XBC, O_DT = 0, 512, 640, 768, 1280, 2304
ZXD = INP - O_Z
EPS = 1e-6
NEG = -1e30
ROPE_THETA = 10000.0
VMEM_LIMIT = 56 * 1024 * 1024

ADAM_LR = 0.001
ADAM_B1 = 0.9
ADAM_B2 = 0.999
ADAM_EPS = 1e-08
ADAM_WD = 0.01
ADAM_STEP = 10

NT_DIMS = (((1,), (1,)), ((), ()))
TN_DIMS = (((0,), (0,)), ((), ()))


def _pcall(body, **kw):
    return pl.pallas_call(body, **kw)


def _sds(shape, dtype):
    return jax.ShapeDtypeStruct(shape, dtype)


def _params(n_grid=1):
    return pltpu.CompilerParams(dimension_semantics=("arbitrary",) * n_grid, vmem_limit_bytes=VMEM_LIMIT)


def _const(shape):
    return pl.BlockSpec(shape, lambda *_: (0,) * len(shape), pipeline_mode=pl.Buffered(1))


def _largest_divisor(n, candidates):
    for cand in candidates:
        if n % cand == 0:
            return cand
    raise ValueError(f"no tile in {candidates} divides {n}")


def _rows(t, w):
    return pl.BlockSpec((t, w), lambda i: (i, 0))


def _dot(a, b):
    return jnp.dot(a, b, preferred_element_type=F32)


def _dot_nt(a, b):
    return lax.dot_general(a, b, NT_DIMS, preferred_element_type=F32)


def _dot_tn(a, b):
    return lax.dot_general(a, b, TN_DIMS, preferred_element_type=F32)


def _sigmoid(v):
    return 1.0 / (1.0 + jnp.exp(-v))


def _softplus(v):
    return jnp.maximum(v, 0.0) + jnp.log1p(jnp.exp(-jnp.abs(v)))


def _rope_sign_mask(shape):
    lane = lax.broadcasted_iota(jnp.int32, shape, 1)
    return (lane % HD) < (HD // 2)


def _rope(t, cs, sn, inverse):
    r_dn = pltpu.roll(t, HD // 2, 1)
    r_up = pltpu.roll(t, LB - HD // 2, 1)
    first = _rope_sign_mask(t.shape)
    if inverse:
        rot = jnp.where(first, r_up, -r_dn)
    else:
        rot = jnp.where(first, -r_up, r_dn)
    return t * cs + rot * sn


def _norm_mod_fwd(xv, nw, shift, scale):
    r = lax.rsqrt(jnp.mean(xv * xv, axis=-1, keepdims=True) + EPS)
    xh = xv * r
    return (xh * nw) * (1.0 + scale) + shift


def _norm_mod_bwd(xv, dh, nw, scale):
    r = lax.rsqrt(jnp.mean(xv * xv, axis=-1, keepdims=True) + EPS)
    xh = xv * r
    xn = xh * nw
    d_shift = jnp.sum(dh, axis=0, keepdims=True)
    d_scale = jnp.sum(dh * xn, axis=0, keepdims=True)
    dxn = dh * (1.0 + scale)
    d_w = jnp.sum(dxn * xh, axis=0, keepdims=True)
    dxh = dxn * nw
    dx = r * (dxh - xh * jnp.mean(dxh * xh, axis=-1, keepdims=True))
    return dx, d_shift, d_scale, d_w


def _inproj_fwd(x, pos, invf, mod8, n1w, w_in):
    s = x.shape[0]
    tt = min(512, s)

    def body(x_ref, pos_ref, invf_ref, mod_ref, nw_ref, w_ref,
             q_ref, k_ref, v_ref, z_ref, xbc_ref, dtr_ref, h1_ref, cos_ref, sin_ref):
        h = _norm_mod_fwd(x_ref[...], nw_ref[...], mod_ref[0:1, :], mod_ref[1:2, :])
        hb = h.astype(BF16)
        h1_ref[...] = hb
        proj = _dot_nt(hb, w_ref[...])
        ang = pos_ref[...].astype(F32) * invf_ref[...]
        cs = jnp.cos(ang)
        sn = jnp.sin(ang)
        cos_ref[...] = cs
        sin_ref[...] = sn
        for a in range(AW // LB):
            q_ref[:, a * LB:(a + 1) * LB] = _rope(proj[:, O_Q + a * LB:O_Q + (a + 1) * LB], cs, sn, False).astype(BF16)
        k_ref[...] = _rope(proj[:, O_K:O_V], cs, sn, False).astype(BF16)
        v_ref[...] = proj[:, O_V:O_Z].astype(BF16)
        z_ref[...] = proj[:, O_Z:O_XBC]
        xbc_ref[...] = proj[:, O_XBC:O_DT]
        dtr_ref[...] = proj[:, O_DT:INP]

    return _pcall(
        body, name="inproj_fwd", grid=(s // tt,),
        in_specs=[_rows(tt, D), _rows(tt, 1), _const((1, LB)), _const((8, D)), _const((1, D)), _const((INP, D))],
        out_specs=[_rows(tt, AW), _rows(tt, KVW), _rows(tt, KVW), _rows(tt, SW), _rows(tt, D), _rows(tt, LB),
                   _rows(tt, D), _rows(tt, LB), _rows(tt, LB)],
        out_shape=[_sds((s, AW), BF16), _sds((s, KVW), BF16), _sds((s, KVW), BF16), _sds((s, SW), F32),
                   _sds((s, D), F32), _sds((s, LB), F32), _sds((s, D), BF16), _sds((s, LB), F32), _sds((s, LB), F32)],
        compiler_params=_params(),
    )(x, pos, invf, mod8, n1w, w_in)


QPG = 4
ATT_SCALE = 1.0 / math.sqrt(HD)


def _stack_heads(val, g):
    return jnp.concatenate([val[:, (QPG * g + hh) * HD:(QPG * g + hh + 1) * HD] for hh in range(QPG)], axis=0)


def _unstack_heads(groups):
    pieces = [grp[hh * LB:(hh + 1) * LB, :] for grp in groups for hh in range(QPG)]
    return [jnp.concatenate(pieces[2 * a:2 * a + 2], axis=1) for a in range(NQ // 2)]


def _upper_mask():
    row = lax.broadcasted_iota(jnp.int32, (QPG * LB, LB), 0)
    col = lax.broadcasted_iota(jnp.int32, (QPG * LB, LB), 1)
    return col > (row % LB)


def _sink_col(sinks, g):
    return jnp.concatenate([jnp.broadcast_to(sinks[QPG * g + hh:QPG * g + hh + 1, 0:1], (LB, 1))
                            for hh in range(QPG)], axis=0)


def _band(upper, prev_part, cur_part):
    return jnp.where(upper, prev_part, cur_part)


def _attn_scores(n, qg, kcat, upper):
    sp = _dot_nt(qg, kcat[0:LB, :]) * ATT_SCALE
    sc = _dot_nt(qg, kcat[LB:2 * LB, :]) * ATT_SCALE
    return _band(upper, jnp.where(n > 0, sp, NEG), sc)


def _attn_softmax(comb, sink):
    m = jnp.maximum(jnp.max(comb, axis=-1, keepdims=True), sink)
    p = jnp.exp(comb - m)
    es = jnp.exp(sink - m)
    denom = jnp.sum(p, axis=-1, keepdims=True) + es
    return p / denom, es / denom


def _attn_fwd_block(n, q_ref, kp_ref, kc_ref, vp_ref, vc_ref, sink_ref, o_ref):
    qv = q_ref[...]
    kcat = jnp.concatenate([kp_ref[...], kc_ref[...]], axis=0)
    vcat = jnp.concatenate([vp_ref[...], vc_ref[...]], axis=0)
    sinks = sink_ref[...]
    upper = _upper_mask()
    outs = []
    for g in range(NQ // QPG):
        sl = slice(g * HD, (g + 1) * HD)
        probs, _ = _attn_softmax(_attn_scores(n, _stack_heads(qv, g), kcat[:, sl], upper), _sink_col(sinks, g))
        outs.append(_dot(jnp.where(upper, probs, 0.0).astype(BF16), vcat[0:LB, sl])
                    + _dot(jnp.where(upper, 0.0, probs).astype(BF16), vcat[LB:2 * LB, sl]))
    for g, grp in enumerate(outs):
        for hh in range(QPG):
            h = QPG * g + hh
            o_ref[:, h * HD:(h + 1) * HD] = grp[hh * LB:(hh + 1) * LB, :].astype(BF16)


def _cumsum_rows(a, reverse):
    row = lax.broadcasted_iota(jnp.int32, a.shape, 0)
    step = 1
    while step < LB:
        if reverse:
            a = a + jnp.where(row < LB - step, pltpu.roll(a, LB - step, 0), 0.0)
        else:
            a = a + jnp.where(row >= step, pltpu.roll(a, step, 0), 0.0)
        step *= 2
    return a


SUB = 8


def _conv_shifts(tail, cur):
    row = lax.broadcasted_iota(jnp.int32, tail.shape, 0)
    out = [cur]
    for j in range(1, CONVK):
        rolled = pltpu.roll(cur, j, 0)
        top = jnp.where(row < j, pltpu.roll(tail, j, 0), rolled[0:SUB, :])
        out.append(jnp.concatenate([top, rolled[SUB:, :]], axis=0))
    return out


def _conv_advances(du, head):
    row = lax.broadcasted_iota(jnp.int32, head.shape, 0)
    out = []
    for j in range(1, CONVK):
        rolled = pltpu.roll(du, LB - j, 0)
        bottom = jnp.where(row >= SUB - j, pltpu.roll(head, SUB - j, 0), rolled[LB - SUB:, :])
        out.append(jnp.concatenate([rolled[:LB - SUB, :], bottom], axis=0))
    return out


def _split(v, terms):
    out = []
    for _ in range(terms - 1):
        t = v.astype(BF16)
        out.append(t)
        v = v - t.astype(F32)
    out.append(v.astype(BF16))
    return out


def _dot_sel(v, sel, terms):
    parts = [_dot(t, sel) for t in _split(v, terms)]
    return functools.reduce(lambda a, b: a + b, parts)


def _dot_nt_sel(v, sel, terms):
    parts = [_dot_nt(t, sel) for t in _split(v, terms)]
    return functools.reduce(lambda a, b: a + b, parts)


def _ssd_pre(xt_ref, xc_ref, cw_ref, cb_ref, dtr_ref, sp_ref, n):
    cur = xc_ref[...]
    tail = jnp.where(n > 0, xt_ref[...], 0.0)
    sh = _conv_shifts(tail, cur)
    u = cb_ref[...] + cw_ref[CONVK - 1:CONVK, :] * sh[0]
    for j in range(1, CONVK):
        u = u + cw_ref[CONVK - 1 - j:CONVK - j, :] * sh[j]
    sg_u = _sigmoid(u)
    xc = u * sg_u
    pre = dtr_ref[...] + sp_ref[0:1, :]
    dt = _softplus(pre)
    a_neg = -jnp.exp(sp_ref[1:2, :])
    acs = _cumsum_rows(dt * a_neg, False)
    return sh, u, sg_u, xc, pre, dt, a_neg, acs


def _gated_norm_fwd(y, z, nw):
    sz = z * _sigmoid(z)
    yz = y * sz
    parts = []
    for g in range(2):
        t = yz[:, g * 256:(g + 1) * 256]
        parts.append(t * lax.rsqrt(jnp.mean(t * t, axis=-1, keepdims=True) + EPS))
    return jnp.concatenate(parts, axis=1) * nw


HPG = 4
GW = HPG * HD


class _SsdChunk:
    def __init__(self, xc, dt, acs, spv, e64, e128):
        self.e64, self.e128 = e64, e128
        self.acs_t = jnp.transpose(acs)
        alast = acs[LB - 1:LB, :]
        self.e_all = jnp.exp(acs)
        self.dte_all = jnp.exp(alast - acs)
        self.elast = jnp.exp(alast)
        wide = _dot_sel(jnp.concatenate([dt, self.e_all, self.dte_all], axis=0), e64, 1)
        self.dt_x, self.e_x, self.dte_x = wide[0:LB], wide[LB:2 * LB], wide[2 * LB:3 * LB]
        self.dsk_x = _dot_sel(spv, e64, 3)[2:3, :]
        ac_x = _dot_sel(acs, e128, 3)
        row = lax.broadcasted_iota(jnp.int32, (HPG * LB, LB), 0)
        col = lax.broadcasted_iota(jnp.int32, (HPG * LB, LB), 1)
        causal = (row % LB) >= col
        lane = lax.broadcasted_iota(jnp.int32, (LB, GW), 1)
        self.head_lanes = [(lane >= hh * HD) & (lane < (hh + 1) * HD) for hh in range(HPG)]
        self.xs, self.xdt, self.b, self.c, self.bb, self.cb16, self.cbm, self.dm_st, self.m_st = ([] for _ in range(9))
        for g in range(2):
            heads = range(HPG * g, HPG * (g + 1))
            ac_st = jnp.concatenate([ac_x[:, j * LB:(j + 1) * LB] for j in heads], axis=0)
            ar_st = jnp.concatenate([jnp.broadcast_to(self.acs_t[j:j + 1, :], (LB, LB)) for j in heads], axis=0)
            dm_st = jnp.exp(jnp.where(causal, ac_st - ar_st, NEG))
            bg = xc[:, SW + g * NST:SW + (g + 1) * NST]
            cg = xc[:, SW + 2 * NST + g * NST:SW + 2 * NST + (g + 1) * NST]
            bgb, cgb = bg.astype(BF16), cg.astype(BF16)
            cbm = _dot_nt(cgb, bgb)
            xs_g = xc[:, g * GW:(g + 1) * GW]
            self.xs.append(xs_g)
            self.xdt.append(xs_g * self.dt_x[:, g * GW:(g + 1) * GW])
            self.b.append(bg)
            self.c.append(cg)
            self.bb.append(bgb)
            self.cb16.append(cgb)
            self.cbm.append(cbm)
            self.dm_st.append(dm_st)
            self.m_st.append(jnp.concatenate([cbm] * HPG, axis=0) * dm_st)

    def elast_rows(self, g):
        return jnp.concatenate([jnp.broadcast_to(self.elast[:, j:j + 1], (HD, NST))
                                for j in range(HPG * g, HPG * (g + 1))], axis=0)

    def diag_blocks(self, stacked):
        out = stacked[(HPG - 1) * LB:HPG * LB, :]
        for hh in range(HPG - 2, -1, -1):
            out = jnp.where(self.head_lanes[hh], stacked[hh * LB:(hh + 1) * LB, :], out)
        return out

    def block_diag(self, v):
        return jnp.concatenate([jnp.where(self.head_lanes[hh], v, 0.0) for hh in range(HPG)], axis=0)


def _ssd_fwd_block(n, xt_ref, xc_ref, cw_ref, cb_ref, dtr_ref, sp_ref, z_ref, nw_ref, e64_ref, e128_ref,
                   yn_ref, y_ref, hs_ref, h_scr):
    @pl.when(n == 0)
    def _():
        h_scr[...] = jnp.zeros_like(h_scr)

    h_all = h_scr[...]
    hs_ref[0] = h_all
    _, _, _, xc, _, dt, _, acs = _ssd_pre(xt_ref, xc_ref, cw_ref, cb_ref, dtr_ref, sp_ref, n)
    ck = _SsdChunk(xc, dt, acs, sp_ref[...], e64_ref[...], e128_ref[...])
    ys, hn = [], []
    for g in range(2):
        gl = slice(g * GW, (g + 1) * GW)
        xdt = ck.xdt[g]
        hg = h_all[gl, :]
        y_diag = ck.diag_blocks(_dot(ck.m_st[g].astype(BF16), xdt.astype(BF16)))
        y_off = ck.e_x[:, gl] * _dot_nt(ck.cb16[g], hg.astype(BF16))
        ys.append(y_diag + y_off + ck.xs[g] * ck.dsk_x[:, gl])
        hn.append(hg * ck.elast_rows(g) + _dot_tn((xdt * ck.dte_x[:, gl]).astype(BF16), ck.bb[g]))
    h_scr[...] = jnp.concatenate(hn, axis=0)
    y = jnp.concatenate(ys, axis=1)
    y_ref[...] = y
    yn_ref[...] = _gated_norm_fwd(y, z_ref[...], nw_ref[...]).astype(BF16)


def _mixer_fwd(q, k, v, sinks8, xbc, conv_w8, conv_b, dtr, ssm_p, z, nw, gathers):
    s = q.shape[0]
    nb = s // LB
    prev = lambda n: (jnp.maximum(n - 1, 0), 0)
    cur = lambda n: (n, 0)
    items, ex_shapes, n_g = _exchange_items(gathers, [])
    ne = len(items)

    n_in = 16
    relay_step = (nb - 1) // 2
    e64, e128 = _head_expanders()

    def body(*refs):
        a_in, s_in, ex_in = refs[:6], refs[6:n_in], refs[n_in:n_in + ne]
        o_ref, yn_ref, y_ref, hs_ref = refs[n_in + ne:n_in + 4 + ne]
        ex_out = refs[n_in + 4 + ne:n_in + 4 + 2 * ne]
        h_scr = refs[n_in + 4 + 2 * ne]
        sems = refs[n_in + 5 + 2 * ne:]
        n = pl.program_id(0)

        @pl.when(n == 0)
        def _():
            _Exchange(n_g, ex_in, ex_out, sems).two_level_start()

        _attn_fwd_block(n, *a_in, o_ref)
        _ssd_fwd_block(n, *s_in, yn_ref, y_ref, hs_ref, h_scr)

        @pl.when(n == relay_step)
        def _():
            _Exchange(n_g, ex_in, ex_out, sems).two_level_relay()

        @pl.when(n == nb - 1)
        def _():
            _Exchange(n_g, ex_in, ex_out, sems).two_level_finish()

    any_spec = pl.BlockSpec(memory_space=pl.ANY)
    tail = pl.BlockSpec((SUB, D), lambda n: (jnp.maximum(n * (LB // SUB) - 1, 0), 0))
    outs = _pcall(
        body, name="mixer_fwd", grid=(nb,),
        in_specs=[pl.BlockSpec((LB, AW), cur), pl.BlockSpec((LB, KVW), prev), pl.BlockSpec((LB, KVW), cur),
                  pl.BlockSpec((LB, KVW), prev), pl.BlockSpec((LB, KVW), cur), _const((8, LB)),
                  tail, pl.BlockSpec((LB, D), cur), _const((8, D)), _const((1, D)),
                  pl.BlockSpec((LB, LB), cur), _const((8, LB)), pl.BlockSpec((LB, SW), cur), _const((1, SW)),
                  _const(e64.shape), _const(e128.shape)]
        + [any_spec] * ne,
        out_specs=[pl.BlockSpec((LB, AW), cur), pl.BlockSpec((LB, SW), cur), pl.BlockSpec((LB, SW), cur),
                   pl.BlockSpec((1, NH * HD, NST), lambda n: (n, 0, 0))] + [any_spec] * ne,
        out_shape=[_sds((s, AW), BF16), _sds((s, SW), BF16), _sds((s, SW), F32), _sds((nb, NH * HD, NST), F32)]
        + ex_shapes,
        scratch_shapes=[pltpu.VMEM((NH * HD, NST), F32)] + _exchange_sems(ne),
        compiler_params=_params(),
    )(q, k, k, v, v, sinks8, xbc, xbc, conv_w8, conv_b, dtr, ssm_p, z, nw, e64, e128, *items)
    return outs[0], outs[1], outs[2], outs[3], outs[4:]


def _head_expanders():
    j = lax.broadcasted_iota(jnp.int32, (LB, NH * HD), 0)
    e64 = (lax.broadcasted_iota(jnp.int32, (LB, NH * HD), 1) // HD == j).astype(BF16)
    j = lax.broadcasted_iota(jnp.int32, (LB, NH * LB), 0)
    e128 = (lax.broadcasted_iota(jnp.int32, (LB, NH * LB), 1) // LB == j).astype(BF16)
    return e64, e128


def _outproj_ffn_fwd_loss(attn, yn, x, tgt, mod8, n2w, fnw, w_out, w_gu_t, w_down):
    s = x.shape[0]
    tf = min(256, s)

    def body(a_ref, y_ref, x_ref, t_ref, mod_ref, nw_ref, fw_ref, wo_ref, wgu_ref, wd_ref,
             x2_ref, h2_ref, mo_ref, mix_ref, gu_ref, act_ref, dx3_ref, sm_ref):
        i = pl.program_id(0)

        @pl.when(i == 0)
        def _():
            sm_ref[...] = jnp.zeros_like(sm_ref)

        mix = jnp.concatenate([a_ref[...], y_ref[...]], axis=1)
        mix_ref[...] = mix
        mo = _dot(mix, wo_ref[...])
        mo_ref[...] = mo.astype(BF16)
        x2 = x_ref[...] + mod_ref[2:3, :] * mo
        x2_ref[...] = x2
        h2 = _norm_mod_fwd(x2, nw_ref[...], mod_ref[3:4, :], mod_ref[4:5, :]).astype(BF16)
        h2_ref[...] = h2
        gu = _dot_nt(h2, wgu_ref[...])
        gu_ref[...] = gu.astype(BF16)
        g = gu[:, :DFF]
        act = (g * _sigmoid(g) * gu[:, DFF:]).astype(BF16)
        act_ref[...] = act
        ff = _dot(act, wd_ref[...])
        x3 = x2 + mod_ref[5:6, :] * ff
        r = lax.rsqrt(jnp.mean(x3 * x3, axis=-1, keepdims=True) + EPS)
        xh = x3 * r
        fw = fw_ref[...]
        err = xh * fw - t_ref[...]
        dy = err * (1.0 / D)
        dxh = dy * fw
        dx3 = r * (dxh - xh * jnp.mean(dxh * xh, axis=-1, keepdims=True))
        dx3_ref[...] = dx3
        sm_ref[0:1, :] += jnp.sum(dx3 * ff, axis=0, keepdims=True)
        sm_ref[1:2, :] += jnp.sum(dy * xh, axis=0, keepdims=True)
        sm_ref[2:3, :] += jnp.sum(err * err, axis=0, keepdims=True)

    return _pcall(
        body, name="outproj_ffn_fwd_loss", grid=(s // tf,),
        in_specs=[_rows(tf, AW), _rows(tf, SW), _rows(tf, D), _rows(tf, D), _const((8, D)), _const((1, D)),
                  _const((1, D)), _const((D, D)), _const((2 * DFF, D)), _const((DFF, D))],
        out_specs=[_rows(tf, D), _rows(tf, D), _rows(tf, D), _rows(tf, D), _rows(tf, 2 * DFF), _rows(tf, DFF),
                   _rows(tf, D), pl.BlockSpec((8, D), lambda i: (0, 0))],
        out_shape=[_sds((s, D), F32), _sds((s, D), BF16), _sds((s, D), BF16), _sds((s, D), BF16),
                   _sds((s, 2 * DFF), BF16), _sds((s, DFF), BF16), _sds((s, D), F32), _sds((8, D), F32)],
        compiler_params=_params(),
    )(attn, yn, x, tgt, mod8, n2w, fnw, w_out, w_gu_t, w_down)


def _ffn_bwd(dx3, gu, x2, mixout, mod8, n2w, w_gu, w_down, w_out):
    s = x2.shape[0]
    tb = min(256, s)

    def body(dx3_ref, gu_ref, x2_ref, mo_ref, mod_ref, nw_ref, wgu_ref, wd_ref, wo_ref,
             dx2_ref, dff_ref, dgu_ref, dmix_ref, dattn_ref, dyn_ref, sm_ref):
        i = pl.program_id(0)

        @pl.when(i == 0)
        def _():
            sm_ref[...] = jnp.zeros_like(sm_ref)

        dx3 = dx3_ref[...]
        dff = (dx3 * mod_ref[5:6, :]).astype(BF16)
        dff_ref[...] = dff
        dact = _dot_nt(dff, wd_ref[...])
        g = gu_ref[:, :DFF].astype(F32)
        u = gu_ref[:, DFF:].astype(F32)
        sg = _sigmoid(g)
        dgu = jnp.concatenate([dact * u * sg * (1.0 + g * (1.0 - sg)), dact * g * sg], axis=1).astype(BF16)
        dgu_ref[...] = dgu
        dh2 = _dot(dgu, wgu_ref[...])
        dxn, d_shift, d_scale, d_w = _norm_mod_bwd(x2_ref[...], dh2, nw_ref[...], mod_ref[4:5, :])
        dx2 = dx3 + dxn
        dx2_ref[...] = dx2
        sm_ref[0:1, :] += d_shift
        sm_ref[1:2, :] += d_scale
        sm_ref[2:3, :] += d_w
        sm_ref[3:4, :] += jnp.sum(dx2 * mo_ref[...].astype(F32), axis=0, keepdims=True)
        dmix = (dx2 * mod_ref[2:3, :]).astype(BF16)
        dmix_ref[...] = dmix
        dmi = _dot_nt(dmix, wo_ref[...])
        dattn_ref[...] = dmi[:, :AW].astype(BF16)
        dyn_ref[...] = dmi[:, AW:]

    return _pcall(
        body, name="ffn_bwd", grid=(s // tb,),
        in_specs=[_rows(tb, D), _rows(tb, 2 * DFF), _rows(tb, D), _rows(tb, D), _const((8, D)), _const((1, D)),
                  _const((2 * DFF, D)), _const((DFF, D)), _const((D, D))],
        out_specs=[_rows(tb, D), _rows(tb, D), _rows(tb, 2 * DFF), _rows(tb, D), _rows(tb, AW), _rows(tb, SW),
                   pl.BlockSpec((8, D), lambda i: (0, 0))],
        out_shape=[_sds((s, D), F32), _sds((s, D), BF16), _sds((s, 2 * DFF), BF16), _sds((s, D), BF16),
                   _sds((s, AW), BF16), _sds((s, SW), F32), _sds((8, D), F32)],
        compiler_params=_params(),
    )(dx3, gu, x2, mixout, mod8, n2w, w_gu, w_down, w_out)


def _ssd_bwd_block(i, n, *refs):
    def run(dyn_ref, y_ref, z_ref, xt_ref, xc_ref, cw_ref, cb_ref, dtr_ref, sp_ref, nw_ref, hs_ref, e64_ref, e128_ref,
            dzxd_ref, sm_ref, dh_scr, dun_scr):
        @pl.when(i == 0)
        def _():
            dh_scr[...] = jnp.zeros_like(dh_scr)
            dun_scr[...] = jnp.zeros_like(dun_scr)
            sm_ref[...] = jnp.zeros_like(sm_ref)

        sh, u, sg_u, xc, pre, dt, a_neg, acs = _ssd_pre(xt_ref, xc_ref, cw_ref, cb_ref, dtr_ref, sp_ref, n)
        ck = _SsdChunk(xc, dt, acs, sp_ref[...], e64_ref[...], e128_ref[...])
        h_all = hs_ref[0]
        dh_all = dh_scr[...]
        riota = lax.broadcasted_iota(jnp.int32, (LB, LB), 0)
        lane1 = lax.broadcasted_iota(jnp.int32, (1, LB), 1)

        z = z_ref[...]
        y = y_ref[...]
        sgz = _sigmoid(z)
        sz = z * sgz
        yz = y * sz
        nwv = nw_ref[...]
        dyn_v = dyn_ref[...]
        dyhat = dyn_v * nwv
        yhat_parts, dyz_parts = [], []
        for g in range(2):
            gs = slice(g * 256, (g + 1) * 256)
            t = yz[:, gs]
            rg = lax.rsqrt(jnp.mean(t * t, axis=-1, keepdims=True) + EPS)
            yh = t * rg
            dyh = dyhat[:, gs]
            yhat_parts.append(yh)
            dyz_parts.append(rg * (dyh - yh * jnp.mean(dyh * yh, axis=-1, keepdims=True)))
        yhat = jnp.concatenate(yhat_parts, axis=1)
        dyz = jnp.concatenate(dyz_parts, axis=1)
        sm_ref[5:6, 0:SW] += jnp.sum(dyn_v * yhat, axis=0, keepdims=True)
        dy = dyz * sz
        dzxd_ref[:, 0:SW] = (dyz * y * sgz * (1.0 + z * (1.0 - sgz))).astype(BF16)

        cat = lambda parts: jnp.concatenate(parts, axis=1)
        dxs, dbs, dcs, dhp, g_cat, de_x, ddte_x, ddt_x, ddsk_x = ([] for _ in range(9))
        dacs_t = jnp.zeros((LB, LB), F32)
        hsum = jnp.zeros((1, LB), F32)
        for g in range(2):
            gl = slice(g * GW, (g + 1) * GW)
            xs_g, xdt, bgb, cgb = ck.xs[g], ck.xdt[g], ck.bb[g], ck.cb16[g]
            m_st, dm_st = ck.m_st[g], ck.dm_st[g]
            dt_x, e_x, dte_x = ck.dt_x[:, gl], ck.e_x[:, gl], ck.dte_x[:, gl]
            xdtb = xdt.astype(BF16)
            hg, dhn = h_all[gl, :], dh_all[gl, :]
            hb, dhnb = hg.astype(BF16), dhn.astype(BF16)
            dy_g = dy[:, gl]
            ddsk_x.append(jnp.sum(dy_g * xs_g, axis=0, keepdims=True))
            dy_bd = ck.block_diag(dy_g).astype(BF16)
            dm4 = _dot_nt(dy_bd, xdtb)
            dxdt = _dot_tn(m_st.astype(BF16), dy_bd)
            gmat = dm4 * m_st
            dcbm = dm4 * dm_st
            dcb = dcbm[0:LB] + dcbm[LB:2 * LB] + dcbm[2 * LB:3 * LB] + dcbm[3 * LB:4 * LB]
            g_cat.append(cat([gmat[hh * LB:(hh + 1) * LB, :] for hh in range(HPG)]))
            for hh in range(HPG):
                j = HPG * g + hh
                col_sum = jnp.sum(gmat[hh * LB:(hh + 1) * LB, :], axis=0, keepdims=True)
                dacs_t = dacs_t - jnp.where(riota == j, col_sum, 0.0)
                hsl = slice(hh * HD, (hh + 1) * HD)
                hsum = hsum + jnp.where(lane1 == j, jnp.sum(dhn[hsl, :] * hg[hsl, :]), 0.0)
            dchb = (dy_g * e_x).astype(BF16)
            dcg = _dot(dchb, hb)
            dh_prev = _dot_tn(dchb, cgb)
            de_x.append(dy_g * _dot_nt(cgb, hb))
            dxs_s = _dot_nt(bgb, dhnb)
            dbg = _dot((xdt * dte_x).astype(BF16), dhnb)
            dxdt = dxdt + dxs_s * dte_x
            ddte_x.append(dxs_s * xdt)
            dhp.append(dhn * ck.elast_rows(g) + dh_prev)
            dxs.append(dy_g * ck.dsk_x[:, gl] + dxdt * dt_x)
            ddt_x.append(dxdt * xs_g)
            dcbb = dcb.astype(BF16)
            dbs.append(dbg + _dot_tn(dcbb, cgb))
            dcs.append(dcg + _dot(dcbb, bgb))
        dh_scr[...] = jnp.concatenate(dhp, axis=0)
        red = _dot_nt_sel(jnp.concatenate([cat(de_x), cat(ddte_x), cat(ddt_x)], axis=0), ck.e64, 2)
        de_c, ddte_c, ddt_c = red[0:LB], red[LB:2 * LB], red[2 * LB:3 * LB]
        ddsk = _dot_nt_sel(jnp.broadcast_to(cat(ddsk_x), (SUB, NH * HD)), ck.e64, 2)[0:1, :]
        t1 = ddte_c * ck.dte_all
        dalast = jnp.sum(t1, axis=0, keepdims=True) + hsum * ck.elast
        dacs = (_dot_nt_sel(cat(g_cat), ck.e128, 2) + de_c * ck.e_all - t1 + jnp.transpose(dacs_t)
                + jnp.where(riota == LB - 1, dalast, 0.0))
        da = _cumsum_rows(dacs, True)
        ddt = ddt_c + da * a_neg
        da_log = jnp.sum(da * dt, axis=0, keepdims=True) * a_neg
        ddtr = ddt * _sigmoid(pre)
        dzxd_ref[:, SW + D:ZXD] = ddtr.astype(BF16)
        sm_ref[6:7, 0:LB] += jnp.sum(ddtr, axis=0, keepdims=True)
        sm_ref[6:7, LB:2 * LB] += da_log
        sm_ref[6:7, 2 * LB:3 * LB] += ddsk

        du = cat(dxs + dbs + dcs) * (sg_u * (1.0 + u * (1.0 - sg_u)))
        sm_ref[0:1, :] += jnp.sum(du, axis=0, keepdims=True)
        for k in range(CONVK):
            sm_ref[1 + k:2 + k, :] += jnp.sum(du * sh[CONVK - 1 - k], axis=0, keepdims=True)
        adv = _conv_advances(du, dun_scr[...])
        dxbc = cw_ref[CONVK - 1:CONVK, :] * du
        for j in range(1, CONVK):
            dxbc = dxbc + cw_ref[CONVK - 1 - j:CONVK - j, :] * adv[j - 1]
        dun_scr[...] = du[0:SUB, :]
        dzxd_ref[:, SW:SW + D] = dxbc.astype(BF16)

    run(*refs)


def _attn_bwd_block(i, n, q_ref, kp_ref, kc_ref, vp_ref, vc_ref, o_ref, do_ref, cos_ref, sin_ref, sink_ref,
                    dq_ref, dkv_ref, ds_ref, ck_scr, cv_scr):
    @pl.when(i == 0)
    def _():
        ds_ref[...] = jnp.zeros_like(ds_ref)
        ck_scr[...] = jnp.zeros_like(ck_scr)
        cv_scr[...] = jnp.zeros_like(cv_scr)

    qv, ov, dov, sinks = q_ref[...], o_ref[...], do_ref[...], sink_ref[...]
    kcat = jnp.concatenate([kp_ref[...], kc_ref[...]], axis=0)
    vcat = jnp.concatenate([vp_ref[...], vc_ref[...]], axis=0)
    upper = _upper_mask()
    srow = lax.broadcasted_iota(jnp.int32, (8, LB), 0)
    slane = lax.broadcasted_iota(jnp.int32, (8, LB), 1)
    dsink = jnp.zeros((8, LB), F32)
    dq_g, dk_g, dv_g = [], [], []
    for g in range(NQ // QPG):
        sl = slice(g * HD, (g + 1) * HD)
        qg = _stack_heads(qv, g)
        dog = _stack_heads(dov, g)
        probs, psink = _attn_softmax(_attn_scores(n, qg, kcat[:, sl], upper), _sink_col(sinks, g))
        delta = jnp.sum(dog.astype(F32) * _stack_heads(ov, g).astype(F32), axis=1, keepdims=True)
        dsc = probs * (_band(upper, _dot_nt(dog, vcat[0:LB, sl]), _dot_nt(dog, vcat[LB:2 * LB, sl])) - delta)
        sink_terms = psink * delta
        for hh in range(QPG):
            dsink = dsink - jnp.where((srow == QPG * g + hh) & (slane == 0),
                                      jnp.sum(sink_terms[hh * LB:(hh + 1) * LB, :]), 0.0)
        ds_p = jnp.where(upper, dsc, 0.0).astype(BF16)
        ds_c = jnp.where(upper, 0.0, dsc).astype(BF16)
        dq_g.append((_dot(ds_p, kcat[0:LB, sl]) + _dot(ds_c, kcat[LB:2 * LB, sl])) * ATT_SCALE)
        dk_g.append(jnp.concatenate([_dot_tn(ds_p, qg), _dot_tn(ds_c, qg)], axis=0) * ATT_SCALE)
        dv_g.append(jnp.concatenate([_dot_tn(jnp.where(upper, probs, 0.0).astype(BF16), dog),
                                     _dot_tn(jnp.where(upper, 0.0, probs).astype(BF16), dog)], axis=0))
    ds_ref[...] += dsink
    cs = cos_ref[...]
    sn = sin_ref[...]
    dk2 = jnp.concatenate(dk_g, axis=1)
    dv2 = jnp.concatenate(dv_g, axis=1)
    for a, tile in enumerate(_unstack_heads(dq_g)):
        dq_ref[:, a * LB:(a + 1) * LB] = _rope(tile, cs, sn, True).astype(BF16)
    dkv_ref[:, 0:KVW] = _rope(ck_scr[...] + dk2[LB:2 * LB, :], cs, sn, True).astype(BF16)
    dkv_ref[:, KVW:2 * KVW] = (cv_scr[...] + dv2[LB:2 * LB, :]).astype(BF16)
    ck_scr[...] = dk2[0:LB, :]
    cv_scr[...] = dv2[0:LB, :]


def _mixer_bwd(q, k, v, o, do, cos, sin, sinks8, dyn, y, z, xbc, conv_w8, conv_b, dtr, ssm_p, nw, hs, scatters):
    s = q.shape[0]
    nb = s // LB
    cur = lambda i: (nb - 1 - i, 0)
    prev = lambda i: (jnp.maximum(nb - 2 - i, 0), 0)
    n_in = 23
    items, ex_shapes, n_g = _exchange_items([], scatters)
    ne = len(items)
    e64, e128 = _head_expanders()

    def body(*refs):
        i = pl.program_id(0)
        n = nb - 1 - i
        a_in, s_in, ex_in = refs[:10], refs[10:n_in], refs[n_in:n_in + ne]
        dp_ref, ds_ref, sm_ref = refs[n_in + ne:n_in + ne + 3]
        dq_ref, dkv_ref, dzxd_ref = dp_ref.at[:, O_Q:O_K], dp_ref.at[:, O_K:O_Z], dp_ref.at[:, O_Z:INP]
        ex_out = refs[n_in + ne + 3:n_in + 2 * ne + 3]
        ck_scr, cv_scr, dh_scr, dun_scr = refs[n_in + 2 * ne + 3:n_in + 2 * ne + 7]
        sems = refs[n_in + 2 * ne + 7:]

        @pl.when(i == 0)
        def _():
            _Exchange(n_g, ex_in, ex_out, sems).start()

        _attn_bwd_block(i, n, *a_in, dq_ref, dkv_ref, ds_ref, ck_scr, cv_scr)
        _ssd_bwd_block(i, n, *s_in, dzxd_ref, sm_ref, dh_scr, dun_scr)

        @pl.when(i == nb - 1)
        def _():
            _Exchange(n_g, ex_in, ex_out, sems).finish()

    any_spec = pl.BlockSpec(memory_space=pl.ANY)
    tail = pl.BlockSpec((SUB, D), lambda i: (jnp.maximum((nb - 1 - i) * (LB // SUB) - 1, 0), 0))
    outs = _pcall(
        body, name="mixer_bwd", grid=(nb,),
        in_specs=[pl.BlockSpec((LB, AW), cur), pl.BlockSpec((LB, KVW), prev), pl.BlockSpec((LB, KVW), cur),
                  pl.BlockSpec((LB, KVW), prev), pl.BlockSpec((LB, KVW), cur), pl.BlockSpec((LB, AW), cur),
                  pl.BlockSpec((LB, AW), cur), pl.BlockSpec((LB, LB), cur), pl.BlockSpec((LB, LB), cur),
                  _const((8, LB)),
                  pl.BlockSpec((LB, SW), cur), pl.BlockSpec((LB, SW), cur), pl.BlockSpec((LB, SW), cur),
                  tail, pl.BlockSpec((LB, D), cur), _const((8, D)), _const((1, D)),
                  pl.BlockSpec((LB, LB), cur), _const((8, LB)), _const((1, SW)),
                  pl.BlockSpec((1, NH * HD, NST), lambda i: (nb - 1 - i, 0, 0)),
                  _const(e64.shape), _const(e128.shape)] + [any_spec] * ne,
        out_specs=[pl.BlockSpec((LB, INP), cur), pl.BlockSpec((8, LB), lambda i: (0, 0)),
                   pl.BlockSpec((8, D), lambda i: (0, 0))] + [any_spec] * ne,
        out_shape=[_sds((s, INP), BF16), _sds((8, LB), F32), _sds((8, D), F32)] + ex_shapes,
        scratch_shapes=[pltpu.VMEM((LB, KVW), F32), pltpu.VMEM((LB, KVW), F32),
                        pltpu.VMEM((NH * HD, NST), F32), pltpu.VMEM((SUB, D), F32)]
        + _exchange_sems(ne),
        compiler_params=_params(),
    )(q, k, k, v, v, o, do, cos, sin, sinks8, dyn, y, z, xbc, xbc, conv_w8, conv_b, dtr, ssm_p, nw, hs, e64, e128,
      *items)
    return outs[0], outs[1], outs[2], outs[3:]


def _inproj_bwd(dproj, x, dx2, mod8, n1w, w_in_t, scatters):
    s = x.shape[0]
    tt = min(512, s)
    nt = s // tt
    items, ex_shapes, n_g = _exchange_items([], scatters)
    ne = len(items)

    def body(*refs):
        dp_ref, x_ref, dx2_ref, mod_ref, nw_ref, w_ref = refs[:6]
        ex_in = refs[6:6 + ne]
        gx_ref, sm_ref = refs[6 + ne:8 + ne]
        ex_out = refs[8 + ne:8 + 2 * ne]
        sems = refs[8 + 2 * ne:]
        i = pl.program_id(0)

        @pl.when(i == 0)
        def _():
            sm_ref[...] = jnp.zeros_like(sm_ref)
            _Exchange(n_g, ex_in, ex_out, sems).start()

        dh1 = _dot(dp_ref[...], w_ref[...])
        dxn, d_shift, d_scale, d_w = _norm_mod_bwd(x_ref[...], dh1, nw_ref[...], mod_ref[1:2, :])
        gx_ref[...] = dx2_ref[...] + dxn
        sm_ref[0:1, :] += d_shift
        sm_ref[1:2, :] += d_scale
        sm_ref[2:3, :] += d_w

        @pl.when(i == nt - 1)
        def _():
            _Exchange(n_g, ex_in, ex_out, sems).finish()

    any_spec = pl.BlockSpec(memory_space=pl.ANY)
    outs = _pcall(
        body, name="inproj_bwd", grid=(nt,),
        in_specs=[_rows(tt, INP), _rows(tt, D), _rows(tt, D), _const((8, D)), _const((1, D)), _const((INP, D))]
        + [any_spec] * ne,
        out_specs=[_rows(tt, D), pl.BlockSpec((8, D), lambda i: (0, 0))] + [any_spec] * ne,
        out_shape=[_sds((s, D), F32), _sds((8, D), F32)] + ex_shapes,
        scratch_shapes=_exchange_sems(ne),
        compiler_params=_params(),
    )(dproj, x, dx2, mod8, n1w, w_in_t, *items)
    return outs[0], outs[1], outs[2:]


def _wgrad(a, b, name):
    s, m = a.shape
    n = b.shape[1]
    tk = min(1024, s)
    wide = (1408, 1024, 512)
    tm = next((t for t in wide if m % t == 0), m)
    tn = n if n <= 2048 else _largest_divisor(n, wide)
    nk = s // tk

    def body(a_ref, b_ref, o_ref, acc):
        kk = pl.program_id(2)

        @pl.when(kk == 0)
        def _():
            acc[...] = jnp.zeros_like(acc)

        acc[...] += _dot_tn(a_ref[...], b_ref[...])

        @pl.when(kk == nk - 1)
        def _():
            o_ref[...] = acc[...].astype(BF16)

    return _pcall(
        body, name=name, grid=(m // tm, n // tn, nk),
        in_specs=[pl.BlockSpec((tk, tm), lambda i, j, kk: (kk, i)), pl.BlockSpec((tk, tn), lambda i, j, kk: (kk, j))],
        out_specs=pl.BlockSpec((tm, tn), lambda i, j, kk: (i, j)),
        out_shape=_sds((m, n), BF16),
        scratch_shapes=[pltpu.VMEM((tm, tn), F32)],
        compiler_params=_params(3),
    )(a, b)


PACK_ROWS = 24


def _pack_small(sm_f, sm_b, sm_s, sm_i, dsink):
    def body(f_ref, b_ref, s_ref, i_ref, k_ref, o_ref):
        o_ref[...] = jnp.zeros_like(o_ref)
        o_ref[0:2, :] = i_ref[0:2, :]
        o_ref[2:3, :] = b_ref[3:4, :]
        o_ref[3:5, :] = b_ref[0:2, :]
        o_ref[5:6, :] = f_ref[0:1, :]
        o_ref[6:7, :] = i_ref[2:3, :]
        o_ref[7:8, :] = b_ref[2:3, :]
        o_ref[8:9, :] = f_ref[1:2, :]
        o_ref[9:14, :] = s_ref[0:5, :]
        o_ref[14:15, :] = s_ref[5:6, :]
        o_ref[15:16, 0:3 * LB] = s_ref[6:7, 0:3 * LB]
        lane = lax.broadcasted_iota(jnp.int32, (1, LB), 1)
        sk = jnp.zeros((1, LB), F32)
        for h in range(NQ):
            sk = sk + jnp.where(lane == h, k_ref[h:h + 1, 0:1], 0.0)
        o_ref[15:16, 3 * LB:4 * LB] = sk
        o_ref[16:17, :] = f_ref[2:3, :]

    return _pcall(body, name="pack_small", out_shape=_sds((PACK_ROWS, D), F32))(sm_f, sm_b, sm_s, sm_i, dsink)


def _exchange(gathers, scatters, name, two_level=False):
    items, shapes, n_g = _exchange_items(gathers, scatters)
    n = len(items)
    assert not (two_level and scatters)

    def body(*refs):
        ex = _Exchange(n_g, refs[:n], refs[n:2 * n], refs[2 * n:])
        if two_level:
            ex.gather_two_level()
        else:
            ex.start()
            ex.finish()

    any_spec = pl.BlockSpec(memory_space=pl.ANY)
    return _pcall(
        body, name=name, in_specs=[any_spec] * n, out_specs=[any_spec] * n, out_shape=shapes,
        scratch_shapes=_exchange_sems(n),
    )(*items)


def _exchange_items(gathers, scatters):
    items = list(gathers) + list(scatters)
    shapes = [_sds((N_DEV,) + a.shape, a.dtype) for a in gathers] + [_sds(a.shape, a.dtype) for a in scatters]
    return items, shapes, len(gathers)


def _exchange_sems(n):
    return [pltpu.SemaphoreType.DMA((n, N_DEV - 1)), pltpu.SemaphoreType.DMA((n, N_DEV - 1)),
            pltpu.SemaphoreType.DMA((n,))]


class _Exchange:
    def __init__(self, n_g, ins, outs, sems):
        self.n_g, self.ins, self.outs = n_g, ins, outs
        self.send_sems, self.recv_sems, self.loc_sems = sems
        xi, yi, ci = lax.axis_index("x"), lax.axis_index("y"), lax.axis_index("c")
        self.me = 4 * xi + 2 * yi + ci
        self.peers = []
        for r in range(1, N_DEV):
            px = 1 - xi if r & 4 else xi
            py = 1 - yi if r & 2 else yi
            pc = 1 - ci if r & 1 else ci
            self.peers.append(((px, py, pc), 4 * px + 2 * py + pc))

    def _copy(self, t, r, landing):
        dev, peer = self.peers[r]
        src = self.ins[t] if t < self.n_g else self.ins[t].at[peer]
        return pltpu.make_async_remote_copy(
            src_ref=src, dst_ref=self.outs[t].at[landing], send_sem=self.send_sems.at[t, r],
            recv_sem=self.recv_sems.at[t, r], device_id=dev, device_id_type=pl.DeviceIdType.MESH)

    def _local(self, t):
        src = self.ins[t] if t < self.n_g else self.ins[t].at[self.me]
        return pltpu.make_async_copy(src, self.outs[t].at[self.me], self.loc_sems.at[t])

    def start(self):
        for t in range(len(self.ins)):
            self._local(t).start()
            for r in range(N_DEV - 1):
                self._copy(t, r, self.me).start()

    def finish(self):
        n = len(self.ins)
        for t in range(n):
            for r in range(N_DEV - 1):
                self._copy(t, r, self.peers[r][1]).wait_recv()
        for t in range(n):
            for r in range(N_DEV - 1):
                self._copy(t, r, self.me).wait_send()
            self._local(t).wait()

    def gather_two_level(self):
        self.two_level_start()
        self.two_level_relay()
        self.two_level_finish()

    DIRECT = (0, 1, 3, 5)

    def two_level_start(self):
        for t in range(len(self.ins)):
            self._local(t).start()
            for r in self.DIRECT:
                self._copy(t, r, self.me).start()

    def _relay(self, t, r):
        peer = self.peers[r][1]
        return pltpu.make_async_remote_copy(
            src_ref=self.outs[t].at[peer], dst_ref=self.outs[t].at[peer], send_sem=self.send_sems.at[t, r + 1],
            recv_sem=self.recv_sems.at[t, r + 1], device_id=self.peers[0][0], device_id_type=pl.DeviceIdType.MESH)

    def two_level_relay(self):
        for t in range(len(self.ins)):
            for r in self.DIRECT[1:]:
                self._copy(t, r, self.peers[r][1]).wait_recv()
                self._relay(t, r).start()

    def two_level_finish(self):
        n = len(self.ins)
        for t in range(n):
            for r in (0, 2, 4, 6):
                self._copy(t, r, self.peers[r][1]).wait_recv()
        for t in range(n):
            for r in self.DIRECT:
                self._copy(t, r, self.me).wait_send()
            for r in self.DIRECT[1:]:
                self._relay(t, r).wait_send()
            self._local(t).wait()


def _ada_fwd(c_all, w_cols, b_cols):
    def body(c_ref, w_ref, b_ref, o_ref):
        cv = c_ref[...]
        sc = (cv * _sigmoid(cv)).astype(BF16)
        o_ref[...] = _dot(sc, w_ref[...].astype(BF16)) + b_ref[...]

    return _pcall(body, name="ada_fwd", out_shape=_sds((N_DEV, w_cols.shape[1]), F32),
                  compiler_params=_params(0))(c_all, w_cols, b_cols)


def _adamw(w, g, m, v):
    m2 = ADAM_B1 * m + (1.0 - ADAM_B1) * g
    v2 = ADAM_B2 * v + (1.0 - ADAM_B2) * (g * g)
    m_hat = m2 / (1.0 - ADAM_B1 ** ADAM_STEP)
    v_hat = v2 / (1.0 - ADAM_B2 ** ADAM_STEP)
    delta = -ADAM_LR * (m_hat / (jnp.sqrt(v_hat) + ADAM_EPS) + ADAM_WD * w)
    return delta, m2, v2


def _sum_adamw(parts, w, m, v, name):
    rws, cols = w.shape
    tr = next((t for t in (256, 176, 128) if rws % t == 0), rws)

    def body(p_ref, w_ref, m_ref, v_ref, g_ref, d_ref, mo_ref, vo_ref):
        g = p_ref[0].astype(F32)
        for dev in range(1, N_DEV):
            g = g + p_ref[dev].astype(F32)
        g_ref[...] = g
        d_ref[...], mo_ref[...], vo_ref[...] = _adamw(w_ref[...], g, m_ref[...], v_ref[...])

    blk = pl.BlockSpec((tr, cols), lambda i: (i, 0))
    return _pcall(
        body, name=name, grid=(rws // tr,),
        in_specs=[pl.BlockSpec((N_DEV, tr, cols), lambda i: (0, i, 0)), blk, blk, blk],
        out_specs=[blk] * 4, out_shape=[_sds((rws, cols), F32)] * 4, compiler_params=_params(),
    )(parts, w, m, v)


def _wada_adamw(c_all, dmod_cols, w, m, v):
    rws, cols = w.shape
    tr = 256

    def body(c_ref, dm_ref, w_ref, m_ref, v_ref, g_ref, d_ref, mo_ref, vo_ref):
        cv = c_ref[...]
        sc = (cv * _sigmoid(cv)).astype(BF16)
        g = _dot_tn(sc, dm_ref[...].astype(BF16))
        g_ref[...] = g
        d_ref[...], mo_ref[...], vo_ref[...] = _adamw(w_ref[...], g, m_ref[...], v_ref[...])

    blk = pl.BlockSpec((tr, cols), lambda i: (i, 0))
    return _pcall(
        body, name="wada_adamw", grid=(rws // tr,),
        in_specs=[pl.BlockSpec((N_DEV, tr), lambda i: (0, i)), pl.BlockSpec((N_DEV, cols), lambda i: (0, 0)),
                  blk, blk, blk],
        out_specs=[blk] * 4, out_shape=[_sds((rws, cols), F32)] * 4, compiler_params=_params(),
    )(c_all, dmod_cols, w, m, v)


def _small_reduce(packs):
    def body(p_ref, o_ref):
        tot = p_ref[0]
        for dev in range(1, N_DEV):
            tot = tot + p_ref[dev]
        o_ref[...] = tot
        o_ref[16:17, :] = jnp.zeros((1, D), F32) + (0.5 / D) * jnp.sum(tot[16:17, :])

    return _pcall(body, name="small_reduce", out_shape=_sds((PACK_ROWS, D), F32))(packs)


def _adamw_many(ws, gs, ms, vs):
    k = len(ws)

    def body(*refs):
        for i in range(k):
            w_ref, g_ref, m_ref, v_ref = refs[i], refs[k + i], refs[2 * k + i], refs[3 * k + i]
            d_ref, mo_ref, vo_ref = refs[4 * k + i], refs[5 * k + i], refs[6 * k + i]
            d_ref[...], mo_ref[...], vo_ref[...] = _adamw(w_ref[...], g_ref[...], m_ref[...], v_ref[...])

    shp = [_sds(w.shape, F32) for w in ws]
    outs = _pcall(body, name="adamw_small", out_shape=shp * 3)(*ws, *gs, *ms, *vs)
    return outs[:k], outs[k:2 * k], outs[2 * k:]


def kernel(x, c, positions, w_ada, b_ada, norm1_w, w_in, conv_w, conv_b, dt_bias, a_log, d_skip, attn_sinks, ssm_norm_w, w_out, norm2_w, w_gate_up, w_down, final_norm_w, loss_target, m_w_ada, m_b_ada, m_norm1_w, m_w_in, m_conv_w, m_conv_b, m_dt_bias, m_a_log, m_d_skip, m_attn_sinks, m_ssm_norm_w, m_w_out, m_norm2_w, m_w_gate_up, m_w_down, m_final_norm_w, v_w_ada, v_b_ada, v_norm1_w, v_w_in, v_conv_w, v_conv_b, v_dt_bias, v_a_log, v_d_skip, v_attn_sinks, v_ssm_norm_w, v_w_out, v_norm2_w, v_w_gate_up, v_w_down, v_final_norm_w):
    s = x.shape[1]
    me = 4 * lax.axis_index("x") + 2 * lax.axis_index("y") + lax.axis_index("c")
    ada_cols = N_MOD * D // N_DEV

    c8 = jnp.pad(c, ((0, 7), (0, 0)))
    cw8 = jnp.pad(conv_w[0], ((0, 8 - CONVK), (0, 0)))
    w_in_t, m_w_in_t, v_w_in_t = jnp.transpose(w_in[0]), jnp.transpose(m_w_in[0]), jnp.transpose(v_w_in[0])
    w_gu_t, m_w_gu_t, v_w_gu_t = (jnp.transpose(w_gate_up[0]), jnp.transpose(m_w_gate_up[0]),
                                  jnp.transpose(v_w_gate_up[0]))
    g_c, g_in, g_cw = _exchange([c8, w_in_t.astype(BF16), cw8], [], "gather_in", two_level=True)
    c_all = g_c[:, 0, :]
    w_in_f = jnp.pad(g_in.reshape(IN_PROJ, D), ((0, INP - IN_PROJ), (0, 0)))
    conv_w8 = jnp.transpose(g_cw, (1, 0, 2)).reshape(8, D)

    b_cols = lax.dynamic_slice(b_ada, (0, me * ada_cols), (1, ada_cols))
    (g_mod,) = _exchange([_ada_fwd(c_all, w_ada[0], b_cols)], [], "gather_mod")
    mod = lax.dynamic_index_in_dim(g_mod, me, axis=1, keepdims=False).reshape(N_MOD, D)
    mod8 = jnp.pad(mod, ((0, 8 - N_MOD), (0, 0)))

    half = HD // 2
    inv_freq = ROPE_THETA ** (-jnp.arange(half, dtype=F32) / half)
    invf = jnp.tile(inv_freq, LB // half).reshape(1, LB)
    lanes = lambda a: jnp.pad(a, ((0, 0), (0, LB - a.shape[1])))
    ssm_p = jnp.pad(jnp.concatenate([lanes(dt_bias), lanes(a_log), lanes(d_skip)], axis=0), ((0, 5), (0, 0)))
    sinks8 = jnp.broadcast_to(attn_sinks.reshape(NQ, 1), (NQ, LB))

    xs, tgt, fnw = x[0], loss_target[0], final_norm_w.reshape(1, D)

    q, k, v, z, xbc, dtr, h1, cos, sin = _inproj_fwd(xs, positions[0].reshape(s, 1), invf, mod8, norm1_w, w_in_f)
    attn, yn, y, hs, (g_out, g_gu, g_down) = _mixer_fwd(
        q, k, v, sinks8, xbc, conv_w8, conv_b, dtr, ssm_p, z, ssm_norm_w,
        [w_out[0].astype(BF16), w_gu_t.astype(BF16), w_down[0].astype(BF16)])
    w_out_f = g_out.reshape(D, D)
    w_gu_f = g_gu.reshape(2 * DFF, D)
    w_down_f = g_down.reshape(DFF, D)
    x2, h2, mo, mix, gu, act, dx3, sm_f = _outproj_ffn_fwd_loss(attn, yn, xs, tgt, mod8, norm2_w, fnw, w_out_f, w_gu_f,
                                                                 w_down_f)

    dx2, dff, dgu, dmix, dattn, dyn, sm_b = _ffn_bwd(dx3, gu, x2, mo, mod8, norm2_w, w_gu_f, w_down_f, w_out_f)
    p_gu = _wgrad(dgu, h2, "wgrad_gate_up").reshape(N_DEV, 2 * DFF // N_DEV, D)
    p_down = _wgrad(act, dff, "wgrad_down").reshape(N_DEV, DFF // N_DEV, D)
    p_out = _wgrad(mix, dmix, "wgrad_out").reshape(N_DEV, D // N_DEV, D)
    dproj, dsink, sm_s, (r_gu, r_down, r_out) = _mixer_bwd(
        q, k, v, attn, dattn, cos, sin, sinks8, dyn, y, z, xbc, conv_w8, conv_b, dtr, ssm_p, ssm_norm_w, hs,
        [p_gu, p_down, p_out])
    p_in = _wgrad(dproj, h1, "wgrad_in")[:IN_PROJ].reshape(N_DEV, IN_PROJ // N_DEV, D)
    gx, sm_i, (r_in,) = _inproj_bwd(dproj, xs, dx2, mod8, norm1_w, w_in_f, [p_in])
    (g_pack,) = _exchange([_pack_small(sm_f, sm_b, sm_s, sm_i, dsink)], [], "gather_small")

    tot = _small_reduce(g_pack)
    loss = tot[16, 0]
    dmod_all = g_pack[:, 0:N_MOD, :].reshape(N_DEV, N_MOD * D)
    dmod_cols = lax.dynamic_slice(dmod_all, (0, me * ada_cols), (N_DEV, ada_cols))

    big = {
        "w_ada": _wada_adamw(c_all, dmod_cols, w_ada[0], m_w_ada[0], v_w_ada[0]),
        "w_in": [jnp.transpose(t) for t in _sum_adamw(r_in, w_in_t, m_w_in_t, v_w_in_t, "adamw_in")],
        "w_out": _sum_adamw(r_out, w_out[0], m_w_out[0], v_w_out[0], "adamw_out"),
        "w_gate_up": [jnp.transpose(t) for t in _sum_adamw(r_gu, w_gu_t, m_w_gu_t, v_w_gu_t, "adamw_gate_up")],
        "w_down": _sum_adamw(r_down, w_down[0], m_w_down[0], v_w_down[0], "adamw_down"),
    }
    small_names = ["b_ada", "norm1_w", "conv_w", "conv_b", "dt_bias", "a_log", "d_skip", "attn_sinks", "ssm_norm_w",
                   "norm2_w", "final_norm_w"]
    row15 = tot[15:16, :]
    small_g = {
        "b_ada": tot[0:N_MOD, :].reshape(1, N_MOD * D),
        "norm1_w": tot[6:7, :],
        "conv_w": lax.dynamic_slice(tot[10:14, :], (0, me * (D // N_DEV)), (CONVK, D // N_DEV)),
        "conv_b": tot[9:10, :],
        "dt_bias": row15[:, 0:NH],
        "a_log": row15[:, LB:LB + NH],
        "d_skip": row15[:, 2 * LB:2 * LB + NH],
        "attn_sinks": row15[:, 3 * LB:3 * LB + NQ],
        "ssm_norm_w": tot[14:15, 0:SW],
        "norm2_w": tot[7:8, :],
        "final_norm_w": tot[8:9, :],
    }
    small_w = {"b_ada": b_ada, "norm1_w": norm1_w, "conv_w": conv_w[0], "conv_b": conv_b, "dt_bias": dt_bias,
               "a_log": a_log, "d_skip": d_skip, "attn_sinks": attn_sinks, "ssm_norm_w": ssm_norm_w,
               "norm2_w": norm2_w, "final_norm_w": final_norm_w.reshape(1, D)}
    small_m = {"b_ada": m_b_ada, "norm1_w": m_norm1_w, "conv_w": m_conv_w[0], "conv_b": m_conv_b,
               "dt_bias": m_dt_bias, "a_log": m_a_log, "d_skip": m_d_skip, "attn_sinks": m_attn_sinks,
               "ssm_norm_w": m_ssm_norm_w, "norm2_w": m_norm2_w, "final_norm_w": m_final_norm_w.reshape(1, D)}
    small_v = {"b_ada": v_b_ada, "norm1_w": v_norm1_w, "conv_w": v_conv_w[0], "conv_b": v_conv_b,
               "dt_bias": v_dt_bias, "a_log": v_a_log, "d_skip": v_d_skip, "attn_sinks": v_attn_sinks,
               "ssm_norm_w": v_ssm_norm_w, "norm2_w": v_norm2_w, "final_norm_w": v_final_norm_w.reshape(1, D)}
    s_d, s_m, s_v = _adamw_many([small_w[k] for k in small_names], [small_g[k] for k in small_names],
                                [small_m[k] for k in small_names], [small_v[k] for k in small_names])

    order = ["w_ada", "b_ada", "norm1_w", "w_in", "conv_w", "conv_b", "dt_bias", "a_log", "d_skip", "attn_sinks",
             "ssm_norm_w", "w_out", "norm2_w", "w_gate_up", "w_down", "final_norm_w"]
    lead = {"w_ada", "w_in", "conv_w", "w_out", "w_gate_up", "w_down"}
    grads, deltas, new_m, new_v = [], [], [], []
    for name in order:
        if name in big:
            g, d, m2, v2 = big[name]
        else:
            i = small_names.index(name)
            g, d, m2, v2 = small_g[name], s_d[i], s_m[i], s_v[i]
        if name in lead:
            g, d, m2, v2 = g[None], d[None], m2[None], v2[None]
        if name == "final_norm_w":
            g, d, m2, v2 = g.reshape(D), d.reshape(D), m2.reshape(D), v2.reshape(D)
        grads.append(g)
        deltas.append(d)
        new_m.append(m2)
        new_v.append(v2)
    return (loss, gx[None], *grads, *deltas, *new_m, *new_v)
```

```python
import functools
import math

import jax
import jax.numpy as jnp
from jax import lax
from jax.experimental import pallas as pl
from jax.experimental.pallas import tpu as pltpu

F32 = jnp.float32
BF16 = jnp.bfloat16

N_DEV = 8
D = 1024
HD = 64
NQ = 8
AW = 512
KVW = 128
SW = 512
NST = 128
NH = 8
LB = 128
CONVK = 4
DFF = 2816
N_MOD = 6
IN_PROJ = 2312
INP = 2432
O_Q, O_K, O_V, O_Z, O_XBC, O_DT = 0, 512, 640, 768, 1280, 2304
ZXD = INP - O_Z
EPS = 1e-6
NEG = -1e30
ROPE_THETA = 10000.0
VMEM_LIMIT = 56 * 1024 * 1024

ADAM_LR = 0.001
ADAM_B1 = 0.9
ADAM_B2 = 0.999
ADAM_EPS = 1e-08
ADAM_WD = 0.01
ADAM_STEP = 10

NT_DIMS = (((1,), (1,)), ((), ()))
TN_DIMS = (((0,), (0,)), ((), ()))


def _pcall(body, **kw):
    return pl.pallas_call(body, **kw)


def _sds(shape, dtype):
    return jax.ShapeDtypeStruct(shape, dtype)


def _params(n_grid=1):
    return pltpu.CompilerParams(dimension_semantics=("arbitrary",) * n_grid, vmem_limit_bytes=VMEM_LIMIT)


def _const(shape):
    return pl.BlockSpec(shape, lambda *_: (0,) * len(shape), pipeline_mode=pl.Buffered(1))


def _largest_divisor(n, candidates):
    for cand in candidates:
        if n % cand == 0:
            return cand
    raise ValueError(f"no tile in {candidates} divides {n}")


def _rows(t, w):
    return pl.BlockSpec((t, w), lambda i: (i, 0))


def _dot(a, b):
    return jnp.dot(a, b, preferred_element_type=F32)


def _dot_nt(a, b):
    return lax.dot_general(a, b, NT_DIMS, preferred_element_type=F32)


def _dot_tn(a, b):
    return lax.dot_general(a, b, TN_DIMS, preferred_element_type=F32)


def _sigmoid(v):
    return 1.0 / (1.0 + jnp.exp(-v))


def _softplus(v):
    return jnp.maximum(v, 0.0) + jnp.log1p(jnp.exp(-jnp.abs(v)))


def _rope_sign_mask(shape):
    lane = lax.broadcasted_iota(jnp.int32, shape, 1)
    return (lane % HD) < (HD // 2)


def _rope(t, cs, sn, inverse):
    r_dn = pltpu.roll(t, HD // 2, 1)
    r_up = pltpu.roll(t, LB - HD // 2, 1)
    first = _rope_sign_mask(t.shape)
    if inverse:
        rot = jnp.where(first, r_up, -r_dn)
    else:
        rot = jnp.where(first, -r_up, r_dn)
    return t * cs + rot * sn


def _norm_mod_fwd(xv, nw, shift, scale):
    r = lax.rsqrt(jnp.mean(xv * xv, axis=-1, keepdims=True) + EPS)
    xh = xv * r
    return (xh * nw) * (1.0 + scale) + shift


def _norm_mod_bwd(xv, dh, nw, scale):
    r = lax.rsqrt(jnp.mean(xv * xv, axis=-1, keepdims=True) + EPS)
    xh = xv * r
    xn = xh * nw
    d_shift = jnp.sum(dh, axis=0, keepdims=True)
    d_scale = jnp.sum(dh * xn, axis=0, keepdims=True)
    dxn = dh * (1.0 + scale)
    d_w = jnp.sum(dxn * xh, axis=0, keepdims=True)
    dxh = dxn * nw
    dx = r * (dxh - xh * jnp.mean(dxh * xh, axis=-1, keepdims=True))
    return dx, d_shift, d_scale, d_w


def _inproj_fwd(x, pos, invf, mod8, n1w, w_in):
    s = x.shape[0]
    tt = min(512, s)

    def body(x_ref, pos_ref, invf_ref, mod_ref, nw_ref, w_ref,
             q_ref, k_ref, v_ref, z_ref, xbc_ref, dtr_ref, h1_ref, cos_ref, sin_ref):
        h = _norm_mod_fwd(x_ref[...], nw_ref[...], mod_ref[0:1, :], mod_ref[1:2, :])
        hb = h.astype(BF16)
        h1_ref[...] = hb
        proj = _dot_nt(hb, w_ref[...])
        ang = pos_ref[...].astype(F32) * invf_ref[...]
        cs = jnp.cos(ang)
        sn = jnp.sin(ang)
        cos_ref[...] = cs
        sin_ref[...] = sn
        for a in range(AW // LB):
            q_ref[:, a * LB:(a + 1) * LB] = _rope(proj[:, O_Q + a * LB:O_Q + (a + 1) * LB], cs, sn, False).astype(BF16)
        k_ref[...] = _rope(proj[:, O_K:O_V], cs, sn, False).astype(BF16)
        v_ref[...] = proj[:, O_V:O_Z].astype(BF16)
        z_ref[...] = proj[:, O_Z:O_XBC]
        xbc_ref[...] = proj[:, O_XBC:O_DT]
        dtr_ref[...] = proj[:, O_DT:INP]

    return _pcall(
        body, name="inproj_fwd", grid=(s // tt,),
        in_specs=[_rows(tt, D), _rows(tt, 1), _const((1, LB)), _const((8, D)), _const((1, D)), _const((INP, D))],
        out_specs=[_rows(tt, AW), _rows(tt, KVW), _rows(tt, KVW), _rows(tt, SW), _rows(tt, D), _rows(tt, LB),
                   _rows(tt, D), _rows(tt, LB), _rows(tt, LB)],
        out_shape=[_sds((s, AW), BF16), _sds((s, KVW), BF16), _sds((s, KVW), BF16), _sds((s, SW), F32),
                   _sds((s, D), F32), _sds((s, LB), F32), _sds((s, D), BF16), _sds((s, LB), F32), _sds((s, LB), F32)],
        compiler_params=_params(),
    )(x, pos, invf, mod8, n1w, w_in)


QPG = 4
ATT_SCALE = 1.0 / math.sqrt(HD)


def _stack_heads(val, g):
    return jnp.concatenate([val[:, (QPG * g + hh) * HD:(QPG * g + hh + 1) * HD] for hh in range(QPG)], axis=0)


def _unstack_heads(groups):
    pieces = [grp[hh * LB:(hh + 1) * LB, :] for grp in groups for hh in range(QPG)]
    return [jnp.concatenate(pieces[2 * a:2 * a + 2], axis=1) for a in range(NQ // 2)]


def _upper_mask():
    row = lax.broadcasted_iota(jnp.int32, (QPG * LB, LB), 0)
    col = lax.broadcasted_iota(jnp.int32, (QPG * LB, LB), 1)
    return col > (row % LB)


def _sink_col(sinks, g):
    return jnp.concatenate([jnp.broadcast_to(sinks[QPG * g + hh:QPG * g + hh + 1, 0:1], (LB, 1))
                            for hh in range(QPG)], axis=0)


def _band(upper, prev_part, cur_part):
    return jnp.where(upper, prev_part, cur_part)


def _attn_scores(n, qg, kcat, upper):
    sp = _dot_nt(qg, kcat[0:LB, :]) * ATT_SCALE
    sc = _dot_nt(qg, kcat[LB:2 * LB, :]) * ATT_SCALE
    return _band(upper, jnp.where(n > 0, sp, NEG), sc)


def _attn_softmax(comb, sink):
    m = jnp.maximum(jnp.max(comb, axis=-1, keepdims=True), sink)
    p = jnp.exp(comb - m)
    es = jnp.exp(sink - m)
    denom = jnp.sum(p, axis=-1, keepdims=True) + es
    return p / denom, es / denom


def _attn_fwd_block(n, q_ref, kp_ref, kc_ref, vp_ref, vc_ref, sink_ref, o_ref):
    qv = q_ref[...]
    kcat = jnp.concatenate([kp_ref[...], kc_ref[...]], axis=0)
    vcat = jnp.concatenate([vp_ref[...], vc_ref[...]], axis=0)
    sinks = sink_ref[...]
    upper = _upper_mask()
    outs = []
    for g in range(NQ // QPG):
        sl = slice(g * HD, (g + 1) * HD)
        probs, _ = _attn_softmax(_attn_scores(n, _stack_heads(qv, g), kcat[:, sl], upper), _sink_col(sinks, g))
        outs.append(_dot(jnp.where(upper, probs, 0.0).astype(BF16), vcat[0:LB, sl])
                    + _dot(jnp.where(upper, 0.0, probs).astype(BF16), vcat[LB:2 * LB, sl]))
    for g, grp in enumerate(outs):
        for hh in range(QPG):
            h = QPG * g + hh
            o_ref[:, h * HD:(h + 1) * HD] = grp[hh * LB:(hh + 1) * LB, :].astype(BF16)


def _cumsum_rows(a, reverse):
    row = lax.broadcasted_iota(jnp.int32, a.shape, 0)
    step = 1
    while step < LB:
        if reverse:
            a = a + jnp.where(row < LB - step, pltpu.roll(a, LB - step, 0), 0.0)
        else:
            a = a + jnp.where(row >= step, pltpu.roll(a, step, 0), 0.0)
        step *= 2
    return a


SUB = 8


def _conv_shifts(tail, cur):
    row = lax.broadcasted_iota(jnp.int32, tail.shape, 0)
    out = [cur]
    for j in range(1, CONVK):
        rolled = pltpu.roll(cur, j, 0)
        top = jnp.where(row < j, pltpu.roll(tail, j, 0), rolled[0:SUB, :])
        out.append(jnp.concatenate([top, rolled[SUB:, :]], axis=0))
    return out


def _conv_advances(du, head):
    row = lax.broadcasted_iota(jnp.int32, head.shape, 0)
    out = []
    for j in range(1, CONVK):
        rolled = pltpu.roll(du, LB - j, 0)
        bottom = jnp.where(row >= SUB - j, pltpu.roll(head, SUB - j, 0), rolled[LB - SUB:, :])
        out.append(jnp.concatenate([rolled[:LB - SUB, :], bottom], axis=0))
    return out


def _split(v, terms):
    out = []
    for _ in range(terms - 1):
        t = v.astype(BF16)
        out.append(t)
        v = v - t.astype(F32)
    out.append(v.astype(BF16))
    return out


def _dot_sel(v, sel, terms):
    parts = [_dot(t, sel) for t in _split(v, terms)]
    return functools.reduce(lambda a, b: a + b, parts)


def _dot_nt_sel(v, sel, terms):
    parts = [_dot_nt(t, sel) for t in _split(v, terms)]
    return functools.reduce(lambda a, b: a + b, parts)


def _ssd_pre(xt_ref, xc_ref, cw_ref, cb_ref, dtr_ref, sp_ref, n):
    cur = xc_ref[...]
    tail = jnp.where(n > 0, xt_ref[...], 0.0)
    sh = _conv_shifts(tail, cur)
    u = cb_ref[...] + cw_ref[CONVK - 1:CONVK, :] * sh[0]
    for j in range(1, CONVK):
        u = u + cw_ref[CONVK - 1 - j:CONVK - j, :] * sh[j]
    sg_u = _sigmoid(u)
    xc = u * sg_u
    pre = dtr_ref[...] + sp_ref[0:1, :]
    dt = _softplus(pre)
    a_neg = -jnp.exp(sp_ref[1:2, :])
    acs = _cumsum_rows(dt * a_neg, False)
    return sh, u, sg_u, xc, pre, dt, a_neg, acs


def _gated_norm_fwd(y, z, nw):
    sz = z * _sigmoid(z)
    yz = y * sz
    parts = []
    for g in range(2):
        t = yz[:, g * 256:(g + 1) * 256]
        parts.append(t * lax.rsqrt(jnp.mean(t * t, axis=-1, keepdims=True) + EPS))
    return jnp.concatenate(parts, axis=1) * nw


HPG = 4
GW = HPG * HD


class _SsdChunk:
    def __init__(self, xc, dt, acs, spv, e64, e128):
        self.e64, self.e128 = e64, e128
        self.acs_t = jnp.transpose(acs)
        alast = acs[LB - 1:LB, :]
        self.e_all = jnp.exp(acs)
        self.dte_all = jnp.exp(alast - acs)
        self.elast = jnp.exp(alast)
        wide = _dot_sel(jnp.concatenate([dt, self.e_all, self.dte_all], axis=0), e64, 1)
        self.dt_x, self.e_x, self.dte_x = wide[0:LB], wide[LB:2 * LB], wide[2 * LB:3 * LB]
        self.dsk_x = _dot_sel(spv, e64, 3)[2:3, :]
        ac_x = _dot_sel(acs, e128, 3)
        row = lax.broadcasted_iota(jnp.int32, (HPG * LB, LB), 0)
        col = lax.broadcasted_iota(jnp.int32, (HPG * LB, LB), 1)
        causal = (row % LB) >= col
        lane = lax.broadcasted_iota(jnp.int32, (LB, GW), 1)
        self.head_lanes = [(lane >= hh * HD) & (lane < (hh + 1) * HD) for hh in range(HPG)]
        self.xs, self.xdt, self.b, self.c, self.bb, self.cb16, self.cbm, self.dm_st, self.m_st = ([] for _ in range(9))
        for g in range(2):
            heads = range(HPG * g, HPG * (g + 1))
            ac_st = jnp.concatenate([ac_x[:, j * LB:(j + 1) * LB] for j in heads], axis=0)
            ar_st = jnp.concatenate([jnp.broadcast_to(self.acs_t[j:j + 1, :], (LB, LB)) for j in heads], axis=0)
            dm_st = jnp.exp(jnp.where(causal, ac_st - ar_st, NEG))
            bg = xc[:, SW + g * NST:SW + (g + 1) * NST]
            cg = xc[:, SW + 2 * NST + g * NST:SW + 2 * NST + (g + 1) * NST]
            bgb, cgb = bg.astype(BF16), cg.astype(BF16)
            cbm = _dot_nt(cgb, bgb)
            xs_g = xc[:, g * GW:(g + 1) * GW]
            self.xs.append(xs_g)
            self.xdt.append(xs_g * self.dt_x[:, g * GW:(g + 1) * GW])
            self.b.append(bg)
            self.c.append(cg)
            self.bb.append(bgb)
            self.cb16.append(cgb)
            self.cbm.append(cbm)
            self.dm_st.append(dm_st)
            self.m_st.append(jnp.concatenate([cbm] * HPG, axis=0) * dm_st)

    def elast_rows(self, g):
        return jnp.concatenate([jnp.broadcast_to(self.elast[:, j:j + 1], (HD, NST))
                                for j in range(HPG * g, HPG * (g + 1))], axis=0)

    def diag_blocks(self, stacked):
        out = stacked[(HPG - 1) * LB:HPG * LB, :]
        for hh in range(HPG - 2, -1, -1):
            out = jnp.where(self.head_lanes[hh], stacked[hh * LB:(hh + 1) * LB, :], out)
        return out

    def block_diag(self, v):
        return jnp.concatenate([jnp.where(self.head_lanes[hh], v, 0.0) for hh in range(HPG)], axis=0)


def _ssd_fwd_block(n, xt_ref, xc_ref, cw_ref, cb_ref, dtr_ref, sp_ref, z_ref, nw_ref, e64_ref, e128_ref,
                   yn_ref, y_ref, hs_ref, h_scr):
    @pl.when(n == 0)
    def _():
        h_scr[...] = jnp.zeros_like(h_scr)

    h_all = h_scr[...]
    hs_ref[0] = h_all
    _, _, _, xc, _, dt, _, acs = _ssd_pre(xt_ref, xc_ref, cw_ref, cb_ref, dtr_ref, sp_ref, n)
    ck = _SsdChunk(xc, dt, acs, sp_ref[...], e64_ref[...], e128_ref[...])
    ys, hn = [], []
    for g in range(2):
        gl = slice(g * GW, (g + 1) * GW)
        xdt = ck.xdt[g]
        hg = h_all[gl, :]
        y_diag = ck.diag_blocks(_dot(ck.m_st[g].astype(BF16), xdt.astype(BF16)))
        y_off = ck.e_x[:, gl] * _dot_nt(ck.cb16[g], hg.astype(BF16))
        ys.append(y_diag + y_off + ck.xs[g] * ck.dsk_x[:, gl])
        hn.append(hg * ck.elast_rows(g) + _dot_tn((xdt * ck.dte_x[:, gl]).astype(BF16), ck.bb[g]))
    h_scr[...] = jnp.concatenate(hn, axis=0)
    y = jnp.concatenate(ys, axis=1)
    y_ref[...] = y
    yn_ref[...] = _gated_norm_fwd(y, z_ref[...], nw_ref[...]).astype(BF16)


def _mixer_fwd(q, k, v, sinks8, xbc, conv_w8, conv_b, dtr, ssm_p, z, nw, gathers):
    s = q.shape[0]
    nb = s // LB
    prev = lambda n: (jnp.maximum(n - 1, 0), 0)
    cur = lambda n: (n, 0)
    items, ex_shapes, n_g = _exchange_items(gathers, [])
    ne = len(items)

    n_in = 16
    relay_step = (nb - 1) // 2
    e64, e128 = _head_expanders()

    def body(*refs):
        a_in, s_in, ex_in = refs[:6], refs[6:n_in], refs[n_in:n_in + ne]
        o_ref, yn_ref, y_ref, hs_ref = refs[n_in + ne:n_in + 4 + ne]
        ex_out = refs[n_in + 4 + ne:n_in + 4 + 2 * ne]
        h_scr = refs[n_in + 4 + 2 * ne]
        sems = refs[n_in + 5 + 2 * ne:]
        n = pl.program_id(0)

        @pl.when(n == 0)
        def _():
            _Exchange(n_g, ex_in, ex_out, sems).two_level_start()

        _attn_fwd_block(n, *a_in, o_ref)
        _ssd_fwd_block(n, *s_in, yn_ref, y_ref, hs_ref, h_scr)

        @pl.when(n == relay_step)
        def _():
            _Exchange(n_g, ex_in, ex_out, sems).two_level_relay()

        @pl.when(n == nb - 1)
        def _():
            _Exchange(n_g, ex_in, ex_out, sems).two_level_finish()

    any_spec = pl.BlockSpec(memory_space=pl.ANY)
    tail = pl.BlockSpec((SUB, D), lambda n: (jnp.maximum(n * (LB // SUB) - 1, 0), 0))
    outs = _pcall(
        body, name="mixer_fwd", grid=(nb,),
        in_specs=[pl.BlockSpec((LB, AW), cur), pl.BlockSpec((LB, KVW), prev), pl.BlockSpec((LB, KVW), cur),
                  pl.BlockSpec((LB, KVW), prev), pl.BlockSpec((LB, KVW), cur), _const((8, LB)),
                  tail, pl.BlockSpec((LB, D), cur), _const((8, D)), _const((1, D)),
                  pl.BlockSpec((LB, LB), cur), _const((8, LB)), pl.BlockSpec((LB, SW), cur), _const((1, SW)),
                  _const(e64.shape), _const(e128.shape)]
        + [any_spec] * ne,
        out_specs=[pl.BlockSpec((LB, AW), cur), pl.BlockSpec((LB, SW), cur), pl.BlockSpec((LB, SW), cur),
                   pl.BlockSpec((1, NH * HD, NST), lambda n: (n, 0, 0))] + [any_spec] * ne,
        out_shape=[_sds((s, AW), BF16), _sds((s, SW), BF16), _sds((s, SW), F32), _sds((nb, NH * HD, NST), F32)]
        + ex_shapes,
        scratch_shapes=[pltpu.VMEM((NH * HD, NST), F32)] + _exchange_sems(ne),
        compiler_params=_params(),
    )(q, k, k, v, v, sinks8, xbc, xbc, conv_w8, conv_b, dtr, ssm_p, z, nw, e64, e128, *items)
    return outs[0], outs[1], outs[2], outs[3], outs[4:]


def _head_expanders():
    j = lax.broadcasted_iota(jnp.int32, (LB, NH * HD), 0)
    e64 = (lax.broadcasted_iota(jnp.int32, (LB, NH * HD), 1) // HD == j).astype(BF16)
    j = lax.broadcasted_iota(jnp.int32, (LB, NH * LB), 0)
    e128 = (lax.broadcasted_iota(jnp.int32, (LB, NH * LB), 1) // LB == j).astype(BF16)
    return e64, e128


def _outproj_ffn_fwd_loss(attn, yn, x, tgt, mod8, n2w, fnw, w_out, w_gu_t, w_down):
    s = x.shape[0]
    tf = min(256, s)

    def body(a_ref, y_ref, x_ref, t_ref, mod_ref, nw_ref, fw_ref, wo_ref, wgu_ref, wd_ref,
             x2_ref, h2_ref, mo_ref, mix_ref, gu_ref, act_ref, dx3_ref, sm_ref):
        i = pl.program_id(0)

        @pl.when(i == 0)
        def _():
            sm_ref[...] = jnp.zeros_like(sm_ref)

        mix = jnp.concatenate([a_ref[...], y_ref[...]], axis=1)
        mix_ref[...] = mix
        mo = _dot(mix, wo_ref[...])
        mo_ref[...] = mo.astype(BF16)
        x2 = x_ref[...] + mod_ref[2:3, :] * mo
        x2_ref[...] = x2
        h2 = _norm_mod_fwd(x2, nw_ref[...], mod_ref[3:4, :], mod_ref[4:5, :]).astype(BF16)
        h2_ref[...] = h2
        gu = _dot_nt(h2, wgu_ref[...])
        gu_ref[...] = gu.astype(BF16)
        g = gu[:, :DFF]
        act = (g * _sigmoid(g) * gu[:, DFF:]).astype(BF16)
        act_ref[...] = act
        ff = _dot(act, wd_ref[...])
        x3 = x2 + mod_ref[5:6, :] * ff
        r = lax.rsqrt(jnp.mean(x3 * x3, axis=-1, keepdims=True) + EPS)
        xh = x3 * r
        fw = fw_ref[...]
        err = xh * fw - t_ref[...]
        dy = err * (1.0 / D)
        dxh = dy * fw
        dx3 = r * (dxh - xh * jnp.mean(dxh * xh, axis=-1, keepdims=True))
        dx3_ref[...] = dx3
        sm_ref[0:1, :] += jnp.sum(dx3 * ff, axis=0, keepdims=True)
        sm_ref[1:2, :] += jnp.sum(dy * xh, axis=0, keepdims=True)
        sm_ref[2:3, :] += jnp.sum(err * err, axis=0, keepdims=True)

    return _pcall(
        body, name="outproj_ffn_fwd_loss", grid=(s // tf,),
        in_specs=[_rows(tf, AW), _rows(tf, SW), _rows(tf, D), _rows(tf, D), _const((8, D)), _const((1, D)),
                  _const((1, D)), _const((D, D)), _const((2 * DFF, D)), _const((DFF, D))],
        out_specs=[_rows(tf, D), _rows(tf, D), _rows(tf, D), _rows(tf, D), _rows(tf, 2 * DFF), _rows(tf, DFF),
                   _rows(tf, D), pl.BlockSpec((8, D), lambda i: (0, 0))],
        out_shape=[_sds((s, D), F32), _sds((s, D), BF16), _sds((s, D), BF16), _sds((s, D), BF16),
                   _sds((s, 2 * DFF), BF16), _sds((s, DFF), BF16), _sds((s, D), F32), _sds((8, D), F32)],
        compiler_params=_params(),
    )(attn, yn, x, tgt, mod8, n2w, fnw, w_out, w_gu_t, w_down)


def _ffn_bwd(dx3, gu, x2, mixout, mod8, n2w, w_gu, w_down, w_out):
    s = x2.shape[0]
    tb = min(256, s)

    def body(dx3_ref, gu_ref, x2_ref, mo_ref, mod_ref, nw_ref, wgu_ref, wd_ref, wo_ref,
             dx2_ref, dff_ref, dgu_ref, dmix_ref, dattn_ref, dyn_ref, sm_ref):
        i = pl.program_id(0)

        @pl.when(i == 0)
        def _():
            sm_ref[...] = jnp.zeros_like(sm_ref)

        dx3 = dx3_ref[...]
        dff = (dx3 * mod_ref[5:6, :]).astype(BF16)
        dff_ref[...] = dff
        dact = _dot_nt(dff, wd_ref[...])
        g = gu_ref[:, :DFF].astype(F32)
        u = gu_ref[:, DFF:].astype(F32)
        sg = _sigmoid(g)
        dgu = jnp.concatenate([dact * u * sg * (1.0 + g * (1.0 - sg)), dact * g * sg], axis=1).astype(BF16)
        dgu_ref[...] = dgu
        dh2 = _dot(dgu, wgu_ref[...])
        dxn, d_shift, d_scale, d_w = _norm_mod_bwd(x2_ref[...], dh2, nw_ref[...], mod_ref[4:5, :])
        dx2 = dx3 + dxn
        dx2_ref[...] = dx2
        sm_ref[0:1, :] += d_shift
        sm_ref[1:2, :] += d_scale
        sm_ref[2:3, :] += d_w
        sm_ref[3:4, :] += jnp.sum(dx2 * mo_ref[...].astype(F32), axis=0, keepdims=True)
        dmix = (dx2 * mod_ref[2:3, :]).astype(BF16)
        dmix_ref[...] = dmix
        dmi = _dot_nt(dmix, wo_ref[...])
        dattn_ref[...] = dmi[:, :AW].astype(BF16)
        dyn_ref[...] = dmi[:, AW:]

    return _pcall(
        body, name="ffn_bwd", grid=(s // tb,),
        in_specs=[_rows(tb, D), _rows(tb, 2 * DFF), _rows(tb, D), _rows(tb, D), _const((8, D)), _const((1, D)),
                  _const((2 * DFF, D)), _const((DFF, D)), _const((D, D))],
        out_specs=[_rows(tb, D), _rows(tb, D), _rows(tb, 2 * DFF), _rows(tb, D), _rows(tb, AW), _rows(tb, SW),
                   pl.BlockSpec((8, D), lambda i: (0, 0))],
        out_shape=[_sds((s, D), F32), _sds((s, D), BF16), _sds((s, 2 * DFF), BF16), _sds((s, D), BF16),
                   _sds((s, AW), BF16), _sds((s, SW), F32), _sds((8, D), F32)],
        compiler_params=_params(),
    )(dx3, gu, x2, mixout, mod8, n2w, w_gu, w_down, w_out)


def _ssd_bwd_block(i, n, *refs):
    def run(dyn_ref, y_ref, z_ref, xt_ref, xc_ref, cw_ref, cb_ref, dtr_ref, sp_ref, nw_ref, hs_ref, e64_ref, e128_ref,
            dzxd_ref, sm_ref, dh_scr, dun_scr):
        @pl.when(i == 0)
        def _():
            dh_scr[...] = jnp.zeros_like(dh_scr)
            dun_scr[...] = jnp.zeros_like(dun_scr)
            sm_ref[...] = jnp.zeros_like(sm_ref)

        sh, u, sg_u, xc, pre, dt, a_neg, acs = _ssd_pre(xt_ref, xc_ref, cw_ref, cb_ref, dtr_ref, sp_ref, n)
        ck = _SsdChunk(xc, dt, acs, sp_ref[...], e64_ref[...], e128_ref[...])
        h_all = hs_ref[0]
        dh_all = dh_scr[...]
        riota = lax.broadcasted_iota(jnp.int32, (LB, LB), 0)
        lane1 = lax.broadcasted_iota(jnp.int32, (1, LB), 1)

        z = z_ref[...]
        y = y_ref[...]
        sgz = _sigmoid(z)
        sz = z * sgz
        yz = y * sz
        nwv = nw_ref[...]
        dyn_v = dyn_ref[...]
        dyhat = dyn_v * nwv
        yhat_parts, dyz_parts = [], []
        for g in range(2):
            gs = slice(g * 256, (g + 1) * 256)
            t = yz[:, gs]
            rg = lax.rsqrt(jnp.mean(t * t, axis=-1, keepdims=True) + EPS)
            yh = t * rg
            dyh = dyhat[:, gs]
            yhat_parts.append(yh)
            dyz_parts.append(rg * (dyh - yh * jnp.mean(dyh * yh, axis=-1, keepdims=True)))
        yhat = jnp.concatenate(yhat_parts, axis=1)
        dyz = jnp.concatenate(dyz_parts, axis=1)
        sm_ref[5:6, 0:SW] += jnp.sum(dyn_v * yhat, axis=0, keepdims=True)
        dy = dyz * sz
        dzxd_ref[:, 0:SW] = (dyz * y * sgz * (1.0 + z * (1.0 - sgz))).astype(BF16)

        cat = lambda parts: jnp.concatenate(parts, axis=1)
        dxs, dbs, dcs, dhp, g_cat, de_x, ddte_x, ddt_x, ddsk_x = ([] for _ in range(9))
        dacs_t = jnp.zeros((LB, LB), F32)
        hsum = jnp.zeros((1, LB), F32)
        for g in range(2):
            gl = slice(g * GW, (g + 1) * GW)
            xs_g, xdt, bgb, cgb = ck.xs[g], ck.xdt[g], ck.bb[g], ck.cb16[g]
            m_st, dm_st = ck.m_st[g], ck.dm_st[g]
            dt_x, e_x, dte_x = ck.dt_x[:, gl], ck.e_x[:, gl], ck.dte_x[:, gl]
            xdtb = xdt.astype(BF16)
            hg, dhn = h_all[gl, :], dh_all[gl, :]
            hb, dhnb = hg.astype(BF16), dhn.astype(BF16)
            dy_g = dy[:, gl]
            ddsk_x.append(jnp.sum(dy_g * xs_g, axis=0, keepdims=True))
            dy_bd = ck.block_diag(dy_g).astype(BF16)
            dm4 = _dot_nt(dy_bd, xdtb)
            dxdt = _dot_tn(m_st.astype(BF16), dy_bd)
            gmat = dm4 * m_st
            dcbm = dm4 * dm_st
            dcb = dcbm[0:LB] + dcbm[LB:2 * LB] + dcbm[2 * LB:3 * LB] + dcbm[3 * LB:4 * LB]
            g_cat.append(cat([gmat[hh * LB:(hh + 1) * LB, :] for hh in range(HPG)]))
            for hh in range(HPG):
                j = HPG * g + hh
                col_sum = jnp.sum(gmat[hh * LB:(hh + 1) * LB, :], axis=0, keepdims=True)
                dacs_t = dacs_t - jnp.where(riota == j, col_sum, 0.0)
                hsl = slice(hh * HD, (hh + 1) * HD)
                hsum = hsum + jnp.where(lane1 == j, jnp.sum(dhn[hsl, :] * hg[hsl, :]), 0.0)
            dchb = (dy_g * e_x).astype(BF16)
            dcg = _dot(dchb, hb)
            dh_prev = _dot_tn(dchb, cgb)
            de_x.append(dy_g * _dot_nt(cgb, hb))
            dxs_s = _dot_nt(bgb, dhnb)
            dbg = _dot((xdt * dte_x).astype(BF16), dhnb)
            dxdt = dxdt + dxs_s * dte_x
            ddte_x.append(dxs_s * xdt)
            dhp.append(dhn * ck.elast_rows(g) + dh_prev)
            dxs.append(dy_g * ck.dsk_x[:, gl] + dxdt * dt_x)
            ddt_x.append(dxdt * xs_g)
            dcbb = dcb.astype(BF16)
            dbs.append(dbg + _dot_tn(dcbb, cgb))
            dcs.append(dcg + _dot(dcbb, bgb))
        dh_scr[...] = jnp.concatenate(dhp, axis=0)
        red = _dot_nt_sel(jnp.concatenate([cat(de_x), cat(ddte_x), cat(ddt_x)], axis=0), ck.e64, 2)
        de_c, ddte_c, ddt_c = red[0:LB], red[LB:2 * LB], red[2 * LB:3 * LB]
        ddsk = _dot_nt_sel(jnp.broadcast_to(cat(ddsk_x), (SUB, NH * HD)), ck.e64, 2)[0:1, :]
        t1 = ddte_c * ck.dte_all
        dalast = jnp.sum(t1, axis=0, keepdims=True) + hsum * ck.elast
        dacs = (_dot_nt_sel(cat(g_cat), ck.e128, 2) + de_c * ck.e_all - t1 + jnp.transpose(dacs_t)
                + jnp.where(riota == LB - 1, dalast, 0.0))
        da = _cumsum_rows(dacs, True)
        ddt = ddt_c + da * a_neg
        da_log = jnp.sum(da * dt, axis=0, keepdims=True) * a_neg
        ddtr = ddt * _sigmoid(pre)
        dzxd_ref[:, SW + D:ZXD] = ddtr.astype(BF16)
        sm_ref[6:7, 0:LB] += jnp.sum(ddtr, axis=0, keepdims=True)
        sm_ref[6:7, LB:2 * LB] += da_log
        sm_ref[6:7, 2 * LB:3 * LB] += ddsk

        du = cat(dxs + dbs + dcs) * (sg_u * (1.0 + u * (1.0 - sg_u)))
        sm_ref[0:1, :] += jnp.sum(du, axis=0, keepdims=True)
        for k in range(CONVK):
            sm_ref[1 + k:2 + k, :] += jnp.sum(du * sh[CONVK - 1 - k], axis=0, keepdims=True)
        adv = _conv_advances(du, dun_scr[...])
        dxbc = cw_ref[CONVK - 1:CONVK, :] * du
        for j in range(1, CONVK):
            dxbc = dxbc + cw_ref[CONVK - 1 - j:CONVK - j, :] * adv[j - 1]
        dun_scr[...] = du[0:SUB, :]
        dzxd_ref[:, SW:SW + D] = dxbc.astype(BF16)

    run(*refs)


def _attn_bwd_block(i, n, q_ref, kp_ref, kc_ref, vp_ref, vc_ref, o_ref, do_ref, cos_ref, sin_ref, sink_ref,
                    dq_ref, dkv_ref, ds_ref, ck_scr, cv_scr):
    @pl.when(i == 0)
    def _():
        ds_ref[...] = jnp.zeros_like(ds_ref)
        ck_scr[...] = jnp.zeros_like(ck_scr)
        cv_scr[...] = jnp.zeros_like(cv_scr)

    qv, ov, dov, sinks = q_ref[...], o_ref[...], do_ref[...], sink_ref[...]
    kcat = jnp.concatenate([kp_ref[...], kc_ref[...]], axis=0)
    vcat = jnp.concatenate([vp_ref[...], vc_ref[...]], axis=0)
    upper = _upper_mask()
    srow = lax.broadcasted_iota(jnp.int32, (8, LB), 0)
    slane = lax.broadcasted_iota(jnp.int32, (8, LB), 1)
    dsink = jnp.zeros((8, LB), F32)
    dq_g, dk_g, dv_g = [], [], []
    for g in range(NQ // QPG):
        sl = slice(g * HD, (g + 1) * HD)
        qg = _stack_heads(qv, g)
        dog = _stack_heads(dov, g)
        probs, psink = _attn_softmax(_attn_scores(n, qg, kcat[:, sl], upper), _sink_col(sinks, g))
        delta = jnp.sum(dog.astype(F32) * _stack_heads(ov, g).astype(F32), axis=1, keepdims=True)
        dsc = probs * (_band(upper, _dot_nt(dog, vcat[0:LB, sl]), _dot_nt(dog, vcat[LB:2 * LB, sl])) - delta)
        sink_terms = psink * delta
        for hh in range(QPG):
            dsink = dsink - jnp.where((srow == QPG * g + hh) & (slane == 0),
                                      jnp.sum(sink_terms[hh * LB:(hh + 1) * LB, :]), 0.0)
        ds_p = jnp.where(upper, dsc, 0.0).astype(BF16)
        ds_c = jnp.where(upper, 0.0, dsc).astype(BF16)
        dq_g.append((_dot(ds_p, kcat[0:LB, sl]) + _dot(ds_c, kcat[LB:2 * LB, sl])) * ATT_SCALE)
        dk_g.append(jnp.concatenate([_dot_tn(ds_p, qg), _dot_tn(ds_c, qg)], axis=0) * ATT_SCALE)
        dv_g.append(jnp.concatenate([_dot_tn(jnp.where(upper, probs, 0.0).astype(BF16), dog),
                                     _dot_tn(jnp.where(upper, 0.0, probs).astype(BF16), dog)], axis=0))
    ds_ref[...] += dsink
    cs = cos_ref[...]
    sn = sin_ref[...]
    dk2 = jnp.concatenate(dk_g, axis=1)
    dv2 = jnp.concatenate(dv_g, axis=1)
    for a, tile in enumerate(_unstack_heads(dq_g)):
        dq_ref[:, a * LB:(a + 1) * LB] = _rope(tile, cs, sn, True).astype(BF16)
    dkv_ref[:, 0:KVW] = _rope(ck_scr[...] + dk2[LB:2 * LB, :], cs, sn, True).astype(BF16)
    dkv_ref[:, KVW:2 * KVW] = (cv_scr[...] + dv2[LB:2 * LB, :]).astype(BF16)
    ck_scr[...] = dk2[0:LB, :]
    cv_scr[...] = dv2[0:LB, :]


def _mixer_bwd(q, k, v, o, do, cos, sin, sinks8, dyn, y, z, xbc, conv_w8, conv_b, dtr, ssm_p, nw, hs, scatters):
    s = q.shape[0]
    nb = s // LB
    cur = lambda i: (nb - 1 - i, 0)
    prev = lambda i: (jnp.maximum(nb - 2 - i, 0), 0)
    n_in = 23
    items, ex_shapes, n_g = _exchange_items([], scatters)
    ne = len(items)
    e64, e128 = _head_expanders()

    def body(*refs):
        i = pl.program_id(0)
        n = nb - 1 - i
        a_in, s_in, ex_in = refs[:10], refs[10:n_in], refs[n_in:n_in + ne]
        dp_ref, ds_ref, sm_ref = refs[n_in + ne:n_in + ne + 3]
        dq_ref, dkv_ref, dzxd_ref = dp_ref.at[:, O_Q:O_K], dp_ref.at[:, O_K:O_Z], dp_ref.at[:, O_Z:INP]
        ex_out = refs[n_in + ne + 3:n_in + 2 * ne + 3]
        ck_scr, cv_scr, dh_scr, dun_scr = refs[n_in + 2 * ne + 3:n_in + 2 * ne + 7]
        sems = refs[n_in + 2 * ne + 7:]

        @pl.when(i == 0)
        def _():
            _Exchange(n_g, ex_in, ex_out, sems).start()

        _attn_bwd_block(i, n, *a_in, dq_ref, dkv_ref, ds_ref, ck_scr, cv_scr)
        _ssd_bwd_block(i, n, *s_in, dzxd_ref, sm_ref, dh_scr, dun_scr)

        @pl.when(i == nb - 1)
        def _():
            _Exchange(n_g, ex_in, ex_out, sems).finish()

    any_spec = pl.BlockSpec(memory_space=pl.ANY)
    tail = pl.BlockSpec((SUB, D), lambda i: (jnp.maximum((nb - 1 - i) * (LB // SUB) - 1, 0), 0))
    outs = _pcall(
        body, name="mixer_bwd", grid=(nb,),
        in_specs=[pl.BlockSpec((LB, AW), cur), pl.BlockSpec((LB, KVW), prev), pl.BlockSpec((LB, KVW), cur),
                  pl.BlockSpec((LB, KVW), prev), pl.BlockSpec((LB, KVW), cur), pl.BlockSpec((LB, AW), cur),
                  pl.BlockSpec((LB, AW), cur), pl.BlockSpec((LB, LB), cur), pl.BlockSpec((LB, LB), cur),
                  _const((8, LB)),
                  pl.BlockSpec((LB, SW), cur), pl.BlockSpec((LB, SW), cur), pl.BlockSpec((LB, SW), cur),
                  tail, pl.BlockSpec((LB, D), cur), _const((8, D)), _const((1, D)),
                  pl.BlockSpec((LB, LB), cur), _const((8, LB)), _const((1, SW)),
                  pl.BlockSpec((1, NH * HD, NST), lambda i: (nb - 1 - i, 0, 0)),
                  _const(e64.shape), _const(e128.shape)] + [any_spec] * ne,
        out_specs=[pl.BlockSpec((LB, INP), cur), pl.BlockSpec((8, LB), lambda i: (0, 0)),
                   pl.BlockSpec((8, D), lambda i: (0, 0))] + [any_spec] * ne,
        out_shape=[_sds((s, INP), BF16), _sds((8, LB), F32), _sds((8, D), F32)] + ex_shapes,
        scratch_shapes=[pltpu.VMEM((LB, KVW), F32), pltpu.VMEM((LB, KVW), F32),
                        pltpu.VMEM((NH * HD, NST), F32), pltpu.VMEM((SUB, D), F32)]
        + _exchange_sems(ne),
        compiler_params=_params(),
    )(q, k, k, v, v, o, do, cos, sin, sinks8, dyn, y, z, xbc, xbc, conv_w8, conv_b, dtr, ssm_p, nw, hs, e64, e128,
      *items)
    return outs[0], outs[1], outs[2], outs[3:]


def _inproj_bwd(dproj, x, dx2, mod8, n1w, w_in_t, scatters):
    s = x.shape[0]
    tt = min(512, s)
    nt = s // tt
    items, ex_shapes, n_g = _exchange_items([], scatters)
    ne = len(items)

    def body(*refs):
        dp_ref, x_ref, dx2_ref, mod_ref, nw_ref, w_ref = refs[:6]
        ex_in = refs[6:6 + ne]
        gx_ref, sm_ref = refs[6 + ne:8 + ne]
        ex_out = refs[8 + ne:8 + 2 * ne]
        sems = refs[8 + 2 * ne:]
        i = pl.program_id(0)

        @pl.when(i == 0)
        def _():
            sm_ref[...] = jnp.zeros_like(sm_ref)
            _Exchange(n_g, ex_in, ex_out, sems).start()

        dh1 = _dot(dp_ref[...], w_ref[...])
        dxn, d_shift, d_scale, d_w = _norm_mod_bwd(x_ref[...], dh1, nw_ref[...], mod_ref[1:2, :])
        gx_ref[...] = dx2_ref[...] + dxn
        sm_ref[0:1, :] += d_shift
        sm_ref[1:2, :] += d_scale
        sm_ref[2:3, :] += d_w

        @pl.when(i == nt - 1)
        def _():
            _Exchange(n_g, ex_in, ex_out, sems).finish()

    any_spec = pl.BlockSpec(memory_space=pl.ANY)
    outs = _pcall(
        body, name="inproj_bwd", grid=(nt,),
        in_specs=[_rows(tt, INP), _rows(tt, D), _rows(tt, D), _const((8, D)), _const((1, D)), _const((INP, D))]
        + [any_spec] * ne,
        out_specs=[_rows(tt, D), pl.BlockSpec((8, D), lambda i: (0, 0))] + [any_spec] * ne,
        out_shape=[_sds((s, D), F32), _sds((8, D), F32)] + ex_shapes,
        scratch_shapes=_exchange_sems(ne),
        compiler_params=_params(),
    )(dproj, x, dx2, mod8, n1w, w_in_t, *items)
    return outs[0], outs[1], outs[2:]


def _wgrad(a, b, name):
    s, m = a.shape
    n = b.shape[1]
    tk = min(2048, s)
    wide = (1408, 1024, 512)
    tm = next((t for t in wide if m % t == 0), m)
    tn = n if n <= 2048 else _largest_divisor(n, wide)
    nk = s // tk

    def body(a_ref, b_ref, o_ref, acc):
        kk = pl.program_id(2)

        @pl.when(kk == 0)
        def _():
            acc[...] = jnp.zeros_like(acc)

        acc[...] += _dot_tn(a_ref[...], b_ref[...])

        @pl.when(kk == nk - 1)
        def _():
            o_ref[...] = acc[...].astype(BF16)

    return _pcall(
        body, name=name, grid=(m // tm, n // tn, nk),
        in_specs=[pl.BlockSpec((tk, tm), lambda i, j, kk: (kk, i)), pl.BlockSpec((tk, tn), lambda i, j, kk: (kk, j))],
        out_specs=pl.BlockSpec((tm, tn), lambda i, j, kk: (i, j)),
        out_shape=_sds((m, n), BF16),
        scratch_shapes=[pltpu.VMEM((tm, tn), F32)],
        compiler_params=_params(3),
    )(a, b)


PACK_ROWS = 24


def _pack_small(sm_f, sm_b, sm_s, sm_i, dsink):
    def body(f_ref, b_ref, s_ref, i_ref, k_ref, o_ref):
        o_ref[...] = jnp.zeros_like(o_ref)
        o_ref[0:2, :] = i_ref[0:2, :]
        o_ref[2:3, :] = b_ref[3:4, :]
        o_ref[3:5, :] = b_ref[0:2, :]
        o_ref[5:6, :] = f_ref[0:1, :]
        o_ref[6:7, :] = i_ref[2:3, :]
        o_ref[7:8, :] = b_ref[2:3, :]
        o_ref[8:9, :] = f_ref[1:2, :]
        o_ref[9:14, :] = s_ref[0:5, :]
        o_ref[14:15, :] = s_ref[5:6, :]
        o_ref[15:16, 0:3 * LB] = s_ref[6:7, 0:3 * LB]
        lane = lax.broadcasted_iota(jnp.int32, (1, LB), 1)
        sk = jnp.zeros((1, LB), F32)
        for h in range(NQ):
            sk = sk + jnp.where(lane == h, k_ref[h:h + 1, 0:1], 0.0)
        o_ref[15:16, 3 * LB:4 * LB] = sk
        o_ref[16:17, :] = f_ref[2:3, :]

    return _pcall(body, name="pack_small", out_shape=_sds((PACK_ROWS, D), F32))(sm_f, sm_b, sm_s, sm_i, dsink)


def _exchange(gathers, scatters, name, two_level=False):
    items, shapes, n_g = _exchange_items(gathers, scatters)
    n = len(items)
    assert not (two_level and scatters)

    def body(*refs):
        ex = _Exchange(n_g, refs[:n], refs[n:2 * n], refs[2 * n:])
        if two_level:
            ex.gather_two_level()
        else:
            ex.start()
            ex.finish()

    any_spec = pl.BlockSpec(memory_space=pl.ANY)
    return _pcall(
        body, name=name, in_specs=[any_spec] * n, out_specs=[any_spec] * n, out_shape=shapes,
        scratch_shapes=_exchange_sems(n),
    )(*items)


def _exchange_items(gathers, scatters):
    items = list(gathers) + list(scatters)
    shapes = [_sds((N_DEV,) + a.shape, a.dtype) for a in gathers] + [_sds(a.shape, a.dtype) for a in scatters]
    return items, shapes, len(gathers)


def _exchange_sems(n):
    return [pltpu.SemaphoreType.DMA((n, N_DEV - 1)), pltpu.SemaphoreType.DMA((n, N_DEV - 1)),
            pltpu.SemaphoreType.DMA((n,))]


class _Exchange:
    def __init__(self, n_g, ins, outs, sems):
        self.n_g, self.ins, self.outs = n_g, ins, outs
        self.send_sems, self.recv_sems, self.loc_sems = sems
        xi, yi, ci = lax.axis_index("x"), lax.axis_index("y"), lax.axis_index("c")
        self.me = 4 * xi + 2 * yi + ci
        self.peers = []
        for r in range(1, N_DEV):
            px = 1 - xi if r & 4 else xi
            py = 1 - yi if r & 2 else yi
            pc = 1 - ci if r & 1 else ci
            self.peers.append(((px, py, pc), 4 * px + 2 * py + pc))

    def _copy(self, t, r, landing):
        dev, peer = self.peers[r]
        src = self.ins[t] if t < self.n_g else self.ins[t].at[peer]
        return pltpu.make_async_remote_copy(
            src_ref=src, dst_ref=self.outs[t].at[landing], send_sem=self.send_sems.at[t, r],
            recv_sem=self.recv_sems.at[t, r], device_id=dev, device_id_type=pl.DeviceIdType.MESH)

    def _local(self, t):
        src = self.ins[t] if t < self.n_g else self.ins[t].at[self.me]
        return pltpu.make_async_copy(src, self.outs[t].at[self.me], self.loc_sems.at[t])

    def start(self):
        for t in range(len(self.ins)):
            self._local(t).start()
            for r in range(N_DEV - 1):
                self._copy(t, r, self.me).start()

    def finish(self):
        n = len(self.ins)
        for t in range(n):
            for r in range(N_DEV - 1):
                self._copy(t, r, self.peers[r][1]).wait_recv()
        for t in range(n):
            for r in range(N_DEV - 1):
                self._copy(t, r, self.me).wait_send()
            self._local(t).wait()

    def gather_two_level(self):
        self.two_level_start()
        self.two_level_relay()
        self.two_level_finish()

    DIRECT = (0, 1, 3, 5)

    def two_level_start(self):
        for t in range(len(self.ins)):
            self._local(t).start()
            for r in self.DIRECT:
                self._copy(t, r, self.me).start()

    def _relay(self, t, r):
        peer = self.peers[r][1]
        return pltpu.make_async_remote_copy(
            src_ref=self.outs[t].at[peer], dst_ref=self.outs[t].at[peer], send_sem=self.send_sems.at[t, r + 1],
            recv_sem=self.recv_sems.at[t, r + 1], device_id=self.peers[0][0], device_id_type=pl.DeviceIdType.MESH)

    def two_level_relay(self):
        for t in range(len(self.ins)):
            for r in self.DIRECT[1:]:
                self._copy(t, r, self.peers[r][1]).wait_recv()
                self._relay(t, r).start()

    def two_level_finish(self):
        n = len(self.ins)
        for t in range(n):
            for r in (0, 2, 4, 6):
                self._copy(t, r, self.peers[r][1]).wait_recv()
        for t in range(n):
            for r in self.DIRECT:
                self._copy(t, r, self.me).wait_send()
            for r in self.DIRECT[1:]:
                self._relay(t, r).wait_send()
            self._local(t).wait()


def _ada_fwd(c_all, w_cols, b_cols):
    def body(c_ref, w_ref, b_ref, o_ref):
        cv = c_ref[...]
        sc = (cv * _sigmoid(cv)).astype(BF16)
        o_ref[...] = _dot(sc, w_ref[...].astype(BF16)) + b_ref[...]

    return _pcall(body, name="ada_fwd", out_shape=_sds((N_DEV, w_cols.shape[1]), F32),
                  compiler_params=_params(0))(c_all, w_cols, b_cols)


def _adamw(w, g, m, v):
    m2 = ADAM_B1 * m + (1.0 - ADAM_B1) * g
    v2 = ADAM_B2 * v + (1.0 - ADAM_B2) * (g * g)
    m_hat = m2 / (1.0 - ADAM_B1 ** ADAM_STEP)
    v_hat = v2 / (1.0 - ADAM_B2 ** ADAM_STEP)
    delta = -ADAM_LR * (m_hat / (jnp.sqrt(v_hat) + ADAM_EPS) + ADAM_WD * w)
    return delta, m2, v2


def _sum_adamw(parts, w, m, v, name):
    rws, cols = w.shape
    tr = next((t for t in (256, 176, 128) if rws % t == 0), rws)

    def body(p_ref, w_ref, m_ref, v_ref, g_ref, d_ref, mo_ref, vo_ref):
        g = p_ref[0].astype(F32)
        for dev in range(1, N_DEV):
            g = g + p_ref[dev].astype(F32)
        g_ref[...] = g
        d_ref[...], mo_ref[...], vo_ref[...] = _adamw(w_ref[...], g, m_ref[...], v_ref[...])

    blk = pl.BlockSpec((tr, cols), lambda i: (i, 0))
    return _pcall(
        body, name=name, grid=(rws // tr,),
        in_specs=[pl.BlockSpec((N_DEV, tr, cols), lambda i: (0, i, 0)), blk, blk, blk],
        out_specs=[blk] * 4, out_shape=[_sds((rws, cols), F32)] * 4, compiler_params=_params(),
    )(parts, w, m, v)


def _wada_adamw(c_all, dmod_cols, w, m, v):
    rws, cols = w.shape
    tr = 256

    def body(c_ref, dm_ref, w_ref, m_ref, v_ref, g_ref, d_ref, mo_ref, vo_ref):
        cv = c_ref[...]
        sc = (cv * _sigmoid(cv)).astype(BF16)
        g = _dot_tn(sc, dm_ref[...].astype(BF16))
        g_ref[...] = g
        d_ref[...], mo_ref[...], vo_ref[...] = _adamw(w_ref[...], g, m_ref[...], v_ref[...])

    blk = pl.BlockSpec((tr, cols), lambda i: (i, 0))
    return _pcall(
        body, name="wada_adamw", grid=(rws // tr,),
        in_specs=[pl.BlockSpec((N_DEV, tr), lambda i: (0, i)), pl.BlockSpec((N_DEV, cols), lambda i: (0, 0)),
                  blk, blk, blk],
        out_specs=[blk] * 4, out_shape=[_sds((rws, cols), F32)] * 4, compiler_params=_params(),
    )(c_all, dmod_cols, w, m, v)


def _small_reduce(packs):
    def body(p_ref, o_ref):
        tot = p_ref[0]
        for dev in range(1, N_DEV):
            tot = tot + p_ref[dev]
        o_ref[...] = tot
        o_ref[16:17, :] = jnp.zeros((1, D), F32) + (0.5 / D) * jnp.sum(tot[16:17, :])

    return _pcall(body, name="small_reduce", out_shape=_sds((PACK_ROWS, D), F32))(packs)


def _adamw_many(ws, gs, ms, vs):
    k = len(ws)

    def body(*refs):
        for i in range(k):
            w_ref, g_ref, m_ref, v_ref = refs[i], refs[k + i], refs[2 * k + i], refs[3 * k + i]
            d_ref, mo_ref, vo_ref = refs[4 * k + i], refs[5 * k + i], refs[6 * k + i]
            d_ref[...], mo_ref[...], vo_ref[...] = _adamw(w_ref[...], g_ref[...], m_ref[...], v_ref[...])

    shp = [_sds(w.shape, F32) for w in ws]
    outs = _pcall(body, name="adamw_small", out_shape=shp * 3)(*ws, *gs, *ms, *vs)
    return outs[:k], outs[k:2 * k], outs[2 * k:]


def kernel(x, c, positions, w_ada, b_ada, norm1_w, w_in, conv_w, conv_b, dt_bias, a_log, d_skip, attn_sinks, ssm_norm_w, w_out, norm2_w, w_gate_up, w_down, final_norm_w, loss_target, m_w_ada, m_b_ada, m_norm1_w, m_w_in, m_conv_w, m_conv_b, m_dt_bias, m_a_log, m_d_skip, m_attn_sinks, m_ssm_norm_w, m_w_out, m_norm2_w, m_w_gate_up, m_w_down, m_final_norm_w, v_w_ada, v_b_ada, v_norm1_w, v_w_in, v_conv_w, v_conv_b, v_dt_bias, v_a_log, v_d_skip, v_attn_sinks, v_ssm_norm_w, v_w_out, v_norm2_w, v_w_gate_up, v_w_down, v_final_norm_w):
    s = x.shape[1]
    me = 4 * lax.axis_index("x") + 2 * lax.axis_index("y") + lax.axis_index("c")
    ada_cols = N_MOD * D // N_DEV

    c8 = jnp.pad(c, ((0, 7), (0, 0)))
    cw8 = jnp.pad(conv_w[0], ((0, 8 - CONVK), (0, 0)))
    w_in_t, m_w_in_t, v_w_in_t = jnp.transpose(w_in[0]), jnp.transpose(m_w_in[0]), jnp.transpose(v_w_in[0])
    w_gu_t, m_w_gu_t, v_w_gu_t = (jnp.transpose(w_gate_up[0]), jnp.transpose(m_w_gate_up[0]),
                                  jnp.transpose(v_w_gate_up[0]))
    g_c, g_in, g_cw = _exchange([c8, w_in_t.astype(BF16), cw8], [], "gather_in", two_level=True)
    c_all = g_c[:, 0, :]
    w_in_f = jnp.pad(g_in.reshape(IN_PROJ, D), ((0, INP - IN_PROJ), (0, 0)))
    conv_w8 = jnp.transpose(g_cw, (1, 0, 2)).reshape(8, D)

    b_cols = lax.dynamic_slice(b_ada, (0, me * ada_cols), (1, ada_cols))
    (g_mod,) = _exchange([_ada_fwd(c_all, w_ada[0], b_cols)], [], "gather_mod")
    mod = lax.dynamic_index_in_dim(g_mod, me, axis=1, keepdims=False).reshape(N_MOD, D)
    mod8 = jnp.pad(mod, ((0, 8 - N_MOD), (0, 0)))

    half = HD // 2
    inv_freq = ROPE_THETA ** (-jnp.arange(half, dtype=F32) / half)
    invf = jnp.tile(inv_freq, LB // half).reshape(1, LB)
    lanes = lambda a: jnp.pad(a, ((0, 0), (0, LB - a.shape[1])))
    ssm_p = jnp.pad(jnp.concatenate([lanes(dt_bias), lanes(a_log), lanes(d_skip)], axis=0), ((0, 5), (0, 0)))
    sinks8 = jnp.broadcast_to(attn_sinks.reshape(NQ, 1), (NQ, LB))

    xs, tgt, fnw = x[0], loss_target[0], final_norm_w.reshape(1, D)

    q, k, v, z, xbc, dtr, h1, cos, sin = _inproj_fwd(xs, positions[0].reshape(s, 1), invf, mod8, norm1_w, w_in_f)
    attn, yn, y, hs, (g_out, g_gu, g_down) = _mixer_fwd(
        q, k, v, sinks8, xbc, conv_w8, conv_b, dtr, ssm_p, z, ssm_norm_w,
        [w_out[0].astype(BF16), w_gu_t.astype(BF16), w_down[0].astype(BF16)])
    w_out_f = g_out.reshape(D, D)
    w_gu_f = g_gu.reshape(2 * DFF, D)
    w_down_f = g_down.reshape(DFF, D)
    x2, h2, mo, mix, gu, act, dx3, sm_f = _outproj_ffn_fwd_loss(attn, yn, xs, tgt, mod8, norm2_w, fnw, w_out_f, w_gu_f,
                                                                 w_down_f)

    dx2, dff, dgu, dmix, dattn, dyn, sm_b = _ffn_bwd(dx3, gu, x2, mo, mod8, norm2_w, w_gu_f, w_down_f, w_out_f)
    p_gu = _wgrad(dgu, h2, "wgrad_gate_up").reshape(N_DEV, 2 * DFF // N_DEV, D)
    p_down = _wgrad(act, dff, "wgrad_down").reshape(N_DEV, DFF // N_DEV, D)
    p_out = _wgrad(mix, dmix, "wgrad_out").reshape(N_DEV, D // N_DEV, D)
    dproj, dsink, sm_s, (r_gu, r_down, r_out) = _mixer_bwd(
        q, k, v, attn, dattn, cos, sin, sinks8, dyn, y, z, xbc, conv_w8, conv_b, dtr, ssm_p, ssm_norm_w, hs,
        [p_gu, p_down, p_out])
    p_in = _wgrad(dproj, h1, "wgrad_in")[:IN_PROJ].reshape(N_DEV, IN_PROJ // N_DEV, D)
    gx, sm_i, (r_in,) = _inproj_bwd(dproj, xs, dx2, mod8, norm1_w, w_in_f, [p_in])
    (g_pack,) = _exchange([_pack_small(sm_f, sm_b, sm_s, sm_i, dsink)], [], "gather_small")

    tot = _small_reduce(g_pack)
    loss = tot[16, 0]
    dmod_all = g_pack[:, 0:N_MOD, :].reshape(N_DEV, N_MOD * D)
    dmod_cols = lax.dynamic_slice(dmod_all, (0, me * ada_cols), (N_DEV, ada_cols))

    big = {
        "w_ada": _wada_adamw(c_all, dmod_cols, w_ada[0], m_w_ada[0], v_w_ada[0]),
        "w_in": [jnp.transpose(t) for t in _sum_adamw(r_in, w_in_t, m_w_in_t, v_w_in_t, "adamw_in")],
        "w_out": _sum_adamw(r_out, w_out[0], m_w_out[0], v_w_out[0], "adamw_out"),
        "w_gate_up": [jnp.transpose(t) for t in _sum_adamw(r_gu, w_gu_t, m_w_gu_t, v_w_gu_t, "adamw_gate_up")],
        "w_down": _sum_adamw(r_down, w_down[0], m_w_down[0], v_w_down[0], "adamw_down"),
    }
    small_names = ["b_ada", "norm1_w", "conv_w", "conv_b", "dt_bias", "a_log", "d_skip", "attn_sinks", "ssm_norm_w",
                   "norm2_w", "final_norm_w"]
    row15 = tot[15:16, :]
    small_g = {
        "b_ada": tot[0:N_MOD, :].reshape(1, N_MOD * D),
        "norm1_w": tot[6:7, :],
        "conv_w": lax.dynamic_slice(tot[10:14, :], (0, me * (D // N_DEV)), (CONVK, D // N_DEV)),
        "conv_b": tot[9:10, :],
        "dt_bias": row15[:, 0:NH],
        "a_log": row15[:, LB:LB + NH],
        "d_skip": row15[:, 2 * LB:2 * LB + NH],
        "attn_sinks": row15[:, 3 * LB:3 * LB + NQ],
        "ssm_norm_w": tot[14:15, 0:SW],
        "norm2_w": tot[7:8, :],
        "final_norm_w": tot[8:9, :],
    }
    small_w = {"b_ada": b_ada, "norm1_w": norm1_w, "conv_w": conv_w[0], "conv_b": conv_b, "dt_bias": dt_bias,
               "a_log": a_log, "d_skip": d_skip, "attn_sinks": attn_sinks, "ssm_norm_w": ssm_norm_w,
               "norm2_w": norm2_w, "final_norm_w": final_norm_w.reshape(1, D)}
    small_m = {"b_ada": m_b_ada, "norm1_w": m_norm1_w, "conv_w": m_conv_w[0], "conv_b": m_conv_b,
               "dt_bias": m_dt_bias, "a_log": m_a_log, "d_skip": m_d_skip, "attn_sinks": m_attn_sinks,
               "ssm_norm_w": m_ssm_norm_w, "norm2_w": m_norm2_w, "final_norm_w": m_final_norm_w.reshape(1, D)}
    small_v = {"b_ada": v_b_ada, "norm1_w": v_norm1_w, "conv_w": v_conv_w[0], "conv_b": v_conv_b,
               "dt_bias": v_dt_bias, "a_log": v_a_log, "d_skip": v_d_skip, "attn_sinks": v_attn_sinks,
               "ssm_norm_w": v_ssm_norm_w, "norm2_w": v_norm2_w, "final_norm_w": v_final_norm_w.reshape(1, D)}
    s_d, s_m, s_v = _adamw_many([small_w[k] for k in small_names], [small_g[k] for k in small_names],
                                [small_m[k] for k in small_names], [small_v[k] for k in small_names])

    order = ["w_ada", "b_ada", "norm1_w", "w_in", "conv_w", "conv_b", "dt_bias", "a_log", "d_skip", "attn_sinks",
             "ssm_norm_w", "w_out", "norm2_w", "w_gate_up", "w_down", "final_norm_w"]
    lead = {"w_ada", "w_in", "conv_w", "w_out", "w_gate_up", "w_down"}
    grads, deltas, new_m, new_v = [], [], [], []
    for name in order:
        if name in big:
            g, d, m2, v2 = big[name]
        else:
            i = small_names.index(name)
            g, d, m2, v2 = small_g[name], s_d[i], s_m[i], s_v[i]
        if name in lead:
            g, d, m2, v2 = g[None], d[None], m2[None], v2[None]
        if name == "final_norm_w":
            g, d, m2, v2 = g.reshape(D), d.reshape(D), m2.reshape(D), v2.reshape(D)
        grads.append(g)
        deltas.append(d)
        new_m.append(m2)
        new_v.append(v2)
    return (loss, gx[None], *grads, *deltas, *new_m, *new_v)
```

```python
import functools
import math

import jax
import jax.numpy as jnp
from jax import lax
from jax.experimental import pallas as pl
from jax.experimental.pallas import tpu as pltpu

F32 = jnp.float32
BF16 = jnp.bfloat16

N_DEV = 8
D = 1024
HD = 64
NQ = 8
AW = 512
KVW = 128
SW = 512
NST = 128
NH = 8
LB = 128
CONVK = 4
DFF = 2816
N_MOD = 6
IN_PROJ = 2312
INP = 2432
O_Q, O_K, O_V, O_Z, O_XBC, O_DT = 0, 512, 640, 768, 1280, 2304
ZXD = INP - O_Z
EPS = 1e-6
NEG = -1e30
ROPE_THETA = 10000.0
VMEM_LIMIT = 56 * 1024 * 1024

ADAM_LR = 0.001
ADAM_B1 = 0.9
ADAM_B2 = 0.999
ADAM_EPS = 1e-08
ADAM_WD = 0.01
ADAM_STEP = 10

NT_DIMS = (((1,), (1,)), ((), ()))
TN_DIMS = (((0,), (0,)), ((), ()))


def _pcall(body, **kw):
    return pl.pallas_call(body, **kw)


def _sds(shape, dtype):
    return jax.ShapeDtypeStruct(shape, dtype)


def _params(n_grid=1):
    return pltpu.CompilerParams(dimension_semantics=("arbitrary",) * n_grid, vmem_limit_bytes=VMEM_LIMIT)


def _const(shape):
    return pl.BlockSpec(shape, lambda *_: (0,) * len(shape), pipeline_mode=pl.Buffered(1))


def _largest_divisor(n, candidates):
    for cand in candidates:
        if n % cand == 0:
            return cand
    raise ValueError(f"no tile in {candidates} divides {n}")


def _rows(t, w):
    return pl.BlockSpec((t, w), lambda i: (i, 0))


def _dot(a, b):
    return jnp.dot(a, b, preferred_element_type=F32)


def _dot_nt(a, b):
    return lax.dot_general(a, b, NT_DIMS, preferred_element_type=F32)


def _dot_tn(a, b):
    return lax.dot_general(a, b, TN_DIMS, preferred_element_type=F32)


def _sigmoid(v):
    return 1.0 / (1.0 + jnp.exp(-v))


def _softplus(v):
    return jnp.maximum(v, 0.0) + jnp.log1p(jnp.exp(-jnp.abs(v)))


def _rope_sign_mask(shape):
    lane = lax.broadcasted_iota(jnp.int32, shape, 1)
    return (lane % HD) < (HD // 2)


def _rope(t, cs, sn, inverse):
    r_dn = pltpu.roll(t, HD // 2, 1)
    r_up = pltpu.roll(t, LB - HD // 2, 1)
    first = _rope_sign_mask(t.shape)
    if inverse:
        rot = jnp.where(first, r_up, -r_dn)
    else:
        rot = jnp.where(first, -r_up, r_dn)
    return t * cs + rot * sn


def _norm_mod_fwd(xv, nw, shift, scale):
    r = lax.rsqrt(jnp.mean(xv * xv, axis=-1, keepdims=True) + EPS)
    xh = xv * r
    return (xh * nw) * (1.0 + scale) + shift


def _norm_mod_bwd(xv, dh, nw, scale):
    r = lax.rsqrt(jnp.mean(xv * xv, axis=-1, keepdims=True) + EPS)
    xh = xv * r
    xn = xh * nw
    d_shift = jnp.sum(dh, axis=0, keepdims=True)
    d_scale = jnp.sum(dh * xn, axis=0, keepdims=True)
    dxn = dh * (1.0 + scale)
    d_w = jnp.sum(dxn * xh, axis=0, keepdims=True)
    dxh = dxn * nw
    dx = r * (dxh - xh * jnp.mean(dxh * xh, axis=-1, keepdims=True))
    return dx, d_shift, d_scale, d_w


def _inproj_fwd(x, pos, invf, mod8, n1w, w_in):
    s = x.shape[0]
    tt = min(512, s)

    def body(x_ref, pos_ref, invf_ref, mod_ref, nw_ref, w_ref,
             q_ref, k_ref, v_ref, z_ref, xbc_ref, dtr_ref, h1_ref, cos_ref, sin_ref):
        h = _norm_mod_fwd(x_ref[...], nw_ref[...], mod_ref[0:1, :], mod_ref[1:2, :])
        hb = h.astype(BF16)
        h1_ref[...] = hb
        proj = _dot_nt(hb, w_ref[...])
        ang = pos_ref[...].astype(F32) * invf_ref[...]
        cs = jnp.cos(ang)
        sn = jnp.sin(ang)
        cos_ref[...] = cs
        sin_ref[...] = sn
        for a in range(AW // LB):
            q_ref[:, a * LB:(a + 1) * LB] = _rope(proj[:, O_Q + a * LB:O_Q + (a + 1) * LB], cs, sn, False).astype(BF16)
        k_ref[...] = _rope(proj[:, O_K:O_V], cs, sn, False).astype(BF16)
        v_ref[...] = proj[:, O_V:O_Z].astype(BF16)
        z_ref[...] = proj[:, O_Z:O_XBC]
        xbc_ref[...] = proj[:, O_XBC:O_DT]
        dtr_ref[...] = proj[:, O_DT:INP]

    return _pcall(
        body, name="inproj_fwd", grid=(s // tt,),
        in_specs=[_rows(tt, D), _rows(tt, 1), _const((1, LB)), _const((8, D)), _const((1, D)), _const((INP, D))],
        out_specs=[_rows(tt, AW), _rows(tt, KVW), _rows(tt, KVW), _rows(tt, SW), _rows(tt, D), _rows(tt, LB),
                   _rows(tt, D), _rows(tt, LB), _rows(tt, LB)],
        out_shape=[_sds((s, AW), BF16), _sds((s, KVW), BF16), _sds((s, KVW), BF16), _sds((s, SW), F32),
                   _sds((s, D), F32), _sds((s, LB), F32), _sds((s, D), BF16), _sds((s, LB), F32), _sds((s, LB), F32)],
        compiler_params=_params(),
    )(x, pos, invf, mod8, n1w, w_in)


QPG = 4
ATT_SCALE = 1.0 / math.sqrt(HD)


def _stack_heads(val, g):
    return jnp.concatenate([val[:, (QPG * g + hh) * HD:(QPG * g + hh + 1) * HD] for hh in range(QPG)], axis=0)


def _unstack_heads(groups):
    pieces = [grp[hh * LB:(hh + 1) * LB, :] for grp in groups for hh in range(QPG)]
    return [jnp.concatenate(pieces[2 * a:2 * a + 2], axis=1) for a in range(NQ // 2)]


def _upper_mask():
    row = lax.broadcasted_iota(jnp.int32, (QPG * LB, LB), 0)
    col = lax.broadcasted_iota(jnp.int32, (QPG * LB, LB), 1)
    return col > (row % LB)


def _sink_wide(sinks, g):
    return jnp.concatenate([jnp.broadcast_to(sinks[QPG * g + hh:QPG * g + hh + 1, 0:1], (LB, LB))
                            for hh in range(QPG)], axis=0)


def _row_sums_wide(v, terms):
    return _dot_sel(v, jnp.ones((v.shape[1], LB), BF16), terms)


def _band(upper, prev_part, cur_part):
    return jnp.where(upper, prev_part, cur_part)


def _attn_scores(n, qg, kcat, upper):
    sp = _dot_nt(qg, kcat[0:LB, :]) * ATT_SCALE
    sc = _dot_nt(qg, kcat[LB:2 * LB, :]) * ATT_SCALE
    return _band(upper, jnp.where(n > 0, sp, NEG), sc)


def _attn_softmax(comb, sink):
    m = jnp.maximum(jnp.max(comb, axis=-1, keepdims=True), sink)
    p = jnp.exp(comb - m)
    es = jnp.exp(sink - m)
    return p, es, _row_sums_wide(p, 1) + es


def _attn_fwd_block(n, q_ref, kp_ref, kc_ref, vp_ref, vc_ref, sink_ref, o_ref):
    qv = q_ref[...]
    kcat = jnp.concatenate([kp_ref[...], kc_ref[...]], axis=0)
    vcat = jnp.concatenate([vp_ref[...], vc_ref[...]], axis=0)
    sinks = sink_ref[...]
    upper = _upper_mask()
    outs = []
    for g in range(NQ // QPG):
        sl = slice(g * HD, (g + 1) * HD)
        p, _, denom = _attn_softmax(_attn_scores(n, _stack_heads(qv, g), kcat[:, sl], upper), _sink_wide(sinks, g))
        outs.append((_dot(jnp.where(upper, p, 0.0).astype(BF16), vcat[0:LB, sl])
                     + _dot(jnp.where(upper, 0.0, p).astype(BF16), vcat[LB:2 * LB, sl])) / denom[:, 0:HD])
    for g, grp in enumerate(outs):
        for hh in range(QPG):
            h = QPG * g + hh
            o_ref[:, h * HD:(h + 1) * HD] = grp[hh * LB:(hh + 1) * LB, :].astype(BF16)


def _cumsum_rows(a, reverse):
    row = lax.broadcasted_iota(jnp.int32, a.shape, 0)
    step = 1
    while step < LB:
        if reverse:
            a = a + jnp.where(row < LB - step, pltpu.roll(a, LB - step, 0), 0.0)
        else:
            a = a + jnp.where(row >= step, pltpu.roll(a, step, 0), 0.0)
        step *= 2
    return a


SUB = 8


def _conv_shifts(tail, cur):
    row = lax.broadcasted_iota(jnp.int32, tail.shape, 0)
    out = [cur]
    for j in range(1, CONVK):
        rolled = pltpu.roll(cur, j, 0)
        top = jnp.where(row < j, pltpu.roll(tail, j, 0), rolled[0:SUB, :])
        out.append(jnp.concatenate([top, rolled[SUB:, :]], axis=0))
    return out


def _conv_advances(du, head):
    row = lax.broadcasted_iota(jnp.int32, head.shape, 0)
    out = []
    for j in range(1, CONVK):
        rolled = pltpu.roll(du, LB - j, 0)
        bottom = jnp.where(row >= SUB - j, pltpu.roll(head, SUB - j, 0), rolled[LB - SUB:, :])
        out.append(jnp.concatenate([rolled[:LB - SUB, :], bottom], axis=0))
    return out


def _split(v, terms):
    out = []
    for _ in range(terms - 1):
        t = v.astype(BF16)
        out.append(t)
        v = v - t.astype(F32)
    out.append(v.astype(BF16))
    return out


def _dot_sel(v, sel, terms):
    parts = [_dot(t, sel) for t in _split(v, terms)]
    return functools.reduce(lambda a, b: a + b, parts)


def _dot_nt_sel(v, sel, terms):
    parts = [_dot_nt(t, sel) for t in _split(v, terms)]
    return functools.reduce(lambda a, b: a + b, parts)


def _ssd_pre(xt_ref, xc_ref, cw_ref, cb_ref, dtr_ref, sp_ref, n):
    cur = xc_ref[...]
    tail = jnp.where(n > 0, xt_ref[...], 0.0)
    sh = _conv_shifts(tail, cur)
    u = cb_ref[...] + cw_ref[CONVK - 1:CONVK, :] * sh[0]
    for j in range(1, CONVK):
        u = u + cw_ref[CONVK - 1 - j:CONVK - j, :] * sh[j]
    sg_u = _sigmoid(u)
    xc = u * sg_u
    pre = dtr_ref[...] + sp_ref[0:1, :]
    dt = _softplus(pre)
    a_neg = -jnp.exp(sp_ref[1:2, :])
    acs = _cumsum_rows(dt * a_neg, False)
    return sh, u, sg_u, xc, pre, dt, a_neg, acs


def _gated_norm_fwd(y, z, nw):
    sz = z * _sigmoid(z)
    yz = y * sz
    parts = []
    for g in range(2):
        t = yz[:, g * 256:(g + 1) * 256]
        parts.append(t * lax.rsqrt(jnp.mean(t * t, axis=-1, keepdims=True) + EPS))
    return jnp.concatenate(parts, axis=1) * nw


HPG = 4
GW = HPG * HD


class _SsdChunk:
    def __init__(self, xc, dt, acs, spv, e64, e128):
        self.e64, self.e128 = e64, e128
        self.acs_t = jnp.transpose(acs)
        alast = acs[LB - 1:LB, :]
        self.e_all = jnp.exp(acs)
        self.dte_all = jnp.exp(alast - acs)
        self.elast = jnp.exp(alast)
        wide = _dot_sel(jnp.concatenate([dt, self.e_all, self.dte_all], axis=0), e64, 1)
        self.dt_x, self.e_x, self.dte_x = wide[0:LB], wide[LB:2 * LB], wide[2 * LB:3 * LB]
        self.dsk_x = _dot_sel(spv, e64, 3)[2:3, :]
        ac_x = _dot_sel(acs, e128, 3)
        row = lax.broadcasted_iota(jnp.int32, (HPG * LB, LB), 0)
        col = lax.broadcasted_iota(jnp.int32, (HPG * LB, LB), 1)
        causal = (row % LB) >= col
        lane = lax.broadcasted_iota(jnp.int32, (LB, GW), 1)
        self.head_lanes = [(lane >= hh * HD) & (lane < (hh + 1) * HD) for hh in range(HPG)]
        self.xs, self.xdt, self.b, self.c, self.bb, self.cb16, self.cbm, self.dm_st, self.m_st = ([] for _ in range(9))
        for g in range(2):
            heads = range(HPG * g, HPG * (g + 1))
            ac_st = jnp.concatenate([ac_x[:, j * LB:(j + 1) * LB] for j in heads], axis=0)
            ar_st = jnp.concatenate([jnp.broadcast_to(self.acs_t[j:j + 1, :], (LB, LB)) for j in heads], axis=0)
            dm_st = jnp.exp(jnp.where(causal, ac_st - ar_st, NEG))
            bg = xc[:, SW + g * NST:SW + (g + 1) * NST]
            cg = xc[:, SW + 2 * NST + g * NST:SW + 2 * NST + (g + 1) * NST]
            bgb, cgb = bg.astype(BF16), cg.astype(BF16)
            cbm = _dot_nt(cgb, bgb)
            xs_g = xc[:, g * GW:(g + 1) * GW]
            self.xs.append(xs_g)
            self.xdt.append(xs_g * self.dt_x[:, g * GW:(g + 1) * GW])
            self.b.append(bg)
            self.c.append(cg)
            self.bb.append(bgb)
            self.cb16.append(cgb)
            self.cbm.append(cbm)
            self.dm_st.append(dm_st)
            self.m_st.append(jnp.concatenate([cbm] * HPG, axis=0) * dm_st)

    def elast_rows(self, g):
        return jnp.concatenate([jnp.broadcast_to(self.elast[:, j:j + 1], (HD, NST))
                                for j in range(HPG * g, HPG * (g + 1))], axis=0)

    def diag_blocks(self, stacked):
        out = stacked[(HPG - 1) * LB:HPG * LB, :]
        for hh in range(HPG - 2, -1, -1):
            out = jnp.where(self.head_lanes[hh], stacked[hh * LB:(hh + 1) * LB, :], out)
        return out

    def block_diag(self, v):
        return jnp.concatenate([jnp.where(self.head_lanes[hh], v, 0.0) for hh in range(HPG)], axis=0)


def _ssd_fwd_block(n, xt_ref, xc_ref, cw_ref, cb_ref, dtr_ref, sp_ref, z_ref, nw_ref, e64_ref, e128_ref,
                   yn_ref, y_ref, hs_ref, h_scr):
    @pl.when(n == 0)
    def _():
        h_scr[...] = jnp.zeros_like(h_scr)

    h_all = h_scr[...]
    hs_ref[0] = h_all
    _, _, _, xc, _, dt, _, acs = _ssd_pre(xt_ref, xc_ref, cw_ref, cb_ref, dtr_ref, sp_ref, n)
    ck = _SsdChunk(xc, dt, acs, sp_ref[...], e64_ref[...], e128_ref[...])
    ys, hn = [], []
    for g in range(2):
        gl = slice(g * GW, (g + 1) * GW)
        xdt = ck.xdt[g]
        hg = h_all[gl, :]
        y_diag = ck.diag_blocks(_dot(ck.m_st[g].astype(BF16), xdt.astype(BF16)))
        y_off = ck.e_x[:, gl] * _dot_nt(ck.cb16[g], hg.astype(BF16))
        ys.append(y_diag + y_off + ck.xs[g] * ck.dsk_x[:, gl])
        hn.append(hg * ck.elast_rows(g) + _dot_tn((xdt * ck.dte_x[:, gl]).astype(BF16), ck.bb[g]))
    h_scr[...] = jnp.concatenate(hn, axis=0)
    y = jnp.concatenate(ys, axis=1)
    y_ref[...] = y
    yn_ref[...] = _gated_norm_fwd(y, z_ref[...], nw_ref[...]).astype(BF16)


def _mixer_fwd(q, k, v, sinks8, xbc, conv_w8, conv_b, dtr, ssm_p, z, nw, gathers):
    s = q.shape[0]
    nb = s // LB
    prev = lambda n: (jnp.maximum(n - 1, 0), 0)
    cur = lambda n: (n, 0)
    items, ex_shapes, n_g = _exchange_items(gathers, [])
    ne = len(items)

    n_in = 16
    relay_step = (nb - 1) // 2
    e64, e128 = _head_expanders()

    def body(*refs):
        a_in, s_in, ex_in = refs[:6], refs[6:n_in], refs[n_in:n_in + ne]
        o_ref, yn_ref, y_ref, hs_ref = refs[n_in + ne:n_in + 4 + ne]
        ex_out = refs[n_in + 4 + ne:n_in + 4 + 2 * ne]
        h_scr = refs[n_in + 4 + 2 * ne]
        sems = refs[n_in + 5 + 2 * ne:]
        n = pl.program_id(0)

        @pl.when(n == 0)
        def _():
            _Exchange(n_g, ex_in, ex_out, sems).two_level_start()

        _attn_fwd_block(n, *a_in, o_ref)
        _ssd_fwd_block(n, *s_in, yn_ref, y_ref, hs_ref, h_scr)

        @pl.when(n == relay_step)
        def _():
            _Exchange(n_g, ex_in, ex_out, sems).two_level_relay()

        @pl.when(n == nb - 1)
        def _():
            _Exchange(n_g, ex_in, ex_out, sems).two_level_finish()

    any_spec = pl.BlockSpec(memory_space=pl.ANY)
    tail = pl.BlockSpec((SUB, D), lambda n: (jnp.maximum(n * (LB // SUB) - 1, 0), 0))
    outs = _pcall(
        body, name="mixer_fwd", grid=(nb,),
        in_specs=[pl.BlockSpec((LB, AW), cur), pl.BlockSpec((LB, KVW), prev), pl.BlockSpec((LB, KVW), cur),
                  pl.BlockSpec((LB, KVW), prev), pl.BlockSpec((LB, KVW), cur), _const((8, LB)),
                  tail, pl.BlockSpec((LB, D), cur), _const((8, D)), _const((1, D)),
                  pl.BlockSpec((LB, LB), cur), _const((8, LB)), pl.BlockSpec((LB, SW), cur), _const((1, SW)),
                  _const(e64.shape), _const(e128.shape)]
        + [any_spec] * ne,
        out_specs=[pl.BlockSpec((LB, AW), cur), pl.BlockSpec((LB, SW), cur), pl.BlockSpec((LB, SW), cur),
                   pl.BlockSpec((1, NH * HD, NST), lambda n: (n, 0, 0))] + [any_spec] * ne,
        out_shape=[_sds((s, AW), BF16), _sds((s, SW), BF16), _sds((s, SW), F32), _sds((nb, NH * HD, NST), F32)]
        + ex_shapes,
        scratch_shapes=[pltpu.VMEM((NH * HD, NST), F32)] + _exchange_sems(ne),
        compiler_params=_params(),
    )(q, k, k, v, v, sinks8, xbc, xbc, conv_w8, conv_b, dtr, ssm_p, z, nw, e64, e128, *items)
    return outs[0], outs[1], outs[2], outs[3], outs[4:]


def _head_expanders():
    j = lax.broadcasted_iota(jnp.int32, (LB, NH * HD), 0)
    e64 = (lax.broadcasted_iota(jnp.int32, (LB, NH * HD), 1) // HD == j).astype(BF16)
    j = lax.broadcasted_iota(jnp.int32, (LB, NH * LB), 0)
    e128 = (lax.broadcasted_iota(jnp.int32, (LB, NH * LB), 1) // LB == j).astype(BF16)
    return e64, e128


def _outproj_ffn_fwd_loss(attn, yn, x, tgt, mod8, n2w, fnw, w_out, w_gu_t, w_down):
    s = x.shape[0]
    tf = min(256, s)

    def body(a_ref, y_ref, x_ref, t_ref, mod_ref, nw_ref, fw_ref, wo_ref, wgu_ref, wd_ref,
             x2_ref, h2_ref, mo_ref, mix_ref, gu_ref, act_ref, dx3_ref, sm_ref):
        i = pl.program_id(0)

        @pl.when(i == 0)
        def _():
            sm_ref[...] = jnp.zeros_like(sm_ref)

        mix = jnp.concatenate([a_ref[...], y_ref[...]], axis=1)
        mix_ref[...] = mix
        mo = _dot(mix, wo_ref[...])
        mo_ref[...] = mo.astype(BF16)
        x2 = x_ref[...] + mod_ref[2:3, :] * mo
        x2_ref[...] = x2
        h2 = _norm_mod_fwd(x2, nw_ref[...], mod_ref[3:4, :], mod_ref[4:5, :]).astype(BF16)
        h2_ref[...] = h2
        gu = _dot_nt(h2, wgu_ref[...])
        gu_ref[...] = gu.astype(BF16)
        g = gu[:, :DFF]
        act = (g * _sigmoid(g) * gu[:, DFF:]).astype(BF16)
        act_ref[...] = act
        ff = _dot(act, wd_ref[...])
        x3 = x2 + mod_ref[5:6, :] * ff
        r = lax.rsqrt(jnp.mean(x3 * x3, axis=-1, keepdims=True) + EPS)
        xh = x3 * r
        fw = fw_ref[...]
        err = xh * fw - t_ref[...]
        dy = err * (1.0 / D)
        dxh = dy * fw
        dx3 = r * (dxh - xh * jnp.mean(dxh * xh, axis=-1, keepdims=True))
        dx3_ref[...] = dx3
        sm_ref[0:1, :] += jnp.sum(dx3 * ff, axis=0, keepdims=True)
        sm_ref[1:2, :] += jnp.sum(dy * xh, axis=0, keepdims=True)
        sm_ref[2:3, :] += jnp.sum(err * err, axis=0, keepdims=True)

    return _pcall(
        body, name="outproj_ffn_fwd_loss", grid=(s // tf,),
        in_specs=[_rows(tf, AW), _rows(tf, SW), _rows(tf, D), _rows(tf, D), _const((8, D)), _const((1, D)),
                  _const((1, D)), _const((D, D)), _const((2 * DFF, D)), _const((DFF, D))],
        out_specs=[_rows(tf, D), _rows(tf, D), _rows(tf, D), _rows(tf, D), _rows(tf, 2 * DFF), _rows(tf, DFF),
                   _rows(tf, D), pl.BlockSpec((8, D), lambda i: (0, 0))],
        out_shape=[_sds((s, D), F32), _sds((s, D), BF16), _sds((s, D), BF16), _sds((s, D), BF16),
                   _sds((s, 2 * DFF), BF16), _sds((s, DFF), BF16), _sds((s, D), F32), _sds((8, D), F32)],
        compiler_params=_params(),
    )(attn, yn, x, tgt, mod8, n2w, fnw, w_out, w_gu_t, w_down)


def _ffn_bwd(dx3, gu, x2, mixout, mod8, n2w, w_gu, w_down, w_out):
    s = x2.shape[0]
    tb = min(256, s)

    def body(dx3_ref, gu_ref, x2_ref, mo_ref, mod_ref, nw_ref, wgu_ref, wd_ref, wo_ref,
             dx2_ref, dff_ref, dgu_ref, dmix_ref, dattn_ref, dyn_ref, sm_ref):
        i = pl.program_id(0)

        @pl.when(i == 0)
        def _():
            sm_ref[...] = jnp.zeros_like(sm_ref)

        dx3 = dx3_ref[...]
        dff = (dx3 * mod_ref[5:6, :]).astype(BF16)
        dff_ref[...] = dff
        dact = _dot_nt(dff, wd_ref[...])
        g = gu_ref[:, :DFF].astype(F32)
        u = gu_ref[:, DFF:].astype(F32)
        sg = _sigmoid(g)
        dgu = jnp.concatenate([dact * u * sg * (1.0 + g * (1.0 - sg)), dact * g * sg], axis=1).astype(BF16)
        dgu_ref[...] = dgu
        dh2 = _dot(dgu, wgu_ref[...])
        dxn, d_shift, d_scale, d_w = _norm_mod_bwd(x2_ref[...], dh2, nw_ref[...], mod_ref[4:5, :])
        dx2 = dx3 + dxn
        dx2_ref[...] = dx2
        sm_ref[0:1, :] += d_shift
        sm_ref[1:2, :] += d_scale
        sm_ref[2:3, :] += d_w
        sm_ref[3:4, :] += jnp.sum(dx2 * mo_ref[...].astype(F32), axis=0, keepdims=True)
        dmix = (dx2 * mod_ref[2:3, :]).astype(BF16)
        dmix_ref[...] = dmix
        dmi = _dot_nt(dmix, wo_ref[...])
        dattn_ref[...] = dmi[:, :AW].astype(BF16)
        dyn_ref[...] = dmi[:, AW:]

    return _pcall(
        body, name="ffn_bwd", grid=(s // tb,),
        in_specs=[_rows(tb, D), _rows(tb, 2 * DFF), _rows(tb, D), _rows(tb, D), _const((8, D)), _const((1, D)),
                  _const((2 * DFF, D)), _const((DFF, D)), _const((D, D))],
        out_specs=[_rows(tb, D), _rows(tb, D), _rows(tb, 2 * DFF), _rows(tb, D), _rows(tb, AW), _rows(tb, SW),
                   pl.BlockSpec((8, D), lambda i: (0, 0))],
        out_shape=[_sds((s, D), F32), _sds((s, D), BF16), _sds((s, 2 * DFF), BF16), _sds((s, D), BF16),
                   _sds((s, AW), BF16), _sds((s, SW), F32), _sds((8, D), F32)],
        compiler_params=_params(),
    )(dx3, gu, x2, mixout, mod8, n2w, w_gu, w_down, w_out)


def _ssd_bwd_block(i, n, *refs):
    def run(dyn_ref, y_ref, z_ref, xt_ref, xc_ref, cw_ref, cb_ref, dtr_ref, sp_ref, nw_ref, hs_ref, e64_ref, e128_ref,
            dzxd_ref, sm_ref, dh_scr, dun_scr):
        @pl.when(i == 0)
        def _():
            dh_scr[...] = jnp.zeros_like(dh_scr)
            dun_scr[...] = jnp.zeros_like(dun_scr)
            sm_ref[...] = jnp.zeros_like(sm_ref)

        sh, u, sg_u, xc, pre, dt, a_neg, acs = _ssd_pre(xt_ref, xc_ref, cw_ref, cb_ref, dtr_ref, sp_ref, n)
        ck = _SsdChunk(xc, dt, acs, sp_ref[...], e64_ref[...], e128_ref[...])
        h_all = hs_ref[0]
        dh_all = dh_scr[...]
        riota = lax.broadcasted_iota(jnp.int32, (LB, LB), 0)
        lane1 = lax.broadcasted_iota(jnp.int32, (1, LB), 1)

        z = z_ref[...]
        y = y_ref[...]
        sgz = _sigmoid(z)
        sz = z * sgz
        yz = y * sz
        nwv = nw_ref[...]
        dyn_v = dyn_ref[...]
        dyhat = dyn_v * nwv
        yhat_parts, dyz_parts = [], []
        for g in range(2):
            gs = slice(g * 256, (g + 1) * 256)
            t = yz[:, gs]
            rg = lax.rsqrt(jnp.mean(t * t, axis=-1, keepdims=True) + EPS)
            yh = t * rg
            dyh = dyhat[:, gs]
            yhat_parts.append(yh)
            dyz_parts.append(rg * (dyh - yh * jnp.mean(dyh * yh, axis=-1, keepdims=True)))
        yhat = jnp.concatenate(yhat_parts, axis=1)
        dyz = jnp.concatenate(dyz_parts, axis=1)
        sm_ref[5:6, 0:SW] += jnp.sum(dyn_v * yhat, axis=0, keepdims=True)
        dy = dyz * sz
        dzxd_ref[:, 0:SW] = (dyz * y * sgz * (1.0 + z * (1.0 - sgz))).astype(BF16)

        cat = lambda parts: jnp.concatenate(parts, axis=1)
        dxs, dbs, dcs, dhp, g_cat, de_x, ddte_x, ddt_x, ddsk_x = ([] for _ in range(9))
        dacs_t = jnp.zeros((LB, LB), F32)
        hsum = jnp.zeros((1, LB), F32)
        for g in range(2):
            gl = slice(g * GW, (g + 1) * GW)
            xs_g, xdt, bgb, cgb = ck.xs[g], ck.xdt[g], ck.bb[g], ck.cb16[g]
            m_st, dm_st = ck.m_st[g], ck.dm_st[g]
            dt_x, e_x, dte_x = ck.dt_x[:, gl], ck.e_x[:, gl], ck.dte_x[:, gl]
            xdtb = xdt.astype(BF16)
            hg, dhn = h_all[gl, :], dh_all[gl, :]
            hb, dhnb = hg.astype(BF16), dhn.astype(BF16)
            dy_g = dy[:, gl]
            ddsk_x.append(jnp.sum(dy_g * xs_g, axis=0, keepdims=True))
            dy_bd = ck.block_diag(dy_g).astype(BF16)
            dm4 = _dot_nt(dy_bd, xdtb)
            dxdt = _dot_tn(m_st.astype(BF16), dy_bd)
            gmat = dm4 * m_st
            dcbm = dm4 * dm_st
            dcb = dcbm[0:LB] + dcbm[LB:2 * LB] + dcbm[2 * LB:3 * LB] + dcbm[3 * LB:4 * LB]
            g_cat.append(cat([gmat[hh * LB:(hh + 1) * LB, :] for hh in range(HPG)]))
            for hh in range(HPG):
                j = HPG * g + hh
                col_sum = jnp.sum(gmat[hh * LB:(hh + 1) * LB, :], axis=0, keepdims=True)
                dacs_t = dacs_t - jnp.where(riota == j, col_sum, 0.0)
                hsl = slice(hh * HD, (hh + 1) * HD)
                hsum = hsum + jnp.where(lane1 == j, jnp.sum(dhn[hsl, :] * hg[hsl, :]), 0.0)
            dchb = (dy_g * e_x).astype(BF16)
            dcg = _dot(dchb, hb)
            dh_prev = _dot_tn(dchb, cgb)
            de_x.append(dy_g * _dot_nt(cgb, hb))
            dxs_s = _dot_nt(bgb, dhnb)
            dbg = _dot((xdt * dte_x).astype(BF16), dhnb)
            dxdt = dxdt + dxs_s * dte_x
            ddte_x.append(dxs_s * xdt)
            dhp.append(dhn * ck.elast_rows(g) + dh_prev)
            dxs.append(dy_g * ck.dsk_x[:, gl] + dxdt * dt_x)
            ddt_x.append(dxdt * xs_g)
            dcbb = dcb.astype(BF16)
            dbs.append(dbg + _dot_tn(dcbb, cgb))
            dcs.append(dcg + _dot(dcbb, bgb))
        dh_scr[...] = jnp.concatenate(dhp, axis=0)
        red = _dot_nt_sel(jnp.concatenate([cat(de_x), cat(ddte_x), cat(ddt_x)], axis=0), ck.e64, 2)
        de_c, ddte_c, ddt_c = red[0:LB], red[LB:2 * LB], red[2 * LB:3 * LB]
        ddsk = _dot_nt_sel(jnp.broadcast_to(cat(ddsk_x), (SUB, NH * HD)), ck.e64, 2)[0:1, :]
        t1 = ddte_c * ck.dte_all
        dalast = jnp.sum(t1, axis=0, keepdims=True) + hsum * ck.elast
        dacs = (_dot_nt_sel(cat(g_cat), ck.e128, 2) + de_c * ck.e_all - t1 + jnp.transpose(dacs_t)
                + jnp.where(riota == LB - 1, dalast, 0.0))
        da = _cumsum_rows(dacs, True)
        ddt = ddt_c + da * a_neg
        da_log = jnp.sum(da * dt, axis=0, keepdims=True) * a_neg
        ddtr = ddt * _sigmoid(pre)
        dzxd_ref[:, SW + D:ZXD] = ddtr.astype(BF16)
        sm_ref[6:7, 0:LB] += jnp.sum(ddtr, axis=0, keepdims=True)
        sm_ref[6:7, LB:2 * LB] += da_log
        sm_ref[6:7, 2 * LB:3 * LB] += ddsk

        du = cat(dxs + dbs + dcs) * (sg_u * (1.0 + u * (1.0 - sg_u)))
        sm_ref[0:1, :] += jnp.sum(du, axis=0, keepdims=True)
        for k in range(CONVK):
            sm_ref[1 + k:2 + k, :] += jnp.sum(du * sh[CONVK - 1 - k], axis=0, keepdims=True)
        adv = _conv_advances(du, dun_scr[...])
        dxbc = cw_ref[CONVK - 1:CONVK, :] * du
        for j in range(1, CONVK):
            dxbc = dxbc + cw_ref[CONVK - 1 - j:CONVK - j, :] * adv[j - 1]
        dun_scr[...] = du[0:SUB, :]
        dzxd_ref[:, SW:SW + D] = dxbc.astype(BF16)

    run(*refs)


def _attn_bwd_block(i, n, q_ref, kp_ref, kc_ref, vp_ref, vc_ref, o_ref, do_ref, cos_ref, sin_ref, sink_ref,
                    dq_ref, dkv_ref, ds_ref, ck_scr, cv_scr):
    @pl.when(i == 0)
    def _():
        ds_ref[...] = jnp.zeros_like(ds_ref)
        ck_scr[...] = jnp.zeros_like(ck_scr)
        cv_scr[...] = jnp.zeros_like(cv_scr)

    qv, ov, dov, sinks = q_ref[...], o_ref[...], do_ref[...], sink_ref[...]
    kcat = jnp.concatenate([kp_ref[...], kc_ref[...]], axis=0)
    vcat = jnp.concatenate([vp_ref[...], vc_ref[...]], axis=0)
    upper = _upper_mask()
    srow = lax.broadcasted_iota(jnp.int32, (8, LB), 0)
    slane = lax.broadcasted_iota(jnp.int32, (8, LB), 1)
    dsink = jnp.zeros((8, LB), F32)
    dq_g, dk_g, dv_g = [], [], []
    for g in range(NQ // QPG):
        sl = slice(g * HD, (g + 1) * HD)
        qg = _stack_heads(qv, g)
        dog = _stack_heads(dov, g)
        p, es, denom = _attn_softmax(_attn_scores(n, qg, kcat[:, sl], upper), _sink_wide(sinks, g))
        probs, psink = p / denom, es / denom
        delta = _row_sums_wide(dog.astype(F32) * _stack_heads(ov, g).astype(F32), 2)
        dsc = probs * (_band(upper, _dot_nt(dog, vcat[0:LB, sl]), _dot_nt(dog, vcat[LB:2 * LB, sl])) - delta)
        sink_terms = (psink * delta)[:, 0:1]
        for hh in range(QPG):
            dsink = dsink - jnp.where((srow == QPG * g + hh) & (slane == 0),
                                      jnp.sum(sink_terms[hh * LB:(hh + 1) * LB, :]), 0.0)
        ds_p = jnp.where(upper, dsc, 0.0).astype(BF16)
        ds_c = jnp.where(upper, 0.0, dsc).astype(BF16)
        dq_g.append((_dot(ds_p, kcat[0:LB, sl]) + _dot(ds_c, kcat[LB:2 * LB, sl])) * ATT_SCALE)
        dk_g.append(jnp.concatenate([_dot_tn(ds_p, qg), _dot_tn(ds_c, qg)], axis=0) * ATT_SCALE)
        dv_g.append(jnp.concatenate([_dot_tn(jnp.where(upper, probs, 0.0).astype(BF16), dog),
                                     _dot_tn(jnp.where(upper, 0.0, probs).astype(BF16), dog)], axis=0))
    ds_ref[...] += dsink
    cs = cos_ref[...]
    sn = sin_ref[...]
    dk2 = jnp.concatenate(dk_g, axis=1)
    dv2 = jnp.concatenate(dv_g, axis=1)
    for a, tile in enumerate(_unstack_heads(dq_g)):
        dq_ref[:, a * LB:(a + 1) * LB] = _rope(tile, cs, sn, True).astype(BF16)
    dkv_ref[:, 0:KVW] = _rope(ck_scr[...] + dk2[LB:2 * LB, :], cs, sn, True).astype(BF16)
    dkv_ref[:, KVW:2 * KVW] = (cv_scr[...] + dv2[LB:2 * LB, :]).astype(BF16)
    ck_scr[...] = dk2[0:LB, :]
    cv_scr[...] = dv2[0:LB, :]


def _mixer_bwd(q, k, v, o, do, cos, sin, sinks8, dyn, y, z, xbc, conv_w8, conv_b, dtr, ssm_p, nw, hs, scatters):
    s = q.shape[0]
    nb = s // LB
    cur = lambda i: (nb - 1 - i, 0)
    prev = lambda i: (jnp.maximum(nb - 2 - i, 0), 0)
    n_in = 23
    items, ex_shapes, n_g = _exchange_items([], scatters)
    ne = len(items)
    e64, e128 = _head_expanders()

    def body(*refs):
        i = pl.program_id(0)
        n = nb - 1 - i
        a_in, s_in, ex_in = refs[:10], refs[10:n_in], refs[n_in:n_in + ne]
        dp_ref, ds_ref, sm_ref = refs[n_in + ne:n_in + ne + 3]
        dq_ref, dkv_ref, dzxd_ref = dp_ref.at[:, O_Q:O_K], dp_ref.at[:, O_K:O_Z], dp_ref.at[:, O_Z:INP]
        ex_out = refs[n_in + ne + 3:n_in + 2 * ne + 3]
        ck_scr, cv_scr, dh_scr, dun_scr = refs[n_in + 2 * ne + 3:n_in + 2 * ne + 7]
        sems = refs[n_in + 2 * ne + 7:]

        @pl.when(i == 0)
        def _():
            _Exchange(n_g, ex_in, ex_out, sems).start()

        _attn_bwd_block(i, n, *a_in, dq_ref, dkv_ref, ds_ref, ck_scr, cv_scr)
        _ssd_bwd_block(i, n, *s_in, dzxd_ref, sm_ref, dh_scr, dun_scr)

        @pl.when(i == nb - 1)
        def _():
            _Exchange(n_g, ex_in, ex_out, sems).finish()

    any_spec = pl.BlockSpec(memory_space=pl.ANY)
    tail = pl.BlockSpec((SUB, D), lambda i: (jnp.maximum((nb - 1 - i) * (LB // SUB) - 1, 0), 0))
    outs = _pcall(
        body, name="mixer_bwd", grid=(nb,),
        in_specs=[pl.BlockSpec((LB, AW), cur), pl.BlockSpec((LB, KVW), prev), pl.BlockSpec((LB, KVW), cur),
                  pl.BlockSpec((LB, KVW), prev), pl.BlockSpec((LB, KVW), cur), pl.BlockSpec((LB, AW), cur),
                  pl.BlockSpec((LB, AW), cur), pl.BlockSpec((LB, LB), cur), pl.BlockSpec((LB, LB), cur),
                  _const((8, LB)),
                  pl.BlockSpec((LB, SW), cur), pl.BlockSpec((LB, SW), cur), pl.BlockSpec((LB, SW), cur),
                  tail, pl.BlockSpec((LB, D), cur), _const((8, D)), _const((1, D)),
                  pl.BlockSpec((LB, LB), cur), _const((8, LB)), _const((1, SW)),
                  pl.BlockSpec((1, NH * HD, NST), lambda i: (nb - 1 - i, 0, 0)),
                  _const(e64.shape), _const(e128.shape)] + [any_spec] * ne,
        out_specs=[pl.BlockSpec((LB, INP), cur), pl.BlockSpec((8, LB), lambda i: (0, 0)),
                   pl.BlockSpec((8, D), lambda i: (0, 0))] + [any_spec] * ne,
        out_shape=[_sds((s, INP), BF16), _sds((8, LB), F32), _sds((8, D), F32)] + ex_shapes,
        scratch_shapes=[pltpu.VMEM((LB, KVW), F32), pltpu.VMEM((LB, KVW), F32),
                        pltpu.VMEM((NH * HD, NST), F32), pltpu.VMEM((SUB, D), F32)]
        + _exchange_sems(ne),
        compiler_params=_params(),
    )(q, k, k, v, v, o, do, cos, sin, sinks8, dyn, y, z, xbc, xbc, conv_w8, conv_b, dtr, ssm_p, nw, hs, e64, e128,
      *items)
    return outs[0], outs[1], outs[2], outs[3:]


def _inproj_bwd(dproj, x, dx2, mod8, n1w, w_in_t, scatters):
    s = x.shape[0]
    tt = min(512, s)
    nt = s // tt
    items, ex_shapes, n_g = _exchange_items([], scatters)
    ne = len(items)

    def body(*refs):
        dp_ref, x_ref, dx2_ref, mod_ref, nw_ref, w_ref = refs[:6]
        ex_in = refs[6:6 + ne]
        gx_ref, sm_ref = refs[6 + ne:8 + ne]
        ex_out = refs[8 + ne:8 + 2 * ne]
        sems = refs[8 + 2 * ne:]
        i = pl.program_id(0)

        @pl.when(i == 0)
        def _():
            sm_ref[...] = jnp.zeros_like(sm_ref)
            _Exchange(n_g, ex_in, ex_out, sems).start()

        dh1 = _dot(dp_ref[...], w_ref[...])
        dxn, d_shift, d_scale, d_w = _norm_mod_bwd(x_ref[...], dh1, nw_ref[...], mod_ref[1:2, :])
        gx_ref[...] = dx2_ref[...] + dxn
        sm_ref[0:1, :] += d_shift
        sm_ref[1:2, :] += d_scale
        sm_ref[2:3, :] += d_w

        @pl.when(i == nt - 1)
        def _():
            _Exchange(n_g, ex_in, ex_out, sems).finish()

    any_spec = pl.BlockSpec(memory_space=pl.ANY)
    outs = _pcall(
        body, name="inproj_bwd", grid=(nt,),
        in_specs=[_rows(tt, INP), _rows(tt, D), _rows(tt, D), _const((8, D)), _const((1, D)), _const((INP, D))]
        + [any_spec] * ne,
        out_specs=[_rows(tt, D), pl.BlockSpec((8, D), lambda i: (0, 0))] + [any_spec] * ne,
        out_shape=[_sds((s, D), F32), _sds((8, D), F32)] + ex_shapes,
        scratch_shapes=_exchange_sems(ne),
        compiler_params=_params(),
    )(dproj, x, dx2, mod8, n1w, w_in_t, *items)
    return outs[0], outs[1], outs[2:]


def _wgrad(a, b, name):
    s, m = a.shape
    n = b.shape[1]
    tk = min(2048, s)
    wide = (1408, 1024, 512)
    tm = next((t for t in wide if m % t == 0), m)
    tn = n if n <= 2048 else _largest_divisor(n, wide)
    nk = s // tk

    def body(a_ref, b_ref, o_ref, acc):
        kk = pl.program_id(2)

        @pl.when(kk == 0)
        def _():
            acc[...] = jnp.zeros_like(acc)

        acc[...] += _dot_tn(a_ref[...], b_ref[...])

        @pl.when(kk == nk - 1)
        def _():
            o_ref[...] = acc[...].astype(BF16)

    return _pcall(
        body, name=name, grid=(m // tm, n // tn, nk),
        in_specs=[pl.BlockSpec((tk, tm), lambda i, j, kk: (kk, i)), pl.BlockSpec((tk, tn), lambda i, j, kk: (kk, j))],
        out_specs=pl.BlockSpec((tm, tn), lambda i, j, kk: (i, j)),
        out_shape=_sds((m, n), BF16),
        scratch_shapes=[pltpu.VMEM((tm, tn), F32)],
        compiler_params=_params(3),
    )(a, b)


PACK_ROWS = 24


def _pack_small(sm_f, sm_b, sm_s, sm_i, dsink):
    def body(f_ref, b_ref, s_ref, i_ref, k_ref, o_ref):
        o_ref[...] = jnp.zeros_like(o_ref)
        o_ref[0:2, :] = i_ref[0:2, :]
        o_ref[2:3, :] = b_ref[3:4, :]
        o_ref[3:5, :] = b_ref[0:2, :]
        o_ref[5:6, :] = f_ref[0:1, :]
        o_ref[6:7, :] = i_ref[2:3, :]
        o_ref[7:8, :] = b_ref[2:3, :]
        o_ref[8:9, :] = f_ref[1:2, :]
        o_ref[9:14, :] = s_ref[0:5, :]
        o_ref[14:15, :] = s_ref[5:6, :]
        o_ref[15:16, 0:3 * LB] = s_ref[6:7, 0:3 * LB]
        lane = lax.broadcasted_iota(jnp.int32, (1, LB), 1)
        sk = jnp.zeros((1, LB), F32)
        for h in range(NQ):
            sk = sk + jnp.where(lane == h, k_ref[h:h + 1, 0:1], 0.0)
        o_ref[15:16, 3 * LB:4 * LB] = sk
        o_ref[16:17, :] = f_ref[2:3, :]

    return _pcall(body, name="pack_small", out_shape=_sds((PACK_ROWS, D), F32))(sm_f, sm_b, sm_s, sm_i, dsink)


def _exchange(gathers, scatters, name, two_level=False):
    items, shapes, n_g = _exchange_items(gathers, scatters)
    n = len(items)
    assert not (two_level and scatters)

    def body(*refs):
        ex = _Exchange(n_g, refs[:n], refs[n:2 * n], refs[2 * n:])
        if two_level:
            ex.gather_two_level()
        else:
            ex.start()
            ex.finish()

    any_spec = pl.BlockSpec(memory_space=pl.ANY)
    return _pcall(
        body, name=name, in_specs=[any_spec] * n, out_specs=[any_spec] * n, out_shape=shapes,
        scratch_shapes=_exchange_sems(n),
    )(*items)


def _exchange_items(gathers, scatters):
    items = list(gathers) + list(scatters)
    shapes = [_sds((N_DEV,) + a.shape, a.dtype) for a in gathers] + [_sds(a.shape, a.dtype) for a in scatters]
    return items, shapes, len(gathers)


def _exchange_sems(n):
    return [pltpu.SemaphoreType.DMA((n, N_DEV - 1)), pltpu.SemaphoreType.DMA((n, N_DEV - 1)),
            pltpu.SemaphoreType.DMA((n,))]


class _Exchange:
    def __init__(self, n_g, ins, outs, sems):
        self.n_g, self.ins, self.outs = n_g, ins, outs
        self.send_sems, self.recv_sems, self.loc_sems = sems
        xi, yi, ci = lax.axis_index("x"), lax.axis_index("y"), lax.axis_index("c")
        self.me = 4 * xi + 2 * yi + ci
        self.peers = []
        for r in range(1, N_DEV):
            px = 1 - xi if r & 4 else xi
            py = 1 - yi if r & 2 else yi
            pc = 1 - ci if r & 1 else ci
            self.peers.append(((px, py, pc), 4 * px + 2 * py + pc))

    def _copy(self, t, r, landing):
        dev, peer = self.peers[r]
        src = self.ins[t] if t < self.n_g else self.ins[t].at[peer]
        return pltpu.make_async_remote_copy(
            src_ref=src, dst_ref=self.outs[t].at[landing], send_sem=self.send_sems.at[t, r],
            recv_sem=self.recv_sems.at[t, r], device_id=dev, device_id_type=pl.DeviceIdType.MESH)

    def _local(self, t):
        src = self.ins[t] if t < self.n_g else self.ins[t].at[self.me]
        return pltpu.make_async_copy(src, self.outs[t].at[self.me], self.loc_sems.at[t])

    def start(self):
        for t in range(len(self.ins)):
            self._local(t).start()
            for r in range(N_DEV - 1):
                self._copy(t, r, self.me).start()

    def finish(self):
        n = len(self.ins)
        for t in range(n):
            for r in range(N_DEV - 1):
                self._copy(t, r, self.peers[r][1]).wait_recv()
        for t in range(n):
            for r in range(N_DEV - 1):
                self._copy(t, r, self.me).wait_send()
            self._local(t).wait()

    def gather_two_level(self):
        self.two_level_start()
        self.two_level_relay()
        self.two_level_finish()

    DIRECT = (0, 1, 3, 5)

    def two_level_start(self):
        for t in range(len(self.ins)):
            self._local(t).start()
            for r in self.DIRECT:
                self._copy(t, r, self.me).start()

    def _relay(self, t, r):
        peer = self.peers[r][1]
        return pltpu.make_async_remote_copy(
            src_ref=self.outs[t].at[peer], dst_ref=self.outs[t].at[peer], send_sem=self.send_sems.at[t, r + 1],
            recv_sem=self.recv_sems.at[t, r + 1], device_id=self.peers[0][0], device_id_type=pl.DeviceIdType.MESH)

    def two_level_relay(self):
        for t in range(len(self.ins)):
            for r in self.DIRECT[1:]:
                self._copy(t, r, self.peers[r][1]).wait_recv()
                self._relay(t, r).start()

    def two_level_finish(self):
        n = len(self.ins)
        for t in range(n):
            for r in (0, 2, 4, 6):
                self._copy(t, r, self.peers[r][1]).wait_recv()
        for t in range(n):
            for r in self.DIRECT:
                self._copy(t, r, self.me).wait_send()
            for r in self.DIRECT[1:]:
                self._relay(t, r).wait_send()
            self._local(t).wait()


def _ada_fwd(c_all, w_cols, b_cols):
    def body(c_ref, w_ref, b_ref, o_ref):
        cv = c_ref[...]
        sc = (cv * _sigmoid(cv)).astype(BF16)
        o_ref[...] = _dot(sc, w_ref[...].astype(BF16)) + b_ref[...]

    return _pcall(body, name="ada_fwd", out_shape=_sds((N_DEV, w_cols.shape[1]), F32),
                  compiler_params=_params(0))(c_all, w_cols, b_cols)


def _adamw(w, g, m, v):
    m2 = ADAM_B1 * m + (1.0 - ADAM_B1) * g
    v2 = ADAM_B2 * v + (1.0 - ADAM_B2) * (g * g)
    m_hat = m2 / (1.0 - ADAM_B1 ** ADAM_STEP)
    v_hat = v2 / (1.0 - ADAM_B2 ** ADAM_STEP)
    delta = -ADAM_LR * (m_hat / (jnp.sqrt(v_hat) + ADAM_EPS) + ADAM_WD * w)
    return delta, m2, v2


def _sum_adamw(parts, w, m, v, name):
    rws, cols = w.shape
    tr = next((t for t in (256, 176, 128) if rws % t == 0), rws)

    def body(p_ref, w_ref, m_ref, v_ref, g_ref, d_ref, mo_ref, vo_ref):
        g = p_ref[0].astype(F32)
        for dev in range(1, N_DEV):
            g = g + p_ref[dev].astype(F32)
        g_ref[...] = g
        d_ref[...], mo_ref[...], vo_ref[...] = _adamw(w_ref[...], g, m_ref[...], v_ref[...])

    blk = pl.BlockSpec((tr, cols), lambda i: (i, 0))
    return _pcall(
        body, name=name, grid=(rws // tr,),
        in_specs=[pl.BlockSpec((N_DEV, tr, cols), lambda i: (0, i, 0)), blk, blk, blk],
        out_specs=[blk] * 4, out_shape=[_sds((rws, cols), F32)] * 4, compiler_params=_params(),
    )(parts, w, m, v)


def _wada_adamw(c_all, dmod_cols, w, m, v):
    rws, cols = w.shape
    tr = 256

    def body(c_ref, dm_ref, w_ref, m_ref, v_ref, g_ref, d_ref, mo_ref, vo_ref):
        cv = c_ref[...]
        sc = (cv * _sigmoid(cv)).astype(BF16)
        g = _dot_tn(sc, dm_ref[...].astype(BF16))
        g_ref[...] = g
        d_ref[...], mo_ref[...], vo_ref[...] = _adamw(w_ref[...], g, m_ref[...], v_ref[...])

    blk = pl.BlockSpec((tr, cols), lambda i: (i, 0))
    return _pcall(
        body, name="wada_adamw", grid=(rws // tr,),
        in_specs=[pl.BlockSpec((N_DEV, tr), lambda i: (0, i)), pl.BlockSpec((N_DEV, cols), lambda i: (0, 0)),
                  blk, blk, blk],
        out_specs=[blk] * 4, out_shape=[_sds((rws, cols), F32)] * 4, compiler_params=_params(),
    )(c_all, dmod_cols, w, m, v)


def _small_reduce(packs):
    def body(p_ref, o_ref):
        tot = p_ref[0]
        for dev in range(1, N_DEV):
            tot = tot + p_ref[dev]
        o_ref[...] = tot
        o_ref[16:17, :] = jnp.zeros((1, D), F32) + (0.5 / D) * jnp.sum(tot[16:17, :])

    return _pcall(body, name="small_reduce", out_shape=_sds((PACK_ROWS, D), F32))(packs)


def _adamw_many(ws, gs, ms, vs):
    k = len(ws)

    def body(*refs):
        for i in range(k):
            w_ref, g_ref, m_ref, v_ref = refs[i], refs[k + i], refs[2 * k + i], refs[3 * k + i]
            d_ref, mo_ref, vo_ref = refs[4 * k + i], refs[5 * k + i], refs[6 * k + i]
            d_ref[...], mo_ref[...], vo_ref[...] = _adamw(w_ref[...], g_ref[...], m_ref[...], v_ref[...])

    shp = [_sds(w.shape, F32) for w in ws]
    outs = _pcall(body, name="adamw_small", out_shape=shp * 3)(*ws, *gs, *ms, *vs)
    return outs[:k], outs[k:2 * k], outs[2 * k:]


def kernel(x, c, positions, w_ada, b_ada, norm1_w, w_in, conv_w, conv_b, dt_bias, a_log, d_skip, attn_sinks, ssm_norm_w, w_out, norm2_w, w_gate_up, w_down, final_norm_w, loss_target, m_w_ada, m_b_ada, m_norm1_w, m_w_in, m_conv_w, m_conv_b, m_dt_bias, m_a_log, m_d_skip, m_attn_sinks, m_ssm_norm_w, m_w_out, m_norm2_w, m_w_gate_up, m_w_down, m_final_norm_w, v_w_ada, v_b_ada, v_norm1_w, v_w_in, v_conv_w, v_conv_b, v_dt_bias, v_a_log, v_d_skip, v_attn_sinks, v_ssm_norm_w, v_w_out, v_norm2_w, v_w_gate_up, v_w_down, v_final_norm_w):
    s = x.shape[1]
    me = 4 * lax.axis_index("x") + 2 * lax.axis_index("y") + lax.axis_index("c")
    ada_cols = N_MOD * D // N_DEV

    c8 = jnp.pad(c, ((0, 7), (0, 0)))
    cw8 = jnp.pad(conv_w[0], ((0, 8 - CONVK), (0, 0)))
    w_in_t, m_w_in_t, v_w_in_t = jnp.transpose(w_in[0]), jnp.transpose(m_w_in[0]), jnp.transpose(v_w_in[0])
    w_gu_t, m_w_gu_t, v_w_gu_t = (jnp.transpose(w_gate_up[0]), jnp.transpose(m_w_gate_up[0]),
                                  jnp.transpose(v_w_gate_up[0]))
    g_c, g_in, g_cw = _exchange([c8, w_in_t.astype(BF16), cw8], [], "gather_in", two_level=True)
    c_all = g_c[:, 0, :]
    w_in_f = jnp.pad(g_in.reshape(IN_PROJ, D), ((0, INP - IN_PROJ), (0, 0)))
    conv_w8 = jnp.transpose(g_cw, (1, 0, 2)).reshape(8, D)

    b_cols = lax.dynamic_slice(b_ada, (0, me * ada_cols), (1, ada_cols))
    (g_mod,) = _exchange([_ada_fwd(c_all, w_ada[0], b_cols)], [], "gather_mod")
    mod = lax.dynamic_index_in_dim(g_mod, me, axis=1, keepdims=False).reshape(N_MOD, D)
    mod8 = jnp.pad(mod, ((0, 8 - N_MOD), (0, 0)))

    half = HD // 2
    inv_freq = ROPE_THETA ** (-jnp.arange(half, dtype=F32) / half)
    invf = jnp.tile(inv_freq, LB // half).reshape(1, LB)
    lanes = lambda a: jnp.pad(a, ((0, 0), (0, LB - a.shape[1])))
    ssm_p = jnp.pad(jnp.concatenate([lanes(dt_bias), lanes(a_log), lanes(d_skip)], axis=0), ((0, 5), (0, 0)))
    sinks8 = jnp.broadcast_to(attn_sinks.reshape(NQ, 1), (NQ, LB))

    xs, tgt, fnw = x[0], loss_target[0], final_norm_w.reshape(1, D)

    q, k, v, z, xbc, dtr, h1, cos, sin = _inproj_fwd(xs, positions[0].reshape(s, 1), invf, mod8, norm1_w, w_in_f)
    attn, yn, y, hs, (g_out, g_gu, g_down) = _mixer_fwd(
        q, k, v, sinks8, xbc, conv_w8, conv_b, dtr, ssm_p, z, ssm_norm_w,
        [w_out[0].astype(BF16), w_gu_t.astype(BF16), w_down[0].astype(BF16)])
    w_out_f = g_out.reshape(D, D)
    w_gu_f = g_gu.reshape(2 * DFF, D)
    w_down_f = g_down.reshape(DFF, D)
    x2, h2, mo, mix, gu, act, dx3, sm_f = _outproj_ffn_fwd_loss(attn, yn, xs, tgt, mod8, norm2_w, fnw, w_out_f, w_gu_f,
                                                                 w_down_f)

    dx2, dff, dgu, dmix, dattn, dyn, sm_b = _ffn_bwd(dx3, gu, x2, mo, mod8, norm2_w, w_gu_f, w_down_f, w_out_f)
    p_gu = _wgrad(dgu, h2, "wgrad_gate_up").reshape(N_DEV, 2 * DFF // N_DEV, D)
    p_down = _wgrad(act, dff, "wgrad_down").reshape(N_DEV, DFF // N_DEV, D)
    p_out = _wgrad(mix, dmix, "wgrad_out").reshape(N_DEV, D // N_DEV, D)
    dproj, dsink, sm_s, (r_gu, r_down, r_out) = _mixer_bwd(
        q, k, v, attn, dattn, cos, sin, sinks8, dyn, y, z, xbc, conv_w8, conv_b, dtr, ssm_p, ssm_norm_w, hs,
        [p_gu, p_down, p_out])
    p_in = _wgrad(dproj, h1, "wgrad_in")[:IN_PROJ].reshape(N_DEV, IN_PROJ // N_DEV, D)
    gx, sm_i, (r_in,) = _inproj_bwd(dproj, xs, dx2, mod8, norm1_w, w_in_f, [p_in])
    (g_pack,) = _exchange([_pack_small(sm_f, sm_b, sm_s, sm_i, dsink)], [], "gather_small")

    tot = _small_reduce(g_pack)
    loss = tot[16, 0]
    dmod_all = g_pack[:, 0:N_MOD, :].reshape(N_DEV, N_MOD * D)
    dmod_cols = lax.dynamic_slice(dmod_all, (0, me * ada_cols), (N_DEV, ada_cols))

    big = {
        "w_ada": _wada_adamw(c_all, dmod_cols, w_ada[0], m_w_ada[0], v_w_ada[0]),
        "w_in": [jnp.transpose(t) for t in _sum_adamw(r_in, w_in_t, m_w_in_t, v_w_in_t, "adamw_in")],
        "w_out": _sum_adamw(r_out, w_out[0], m_w_out[0], v_w_out[0], "adamw_out"),
        "w_gate_up": [jnp.transpose(t) for t in _sum_adamw(r_gu, w_gu_t, m_w_gu_t, v_w_gu_t, "adamw_gate_up")],
        "w_down": _sum_adamw(r_down, w_down[0], m_w_down[0], v_w_down[0], "adamw_down"),
    }
    small_names = ["b_ada", "norm1_w", "conv_w", "conv_b", "dt_bias", "a_log", "d_skip", "attn_sinks", "ssm_norm_w",
                   "norm2_w", "final_norm_w"]
    row15 = tot[15:16, :]
    small_g = {
        "b_ada": tot[0:N_MOD, :].reshape(1, N_MOD * D),
        "norm1_w": tot[6:7, :],
        "conv_w": lax.dynamic_slice(tot[10:14, :], (0, me * (D // N_DEV)), (CONVK, D // N_DEV)),
        "conv_b": tot[9:10, :],
        "dt_bias": row15[:, 0:NH],
        "a_log": row15[:, LB:LB + NH],
        "d_skip": row15[:, 2 * LB:2 * LB + NH],
        "attn_sinks": row15[:, 3 * LB:3 * LB + NQ],
        "ssm_norm_w": tot[14:15, 0:SW],
        "norm2_w": tot[7:8, :],
        "final_norm_w": tot[8:9, :],
    }
    small_w = {"b_ada": b_ada, "norm1_w": norm1_w, "conv_w": conv_w[0], "conv_b": conv_b, "dt_bias": dt_bias,
               "a_log": a_log, "d_skip": d_skip, "attn_sinks": attn_sinks, "ssm_norm_w": ssm_norm_w,
               "norm2_w": norm2_w, "final_norm_w": final_norm_w.reshape(1, D)}
    small_m = {"b_ada": m_b_ada, "norm1_w": m_norm1_w, "conv_w": m_conv_w[0], "conv_b": m_conv_b,
               "dt_bias": m_dt_bias, "a_log": m_a_log, "d_skip": m_d_skip, "attn_sinks": m_attn_sinks,
               "ssm_norm_w": m_ssm_norm_w, "norm2_w": m_norm2_w, "final_norm_w": m_final_norm_w.reshape(1, D)}
    small_v = {"b_ada": v_b_ada, "norm1_w": v_norm1_w, "conv_w": v_conv_w[0], "conv_b": v_conv_b,
               "dt_bias": v_dt_bias, "a_log": v_a_log, "d_skip": v_d_skip, "attn_sinks": v_attn_sinks,
               "ssm_norm_w": v_ssm_norm_w, "norm2_w": v_norm2_w, "final_norm_w": v_final_norm_w.reshape(1, D)}
    s_d, s_m, s_v = _adamw_many([small_w[k] for k in small_names], [small_g[k] for k in small_names],
                                [small_m[k] for k in small_names], [small_v[k] for k in small_names])

    order = ["w_ada", "b_ada", "norm1_w", "w_in", "conv_w", "conv_b", "dt_bias", "a_log", "d_skip", "attn_sinks",
             "ssm_norm_w", "w_out", "norm2_w", "w_gate_up", "w_down", "final_norm_w"]
    lead = {"w_ada", "w_in", "conv_w", "w_out", "w_gate_up", "w_down"}
    grads, deltas, new_m, new_v = [], [], [], []
    for name in order:
        if name in big:
            g, d, m2, v2 = big[name]
        else:
            i = small_names.index(name)
            g, d, m2, v2 = small_g[name], s_d[i], s_m[i], s_v[i]
        if name in lead:
            g, d, m2, v2 = g[None], d[None], m2[None], v2[None]
        if name == "final_norm_w":
            g, d, m2, v2 = g.reshape(D), d.reshape(D), m2.reshape(D), v2.reshape(D)
        grads.append(g)
        deltas.append(d)
        new_m.append(m2)
        new_v.append(v2)
    return (loss, gx[None], *grads, *deltas, *new_m, *new_v)
```

```python
import functools
import math

import jax
import jax.numpy as jnp
from jax import lax
from jax.experimental import pallas as pl
from jax.experimental.pallas import tpu as pltpu

F32 = jnp.float32
BF16 = jnp.bfloat16

N_DEV = 8
D = 1024
HD = 64
NQ = 8
AW = 512
KVW = 128
SW = 512
NST = 128
NH = 8
LB = 128
CONVK = 4
DFF = 2816
N_MOD = 6
IN_PROJ = 2312
INP = 2432
O_Q, O_K, O_V, O_Z, O_XBC, O_DT = 0, 512, 640, 768, 1280, 2304
ZXD = INP - O_Z
EPS = 1e-6
NEG = -1e30
ROPE_THETA = 10000.0
VMEM_LIMIT = 56 * 1024 * 1024

ADAM_LR = 0.001
ADAM_B1 = 0.9
ADAM_B2 = 0.999
ADAM_EPS = 1e-08
ADAM_WD = 0.01
ADAM_STEP = 10

NT_DIMS = (((1,), (1,)), ((), ()))
TN_DIMS = (((0,), (0,)), ((), ()))


def _pcall(body, **kw):
    return pl.pallas_call(body, **kw)


def _sds(shape, dtype):
    return jax.ShapeDtypeStruct(shape, dtype)


def _params(n_grid=1):
    return pltpu.CompilerParams(dimension_semantics=("arbitrary",) * n_grid, vmem_limit_bytes=VMEM_LIMIT)


def _const(shape):
    return pl.BlockSpec(shape, lambda *_: (0,) * len(shape), pipeline_mode=pl.Buffered(1))


def _largest_divisor(n, candidates):
    for cand in candidates:
        if n % cand == 0:
            return cand
    raise ValueError(f"no tile in {candidates} divides {n}")


def _rows(t, w):
    return pl.BlockSpec((t, w), lambda i: (i, 0))


def _dot(a, b):
    return jnp.dot(a, b, preferred_element_type=F32)


def _dot_nt(a, b):
    return lax.dot_general(a, b, NT_DIMS, preferred_element_type=F32)


def _dot_tn(a, b):
    return lax.dot_general(a, b, TN_DIMS, preferred_element_type=F32)


def _sigmoid(v):
    return 1.0 / (1.0 + jnp.exp(-v))


def _softplus(v):
    return jnp.maximum(v, 0.0) + jnp.log1p(jnp.exp(-jnp.abs(v)))


def _rope_sign_mask(shape):
    lane = lax.broadcasted_iota(jnp.int32, shape, 1)
    return (lane % HD) < (HD // 2)


def _rope(t, cs, sn, inverse):
    r_dn = pltpu.roll(t, HD // 2, 1)
    r_up = pltpu.roll(t, LB - HD // 2, 1)
    first = _rope_sign_mask(t.shape)
    if inverse:
        rot = jnp.where(first, r_up, -r_dn)
    else:
        rot = jnp.where(first, -r_up, r_dn)
    return t * cs + rot * sn


def _norm_mod_fwd(xv, nw, shift, scale):
    r = lax.rsqrt(jnp.mean(xv * xv, axis=-1, keepdims=True) + EPS)
    xh = xv * r
    return (xh * nw) * (1.0 + scale) + shift


def _norm_mod_bwd(xv, dh, nw, scale):
    r = lax.rsqrt(jnp.mean(xv * xv, axis=-1, keepdims=True) + EPS)
    xh = xv * r
    xn = xh * nw
    d_shift = jnp.sum(dh, axis=0, keepdims=True)
    d_scale = jnp.sum(dh * xn, axis=0, keepdims=True)
    dxn = dh * (1.0 + scale)
    d_w = jnp.sum(dxn * xh, axis=0, keepdims=True)
    dxh = dxn * nw
    dx = r * (dxh - xh * jnp.mean(dxh * xh, axis=-1, keepdims=True))
    return dx, d_shift, d_scale, d_w


def _inproj_fwd(x, pos, invf, mod8, n1w, w_in):
    s = x.shape[0]
    tt = min(512, s)

    def body(x_ref, pos_ref, invf_ref, mod_ref, nw_ref, w_ref,
             q_ref, k_ref, v_ref, z_ref, xbc_ref, dtr_ref, h1_ref, cos_ref, sin_ref):
        h = _norm_mod_fwd(x_ref[...], nw_ref[...], mod_ref[0:1, :], mod_ref[1:2, :])
        hb = h.astype(BF16)
        h1_ref[...] = hb
        proj = _dot_nt(hb, w_ref[...])
        ang = pos_ref[...].astype(F32) * invf_ref[...]
        cs = jnp.cos(ang)
        sn = jnp.sin(ang)
        cos_ref[...] = cs
        sin_ref[...] = sn
        for a in range(AW // LB):
            q_ref[:, a * LB:(a + 1) * LB] = _rope(proj[:, O_Q + a * LB:O_Q + (a + 1) * LB], cs, sn, False).astype(BF16)
        k_ref[...] = _rope(proj[:, O_K:O_V], cs, sn, False).astype(BF16)
        v_ref[...] = proj[:, O_V:O_Z].astype(BF16)
        z_ref[...] = proj[:, O_Z:O_XBC]
        xbc_ref[...] = proj[:, O_XBC:O_DT]
        dtr_ref[...] = proj[:, O_DT:INP]

    return _pcall(
        body, name="inproj_fwd", grid=(s // tt,),
        in_specs=[_rows(tt, D), _rows(tt, 1), _const((1, LB)), _const((8, D)), _const((1, D)), _const((INP, D))],
        out_specs=[_rows(tt, AW), _rows(tt, KVW), _rows(tt, KVW), _rows(tt, SW), _rows(tt, D), _rows(tt, LB),
                   _rows(tt, D), _rows(tt, LB), _rows(tt, LB)],
        out_shape=[_sds((s, AW), BF16), _sds((s, KVW), BF16), _sds((s, KVW), BF16), _sds((s, SW), F32),
                   _sds((s, D), F32), _sds((s, LB), F32), _sds((s, D), BF16), _sds((s, LB), F32), _sds((s, LB), F32)],
        compiler_params=_params(),
    )(x, pos, invf, mod8, n1w, w_in)


QPG = 4
ATT_SCALE = 1.0 / math.sqrt(HD)


def _stack_heads(val, g):
    return jnp.concatenate([val[:, (QPG * g + hh) * HD:(QPG * g + hh + 1) * HD] for hh in range(QPG)], axis=0)


def _unstack_heads(groups):
    pieces = [grp[hh * LB:(hh + 1) * LB, :] for grp in groups for hh in range(QPG)]
    return [jnp.concatenate(pieces[2 * a:2 * a + 2], axis=1) for a in range(NQ // 2)]


def _upper_mask():
    row = lax.broadcasted_iota(jnp.int32, (QPG * LB, LB), 0)
    col = lax.broadcasted_iota(jnp.int32, (QPG * LB, LB), 1)
    return col > (row % LB)


def _sink_wide(sinks, g):
    return jnp.concatenate([jnp.broadcast_to(sinks[QPG * g + hh:QPG * g + hh + 1, 0:1], (LB, LB))
                            for hh in range(QPG)], axis=0)


def _row_sums_wide(v, terms):
    return _dot_sel(v, jnp.ones((v.shape[1], LB), BF16), terms)


def _band(upper, prev_part, cur_part):
    return jnp.where(upper, prev_part, cur_part)


def _attn_scores(n, qg, kcat, upper):
    sp = _dot_nt(qg, kcat[0:LB, :]) * ATT_SCALE
    sc = _dot_nt(qg, kcat[LB:2 * LB, :]) * ATT_SCALE
    return _band(upper, jnp.where(n > 0, sp, NEG), sc)


def _attn_softmax(comb, sink):
    m = jnp.maximum(jnp.max(comb, axis=-1, keepdims=True), sink)
    p = jnp.exp(comb - m)
    es = jnp.exp(sink - m)
    return p, es, _row_sums_wide(p, 1) + es


def _attn_fwd_block(n, q_ref, kp_ref, kc_ref, vp_ref, vc_ref, sink_ref, o_ref):
    qv = q_ref[...]
    kcat = jnp.concatenate([kp_ref[...], kc_ref[...]], axis=0)
    vcat = jnp.concatenate([vp_ref[...], vc_ref[...]], axis=0)
    sinks = sink_ref[...]
    upper = _upper_mask()
    outs = []
    for g in range(NQ // QPG):
        sl = slice(g * HD, (g + 1) * HD)
        p, _, denom = _attn_softmax(_attn_scores(n, _stack_heads(qv, g), kcat[:, sl], upper), _sink_wide(sinks, g))
        outs.append((_dot(jnp.where(upper, p, 0.0).astype(BF16), vcat[0:LB, sl])
                     + _dot(jnp.where(upper, 0.0, p).astype(BF16), vcat[LB:2 * LB, sl])) / denom[:, 0:HD])
    for g, grp in enumerate(outs):
        for hh in range(QPG):
            h = QPG * g + hh
            o_ref[:, h * HD:(h + 1) * HD] = grp[hh * LB:(hh + 1) * LB, :].astype(BF16)


def _cumsum_rows(a, reverse):
    row = lax.broadcasted_iota(jnp.int32, a.shape, 0)
    step = 1
    while step < LB:
        if reverse:
            a = a + jnp.where(row < LB - step, pltpu.roll(a, LB - step, 0), 0.0)
        else:
            a = a + jnp.where(row >= step, pltpu.roll(a, step, 0), 0.0)
        step *= 2
    return a


SUB = 8


def _conv_shifts(tail, cur):
    row = lax.broadcasted_iota(jnp.int32, tail.shape, 0)
    out = [cur]
    for j in range(1, CONVK):
        rolled = pltpu.roll(cur, j, 0)
        top = jnp.where(row < j, pltpu.roll(tail, j, 0), rolled[0:SUB, :])
        out.append(jnp.concatenate([top, rolled[SUB:, :]], axis=0))
    return out


def _conv_advances(du, head):
    row = lax.broadcasted_iota(jnp.int32, head.shape, 0)
    out = []
    for j in range(1, CONVK):
        rolled = pltpu.roll(du, LB - j, 0)
        bottom = jnp.where(row >= SUB - j, pltpu.roll(head, SUB - j, 0), rolled[LB - SUB:, :])
        out.append(jnp.concatenate([rolled[:LB - SUB, :], bottom], axis=0))
    return out


def _split(v, terms):
    out = []
    for _ in range(terms - 1):
        t = v.astype(BF16)
        out.append(t)
        v = v - t.astype(F32)
    out.append(v.astype(BF16))
    return out


def _dot_sel(v, sel, terms):
    parts = [_dot(t, sel) for t in _split(v, terms)]
    return functools.reduce(lambda a, b: a + b, parts)


def _dot_nt_sel(v, sel, terms):
    parts = [_dot_nt(t, sel) for t in _split(v, terms)]
    return functools.reduce(lambda a, b: a + b, parts)


def _ssd_pre(xt_ref, xc_ref, cw_ref, cb_ref, dtr_ref, sp_ref, n):
    cur = xc_ref[...]
    tail = jnp.where(n > 0, xt_ref[...], 0.0)
    sh = _conv_shifts(tail, cur)
    u = cb_ref[...] + cw_ref[CONVK - 1:CONVK, :] * sh[0]
    for j in range(1, CONVK):
        u = u + cw_ref[CONVK - 1 - j:CONVK - j, :] * sh[j]
    sg_u = _sigmoid(u)
    xc = u * sg_u
    pre = dtr_ref[...] + sp_ref[0:1, :]
    dt = _softplus(pre)
    a_neg = -jnp.exp(sp_ref[1:2, :])
    acs = _cumsum_rows(dt * a_neg, False)
    return sh, u, sg_u, xc, pre, dt, a_neg, acs


def _gated_norm_fwd(y, z, nw):
    sz = z * _sigmoid(z)
    yz = y * sz
    parts = []
    for g in range(2):
        t = yz[:, g * 256:(g + 1) * 256]
        parts.append(t * lax.rsqrt(jnp.mean(t * t, axis=-1, keepdims=True) + EPS))
    return jnp.concatenate(parts, axis=1) * nw


HPG = 4
GW = HPG * HD


class _SsdChunk:
    def __init__(self, xc, dt, acs, spv, e64, e128):
        self.e64, self.e128 = e64, e128
        self.acs_t = jnp.transpose(acs)
        alast = acs[LB - 1:LB, :]
        self.e_all = jnp.exp(acs)
        self.dte_all = jnp.exp(alast - acs)
        self.elast = jnp.exp(alast)
        wide = _dot_sel(jnp.concatenate([dt, self.e_all, self.dte_all], axis=0), e64, 2)
        self.dt_x, self.e_x, self.dte_x = wide[0:LB], wide[LB:2 * LB], wide[2 * LB:3 * LB]
        self.dsk_x = _dot_sel(spv, e64, 3)[2:3, :]
        ac_x = _dot_sel(acs, e128, 3)
        row = lax.broadcasted_iota(jnp.int32, (HPG * LB, LB), 0)
        col = lax.broadcasted_iota(jnp.int32, (HPG * LB, LB), 1)
        causal = (row % LB) >= col
        lane = lax.broadcasted_iota(jnp.int32, (LB, GW), 1)
        self.head_lanes = [(lane >= hh * HD) & (lane < (hh + 1) * HD) for hh in range(HPG)]
        self.xs, self.xdt, self.b, self.c, self.bb, self.cb16, self.cbm, self.dm_st, self.m_st = ([] for _ in range(9))
        for g in range(2):
            heads = range(HPG * g, HPG * (g + 1))
            ac_st = jnp.concatenate([ac_x[:, j * LB:(j + 1) * LB] for j in heads], axis=0)
            ar_st = jnp.concatenate([jnp.broadcast_to(self.acs_t[j:j + 1, :], (LB, LB)) for j in heads], axis=0)
            dm_st = jnp.exp(jnp.where(causal, ac_st - ar_st, NEG))
            bg = xc[:, SW + g * NST:SW + (g + 1) * NST]
            cg = xc[:, SW + 2 * NST + g * NST:SW + 2 * NST + (g + 1) * NST]
            bgb, cgb = bg.astype(BF16), cg.astype(BF16)
            cbm = _dot_nt(cgb, bgb)
            xs_g = xc[:, g * GW:(g + 1) * GW]
            self.xs.append(xs_g)
            self.xdt.append(xs_g * self.dt_x[:, g * GW:(g + 1) * GW])
            self.b.append(bg)
            self.c.append(cg)
            self.bb.append(bgb)
            self.cb16.append(cgb)
            self.cbm.append(cbm)
            self.dm_st.append(dm_st)
            self.m_st.append(jnp.concatenate([cbm] * HPG, axis=0) * dm_st)

    def elast_rows(self, g):
        return jnp.concatenate([jnp.broadcast_to(self.elast[:, j:j + 1], (HD, NST))
                                for j in range(HPG * g, HPG * (g + 1))], axis=0)

    def diag_blocks(self, stacked):
        out = stacked[(HPG - 1) * LB:HPG * LB, :]
        for hh in range(HPG - 2, -1, -1):
            out = jnp.where(self.head_lanes[hh], stacked[hh * LB:(hh + 1) * LB, :], out)
        return out

    def block_diag(self, v):
        return jnp.concatenate([jnp.where(self.head_lanes[hh], v, 0.0) for hh in range(HPG)], axis=0)


def _ssd_fwd_block(n, xt_ref, xc_ref, cw_ref, cb_ref, dtr_ref, sp_ref, z_ref, nw_ref, e64_ref, e128_ref,
                   yn_ref, y_ref, hs_ref, h_scr):
    @pl.when(n == 0)
    def _():
        h_scr[...] = jnp.zeros_like(h_scr)

    h_all = h_scr[...]
    hs_ref[0] = h_all
    _, _, _, xc, _, dt, _, acs = _ssd_pre(xt_ref, xc_ref, cw_ref, cb_ref, dtr_ref, sp_ref, n)
    ck = _SsdChunk(xc, dt, acs, sp_ref[...], e64_ref[...], e128_ref[...])
    ys, hn = [], []
    for g in range(2):
        gl = slice(g * GW, (g + 1) * GW)
        xdt = ck.xdt[g]
        hg = h_all[gl, :]
        y_diag = ck.diag_blocks(_dot(ck.m_st[g].astype(BF16), xdt.astype(BF16)))
        y_off = ck.e_x[:, gl] * _dot_nt(ck.cb16[g], hg.astype(BF16))
        ys.append(y_diag + y_off + ck.xs[g] * ck.dsk_x[:, gl])
        hn.append(hg * ck.elast_rows(g) + _dot_tn((xdt * ck.dte_x[:, gl]).astype(BF16), ck.bb[g]))
    h_scr[...] = jnp.concatenate(hn, axis=0)
    y = jnp.concatenate(ys, axis=1)
    y_ref[...] = y
    yn_ref[...] = _gated_norm_fwd(y, z_ref[...], nw_ref[...]).astype(BF16)


def _mixer_fwd(q, k, v, sinks8, xbc, conv_w8, conv_b, dtr, ssm_p, z, nw, gathers):
    s = q.shape[0]
    nb = s // LB
    prev = lambda n: (jnp.maximum(n - 1, 0), 0)
    cur = lambda n: (n, 0)
    items, ex_shapes, n_g = _exchange_items(gathers, [])
    ne = len(items)

    n_in = 16
    relay_step = (nb - 1) // 2
    e64, e128 = _head_expanders()

    def body(*refs):
        a_in, s_in, ex_in = refs[:6], refs[6:n_in], refs[n_in:n_in + ne]
        o_ref, yn_ref, y_ref, hs_ref = refs[n_in + ne:n_in + 4 + ne]
        ex_out = refs[n_in + 4 + ne:n_in + 4 + 2 * ne]
        h_scr = refs[n_in + 4 + 2 * ne]
        sems = refs[n_in + 5 + 2 * ne:]
        n = pl.program_id(0)

        @pl.when(n == 0)
        def _():
            _Exchange(n_g, ex_in, ex_out, sems).two_level_start()

        _attn_fwd_block(n, *a_in, o_ref)
        _ssd_fwd_block(n, *s_in, yn_ref, y_ref, hs_ref, h_scr)

        @pl.when(n == relay_step)
        def _():
            _Exchange(n_g, ex_in, ex_out, sems).two_level_relay()

        @pl.when(n == nb - 1)
        def _():
            _Exchange(n_g, ex_in, ex_out, sems).two_level_finish()

    any_spec = pl.BlockSpec(memory_space=pl.ANY)
    tail = pl.BlockSpec((SUB, D), lambda n: (jnp.maximum(n * (LB // SUB) - 1, 0), 0))
    outs = _pcall(
        body, name="mixer_fwd", grid=(nb,),
        in_specs=[pl.BlockSpec((LB, AW), cur), pl.BlockSpec((LB, KVW), prev), pl.BlockSpec((LB, KVW), cur),
                  pl.BlockSpec((LB, KVW), prev), pl.BlockSpec((LB, KVW), cur), _const((8, LB)),
                  tail, pl.BlockSpec((LB, D), cur), _const((8, D)), _const((1, D)),
                  pl.BlockSpec((LB, LB), cur), _const((8, LB)), pl.BlockSpec((LB, SW), cur), _const((1, SW)),
                  _const(e64.shape), _const(e128.shape)]
        + [any_spec] * ne,
        out_specs=[pl.BlockSpec((LB, AW), cur), pl.BlockSpec((LB, SW), cur), pl.BlockSpec((LB, SW), cur),
                   pl.BlockSpec((1, NH * HD, NST), lambda n: (n, 0, 0))] + [any_spec] * ne,
        out_shape=[_sds((s, AW), BF16), _sds((s, SW), BF16), _sds((s, SW), F32), _sds((nb, NH * HD, NST), F32)]
        + ex_shapes,
        scratch_shapes=[pltpu.VMEM((NH * HD, NST), F32)] + _exchange_sems(ne),
        compiler_params=_params(),
    )(q, k, k, v, v, sinks8, xbc, xbc, conv_w8, conv_b, dtr, ssm_p, z, nw, e64, e128, *items)
    return outs[0], outs[1], outs[2], outs[3], outs[4:]


def _head_expanders():
    j = lax.broadcasted_iota(jnp.int32, (LB, NH * HD), 0)
    e64 = (lax.broadcasted_iota(jnp.int32, (LB, NH * HD), 1) // HD == j).astype(BF16)
    j = lax.broadcasted_iota(jnp.int32, (LB, NH * LB), 0)
    e128 = (lax.broadcasted_iota(jnp.int32, (LB, NH * LB), 1) // LB == j).astype(BF16)
    return e64, e128


def _outproj_ffn_fwd_loss(attn, yn, x, tgt, mod8, n2w, fnw, w_out, w_gu_t, w_down):
    s = x.shape[0]
    tf = min(256, s)

    def body(a_ref, y_ref, x_ref, t_ref, mod_ref, nw_ref, fw_ref, wo_ref, wgu_ref, wd_ref,
             x2_ref, h2_ref, mo_ref, mix_ref, gu_ref, act_ref, dx3_ref, sm_ref):
        i = pl.program_id(0)

        @pl.when(i == 0)
        def _():
            sm_ref[...] = jnp.zeros_like(sm_ref)

        mix = jnp.concatenate([a_ref[...], y_ref[...]], axis=1)
        mix_ref[...] = mix
        mo = _dot(mix, wo_ref[...])
        mo_ref[...] = mo.astype(BF16)
        x2 = x_ref[...] + mod_ref[2:3, :] * mo
        x2_ref[...] = x2
        h2 = _norm_mod_fwd(x2, nw_ref[...], mod_ref[3:4, :], mod_ref[4:5, :]).astype(BF16)
        h2_ref[...] = h2
        gu = _dot_nt(h2, wgu_ref[...])
        gu_ref[...] = gu.astype(BF16)
        g = gu[:, :DFF]
        act = (g * _sigmoid(g) * gu[:, DFF:]).astype(BF16)
        act_ref[...] = act
        ff = _dot(act, wd_ref[...])
        x3 = x2 + mod_ref[5:6, :] * ff
        r = lax.rsqrt(jnp.mean(x3 * x3, axis=-1, keepdims=True) + EPS)
        xh = x3 * r
        fw = fw_ref[...]
        err = xh * fw - t_ref[...]
        dy = err * (1.0 / D)
        dxh = dy * fw
        dx3 = r * (dxh - xh * jnp.mean(dxh * xh, axis=-1, keepdims=True))
        dx3_ref[...] = dx3
        sm_ref[0:1, :] += jnp.sum(dx3 * ff, axis=0, keepdims=True)
        sm_ref[1:2, :] += jnp.sum(dy * xh, axis=0, keepdims=True)
        sm_ref[2:3, :] += jnp.sum(err * err, axis=0, keepdims=True)

    return _pcall(
        body, name="outproj_ffn_fwd_loss", grid=(s // tf,),
        in_specs=[_rows(tf, AW), _rows(tf, SW), _rows(tf, D), _rows(tf, D), _const((8, D)), _const((1, D)),
                  _const((1, D)), _const((D, D)), _const((2 * DFF, D)), _const((DFF, D))],
        out_specs=[_rows(tf, D), _rows(tf, D), _rows(tf, D), _rows(tf, D), _rows(tf, 2 * DFF), _rows(tf, DFF),
                   _rows(tf, D), pl.BlockSpec((8, D), lambda i: (0, 0))],
        out_shape=[_sds((s, D), F32), _sds((s, D), BF16), _sds((s, D), BF16), _sds((s, D), BF16),
                   _sds((s, 2 * DFF), BF16), _sds((s, DFF), BF16), _sds((s, D), F32), _sds((8, D), F32)],
        compiler_params=_params(),
    )(attn, yn, x, tgt, mod8, n2w, fnw, w_out, w_gu_t, w_down)


def _ffn_bwd(dx3, gu, x2, mixout, mod8, n2w, w_gu, w_down, w_out):
    s = x2.shape[0]
    tb = min(256, s)

    def body(dx3_ref, gu_ref, x2_ref, mo_ref, mod_ref, nw_ref, wgu_ref, wd_ref, wo_ref,
             dx2_ref, dff_ref, dgu_ref, dmix_ref, dattn_ref, dyn_ref, sm_ref):
        i = pl.program_id(0)

        @pl.when(i == 0)
        def _():
            sm_ref[...] = jnp.zeros_like(sm_ref)

        dx3 = dx3_ref[...]
        dff = (dx3 * mod_ref[5:6, :]).astype(BF16)
        dff_ref[...] = dff
        dact = _dot_nt(dff, wd_ref[...])
        g = gu_ref[:, :DFF].astype(F32)
        u = gu_ref[:, DFF:].astype(F32)
        sg = _sigmoid(g)
        dgu = jnp.concatenate([dact * u * sg * (1.0 + g * (1.0 - sg)), dact * g * sg], axis=1).astype(BF16)
        dgu_ref[...] = dgu
        dh2 = _dot(dgu, wgu_ref[...])
        dxn, d_shift, d_scale, d_w = _norm_mod_bwd(x2_ref[...], dh2, nw_ref[...], mod_ref[4:5, :])
        dx2 = dx3 + dxn
        dx2_ref[...] = dx2
        sm_ref[0:1, :] += d_shift
        sm_ref[1:2, :] += d_scale
        sm_ref[2:3, :] += d_w
        sm_ref[3:4, :] += jnp.sum(dx2 * mo_ref[...].astype(F32), axis=0, keepdims=True)
        dmix = (dx2 * mod_ref[2:3, :]).astype(BF16)
        dmix_ref[...] = dmix
        dmi = _dot_nt(dmix, wo_ref[...])
        dattn_ref[...] = dmi[:, :AW].astype(BF16)
        dyn_ref[...] = dmi[:, AW:]

    return _pcall(
        body, name="ffn_bwd", grid=(s // tb,),
        in_specs=[_rows(tb, D), _rows(tb, 2 * DFF), _rows(tb, D), _rows(tb, D), _const((8, D)), _const((1, D)),
                  _const((2 * DFF, D)), _const((DFF, D)), _const((D, D))],
        out_specs=[_rows(tb, D), _rows(tb, D), _rows(tb, 2 * DFF), _rows(tb, D), _rows(tb, AW), _rows(tb, SW),
                   pl.BlockSpec((8, D), lambda i: (0, 0))],
        out_shape=[_sds((s, D), F32), _sds((s, D), BF16), _sds((s, 2 * DFF), BF16), _sds((s, D), BF16),
                   _sds((s, AW), BF16), _sds((s, SW), F32), _sds((8, D), F32)],
        compiler_params=_params(),
    )(dx3, gu, x2, mixout, mod8, n2w, w_gu, w_down, w_out)


def _ssd_bwd_block(i, n, *refs):
    def run(dyn_ref, y_ref, z_ref, xt_ref, xc_ref, cw_ref, cb_ref, dtr_ref, sp_ref, nw_ref, hs_ref, e64_ref, e128_ref,
            dzxd_ref, sm_ref, dh_scr, dun_scr):
        @pl.when(i == 0)
        def _():
            dh_scr[...] = jnp.zeros_like(dh_scr)
            dun_scr[...] = jnp.zeros_like(dun_scr)
            sm_ref[...] = jnp.zeros_like(sm_ref)

        sh, u, sg_u, xc, pre, dt, a_neg, acs = _ssd_pre(xt_ref, xc_ref, cw_ref, cb_ref, dtr_ref, sp_ref, n)
        ck = _SsdChunk(xc, dt, acs, sp_ref[...], e64_ref[...], e128_ref[...])
        h_all = hs_ref[0]
        dh_all = dh_scr[...]
        riota = lax.broadcasted_iota(jnp.int32, (LB, LB), 0)
        lane1 = lax.broadcasted_iota(jnp.int32, (1, LB), 1)

        z = z_ref[...]
        y = y_ref[...]
        sgz = _sigmoid(z)
        sz = z * sgz
        yz = y * sz
        nwv = nw_ref[...]
        dyn_v = dyn_ref[...]
        dyhat = dyn_v * nwv
        yhat_parts, dyz_parts = [], []
        for g in range(2):
            gs = slice(g * 256, (g + 1) * 256)
            t = yz[:, gs]
            rg = lax.rsqrt(jnp.mean(t * t, axis=-1, keepdims=True) + EPS)
            yh = t * rg
            dyh = dyhat[:, gs]
            yhat_parts.append(yh)
            dyz_parts.append(rg * (dyh - yh * jnp.mean(dyh * yh, axis=-1, keepdims=True)))
        yhat = jnp.concatenate(yhat_parts, axis=1)
        dyz = jnp.concatenate(dyz_parts, axis=1)
        sm_ref[5:6, 0:SW] += jnp.sum(dyn_v * yhat, axis=0, keepdims=True)
        dy = dyz * sz
        dzxd_ref[:, 0:SW] = (dyz * y * sgz * (1.0 + z * (1.0 - sgz))).astype(BF16)

        cat = lambda parts: jnp.concatenate(parts, axis=1)
        dxs, dbs, dcs, dhp, g_cat, de_x, ddte_x, ddt_x, ddsk_x = ([] for _ in range(9))
        dacs_t = jnp.zeros((LB, LB), F32)
        hsum = jnp.zeros((1, LB), F32)
        for g in range(2):
            gl = slice(g * GW, (g + 1) * GW)
            xs_g, xdt, bgb, cgb = ck.xs[g], ck.xdt[g], ck.bb[g], ck.cb16[g]
            m_st, dm_st = ck.m_st[g], ck.dm_st[g]
            dt_x, e_x, dte_x = ck.dt_x[:, gl], ck.e_x[:, gl], ck.dte_x[:, gl]
            xdtb = xdt.astype(BF16)
            hg, dhn = h_all[gl, :], dh_all[gl, :]
            hb, dhnb = hg.astype(BF16), dhn.astype(BF16)
            dy_g = dy[:, gl]
            ddsk_x.append(jnp.sum(dy_g * xs_g, axis=0, keepdims=True))
            dy_bd = ck.block_diag(dy_g).astype(BF16)
            dm4 = _dot_nt(dy_bd, xdtb)
            dxdt = _dot_tn(m_st.astype(BF16), dy_bd)
            gmat = dm4 * m_st
            dcbm = dm4 * dm_st
            dcb = dcbm[0:LB] + dcbm[LB:2 * LB] + dcbm[2 * LB:3 * LB] + dcbm[3 * LB:4 * LB]
            g_cat.append(cat([gmat[hh * LB:(hh + 1) * LB, :] for hh in range(HPG)]))
            for hh in range(HPG):
                j = HPG * g + hh
                col_sum = jnp.sum(gmat[hh * LB:(hh + 1) * LB, :], axis=0, keepdims=True)
                dacs_t = dacs_t - jnp.where(riota == j, col_sum, 0.0)
                hsl = slice(hh * HD, (hh + 1) * HD)
                hsum = hsum + jnp.where(lane1 == j, jnp.sum(dhn[hsl, :] * hg[hsl, :]), 0.0)
            dchb = (dy_g * e_x).astype(BF16)
            dcg = _dot(dchb, hb)
            dh_prev = _dot_tn(dchb, cgb)
            de_x.append(dy_g * _dot_nt(cgb, hb))
            dxs_s = _dot_nt(bgb, dhnb)
            dbg = _dot((xdt * dte_x).astype(BF16), dhnb)
            dxdt = dxdt + dxs_s * dte_x
            ddte_x.append(dxs_s * xdt)
            dhp.append(dhn * ck.elast_rows(g) + dh_prev)
            dxs.append(dy_g * ck.dsk_x[:, gl] + dxdt * dt_x)
            ddt_x.append(dxdt * xs_g)
            dcbb = dcb.astype(BF16)
            dbs.append(dbg + _dot_tn(dcbb, cgb))
            dcs.append(dcg + _dot(dcbb, bgb))
        dh_scr[...] = jnp.concatenate(dhp, axis=0)
        red = _dot_nt_sel(jnp.concatenate([cat(de_x), cat(ddte_x), cat(ddt_x)], axis=0), ck.e64, 2)
        de_c, ddte_c, ddt_c = red[0:LB], red[LB:2 * LB], red[2 * LB:3 * LB]
        ddsk = _dot_nt_sel(jnp.broadcast_to(cat(ddsk_x), (SUB, NH * HD)), ck.e64, 2)[0:1, :]
        t1 = ddte_c * ck.dte_all
        dalast = jnp.sum(t1, axis=0, keepdims=True) + hsum * ck.elast
        dacs = (_dot_nt_sel(cat(g_cat), ck.e128, 2) + de_c * ck.e_all - t1 + jnp.transpose(dacs_t)
                + jnp.where(riota == LB - 1, dalast, 0.0))
        da = _cumsum_rows(dacs, True)
        ddt = ddt_c + da * a_neg
        da_log = jnp.sum(da * dt, axis=0, keepdims=True) * a_neg
        ddtr = ddt * _sigmoid(pre)
        dzxd_ref[:, SW + D:ZXD] = ddtr.astype(BF16)
        sm_ref[6:7, 0:LB] += jnp.sum(ddtr, axis=0, keepdims=True)
        sm_ref[6:7, LB:2 * LB] += da_log
        sm_ref[6:7, 2 * LB:3 * LB] += ddsk

        du = cat(dxs + dbs + dcs) * (sg_u * (1.0 + u * (1.0 - sg_u)))
        sm_ref[0:1, :] += jnp.sum(du, axis=0, keepdims=True)
        for k in range(CONVK):
            sm_ref[1 + k:2 + k, :] += jnp.sum(du * sh[CONVK - 1 - k], axis=0, keepdims=True)
        adv = _conv_advances(du, dun_scr[...])
        dxbc = cw_ref[CONVK - 1:CONVK, :] * du
        for j in range(1, CONVK):
            dxbc = dxbc + cw_ref[CONVK - 1 - j:CONVK - j, :] * adv[j - 1]
        dun_scr[...] = du[0:SUB, :]
        dzxd_ref[:, SW:SW + D] = dxbc.astype(BF16)

    run(*refs)


def _attn_bwd_block(i, n, q_ref, kp_ref, kc_ref, vp_ref, vc_ref, o_ref, do_ref, cos_ref, sin_ref, sink_ref,
                    dq_ref, dkv_ref, ds_ref, ck_scr, cv_scr):
    @pl.when(i == 0)
    def _():
        ds_ref[...] = jnp.zeros_like(ds_ref)
        ck_scr[...] = jnp.zeros_like(ck_scr)
        cv_scr[...] = jnp.zeros_like(cv_scr)

    qv, ov, dov, sinks = q_ref[...], o_ref[...], do_ref[...], sink_ref[...]
    kcat = jnp.concatenate([kp_ref[...], kc_ref[...]], axis=0)
    vcat = jnp.concatenate([vp_ref[...], vc_ref[...]], axis=0)
    upper = _upper_mask()
    srow = lax.broadcasted_iota(jnp.int32, (8, LB), 0)
    slane = lax.broadcasted_iota(jnp.int32, (8, LB), 1)
    dsink = jnp.zeros((8, LB), F32)
    dq_g, dk_g, dv_g = [], [], []
    for g in range(NQ // QPG):
        sl = slice(g * HD, (g + 1) * HD)
        qg = _stack_heads(qv, g)
        dog = _stack_heads(dov, g)
        p, es, denom = _attn_softmax(_attn_scores(n, qg, kcat[:, sl], upper), _sink_wide(sinks, g))
        probs, psink = p / denom, es / denom
        delta = _row_sums_wide(dog.astype(F32) * _stack_heads(ov, g).astype(F32), 2)
        dsc = probs * (_band(upper, _dot_nt(dog, vcat[0:LB, sl]), _dot_nt(dog, vcat[LB:2 * LB, sl])) - delta)
        sink_terms = (psink * delta)[:, 0:1]
        for hh in range(QPG):
            dsink = dsink - jnp.where((srow == QPG * g + hh) & (slane == 0),
                                      jnp.sum(sink_terms[hh * LB:(hh + 1) * LB, :]), 0.0)
        ds_p = jnp.where(upper, dsc, 0.0).astype(BF16)
        ds_c = jnp.where(upper, 0.0, dsc).astype(BF16)
        dq_g.append((_dot(ds_p, kcat[0:LB, sl]) + _dot(ds_c, kcat[LB:2 * LB, sl])) * ATT_SCALE)
        dk_g.append(jnp.concatenate([_dot_tn(ds_p, qg), _dot_tn(ds_c, qg)], axis=0) * ATT_SCALE)
        dv_g.append(jnp.concatenate([_dot_tn(jnp.where(upper, probs, 0.0).astype(BF16), dog),
                                     _dot_tn(jnp.where(upper, 0.0, probs).astype(BF16), dog)], axis=0))
    ds_ref[...] += dsink
    cs = cos_ref[...]
    sn = sin_ref[...]
    dk2 = jnp.concatenate(dk_g, axis=1)
    dv2 = jnp.concatenate(dv_g, axis=1)
    for a, tile in enumerate(_unstack_heads(dq_g)):
        dq_ref[:, a * LB:(a + 1) * LB] = _rope(tile, cs, sn, True).astype(BF16)
    dkv_ref[:, 0:KVW] = _rope(ck_scr[...] + dk2[LB:2 * LB, :], cs, sn, True).astype(BF16)
    dkv_ref[:, KVW:2 * KVW] = (cv_scr[...] + dv2[LB:2 * LB, :]).astype(BF16)
    ck_scr[...] = dk2[0:LB, :]
    cv_scr[...] = dv2[0:LB, :]


def _mixer_bwd(q, k, v, o, do, cos, sin, sinks8, dyn, y, z, xbc, conv_w8, conv_b, dtr, ssm_p, nw, hs, scatters):
    s = q.shape[0]
    nb = s // LB
    cur = lambda i: (nb - 1 - i, 0)
    prev = lambda i: (jnp.maximum(nb - 2 - i, 0), 0)
    n_in = 23
    items, ex_shapes, n_g = _exchange_items([], scatters)
    ne = len(items)
    e64, e128 = _head_expanders()

    def body(*refs):
        i = pl.program_id(0)
        n = nb - 1 - i
        a_in, s_in, ex_in = refs[:10], refs[10:n_in], refs[n_in:n_in + ne]
        dp_ref, ds_ref, sm_ref = refs[n_in + ne:n_in + ne + 3]
        dq_ref, dkv_ref, dzxd_ref = dp_ref.at[:, O_Q:O_K], dp_ref.at[:, O_K:O_Z], dp_ref.at[:, O_Z:INP]
        ex_out = refs[n_in + ne + 3:n_in + 2 * ne + 3]
        ck_scr, cv_scr, dh_scr, dun_scr = refs[n_in + 2 * ne + 3:n_in + 2 * ne + 7]
        sems = refs[n_in + 2 * ne + 7:]

        @pl.when(i == 0)
        def _():
            _Exchange(n_g, ex_in, ex_out, sems).start()

        _attn_bwd_block(i, n, *a_in, dq_ref, dkv_ref, ds_ref, ck_scr, cv_scr)
        _ssd_bwd_block(i, n, *s_in, dzxd_ref, sm_ref, dh_scr, dun_scr)

        @pl.when(i == nb - 1)
        def _():
            _Exchange(n_g, ex_in, ex_out, sems).finish()

    any_spec = pl.BlockSpec(memory_space=pl.ANY)
    tail = pl.BlockSpec((SUB, D), lambda i: (jnp.maximum((nb - 1 - i) * (LB // SUB) - 1, 0), 0))
    outs = _pcall(
        body, name="mixer_bwd", grid=(nb,),
        in_specs=[pl.BlockSpec((LB, AW), cur), pl.BlockSpec((LB, KVW), prev), pl.BlockSpec((LB, KVW), cur),
                  pl.BlockSpec((LB, KVW), prev), pl.BlockSpec((LB, KVW), cur), pl.BlockSpec((LB, AW), cur),
                  pl.BlockSpec((LB, AW), cur), pl.BlockSpec((LB, LB), cur), pl.BlockSpec((LB, LB), cur),
                  _const((8, LB)),
                  pl.BlockSpec((LB, SW), cur), pl.BlockSpec((LB, SW), cur), pl.BlockSpec((LB, SW), cur),
                  tail, pl.BlockSpec((LB, D), cur), _const((8, D)), _const((1, D)),
                  pl.BlockSpec((LB, LB), cur), _const((8, LB)), _const((1, SW)),
                  pl.BlockSpec((1, NH * HD, NST), lambda i: (nb - 1 - i, 0, 0)),
                  _const(e64.shape), _const(e128.shape)] + [any_spec] * ne,
        out_specs=[pl.BlockSpec((LB, INP), cur), pl.BlockSpec((8, LB), lambda i: (0, 0)),
                   pl.BlockSpec((8, D), lambda i: (0, 0))] + [any_spec] * ne,
        out_shape=[_sds((s, INP), BF16), _sds((8, LB), F32), _sds((8, D), F32)] + ex_shapes,
        scratch_shapes=[pltpu.VMEM((LB, KVW), F32), pltpu.VMEM((LB, KVW), F32),
                        pltpu.VMEM((NH * HD, NST), F32), pltpu.VMEM((SUB, D), F32)]
        + _exchange_sems(ne),
        compiler_params=_params(),
    )(q, k, k, v, v, o, do, cos, sin, sinks8, dyn, y, z, xbc, xbc, conv_w8, conv_b, dtr, ssm_p, nw, hs, e64, e128,
      *items)
    return outs[0], outs[1], outs[2], outs[3:]


def _inproj_bwd(dproj, x, dx2, mod8, n1w, w_in_t, scatters):
    s = x.shape[0]
    tt = min(512, s)
    nt = s // tt
    items, ex_shapes, n_g = _exchange_items([], scatters)
    ne = len(items)

    def body(*refs):
        dp_ref, x_ref, dx2_ref, mod_ref, nw_ref, w_ref = refs[:6]
        ex_in = refs[6:6 + ne]
        gx_ref, sm_ref = refs[6 + ne:8 + ne]
        ex_out = refs[8 + ne:8 + 2 * ne]
        sems = refs[8 + 2 * ne:]
        i = pl.program_id(0)

        @pl.when(i == 0)
        def _():
            sm_ref[...] = jnp.zeros_like(sm_ref)
            _Exchange(n_g, ex_in, ex_out, sems).start()

        dh1 = _dot(dp_ref[...], w_ref[...])
        dxn, d_shift, d_scale, d_w = _norm_mod_bwd(x_ref[...], dh1, nw_ref[...], mod_ref[1:2, :])
        gx_ref[...] = dx2_ref[...] + dxn
        sm_ref[0:1, :] += d_shift
        sm_ref[1:2, :] += d_scale
        sm_ref[2:3, :] += d_w

        @pl.when(i == nt - 1)
        def _():
            _Exchange(n_g, ex_in, ex_out, sems).finish()

    any_spec = pl.BlockSpec(memory_space=pl.ANY)
    outs = _pcall(
        body, name="inproj_bwd", grid=(nt,),
        in_specs=[_rows(tt, INP), _rows(tt, D), _rows(tt, D), _const((8, D)), _const((1, D)), _const((INP, D))]
        + [any_spec] * ne,
        out_specs=[_rows(tt, D), pl.BlockSpec((8, D), lambda i: (0, 0))] + [any_spec] * ne,
        out_shape=[_sds((s, D), F32), _sds((8, D), F32)] + ex_shapes,
        scratch_shapes=_exchange_sems(ne),
        compiler_params=_params(),
    )(dproj, x, dx2, mod8, n1w, w_in_t, *items)
    return outs[0], outs[1], outs[2:]


def _wgrad(a, b, name):
    s, m = a.shape
    n = b.shape[1]
    tk = min(2048, s)
    wide = (1408, 1024, 512)
    tm = next((t for t in wide if m % t == 0), m)
    tn = n if n <= 2048 else _largest_divisor(n, wide)
    nk = s // tk

    def body(a_ref, b_ref, o_ref, acc):
        kk = pl.program_id(2)

        @pl.when(kk == 0)
        def _():
            acc[...] = jnp.zeros_like(acc)

        acc[...] += _dot_tn(a_ref[...], b_ref[...])

        @pl.when(kk == nk - 1)
        def _():
            o_ref[...] = acc[...].astype(BF16)

    return _pcall(
        body, name=name, grid=(m // tm, n // tn, nk),
        in_specs=[pl.BlockSpec((tk, tm), lambda i, j, kk: (kk, i)), pl.BlockSpec((tk, tn), lambda i, j, kk: (kk, j))],
        out_specs=pl.BlockSpec((tm, tn), lambda i, j, kk: (i, j)),
        out_shape=_sds((m, n), BF16),
        scratch_shapes=[pltpu.VMEM((tm, tn), F32)],
        compiler_params=_params(3),
    )(a, b)


PACK_ROWS = 24


def _pack_small(sm_f, sm_b, sm_s, sm_i, dsink):
    def body(f_ref, b_ref, s_ref, i_ref, k_ref, o_ref):
        o_ref[...] = jnp.zeros_like(o_ref)
        o_ref[0:2, :] = i_ref[0:2, :]
        o_ref[2:3, :] = b_ref[3:4, :]
        o_ref[3:5, :] = b_ref[0:2, :]
        o_ref[5:6, :] = f_ref[0:1, :]
        o_ref[6:7, :] = i_ref[2:3, :]
        o_ref[7:8, :] = b_ref[2:3, :]
        o_ref[8:9, :] = f_ref[1:2, :]
        o_ref[9:14, :] = s_ref[0:5, :]
        o_ref[14:15, :] = s_ref[5:6, :]
        o_ref[15:16, 0:3 * LB] = s_ref[6:7, 0:3 * LB]
        lane = lax.broadcasted_iota(jnp.int32, (1, LB), 1)
        sk = jnp.zeros((1, LB), F32)
        for h in range(NQ):
            sk = sk + jnp.where(lane == h, k_ref[h:h + 1, 0:1], 0.0)
        o_ref[15:16, 3 * LB:4 * LB] = sk
        o_ref[16:17, :] = f_ref[2:3, :]

    return _pcall(body, name="pack_small", out_shape=_sds((PACK_ROWS, D), F32))(sm_f, sm_b, sm_s, sm_i, dsink)


def _exchange(gathers, scatters, name, two_level=False):
    items, shapes, n_g = _exchange_items(gathers, scatters)
    n = len(items)
    assert not (two_level and scatters)

    def body(*refs):
        ex = _Exchange(n_g, refs[:n], refs[n:2 * n], refs[2 * n:])
        if two_level:
            ex.gather_two_level()
        else:
            ex.start()
            ex.finish()

    any_spec = pl.BlockSpec(memory_space=pl.ANY)
    return _pcall(
        body, name=name, in_specs=[any_spec] * n, out_specs=[any_spec] * n, out_shape=shapes,
        scratch_shapes=_exchange_sems(n),
    )(*items)


def _exchange_items(gathers, scatters):
    items = list(gathers) + list(scatters)
    shapes = [_sds((N_DEV,) + a.shape, a.dtype) for a in gathers] + [_sds(a.shape, a.dtype) for a in scatters]
    return items, shapes, len(gathers)


def _exchange_sems(n):
    return [pltpu.SemaphoreType.DMA((n, N_DEV - 1)), pltpu.SemaphoreType.DMA((n, N_DEV - 1)),
            pltpu.SemaphoreType.DMA((n,))]


class _Exchange:
    def __init__(self, n_g, ins, outs, sems):
        self.n_g, self.ins, self.outs = n_g, ins, outs
        self.send_sems, self.recv_sems, self.loc_sems = sems
        xi, yi, ci = lax.axis_index("x"), lax.axis_index("y"), lax.axis_index("c")
        self.me = 4 * xi + 2 * yi + ci
        self.peers = []
        for r in range(1, N_DEV):
            px = 1 - xi if r & 4 else xi
            py = 1 - yi if r & 2 else yi
            pc = 1 - ci if r & 1 else ci
            self.peers.append(((px, py, pc), 4 * px + 2 * py + pc))

    def _copy(self, t, r, landing):
        dev, peer = self.peers[r]
        src = self.ins[t] if t < self.n_g else self.ins[t].at[peer]
        return pltpu.make_async_remote_copy(
            src_ref=src, dst_ref=self.outs[t].at[landing], send_sem=self.send_sems.at[t, r],
            recv_sem=self.recv_sems.at[t, r], device_id=dev, device_id_type=pl.DeviceIdType.MESH)

    def _local(self, t):
        src = self.ins[t] if t < self.n_g else self.ins[t].at[self.me]
        return pltpu.make_async_copy(src, self.outs[t].at[self.me], self.loc_sems.at[t])

    def start(self):
        for t in range(len(self.ins)):
            self._local(t).start()
            for r in range(N_DEV - 1):
                self._copy(t, r, self.me).start()

    def finish(self):
        n = len(self.ins)
        for t in range(n):
            for r in range(N_DEV - 1):
                self._copy(t, r, self.peers[r][1]).wait_recv()
        for t in range(n):
            for r in range(N_DEV - 1):
                self._copy(t, r, self.me).wait_send()
            self._local(t).wait()

    def gather_two_level(self):
        self.two_level_start()
        self.two_level_relay()
        self.two_level_finish()

    DIRECT = (0, 1, 3, 5)

    def two_level_start(self):
        for t in range(len(self.ins)):
            self._local(t).start()
            for r in self.DIRECT:
                self._copy(t, r, self.me).start()

    def _relay(self, t, r):
        peer = self.peers[r][1]
        return pltpu.make_async_remote_copy(
            src_ref=self.outs[t].at[peer], dst_ref=self.outs[t].at[peer], send_sem=self.send_sems.at[t, r + 1],
            recv_sem=self.recv_sems.at[t, r + 1], device_id=self.peers[0][0], device_id_type=pl.DeviceIdType.MESH)

    def two_level_relay(self):
        for t in range(len(self.ins)):
            for r in self.DIRECT[1:]:
                self._copy(t, r, self.peers[r][1]).wait_recv()
                self._relay(t, r).start()

    def two_level_finish(self):
        n = len(self.ins)
        for t in range(n):
            for r in (0, 2, 4, 6):
                self._copy(t, r, self.peers[r][1]).wait_recv()
        for t in range(n):
            for r in self.DIRECT:
                self._copy(t, r, self.me).wait_send()
            for r in self.DIRECT[1:]:
                self._relay(t, r).wait_send()
            self._local(t).wait()


def _ada_fwd(c_all, w_cols, b_cols):
    def body(c_ref, w_ref, b_ref, o_ref):
        cv = c_ref[...]
        sc = (cv * _sigmoid(cv)).astype(BF16)
        o_ref[...] = _dot(sc, w_ref[...].astype(BF16)) + b_ref[...]

    return _pcall(body, name="ada_fwd", out_shape=_sds((N_DEV, w_cols.shape[1]), F32),
                  compiler_params=_params(0))(c_all, w_cols, b_cols)


def _adamw(w, g, m, v):
    m2 = ADAM_B1 * m + (1.0 - ADAM_B1) * g
    v2 = ADAM_B2 * v + (1.0 - ADAM_B2) * (g * g)
    m_hat = m2 / (1.0 - ADAM_B1 ** ADAM_STEP)
    v_hat = v2 / (1.0 - ADAM_B2 ** ADAM_STEP)
    delta = -ADAM_LR * (m_hat / (jnp.sqrt(v_hat) + ADAM_EPS) + ADAM_WD * w)
    return delta, m2, v2


def _sum_adamw(parts, w, m, v, name):
    rws, cols = w.shape
    tr = next((t for t in (256, 176, 128) if rws % t == 0), rws)

    def body(p_ref, w_ref, m_ref, v_ref, g_ref, d_ref, mo_ref, vo_ref):
        g = p_ref[0].astype(F32)
        for dev in range(1, N_DEV):
            g = g + p_ref[dev].astype(F32)
        g_ref[...] = g
        d_ref[...], mo_ref[...], vo_ref[...] = _adamw(w_ref[...], g, m_ref[...], v_ref[...])

    blk = pl.BlockSpec((tr, cols), lambda i: (i, 0))
    return _pcall(
        body, name=name, grid=(rws // tr,),
        in_specs=[pl.BlockSpec((N_DEV, tr, cols), lambda i: (0, i, 0)), blk, blk, blk],
        out_specs=[blk] * 4, out_shape=[_sds((rws, cols), F32)] * 4, compiler_params=_params(),
    )(parts, w, m, v)


def _wada_adamw(c_all, dmod_cols, w, m, v):
    rws, cols = w.shape
    tr = 256

    def body(c_ref, dm_ref, w_ref, m_ref, v_ref, g_ref, d_ref, mo_ref, vo_ref):
        cv = c_ref[...]
        sc = (cv * _sigmoid(cv)).astype(BF16)
        g = _dot_tn(sc, dm_ref[...].astype(BF16))
        g_ref[...] = g
        d_ref[...], mo_ref[...], vo_ref[...] = _adamw(w_ref[...], g, m_ref[...], v_ref[...])

    blk = pl.BlockSpec((tr, cols), lambda i: (i, 0))
    return _pcall(
        body, name="wada_adamw", grid=(rws // tr,),
        in_specs=[pl.BlockSpec((N_DEV, tr), lambda i: (0, i)), pl.BlockSpec((N_DEV, cols), lambda i: (0, 0)),
                  blk, blk, blk],
        out_specs=[blk] * 4, out_shape=[_sds((rws, cols), F32)] * 4, compiler_params=_params(),
    )(c_all, dmod_cols, w, m, v)


def _small_reduce(packs):
    def body(p_ref, o_ref):
        tot = p_ref[0]
        for dev in range(1, N_DEV):
            tot = tot + p_ref[dev]
        o_ref[...] = tot
        o_ref[16:17, :] = jnp.zeros((1, D), F32) + (0.5 / D) * jnp.sum(tot[16:17, :])

    return _pcall(body, name="small_reduce", out_shape=_sds((PACK_ROWS, D), F32))(packs)


def _adamw_many(ws, gs, ms, vs):
    k = len(ws)

    def body(*refs):
        for i in range(k):
            w_ref, g_ref, m_ref, v_ref = refs[i], refs[k + i], refs[2 * k + i], refs[3 * k + i]
            d_ref, mo_ref, vo_ref = refs[4 * k + i], refs[5 * k + i], refs[6 * k + i]
            d_ref[...], mo_ref[...], vo_ref[...] = _adamw(w_ref[...], g_ref[...], m_ref[...], v_ref[...])

    shp = [_sds(w.shape, F32) for w in ws]
    outs = _pcall(body, name="adamw_small", out_shape=shp * 3)(*ws, *gs, *ms, *vs)
    return outs[:k], outs[k:2 * k], outs[2 * k:]


def kernel(x, c, positions, w_ada, b_ada, norm1_w, w_in, conv_w, conv_b, dt_bias, a_log, d_skip, attn_sinks, ssm_norm_w, w_out, norm2_w, w_gate_up, w_down, final_norm_w, loss_target, m_w_ada, m_b_ada, m_norm1_w, m_w_in, m_conv_w, m_conv_b, m_dt_bias, m_a_log, m_d_skip, m_attn_sinks, m_ssm_norm_w, m_w_out, m_norm2_w, m_w_gate_up, m_w_down, m_final_norm_w, v_w_ada, v_b_ada, v_norm1_w, v_w_in, v_conv_w, v_conv_b, v_dt_bias, v_a_log, v_d_skip, v_attn_sinks, v_ssm_norm_w, v_w_out, v_norm2_w, v_w_gate_up, v_w_down, v_final_norm_w):
    s = x.shape[1]
    me = 4 * lax.axis_index("x") + 2 * lax.axis_index("y") + lax.axis_index("c")
    ada_cols = N_MOD * D // N_DEV

    c8 = jnp.pad(c, ((0, 7), (0, 0)))
    cw8 = jnp.pad(conv_w[0], ((0, 8 - CONVK), (0, 0)))
    w_in_t, m_w_in_t, v_w_in_t = jnp.transpose(w_in[0]), jnp.transpose(m_w_in[0]), jnp.transpose(v_w_in[0])
    w_gu_t, m_w_gu_t, v_w_gu_t = (jnp.transpose(w_gate_up[0]), jnp.transpose(m_w_gate_up[0]),
                                  jnp.transpose(v_w_gate_up[0]))
    g_c, g_in, g_cw = _exchange([c8, w_in_t.astype(BF16), cw8], [], "gather_in", two_level=True)
    c_all = g_c[:, 0, :]
    w_in_f = jnp.pad(g_in.reshape(IN_PROJ, D), ((0, INP - IN_PROJ), (0, 0)))
    conv_w8 = jnp.transpose(g_cw, (1, 0, 2)).reshape(8, D)

    b_cols = lax.dynamic_slice(b_ada, (0, me * ada_cols), (1, ada_cols))
    (g_mod,) = _exchange([_ada_fwd(c_all, w_ada[0], b_cols)], [], "gather_mod")
    mod = lax.dynamic_index_in_dim(g_mod, me, axis=1, keepdims=False).reshape(N_MOD, D)
    mod8 = jnp.pad(mod, ((0, 8 - N_MOD), (0, 0)))

    half = HD // 2
    inv_freq = ROPE_THETA ** (-jnp.arange(half, dtype=F32) / half)
    invf = jnp.tile(inv_freq, LB // half).reshape(1, LB)
    lanes = lambda a: jnp.pad(a, ((0, 0), (0, LB - a.shape[1])))
    ssm_p = jnp.pad(jnp.concatenate([lanes(dt_bias), lanes(a_log), lanes(d_skip)], axis=0), ((0, 5), (0, 0)))
    sinks8 = jnp.broadcast_to(attn_sinks.reshape(NQ, 1), (NQ, LB))

    xs, tgt, fnw = x[0], loss_target[0], final_norm_w.reshape(1, D)

    q, k, v, z, xbc, dtr, h1, cos, sin = _inproj_fwd(xs, positions[0].reshape(s, 1), invf, mod8, norm1_w, w_in_f)
    attn, yn, y, hs, (g_out, g_gu, g_down) = _mixer_fwd(
        q, k, v, sinks8, xbc, conv_w8, conv_b, dtr, ssm_p, z, ssm_norm_w,
        [w_out[0].astype(BF16), w_gu_t.astype(BF16), w_down[0].astype(BF16)])
    w_out_f = g_out.reshape(D, D)
    w_gu_f = g_gu.reshape(2 * DFF, D)
    w_down_f = g_down.reshape(DFF, D)
    x2, h2, mo, mix, gu, act, dx3, sm_f = _outproj_ffn_fwd_loss(attn, yn, xs, tgt, mod8, norm2_w, fnw, w_out_f, w_gu_f,
                                                                 w_down_f)

    dx2, dff, dgu, dmix, dattn, dyn, sm_b = _ffn_bwd(dx3, gu, x2, mo, mod8, norm2_w, w_gu_f, w_down_f, w_out_f)
    p_gu = _wgrad(dgu, h2, "wgrad_gate_up").reshape(N_DEV, 2 * DFF // N_DEV, D)
    p_down = _wgrad(act, dff, "wgrad_down").reshape(N_DEV, DFF // N_DEV, D)
    p_out = _wgrad(mix, dmix, "wgrad_out").reshape(N_DEV, D // N_DEV, D)
    dproj, dsink, sm_s, (r_gu, r_down, r_out) = _mixer_bwd(
        q, k, v, attn, dattn, cos, sin, sinks8, dyn, y, z, xbc, conv_w8, conv_b, dtr, ssm_p, ssm_norm_w, hs,
        [p_gu, p_down, p_out])
    p_in = _wgrad(dproj, h1, "wgrad_in")[:IN_PROJ].reshape(N_DEV, IN_PROJ // N_DEV, D)
    gx, sm_i, (r_in,) = _inproj_bwd(dproj, xs, dx2, mod8, norm1_w, w_in_f, [p_in])
    (g_pack,) = _exchange([_pack_small(sm_f, sm_b, sm_s, sm_i, dsink)], [], "gather_small")

    tot = _small_reduce(g_pack)
    loss = tot[16, 0]
    dmod_all = g_pack[:, 0:N_MOD, :].reshape(N_DEV, N_MOD * D)
    dmod_cols = lax.dynamic_slice(dmod_all, (0, me * ada_cols), (N_DEV, ada_cols))

    big = {
        "w_ada": _wada_adamw(c_all, dmod_cols, w_ada[0], m_w_ada[0], v_w_ada[0]),
        "w_in": [jnp.transpose(t) for t in _sum_adamw(r_in, w_in_t, m_w_in_t, v_w_in_t, "adamw_in")],
        "w_out": _sum_adamw(r_out, w_out[0], m_w_out[0], v_w_out[0], "adamw_out"),
        "w_gate_up": [jnp.transpose(t) for t in _sum_adamw(r_gu, w_gu_t, m_w_gu_t, v_w_gu_t, "adamw_gate_up")],
        "w_down": _sum_adamw(r_down, w_down[0], m_w_down[0], v_w_down[0], "adamw_down"),
    }
    small_names = ["b_ada", "norm1_w", "conv_w", "conv_b", "dt_bias", "a_log", "d_skip", "attn_sinks", "ssm_norm_w",
                   "norm2_w", "final_norm_w"]
    row15 = tot[15:16, :]
    small_g = {
        "b_ada": tot[0:N_MOD, :].reshape(1, N_MOD * D),
        "norm1_w": tot[6:7, :],
        "conv_w": lax.dynamic_slice(tot[10:14, :], (0, me * (D // N_DEV)), (CONVK, D // N_DEV)),
        "conv_b": tot[9:10, :],
        "dt_bias": row15[:, 0:NH],
        "a_log": row15[:, LB:LB + NH],
        "d_skip": row15[:, 2 * LB:2 * LB + NH],
        "attn_sinks": row15[:, 3 * LB:3 * LB + NQ],
        "ssm_norm_w": tot[14:15, 0:SW],
        "norm2_w": tot[7:8, :],
        "final_norm_w": tot[8:9, :],
    }
    small_w = {"b_ada": b_ada, "norm1_w": norm1_w, "conv_w": conv_w[0], "conv_b": conv_b, "dt_bias": dt_bias,
               "a_log": a_log, "d_skip": d_skip, "attn_sinks": attn_sinks, "ssm_norm_w": ssm_norm_w,
               "norm2_w": norm2_w, "final_norm_w": final_norm_w.reshape(1, D)}
    small_m = {"b_ada": m_b_ada, "norm1_w": m_norm1_w, "conv_w": m_conv_w[0], "conv_b": m_conv_b,
               "dt_bias": m_dt_bias, "a_log": m_a_log, "d_skip": m_d_skip, "attn_sinks": m_attn_sinks,
               "ssm_norm_w": m_ssm_norm_w, "norm2_w": m_norm2_w, "final_norm_w": m_final_norm_w.reshape(1, D)}
    small_v = {"b_ada": v_b_ada, "norm1_w": v_norm1_w, "conv_w": v_conv_w[0], "conv_b": v_conv_b,
               "dt_bias": v_dt_bias, "a_log": v_a_log, "d_skip": v_d_skip, "attn_sinks": v_attn_sinks,
               "ssm_norm_w": v_ssm_norm_w, "norm2_w": v_norm2_w, "final_norm_w": v_final_norm_w.reshape(1, D)}
    s_d, s_m, s_v = _adamw_many([small_w[k] for k in small_names], [small_g[k] for k in small_names],
                                [small_m[k] for k in small_names], [small_v[k] for k in small_names])

    order = ["w_ada", "b_ada", "norm1_w", "w_in", "conv_w", "conv_b", "dt_bias", "a_log", "d_skip", "attn_sinks",
             "ssm_norm_w", "w_out", "norm2_w", "w_gate_up", "w_down", "final_norm_w"]
    lead = {"w_ada", "w_in", "conv_w", "w_out", "w_gate_up", "w_down"}
    grads, deltas, new_m, new_v = [], [], [], []
    for name in order:
        if name in big:
            g, d, m2, v2 = big[name]
        else:
            i = small_names.index(name)
            g, d, m2, v2 = small_g[name], s_d[i], s_m[i], s_v[i]
        if name in lead:
            g, d, m2, v2 = g[None], d[None], m2[None], v2[None]
        if name == "final_norm_w":
            g, d, m2, v2 = g.reshape(D), d.reshape(D), m2.reshape(D), v2.reshape(D)
        grads.append(g)
        deltas.append(d)
        new_m.append(m2)
        new_v.append(v2)
    return (loss, gx[None], *grads, *deltas, *new_m, *new_v)
```

```python
import functools
import math

import jax
import jax.numpy as jnp
from jax import lax
from jax.experimental import pallas as pl
from jax.experimental.pallas import tpu as pltpu

F32 = jnp.float32
BF16 = jnp.bfloat16

N_DEV = 8
D = 1024
HD = 64
NQ = 8
AW = 512
KVW = 128
SW = 512
NST = 128
NH = 8
LB = 128
CONVK = 4
DFF = 2816
N_MOD = 6
IN_PROJ = 2312
INP = 2432
O_Q, O_K, O_V, O_Z, O_XBC, O_DT = 0, 512, 640, 768, 1280, 2304
ZXD = INP - O_Z
EPS = 1e-6
NEG = -1e30
ROPE_THETA = 10000.0
VMEM_LIMIT = 56 * 1024 * 1024

ADAM_LR = 0.001
ADAM_B1 = 0.9
ADAM_B2 = 0.999
ADAM_EPS = 1e-08
ADAM_WD = 0.01
ADAM_STEP = 10

NT_DIMS = (((1,), (1,)), ((), ()))
TN_DIMS = (((0,), (0,)), ((), ()))


def _pcall(body, **kw):
    return pl.pallas_call(body, **kw)


def _sds(shape, dtype):
    return jax.ShapeDtypeStruct(shape, dtype)


def _params(n_grid=1):
    return pltpu.CompilerParams(dimension_semantics=("arbitrary",) * n_grid, vmem_limit_bytes=VMEM_LIMIT)


def _const(shape):
    return pl.BlockSpec(shape, lambda *_: (0,) * len(shape), pipeline_mode=pl.Buffered(1))


def _largest_divisor(n, candidates):
    for cand in candidates:
        if n % cand == 0:
            return cand
    raise ValueError(f"no tile in {candidates} divides {n}")


def _rows(t, w):
    return pl.BlockSpec((t, w), lambda i: (i, 0))


def _dot(a, b):
    return jnp.dot(a, b, preferred_element_type=F32)


def _dot_nt(a, b):
    return lax.dot_general(a, b, NT_DIMS, preferred_element_type=F32)


def _dot_tn(a, b):
    return lax.dot_general(a, b, TN_DIMS, preferred_element_type=F32)


def _sigmoid(v):
    return 1.0 / (1.0 + jnp.exp(-v))


def _softplus(v):
    return jnp.maximum(v, 0.0) + jnp.log1p(jnp.exp(-jnp.abs(v)))


def _rope_sign_mask(shape):
    lane = lax.broadcasted_iota(jnp.int32, shape, 1)
    return (lane % HD) < (HD // 2)


def _rope(t, cs, sn, inverse):
    r_dn = pltpu.roll(t, HD // 2, 1)
    r_up = pltpu.roll(t, LB - HD // 2, 1)
    first = _rope_sign_mask(t.shape)
    if inverse:
        rot = jnp.where(first, r_up, -r_dn)
    else:
        rot = jnp.where(first, -r_up, r_dn)
    return t * cs + rot * sn


def _norm_mod_fwd(xv, nw, shift, scale):
    r = lax.rsqrt(jnp.mean(xv * xv, axis=-1, keepdims=True) + EPS)
    xh = xv * r
    return (xh * nw) * (1.0 + scale) + shift


def _norm_mod_bwd(xv, dh, nw, scale):
    r = lax.rsqrt(jnp.mean(xv * xv, axis=-1, keepdims=True) + EPS)
    xh = xv * r
    xn = xh * nw
    d_shift = jnp.sum(dh, axis=0, keepdims=True)
    d_scale = jnp.sum(dh * xn, axis=0, keepdims=True)
    dxn = dh * (1.0 + scale)
    d_w = jnp.sum(dxn * xh, axis=0, keepdims=True)
    dxh = dxn * nw
    dx = r * (dxh - xh * jnp.mean(dxh * xh, axis=-1, keepdims=True))
    return dx, d_shift, d_scale, d_w


def _inproj_fwd(x, pos, invf, mod8, n1w, w_in):
    s = x.shape[0]
    tt = min(512, s)

    def body(x_ref, pos_ref, invf_ref, mod_ref, nw_ref, w_ref,
             q_ref, k_ref, v_ref, z_ref, xbc_ref, dtr_ref, h1_ref, cos_ref, sin_ref):
        h = _norm_mod_fwd(x_ref[...], nw_ref[...], mod_ref[0:1, :], mod_ref[1:2, :])
        hb = h.astype(BF16)
        h1_ref[...] = hb
        proj = _dot_nt(hb, w_ref[...])
        ang = pos_ref[...].astype(F32) * invf_ref[...]
        cs = jnp.cos(ang)
        sn = jnp.sin(ang)
        cos_ref[...] = cs
        sin_ref[...] = sn
        for a in range(AW // LB):
            q_ref[:, a * LB:(a + 1) * LB] = _rope(proj[:, O_Q + a * LB:O_Q + (a + 1) * LB], cs, sn, False).astype(BF16)
        k_ref[...] = _rope(proj[:, O_K:O_V], cs, sn, False).astype(BF16)
        v_ref[...] = proj[:, O_V:O_Z].astype(BF16)
        z_ref[...] = proj[:, O_Z:O_XBC]
        xbc_ref[...] = proj[:, O_XBC:O_DT]
        dtr_ref[...] = proj[:, O_DT:INP]

    return _pcall(
        body, name="inproj_fwd", grid=(s // tt,),
        in_specs=[_rows(tt, D), _rows(tt, 1), _const((1, LB)), _const((8, D)), _const((1, D)), _const((INP, D))],
        out_specs=[_rows(tt, AW), _rows(tt, KVW), _rows(tt, KVW), _rows(tt, SW), _rows(tt, D), _rows(tt, LB),
                   _rows(tt, D), _rows(tt, LB), _rows(tt, LB)],
        out_shape=[_sds((s, AW), BF16), _sds((s, KVW), BF16), _sds((s, KVW), BF16), _sds((s, SW), F32),
                   _sds((s, D), F32), _sds((s, LB), F32), _sds((s, D), BF16), _sds((s, LB), F32), _sds((s, LB), F32)],
        compiler_params=_params(),
    )(x, pos, invf, mod8, n1w, w_in)


QPG = 4
ATT_SCALE = 1.0 / math.sqrt(HD)


def _stack_heads(val, g):
    return jnp.concatenate([val[:, (QPG * g + hh) * HD:(QPG * g + hh + 1) * HD] for hh in range(QPG)], axis=0)


def _unstack_heads(groups):
    pieces = [grp[hh * LB:(hh + 1) * LB, :] for grp in groups for hh in range(QPG)]
    return [jnp.concatenate(pieces[2 * a:2 * a + 2], axis=1) for a in range(NQ // 2)]


def _upper_mask():
    row = lax.broadcasted_iota(jnp.int32, (QPG * LB, LB), 0)
    col = lax.broadcasted_iota(jnp.int32, (QPG * LB, LB), 1)
    return col > (row % LB)


def _sink_wide(sinks, g):
    return jnp.concatenate([jnp.broadcast_to(sinks[QPG * g + hh:QPG * g + hh + 1, 0:1], (LB, LB))
                            for hh in range(QPG)], axis=0)


def _row_sums_wide(v, terms):
    return _dot_sel(v, jnp.ones((v.shape[1], LB), BF16), terms)


def _band(upper, prev_part, cur_part):
    return jnp.where(upper, prev_part, cur_part)


def _attn_scores(n, qg, kcat, upper):
    sp = _dot_nt(qg, kcat[0:LB, :]) * ATT_SCALE
    sc = _dot_nt(qg, kcat[LB:2 * LB, :]) * ATT_SCALE
    return _band(upper, jnp.where(n > 0, sp, NEG), sc)


def _attn_softmax(comb, sink):
    m = jnp.maximum(jnp.max(comb, axis=-1, keepdims=True), sink)
    p = jnp.exp(comb - m)
    es = jnp.exp(sink - m)
    return p, es, _row_sums_wide(p, 1) + es


def _attn_fwd_block(n, q_ref, kp_ref, kc_ref, vp_ref, vc_ref, sink_ref, o_ref):
    qv = q_ref[...]
    kcat = jnp.concatenate([kp_ref[...], kc_ref[...]], axis=0)
    vcat = jnp.concatenate([vp_ref[...], vc_ref[...]], axis=0)
    sinks = sink_ref[...]
    upper = _upper_mask()
    outs = []
    for g in range(NQ // QPG):
        sl = slice(g * HD, (g + 1) * HD)
        p, _, denom = _attn_softmax(_attn_scores(n, _stack_heads(qv, g), kcat[:, sl], upper), _sink_wide(sinks, g))
        outs.append((_dot(jnp.where(upper, p, 0.0).astype(BF16), vcat[0:LB, sl])
                     + _dot(jnp.where(upper, 0.0, p).astype(BF16), vcat[LB:2 * LB, sl])) / denom[:, 0:HD])
    for g, grp in enumerate(outs):
        for hh in range(QPG):
            h = QPG * g + hh
            o_ref[:, h * HD:(h + 1) * HD] = grp[hh * LB:(hh + 1) * LB, :].astype(BF16)


def _cumsum_rows(a, reverse):
    row = lax.broadcasted_iota(jnp.int32, a.shape, 0)
    step = 1
    while step < LB:
        if reverse:
            a = a + jnp.where(row < LB - step, pltpu.roll(a, LB - step, 0), 0.0)
        else:
            a = a + jnp.where(row >= step, pltpu.roll(a, step, 0), 0.0)
        step *= 2
    return a


SUB = 8


def _conv_shifts(tail, cur):
    row = lax.broadcasted_iota(jnp.int32, tail.shape, 0)
    out = [cur]
    for j in range(1, CONVK):
        rolled = pltpu.roll(cur, j, 0)
        top = jnp.where(row < j, pltpu.roll(tail, j, 0), rolled[0:SUB, :])
        out.append(jnp.concatenate([top, rolled[SUB:, :]], axis=0))
    return out


def _conv_advances(du, head):
    row = lax.broadcasted_iota(jnp.int32, head.shape, 0)
    out = []
    for j in range(1, CONVK):
        rolled = pltpu.roll(du, LB - j, 0)
        bottom = jnp.where(row >= SUB - j, pltpu.roll(head, SUB - j, 0), rolled[LB - SUB:, :])
        out.append(jnp.concatenate([rolled[:LB - SUB, :], bottom], axis=0))
    return out


def _split(v, terms):
    out = []
    for _ in range(terms - 1):
        t = v.astype(BF16)
        out.append(t)
        v = v - t.astype(F32)
    out.append(v.astype(BF16))
    return out


def _dot_sel(v, sel, terms):
    parts = [_dot(t, sel) for t in _split(v, terms)]
    return functools.reduce(lambda a, b: a + b, parts)


def _dot_nt_sel(v, sel, terms):
    parts = [_dot_nt(t, sel) for t in _split(v, terms)]
    return functools.reduce(lambda a, b: a + b, parts)


def _ssd_pre(xt_ref, xc_ref, cw_ref, cb_ref, dtr_ref, sp_ref, n):
    cur = xc_ref[...]
    tail = jnp.where(n > 0, xt_ref[...], 0.0)
    sh = _conv_shifts(tail, cur)
    u = cb_ref[...] + cw_ref[CONVK - 1:CONVK, :] * sh[0]
    for j in range(1, CONVK):
        u = u + cw_ref[CONVK - 1 - j:CONVK - j, :] * sh[j]
    dt = _softplus(dtr_ref[...] + sp_ref[0:1, :])
    acs = _cumsum_rows(dt * -jnp.exp(sp_ref[1:2, :]), False)
    return u, dt, acs


def _gated_norm_fwd(y, z, nw):
    sz = z * _sigmoid(z)
    yz = y * sz
    parts = []
    for g in range(2):
        t = yz[:, g * 256:(g + 1) * 256]
        parts.append(t * lax.rsqrt(jnp.mean(t * t, axis=-1, keepdims=True) + EPS))
    return jnp.concatenate(parts, axis=1) * nw


HPG = 4
GW = HPG * HD


class _SsdChunk:
    def __init__(self, xc, dt, acs, spv, e64, e128, decay=None):
        self.e64, self.e128 = e64, e128
        alast = acs[LB - 1:LB, :]
        self.e_all = jnp.exp(acs)
        self.dte_all = jnp.exp(alast - acs)
        self.elast = jnp.exp(alast)
        wide = _dot_sel(jnp.concatenate([dt, self.e_all, self.dte_all], axis=0), e64, 2)
        self.dt_x, self.e_x, self.dte_x = wide[0:LB], wide[LB:2 * LB], wide[2 * LB:3 * LB]
        self.dsk_x = _dot_sel(spv, e64, 3)[2:3, :]
        if decay is None:
            acs_t = jnp.transpose(acs)
            ac_x = _dot_sel(acs, e128, 3)
            row = lax.broadcasted_iota(jnp.int32, (HPG * LB, LB), 0)
            col = lax.broadcasted_iota(jnp.int32, (HPG * LB, LB), 1)
            causal = (row % LB) >= col
        lane = lax.broadcasted_iota(jnp.int32, (LB, GW), 1)
        self.head_lanes = [(lane >= hh * HD) & (lane < (hh + 1) * HD) for hh in range(HPG)]
        self.xs, self.xdt, self.b, self.c, self.bb, self.cb16, self.cbm, self.dm_st, self.m_st = ([] for _ in range(9))
        for g in range(2):
            heads = range(HPG * g, HPG * (g + 1))
            if decay is None:
                ac_st = jnp.concatenate([ac_x[:, j * LB:(j + 1) * LB] for j in heads], axis=0)
                ar_st = jnp.concatenate([jnp.broadcast_to(acs_t[j:j + 1, :], (LB, LB)) for j in heads], axis=0)
                dm_st = jnp.exp(jnp.where(causal, ac_st - ar_st, NEG))
            else:
                dm_st = decay[g]
            bg = xc[:, SW + g * NST:SW + (g + 1) * NST]
            cg = xc[:, SW + 2 * NST + g * NST:SW + 2 * NST + (g + 1) * NST]
            bgb, cgb = bg.astype(BF16), cg.astype(BF16)
            cbm = _dot_nt(cgb, bgb)
            xs_g = xc[:, g * GW:(g + 1) * GW]
            self.xs.append(xs_g)
            self.xdt.append(xs_g * self.dt_x[:, g * GW:(g + 1) * GW])
            self.b.append(bg)
            self.c.append(cg)
            self.bb.append(bgb)
            self.cb16.append(cgb)
            self.cbm.append(cbm)
            self.dm_st.append(dm_st)
            self.m_st.append(jnp.concatenate([cbm] * HPG, axis=0) * dm_st)

    def elast_rows(self, g):
        return jnp.concatenate([jnp.broadcast_to(self.elast[:, j:j + 1], (HD, NST))
                                for j in range(HPG * g, HPG * (g + 1))], axis=0)

    def diag_blocks(self, stacked):
        out = stacked[(HPG - 1) * LB:HPG * LB, :]
        for hh in range(HPG - 2, -1, -1):
            out = jnp.where(self.head_lanes[hh], stacked[hh * LB:(hh + 1) * LB, :], out)
        return out

    def block_diag(self, v):
        return jnp.concatenate([jnp.where(self.head_lanes[hh], v, 0.0) for hh in range(HPG)], axis=0)


def _ssd_fwd_block(n, xt_ref, xc_ref, cw_ref, cb_ref, dtr_ref, sp_ref, z_ref, nw_ref, e64_ref, e128_ref,
                   yn_ref, y_ref, hs_ref, u_ref, dt_ref, acs_ref, dm_ref, h_scr):
    @pl.when(n == 0)
    def _():
        h_scr[...] = jnp.zeros_like(h_scr)

    h_all = h_scr[...]
    hs_ref[0] = h_all
    u, dt, acs = _ssd_pre(xt_ref, xc_ref, cw_ref, cb_ref, dtr_ref, sp_ref, n)
    u_ref[...] = u
    dt_ref[...] = dt
    acs_ref[...] = acs
    xc = u * _sigmoid(u)
    ck = _SsdChunk(xc, dt, acs, sp_ref[...], e64_ref[...], e128_ref[...])
    dm_ref[0] = jnp.concatenate(ck.dm_st, axis=0)
    ys, hn = [], []
    for g in range(2):
        gl = slice(g * GW, (g + 1) * GW)
        xdt = ck.xdt[g]
        hg = h_all[gl, :]
        y_diag = ck.diag_blocks(_dot(ck.m_st[g].astype(BF16), xdt.astype(BF16)))
        y_off = ck.e_x[:, gl] * _dot_nt(ck.cb16[g], hg.astype(BF16))
        ys.append(y_diag + y_off + ck.xs[g] * ck.dsk_x[:, gl])
        hn.append(hg * ck.elast_rows(g) + _dot_tn((xdt * ck.dte_x[:, gl]).astype(BF16), ck.bb[g]))
    h_scr[...] = jnp.concatenate(hn, axis=0)
    y = jnp.concatenate(ys, axis=1)
    y_ref[...] = y
    yn_ref[...] = _gated_norm_fwd(y, z_ref[...], nw_ref[...]).astype(BF16)


def _mixer_fwd(q, k, v, sinks8, xbc, conv_w8, conv_b, dtr, ssm_p, z, nw, gathers):
    s = q.shape[0]
    nb = s // LB
    prev = lambda n: (jnp.maximum(n - 1, 0), 0)
    cur = lambda n: (n, 0)
    items, ex_shapes, n_g = _exchange_items(gathers, [])
    ne = len(items)

    n_in, n_out = 16, 8
    relay_step = (nb - 1) // 2
    e64, e128 = _head_expanders()

    def body(*refs):
        a_in, s_in, ex_in = refs[:6], refs[6:n_in], refs[n_in:n_in + ne]
        o_ref, yn_ref, y_ref, hs_ref, u_ref, dt_ref, acs_ref, dm_ref = refs[n_in + ne:n_in + n_out + ne]
        ex_out = refs[n_in + n_out + ne:n_in + n_out + 2 * ne]
        h_scr = refs[n_in + n_out + 2 * ne]
        sems = refs[n_in + n_out + 1 + 2 * ne:]
        n = pl.program_id(0)

        @pl.when(n == 0)
        def _():
            _Exchange(n_g, ex_in, ex_out, sems).two_level_start()

        _attn_fwd_block(n, *a_in, o_ref)
        _ssd_fwd_block(n, *s_in, yn_ref, y_ref, hs_ref, u_ref, dt_ref, acs_ref, dm_ref, h_scr)

        @pl.when(n == relay_step)
        def _():
            _Exchange(n_g, ex_in, ex_out, sems).two_level_relay()

        @pl.when(n == nb - 1)
        def _():
            _Exchange(n_g, ex_in, ex_out, sems).two_level_finish()

    any_spec = pl.BlockSpec(memory_space=pl.ANY)
    tail = pl.BlockSpec((SUB, D), lambda n: (jnp.maximum(n * (LB // SUB) - 1, 0), 0))
    outs = _pcall(
        body, name="mixer_fwd", grid=(nb,),
        in_specs=[pl.BlockSpec((LB, AW), cur), pl.BlockSpec((LB, KVW), prev), pl.BlockSpec((LB, KVW), cur),
                  pl.BlockSpec((LB, KVW), prev), pl.BlockSpec((LB, KVW), cur), _const((8, LB)),
                  tail, pl.BlockSpec((LB, D), cur), _const((8, D)), _const((1, D)),
                  pl.BlockSpec((LB, LB), cur), _const((8, LB)), pl.BlockSpec((LB, SW), cur), _const((1, SW)),
                  _const(e64.shape), _const(e128.shape)]
        + [any_spec] * ne,
        out_specs=[pl.BlockSpec((LB, AW), cur), pl.BlockSpec((LB, SW), cur), pl.BlockSpec((LB, SW), cur),
                   pl.BlockSpec((1, NH * HD, NST), lambda n: (n, 0, 0)), pl.BlockSpec((LB, D), cur),
                   pl.BlockSpec((LB, LB), cur), pl.BlockSpec((LB, LB), cur),
                   pl.BlockSpec((1, NH * LB, LB), lambda n: (n, 0, 0))] + [any_spec] * ne,
        out_shape=[_sds((s, AW), BF16), _sds((s, SW), BF16), _sds((s, SW), F32), _sds((nb, NH * HD, NST), F32),
                   _sds((s, D), F32), _sds((s, LB), F32), _sds((s, LB), F32), _sds((nb, NH * LB, LB), F32)]
        + ex_shapes,
        scratch_shapes=[pltpu.VMEM((NH * HD, NST), F32)] + _exchange_sems(ne),
        compiler_params=_params(),
    )(q, k, k, v, v, sinks8, xbc, xbc, conv_w8, conv_b, dtr, ssm_p, z, nw, e64, e128, *items)
    return outs[:n_out], outs[n_out:]


def _head_expanders():
    j = lax.broadcasted_iota(jnp.int32, (LB, NH * HD), 0)
    e64 = (lax.broadcasted_iota(jnp.int32, (LB, NH * HD), 1) // HD == j).astype(BF16)
    j = lax.broadcasted_iota(jnp.int32, (LB, NH * LB), 0)
    e128 = (lax.broadcasted_iota(jnp.int32, (LB, NH * LB), 1) // LB == j).astype(BF16)
    return e64, e128


def _outproj_ffn_fwd_loss(attn, yn, x, tgt, mod8, n2w, fnw, w_out, w_gu_t, w_down):
    s = x.shape[0]
    tf = min(256, s)

    def body(a_ref, y_ref, x_ref, t_ref, mod_ref, nw_ref, fw_ref, wo_ref, wgu_ref, wd_ref,
             x2_ref, h2_ref, mo_ref, mix_ref, gu_ref, act_ref, dx3_ref, sm_ref):
        i = pl.program_id(0)

        @pl.when(i == 0)
        def _():
            sm_ref[...] = jnp.zeros_like(sm_ref)

        mix = jnp.concatenate([a_ref[...], y_ref[...]], axis=1)
        mix_ref[...] = mix
        mo = _dot(mix, wo_ref[...])
        mo_ref[...] = mo.astype(BF16)
        x2 = x_ref[...] + mod_ref[2:3, :] * mo
        x2_ref[...] = x2
        h2 = _norm_mod_fwd(x2, nw_ref[...], mod_ref[3:4, :], mod_ref[4:5, :]).astype(BF16)
        h2_ref[...] = h2
        gu = _dot_nt(h2, wgu_ref[...])
        gu_ref[...] = gu.astype(BF16)
        g = gu[:, :DFF]
        act = (g * _sigmoid(g) * gu[:, DFF:]).astype(BF16)
        act_ref[...] = act
        ff = _dot(act, wd_ref[...])
        x3 = x2 + mod_ref[5:6, :] * ff
        r = lax.rsqrt(jnp.mean(x3 * x3, axis=-1, keepdims=True) + EPS)
        xh = x3 * r
        fw = fw_ref[...]
        err = xh * fw - t_ref[...]
        dy = err * (1.0 / D)
        dxh = dy * fw
        dx3 = r * (dxh - xh * jnp.mean(dxh * xh, axis=-1, keepdims=True))
        dx3_ref[...] = dx3
        sm_ref[0:1, :] += jnp.sum(dx3 * ff, axis=0, keepdims=True)
        sm_ref[1:2, :] += jnp.sum(dy * xh, axis=0, keepdims=True)
        sm_ref[2:3, :] += jnp.sum(err * err, axis=0, keepdims=True)

    return _pcall(
        body, name="outproj_ffn_fwd_loss", grid=(s // tf,),
        in_specs=[_rows(tf, AW), _rows(tf, SW), _rows(tf, D), _rows(tf, D), _const((8, D)), _const((1, D)),
                  _const((1, D)), _const((D, D)), _const((2 * DFF, D)), _const((DFF, D))],
        out_specs=[_rows(tf, D), _rows(tf, D), _rows(tf, D), _rows(tf, D), _rows(tf, 2 * DFF), _rows(tf, DFF),
                   _rows(tf, D), pl.BlockSpec((8, D), lambda i: (0, 0))],
        out_shape=[_sds((s, D), F32), _sds((s, D), BF16), _sds((s, D), BF16), _sds((s, D), BF16),
                   _sds((s, 2 * DFF), BF16), _sds((s, DFF), BF16), _sds((s, D), F32), _sds((8, D), F32)],
        compiler_params=_params(),
    )(attn, yn, x, tgt, mod8, n2w, fnw, w_out, w_gu_t, w_down)


def _ffn_bwd(dx3, gu, x2, mixout, mod8, n2w, w_gu, w_down, w_out):
    s = x2.shape[0]
    tb = min(256, s)

    def body(dx3_ref, gu_ref, x2_ref, mo_ref, mod_ref, nw_ref, wgu_ref, wd_ref, wo_ref,
             dx2_ref, dff_ref, dgu_ref, dmix_ref, dattn_ref, dyn_ref, sm_ref):
        i = pl.program_id(0)

        @pl.when(i == 0)
        def _():
            sm_ref[...] = jnp.zeros_like(sm_ref)

        dx3 = dx3_ref[...]
        dff = (dx3 * mod_ref[5:6, :]).astype(BF16)
        dff_ref[...] = dff
        dact = _dot_nt(dff, wd_ref[...])
        g = gu_ref[:, :DFF].astype(F32)
        u = gu_ref[:, DFF:].astype(F32)
        sg = _sigmoid(g)
        dgu = jnp.concatenate([dact * u * sg * (1.0 + g * (1.0 - sg)), dact * g * sg], axis=1).astype(BF16)
        dgu_ref[...] = dgu
        dh2 = _dot(dgu, wgu_ref[...])
        dxn, d_shift, d_scale, d_w = _norm_mod_bwd(x2_ref[...], dh2, nw_ref[...], mod_ref[4:5, :])
        dx2 = dx3 + dxn
        dx2_ref[...] = dx2
        sm_ref[0:1, :] += d_shift
        sm_ref[1:2, :] += d_scale
        sm_ref[2:3, :] += d_w
        sm_ref[3:4, :] += jnp.sum(dx2 * mo_ref[...].astype(F32), axis=0, keepdims=True)
        dmix = (dx2 * mod_ref[2:3, :]).astype(BF16)
        dmix_ref[...] = dmix
        dmi = _dot_nt(dmix, wo_ref[...])
        dattn_ref[...] = dmi[:, :AW].astype(BF16)
        dyn_ref[...] = dmi[:, AW:]

    return _pcall(
        body, name="ffn_bwd", grid=(s // tb,),
        in_specs=[_rows(tb, D), _rows(tb, 2 * DFF), _rows(tb, D), _rows(tb, D), _const((8, D)), _const((1, D)),
                  _const((2 * DFF, D)), _const((DFF, D)), _const((D, D))],
        out_specs=[_rows(tb, D), _rows(tb, D), _rows(tb, 2 * DFF), _rows(tb, D), _rows(tb, AW), _rows(tb, SW),
                   pl.BlockSpec((8, D), lambda i: (0, 0))],
        out_shape=[_sds((s, D), F32), _sds((s, D), BF16), _sds((s, 2 * DFF), BF16), _sds((s, D), BF16),
                   _sds((s, AW), BF16), _sds((s, SW), F32), _sds((8, D), F32)],
        compiler_params=_params(),
    )(dx3, gu, x2, mixout, mod8, n2w, w_gu, w_down, w_out)


def _ssd_bwd_block(i, n, *refs):
    def run(dyn_ref, y_ref, z_ref, x_ref, u_ref, dt_ref, acs_ref, dm_ref, cw_ref, sp_ref, nw_ref, hs_ref, e64_ref,
            e128_ref, dzxd_ref, sm_ref, dh_scr, dun_scr):
        @pl.when(i == 0)
        def _():
            dh_scr[...] = jnp.zeros_like(dh_scr)
            dun_scr[...] = jnp.zeros_like(dun_scr)
            sm_ref[...] = jnp.zeros_like(sm_ref)

        u, dt, acs = u_ref[...], dt_ref[...], acs_ref[...]
        sg_u = _sigmoid(u)
        xc = u * sg_u
        a_neg = -jnp.exp(sp_ref[1:2, :])
        ck = _SsdChunk(xc, dt, acs, sp_ref[...], e64_ref[...], e128_ref[...],
                       decay=[dm_ref[0, g * HPG * LB:(g + 1) * HPG * LB, :] for g in range(2)])
        h_all = hs_ref[0]
        dh_all = dh_scr[...]
        riota = lax.broadcasted_iota(jnp.int32, (LB, LB), 0)
        lane1 = lax.broadcasted_iota(jnp.int32, (1, LB), 1)

        z = z_ref[...]
        y = y_ref[...]
        sgz = _sigmoid(z)
        sz = z * sgz
        yz = y * sz
        nwv = nw_ref[...]
        dyn_v = dyn_ref[...]
        dyhat = dyn_v * nwv
        yhat_parts, dyz_parts = [], []
        for g in range(2):
            gs = slice(g * 256, (g + 1) * 256)
            t = yz[:, gs]
            rg = lax.rsqrt(jnp.mean(t * t, axis=-1, keepdims=True) + EPS)
            yh = t * rg
            dyh = dyhat[:, gs]
            yhat_parts.append(yh)
            dyz_parts.append(rg * (dyh - yh * jnp.mean(dyh * yh, axis=-1, keepdims=True)))
        yhat = jnp.concatenate(yhat_parts, axis=1)
        dyz = jnp.concatenate(dyz_parts, axis=1)
        sm_ref[5:6, 0:SW] += jnp.sum(dyn_v * yhat, axis=0, keepdims=True)
        dy = dyz * sz
        dzxd_ref[:, 0:SW] = (dyz * y * sgz * (1.0 + z * (1.0 - sgz))).astype(BF16)

        cat = lambda parts: jnp.concatenate(parts, axis=1)
        dxs, dbs, dcs, dhp, g_cat, de_x, ddte_x, ddt_x, ddsk_x = ([] for _ in range(9))
        dacs_t = jnp.zeros((LB, LB), F32)
        hsum = jnp.zeros((1, LB), F32)
        for g in range(2):
            gl = slice(g * GW, (g + 1) * GW)
            xs_g, xdt, bgb, cgb = ck.xs[g], ck.xdt[g], ck.bb[g], ck.cb16[g]
            m_st, dm_st = ck.m_st[g], ck.dm_st[g]
            dt_x, e_x, dte_x = ck.dt_x[:, gl], ck.e_x[:, gl], ck.dte_x[:, gl]
            xdtb = xdt.astype(BF16)
            hg, dhn = h_all[gl, :], dh_all[gl, :]
            hb, dhnb = hg.astype(BF16), dhn.astype(BF16)
            dy_g = dy[:, gl]
            ddsk_x.append(jnp.sum(dy_g * xs_g, axis=0, keepdims=True))
            dy_bd = ck.block_diag(dy_g).astype(BF16)
            dm4 = _dot_nt(dy_bd, xdtb)
            dxdt = _dot_tn(m_st.astype(BF16), dy_bd)
            gmat = dm4 * m_st
            dcbm = dm4 * dm_st
            dcb = dcbm[0:LB] + dcbm[LB:2 * LB] + dcbm[2 * LB:3 * LB] + dcbm[3 * LB:4 * LB]
            g_cat.append(cat([gmat[hh * LB:(hh + 1) * LB, :] for hh in range(HPG)]))
            for hh in range(HPG):
                j = HPG * g + hh
                col_sum = jnp.sum(gmat[hh * LB:(hh + 1) * LB, :], axis=0, keepdims=True)
                dacs_t = dacs_t - jnp.where(riota == j, col_sum, 0.0)
                hsl = slice(hh * HD, (hh + 1) * HD)
                hsum = hsum + jnp.where(lane1 == j, jnp.sum(dhn[hsl, :] * hg[hsl, :]), 0.0)
            dchb = (dy_g * e_x).astype(BF16)
            dcg = _dot(dchb, hb)
            dh_prev = _dot_tn(dchb, cgb)
            de_x.append(dy_g * _dot_nt(cgb, hb))
            dxs_s = _dot_nt(bgb, dhnb)
            dbg = _dot((xdt * dte_x).astype(BF16), dhnb)
            dxdt = dxdt + dxs_s * dte_x
            ddte_x.append(dxs_s * xdt)
            dhp.append(dhn * ck.elast_rows(g) + dh_prev)
            dxs.append(dy_g * ck.dsk_x[:, gl] + dxdt * dt_x)
            ddt_x.append(dxdt * xs_g)
            dcbb = dcb.astype(BF16)
            dbs.append(dbg + _dot_tn(dcbb, cgb))
            dcs.append(dcg + _dot(dcbb, bgb))
        dh_scr[...] = jnp.concatenate(dhp, axis=0)
        red = _dot_nt_sel(jnp.concatenate([cat(de_x), cat(ddte_x), cat(ddt_x)], axis=0), ck.e64, 1)
        de_c, ddte_c, ddt_c = red[0:LB], red[LB:2 * LB], red[2 * LB:3 * LB]
        ddsk = _dot_nt_sel(jnp.broadcast_to(cat(ddsk_x), (SUB, NH * HD)), ck.e64, 2)[0:1, :]
        t1 = ddte_c * ck.dte_all
        dalast = jnp.sum(t1, axis=0, keepdims=True) + hsum * ck.elast
        dacs = (_dot_nt_sel(cat(g_cat), ck.e128, 2) + de_c * ck.e_all - t1 + jnp.transpose(dacs_t)
                + jnp.where(riota == LB - 1, dalast, 0.0))
        da = _cumsum_rows(dacs, True)
        ddt = ddt_c + da * a_neg
        da_log = jnp.sum(da * dt, axis=0, keepdims=True) * a_neg
        ddtr = ddt * (1.0 - jnp.exp(-dt))
        dzxd_ref[:, SW + D:ZXD] = ddtr.astype(BF16)
        sm_ref[6:7, 0:LB] += jnp.sum(ddtr, axis=0, keepdims=True)
        sm_ref[6:7, LB:2 * LB] += da_log
        sm_ref[6:7, 2 * LB:3 * LB] += ddsk

        du = cat(dxs + dbs + dcs) * (sg_u * (1.0 + u * (1.0 - sg_u)))
        xv = x_ref[...]
        adv = [du] + _conv_advances(du, dun_scr[...])
        sm_ref[0:1, :] += jnp.sum(du, axis=0, keepdims=True)
        dxbc = cw_ref[CONVK - 1:CONVK, :] * du
        for j in range(CONVK):
            sm_ref[CONVK - j:CONVK + 1 - j, :] += jnp.sum(adv[j] * xv, axis=0, keepdims=True)
            if j:
                dxbc = dxbc + cw_ref[CONVK - 1 - j:CONVK - j, :] * adv[j]
        dun_scr[...] = du[0:SUB, :]
        dzxd_ref[:, SW:SW + D] = dxbc.astype(BF16)

    run(*refs)


def _attn_bwd_block(i, n, q_ref, kp_ref, kc_ref, vp_ref, vc_ref, o_ref, do_ref, cos_ref, sin_ref, sink_ref,
                    dq_ref, dkv_ref, ds_ref, ck_scr, cv_scr):
    @pl.when(i == 0)
    def _():
        ds_ref[...] = jnp.zeros_like(ds_ref)
        ck_scr[...] = jnp.zeros_like(ck_scr)
        cv_scr[...] = jnp.zeros_like(cv_scr)

    qv, ov, dov, sinks = q_ref[...], o_ref[...], do_ref[...], sink_ref[...]
    kcat = jnp.concatenate([kp_ref[...], kc_ref[...]], axis=0)
    vcat = jnp.concatenate([vp_ref[...], vc_ref[...]], axis=0)
    upper = _upper_mask()
    srow = lax.broadcasted_iota(jnp.int32, (8, LB), 0)
    slane = lax.broadcasted_iota(jnp.int32, (8, LB), 1)
    dsink = jnp.zeros((8, LB), F32)
    dq_g, dk_g, dv_g = [], [], []
    for g in range(NQ // QPG):
        sl = slice(g * HD, (g + 1) * HD)
        qg = _stack_heads(qv, g)
        dog = _stack_heads(dov, g)
        p, es, denom = _attn_softmax(_attn_scores(n, qg, kcat[:, sl], upper), _sink_wide(sinks, g))
        probs, psink = p / denom, es / denom
        delta = _row_sums_wide(dog.astype(F32) * _stack_heads(ov, g).astype(F32), 2)
        dsc = probs * (_band(upper, _dot_nt(dog, vcat[0:LB, sl]), _dot_nt(dog, vcat[LB:2 * LB, sl])) - delta)
        sink_terms = (psink * delta)[:, 0:1]
        for hh in range(QPG):
            dsink = dsink - jnp.where((srow == QPG * g + hh) & (slane == 0),
                                      jnp.sum(sink_terms[hh * LB:(hh + 1) * LB, :]), 0.0)
        ds_p = jnp.where(upper, dsc, 0.0).astype(BF16)
        ds_c = jnp.where(upper, 0.0, dsc).astype(BF16)
        dq_g.append((_dot(ds_p, kcat[0:LB, sl]) + _dot(ds_c, kcat[LB:2 * LB, sl])) * ATT_SCALE)
        dk_g.append(jnp.concatenate([_dot_tn(ds_p, qg), _dot_tn(ds_c, qg)], axis=0) * ATT_SCALE)
        dv_g.append(jnp.concatenate([_dot_tn(jnp.where(upper, probs, 0.0).astype(BF16), dog),
                                     _dot_tn(jnp.where(upper, 0.0, probs).astype(BF16), dog)], axis=0))
    ds_ref[...] += dsink
    cs = cos_ref[...]
    sn = sin_ref[...]
    dk2 = jnp.concatenate(dk_g, axis=1)
    dv2 = jnp.concatenate(dv_g, axis=1)
    for a, tile in enumerate(_unstack_heads(dq_g)):
        dq_ref[:, a * LB:(a + 1) * LB] = _rope(tile, cs, sn, True).astype(BF16)
    dkv_ref[:, 0:KVW] = _rope(ck_scr[...] + dk2[LB:2 * LB, :], cs, sn, True).astype(BF16)
    dkv_ref[:, KVW:2 * KVW] = (cv_scr[...] + dv2[LB:2 * LB, :]).astype(BF16)
    ck_scr[...] = dk2[0:LB, :]
    cv_scr[...] = dv2[0:LB, :]


def _mixer_bwd(q, k, v, o, do, cos, sin, sinks8, dyn, y, z, xbc, u, dtv, acs, decay, conv_w8, ssm_p, nw, hs, scatters):
    s = q.shape[0]
    nb = s // LB
    cur = lambda i: (nb - 1 - i, 0)
    prev = lambda i: (jnp.maximum(nb - 2 - i, 0), 0)
    n_in = 24
    items, ex_shapes, n_g = _exchange_items([], scatters)
    ne = len(items)
    e64, e128 = _head_expanders()

    def body(*refs):
        i = pl.program_id(0)
        n = nb - 1 - i
        a_in, s_in, ex_in = refs[:10], refs[10:n_in], refs[n_in:n_in + ne]
        dp_ref, ds_ref, sm_ref = refs[n_in + ne:n_in + ne + 3]
        dq_ref, dkv_ref, dzxd_ref = dp_ref.at[:, O_Q:O_K], dp_ref.at[:, O_K:O_Z], dp_ref.at[:, O_Z:INP]
        ex_out = refs[n_in + ne + 3:n_in + 2 * ne + 3]
        ck_scr, cv_scr, dh_scr, dun_scr = refs[n_in + 2 * ne + 3:n_in + 2 * ne + 7]
        sems = refs[n_in + 2 * ne + 7:]

        @pl.when(i == 0)
        def _():
            _Exchange(n_g, ex_in, ex_out, sems).start()

        _attn_bwd_block(i, n, *a_in, dq_ref, dkv_ref, ds_ref, ck_scr, cv_scr)
        _ssd_bwd_block(i, n, *s_in, dzxd_ref, sm_ref, dh_scr, dun_scr)

        @pl.when(i == nb - 1)
        def _():
            _Exchange(n_g, ex_in, ex_out, sems).finish()

    any_spec = pl.BlockSpec(memory_space=pl.ANY)
    outs = _pcall(
        body, name="mixer_bwd", grid=(nb,),
        in_specs=[pl.BlockSpec((LB, AW), cur), pl.BlockSpec((LB, KVW), prev), pl.BlockSpec((LB, KVW), cur),
                  pl.BlockSpec((LB, KVW), prev), pl.BlockSpec((LB, KVW), cur), pl.BlockSpec((LB, AW), cur),
                  pl.BlockSpec((LB, AW), cur), pl.BlockSpec((LB, LB), cur), pl.BlockSpec((LB, LB), cur),
                  _const((8, LB)),
                  pl.BlockSpec((LB, SW), cur), pl.BlockSpec((LB, SW), cur), pl.BlockSpec((LB, SW), cur),
                  pl.BlockSpec((LB, D), cur), pl.BlockSpec((LB, D), cur), pl.BlockSpec((LB, LB), cur),
                  pl.BlockSpec((LB, LB), cur), pl.BlockSpec((1, NH * LB, LB), lambda i: (nb - 1 - i, 0, 0)),
                  _const((8, D)), _const((8, LB)), _const((1, SW)),
                  pl.BlockSpec((1, NH * HD, NST), lambda i: (nb - 1 - i, 0, 0)),
                  _const(e64.shape), _const(e128.shape)] + [any_spec] * ne,
        out_specs=[pl.BlockSpec((LB, INP), cur), pl.BlockSpec((8, LB), lambda i: (0, 0)),
                   pl.BlockSpec((8, D), lambda i: (0, 0))] + [any_spec] * ne,
        out_shape=[_sds((s, INP), BF16), _sds((8, LB), F32), _sds((8, D), F32)] + ex_shapes,
        scratch_shapes=[pltpu.VMEM((LB, KVW), F32), pltpu.VMEM((LB, KVW), F32),
                        pltpu.VMEM((NH * HD, NST), F32), pltpu.VMEM((SUB, D), F32)]
        + _exchange_sems(ne),
        compiler_params=_params(),
    )(q, k, k, v, v, o, do, cos, sin, sinks8, dyn, y, z, xbc, u, dtv, acs, decay, conv_w8, ssm_p, nw, hs, e64, e128,
      *items)
    return outs[0], outs[1], outs[2], outs[3:]


def _inproj_bwd(dproj, x, dx2, mod8, n1w, w_in_t, scatters):
    s = x.shape[0]
    tt = min(512, s)
    nt = s // tt
    items, ex_shapes, n_g = _exchange_items([], scatters)
    ne = len(items)

    def body(*refs):
        dp_ref, x_ref, dx2_ref, mod_ref, nw_ref, w_ref = refs[:6]
        ex_in = refs[6:6 + ne]
        gx_ref, sm_ref = refs[6 + ne:8 + ne]
        ex_out = refs[8 + ne:8 + 2 * ne]
        sems = refs[8 + 2 * ne:]
        i = pl.program_id(0)

        @pl.when(i == 0)
        def _():
            sm_ref[...] = jnp.zeros_like(sm_ref)
            _Exchange(n_g, ex_in, ex_out, sems).start()

        dh1 = _dot(dp_ref[...], w_ref[...])
        dxn, d_shift, d_scale, d_w = _norm_mod_bwd(x_ref[...], dh1, nw_ref[...], mod_ref[1:2, :])
        gx_ref[...] = dx2_ref[...] + dxn
        sm_ref[0:1, :] += d_shift
        sm_ref[1:2, :] += d_scale
        sm_ref[2:3, :] += d_w

        @pl.when(i == nt - 1)
        def _():
            _Exchange(n_g, ex_in, ex_out, sems).finish()

    any_spec = pl.BlockSpec(memory_space=pl.ANY)
    outs = _pcall(
        body, name="inproj_bwd", grid=(nt,),
        in_specs=[_rows(tt, INP), _rows(tt, D), _rows(tt, D), _const((8, D)), _const((1, D)), _const((INP, D))]
        + [any_spec] * ne,
        out_specs=[_rows(tt, D), pl.BlockSpec((8, D), lambda i: (0, 0))] + [any_spec] * ne,
        out_shape=[_sds((s, D), F32), _sds((8, D), F32)] + ex_shapes,
        scratch_shapes=_exchange_sems(ne),
        compiler_params=_params(),
    )(dproj, x, dx2, mod8, n1w, w_in_t, *items)
    return outs[0], outs[1], outs[2:]


def _wgrad(a, b, name):
    s, m = a.shape
    n = b.shape[1]
    tk = min(2048, s)
    wide = (1408, 1024, 512)
    tm = next((t for t in wide if m % t == 0), m)
    tn = n if n <= 2048 else _largest_divisor(n, wide)
    nk = s // tk

    def body(a_ref, b_ref, o_ref, acc):
        kk = pl.program_id(2)

        @pl.when(kk == 0)
        def _():
            acc[...] = jnp.zeros_like(acc)

        acc[...] += _dot_tn(a_ref[...], b_ref[...])

        @pl.when(kk == nk - 1)
        def _():
            o_ref[...] = acc[...].astype(BF16)

    return _pcall(
        body, name=name, grid=(m // tm, n // tn, nk),
        in_specs=[pl.BlockSpec((tk, tm), lambda i, j, kk: (kk, i)), pl.BlockSpec((tk, tn), lambda i, j, kk: (kk, j))],
        out_specs=pl.BlockSpec((tm, tn), lambda i, j, kk: (i, j)),
        out_shape=_sds((m, n), BF16),
        scratch_shapes=[pltpu.VMEM((tm, tn), F32)],
        compiler_params=_params(3),
    )(a, b)


PACK_ROWS = 24


def _pack_small(sm_f, sm_b, sm_s, sm_i, dsink):
    def body(f_ref, b_ref, s_ref, i_ref, k_ref, o_ref):
        o_ref[...] = jnp.zeros_like(o_ref)
        o_ref[0:2, :] = i_ref[0:2, :]
        o_ref[2:3, :] = b_ref[3:4, :]
        o_ref[3:5, :] = b_ref[0:2, :]
        o_ref[5:6, :] = f_ref[0:1, :]
        o_ref[6:7, :] = i_ref[2:3, :]
        o_ref[7:8, :] = b_ref[2:3, :]
        o_ref[8:9, :] = f_ref[1:2, :]
        o_ref[9:14, :] = s_ref[0:5, :]
        o_ref[14:15, :] = s_ref[5:6, :]
        o_ref[15:16, 0:3 * LB] = s_ref[6:7, 0:3 * LB]
        lane = lax.broadcasted_iota(jnp.int32, (1, LB), 1)
        sk = jnp.zeros((1, LB), F32)
        for h in range(NQ):
            sk = sk + jnp.where(lane == h, k_ref[h:h + 1, 0:1], 0.0)
        o_ref[15:16, 3 * LB:4 * LB] = sk
        o_ref[16:17, :] = f_ref[2:3, :]

    return _pcall(body, name="pack_small", out_shape=_sds((PACK_ROWS, D), F32))(sm_f, sm_b, sm_s, sm_i, dsink)


def _exchange(gathers, scatters, name, two_level=False):
    items, shapes, n_g = _exchange_items(gathers, scatters)
    n = len(items)
    assert not (two_level and scatters)

    def body(*refs):
        ex = _Exchange(n_g, refs[:n], refs[n:2 * n], refs[2 * n:])
        if two_level:
            ex.gather_two_level()
        else:
            ex.start()
            ex.finish()

    any_spec = pl.BlockSpec(memory_space=pl.ANY)
    return _pcall(
        body, name=name, in_specs=[any_spec] * n, out_specs=[any_spec] * n, out_shape=shapes,
        scratch_shapes=_exchange_sems(n),
    )(*items)


def _exchange_items(gathers, scatters):
    items = list(gathers) + list(scatters)
    shapes = [_sds((N_DEV,) + a.shape, a.dtype) for a in gathers] + [_sds(a.shape, a.dtype) for a in scatters]
    return items, shapes, len(gathers)


def _exchange_sems(n):
    return [pltpu.SemaphoreType.DMA((n, N_DEV - 1)), pltpu.SemaphoreType.DMA((n, N_DEV - 1)),
            pltpu.SemaphoreType.DMA((n,))]


class _Exchange:
    def __init__(self, n_g, ins, outs, sems):
        self.n_g, self.ins, self.outs = n_g, ins, outs
        self.send_sems, self.recv_sems, self.loc_sems = sems
        xi, yi, ci = lax.axis_index("x"), lax.axis_index("y"), lax.axis_index("c")
        self.me = 4 * xi + 2 * yi + ci
        self.peers = []
        for r in range(1, N_DEV):
            px = 1 - xi if r & 4 else xi
            py = 1 - yi if r & 2 else yi
            pc = 1 - ci if r & 1 else ci
            self.peers.append(((px, py, pc), 4 * px + 2 * py + pc))

    def _copy(self, t, r, landing):
        dev, peer = self.peers[r]
        src = self.ins[t] if t < self.n_g else self.ins[t].at[peer]
        return pltpu.make_async_remote_copy(
            src_ref=src, dst_ref=self.outs[t].at[landing], send_sem=self.send_sems.at[t, r],
            recv_sem=self.recv_sems.at[t, r], device_id=dev, device_id_type=pl.DeviceIdType.MESH)

    def _local(self, t):
        src = self.ins[t] if t < self.n_g else self.ins[t].at[self.me]
        return pltpu.make_async_copy(src, self.outs[t].at[self.me], self.loc_sems.at[t])

    def start(self):
        for t in range(len(self.ins)):
            self._local(t).start()
            for r in range(N_DEV - 1):
                self._copy(t, r, self.me).start()

    def finish(self):
        n = len(self.ins)
        for t in range(n):
            for r in range(N_DEV - 1):
                self._copy(t, r, self.peers[r][1]).wait_recv()
        for t in range(n):
            for r in range(N_DEV - 1):
                self._copy(t, r, self.me).wait_send()
            self._local(t).wait()

    def gather_two_level(self):
        self.two_level_start()
        self.two_level_relay()
        self.two_level_finish()

    DIRECT = (0, 1, 3, 5)

    def two_level_start(self):
        for t in range(len(self.ins)):
            self._local(t).start()
            for r in self.DIRECT:
                self._copy(t, r, self.me).start()

    def _relay(self, t, r):
        peer = self.peers[r][1]
        return pltpu.make_async_remote_copy(
            src_ref=self.outs[t].at[peer], dst_ref=self.outs[t].at[peer], send_sem=self.send_sems.at[t, r + 1],
            recv_sem=self.recv_sems.at[t, r + 1], device_id=self.peers[0][0], device_id_type=pl.DeviceIdType.MESH)

    def two_level_relay(self):
        for t in range(len(self.ins)):
            for r in self.DIRECT[1:]:
                self._copy(t, r, self.peers[r][1]).wait_recv()
                self._relay(t, r).start()

    def two_level_finish(self):
        n = len(self.ins)
        for t in range(n):
            for r in (0, 2, 4, 6):
                self._copy(t, r, self.peers[r][1]).wait_recv()
        for t in range(n):
            for r in self.DIRECT:
                self._copy(t, r, self.me).wait_send()
            for r in self.DIRECT[1:]:
                self._relay(t, r).wait_send()
            self._local(t).wait()


def _ada_fwd(c_all, w_cols, b_cols):
    def body(c_ref, w_ref, b_ref, o_ref):
        cv = c_ref[...]
        sc = (cv * _sigmoid(cv)).astype(BF16)
        o_ref[...] = _dot(sc, w_ref[...].astype(BF16)) + b_ref[...]

    return _pcall(body, name="ada_fwd", out_shape=_sds((N_DEV, w_cols.shape[1]), F32),
                  compiler_params=_params(0))(c_all, w_cols, b_cols)


def _adamw(w, g, m, v):
    m2 = ADAM_B1 * m + (1.0 - ADAM_B1) * g
    v2 = ADAM_B2 * v + (1.0 - ADAM_B2) * (g * g)
    m_hat = m2 / (1.0 - ADAM_B1 ** ADAM_STEP)
    v_hat = v2 / (1.0 - ADAM_B2 ** ADAM_STEP)
    delta = -ADAM_LR * (m_hat / (jnp.sqrt(v_hat) + ADAM_EPS) + ADAM_WD * w)
    return delta, m2, v2


def _sum_adamw(parts, w, m, v, name):
    rws, cols = w.shape
    tr = next((t for t in (256, 176, 128) if rws % t == 0), rws)

    def body(p_ref, w_ref, m_ref, v_ref, g_ref, d_ref, mo_ref, vo_ref):
        g = p_ref[0].astype(F32)
        for dev in range(1, N_DEV):
            g = g + p_ref[dev].astype(F32)
        g_ref[...] = g
        d_ref[...], mo_ref[...], vo_ref[...] = _adamw(w_ref[...], g, m_ref[...], v_ref[...])

    blk = pl.BlockSpec((tr, cols), lambda i: (i, 0))
    return _pcall(
        body, name=name, grid=(rws // tr,),
        in_specs=[pl.BlockSpec((N_DEV, tr, cols), lambda i: (0, i, 0)), blk, blk, blk],
        out_specs=[blk] * 4, out_shape=[_sds((rws, cols), F32)] * 4, compiler_params=_params(),
    )(parts, w, m, v)


def _wada_adamw(c_all, dmod_cols, w, m, v):
    rws, cols = w.shape
    tr = 256

    def body(c_ref, dm_ref, w_ref, m_ref, v_ref, g_ref, d_ref, mo_ref, vo_ref):
        cv = c_ref[...]
        sc = (cv * _sigmoid(cv)).astype(BF16)
        g = _dot_tn(sc, dm_ref[...].astype(BF16))
        g_ref[...] = g
        d_ref[...], mo_ref[...], vo_ref[...] = _adamw(w_ref[...], g, m_ref[...], v_ref[...])

    blk = pl.BlockSpec((tr, cols), lambda i: (i, 0))
    return _pcall(
        body, name="wada_adamw", grid=(rws // tr,),
        in_specs=[pl.BlockSpec((N_DEV, tr), lambda i: (0, i)), pl.BlockSpec((N_DEV, cols), lambda i: (0, 0)),
                  blk, blk, blk],
        out_specs=[blk] * 4, out_shape=[_sds((rws, cols), F32)] * 4, compiler_params=_params(),
    )(c_all, dmod_cols, w, m, v)


def _small_reduce(packs):
    def body(p_ref, o_ref):
        tot = p_ref[0]
        for dev in range(1, N_DEV):
            tot = tot + p_ref[dev]
        o_ref[...] = tot
        o_ref[16:17, :] = jnp.zeros((1, D), F32) + (0.5 / D) * jnp.sum(tot[16:17, :])

    return _pcall(body, name="small_reduce", out_shape=_sds((PACK_ROWS, D), F32))(packs)


def _adamw_many(ws, gs, ms, vs):
    k = len(ws)

    def body(*refs):
        for i in range(k):
            w_ref, g_ref, m_ref, v_ref = refs[i], refs[k + i], refs[2 * k + i], refs[3 * k + i]
            d_ref, mo_ref, vo_ref = refs[4 * k + i], refs[5 * k + i], refs[6 * k + i]
            d_ref[...], mo_ref[...], vo_ref[...] = _adamw(w_ref[...], g_ref[...], m_ref[...], v_ref[...])

    shp = [_sds(w.shape, F32) for w in ws]
    outs = _pcall(body, name="adamw_small", out_shape=shp * 3)(*ws, *gs, *ms, *vs)
    return outs[:k], outs[k:2 * k], outs[2 * k:]


def kernel(x, c, positions, w_ada, b_ada, norm1_w, w_in, conv_w, conv_b, dt_bias, a_log, d_skip, attn_sinks, ssm_norm_w, w_out, norm2_w, w_gate_up, w_down, final_norm_w, loss_target, m_w_ada, m_b_ada, m_norm1_w, m_w_in, m_conv_w, m_conv_b, m_dt_bias, m_a_log, m_d_skip, m_attn_sinks, m_ssm_norm_w, m_w_out, m_norm2_w, m_w_gate_up, m_w_down, m_final_norm_w, v_w_ada, v_b_ada, v_norm1_w, v_w_in, v_conv_w, v_conv_b, v_dt_bias, v_a_log, v_d_skip, v_attn_sinks, v_ssm_norm_w, v_w_out, v_norm2_w, v_w_gate_up, v_w_down, v_final_norm_w):
    s = x.shape[1]
    me = 4 * lax.axis_index("x") + 2 * lax.axis_index("y") + lax.axis_index("c")
    ada_cols = N_MOD * D // N_DEV

    c8 = jnp.pad(c, ((0, 7), (0, 0)))
    cw8 = jnp.pad(conv_w[0], ((0, 8 - CONVK), (0, 0)))
    w_in_t, m_w_in_t, v_w_in_t = jnp.transpose(w_in[0]), jnp.transpose(m_w_in[0]), jnp.transpose(v_w_in[0])
    w_gu_t, m_w_gu_t, v_w_gu_t = (jnp.transpose(w_gate_up[0]), jnp.transpose(m_w_gate_up[0]),
                                  jnp.transpose(v_w_gate_up[0]))
    g_c, g_in, g_cw = _exchange([c8, w_in_t.astype(BF16), cw8], [], "gather_in", two_level=True)
    c_all = g_c[:, 0, :]
    w_in_f = jnp.pad(g_in.reshape(IN_PROJ, D), ((0, INP - IN_PROJ), (0, 0)))
    conv_w8 = jnp.transpose(g_cw, (1, 0, 2)).reshape(8, D)

    b_cols = lax.dynamic_slice(b_ada, (0, me * ada_cols), (1, ada_cols))
    (g_mod,) = _exchange([_ada_fwd(c_all, w_ada[0], b_cols)], [], "gather_mod")
    mod = lax.dynamic_index_in_dim(g_mod, me, axis=1, keepdims=False).reshape(N_MOD, D)
    mod8 = jnp.pad(mod, ((0, 8 - N_MOD), (0, 0)))

    half = HD // 2
    inv_freq = ROPE_THETA ** (-jnp.arange(half, dtype=F32) / half)
    invf = jnp.tile(inv_freq, LB // half).reshape(1, LB)
    lanes = lambda a: jnp.pad(a, ((0, 0), (0, LB - a.shape[1])))
    ssm_p = jnp.pad(jnp.concatenate([lanes(dt_bias), lanes(a_log), lanes(d_skip)], axis=0), ((0, 5), (0, 0)))
    sinks8 = jnp.broadcast_to(attn_sinks.reshape(NQ, 1), (NQ, LB))

    xs, tgt, fnw = x[0], loss_target[0], final_norm_w.reshape(1, D)

    q, k, v, z, xbc, dtr, h1, cos, sin = _inproj_fwd(xs, positions[0].reshape(s, 1), invf, mod8, norm1_w, w_in_f)
    (attn, yn, y, hs, conv_u, dtv, acs, decay), (g_out, g_gu, g_down) = _mixer_fwd(
        q, k, v, sinks8, xbc, conv_w8, conv_b, dtr, ssm_p, z, ssm_norm_w,
        [w_out[0].astype(BF16), w_gu_t.astype(BF16), w_down[0].astype(BF16)])
    w_out_f = g_out.reshape(D, D)
    w_gu_f = g_gu.reshape(2 * DFF, D)
    w_down_f = g_down.reshape(DFF, D)
    x2, h2, mo, mix, gu, act, dx3, sm_f = _outproj_ffn_fwd_loss(attn, yn, xs, tgt, mod8, norm2_w, fnw, w_out_f, w_gu_f,
                                                                 w_down_f)

    dx2, dff, dgu, dmix, dattn, dyn, sm_b = _ffn_bwd(dx3, gu, x2, mo, mod8, norm2_w, w_gu_f, w_down_f, w_out_f)
    p_gu = _wgrad(dgu, h2, "wgrad_gate_up").reshape(N_DEV, 2 * DFF // N_DEV, D)
    p_down = _wgrad(act, dff, "wgrad_down").reshape(N_DEV, DFF // N_DEV, D)
    p_out = _wgrad(mix, dmix, "wgrad_out").reshape(N_DEV, D // N_DEV, D)
    dproj, dsink, sm_s, (r_gu, r_down, r_out) = _mixer_bwd(
        q, k, v, attn, dattn, cos, sin, sinks8, dyn, y, z, xbc, conv_u, dtv, acs, decay, conv_w8, ssm_p, ssm_norm_w, hs,
        [p_gu, p_down, p_out])
    p_in = _wgrad(dproj, h1, "wgrad_in")[:IN_PROJ].reshape(N_DEV, IN_PROJ // N_DEV, D)
    gx, sm_i, (r_in,) = _inproj_bwd(dproj, xs, dx2, mod8, norm1_w, w_in_f, [p_in])
    (g_pack,) = _exchange([_pack_small(sm_f, sm_b, sm_s, sm_i, dsink)], [], "gather_small")

    tot = _small_reduce(g_pack)
    loss = tot[16, 0]
    dmod_all = g_pack[:, 0:N_MOD, :].reshape(N_DEV, N_MOD * D)
    dmod_cols = lax.dynamic_slice(dmod_all, (0, me * ada_cols), (N_DEV, ada_cols))

    big = {
        "w_ada": _wada_adamw(c_all, dmod_cols, w_ada[0], m_w_ada[0], v_w_ada[0]),
        "w_in": [jnp.transpose(t) for t in _sum_adamw(r_in, w_in_t, m_w_in_t, v_w_in_t, "adamw_in")],
        "w_out": _sum_adamw(r_out, w_out[0], m_w_out[0], v_w_out[0], "adamw_out"),
        "w_gate_up": [jnp.transpose(t) for t in _sum_adamw(r_gu, w_gu_t, m_w_gu_t, v_w_gu_t, "adamw_gate_up")],
        "w_down": _sum_adamw(r_down, w_down[0], m_w_down[0], v_w_down[0], "adamw_down"),
    }
    small_names = ["b_ada", "norm1_w", "conv_w", "conv_b", "dt_bias", "a_log", "d_skip", "attn_sinks", "ssm_norm_w",
                   "norm2_w", "final_norm_w"]
    row15 = tot[15:16, :]
    small_g = {
        "b_ada": tot[0:N_MOD, :].reshape(1, N_MOD * D),
        "norm1_w": tot[6:7, :],
        "conv_w": lax.dynamic_slice(tot[10:14, :], (0, me * (D // N_DEV)), (CONVK, D // N_DEV)),
        "conv_b": tot[9:10, :],
        "dt_bias": row15[:, 0:NH],
        "a_log": row15[:, LB:LB + NH],
        "d_skip": row15[:, 2 * LB:2 * LB + NH],
        "attn_sinks": row15[:, 3 * LB:3 * LB + NQ],
        "ssm_norm_w": tot[14:15, 0:SW],
        "norm2_w": tot[7:8, :],
        "final_norm_w": tot[8:9, :],
    }
    small_w = {"b_ada": b_ada, "norm1_w": norm1_w, "conv_w": conv_w[0], "conv_b": conv_b, "dt_bias": dt_bias,
               "a_log": a_log, "d_skip": d_skip, "attn_sinks": attn_sinks, "ssm_norm_w": ssm_norm_w,
               "norm2_w": norm2_w, "final_norm_w": final_norm_w.reshape(1, D)}
    small_m = {"b_ada": m_b_ada, "norm1_w": m_norm1_w, "conv_w": m_conv_w[0], "conv_b": m_conv_b,
               "dt_bias": m_dt_bias, "a_log": m_a_log, "d_skip": m_d_skip, "attn_sinks": m_attn_sinks,
               "ssm_norm_w": m_ssm_norm_w, "norm2_w": m_norm2_w, "final_norm_w": m_final_norm_w.reshape(1, D)}
    small_v = {"b_ada": v_b_ada, "norm1_w": v_norm1_w, "conv_w": v_conv_w[0], "conv_b": v_conv_b,
               "dt_bias": v_dt_bias, "a_log": v_a_log, "d_skip": v_d_skip, "attn_sinks": v_attn_sinks,
               "ssm_norm_w": v_ssm_norm_w, "norm2_w": v_norm2_w, "final_norm_w": v_final_norm_w.reshape(1, D)}
    s_d, s_m, s_v = _adamw_many([small_w[k] for k in small_names], [small_g[k] for k in small_names],
                                [small_m[k] for k in small_names], [small_v[k] for k in small_names])

    order = ["w_ada", "b_ada", "norm1_w", "w_in", "conv_w", "conv_b", "dt_bias", "a_log", "d_skip", "attn_sinks",
             "ssm_norm_w", "w_out", "norm2_w", "w_gate_up", "w_down", "final_norm_w"]
    lead = {"w_ada", "w_in", "conv_w", "w_out", "w_gate_up", "w_down"}
    grads, deltas, new_m, new_v = [], [], [], []
    for name in order:
        if name in big:
            g, d, m2, v2 = big[name]
        else:
            i = small_names.index(name)
            g, d, m2, v2 = small_g[name], s_d[i], s_m[i], s_v[i]
        if name in lead:
            g, d, m2, v2 = g[None], d[None], m2[None], v2[None]
        if name == "final_norm_w":
            g, d, m2, v2 = g.reshape(D), d.reshape(D), m2.reshape(D), v2.reshape(D)
        grads.append(g)
        deltas.append(d)
        new_m.append(m2)
        new_v.append(v2)
    return (loss, gx[None], *grads, *deltas, *new_m, *new_v)
```

```python
import functools
import math

import jax
import jax.numpy as jnp
from jax import lax
from jax.experimental import pallas as pl
from jax.experimental.pallas import tpu as pltpu

F32 = jnp.float32
BF16 = jnp.bfloat16

N_DEV = 8
D = 1024
HD = 64
NQ = 8
AW = 512
KVW = 128
SW = 512
NST = 128
NH = 8
LB = 128
CONVK = 4
DFF = 2816
N_MOD = 6
IN_PROJ = 2312
INP = 2432
O_Q, O_K, O_V, O_Z, O_XBC, O_DT = 0, 512, 640, 768, 1280, 2304
ZXD = INP - O_Z
EPS = 1e-6
NEG = -1e30
ROPE_THETA = 10000.0
VMEM_LIMIT = 56 * 1024 * 1024

ADAM_LR = 0.001
ADAM_B1 = 0.9
ADAM_B2 = 0.999
ADAM_EPS = 1e-08
ADAM_WD = 0.01
ADAM_STEP = 10

NT_DIMS = (((1,), (1,)), ((), ()))
TN_DIMS = (((0,), (0,)), ((), ()))


def _pcall(body, **kw):
    return pl.pallas_call(body, **kw)


def _sds(shape, dtype):
    return jax.ShapeDtypeStruct(shape, dtype)


def _params(n_grid=1):
    return pltpu.CompilerParams(dimension_semantics=("arbitrary",) * n_grid, vmem_limit_bytes=VMEM_LIMIT)


def _const(shape):
    return pl.BlockSpec(shape, lambda *_: (0,) * len(shape), pipeline_mode=pl.Buffered(1))


def _largest_divisor(n, candidates):
    for cand in candidates:
        if n % cand == 0:
            return cand
    raise ValueError(f"no tile in {candidates} divides {n}")


def _rows(t, w):
    return pl.BlockSpec((t, w), lambda i: (i, 0))


def _dot(a, b):
    return jnp.dot(a, b, preferred_element_type=F32)


def _dot_nt(a, b):
    return lax.dot_general(a, b, NT_DIMS, preferred_element_type=F32)


def _dot_tn(a, b):
    return lax.dot_general(a, b, TN_DIMS, preferred_element_type=F32)


def _sigmoid(v):
    return 1.0 / (1.0 + jnp.exp(-v))


def _softplus(v):
    return jnp.maximum(v, 0.0) + jnp.log1p(jnp.exp(-jnp.abs(v)))


def _rope_sign_mask(shape):
    lane = lax.broadcasted_iota(jnp.int32, shape, 1)
    return (lane % HD) < (HD // 2)


def _rope(t, cs, sn, inverse):
    r_dn = pltpu.roll(t, HD // 2, 1)
    r_up = pltpu.roll(t, LB - HD // 2, 1)
    first = _rope_sign_mask(t.shape)
    if inverse:
        rot = jnp.where(first, r_up, -r_dn)
    else:
        rot = jnp.where(first, -r_up, r_dn)
    return t * cs + rot * sn


def _norm_mod_fwd(xv, nw, shift, scale):
    r = lax.rsqrt(jnp.mean(xv * xv, axis=-1, keepdims=True) + EPS)
    xh = xv * r
    return (xh * nw) * (1.0 + scale) + shift


def _norm_mod_bwd(xv, dh, nw, scale):
    r = lax.rsqrt(jnp.mean(xv * xv, axis=-1, keepdims=True) + EPS)
    xh = xv * r
    xn = xh * nw
    d_shift = jnp.sum(dh, axis=0, keepdims=True)
    d_scale = jnp.sum(dh * xn, axis=0, keepdims=True)
    dxn = dh * (1.0 + scale)
    d_w = jnp.sum(dxn * xh, axis=0, keepdims=True)
    dxh = dxn * nw
    dx = r * (dxh - xh * jnp.mean(dxh * xh, axis=-1, keepdims=True))
    return dx, d_shift, d_scale, d_w


def _inproj_fwd(x, pos, invf, mod8, n1w, w_in):
    s = x.shape[0]
    tt = min(512, s)

    def body(x_ref, pos_ref, invf_ref, mod_ref, nw_ref, w_ref,
             q_ref, k_ref, v_ref, z_ref, xbc_ref, dtr_ref, h1_ref, cos_ref, sin_ref):
        h = _norm_mod_fwd(x_ref[...], nw_ref[...], mod_ref[0:1, :], mod_ref[1:2, :])
        hb = h.astype(BF16)
        h1_ref[...] = hb
        proj = _dot_nt(hb, w_ref[...])
        ang = pos_ref[...].astype(F32) * invf_ref[...]
        cs = jnp.cos(ang)
        sn = jnp.sin(ang)
        cos_ref[...] = cs
        sin_ref[...] = sn
        for a in range(AW // LB):
            q_ref[:, a * LB:(a + 1) * LB] = _rope(proj[:, O_Q + a * LB:O_Q + (a + 1) * LB], cs, sn, False).astype(BF16)
        k_ref[...] = _rope(proj[:, O_K:O_V], cs, sn, False).astype(BF16)
        v_ref[...] = proj[:, O_V:O_Z].astype(BF16)
        z_ref[...] = proj[:, O_Z:O_XBC]
        xbc_ref[...] = proj[:, O_XBC:O_DT]
        dtr_ref[...] = proj[:, O_DT:INP]

    return _pcall(
        body, name="inproj_fwd", grid=(s // tt,),
        in_specs=[_rows(tt, D), _rows(tt, 1), _const((1, LB)), _const((8, D)), _const((1, D)), _const((INP, D))],
        out_specs=[_rows(tt, AW), _rows(tt, KVW), _rows(tt, KVW), _rows(tt, SW), _rows(tt, D), _rows(tt, LB),
                   _rows(tt, D), _rows(tt, LB), _rows(tt, LB)],
        out_shape=[_sds((s, AW), BF16), _sds((s, KVW), BF16), _sds((s, KVW), BF16), _sds((s, SW), F32),
                   _sds((s, D), F32), _sds((s, LB), F32), _sds((s, D), BF16), _sds((s, LB), F32), _sds((s, LB), F32)],
        compiler_params=_params(),
    )(x, pos, invf, mod8, n1w, w_in)


QPG = 4
ATT_SCALE = 1.0 / math.sqrt(HD)


def _stack_heads(val, g):
    return jnp.concatenate([val[:, (QPG * g + hh) * HD:(QPG * g + hh + 1) * HD] for hh in range(QPG)], axis=0)


def _unstack_heads(groups):
    pieces = [grp[hh * LB:(hh + 1) * LB, :] for grp in groups for hh in range(QPG)]
    return [jnp.concatenate(pieces[2 * a:2 * a + 2], axis=1) for a in range(NQ // 2)]


def _upper_mask():
    row = lax.broadcasted_iota(jnp.int32, (QPG * LB, LB), 0)
    col = lax.broadcasted_iota(jnp.int32, (QPG * LB, LB), 1)
    return col > (row % LB)


def _sink_wide(sinks, g):
    return jnp.concatenate([jnp.broadcast_to(sinks[QPG * g + hh:QPG * g + hh + 1, 0:1], (LB, LB))
                            for hh in range(QPG)], axis=0)


def _row_sums_wide(v, terms):
    return _dot_sel(v, jnp.ones((v.shape[1], LB), BF16), terms)


def _band(upper, prev_part, cur_part):
    return jnp.where(upper, prev_part, cur_part)


def _attn_scores(n, qg, kcat, upper):
    sp = _dot_nt(qg, kcat[0:LB, :]) * ATT_SCALE
    sc = _dot_nt(qg, kcat[LB:2 * LB, :]) * ATT_SCALE
    return _band(upper, jnp.where(n > 0, sp, NEG), sc)


def _attn_softmax(comb, sink):
    m = jnp.maximum(jnp.max(comb, axis=-1, keepdims=True), sink)
    p = jnp.exp(comb - m)
    es = jnp.exp(sink - m)
    return p, es, _row_sums_wide(p, 1) + es


def _attn_fwd_block(n, q_ref, kp_ref, kc_ref, vp_ref, vc_ref, sink_ref, o_ref):
    qv = q_ref[...]
    kcat = jnp.concatenate([kp_ref[...], kc_ref[...]], axis=0)
    vcat = jnp.concatenate([vp_ref[...], vc_ref[...]], axis=0)
    sinks = sink_ref[...]
    upper = _upper_mask()
    outs = []
    for g in range(NQ // QPG):
        sl = slice(g * HD, (g + 1) * HD)
        p, _, denom = _attn_softmax(_attn_scores(n, _stack_heads(qv, g), kcat[:, sl], upper), _sink_wide(sinks, g))
        outs.append((_dot(jnp.where(upper, p, 0.0).astype(BF16), vcat[0:LB, sl])
                     + _dot(jnp.where(upper, 0.0, p).astype(BF16), vcat[LB:2 * LB, sl])) / denom[:, 0:HD])
    for g, grp in enumerate(outs):
        for hh in range(QPG):
            h = QPG * g + hh
            o_ref[:, h * HD:(h + 1) * HD] = grp[hh * LB:(hh + 1) * LB, :].astype(BF16)


def _cumsum_rows(a, reverse):
    row = lax.broadcasted_iota(jnp.int32, a.shape, 0)
    step = 1
    while step < LB:
        if reverse:
            a = a + jnp.where(row < LB - step, pltpu.roll(a, LB - step, 0), 0.0)
        else:
            a = a + jnp.where(row >= step, pltpu.roll(a, step, 0), 0.0)
        step *= 2
    return a


SUB = 8


def _conv_shifts(tail, cur):
    row = lax.broadcasted_iota(jnp.int32, tail.shape, 0)
    out = [cur]
    for j in range(1, CONVK):
        rolled = pltpu.roll(cur, j, 0)
        top = jnp.where(row < j, pltpu.roll(tail, j, 0), rolled[0:SUB, :])
        out.append(jnp.concatenate([top, rolled[SUB:, :]], axis=0))
    return out


def _conv_advances(du, head):
    row = lax.broadcasted_iota(jnp.int32, head.shape, 0)
    out = []
    for j in range(1, CONVK):
        rolled = pltpu.roll(du, LB - j, 0)
        bottom = jnp.where(row >= SUB - j, pltpu.roll(head, SUB - j, 0), rolled[LB - SUB:, :])
        out.append(jnp.concatenate([rolled[:LB - SUB, :], bottom], axis=0))
    return out


def _split(v, terms):
    out = []
    for _ in range(terms - 1):
        t = v.astype(BF16)
        out.append(t)
        v = v - t.astype(F32)
    out.append(v.astype(BF16))
    return out


def _dot_sel(v, sel, terms):
    parts = [_dot(t, sel) for t in _split(v, terms)]
    return functools.reduce(lambda a, b: a + b, parts)


def _dot_nt_sel(v, sel, terms):
    parts = [_dot_nt(t, sel) for t in _split(v, terms)]
    return functools.reduce(lambda a, b: a + b, parts)


def _ssd_pre(xt_ref, xc_ref, cw_ref, cb_ref, dtr_ref, sp_ref, n):
    cur = xc_ref[...]
    tail = jnp.where(n > 0, xt_ref[...], 0.0)
    sh = _conv_shifts(tail, cur)
    u = cb_ref[...] + cw_ref[CONVK - 1:CONVK, :] * sh[0]
    for j in range(1, CONVK):
        u = u + cw_ref[CONVK - 1 - j:CONVK - j, :] * sh[j]
    dt = _softplus(dtr_ref[...] + sp_ref[0:1, :])
    acs = _cumsum_rows(dt * -jnp.exp(sp_ref[1:2, :]), False)
    return u, dt, acs


def _gated_norm_fwd(y, z, nw):
    sz = z * _sigmoid(z)
    yz = y * sz
    parts = []
    for g in range(2):
        t = yz[:, g * 256:(g + 1) * 256]
        parts.append(t * lax.rsqrt(jnp.mean(t * t, axis=-1, keepdims=True) + EPS))
    return jnp.concatenate(parts, axis=1) * nw


HPG = 4
GW = HPG * HD


class _SsdChunk:
    def __init__(self, xc, dt, acs, spv, e64, e128, decay=None):
        self.e64, self.e128 = e64, e128
        alast = acs[LB - 1:LB, :]
        self.e_all = jnp.exp(acs)
        self.dte_all = jnp.exp(alast - acs)
        self.elast = jnp.exp(alast)
        wide = _dot_sel(jnp.concatenate([dt, self.e_all, self.dte_all], axis=0), e64, 2)
        self.dt_x, self.e_x, self.dte_x = wide[0:LB], wide[LB:2 * LB], wide[2 * LB:3 * LB]
        self.dsk_x = _dot_sel(spv, e64, 3)[2:3, :]
        if decay is None:
            acs_t = jnp.transpose(acs)
            ac_x = _dot_sel(acs, e128, 3)
            row = lax.broadcasted_iota(jnp.int32, (HPG * LB, LB), 0)
            col = lax.broadcasted_iota(jnp.int32, (HPG * LB, LB), 1)
            causal = (row % LB) >= col
        lane = lax.broadcasted_iota(jnp.int32, (LB, GW), 1)
        self.head_lanes = [(lane >= hh * HD) & (lane < (hh + 1) * HD) for hh in range(HPG)]
        self.xs, self.xdt, self.b, self.c, self.bb, self.cb16, self.cbm, self.dm_st, self.m_st = ([] for _ in range(9))
        for g in range(2):
            heads = range(HPG * g, HPG * (g + 1))
            if decay is None:
                ac_st = jnp.concatenate([ac_x[:, j * LB:(j + 1) * LB] for j in heads], axis=0)
                ar_st = jnp.concatenate([jnp.broadcast_to(acs_t[j:j + 1, :], (LB, LB)) for j in heads], axis=0)
                dm_st = jnp.exp(jnp.where(causal, ac_st - ar_st, NEG))
            else:
                dm_st = decay[g]
            bg = xc[:, SW + g * NST:SW + (g + 1) * NST]
            cg = xc[:, SW + 2 * NST + g * NST:SW + 2 * NST + (g + 1) * NST]
            bgb, cgb = bg.astype(BF16), cg.astype(BF16)
            cbm = _dot_nt(cgb, bgb)
            xs_g = xc[:, g * GW:(g + 1) * GW]
            self.xs.append(xs_g)
            self.xdt.append(xs_g * self.dt_x[:, g * GW:(g + 1) * GW])
            self.b.append(bg)
            self.c.append(cg)
            self.bb.append(bgb)
            self.cb16.append(cgb)
            self.cbm.append(cbm)
            self.dm_st.append(dm_st)
            self.m_st.append(jnp.concatenate([cbm] * HPG, axis=0) * dm_st)

    def elast_rows(self, g):
        return jnp.concatenate([jnp.broadcast_to(self.elast[:, j:j + 1], (HD, NST))
                                for j in range(HPG * g, HPG * (g + 1))], axis=0)

    def diag_blocks(self, stacked):
        out = stacked[(HPG - 1) * LB:HPG * LB, :]
        for hh in range(HPG - 2, -1, -1):
            out = jnp.where(self.head_lanes[hh], stacked[hh * LB:(hh + 1) * LB, :], out)
        return out

    def block_diag(self, v):
        return jnp.concatenate([jnp.where(self.head_lanes[hh], v, 0.0) for hh in range(HPG)], axis=0)


def _ssd_fwd_block(n, xt_ref, xc_ref, cw_ref, cb_ref, dtr_ref, sp_ref, z_ref, nw_ref, e64_ref, e128_ref,
                   yn_ref, y_ref, hs_ref, u_ref, dt_ref, acs_ref, dm_ref, h_scr):
    @pl.when(n == 0)
    def _():
        h_scr[...] = jnp.zeros_like(h_scr)

    h_all = h_scr[...]
    hs_ref[0] = h_all
    u, dt, acs = _ssd_pre(xt_ref, xc_ref, cw_ref, cb_ref, dtr_ref, sp_ref, n)
    u_ref[...] = u
    dt_ref[...] = dt
    acs_ref[...] = acs
    xc = u * _sigmoid(u)
    ck = _SsdChunk(xc, dt, acs, sp_ref[...], e64_ref[...], e128_ref[...])
    dm_ref[0] = jnp.concatenate(ck.dm_st, axis=0)
    ys, hn = [], []
    for g in range(2):
        gl = slice(g * GW, (g + 1) * GW)
        xdt = ck.xdt[g]
        hg = h_all[gl, :]
        y_diag = ck.diag_blocks(_dot(ck.m_st[g].astype(BF16), xdt.astype(BF16)))
        y_off = ck.e_x[:, gl] * _dot_nt(ck.cb16[g], hg.astype(BF16))
        ys.append(y_diag + y_off + ck.xs[g] * ck.dsk_x[:, gl])
        hn.append(hg * ck.elast_rows(g) + _dot_tn((xdt * ck.dte_x[:, gl]).astype(BF16), ck.bb[g]))
    h_scr[...] = jnp.concatenate(hn, axis=0)
    y = jnp.concatenate(ys, axis=1)
    y_ref[...] = y
    yn_ref[...] = _gated_norm_fwd(y, z_ref[...], nw_ref[...]).astype(BF16)


def _mixer_fwd(q, k, v, sinks8, xbc, conv_w8, conv_b, dtr, ssm_p, z, nw, gathers):
    s = q.shape[0]
    nb = s // LB
    prev = lambda n: (jnp.maximum(n - 1, 0), 0)
    cur = lambda n: (n, 0)
    items, ex_shapes, n_g = _exchange_items(gathers, [])
    ne = len(items)

    n_in, n_out = 16, 8
    relay_step = (nb - 1) // 2
    e64, e128 = _head_expanders()

    def body(*refs):
        a_in, s_in, ex_in = refs[:6], refs[6:n_in], refs[n_in:n_in + ne]
        o_ref, yn_ref, y_ref, hs_ref, u_ref, dt_ref, acs_ref, dm_ref = refs[n_in + ne:n_in + n_out + ne]
        ex_out = refs[n_in + n_out + ne:n_in + n_out + 2 * ne]
        h_scr = refs[n_in + n_out + 2 * ne]
        sems = refs[n_in + n_out + 1 + 2 * ne:]
        n = pl.program_id(0)

        @pl.when(n == 0)
        def _():
            _Exchange(n_g, ex_in, ex_out, sems).two_level_start()

        _attn_fwd_block(n, *a_in, o_ref)
        _ssd_fwd_block(n, *s_in, yn_ref, y_ref, hs_ref, u_ref, dt_ref, acs_ref, dm_ref, h_scr)

        @pl.when(n == relay_step)
        def _():
            _Exchange(n_g, ex_in, ex_out, sems).two_level_relay()

        @pl.when(n == nb - 1)
        def _():
            _Exchange(n_g, ex_in, ex_out, sems).two_level_finish()

    any_spec = pl.BlockSpec(memory_space=pl.ANY)
    tail = pl.BlockSpec((SUB, D), lambda n: (jnp.maximum(n * (LB // SUB) - 1, 0), 0))
    outs = _pcall(
        body, name="mixer_fwd", grid=(nb,),
        in_specs=[pl.BlockSpec((LB, AW), cur), pl.BlockSpec((LB, KVW), prev), pl.BlockSpec((LB, KVW), cur),
                  pl.BlockSpec((LB, KVW), prev), pl.BlockSpec((LB, KVW), cur), _const((8, LB)),
                  tail, pl.BlockSpec((LB, D), cur), _const((8, D)), _const((1, D)),
                  pl.BlockSpec((LB, LB), cur), _const((8, LB)), pl.BlockSpec((LB, SW), cur), _const((1, SW)),
                  _const(e64.shape), _const(e128.shape)]
        + [any_spec] * ne,
        out_specs=[pl.BlockSpec((LB, AW), cur), pl.BlockSpec((LB, SW), cur), pl.BlockSpec((LB, SW), cur),
                   pl.BlockSpec((1, NH * HD, NST), lambda n: (n, 0, 0)), pl.BlockSpec((LB, D), cur),
                   pl.BlockSpec((LB, LB), cur), pl.BlockSpec((LB, LB), cur),
                   pl.BlockSpec((1, NH * LB, LB), lambda n: (n, 0, 0))] + [any_spec] * ne,
        out_shape=[_sds((s, AW), BF16), _sds((s, SW), BF16), _sds((s, SW), F32), _sds((nb, NH * HD, NST), F32),
                   _sds((s, D), F32), _sds((s, LB), F32), _sds((s, LB), F32), _sds((nb, NH * LB, LB), F32)]
        + ex_shapes,
        scratch_shapes=[pltpu.VMEM((NH * HD, NST), F32)] + _exchange_sems(ne),
        compiler_params=_params(),
    )(q, k, k, v, v, sinks8, xbc, xbc, conv_w8, conv_b, dtr, ssm_p, z, nw, e64, e128, *items)
    return outs[:n_out], outs[n_out:]


def _head_expanders():
    j = lax.broadcasted_iota(jnp.int32, (LB, NH * HD), 0)
    e64 = (lax.broadcasted_iota(jnp.int32, (LB, NH * HD), 1) // HD == j).astype(BF16)
    j = lax.broadcasted_iota(jnp.int32, (LB, NH * LB), 0)
    e128 = (lax.broadcasted_iota(jnp.int32, (LB, NH * LB), 1) // LB == j).astype(BF16)
    return e64, e128


def _outproj_ffn_fwd_loss(attn, yn, x, tgt, mod8, n2w, fnw, w_out, w_gu_t, w_down):
    s = x.shape[0]
    tf = min(256, s)

    def body(a_ref, y_ref, x_ref, t_ref, mod_ref, nw_ref, fw_ref, wo_ref, wgu_ref, wd_ref,
             x2_ref, h2_ref, mo_ref, mix_ref, gu_ref, act_ref, dx3_ref, sm_ref):
        i = pl.program_id(0)

        @pl.when(i == 0)
        def _():
            sm_ref[...] = jnp.zeros_like(sm_ref)

        mix = jnp.concatenate([a_ref[...], y_ref[...]], axis=1)
        mix_ref[...] = mix
        mo = _dot(mix, wo_ref[...])
        mo_ref[...] = mo.astype(BF16)
        x2 = x_ref[...] + mod_ref[2:3, :] * mo
        x2_ref[...] = x2
        h2 = _norm_mod_fwd(x2, nw_ref[...], mod_ref[3:4, :], mod_ref[4:5, :]).astype(BF16)
        h2_ref[...] = h2
        gu = _dot_nt(h2, wgu_ref[...])
        gu_ref[...] = gu.astype(BF16)
        g = gu[:, :DFF]
        act = (g * _sigmoid(g) * gu[:, DFF:]).astype(BF16)
        act_ref[...] = act
        ff = _dot(act, wd_ref[...])
        x3 = x2 + mod_ref[5:6, :] * ff
        r = lax.rsqrt(jnp.mean(x3 * x3, axis=-1, keepdims=True) + EPS)
        xh = x3 * r
        fw = fw_ref[...]
        err = xh * fw - t_ref[...]
        dy = err * (1.0 / D)
        dxh = dy * fw
        dx3 = r * (dxh - xh * jnp.mean(dxh * xh, axis=-1, keepdims=True))
        dx3_ref[...] = dx3
        sm_ref[0:1, :] += jnp.sum(dx3 * ff, axis=0, keepdims=True)
        sm_ref[1:2, :] += jnp.sum(dy * xh, axis=0, keepdims=True)
        sm_ref[2:3, :] += jnp.sum(err * err, axis=0, keepdims=True)

    return _pcall(
        body, name="outproj_ffn_fwd_loss", grid=(s // tf,),
        in_specs=[_rows(tf, AW), _rows(tf, SW), _rows(tf, D), _rows(tf, D), _const((8, D)), _const((1, D)),
                  _const((1, D)), _const((D, D)), _const((2 * DFF, D)), _const((DFF, D))],
        out_specs=[_rows(tf, D), _rows(tf, D), _rows(tf, D), _rows(tf, D), _rows(tf, 2 * DFF), _rows(tf, DFF),
                   _rows(tf, D), pl.BlockSpec((8, D), lambda i: (0, 0))],
        out_shape=[_sds((s, D), F32), _sds((s, D), BF16), _sds((s, D), BF16), _sds((s, D), BF16),
                   _sds((s, 2 * DFF), BF16), _sds((s, DFF), BF16), _sds((s, D), F32), _sds((8, D), F32)],
        compiler_params=_params(),
    )(attn, yn, x, tgt, mod8, n2w, fnw, w_out, w_gu_t, w_down)


def _ffn_bwd(dx3, gu, x2, mixout, mod8, n2w, w_gu, w_down, w_out):
    s = x2.shape[0]
    tb = min(256, s)

    def body(dx3_ref, gu_ref, x2_ref, mo_ref, mod_ref, nw_ref, wgu_ref, wd_ref, wo_ref,
             dx2_ref, dff_ref, dgu_ref, dmix_ref, dattn_ref, dyn_ref, sm_ref):
        i = pl.program_id(0)

        @pl.when(i == 0)
        def _():
            sm_ref[...] = jnp.zeros_like(sm_ref)

        dx3 = dx3_ref[...]
        dff = (dx3 * mod_ref[5:6, :]).astype(BF16)
        dff_ref[...] = dff
        dact = _dot_nt(dff, wd_ref[...])
        g = gu_ref[:, :DFF].astype(F32)
        u = gu_ref[:, DFF:].astype(F32)
        sg = _sigmoid(g)
        dgu = jnp.concatenate([dact * u * sg * (1.0 + g * (1.0 - sg)), dact * g * sg], axis=1).astype(BF16)
        dgu_ref[...] = dgu
        dh2 = _dot(dgu, wgu_ref[...])
        dxn, d_shift, d_scale, d_w = _norm_mod_bwd(x2_ref[...], dh2, nw_ref[...], mod_ref[4:5, :])
        dx2 = dx3 + dxn
        dx2_ref[...] = dx2
        sm_ref[0:1, :] += d_shift
        sm_ref[1:2, :] += d_scale
        sm_ref[2:3, :] += d_w
        sm_ref[3:4, :] += jnp.sum(dx2 * mo_ref[...].astype(F32), axis=0, keepdims=True)
        dmix = (dx2 * mod_ref[2:3, :]).astype(BF16)
        dmix_ref[...] = dmix
        dmi = _dot_nt(dmix, wo_ref[...])
        dattn_ref[...] = dmi[:, :AW].astype(BF16)
        dyn_ref[...] = dmi[:, AW:]

    return _pcall(
        body, name="ffn_bwd", grid=(s // tb,),
        in_specs=[_rows(tb, D), _rows(tb, 2 * DFF), _rows(tb, D), _rows(tb, D), _const((8, D)), _const((1, D)),
                  _const((2 * DFF, D)), _const((DFF, D)), _const((D, D))],
        out_specs=[_rows(tb, D), _rows(tb, D), _rows(tb, 2 * DFF), _rows(tb, D), _rows(tb, AW), _rows(tb, SW),
                   pl.BlockSpec((8, D), lambda i: (0, 0))],
        out_shape=[_sds((s, D), F32), _sds((s, D), BF16), _sds((s, 2 * DFF), BF16), _sds((s, D), BF16),
                   _sds((s, AW), BF16), _sds((s, SW), F32), _sds((8, D), F32)],
        compiler_params=_params(),
    )(dx3, gu, x2, mixout, mod8, n2w, w_gu, w_down, w_out)


def _ssd_bwd_block(i, n, *refs):
    def run(dyn_ref, y_ref, z_ref, x_ref, u_ref, dt_ref, acs_ref, dm_ref, cw_ref, sp_ref, nw_ref, hs_ref, e64_ref,
            e128_ref, dzxd_ref, sm_ref, dh_scr, dun_scr):
        @pl.when(i == 0)
        def _():
            dh_scr[...] = jnp.zeros_like(dh_scr)
            dun_scr[...] = jnp.zeros_like(dun_scr)
            sm_ref[...] = jnp.zeros_like(sm_ref)

        u, dt, acs = u_ref[...], dt_ref[...], acs_ref[...]
        sg_u = _sigmoid(u)
        xc = u * sg_u
        a_neg = -jnp.exp(sp_ref[1:2, :])
        ck = _SsdChunk(xc, dt, acs, sp_ref[...], e64_ref[...], e128_ref[...],
                       decay=[dm_ref[0, g * HPG * LB:(g + 1) * HPG * LB, :] for g in range(2)])
        h_all = hs_ref[0]
        dh_all = dh_scr[...]
        riota = lax.broadcasted_iota(jnp.int32, (LB, LB), 0)
        lane1 = lax.broadcasted_iota(jnp.int32, (1, LB), 1)

        z = z_ref[...]
        y = y_ref[...]
        sgz = _sigmoid(z)
        sz = z * sgz
        yz = y * sz
        nwv = nw_ref[...]
        dyn_v = dyn_ref[...]
        dyhat = dyn_v * nwv
        yhat_parts, dyz_parts = [], []
        for g in range(2):
            gs = slice(g * 256, (g + 1) * 256)
            t = yz[:, gs]
            rg = lax.rsqrt(jnp.mean(t * t, axis=-1, keepdims=True) + EPS)
            yh = t * rg
            dyh = dyhat[:, gs]
            yhat_parts.append(yh)
            dyz_parts.append(rg * (dyh - yh * jnp.mean(dyh * yh, axis=-1, keepdims=True)))
        yhat = jnp.concatenate(yhat_parts, axis=1)
        dyz = jnp.concatenate(dyz_parts, axis=1)
        sm_ref[5:6, 0:SW] += jnp.sum(dyn_v * yhat, axis=0, keepdims=True)
        dy = dyz * sz
        dzxd_ref[:, 0:SW] = (dyz * y * sgz * (1.0 + z * (1.0 - sgz))).astype(BF16)

        cat = lambda parts: jnp.concatenate(parts, axis=1)
        dxs, dbs, dcs, dhp, g_cat, de_x, ddte_x, ddt_x, ddsk_x = ([] for _ in range(9))
        dacs_t = jnp.zeros((LB, LB), F32)
        hsum = jnp.zeros((1, LB), F32)
        for g in range(2):
            gl = slice(g * GW, (g + 1) * GW)
            xs_g, xdt, bgb, cgb = ck.xs[g], ck.xdt[g], ck.bb[g], ck.cb16[g]
            m_st, dm_st = ck.m_st[g], ck.dm_st[g]
            dt_x, e_x, dte_x = ck.dt_x[:, gl], ck.e_x[:, gl], ck.dte_x[:, gl]
            xdtb = xdt.astype(BF16)
            hg, dhn = h_all[gl, :], dh_all[gl, :]
            hb, dhnb = hg.astype(BF16), dhn.astype(BF16)
            dy_g = dy[:, gl]
            ddsk_x.append(jnp.sum(dy_g * xs_g, axis=0, keepdims=True))
            dy_bd = ck.block_diag(dy_g).astype(BF16)
            dm4 = _dot_nt(dy_bd, xdtb)
            dxdt = _dot_tn(m_st.astype(BF16), dy_bd)
            gmat = dm4 * m_st
            dcbm = dm4 * dm_st
            dcb = dcbm[0:LB] + dcbm[LB:2 * LB] + dcbm[2 * LB:3 * LB] + dcbm[3 * LB:4 * LB]
            g_cat.append(cat([gmat[hh * LB:(hh + 1) * LB, :] for hh in range(HPG)]))
            for hh in range(HPG):
                j = HPG * g + hh
                col_sum = jnp.sum(gmat[hh * LB:(hh + 1) * LB, :], axis=0, keepdims=True)
                dacs_t = dacs_t - jnp.where(riota == j, col_sum, 0.0)
                hsl = slice(hh * HD, (hh + 1) * HD)
                hsum = hsum + jnp.where(lane1 == j, jnp.sum(dhn[hsl, :] * hg[hsl, :]), 0.0)
            dchb = (dy_g * e_x).astype(BF16)
            dcg = _dot(dchb, hb)
            dh_prev = _dot_tn(dchb, cgb)
            de_x.append(dy_g * _dot_nt(cgb, hb))
            dxs_s = _dot_nt(bgb, dhnb)
            dbg = _dot((xdt * dte_x).astype(BF16), dhnb)
            dxdt = dxdt + dxs_s * dte_x
            ddte_x.append(dxs_s * xdt)
            dhp.append(dhn * ck.elast_rows(g) + dh_prev)
            dxs.append(dy_g * ck.dsk_x[:, gl] + dxdt * dt_x)
            ddt_x.append(dxdt * xs_g)
            dcbb = dcb.astype(BF16)
            dbs.append(dbg + _dot_tn(dcbb, cgb))
            dcs.append(dcg + _dot(dcbb, bgb))
        dh_scr[...] = jnp.concatenate(dhp, axis=0)
        red = _dot_nt_sel(jnp.concatenate([cat(de_x), cat(ddte_x), cat(ddt_x)], axis=0), ck.e64, 1)
        de_c, ddte_c, ddt_c = red[0:LB], red[LB:2 * LB], red[2 * LB:3 * LB]
        ddsk = _dot_nt_sel(jnp.broadcast_to(cat(ddsk_x), (SUB, NH * HD)), ck.e64, 2)[0:1, :]
        t1 = ddte_c * ck.dte_all
        dalast = jnp.sum(t1, axis=0, keepdims=True) + hsum * ck.elast
        dacs = (_dot_nt_sel(cat(g_cat), ck.e128, 2) + de_c * ck.e_all - t1 + jnp.transpose(dacs_t)
                + jnp.where(riota == LB - 1, dalast, 0.0))
        da = _cumsum_rows(dacs, True)
        ddt = ddt_c + da * a_neg
        da_log = jnp.sum(da * dt, axis=0, keepdims=True) * a_neg
        ddtr = ddt * (1.0 - jnp.exp(-dt))
        dzxd_ref[:, SW + D:ZXD] = ddtr.astype(BF16)
        sm_ref[6:7, 0:LB] += jnp.sum(ddtr, axis=0, keepdims=True)
        sm_ref[6:7, LB:2 * LB] += da_log
        sm_ref[6:7, 2 * LB:3 * LB] += ddsk

        du = cat(dxs + dbs + dcs) * (sg_u * (1.0 + u * (1.0 - sg_u)))
        xv = x_ref[...]
        adv = [du] + _conv_advances(du, dun_scr[...])
        sm_ref[0:1, :] += jnp.sum(du, axis=0, keepdims=True)
        dxbc = cw_ref[CONVK - 1:CONVK, :] * du
        for j in range(CONVK):
            sm_ref[CONVK - j:CONVK + 1 - j, :] += jnp.sum(adv[j] * xv, axis=0, keepdims=True)
            if j:
                dxbc = dxbc + cw_ref[CONVK - 1 - j:CONVK - j, :] * adv[j]
        dun_scr[...] = du[0:SUB, :]
        dzxd_ref[:, SW:SW + D] = dxbc.astype(BF16)

    run(*refs)


def _attn_bwd_block(i, n, q_ref, kp_ref, kc_ref, vp_ref, vc_ref, o_ref, do_ref, cos_ref, sin_ref, sink_ref,
                    dq_ref, dkv_ref, ds_ref, ck_scr, cv_scr):
    @pl.when(i == 0)
    def _():
        ds_ref[...] = jnp.zeros_like(ds_ref)
        ck_scr[...] = jnp.zeros_like(ck_scr)
        cv_scr[...] = jnp.zeros_like(cv_scr)

    qv, ov, dov, sinks = q_ref[...], o_ref[...], do_ref[...], sink_ref[...]
    kcat = jnp.concatenate([kp_ref[...], kc_ref[...]], axis=0)
    vcat = jnp.concatenate([vp_ref[...], vc_ref[...]], axis=0)
    upper = _upper_mask()
    srow = lax.broadcasted_iota(jnp.int32, (8, LB), 0)
    slane = lax.broadcasted_iota(jnp.int32, (8, LB), 1)
    dsink = jnp.zeros((8, LB), F32)
    dq_g, dk_g, dv_g = [], [], []
    for g in range(NQ // QPG):
        sl = slice(g * HD, (g + 1) * HD)
        qg = _stack_heads(qv, g)
        dog = _stack_heads(dov, g)
        p, es, denom = _attn_softmax(_attn_scores(n, qg, kcat[:, sl], upper), _sink_wide(sinks, g))
        probs, psink = p / denom, es / denom
        delta = _row_sums_wide(dog.astype(F32) * _stack_heads(ov, g).astype(F32), 2)
        dsc = probs * (_band(upper, _dot_nt(dog, vcat[0:LB, sl]), _dot_nt(dog, vcat[LB:2 * LB, sl])) - delta)
        sink_terms = (psink * delta)[:, 0:1]
        for hh in range(QPG):
            dsink = dsink - jnp.where((srow == QPG * g + hh) & (slane == 0),
                                      jnp.sum(sink_terms[hh * LB:(hh + 1) * LB, :]), 0.0)
        ds_p = jnp.where(upper, dsc, 0.0).astype(BF16)
        ds_c = jnp.where(upper, 0.0, dsc).astype(BF16)
        dq_g.append((_dot(ds_p, kcat[0:LB, sl]) + _dot(ds_c, kcat[LB:2 * LB, sl])) * ATT_SCALE)
        dk_g.append(jnp.concatenate([_dot_tn(ds_p, qg), _dot_tn(ds_c, qg)], axis=0) * ATT_SCALE)
        dv_g.append(jnp.concatenate([_dot_tn(jnp.where(upper, probs, 0.0).astype(BF16), dog),
                                     _dot_tn(jnp.where(upper, 0.0, probs).astype(BF16), dog)], axis=0))
    ds_ref[...] += dsink
    cs = cos_ref[...]
    sn = sin_ref[...]
    dk2 = jnp.concatenate(dk_g, axis=1)
    dv2 = jnp.concatenate(dv_g, axis=1)
    for a, tile in enumerate(_unstack_heads(dq_g)):
        dq_ref[:, a * LB:(a + 1) * LB] = _rope(tile, cs, sn, True).astype(BF16)
    dkv_ref[:, 0:KVW] = _rope(ck_scr[...] + dk2[LB:2 * LB, :], cs, sn, True).astype(BF16)
    dkv_ref[:, KVW:2 * KVW] = (cv_scr[...] + dv2[LB:2 * LB, :]).astype(BF16)
    ck_scr[...] = dk2[0:LB, :]
    cv_scr[...] = dv2[0:LB, :]


def _mixer_bwd(q, k, v, o, do, cos, sin, sinks8, dyn, y, z, xbc, u, dtv, acs, decay, conv_w8, ssm_p, nw, hs, scatters):
    s = q.shape[0]
    nb = s // LB
    cur = lambda i: (nb - 1 - i, 0)
    prev = lambda i: (jnp.maximum(nb - 2 - i, 0), 0)
    n_in = 24
    items, ex_shapes, n_g = _exchange_items([], scatters)
    ne = len(items)
    e64, e128 = _head_expanders()

    def body(*refs):
        i = pl.program_id(0)
        n = nb - 1 - i
        a_in, s_in, ex_in = refs[:10], refs[10:n_in], refs[n_in:n_in + ne]
        dp_ref, ds_ref, sm_ref = refs[n_in + ne:n_in + ne + 3]
        dq_ref, dkv_ref, dzxd_ref = dp_ref.at[:, O_Q:O_K], dp_ref.at[:, O_K:O_Z], dp_ref.at[:, O_Z:INP]
        ex_out = refs[n_in + ne + 3:n_in + 2 * ne + 3]
        ck_scr, cv_scr, dh_scr, dun_scr = refs[n_in + 2 * ne + 3:n_in + 2 * ne + 7]
        sems = refs[n_in + 2 * ne + 7:]

        @pl.when(i == 0)
        def _():
            _Exchange(n_g, ex_in, ex_out, sems).start()

        _attn_bwd_block(i, n, *a_in, dq_ref, dkv_ref, ds_ref, ck_scr, cv_scr)
        _ssd_bwd_block(i, n, *s_in, dzxd_ref, sm_ref, dh_scr, dun_scr)

        @pl.when(i == nb - 1)
        def _():
            _Exchange(n_g, ex_in, ex_out, sems).finish()

    any_spec = pl.BlockSpec(memory_space=pl.ANY)
    outs = _pcall(
        body, name="mixer_bwd", grid=(nb,),
        in_specs=[pl.BlockSpec((LB, AW), cur), pl.BlockSpec((LB, KVW), prev), pl.BlockSpec((LB, KVW), cur),
                  pl.BlockSpec((LB, KVW), prev), pl.BlockSpec((LB, KVW), cur), pl.BlockSpec((LB, AW), cur),
                  pl.BlockSpec((LB, AW), cur), pl.BlockSpec((LB, LB), cur), pl.BlockSpec((LB, LB), cur),
                  _const((8, LB)),
                  pl.BlockSpec((LB, SW), cur), pl.BlockSpec((LB, SW), cur), pl.BlockSpec((LB, SW), cur),
                  pl.BlockSpec((LB, D), cur), pl.BlockSpec((LB, D), cur), pl.BlockSpec((LB, LB), cur),
                  pl.BlockSpec((LB, LB), cur), pl.BlockSpec((1, NH * LB, LB), lambda i: (nb - 1 - i, 0, 0)),
                  _const((8, D)), _const((8, LB)), _const((1, SW)),
                  pl.BlockSpec((1, NH * HD, NST), lambda i: (nb - 1 - i, 0, 0)),
                  _const(e64.shape), _const(e128.shape)] + [any_spec] * ne,
        out_specs=[pl.BlockSpec((LB, INP), cur), pl.BlockSpec((8, LB), lambda i: (0, 0)),
                   pl.BlockSpec((8, D), lambda i: (0, 0))] + [any_spec] * ne,
        out_shape=[_sds((s, INP), BF16), _sds((8, LB), F32), _sds((8, D), F32)] + ex_shapes,
        scratch_shapes=[pltpu.VMEM((LB, KVW), F32), pltpu.VMEM((LB, KVW), F32),
                        pltpu.VMEM((NH * HD, NST), F32), pltpu.VMEM((SUB, D), F32)]
        + _exchange_sems(ne),
        compiler_params=_params(),
    )(q, k, k, v, v, o, do, cos, sin, sinks8, dyn, y, z, xbc, u, dtv, acs, decay, conv_w8, ssm_p, nw, hs, e64, e128,
      *items)
    return outs[0], outs[1], outs[2], outs[3:]


def _inproj_bwd(dproj, x, dx2, mod8, n1w, w_in_t, scatters, smalls):
    s = x.shape[0]
    tt = min(512, s)
    nt = s // tt
    items, ex_shapes, n_g = _exchange_items([], scatters)
    ne = len(items)
    n_in = 10

    def body(*refs):
        dp_ref, x_ref, dx2_ref, mod_ref, nw_ref, w_ref, f_ref, b_ref, s_ref, k_ref = refs[:n_in]
        ex_in = refs[n_in:n_in + ne]
        gx_ref, sm_ref = refs[n_in + ne:n_in + 2 + ne]
        ex_out = refs[n_in + 2 + ne:n_in + 2 + 2 * ne]
        gpack_ref = refs[n_in + 2 + 2 * ne]
        pack_scr = refs[n_in + 3 + 2 * ne]
        sems = refs[n_in + 4 + 2 * ne:n_in + 7 + 2 * ne]
        pack_sems = refs[n_in + 7 + 2 * ne:]
        i = pl.program_id(0)

        @pl.when(i == 0)
        def _():
            sm_ref[...] = jnp.zeros_like(sm_ref)
            _Exchange(n_g, ex_in, ex_out, sems).start()

        w = w_ref[...]
        hr = tt // 2
        dh1 = [_dot(dp_ref[h * hr:(h + 1) * hr, :], w) for h in range(2)]
        sums = jnp.zeros((3, D), F32)
        for h in range(2):
            rows = slice(h * hr, (h + 1) * hr)
            dxn, d_shift, d_scale, d_w = _norm_mod_bwd(x_ref[rows, :], dh1[h], nw_ref[...], mod_ref[1:2, :])
            gx_ref[rows, :] = dx2_ref[rows, :] + dxn
            sums = sums + jnp.concatenate([d_shift, d_scale, d_w], axis=0)
        sm_ref[0:3, :] += sums

        @pl.when(i == nt - 1)
        def _():
            _pack_rows(f_ref, b_ref, s_ref, sm_ref, k_ref, pack_scr)
            small = _Exchange(1, [pack_scr], [gpack_ref], pack_sems)
            small.start()
            _Exchange(n_g, ex_in, ex_out, sems).finish()
            small.finish()

    any_spec = pl.BlockSpec(memory_space=pl.ANY)
    outs = _pcall(
        body, name="inproj_bwd", grid=(nt,),
        in_specs=[_rows(tt, INP), _rows(tt, D), _rows(tt, D), _const((8, D)), _const((1, D)), _const((INP, D)),
                  _const((8, D)), _const((8, D)), _const((8, D)), _const((8, LB))]
        + [any_spec] * ne,
        out_specs=[_rows(tt, D), pl.BlockSpec((8, D), lambda i: (0, 0))] + [any_spec] * (ne + 1),
        out_shape=[_sds((s, D), F32), _sds((8, D), F32)] + ex_shapes + [_sds((N_DEV, PACK_ROWS, D), F32)],
        scratch_shapes=[pltpu.VMEM((PACK_ROWS, D), F32)] + _exchange_sems(ne) + _exchange_sems(1),
        compiler_params=_params(),
    )(dproj, x, dx2, mod8, n1w, w_in_t, *smalls, *items)
    return outs[0], outs[2:2 + ne], outs[2 + ne]


def _wgrad(a, b, name):
    s, m = a.shape
    n = b.shape[1]
    tk = min(2048, s)
    wide = (1408, 1024, 512)
    tm = next((t for t in wide if m % t == 0), m)
    tn = n if n <= 2048 else _largest_divisor(n, wide)
    nk = s // tk

    def body(a_ref, b_ref, o_ref, acc):
        kk = pl.program_id(2)

        @pl.when(kk == 0)
        def _():
            acc[...] = jnp.zeros_like(acc)

        acc[...] += _dot_tn(a_ref[...], b_ref[...])

        @pl.when(kk == nk - 1)
        def _():
            o_ref[...] = acc[...].astype(BF16)

    return _pcall(
        body, name=name, grid=(m // tm, n // tn, nk),
        in_specs=[pl.BlockSpec((tk, tm), lambda i, j, kk: (kk, i)), pl.BlockSpec((tk, tn), lambda i, j, kk: (kk, j))],
        out_specs=pl.BlockSpec((tm, tn), lambda i, j, kk: (i, j)),
        out_shape=_sds((m, n), BF16),
        scratch_shapes=[pltpu.VMEM((tm, tn), F32)],
        compiler_params=_params(3),
    )(a, b)


PACK_ROWS = 24


def _pack_rows(f_ref, b_ref, s_ref, i_ref, k_ref, o_ref):
    o_ref[...] = jnp.zeros_like(o_ref)
    o_ref[0:2, :] = i_ref[0:2, :]
    o_ref[2:3, :] = b_ref[3:4, :]
    o_ref[3:5, :] = b_ref[0:2, :]
    o_ref[5:6, :] = f_ref[0:1, :]
    o_ref[6:7, :] = i_ref[2:3, :]
    o_ref[7:8, :] = b_ref[2:3, :]
    o_ref[8:9, :] = f_ref[1:2, :]
    o_ref[9:14, :] = s_ref[0:5, :]
    o_ref[14:15, :] = s_ref[5:6, :]
    o_ref[15:16, 0:3 * LB] = s_ref[6:7, 0:3 * LB]
    lane = lax.broadcasted_iota(jnp.int32, (1, LB), 1)
    sk = jnp.zeros((1, LB), F32)
    for h in range(NQ):
        sk = sk + jnp.where(lane == h, k_ref[h:h + 1, 0:1], 0.0)
    o_ref[15:16, 3 * LB:4 * LB] = sk
    o_ref[16:17, :] = f_ref[2:3, :]


def _exchange(gathers, scatters, name, two_level=False):
    items, shapes, n_g = _exchange_items(gathers, scatters)
    n = len(items)
    assert not (two_level and scatters)

    def body(*refs):
        ex = _Exchange(n_g, refs[:n], refs[n:2 * n], refs[2 * n:])
        if two_level:
            ex.gather_two_level()
        else:
            ex.start()
            ex.finish()

    any_spec = pl.BlockSpec(memory_space=pl.ANY)
    return _pcall(
        body, name=name, in_specs=[any_spec] * n, out_specs=[any_spec] * n, out_shape=shapes,
        scratch_shapes=_exchange_sems(n),
    )(*items)


def _exchange_items(gathers, scatters):
    items = list(gathers) + list(scatters)
    shapes = [_sds((N_DEV,) + a.shape, a.dtype) for a in gathers] + [_sds(a.shape, a.dtype) for a in scatters]
    return items, shapes, len(gathers)


def _exchange_sems(n):
    return [pltpu.SemaphoreType.DMA((n, N_DEV - 1)), pltpu.SemaphoreType.DMA((n, N_DEV - 1)),
            pltpu.SemaphoreType.DMA((n,))]


class _Exchange:
    def __init__(self, n_g, ins, outs, sems):
        self.n_g, self.ins, self.outs = n_g, ins, outs
        self.send_sems, self.recv_sems, self.loc_sems = sems
        xi, yi, ci = lax.axis_index("x"), lax.axis_index("y"), lax.axis_index("c")
        self.me = 4 * xi + 2 * yi + ci
        self.peers = []
        for r in range(1, N_DEV):
            px = 1 - xi if r & 4 else xi
            py = 1 - yi if r & 2 else yi
            pc = 1 - ci if r & 1 else ci
            self.peers.append(((px, py, pc), 4 * px + 2 * py + pc))

    def _copy(self, t, r, landing):
        dev, peer = self.peers[r]
        src = self.ins[t] if t < self.n_g else self.ins[t].at[peer]
        return pltpu.make_async_remote_copy(
            src_ref=src, dst_ref=self.outs[t].at[landing], send_sem=self.send_sems.at[t, r],
            recv_sem=self.recv_sems.at[t, r], device_id=dev, device_id_type=pl.DeviceIdType.MESH)

    def _local(self, t):
        src = self.ins[t] if t < self.n_g else self.ins[t].at[self.me]
        return pltpu.make_async_copy(src, self.outs[t].at[self.me], self.loc_sems.at[t])

    def start(self):
        for t in range(len(self.ins)):
            self._local(t).start()
            for r in range(N_DEV - 1):
                self._copy(t, r, self.me).start()

    def finish(self):
        n = len(self.ins)
        for t in range(n):
            for r in range(N_DEV - 1):
                self._copy(t, r, self.peers[r][1]).wait_recv()
        for t in range(n):
            for r in range(N_DEV - 1):
                self._copy(t, r, self.me).wait_send()
            self._local(t).wait()

    def gather_two_level(self):
        self.two_level_start()
        self.two_level_relay()
        self.two_level_finish()

    DIRECT = (0, 1, 3, 5)

    def two_level_start(self):
        for t in range(len(self.ins)):
            self._local(t).start()
            for r in self.DIRECT:
                self._copy(t, r, self.me).start()

    def _relay(self, t, r):
        peer = self.peers[r][1]
        return pltpu.make_async_remote_copy(
            src_ref=self.outs[t].at[peer], dst_ref=self.outs[t].at[peer], send_sem=self.send_sems.at[t, r + 1],
            recv_sem=self.recv_sems.at[t, r + 1], device_id=self.peers[0][0], device_id_type=pl.DeviceIdType.MESH)

    def two_level_relay(self):
        for t in range(len(self.ins)):
            for r in self.DIRECT[1:]:
                self._copy(t, r, self.peers[r][1]).wait_recv()
                self._relay(t, r).start()

    def two_level_finish(self):
        n = len(self.ins)
        for t in range(n):
            for r in (0, 2, 4, 6):
                self._copy(t, r, self.peers[r][1]).wait_recv()
        for t in range(n):
            for r in self.DIRECT:
                self._copy(t, r, self.me).wait_send()
            for r in self.DIRECT[1:]:
                self._relay(t, r).wait_send()
            self._local(t).wait()


def _ada_fwd(c_all, w_cols, b_cols):
    def body(c_ref, w_ref, b_ref, o_ref):
        cv = c_ref[...]
        sc = (cv * _sigmoid(cv)).astype(BF16)
        o_ref[...] = _dot(sc, w_ref[...].astype(BF16)) + b_ref[...]

    return _pcall(body, name="ada_fwd", out_shape=_sds((N_DEV, w_cols.shape[1]), F32),
                  compiler_params=_params(0))(c_all, w_cols, b_cols)


def _adamw(w, g, m, v):
    m2 = ADAM_B1 * m + (1.0 - ADAM_B1) * g
    v2 = ADAM_B2 * v + (1.0 - ADAM_B2) * (g * g)
    m_hat = m2 / (1.0 - ADAM_B1 ** ADAM_STEP)
    v_hat = v2 / (1.0 - ADAM_B2 ** ADAM_STEP)
    delta = -ADAM_LR * (m_hat / (jnp.sqrt(v_hat) + ADAM_EPS) + ADAM_WD * w)
    return delta, m2, v2


def _sum_adamw(parts, w, m, v, name):
    rws, cols = w.shape
    tr = next((t for t in (256, 176, 128) if rws % t == 0), rws)

    def body(p_ref, w_ref, m_ref, v_ref, g_ref, d_ref, mo_ref, vo_ref):
        g = p_ref[0].astype(F32)
        for dev in range(1, N_DEV):
            g = g + p_ref[dev].astype(F32)
        g_ref[...] = g
        d_ref[...], mo_ref[...], vo_ref[...] = _adamw(w_ref[...], g, m_ref[...], v_ref[...])

    blk = pl.BlockSpec((tr, cols), lambda i: (i, 0))
    return _pcall(
        body, name=name, grid=(rws // tr,),
        in_specs=[pl.BlockSpec((N_DEV, tr, cols), lambda i: (0, i, 0)), blk, blk, blk],
        out_specs=[blk] * 4, out_shape=[_sds((rws, cols), F32)] * 4, compiler_params=_params(),
    )(parts, w, m, v)


def _wada_adamw(c_all, dmod_cols, w, m, v):
    rws, cols = w.shape
    tr = 256

    def body(c_ref, dm_ref, w_ref, m_ref, v_ref, g_ref, d_ref, mo_ref, vo_ref):
        cv = c_ref[...]
        sc = (cv * _sigmoid(cv)).astype(BF16)
        g = _dot_tn(sc, dm_ref[...].astype(BF16))
        g_ref[...] = g
        d_ref[...], mo_ref[...], vo_ref[...] = _adamw(w_ref[...], g, m_ref[...], v_ref[...])

    blk = pl.BlockSpec((tr, cols), lambda i: (i, 0))
    return _pcall(
        body, name="wada_adamw", grid=(rws // tr,),
        in_specs=[pl.BlockSpec((N_DEV, tr), lambda i: (0, i)), pl.BlockSpec((N_DEV, cols), lambda i: (0, 0)),
                  blk, blk, blk],
        out_specs=[blk] * 4, out_shape=[_sds((rws, cols), F32)] * 4, compiler_params=_params(),
    )(c_all, dmod_cols, w, m, v)


def _small_reduce(packs):
    def body(p_ref, o_ref):
        tot = p_ref[0]
        for dev in range(1, N_DEV):
            tot = tot + p_ref[dev]
        o_ref[...] = tot
        o_ref[16:17, :] = jnp.zeros((1, D), F32) + (0.5 / D) * jnp.sum(tot[16:17, :])

    return _pcall(body, name="small_reduce", out_shape=_sds((PACK_ROWS, D), F32))(packs)


def _adamw_many(ws, gs, ms, vs):
    k = len(ws)

    def body(*refs):
        for i in range(k):
            w_ref, g_ref, m_ref, v_ref = refs[i], refs[k + i], refs[2 * k + i], refs[3 * k + i]
            d_ref, mo_ref, vo_ref = refs[4 * k + i], refs[5 * k + i], refs[6 * k + i]
            d_ref[...], mo_ref[...], vo_ref[...] = _adamw(w_ref[...], g_ref[...], m_ref[...], v_ref[...])

    shp = [_sds(w.shape, F32) for w in ws]
    outs = _pcall(body, name="adamw_small", out_shape=shp * 3)(*ws, *gs, *ms, *vs)
    return outs[:k], outs[k:2 * k], outs[2 * k:]


def kernel(x, c, positions, w_ada, b_ada, norm1_w, w_in, conv_w, conv_b, dt_bias, a_log, d_skip, attn_sinks, ssm_norm_w, w_out, norm2_w, w_gate_up, w_down, final_norm_w, loss_target, m_w_ada, m_b_ada, m_norm1_w, m_w_in, m_conv_w, m_conv_b, m_dt_bias, m_a_log, m_d_skip, m_attn_sinks, m_ssm_norm_w, m_w_out, m_norm2_w, m_w_gate_up, m_w_down, m_final_norm_w, v_w_ada, v_b_ada, v_norm1_w, v_w_in, v_conv_w, v_conv_b, v_dt_bias, v_a_log, v_d_skip, v_attn_sinks, v_ssm_norm_w, v_w_out, v_norm2_w, v_w_gate_up, v_w_down, v_final_norm_w):
    s = x.shape[1]
    me = 4 * lax.axis_index("x") + 2 * lax.axis_index("y") + lax.axis_index("c")
    ada_cols = N_MOD * D // N_DEV

    c8 = jnp.pad(c, ((0, 7), (0, 0)))
    cw8 = jnp.pad(conv_w[0], ((0, 8 - CONVK), (0, 0)))
    w_in_t, m_w_in_t, v_w_in_t = jnp.transpose(w_in[0]), jnp.transpose(m_w_in[0]), jnp.transpose(v_w_in[0])
    w_gu_t, m_w_gu_t, v_w_gu_t = (jnp.transpose(w_gate_up[0]), jnp.transpose(m_w_gate_up[0]),
                                  jnp.transpose(v_w_gate_up[0]))
    g_c, g_in, g_cw = _exchange([c8, w_in_t.astype(BF16), cw8], [], "gather_in", two_level=True)
    c_all = g_c[:, 0, :]
    w_in_f = jnp.pad(g_in.reshape(IN_PROJ, D), ((0, INP - IN_PROJ), (0, 0)))
    conv_w8 = jnp.transpose(g_cw, (1, 0, 2)).reshape(8, D)

    b_cols = lax.dynamic_slice(b_ada, (0, me * ada_cols), (1, ada_cols))
    (g_mod,) = _exchange([_ada_fwd(c_all, w_ada[0], b_cols)], [], "gather_mod")
    mod = lax.dynamic_index_in_dim(g_mod, me, axis=1, keepdims=False).reshape(N_MOD, D)
    mod8 = jnp.pad(mod, ((0, 8 - N_MOD), (0, 0)))

    half = HD // 2
    inv_freq = ROPE_THETA ** (-jnp.arange(half, dtype=F32) / half)
    invf = jnp.tile(inv_freq, LB // half).reshape(1, LB)
    lanes = lambda a: jnp.pad(a, ((0, 0), (0, LB - a.shape[1])))
    ssm_p = jnp.pad(jnp.concatenate([lanes(dt_bias), lanes(a_log), lanes(d_skip)], axis=0), ((0, 5), (0, 0)))
    sinks8 = jnp.broadcast_to(attn_sinks.reshape(NQ, 1), (NQ, LB))

    xs, tgt, fnw = x[0], loss_target[0], final_norm_w.reshape(1, D)

    q, k, v, z, xbc, dtr, h1, cos, sin = _inproj_fwd(xs, positions[0].reshape(s, 1), invf, mod8, norm1_w, w_in_f)
    (attn, yn, y, hs, conv_u, dtv, acs, decay), (g_out, g_gu, g_down) = _mixer_fwd(
        q, k, v, sinks8, xbc, conv_w8, conv_b, dtr, ssm_p, z, ssm_norm_w,
        [w_out[0].astype(BF16), w_gu_t.astype(BF16), w_down[0].astype(BF16)])
    w_out_f = g_out.reshape(D, D)
    w_gu_f = g_gu.reshape(2 * DFF, D)
    w_down_f = g_down.reshape(DFF, D)
    x2, h2, mo, mix, gu, act, dx3, sm_f = _outproj_ffn_fwd_loss(attn, yn, xs, tgt, mod8, norm2_w, fnw, w_out_f, w_gu_f,
                                                                 w_down_f)

    dx2, dff, dgu, dmix, dattn, dyn, sm_b = _ffn_bwd(dx3, gu, x2, mo, mod8, norm2_w, w_gu_f, w_down_f, w_out_f)
    p_gu = _wgrad(dgu, h2, "wgrad_gate_up").reshape(N_DEV, 2 * DFF // N_DEV, D)
    p_down = _wgrad(act, dff, "wgrad_down").reshape(N_DEV, DFF // N_DEV, D)
    p_out = _wgrad(mix, dmix, "wgrad_out").reshape(N_DEV, D // N_DEV, D)
    dproj, dsink, sm_s, (r_gu, r_down, r_out) = _mixer_bwd(
        q, k, v, attn, dattn, cos, sin, sinks8, dyn, y, z, xbc, conv_u, dtv, acs, decay, conv_w8, ssm_p, ssm_norm_w, hs,
        [p_gu, p_down, p_out])
    p_in = _wgrad(dproj, h1, "wgrad_in")[:IN_PROJ].reshape(N_DEV, IN_PROJ // N_DEV, D)
    gx, (r_in,), g_pack = _inproj_bwd(dproj, xs, dx2, mod8, norm1_w, w_in_f, [p_in], (sm_f, sm_b, sm_s, dsink))

    tot = _small_reduce(g_pack)
    loss = tot[16, 0]
    dmod_all = g_pack[:, 0:N_MOD, :].reshape(N_DEV, N_MOD * D)
    dmod_cols = lax.dynamic_slice(dmod_all, (0, me * ada_cols), (N_DEV, ada_cols))

    big = {
        "w_ada": _wada_adamw(c_all, dmod_cols, w_ada[0], m_w_ada[0], v_w_ada[0]),
        "w_in": [jnp.transpose(t) for t in _sum_adamw(r_in, w_in_t, m_w_in_t, v_w_in_t, "adamw_in")],
        "w_out": _sum_adamw(r_out, w_out[0], m_w_out[0], v_w_out[0], "adamw_out"),
        "w_gate_up": [jnp.transpose(t) for t in _sum_adamw(r_gu, w_gu_t, m_w_gu_t, v_w_gu_t, "adamw_gate_up")],
        "w_down": _sum_adamw(r_down, w_down[0], m_w_down[0], v_w_down[0], "adamw_down"),
    }
    small_names = ["b_ada", "norm1_w", "conv_w", "conv_b", "dt_bias", "a_log", "d_skip", "attn_sinks", "ssm_norm_w",
                   "norm2_w", "final_norm_w"]
    row15 = tot[15:16, :]
    small_g = {
        "b_ada": tot[0:N_MOD, :].reshape(1, N_MOD * D),
        "norm1_w": tot[6:7, :],
        "conv_w": lax.dynamic_slice(tot[10:14, :], (0, me * (D // N_DEV)), (CONVK, D // N_DEV)),
        "conv_b": tot[9:10, :],
        "dt_bias": row15[:, 0:NH],
        "a_log": row15[:, LB:LB + NH],
        "d_skip": row15[:, 2 * LB:2 * LB + NH],
        "attn_sinks": row15[:, 3 * LB:3 * LB + NQ],
        "ssm_norm_w": tot[14:15, 0:SW],
        "norm2_w": tot[7:8, :],
        "final_norm_w": tot[8:9, :],
    }
    small_w = {"b_ada": b_ada, "norm1_w": norm1_w, "conv_w": conv_w[0], "conv_b": conv_b, "dt_bias": dt_bias,
               "a_log": a_log, "d_skip": d_skip, "attn_sinks": attn_sinks, "ssm_norm_w": ssm_norm_w,
               "norm2_w": norm2_w, "final_norm_w": final_norm_w.reshape(1, D)}
    small_m = {"b_ada": m_b_ada, "norm1_w": m_norm1_w, "conv_w": m_conv_w[0], "conv_b": m_conv_b,
               "dt_bias": m_dt_bias, "a_log": m_a_log, "d_skip": m_d_skip, "attn_sinks": m_attn_sinks,
               "ssm_norm_w": m_ssm_norm_w, "norm2_w": m_norm2_w, "final_norm_w": m_final_norm_w.reshape(1, D)}
    small_v = {"b_ada": v_b_ada, "norm1_w": v_norm1_w, "conv_w": v_conv_w[0], "conv_b": v_conv_b,
               "dt_bias": v_dt_bias, "a_log": v_a_log, "d_skip": v_d_skip, "attn_sinks": v_attn_sinks,
               "ssm_norm_w": v_ssm_norm_w, "norm2_w": v_norm2_w, "final_norm_w": v_final_norm_w.reshape(1, D)}
    s_d, s_m, s_v = _adamw_many([small_w[k] for k in small_names], [small_g[k] for k in small_names],
                                [small_m[k] for k in small_names], [small_v[k] for k in small_names])

    order = ["w_ada", "b_ada", "norm1_w", "w_in", "conv_w", "conv_b", "dt_bias", "a_log", "d_skip", "attn_sinks",
             "ssm_norm_w", "w_out", "norm2_w", "w_gate_up", "w_down", "final_norm_w"]
    lead = {"w_ada", "w_in", "conv_w", "w_out", "w_gate_up", "w_down"}
    grads, deltas, new_m, new_v = [], [], [], []
    for name in order:
        if name in big:
            g, d, m2, v2 = big[name]
        else:
            i = small_names.index(name)
            g, d, m2, v2 = small_g[name], s_d[i], s_m[i], s_v[i]
        if name in lead:
            g, d, m2, v2 = g[None], d[None], m2[None], v2[None]
        if name == "final_norm_w":
            g, d, m2, v2 = g.reshape(D), d.reshape(D), m2.reshape(D), v2.reshape(D)
        grads.append(g)
        deltas.append(d)
        new_m.append(m2)
        new_v.append(v2)
    return (loss, gx[None], *grads, *deltas, *new_m, *new_v)
```

```python
import functools
import math

import jax
import jax.numpy as jnp
from jax import lax
from jax.experimental import pallas as pl
from jax.experimental.pallas import tpu as pltpu

F32 = jnp.float32
BF16 = jnp.bfloat16

N_DEV = 8
D = 1024
HD = 64
NQ = 8
AW = 512
KVW = 128
SW = 512
NST = 128
NH = 8
LB = 128
CONVK = 4
DFF = 2816
N_MOD = 6
IN_PROJ = 2312
INP = 2432
O_Q, O_K, O_V, O_Z, O_XBC, O_DT = 0, 512, 640, 768, 1280, 2304
ZXD = INP - O_Z
EPS = 1e-6
NEG = -1e30
ROPE_THETA = 10000.0
VMEM_LIMIT = 56 * 1024 * 1024

ADAM_LR = 0.001
ADAM_B1 = 0.9
ADAM_B2 = 0.999
ADAM_EPS = 1e-08
ADAM_WD = 0.01
ADAM_STEP = 10

NT_DIMS = (((1,), (1,)), ((), ()))
TN_DIMS = (((0,), (0,)), ((), ()))


def _pcall(body, **kw):
    return pl.pallas_call(body, **kw)


def _sds(shape, dtype):
    return jax.ShapeDtypeStruct(shape, dtype)


def _params(n_grid=1):
    return pltpu.CompilerParams(dimension_semantics=("arbitrary",) * n_grid, vmem_limit_bytes=VMEM_LIMIT)


def _const(shape):
    return pl.BlockSpec(shape, lambda *_: (0,) * len(shape), pipeline_mode=pl.Buffered(1))


def _largest_divisor(n, candidates):
    for cand in candidates:
        if n % cand == 0:
            return cand
    raise ValueError(f"no tile in {candidates} divides {n}")


def _rows(t, w):
    return pl.BlockSpec((t, w), lambda i: (i, 0))


def _dot(a, b):
    return jnp.dot(a, b, preferred_element_type=F32)


def _dot_nt(a, b):
    return lax.dot_general(a, b, NT_DIMS, preferred_element_type=F32)


def _dot_tn(a, b):
    return lax.dot_general(a, b, TN_DIMS, preferred_element_type=F32)


def _sigmoid(v):
    return 1.0 / (1.0 + jnp.exp(-v))


def _softplus(v):
    return jnp.maximum(v, 0.0) + jnp.log1p(jnp.exp(-jnp.abs(v)))


def _rope_sign_mask(shape):
    lane = lax.broadcasted_iota(jnp.int32, shape, 1)
    return (lane % HD) < (HD // 2)


def _rope(t, cs, sn, inverse):
    r_dn = pltpu.roll(t, HD // 2, 1)
    r_up = pltpu.roll(t, LB - HD // 2, 1)
    first = _rope_sign_mask(t.shape)
    if inverse:
        rot = jnp.where(first, r_up, -r_dn)
    else:
        rot = jnp.where(first, -r_up, r_dn)
    return t * cs + rot * sn


def _norm_mod_fwd(xv, nw, shift, scale):
    r = lax.rsqrt(jnp.mean(xv * xv, axis=-1, keepdims=True) + EPS)
    xh = xv * r
    return (xh * nw) * (1.0 + scale) + shift


def _norm_mod_bwd(xv, dh, nw, scale):
    r = lax.rsqrt(jnp.mean(xv * xv, axis=-1, keepdims=True) + EPS)
    xh = xv * r
    xn = xh * nw
    d_shift = jnp.sum(dh, axis=0, keepdims=True)
    d_scale = jnp.sum(dh * xn, axis=0, keepdims=True)
    dxn = dh * (1.0 + scale)
    d_w = jnp.sum(dxn * xh, axis=0, keepdims=True)
    dxh = dxn * nw
    dx = r * (dxh - xh * jnp.mean(dxh * xh, axis=-1, keepdims=True))
    return dx, d_shift, d_scale, d_w


def _inproj_fwd(x, pos, invf, mod8, n1w, w_in):
    s = x.shape[0]
    tt = min(512, s)

    def body(x_ref, pos_ref, invf_ref, mod_ref, nw_ref, w_ref,
             q_ref, k_ref, v_ref, z_ref, xbc_ref, dtr_ref, h1_ref, cos_ref, sin_ref):
        h = _norm_mod_fwd(x_ref[...], nw_ref[...], mod_ref[0:1, :], mod_ref[1:2, :])
        hb = h.astype(BF16)
        h1_ref[...] = hb
        proj = _dot_nt(hb, w_ref[...])
        ang = pos_ref[...].astype(F32) * invf_ref[...]
        cs = jnp.cos(ang)
        sn = jnp.sin(ang)
        cos_ref[...] = cs
        sin_ref[...] = sn
        for a in range(AW // LB):
            q_ref[:, a * LB:(a + 1) * LB] = _rope(proj[:, O_Q + a * LB:O_Q + (a + 1) * LB], cs, sn, False).astype(BF16)
        k_ref[...] = _rope(proj[:, O_K:O_V], cs, sn, False).astype(BF16)
        v_ref[...] = proj[:, O_V:O_Z].astype(BF16)
        z_ref[...] = proj[:, O_Z:O_XBC]
        xbc_ref[...] = proj[:, O_XBC:O_DT]
        dtr_ref[...] = proj[:, O_DT:INP]

    return _pcall(
        body, name="inproj_fwd", grid=(s // tt,),
        in_specs=[_rows(tt, D), _rows(tt, 1), _const((1, LB)), _const((8, D)), _const((1, D)), _const((INP, D))],
        out_specs=[_rows(tt, AW), _rows(tt, KVW), _rows(tt, KVW), _rows(tt, SW), _rows(tt, D), _rows(tt, LB),
                   _rows(tt, D), _rows(tt, LB), _rows(tt, LB)],
        out_shape=[_sds((s, AW), BF16), _sds((s, KVW), BF16), _sds((s, KVW), BF16), _sds((s, SW), F32),
                   _sds((s, D), F32), _sds((s, LB), F32), _sds((s, D), BF16), _sds((s, LB), F32), _sds((s, LB), F32)],
        compiler_params=_params(),
    )(x, pos, invf, mod8, n1w, w_in)


QPG = 4
ATT_SCALE = 1.0 / math.sqrt(HD)


def _stack_heads(val, g):
    return jnp.concatenate([val[:, (QPG * g + hh) * HD:(QPG * g + hh + 1) * HD] for hh in range(QPG)], axis=0)


def _unstack_heads(groups):
    pieces = [grp[hh * LB:(hh + 1) * LB, :] for grp in groups for hh in range(QPG)]
    return [jnp.concatenate(pieces[2 * a:2 * a + 2], axis=1) for a in range(NQ // 2)]


def _upper_mask():
    row = lax.broadcasted_iota(jnp.int32, (QPG * LB, LB), 0)
    col = lax.broadcasted_iota(jnp.int32, (QPG * LB, LB), 1)
    return col > (row % LB)


def _sink_wide(sinks, g):
    return jnp.concatenate([jnp.broadcast_to(sinks[QPG * g + hh:QPG * g + hh + 1, 0:1], (LB, LB))
                            for hh in range(QPG)], axis=0)


def _row_sums_wide(v, terms):
    return _dot_sel(v, jnp.ones((v.shape[1], LB), BF16), terms)


def _band(upper, prev_part, cur_part):
    return jnp.where(upper, prev_part, cur_part)


def _attn_scores(n, qg, kcat, upper):
    sp = _dot_nt(qg, kcat[0:LB, :]) * ATT_SCALE
    sc = _dot_nt(qg, kcat[LB:2 * LB, :]) * ATT_SCALE
    return _band(upper, jnp.where(n > 0, sp, NEG), sc)


def _attn_softmax(comb, sink):
    m = jnp.maximum(jnp.max(comb, axis=-1, keepdims=True), sink)
    p = jnp.exp(comb - m)
    es = jnp.exp(sink - m)
    return p, es, _row_sums_wide(p, 1) + es


def _attn_fwd_block(n, q_ref, kp_ref, kc_ref, vp_ref, vc_ref, sink_ref, o_ref, pr_ref, ps_ref):
    qv = q_ref[...]
    kcat = jnp.concatenate([kp_ref[...], kc_ref[...]], axis=0)
    vcat = jnp.concatenate([vp_ref[...], vc_ref[...]], axis=0)
    sinks = sink_ref[...]
    upper = _upper_mask()
    outs = []
    for g in range(NQ // QPG):
        sl = slice(g * HD, (g + 1) * HD)
        rows = slice(g * QPG * LB, (g + 1) * QPG * LB)
        p, es, denom = _attn_softmax(_attn_scores(n, _stack_heads(qv, g), kcat[:, sl], upper), _sink_wide(sinks, g))
        rden = 1.0 / denom
        pr_ref[0, rows, :] = (p * rden).astype(BF16)
        ps_ref[0, rows, :] = (es * rden).astype(BF16)
        outs.append((_dot(jnp.where(upper, p, 0.0).astype(BF16), vcat[0:LB, sl])
                     + _dot(jnp.where(upper, 0.0, p).astype(BF16), vcat[LB:2 * LB, sl])) * rden[:, 0:HD])
    for g, grp in enumerate(outs):
        for hh in range(QPG):
            h = QPG * g + hh
            o_ref[:, h * HD:(h + 1) * HD] = grp[hh * LB:(hh + 1) * LB, :].astype(BF16)


def _cumsum_rows(a, reverse):
    row = lax.broadcasted_iota(jnp.int32, a.shape, 0)
    step = 1
    while step < LB:
        if reverse:
            a = a + jnp.where(row < LB - step, pltpu.roll(a, LB - step, 0), 0.0)
        else:
            a = a + jnp.where(row >= step, pltpu.roll(a, step, 0), 0.0)
        step *= 2
    return a


SUB = 8


def _conv_shifts(tail, cur):
    row = lax.broadcasted_iota(jnp.int32, tail.shape, 0)
    out = [cur]
    for j in range(1, CONVK):
        rolled = pltpu.roll(cur, j, 0)
        top = jnp.where(row < j, pltpu.roll(tail, j, 0), rolled[0:SUB, :])
        out.append(jnp.concatenate([top, rolled[SUB:, :]], axis=0))
    return out


def _conv_advances(du, head):
    row = lax.broadcasted_iota(jnp.int32, head.shape, 0)
    out = []
    for j in range(1, CONVK):
        rolled = pltpu.roll(du, LB - j, 0)
        bottom = jnp.where(row >= SUB - j, pltpu.roll(head, SUB - j, 0), rolled[LB - SUB:, :])
        out.append(jnp.concatenate([rolled[:LB - SUB, :], bottom], axis=0))
    return out


def _split(v, terms):
    out = []
    for _ in range(terms - 1):
        t = v.astype(BF16)
        out.append(t)
        v = v - t.astype(F32)
    out.append(v.astype(BF16))
    return out


def _dot_sel(v, sel, terms):
    parts = [_dot(t, sel) for t in _split(v, terms)]
    return functools.reduce(lambda a, b: a + b, parts)


def _dot_nt_sel(v, sel, terms):
    parts = [_dot_nt(t, sel) for t in _split(v, terms)]
    return functools.reduce(lambda a, b: a + b, parts)


def _ssd_pre(xt_ref, xc_ref, cw_ref, cb_ref, dtr_ref, sp_ref, n):
    cur = xc_ref[...]
    tail = jnp.where(n > 0, xt_ref[...], 0.0)
    sh = _conv_shifts(tail, cur)
    u = cb_ref[...] + cw_ref[CONVK - 1:CONVK, :] * sh[0]
    for j in range(1, CONVK):
        u = u + cw_ref[CONVK - 1 - j:CONVK - j, :] * sh[j]
    dt = _softplus(dtr_ref[...] + sp_ref[0:1, :])
    acs = _cumsum_rows(dt * -jnp.exp(sp_ref[1:2, :]), False)
    return u, dt, acs


def _gated_norm_fwd(y, z, sgz, nw):
    yz = y * (z * sgz)
    parts = []
    for g in range(2):
        t = yz[:, g * 256:(g + 1) * 256]
        parts.append(t * lax.rsqrt(jnp.mean(t * t, axis=-1, keepdims=True) + EPS))
    return jnp.concatenate(parts, axis=1) * nw


HPG = 4
GW = HPG * HD


class _SsdChunk:
    def __init__(self, xc, dt, acs, spv, e64, e128, decay=None):
        self.e64, self.e128 = e64, e128
        alast = acs[LB - 1:LB, :]
        self.e_all = jnp.exp(acs)
        self.dte_all = jnp.exp(alast - acs)
        self.elast = jnp.exp(alast)
        wide = _dot_sel(jnp.concatenate([dt, self.e_all, self.dte_all], axis=0), e64, 2)
        self.dt_x, self.e_x, self.dte_x = wide[0:LB], wide[LB:2 * LB], wide[2 * LB:3 * LB]
        self.dsk_x = _dot_sel(spv, e64, 3)[2:3, :]
        if decay is None:
            acs_t = jnp.transpose(acs)
            ac_x = _dot_sel(acs, e128, 3)
            row = lax.broadcasted_iota(jnp.int32, (HPG * LB, LB), 0)
            col = lax.broadcasted_iota(jnp.int32, (HPG * LB, LB), 1)
            causal = (row % LB) >= col
        lane = lax.broadcasted_iota(jnp.int32, (LB, GW), 1)
        self.head_lanes = [(lane >= hh * HD) & (lane < (hh + 1) * HD) for hh in range(HPG)]
        self.xs, self.xdt, self.b, self.c, self.bb, self.cb16, self.cbm, self.dm_st, self.m_st = ([] for _ in range(9))
        for g in range(2):
            heads = range(HPG * g, HPG * (g + 1))
            if decay is None:
                ac_st = jnp.concatenate([ac_x[:, j * LB:(j + 1) * LB] for j in heads], axis=0)
                ar_st = jnp.concatenate([jnp.broadcast_to(acs_t[j:j + 1, :], (LB, LB)) for j in heads], axis=0)
                dm_st = jnp.exp(jnp.where(causal, ac_st - ar_st, NEG))
            else:
                dm_st = decay[g]
            bg = xc[:, SW + g * NST:SW + (g + 1) * NST]
            cg = xc[:, SW + 2 * NST + g * NST:SW + 2 * NST + (g + 1) * NST]
            bgb, cgb = bg.astype(BF16), cg.astype(BF16)
            cbm = _dot_nt(cgb, bgb)
            xs_g = xc[:, g * GW:(g + 1) * GW]
            self.xs.append(xs_g)
            self.xdt.append(xs_g * self.dt_x[:, g * GW:(g + 1) * GW])
            self.b.append(bg)
            self.c.append(cg)
            self.bb.append(bgb)
            self.cb16.append(cgb)
            self.cbm.append(cbm)
            self.dm_st.append(dm_st)
            self.m_st.append(jnp.concatenate([cbm] * HPG, axis=0) * dm_st)

    def elast_rows(self, g):
        return jnp.concatenate([jnp.broadcast_to(self.elast[:, j:j + 1], (HD, NST))
                                for j in range(HPG * g, HPG * (g + 1))], axis=0)

    def diag_blocks(self, stacked):
        out = stacked[(HPG - 1) * LB:HPG * LB, :]
        for hh in range(HPG - 2, -1, -1):
            out = jnp.where(self.head_lanes[hh], stacked[hh * LB:(hh + 1) * LB, :], out)
        return out

    def block_diag(self, v):
        return jnp.concatenate([jnp.where(self.head_lanes[hh], v, 0.0) for hh in range(HPG)], axis=0)


def _ssd_fwd_block(n, xt_ref, xc_ref, cw_ref, cb_ref, dtr_ref, sp_ref, z_ref, nw_ref, e64_ref, e128_ref,
                   yn_ref, y_ref, hs_ref, u_ref, dt_ref, acs_ref, dm_ref, sgu_ref, sgz_ref, h_scr):
    @pl.when(n == 0)
    def _():
        h_scr[...] = jnp.zeros_like(h_scr)

    h_all = h_scr[...]
    hs_ref[0] = h_all
    u, dt, acs = _ssd_pre(xt_ref, xc_ref, cw_ref, cb_ref, dtr_ref, sp_ref, n)
    u_ref[...] = u
    dt_ref[...] = dt
    acs_ref[...] = acs
    sg_u = _sigmoid(u)
    sgu_ref[...] = sg_u
    xc = u * sg_u
    ck = _SsdChunk(xc, dt, acs, sp_ref[...], e64_ref[...], e128_ref[...])
    dm_ref[0] = jnp.concatenate(ck.dm_st, axis=0)
    ys, hn = [], []
    for g in range(2):
        gl = slice(g * GW, (g + 1) * GW)
        xdt = ck.xdt[g]
        hg = h_all[gl, :]
        y_diag = ck.diag_blocks(_dot(ck.m_st[g].astype(BF16), xdt.astype(BF16)))
        y_off = ck.e_x[:, gl] * _dot_nt(ck.cb16[g], hg.astype(BF16))
        ys.append(y_diag + y_off + ck.xs[g] * ck.dsk_x[:, gl])
        hn.append(hg * ck.elast_rows(g) + _dot_tn((xdt * ck.dte_x[:, gl]).astype(BF16), ck.bb[g]))
    h_scr[...] = jnp.concatenate(hn, axis=0)
    y = jnp.concatenate(ys, axis=1)
    y_ref[...] = y
    z = z_ref[...]
    sgz = _sigmoid(z)
    sgz_ref[...] = sgz
    yn_ref[...] = _gated_norm_fwd(y, z, sgz, nw_ref[...]).astype(BF16)


def _mixer_fwd(q, k, v, sinks8, xbc, conv_w8, conv_b, dtr, ssm_p, z, nw, gathers):
    s = q.shape[0]
    nb = s // LB
    prev = lambda n: (jnp.maximum(n - 1, 0), 0)
    cur = lambda n: (n, 0)
    items, ex_shapes, n_g = _exchange_items(gathers, [])
    ne = len(items)

    n_in, n_out = 16, 12
    relay_step = (nb - 1) // 2
    stack = pl.BlockSpec((1, NH * LB, LB), lambda n: (n, 0, 0))
    e64, e128 = _head_expanders()

    def body(*refs):
        a_in, s_in, ex_in = refs[:6], refs[6:n_in], refs[n_in:n_in + ne]
        (o_ref, yn_ref, y_ref, hs_ref, u_ref, dt_ref, acs_ref, dm_ref, sgu_ref, sgz_ref, pr_ref,
         ps_ref) = refs[n_in + ne:n_in + n_out + ne]
        ex_out = refs[n_in + n_out + ne:n_in + n_out + 2 * ne]
        h_scr = refs[n_in + n_out + 2 * ne]
        sems = refs[n_in + n_out + 1 + 2 * ne:]
        n = pl.program_id(0)

        @pl.when(n == 0)
        def _():
            _Exchange(n_g, ex_in, ex_out, sems).two_level_start()

        _attn_fwd_block(n, *a_in, o_ref, pr_ref, ps_ref)
        _ssd_fwd_block(n, *s_in, yn_ref, y_ref, hs_ref, u_ref, dt_ref, acs_ref, dm_ref, sgu_ref, sgz_ref, h_scr)

        @pl.when(n == relay_step)
        def _():
            _Exchange(n_g, ex_in, ex_out, sems).two_level_relay()

        @pl.when(n == nb - 1)
        def _():
            _Exchange(n_g, ex_in, ex_out, sems).two_level_finish()

    any_spec = pl.BlockSpec(memory_space=pl.ANY)
    tail = pl.BlockSpec((SUB, D), lambda n: (jnp.maximum(n * (LB // SUB) - 1, 0), 0))
    outs = _pcall(
        body, name="mixer_fwd", grid=(nb,),
        in_specs=[pl.BlockSpec((LB, AW), cur), pl.BlockSpec((LB, KVW), prev), pl.BlockSpec((LB, KVW), cur),
                  pl.BlockSpec((LB, KVW), prev), pl.BlockSpec((LB, KVW), cur), _const((8, LB)),
                  tail, pl.BlockSpec((LB, D), cur), _const((8, D)), _const((1, D)),
                  pl.BlockSpec((LB, LB), cur), _const((8, LB)), pl.BlockSpec((LB, SW), cur), _const((1, SW)),
                  _const(e64.shape), _const(e128.shape)]
        + [any_spec] * ne,
        out_specs=[pl.BlockSpec((LB, AW), cur), pl.BlockSpec((LB, SW), cur), pl.BlockSpec((LB, SW), cur),
                   pl.BlockSpec((1, NH * HD, NST), lambda n: (n, 0, 0)), pl.BlockSpec((LB, D), cur),
                   pl.BlockSpec((LB, LB), cur), pl.BlockSpec((LB, LB), cur),
                   stack, pl.BlockSpec((LB, D), cur), pl.BlockSpec((LB, SW), cur), stack, stack] + [any_spec] * ne,
        out_shape=[_sds((s, AW), BF16), _sds((s, SW), BF16), _sds((s, SW), F32), _sds((nb, NH * HD, NST), F32),
                   _sds((s, D), F32), _sds((s, LB), F32), _sds((s, LB), F32), _sds((nb, NH * LB, LB), F32),
                   _sds((s, D), F32), _sds((s, SW), F32), _sds((nb, NH * LB, LB), BF16),
                   _sds((nb, NH * LB, LB), BF16)]
        + ex_shapes,
        scratch_shapes=[pltpu.VMEM((NH * HD, NST), F32)] + _exchange_sems(ne),
        compiler_params=_params(),
    )(q, k, k, v, v, sinks8, xbc, xbc, conv_w8, conv_b, dtr, ssm_p, z, nw, e64, e128, *items)
    return outs[:n_out], outs[n_out:]


def _head_expanders():
    j = lax.broadcasted_iota(jnp.int32, (LB, NH * HD), 0)
    e64 = (lax.broadcasted_iota(jnp.int32, (LB, NH * HD), 1) // HD == j).astype(BF16)
    j = lax.broadcasted_iota(jnp.int32, (LB, NH * LB), 0)
    e128 = (lax.broadcasted_iota(jnp.int32, (LB, NH * LB), 1) // LB == j).astype(BF16)
    return e64, e128


def _outproj_ffn_fwd_loss(attn, yn, x, tgt, mod8, n2w, fnw, w_out, w_gu_t, w_down):
    s = x.shape[0]
    tf = min(256, s)

    def body(a_ref, y_ref, x_ref, t_ref, mod_ref, nw_ref, fw_ref, wo_ref, wgu_ref, wd_ref,
             x2_ref, h2_ref, mo_ref, mix_ref, gu_ref, act_ref, dx3_ref, sm_ref):
        i = pl.program_id(0)

        @pl.when(i == 0)
        def _():
            sm_ref[...] = jnp.zeros_like(sm_ref)

        mix = jnp.concatenate([a_ref[...], y_ref[...]], axis=1)
        mix_ref[...] = mix
        mo = _dot(mix, wo_ref[...])
        mo_ref[...] = mo.astype(BF16)
        x2 = x_ref[...] + mod_ref[2:3, :] * mo
        x2_ref[...] = x2
        h2 = _norm_mod_fwd(x2, nw_ref[...], mod_ref[3:4, :], mod_ref[4:5, :]).astype(BF16)
        h2_ref[...] = h2
        gu = _dot_nt(h2, wgu_ref[...])
        gu_ref[...] = gu.astype(BF16)
        g = gu[:, :DFF]
        act = (g * _sigmoid(g) * gu[:, DFF:]).astype(BF16)
        act_ref[...] = act
        ff = _dot(act, wd_ref[...])
        x3 = x2 + mod_ref[5:6, :] * ff
        r = lax.rsqrt(jnp.mean(x3 * x3, axis=-1, keepdims=True) + EPS)
        xh = x3 * r
        fw = fw_ref[...]
        err = xh * fw - t_ref[...]
        dy = err * (1.0 / D)
        dxh = dy * fw
        dx3 = r * (dxh - xh * jnp.mean(dxh * xh, axis=-1, keepdims=True))
        dx3_ref[...] = dx3
        sm_ref[0:1, :] += jnp.sum(dx3 * ff, axis=0, keepdims=True)
        sm_ref[1:2, :] += jnp.sum(dy * xh, axis=0, keepdims=True)
        sm_ref[2:3, :] += jnp.sum(err * err, axis=0, keepdims=True)

    return _pcall(
        body, name="outproj_ffn_fwd_loss", grid=(s // tf,),
        in_specs=[_rows(tf, AW), _rows(tf, SW), _rows(tf, D), _rows(tf, D), _const((8, D)), _const((1, D)),
                  _const((1, D)), _const((D, D)), _const((2 * DFF, D)), _const((DFF, D))],
        out_specs=[_rows(tf, D), _rows(tf, D), _rows(tf, D), _rows(tf, D), _rows(tf, 2 * DFF), _rows(tf, DFF),
                   _rows(tf, D), pl.BlockSpec((8, D), lambda i: (0, 0))],
        out_shape=[_sds((s, D), F32), _sds((s, D), BF16), _sds((s, D), BF16), _sds((s, D), BF16),
                   _sds((s, 2 * DFF), BF16), _sds((s, DFF), BF16), _sds((s, D), F32), _sds((8, D), F32)],
        compiler_params=_params(),
    )(attn, yn, x, tgt, mod8, n2w, fnw, w_out, w_gu_t, w_down)


def _ffn_bwd(dx3, gu, x2, mixout, mod8, n2w, w_gu, w_down, w_out):
    s = x2.shape[0]
    tb = min(256, s)

    def body(dx3_ref, gu_ref, x2_ref, mo_ref, mod_ref, nw_ref, wgu_ref, wd_ref, wo_ref,
             dx2_ref, dff_ref, dgu_ref, dmix_ref, dattn_ref, dyn_ref, sm_ref):
        i = pl.program_id(0)

        @pl.when(i == 0)
        def _():
            sm_ref[...] = jnp.zeros_like(sm_ref)

        dx3 = dx3_ref[...]
        dff = (dx3 * mod_ref[5:6, :]).astype(BF16)
        dff_ref[...] = dff
        dact = _dot_nt(dff, wd_ref[...])
        g = gu_ref[:, :DFF].astype(F32)
        u = gu_ref[:, DFF:].astype(F32)
        sg = _sigmoid(g)
        dgu = jnp.concatenate([dact * u * sg * (1.0 + g * (1.0 - sg)), dact * g * sg], axis=1).astype(BF16)
        dgu_ref[...] = dgu
        dh2 = _dot(dgu, wgu_ref[...])
        dxn, d_shift, d_scale, d_w = _norm_mod_bwd(x2_ref[...], dh2, nw_ref[...], mod_ref[4:5, :])
        dx2 = dx3 + dxn
        dx2_ref[...] = dx2
        sm_ref[0:1, :] += d_shift
        sm_ref[1:2, :] += d_scale
        sm_ref[2:3, :] += d_w
        sm_ref[3:4, :] += jnp.sum(dx2 * mo_ref[...].astype(F32), axis=0, keepdims=True)
        dmix = (dx2 * mod_ref[2:3, :]).astype(BF16)
        dmix_ref[...] = dmix
        dmi = _dot_nt(dmix, wo_ref[...])
        dattn_ref[...] = dmi[:, :AW].astype(BF16)
        dyn_ref[...] = dmi[:, AW:]

    return _pcall(
        body, name="ffn_bwd", grid=(s // tb,),
        in_specs=[_rows(tb, D), _rows(tb, 2 * DFF), _rows(tb, D), _rows(tb, D), _const((8, D)), _const((1, D)),
                  _const((2 * DFF, D)), _const((DFF, D)), _const((D, D))],
        out_specs=[_rows(tb, D), _rows(tb, D), _rows(tb, 2 * DFF), _rows(tb, D), _rows(tb, AW), _rows(tb, SW),
                   pl.BlockSpec((8, D), lambda i: (0, 0))],
        out_shape=[_sds((s, D), F32), _sds((s, D), BF16), _sds((s, 2 * DFF), BF16), _sds((s, D), BF16),
                   _sds((s, AW), BF16), _sds((s, SW), F32), _sds((8, D), F32)],
        compiler_params=_params(),
    )(dx3, gu, x2, mixout, mod8, n2w, w_gu, w_down, w_out)


def _ssd_bwd_block(i, n, *refs):
    def run(dyn_ref, y_ref, z_ref, x_ref, u_ref, dt_ref, acs_ref, dm_ref, sgu_ref, sgz_ref, cw_ref, sp_ref, nw_ref,
            hs_ref, e64_ref, e128_ref, dzxd_ref, sm_ref, dh_scr, dun_scr):
        @pl.when(i == 0)
        def _():
            dh_scr[...] = jnp.zeros_like(dh_scr)
            dun_scr[...] = jnp.zeros_like(dun_scr)
            sm_ref[...] = jnp.zeros_like(sm_ref)

        u, dt, acs, sg_u = u_ref[...], dt_ref[...], acs_ref[...], sgu_ref[...]
        xc = u * sg_u
        a_neg = -jnp.exp(sp_ref[1:2, :])
        ck = _SsdChunk(xc, dt, acs, sp_ref[...], e64_ref[...], e128_ref[...],
                       decay=[dm_ref[0, g * HPG * LB:(g + 1) * HPG * LB, :] for g in range(2)])
        h_all = hs_ref[0]
        dh_all = dh_scr[...]
        riota = lax.broadcasted_iota(jnp.int32, (LB, LB), 0)
        lane1 = lax.broadcasted_iota(jnp.int32, (1, LB), 1)

        z = z_ref[...]
        y = y_ref[...]
        sgz = sgz_ref[...]
        sz = z * sgz
        yz = y * sz
        nwv = nw_ref[...]
        dyn_v = dyn_ref[...]
        dyhat = dyn_v * nwv
        yhat_parts, dyz_parts = [], []
        for g in range(2):
            gs = slice(g * 256, (g + 1) * 256)
            t = yz[:, gs]
            rg = lax.rsqrt(jnp.mean(t * t, axis=-1, keepdims=True) + EPS)
            yh = t * rg
            dyh = dyhat[:, gs]
            yhat_parts.append(yh)
            dyz_parts.append(rg * (dyh - yh * jnp.mean(dyh * yh, axis=-1, keepdims=True)))
        yhat = jnp.concatenate(yhat_parts, axis=1)
        dyz = jnp.concatenate(dyz_parts, axis=1)
        sm_ref[5:6, 0:SW] += jnp.sum(dyn_v * yhat, axis=0, keepdims=True)
        dy = dyz * sz
        dzxd_ref[:, 0:SW] = (dyz * y * sgz * (1.0 + z * (1.0 - sgz))).astype(BF16)

        cat = lambda parts: jnp.concatenate(parts, axis=1)
        dxs, dbs, dcs, dhp, g_cat, de_x, ddte_x, ddt_x, ddsk_x = ([] for _ in range(9))
        dacs_t = jnp.zeros((LB, LB), F32)
        hsum = jnp.zeros((1, LB), F32)
        for g in range(2):
            gl = slice(g * GW, (g + 1) * GW)
            xs_g, xdt, bgb, cgb = ck.xs[g], ck.xdt[g], ck.bb[g], ck.cb16[g]
            m_st, dm_st = ck.m_st[g], ck.dm_st[g]
            dt_x, e_x, dte_x = ck.dt_x[:, gl], ck.e_x[:, gl], ck.dte_x[:, gl]
            xdtb = xdt.astype(BF16)
            hg, dhn = h_all[gl, :], dh_all[gl, :]
            hb, dhnb = hg.astype(BF16), dhn.astype(BF16)
            dy_g = dy[:, gl]
            ddsk_x.append(jnp.sum(dy_g * xs_g, axis=0, keepdims=True))
            dy_bd = ck.block_diag(dy_g).astype(BF16)
            dm4 = _dot_nt(dy_bd, xdtb)
            dxdt = _dot_tn(m_st.astype(BF16), dy_bd)
            gmat = dm4 * m_st
            dcbm = dm4 * dm_st
            dcb = dcbm[0:LB] + dcbm[LB:2 * LB] + dcbm[2 * LB:3 * LB] + dcbm[3 * LB:4 * LB]
            g_cat.append(cat([gmat[hh * LB:(hh + 1) * LB, :] for hh in range(HPG)]))
            for hh in range(HPG):
                j = HPG * g + hh
                col_sum = jnp.sum(gmat[hh * LB:(hh + 1) * LB, :], axis=0, keepdims=True)
                dacs_t = dacs_t - jnp.where(riota == j, col_sum, 0.0)
                hsl = slice(hh * HD, (hh + 1) * HD)
                hsum = hsum + jnp.where(lane1 == j, jnp.sum(dhn[hsl, :] * hg[hsl, :]), 0.0)
            dchb = (dy_g * e_x).astype(BF16)
            dcg = _dot(dchb, hb)
            dh_prev = _dot_tn(dchb, cgb)
            de_x.append(dy_g * _dot_nt(cgb, hb))
            dxs_s = _dot_nt(bgb, dhnb)
            dbg = _dot((xdt * dte_x).astype(BF16), dhnb)
            dxdt = dxdt + dxs_s * dte_x
            ddte_x.append(dxs_s * xdt)
            dhp.append(dhn * ck.elast_rows(g) + dh_prev)
            dxs.append(dy_g * ck.dsk_x[:, gl] + dxdt * dt_x)
            ddt_x.append(dxdt * xs_g)
            dcbb = dcb.astype(BF16)
            dbs.append(dbg + _dot_tn(dcbb, cgb))
            dcs.append(dcg + _dot(dcbb, bgb))
        dh_scr[...] = jnp.concatenate(dhp, axis=0)
        red = _dot_nt_sel(jnp.concatenate([cat(de_x), cat(ddte_x), cat(ddt_x)], axis=0), ck.e64, 1)
        de_c, ddte_c, ddt_c = red[0:LB], red[LB:2 * LB], red[2 * LB:3 * LB]
        ddsk = _dot_nt_sel(jnp.broadcast_to(cat(ddsk_x), (SUB, NH * HD)), ck.e64, 2)[0:1, :]
        t1 = ddte_c * ck.dte_all
        dalast = jnp.sum(t1, axis=0, keepdims=True) + hsum * ck.elast
        dacs = (_dot_nt_sel(cat(g_cat), ck.e128, 2) + de_c * ck.e_all - t1 + jnp.transpose(dacs_t)
                + jnp.where(riota == LB - 1, dalast, 0.0))
        da = _cumsum_rows(dacs, True)
        ddt = ddt_c + da * a_neg
        da_log = jnp.sum(da * dt, axis=0, keepdims=True) * a_neg
        ddtr = ddt * (1.0 - jnp.exp(-dt))
        dzxd_ref[:, SW + D:ZXD] = ddtr.astype(BF16)
        sm_ref[6:7, 0:LB] += jnp.sum(ddtr, axis=0, keepdims=True)
        sm_ref[6:7, LB:2 * LB] += da_log
        sm_ref[6:7, 2 * LB:3 * LB] += ddsk

        du = cat(dxs + dbs + dcs) * (sg_u * (1.0 + u * (1.0 - sg_u)))
        xv = x_ref[...]
        adv = [du] + _conv_advances(du, dun_scr[...])
        sm_ref[0:1, :] += jnp.sum(du, axis=0, keepdims=True)
        dxbc = cw_ref[CONVK - 1:CONVK, :] * du
        for j in range(CONVK):
            sm_ref[CONVK - j:CONVK + 1 - j, :] += jnp.sum(adv[j] * xv, axis=0, keepdims=True)
            if j:
                dxbc = dxbc + cw_ref[CONVK - 1 - j:CONVK - j, :] * adv[j]
        dun_scr[...] = du[0:SUB, :]
        dzxd_ref[:, SW:SW + D] = dxbc.astype(BF16)

    run(*refs)


def _attn_bwd_block(i, n, q_ref, kp_ref, kc_ref, vp_ref, vc_ref, o_ref, do_ref, cos_ref, sin_ref, pr_ref, ps_ref,
                    dq_ref, dkv_ref, ds_ref, ck_scr, cv_scr):
    @pl.when(i == 0)
    def _():
        ds_ref[...] = jnp.zeros_like(ds_ref)
        ck_scr[...] = jnp.zeros_like(ck_scr)
        cv_scr[...] = jnp.zeros_like(cv_scr)

    qv, ov, dov = q_ref[...], o_ref[...], do_ref[...]
    kcat = jnp.concatenate([kp_ref[...], kc_ref[...]], axis=0)
    vcat = jnp.concatenate([vp_ref[...], vc_ref[...]], axis=0)
    upper = _upper_mask()
    srow = lax.broadcasted_iota(jnp.int32, (8, LB), 0)
    slane = lax.broadcasted_iota(jnp.int32, (8, LB), 1)
    dsink = jnp.zeros((8, LB), F32)
    dq_g, dk_g, dv_g = [], [], []
    for g in range(NQ // QPG):
        sl = slice(g * HD, (g + 1) * HD)
        qg = _stack_heads(qv, g)
        dog = _stack_heads(dov, g)
        rows = slice(g * QPG * LB, (g + 1) * QPG * LB)
        probs = pr_ref[0, rows, :].astype(F32)
        psink = ps_ref[0, rows, :].astype(F32)
        delta = _row_sums_wide(dog.astype(F32) * _stack_heads(ov, g).astype(F32), 2)
        dsc = probs * (_band(upper, _dot_nt(dog, vcat[0:LB, sl]), _dot_nt(dog, vcat[LB:2 * LB, sl])) - delta)
        sink_terms = (psink * delta)[:, 0:1]
        for hh in range(QPG):
            dsink = dsink - jnp.where((srow == QPG * g + hh) & (slane == 0),
                                      jnp.sum(sink_terms[hh * LB:(hh + 1) * LB, :]), 0.0)
        ds_p = jnp.where(upper, dsc, 0.0).astype(BF16)
        ds_c = jnp.where(upper, 0.0, dsc).astype(BF16)
        dq_g.append((_dot(ds_p, kcat[0:LB, sl]) + _dot(ds_c, kcat[LB:2 * LB, sl])) * ATT_SCALE)
        dk_g.append(jnp.concatenate([_dot_tn(ds_p, qg), _dot_tn(ds_c, qg)], axis=0) * ATT_SCALE)
        dv_g.append(jnp.concatenate([_dot_tn(jnp.where(upper, probs, 0.0).astype(BF16), dog),
                                     _dot_tn(jnp.where(upper, 0.0, probs).astype(BF16), dog)], axis=0))
    ds_ref[...] += dsink
    cs = cos_ref[...]
    sn = sin_ref[...]
    dk2 = jnp.concatenate(dk_g, axis=1)
    dv2 = jnp.concatenate(dv_g, axis=1)
    for a, tile in enumerate(_unstack_heads(dq_g)):
        dq_ref[:, a * LB:(a + 1) * LB] = _rope(tile, cs, sn, True).astype(BF16)
    dkv_ref[:, 0:KVW] = _rope(ck_scr[...] + dk2[LB:2 * LB, :], cs, sn, True).astype(BF16)
    dkv_ref[:, KVW:2 * KVW] = (cv_scr[...] + dv2[LB:2 * LB, :]).astype(BF16)
    ck_scr[...] = dk2[0:LB, :]
    cv_scr[...] = dv2[0:LB, :]


def _mixer_bwd(q, k, v, o, do, cos, sin, probs, psink, dyn, y, z, xbc, u, dtv, acs, decay, sgu, sgz, conv_w8, ssm_p, nw,
               hs, scatters):
    s = q.shape[0]
    nb = s // LB
    cur = lambda i: (nb - 1 - i, 0)
    prev = lambda i: (jnp.maximum(nb - 2 - i, 0), 0)
    n_attn, n_in = 11, 27
    items, ex_shapes, n_g = _exchange_items([], scatters)
    ne = len(items)
    e64, e128 = _head_expanders()
    stack = pl.BlockSpec((1, NH * LB, LB), lambda i: (nb - 1 - i, 0, 0))

    def body(*refs):
        i = pl.program_id(0)
        n = nb - 1 - i
        a_in, s_in, ex_in = refs[:n_attn], refs[n_attn:n_in], refs[n_in:n_in + ne]
        dp_ref, ds_ref, sm_ref = refs[n_in + ne:n_in + ne + 3]
        dq_ref, dkv_ref, dzxd_ref = dp_ref.at[:, O_Q:O_K], dp_ref.at[:, O_K:O_Z], dp_ref.at[:, O_Z:INP]
        ex_out = refs[n_in + ne + 3:n_in + 2 * ne + 3]
        ck_scr, cv_scr, dh_scr, dun_scr = refs[n_in + 2 * ne + 3:n_in + 2 * ne + 7]
        sems = refs[n_in + 2 * ne + 7:]

        @pl.when(i == 0)
        def _():
            _Exchange(n_g, ex_in, ex_out, sems).start()

        _attn_bwd_block(i, n, *a_in, dq_ref, dkv_ref, ds_ref, ck_scr, cv_scr)
        _ssd_bwd_block(i, n, *s_in, dzxd_ref, sm_ref, dh_scr, dun_scr)

        @pl.when(i == nb - 1)
        def _():
            _Exchange(n_g, ex_in, ex_out, sems).finish()

    any_spec = pl.BlockSpec(memory_space=pl.ANY)
    outs = _pcall(
        body, name="mixer_bwd", grid=(nb,),
        in_specs=[pl.BlockSpec((LB, AW), cur), pl.BlockSpec((LB, KVW), prev), pl.BlockSpec((LB, KVW), cur),
                  pl.BlockSpec((LB, KVW), prev), pl.BlockSpec((LB, KVW), cur), pl.BlockSpec((LB, AW), cur),
                  pl.BlockSpec((LB, AW), cur), pl.BlockSpec((LB, LB), cur), pl.BlockSpec((LB, LB), cur),
                  stack, stack,
                  pl.BlockSpec((LB, SW), cur), pl.BlockSpec((LB, SW), cur), pl.BlockSpec((LB, SW), cur),
                  pl.BlockSpec((LB, D), cur), pl.BlockSpec((LB, D), cur), pl.BlockSpec((LB, LB), cur),
                  pl.BlockSpec((LB, LB), cur), stack, pl.BlockSpec((LB, D), cur), pl.BlockSpec((LB, SW), cur),
                  _const((8, D)), _const((8, LB)), _const((1, SW)),
                  pl.BlockSpec((1, NH * HD, NST), lambda i: (nb - 1 - i, 0, 0)),
                  _const(e64.shape), _const(e128.shape)] + [any_spec] * ne,
        out_specs=[pl.BlockSpec((LB, INP), cur), pl.BlockSpec((8, LB), lambda i: (0, 0)),
                   pl.BlockSpec((8, D), lambda i: (0, 0))] + [any_spec] * ne,
        out_shape=[_sds((s, INP), BF16), _sds((8, LB), F32), _sds((8, D), F32)] + ex_shapes,
        scratch_shapes=[pltpu.VMEM((LB, KVW), F32), pltpu.VMEM((LB, KVW), F32),
                        pltpu.VMEM((NH * HD, NST), F32), pltpu.VMEM((SUB, D), F32)]
        + _exchange_sems(ne),
        compiler_params=_params(),
    )(q, k, k, v, v, o, do, cos, sin, probs, psink, dyn, y, z, xbc, u, dtv, acs, decay, sgu, sgz, conv_w8, ssm_p, nw, hs,
      e64, e128, *items)
    return outs[0], outs[1], outs[2], outs[3:]


def _inproj_bwd(dproj, x, dx2, mod8, n1w, w_in_t, scatters, smalls):
    s = x.shape[0]
    tt = min(512, s)
    nt = s // tt
    items, ex_shapes, n_g = _exchange_items([], scatters)
    ne = len(items)
    n_in = 10

    def body(*refs):
        dp_ref, x_ref, dx2_ref, mod_ref, nw_ref, w_ref, f_ref, b_ref, s_ref, k_ref = refs[:n_in]
        ex_in = refs[n_in:n_in + ne]
        gx_ref, sm_ref = refs[n_in + ne:n_in + 2 + ne]
        ex_out = refs[n_in + 2 + ne:n_in + 2 + 2 * ne]
        gpack_ref = refs[n_in + 2 + 2 * ne]
        pack_scr = refs[n_in + 3 + 2 * ne]
        sems = refs[n_in + 4 + 2 * ne:n_in + 7 + 2 * ne]
        pack_sems = refs[n_in + 7 + 2 * ne:]
        i = pl.program_id(0)

        @pl.when(i == 0)
        def _():
            sm_ref[...] = jnp.zeros_like(sm_ref)
            _Exchange(n_g, ex_in, ex_out, sems).start()

        w = w_ref[...]
        hr = tt // 2
        dh1 = [_dot(dp_ref[h * hr:(h + 1) * hr, :], w) for h in range(2)]
        sums = jnp.zeros((3, D), F32)
        for h in range(2):
            rows = slice(h * hr, (h + 1) * hr)
            dxn, d_shift, d_scale, d_w = _norm_mod_bwd(x_ref[rows, :], dh1[h], nw_ref[...], mod_ref[1:2, :])
            gx_ref[rows, :] = dx2_ref[rows, :] + dxn
            sums = sums + jnp.concatenate([d_shift, d_scale, d_w], axis=0)
        sm_ref[0:3, :] += sums

        @pl.when(i == nt - 1)
        def _():
            _pack_rows(f_ref, b_ref, s_ref, sm_ref, k_ref, pack_scr)
            small = _Exchange(1, [pack_scr], [gpack_ref], pack_sems)
            small.start()
            _Exchange(n_g, ex_in, ex_out, sems).finish()
            small.finish()

    any_spec = pl.BlockSpec(memory_space=pl.ANY)
    outs = _pcall(
        body, name="inproj_bwd", grid=(nt,),
        in_specs=[_rows(tt, INP), _rows(tt, D), _rows(tt, D), _const((8, D)), _const((1, D)), _const((INP, D)),
                  _const((8, D)), _const((8, D)), _const((8, D)), _const((8, LB))]
        + [any_spec] * ne,
        out_specs=[_rows(tt, D), pl.BlockSpec((8, D), lambda i: (0, 0))] + [any_spec] * (ne + 1),
        out_shape=[_sds((s, D), F32), _sds((8, D), F32)] + ex_shapes + [_sds((N_DEV, PACK_ROWS, D), F32)],
        scratch_shapes=[pltpu.VMEM((PACK_ROWS, D), F32)] + _exchange_sems(ne) + _exchange_sems(1),
        compiler_params=_params(),
    )(dproj, x, dx2, mod8, n1w, w_in_t, *smalls, *items)
    return outs[0], outs[2:2 + ne], outs[2 + ne]


def _wgrad(a, b, name):
    s, m = a.shape
    n = b.shape[1]
    tk = min(2048, s)
    wide = (1408, 1024, 512)
    tm = next((t for t in wide if m % t == 0), m)
    tn = n if n <= 2048 else _largest_divisor(n, wide)
    nk = s // tk

    def body(a_ref, b_ref, o_ref, acc):
        kk = pl.program_id(2)

        @pl.when(kk == 0)
        def _():
            acc[...] = jnp.zeros_like(acc)

        acc[...] += _dot_tn(a_ref[...], b_ref[...])

        @pl.when(kk == nk - 1)
        def _():
            o_ref[...] = acc[...].astype(BF16)

    return _pcall(
        body, name=name, grid=(m // tm, n // tn, nk),
        in_specs=[pl.BlockSpec((tk, tm), lambda i, j, kk: (kk, i)), pl.BlockSpec((tk, tn), lambda i, j, kk: (kk, j))],
        out_specs=pl.BlockSpec((tm, tn), lambda i, j, kk: (i, j)),
        out_shape=_sds((m, n), BF16),
        scratch_shapes=[pltpu.VMEM((tm, tn), F32)],
        compiler_params=_params(3),
    )(a, b)


PACK_ROWS = 24


def _pack_rows(f_ref, b_ref, s_ref, i_ref, k_ref, o_ref):
    o_ref[...] = jnp.zeros_like(o_ref)
    o_ref[0:2, :] = i_ref[0:2, :]
    o_ref[2:3, :] = b_ref[3:4, :]
    o_ref[3:5, :] = b_ref[0:2, :]
    o_ref[5:6, :] = f_ref[0:1, :]
    o_ref[6:7, :] = i_ref[2:3, :]
    o_ref[7:8, :] = b_ref[2:3, :]
    o_ref[8:9, :] = f_ref[1:2, :]
    o_ref[9:14, :] = s_ref[0:5, :]
    o_ref[14:15, :] = s_ref[5:6, :]
    o_ref[15:16, 0:3 * LB] = s_ref[6:7, 0:3 * LB]
    lane = lax.broadcasted_iota(jnp.int32, (1, LB), 1)
    sk = jnp.zeros((1, LB), F32)
    for h in range(NQ):
        sk = sk + jnp.where(lane == h, k_ref[h:h + 1, 0:1], 0.0)
    o_ref[15:16, 3 * LB:4 * LB] = sk
    o_ref[16:17, :] = f_ref[2:3, :]


def _exchange(gathers, scatters, name, two_level=False):
    items, shapes, n_g = _exchange_items(gathers, scatters)
    n = len(items)
    assert not (two_level and scatters)

    def body(*refs):
        ex = _Exchange(n_g, refs[:n], refs[n:2 * n], refs[2 * n:])
        if two_level:
            ex.gather_two_level()
        else:
            ex.start()
            ex.finish()

    any_spec = pl.BlockSpec(memory_space=pl.ANY)
    return _pcall(
        body, name=name, in_specs=[any_spec] * n, out_specs=[any_spec] * n, out_shape=shapes,
        scratch_shapes=_exchange_sems(n),
    )(*items)


def _exchange_items(gathers, scatters):
    items = list(gathers) + list(scatters)
    shapes = [_sds((N_DEV,) + a.shape, a.dtype) for a in gathers] + [_sds(a.shape, a.dtype) for a in scatters]
    return items, shapes, len(gathers)


def _exchange_sems(n):
    return [pltpu.SemaphoreType.DMA((n, N_DEV - 1)), pltpu.SemaphoreType.DMA((n, N_DEV - 1)),
            pltpu.SemaphoreType.DMA((n,))]


class _Exchange:
    def __init__(self, n_g, ins, outs, sems):
        self.n_g, self.ins, self.outs = n_g, ins, outs
        self.send_sems, self.recv_sems, self.loc_sems = sems
        xi, yi, ci = lax.axis_index("x"), lax.axis_index("y"), lax.axis_index("c")
        self.me = 4 * xi + 2 * yi + ci
        self.peers = []
        for r in range(1, N_DEV):
            px = 1 - xi if r & 4 else xi
            py = 1 - yi if r & 2 else yi
            pc = 1 - ci if r & 1 else ci
            self.peers.append(((px, py, pc), 4 * px + 2 * py + pc))

    def _copy(self, t, r, landing):
        dev, peer = self.peers[r]
        src = self.ins[t] if t < self.n_g else self.ins[t].at[peer]
        return pltpu.make_async_remote_copy(
            src_ref=src, dst_ref=self.outs[t].at[landing], send_sem=self.send_sems.at[t, r],
            recv_sem=self.recv_sems.at[t, r], device_id=dev, device_id_type=pl.DeviceIdType.MESH)

    def _local(self, t):
        src = self.ins[t] if t < self.n_g else self.ins[t].at[self.me]
        return pltpu.make_async_copy(src, self.outs[t].at[self.me], self.loc_sems.at[t])

    def start(self):
        for t in range(len(self.ins)):
            self._local(t).start()
            for r in range(N_DEV - 1):
                self._copy(t, r, self.me).start()

    def finish(self):
        n = len(self.ins)
        for t in range(n):
            for r in range(N_DEV - 1):
                self._copy(t, r, self.peers[r][1]).wait_recv()
        for t in range(n):
            for r in range(N_DEV - 1):
                self._copy(t, r, self.me).wait_send()
            self._local(t).wait()

    def gather_two_level(self):
        self.two_level_start()
        self.two_level_relay()
        self.two_level_finish()

    DIRECT = (0, 1, 3, 5)

    def two_level_start(self):
        for t in range(len(self.ins)):
            self._local(t).start()
            for r in self.DIRECT:
                self._copy(t, r, self.me).start()

    def _relay(self, t, r):
        peer = self.peers[r][1]
        return pltpu.make_async_remote_copy(
            src_ref=self.outs[t].at[peer], dst_ref=self.outs[t].at[peer], send_sem=self.send_sems.at[t, r + 1],
            recv_sem=self.recv_sems.at[t, r + 1], device_id=self.peers[0][0], device_id_type=pl.DeviceIdType.MESH)

    def two_level_relay(self):
        for t in range(len(self.ins)):
            for r in self.DIRECT[1:]:
                self._copy(t, r, self.peers[r][1]).wait_recv()
                self._relay(t, r).start()

    def two_level_finish(self):
        n = len(self.ins)
        for t in range(n):
            for r in (0, 2, 4, 6):
                self._copy(t, r, self.peers[r][1]).wait_recv()
        for t in range(n):
            for r in self.DIRECT:
                self._copy(t, r, self.me).wait_send()
            for r in self.DIRECT[1:]:
                self._relay(t, r).wait_send()
            self._local(t).wait()


def _ada_fwd(c_all, w_cols, b_cols):
    def body(c_ref, w_ref, b_ref, o_ref):
        cv = c_ref[...]
        sc = (cv * _sigmoid(cv)).astype(BF16)
        o_ref[...] = _dot(sc, w_ref[...].astype(BF16)) + b_ref[...]

    return _pcall(body, name="ada_fwd", out_shape=_sds((N_DEV, w_cols.shape[1]), F32),
                  compiler_params=_params(0))(c_all, w_cols, b_cols)


def _adamw(w, g, m, v):
    m2 = ADAM_B1 * m + (1.0 - ADAM_B1) * g
    v2 = ADAM_B2 * v + (1.0 - ADAM_B2) * (g * g)
    m_hat = m2 / (1.0 - ADAM_B1 ** ADAM_STEP)
    v_hat = v2 / (1.0 - ADAM_B2 ** ADAM_STEP)
    delta = -ADAM_LR * (m_hat / (jnp.sqrt(v_hat) + ADAM_EPS) + ADAM_WD * w)
    return delta, m2, v2


def _sum_adamw(parts, w, m, v, name):
    rws, cols = w.shape
    tr = next((t for t in (256, 176, 128) if rws % t == 0), rws)

    def body(p_ref, w_ref, m_ref, v_ref, g_ref, d_ref, mo_ref, vo_ref):
        g = p_ref[0].astype(F32)
        for dev in range(1, N_DEV):
            g = g + p_ref[dev].astype(F32)
        g_ref[...] = g
        d_ref[...], mo_ref[...], vo_ref[...] = _adamw(w_ref[...], g, m_ref[...], v_ref[...])

    blk = pl.BlockSpec((tr, cols), lambda i: (i, 0))
    return _pcall(
        body, name=name, grid=(rws // tr,),
        in_specs=[pl.BlockSpec((N_DEV, tr, cols), lambda i: (0, i, 0)), blk, blk, blk],
        out_specs=[blk] * 4, out_shape=[_sds((rws, cols), F32)] * 4, compiler_params=_params(),
    )(parts, w, m, v)


def _wada_adamw(c_all, dmod_cols, w, m, v):
    rws, cols = w.shape
    tr = 256

    def body(c_ref, dm_ref, w_ref, m_ref, v_ref, g_ref, d_ref, mo_ref, vo_ref):
        cv = c_ref[...]
        sc = (cv * _sigmoid(cv)).astype(BF16)
        g = _dot_tn(sc, dm_ref[...].astype(BF16))
        g_ref[...] = g
        d_ref[...], mo_ref[...], vo_ref[...] = _adamw(w_ref[...], g, m_ref[...], v_ref[...])

    blk = pl.BlockSpec((tr, cols), lambda i: (i, 0))
    return _pcall(
        body, name="wada_adamw", grid=(rws // tr,),
        in_specs=[pl.BlockSpec((N_DEV, tr), lambda i: (0, i)), pl.BlockSpec((N_DEV, cols), lambda i: (0, 0)),
                  blk, blk, blk],
        out_specs=[blk] * 4, out_shape=[_sds((rws, cols), F32)] * 4, compiler_params=_params(),
    )(c_all, dmod_cols, w, m, v)


def _small_reduce(packs):
    def body(p_ref, o_ref):
        tot = p_ref[0]
        for dev in range(1, N_DEV):
            tot = tot + p_ref[dev]
        o_ref[...] = tot
        o_ref[16:17, :] = jnp.zeros((1, D), F32) + (0.5 / D) * jnp.sum(tot[16:17, :])

    return _pcall(body, name="small_reduce", out_shape=_sds((PACK_ROWS, D), F32))(packs)


def _adamw_many(ws, gs, ms, vs):
    k = len(ws)

    def body(*refs):
        for i in range(k):
            w_ref, g_ref, m_ref, v_ref = refs[i], refs[k + i], refs[2 * k + i], refs[3 * k + i]
            d_ref, mo_ref, vo_ref = refs[4 * k + i], refs[5 * k + i], refs[6 * k + i]
            d_ref[...], mo_ref[...], vo_ref[...] = _adamw(w_ref[...], g_ref[...], m_ref[...], v_ref[...])

    shp = [_sds(w.shape, F32) for w in ws]
    outs = _pcall(body, name="adamw_small", out_shape=shp * 3)(*ws, *gs, *ms, *vs)
    return outs[:k], outs[k:2 * k], outs[2 * k:]


def kernel(x, c, positions, w_ada, b_ada, norm1_w, w_in, conv_w, conv_b, dt_bias, a_log, d_skip, attn_sinks, ssm_norm_w, w_out, norm2_w, w_gate_up, w_down, final_norm_w, loss_target, m_w_ada, m_b_ada, m_norm1_w, m_w_in, m_conv_w, m_conv_b, m_dt_bias, m_a_log, m_d_skip, m_attn_sinks, m_ssm_norm_w, m_w_out, m_norm2_w, m_w_gate_up, m_w_down, m_final_norm_w, v_w_ada, v_b_ada, v_norm1_w, v_w_in, v_conv_w, v_conv_b, v_dt_bias, v_a_log, v_d_skip, v_attn_sinks, v_ssm_norm_w, v_w_out, v_norm2_w, v_w_gate_up, v_w_down, v_final_norm_w):
    s = x.shape[1]
    me = 4 * lax.axis_index("x") + 2 * lax.axis_index("y") + lax.axis_index("c")
    ada_cols = N_MOD * D // N_DEV

    c8 = jnp.pad(c, ((0, 7), (0, 0)))
    cw8 = jnp.pad(conv_w[0], ((0, 8 - CONVK), (0, 0)))
    w_in_t, m_w_in_t, v_w_in_t = jnp.transpose(w_in[0]), jnp.transpose(m_w_in[0]), jnp.transpose(v_w_in[0])
    w_gu_t, m_w_gu_t, v_w_gu_t = (jnp.transpose(w_gate_up[0]), jnp.transpose(m_w_gate_up[0]),
                                  jnp.transpose(v_w_gate_up[0]))
    g_c, g_in, g_cw = _exchange([c8, w_in_t.astype(BF16), cw8], [], "gather_in", two_level=True)
    c_all = g_c[:, 0, :]
    w_in_f = jnp.pad(g_in.reshape(IN_PROJ, D), ((0, INP - IN_PROJ), (0, 0)))
    conv_w8 = jnp.transpose(g_cw, (1, 0, 2)).reshape(8, D)

    b_cols = lax.dynamic_slice(b_ada, (0, me * ada_cols), (1, ada_cols))
    (g_mod,) = _exchange([_ada_fwd(c_all, w_ada[0], b_cols)], [], "gather_mod")
    mod = lax.dynamic_index_in_dim(g_mod, me, axis=1, keepdims=False).reshape(N_MOD, D)
    mod8 = jnp.pad(mod, ((0, 8 - N_MOD), (0, 0)))

    half = HD // 2
    inv_freq = ROPE_THETA ** (-jnp.arange(half, dtype=F32) / half)
    invf = jnp.tile(inv_freq, LB // half).reshape(1, LB)
    lanes = lambda a: jnp.pad(a, ((0, 0), (0, LB - a.shape[1])))
    ssm_p = jnp.pad(jnp.concatenate([lanes(dt_bias), lanes(a_log), lanes(d_skip)], axis=0), ((0, 5), (0, 0)))
    sinks8 = jnp.broadcast_to(attn_sinks.reshape(NQ, 1), (NQ, LB))

    xs, tgt, fnw = x[0], loss_target[0], final_norm_w.reshape(1, D)

    q, k, v, z, xbc, dtr, h1, cos, sin = _inproj_fwd(xs, positions[0].reshape(s, 1), invf, mod8, norm1_w, w_in_f)
    (attn, yn, y, hs, conv_u, dtv, acs, decay, sgu, sgz, probs, psink), (g_out, g_gu, g_down) = _mixer_fwd(
        q, k, v, sinks8, xbc, conv_w8, conv_b, dtr, ssm_p, z, ssm_norm_w,
        [w_out[0].astype(BF16), w_gu_t.astype(BF16), w_down[0].astype(BF16)])
    w_out_f = g_out.reshape(D, D)
    w_gu_f = g_gu.reshape(2 * DFF, D)
    w_down_f = g_down.reshape(DFF, D)
    x2, h2, mo, mix, gu, act, dx3, sm_f = _outproj_ffn_fwd_loss(attn, yn, xs, tgt, mod8, norm2_w, fnw, w_out_f, w_gu_f,
                                                                 w_down_f)

    dx2, dff, dgu, dmix, dattn, dyn, sm_b = _ffn_bwd(dx3, gu, x2, mo, mod8, norm2_w, w_gu_f, w_down_f, w_out_f)
    p_gu = _wgrad(dgu, h2, "wgrad_gate_up").reshape(N_DEV, 2 * DFF // N_DEV, D)
    p_down = _wgrad(act, dff, "wgrad_down").reshape(N_DEV, DFF // N_DEV, D)
    p_out = _wgrad(mix, dmix, "wgrad_out").reshape(N_DEV, D // N_DEV, D)
    dproj, dsink, sm_s, (r_gu, r_down, r_out) = _mixer_bwd(
        q, k, v, attn, dattn, cos, sin, probs, psink, dyn, y, z, xbc, conv_u, dtv, acs, decay, sgu, sgz, conv_w8, ssm_p,
        ssm_norm_w, hs, [p_gu, p_down, p_out])
    p_in = _wgrad(dproj, h1, "wgrad_in")[:IN_PROJ].reshape(N_DEV, IN_PROJ // N_DEV, D)
    gx, (r_in,), g_pack = _inproj_bwd(dproj, xs, dx2, mod8, norm1_w, w_in_f, [p_in], (sm_f, sm_b, sm_s, dsink))

    tot = _small_reduce(g_pack)
    loss = tot[16, 0]
    dmod_all = g_pack[:, 0:N_MOD, :].reshape(N_DEV, N_MOD * D)
    dmod_cols = lax.dynamic_slice(dmod_all, (0, me * ada_cols), (N_DEV, ada_cols))

    big = {
        "w_ada": _wada_adamw(c_all, dmod_cols, w_ada[0], m_w_ada[0], v_w_ada[0]),
        "w_in": [jnp.transpose(t) for t in _sum_adamw(r_in, w_in_t, m_w_in_t, v_w_in_t, "adamw_in")],
        "w_out": _sum_adamw(r_out, w_out[0], m_w_out[0], v_w_out[0], "adamw_out"),
        "w_gate_up": [jnp.transpose(t) for t in _sum_adamw(r_gu, w_gu_t, m_w_gu_t, v_w_gu_t, "adamw_gate_up")],
        "w_down": _sum_adamw(r_down, w_down[0], m_w_down[0], v_w_down[0], "adamw_down"),
    }
    small_names = ["b_ada", "norm1_w", "conv_w", "conv_b", "dt_bias", "a_log", "d_skip", "attn_sinks", "ssm_norm_w",
                   "norm2_w", "final_norm_w"]
    row15 = tot[15:16, :]
    small_g = {
        "b_ada": tot[0:N_MOD, :].reshape(1, N_MOD * D),
        "norm1_w": tot[6:7, :],
        "conv_w": lax.dynamic_slice(tot[10:14, :], (0, me * (D // N_DEV)), (CONVK, D // N_DEV)),
        "conv_b": tot[9:10, :],
        "dt_bias": row15[:, 0:NH],
        "a_log": row15[:, LB:LB + NH],
        "d_skip": row15[:, 2 * LB:2 * LB + NH],
        "attn_sinks": row15[:, 3 * LB:3 * LB + NQ],
        "ssm_norm_w": tot[14:15, 0:SW],
        "norm2_w": tot[7:8, :],
        "final_norm_w": tot[8:9, :],
    }
    small_w = {"b_ada": b_ada, "norm1_w": norm1_w, "conv_w": conv_w[0], "conv_b": conv_b, "dt_bias": dt_bias,
               "a_log": a_log, "d_skip": d_skip, "attn_sinks": attn_sinks, "ssm_norm_w": ssm_norm_w,
               "norm2_w": norm2_w, "final_norm_w": final_norm_w.reshape(1, D)}
    small_m = {"b_ada": m_b_ada, "norm1_w": m_norm1_w, "conv_w": m_conv_w[0], "conv_b": m_conv_b,
               "dt_bias": m_dt_bias, "a_log": m_a_log, "d_skip": m_d_skip, "attn_sinks": m_attn_sinks,
               "ssm_norm_w": m_ssm_norm_w, "norm2_w": m_norm2_w, "final_norm_w": m_final_norm_w.reshape(1, D)}
    small_v = {"b_ada": v_b_ada, "norm1_w": v_norm1_w, "conv_w": v_conv_w[0], "conv_b": v_conv_b,
               "dt_bias": v_dt_bias, "a_log": v_a_log, "d_skip": v_d_skip, "attn_sinks": v_attn_sinks,
               "ssm_norm_w": v_ssm_norm_w, "norm2_w": v_norm2_w, "final_norm_w": v_final_norm_w.reshape(1, D)}
    s_d, s_m, s_v = _adamw_many([small_w[k] for k in small_names], [small_g[k] for k in small_names],
                                [small_m[k] for k in small_names], [small_v[k] for k in small_names])

    order = ["w_ada", "b_ada", "norm1_w", "w_in", "conv_w", "conv_b", "dt_bias", "a_log", "d_skip", "attn_sinks",
             "ssm_norm_w", "w_out", "norm2_w", "w_gate_up", "w_down", "final_norm_w"]
    lead = {"w_ada", "w_in", "conv_w", "w_out", "w_gate_up", "w_down"}
    grads, deltas, new_m, new_v = [], [], [], []
    for name in order:
        if name in big:
            g, d, m2, v2 = big[name]
        else:
            i = small_names.index(name)
            g, d, m2, v2 = small_g[name], s_d[i], s_m[i], s_v[i]
        if name in lead:
            g, d, m2, v2 = g[None], d[None], m2[None], v2[None]
        if name == "final_norm_w":
            g, d, m2, v2 = g.reshape(D), d.reshape(D), m2.reshape(D), v2.reshape(D)
        grads.append(g)
        deltas.append(d)
        new_m.append(m2)
        new_v.append(v2)
    return (loss, gx[None], *grads, *deltas, *new_m, *new_v)
```

```python
import functools
import math

import jax
import jax.numpy as jnp
from jax import lax
from jax.experimental import pallas as pl
from jax.experimental.pallas import tpu as pltpu

F32 = jnp.float32
BF16 = jnp.bfloat16

N_DEV = 8
D = 1024
HD = 64
NQ = 8
AW = 512
KVW = 128
SW = 512
NST = 128
NH = 8
LB = 128
CONVK = 4
DFF = 2816
N_MOD = 6
IN_PROJ = 2312
INP = 2432
O_Q, O_K, O_V, O_Z, O_XBC, O_DT = 0, 512, 640, 768, 1280, 2304
ZXD = INP - O_Z
EPS = 1e-6
NEG = -1e30
ROPE_THETA = 10000.0
VMEM_LIMIT = 56 * 1024 * 1024

ADAM_LR = 0.001
ADAM_B1 = 0.9
ADAM_B2 = 0.999
ADAM_EPS = 1e-08
ADAM_WD = 0.01
ADAM_STEP = 10

NT_DIMS = (((1,), (1,)), ((), ()))
TN_DIMS = (((0,), (0,)), ((), ()))


def _pcall(body, **kw):
    return pl.pallas_call(body, **kw)


def _sds(shape, dtype):
    return jax.ShapeDtypeStruct(shape, dtype)


def _params(n_grid=1):
    return pltpu.CompilerParams(dimension_semantics=("arbitrary",) * n_grid, vmem_limit_bytes=VMEM_LIMIT)


def _const(shape):
    return pl.BlockSpec(shape, lambda *_: (0,) * len(shape), pipeline_mode=pl.Buffered(1))


def _largest_divisor(n, candidates):
    for cand in candidates:
        if n % cand == 0:
            return cand
    raise ValueError(f"no tile in {candidates} divides {n}")


def _rows(t, w):
    return pl.BlockSpec((t, w), lambda i: (i, 0))


def _dot(a, b):
    return jnp.dot(a, b, preferred_element_type=F32)


def _dot_nt(a, b):
    return lax.dot_general(a, b, NT_DIMS, preferred_element_type=F32)


def _dot_tn(a, b):
    return lax.dot_general(a, b, TN_DIMS, preferred_element_type=F32)


def _sigmoid(v):
    return 1.0 / (1.0 + jnp.exp(-v))


def _softplus(v):
    return jnp.maximum(v, 0.0) + jnp.log1p(jnp.exp(-jnp.abs(v)))


def _rope_sign_mask(shape):
    lane = lax.broadcasted_iota(jnp.int32, shape, 1)
    return (lane % HD) < (HD // 2)


def _rope(t, cs, sn, inverse):
    r_dn = pltpu.roll(t, HD // 2, 1)
    r_up = pltpu.roll(t, LB - HD // 2, 1)
    first = _rope_sign_mask(t.shape)
    if inverse:
        rot = jnp.where(first, r_up, -r_dn)
    else:
        rot = jnp.where(first, -r_up, r_dn)
    return t * cs + rot * sn


def _norm_mod_fwd(xv, nw, shift, scale):
    r = lax.rsqrt(jnp.mean(xv * xv, axis=-1, keepdims=True) + EPS)
    xh = xv * r
    return (xh * nw) * (1.0 + scale) + shift


def _norm_mod_bwd(xv, dh, nw, scale):
    r = lax.rsqrt(jnp.mean(xv * xv, axis=-1, keepdims=True) + EPS)
    xh = xv * r
    xn = xh * nw
    d_shift = jnp.sum(dh, axis=0, keepdims=True)
    d_scale = jnp.sum(dh * xn, axis=0, keepdims=True)
    dxn = dh * (1.0 + scale)
    d_w = jnp.sum(dxn * xh, axis=0, keepdims=True)
    dxh = dxn * nw
    dx = r * (dxh - xh * jnp.mean(dxh * xh, axis=-1, keepdims=True))
    return dx, d_shift, d_scale, d_w


def _inproj_fwd(x, pos, invf, mod8, n1w, w_in):
    s = x.shape[0]
    tt = min(512, s)

    def body(x_ref, pos_ref, invf_ref, mod_ref, nw_ref, w_ref,
             q_ref, k_ref, v_ref, z_ref, xbc_ref, dtr_ref, h1_ref, cos_ref, sin_ref):
        h = _norm_mod_fwd(x_ref[...], nw_ref[...], mod_ref[0:1, :], mod_ref[1:2, :])
        hb = h.astype(BF16)
        h1_ref[...] = hb
        proj = _dot_nt(hb, w_ref[...])
        ang = pos_ref[...].astype(F32) * invf_ref[...]
        cs = jnp.cos(ang)
        sn = jnp.sin(ang)
        cos_ref[...] = cs
        sin_ref[...] = sn
        for a in range(AW // LB):
            q_ref[:, a * LB:(a + 1) * LB] = _rope(proj[:, O_Q + a * LB:O_Q + (a + 1) * LB], cs, sn, False).astype(BF16)
        k_ref[...] = _rope(proj[:, O_K:O_V], cs, sn, False).astype(BF16)
        v_ref[...] = proj[:, O_V:O_Z].astype(BF16)
        z_ref[...] = proj[:, O_Z:O_XBC]
        xbc_ref[...] = proj[:, O_XBC:O_DT]
        dtr_ref[...] = proj[:, O_DT:INP]

    return _pcall(
        body, name="inproj_fwd", grid=(s // tt,),
        in_specs=[_rows(tt, D), _rows(tt, 1), _const((1, LB)), _const((8, D)), _const((1, D)), _const((INP, D))],
        out_specs=[_rows(tt, AW), _rows(tt, KVW), _rows(tt, KVW), _rows(tt, SW), _rows(tt, D), _rows(tt, LB),
                   _rows(tt, D), _rows(tt, LB), _rows(tt, LB)],
        out_shape=[_sds((s, AW), BF16), _sds((s, KVW), BF16), _sds((s, KVW), BF16), _sds((s, SW), F32),
                   _sds((s, D), F32), _sds((s, LB), F32), _sds((s, D), BF16), _sds((s, LB), F32), _sds((s, LB), F32)],
        compiler_params=_params(),
    )(x, pos, invf, mod8, n1w, w_in)


QPG = 4
ATT_SCALE = 1.0 / math.sqrt(HD)


def _stack_heads(val, g):
    return jnp.concatenate([val[:, (QPG * g + hh) * HD:(QPG * g + hh + 1) * HD] for hh in range(QPG)], axis=0)


def _unstack_heads(groups):
    pieces = [grp[hh * LB:(hh + 1) * LB, :] for grp in groups for hh in range(QPG)]
    return [jnp.concatenate(pieces[2 * a:2 * a + 2], axis=1) for a in range(NQ // 2)]


def _upper_mask():
    row = lax.broadcasted_iota(jnp.int32, (QPG * LB, LB), 0)
    col = lax.broadcasted_iota(jnp.int32, (QPG * LB, LB), 1)
    return col > (row % LB)


def _sink_wide(sinks, g):
    return jnp.concatenate([jnp.broadcast_to(sinks[QPG * g + hh:QPG * g + hh + 1, 0:1], (LB, LB))
                            for hh in range(QPG)], axis=0)


def _row_sums_wide(v, terms):
    return _dot_sel(v, jnp.ones((v.shape[1], LB), BF16), terms)


def _band(upper, prev_part, cur_part):
    return jnp.where(upper, prev_part, cur_part)


def _attn_scores(n, qg, kcat, upper):
    sp = _dot_nt(qg, kcat[0:LB, :]) * ATT_SCALE
    sc = _dot_nt(qg, kcat[LB:2 * LB, :]) * ATT_SCALE
    return _band(upper, jnp.where(n > 0, sp, NEG), sc)


def _attn_softmax(comb, sink):
    m = jnp.maximum(jnp.max(comb, axis=-1, keepdims=True), sink)
    p = jnp.exp(comb - m)
    es = jnp.exp(sink - m)
    return p, es, _row_sums_wide(p, 1) + es


def _attn_fwd_block(n, q_ref, kp_ref, kc_ref, vp_ref, vc_ref, sink_ref, o_ref, pr_ref, ps_ref):
    qv = q_ref[...]
    kcat = jnp.concatenate([kp_ref[...], kc_ref[...]], axis=0)
    vcat = jnp.concatenate([vp_ref[...], vc_ref[...]], axis=0)
    sinks = sink_ref[...]
    upper = _upper_mask()
    outs = []
    for g in range(NQ // QPG):
        sl = slice(g * HD, (g + 1) * HD)
        rows = slice(g * QPG * LB, (g + 1) * QPG * LB)
        p, es, denom = _attn_softmax(_attn_scores(n, _stack_heads(qv, g), kcat[:, sl], upper), _sink_wide(sinks, g))
        rden = 1.0 / denom
        pr_ref[0, rows, :] = (p * rden).astype(BF16)
        ps_ref[0, rows, :] = (es * rden).astype(BF16)
        outs.append((_dot(jnp.where(upper, p, 0.0).astype(BF16), vcat[0:LB, sl])
                     + _dot(jnp.where(upper, 0.0, p).astype(BF16), vcat[LB:2 * LB, sl])) * rden[:, 0:HD])
    for g, grp in enumerate(outs):
        for hh in range(QPG):
            h = QPG * g + hh
            o_ref[:, h * HD:(h + 1) * HD] = grp[hh * LB:(hh + 1) * LB, :].astype(BF16)


def _cumsum_rows(a, reverse):
    row = lax.broadcasted_iota(jnp.int32, a.shape, 0)
    step = 1
    while step < LB:
        if reverse:
            a = a + jnp.where(row < LB - step, pltpu.roll(a, LB - step, 0), 0.0)
        else:
            a = a + jnp.where(row >= step, pltpu.roll(a, step, 0), 0.0)
        step *= 2
    return a


SUB = 8


def _conv_shifts(tail, cur):
    row = lax.broadcasted_iota(jnp.int32, tail.shape, 0)
    out = [cur]
    for j in range(1, CONVK):
        rolled = pltpu.roll(cur, j, 0)
        top = jnp.where(row < j, pltpu.roll(tail, j, 0), rolled[0:SUB, :])
        out.append(jnp.concatenate([top, rolled[SUB:, :]], axis=0))
    return out


def _conv_advances(du, head):
    row = lax.broadcasted_iota(jnp.int32, head.shape, 0)
    out = []
    for j in range(1, CONVK):
        rolled = pltpu.roll(du, LB - j, 0)
        bottom = jnp.where(row >= SUB - j, pltpu.roll(head, SUB - j, 0), rolled[LB - SUB:, :])
        out.append(jnp.concatenate([rolled[:LB - SUB, :], bottom], axis=0))
    return out


def _split(v, terms):
    out = []
    for _ in range(terms - 1):
        t = v.astype(BF16)
        out.append(t)
        v = v - t.astype(F32)
    out.append(v.astype(BF16))
    return out


def _dot_sel(v, sel, terms):
    parts = [_dot(t, sel) for t in _split(v, terms)]
    return functools.reduce(lambda a, b: a + b, parts)


def _dot_nt_sel(v, sel, terms):
    parts = [_dot_nt(t, sel) for t in _split(v, terms)]
    return functools.reduce(lambda a, b: a + b, parts)


def _ssd_pre(xt_ref, xc_ref, cw_ref, cb_ref, dtr_ref, sp_ref, n):
    cur = xc_ref[...]
    tail = jnp.where(n > 0, xt_ref[...], 0.0)
    sh = _conv_shifts(tail, cur)
    u = cb_ref[...] + cw_ref[CONVK - 1:CONVK, :] * sh[0]
    for j in range(1, CONVK):
        u = u + cw_ref[CONVK - 1 - j:CONVK - j, :] * sh[j]
    dt = _softplus(dtr_ref[...] + sp_ref[0:1, :])
    acs = _cumsum_rows(dt * -jnp.exp(sp_ref[1:2, :]), False)
    return u, dt, acs


def _gated_norm_fwd(y, z, sgz, nw):
    yz = y * (z * sgz)
    parts = []
    for g in range(2):
        t = yz[:, g * 256:(g + 1) * 256]
        parts.append(t * lax.rsqrt(jnp.mean(t * t, axis=-1, keepdims=True) + EPS))
    return jnp.concatenate(parts, axis=1) * nw


HPG = 4
GW = HPG * HD


class _SsdChunk:
    def __init__(self, xc, dt, acs, spv, e64, e128, decay=None):
        self.e64, self.e128 = e64, e128
        alast = acs[LB - 1:LB, :]
        self.e_all = jnp.exp(acs)
        self.dte_all = jnp.exp(alast - acs)
        self.elast = jnp.exp(alast)
        wide = _dot_sel(jnp.concatenate([dt, self.e_all, self.dte_all], axis=0), e64, 2)
        self.dt_x, self.e_x, self.dte_x = wide[0:LB], wide[LB:2 * LB], wide[2 * LB:3 * LB]
        self.dsk_x = _dot_sel(spv, e64, 3)[2:3, :]
        if decay is None:
            acs_t = jnp.transpose(acs)
            ac_x = _dot_sel(acs, e128, 3)
            row = lax.broadcasted_iota(jnp.int32, (HPG * LB, LB), 0)
            col = lax.broadcasted_iota(jnp.int32, (HPG * LB, LB), 1)
            causal = (row % LB) >= col
        lane = lax.broadcasted_iota(jnp.int32, (LB, GW), 1)
        self.head_lanes = [(lane >= hh * HD) & (lane < (hh + 1) * HD) for hh in range(HPG)]
        self.xs, self.xdt, self.b, self.c, self.bb, self.cb16, self.cbm, self.dm_st, self.m_st = ([] for _ in range(9))
        for g in range(2):
            heads = range(HPG * g, HPG * (g + 1))
            if decay is None:
                ac_st = jnp.concatenate([ac_x[:, j * LB:(j + 1) * LB] for j in heads], axis=0)
                ar_st = jnp.concatenate([jnp.broadcast_to(acs_t[j:j + 1, :], (LB, LB)) for j in heads], axis=0)
                dm_st = jnp.exp(jnp.where(causal, ac_st - ar_st, NEG))
            else:
                dm_st = decay[g]
            bg = xc[:, SW + g * NST:SW + (g + 1) * NST]
            cg = xc[:, SW + 2 * NST + g * NST:SW + 2 * NST + (g + 1) * NST]
            bgb, cgb = bg.astype(BF16), cg.astype(BF16)
            cbm = _dot_nt(cgb, bgb)
            xs_g = xc[:, g * GW:(g + 1) * GW]
            self.xs.append(xs_g)
            self.xdt.append(xs_g * self.dt_x[:, g * GW:(g + 1) * GW])
            self.b.append(bg)
            self.c.append(cg)
            self.bb.append(bgb)
            self.cb16.append(cgb)
            self.cbm.append(cbm)
            self.dm_st.append(dm_st)
            self.m_st.append(jnp.concatenate([cbm] * HPG, axis=0) * dm_st)

    def elast_rows(self, g):
        return jnp.concatenate([jnp.broadcast_to(self.elast[:, j:j + 1], (HD, NST))
                                for j in range(HPG * g, HPG * (g + 1))], axis=0)

    def diag_blocks(self, stacked):
        out = stacked[(HPG - 1) * LB:HPG * LB, :]
        for hh in range(HPG - 2, -1, -1):
            out = jnp.where(self.head_lanes[hh], stacked[hh * LB:(hh + 1) * LB, :], out)
        return out

    def block_diag(self, v):
        return jnp.concatenate([jnp.where(self.head_lanes[hh], v, 0.0) for hh in range(HPG)], axis=0)


def _ssd_fwd_block(n, xt_ref, xc_ref, cw_ref, cb_ref, dtr_ref, sp_ref, z_ref, nw_ref, e64_ref, e128_ref,
                   yn_ref, y_ref, hs_ref, u_ref, dt_ref, acs_ref, dm_ref, sgu_ref, sgz_ref, h_scr):
    @pl.when(n == 0)
    def _():
        h_scr[...] = jnp.zeros_like(h_scr)

    h_all = h_scr[...]
    hs_ref[0] = h_all
    u, dt, acs = _ssd_pre(xt_ref, xc_ref, cw_ref, cb_ref, dtr_ref, sp_ref, n)
    u_ref[...] = u
    dt_ref[...] = dt
    acs_ref[...] = acs
    sg_u = _sigmoid(u)
    sgu_ref[...] = sg_u
    xc = u * sg_u
    ck = _SsdChunk(xc, dt, acs, sp_ref[...], e64_ref[...], e128_ref[...])
    dm_ref[0] = jnp.concatenate(ck.dm_st, axis=0)
    ys, hn = [], []
    for g in range(2):
        gl = slice(g * GW, (g + 1) * GW)
        xdt = ck.xdt[g]
        hg = h_all[gl, :]
        y_diag = ck.diag_blocks(_dot(ck.m_st[g].astype(BF16), xdt.astype(BF16)))
        y_off = ck.e_x[:, gl] * _dot_nt(ck.cb16[g], hg.astype(BF16))
        ys.append(y_diag + y_off + ck.xs[g] * ck.dsk_x[:, gl])
        hn.append(hg * ck.elast_rows(g) + _dot_tn((xdt * ck.dte_x[:, gl]).astype(BF16), ck.bb[g]))
    h_scr[...] = jnp.concatenate(hn, axis=0)
    y = jnp.concatenate(ys, axis=1)
    y_ref[...] = y
    z = z_ref[...]
    sgz = _sigmoid(z)
    sgz_ref[...] = sgz
    yn_ref[...] = _gated_norm_fwd(y, z, sgz, nw_ref[...]).astype(BF16)


def _mixer_fwd(q, k, v, sinks8, xbc, conv_w8, conv_b, dtr, ssm_p, z, nw, gathers):
    s = q.shape[0]
    nb = s // LB
    prev = lambda n: (jnp.maximum(n - 1, 0), 0)
    cur = lambda n: (n, 0)
    items, ex_shapes, n_g = _exchange_items(gathers, [])
    ne = len(items)

    n_in, n_out = 16, 12
    relay_step = (nb - 1) // 2
    stack = pl.BlockSpec((1, NH * LB, LB), lambda n: (n, 0, 0))
    e64, e128 = _head_expanders()

    def body(*refs):
        a_in, s_in, ex_in = refs[:6], refs[6:n_in], refs[n_in:n_in + ne]
        (o_ref, yn_ref, y_ref, hs_ref, u_ref, dt_ref, acs_ref, dm_ref, sgu_ref, sgz_ref, pr_ref,
         ps_ref) = refs[n_in + ne:n_in + n_out + ne]
        ex_out = refs[n_in + n_out + ne:n_in + n_out + 2 * ne]
        h_scr = refs[n_in + n_out + 2 * ne]
        sems = refs[n_in + n_out + 1 + 2 * ne:]
        n = pl.program_id(0)

        @pl.when(n == 0)
        def _():
            _Exchange(n_g, ex_in, ex_out, sems).two_level_start()

        _attn_fwd_block(n, *a_in, o_ref, pr_ref, ps_ref)
        _ssd_fwd_block(n, *s_in, yn_ref, y_ref, hs_ref, u_ref, dt_ref, acs_ref, dm_ref, sgu_ref, sgz_ref, h_scr)

        @pl.when(n == relay_step)
        def _():
            _Exchange(n_g, ex_in, ex_out, sems).two_level_relay()

        @pl.when(n == nb - 1)
        def _():
            _Exchange(n_g, ex_in, ex_out, sems).two_level_finish()

    any_spec = pl.BlockSpec(memory_space=pl.ANY)
    tail = pl.BlockSpec((SUB, D), lambda n: (jnp.maximum(n * (LB // SUB) - 1, 0), 0))
    outs = _pcall(
        body, name="mixer_fwd", grid=(nb,),
        in_specs=[pl.BlockSpec((LB, AW), cur), pl.BlockSpec((LB, KVW), prev), pl.BlockSpec((LB, KVW), cur),
                  pl.BlockSpec((LB, KVW), prev), pl.BlockSpec((LB, KVW), cur), _const((8, LB)),
                  tail, pl.BlockSpec((LB, D), cur), _const((8, D)), _const((1, D)),
                  pl.BlockSpec((LB, LB), cur), _const((8, LB)), pl.BlockSpec((LB, SW), cur), _const((1, SW)),
                  _const(e64.shape), _const(e128.shape)]
        + [any_spec] * ne,
        out_specs=[pl.BlockSpec((LB, AW), cur), pl.BlockSpec((LB, SW), cur), pl.BlockSpec((LB, SW), cur),
                   pl.BlockSpec((1, NH * HD, NST), lambda n: (n, 0, 0)), pl.BlockSpec((LB, D), cur),
                   pl.BlockSpec((LB, LB), cur), pl.BlockSpec((LB, LB), cur),
                   stack, pl.BlockSpec((LB, D), cur), pl.BlockSpec((LB, SW), cur), stack, stack] + [any_spec] * ne,
        out_shape=[_sds((s, AW), BF16), _sds((s, SW), BF16), _sds((s, SW), F32), _sds((nb, NH * HD, NST), F32),
                   _sds((s, D), F32), _sds((s, LB), F32), _sds((s, LB), F32), _sds((nb, NH * LB, LB), F32),
                   _sds((s, D), F32), _sds((s, SW), F32), _sds((nb, NH * LB, LB), BF16),
                   _sds((nb, NH * LB, LB), BF16)]
        + ex_shapes,
        scratch_shapes=[pltpu.VMEM((NH * HD, NST), F32)] + _exchange_sems(ne),
        compiler_params=_params(),
    )(q, k, k, v, v, sinks8, xbc, xbc, conv_w8, conv_b, dtr, ssm_p, z, nw, e64, e128, *items)
    return outs[:n_out], outs[n_out:]


def _head_expanders():
    j = lax.broadcasted_iota(jnp.int32, (LB, NH * HD), 0)
    e64 = (lax.broadcasted_iota(jnp.int32, (LB, NH * HD), 1) // HD == j).astype(BF16)
    j = lax.broadcasted_iota(jnp.int32, (LB, NH * LB), 0)
    e128 = (lax.broadcasted_iota(jnp.int32, (LB, NH * LB), 1) // LB == j).astype(BF16)
    return e64, e128


def _outproj_ffn_fwd_loss(attn, yn, x, tgt, mod8, n2w, fnw, w_out, w_gu_t, w_down):
    s = x.shape[0]
    tf = min(256, s)

    def body(a_ref, y_ref, x_ref, t_ref, mod_ref, nw_ref, fw_ref, wo_ref, wgu_ref, wd_ref,
             x2_ref, h2_ref, mo_ref, mix_ref, gu_ref, act_ref, dx3_ref, sm_ref):
        i = pl.program_id(0)

        @pl.when(i == 0)
        def _():
            sm_ref[...] = jnp.zeros_like(sm_ref)

        mix = jnp.concatenate([a_ref[...], y_ref[...]], axis=1)
        mix_ref[...] = mix
        mo = _dot(mix, wo_ref[...])
        mo_ref[...] = mo.astype(BF16)
        x2 = x_ref[...] + mod_ref[2:3, :] * mo
        x2_ref[...] = x2
        h2 = _norm_mod_fwd(x2, nw_ref[...], mod_ref[3:4, :], mod_ref[4:5, :]).astype(BF16)
        h2_ref[...] = h2
        gu = _dot_nt(h2, wgu_ref[...])
        gu_ref[...] = gu.astype(BF16)
        g = gu[:, :DFF]
        act = (g * _sigmoid(g) * gu[:, DFF:]).astype(BF16)
        act_ref[...] = act
        ff = _dot(act, wd_ref[...])
        x3 = x2 + mod_ref[5:6, :] * ff
        r = lax.rsqrt(jnp.mean(x3 * x3, axis=-1, keepdims=True) + EPS)
        xh = x3 * r
        fw = fw_ref[...]
        err = xh * fw - t_ref[...]
        dy = err * (1.0 / D)
        dxh = dy * fw
        dx3 = r * (dxh - xh * jnp.mean(dxh * xh, axis=-1, keepdims=True))
        dx3_ref[...] = dx3
        sm_ref[0:1, :] += jnp.sum(dx3 * ff, axis=0, keepdims=True)
        sm_ref[1:2, :] += jnp.sum(dy * xh, axis=0, keepdims=True)
        sm_ref[2:3, :] += jnp.sum(err * err, axis=0, keepdims=True)

    return _pcall(
        body, name="outproj_ffn_fwd_loss", grid=(s // tf,),
        in_specs=[_rows(tf, AW), _rows(tf, SW), _rows(tf, D), _rows(tf, D), _const((8, D)), _const((1, D)),
                  _const((1, D)), _const((D, D)), _const((2 * DFF, D)), _const((DFF, D))],
        out_specs=[_rows(tf, D), _rows(tf, D), _rows(tf, D), _rows(tf, D), _rows(tf, 2 * DFF), _rows(tf, DFF),
                   _rows(tf, D), pl.BlockSpec((8, D), lambda i: (0, 0))],
        out_shape=[_sds((s, D), F32), _sds((s, D), BF16), _sds((s, D), BF16), _sds((s, D), BF16),
                   _sds((s, 2 * DFF), BF16), _sds((s, DFF), BF16), _sds((s, D), F32), _sds((8, D), F32)],
        compiler_params=_params(),
    )(attn, yn, x, tgt, mod8, n2w, fnw, w_out, w_gu_t, w_down)


def _ffn_bwd(dx3, gu, x2, mixout, mod8, n2w, w_gu, w_down, w_out):
    s = x2.shape[0]
    tb = min(256, s)

    def body(dx3_ref, gu_ref, x2_ref, mo_ref, mod_ref, nw_ref, wgu_ref, wd_ref, wo_ref,
             dx2_ref, dff_ref, dgu_ref, dmix_ref, dattn_ref, dyn_ref, sm_ref):
        i = pl.program_id(0)

        @pl.when(i == 0)
        def _():
            sm_ref[...] = jnp.zeros_like(sm_ref)

        dx3 = dx3_ref[...]
        dff = (dx3 * mod_ref[5:6, :]).astype(BF16)
        dff_ref[...] = dff
        dact = _dot_nt(dff, wd_ref[...])
        g = gu_ref[:, :DFF].astype(F32)
        u = gu_ref[:, DFF:].astype(F32)
        sg = _sigmoid(g)
        dgu = jnp.concatenate([dact * u * sg * (1.0 + g * (1.0 - sg)), dact * g * sg], axis=1).astype(BF16)
        dgu_ref[...] = dgu
        dh2 = _dot(dgu, wgu_ref[...])
        dxn, d_shift, d_scale, d_w = _norm_mod_bwd(x2_ref[...], dh2, nw_ref[...], mod_ref[4:5, :])
        dx2 = dx3 + dxn
        dx2_ref[...] = dx2
        sm_ref[0:1, :] += d_shift
        sm_ref[1:2, :] += d_scale
        sm_ref[2:3, :] += d_w
        sm_ref[3:4, :] += jnp.sum(dx2 * mo_ref[...].astype(F32), axis=0, keepdims=True)
        dmix = (dx2 * mod_ref[2:3, :]).astype(BF16)
        dmix_ref[...] = dmix
        dmi = _dot_nt(dmix, wo_ref[...])
        dattn_ref[...] = dmi[:, :AW].astype(BF16)
        dyn_ref[...] = dmi[:, AW:]

    return _pcall(
        body, name="ffn_bwd", grid=(s // tb,),
        in_specs=[_rows(tb, D), _rows(tb, 2 * DFF), _rows(tb, D), _rows(tb, D), _const((8, D)), _const((1, D)),
                  _const((2 * DFF, D)), _const((DFF, D)), _const((D, D))],
        out_specs=[_rows(tb, D), _rows(tb, D), _rows(tb, 2 * DFF), _rows(tb, D), _rows(tb, AW), _rows(tb, SW),
                   pl.BlockSpec((8, D), lambda i: (0, 0))],
        out_shape=[_sds((s, D), F32), _sds((s, D), BF16), _sds((s, 2 * DFF), BF16), _sds((s, D), BF16),
                   _sds((s, AW), BF16), _sds((s, SW), F32), _sds((8, D), F32)],
        compiler_params=_params(),
    )(dx3, gu, x2, mixout, mod8, n2w, w_gu, w_down, w_out)


def _ssd_bwd_block(i, n, *refs):
    def run(dyn_ref, y_ref, z_ref, x_ref, u_ref, dt_ref, acs_ref, dm_ref, sgu_ref, sgz_ref, cw_ref, sp_ref, nw_ref,
            hs_ref, e64_ref, e128_ref, dzxd_ref, sm_ref, dh_scr, dun_scr):
        @pl.when(i == 0)
        def _():
            dh_scr[...] = jnp.zeros_like(dh_scr)
            dun_scr[...] = jnp.zeros_like(dun_scr)
            sm_ref[...] = jnp.zeros_like(sm_ref)

        u, dt, acs, sg_u = u_ref[...], dt_ref[...], acs_ref[...], sgu_ref[...]
        xc = u * sg_u
        a_neg = -jnp.exp(sp_ref[1:2, :])
        ck = _SsdChunk(xc, dt, acs, sp_ref[...], e64_ref[...], e128_ref[...],
                       decay=[dm_ref[0, g * HPG * LB:(g + 1) * HPG * LB, :] for g in range(2)])
        h_all = hs_ref[0]
        dh_all = dh_scr[...]
        riota = lax.broadcasted_iota(jnp.int32, (LB, LB), 0)
        lane1 = lax.broadcasted_iota(jnp.int32, (1, LB), 1)

        z = z_ref[...]
        y = y_ref[...]
        sgz = sgz_ref[...]
        sz = z * sgz
        yz = y * sz
        nwv = nw_ref[...]
        dyn_v = dyn_ref[...]
        dyhat = dyn_v * nwv
        yhat_parts, dyz_parts = [], []
        for g in range(2):
            gs = slice(g * 256, (g + 1) * 256)
            t = yz[:, gs]
            rg = lax.rsqrt(jnp.mean(t * t, axis=-1, keepdims=True) + EPS)
            yh = t * rg
            dyh = dyhat[:, gs]
            yhat_parts.append(yh)
            dyz_parts.append(rg * (dyh - yh * jnp.mean(dyh * yh, axis=-1, keepdims=True)))
        yhat = jnp.concatenate(yhat_parts, axis=1)
        dyz = jnp.concatenate(dyz_parts, axis=1)
        sm_ref[5:6, 0:SW] += jnp.sum(dyn_v * yhat, axis=0, keepdims=True)
        dy = dyz * sz
        dzxd_ref[:, 0:SW] = (dyz * y * sgz * (1.0 + z * (1.0 - sgz))).astype(BF16)

        cat = lambda parts: jnp.concatenate(parts, axis=1)
        dxs, dbs, dcs, dhp, g_cat, de_x, ddte_x, ddt_x, ddsk_x = ([] for _ in range(9))
        dacs_t = jnp.zeros((LB, LB), F32)
        hsum = jnp.zeros((1, LB), F32)
        for g in range(2):
            gl = slice(g * GW, (g + 1) * GW)
            xs_g, xdt, bgb, cgb = ck.xs[g], ck.xdt[g], ck.bb[g], ck.cb16[g]
            m_st, dm_st = ck.m_st[g], ck.dm_st[g]
            dt_x, e_x, dte_x = ck.dt_x[:, gl], ck.e_x[:, gl], ck.dte_x[:, gl]
            xdtb = xdt.astype(BF16)
            hg, dhn = h_all[gl, :], dh_all[gl, :]
            hb, dhnb = hg.astype(BF16), dhn.astype(BF16)
            dy_g = dy[:, gl]
            ddsk_x.append(jnp.sum(dy_g * xs_g, axis=0, keepdims=True))
            dy_bd = ck.block_diag(dy_g).astype(BF16)
            dm4 = _dot_nt(dy_bd, xdtb)
            dxdt = _dot_tn(m_st.astype(BF16), dy_bd)
            gmat = dm4 * m_st
            dcbm = dm4 * dm_st
            dcb = dcbm[0:LB] + dcbm[LB:2 * LB] + dcbm[2 * LB:3 * LB] + dcbm[3 * LB:4 * LB]
            g_cat.append(cat([gmat[hh * LB:(hh + 1) * LB, :] for hh in range(HPG)]))
            for hh in range(HPG):
                j = HPG * g + hh
                col_sum = jnp.sum(gmat[hh * LB:(hh + 1) * LB, :], axis=0, keepdims=True)
                dacs_t = dacs_t - jnp.where(riota == j, col_sum, 0.0)
                hsl = slice(hh * HD, (hh + 1) * HD)
                hsum = hsum + jnp.where(lane1 == j, jnp.sum(dhn[hsl, :] * hg[hsl, :]), 0.0)
            dchb = (dy_g * e_x).astype(BF16)
            dcg = _dot(dchb, hb)
            dh_prev = _dot_tn(dchb, cgb)
            de_x.append(dy_g * _dot_nt(cgb, hb))
            dxs_s = _dot_nt(bgb, dhnb)
            dbg = _dot((xdt * dte_x).astype(BF16), dhnb)
            dxdt = dxdt + dxs_s * dte_x
            ddte_x.append(dxs_s * xdt)
            dhp.append(dhn * ck.elast_rows(g) + dh_prev)
            dxs.append(dy_g * ck.dsk_x[:, gl] + dxdt * dt_x)
            ddt_x.append(dxdt * xs_g)
            dcbb = dcb.astype(BF16)
            dbs.append(dbg + _dot_tn(dcbb, cgb))
            dcs.append(dcg + _dot(dcbb, bgb))
        dh_scr[...] = jnp.concatenate(dhp, axis=0)
        red = _dot_nt_sel(jnp.concatenate([cat(de_x), cat(ddte_x), cat(ddt_x)], axis=0), ck.e64, 1)
        de_c, ddte_c, ddt_c = red[0:LB], red[LB:2 * LB], red[2 * LB:3 * LB]
        ddsk = _dot_nt_sel(jnp.broadcast_to(cat(ddsk_x), (SUB, NH * HD)), ck.e64, 2)[0:1, :]
        t1 = ddte_c * ck.dte_all
        dalast = jnp.sum(t1, axis=0, keepdims=True) + hsum * ck.elast
        dacs = (_dot_nt_sel(cat(g_cat), ck.e128, 2) + de_c * ck.e_all - t1 + jnp.transpose(dacs_t)
                + jnp.where(riota == LB - 1, dalast, 0.0))
        da = _cumsum_rows(dacs, True)
        ddt = ddt_c + da * a_neg
        da_log = jnp.sum(da * dt, axis=0, keepdims=True) * a_neg
        ddtr = ddt * (1.0 - jnp.exp(-dt))
        dzxd_ref[:, SW + D:ZXD] = ddtr.astype(BF16)
        sm_ref[6:7, 0:LB] += jnp.sum(ddtr, axis=0, keepdims=True)
        sm_ref[6:7, LB:2 * LB] += da_log
        sm_ref[6:7, 2 * LB:3 * LB] += ddsk

        du = cat(dxs + dbs + dcs) * (sg_u * (1.0 + u * (1.0 - sg_u)))
        xv = x_ref[...]
        adv = [du] + _conv_advances(du, dun_scr[...])
        sm_ref[0:1, :] += jnp.sum(du, axis=0, keepdims=True)
        dxbc = cw_ref[CONVK - 1:CONVK, :] * du
        for j in range(CONVK):
            sm_ref[CONVK - j:CONVK + 1 - j, :] += jnp.sum(adv[j] * xv, axis=0, keepdims=True)
            if j:
                dxbc = dxbc + cw_ref[CONVK - 1 - j:CONVK - j, :] * adv[j]
        dun_scr[...] = du[0:SUB, :]
        dzxd_ref[:, SW:SW + D] = dxbc.astype(BF16)

    run(*refs)


def _attn_bwd_block(i, n, q_ref, kp_ref, kc_ref, vp_ref, vc_ref, o_ref, do_ref, cos_ref, sin_ref, pr_ref, ps_ref,
                    dq_ref, dkv_ref, ds_ref, ck_scr, cv_scr):
    @pl.when(i == 0)
    def _():
        ds_ref[...] = jnp.zeros_like(ds_ref)
        ck_scr[...] = jnp.zeros_like(ck_scr)
        cv_scr[...] = jnp.zeros_like(cv_scr)

    qv, ov, dov = q_ref[...], o_ref[...], do_ref[...]
    kcat = jnp.concatenate([kp_ref[...], kc_ref[...]], axis=0)
    vcat = jnp.concatenate([vp_ref[...], vc_ref[...]], axis=0)
    upper = _upper_mask()
    srow = lax.broadcasted_iota(jnp.int32, (8, LB), 0)
    slane = lax.broadcasted_iota(jnp.int32, (8, LB), 1)
    dsink = jnp.zeros((8, LB), F32)
    dq_g, dk_g, dv_g = [], [], []
    for g in range(NQ // QPG):
        sl = slice(g * HD, (g + 1) * HD)
        qg = _stack_heads(qv, g)
        dog = _stack_heads(dov, g)
        rows = slice(g * QPG * LB, (g + 1) * QPG * LB)
        probs = pr_ref[0, rows, :].astype(F32)
        psink = ps_ref[0, rows, :].astype(F32)
        delta = _row_sums_wide(dog.astype(F32) * _stack_heads(ov, g).astype(F32), 2)
        dsc = probs * (_band(upper, _dot_nt(dog, vcat[0:LB, sl]), _dot_nt(dog, vcat[LB:2 * LB, sl])) - delta)
        sink_terms = (psink * delta)[:, 0:1]
        for hh in range(QPG):
            dsink = dsink - jnp.where((srow == QPG * g + hh) & (slane == 0),
                                      jnp.sum(sink_terms[hh * LB:(hh + 1) * LB, :]), 0.0)
        ds_p = jnp.where(upper, dsc, 0.0).astype(BF16)
        ds_c = jnp.where(upper, 0.0, dsc).astype(BF16)
        dq_g.append((_dot(ds_p, kcat[0:LB, sl]) + _dot(ds_c, kcat[LB:2 * LB, sl])) * ATT_SCALE)
        dk_g.append(jnp.concatenate([_dot_tn(ds_p, qg), _dot_tn(ds_c, qg)], axis=0) * ATT_SCALE)
        dv_g.append(jnp.concatenate([_dot_tn(jnp.where(upper, probs, 0.0).astype(BF16), dog),
                                     _dot_tn(jnp.where(upper, 0.0, probs).astype(BF16), dog)], axis=0))
    ds_ref[...] += dsink
    cs = cos_ref[...]
    sn = sin_ref[...]
    dk2 = jnp.concatenate(dk_g, axis=1)
    dv2 = jnp.concatenate(dv_g, axis=1)
    for a, tile in enumerate(_unstack_heads(dq_g)):
        dq_ref[:, a * LB:(a + 1) * LB] = _rope(tile, cs, sn, True).astype(BF16)
    dkv_ref[:, 0:KVW] = _rope(ck_scr[...] + dk2[LB:2 * LB, :], cs, sn, True).astype(BF16)
    dkv_ref[:, KVW:2 * KVW] = (cv_scr[...] + dv2[LB:2 * LB, :]).astype(BF16)
    ck_scr[...] = dk2[0:LB, :]
    cv_scr[...] = dv2[0:LB, :]


def _mixer_bwd(q, k, v, o, do, cos, sin, probs, psink, dyn, y, z, xbc, u, dtv, acs, decay, sgu, sgz, conv_w8, ssm_p, nw,
               hs, scatters):
    s = q.shape[0]
    nb = s // LB
    cur = lambda i: (nb - 1 - i, 0)
    prev = lambda i: (jnp.maximum(nb - 2 - i, 0), 0)
    n_attn, n_in = 11, 27
    items, ex_shapes, n_g = _exchange_items([], scatters)
    ne = len(items)
    e64, e128 = _head_expanders()
    stack = pl.BlockSpec((1, NH * LB, LB), lambda i: (nb - 1 - i, 0, 0))

    def body(*refs):
        i = pl.program_id(0)
        n = nb - 1 - i
        a_in, s_in, ex_in = refs[:n_attn], refs[n_attn:n_in], refs[n_in:n_in + ne]
        dp_ref, ds_ref, sm_ref = refs[n_in + ne:n_in + ne + 3]
        dq_ref, dkv_ref, dzxd_ref = dp_ref.at[:, O_Q:O_K], dp_ref.at[:, O_K:O_Z], dp_ref.at[:, O_Z:INP]
        ex_out = refs[n_in + ne + 3:n_in + 2 * ne + 3]
        ck_scr, cv_scr, dh_scr, dun_scr = refs[n_in + 2 * ne + 3:n_in + 2 * ne + 7]
        sems = refs[n_in + 2 * ne + 7:]

        @pl.when(i == 0)
        def _():
            _Exchange(n_g, ex_in, ex_out, sems).start()

        _attn_bwd_block(i, n, *a_in, dq_ref, dkv_ref, ds_ref, ck_scr, cv_scr)
        _ssd_bwd_block(i, n, *s_in, dzxd_ref, sm_ref, dh_scr, dun_scr)

        @pl.when(i == nb - 1)
        def _():
            _Exchange(n_g, ex_in, ex_out, sems).finish()

    any_spec = pl.BlockSpec(memory_space=pl.ANY)
    outs = _pcall(
        body, name="mixer_bwd", grid=(nb,),
        in_specs=[pl.BlockSpec((LB, AW), cur), pl.BlockSpec((LB, KVW), prev), pl.BlockSpec((LB, KVW), cur),
                  pl.BlockSpec((LB, KVW), prev), pl.BlockSpec((LB, KVW), cur), pl.BlockSpec((LB, AW), cur),
                  pl.BlockSpec((LB, AW), cur), pl.BlockSpec((LB, LB), cur), pl.BlockSpec((LB, LB), cur),
                  stack, stack,
                  pl.BlockSpec((LB, SW), cur), pl.BlockSpec((LB, SW), cur), pl.BlockSpec((LB, SW), cur),
                  pl.BlockSpec((LB, D), cur), pl.BlockSpec((LB, D), cur), pl.BlockSpec((LB, LB), cur),
                  pl.BlockSpec((LB, LB), cur), stack, pl.BlockSpec((LB, D), cur), pl.BlockSpec((LB, SW), cur),
                  _const((8, D)), _const((8, LB)), _const((1, SW)),
                  pl.BlockSpec((1, NH * HD, NST), lambda i: (nb - 1 - i, 0, 0)),
                  _const(e64.shape), _const(e128.shape)] + [any_spec] * ne,
        out_specs=[pl.BlockSpec((LB, INP), cur), pl.BlockSpec((8, LB), lambda i: (0, 0)),
                   pl.BlockSpec((8, D), lambda i: (0, 0))] + [any_spec] * ne,
        out_shape=[_sds((s, INP), BF16), _sds((8, LB), F32), _sds((8, D), F32)] + ex_shapes,
        scratch_shapes=[pltpu.VMEM((LB, KVW), F32), pltpu.VMEM((LB, KVW), F32),
                        pltpu.VMEM((NH * HD, NST), F32), pltpu.VMEM((SUB, D), F32)]
        + _exchange_sems(ne),
        compiler_params=_params(),
    )(q, k, k, v, v, o, do, cos, sin, probs, psink, dyn, y, z, xbc, u, dtv, acs, decay, sgu, sgz, conv_w8, ssm_p, nw, hs,
      e64, e128, *items)
    return outs[0], outs[1], outs[2], outs[3:]


def _inproj_bwd(dproj, x, dx2, mod8, n1w, w_in_t, scatters, smalls):
    s = x.shape[0]
    tt = min(512, s)
    nt = s // tt
    items, ex_shapes, n_g = _exchange_items([], scatters)
    ne = len(items)
    n_in = 10

    def body(*refs):
        dp_ref, x_ref, dx2_ref, mod_ref, nw_ref, w_ref, f_ref, b_ref, s_ref, k_ref = refs[:n_in]
        ex_in = refs[n_in:n_in + ne]
        gx_ref, sm_ref = refs[n_in + ne:n_in + 2 + ne]
        ex_out = refs[n_in + 2 + ne:n_in + 2 + 2 * ne]
        gpack_ref = refs[n_in + 2 + 2 * ne]
        pack_scr = refs[n_in + 3 + 2 * ne]
        sems = refs[n_in + 4 + 2 * ne:n_in + 7 + 2 * ne]
        pack_sems = refs[n_in + 7 + 2 * ne:]
        i = pl.program_id(0)

        @pl.when(i == 0)
        def _():
            sm_ref[...] = jnp.zeros_like(sm_ref)
            _Exchange(n_g, ex_in, ex_out, sems).start()

        w = w_ref[...]
        hr = tt // 2
        dh1 = [_dot(dp_ref[h * hr:(h + 1) * hr, :], w) for h in range(2)]
        sums = jnp.zeros((3, D), F32)
        for h in range(2):
            rows = slice(h * hr, (h + 1) * hr)
            dxn, d_shift, d_scale, d_w = _norm_mod_bwd(x_ref[rows, :], dh1[h], nw_ref[...], mod_ref[1:2, :])
            gx_ref[rows, :] = dx2_ref[rows, :] + dxn
            sums = sums + jnp.concatenate([d_shift, d_scale, d_w], axis=0)
        sm_ref[0:3, :] += sums

        @pl.when(i == nt - 1)
        def _():
            _pack_rows(f_ref, b_ref, s_ref, sm_ref, k_ref, pack_scr)
            small = _Exchange(1, [pack_scr], [gpack_ref], pack_sems)
            small.start()
            _Exchange(n_g, ex_in, ex_out, sems).finish()
            small.finish()

    any_spec = pl.BlockSpec(memory_space=pl.ANY)
    outs = _pcall(
        body, name="inproj_bwd", grid=(nt,),
        in_specs=[_rows(tt, INP), _rows(tt, D), _rows(tt, D), _const((8, D)), _const((1, D)), _const((INP, D)),
                  _const((8, D)), _const((8, D)), _const((8, D)), _const((8, LB))]
        + [any_spec] * ne,
        out_specs=[_rows(tt, D), pl.BlockSpec((8, D), lambda i: (0, 0))] + [any_spec] * (ne + 1),
        out_shape=[_sds((s, D), F32), _sds((8, D), F32)] + ex_shapes + [_sds((N_DEV, PACK_ROWS, D), F32)],
        scratch_shapes=[pltpu.VMEM((PACK_ROWS, D), F32)] + _exchange_sems(ne) + _exchange_sems(1),
        compiler_params=_params(),
    )(dproj, x, dx2, mod8, n1w, w_in_t, *smalls, *items)
    return outs[0], outs[2:2 + ne], outs[2 + ne]


def _wgrad(a, b, name):
    s, m = a.shape
    n = b.shape[1]
    tk = min(2048, s)
    wide = (1408, 1024, 512)
    tm = next((t for t in wide if m % t == 0), m)
    tn = n if n <= 2048 else _largest_divisor(n, wide)
    nk = s // tk

    def body(a_ref, b_ref, o_ref, acc):
        kk = pl.program_id(2)

        @pl.when(kk == 0)
        def _():
            acc[...] = jnp.zeros_like(acc)

        acc[...] += _dot_tn(a_ref[...], b_ref[...])

        @pl.when(kk == nk - 1)
        def _():
            o_ref[...] = acc[...].astype(BF16)

    return _pcall(
        body, name=name, grid=(m // tm, n // tn, nk),
        in_specs=[pl.BlockSpec((tk, tm), lambda i, j, kk: (kk, i)), pl.BlockSpec((tk, tn), lambda i, j, kk: (kk, j))],
        out_specs=pl.BlockSpec((tm, tn), lambda i, j, kk: (i, j)),
        out_shape=_sds((m, n), BF16),
        scratch_shapes=[pltpu.VMEM((tm, tn), F32)],
        compiler_params=_params(3),
    )(a, b)


PACK_ROWS = 24


def _pack_rows(f_ref, b_ref, s_ref, i_ref, k_ref, o_ref):
    o_ref[...] = jnp.zeros_like(o_ref)
    o_ref[0:2, :] = i_ref[0:2, :]
    o_ref[2:3, :] = b_ref[3:4, :]
    o_ref[3:5, :] = b_ref[0:2, :]
    o_ref[5:6, :] = f_ref[0:1, :]
    o_ref[6:7, :] = i_ref[2:3, :]
    o_ref[7:8, :] = b_ref[2:3, :]
    o_ref[8:9, :] = f_ref[1:2, :]
    o_ref[9:14, :] = s_ref[0:5, :]
    o_ref[14:15, :] = s_ref[5:6, :]
    o_ref[15:16, 0:3 * LB] = s_ref[6:7, 0:3 * LB]
    lane = lax.broadcasted_iota(jnp.int32, (1, LB), 1)
    sk = jnp.zeros((1, LB), F32)
    for h in range(NQ):
        sk = sk + jnp.where(lane == h, k_ref[h:h + 1, 0:1], 0.0)
    o_ref[15:16, 3 * LB:4 * LB] = sk
    o_ref[16:17, :] = f_ref[2:3, :]


def _exchange(gathers, scatters, name, two_level=False):
    items, shapes, n_g = _exchange_items(gathers, scatters)
    n = len(items)
    assert not (two_level and scatters)

    def body(*refs):
        ex = _Exchange(n_g, refs[:n], refs[n:2 * n], refs[2 * n:])
        if two_level:
            ex.gather_two_level()
        else:
            ex.start()
            ex.finish()

    any_spec = pl.BlockSpec(memory_space=pl.ANY)
    return _pcall(
        body, name=name, in_specs=[any_spec] * n, out_specs=[any_spec] * n, out_shape=shapes,
        scratch_shapes=_exchange_sems(n),
    )(*items)


def _exchange_items(gathers, scatters):
    items = list(gathers) + list(scatters)
    shapes = [_sds((N_DEV,) + a.shape, a.dtype) for a in gathers] + [_sds(a.shape, a.dtype) for a in scatters]
    return items, shapes, len(gathers)


def _exchange_sems(n):
    return [pltpu.SemaphoreType.DMA((n, N_DEV - 1)), pltpu.SemaphoreType.DMA((n, N_DEV - 1)),
            pltpu.SemaphoreType.DMA((n,))]


class _Exchange:
    def __init__(self, n_g, ins, outs, sems):
        self.n_g, self.ins, self.outs = n_g, ins, outs
        self.send_sems, self.recv_sems, self.loc_sems = sems
        xi, yi, ci = lax.axis_index("x"), lax.axis_index("y"), lax.axis_index("c")
        self.me = 4 * xi + 2 * yi + ci
        self.peers = []
        for r in range(1, N_DEV):
            px = 1 - xi if r & 4 else xi
            py = 1 - yi if r & 2 else yi
            pc = 1 - ci if r & 1 else ci
            self.peers.append(((px, py, pc), 4 * px + 2 * py + pc))

    def _copy(self, t, r, landing):
        dev, peer = self.peers[r]
        src = self.ins[t] if t < self.n_g else self.ins[t].at[peer]
        return pltpu.make_async_remote_copy(
            src_ref=src, dst_ref=self.outs[t].at[landing], send_sem=self.send_sems.at[t, r],
            recv_sem=self.recv_sems.at[t, r], device_id=dev, device_id_type=pl.DeviceIdType.MESH)

    def _local(self, t):
        src = self.ins[t] if t < self.n_g else self.ins[t].at[self.me]
        return pltpu.make_async_copy(src, self.outs[t].at[self.me], self.loc_sems.at[t])

    def start(self):
        for t in range(len(self.ins)):
            self._local(t).start()
            for r in range(N_DEV - 1):
                self._copy(t, r, self.me).start()

    def finish(self):
        n = len(self.ins)
        for t in range(n):
            for r in range(N_DEV - 1):
                self._copy(t, r, self.peers[r][1]).wait_recv()
        for t in range(n):
            for r in range(N_DEV - 1):
                self._copy(t, r, self.me).wait_send()
            self._local(t).wait()

    def gather_two_level(self):
        self.two_level_start()
        self.two_level_relay()
        self.two_level_finish()

    DIRECT = (0, 1, 3, 5)

    def two_level_start(self):
        for t in range(len(self.ins)):
            self._local(t).start()
            for r in self.DIRECT:
                self._copy(t, r, self.me).start()

    def _relay(self, t, r):
        peer = self.peers[r][1]
        return pltpu.make_async_remote_copy(
            src_ref=self.outs[t].at[peer], dst_ref=self.outs[t].at[peer], send_sem=self.send_sems.at[t, r + 1],
            recv_sem=self.recv_sems.at[t, r + 1], device_id=self.peers[0][0], device_id_type=pl.DeviceIdType.MESH)

    def two_level_relay(self):
        for t in range(len(self.ins)):
            for r in self.DIRECT[1:]:
                self._copy(t, r, self.peers[r][1]).wait_recv()
                self._relay(t, r).start()

    def two_level_finish(self):
        n = len(self.ins)
        for t in range(n):
            for r in (0, 2, 4, 6):
                self._copy(t, r, self.peers[r][1]).wait_recv()
        for t in range(n):
            for r in self.DIRECT:
                self._copy(t, r, self.me).wait_send()
            for r in self.DIRECT[1:]:
                self._relay(t, r).wait_send()
            self._local(t).wait()


def _prologue(c8, w_in_tb, cw8, w_cols, b_cols):
    ncol = w_cols.shape[1]

    def body(c_ref, win_ref, cw_ref, w_ref, b_ref, gc_ref, gin_ref, gcw_ref, gmod_ref, call_scr, mod_scr, loc_sem,
             *sems):
        big = _Exchange(2, [win_ref, cw_ref], [gin_ref, gcw_ref], sems[0:3])
        big.two_level_start()
        small = _Exchange(1, [c_ref], [gc_ref], sems[3:6])
        small.start()
        small.finish()
        landed = pltpu.make_async_copy(gc_ref, call_scr, loc_sem)
        landed.start()
        landed.wait()
        cv = call_scr[:, 0, :]
        sc = (cv * _sigmoid(cv)).astype(BF16)
        mod_scr[...] = _dot(sc, w_ref[...].astype(BF16)) + b_ref[...]
        mods = _Exchange(1, [mod_scr], [gmod_ref], sems[6:9])
        mods.start()
        mods.finish()
        big.two_level_relay()
        big.two_level_finish()

    any_spec = pl.BlockSpec(memory_space=pl.ANY)
    vmem_spec = pl.BlockSpec(memory_space=pltpu.VMEM)
    return _pcall(
        body, name="prologue", in_specs=[any_spec, any_spec, any_spec, vmem_spec, vmem_spec],
        out_specs=[any_spec] * 4,
        out_shape=[_sds((N_DEV,) + c8.shape, F32), _sds((N_DEV,) + w_in_tb.shape, BF16),
                   _sds((N_DEV,) + cw8.shape, F32), _sds((N_DEV, N_DEV, ncol), F32)],
        scratch_shapes=[pltpu.VMEM((N_DEV,) + c8.shape, F32), pltpu.VMEM((N_DEV, ncol), F32),
                        pltpu.SemaphoreType.DMA] + _exchange_sems(2) + _exchange_sems(1) + _exchange_sems(1),
        compiler_params=pltpu.CompilerParams(vmem_limit_bytes=VMEM_LIMIT),
    )(c8, w_in_tb, cw8, w_cols, b_cols)


def _adamw(w, g, m, v):
    m2 = ADAM_B1 * m + (1.0 - ADAM_B1) * g
    v2 = ADAM_B2 * v + (1.0 - ADAM_B2) * (g * g)
    m_hat = m2 / (1.0 - ADAM_B1 ** ADAM_STEP)
    v_hat = v2 / (1.0 - ADAM_B2 ** ADAM_STEP)
    delta = -ADAM_LR * (m_hat / (jnp.sqrt(v_hat) + ADAM_EPS) + ADAM_WD * w)
    return delta, m2, v2


def _sum_adamw(parts, w, m, v, name):
    rws, cols = w.shape
    tr = next((t for t in (256, 176, 128) if rws % t == 0), rws)

    def body(p_ref, w_ref, m_ref, v_ref, g_ref, d_ref, mo_ref, vo_ref):
        g = p_ref[0].astype(F32)
        for dev in range(1, N_DEV):
            g = g + p_ref[dev].astype(F32)
        g_ref[...] = g
        d_ref[...], mo_ref[...], vo_ref[...] = _adamw(w_ref[...], g, m_ref[...], v_ref[...])

    blk = pl.BlockSpec((tr, cols), lambda i: (i, 0))
    return _pcall(
        body, name=name, grid=(rws // tr,),
        in_specs=[pl.BlockSpec((N_DEV, tr, cols), lambda i: (0, i, 0)), blk, blk, blk],
        out_specs=[blk] * 4, out_shape=[_sds((rws, cols), F32)] * 4, compiler_params=_params(),
    )(parts, w, m, v)


def _wada_adamw(c_all, dmod_cols, w, m, v):
    rws, cols = w.shape
    tr = 256

    def body(c_ref, dm_ref, w_ref, m_ref, v_ref, g_ref, d_ref, mo_ref, vo_ref):
        cv = c_ref[...]
        sc = (cv * _sigmoid(cv)).astype(BF16)
        g = _dot_tn(sc, dm_ref[...].astype(BF16))
        g_ref[...] = g
        d_ref[...], mo_ref[...], vo_ref[...] = _adamw(w_ref[...], g, m_ref[...], v_ref[...])

    blk = pl.BlockSpec((tr, cols), lambda i: (i, 0))
    return _pcall(
        body, name="wada_adamw", grid=(rws // tr,),
        in_specs=[pl.BlockSpec((N_DEV, tr), lambda i: (0, i)), pl.BlockSpec((N_DEV, cols), lambda i: (0, 0)),
                  blk, blk, blk],
        out_specs=[blk] * 4, out_shape=[_sds((rws, cols), F32)] * 4, compiler_params=_params(),
    )(c_all, dmod_cols, w, m, v)


def _small_reduce(packs):
    def body(p_ref, o_ref):
        tot = p_ref[0]
        for dev in range(1, N_DEV):
            tot = tot + p_ref[dev]
        o_ref[...] = tot
        o_ref[16:17, :] = jnp.zeros((1, D), F32) + (0.5 / D) * jnp.sum(tot[16:17, :])

    return _pcall(body, name="small_reduce", out_shape=_sds((PACK_ROWS, D), F32))(packs)


def _adamw_many(ws, gs, ms, vs):
    k = len(ws)

    def body(*refs):
        for i in range(k):
            w_ref, g_ref, m_ref, v_ref = refs[i], refs[k + i], refs[2 * k + i], refs[3 * k + i]
            d_ref, mo_ref, vo_ref = refs[4 * k + i], refs[5 * k + i], refs[6 * k + i]
            d_ref[...], mo_ref[...], vo_ref[...] = _adamw(w_ref[...], g_ref[...], m_ref[...], v_ref[...])

    shp = [_sds(w.shape, F32) for w in ws]
    outs = _pcall(body, name="adamw_small", out_shape=shp * 3)(*ws, *gs, *ms, *vs)
    return outs[:k], outs[k:2 * k], outs[2 * k:]


def kernel(x, c, positions, w_ada, b_ada, norm1_w, w_in, conv_w, conv_b, dt_bias, a_log, d_skip, attn_sinks, ssm_norm_w, w_out, norm2_w, w_gate_up, w_down, final_norm_w, loss_target, m_w_ada, m_b_ada, m_norm1_w, m_w_in, m_conv_w, m_conv_b, m_dt_bias, m_a_log, m_d_skip, m_attn_sinks, m_ssm_norm_w, m_w_out, m_norm2_w, m_w_gate_up, m_w_down, m_final_norm_w, v_w_ada, v_b_ada, v_norm1_w, v_w_in, v_conv_w, v_conv_b, v_dt_bias, v_a_log, v_d_skip, v_attn_sinks, v_ssm_norm_w, v_w_out, v_norm2_w, v_w_gate_up, v_w_down, v_final_norm_w):
    s = x.shape[1]
    me = 4 * lax.axis_index("x") + 2 * lax.axis_index("y") + lax.axis_index("c")
    ada_cols = N_MOD * D // N_DEV

    c8 = jnp.pad(c, ((0, 7), (0, 0)))
    cw8 = jnp.pad(conv_w[0], ((0, 8 - CONVK), (0, 0)))
    w_in_t, m_w_in_t, v_w_in_t = jnp.transpose(w_in[0]), jnp.transpose(m_w_in[0]), jnp.transpose(v_w_in[0])
    w_gu_t, m_w_gu_t, v_w_gu_t = (jnp.transpose(w_gate_up[0]), jnp.transpose(m_w_gate_up[0]),
                                  jnp.transpose(v_w_gate_up[0]))
    b_cols = lax.dynamic_slice(b_ada, (0, me * ada_cols), (1, ada_cols))
    g_c, g_in, g_cw, g_mod = _prologue(c8, w_in_t.astype(BF16), cw8, w_ada[0], b_cols)
    c_all = g_c[:, 0, :]
    w_in_f = jnp.pad(g_in.reshape(IN_PROJ, D), ((0, INP - IN_PROJ), (0, 0)))
    conv_w8 = jnp.transpose(g_cw, (1, 0, 2)).reshape(8, D)
    mod = lax.dynamic_index_in_dim(g_mod, me, axis=1, keepdims=False).reshape(N_MOD, D)
    mod8 = jnp.pad(mod, ((0, 8 - N_MOD), (0, 0)))

    half = HD // 2
    inv_freq = ROPE_THETA ** (-jnp.arange(half, dtype=F32) / half)
    invf = jnp.tile(inv_freq, LB // half).reshape(1, LB)
    lanes = lambda a: jnp.pad(a, ((0, 0), (0, LB - a.shape[1])))
    ssm_p = jnp.pad(jnp.concatenate([lanes(dt_bias), lanes(a_log), lanes(d_skip)], axis=0), ((0, 5), (0, 0)))
    sinks8 = jnp.broadcast_to(attn_sinks.reshape(NQ, 1), (NQ, LB))

    xs, tgt, fnw = x[0], loss_target[0], final_norm_w.reshape(1, D)

    q, k, v, z, xbc, dtr, h1, cos, sin = _inproj_fwd(xs, positions[0].reshape(s, 1), invf, mod8, norm1_w, w_in_f)
    (attn, yn, y, hs, conv_u, dtv, acs, decay, sgu, sgz, probs, psink), (g_out, g_gu, g_down) = _mixer_fwd(
        q, k, v, sinks8, xbc, conv_w8, conv_b, dtr, ssm_p, z, ssm_norm_w,
        [w_out[0].astype(BF16), w_gu_t.astype(BF16), w_down[0].astype(BF16)])
    w_out_f = g_out.reshape(D, D)
    w_gu_f = g_gu.reshape(2 * DFF, D)
    w_down_f = g_down.reshape(DFF, D)
    x2, h2, mo, mix, gu, act, dx3, sm_f = _outproj_ffn_fwd_loss(attn, yn, xs, tgt, mod8, norm2_w, fnw, w_out_f, w_gu_f,
                                                                 w_down_f)

    dx2, dff, dgu, dmix, dattn, dyn, sm_b = _ffn_bwd(dx3, gu, x2, mo, mod8, norm2_w, w_gu_f, w_down_f, w_out_f)
    p_gu = _wgrad(dgu, h2, "wgrad_gate_up").reshape(N_DEV, 2 * DFF // N_DEV, D)
    p_down = _wgrad(act, dff, "wgrad_down").reshape(N_DEV, DFF // N_DEV, D)
    p_out = _wgrad(mix, dmix, "wgrad_out").reshape(N_DEV, D // N_DEV, D)
    dproj, dsink, sm_s, (r_gu, r_down, r_out) = _mixer_bwd(
        q, k, v, attn, dattn, cos, sin, probs, psink, dyn, y, z, xbc, conv_u, dtv, acs, decay, sgu, sgz, conv_w8, ssm_p,
        ssm_norm_w, hs, [p_gu, p_down, p_out])
    p_in = _wgrad(dproj, h1, "wgrad_in")[:IN_PROJ].reshape(N_DEV, IN_PROJ // N_DEV, D)
    gx, (r_in,), g_pack = _inproj_bwd(dproj, xs, dx2, mod8, norm1_w, w_in_f, [p_in], (sm_f, sm_b, sm_s, dsink))

    tot = _small_reduce(g_pack)
    loss = tot[16, 0]
    dmod_all = g_pack[:, 0:N_MOD, :].reshape(N_DEV, N_MOD * D)
    dmod_cols = lax.dynamic_slice(dmod_all, (0, me * ada_cols), (N_DEV, ada_cols))

    big = {
        "w_ada": _wada_adamw(c_all, dmod_cols, w_ada[0], m_w_ada[0], v_w_ada[0]),
        "w_in": [jnp.transpose(t) for t in _sum_adamw(r_in, w_in_t, m_w_in_t, v_w_in_t, "adamw_in")],
        "w_out": _sum_adamw(r_out, w_out[0], m_w_out[0], v_w_out[0], "adamw_out"),
        "w_gate_up": [jnp.transpose(t) for t in _sum_adamw(r_gu, w_gu_t, m_w_gu_t, v_w_gu_t, "adamw_gate_up")],
        "w_down": _sum_adamw(r_down, w_down[0], m_w_down[0], v_w_down[0], "adamw_down"),
    }
    small_names = ["b_ada", "norm1_w", "conv_w", "conv_b", "dt_bias", "a_log", "d_skip", "attn_sinks", "ssm_norm_w",
                   "norm2_w", "final_norm_w"]
    row15 = tot[15:16, :]
    small_g = {
        "b_ada": tot[0:N_MOD, :].reshape(1, N_MOD * D),
        "norm1_w": tot[6:7, :],
        "conv_w": lax.dynamic_slice(tot[10:14, :], (0, me * (D // N_DEV)), (CONVK, D // N_DEV)),
        "conv_b": tot[9:10, :],
        "dt_bias": row15[:, 0:NH],
        "a_log": row15[:, LB:LB + NH],
        "d_skip": row15[:, 2 * LB:2 * LB + NH],
        "attn_sinks": row15[:, 3 * LB:3 * LB + NQ],
        "ssm_norm_w": tot[14:15, 0:SW],
        "norm2_w": tot[7:8, :],
        "final_norm_w": tot[8:9, :],
    }
    small_w = {"b_ada": b_ada, "norm1_w": norm1_w, "conv_w": conv_w[0], "conv_b": conv_b, "dt_bias": dt_bias,
               "a_log": a_log, "d_skip": d_skip, "attn_sinks": attn_sinks, "ssm_norm_w": ssm_norm_w,
               "norm2_w": norm2_w, "final_norm_w": final_norm_w.reshape(1, D)}
    small_m = {"b_ada": m_b_ada, "norm1_w": m_norm1_w, "conv_w": m_conv_w[0], "conv_b": m_conv_b,
               "dt_bias": m_dt_bias, "a_log": m_a_log, "d_skip": m_d_skip, "attn_sinks": m_attn_sinks,
               "ssm_norm_w": m_ssm_norm_w, "norm2_w": m_norm2_w, "final_norm_w": m_final_norm_w.reshape(1, D)}
    small_v = {"b_ada": v_b_ada, "norm1_w": v_norm1_w, "conv_w": v_conv_w[0], "conv_b": v_conv_b,
               "dt_bias": v_dt_bias, "a_log": v_a_log, "d_skip": v_d_skip, "attn_sinks": v_attn_sinks,
               "ssm_norm_w": v_ssm_norm_w, "norm2_w": v_norm2_w, "final_norm_w": v_final_norm_w.reshape(1, D)}
    s_d, s_m, s_v = _adamw_many([small_w[k] for k in small_names], [small_g[k] for k in small_names],
                                [small_m[k] for k in small_names], [small_v[k] for k in small_names])

    order = ["w_ada", "b_ada", "norm1_w", "w_in", "conv_w", "conv_b", "dt_bias", "a_log", "d_skip", "attn_sinks",
             "ssm_norm_w", "w_out", "norm2_w", "w_gate_up", "w_down", "final_norm_w"]
    lead = {"w_ada", "w_in", "conv_w", "w_out", "w_gate_up", "w_down"}
    grads, deltas, new_m, new_v = [], [], [], []
    for name in order:
        if name in big:
            g, d, m2, v2 = big[name]
        else:
            i = small_names.index(name)
            g, d, m2, v2 = small_g[name], s_d[i], s_m[i], s_v[i]
        if name in lead:
            g, d, m2, v2 = g[None], d[None], m2[None], v2[None]
        if name == "final_norm_w":
            g, d, m2, v2 = g.reshape(D), d.reshape(D), m2.reshape(D), v2.reshape(D)
        grads.append(g)
        deltas.append(d)
        new_m.append(m2)
        new_v.append(v2)
    return (loss, gx[None], *grads, *deltas, *new_m, *new_v)
```

```python
import functools
import math

import jax
import jax.numpy as jnp
from jax import lax
from jax.experimental import pallas as pl
from jax.experimental.pallas import tpu as pltpu

F32 = jnp.float32
BF16 = jnp.bfloat16

N_DEV = 8
D = 1024
HD = 64
NQ = 8
AW = 512
KVW = 128
SW = 512
NST = 128
NH = 8
LB = 128
CONVK = 4
DFF = 2816
N_MOD = 6
IN_PROJ = 2312
INP = 2432
O_Q, O_K, O_V, O_Z, O_XBC, O_DT = 0, 512, 640, 768, 1280, 2304
ZXD = INP - O_Z
EPS = 1e-6
NEG = -1e30
ROPE_THETA = 10000.0
VMEM_LIMIT = 56 * 1024 * 1024

ADAM_LR = 0.001
ADAM_B1 = 0.9
ADAM_B2 = 0.999
ADAM_EPS = 1e-08
ADAM_WD = 0.01
ADAM_STEP = 10

NT_DIMS = (((1,), (1,)), ((), ()))
TN_DIMS = (((0,), (0,)), ((), ()))


def _pcall(body, **kw):
    return pl.pallas_call(body, **kw)


def _sds(shape, dtype):
    return jax.ShapeDtypeStruct(shape, dtype)


def _params(n_grid=1):
    return pltpu.CompilerParams(dimension_semantics=("arbitrary",) * n_grid, vmem_limit_bytes=VMEM_LIMIT)


def _const(shape):
    return pl.BlockSpec(shape, lambda *_: (0,) * len(shape), pipeline_mode=pl.Buffered(1))


def _largest_divisor(n, candidates):
    for cand in candidates:
        if n % cand == 0:
            return cand
    raise ValueError(f"no tile in {candidates} divides {n}")


def _rows(t, w):
    return pl.BlockSpec((t, w), lambda i: (i, 0))


def _dot(a, b):
    return jnp.dot(a, b, preferred_element_type=F32)


def _dot_nt(a, b):
    return lax.dot_general(a, b, NT_DIMS, preferred_element_type=F32)


def _dot_tn(a, b):
    return lax.dot_general(a, b, TN_DIMS, preferred_element_type=F32)


def _sigmoid(v):
    return 1.0 / (1.0 + jnp.exp(-v))


def _softplus(v):
    return jnp.maximum(v, 0.0) + jnp.log1p(jnp.exp(-jnp.abs(v)))


def _rope_sign_mask(shape):
    lane = lax.broadcasted_iota(jnp.int32, shape, 1)
    return (lane % HD) < (HD // 2)


def _rope(t, cs, sn, inverse):
    r_dn = pltpu.roll(t, HD // 2, 1)
    r_up = pltpu.roll(t, LB - HD // 2, 1)
    first = _rope_sign_mask(t.shape)
    if inverse:
        rot = jnp.where(first, r_up, -r_dn)
    else:
        rot = jnp.where(first, -r_up, r_dn)
    return t * cs + rot * sn


def _norm_mod_fwd(xv, nw, shift, scale):
    r = lax.rsqrt(jnp.mean(xv * xv, axis=-1, keepdims=True) + EPS)
    xh = xv * r
    return (xh * nw) * (1.0 + scale) + shift


def _norm_mod_bwd(xv, dh, nw, scale):
    r = lax.rsqrt(jnp.mean(xv * xv, axis=-1, keepdims=True) + EPS)
    xh = xv * r
    xn = xh * nw
    d_shift = jnp.sum(dh, axis=0, keepdims=True)
    d_scale = jnp.sum(dh * xn, axis=0, keepdims=True)
    dxn = dh * (1.0 + scale)
    d_w = jnp.sum(dxn * xh, axis=0, keepdims=True)
    dxh = dxn * nw
    dx = r * (dxh - xh * jnp.mean(dxh * xh, axis=-1, keepdims=True))
    return dx, d_shift, d_scale, d_w


def _inproj_fwd(x, pos, invf, mod8, n1w, w_in):
    s = x.shape[0]
    tt = min(512, s)

    def body(x_ref, pos_ref, invf_ref, mod_ref, nw_ref, w_ref,
             q_ref, k_ref, v_ref, z_ref, xbc_ref, dtr_ref, h1_ref, cos_ref, sin_ref):
        h = _norm_mod_fwd(x_ref[...], nw_ref[...], mod_ref[0:1, :], mod_ref[1:2, :])
        hb = h.astype(BF16)
        h1_ref[...] = hb
        proj = _dot_nt(hb, w_ref[...])
        ang = pos_ref[...].astype(F32) * invf_ref[...]
        cs = jnp.cos(ang)
        sn = jnp.sin(ang)
        cos_ref[...] = cs
        sin_ref[...] = sn
        for a in range(AW // LB):
            q_ref[:, a * LB:(a + 1) * LB] = _rope(proj[:, O_Q + a * LB:O_Q + (a + 1) * LB], cs, sn, False).astype(BF16)
        k_ref[...] = _rope(proj[:, O_K:O_V], cs, sn, False).astype(BF16)
        v_ref[...] = proj[:, O_V:O_Z].astype(BF16)
        z_ref[...] = proj[:, O_Z:O_XBC]
        xbc_ref[...] = proj[:, O_XBC:O_DT]
        dtr_ref[...] = proj[:, O_DT:INP]

    return _pcall(
        body, name="inproj_fwd", grid=(s // tt,),
        in_specs=[_rows(tt, D), _rows(tt, 1), _const((1, LB)), _const((8, D)), _const((1, D)), _const((INP, D))],
        out_specs=[_rows(tt, AW), _rows(tt, KVW), _rows(tt, KVW), _rows(tt, SW), _rows(tt, D), _rows(tt, LB),
                   _rows(tt, D), _rows(tt, LB), _rows(tt, LB)],
        out_shape=[_sds((s, AW), BF16), _sds((s, KVW), BF16), _sds((s, KVW), BF16), _sds((s, SW), F32),
                   _sds((s, D), F32), _sds((s, LB), F32), _sds((s, D), BF16), _sds((s, LB), F32), _sds((s, LB), F32)],
        compiler_params=_params(),
    )(x, pos, invf, mod8, n1w, w_in)


QPG = 4
ATT_SCALE = 1.0 / math.sqrt(HD)


def _stack_heads(val, g):
    return jnp.concatenate([val[:, (QPG * g + hh) * HD:(QPG * g + hh + 1) * HD] for hh in range(QPG)], axis=0)


def _unstack_heads(groups):
    pieces = [grp[hh * LB:(hh + 1) * LB, :] for grp in groups for hh in range(QPG)]
    return [jnp.concatenate(pieces[2 * a:2 * a + 2], axis=1) for a in range(NQ // 2)]


def _upper_mask():
    row = lax.broadcasted_iota(jnp.int32, (QPG * LB, LB), 0)
    col = lax.broadcasted_iota(jnp.int32, (QPG * LB, LB), 1)
    return col > (row % LB)


def _sink_wide(sinks, g):
    return jnp.concatenate([jnp.broadcast_to(sinks[QPG * g + hh:QPG * g + hh + 1, 0:1], (LB, LB))
                            for hh in range(QPG)], axis=0)


def _row_sums_wide(v, terms):
    return _dot_sel(v, jnp.ones((v.shape[1], LB), BF16), terms)


def _band(upper, prev_part, cur_part):
    return jnp.where(upper, prev_part, cur_part)


def _attn_scores(n, qg, kcat, upper):
    sp = _dot_nt(qg, kcat[0:LB, :]) * ATT_SCALE
    sc = _dot_nt(qg, kcat[LB:2 * LB, :]) * ATT_SCALE
    return _band(upper, jnp.where(n > 0, sp, NEG), sc)


def _attn_softmax(comb, sink):
    m = jnp.maximum(jnp.max(comb, axis=-1, keepdims=True), sink)
    p = jnp.exp(comb - m)
    es = jnp.exp(sink - m)
    return p, es, _row_sums_wide(p, 1) + es


def _attn_fwd_block(n, q_ref, kp_ref, kc_ref, vp_ref, vc_ref, sink_ref, o_ref, pr_ref, ps_ref):
    qv = q_ref[...]
    kcat = jnp.concatenate([kp_ref[...], kc_ref[...]], axis=0)
    vcat = jnp.concatenate([vp_ref[...], vc_ref[...]], axis=0)
    sinks = sink_ref[...]
    upper = _upper_mask()
    outs = []
    for g in range(NQ // QPG):
        sl = slice(g * HD, (g + 1) * HD)
        rows = slice(g * QPG * LB, (g + 1) * QPG * LB)
        p, es, denom = _attn_softmax(_attn_scores(n, _stack_heads(qv, g), kcat[:, sl], upper), _sink_wide(sinks, g))
        rden = 1.0 / denom
        pr_ref[0, rows, :] = (p * rden).astype(BF16)
        ps_ref[0, rows, :] = (es * rden).astype(BF16)
        outs.append((_dot(jnp.where(upper, p, 0.0).astype(BF16), vcat[0:LB, sl])
                     + _dot(jnp.where(upper, 0.0, p).astype(BF16), vcat[LB:2 * LB, sl])) * rden[:, 0:HD])
    for g, grp in enumerate(outs):
        for hh in range(QPG):
            h = QPG * g + hh
            o_ref[:, h * HD:(h + 1) * HD] = grp[hh * LB:(hh + 1) * LB, :].astype(BF16)


def _cumsum_rows(a, reverse):
    row = lax.broadcasted_iota(jnp.int32, a.shape, 0)
    step = 1
    while step < LB:
        if reverse:
            a = a + jnp.where(row < LB - step, pltpu.roll(a, LB - step, 0), 0.0)
        else:
            a = a + jnp.where(row >= step, pltpu.roll(a, step, 0), 0.0)
        step *= 2
    return a


SUB = 8


def _conv_shifts(tail, cur):
    row = lax.broadcasted_iota(jnp.int32, tail.shape, 0)
    out = [cur]
    for j in range(1, CONVK):
        rolled = pltpu.roll(cur, j, 0)
        top = jnp.where(row < j, pltpu.roll(tail, j, 0), rolled[0:SUB, :])
        out.append(jnp.concatenate([top, rolled[SUB:, :]], axis=0))
    return out


def _conv_advances(du, head):
    row = lax.broadcasted_iota(jnp.int32, head.shape, 0)
    out = []
    for j in range(1, CONVK):
        rolled = pltpu.roll(du, LB - j, 0)
        bottom = jnp.where(row >= SUB - j, pltpu.roll(head, SUB - j, 0), rolled[LB - SUB:, :])
        out.append(jnp.concatenate([rolled[:LB - SUB, :], bottom], axis=0))
    return out


def _split(v, terms):
    out = []
    for _ in range(terms - 1):
        t = v.astype(BF16)
        out.append(t)
        v = v - t.astype(F32)
    out.append(v.astype(BF16))
    return out


def _dot_sel(v, sel, terms):
    parts = [_dot(t, sel) for t in _split(v, terms)]
    return functools.reduce(lambda a, b: a + b, parts)


def _dot_nt_sel(v, sel, terms):
    parts = [_dot_nt(t, sel) for t in _split(v, terms)]
    return functools.reduce(lambda a, b: a + b, parts)


def _ssd_pre(xt_ref, xc_ref, cw_ref, cb_ref, dtr_ref, sp_ref, n):
    cur = xc_ref[...]
    tail = jnp.where(n > 0, xt_ref[...], 0.0)
    sh = _conv_shifts(tail, cur)
    u = cb_ref[...] + cw_ref[CONVK - 1:CONVK, :] * sh[0]
    for j in range(1, CONVK):
        u = u + cw_ref[CONVK - 1 - j:CONVK - j, :] * sh[j]
    dt = _softplus(dtr_ref[...] + sp_ref[0:1, :])
    acs = _cumsum_rows(dt * -jnp.exp(sp_ref[1:2, :]), False)
    return u, dt, acs


def _gated_norm_fwd(y, z, sgz, nw):
    yz = y * (z * sgz)
    parts = []
    for g in range(2):
        t = yz[:, g * 256:(g + 1) * 256]
        parts.append(t * lax.rsqrt(jnp.mean(t * t, axis=-1, keepdims=True) + EPS))
    return jnp.concatenate(parts, axis=1) * nw


HPG = 4
GW = HPG * HD


class _SsdChunk:
    def __init__(self, xc, dt, acs, spv, e64, e128, decay=None):
        self.e64, self.e128 = e64, e128
        alast = acs[LB - 1:LB, :]
        self.e_all = jnp.exp(acs)
        self.dte_all = jnp.exp(alast - acs)
        self.elast = jnp.exp(alast)
        wide = _dot_sel(jnp.concatenate([dt, self.e_all, self.dte_all], axis=0), e64, 2)
        self.dt_x, self.e_x, self.dte_x = wide[0:LB], wide[LB:2 * LB], wide[2 * LB:3 * LB]
        self.dsk_x = _dot_sel(spv, e64, 3)[2:3, :]
        if decay is None:
            acs_t = jnp.transpose(acs)
            ac_x = _dot_sel(acs, e128, 3)
            row = lax.broadcasted_iota(jnp.int32, (HPG * LB, LB), 0)
            col = lax.broadcasted_iota(jnp.int32, (HPG * LB, LB), 1)
            causal = (row % LB) >= col
        lane = lax.broadcasted_iota(jnp.int32, (LB, GW), 1)
        self.head_lanes = [(lane >= hh * HD) & (lane < (hh + 1) * HD) for hh in range(HPG)]
        self.xs, self.xdt, self.b, self.c, self.bb, self.cb16, self.cbm, self.dm_st, self.m_st = ([] for _ in range(9))
        for g in range(2):
            heads = range(HPG * g, HPG * (g + 1))
            if decay is None:
                ac_st = jnp.concatenate([ac_x[:, j * LB:(j + 1) * LB] for j in heads], axis=0)
                ar_st = jnp.concatenate([jnp.broadcast_to(acs_t[j:j + 1, :], (LB, LB)) for j in heads], axis=0)
                dm_st = jnp.exp(jnp.where(causal, ac_st - ar_st, NEG))
            else:
                dm_st = decay[g]
            bg = xc[:, SW + g * NST:SW + (g + 1) * NST]
            cg = xc[:, SW + 2 * NST + g * NST:SW + 2 * NST + (g + 1) * NST]
            bgb, cgb = bg.astype(BF16), cg.astype(BF16)
            cbm = _dot_nt(cgb, bgb)
            xs_g = xc[:, g * GW:(g + 1) * GW]
            self.xs.append(xs_g)
            self.xdt.append(xs_g * self.dt_x[:, g * GW:(g + 1) * GW])
            self.b.append(bg)
            self.c.append(cg)
            self.bb.append(bgb)
            self.cb16.append(cgb)
            self.cbm.append(cbm)
            self.dm_st.append(dm_st)
            self.m_st.append(jnp.concatenate([cbm] * HPG, axis=0) * dm_st)

    def elast_rows(self, g):
        return jnp.concatenate([jnp.broadcast_to(self.elast[:, j:j + 1], (HD, NST))
                                for j in range(HPG * g, HPG * (g + 1))], axis=0)

    def diag_blocks(self, stacked):
        out = stacked[(HPG - 1) * LB:HPG * LB, :]
        for hh in range(HPG - 2, -1, -1):
            out = jnp.where(self.head_lanes[hh], stacked[hh * LB:(hh + 1) * LB, :], out)
        return out

    def block_diag(self, v):
        return jnp.concatenate([jnp.where(self.head_lanes[hh], v, 0.0) for hh in range(HPG)], axis=0)


def _ssd_fwd_block(n, xt_ref, xc_ref, cw_ref, cb_ref, dtr_ref, sp_ref, z_ref, nw_ref, e64_ref, e128_ref,
                   yn_ref, y_ref, hs_ref, u_ref, dt_ref, acs_ref, dm_ref, sgu_ref, sgz_ref, h_scr):
    @pl.when(n == 0)
    def _():
        h_scr[...] = jnp.zeros_like(h_scr)

    h_all = h_scr[...]
    hs_ref[0] = h_all
    u, dt, acs = _ssd_pre(xt_ref, xc_ref, cw_ref, cb_ref, dtr_ref, sp_ref, n)
    u_ref[...] = u
    dt_ref[...] = dt
    acs_ref[...] = acs
    sg_u = _sigmoid(u)
    sgu_ref[...] = sg_u
    xc = u * sg_u
    ck = _SsdChunk(xc, dt, acs, sp_ref[...], e64_ref[...], e128_ref[...])
    dm_ref[0] = jnp.concatenate(ck.dm_st, axis=0)
    ys, hn = [], []
    for g in range(2):
        gl = slice(g * GW, (g + 1) * GW)
        xdt = ck.xdt[g]
        hg = h_all[gl, :]
        y_diag = ck.diag_blocks(_dot(ck.m_st[g].astype(BF16), xdt.astype(BF16)))
        y_off = ck.e_x[:, gl] * _dot_nt(ck.cb16[g], hg.astype(BF16))
        ys.append(y_diag + y_off + ck.xs[g] * ck.dsk_x[:, gl])
        hn.append(hg * ck.elast_rows(g) + _dot_tn((xdt * ck.dte_x[:, gl]).astype(BF16), ck.bb[g]))
    h_scr[...] = jnp.concatenate(hn, axis=0)
    y = jnp.concatenate(ys, axis=1)
    y_ref[...] = y
    z = z_ref[...]
    sgz = _sigmoid(z)
    sgz_ref[...] = sgz
    yn_ref[...] = _gated_norm_fwd(y, z, sgz, nw_ref[...]).astype(BF16)


def _mixer_fwd(q, k, v, sinks8, xbc, conv_w8, conv_b, dtr, ssm_p, z, nw, gathers):
    s = q.shape[0]
    nb = s // LB
    bps = _blocks_per_step(nb)
    nsteps = nb // bps
    tl = bps * LB
    cur = lambda n: (n, 0)
    prev = lambda n: (jnp.maximum(n * bps - 1, 0), 0)
    items, ex_shapes, n_g = _exchange_items(gathers, [])
    ne = len(items)

    n_in, n_out = 16, 12
    relay_step = (nsteps - 1) // 2
    stack = pl.BlockSpec((bps, NH * LB, LB), lambda n: (n, 0, 0))
    e64, e128 = _head_expanders()

    def body(*refs):
        (q_ref, kp_ref, kc_ref, vp_ref, vc_ref, sink_ref, xt_ref, xc_ref, cw_ref, cb_ref, dtr_ref, sp_ref, z_ref,
         nw_ref, e64_ref, e128_ref) = refs[:n_in]
        ex_in = refs[n_in:n_in + ne]
        (o_ref, yn_ref, y_ref, hs_ref, u_ref, dt_ref, acs_ref, dm_ref, sgu_ref, sgz_ref, pr_ref,
         ps_ref) = refs[n_in + ne:n_in + n_out + ne]
        ex_out = refs[n_in + n_out + ne:n_in + n_out + 2 * ne]
        h_scr = refs[n_in + n_out + 2 * ne]
        sems = refs[n_in + n_out + 1 + 2 * ne:]
        n = pl.program_id(0)

        @pl.when(n == 0)
        def _():
            _Exchange(n_g, ex_in, ex_out, sems).two_level_start()

        for sub in range(bps):
            blk = n * bps + sub
            r = slice(sub * LB, (sub + 1) * LB)
            before = slice((sub - 1) * LB, sub * LB)
            one = slice(sub, sub + 1)
            _attn_fwd_block(blk, q_ref.at[r], kp_ref if sub == 0 else kc_ref.at[before], kc_ref.at[r],
                            vp_ref if sub == 0 else vc_ref.at[before], vc_ref.at[r], sink_ref,
                            o_ref.at[r], pr_ref.at[one], ps_ref.at[one])
            _ssd_fwd_block(blk, xt_ref if sub == 0 else xc_ref.at[sub * LB - SUB:sub * LB], xc_ref.at[r], cw_ref,
                           cb_ref, dtr_ref.at[r], sp_ref, z_ref.at[r], nw_ref, e64_ref, e128_ref,
                           yn_ref.at[r], y_ref.at[r], hs_ref.at[one], u_ref.at[r], dt_ref.at[r], acs_ref.at[r],
                           dm_ref.at[one], sgu_ref.at[r], sgz_ref.at[r], h_scr)

        @pl.when(n == relay_step)
        def _():
            _Exchange(n_g, ex_in, ex_out, sems).two_level_relay()

        @pl.when(n == nsteps - 1)
        def _():
            _Exchange(n_g, ex_in, ex_out, sems).two_level_finish()

    any_spec = pl.BlockSpec(memory_space=pl.ANY)
    tail = pl.BlockSpec((SUB, D), lambda n: (jnp.maximum(n * (tl // SUB) - 1, 0), 0))
    outs = _pcall(
        body, name="mixer_fwd", grid=(nsteps,),
        in_specs=[pl.BlockSpec((tl, AW), cur), pl.BlockSpec((LB, KVW), prev), pl.BlockSpec((tl, KVW), cur),
                  pl.BlockSpec((LB, KVW), prev), pl.BlockSpec((tl, KVW), cur), _const((8, LB)),
                  tail, pl.BlockSpec((tl, D), cur), _const((8, D)), _const((1, D)),
                  pl.BlockSpec((tl, LB), cur), _const((8, LB)), pl.BlockSpec((tl, SW), cur), _const((1, SW)),
                  _const(e64.shape), _const(e128.shape)]
        + [any_spec] * ne,
        out_specs=[pl.BlockSpec((tl, AW), cur), pl.BlockSpec((tl, SW), cur), pl.BlockSpec((tl, SW), cur),
                   pl.BlockSpec((bps, NH * HD, NST), lambda n: (n, 0, 0)), pl.BlockSpec((tl, D), cur),
                   pl.BlockSpec((tl, LB), cur), pl.BlockSpec((tl, LB), cur),
                   stack, pl.BlockSpec((tl, D), cur), pl.BlockSpec((tl, SW), cur), stack, stack] + [any_spec] * ne,
        out_shape=[_sds((s, AW), BF16), _sds((s, SW), BF16), _sds((s, SW), F32), _sds((nb, NH * HD, NST), F32),
                   _sds((s, D), F32), _sds((s, LB), F32), _sds((s, LB), F32), _sds((nb, NH * LB, LB), F32),
                   _sds((s, D), F32), _sds((s, SW), F32), _sds((nb, NH * LB, LB), BF16),
                   _sds((nb, NH * LB, LB), BF16)]
        + ex_shapes,
        scratch_shapes=[pltpu.VMEM((NH * HD, NST), F32)] + _exchange_sems(ne),
        compiler_params=_params(),
    )(q, k, k, v, v, sinks8, xbc, xbc, conv_w8, conv_b, dtr, ssm_p, z, nw, e64, e128, *items)
    return outs[:n_out], outs[n_out:]


def _blocks_per_step(nb):
    return 2 if nb % 2 == 0 else 1


def _head_expanders():
    j = lax.broadcasted_iota(jnp.int32, (LB, NH * HD), 0)
    e64 = (lax.broadcasted_iota(jnp.int32, (LB, NH * HD), 1) // HD == j).astype(BF16)
    j = lax.broadcasted_iota(jnp.int32, (LB, NH * LB), 0)
    e128 = (lax.broadcasted_iota(jnp.int32, (LB, NH * LB), 1) // LB == j).astype(BF16)
    return e64, e128


def _outproj_ffn_fwd_loss(attn, yn, x, tgt, mod8, n2w, fnw, w_out, w_gu_t, w_down):
    s = x.shape[0]
    tf = min(256, s)

    def body(a_ref, y_ref, x_ref, t_ref, mod_ref, nw_ref, fw_ref, wo_ref, wgu_ref, wd_ref,
             x2_ref, h2_ref, mo_ref, mix_ref, gu_ref, act_ref, dx3_ref, sm_ref):
        i = pl.program_id(0)

        @pl.when(i == 0)
        def _():
            sm_ref[...] = jnp.zeros_like(sm_ref)

        mix = jnp.concatenate([a_ref[...], y_ref[...]], axis=1)
        mix_ref[...] = mix
        mo = _dot(mix, wo_ref[...])
        mo_ref[...] = mo.astype(BF16)
        x2 = x_ref[...] + mod_ref[2:3, :] * mo
        x2_ref[...] = x2
        h2 = _norm_mod_fwd(x2, nw_ref[...], mod_ref[3:4, :], mod_ref[4:5, :]).astype(BF16)
        h2_ref[...] = h2
        gu = _dot_nt(h2, wgu_ref[...])
        gu_ref[...] = gu.astype(BF16)
        g = gu[:, :DFF]
        act = (g * _sigmoid(g) * gu[:, DFF:]).astype(BF16)
        act_ref[...] = act
        ff = _dot(act, wd_ref[...])
        x3 = x2 + mod_ref[5:6, :] * ff
        r = lax.rsqrt(jnp.mean(x3 * x3, axis=-1, keepdims=True) + EPS)
        xh = x3 * r
        fw = fw_ref[...]
        err = xh * fw - t_ref[...]
        dy = err * (1.0 / D)
        dxh = dy * fw
        dx3 = r * (dxh - xh * jnp.mean(dxh * xh, axis=-1, keepdims=True))
        dx3_ref[...] = dx3
        sm_ref[0:1, :] += jnp.sum(dx3 * ff, axis=0, keepdims=True)
        sm_ref[1:2, :] += jnp.sum(dy * xh, axis=0, keepdims=True)
        sm_ref[2:3, :] += jnp.sum(err * err, axis=0, keepdims=True)

    return _pcall(
        body, name="outproj_ffn_fwd_loss", grid=(s // tf,),
        in_specs=[_rows(tf, AW), _rows(tf, SW), _rows(tf, D), _rows(tf, D), _const((8, D)), _const((1, D)),
                  _const((1, D)), _const((D, D)), _const((2 * DFF, D)), _const((DFF, D))],
        out_specs=[_rows(tf, D), _rows(tf, D), _rows(tf, D), _rows(tf, D), _rows(tf, 2 * DFF), _rows(tf, DFF),
                   _rows(tf, D), pl.BlockSpec((8, D), lambda i: (0, 0))],
        out_shape=[_sds((s, D), F32), _sds((s, D), BF16), _sds((s, D), BF16), _sds((s, D), BF16),
                   _sds((s, 2 * DFF), BF16), _sds((s, DFF), BF16), _sds((s, D), F32), _sds((8, D), F32)],
        compiler_params=_params(),
    )(attn, yn, x, tgt, mod8, n2w, fnw, w_out, w_gu_t, w_down)


def _ffn_bwd(dx3, gu, x2, mixout, mod8, n2w, w_gu, w_down, w_out):
    s = x2.shape[0]
    tb = min(256, s)

    def body(dx3_ref, gu_ref, x2_ref, mo_ref, mod_ref, nw_ref, wgu_ref, wd_ref, wo_ref,
             dx2_ref, dff_ref, dgu_ref, dmix_ref, dattn_ref, dyn_ref, sm_ref):
        i = pl.program_id(0)

        @pl.when(i == 0)
        def _():
            sm_ref[...] = jnp.zeros_like(sm_ref)

        dx3 = dx3_ref[...]
        dff = (dx3 * mod_ref[5:6, :]).astype(BF16)
        dff_ref[...] = dff
        dact = _dot_nt(dff, wd_ref[...])
        g = gu_ref[:, :DFF].astype(F32)
        u = gu_ref[:, DFF:].astype(F32)
        sg = _sigmoid(g)
        dgu = jnp.concatenate([dact * u * sg * (1.0 + g * (1.0 - sg)), dact * g * sg], axis=1).astype(BF16)
        dgu_ref[...] = dgu
        dh2 = _dot(dgu, wgu_ref[...])
        dxn, d_shift, d_scale, d_w = _norm_mod_bwd(x2_ref[...], dh2, nw_ref[...], mod_ref[4:5, :])
        dx2 = dx3 + dxn
        dx2_ref[...] = dx2
        sm_ref[0:1, :] += d_shift
        sm_ref[1:2, :] += d_scale
        sm_ref[2:3, :] += d_w
        sm_ref[3:4, :] += jnp.sum(dx2 * mo_ref[...].astype(F32), axis=0, keepdims=True)
        dmix = (dx2 * mod_ref[2:3, :]).astype(BF16)
        dmix_ref[...] = dmix
        dmi = _dot_nt(dmix, wo_ref[...])
        dattn_ref[...] = dmi[:, :AW].astype(BF16)
        dyn_ref[...] = dmi[:, AW:]

    return _pcall(
        body, name="ffn_bwd", grid=(s // tb,),
        in_specs=[_rows(tb, D), _rows(tb, 2 * DFF), _rows(tb, D), _rows(tb, D), _const((8, D)), _const((1, D)),
                  _const((2 * DFF, D)), _const((DFF, D)), _const((D, D))],
        out_specs=[_rows(tb, D), _rows(tb, D), _rows(tb, 2 * DFF), _rows(tb, D), _rows(tb, AW), _rows(tb, SW),
                   pl.BlockSpec((8, D), lambda i: (0, 0))],
        out_shape=[_sds((s, D), F32), _sds((s, D), BF16), _sds((s, 2 * DFF), BF16), _sds((s, D), BF16),
                   _sds((s, AW), BF16), _sds((s, SW), F32), _sds((8, D), F32)],
        compiler_params=_params(),
    )(dx3, gu, x2, mixout, mod8, n2w, w_gu, w_down, w_out)


def _ssd_bwd_block(i, n, *refs):
    def run(dyn_ref, y_ref, z_ref, x_ref, u_ref, dt_ref, acs_ref, dm_ref, sgu_ref, sgz_ref, cw_ref, sp_ref, nw_ref,
            hs_ref, e64_ref, e128_ref, dzxd_ref, sm_ref, dh_scr, dun_scr):
        @pl.when(i == 0)
        def _():
            dh_scr[...] = jnp.zeros_like(dh_scr)
            dun_scr[...] = jnp.zeros_like(dun_scr)
            sm_ref[...] = jnp.zeros_like(sm_ref)

        u, dt, acs, sg_u = u_ref[...], dt_ref[...], acs_ref[...], sgu_ref[...]
        xc = u * sg_u
        a_neg = -jnp.exp(sp_ref[1:2, :])
        ck = _SsdChunk(xc, dt, acs, sp_ref[...], e64_ref[...], e128_ref[...],
                       decay=[dm_ref[0, g * HPG * LB:(g + 1) * HPG * LB, :] for g in range(2)])
        h_all = hs_ref[0]
        dh_all = dh_scr[...]
        riota = lax.broadcasted_iota(jnp.int32, (LB, LB), 0)
        lane1 = lax.broadcasted_iota(jnp.int32, (1, LB), 1)

        z = z_ref[...]
        y = y_ref[...]
        sgz = sgz_ref[...]
        sz = z * sgz
        yz = y * sz
        nwv = nw_ref[...]
        dyn_v = dyn_ref[...]
        dyhat = dyn_v * nwv
        yhat_parts, dyz_parts = [], []
        for g in range(2):
            gs = slice(g * 256, (g + 1) * 256)
            t = yz[:, gs]
            rg = lax.rsqrt(jnp.mean(t * t, axis=-1, keepdims=True) + EPS)
            yh = t * rg
            dyh = dyhat[:, gs]
            yhat_parts.append(yh)
            dyz_parts.append(rg * (dyh - yh * jnp.mean(dyh * yh, axis=-1, keepdims=True)))
        yhat = jnp.concatenate(yhat_parts, axis=1)
        dyz = jnp.concatenate(dyz_parts, axis=1)
        sm_ref[5:6, 0:SW] += jnp.sum(dyn_v * yhat, axis=0, keepdims=True)
        dy = dyz * sz
        dzxd_ref[:, 0:SW] = (dyz * y * sgz * (1.0 + z * (1.0 - sgz))).astype(BF16)

        cat = lambda parts: jnp.concatenate(parts, axis=1)
        dxs, dbs, dcs, dhp, g_cat, de_x, ddte_x, ddt_x, ddsk_x = ([] for _ in range(9))
        dacs_t = jnp.zeros((LB, LB), F32)
        hsum = jnp.zeros((1, LB), F32)
        for g in range(2):
            gl = slice(g * GW, (g + 1) * GW)
            xs_g, xdt, bgb, cgb = ck.xs[g], ck.xdt[g], ck.bb[g], ck.cb16[g]
            m_st, dm_st = ck.m_st[g], ck.dm_st[g]
            dt_x, e_x, dte_x = ck.dt_x[:, gl], ck.e_x[:, gl], ck.dte_x[:, gl]
            xdtb = xdt.astype(BF16)
            hg, dhn = h_all[gl, :], dh_all[gl, :]
            hb, dhnb = hg.astype(BF16), dhn.astype(BF16)
            dy_g = dy[:, gl]
            ddsk_x.append(jnp.sum(dy_g * xs_g, axis=0, keepdims=True))
            dy_bd = ck.block_diag(dy_g).astype(BF16)
            dm4 = _dot_nt(dy_bd, xdtb)
            dxdt = _dot_tn(m_st.astype(BF16), dy_bd)
            gmat = dm4 * m_st
            dcbm = dm4 * dm_st
            dcb = dcbm[0:LB] + dcbm[LB:2 * LB] + dcbm[2 * LB:3 * LB] + dcbm[3 * LB:4 * LB]
            g_cat.append(cat([gmat[hh * LB:(hh + 1) * LB, :] for hh in range(HPG)]))
            for hh in range(HPG):
                j = HPG * g + hh
                col_sum = jnp.sum(gmat[hh * LB:(hh + 1) * LB, :], axis=0, keepdims=True)
                dacs_t = dacs_t - jnp.where(riota == j, col_sum, 0.0)
                hsl = slice(hh * HD, (hh + 1) * HD)
                hsum = hsum + jnp.where(lane1 == j, jnp.sum(dhn[hsl, :] * hg[hsl, :]), 0.0)
            dchb = (dy_g * e_x).astype(BF16)
            dcg = _dot(dchb, hb)
            dh_prev = _dot_tn(dchb, cgb)
            de_x.append(dy_g * _dot_nt(cgb, hb))
            dxs_s = _dot_nt(bgb, dhnb)
            dbg = _dot((xdt * dte_x).astype(BF16), dhnb)
            dxdt = dxdt + dxs_s * dte_x
            ddte_x.append(dxs_s * xdt)
            dhp.append(dhn * ck.elast_rows(g) + dh_prev)
            dxs.append(dy_g * ck.dsk_x[:, gl] + dxdt * dt_x)
            ddt_x.append(dxdt * xs_g)
            dcbb = dcb.astype(BF16)
            dbs.append(dbg + _dot_tn(dcbb, cgb))
            dcs.append(dcg + _dot(dcbb, bgb))
        dh_scr[...] = jnp.concatenate(dhp, axis=0)
        red = _dot_nt_sel(jnp.concatenate([cat(de_x), cat(ddte_x), cat(ddt_x)], axis=0), ck.e64, 1)
        de_c, ddte_c, ddt_c = red[0:LB], red[LB:2 * LB], red[2 * LB:3 * LB]
        ddsk = _dot_nt_sel(jnp.broadcast_to(cat(ddsk_x), (SUB, NH * HD)), ck.e64, 2)[0:1, :]
        t1 = ddte_c * ck.dte_all
        dalast = jnp.sum(t1, axis=0, keepdims=True) + hsum * ck.elast
        dacs = (_dot_nt_sel(cat(g_cat), ck.e128, 2) + de_c * ck.e_all - t1 + jnp.transpose(dacs_t)
                + jnp.where(riota == LB - 1, dalast, 0.0))
        da = _cumsum_rows(dacs, True)
        ddt = ddt_c + da * a_neg
        da_log = jnp.sum(da * dt, axis=0, keepdims=True) * a_neg
        ddtr = ddt * (1.0 - jnp.exp(-dt))
        dzxd_ref[:, SW + D:ZXD] = ddtr.astype(BF16)
        sm_ref[6:7, 0:LB] += jnp.sum(ddtr, axis=0, keepdims=True)
        sm_ref[6:7, LB:2 * LB] += da_log
        sm_ref[6:7, 2 * LB:3 * LB] += ddsk

        du = cat(dxs + dbs + dcs) * (sg_u * (1.0 + u * (1.0 - sg_u)))
        xv = x_ref[...]
        adv = [du] + _conv_advances(du, dun_scr[...])
        sm_ref[0:1, :] += jnp.sum(du, axis=0, keepdims=True)
        dxbc = cw_ref[CONVK - 1:CONVK, :] * du
        for j in range(CONVK):
            sm_ref[CONVK - j:CONVK + 1 - j, :] += jnp.sum(adv[j] * xv, axis=0, keepdims=True)
            if j:
                dxbc = dxbc + cw_ref[CONVK - 1 - j:CONVK - j, :] * adv[j]
        dun_scr[...] = du[0:SUB, :]
        dzxd_ref[:, SW:SW + D] = dxbc.astype(BF16)

    run(*refs)


def _attn_bwd_block(i, n, q_ref, kp_ref, kc_ref, vp_ref, vc_ref, o_ref, do_ref, cos_ref, sin_ref, pr_ref, ps_ref,
                    dq_ref, dkv_ref, ds_ref, ck_scr, cv_scr):
    @pl.when(i == 0)
    def _():
        ds_ref[...] = jnp.zeros_like(ds_ref)
        ck_scr[...] = jnp.zeros_like(ck_scr)
        cv_scr[...] = jnp.zeros_like(cv_scr)

    qv, ov, dov = q_ref[...], o_ref[...], do_ref[...]
    kcat = jnp.concatenate([kp_ref[...], kc_ref[...]], axis=0)
    vcat = jnp.concatenate([vp_ref[...], vc_ref[...]], axis=0)
    upper = _upper_mask()
    srow = lax.broadcasted_iota(jnp.int32, (8, LB), 0)
    slane = lax.broadcasted_iota(jnp.int32, (8, LB), 1)
    dsink = jnp.zeros((8, LB), F32)
    dq_g, dk_g, dv_g = [], [], []
    for g in range(NQ // QPG):
        sl = slice(g * HD, (g + 1) * HD)
        qg = _stack_heads(qv, g)
        dog = _stack_heads(dov, g)
        rows = slice(g * QPG * LB, (g + 1) * QPG * LB)
        probs = pr_ref[0, rows, :].astype(F32)
        psink = ps_ref[0, rows, :].astype(F32)
        delta = _row_sums_wide(dog.astype(F32) * _stack_heads(ov, g).astype(F32), 2)
        dsc = probs * (_band(upper, _dot_nt(dog, vcat[0:LB, sl]), _dot_nt(dog, vcat[LB:2 * LB, sl])) - delta)
        sink_terms = (psink * delta)[:, 0:1]
        for hh in range(QPG):
            dsink = dsink - jnp.where((srow == QPG * g + hh) & (slane == 0),
                                      jnp.sum(sink_terms[hh * LB:(hh + 1) * LB, :]), 0.0)
        ds_p = jnp.where(upper, dsc, 0.0).astype(BF16)
        ds_c = jnp.where(upper, 0.0, dsc).astype(BF16)
        dq_g.append((_dot(ds_p, kcat[0:LB, sl]) + _dot(ds_c, kcat[LB:2 * LB, sl])) * ATT_SCALE)
        dk_g.append(jnp.concatenate([_dot_tn(ds_p, qg), _dot_tn(ds_c, qg)], axis=0) * ATT_SCALE)
        dv_g.append(jnp.concatenate([_dot_tn(jnp.where(upper, probs, 0.0).astype(BF16), dog),
                                     _dot_tn(jnp.where(upper, 0.0, probs).astype(BF16), dog)], axis=0))
    ds_ref[...] += dsink
    cs = cos_ref[...]
    sn = sin_ref[...]
    dk2 = jnp.concatenate(dk_g, axis=1)
    dv2 = jnp.concatenate(dv_g, axis=1)
    for a, tile in enumerate(_unstack_heads(dq_g)):
        dq_ref[:, a * LB:(a + 1) * LB] = _rope(tile, cs, sn, True).astype(BF16)
    dkv_ref[:, 0:KVW] = _rope(ck_scr[...] + dk2[LB:2 * LB, :], cs, sn, True).astype(BF16)
    dkv_ref[:, KVW:2 * KVW] = (cv_scr[...] + dv2[LB:2 * LB, :]).astype(BF16)
    ck_scr[...] = dk2[0:LB, :]
    cv_scr[...] = dv2[0:LB, :]


def _mixer_bwd(q, k, v, o, do, cos, sin, probs, psink, dyn, y, z, xbc, u, dtv, acs, decay, sgu, sgz, conv_w8, ssm_p, nw,
               hs, scatters):
    s = q.shape[0]
    nb = s // LB
    bps = _blocks_per_step(nb)
    nsteps = nb // bps
    tl = bps * LB
    cur = lambda i: (nsteps - 1 - i, 0)
    prev = lambda i: (jnp.maximum((nsteps - 1 - i) * bps - 1, 0), 0)
    n_in = 27
    items, ex_shapes, n_g = _exchange_items([], scatters)
    ne = len(items)
    e64, e128 = _head_expanders()
    stack = pl.BlockSpec((bps, NH * LB, LB), lambda i: (nsteps - 1 - i, 0, 0))

    def body(*refs):
        i = pl.program_id(0)
        (q_ref, kp_ref, kc_ref, vp_ref, vc_ref, o_ref, do_ref, cos_ref, sin_ref, pr_ref, ps_ref,
         dyn_ref, y_ref, z_ref, x_ref, u_ref, dt_ref, acs_ref, dm_ref, sgu_ref, sgz_ref, cw_ref, sp_ref, nw_ref,
         hs_ref, e64_ref, e128_ref) = refs[:n_in]
        ex_in = refs[n_in:n_in + ne]
        dp_ref, ds_ref, sm_ref = refs[n_in + ne:n_in + ne + 3]
        ex_out = refs[n_in + ne + 3:n_in + 2 * ne + 3]
        ck_scr, cv_scr, dh_scr, dun_scr = refs[n_in + 2 * ne + 3:n_in + 2 * ne + 7]
        sems = refs[n_in + 2 * ne + 7:]

        @pl.when(i == 0)
        def _():
            _Exchange(n_g, ex_in, ex_out, sems).start()

        for back in range(bps):
            sub = bps - 1 - back
            step = i * bps + back
            blk = (nsteps - 1 - i) * bps + sub
            r = slice(sub * LB, (sub + 1) * LB)
            before = slice((sub - 1) * LB, sub * LB)
            one = slice(sub, sub + 1)
            _attn_bwd_block(step, blk, q_ref.at[r], kp_ref if sub == 0 else kc_ref.at[before], kc_ref.at[r],
                            vp_ref if sub == 0 else vc_ref.at[before], vc_ref.at[r], o_ref.at[r], do_ref.at[r],
                            cos_ref.at[r], sin_ref.at[r], pr_ref.at[one], ps_ref.at[one],
                            dp_ref.at[r, O_Q:O_K], dp_ref.at[r, O_K:O_Z], ds_ref, ck_scr, cv_scr)
            _ssd_bwd_block(step, blk, dyn_ref.at[r], y_ref.at[r], z_ref.at[r], x_ref.at[r], u_ref.at[r], dt_ref.at[r],
                           acs_ref.at[r], dm_ref.at[one], sgu_ref.at[r], sgz_ref.at[r], cw_ref, sp_ref, nw_ref,
                           hs_ref.at[one], e64_ref, e128_ref, dp_ref.at[r, O_Z:INP], sm_ref, dh_scr, dun_scr)

        @pl.when(i == nsteps - 1)
        def _():
            _Exchange(n_g, ex_in, ex_out, sems).finish()

    any_spec = pl.BlockSpec(memory_space=pl.ANY)
    outs = _pcall(
        body, name="mixer_bwd", grid=(nsteps,),
        in_specs=[pl.BlockSpec((tl, AW), cur), pl.BlockSpec((LB, KVW), prev), pl.BlockSpec((tl, KVW), cur),
                  pl.BlockSpec((LB, KVW), prev), pl.BlockSpec((tl, KVW), cur), pl.BlockSpec((tl, AW), cur),
                  pl.BlockSpec((tl, AW), cur), pl.BlockSpec((tl, LB), cur), pl.BlockSpec((tl, LB), cur),
                  stack, stack,
                  pl.BlockSpec((tl, SW), cur), pl.BlockSpec((tl, SW), cur), pl.BlockSpec((tl, SW), cur),
                  pl.BlockSpec((tl, D), cur), pl.BlockSpec((tl, D), cur), pl.BlockSpec((tl, LB), cur),
                  pl.BlockSpec((tl, LB), cur), stack, pl.BlockSpec((tl, D), cur), pl.BlockSpec((tl, SW), cur),
                  _const((8, D)), _const((8, LB)), _const((1, SW)),
                  pl.BlockSpec((bps, NH * HD, NST), lambda i: (nsteps - 1 - i, 0, 0)),
                  _const(e64.shape), _const(e128.shape)] + [any_spec] * ne,
        out_specs=[pl.BlockSpec((tl, INP), cur), pl.BlockSpec((8, LB), lambda i: (0, 0)),
                   pl.BlockSpec((8, D), lambda i: (0, 0))] + [any_spec] * ne,
        out_shape=[_sds((s, INP), BF16), _sds((8, LB), F32), _sds((8, D), F32)] + ex_shapes,
        scratch_shapes=[pltpu.VMEM((LB, KVW), F32), pltpu.VMEM((LB, KVW), F32),
                        pltpu.VMEM((NH * HD, NST), F32), pltpu.VMEM((SUB, D), F32)]
        + _exchange_sems(ne),
        compiler_params=_params(),
    )(q, k, k, v, v, o, do, cos, sin, probs, psink, dyn, y, z, xbc, u, dtv, acs, decay, sgu, sgz, conv_w8, ssm_p, nw, hs,
      e64, e128, *items)
    return outs[0], outs[1], outs[2], outs[3:]


def _inproj_bwd(dproj, x, dx2, mod8, n1w, w_in_t, scatters, smalls):
    s = x.shape[0]
    tt = min(512, s)
    nt = s // tt
    items, ex_shapes, n_g = _exchange_items([], scatters)
    ne = len(items)
    n_in = 10

    def body(*refs):
        dp_ref, x_ref, dx2_ref, mod_ref, nw_ref, w_ref, f_ref, b_ref, s_ref, k_ref = refs[:n_in]
        ex_in = refs[n_in:n_in + ne]
        gx_ref, sm_ref = refs[n_in + ne:n_in + 2 + ne]
        ex_out = refs[n_in + 2 + ne:n_in + 2 + 2 * ne]
        gpack_ref = refs[n_in + 2 + 2 * ne]
        pack_scr = refs[n_in + 3 + 2 * ne]
        sems = refs[n_in + 4 + 2 * ne:n_in + 7 + 2 * ne]
        pack_sems = refs[n_in + 7 + 2 * ne:]
        i = pl.program_id(0)

        @pl.when(i == 0)
        def _():
            sm_ref[...] = jnp.zeros_like(sm_ref)
            _Exchange(n_g, ex_in, ex_out, sems).start()

        w = w_ref[...]
        hr = tt // 2
        dh1 = [_dot(dp_ref[h * hr:(h + 1) * hr, :], w) for h in range(2)]
        sums = jnp.zeros((3, D), F32)
        for h in range(2):
            rows = slice(h * hr, (h + 1) * hr)
            dxn, d_shift, d_scale, d_w = _norm_mod_bwd(x_ref[rows, :], dh1[h], nw_ref[...], mod_ref[1:2, :])
            gx_ref[rows, :] = dx2_ref[rows, :] + dxn
            sums = sums + jnp.concatenate([d_shift, d_scale, d_w], axis=0)
        sm_ref[0:3, :] += sums

        @pl.when(i == nt - 1)
        def _():
            _pack_rows(f_ref, b_ref, s_ref, sm_ref, k_ref, pack_scr)
            small = _Exchange(1, [pack_scr], [gpack_ref], pack_sems)
            small.start()
            _Exchange(n_g, ex_in, ex_out, sems).finish()
            small.finish()

    any_spec = pl.BlockSpec(memory_space=pl.ANY)
    outs = _pcall(
        body, name="inproj_bwd", grid=(nt,),
        in_specs=[_rows(tt, INP), _rows(tt, D), _rows(tt, D), _const((8, D)), _const((1, D)), _const((INP, D)),
                  _const((8, D)), _const((8, D)), _const((8, D)), _const((8, LB))]
        + [any_spec] * ne,
        out_specs=[_rows(tt, D), pl.BlockSpec((8, D), lambda i: (0, 0))] + [any_spec] * (ne + 1),
        out_shape=[_sds((s, D), F32), _sds((8, D), F32)] + ex_shapes + [_sds((N_DEV, PACK_ROWS, D), F32)],
        scratch_shapes=[pltpu.VMEM((PACK_ROWS, D), F32)] + _exchange_sems(ne) + _exchange_sems(1),
        compiler_params=_params(),
    )(dproj, x, dx2, mod8, n1w, w_in_t, *smalls, *items)
    return outs[0], outs[2:2 + ne], outs[2 + ne]


def _wgrad(a, b, name):
    s, m = a.shape
    n = b.shape[1]
    tk = min(2048, s)
    wide = (1408, 1024, 512)
    tm = next((t for t in wide if m % t == 0), m)
    tn = n if n <= 2048 else _largest_divisor(n, wide)
    nk = s // tk

    def body(a_ref, b_ref, o_ref, acc):
        kk = pl.program_id(2)

        @pl.when(kk == 0)
        def _():
            acc[...] = jnp.zeros_like(acc)

        acc[...] += _dot_tn(a_ref[...], b_ref[...])

        @pl.when(kk == nk - 1)
        def _():
            o_ref[...] = acc[...].astype(BF16)

    return _pcall(
        body, name=name, grid=(m // tm, n // tn, nk),
        in_specs=[pl.BlockSpec((tk, tm), lambda i, j, kk: (kk, i)), pl.BlockSpec((tk, tn), lambda i, j, kk: (kk, j))],
        out_specs=pl.BlockSpec((tm, tn), lambda i, j, kk: (i, j)),
        out_shape=_sds((m, n), BF16),
        scratch_shapes=[pltpu.VMEM((tm, tn), F32)],
        compiler_params=_params(3),
    )(a, b)


PACK_ROWS = 24


def _pack_rows(f_ref, b_ref, s_ref, i_ref, k_ref, o_ref):
    o_ref[...] = jnp.zeros_like(o_ref)
    o_ref[0:2, :] = i_ref[0:2, :]
    o_ref[2:3, :] = b_ref[3:4, :]
    o_ref[3:5, :] = b_ref[0:2, :]
    o_ref[5:6, :] = f_ref[0:1, :]
    o_ref[6:7, :] = i_ref[2:3, :]
    o_ref[7:8, :] = b_ref[2:3, :]
    o_ref[8:9, :] = f_ref[1:2, :]
    o_ref[9:14, :] = s_ref[0:5, :]
    o_ref[14:15, :] = s_ref[5:6, :]
    o_ref[15:16, 0:3 * LB] = s_ref[6:7, 0:3 * LB]
    lane = lax.broadcasted_iota(jnp.int32, (1, LB), 1)
    sk = jnp.zeros((1, LB), F32)
    for h in range(NQ):
        sk = sk + jnp.where(lane == h, k_ref[h:h + 1, 0:1], 0.0)
    o_ref[15:16, 3 * LB:4 * LB] = sk
    o_ref[16:17, :] = f_ref[2:3, :]


def _exchange(gathers, scatters, name, two_level=False):
    items, shapes, n_g = _exchange_items(gathers, scatters)
    n = len(items)
    assert not (two_level and scatters)

    def body(*refs):
        ex = _Exchange(n_g, refs[:n], refs[n:2 * n], refs[2 * n:])
        if two_level:
            ex.gather_two_level()
        else:
            ex.start()
            ex.finish()

    any_spec = pl.BlockSpec(memory_space=pl.ANY)
    return _pcall(
        body, name=name, in_specs=[any_spec] * n, out_specs=[any_spec] * n, out_shape=shapes,
        scratch_shapes=_exchange_sems(n),
    )(*items)


def _exchange_items(gathers, scatters):
    items = list(gathers) + list(scatters)
    shapes = [_sds((N_DEV,) + a.shape, a.dtype) for a in gathers] + [_sds(a.shape, a.dtype) for a in scatters]
    return items, shapes, len(gathers)


def _exchange_sems(n):
    return [pltpu.SemaphoreType.DMA((n, N_DEV - 1)), pltpu.SemaphoreType.DMA((n, N_DEV - 1)),
            pltpu.SemaphoreType.DMA((n,))]


class _Exchange:
    def __init__(self, n_g, ins, outs, sems):
        self.n_g, self.ins, self.outs = n_g, ins, outs
        self.send_sems, self.recv_sems, self.loc_sems = sems
        xi, yi, ci = lax.axis_index("x"), lax.axis_index("y"), lax.axis_index("c")
        self.me = 4 * xi + 2 * yi + ci
        self.peers = []
        for r in range(1, N_DEV):
            px = 1 - xi if r & 4 else xi
            py = 1 - yi if r & 2 else yi
            pc = 1 - ci if r & 1 else ci
            self.peers.append(((px, py, pc), 4 * px + 2 * py + pc))

    def _copy(self, t, r, landing):
        dev, peer = self.peers[r]
        src = self.ins[t] if t < self.n_g else self.ins[t].at[peer]
        return pltpu.make_async_remote_copy(
            src_ref=src, dst_ref=self.outs[t].at[landing], send_sem=self.send_sems.at[t, r],
            recv_sem=self.recv_sems.at[t, r], device_id=dev, device_id_type=pl.DeviceIdType.MESH)

    def _local(self, t):
        src = self.ins[t] if t < self.n_g else self.ins[t].at[self.me]
        return pltpu.make_async_copy(src, self.outs[t].at[self.me], self.loc_sems.at[t])

    def start(self):
        for t in range(len(self.ins)):
            self._local(t).start()
            for r in range(N_DEV - 1):
                self._copy(t, r, self.me).start()

    def finish(self):
        n = len(self.ins)
        for t in range(n):
            for r in range(N_DEV - 1):
                self._copy(t, r, self.peers[r][1]).wait_recv()
        for t in range(n):
            for r in range(N_DEV - 1):
                self._copy(t, r, self.me).wait_send()
            self._local(t).wait()

    def gather_two_level(self):
        self.two_level_start()
        self.two_level_relay()
        self.two_level_finish()

    DIRECT = (0, 1, 3, 5)

    def two_level_start(self):
        for t in range(len(self.ins)):
            self._local(t).start()
            for r in self.DIRECT:
                self._copy(t, r, self.me).start()

    def _relay(self, t, r):
        peer = self.peers[r][1]
        return pltpu.make_async_remote_copy(
            src_ref=self.outs[t].at[peer], dst_ref=self.outs[t].at[peer], send_sem=self.send_sems.at[t, r + 1],
            recv_sem=self.recv_sems.at[t, r + 1], device_id=self.peers[0][0], device_id_type=pl.DeviceIdType.MESH)

    def two_level_relay(self):
        for t in range(len(self.ins)):
            for r in self.DIRECT[1:]:
                self._copy(t, r, self.peers[r][1]).wait_recv()
                self._relay(t, r).start()

    def two_level_finish(self):
        n = len(self.ins)
        for t in range(n):
            for r in (0, 2, 4, 6):
                self._copy(t, r, self.peers[r][1]).wait_recv()
        for t in range(n):
            for r in self.DIRECT:
                self._copy(t, r, self.me).wait_send()
            for r in self.DIRECT[1:]:
                self._relay(t, r).wait_send()
            self._local(t).wait()


def _prologue(c8, w_in_tb, cw8, w_cols, b_cols):
    ncol = w_cols.shape[1]

    def body(c_ref, win_ref, cw_ref, w_ref, b_ref, gc_ref, gin_ref, gcw_ref, gmod_ref, call_scr, mod_scr, loc_sem,
             *sems):
        big = _Exchange(2, [win_ref, cw_ref], [gin_ref, gcw_ref], sems[0:3])
        big.two_level_start()
        small = _Exchange(1, [c_ref], [gc_ref], sems[3:6])
        small.start()
        small.finish()
        landed = pltpu.make_async_copy(gc_ref, call_scr, loc_sem)
        landed.start()
        landed.wait()
        cv = call_scr[:, 0, :]
        sc = (cv * _sigmoid(cv)).astype(BF16)
        mod_scr[...] = _dot(sc, w_ref[...].astype(BF16)) + b_ref[...]
        mods = _Exchange(1, [mod_scr], [gmod_ref], sems[6:9])
        mods.start()
        mods.finish()
        big.two_level_relay()
        big.two_level_finish()

    any_spec = pl.BlockSpec(memory_space=pl.ANY)
    vmem_spec = pl.BlockSpec(memory_space=pltpu.VMEM)
    return _pcall(
        body, name="prologue", in_specs=[any_spec, any_spec, any_spec, vmem_spec, vmem_spec],
        out_specs=[any_spec] * 4,
        out_shape=[_sds((N_DEV,) + c8.shape, F32), _sds((N_DEV,) + w_in_tb.shape, BF16),
                   _sds((N_DEV,) + cw8.shape, F32), _sds((N_DEV, N_DEV, ncol), F32)],
        scratch_shapes=[pltpu.VMEM((N_DEV,) + c8.shape, F32), pltpu.VMEM((N_DEV, ncol), F32),
                        pltpu.SemaphoreType.DMA] + _exchange_sems(2) + _exchange_sems(1) + _exchange_sems(1),
        compiler_params=pltpu.CompilerParams(vmem_limit_bytes=VMEM_LIMIT),
    )(c8, w_in_tb, cw8, w_cols, b_cols)


def _adamw(w, g, m, v):
    m2 = ADAM_B1 * m + (1.0 - ADAM_B1) * g
    v2 = ADAM_B2 * v + (1.0 - ADAM_B2) * (g * g)
    m_hat = m2 / (1.0 - ADAM_B1 ** ADAM_STEP)
    v_hat = v2 / (1.0 - ADAM_B2 ** ADAM_STEP)
    delta = -ADAM_LR * (m_hat / (jnp.sqrt(v_hat) + ADAM_EPS) + ADAM_WD * w)
    return delta, m2, v2


def _sum_adamw(parts, w, m, v, name):
    rws, cols = w.shape
    tr = next((t for t in (256, 176, 128) if rws % t == 0), rws)

    def body(p_ref, w_ref, m_ref, v_ref, g_ref, d_ref, mo_ref, vo_ref):
        g = p_ref[0].astype(F32)
        for dev in range(1, N_DEV):
            g = g + p_ref[dev].astype(F32)
        g_ref[...] = g
        d_ref[...], mo_ref[...], vo_ref[...] = _adamw(w_ref[...], g, m_ref[...], v_ref[...])

    blk = pl.BlockSpec((tr, cols), lambda i: (i, 0))
    return _pcall(
        body, name=name, grid=(rws // tr,),
        in_specs=[pl.BlockSpec((N_DEV, tr, cols), lambda i: (0, i, 0)), blk, blk, blk],
        out_specs=[blk] * 4, out_shape=[_sds((rws, cols), F32)] * 4, compiler_params=_params(),
    )(parts, w, m, v)


def _wada_adamw(c_all, dmod_cols, w, m, v):
    rws, cols = w.shape
    tr = 256

    def body(c_ref, dm_ref, w_ref, m_ref, v_ref, g_ref, d_ref, mo_ref, vo_ref):
        cv = c_ref[...]
        sc = (cv * _sigmoid(cv)).astype(BF16)
        g = _dot_tn(sc, dm_ref[...].astype(BF16))
        g_ref[...] = g
        d_ref[...], mo_ref[...], vo_ref[...] = _adamw(w_ref[...], g, m_ref[...], v_ref[...])

    blk = pl.BlockSpec((tr, cols), lambda i: (i, 0))
    return _pcall(
        body, name="wada_adamw", grid=(rws // tr,),
        in_specs=[pl.BlockSpec((N_DEV, tr), lambda i: (0, i)), pl.BlockSpec((N_DEV, cols), lambda i: (0, 0)),
                  blk, blk, blk],
        out_specs=[blk] * 4, out_shape=[_sds((rws, cols), F32)] * 4, compiler_params=_params(),
    )(c_all, dmod_cols, w, m, v)


def _small_reduce(packs):
    def body(p_ref, o_ref):
        tot = p_ref[0]
        for dev in range(1, N_DEV):
            tot = tot + p_ref[dev]
        o_ref[...] = tot
        o_ref[16:17, :] = jnp.zeros((1, D), F32) + (0.5 / D) * jnp.sum(tot[16:17, :])

    return _pcall(body, name="small_reduce", out_shape=_sds((PACK_ROWS, D), F32))(packs)


def _adamw_many(ws, gs, ms, vs):
    k = len(ws)

    def body(*refs):
        for i in range(k):
            w_ref, g_ref, m_ref, v_ref = refs[i], refs[k + i], refs[2 * k + i], refs[3 * k + i]
            d_ref, mo_ref, vo_ref = refs[4 * k + i], refs[5 * k + i], refs[6 * k + i]
            d_ref[...], mo_ref[...], vo_ref[...] = _adamw(w_ref[...], g_ref[...], m_ref[...], v_ref[...])

    shp = [_sds(w.shape, F32) for w in ws]
    outs = _pcall(body, name="adamw_small", out_shape=shp * 3)(*ws, *gs, *ms, *vs)
    return outs[:k], outs[k:2 * k], outs[2 * k:]


def kernel(x, c, positions, w_ada, b_ada, norm1_w, w_in, conv_w, conv_b, dt_bias, a_log, d_skip, attn_sinks, ssm_norm_w, w_out, norm2_w, w_gate_up, w_down, final_norm_w, loss_target, m_w_ada, m_b_ada, m_norm1_w, m_w_in, m_conv_w, m_conv_b, m_dt_bias, m_a_log, m_d_skip, m_attn_sinks, m_ssm_norm_w, m_w_out, m_norm2_w, m_w_gate_up, m_w_down, m_final_norm_w, v_w_ada, v_b_ada, v_norm1_w, v_w_in, v_conv_w, v_conv_b, v_dt_bias, v_a_log, v_d_skip, v_attn_sinks, v_ssm_norm_w, v_w_out, v_norm2_w, v_w_gate_up, v_w_down, v_final_norm_w):
    s = x.shape[1]
    me = 4 * lax.axis_index("x") + 2 * lax.axis_index("y") + lax.axis_index("c")
    ada_cols = N_MOD * D // N_DEV

    c8 = jnp.pad(c, ((0, 7), (0, 0)))
    cw8 = jnp.pad(conv_w[0], ((0, 8 - CONVK), (0, 0)))
    w_in_t, m_w_in_t, v_w_in_t = jnp.transpose(w_in[0]), jnp.transpose(m_w_in[0]), jnp.transpose(v_w_in[0])
    w_gu_t, m_w_gu_t, v_w_gu_t = (jnp.transpose(w_gate_up[0]), jnp.transpose(m_w_gate_up[0]),
                                  jnp.transpose(v_w_gate_up[0]))
    b_cols = lax.dynamic_slice(b_ada, (0, me * ada_cols), (1, ada_cols))
    g_c, g_in, g_cw, g_mod = _prologue(c8, w_in_t.astype(BF16), cw8, w_ada[0], b_cols)
    c_all = g_c[:, 0, :]
    w_in_f = jnp.pad(g_in.reshape(IN_PROJ, D), ((0, INP - IN_PROJ), (0, 0)))
    conv_w8 = jnp.transpose(g_cw, (1, 0, 2)).reshape(8, D)
    mod = lax.dynamic_index_in_dim(g_mod, me, axis=1, keepdims=False).reshape(N_MOD, D)
    mod8 = jnp.pad(mod, ((0, 8 - N_MOD), (0, 0)))

    half = HD // 2
    inv_freq = ROPE_THETA ** (-jnp.arange(half, dtype=F32) / half)
    invf = jnp.tile(inv_freq, LB // half).reshape(1, LB)
    lanes = lambda a: jnp.pad(a, ((0, 0), (0, LB - a.shape[1])))
    ssm_p = jnp.pad(jnp.concatenate([lanes(dt_bias), lanes(a_log), lanes(d_skip)], axis=0), ((0, 5), (0, 0)))
    sinks8 = jnp.broadcast_to(attn_sinks.reshape(NQ, 1), (NQ, LB))

    xs, tgt, fnw = x[0], loss_target[0], final_norm_w.reshape(1, D)

    q, k, v, z, xbc, dtr, h1, cos, sin = _inproj_fwd(xs, positions[0].reshape(s, 1), invf, mod8, norm1_w, w_in_f)
    (attn, yn, y, hs, conv_u, dtv, acs, decay, sgu, sgz, probs, psink), (g_out, g_gu, g_down) = _mixer_fwd(
        q, k, v, sinks8, xbc, conv_w8, conv_b, dtr, ssm_p, z, ssm_norm_w,
        [w_out[0].astype(BF16), w_gu_t.astype(BF16), w_down[0].astype(BF16)])
    w_out_f = g_out.reshape(D, D)
    w_gu_f = g_gu.reshape(2 * DFF, D)
    w_down_f = g_down.reshape(DFF, D)
    x2, h2, mo, mix, gu, act, dx3, sm_f = _outproj_ffn_fwd_loss(attn, yn, xs, tgt, mod8, norm2_w, fnw, w_out_f, w_gu_f,
                                                                 w_down_f)

    dx2, dff, dgu, dmix, dattn, dyn, sm_b = _ffn_bwd(dx3, gu, x2, mo, mod8, norm2_w, w_gu_f, w_down_f, w_out_f)
    p_gu = _wgrad(dgu, h2, "wgrad_gate_up").reshape(N_DEV, 2 * DFF // N_DEV, D)
    p_down = _wgrad(act, dff, "wgrad_down").reshape(N_DEV, DFF // N_DEV, D)
    p_out = _wgrad(mix, dmix, "wgrad_out").reshape(N_DEV, D // N_DEV, D)
    dproj, dsink, sm_s, (r_gu, r_down, r_out) = _mixer_bwd(
        q, k, v, attn, dattn, cos, sin, probs, psink, dyn, y, z, xbc, conv_u, dtv, acs, decay, sgu, sgz, conv_w8, ssm_p,
        ssm_norm_w, hs, [p_gu, p_down, p_out])
    p_in = _wgrad(dproj, h1, "wgrad_in")[:IN_PROJ].reshape(N_DEV, IN_PROJ // N_DEV, D)
    gx, (r_in,), g_pack = _inproj_bwd(dproj, xs, dx2, mod8, norm1_w, w_in_f, [p_in], (sm_f, sm_b, sm_s, dsink))

    tot = _small_reduce(g_pack)
    loss = tot[16, 0]
    dmod_all = g_pack[:, 0:N_MOD, :].reshape(N_DEV, N_MOD * D)
    dmod_cols = lax.dynamic_slice(dmod_all, (0, me * ada_cols), (N_DEV, ada_cols))

    big = {
        "w_ada": _wada_adamw(c_all, dmod_cols, w_ada[0], m_w_ada[0], v_w_ada[0]),
        "w_in": [jnp.transpose(t) for t in _sum_adamw(r_in, w_in_t, m_w_in_t, v_w_in_t, "adamw_in")],
        "w_out": _sum_adamw(r_out, w_out[0], m_w_out[0], v_w_out[0], "adamw_out"),
        "w_gate_up": [jnp.transpose(t) for t in _sum_adamw(r_gu, w_gu_t, m_w_gu_t, v_w_gu_t, "adamw_gate_up")],
        "w_down": _sum_adamw(r_down, w_down[0], m_w_down[0], v_w_down[0], "adamw_down"),
    }
    small_names = ["b_ada", "norm1_w", "conv_w", "conv_b", "dt_bias", "a_log", "d_skip", "attn_sinks", "ssm_norm_w",
                   "norm2_w", "final_norm_w"]
    row15 = tot[15:16, :]
    small_g = {
        "b_ada": tot[0:N_MOD, :].reshape(1, N_MOD * D),
        "norm1_w": tot[6:7, :],
        "conv_w": lax.dynamic_slice(tot[10:14, :], (0, me * (D // N_DEV)), (CONVK, D // N_DEV)),
        "conv_b": tot[9:10, :],
        "dt_bias": row15[:, 0:NH],
        "a_log": row15[:, LB:LB + NH],
        "d_skip": row15[:, 2 * LB:2 * LB + NH],
        "attn_sinks": row15[:, 3 * LB:3 * LB + NQ],
        "ssm_norm_w": tot[14:15, 0:SW],
        "norm2_w": tot[7:8, :],
        "final_norm_w": tot[8:9, :],
    }
    small_w = {"b_ada": b_ada, "norm1_w": norm1_w, "conv_w": conv_w[0], "conv_b": conv_b, "dt_bias": dt_bias,
               "a_log": a_log, "d_skip": d_skip, "attn_sinks": attn_sinks, "ssm_norm_w": ssm_norm_w,
               "norm2_w": norm2_w, "final_norm_w": final_norm_w.reshape(1, D)}
    small_m = {"b_ada": m_b_ada, "norm1_w": m_norm1_w, "conv_w": m_conv_w[0], "conv_b": m_conv_b,
               "dt_bias": m_dt_bias, "a_log": m_a_log, "d_skip": m_d_skip, "attn_sinks": m_attn_sinks,
               "ssm_norm_w": m_ssm_norm_w, "norm2_w": m_norm2_w, "final_norm_w": m_final_norm_w.reshape(1, D)}
    small_v = {"b_ada": v_b_ada, "norm1_w": v_norm1_w, "conv_w": v_conv_w[0], "conv_b": v_conv_b,
               "dt_bias": v_dt_bias, "a_log": v_a_log, "d_skip": v_d_skip, "attn_sinks": v_attn_sinks,
               "ssm_norm_w": v_ssm_norm_w, "norm2_w": v_norm2_w, "final_norm_w": v_final_norm_w.reshape(1, D)}
    s_d, s_m, s_v = _adamw_many([small_w[k] for k in small_names], [small_g[k] for k in small_names],
                                [small_m[k] for k in small_names], [small_v[k] for k in small_names])

    order = ["w_ada", "b_ada", "norm1_w", "w_in", "conv_w", "conv_b", "dt_bias", "a_log", "d_skip", "attn_sinks",
             "ssm_norm_w", "w_out", "norm2_w", "w_gate_up", "w_down", "final_norm_w"]
    lead = {"w_ada", "w_in", "conv_w", "w_out", "w_gate_up", "w_down"}
    grads, deltas, new_m, new_v = [], [], [], []
    for name in order:
        if name in big:
            g, d, m2, v2 = big[name]
        else:
            i = small_names.index(name)
            g, d, m2, v2 = small_g[name], s_d[i], s_m[i], s_v[i]
        if name in lead:
            g, d, m2, v2 = g[None], d[None], m2[None], v2[None]
        if name == "final_norm_w":
            g, d, m2, v2 = g.reshape(D), d.reshape(D), m2.reshape(D), v2.reshape(D)
        grads.append(g)
        deltas.append(d)
        new_m.append(m2)
        new_v.append(v2)
    return (loss, gx[None], *grads, *deltas, *new_m, *new_v)
```

```python
import functools
import math

import jax
import jax.numpy as jnp
from jax import lax
from jax.experimental import pallas as pl
from jax.experimental.pallas import tpu as pltpu

F32 = jnp.float32
BF16 = jnp.bfloat16

N_DEV = 8
D = 1024
HD = 64
NQ = 8
AW = 512
KVW = 128
SW = 512
NST = 128
NH = 8
LB = 128
CONVK = 4
DFF = 2816
N_MOD = 6
IN_PROJ = 2312
INP = 2432
O_Q, O_K, O_V, O_Z, O_XBC, O_DT = 0, 512, 640, 768, 1280, 2304
ZXD = INP - O_Z
EPS = 1e-6
NEG = -1e30
ROPE_THETA = 10000.0
VMEM_LIMIT = 56 * 1024 * 1024

ADAM_LR = 0.001
ADAM_B1 = 0.9
ADAM_B2 = 0.999
ADAM_EPS = 1e-08
ADAM_WD = 0.01
ADAM_STEP = 10

NT_DIMS = (((1,), (1,)), ((), ()))
TN_DIMS = (((0,), (0,)), ((), ()))


def _pcall(body, **kw):
    return pl.pallas_call(body, **kw)


def _sds(shape, dtype):
    return jax.ShapeDtypeStruct(shape, dtype)


def _params(n_grid=1):
    return pltpu.CompilerParams(dimension_semantics=("arbitrary",) * n_grid, vmem_limit_bytes=VMEM_LIMIT)


def _const(shape):
    return pl.BlockSpec(shape, lambda *_: (0,) * len(shape), pipeline_mode=pl.Buffered(1))


def _largest_divisor(n, candidates):
    for cand in candidates:
        if n % cand == 0:
            return cand
    raise ValueError(f"no tile in {candidates} divides {n}")


def _rows(t, w):
    return pl.BlockSpec((t, w), lambda i: (i, 0))


def _dot(a, b):
    return jnp.dot(a, b, preferred_element_type=F32)


def _dot_nt(a, b):
    return lax.dot_general(a, b, NT_DIMS, preferred_element_type=F32)


def _dot_tn(a, b):
    return lax.dot_general(a, b, TN_DIMS, preferred_element_type=F32)


def _sigmoid(v):
    return 1.0 / (1.0 + jnp.exp(-v))


def _softplus(v):
    return jnp.maximum(v, 0.0) + jnp.log1p(jnp.exp(-jnp.abs(v)))


def _rope_sign_mask(shape):
    lane = lax.broadcasted_iota(jnp.int32, shape, 1)
    return (lane % HD) < (HD // 2)


def _rope(t, cs, sn, inverse):
    r_dn = pltpu.roll(t, HD // 2, 1)
    r_up = pltpu.roll(t, LB - HD // 2, 1)
    first = _rope_sign_mask(t.shape)
    if inverse:
        rot = jnp.where(first, r_up, -r_dn)
    else:
        rot = jnp.where(first, -r_up, r_dn)
    return t * cs + rot * sn


def _norm_mod_fwd(xv, nw, shift, scale):
    r = lax.rsqrt(jnp.mean(xv * xv, axis=-1, keepdims=True) + EPS)
    xh = xv * r
    return (xh * nw) * (1.0 + scale) + shift


def _norm_mod_bwd(xv, dh, nw, scale):
    r = lax.rsqrt(jnp.mean(xv * xv, axis=-1, keepdims=True) + EPS)
    xh = xv * r
    xn = xh * nw
    d_shift = jnp.sum(dh, axis=0, keepdims=True)
    d_scale = jnp.sum(dh * xn, axis=0, keepdims=True)
    dxn = dh * (1.0 + scale)
    d_w = jnp.sum(dxn * xh, axis=0, keepdims=True)
    dxh = dxn * nw
    dx = r * (dxh - xh * jnp.mean(dxh * xh, axis=-1, keepdims=True))
    return dx, d_shift, d_scale, d_w


def _inproj_fwd(x, pos, invf, mod8, n1w, w_in):
    s = x.shape[0]
    tt = min(512, s)

    def body(x_ref, pos_ref, invf_ref, mod_ref, nw_ref, w_ref,
             q_ref, k_ref, v_ref, z_ref, xbc_ref, dtr_ref, h1_ref, cos_ref, sin_ref):
        h = _norm_mod_fwd(x_ref[...], nw_ref[...], mod_ref[0:1, :], mod_ref[1:2, :])
        hb = h.astype(BF16)
        h1_ref[...] = hb
        proj = _dot_nt(hb, w_ref[...])
        ang = pos_ref[...].astype(F32) * invf_ref[...]
        cs = jnp.cos(ang)
        sn = jnp.sin(ang)
        cos_ref[...] = cs
        sin_ref[...] = sn
        for a in range(AW // LB):
            q_ref[:, a * LB:(a + 1) * LB] = _rope(proj[:, O_Q + a * LB:O_Q + (a + 1) * LB], cs, sn, False).astype(BF16)
        k_ref[...] = _rope(proj[:, O_K:O_V], cs, sn, False).astype(BF16)
        v_ref[...] = proj[:, O_V:O_Z].astype(BF16)
        z_ref[...] = proj[:, O_Z:O_XBC]
        xbc_ref[...] = proj[:, O_XBC:O_DT]
        dtr_ref[...] = proj[:, O_DT:INP]

    return _pcall(
        body, name="inproj_fwd", grid=(s // tt,),
        in_specs=[_rows(tt, D), _rows(tt, 1), _const((1, LB)), _const((8, D)), _const((1, D)), _const((INP, D))],
        out_specs=[_rows(tt, AW), _rows(tt, KVW), _rows(tt, KVW), _rows(tt, SW), _rows(tt, D), _rows(tt, LB),
                   _rows(tt, D), _rows(tt, LB), _rows(tt, LB)],
        out_shape=[_sds((s, AW), BF16), _sds((s, KVW), BF16), _sds((s, KVW), BF16), _sds((s, SW), F32),
                   _sds((s, D), F32), _sds((s, LB), F32), _sds((s, D), BF16), _sds((s, LB), F32), _sds((s, LB), F32)],
        compiler_params=_params(),
    )(x, pos, invf, mod8, n1w, w_in)


QPG = 4
ATT_SCALE = 1.0 / math.sqrt(HD)


def _stack_heads(val, g):
    return jnp.concatenate([val[:, (QPG * g + hh) * HD:(QPG * g + hh + 1) * HD] for hh in range(QPG)], axis=0)


def _unstack_heads(groups):
    pieces = [grp[hh * LB:(hh + 1) * LB, :] for grp in groups for hh in range(QPG)]
    return [jnp.concatenate(pieces[2 * a:2 * a + 2], axis=1) for a in range(NQ // 2)]


def _upper_mask():
    row = lax.broadcasted_iota(jnp.int32, (QPG * LB, LB), 0)
    col = lax.broadcasted_iota(jnp.int32, (QPG * LB, LB), 1)
    return col > (row % LB)


def _sink_wide(sinks, g):
    return jnp.concatenate([jnp.broadcast_to(sinks[QPG * g + hh:QPG * g + hh + 1, 0:1], (LB, LB))
                            for hh in range(QPG)], axis=0)


def _row_sums_wide(v, terms):
    return _dot_sel(v, jnp.ones((v.shape[1], LB), BF16), terms)


def _band(upper, prev_part, cur_part):
    return jnp.where(upper, prev_part, cur_part)


def _attn_scores(n, qg, kcat, upper):
    sp = _dot_nt(qg, kcat[0:LB, :]) * ATT_SCALE
    sc = _dot_nt(qg, kcat[LB:2 * LB, :]) * ATT_SCALE
    return _band(upper, jnp.where(n > 0, sp, NEG), sc)


def _attn_softmax(comb, sink):
    m = jnp.maximum(jnp.max(comb, axis=-1, keepdims=True), sink)
    p = jnp.exp(comb - m)
    es = jnp.exp(sink - m)
    return p, es, _row_sums_wide(p, 1) + es


def _attn_fwd_block(n, q_ref, kp_ref, kc_ref, vp_ref, vc_ref, sink_ref, o_ref, pr_ref, ps_ref):
    qv = q_ref[...]
    kcat = jnp.concatenate([kp_ref[...], kc_ref[...]], axis=0)
    vcat = jnp.concatenate([vp_ref[...], vc_ref[...]], axis=0)
    sinks = sink_ref[...]
    upper = _upper_mask()
    outs = []
    for g in range(NQ // QPG):
        sl = slice(g * HD, (g + 1) * HD)
        rows = slice(g * QPG * LB, (g + 1) * QPG * LB)
        p, es, denom = _attn_softmax(_attn_scores(n, _stack_heads(qv, g), kcat[:, sl], upper), _sink_wide(sinks, g))
        rden = 1.0 / denom
        pr_ref[0, rows, :] = (p * rden).astype(BF16)
        ps_ref[0, rows, :] = (es * rden).astype(BF16)
        outs.append((_dot(jnp.where(upper, p, 0.0).astype(BF16), vcat[0:LB, sl])
                     + _dot(jnp.where(upper, 0.0, p).astype(BF16), vcat[LB:2 * LB, sl])) * rden[:, 0:HD])
    for g, grp in enumerate(outs):
        for hh in range(QPG):
            h = QPG * g + hh
            o_ref[:, h * HD:(h + 1) * HD] = grp[hh * LB:(hh + 1) * LB, :].astype(BF16)


def _cumsum_rows(a, reverse):
    row = lax.broadcasted_iota(jnp.int32, a.shape, 0)
    step = 1
    while step < LB:
        if reverse:
            a = a + jnp.where(row < LB - step, pltpu.roll(a, LB - step, 0), 0.0)
        else:
            a = a + jnp.where(row >= step, pltpu.roll(a, step, 0), 0.0)
        step *= 2
    return a


SUB = 8


def _conv_shifts(tail, cur):
    row = lax.broadcasted_iota(jnp.int32, tail.shape, 0)
    out = [cur]
    for j in range(1, CONVK):
        rolled = pltpu.roll(cur, j, 0)
        top = jnp.where(row < j, pltpu.roll(tail, j, 0), rolled[0:SUB, :])
        out.append(jnp.concatenate([top, rolled[SUB:, :]], axis=0))
    return out


def _conv_advances(du, head):
    row = lax.broadcasted_iota(jnp.int32, head.shape, 0)
    out = []
    for j in range(1, CONVK):
        rolled = pltpu.roll(du, LB - j, 0)
        bottom = jnp.where(row >= SUB - j, pltpu.roll(head, SUB - j, 0), rolled[LB - SUB:, :])
        out.append(jnp.concatenate([rolled[:LB - SUB, :], bottom], axis=0))
    return out


def _split(v, terms):
    out = []
    for _ in range(terms - 1):
        t = v.astype(BF16)
        out.append(t)
        v = v - t.astype(F32)
    out.append(v.astype(BF16))
    return out


def _dot_sel(v, sel, terms):
    parts = [_dot(t, sel) for t in _split(v, terms)]
    return functools.reduce(lambda a, b: a + b, parts)


def _dot_nt_sel(v, sel, terms):
    parts = [_dot_nt(t, sel) for t in _split(v, terms)]
    return functools.reduce(lambda a, b: a + b, parts)


def _ssd_pre(xt_ref, xc_ref, cw_ref, cb_ref, dtr_ref, sp_ref, n):
    cur = xc_ref[...]
    tail = jnp.where(n > 0, xt_ref[...], 0.0)
    sh = _conv_shifts(tail, cur)
    u = cb_ref[...] + cw_ref[CONVK - 1:CONVK, :] * sh[0]
    for j in range(1, CONVK):
        u = u + cw_ref[CONVK - 1 - j:CONVK - j, :] * sh[j]
    dt = _softplus(dtr_ref[...] + sp_ref[0:1, :])
    acs = _cumsum_rows(dt * -jnp.exp(sp_ref[1:2, :]), False)
    return u, dt, acs


def _gated_norm_fwd(y, z, sgz, nw):
    yz = y * (z * sgz)
    parts = []
    for g in range(2):
        t = yz[:, g * 256:(g + 1) * 256]
        parts.append(t * lax.rsqrt(jnp.mean(t * t, axis=-1, keepdims=True) + EPS))
    return jnp.concatenate(parts, axis=1) * nw


HPG = 4
GW = HPG * HD


class _SsdChunk:
    def __init__(self, xc, dt, acs, spv, e64, e128, decay=None):
        self.e64, self.e128 = e64, e128
        alast = acs[LB - 1:LB, :]
        self.e_all = jnp.exp(acs)
        self.dte_all = jnp.exp(alast - acs)
        self.elast = jnp.exp(alast)
        wide = _dot_sel(jnp.concatenate([dt, self.e_all, self.dte_all], axis=0), e64, 2)
        self.dt_x, self.e_x, self.dte_x = wide[0:LB], wide[LB:2 * LB], wide[2 * LB:3 * LB]
        self.dsk_x = _dot_sel(spv, e64, 3)[2:3, :]
        if decay is None:
            acs_t = jnp.transpose(acs)
            ac_x = _dot_sel(acs, e128, 3)
            row = lax.broadcasted_iota(jnp.int32, (HPG * LB, LB), 0)
            col = lax.broadcasted_iota(jnp.int32, (HPG * LB, LB), 1)
            causal = (row % LB) >= col
        lane = lax.broadcasted_iota(jnp.int32, (LB, GW), 1)
        self.head_lanes = [(lane >= hh * HD) & (lane < (hh + 1) * HD) for hh in range(HPG)]
        self.xs, self.xdt, self.b, self.c, self.bb, self.cb16, self.cbm, self.dm_st, self.m_st = ([] for _ in range(9))
        for g in range(2):
            heads = range(HPG * g, HPG * (g + 1))
            if decay is None:
                ac_st = jnp.concatenate([ac_x[:, j * LB:(j + 1) * LB] for j in heads], axis=0)
                ar_st = jnp.concatenate([jnp.broadcast_to(acs_t[j:j + 1, :], (LB, LB)) for j in heads], axis=0)
                dm_st = jnp.exp(jnp.where(causal, ac_st - ar_st, NEG))
            else:
                dm_st = decay[g]
            bg = xc[:, SW + g * NST:SW + (g + 1) * NST]
            cg = xc[:, SW + 2 * NST + g * NST:SW + 2 * NST + (g + 1) * NST]
            bgb, cgb = bg.astype(BF16), cg.astype(BF16)
            cbm = _dot_nt(cgb, bgb)
            xs_g = xc[:, g * GW:(g + 1) * GW]
            self.xs.append(xs_g)
            self.xdt.append(xs_g * self.dt_x[:, g * GW:(g + 1) * GW])
            self.b.append(bg)
            self.c.append(cg)
            self.bb.append(bgb)
            self.cb16.append(cgb)
            self.cbm.append(cbm)
            self.dm_st.append(dm_st)
            self.m_st.append(jnp.concatenate([cbm] * HPG, axis=0) * dm_st)

    def elast_rows(self, g):
        return jnp.concatenate([jnp.broadcast_to(self.elast[:, j:j + 1], (HD, NST))
                                for j in range(HPG * g, HPG * (g + 1))], axis=0)

    def diag_blocks(self, stacked):
        out = stacked[(HPG - 1) * LB:HPG * LB, :]
        for hh in range(HPG - 2, -1, -1):
            out = jnp.where(self.head_lanes[hh], stacked[hh * LB:(hh + 1) * LB, :], out)
        return out

    def block_diag(self, v):
        return jnp.concatenate([jnp.where(self.head_lanes[hh], v, 0.0) for hh in range(HPG)], axis=0)


def _ssd_fwd_block(n, xt_ref, xc_ref, cw_ref, cb_ref, dtr_ref, sp_ref, z_ref, nw_ref, e64_ref, e128_ref,
                   yn_ref, y_ref, hs_ref, u_ref, dt_ref, acs_ref, dm_ref, h_scr):
    @pl.when(n == 0)
    def _():
        h_scr[...] = jnp.zeros_like(h_scr)

    h_all = h_scr[...]
    hs_ref[0] = h_all
    u, dt, acs = _ssd_pre(xt_ref, xc_ref, cw_ref, cb_ref, dtr_ref, sp_ref, n)
    u_ref[...] = u
    dt_ref[...] = dt
    acs_ref[...] = acs
    xc = u * _sigmoid(u)
    ck = _SsdChunk(xc, dt, acs, sp_ref[...], e64_ref[...], e128_ref[...])
    dm_ref[0] = jnp.concatenate(ck.dm_st, axis=0)
    ys, hn = [], []
    for g in range(2):
        gl = slice(g * GW, (g + 1) * GW)
        xdt = ck.xdt[g]
        hg = h_all[gl, :]
        y_diag = ck.diag_blocks(_dot(ck.m_st[g].astype(BF16), xdt.astype(BF16)))
        y_off = ck.e_x[:, gl] * _dot_nt(ck.cb16[g], hg.astype(BF16))
        ys.append(y_diag + y_off + ck.xs[g] * ck.dsk_x[:, gl])
        hn.append(hg * ck.elast_rows(g) + _dot_tn((xdt * ck.dte_x[:, gl]).astype(BF16), ck.bb[g]))
    h_scr[...] = jnp.concatenate(hn, axis=0)
    y = jnp.concatenate(ys, axis=1)
    y_ref[...] = y
    z = z_ref[...]
    yn_ref[...] = _gated_norm_fwd(y, z, _sigmoid(z), nw_ref[...]).astype(BF16)


def _mixer_fwd(q, k, v, sinks8, xbc, conv_w8, conv_b, dtr, ssm_p, z, nw, gathers):
    s = q.shape[0]
    nb = s // LB
    bps = _blocks_per_step(nb)
    nsteps = nb // bps
    tl = bps * LB
    cur = lambda n: (n, 0)
    prev = lambda n: (jnp.maximum(n * bps - 1, 0), 0)
    items, ex_shapes, n_g = _exchange_items(gathers, [])
    ne = len(items)

    n_in, n_out = 16, 10
    relay_step = (nsteps - 1) // 2
    stack = pl.BlockSpec((bps, NH * LB, LB), lambda n: (n, 0, 0))
    e64, e128 = _head_expanders()

    def body(*refs):
        (q_ref, kp_ref, kc_ref, vp_ref, vc_ref, sink_ref, xt_ref, xc_ref, cw_ref, cb_ref, dtr_ref, sp_ref, z_ref,
         nw_ref, e64_ref, e128_ref) = refs[:n_in]
        ex_in = refs[n_in:n_in + ne]
        (o_ref, yn_ref, y_ref, hs_ref, u_ref, dt_ref, acs_ref, dm_ref, pr_ref,
         ps_ref) = refs[n_in + ne:n_in + n_out + ne]
        ex_out = refs[n_in + n_out + ne:n_in + n_out + 2 * ne]
        h_scr = refs[n_in + n_out + 2 * ne]
        sems = refs[n_in + n_out + 1 + 2 * ne:]
        n = pl.program_id(0)

        @pl.when(n == 0)
        def _():
            _Exchange(n_g, ex_in, ex_out, sems).two_level_start()

        for sub in range(bps):
            blk = n * bps + sub
            r = slice(sub * LB, (sub + 1) * LB)
            before = slice((sub - 1) * LB, sub * LB)
            one = slice(sub, sub + 1)
            _attn_fwd_block(blk, q_ref.at[r], kp_ref if sub == 0 else kc_ref.at[before], kc_ref.at[r],
                            vp_ref if sub == 0 else vc_ref.at[before], vc_ref.at[r], sink_ref,
                            o_ref.at[r], pr_ref.at[one], ps_ref.at[one])
            _ssd_fwd_block(blk, xt_ref if sub == 0 else xc_ref.at[sub * LB - SUB:sub * LB], xc_ref.at[r], cw_ref,
                           cb_ref, dtr_ref.at[r], sp_ref, z_ref.at[r], nw_ref, e64_ref, e128_ref,
                           yn_ref.at[r], y_ref.at[r], hs_ref.at[one], u_ref.at[r], dt_ref.at[r], acs_ref.at[r],
                           dm_ref.at[one], h_scr)

        @pl.when(n == relay_step)
        def _():
            _Exchange(n_g, ex_in, ex_out, sems).two_level_relay()

        @pl.when(n == nsteps - 1)
        def _():
            _Exchange(n_g, ex_in, ex_out, sems).two_level_finish()

    any_spec = pl.BlockSpec(memory_space=pl.ANY)
    tail = pl.BlockSpec((SUB, D), lambda n: (jnp.maximum(n * (tl // SUB) - 1, 0), 0))
    outs = _pcall(
        body, name="mixer_fwd", grid=(nsteps,),
        in_specs=[pl.BlockSpec((tl, AW), cur), pl.BlockSpec((LB, KVW), prev), pl.BlockSpec((tl, KVW), cur),
                  pl.BlockSpec((LB, KVW), prev), pl.BlockSpec((tl, KVW), cur), _const((8, LB)),
                  tail, pl.BlockSpec((tl, D), cur), _const((8, D)), _const((1, D)),
                  pl.BlockSpec((tl, LB), cur), _const((8, LB)), pl.BlockSpec((tl, SW), cur), _const((1, SW)),
                  _const(e64.shape), _const(e128.shape)]
        + [any_spec] * ne,
        out_specs=[pl.BlockSpec((tl, AW), cur), pl.BlockSpec((tl, SW), cur), pl.BlockSpec((tl, SW), cur),
                   pl.BlockSpec((bps, NH * HD, NST), lambda n: (n, 0, 0)), pl.BlockSpec((tl, D), cur),
                   pl.BlockSpec((tl, LB), cur), pl.BlockSpec((tl, LB), cur),
                   stack, stack, stack] + [any_spec] * ne,
        out_shape=[_sds((s, AW), BF16), _sds((s, SW), BF16), _sds((s, SW), F32), _sds((nb, NH * HD, NST), F32),
                   _sds((s, D), F32), _sds((s, LB), F32), _sds((s, LB), F32), _sds((nb, NH * LB, LB), F32),
                   _sds((nb, NH * LB, LB), BF16), _sds((nb, NH * LB, LB), BF16)]
        + ex_shapes,
        scratch_shapes=[pltpu.VMEM((NH * HD, NST), F32)] + _exchange_sems(ne),
        compiler_params=_params(),
    )(q, k, k, v, v, sinks8, xbc, xbc, conv_w8, conv_b, dtr, ssm_p, z, nw, e64, e128, *items)
    return outs[:n_out], outs[n_out:]


def _blocks_per_step(nb):
    return 2 if nb % 2 == 0 else 1


def _head_expanders():
    j = lax.broadcasted_iota(jnp.int32, (LB, NH * HD), 0)
    e64 = (lax.broadcasted_iota(jnp.int32, (LB, NH * HD), 1) // HD == j).astype(BF16)
    j = lax.broadcasted_iota(jnp.int32, (LB, NH * LB), 0)
    e128 = (lax.broadcasted_iota(jnp.int32, (LB, NH * LB), 1) // LB == j).astype(BF16)
    return e64, e128


def _outproj_ffn_fwd_loss(attn, yn, x, tgt, mod8, n2w, fnw, w_out, w_gu_t, w_down):
    s = x.shape[0]
    tf = min(256, s)

    def body(a_ref, y_ref, x_ref, t_ref, mod_ref, nw_ref, fw_ref, wo_ref, wgu_ref, wd_ref,
             x2_ref, h2_ref, mo_ref, mix_ref, gu_ref, act_ref, dx3_ref, sm_ref):
        i = pl.program_id(0)

        @pl.when(i == 0)
        def _():
            sm_ref[...] = jnp.zeros_like(sm_ref)

        mix = jnp.concatenate([a_ref[...], y_ref[...]], axis=1)
        mix_ref[...] = mix
        mo = _dot(mix, wo_ref[...])
        mo_ref[...] = mo.astype(BF16)
        x2 = x_ref[...] + mod_ref[2:3, :] * mo
        x2_ref[...] = x2
        h2 = _norm_mod_fwd(x2, nw_ref[...], mod_ref[3:4, :], mod_ref[4:5, :]).astype(BF16)
        h2_ref[...] = h2
        gu = _dot_nt(h2, wgu_ref[...])
        gu_ref[...] = gu.astype(BF16)
        g = gu[:, :DFF]
        act = (g * _sigmoid(g) * gu[:, DFF:]).astype(BF16)
        act_ref[...] = act
        ff = _dot(act, wd_ref[...])
        x3 = x2 + mod_ref[5:6, :] * ff
        r = lax.rsqrt(jnp.mean(x3 * x3, axis=-1, keepdims=True) + EPS)
        xh = x3 * r
        fw = fw_ref[...]
        err = xh * fw - t_ref[...]
        dy = err * (1.0 / D)
        dxh = dy * fw
        dx3 = r * (dxh - xh * jnp.mean(dxh * xh, axis=-1, keepdims=True))
        dx3_ref[...] = dx3
        sm_ref[0:1, :] += jnp.sum(dx3 * ff, axis=0, keepdims=True)
        sm_ref[1:2, :] += jnp.sum(dy * xh, axis=0, keepdims=True)
        sm_ref[2:3, :] += jnp.sum(err * err, axis=0, keepdims=True)

    return _pcall(
        body, name="outproj_ffn_fwd_loss", grid=(s // tf,),
        in_specs=[_rows(tf, AW), _rows(tf, SW), _rows(tf, D), _rows(tf, D), _const((8, D)), _const((1, D)),
                  _const((1, D)), _const((D, D)), _const((2 * DFF, D)), _const((DFF, D))],
        out_specs=[_rows(tf, D), _rows(tf, D), _rows(tf, D), _rows(tf, D), _rows(tf, 2 * DFF), _rows(tf, DFF),
                   _rows(tf, D), pl.BlockSpec((8, D), lambda i: (0, 0))],
        out_shape=[_sds((s, D), F32), _sds((s, D), BF16), _sds((s, D), BF16), _sds((s, D), BF16),
                   _sds((s, 2 * DFF), BF16), _sds((s, DFF), BF16), _sds((s, D), F32), _sds((8, D), F32)],
        compiler_params=_params(),
    )(attn, yn, x, tgt, mod8, n2w, fnw, w_out, w_gu_t, w_down)


def _ffn_bwd(dx3, gu, x2, mixout, mod8, n2w, w_gu, w_down, w_out):
    s = x2.shape[0]
    tb = min(256, s)

    def body(dx3_ref, gu_ref, x2_ref, mo_ref, mod_ref, nw_ref, wgu_ref, wd_ref, wo_ref,
             dx2_ref, dff_ref, dgu_ref, dmix_ref, dattn_ref, dyn_ref, sm_ref):
        i = pl.program_id(0)

        @pl.when(i == 0)
        def _():
            sm_ref[...] = jnp.zeros_like(sm_ref)

        dx3 = dx3_ref[...]
        dff = (dx3 * mod_ref[5:6, :]).astype(BF16)
        dff_ref[...] = dff
        dact = _dot_nt(dff, wd_ref[...])
        g = gu_ref[:, :DFF].astype(F32)
        u = gu_ref[:, DFF:].astype(F32)
        sg = _sigmoid(g)
        dgu = jnp.concatenate([dact * u * sg * (1.0 + g * (1.0 - sg)), dact * g * sg], axis=1).astype(BF16)
        dgu_ref[...] = dgu
        dh2 = _dot(dgu, wgu_ref[...])
        dxn, d_shift, d_scale, d_w = _norm_mod_bwd(x2_ref[...], dh2, nw_ref[...], mod_ref[4:5, :])
        dx2 = dx3 + dxn
        dx2_ref[...] = dx2
        sm_ref[0:1, :] += d_shift
        sm_ref[1:2, :] += d_scale
        sm_ref[2:3, :] += d_w
        sm_ref[3:4, :] += jnp.sum(dx2 * mo_ref[...].astype(F32), axis=0, keepdims=True)
        dmix = (dx2 * mod_ref[2:3, :]).astype(BF16)
        dmix_ref[...] = dmix
        dmi = _dot_nt(dmix, wo_ref[...])
        dattn_ref[...] = dmi[:, :AW].astype(BF16)
        dyn_ref[...] = dmi[:, AW:]

    return _pcall(
        body, name="ffn_bwd", grid=(s // tb,),
        in_specs=[_rows(tb, D), _rows(tb, 2 * DFF), _rows(tb, D), _rows(tb, D), _const((8, D)), _const((1, D)),
                  _const((2 * DFF, D)), _const((DFF, D)), _const((D, D))],
        out_specs=[_rows(tb, D), _rows(tb, D), _rows(tb, 2 * DFF), _rows(tb, D), _rows(tb, AW), _rows(tb, SW),
                   pl.BlockSpec((8, D), lambda i: (0, 0))],
        out_shape=[_sds((s, D), F32), _sds((s, D), BF16), _sds((s, 2 * DFF), BF16), _sds((s, D), BF16),
                   _sds((s, AW), BF16), _sds((s, SW), F32), _sds((8, D), F32)],
        compiler_params=_params(),
    )(dx3, gu, x2, mixout, mod8, n2w, w_gu, w_down, w_out)


def _ssd_bwd_block(i, n, *refs):
    def run(dyn_ref, y_ref, z_ref, x_ref, u_ref, dt_ref, acs_ref, dm_ref, cw_ref, sp_ref, nw_ref,
            hs_ref, e64_ref, e128_ref, dzxd_ref, sm_ref, dh_scr, dun_scr):
        @pl.when(i == 0)
        def _():
            dh_scr[...] = jnp.zeros_like(dh_scr)
            dun_scr[...] = jnp.zeros_like(dun_scr)
            sm_ref[...] = jnp.zeros_like(sm_ref)

        u, dt, acs = u_ref[...], dt_ref[...], acs_ref[...]
        sg_u = _sigmoid(u)
        xc = u * sg_u
        a_neg = -jnp.exp(sp_ref[1:2, :])
        ck = _SsdChunk(xc, dt, acs, sp_ref[...], e64_ref[...], e128_ref[...],
                       decay=[dm_ref[0, g * HPG * LB:(g + 1) * HPG * LB, :] for g in range(2)])
        h_all = hs_ref[0]
        dh_all = dh_scr[...]
        riota = lax.broadcasted_iota(jnp.int32, (LB, LB), 0)
        lane1 = lax.broadcasted_iota(jnp.int32, (1, LB), 1)

        z = z_ref[...]
        y = y_ref[...]
        sgz = _sigmoid(z)
        sz = z * sgz
        yz = y * sz
        nwv = nw_ref[...]
        dyn_v = dyn_ref[...]
        dyhat = dyn_v * nwv
        yhat_parts, dyz_parts = [], []
        for g in range(2):
            gs = slice(g * 256, (g + 1) * 256)
            t = yz[:, gs]
            rg = lax.rsqrt(jnp.mean(t * t, axis=-1, keepdims=True) + EPS)
            yh = t * rg
            dyh = dyhat[:, gs]
            yhat_parts.append(yh)
            dyz_parts.append(rg * (dyh - yh * jnp.mean(dyh * yh, axis=-1, keepdims=True)))
        yhat = jnp.concatenate(yhat_parts, axis=1)
        dyz = jnp.concatenate(dyz_parts, axis=1)
        sm_ref[5:6, 0:SW] += jnp.sum(dyn_v * yhat, axis=0, keepdims=True)
        dy = dyz * sz
        dzxd_ref[:, 0:SW] = (dyz * y * sgz * (1.0 + z * (1.0 - sgz))).astype(BF16)

        cat = lambda parts: jnp.concatenate(parts, axis=1)
        dxs, dbs, dcs, dhp, g_cat, de_x, ddte_x, ddt_x, ddsk_x = ([] for _ in range(9))
        dacs_t = jnp.zeros((LB, LB), F32)
        hsum = jnp.zeros((1, LB), F32)
        for g in range(2):
            gl = slice(g * GW, (g + 1) * GW)
            xs_g, xdt, bgb, cgb = ck.xs[g], ck.xdt[g], ck.bb[g], ck.cb16[g]
            m_st, dm_st = ck.m_st[g], ck.dm_st[g]
            dt_x, e_x, dte_x = ck.dt_x[:, gl], ck.e_x[:, gl], ck.dte_x[:, gl]
            xdtb = xdt.astype(BF16)
            hg, dhn = h_all[gl, :], dh_all[gl, :]
            hb, dhnb = hg.astype(BF16), dhn.astype(BF16)
            dy_g = dy[:, gl]
            ddsk_x.append(jnp.sum(dy_g * xs_g, axis=0, keepdims=True))
            dy_bd = ck.block_diag(dy_g).astype(BF16)
            dm4 = _dot_nt(dy_bd, xdtb)
            dxdt = _dot_tn(m_st.astype(BF16), dy_bd)
            gmat = dm4 * m_st
            dcbm = dm4 * dm_st
            dcb = dcbm[0:LB] + dcbm[LB:2 * LB] + dcbm[2 * LB:3 * LB] + dcbm[3 * LB:4 * LB]
            g_cat.append(cat([gmat[hh * LB:(hh + 1) * LB, :] for hh in range(HPG)]))
            for hh in range(HPG):
                j = HPG * g + hh
                col_sum = jnp.sum(gmat[hh * LB:(hh + 1) * LB, :], axis=0, keepdims=True)
                dacs_t = dacs_t - jnp.where(riota == j, col_sum, 0.0)
                hsl = slice(hh * HD, (hh + 1) * HD)
                hsum = hsum + jnp.where(lane1 == j, jnp.sum(dhn[hsl, :] * hg[hsl, :]), 0.0)
            dchb = (dy_g * e_x).astype(BF16)
            dcg = _dot(dchb, hb)
            dh_prev = _dot_tn(dchb, cgb)
            de_x.append(dy_g * _dot_nt(cgb, hb))
            dxs_s = _dot_nt(bgb, dhnb)
            dbg = _dot((xdt * dte_x).astype(BF16), dhnb)
            dxdt = dxdt + dxs_s * dte_x
            ddte_x.append(dxs_s * xdt)
            dhp.append(dhn * ck.elast_rows(g) + dh_prev)
            dxs.append(dy_g * ck.dsk_x[:, gl] + dxdt * dt_x)
            ddt_x.append(dxdt * xs_g)
            dcbb = dcb.astype(BF16)
            dbs.append(dbg + _dot_tn(dcbb, cgb))
            dcs.append(dcg + _dot(dcbb, bgb))
        dh_scr[...] = jnp.concatenate(dhp, axis=0)
        red = _dot_nt_sel(jnp.concatenate([cat(de_x), cat(ddte_x), cat(ddt_x)], axis=0), ck.e64, 1)
        de_c, ddte_c, ddt_c = red[0:LB], red[LB:2 * LB], red[2 * LB:3 * LB]
        ddsk = _dot_nt_sel(jnp.broadcast_to(cat(ddsk_x), (SUB, NH * HD)), ck.e64, 2)[0:1, :]
        t1 = ddte_c * ck.dte_all
        dalast = jnp.sum(t1, axis=0, keepdims=True) + hsum * ck.elast
        dacs = (_dot_nt_sel(cat(g_cat), ck.e128, 2) + de_c * ck.e_all - t1 + jnp.transpose(dacs_t)
                + jnp.where(riota == LB - 1, dalast, 0.0))
        da = _cumsum_rows(dacs, True)
        ddt = ddt_c + da * a_neg
        da_log = jnp.sum(da * dt, axis=0, keepdims=True) * a_neg
        ddtr = ddt * (1.0 - jnp.exp(-dt))
        dzxd_ref[:, SW + D:ZXD] = ddtr.astype(BF16)
        sm_ref[6:7, 0:LB] += jnp.sum(ddtr, axis=0, keepdims=True)
        sm_ref[6:7, LB:2 * LB] += da_log
        sm_ref[6:7, 2 * LB:3 * LB] += ddsk

        du = cat(dxs + dbs + dcs) * (sg_u * (1.0 + u * (1.0 - sg_u)))
        xv = x_ref[...]
        adv = [du] + _conv_advances(du, dun_scr[...])
        sm_ref[0:1, :] += jnp.sum(du, axis=0, keepdims=True)
        dxbc = cw_ref[CONVK - 1:CONVK, :] * du
        for j in range(CONVK):
            sm_ref[CONVK - j:CONVK + 1 - j, :] += jnp.sum(adv[j] * xv, axis=0, keepdims=True)
            if j:
                dxbc = dxbc + cw_ref[CONVK - 1 - j:CONVK - j, :] * adv[j]
        dun_scr[...] = du[0:SUB, :]
        dzxd_ref[:, SW:SW + D] = dxbc.astype(BF16)

    run(*refs)


def _attn_bwd_block(i, n, q_ref, kp_ref, kc_ref, vp_ref, vc_ref, o_ref, do_ref, cos_ref, sin_ref, pr_ref, ps_ref,
                    dq_ref, dkv_ref, ds_ref, ck_scr, cv_scr):
    @pl.when(i == 0)
    def _():
        ds_ref[...] = jnp.zeros_like(ds_ref)
        ck_scr[...] = jnp.zeros_like(ck_scr)
        cv_scr[...] = jnp.zeros_like(cv_scr)

    qv, ov, dov = q_ref[...], o_ref[...], do_ref[...]
    kcat = jnp.concatenate([kp_ref[...], kc_ref[...]], axis=0)
    vcat = jnp.concatenate([vp_ref[...], vc_ref[...]], axis=0)
    upper = _upper_mask()
    srow = lax.broadcasted_iota(jnp.int32, (8, LB), 0)
    slane = lax.broadcasted_iota(jnp.int32, (8, LB), 1)
    dsink = jnp.zeros((8, LB), F32)
    dq_g, dk_g, dv_g = [], [], []
    for g in range(NQ // QPG):
        sl = slice(g * HD, (g + 1) * HD)
        qg = _stack_heads(qv, g)
        dog = _stack_heads(dov, g)
        rows = slice(g * QPG * LB, (g + 1) * QPG * LB)
        probs = pr_ref[0, rows, :].astype(F32)
        psink = ps_ref[0, rows, :].astype(F32)
        delta = _row_sums_wide(dog.astype(F32) * _stack_heads(ov, g).astype(F32), 2)
        dsc = probs * (_band(upper, _dot_nt(dog, vcat[0:LB, sl]), _dot_nt(dog, vcat[LB:2 * LB, sl])) - delta)
        sink_terms = (psink * delta)[:, 0:1]
        for hh in range(QPG):
            dsink = dsink - jnp.where((srow == QPG * g + hh) & (slane == 0),
                                      jnp.sum(sink_terms[hh * LB:(hh + 1) * LB, :]), 0.0)
        ds_p = jnp.where(upper, dsc, 0.0).astype(BF16)
        ds_c = jnp.where(upper, 0.0, dsc).astype(BF16)
        dq_g.append((_dot(ds_p, kcat[0:LB, sl]) + _dot(ds_c, kcat[LB:2 * LB, sl])) * ATT_SCALE)
        dk_g.append(jnp.concatenate([_dot_tn(ds_p, qg), _dot_tn(ds_c, qg)], axis=0) * ATT_SCALE)
        dv_g.append(jnp.concatenate([_dot_tn(jnp.where(upper, probs, 0.0).astype(BF16), dog),
                                     _dot_tn(jnp.where(upper, 0.0, probs).astype(BF16), dog)], axis=0))
    ds_ref[...] += dsink
    cs = cos_ref[...]
    sn = sin_ref[...]
    dk2 = jnp.concatenate(dk_g, axis=1)
    dv2 = jnp.concatenate(dv_g, axis=1)
    for a, tile in enumerate(_unstack_heads(dq_g)):
        dq_ref[:, a * LB:(a + 1) * LB] = _rope(tile, cs, sn, True).astype(BF16)
    dkv_ref[:, 0:KVW] = _rope(ck_scr[...] + dk2[LB:2 * LB, :], cs, sn, True).astype(BF16)
    dkv_ref[:, KVW:2 * KVW] = (cv_scr[...] + dv2[LB:2 * LB, :]).astype(BF16)
    ck_scr[...] = dk2[0:LB, :]
    cv_scr[...] = dv2[0:LB, :]


def _mixer_bwd(q, k, v, o, do, cos, sin, probs, psink, dyn, y, z, xbc, u, dtv, acs, decay, conv_w8, ssm_p, nw, hs,
               scatters):
    s = q.shape[0]
    nb = s // LB
    bps = _blocks_per_step(nb)
    nsteps = nb // bps
    tl = bps * LB
    cur = lambda i: (nsteps - 1 - i, 0)
    prev = lambda i: (jnp.maximum((nsteps - 1 - i) * bps - 1, 0), 0)
    n_in = 25
    items, ex_shapes, n_g = _exchange_items([], scatters)
    ne = len(items)
    e64, e128 = _head_expanders()
    stack = pl.BlockSpec((bps, NH * LB, LB), lambda i: (nsteps - 1 - i, 0, 0))

    def body(*refs):
        i = pl.program_id(0)
        (q_ref, kp_ref, kc_ref, vp_ref, vc_ref, o_ref, do_ref, cos_ref, sin_ref, pr_ref, ps_ref,
         dyn_ref, y_ref, z_ref, x_ref, u_ref, dt_ref, acs_ref, dm_ref, cw_ref, sp_ref, nw_ref,
         hs_ref, e64_ref, e128_ref) = refs[:n_in]
        ex_in = refs[n_in:n_in + ne]
        dp_ref, ds_ref, sm_ref = refs[n_in + ne:n_in + ne + 3]
        ex_out = refs[n_in + ne + 3:n_in + 2 * ne + 3]
        ck_scr, cv_scr, dh_scr, dun_scr = refs[n_in + 2 * ne + 3:n_in + 2 * ne + 7]
        sems = refs[n_in + 2 * ne + 7:]

        @pl.when(i == 0)
        def _():
            _Exchange(n_g, ex_in, ex_out, sems).start()

        for back in range(bps):
            sub = bps - 1 - back
            step = i * bps + back
            blk = (nsteps - 1 - i) * bps + sub
            r = slice(sub * LB, (sub + 1) * LB)
            before = slice((sub - 1) * LB, sub * LB)
            one = slice(sub, sub + 1)
            _attn_bwd_block(step, blk, q_ref.at[r], kp_ref if sub == 0 else kc_ref.at[before], kc_ref.at[r],
                            vp_ref if sub == 0 else vc_ref.at[before], vc_ref.at[r], o_ref.at[r], do_ref.at[r],
                            cos_ref.at[r], sin_ref.at[r], pr_ref.at[one], ps_ref.at[one],
                            dp_ref.at[r, O_Q:O_K], dp_ref.at[r, O_K:O_Z], ds_ref, ck_scr, cv_scr)
            _ssd_bwd_block(step, blk, dyn_ref.at[r], y_ref.at[r], z_ref.at[r], x_ref.at[r], u_ref.at[r], dt_ref.at[r],
                           acs_ref.at[r], dm_ref.at[one], cw_ref, sp_ref, nw_ref,
                           hs_ref.at[one], e64_ref, e128_ref, dp_ref.at[r, O_Z:INP], sm_ref, dh_scr, dun_scr)

        @pl.when(i == nsteps - 1)
        def _():
            _Exchange(n_g, ex_in, ex_out, sems).finish()

    any_spec = pl.BlockSpec(memory_space=pl.ANY)
    outs = _pcall(
        body, name="mixer_bwd", grid=(nsteps,),
        in_specs=[pl.BlockSpec((tl, AW), cur), pl.BlockSpec((LB, KVW), prev), pl.BlockSpec((tl, KVW), cur),
                  pl.BlockSpec((LB, KVW), prev), pl.BlockSpec((tl, KVW), cur), pl.BlockSpec((tl, AW), cur),
                  pl.BlockSpec((tl, AW), cur), pl.BlockSpec((tl, LB), cur), pl.BlockSpec((tl, LB), cur),
                  stack, stack,
                  pl.BlockSpec((tl, SW), cur), pl.BlockSpec((tl, SW), cur), pl.BlockSpec((tl, SW), cur),
                  pl.BlockSpec((tl, D), cur), pl.BlockSpec((tl, D), cur), pl.BlockSpec((tl, LB), cur),
                  pl.BlockSpec((tl, LB), cur), stack,
                  _const((8, D)), _const((8, LB)), _const((1, SW)),
                  pl.BlockSpec((bps, NH * HD, NST), lambda i: (nsteps - 1 - i, 0, 0)),
                  _const(e64.shape), _const(e128.shape)] + [any_spec] * ne,
        out_specs=[pl.BlockSpec((tl, INP), cur), pl.BlockSpec((8, LB), lambda i: (0, 0)),
                   pl.BlockSpec((8, D), lambda i: (0, 0))] + [any_spec] * ne,
        out_shape=[_sds((s, INP), BF16), _sds((8, LB), F32), _sds((8, D), F32)] + ex_shapes,
        scratch_shapes=[pltpu.VMEM((LB, KVW), F32), pltpu.VMEM((LB, KVW), F32),
                        pltpu.VMEM((NH * HD, NST), F32), pltpu.VMEM((SUB, D), F32)]
        + _exchange_sems(ne),
        compiler_params=_params(),
    )(q, k, k, v, v, o, do, cos, sin, probs, psink, dyn, y, z, xbc, u, dtv, acs, decay, conv_w8, ssm_p, nw, hs,
      e64, e128, *items)
    return outs[0], outs[1], outs[2], outs[3:]


def _inproj_bwd(dproj, x, dx2, mod8, n1w, w_in_t, scatters, smalls):
    s = x.shape[0]
    tt = min(512, s)
    nt = s // tt
    items, ex_shapes, n_g = _exchange_items([], scatters)
    ne = len(items)
    n_in = 10

    def body(*refs):
        dp_ref, x_ref, dx2_ref, mod_ref, nw_ref, w_ref, f_ref, b_ref, s_ref, k_ref = refs[:n_in]
        ex_in = refs[n_in:n_in + ne]
        gx_ref, sm_ref = refs[n_in + ne:n_in + 2 + ne]
        ex_out = refs[n_in + 2 + ne:n_in + 2 + 2 * ne]
        gpack_ref = refs[n_in + 2 + 2 * ne]
        pack_scr = refs[n_in + 3 + 2 * ne]
        sems = refs[n_in + 4 + 2 * ne:n_in + 7 + 2 * ne]
        pack_sems = refs[n_in + 7 + 2 * ne:]
        i = pl.program_id(0)

        @pl.when(i == 0)
        def _():
            sm_ref[...] = jnp.zeros_like(sm_ref)
            _Exchange(n_g, ex_in, ex_out, sems).start()

        w = w_ref[...]
        hr = tt // 2
        dh1 = [_dot(dp_ref[h * hr:(h + 1) * hr, :], w) for h in range(2)]
        sums = jnp.zeros((3, D), F32)
        for h in range(2):
            rows = slice(h * hr, (h + 1) * hr)
            dxn, d_shift, d_scale, d_w = _norm_mod_bwd(x_ref[rows, :], dh1[h], nw_ref[...], mod_ref[1:2, :])
            gx_ref[rows, :] = dx2_ref[rows, :] + dxn
            sums = sums + jnp.concatenate([d_shift, d_scale, d_w], axis=0)
        sm_ref[0:3, :] += sums

        @pl.when(i == nt - 1)
        def _():
            _pack_rows(f_ref, b_ref, s_ref, sm_ref, k_ref, pack_scr)
            small = _Exchange(1, [pack_scr], [gpack_ref], pack_sems)
            small.start()
            _Exchange(n_g, ex_in, ex_out, sems).finish()
            small.finish()

    any_spec = pl.BlockSpec(memory_space=pl.ANY)
    outs = _pcall(
        body, name="inproj_bwd", grid=(nt,),
        in_specs=[_rows(tt, INP), _rows(tt, D), _rows(tt, D), _const((8, D)), _const((1, D)), _const((INP, D)),
                  _const((8, D)), _const((8, D)), _const((8, D)), _const((8, LB))]
        + [any_spec] * ne,
        out_specs=[_rows(tt, D), pl.BlockSpec((8, D), lambda i: (0, 0))] + [any_spec] * (ne + 1),
        out_shape=[_sds((s, D), F32), _sds((8, D), F32)] + ex_shapes + [_sds((N_DEV, PACK_ROWS, D), F32)],
        scratch_shapes=[pltpu.VMEM((PACK_ROWS, D), F32)] + _exchange_sems(ne) + _exchange_sems(1),
        compiler_params=_params(),
    )(dproj, x, dx2, mod8, n1w, w_in_t, *smalls, *items)
    return outs[0], outs[2:2 + ne], outs[2 + ne]


def _wgrad(a, b, name):
    s, m = a.shape
    n = b.shape[1]
    tk = min(2048, s)
    wide = (1408, 1024, 512)
    tm = next((t for t in wide if m % t == 0), m)
    tn = n if n <= 2048 else _largest_divisor(n, wide)
    nk = s // tk

    def body(a_ref, b_ref, o_ref, acc):
        kk = pl.program_id(2)

        @pl.when(kk == 0)
        def _():
            acc[...] = jnp.zeros_like(acc)

        acc[...] += _dot_tn(a_ref[...], b_ref[...])

        @pl.when(kk == nk - 1)
        def _():
            o_ref[...] = acc[...].astype(BF16)

    return _pcall(
        body, name=name, grid=(m // tm, n // tn, nk),
        in_specs=[pl.BlockSpec((tk, tm), lambda i, j, kk: (kk, i)), pl.BlockSpec((tk, tn), lambda i, j, kk: (kk, j))],
        out_specs=pl.BlockSpec((tm, tn), lambda i, j, kk: (i, j)),
        out_shape=_sds((m, n), BF16),
        scratch_shapes=[pltpu.VMEM((tm, tn), F32)],
        compiler_params=_params(3),
    )(a, b)


PACK_ROWS = 24


def _pack_rows(f_ref, b_ref, s_ref, i_ref, k_ref, o_ref):
    o_ref[...] = jnp.zeros_like(o_ref)
    o_ref[0:2, :] = i_ref[0:2, :]
    o_ref[2:3, :] = b_ref[3:4, :]
    o_ref[3:5, :] = b_ref[0:2, :]
    o_ref[5:6, :] = f_ref[0:1, :]
    o_ref[6:7, :] = i_ref[2:3, :]
    o_ref[7:8, :] = b_ref[2:3, :]
    o_ref[8:9, :] = f_ref[1:2, :]
    o_ref[9:14, :] = s_ref[0:5, :]
    o_ref[14:15, :] = s_ref[5:6, :]
    o_ref[15:16, 0:3 * LB] = s_ref[6:7, 0:3 * LB]
    lane = lax.broadcasted_iota(jnp.int32, (1, LB), 1)
    sk = jnp.zeros((1, LB), F32)
    for h in range(NQ):
        sk = sk + jnp.where(lane == h, k_ref[h:h + 1, 0:1], 0.0)
    o_ref[15:16, 3 * LB:4 * LB] = sk
    o_ref[16:17, :] = f_ref[2:3, :]


def _exchange(gathers, scatters, name, two_level=False):
    items, shapes, n_g = _exchange_items(gathers, scatters)
    n = len(items)
    assert not (two_level and scatters)

    def body(*refs):
        ex = _Exchange(n_g, refs[:n], refs[n:2 * n], refs[2 * n:])
        if two_level:
            ex.gather_two_level()
        else:
            ex.start()
            ex.finish()

    any_spec = pl.BlockSpec(memory_space=pl.ANY)
    return _pcall(
        body, name=name, in_specs=[any_spec] * n, out_specs=[any_spec] * n, out_shape=shapes,
        scratch_shapes=_exchange_sems(n),
    )(*items)


def _exchange_items(gathers, scatters):
    items = list(gathers) + list(scatters)
    shapes = [_sds((N_DEV,) + a.shape, a.dtype) for a in gathers] + [_sds(a.shape, a.dtype) for a in scatters]
    return items, shapes, len(gathers)


def _exchange_sems(n):
    return [pltpu.SemaphoreType.DMA((n, N_DEV - 1)), pltpu.SemaphoreType.DMA((n, N_DEV - 1)),
            pltpu.SemaphoreType.DMA((n,))]


class _Exchange:
    def __init__(self, n_g, ins, outs, sems):
        self.n_g, self.ins, self.outs = n_g, ins, outs
        self.send_sems, self.recv_sems, self.loc_sems = sems
        xi, yi, ci = lax.axis_index("x"), lax.axis_index("y"), lax.axis_index("c")
        self.me = 4 * xi + 2 * yi + ci
        self.peers = []
        for r in range(1, N_DEV):
            px = 1 - xi if r & 4 else xi
            py = 1 - yi if r & 2 else yi
            pc = 1 - ci if r & 1 else ci
            self.peers.append(((px, py, pc), 4 * px + 2 * py + pc))

    def _copy(self, t, r, landing):
        dev, peer = self.peers[r]
        src = self.ins[t] if t < self.n_g else self.ins[t].at[peer]
        return pltpu.make_async_remote_copy(
            src_ref=src, dst_ref=self.outs[t].at[landing], send_sem=self.send_sems.at[t, r],
            recv_sem=self.recv_sems.at[t, r], device_id=dev, device_id_type=pl.DeviceIdType.MESH)

    def _local(self, t):
        src = self.ins[t] if t < self.n_g else self.ins[t].at[self.me]
        return pltpu.make_async_copy(src, self.outs[t].at[self.me], self.loc_sems.at[t])

    def start(self):
        for t in range(len(self.ins)):
            self._local(t).start()
            for r in range(N_DEV - 1):
                self._copy(t, r, self.me).start()

    def finish(self):
        n = len(self.ins)
        for t in range(n):
            for r in range(N_DEV - 1):
                self._copy(t, r, self.peers[r][1]).wait_recv()
        for t in range(n):
            for r in range(N_DEV - 1):
                self._copy(t, r, self.me).wait_send()
            self._local(t).wait()

    def gather_two_level(self):
        self.two_level_start()
        self.two_level_relay()
        self.two_level_finish()

    DIRECT = (0, 1, 3, 5)

    def two_level_start(self):
        for t in range(len(self.ins)):
            self._local(t).start()
            for r in self.DIRECT:
                self._copy(t, r, self.me).start()

    def _relay(self, t, r):
        peer = self.peers[r][1]
        return pltpu.make_async_remote_copy(
            src_ref=self.outs[t].at[peer], dst_ref=self.outs[t].at[peer], send_sem=self.send_sems.at[t, r + 1],
            recv_sem=self.recv_sems.at[t, r + 1], device_id=self.peers[0][0], device_id_type=pl.DeviceIdType.MESH)

    def two_level_relay(self):
        for t in range(len(self.ins)):
            for r in self.DIRECT[1:]:
                self._copy(t, r, self.peers[r][1]).wait_recv()
                self._relay(t, r).start()

    def two_level_finish(self):
        n = len(self.ins)
        for t in range(n):
            for r in (0, 2, 4, 6):
                self._copy(t, r, self.peers[r][1]).wait_recv()
        for t in range(n):
            for r in self.DIRECT:
                self._copy(t, r, self.me).wait_send()
            for r in self.DIRECT[1:]:
                self._relay(t, r).wait_send()
            self._local(t).wait()


def _prologue(c8, w_in_tb, cw8, w_cols, b_cols):
    ncol = w_cols.shape[1]

    def body(c_ref, win_ref, cw_ref, w_ref, b_ref, gc_ref, gin_ref, gcw_ref, gmod_ref, call_scr, mod_scr, loc_sem,
             *sems):
        big = _Exchange(2, [win_ref, cw_ref], [gin_ref, gcw_ref], sems[0:3])
        big.two_level_start()
        small = _Exchange(1, [c_ref], [gc_ref], sems[3:6])
        small.start()
        small.finish()
        landed = pltpu.make_async_copy(gc_ref, call_scr, loc_sem)
        landed.start()
        landed.wait()
        cv = call_scr[:, 0, :]
        sc = (cv * _sigmoid(cv)).astype(BF16)
        mod_scr[...] = _dot(sc, w_ref[...].astype(BF16)) + b_ref[...]
        mods = _Exchange(1, [mod_scr], [gmod_ref], sems[6:9])
        mods.start()
        mods.finish()
        big.two_level_relay()
        big.two_level_finish()

    any_spec = pl.BlockSpec(memory_space=pl.ANY)
    vmem_spec = pl.BlockSpec(memory_space=pltpu.VMEM)
    return _pcall(
        body, name="prologue", in_specs=[any_spec, any_spec, any_spec, vmem_spec, vmem_spec],
        out_specs=[any_spec] * 4,
        out_shape=[_sds((N_DEV,) + c8.shape, F32), _sds((N_DEV,) + w_in_tb.shape, BF16),
                   _sds((N_DEV,) + cw8.shape, F32), _sds((N_DEV, N_DEV, ncol), F32)],
        scratch_shapes=[pltpu.VMEM((N_DEV,) + c8.shape, F32), pltpu.VMEM((N_DEV, ncol), F32),
                        pltpu.SemaphoreType.DMA] + _exchange_sems(2) + _exchange_sems(1) + _exchange_sems(1),
        compiler_params=pltpu.CompilerParams(vmem_limit_bytes=VMEM_LIMIT),
    )(c8, w_in_tb, cw8, w_cols, b_cols)


def _adamw(w, g, m, v):
    m2 = ADAM_B1 * m + (1.0 - ADAM_B1) * g
    v2 = ADAM_B2 * v + (1.0 - ADAM_B2) * (g * g)
    m_hat = m2 / (1.0 - ADAM_B1 ** ADAM_STEP)
    v_hat = v2 / (1.0 - ADAM_B2 ** ADAM_STEP)
    delta = -ADAM_LR * (m_hat / (jnp.sqrt(v_hat) + ADAM_EPS) + ADAM_WD * w)
    return delta, m2, v2


def _sum_adamw(parts, w, m, v, name):
    rws, cols = w.shape
    tr = next((t for t in (256, 176, 128) if rws % t == 0), rws)

    def body(p_ref, w_ref, m_ref, v_ref, g_ref, d_ref, mo_ref, vo_ref):
        g = p_ref[0].astype(F32)
        for dev in range(1, N_DEV):
            g = g + p_ref[dev].astype(F32)
        g_ref[...] = g
        d_ref[...], mo_ref[...], vo_ref[...] = _adamw(w_ref[...], g, m_ref[...], v_ref[...])

    blk = pl.BlockSpec((tr, cols), lambda i: (i, 0))
    return _pcall(
        body, name=name, grid=(rws // tr,),
        in_specs=[pl.BlockSpec((N_DEV, tr, cols), lambda i: (0, i, 0)), blk, blk, blk],
        out_specs=[blk] * 4, out_shape=[_sds((rws, cols), F32)] * 4, compiler_params=_params(),
    )(parts, w, m, v)


def _wada_adamw(c_all, dmod_cols, w, m, v):
    rws, cols = w.shape
    tr = 256

    def body(c_ref, dm_ref, w_ref, m_ref, v_ref, g_ref, d_ref, mo_ref, vo_ref):
        cv = c_ref[...]
        sc = (cv * _sigmoid(cv)).astype(BF16)
        g = _dot_tn(sc, dm_ref[...].astype(BF16))
        g_ref[...] = g
        d_ref[...], mo_ref[...], vo_ref[...] = _adamw(w_ref[...], g, m_ref[...], v_ref[...])

    blk = pl.BlockSpec((tr, cols), lambda i: (i, 0))
    return _pcall(
        body, name="wada_adamw", grid=(rws // tr,),
        in_specs=[pl.BlockSpec((N_DEV, tr), lambda i: (0, i)), pl.BlockSpec((N_DEV, cols), lambda i: (0, 0)),
                  blk, blk, blk],
        out_specs=[blk] * 4, out_shape=[_sds((rws, cols), F32)] * 4, compiler_params=_params(),
    )(c_all, dmod_cols, w, m, v)


def _small_reduce(packs):
    def body(p_ref, o_ref):
        tot = p_ref[0]
        for dev in range(1, N_DEV):
            tot = tot + p_ref[dev]
        o_ref[...] = tot
        o_ref[16:17, :] = jnp.zeros((1, D), F32) + (0.5 / D) * jnp.sum(tot[16:17, :])

    return _pcall(body, name="small_reduce", out_shape=_sds((PACK_ROWS, D), F32))(packs)


def _adamw_many(ws, gs, ms, vs):
    k = len(ws)

    def body(*refs):
        for i in range(k):
            w_ref, g_ref, m_ref, v_ref = refs[i], refs[k + i], refs[2 * k + i], refs[3 * k + i]
            d_ref, mo_ref, vo_ref = refs[4 * k + i], refs[5 * k + i], refs[6 * k + i]
            d_ref[...], mo_ref[...], vo_ref[...] = _adamw(w_ref[...], g_ref[...], m_ref[...], v_ref[...])

    shp = [_sds(w.shape, F32) for w in ws]
    outs = _pcall(body, name="adamw_small", out_shape=shp * 3)(*ws, *gs, *ms, *vs)
    return outs[:k], outs[k:2 * k], outs[2 * k:]


def kernel(x, c, positions, w_ada, b_ada, norm1_w, w_in, conv_w, conv_b, dt_bias, a_log, d_skip, attn_sinks, ssm_norm_w, w_out, norm2_w, w_gate_up, w_down, final_norm_w, loss_target, m_w_ada, m_b_ada, m_norm1_w, m_w_in, m_conv_w, m_conv_b, m_dt_bias, m_a_log, m_d_skip, m_attn_sinks, m_ssm_norm_w, m_w_out, m_norm2_w, m_w_gate_up, m_w_down, m_final_norm_w, v_w_ada, v_b_ada, v_norm1_w, v_w_in, v_conv_w, v_conv_b, v_dt_bias, v_a_log, v_d_skip, v_attn_sinks, v_ssm_norm_w, v_w_out, v_norm2_w, v_w_gate_up, v_w_down, v_final_norm_w):
    s = x.shape[1]
    me = 4 * lax.axis_index("x") + 2 * lax.axis_index("y") + lax.axis_index("c")
    ada_cols = N_MOD * D // N_DEV

    c8 = jnp.pad(c, ((0, 7), (0, 0)))
    cw8 = jnp.pad(conv_w[0], ((0, 8 - CONVK), (0, 0)))
    w_in_t, m_w_in_t, v_w_in_t = jnp.transpose(w_in[0]), jnp.transpose(m_w_in[0]), jnp.transpose(v_w_in[0])
    w_gu_t, m_w_gu_t, v_w_gu_t = (jnp.transpose(w_gate_up[0]), jnp.transpose(m_w_gate_up[0]),
                                  jnp.transpose(v_w_gate_up[0]))
    b_cols = lax.dynamic_slice(b_ada, (0, me * ada_cols), (1, ada_cols))
    g_c, g_in, g_cw, g_mod = _prologue(c8, w_in_t.astype(BF16), cw8, w_ada[0], b_cols)
    c_all = g_c[:, 0, :]
    w_in_f = jnp.pad(g_in.reshape(IN_PROJ, D), ((0, INP - IN_PROJ), (0, 0)))
    conv_w8 = jnp.transpose(g_cw, (1, 0, 2)).reshape(8, D)
    mod = lax.dynamic_index_in_dim(g_mod, me, axis=1, keepdims=False).reshape(N_MOD, D)
    mod8 = jnp.pad(mod, ((0, 8 - N_MOD), (0, 0)))

    half = HD // 2
    inv_freq = ROPE_THETA ** (-jnp.arange(half, dtype=F32) / half)
    invf = jnp.tile(inv_freq, LB // half).reshape(1, LB)
    lanes = lambda a: jnp.pad(a, ((0, 0), (0, LB - a.shape[1])))
    ssm_p = jnp.pad(jnp.concatenate([lanes(dt_bias), lanes(a_log), lanes(d_skip)], axis=0), ((0, 5), (0, 0)))
    sinks8 = jnp.broadcast_to(attn_sinks.reshape(NQ, 1), (NQ, LB))

    xs, tgt, fnw = x[0], loss_target[0], final_norm_w.reshape(1, D)

    q, k, v, z, xbc, dtr, h1, cos, sin = _inproj_fwd(xs, positions[0].reshape(s, 1), invf, mod8, norm1_w, w_in_f)
    (attn, yn, y, hs, conv_u, dtv, acs, decay, probs, psink), (g_out, g_gu, g_down) = _mixer_fwd(
        q, k, v, sinks8, xbc, conv_w8, conv_b, dtr, ssm_p, z, ssm_norm_w,
        [w_out[0].astype(BF16), w_gu_t.astype(BF16), w_down[0].astype(BF16)])
    w_out_f = g_out.reshape(D, D)
    w_gu_f = g_gu.reshape(2 * DFF, D)
    w_down_f = g_down.reshape(DFF, D)
    x2, h2, mo, mix, gu, act, dx3, sm_f = _outproj_ffn_fwd_loss(attn, yn, xs, tgt, mod8, norm2_w, fnw, w_out_f, w_gu_f,
                                                                 w_down_f)

    dx2, dff, dgu, dmix, dattn, dyn, sm_b = _ffn_bwd(dx3, gu, x2, mo, mod8, norm2_w, w_gu_f, w_down_f, w_out_f)
    p_gu = _wgrad(dgu, h2, "wgrad_gate_up").reshape(N_DEV, 2 * DFF // N_DEV, D)
    p_down = _wgrad(act, dff, "wgrad_down").reshape(N_DEV, DFF // N_DEV, D)
    p_out = _wgrad(mix, dmix, "wgrad_out").reshape(N_DEV, D // N_DEV, D)
    dproj, dsink, sm_s, (r_gu, r_down, r_out) = _mixer_bwd(
        q, k, v, attn, dattn, cos, sin, probs, psink, dyn, y, z, xbc, conv_u, dtv, acs, decay, conv_w8, ssm_p,
        ssm_norm_w, hs, [p_gu, p_down, p_out])
    p_in = _wgrad(dproj, h1, "wgrad_in")[:IN_PROJ].reshape(N_DEV, IN_PROJ // N_DEV, D)
    gx, (r_in,), g_pack = _inproj_bwd(dproj, xs, dx2, mod8, norm1_w, w_in_f, [p_in], (sm_f, sm_b, sm_s, dsink))

    tot = _small_reduce(g_pack)
    loss = tot[16, 0]
    dmod_all = g_pack[:, 0:N_MOD, :].reshape(N_DEV, N_MOD * D)
    dmod_cols = lax.dynamic_slice(dmod_all, (0, me * ada_cols), (N_DEV, ada_cols))

    big = {
        "w_ada": _wada_adamw(c_all, dmod_cols, w_ada[0], m_w_ada[0], v_w_ada[0]),
        "w_in": [jnp.transpose(t) for t in _sum_adamw(r_in, w_in_t, m_w_in_t, v_w_in_t, "adamw_in")],
        "w_out": _sum_adamw(r_out, w_out[0], m_w_out[0], v_w_out[0], "adamw_out"),
        "w_gate_up": [jnp.transpose(t) for t in _sum_adamw(r_gu, w_gu_t, m_w_gu_t, v_w_gu_t, "adamw_gate_up")],
        "w_down": _sum_adamw(r_down, w_down[0], m_w_down[0], v_w_down[0], "adamw_down"),
    }
    small_names = ["b_ada", "norm1_w", "conv_w", "conv_b", "dt_bias", "a_log", "d_skip", "attn_sinks", "ssm_norm_w",
                   "norm2_w", "final_norm_w"]
    row15 = tot[15:16, :]
    small_g = {
        "b_ada": tot[0:N_MOD, :].reshape(1, N_MOD * D),
        "norm1_w": tot[6:7, :],
        "conv_w": lax.dynamic_slice(tot[10:14, :], (0, me * (D // N_DEV)), (CONVK, D // N_DEV)),
        "conv_b": tot[9:10, :],
        "dt_bias": row15[:, 0:NH],
        "a_log": row15[:, LB:LB + NH],
        "d_skip": row15[:, 2 * LB:2 * LB + NH],
        "attn_sinks": row15[:, 3 * LB:3 * LB + NQ],
        "ssm_norm_w": tot[14:15, 0:SW],
        "norm2_w": tot[7:8, :],
        "final_norm_w": tot[8:9, :],
    }
    small_w = {"b_ada": b_ada, "norm1_w": norm1_w, "conv_w": conv_w[0], "conv_b": conv_b, "dt_bias": dt_bias,
               "a_log": a_log, "d_skip": d_skip, "attn_sinks": attn_sinks, "ssm_norm_w": ssm_norm_w,
               "norm2_w": norm2_w, "final_norm_w": final_norm_w.reshape(1, D)}
    small_m = {"b_ada": m_b_ada, "norm1_w": m_norm1_w, "conv_w": m_conv_w[0], "conv_b": m_conv_b,
               "dt_bias": m_dt_bias, "a_log": m_a_log, "d_skip": m_d_skip, "attn_sinks": m_attn_sinks,
               "ssm_norm_w": m_ssm_norm_w, "norm2_w": m_norm2_w, "final_norm_w": m_final_norm_w.reshape(1, D)}
    small_v = {"b_ada": v_b_ada, "norm1_w": v_norm1_w, "conv_w": v_conv_w[0], "conv_b": v_conv_b,
               "dt_bias": v_dt_bias, "a_log": v_a_log, "d_skip": v_d_skip, "attn_sinks": v_attn_sinks,
               "ssm_norm_w": v_ssm_norm_w, "norm2_w": v_norm2_w, "final_norm_w": v_final_norm_w.reshape(1, D)}
    s_d, s_m, s_v = _adamw_many([small_w[k] for k in small_names], [small_g[k] for k in small_names],
                                [small_m[k] for k in small_names], [small_v[k] for k in small_names])

    order = ["w_ada", "b_ada", "norm1_w", "w_in", "conv_w", "conv_b", "dt_bias", "a_log", "d_skip", "attn_sinks",
             "ssm_norm_w", "w_out", "norm2_w", "w_gate_up", "w_down", "final_norm_w"]
    lead = {"w_ada", "w_in", "conv_w", "w_out", "w_gate_up", "w_down"}
    grads, deltas, new_m, new_v = [], [], [], []
    for name in order:
        if name in big:
            g, d, m2, v2 = big[name]
        else:
            i = small_names.index(name)
            g, d, m2, v2 = small_g[name], s_d[i], s_m[i], s_v[i]
        if name in lead:
            g, d, m2, v2 = g[None], d[None], m2[None], v2[None]
        if name == "final_norm_w":
            g, d, m2, v2 = g.reshape(D), d.reshape(D), m2.reshape(D), v2.reshape(D)
        grads.append(g)
        deltas.append(d)
        new_m.append(m2)
        new_v.append(v2)
    return (loss, gx[None], *grads, *deltas, *new_m, *new_v)
```

```python
import functools
import math

import jax
import jax.numpy as jnp
from jax import lax
from jax.experimental import pallas as pl
from jax.experimental.pallas import tpu as pltpu

F32 = jnp.float32
BF16 = jnp.bfloat16

N_DEV = 8
D = 1024
HD = 64
NQ = 8
AW = 512
KVW = 128
SW = 512
NST = 128
NH = 8
LB = 128
CONVK = 4
DFF = 2816
N_MOD = 6
IN_PROJ = 2312
INP = 2432
O_Q, O_K, O_V, O_Z, O_XBC, O_DT = 0, 512, 640, 768, 1280, 2304
ZXD = INP - O_Z
EPS = 1e-6
NEG = -1e30
ROPE_THETA = 10000.0
VMEM_LIMIT = 56 * 1024 * 1024

ADAM_LR = 0.001
ADAM_B1 = 0.9
ADAM_B2 = 0.999
ADAM_EPS = 1e-08
ADAM_WD = 0.01
ADAM_STEP = 10

NT_DIMS = (((1,), (1,)), ((), ()))
TN_DIMS = (((0,), (0,)), ((), ()))


def _pcall(body, **kw):
    return pl.pallas_call(body, **kw)


def _sds(shape, dtype):
    return jax.ShapeDtypeStruct(shape, dtype)


def _params(n_grid=1):
    return pltpu.CompilerParams(dimension_semantics=("arbitrary",) * n_grid, vmem_limit_bytes=VMEM_LIMIT)


def _const(shape):
    return pl.BlockSpec(shape, lambda *_: (0,) * len(shape), pipeline_mode=pl.Buffered(1))


def _largest_divisor(n, candidates):
    for cand in candidates:
        if n % cand == 0:
            return cand
    raise ValueError(f"no tile in {candidates} divides {n}")


def _rows(t, w):
    return pl.BlockSpec((t, w), lambda i: (i, 0))


def _dot(a, b):
    return jnp.dot(a, b, preferred_element_type=F32)


def _dot_nt(a, b):
    return lax.dot_general(a, b, NT_DIMS, preferred_element_type=F32)


def _dot_tn(a, b):
    return lax.dot_general(a, b, TN_DIMS, preferred_element_type=F32)


def _sigmoid(v):
    return 1.0 / (1.0 + jnp.exp(-v))


def _softplus(v):
    return jnp.maximum(v, 0.0) + jnp.log1p(jnp.exp(-jnp.abs(v)))


def _rope_sign_mask(shape):
    lane = lax.broadcasted_iota(jnp.int32, shape, 1)
    return (lane % HD) < (HD // 2)


def _rope(t, cs, sn, inverse):
    r_dn = pltpu.roll(t, HD // 2, 1)
    r_up = pltpu.roll(t, LB - HD // 2, 1)
    first = _rope_sign_mask(t.shape)
    if inverse:
        rot = jnp.where(first, r_up, -r_dn)
    else:
        rot = jnp.where(first, -r_up, r_dn)
    return t * cs + rot * sn


def _norm_mod_fwd(xv, nw, shift, scale):
    r = lax.rsqrt(jnp.mean(xv * xv, axis=-1, keepdims=True) + EPS)
    xh = xv * r
    return (xh * nw) * (1.0 + scale) + shift


def _norm_mod_bwd(xv, dh, nw, scale):
    r = lax.rsqrt(jnp.mean(xv * xv, axis=-1, keepdims=True) + EPS)
    xh = xv * r
    xn = xh * nw
    d_shift = jnp.sum(dh, axis=0, keepdims=True)
    d_scale = jnp.sum(dh * xn, axis=0, keepdims=True)
    dxn = dh * (1.0 + scale)
    d_w = jnp.sum(dxn * xh, axis=0, keepdims=True)
    dxh = dxn * nw
    dx = r * (dxh - xh * jnp.mean(dxh * xh, axis=-1, keepdims=True))
    return dx, d_shift, d_scale, d_w


def _inproj_fwd(x, pos, invf, mod8, n1w, w_in):
    s = x.shape[0]
    tt = min(512, s)

    def body(x_ref, pos_ref, invf_ref, mod_ref, nw_ref, w_ref,
             q_ref, k_ref, v_ref, z_ref, xbc_ref, dtr_ref, h1_ref, cos_ref, sin_ref):
        h = _norm_mod_fwd(x_ref[...], nw_ref[...], mod_ref[0:1, :], mod_ref[1:2, :])
        hb = h.astype(BF16)
        h1_ref[...] = hb
        proj = _dot_nt(hb, w_ref[...])
        ang = pos_ref[...].astype(F32) * invf_ref[...]
        cs = jnp.cos(ang)
        sn = jnp.sin(ang)
        cos_ref[...] = cs
        sin_ref[...] = sn
        for a in range(AW // LB):
            q_ref[:, a * LB:(a + 1) * LB] = _rope(proj[:, O_Q + a * LB:O_Q + (a + 1) * LB], cs, sn, False).astype(BF16)
        k_ref[...] = _rope(proj[:, O_K:O_V], cs, sn, False).astype(BF16)
        v_ref[...] = proj[:, O_V:O_Z].astype(BF16)
        z_ref[...] = proj[:, O_Z:O_XBC]
        xbc_ref[...] = proj[:, O_XBC:O_DT]
        dtr_ref[...] = proj[:, O_DT:INP]

    return _pcall(
        body, name="inproj_fwd", grid=(s // tt,),
        in_specs=[_rows(tt, D), _rows(tt, 1), _const((1, LB)), _const((8, D)), _const((1, D)), _const((INP, D))],
        out_specs=[_rows(tt, AW), _rows(tt, KVW), _rows(tt, KVW), _rows(tt, SW), _rows(tt, D), _rows(tt, LB),
                   _rows(tt, D), _rows(tt, LB), _rows(tt, LB)],
        out_shape=[_sds((s, AW), BF16), _sds((s, KVW), BF16), _sds((s, KVW), BF16), _sds((s, SW), F32),
                   _sds((s, D), F32), _sds((s, LB), F32), _sds((s, D), BF16), _sds((s, LB), F32), _sds((s, LB), F32)],
        compiler_params=_params(),
    )(x, pos, invf, mod8, n1w, w_in)


QPG = 4
ATT_SCALE = 1.0 / math.sqrt(HD)


def _stack_heads(val, g):
    return jnp.concatenate([val[:, (QPG * g + hh) * HD:(QPG * g + hh + 1) * HD] for hh in range(QPG)], axis=0)


def _unstack_heads(groups):
    pieces = [grp[hh * LB:(hh + 1) * LB, :] for grp in groups for hh in range(QPG)]
    return [jnp.concatenate(pieces[2 * a:2 * a + 2], axis=1) for a in range(NQ // 2)]


def _upper_mask():
    row = lax.broadcasted_iota(jnp.int32, (QPG * LB, LB), 0)
    col = lax.broadcasted_iota(jnp.int32, (QPG * LB, LB), 1)
    return col > (row % LB)


def _sink_wide(sinks, g):
    return jnp.concatenate([jnp.broadcast_to(sinks[QPG * g + hh:QPG * g + hh + 1, 0:1], (LB, LB))
                            for hh in range(QPG)], axis=0)


def _row_sums_wide(v, terms):
    return _dot_sel(v, jnp.ones((v.shape[1], LB), BF16), terms)


def _band(upper, prev_part, cur_part):
    return jnp.where(upper, prev_part, cur_part)


def _attn_scores(n, qg, kcat, upper):
    sp = _dot_nt(qg, kcat[0:LB, :]) * ATT_SCALE
    sc = _dot_nt(qg, kcat[LB:2 * LB, :]) * ATT_SCALE
    return _band(upper, jnp.where(n > 0, sp, NEG), sc)


def _attn_softmax(comb, sink):
    m = jnp.maximum(jnp.max(comb, axis=-1, keepdims=True), sink)
    p = jnp.exp(comb - m)
    es = jnp.exp(sink - m)
    return p, es, _row_sums_wide(p, 1) + es


def _attn_fwd_block(n, q_ref, kp_ref, kc_ref, vp_ref, vc_ref, sink_ref, o_ref, pr_ref, ps_ref):
    qv = q_ref[...]
    kcat = jnp.concatenate([kp_ref[...], kc_ref[...]], axis=0)
    vcat = jnp.concatenate([vp_ref[...], vc_ref[...]], axis=0)
    sinks = sink_ref[...]
    upper = _upper_mask()
    outs = []
    for g in range(NQ // QPG):
        sl = slice(g * HD, (g + 1) * HD)
        rows = slice(g * QPG * LB, (g + 1) * QPG * LB)
        p, es, denom = _attn_softmax(_attn_scores(n, _stack_heads(qv, g), kcat[:, sl], upper), _sink_wide(sinks, g))
        rden = 1.0 / denom
        pr_ref[0, rows, :] = (p * rden).astype(BF16)
        ps_ref[0, rows, :] = (es * rden).astype(BF16)
        outs.append((_dot(jnp.where(upper, p, 0.0).astype(BF16), vcat[0:LB, sl])
                     + _dot(jnp.where(upper, 0.0, p).astype(BF16), vcat[LB:2 * LB, sl])) * rden[:, 0:HD])
    for g, grp in enumerate(outs):
        for hh in range(QPG):
            h = QPG * g + hh
            o_ref[:, h * HD:(h + 1) * HD] = grp[hh * LB:(hh + 1) * LB, :].astype(BF16)


def _cumsum_rows(a, reverse):
    row = lax.broadcasted_iota(jnp.int32, a.shape, 0)
    step = 1
    while step < LB:
        if reverse:
            a = a + jnp.where(row < LB - step, pltpu.roll(a, LB - step, 0), 0.0)
        else:
            a = a + jnp.where(row >= step, pltpu.roll(a, step, 0), 0.0)
        step *= 2
    return a


SUB = 8


def _conv_shifts(tail, cur):
    row = lax.broadcasted_iota(jnp.int32, tail.shape, 0)
    out = [cur]
    for j in range(1, CONVK):
        rolled = pltpu.roll(cur, j, 0)
        top = jnp.where(row < j, pltpu.roll(tail, j, 0), rolled[0:SUB, :])
        out.append(jnp.concatenate([top, rolled[SUB:, :]], axis=0))
    return out


def _conv_advances(du, head):
    row = lax.broadcasted_iota(jnp.int32, head.shape, 0)
    out = []
    for j in range(1, CONVK):
        rolled = pltpu.roll(du, LB - j, 0)
        bottom = jnp.where(row >= SUB - j, pltpu.roll(head, SUB - j, 0), rolled[LB - SUB:, :])
        out.append(jnp.concatenate([rolled[:LB - SUB, :], bottom], axis=0))
    return out


def _split(v, terms):
    out = []
    for _ in range(terms - 1):
        t = v.astype(BF16)
        out.append(t)
        v = v - t.astype(F32)
    out.append(v.astype(BF16))
    return out


def _dot_sel(v, sel, terms):
    parts = [_dot(t, sel) for t in _split(v, terms)]
    return functools.reduce(lambda a, b: a + b, parts)


def _dot_nt_sel(v, sel, terms):
    parts = [_dot_nt(t, sel) for t in _split(v, terms)]
    return functools.reduce(lambda a, b: a + b, parts)


def _ssd_pre(xt_ref, xc_ref, cw_ref, cb_ref, dtr_ref, sp_ref, n):
    cur = xc_ref[...]
    tail = jnp.where(n > 0, xt_ref[...], 0.0)
    sh = _conv_shifts(tail, cur)
    u = cb_ref[...] + cw_ref[CONVK - 1:CONVK, :] * sh[0]
    for j in range(1, CONVK):
        u = u + cw_ref[CONVK - 1 - j:CONVK - j, :] * sh[j]
    dt = _softplus(dtr_ref[...] + sp_ref[0:1, :])
    acs = _cumsum_rows(dt * -jnp.exp(sp_ref[1:2, :]), False)
    return u, dt, acs


def _gated_norm_fwd(y, z, sgz, nw):
    yz = y * (z * sgz)
    parts = []
    for g in range(2):
        t = yz[:, g * 256:(g + 1) * 256]
        parts.append(t * lax.rsqrt(jnp.mean(t * t, axis=-1, keepdims=True) + EPS))
    return jnp.concatenate(parts, axis=1) * nw


HPG = 4
GW = HPG * HD


class _SsdChunk:
    def __init__(self, xc, dt, acs, spv, e64, e128, decay=None):
        self.e64, self.e128 = e64, e128
        alast = acs[LB - 1:LB, :]
        self.e_all = jnp.exp(acs)
        self.dte_all = jnp.exp(alast - acs)
        self.elast = jnp.exp(alast)
        wide = _dot_sel(jnp.concatenate([dt, self.e_all, self.dte_all], axis=0), e64, 2)
        self.dt_x, self.e_x, self.dte_x = wide[0:LB], wide[LB:2 * LB], wide[2 * LB:3 * LB]
        self.dsk_x = _dot_sel(spv, e64, 3)[2:3, :]
        if decay is None:
            acs_t = jnp.transpose(acs)
            ac_x = _dot_sel(acs, e128, 3)
            row = lax.broadcasted_iota(jnp.int32, (HPG * LB, LB), 0)
            col = lax.broadcasted_iota(jnp.int32, (HPG * LB, LB), 1)
            causal = (row % LB) >= col
        lane = lax.broadcasted_iota(jnp.int32, (LB, GW), 1)
        self.head_lanes = [(lane >= hh * HD) & (lane < (hh + 1) * HD) for hh in range(HPG)]
        self.xs, self.xdt, self.b, self.c, self.bb, self.cb16, self.cbm, self.dm_st, self.m_st = ([] for _ in range(9))
        for g in range(2):
            heads = range(HPG * g, HPG * (g + 1))
            if decay is None:
                ac_st = jnp.concatenate([ac_x[:, j * LB:(j + 1) * LB] for j in heads], axis=0)
                ar_st = jnp.concatenate([jnp.broadcast_to(acs_t[j:j + 1, :], (LB, LB)) for j in heads], axis=0)
                dm_st = jnp.exp(jnp.where(causal, ac_st - ar_st, NEG))
            else:
                dm_st = decay[g]
            bg = xc[:, SW + g * NST:SW + (g + 1) * NST]
            cg = xc[:, SW + 2 * NST + g * NST:SW + 2 * NST + (g + 1) * NST]
            bgb, cgb = bg.astype(BF16), cg.astype(BF16)
            cbm = _dot_nt(cgb, bgb)
            xs_g = xc[:, g * GW:(g + 1) * GW]
            self.xs.append(xs_g)
            self.xdt.append(xs_g * self.dt_x[:, g * GW:(g + 1) * GW])
            self.b.append(bg)
            self.c.append(cg)
            self.bb.append(bgb)
            self.cb16.append(cgb)
            self.cbm.append(cbm)
            self.dm_st.append(dm_st)
            self.m_st.append(jnp.concatenate([cbm] * HPG, axis=0) * dm_st)

    def elast_rows(self, g):
        return jnp.concatenate([jnp.broadcast_to(self.elast[:, j:j + 1], (HD, NST))
                                for j in range(HPG * g, HPG * (g + 1))], axis=0)

    def diag_blocks(self, stacked):
        out = stacked[(HPG - 1) * LB:HPG * LB, :]
        for hh in range(HPG - 2, -1, -1):
            out = jnp.where(self.head_lanes[hh], stacked[hh * LB:(hh + 1) * LB, :], out)
        return out

    def block_diag(self, v):
        return jnp.concatenate([jnp.where(self.head_lanes[hh], v, 0.0) for hh in range(HPG)], axis=0)


def _ssd_fwd_block(n, xt_ref, xc_ref, cw_ref, cb_ref, dtr_ref, sp_ref, z_ref, nw_ref, e64_ref, e128_ref,
                   yn_ref, y_ref, hs_ref, u_ref, dt_ref, acs_ref, dm_ref, h_scr):
    @pl.when(n == 0)
    def _():
        h_scr[...] = jnp.zeros_like(h_scr)

    h_all = h_scr[...]
    hs_ref[0] = h_all
    u, dt, acs = _ssd_pre(xt_ref, xc_ref, cw_ref, cb_ref, dtr_ref, sp_ref, n)
    u_ref[...] = u
    dt_ref[...] = dt
    acs_ref[...] = acs
    xc = u * _sigmoid(u)
    ck = _SsdChunk(xc, dt, acs, sp_ref[...], e64_ref[...], e128_ref[...])
    dm_ref[0] = jnp.concatenate(ck.dm_st, axis=0)
    ys, hn = [], []
    for g in range(2):
        gl = slice(g * GW, (g + 1) * GW)
        xdt = ck.xdt[g]
        hg = h_all[gl, :]
        y_diag = ck.diag_blocks(_dot(ck.m_st[g].astype(BF16), xdt.astype(BF16)))
        y_off = ck.e_x[:, gl] * _dot_nt(ck.cb16[g], hg.astype(BF16))
        ys.append(y_diag + y_off + ck.xs[g] * ck.dsk_x[:, gl])
        hn.append(hg * ck.elast_rows(g) + _dot_tn((xdt * ck.dte_x[:, gl]).astype(BF16), ck.bb[g]))
    h_scr[...] = jnp.concatenate(hn, axis=0)
    y = jnp.concatenate(ys, axis=1)
    y_ref[...] = y
    z = z_ref[...]
    yn_ref[...] = _gated_norm_fwd(y, z, _sigmoid(z), nw_ref[...]).astype(BF16)


def _mixer_fwd(q, k, v, sinks8, xbc, conv_w8, conv_b, dtr, ssm_p, z, nw, gathers):
    s = q.shape[0]
    nb = s // LB
    bps = _blocks_per_step(nb)
    nsteps = nb // bps
    tl = bps * LB
    cur = lambda n: (n, 0)
    prev = lambda n: (jnp.maximum(n * bps - 1, 0), 0)
    items, ex_shapes, n_g = _exchange_items(gathers, [])
    ne = len(items)

    n_in, n_out = 16, 10
    relay_step = (3 * (nsteps - 1)) // 4
    stack = pl.BlockSpec((bps, NH * LB, LB), lambda n: (n, 0, 0))
    e64, e128 = _head_expanders()

    def body(*refs):
        (q_ref, kp_ref, kc_ref, vp_ref, vc_ref, sink_ref, xt_ref, xc_ref, cw_ref, cb_ref, dtr_ref, sp_ref, z_ref,
         nw_ref, e64_ref, e128_ref) = refs[:n_in]
        ex_in = refs[n_in:n_in + ne]
        (o_ref, yn_ref, y_ref, hs_ref, u_ref, dt_ref, acs_ref, dm_ref, pr_ref,
         ps_ref) = refs[n_in + ne:n_in + n_out + ne]
        ex_out = refs[n_in + n_out + ne:n_in + n_out + 2 * ne]
        h_scr = refs[n_in + n_out + 2 * ne]
        sems = refs[n_in + n_out + 1 + 2 * ne:]
        n = pl.program_id(0)

        @pl.when(n == 0)
        def _():
            _Exchange(n_g, ex_in, ex_out, sems).two_level_start()

        for sub in range(bps):
            blk = n * bps + sub
            r = slice(sub * LB, (sub + 1) * LB)
            before = slice((sub - 1) * LB, sub * LB)
            one = slice(sub, sub + 1)
            _attn_fwd_block(blk, q_ref.at[r], kp_ref if sub == 0 else kc_ref.at[before], kc_ref.at[r],
                            vp_ref if sub == 0 else vc_ref.at[before], vc_ref.at[r], sink_ref,
                            o_ref.at[r], pr_ref.at[one], ps_ref.at[one])
            _ssd_fwd_block(blk, xt_ref if sub == 0 else xc_ref.at[sub * LB - SUB:sub * LB], xc_ref.at[r], cw_ref,
                           cb_ref, dtr_ref.at[r], sp_ref, z_ref.at[r], nw_ref, e64_ref, e128_ref,
                           yn_ref.at[r], y_ref.at[r], hs_ref.at[one], u_ref.at[r], dt_ref.at[r], acs_ref.at[r],
                           dm_ref.at[one], h_scr)

        @pl.when(n == relay_step)
        def _():
            _Exchange(n_g, ex_in, ex_out, sems).two_level_relay()

        @pl.when(n == nsteps - 1)
        def _():
            _Exchange(n_g, ex_in, ex_out, sems).two_level_finish()

    any_spec = pl.BlockSpec(memory_space=pl.ANY)
    tail = pl.BlockSpec((SUB, D), lambda n: (jnp.maximum(n * (tl // SUB) - 1, 0), 0))
    outs = _pcall(
        body, name="mixer_fwd", grid=(nsteps,),
        in_specs=[pl.BlockSpec((tl, AW), cur), pl.BlockSpec((LB, KVW), prev), pl.BlockSpec((tl, KVW), cur),
                  pl.BlockSpec((LB, KVW), prev), pl.BlockSpec((tl, KVW), cur), _const((8, LB)),
                  tail, pl.BlockSpec((tl, D), cur), _const((8, D)), _const((1, D)),
                  pl.BlockSpec((tl, LB), cur), _const((8, LB)), pl.BlockSpec((tl, SW), cur), _const((1, SW)),
                  _const(e64.shape), _const(e128.shape)]
        + [any_spec] * ne,
        out_specs=[pl.BlockSpec((tl, AW), cur), pl.BlockSpec((tl, SW), cur), pl.BlockSpec((tl, SW), cur),
                   pl.BlockSpec((bps, NH * HD, NST), lambda n: (n, 0, 0)), pl.BlockSpec((tl, D), cur),
                   pl.BlockSpec((tl, LB), cur), pl.BlockSpec((tl, LB), cur),
                   stack, stack, stack] + [any_spec] * ne,
        out_shape=[_sds((s, AW), BF16), _sds((s, SW), BF16), _sds((s, SW), F32), _sds((nb, NH * HD, NST), F32),
                   _sds((s, D), F32), _sds((s, LB), F32), _sds((s, LB), F32), _sds((nb, NH * LB, LB), F32),
                   _sds((nb, NH * LB, LB), BF16), _sds((nb, NH * LB, LB), BF16)]
        + ex_shapes,
        scratch_shapes=[pltpu.VMEM((NH * HD, NST), F32)] + _exchange_sems(ne),
        compiler_params=_params(),
    )(q, k, k, v, v, sinks8, xbc, xbc, conv_w8, conv_b, dtr, ssm_p, z, nw, e64, e128, *items)
    return outs[:n_out], outs[n_out:]


def _blocks_per_step(nb):
    return 2 if nb % 2 == 0 else 1


def _head_expanders():
    j = lax.broadcasted_iota(jnp.int32, (LB, NH * HD), 0)
    e64 = (lax.broadcasted_iota(jnp.int32, (LB, NH * HD), 1) // HD == j).astype(BF16)
    j = lax.broadcasted_iota(jnp.int32, (LB, NH * LB), 0)
    e128 = (lax.broadcasted_iota(jnp.int32, (LB, NH * LB), 1) // LB == j).astype(BF16)
    return e64, e128


def _outproj_ffn_fwd_loss(attn, yn, x, tgt, mod8, n2w, fnw, w_out, w_gu_t, w_down):
    s = x.shape[0]
    tf = min(256, s)

    def body(a_ref, y_ref, x_ref, t_ref, mod_ref, nw_ref, fw_ref, wo_ref, wgu_ref, wd_ref,
             x2_ref, h2_ref, mo_ref, mix_ref, gu_ref, act_ref, dx3_ref, sm_ref):
        i = pl.program_id(0)

        @pl.when(i == 0)
        def _():
            sm_ref[...] = jnp.zeros_like(sm_ref)

        mix = jnp.concatenate([a_ref[...], y_ref[...]], axis=1)
        mix_ref[...] = mix
        mo = _dot(mix, wo_ref[...])
        mo_ref[...] = mo.astype(BF16)
        x2 = x_ref[...] + mod_ref[2:3, :] * mo
        x2_ref[...] = x2
        h2 = _norm_mod_fwd(x2, nw_ref[...], mod_ref[3:4, :], mod_ref[4:5, :]).astype(BF16)
        h2_ref[...] = h2
        gu = _dot_nt(h2, wgu_ref[...])
        gu_ref[...] = gu.astype(BF16)
        g = gu[:, :DFF]
        act = (g * _sigmoid(g) * gu[:, DFF:]).astype(BF16)
        act_ref[...] = act
        ff = _dot(act, wd_ref[...])
        x3 = x2 + mod_ref[5:6, :] * ff
        r = lax.rsqrt(jnp.mean(x3 * x3, axis=-1, keepdims=True) + EPS)
        xh = x3 * r
        fw = fw_ref[...]
        err = xh * fw - t_ref[...]
        dy = err * (1.0 / D)
        dxh = dy * fw
        dx3 = r * (dxh - xh * jnp.mean(dxh * xh, axis=-1, keepdims=True))
        dx3_ref[...] = dx3
        sm_ref[0:1, :] += jnp.sum(dx3 * ff, axis=0, keepdims=True)
        sm_ref[1:2, :] += jnp.sum(dy * xh, axis=0, keepdims=True)
        sm_ref[2:3, :] += jnp.sum(err * err, axis=0, keepdims=True)

    return _pcall(
        body, name="outproj_ffn_fwd_loss", grid=(s // tf,),
        in_specs=[_rows(tf, AW), _rows(tf, SW), _rows(tf, D), _rows(tf, D), _const((8, D)), _const((1, D)),
                  _const((1, D)), _const((D, D)), _const((2 * DFF, D)), _const((DFF, D))],
        out_specs=[_rows(tf, D), _rows(tf, D), _rows(tf, D), _rows(tf, D), _rows(tf, 2 * DFF), _rows(tf, DFF),
                   _rows(tf, D), pl.BlockSpec((8, D), lambda i: (0, 0))],
        out_shape=[_sds((s, D), F32), _sds((s, D), BF16), _sds((s, D), BF16), _sds((s, D), BF16),
                   _sds((s, 2 * DFF), BF16), _sds((s, DFF), BF16), _sds((s, D), F32), _sds((8, D), F32)],
        compiler_params=_params(),
    )(attn, yn, x, tgt, mod8, n2w, fnw, w_out, w_gu_t, w_down)


def _ffn_bwd(dx3, gu, x2, mixout, mod8, n2w, w_gu, w_down, w_out):
    s = x2.shape[0]
    tb = min(256, s)

    def body(dx3_ref, gu_ref, x2_ref, mo_ref, mod_ref, nw_ref, wgu_ref, wd_ref, wo_ref,
             dx2_ref, dff_ref, dgu_ref, dmix_ref, dattn_ref, dyn_ref, sm_ref):
        i = pl.program_id(0)

        @pl.when(i == 0)
        def _():
            sm_ref[...] = jnp.zeros_like(sm_ref)

        dx3 = dx3_ref[...]
        dff = (dx3 * mod_ref[5:6, :]).astype(BF16)
        dff_ref[...] = dff
        dact = _dot_nt(dff, wd_ref[...])
        g = gu_ref[:, :DFF].astype(F32)
        u = gu_ref[:, DFF:].astype(F32)
        sg = _sigmoid(g)
        dgu = jnp.concatenate([dact * u * sg * (1.0 + g * (1.0 - sg)), dact * g * sg], axis=1).astype(BF16)
        dgu_ref[...] = dgu
        dh2 = _dot(dgu, wgu_ref[...])
        dxn, d_shift, d_scale, d_w = _norm_mod_bwd(x2_ref[...], dh2, nw_ref[...], mod_ref[4:5, :])
        dx2 = dx3 + dxn
        dx2_ref[...] = dx2
        sm_ref[0:1, :] += d_shift
        sm_ref[1:2, :] += d_scale
        sm_ref[2:3, :] += d_w
        sm_ref[3:4, :] += jnp.sum(dx2 * mo_ref[...].astype(F32), axis=0, keepdims=True)
        dmix = (dx2 * mod_ref[2:3, :]).astype(BF16)
        dmix_ref[...] = dmix
        dmi = _dot_nt(dmix, wo_ref[...])
        dattn_ref[...] = dmi[:, :AW].astype(BF16)
        dyn_ref[...] = dmi[:, AW:]

    return _pcall(
        body, name="ffn_bwd", grid=(s // tb,),
        in_specs=[_rows(tb, D), _rows(tb, 2 * DFF), _rows(tb, D), _rows(tb, D), _const((8, D)), _const((1, D)),
                  _const((2 * DFF, D)), _const((DFF, D)), _const((D, D))],
        out_specs=[_rows(tb, D), _rows(tb, D), _rows(tb, 2 * DFF), _rows(tb, D), _rows(tb, AW), _rows(tb, SW),
                   pl.BlockSpec((8, D), lambda i: (0, 0))],
        out_shape=[_sds((s, D), F32), _sds((s, D), BF16), _sds((s, 2 * DFF), BF16), _sds((s, D), BF16),
                   _sds((s, AW), BF16), _sds((s, SW), F32), _sds((8, D), F32)],
        compiler_params=_params(),
    )(dx3, gu, x2, mixout, mod8, n2w, w_gu, w_down, w_out)


def _ssd_bwd_block(i, n, *refs):
    def run(dyn_ref, y_ref, z_ref, x_ref, u_ref, dt_ref, acs_ref, dm_ref, cw_ref, sp_ref, nw_ref,
            hs_ref, e64_ref, e128_ref, dzxd_ref, sm_ref, dh_scr, dun_scr):
        @pl.when(i == 0)
        def _():
            dh_scr[...] = jnp.zeros_like(dh_scr)
            dun_scr[...] = jnp.zeros_like(dun_scr)
            sm_ref[...] = jnp.zeros_like(sm_ref)

        u, dt, acs = u_ref[...], dt_ref[...], acs_ref[...]
        sg_u = _sigmoid(u)
        xc = u * sg_u
        a_neg = -jnp.exp(sp_ref[1:2, :])
        ck = _SsdChunk(xc, dt, acs, sp_ref[...], e64_ref[...], e128_ref[...],
                       decay=[dm_ref[0, g * HPG * LB:(g + 1) * HPG * LB, :] for g in range(2)])
        h_all = hs_ref[0]
        dh_all = dh_scr[...]
        riota = lax.broadcasted_iota(jnp.int32, (LB, LB), 0)
        lane1 = lax.broadcasted_iota(jnp.int32, (1, LB), 1)

        z = z_ref[...]
        y = y_ref[...]
        sgz = _sigmoid(z)
        sz = z * sgz
        yz = y * sz
        nwv = nw_ref[...]
        dyn_v = dyn_ref[...]
        dyhat = dyn_v * nwv
        yhat_parts, dyz_parts = [], []
        for g in range(2):
            gs = slice(g * 256, (g + 1) * 256)
            t = yz[:, gs]
            rg = lax.rsqrt(jnp.mean(t * t, axis=-1, keepdims=True) + EPS)
            yh = t * rg
            dyh = dyhat[:, gs]
            yhat_parts.append(yh)
            dyz_parts.append(rg * (dyh - yh * jnp.mean(dyh * yh, axis=-1, keepdims=True)))
        yhat = jnp.concatenate(yhat_parts, axis=1)
        dyz = jnp.concatenate(dyz_parts, axis=1)
        sm_ref[5:6, 0:SW] += jnp.sum(dyn_v * yhat, axis=0, keepdims=True)
        dy = dyz * sz
        dzxd_ref[:, 0:SW] = (dyz * y * sgz * (1.0 + z * (1.0 - sgz))).astype(BF16)

        cat = lambda parts: jnp.concatenate(parts, axis=1)
        dxs, dbs, dcs, dhp, g_cat, de_x, ddte_x, ddt_x, ddsk_x = ([] for _ in range(9))
        dacs_t = jnp.zeros((LB, LB), F32)
        hsum = jnp.zeros((1, LB), F32)
        for g in range(2):
            gl = slice(g * GW, (g + 1) * GW)
            xs_g, xdt, bgb, cgb = ck.xs[g], ck.xdt[g], ck.bb[g], ck.cb16[g]
            m_st, dm_st = ck.m_st[g], ck.dm_st[g]
            dt_x, e_x, dte_x = ck.dt_x[:, gl], ck.e_x[:, gl], ck.dte_x[:, gl]
            xdtb = xdt.astype(BF16)
            hg, dhn = h_all[gl, :], dh_all[gl, :]
            hb, dhnb = hg.astype(BF16), dhn.astype(BF16)
            dy_g = dy[:, gl]
            ddsk_x.append(jnp.sum(dy_g * xs_g, axis=0, keepdims=True))
            dy_bd = ck.block_diag(dy_g).astype(BF16)
            dm4 = _dot_nt(dy_bd, xdtb)
            dxdt = _dot_tn(m_st.astype(BF16), dy_bd)
            gmat = dm4 * m_st
            dcbm = dm4 * dm_st
            dcb = dcbm[0:LB] + dcbm[LB:2 * LB] + dcbm[2 * LB:3 * LB] + dcbm[3 * LB:4 * LB]
            g_cat.append(cat([gmat[hh * LB:(hh + 1) * LB, :] for hh in range(HPG)]))
            for hh in range(HPG):
                j = HPG * g + hh
                col_sum = jnp.sum(gmat[hh * LB:(hh + 1) * LB, :], axis=0, keepdims=True)
                dacs_t = dacs_t - jnp.where(riota == j, col_sum, 0.0)
                hsl = slice(hh * HD, (hh + 1) * HD)
                hsum = hsum + jnp.where(lane1 == j, jnp.sum(dhn[hsl, :] * hg[hsl, :]), 0.0)
            dchb = (dy_g * e_x).astype(BF16)
            dcg = _dot(dchb, hb)
            dh_prev = _dot_tn(dchb, cgb)
            de_x.append(dy_g * _dot_nt(cgb, hb))
            dxs_s = _dot_nt(bgb, dhnb)
            dbg = _dot((xdt * dte_x).astype(BF16), dhnb)
            dxdt = dxdt + dxs_s * dte_x
            ddte_x.append(dxs_s * xdt)
            dhp.append(dhn * ck.elast_rows(g) + dh_prev)
            dxs.append(dy_g * ck.dsk_x[:, gl] + dxdt * dt_x)
            ddt_x.append(dxdt * xs_g)
            dcbb = dcb.astype(BF16)
            dbs.append(dbg + _dot_tn(dcbb, cgb))
            dcs.append(dcg + _dot(dcbb, bgb))
        dh_scr[...] = jnp.concatenate(dhp, axis=0)
        red = _dot_nt_sel(jnp.concatenate([cat(de_x), cat(ddte_x), cat(ddt_x)], axis=0), ck.e64, 1)
        de_c, ddte_c, ddt_c = red[0:LB], red[LB:2 * LB], red[2 * LB:3 * LB]
        ddsk = _dot_nt_sel(jnp.broadcast_to(cat(ddsk_x), (SUB, NH * HD)), ck.e64, 2)[0:1, :]
        t1 = ddte_c * ck.dte_all
        dalast = jnp.sum(t1, axis=0, keepdims=True) + hsum * ck.elast
        dacs = (_dot_nt_sel(cat(g_cat), ck.e128, 2) + de_c * ck.e_all - t1 + jnp.transpose(dacs_t)
                + jnp.where(riota == LB - 1, dalast, 0.0))
        da = _cumsum_rows(dacs, True)
        ddt = ddt_c + da * a_neg
        da_log = jnp.sum(da * dt, axis=0, keepdims=True) * a_neg
        ddtr = ddt * (1.0 - jnp.exp(-dt))
        dzxd_ref[:, SW + D:ZXD] = ddtr.astype(BF16)
        sm_ref[6:7, 0:LB] += jnp.sum(ddtr, axis=0, keepdims=True)
        sm_ref[6:7, LB:2 * LB] += da_log
        sm_ref[6:7, 2 * LB:3 * LB] += ddsk

        du = cat(dxs + dbs + dcs) * (sg_u * (1.0 + u * (1.0 - sg_u)))
        xv = x_ref[...]
        adv = [du] + _conv_advances(du, dun_scr[...])
        sm_ref[0:1, :] += jnp.sum(du, axis=0, keepdims=True)
        dxbc = cw_ref[CONVK - 1:CONVK, :] * du
        for j in range(CONVK):
            sm_ref[CONVK - j:CONVK + 1 - j, :] += jnp.sum(adv[j] * xv, axis=0, keepdims=True)
            if j:
                dxbc = dxbc + cw_ref[CONVK - 1 - j:CONVK - j, :] * adv[j]
        dun_scr[...] = du[0:SUB, :]
        dzxd_ref[:, SW:SW + D] = dxbc.astype(BF16)

    run(*refs)


def _attn_bwd_block(i, n, q_ref, kp_ref, kc_ref, vp_ref, vc_ref, o_ref, do_ref, cos_ref, sin_ref, pr_ref, ps_ref,
                    dq_ref, dkv_ref, ds_ref, ck_scr, cv_scr):
    @pl.when(i == 0)
    def _():
        ds_ref[...] = jnp.zeros_like(ds_ref)
        ck_scr[...] = jnp.zeros_like(ck_scr)
        cv_scr[...] = jnp.zeros_like(cv_scr)

    qv, ov, dov = q_ref[...], o_ref[...], do_ref[...]
    kcat = jnp.concatenate([kp_ref[...], kc_ref[...]], axis=0)
    vcat = jnp.concatenate([vp_ref[...], vc_ref[...]], axis=0)
    upper = _upper_mask()
    srow = lax.broadcasted_iota(jnp.int32, (8, LB), 0)
    slane = lax.broadcasted_iota(jnp.int32, (8, LB), 1)
    dsink = jnp.zeros((8, LB), F32)
    dq_g, dk_g, dv_g = [], [], []
    for g in range(NQ // QPG):
        sl = slice(g * HD, (g + 1) * HD)
        qg = _stack_heads(qv, g)
        dog = _stack_heads(dov, g)
        rows = slice(g * QPG * LB, (g + 1) * QPG * LB)
        probs = pr_ref[0, rows, :].astype(F32)
        psink = ps_ref[0, rows, :].astype(F32)
        delta = _row_sums_wide(dog.astype(F32) * _stack_heads(ov, g).astype(F32), 2)
        dsc = probs * (_band(upper, _dot_nt(dog, vcat[0:LB, sl]), _dot_nt(dog, vcat[LB:2 * LB, sl])) - delta)
        sink_terms = (psink * delta)[:, 0:1]
        for hh in range(QPG):
            dsink = dsink - jnp.where((srow == QPG * g + hh) & (slane == 0),
                                      jnp.sum(sink_terms[hh * LB:(hh + 1) * LB, :]), 0.0)
        ds_p = jnp.where(upper, dsc, 0.0).astype(BF16)
        ds_c = jnp.where(upper, 0.0, dsc).astype(BF16)
        dq_g.append((_dot(ds_p, kcat[0:LB, sl]) + _dot(ds_c, kcat[LB:2 * LB, sl])) * ATT_SCALE)
        dk_g.append(jnp.concatenate([_dot_tn(ds_p, qg), _dot_tn(ds_c, qg)], axis=0) * ATT_SCALE)
        dv_g.append(jnp.concatenate([_dot_tn(jnp.where(upper, probs, 0.0).astype(BF16), dog),
                                     _dot_tn(jnp.where(upper, 0.0, probs).astype(BF16), dog)], axis=0))
    ds_ref[...] += dsink
    cs = cos_ref[...]
    sn = sin_ref[...]
    dk2 = jnp.concatenate(dk_g, axis=1)
    dv2 = jnp.concatenate(dv_g, axis=1)
    for a, tile in enumerate(_unstack_heads(dq_g)):
        dq_ref[:, a * LB:(a + 1) * LB] = _rope(tile, cs, sn, True).astype(BF16)
    dkv_ref[:, 0:KVW] = _rope(ck_scr[...] + dk2[LB:2 * LB, :], cs, sn, True).astype(BF16)
    dkv_ref[:, KVW:2 * KVW] = (cv_scr[...] + dv2[LB:2 * LB, :]).astype(BF16)
    ck_scr[...] = dk2[0:LB, :]
    cv_scr[...] = dv2[0:LB, :]


def _mixer_bwd(q, k, v, o, do, cos, sin, probs, psink, dyn, y, z, xbc, u, dtv, acs, decay, conv_w8, ssm_p, nw, hs,
               scatters):
    s = q.shape[0]
    nb = s // LB
    bps = _blocks_per_step(nb)
    nsteps = nb // bps
    tl = bps * LB
    cur = lambda i: (nsteps - 1 - i, 0)
    prev = lambda i: (jnp.maximum((nsteps - 1 - i) * bps - 1, 0), 0)
    n_in = 25
    items, ex_shapes, n_g = _exchange_items([], scatters)
    ne = len(items)
    e64, e128 = _head_expanders()
    stack = pl.BlockSpec((bps, NH * LB, LB), lambda i: (nsteps - 1 - i, 0, 0))

    def body(*refs):
        i = pl.program_id(0)
        (q_ref, kp_ref, kc_ref, vp_ref, vc_ref, o_ref, do_ref, cos_ref, sin_ref, pr_ref, ps_ref,
         dyn_ref, y_ref, z_ref, x_ref, u_ref, dt_ref, acs_ref, dm_ref, cw_ref, sp_ref, nw_ref,
         hs_ref, e64_ref, e128_ref) = refs[:n_in]
        ex_in = refs[n_in:n_in + ne]
        dp_ref, ds_ref, sm_ref = refs[n_in + ne:n_in + ne + 3]
        ex_out = refs[n_in + ne + 3:n_in + 2 * ne + 3]
        ck_scr, cv_scr, dh_scr, dun_scr = refs[n_in + 2 * ne + 3:n_in + 2 * ne + 7]
        sems = refs[n_in + 2 * ne + 7:]

        @pl.when(i == 0)
        def _():
            _Exchange(n_g, ex_in, ex_out, sems).start()

        for back in range(bps):
            sub = bps - 1 - back
            step = i * bps + back
            blk = (nsteps - 1 - i) * bps + sub
            r = slice(sub * LB, (sub + 1) * LB)
            before = slice((sub - 1) * LB, sub * LB)
            one = slice(sub, sub + 1)
            _attn_bwd_block(step, blk, q_ref.at[r], kp_ref if sub == 0 else kc_ref.at[before], kc_ref.at[r],
                            vp_ref if sub == 0 else vc_ref.at[before], vc_ref.at[r], o_ref.at[r], do_ref.at[r],
                            cos_ref.at[r], sin_ref.at[r], pr_ref.at[one], ps_ref.at[one],
                            dp_ref.at[r, O_Q:O_K], dp_ref.at[r, O_K:O_Z], ds_ref, ck_scr, cv_scr)
            _ssd_bwd_block(step, blk, dyn_ref.at[r], y_ref.at[r], z_ref.at[r], x_ref.at[r], u_ref.at[r], dt_ref.at[r],
                           acs_ref.at[r], dm_ref.at[one], cw_ref, sp_ref, nw_ref,
                           hs_ref.at[one], e64_ref, e128_ref, dp_ref.at[r, O_Z:INP], sm_ref, dh_scr, dun_scr)

        @pl.when(i == nsteps - 1)
        def _():
            _Exchange(n_g, ex_in, ex_out, sems).finish()

    any_spec = pl.BlockSpec(memory_space=pl.ANY)
    outs = _pcall(
        body, name="mixer_bwd", grid=(nsteps,),
        in_specs=[pl.BlockSpec((tl, AW), cur), pl.BlockSpec((LB, KVW), prev), pl.BlockSpec((tl, KVW), cur),
                  pl.BlockSpec((LB, KVW), prev), pl.BlockSpec((tl, KVW), cur), pl.BlockSpec((tl, AW), cur),
                  pl.BlockSpec((tl, AW), cur), pl.BlockSpec((tl, LB), cur), pl.BlockSpec((tl, LB), cur),
                  stack, stack,
                  pl.BlockSpec((tl, SW), cur), pl.BlockSpec((tl, SW), cur), pl.BlockSpec((tl, SW), cur),
                  pl.BlockSpec((tl, D), cur), pl.BlockSpec((tl, D), cur), pl.BlockSpec((tl, LB), cur),
                  pl.BlockSpec((tl, LB), cur), stack,
                  _const((8, D)), _const((8, LB)), _const((1, SW)),
                  pl.BlockSpec((bps, NH * HD, NST), lambda i: (nsteps - 1 - i, 0, 0)),
                  _const(e64.shape), _const(e128.shape)] + [any_spec] * ne,
        out_specs=[pl.BlockSpec((tl, INP), cur), pl.BlockSpec((8, LB), lambda i: (0, 0)),
                   pl.BlockSpec((8, D), lambda i: (0, 0))] + [any_spec] * ne,
        out_shape=[_sds((s, INP), BF16), _sds((8, LB), F32), _sds((8, D), F32)] + ex_shapes,
        scratch_shapes=[pltpu.VMEM((LB, KVW), F32), pltpu.VMEM((LB, KVW), F32),
                        pltpu.VMEM((NH * HD, NST), F32), pltpu.VMEM((SUB, D), F32)]
        + _exchange_sems(ne),
        compiler_params=_params(),
    )(q, k, k, v, v, o, do, cos, sin, probs, psink, dyn, y, z, xbc, u, dtv, acs, decay, conv_w8, ssm_p, nw, hs,
      e64, e128, *items)
    return outs[0], outs[1], outs[2], outs[3:]


def _inproj_bwd(dproj, x, dx2, mod8, n1w, w_in_t, scatters, smalls):
    s = x.shape[0]
    tt = min(512, s)
    nt = s // tt
    items, ex_shapes, n_g = _exchange_items([], scatters)
    ne = len(items)
    n_in = 10

    def body(*refs):
        dp_ref, x_ref, dx2_ref, mod_ref, nw_ref, w_ref, f_ref, b_ref, s_ref, k_ref = refs[:n_in]
        ex_in = refs[n_in:n_in + ne]
        gx_ref, sm_ref = refs[n_in + ne:n_in + 2 + ne]
        ex_out = refs[n_in + 2 + ne:n_in + 2 + 2 * ne]
        gpack_ref = refs[n_in + 2 + 2 * ne]
        pack_scr = refs[n_in + 3 + 2 * ne]
        sems = refs[n_in + 4 + 2 * ne:n_in + 7 + 2 * ne]
        pack_sems = refs[n_in + 7 + 2 * ne:]
        i = pl.program_id(0)

        @pl.when(i == 0)
        def _():
            sm_ref[...] = jnp.zeros_like(sm_ref)
            _Exchange(n_g, ex_in, ex_out, sems).start()

        w = w_ref[...]
        hr = tt // 2
        dh1 = [_dot(dp_ref[h * hr:(h + 1) * hr, :], w) for h in range(2)]
        sums = jnp.zeros((3, D), F32)
        for h in range(2):
            rows = slice(h * hr, (h + 1) * hr)
            dxn, d_shift, d_scale, d_w = _norm_mod_bwd(x_ref[rows, :], dh1[h], nw_ref[...], mod_ref[1:2, :])
            gx_ref[rows, :] = dx2_ref[rows, :] + dxn
            sums = sums + jnp.concatenate([d_shift, d_scale, d_w], axis=0)
        sm_ref[0:3, :] += sums

        @pl.when(i == nt - 1)
        def _():
            _pack_rows(f_ref, b_ref, s_ref, sm_ref, k_ref, pack_scr)
            small = _Exchange(1, [pack_scr], [gpack_ref], pack_sems)
            small.start()
            _Exchange(n_g, ex_in, ex_out, sems).finish()
            small.finish()

    any_spec = pl.BlockSpec(memory_space=pl.ANY)
    outs = _pcall(
        body, name="inproj_bwd", grid=(nt,),
        in_specs=[_rows(tt, INP), _rows(tt, D), _rows(tt, D), _const((8, D)), _const((1, D)), _const((INP, D)),
                  _const((8, D)), _const((8, D)), _const((8, D)), _const((8, LB))]
        + [any_spec] * ne,
        out_specs=[_rows(tt, D), pl.BlockSpec((8, D), lambda i: (0, 0))] + [any_spec] * (ne + 1),
        out_shape=[_sds((s, D), F32), _sds((8, D), F32)] + ex_shapes + [_sds((N_DEV, PACK_ROWS, D), F32)],
        scratch_shapes=[pltpu.VMEM((PACK_ROWS, D), F32)] + _exchange_sems(ne) + _exchange_sems(1),
        compiler_params=_params(),
    )(dproj, x, dx2, mod8, n1w, w_in_t, *smalls, *items)
    return outs[0], outs[2:2 + ne], outs[2 + ne]


def _wgrad(a, b, name):
    s, m = a.shape
    n = b.shape[1]
    tk = min(2048, s)
    wide = (1408, 1024, 512)
    tm = next((t for t in wide if m % t == 0), m)
    tn = n if n <= 2048 else _largest_divisor(n, wide)
    nk = s // tk

    def body(a_ref, b_ref, o_ref, acc):
        kk = pl.program_id(2)

        @pl.when(kk == 0)
        def _():
            acc[...] = jnp.zeros_like(acc)

        acc[...] += _dot_tn(a_ref[...], b_ref[...])

        @pl.when(kk == nk - 1)
        def _():
            o_ref[...] = acc[...].astype(BF16)

    return _pcall(
        body, name=name, grid=(m // tm, n // tn, nk),
        in_specs=[pl.BlockSpec((tk, tm), lambda i, j, kk: (kk, i)), pl.BlockSpec((tk, tn), lambda i, j, kk: (kk, j))],
        out_specs=pl.BlockSpec((tm, tn), lambda i, j, kk: (i, j)),
        out_shape=_sds((m, n), BF16),
        scratch_shapes=[pltpu.VMEM((tm, tn), F32)],
        compiler_params=_params(3),
    )(a, b)


PACK_ROWS = 24


def _pack_rows(f_ref, b_ref, s_ref, i_ref, k_ref, o_ref):
    o_ref[...] = jnp.zeros_like(o_ref)
    o_ref[0:2, :] = i_ref[0:2, :]
    o_ref[2:3, :] = b_ref[3:4, :]
    o_ref[3:5, :] = b_ref[0:2, :]
    o_ref[5:6, :] = f_ref[0:1, :]
    o_ref[6:7, :] = i_ref[2:3, :]
    o_ref[7:8, :] = b_ref[2:3, :]
    o_ref[8:9, :] = f_ref[1:2, :]
    o_ref[9:14, :] = s_ref[0:5, :]
    o_ref[14:15, :] = s_ref[5:6, :]
    o_ref[15:16, 0:3 * LB] = s_ref[6:7, 0:3 * LB]
    lane = lax.broadcasted_iota(jnp.int32, (1, LB), 1)
    sk = jnp.zeros((1, LB), F32)
    for h in range(NQ):
        sk = sk + jnp.where(lane == h, k_ref[h:h + 1, 0:1], 0.0)
    o_ref[15:16, 3 * LB:4 * LB] = sk
    o_ref[16:17, :] = f_ref[2:3, :]


def _exchange(gathers, scatters, name, two_level=False):
    items, shapes, n_g = _exchange_items(gathers, scatters)
    n = len(items)
    assert not (two_level and scatters)

    def body(*refs):
        ex = _Exchange(n_g, refs[:n], refs[n:2 * n], refs[2 * n:])
        if two_level:
            ex.gather_two_level()
        else:
            ex.start()
            ex.finish()

    any_spec = pl.BlockSpec(memory_space=pl.ANY)
    return _pcall(
        body, name=name, in_specs=[any_spec] * n, out_specs=[any_spec] * n, out_shape=shapes,
        scratch_shapes=_exchange_sems(n),
    )(*items)


def _exchange_items(gathers, scatters):
    items = list(gathers) + list(scatters)
    shapes = [_sds((N_DEV,) + a.shape, a.dtype) for a in gathers] + [_sds(a.shape, a.dtype) for a in scatters]
    return items, shapes, len(gathers)


def _exchange_sems(n):
    return [pltpu.SemaphoreType.DMA((n, N_DEV - 1)), pltpu.SemaphoreType.DMA((n, N_DEV - 1)),
            pltpu.SemaphoreType.DMA((n,))]


class _Exchange:
    def __init__(self, n_g, ins, outs, sems):
        self.n_g, self.ins, self.outs = n_g, ins, outs
        self.send_sems, self.recv_sems, self.loc_sems = sems
        xi, yi, ci = lax.axis_index("x"), lax.axis_index("y"), lax.axis_index("c")
        self.me = 4 * xi + 2 * yi + ci
        self.peers = []
        for r in range(1, N_DEV):
            px = 1 - xi if r & 4 else xi
            py = 1 - yi if r & 2 else yi
            pc = 1 - ci if r & 1 else ci
            self.peers.append(((px, py, pc), 4 * px + 2 * py + pc))

    def _copy(self, t, r, landing):
        dev, peer = self.peers[r]
        src = self.ins[t] if t < self.n_g else self.ins[t].at[peer]
        return pltpu.make_async_remote_copy(
            src_ref=src, dst_ref=self.outs[t].at[landing], send_sem=self.send_sems.at[t, r],
            recv_sem=self.recv_sems.at[t, r], device_id=dev, device_id_type=pl.DeviceIdType.MESH)

    def _local(self, t):
        src = self.ins[t] if t < self.n_g else self.ins[t].at[self.me]
        return pltpu.make_async_copy(src, self.outs[t].at[self.me], self.loc_sems.at[t])

    def start(self):
        for t in range(len(self.ins)):
            self._local(t).start()
            for r in range(N_DEV - 1):
                self._copy(t, r, self.me).start()

    def finish(self):
        n = len(self.ins)
        for t in range(n):
            for r in range(N_DEV - 1):
                self._copy(t, r, self.peers[r][1]).wait_recv()
        for t in range(n):
            for r in range(N_DEV - 1):
                self._copy(t, r, self.me).wait_send()
            self._local(t).wait()

    def gather_two_level(self):
        self.two_level_start()
        self.two_level_relay()
        self.two_level_finish()

    DIRECT = (0, 1, 3, 5)

    def two_level_start(self):
        for t in range(len(self.ins)):
            self._local(t).start()
            for r in self.DIRECT:
                self._copy(t, r, self.me).start()

    def _relay(self, t, r):
        peer = self.peers[r][1]
        return pltpu.make_async_remote_copy(
            src_ref=self.outs[t].at[peer], dst_ref=self.outs[t].at[peer], send_sem=self.send_sems.at[t, r + 1],
            recv_sem=self.recv_sems.at[t, r + 1], device_id=self.peers[0][0], device_id_type=pl.DeviceIdType.MESH)

    def two_level_relay(self):
        for t in range(len(self.ins)):
            for r in self.DIRECT[1:]:
                self._copy(t, r, self.peers[r][1]).wait_recv()
                self._relay(t, r).start()

    def two_level_finish(self):
        n = len(self.ins)
        for t in range(n):
            for r in (0, 2, 4, 6):
                self._copy(t, r, self.peers[r][1]).wait_recv()
        for t in range(n):
            for r in self.DIRECT:
                self._copy(t, r, self.me).wait_send()
            for r in self.DIRECT[1:]:
                self._relay(t, r).wait_send()
            self._local(t).wait()


def _prologue(c8, w_in_tb, cw8, w_cols, b_cols):
    ncol = w_cols.shape[1]

    def body(c_ref, win_ref, cw_ref, w_ref, b_ref, gc_ref, gin_ref, gcw_ref, gmod_ref, call_scr, mod_scr, loc_sem,
             *sems):
        big = _Exchange(2, [win_ref, cw_ref], [gin_ref, gcw_ref], sems[0:3])
        big.two_level_start()
        small = _Exchange(1, [c_ref], [gc_ref], sems[3:6])
        small.start()
        small.finish()
        landed = pltpu.make_async_copy(gc_ref, call_scr, loc_sem)
        landed.start()
        landed.wait()
        cv = call_scr[:, 0, :]
        sc = (cv * _sigmoid(cv)).astype(BF16)
        mod_scr[...] = _dot(sc, w_ref[...].astype(BF16)) + b_ref[...]
        mods = _Exchange(1, [mod_scr], [gmod_ref], sems[6:9])
        mods.start()
        mods.finish()
        big.two_level_relay()
        big.two_level_finish()

    any_spec = pl.BlockSpec(memory_space=pl.ANY)
    vmem_spec = pl.BlockSpec(memory_space=pltpu.VMEM)
    return _pcall(
        body, name="prologue", in_specs=[any_spec, any_spec, any_spec, vmem_spec, vmem_spec],
        out_specs=[any_spec] * 4,
        out_shape=[_sds((N_DEV,) + c8.shape, F32), _sds((N_DEV,) + w_in_tb.shape, BF16),
                   _sds((N_DEV,) + cw8.shape, F32), _sds((N_DEV, N_DEV, ncol), F32)],
        scratch_shapes=[pltpu.VMEM((N_DEV,) + c8.shape, F32), pltpu.VMEM((N_DEV, ncol), F32),
                        pltpu.SemaphoreType.DMA] + _exchange_sems(2) + _exchange_sems(1) + _exchange_sems(1),
        compiler_params=pltpu.CompilerParams(vmem_limit_bytes=VMEM_LIMIT),
    )(c8, w_in_tb, cw8, w_cols, b_cols)


def _adamw(w, g, m, v):
    m2 = ADAM_B1 * m + (1.0 - ADAM_B1) * g
    v2 = ADAM_B2 * v + (1.0 - ADAM_B2) * (g * g)
    m_hat = m2 / (1.0 - ADAM_B1 ** ADAM_STEP)
    v_hat = v2 / (1.0 - ADAM_B2 ** ADAM_STEP)
    delta = -ADAM_LR * (m_hat / (jnp.sqrt(v_hat) + ADAM_EPS) + ADAM_WD * w)
    return delta, m2, v2


def _sum_adamw(parts, w, m, v, name):
    rws, cols = w.shape
    tr = next((t for t in (256, 176, 128) if rws % t == 0), rws)

    def body(p_ref, w_ref, m_ref, v_ref, g_ref, d_ref, mo_ref, vo_ref):
        g = p_ref[0].astype(F32)
        for dev in range(1, N_DEV):
            g = g + p_ref[dev].astype(F32)
        g_ref[...] = g
        d_ref[...], mo_ref[...], vo_ref[...] = _adamw(w_ref[...], g, m_ref[...], v_ref[...])

    blk = pl.BlockSpec((tr, cols), lambda i: (i, 0))
    return _pcall(
        body, name=name, grid=(rws // tr,),
        in_specs=[pl.BlockSpec((N_DEV, tr, cols), lambda i: (0, i, 0)), blk, blk, blk],
        out_specs=[blk] * 4, out_shape=[_sds((rws, cols), F32)] * 4, compiler_params=_params(),
    )(parts, w, m, v)


def _wada_adamw(c_all, dmod_cols, w, m, v):
    rws, cols = w.shape
    tr = 256

    def body(c_ref, dm_ref, w_ref, m_ref, v_ref, g_ref, d_ref, mo_ref, vo_ref):
        cv = c_ref[...]
        sc = (cv * _sigmoid(cv)).astype(BF16)
        g = _dot_tn(sc, dm_ref[...].astype(BF16))
        g_ref[...] = g
        d_ref[...], mo_ref[...], vo_ref[...] = _adamw(w_ref[...], g, m_ref[...], v_ref[...])

    blk = pl.BlockSpec((tr, cols), lambda i: (i, 0))
    return _pcall(
        body, name="wada_adamw", grid=(rws // tr,),
        in_specs=[pl.BlockSpec((N_DEV, tr), lambda i: (0, i)), pl.BlockSpec((N_DEV, cols), lambda i: (0, 0)),
                  blk, blk, blk],
        out_specs=[blk] * 4, out_shape=[_sds((rws, cols), F32)] * 4, compiler_params=_params(),
    )(c_all, dmod_cols, w, m, v)


def _small_reduce(packs):
    def body(p_ref, o_ref):
        tot = p_ref[0]
        for dev in range(1, N_DEV):
            tot = tot + p_ref[dev]
        o_ref[...] = tot
        o_ref[16:17, :] = jnp.zeros((1, D), F32) + (0.5 / D) * jnp.sum(tot[16:17, :])

    return _pcall(body, name="small_reduce", out_shape=_sds((PACK_ROWS, D), F32))(packs)


def _adamw_many(ws, gs, ms, vs):
    k = len(ws)

    def body(*refs):
        for i in range(k):
            w_ref, g_ref, m_ref, v_ref = refs[i], refs[k + i], refs[2 * k + i], refs[3 * k + i]
            d_ref, mo_ref, vo_ref = refs[4 * k + i], refs[5 * k + i], refs[6 * k + i]
            d_ref[...], mo_ref[...], vo_ref[...] = _adamw(w_ref[...], g_ref[...], m_ref[...], v_ref[...])

    shp = [_sds(w.shape, F32) for w in ws]
    outs = _pcall(body, name="adamw_small", out_shape=shp * 3)(*ws, *gs, *ms, *vs)
    return outs[:k], outs[k:2 * k], outs[2 * k:]


def kernel(x, c, positions, w_ada, b_ada, norm1_w, w_in, conv_w, conv_b, dt_bias, a_log, d_skip, attn_sinks, ssm_norm_w, w_out, norm2_w, w_gate_up, w_down, final_norm_w, loss_target, m_w_ada, m_b_ada, m_norm1_w, m_w_in, m_conv_w, m_conv_b, m_dt_bias, m_a_log, m_d_skip, m_attn_sinks, m_ssm_norm_w, m_w_out, m_norm2_w, m_w_gate_up, m_w_down, m_final_norm_w, v_w_ada, v_b_ada, v_norm1_w, v_w_in, v_conv_w, v_conv_b, v_dt_bias, v_a_log, v_d_skip, v_attn_sinks, v_ssm_norm_w, v_w_out, v_norm2_w, v_w_gate_up, v_w_down, v_final_norm_w):
    s = x.shape[1]
    me = 4 * lax.axis_index("x") + 2 * lax.axis_index("y") + lax.axis_index("c")
    ada_cols = N_MOD * D // N_DEV

    c8 = jnp.pad(c, ((0, 7), (0, 0)))
    cw8 = jnp.pad(conv_w[0], ((0, 8 - CONVK), (0, 0)))
    w_in_t, m_w_in_t, v_w_in_t = jnp.transpose(w_in[0]), jnp.transpose(m_w_in[0]), jnp.transpose(v_w_in[0])
    w_gu_t, m_w_gu_t, v_w_gu_t = (jnp.transpose(w_gate_up[0]), jnp.transpose(m_w_gate_up[0]),
                                  jnp.transpose(v_w_gate_up[0]))
    b_cols = lax.dynamic_slice(b_ada, (0, me * ada_cols), (1, ada_cols))
    g_c, g_in, g_cw, g_mod = _prologue(c8, w_in_t.astype(BF16), cw8, w_ada[0], b_cols)
    c_all = g_c[:, 0, :]
    w_in_f = jnp.pad(g_in.reshape(IN_PROJ, D), ((0, INP - IN_PROJ), (0, 0)))
    conv_w8 = jnp.transpose(g_cw, (1, 0, 2)).reshape(8, D)
    mod = lax.dynamic_index_in_dim(g_mod, me, axis=1, keepdims=False).reshape(N_MOD, D)
    mod8 = jnp.pad(mod, ((0, 8 - N_MOD), (0, 0)))

    half = HD // 2
    inv_freq = ROPE_THETA ** (-jnp.arange(half, dtype=F32) / half)
    invf = jnp.tile(inv_freq, LB // half).reshape(1, LB)
    lanes = lambda a: jnp.pad(a, ((0, 0), (0, LB - a.shape[1])))
    ssm_p = jnp.pad(jnp.concatenate([lanes(dt_bias), lanes(a_log), lanes(d_skip)], axis=0), ((0, 5), (0, 0)))
    sinks8 = jnp.broadcast_to(attn_sinks.reshape(NQ, 1), (NQ, LB))

    xs, tgt, fnw = x[0], loss_target[0], final_norm_w.reshape(1, D)

    q, k, v, z, xbc, dtr, h1, cos, sin = _inproj_fwd(xs, positions[0].reshape(s, 1), invf, mod8, norm1_w, w_in_f)
    (attn, yn, y, hs, conv_u, dtv, acs, decay, probs, psink), (g_out, g_gu, g_down) = _mixer_fwd(
        q, k, v, sinks8, xbc, conv_w8, conv_b, dtr, ssm_p, z, ssm_norm_w,
        [w_out[0].astype(BF16), w_gu_t.astype(BF16), w_down[0].astype(BF16)])
    w_out_f = g_out.reshape(D, D)
    w_gu_f = g_gu.reshape(2 * DFF, D)
    w_down_f = g_down.reshape(DFF, D)
    x2, h2, mo, mix, gu, act, dx3, sm_f = _outproj_ffn_fwd_loss(attn, yn, xs, tgt, mod8, norm2_w, fnw, w_out_f, w_gu_f,
                                                                 w_down_f)

    dx2, dff, dgu, dmix, dattn, dyn, sm_b = _ffn_bwd(dx3, gu, x2, mo, mod8, norm2_w, w_gu_f, w_down_f, w_out_f)
    p_gu = _wgrad(dgu, h2, "wgrad_gate_up").reshape(N_DEV, 2 * DFF // N_DEV, D)
    p_down = _wgrad(act, dff, "wgrad_down").reshape(N_DEV, DFF // N_DEV, D)
    p_out = _wgrad(mix, dmix, "wgrad_out").reshape(N_DEV, D // N_DEV, D)
    dproj, dsink, sm_s, (r_gu, r_down, r_out) = _mixer_bwd(
        q, k, v, attn, dattn, cos, sin, probs, psink, dyn, y, z, xbc, conv_u, dtv, acs, decay, conv_w8, ssm_p,
        ssm_norm_w, hs, [p_gu, p_down, p_out])
    p_in = _wgrad(dproj, h1, "wgrad_in")[:IN_PROJ].reshape(N_DEV, IN_PROJ // N_DEV, D)
    gx, (r_in,), g_pack = _inproj_bwd(dproj, xs, dx2, mod8, norm1_w, w_in_f, [p_in], (sm_f, sm_b, sm_s, dsink))

    tot = _small_reduce(g_pack)
    loss = tot[16, 0]
    dmod_all = g_pack[:, 0:N_MOD, :].reshape(N_DEV, N_MOD * D)
    dmod_cols = lax.dynamic_slice(dmod_all, (0, me * ada_cols), (N_DEV, ada_cols))

    big = {
        "w_ada": _wada_adamw(c_all, dmod_cols, w_ada[0], m_w_ada[0], v_w_ada[0]),
        "w_in": [jnp.transpose(t) for t in _sum_adamw(r_in, w_in_t, m_w_in_t, v_w_in_t, "adamw_in")],
        "w_out": _sum_adamw(r_out, w_out[0], m_w_out[0], v_w_out[0], "adamw_out"),
        "w_gate_up": [jnp.transpose(t) for t in _sum_adamw(r_gu, w_gu_t, m_w_gu_t, v_w_gu_t, "adamw_gate_up")],
        "w_down": _sum_adamw(r_down, w_down[0], m_w_down[0], v_w_down[0], "adamw_down"),
    }
    small_names = ["b_ada", "norm1_w", "conv_w", "conv_b", "dt_bias", "a_log", "d_skip", "attn_sinks", "ssm_norm_w",
                   "norm2_w", "final_norm_w"]
    row15 = tot[15:16, :]
    small_g = {
        "b_ada": tot[0:N_MOD, :].reshape(1, N_MOD * D),
        "norm1_w": tot[6:7, :],
        "conv_w": lax.dynamic_slice(tot[10:14, :], (0, me * (D // N_DEV)), (CONVK, D // N_DEV)),
        "conv_b": tot[9:10, :],
        "dt_bias": row15[:, 0:NH],
        "a_log": row15[:, LB:LB + NH],
        "d_skip": row15[:, 2 * LB:2 * LB + NH],
        "attn_sinks": row15[:, 3 * LB:3 * LB + NQ],
        "ssm_norm_w": tot[14:15, 0:SW],
        "norm2_w": tot[7:8, :],
        "final_norm_w": tot[8:9, :],
    }
    small_w = {"b_ada": b_ada, "norm1_w": norm1_w, "conv_w": conv_w[0], "conv_b": conv_b, "dt_bias": dt_bias,
               "a_log": a_log, "d_skip": d_skip, "attn_sinks": attn_sinks, "ssm_norm_w": ssm_norm_w,
               "norm2_w": norm2_w, "final_norm_w": final_norm_w.reshape(1, D)}
    small_m = {"b_ada": m_b_ada, "norm1_w": m_norm1_w, "conv_w": m_conv_w[0], "conv_b": m_conv_b,
               "dt_bias": m_dt_bias, "a_log": m_a_log, "d_skip": m_d_skip, "attn_sinks": m_attn_sinks,
               "ssm_norm_w": m_ssm_norm_w, "norm2_w": m_norm2_w, "final_norm_w": m_final_norm_w.reshape(1, D)}
    small_v = {"b_ada": v_b_ada, "norm1_w": v_norm1_w, "conv_w": v_conv_w[0], "conv_b": v_conv_b,
               "dt_bias": v_dt_bias, "a_log": v_a_log, "d_skip": v_d_skip, "attn_sinks": v_attn_sinks,
               "ssm_norm_w": v_ssm_norm_w, "norm2_w": v_norm2_w, "final_norm_w": v_final_norm_w.reshape(1, D)}
    s_d, s_m, s_v = _adamw_many([small_w[k] for k in small_names], [small_g[k] for k in small_names],
                                [small_m[k] for k in small_names], [small_v[k] for k in small_names])

    order = ["w_ada", "b_ada", "norm1_w", "w_in", "conv_w", "conv_b", "dt_bias", "a_log", "d_skip", "attn_sinks",
             "ssm_norm_w", "w_out", "norm2_w", "w_gate_up", "w_down", "final_norm_w"]
    lead = {"w_ada", "w_in", "conv_w", "w_out", "w_gate_up", "w_down"}
    grads, deltas, new_m, new_v = [], [], [], []
    for name in order:
        if name in big:
            g, d, m2, v2 = big[name]
        else:
            i = small_names.index(name)
            g, d, m2, v2 = small_g[name], s_d[i], s_m[i], s_v[i]
        if name in lead:
            g, d, m2, v2 = g[None], d[None], m2[None], v2[None]
        if name == "final_norm_w":
            g, d, m2, v2 = g.reshape(D), d.reshape(D), m2.reshape(D), v2.reshape(D)
        grads.append(g)
        deltas.append(d)
        new_m.append(m2)
        new_v.append(v2)
    return (loss, gx[None], *grads, *deltas, *new_m, *new_v)
```

```python
import functools
import math

import jax
import jax.numpy as jnp
from jax import lax
from jax.experimental import pallas as pl
from jax.experimental.pallas import tpu as pltpu

F32 = jnp.float32
BF16 = jnp.bfloat16

N_DEV = 8
D = 1024
HD = 64
NQ = 8
AW = 512
KVW = 128
SW = 512
NST = 128
NH = 8
LB = 128
CONVK = 4
DFF = 2816
N_MOD = 6
IN_PROJ = 2312
INP = 2432
O_Q, O_K, O_V, O_Z, O_XBC, O_DT = 0, 512, 640, 768, 1280, 2304
ZXD = INP - O_Z
EPS = 1e-6
NEG = -1e30
ROPE_THETA = 10000.0
VMEM_LIMIT = 56 * 1024 * 1024

ADAM_LR = 0.001
ADAM_B1 = 0.9
ADAM_B2 = 0.999
ADAM_EPS = 1e-08
ADAM_WD = 0.01
ADAM_STEP = 10

NT_DIMS = (((1,), (1,)), ((), ()))
TN_DIMS = (((0,), (0,)), ((), ()))


def _pcall(body, **kw):
    return pl.pallas_call(body, **kw)


def _sds(shape, dtype):
    return jax.ShapeDtypeStruct(shape, dtype)


def _params(n_grid=1):
    return pltpu.CompilerParams(dimension_semantics=("arbitrary",) * n_grid, vmem_limit_bytes=VMEM_LIMIT)


def _const(shape):
    return pl.BlockSpec(shape, lambda *_: (0,) * len(shape), pipeline_mode=pl.Buffered(1))


def _largest_divisor(n, candidates):
    for cand in candidates:
        if n % cand == 0:
            return cand
    raise ValueError(f"no tile in {candidates} divides {n}")


def _rows(t, w):
    return pl.BlockSpec((t, w), lambda i: (i, 0))


def _dot(a, b):
    return jnp.dot(a, b, preferred_element_type=F32)


def _dot_nt(a, b):
    return lax.dot_general(a, b, NT_DIMS, preferred_element_type=F32)


def _dot_tn(a, b):
    return lax.dot_general(a, b, TN_DIMS, preferred_element_type=F32)


def _sigmoid(v):
    return 1.0 / (1.0 + jnp.exp(-v))


def _softplus(v):
    return jnp.maximum(v, 0.0) + jnp.log1p(jnp.exp(-jnp.abs(v)))


def _rope_sign_mask(shape):
    lane = lax.broadcasted_iota(jnp.int32, shape, 1)
    return (lane % HD) < (HD // 2)


def _rope(t, cs, sn, inverse):
    r_dn = pltpu.roll(t, HD // 2, 1)
    r_up = pltpu.roll(t, LB - HD // 2, 1)
    first = _rope_sign_mask(t.shape)
    if inverse:
        rot = jnp.where(first, r_up, -r_dn)
    else:
        rot = jnp.where(first, -r_up, r_dn)
    return t * cs + rot * sn


def _norm_mod_fwd(xv, nw, shift, scale):
    r = lax.rsqrt(jnp.mean(xv * xv, axis=-1, keepdims=True) + EPS)
    xh = xv * r
    return (xh * nw) * (1.0 + scale) + shift


def _norm_mod_bwd(xv, dh, nw, scale):
    r = lax.rsqrt(jnp.mean(xv * xv, axis=-1, keepdims=True) + EPS)
    xh = xv * r
    xn = xh * nw
    d_shift = jnp.sum(dh, axis=0, keepdims=True)
    d_scale = jnp.sum(dh * xn, axis=0, keepdims=True)
    dxn = dh * (1.0 + scale)
    d_w = jnp.sum(dxn * xh, axis=0, keepdims=True)
    dxh = dxn * nw
    dx = r * (dxh - xh * jnp.mean(dxh * xh, axis=-1, keepdims=True))
    return dx, d_shift, d_scale, d_w


def _inproj_fwd(x, pos, invf, mod8, n1w, w_in):
    s = x.shape[0]
    tt = min(512, s)

    def body(x_ref, pos_ref, invf_ref, mod_ref, nw_ref, w_ref,
             q_ref, k_ref, v_ref, z_ref, xbc_ref, dtr_ref, h1_ref, cos_ref, sin_ref):
        h = _norm_mod_fwd(x_ref[...], nw_ref[...], mod_ref[0:1, :], mod_ref[1:2, :])
        hb = h.astype(BF16)
        h1_ref[...] = hb
        proj = _dot_nt(hb, w_ref[...])
        ang = pos_ref[...].astype(F32) * invf_ref[...]
        cs = jnp.cos(ang)
        sn = jnp.sin(ang)
        cos_ref[...] = cs
        sin_ref[...] = sn
        for a in range(AW // LB):
            q_ref[:, a * LB:(a + 1) * LB] = _rope(proj[:, O_Q + a * LB:O_Q + (a + 1) * LB], cs, sn, False).astype(BF16)
        k_ref[...] = _rope(proj[:, O_K:O_V], cs, sn, False).astype(BF16)
        v_ref[...] = proj[:, O_V:O_Z].astype(BF16)
        z_ref[...] = proj[:, O_Z:O_XBC]
        xbc_ref[...] = proj[:, O_XBC:O_DT]
        dtr_ref[...] = proj[:, O_DT:INP]

    return _pcall(
        body, name="inproj_fwd", grid=(s // tt,),
        in_specs=[_rows(tt, D), _rows(tt, 1), _const((1, LB)), _const((8, D)), _const((1, D)), _const((INP, D))],
        out_specs=[_rows(tt, AW), _rows(tt, KVW), _rows(tt, KVW), _rows(tt, SW), _rows(tt, D), _rows(tt, LB),
                   _rows(tt, D), _rows(tt, LB), _rows(tt, LB)],
        out_shape=[_sds((s, AW), BF16), _sds((s, KVW), BF16), _sds((s, KVW), BF16), _sds((s, SW), F32),
                   _sds((s, D), F32), _sds((s, LB), F32), _sds((s, D), BF16), _sds((s, LB), F32), _sds((s, LB), F32)],
        compiler_params=_params(),
    )(x, pos, invf, mod8, n1w, w_in)


QPG = 4
ATT_SCALE = 1.0 / math.sqrt(HD)


def _stack_heads(val, g):
    return jnp.concatenate([val[:, (QPG * g + hh) * HD:(QPG * g + hh + 1) * HD] for hh in range(QPG)], axis=0)


def _unstack_heads(groups):
    pieces = [grp[hh * LB:(hh + 1) * LB, :] for grp in groups for hh in range(QPG)]
    return [jnp.concatenate(pieces[2 * a:2 * a + 2], axis=1) for a in range(NQ // 2)]


def _upper_mask():
    row = lax.broadcasted_iota(jnp.int32, (QPG * LB, LB), 0)
    col = lax.broadcasted_iota(jnp.int32, (QPG * LB, LB), 1)
    return col > (row % LB)


def _sink_wide(sinks, g):
    return jnp.concatenate([jnp.broadcast_to(sinks[QPG * g + hh:QPG * g + hh + 1, 0:1], (LB, LB))
                            for hh in range(QPG)], axis=0)


def _row_sums_wide(v, terms):
    return _dot_sel(v, jnp.ones((v.shape[1], LB), BF16), terms)


def _band(upper, prev_part, cur_part):
    return jnp.where(upper, prev_part, cur_part)


def _attn_scores(n, qg, kcat, upper):
    sp = _dot_nt(qg, kcat[0:LB, :]) * ATT_SCALE
    sc = _dot_nt(qg, kcat[LB:2 * LB, :]) * ATT_SCALE
    return _band(upper, jnp.where(n > 0, sp, NEG), sc)


def _attn_softmax(comb, sink):
    m = jnp.maximum(jnp.max(comb, axis=-1, keepdims=True), sink)
    p = jnp.exp(comb - m)
    es = jnp.exp(sink - m)
    return p, es, _row_sums_wide(p, 1) + es


def _attn_fwd_block(n, q_ref, kp_ref, kc_ref, vp_ref, vc_ref, sink_ref, o_ref, pr_ref, ps_ref):
    qv = q_ref[...]
    kcat = jnp.concatenate([kp_ref[...], kc_ref[...]], axis=0)
    vcat = jnp.concatenate([vp_ref[...], vc_ref[...]], axis=0)
    sinks = sink_ref[...]
    upper = _upper_mask()
    outs = []
    for g in range(NQ // QPG):
        sl = slice(g * HD, (g + 1) * HD)
        rows = slice(g * QPG * LB, (g + 1) * QPG * LB)
        p, es, denom = _attn_softmax(_attn_scores(n, _stack_heads(qv, g), kcat[:, sl], upper), _sink_wide(sinks, g))
        rden = 1.0 / denom
        pr_ref[0, rows, :] = (p * rden).astype(BF16)
        ps_ref[0, rows, :] = (es * rden).astype(BF16)
        outs.append((_dot(jnp.where(upper, p, 0.0).astype(BF16), vcat[0:LB, sl])
                     + _dot(jnp.where(upper, 0.0, p).astype(BF16), vcat[LB:2 * LB, sl])) * rden[:, 0:HD])
    for g, grp in enumerate(outs):
        for hh in range(QPG):
            h = QPG * g + hh
            o_ref[:, h * HD:(h + 1) * HD] = grp[hh * LB:(hh + 1) * LB, :].astype(BF16)


def _cumsum_rows(a, reverse):
    row = lax.broadcasted_iota(jnp.int32, a.shape, 0)
    step = 1
    while step < LB:
        if reverse:
            a = a + jnp.where(row < LB - step, pltpu.roll(a, LB - step, 0), 0.0)
        else:
            a = a + jnp.where(row >= step, pltpu.roll(a, step, 0), 0.0)
        step *= 2
    return a


SUB = 8


def _conv_shifts(tail, cur):
    row = lax.broadcasted_iota(jnp.int32, tail.shape, 0)
    out = [cur]
    for j in range(1, CONVK):
        rolled = pltpu.roll(cur, j, 0)
        top = jnp.where(row < j, pltpu.roll(tail, j, 0), rolled[0:SUB, :])
        out.append(jnp.concatenate([top, rolled[SUB:, :]], axis=0))
    return out


def _conv_advances(du, head):
    row = lax.broadcasted_iota(jnp.int32, head.shape, 0)
    out = []
    for j in range(1, CONVK):
        rolled = pltpu.roll(du, LB - j, 0)
        bottom = jnp.where(row >= SUB - j, pltpu.roll(head, SUB - j, 0), rolled[LB - SUB:, :])
        out.append(jnp.concatenate([rolled[:LB - SUB, :], bottom], axis=0))
    return out


def _split(v, terms):
    out = []
    for _ in range(terms - 1):
        t = v.astype(BF16)
        out.append(t)
        v = v - t.astype(F32)
    out.append(v.astype(BF16))
    return out


def _dot_sel(v, sel, terms):
    parts = [_dot(t, sel) for t in _split(v, terms)]
    return functools.reduce(lambda a, b: a + b, parts)


def _dot_nt_sel(v, sel, terms):
    parts = [_dot_nt(t, sel) for t in _split(v, terms)]
    return functools.reduce(lambda a, b: a + b, parts)


def _ssd_pre(xt_ref, xc_ref, cw_ref, cb_ref, dtr_ref, sp_ref, n):
    cur = xc_ref[...]
    tail = jnp.where(n > 0, xt_ref[...], 0.0)
    sh = _conv_shifts(tail, cur)
    u = cb_ref[...] + cw_ref[CONVK - 1:CONVK, :] * sh[0]
    for j in range(1, CONVK):
        u = u + cw_ref[CONVK - 1 - j:CONVK - j, :] * sh[j]
    dt = _softplus(dtr_ref[...] + sp_ref[0:1, :])
    acs = _cumsum_rows(dt * -jnp.exp(sp_ref[1:2, :]), False)
    return u, dt, acs


def _gated_norm_fwd(y, z, sgz, nw):
    yz = y * (z * sgz)
    parts = []
    for g in range(2):
        t = yz[:, g * 256:(g + 1) * 256]
        parts.append(t * lax.rsqrt(jnp.mean(t * t, axis=-1, keepdims=True) + EPS))
    return jnp.concatenate(parts, axis=1) * nw


HPG = 4
GW = HPG * HD


class _SsdChunk:
    def __init__(self, xc, dt, acs, spv, e64, e128, decay=None):
        self.e64, self.e128 = e64, e128
        alast = acs[LB - 1:LB, :]
        self.e_all = jnp.exp(acs)
        self.dte_all = jnp.exp(alast - acs)
        self.elast = jnp.exp(alast)
        wide = _dot_sel(jnp.concatenate([dt, self.e_all, self.dte_all], axis=0), e64, 2)
        self.dt_x, self.e_x, self.dte_x = wide[0:LB], wide[LB:2 * LB], wide[2 * LB:3 * LB]
        self.dsk_x = _dot_sel(spv, e64, 3)[2:3, :]
        if decay is None:
            acs_t = jnp.transpose(acs)
            ac_x = _dot_sel(acs, e128, 3)
            row = lax.broadcasted_iota(jnp.int32, (HPG * LB, LB), 0)
            col = lax.broadcasted_iota(jnp.int32, (HPG * LB, LB), 1)
            causal = (row % LB) >= col
        lane = lax.broadcasted_iota(jnp.int32, (LB, GW), 1)
        self.head_lanes = [(lane >= hh * HD) & (lane < (hh + 1) * HD) for hh in range(HPG)]
        self.xs, self.xdt, self.b, self.c, self.bb, self.cb16, self.cbm, self.dm_st, self.m_st = ([] for _ in range(9))
        for g in range(2):
            heads = range(HPG * g, HPG * (g + 1))
            if decay is None:
                ac_st = jnp.concatenate([ac_x[:, j * LB:(j + 1) * LB] for j in heads], axis=0)
                ar_st = jnp.concatenate([jnp.broadcast_to(acs_t[j:j + 1, :], (LB, LB)) for j in heads], axis=0)
                dm_st = jnp.exp(jnp.where(causal, ac_st - ar_st, NEG))
            else:
                dm_st = decay[g]
            bg = xc[:, SW + g * NST:SW + (g + 1) * NST]
            cg = xc[:, SW + 2 * NST + g * NST:SW + 2 * NST + (g + 1) * NST]
            bgb, cgb = bg.astype(BF16), cg.astype(BF16)
            cbm = _dot_nt(cgb, bgb)
            xs_g = xc[:, g * GW:(g + 1) * GW]
            self.xs.append(xs_g)
            self.xdt.append(xs_g * self.dt_x[:, g * GW:(g + 1) * GW])
            self.b.append(bg)
            self.c.append(cg)
            self.bb.append(bgb)
            self.cb16.append(cgb)
            self.cbm.append(cbm)
            self.dm_st.append(dm_st)
            self.m_st.append(jnp.concatenate([cbm] * HPG, axis=0) * dm_st)

    def elast_rows(self, g):
        return jnp.concatenate([jnp.broadcast_to(self.elast[:, j:j + 1], (HD, NST))
                                for j in range(HPG * g, HPG * (g + 1))], axis=0)

    def diag_blocks(self, stacked):
        out = stacked[(HPG - 1) * LB:HPG * LB, :]
        for hh in range(HPG - 2, -1, -1):
            out = jnp.where(self.head_lanes[hh], stacked[hh * LB:(hh + 1) * LB, :], out)
        return out

    def block_diag(self, v):
        return jnp.concatenate([jnp.where(self.head_lanes[hh], v, 0.0) for hh in range(HPG)], axis=0)


def _ssd_fwd_block(n, xt_ref, xc_ref, cw_ref, cb_ref, dtr_ref, sp_ref, z_ref, nw_ref, e64_ref, e128_ref,
                   yn_ref, y_ref, hs_ref, u_ref, dt_ref, acs_ref, dm_ref, h_scr):
    @pl.when(n == 0)
    def _():
        h_scr[...] = jnp.zeros_like(h_scr)

    h_all = h_scr[...]
    hs_ref[0] = h_all
    u, dt, acs = _ssd_pre(xt_ref, xc_ref, cw_ref, cb_ref, dtr_ref, sp_ref, n)
    u_ref[...] = u
    dt_ref[...] = dt
    acs_ref[...] = acs
    xc = u * _sigmoid(u)
    ck = _SsdChunk(xc, dt, acs, sp_ref[...], e64_ref[...], e128_ref[...])
    dm_ref[0] = jnp.concatenate(ck.dm_st, axis=0)
    ys, hn = [], []
    for g in range(2):
        gl = slice(g * GW, (g + 1) * GW)
        xdt = ck.xdt[g]
        hg = h_all[gl, :]
        y_diag = ck.diag_blocks(_dot(ck.m_st[g].astype(BF16), xdt.astype(BF16)))
        y_off = ck.e_x[:, gl] * _dot_nt(ck.cb16[g], hg.astype(BF16))
        ys.append(y_diag + y_off + ck.xs[g] * ck.dsk_x[:, gl])
        hn.append(hg * ck.elast_rows(g) + _dot_tn((xdt * ck.dte_x[:, gl]).astype(BF16), ck.bb[g]))
    h_scr[...] = jnp.concatenate(hn, axis=0)
    y = jnp.concatenate(ys, axis=1)
    y_ref[...] = y
    z = z_ref[...]
    yn_ref[...] = _gated_norm_fwd(y, z, _sigmoid(z), nw_ref[...]).astype(BF16)


def _mixer_fwd(q, k, v, sinks8, xbc, conv_w8, conv_b, dtr, ssm_p, z, nw, gathers):
    s = q.shape[0]
    nb = s // LB
    bps = _blocks_per_step(nb)
    nsteps = nb // bps
    tl = bps * LB
    cur = lambda n: (n, 0)
    prev = lambda n: (jnp.maximum(n * bps - 1, 0), 0)
    items, ex_shapes, n_g = _exchange_items(gathers, [])
    ne = len(items)

    n_in, n_out = 16, 10
    relay_step = (3 * (nsteps - 1)) // 4
    stack = pl.BlockSpec((bps, NH * LB, LB), lambda n: (n, 0, 0))
    e64, e128 = _head_expanders()

    def body(*refs):
        (q_ref, kp_ref, kc_ref, vp_ref, vc_ref, sink_ref, xt_ref, xc_ref, cw_ref, cb_ref, dtr_ref, sp_ref, z_ref,
         nw_ref, e64_ref, e128_ref) = refs[:n_in]
        ex_in = refs[n_in:n_in + ne]
        (o_ref, yn_ref, y_ref, hs_ref, u_ref, dt_ref, acs_ref, dm_ref, pr_ref,
         ps_ref) = refs[n_in + ne:n_in + n_out + ne]
        ex_out = refs[n_in + n_out + ne:n_in + n_out + 2 * ne]
        h_scr = refs[n_in + n_out + 2 * ne]
        sems = refs[n_in + n_out + 1 + 2 * ne:]
        n = pl.program_id(0)

        @pl.when(n == 0)
        def _():
            _Exchange(n_g, ex_in, ex_out, sems).two_level_start()

        for sub in range(bps):
            blk = n * bps + sub
            r = slice(sub * LB, (sub + 1) * LB)
            before = slice((sub - 1) * LB, sub * LB)
            one = slice(sub, sub + 1)
            _attn_fwd_block(blk, q_ref.at[r], kp_ref if sub == 0 else kc_ref.at[before], kc_ref.at[r],
                            vp_ref if sub == 0 else vc_ref.at[before], vc_ref.at[r], sink_ref,
                            o_ref.at[r], pr_ref.at[one], ps_ref.at[one])
            _ssd_fwd_block(blk, xt_ref if sub == 0 else xc_ref.at[sub * LB - SUB:sub * LB], xc_ref.at[r], cw_ref,
                           cb_ref, dtr_ref.at[r], sp_ref, z_ref.at[r], nw_ref, e64_ref, e128_ref,
                           yn_ref.at[r], y_ref.at[r], hs_ref.at[one], u_ref.at[r], dt_ref.at[r], acs_ref.at[r],
                           dm_ref.at[one], h_scr)

        @pl.when(n == relay_step)
        def _():
            _Exchange(n_g, ex_in, ex_out, sems).two_level_relay()

        @pl.when(n == nsteps - 1)
        def _():
            _Exchange(n_g, ex_in, ex_out, sems).two_level_finish()

    any_spec = pl.BlockSpec(memory_space=pl.ANY)
    tail = pl.BlockSpec((SUB, D), lambda n: (jnp.maximum(n * (tl // SUB) - 1, 0), 0))
    outs = _pcall(
        body, name="mixer_fwd", grid=(nsteps,),
        in_specs=[pl.BlockSpec((tl, AW), cur), pl.BlockSpec((LB, KVW), prev), pl.BlockSpec((tl, KVW), cur),
                  pl.BlockSpec((LB, KVW), prev), pl.BlockSpec((tl, KVW), cur), _const((8, LB)),
                  tail, pl.BlockSpec((tl, D), cur), _const((8, D)), _const((1, D)),
                  pl.BlockSpec((tl, LB), cur), _const((8, LB)), pl.BlockSpec((tl, SW), cur), _const((1, SW)),
                  _const(e64.shape), _const(e128.shape)]
        + [any_spec] * ne,
        out_specs=[pl.BlockSpec((tl, AW), cur), pl.BlockSpec((tl, SW), cur), pl.BlockSpec((tl, SW), cur),
                   pl.BlockSpec((bps, NH * HD, NST), lambda n: (n, 0, 0)), pl.BlockSpec((tl, D), cur),
                   pl.BlockSpec((tl, LB), cur), pl.BlockSpec((tl, LB), cur),
                   stack, stack, stack] + [any_spec] * ne,
        out_shape=[_sds((s, AW), BF16), _sds((s, SW), BF16), _sds((s, SW), F32), _sds((nb, NH * HD, NST), F32),
                   _sds((s, D), F32), _sds((s, LB), F32), _sds((s, LB), F32), _sds((nb, NH * LB, LB), F32),
                   _sds((nb, NH * LB, LB), BF16), _sds((nb, NH * LB, LB), BF16)]
        + ex_shapes,
        scratch_shapes=[pltpu.VMEM((NH * HD, NST), F32)] + _exchange_sems(ne),
        compiler_params=_params(),
    )(q, k, k, v, v, sinks8, xbc, xbc, conv_w8, conv_b, dtr, ssm_p, z, nw, e64, e128, *items)
    return outs[:n_out], outs[n_out:]


def _blocks_per_step(nb):
    return next(b for b in (4, 2, 1) if nb % b == 0)


def _head_expanders():
    j = lax.broadcasted_iota(jnp.int32, (LB, NH * HD), 0)
    e64 = (lax.broadcasted_iota(jnp.int32, (LB, NH * HD), 1) // HD == j).astype(BF16)
    j = lax.broadcasted_iota(jnp.int32, (LB, NH * LB), 0)
    e128 = (lax.broadcasted_iota(jnp.int32, (LB, NH * LB), 1) // LB == j).astype(BF16)
    return e64, e128


def _outproj_ffn_fwd_loss(attn, yn, x, tgt, mod8, n2w, fnw, w_out, w_gu_t, w_down):
    s = x.shape[0]
    tf = min(256, s)

    def body(a_ref, y_ref, x_ref, t_ref, mod_ref, nw_ref, fw_ref, wo_ref, wgu_ref, wd_ref,
             x2_ref, h2_ref, mo_ref, mix_ref, gu_ref, act_ref, dx3_ref, sm_ref):
        i = pl.program_id(0)

        @pl.when(i == 0)
        def _():
            sm_ref[...] = jnp.zeros_like(sm_ref)

        mix = jnp.concatenate([a_ref[...], y_ref[...]], axis=1)
        mix_ref[...] = mix
        mo = _dot(mix, wo_ref[...])
        mo_ref[...] = mo.astype(BF16)
        x2 = x_ref[...] + mod_ref[2:3, :] * mo
        x2_ref[...] = x2
        h2 = _norm_mod_fwd(x2, nw_ref[...], mod_ref[3:4, :], mod_ref[4:5, :]).astype(BF16)
        h2_ref[...] = h2
        gu = _dot_nt(h2, wgu_ref[...])
        gu_ref[...] = gu.astype(BF16)
        g = gu[:, :DFF]
        act = (g * _sigmoid(g) * gu[:, DFF:]).astype(BF16)
        act_ref[...] = act
        ff = _dot(act, wd_ref[...])
        x3 = x2 + mod_ref[5:6, :] * ff
        r = lax.rsqrt(jnp.mean(x3 * x3, axis=-1, keepdims=True) + EPS)
        xh = x3 * r
        fw = fw_ref[...]
        err = xh * fw - t_ref[...]
        dy = err * (1.0 / D)
        dxh = dy * fw
        dx3 = r * (dxh - xh * jnp.mean(dxh * xh, axis=-1, keepdims=True))
        dx3_ref[...] = dx3
        sm_ref[0:1, :] += jnp.sum(dx3 * ff, axis=0, keepdims=True)
        sm_ref[1:2, :] += jnp.sum(dy * xh, axis=0, keepdims=True)
        sm_ref[2:3, :] += jnp.sum(err * err, axis=0, keepdims=True)

    return _pcall(
        body, name="outproj_ffn_fwd_loss", grid=(s // tf,),
        in_specs=[_rows(tf, AW), _rows(tf, SW), _rows(tf, D), _rows(tf, D), _const((8, D)), _const((1, D)),
                  _const((1, D)), _const((D, D)), _const((2 * DFF, D)), _const((DFF, D))],
        out_specs=[_rows(tf, D), _rows(tf, D), _rows(tf, D), _rows(tf, D), _rows(tf, 2 * DFF), _rows(tf, DFF),
                   _rows(tf, D), pl.BlockSpec((8, D), lambda i: (0, 0))],
        out_shape=[_sds((s, D), F32), _sds((s, D), BF16), _sds((s, D), BF16), _sds((s, D), BF16),
                   _sds((s, 2 * DFF), BF16), _sds((s, DFF), BF16), _sds((s, D), F32), _sds((8, D), F32)],
        compiler_params=_params(),
    )(attn, yn, x, tgt, mod8, n2w, fnw, w_out, w_gu_t, w_down)


def _ffn_bwd(dx3, gu, x2, mixout, mod8, n2w, w_gu, w_down, w_out):
    s = x2.shape[0]
    tb = min(256, s)

    def body(dx3_ref, gu_ref, x2_ref, mo_ref, mod_ref, nw_ref, wgu_ref, wd_ref, wo_ref,
             dx2_ref, dff_ref, dgu_ref, dmix_ref, dattn_ref, dyn_ref, sm_ref):
        i = pl.program_id(0)

        @pl.when(i == 0)
        def _():
            sm_ref[...] = jnp.zeros_like(sm_ref)

        dx3 = dx3_ref[...]
        dff = (dx3 * mod_ref[5:6, :]).astype(BF16)
        dff_ref[...] = dff
        dact = _dot_nt(dff, wd_ref[...])
        g = gu_ref[:, :DFF].astype(F32)
        u = gu_ref[:, DFF:].astype(F32)
        sg = _sigmoid(g)
        dgu = jnp.concatenate([dact * u * sg * (1.0 + g * (1.0 - sg)), dact * g * sg], axis=1).astype(BF16)
        dgu_ref[...] = dgu
        dh2 = _dot(dgu, wgu_ref[...])
        dxn, d_shift, d_scale, d_w = _norm_mod_bwd(x2_ref[...], dh2, nw_ref[...], mod_ref[4:5, :])
        dx2 = dx3 + dxn
        dx2_ref[...] = dx2
        sm_ref[0:1, :] += d_shift
        sm_ref[1:2, :] += d_scale
        sm_ref[2:3, :] += d_w
        sm_ref[3:4, :] += jnp.sum(dx2 * mo_ref[...].astype(F32), axis=0, keepdims=True)
        dmix = (dx2 * mod_ref[2:3, :]).astype(BF16)
        dmix_ref[...] = dmix
        dmi = _dot_nt(dmix, wo_ref[...])
        dattn_ref[...] = dmi[:, :AW].astype(BF16)
        dyn_ref[...] = dmi[:, AW:]

    return _pcall(
        body, name="ffn_bwd", grid=(s // tb,),
        in_specs=[_rows(tb, D), _rows(tb, 2 * DFF), _rows(tb, D), _rows(tb, D), _const((8, D)), _const((1, D)),
                  _const((2 * DFF, D)), _const((DFF, D)), _const((D, D))],
        out_specs=[_rows(tb, D), _rows(tb, D), _rows(tb, 2 * DFF), _rows(tb, D), _rows(tb, AW), _rows(tb, SW),
                   pl.BlockSpec((8, D), lambda i: (0, 0))],
        out_shape=[_sds((s, D), F32), _sds((s, D), BF16), _sds((s, 2 * DFF), BF16), _sds((s, D), BF16),
                   _sds((s, AW), BF16), _sds((s, SW), F32), _sds((8, D), F32)],
        compiler_params=_params(),
    )(dx3, gu, x2, mixout, mod8, n2w, w_gu, w_down, w_out)


def _ssd_bwd_block(i, n, *refs):
    def run(dyn_ref, y_ref, z_ref, x_ref, u_ref, dt_ref, acs_ref, dm_ref, cw_ref, sp_ref, nw_ref,
            hs_ref, e64_ref, e128_ref, dzxd_ref, sm_ref, dh_scr, dun_scr):
        @pl.when(i == 0)
        def _():
            dh_scr[...] = jnp.zeros_like(dh_scr)
            dun_scr[...] = jnp.zeros_like(dun_scr)
            sm_ref[...] = jnp.zeros_like(sm_ref)

        u, dt, acs = u_ref[...], dt_ref[...], acs_ref[...]
        sg_u = _sigmoid(u)
        xc = u * sg_u
        a_neg = -jnp.exp(sp_ref[1:2, :])
        ck = _SsdChunk(xc, dt, acs, sp_ref[...], e64_ref[...], e128_ref[...],
                       decay=[dm_ref[0, g * HPG * LB:(g + 1) * HPG * LB, :] for g in range(2)])
        h_all = hs_ref[0]
        dh_all = dh_scr[...]
        riota = lax.broadcasted_iota(jnp.int32, (LB, LB), 0)
        lane1 = lax.broadcasted_iota(jnp.int32, (1, LB), 1)

        z = z_ref[...]
        y = y_ref[...]
        sgz = _sigmoid(z)
        sz = z * sgz
        yz = y * sz
        nwv = nw_ref[...]
        dyn_v = dyn_ref[...]
        dyhat = dyn_v * nwv
        yhat_parts, dyz_parts = [], []
        for g in range(2):
            gs = slice(g * 256, (g + 1) * 256)
            t = yz[:, gs]
            rg = lax.rsqrt(jnp.mean(t * t, axis=-1, keepdims=True) + EPS)
            yh = t * rg
            dyh = dyhat[:, gs]
            yhat_parts.append(yh)
            dyz_parts.append(rg * (dyh - yh * jnp.mean(dyh * yh, axis=-1, keepdims=True)))
        yhat = jnp.concatenate(yhat_parts, axis=1)
        dyz = jnp.concatenate(dyz_parts, axis=1)
        sm_ref[5:6, 0:SW] += jnp.sum(dyn_v * yhat, axis=0, keepdims=True)
        dy = dyz * sz
        dzxd_ref[:, 0:SW] = (dyz * y * sgz * (1.0 + z * (1.0 - sgz))).astype(BF16)

        cat = lambda parts: jnp.concatenate(parts, axis=1)
        dxs, dbs, dcs, dhp, g_cat, de_x, ddte_x, ddt_x, ddsk_x = ([] for _ in range(9))
        dacs_t = jnp.zeros((LB, LB), F32)
        hsum = jnp.zeros((1, LB), F32)
        for g in range(2):
            gl = slice(g * GW, (g + 1) * GW)
            xs_g, xdt, bgb, cgb = ck.xs[g], ck.xdt[g], ck.bb[g], ck.cb16[g]
            m_st, dm_st = ck.m_st[g], ck.dm_st[g]
            dt_x, e_x, dte_x = ck.dt_x[:, gl], ck.e_x[:, gl], ck.dte_x[:, gl]
            xdtb = xdt.astype(BF16)
            hg, dhn = h_all[gl, :], dh_all[gl, :]
            hb, dhnb = hg.astype(BF16), dhn.astype(BF16)
            dy_g = dy[:, gl]
            ddsk_x.append(jnp.sum(dy_g * xs_g, axis=0, keepdims=True))
            dy_bd = ck.block_diag(dy_g).astype(BF16)
            dm4 = _dot_nt(dy_bd, xdtb)
            dxdt = _dot_tn(m_st.astype(BF16), dy_bd)
            gmat = dm4 * m_st
            dcbm = dm4 * dm_st
            dcb = dcbm[0:LB] + dcbm[LB:2 * LB] + dcbm[2 * LB:3 * LB] + dcbm[3 * LB:4 * LB]
            g_cat.append(cat([gmat[hh * LB:(hh + 1) * LB, :] for hh in range(HPG)]))
            for hh in range(HPG):
                j = HPG * g + hh
                col_sum = jnp.sum(gmat[hh * LB:(hh + 1) * LB, :], axis=0, keepdims=True)
                dacs_t = dacs_t - jnp.where(riota == j, col_sum, 0.0)
                hsl = slice(hh * HD, (hh + 1) * HD)
                hsum = hsum + jnp.where(lane1 == j, jnp.sum(dhn[hsl, :] * hg[hsl, :]), 0.0)
            dchb = (dy_g * e_x).astype(BF16)
            dcg = _dot(dchb, hb)
            dh_prev = _dot_tn(dchb, cgb)
            de_x.append(dy_g * _dot_nt(cgb, hb))
            dxs_s = _dot_nt(bgb, dhnb)
            dbg = _dot((xdt * dte_x).astype(BF16), dhnb)
            dxdt = dxdt + dxs_s * dte_x
            ddte_x.append(dxs_s * xdt)
            dhp.append(dhn * ck.elast_rows(g) + dh_prev)
            dxs.append(dy_g * ck.dsk_x[:, gl] + dxdt * dt_x)
            ddt_x.append(dxdt * xs_g)
            dcbb = dcb.astype(BF16)
            dbs.append(dbg + _dot_tn(dcbb, cgb))
            dcs.append(dcg + _dot(dcbb, bgb))
        dh_scr[...] = jnp.concatenate(dhp, axis=0)
        red = _dot_nt_sel(jnp.concatenate([cat(de_x), cat(ddte_x), cat(ddt_x)], axis=0), ck.e64, 1)
        de_c, ddte_c, ddt_c = red[0:LB], red[LB:2 * LB], red[2 * LB:3 * LB]
        ddsk = _dot_nt_sel(jnp.broadcast_to(cat(ddsk_x), (SUB, NH * HD)), ck.e64, 2)[0:1, :]
        t1 = ddte_c * ck.dte_all
        dalast = jnp.sum(t1, axis=0, keepdims=True) + hsum * ck.elast
        dacs = (_dot_nt_sel(cat(g_cat), ck.e128, 2) + de_c * ck.e_all - t1 + jnp.transpose(dacs_t)
                + jnp.where(riota == LB - 1, dalast, 0.0))
        da = _cumsum_rows(dacs, True)
        ddt = ddt_c + da * a_neg
        da_log = jnp.sum(da * dt, axis=0, keepdims=True) * a_neg
        ddtr = ddt * (1.0 - jnp.exp(-dt))
        dzxd_ref[:, SW + D:ZXD] = ddtr.astype(BF16)
        sm_ref[6:7, 0:LB] += jnp.sum(ddtr, axis=0, keepdims=True)
        sm_ref[6:7, LB:2 * LB] += da_log
        sm_ref[6:7, 2 * LB:3 * LB] += ddsk

        du = cat(dxs + dbs + dcs) * (sg_u * (1.0 + u * (1.0 - sg_u)))
        xv = x_ref[...]
        adv = [du] + _conv_advances(du, dun_scr[...])
        sm_ref[0:1, :] += jnp.sum(du, axis=0, keepdims=True)
        dxbc = cw_ref[CONVK - 1:CONVK, :] * du
        for j in range(CONVK):
            sm_ref[CONVK - j:CONVK + 1 - j, :] += jnp.sum(adv[j] * xv, axis=0, keepdims=True)
            if j:
                dxbc = dxbc + cw_ref[CONVK - 1 - j:CONVK - j, :] * adv[j]
        dun_scr[...] = du[0:SUB, :]
        dzxd_ref[:, SW:SW + D] = dxbc.astype(BF16)

    run(*refs)


def _attn_bwd_block(i, n, q_ref, kp_ref, kc_ref, vp_ref, vc_ref, o_ref, do_ref, cos_ref, sin_ref, pr_ref, ps_ref,
                    dq_ref, dkv_ref, ds_ref, ck_scr, cv_scr):
    @pl.when(i == 0)
    def _():
        ds_ref[...] = jnp.zeros_like(ds_ref)
        ck_scr[...] = jnp.zeros_like(ck_scr)
        cv_scr[...] = jnp.zeros_like(cv_scr)

    qv, ov, dov = q_ref[...], o_ref[...], do_ref[...]
    kcat = jnp.concatenate([kp_ref[...], kc_ref[...]], axis=0)
    vcat = jnp.concatenate([vp_ref[...], vc_ref[...]], axis=0)
    upper = _upper_mask()
    srow = lax.broadcasted_iota(jnp.int32, (8, LB), 0)
    slane = lax.broadcasted_iota(jnp.int32, (8, LB), 1)
    dsink = jnp.zeros((8, LB), F32)
    dq_g, dk_g, dv_g = [], [], []
    for g in range(NQ // QPG):
        sl = slice(g * HD, (g + 1) * HD)
        qg = _stack_heads(qv, g)
        dog = _stack_heads(dov, g)
        rows = slice(g * QPG * LB, (g + 1) * QPG * LB)
        probs = pr_ref[0, rows, :].astype(F32)
        psink = ps_ref[0, rows, :].astype(F32)
        delta = _row_sums_wide(dog.astype(F32) * _stack_heads(ov, g).astype(F32), 2)
        dsc = probs * (_band(upper, _dot_nt(dog, vcat[0:LB, sl]), _dot_nt(dog, vcat[LB:2 * LB, sl])) - delta)
        sink_terms = (psink * delta)[:, 0:1]
        for hh in range(QPG):
            dsink = dsink - jnp.where((srow == QPG * g + hh) & (slane == 0),
                                      jnp.sum(sink_terms[hh * LB:(hh + 1) * LB, :]), 0.0)
        ds_p = jnp.where(upper, dsc, 0.0).astype(BF16)
        ds_c = jnp.where(upper, 0.0, dsc).astype(BF16)
        dq_g.append((_dot(ds_p, kcat[0:LB, sl]) + _dot(ds_c, kcat[LB:2 * LB, sl])) * ATT_SCALE)
        dk_g.append(jnp.concatenate([_dot_tn(ds_p, qg), _dot_tn(ds_c, qg)], axis=0) * ATT_SCALE)
        dv_g.append(jnp.concatenate([_dot_tn(jnp.where(upper, probs, 0.0).astype(BF16), dog),
                                     _dot_tn(jnp.where(upper, 0.0, probs).astype(BF16), dog)], axis=0))
    ds_ref[...] += dsink
    cs = cos_ref[...]
    sn = sin_ref[...]
    dk2 = jnp.concatenate(dk_g, axis=1)
    dv2 = jnp.concatenate(dv_g, axis=1)
    for a, tile in enumerate(_unstack_heads(dq_g)):
        dq_ref[:, a * LB:(a + 1) * LB] = _rope(tile, cs, sn, True).astype(BF16)
    dkv_ref[:, 0:KVW] = _rope(ck_scr[...] + dk2[LB:2 * LB, :], cs, sn, True).astype(BF16)
    dkv_ref[:, KVW:2 * KVW] = (cv_scr[...] + dv2[LB:2 * LB, :]).astype(BF16)
    ck_scr[...] = dk2[0:LB, :]
    cv_scr[...] = dv2[0:LB, :]


def _mixer_bwd(q, k, v, o, do, cos, sin, probs, psink, dyn, y, z, xbc, u, dtv, acs, decay, conv_w8, ssm_p, nw, hs,
               scatters):
    s = q.shape[0]
    nb = s // LB
    bps = _blocks_per_step(nb)
    nsteps = nb // bps
    tl = bps * LB
    cur = lambda i: (nsteps - 1 - i, 0)
    prev = lambda i: (jnp.maximum((nsteps - 1 - i) * bps - 1, 0), 0)
    n_in = 25
    items, ex_shapes, n_g = _exchange_items([], scatters)
    ne = len(items)
    e64, e128 = _head_expanders()
    stack = pl.BlockSpec((bps, NH * LB, LB), lambda i: (nsteps - 1 - i, 0, 0))

    def body(*refs):
        i = pl.program_id(0)
        (q_ref, kp_ref, kc_ref, vp_ref, vc_ref, o_ref, do_ref, cos_ref, sin_ref, pr_ref, ps_ref,
         dyn_ref, y_ref, z_ref, x_ref, u_ref, dt_ref, acs_ref, dm_ref, cw_ref, sp_ref, nw_ref,
         hs_ref, e64_ref, e128_ref) = refs[:n_in]
        ex_in = refs[n_in:n_in + ne]
        dp_ref, ds_ref, sm_ref = refs[n_in + ne:n_in + ne + 3]
        ex_out = refs[n_in + ne + 3:n_in + 2 * ne + 3]
        ck_scr, cv_scr, dh_scr, dun_scr = refs[n_in + 2 * ne + 3:n_in + 2 * ne + 7]
        sems = refs[n_in + 2 * ne + 7:]

        @pl.when(i == 0)
        def _():
            _Exchange(n_g, ex_in, ex_out, sems).start()

        for back in range(bps):
            sub = bps - 1 - back
            step = i * bps + back
            blk = (nsteps - 1 - i) * bps + sub
            r = slice(sub * LB, (sub + 1) * LB)
            before = slice((sub - 1) * LB, sub * LB)
            one = slice(sub, sub + 1)
            _attn_bwd_block(step, blk, q_ref.at[r], kp_ref if sub == 0 else kc_ref.at[before], kc_ref.at[r],
                            vp_ref if sub == 0 else vc_ref.at[before], vc_ref.at[r], o_ref.at[r], do_ref.at[r],
                            cos_ref.at[r], sin_ref.at[r], pr_ref.at[one], ps_ref.at[one],
                            dp_ref.at[r, O_Q:O_K], dp_ref.at[r, O_K:O_Z], ds_ref, ck_scr, cv_scr)
            _ssd_bwd_block(step, blk, dyn_ref.at[r], y_ref.at[r], z_ref.at[r], x_ref.at[r], u_ref.at[r], dt_ref.at[r],
                           acs_ref.at[r], dm_ref.at[one], cw_ref, sp_ref, nw_ref,
                           hs_ref.at[one], e64_ref, e128_ref, dp_ref.at[r, O_Z:INP], sm_ref, dh_scr, dun_scr)

        @pl.when(i == nsteps - 1)
        def _():
            _Exchange(n_g, ex_in, ex_out, sems).finish()

    any_spec = pl.BlockSpec(memory_space=pl.ANY)
    outs = _pcall(
        body, name="mixer_bwd", grid=(nsteps,),
        in_specs=[pl.BlockSpec((tl, AW), cur), pl.BlockSpec((LB, KVW), prev), pl.BlockSpec((tl, KVW), cur),
                  pl.BlockSpec((LB, KVW), prev), pl.BlockSpec((tl, KVW), cur), pl.BlockSpec((tl, AW), cur),
                  pl.BlockSpec((tl, AW), cur), pl.BlockSpec((tl, LB), cur), pl.BlockSpec((tl, LB), cur),
                  stack, stack,
                  pl.BlockSpec((tl, SW), cur), pl.BlockSpec((tl, SW), cur), pl.BlockSpec((tl, SW), cur),
                  pl.BlockSpec((tl, D), cur), pl.BlockSpec((tl, D), cur), pl.BlockSpec((tl, LB), cur),
                  pl.BlockSpec((tl, LB), cur), stack,
                  _const((8, D)), _const((8, LB)), _const((1, SW)),
                  pl.BlockSpec((bps, NH * HD, NST), lambda i: (nsteps - 1 - i, 0, 0)),
                  _const(e64.shape), _const(e128.shape)] + [any_spec] * ne,
        out_specs=[pl.BlockSpec((tl, INP), cur), pl.BlockSpec((8, LB), lambda i: (0, 0)),
                   pl.BlockSpec((8, D), lambda i: (0, 0))] + [any_spec] * ne,
        out_shape=[_sds((s, INP), BF16), _sds((8, LB), F32), _sds((8, D), F32)] + ex_shapes,
        scratch_shapes=[pltpu.VMEM((LB, KVW), F32), pltpu.VMEM((LB, KVW), F32),
                        pltpu.VMEM((NH * HD, NST), F32), pltpu.VMEM((SUB, D), F32)]
        + _exchange_sems(ne),
        compiler_params=_params(),
    )(q, k, k, v, v, o, do, cos, sin, probs, psink, dyn, y, z, xbc, u, dtv, acs, decay, conv_w8, ssm_p, nw, hs,
      e64, e128, *items)
    return outs[0], outs[1], outs[2], outs[3:]


def _inproj_bwd(dproj, x, dx2, mod8, n1w, w_in_t, scatters, smalls):
    s = x.shape[0]
    tt = min(512, s)
    nt = s // tt
    items, ex_shapes, n_g = _exchange_items([], scatters)
    ne = len(items)
    n_in = 10

    def body(*refs):
        dp_ref, x_ref, dx2_ref, mod_ref, nw_ref, w_ref, f_ref, b_ref, s_ref, k_ref = refs[:n_in]
        ex_in = refs[n_in:n_in + ne]
        gx_ref, sm_ref = refs[n_in + ne:n_in + 2 + ne]
        ex_out = refs[n_in + 2 + ne:n_in + 2 + 2 * ne]
        gpack_ref = refs[n_in + 2 + 2 * ne]
        pack_scr = refs[n_in + 3 + 2 * ne]
        sems = refs[n_in + 4 + 2 * ne:n_in + 7 + 2 * ne]
        pack_sems = refs[n_in + 7 + 2 * ne:]
        i = pl.program_id(0)

        @pl.when(i == 0)
        def _():
            sm_ref[...] = jnp.zeros_like(sm_ref)
            _Exchange(n_g, ex_in, ex_out, sems).start()

        w = w_ref[...]
        hr = tt // 2
        dh1 = [_dot(dp_ref[h * hr:(h + 1) * hr, :], w) for h in range(2)]
        sums = jnp.zeros((3, D), F32)
        for h in range(2):
            rows = slice(h * hr, (h + 1) * hr)
            dxn, d_shift, d_scale, d_w = _norm_mod_bwd(x_ref[rows, :], dh1[h], nw_ref[...], mod_ref[1:2, :])
            gx_ref[rows, :] = dx2_ref[rows, :] + dxn
            sums = sums + jnp.concatenate([d_shift, d_scale, d_w], axis=0)
        sm_ref[0:3, :] += sums

        @pl.when(i == nt - 1)
        def _():
            _pack_rows(f_ref, b_ref, s_ref, sm_ref, k_ref, pack_scr)
            small = _Exchange(1, [pack_scr], [gpack_ref], pack_sems)
            small.start()
            _Exchange(n_g, ex_in, ex_out, sems).finish()
            small.finish()

    any_spec = pl.BlockSpec(memory_space=pl.ANY)
    outs = _pcall(
        body, name="inproj_bwd", grid=(nt,),
        in_specs=[_rows(tt, INP), _rows(tt, D), _rows(tt, D), _const((8, D)), _const((1, D)), _const((INP, D)),
                  _const((8, D)), _const((8, D)), _const((8, D)), _const((8, LB))]
        + [any_spec] * ne,
        out_specs=[_rows(tt, D), pl.BlockSpec((8, D), lambda i: (0, 0))] + [any_spec] * (ne + 1),
        out_shape=[_sds((s, D), F32), _sds((8, D), F32)] + ex_shapes + [_sds((N_DEV, PACK_ROWS, D), F32)],
        scratch_shapes=[pltpu.VMEM((PACK_ROWS, D), F32)] + _exchange_sems(ne) + _exchange_sems(1),
        compiler_params=_params(),
    )(dproj, x, dx2, mod8, n1w, w_in_t, *smalls, *items)
    return outs[0], outs[2:2 + ne], outs[2 + ne]


def _wgrad(a, b, name):
    s, m = a.shape
    n = b.shape[1]
    tk = min(2048, s)
    wide = (1408, 1024, 512)
    tm = next((t for t in wide if m % t == 0), m)
    tn = n if n <= 2048 else _largest_divisor(n, wide)
    nk = s // tk

    def body(a_ref, b_ref, o_ref, acc):
        kk = pl.program_id(2)

        @pl.when(kk == 0)
        def _():
            acc[...] = jnp.zeros_like(acc)

        acc[...] += _dot_tn(a_ref[...], b_ref[...])

        @pl.when(kk == nk - 1)
        def _():
            o_ref[...] = acc[...].astype(BF16)

    return _pcall(
        body, name=name, grid=(m // tm, n // tn, nk),
        in_specs=[pl.BlockSpec((tk, tm), lambda i, j, kk: (kk, i)), pl.BlockSpec((tk, tn), lambda i, j, kk: (kk, j))],
        out_specs=pl.BlockSpec((tm, tn), lambda i, j, kk: (i, j)),
        out_shape=_sds((m, n), BF16),
        scratch_shapes=[pltpu.VMEM((tm, tn), F32)],
        compiler_params=_params(3),
    )(a, b)


PACK_ROWS = 24


def _pack_rows(f_ref, b_ref, s_ref, i_ref, k_ref, o_ref):
    o_ref[...] = jnp.zeros_like(o_ref)
    o_ref[0:2, :] = i_ref[0:2, :]
    o_ref[2:3, :] = b_ref[3:4, :]
    o_ref[3:5, :] = b_ref[0:2, :]
    o_ref[5:6, :] = f_ref[0:1, :]
    o_ref[6:7, :] = i_ref[2:3, :]
    o_ref[7:8, :] = b_ref[2:3, :]
    o_ref[8:9, :] = f_ref[1:2, :]
    o_ref[9:14, :] = s_ref[0:5, :]
    o_ref[14:15, :] = s_ref[5:6, :]
    o_ref[15:16, 0:3 * LB] = s_ref[6:7, 0:3 * LB]
    lane = lax.broadcasted_iota(jnp.int32, (1, LB), 1)
    sk = jnp.zeros((1, LB), F32)
    for h in range(NQ):
        sk = sk + jnp.where(lane == h, k_ref[h:h + 1, 0:1], 0.0)
    o_ref[15:16, 3 * LB:4 * LB] = sk
    o_ref[16:17, :] = f_ref[2:3, :]


def _exchange(gathers, scatters, name, two_level=False):
    items, shapes, n_g = _exchange_items(gathers, scatters)
    n = len(items)
    assert not (two_level and scatters)

    def body(*refs):
        ex = _Exchange(n_g, refs[:n], refs[n:2 * n], refs[2 * n:])
        if two_level:
            ex.gather_two_level()
        else:
            ex.start()
            ex.finish()

    any_spec = pl.BlockSpec(memory_space=pl.ANY)
    return _pcall(
        body, name=name, in_specs=[any_spec] * n, out_specs=[any_spec] * n, out_shape=shapes,
        scratch_shapes=_exchange_sems(n),
    )(*items)


def _exchange_items(gathers, scatters):
    items = list(gathers) + list(scatters)
    shapes = [_sds((N_DEV,) + a.shape, a.dtype) for a in gathers] + [_sds(a.shape, a.dtype) for a in scatters]
    return items, shapes, len(gathers)


def _exchange_sems(n):
    return [pltpu.SemaphoreType.DMA((n, N_DEV - 1)), pltpu.SemaphoreType.DMA((n, N_DEV - 1)),
            pltpu.SemaphoreType.DMA((n,))]


class _Exchange:
    def __init__(self, n_g, ins, outs, sems):
        self.n_g, self.ins, self.outs = n_g, ins, outs
        self.send_sems, self.recv_sems, self.loc_sems = sems
        xi, yi, ci = lax.axis_index("x"), lax.axis_index("y"), lax.axis_index("c")
        self.me = 4 * xi + 2 * yi + ci
        self.peers = []
        for r in range(1, N_DEV):
            px = 1 - xi if r & 4 else xi
            py = 1 - yi if r & 2 else yi
            pc = 1 - ci if r & 1 else ci
            self.peers.append(((px, py, pc), 4 * px + 2 * py + pc))

    def _copy(self, t, r, landing):
        dev, peer = self.peers[r]
        src = self.ins[t] if t < self.n_g else self.ins[t].at[peer]
        return pltpu.make_async_remote_copy(
            src_ref=src, dst_ref=self.outs[t].at[landing], send_sem=self.send_sems.at[t, r],
            recv_sem=self.recv_sems.at[t, r], device_id=dev, device_id_type=pl.DeviceIdType.MESH)

    def _local(self, t):
        src = self.ins[t] if t < self.n_g else self.ins[t].at[self.me]
        return pltpu.make_async_copy(src, self.outs[t].at[self.me], self.loc_sems.at[t])

    def start(self):
        for t in range(len(self.ins)):
            self._local(t).start()
            for r in range(N_DEV - 1):
                self._copy(t, r, self.me).start()

    def finish(self):
        n = len(self.ins)
        for t in range(n):
            for r in range(N_DEV - 1):
                self._copy(t, r, self.peers[r][1]).wait_recv()
        for t in range(n):
            for r in range(N_DEV - 1):
                self._copy(t, r, self.me).wait_send()
            self._local(t).wait()

    def gather_two_level(self):
        self.two_level_start()
        self.two_level_relay()
        self.two_level_finish()

    DIRECT = (0, 1, 3, 5)

    def two_level_start(self):
        for t in range(len(self.ins)):
            self._local(t).start()
            for r in self.DIRECT:
                self._copy(t, r, self.me).start()

    def _relay(self, t, r):
        peer = self.peers[r][1]
        return pltpu.make_async_remote_copy(
            src_ref=self.outs[t].at[peer], dst_ref=self.outs[t].at[peer], send_sem=self.send_sems.at[t, r + 1],
            recv_sem=self.recv_sems.at[t, r + 1], device_id=self.peers[0][0], device_id_type=pl.DeviceIdType.MESH)

    def two_level_relay(self):
        for t in range(len(self.ins)):
            for r in self.DIRECT[1:]:
                self._copy(t, r, self.peers[r][1]).wait_recv()
                self._relay(t, r).start()

    def two_level_finish(self):
        n = len(self.ins)
        for t in range(n):
            for r in (0, 2, 4, 6):
                self._copy(t, r, self.peers[r][1]).wait_recv()
        for t in range(n):
            for r in self.DIRECT:
                self._copy(t, r, self.me).wait_send()
            for r in self.DIRECT[1:]:
                self._relay(t, r).wait_send()
            self._local(t).wait()


def _prologue(c8, w_in_tb, cw8, w_cols, b_cols):
    ncol = w_cols.shape[1]

    def body(c_ref, win_ref, cw_ref, w_ref, b_ref, gc_ref, gin_ref, gcw_ref, gmod_ref, call_scr, mod_scr, loc_sem,
             *sems):
        big = _Exchange(2, [win_ref, cw_ref], [gin_ref, gcw_ref], sems[0:3])
        big.two_level_start()
        small = _Exchange(1, [c_ref], [gc_ref], sems[3:6])
        small.start()
        small.finish()
        landed = pltpu.make_async_copy(gc_ref, call_scr, loc_sem)
        landed.start()
        landed.wait()
        cv = call_scr[:, 0, :]
        sc = (cv * _sigmoid(cv)).astype(BF16)
        mod_scr[...] = _dot(sc, w_ref[...].astype(BF16)) + b_ref[...]
        mods = _Exchange(1, [mod_scr], [gmod_ref], sems[6:9])
        mods.start()
        mods.finish()
        big.two_level_relay()
        big.two_level_finish()

    any_spec = pl.BlockSpec(memory_space=pl.ANY)
    vmem_spec = pl.BlockSpec(memory_space=pltpu.VMEM)
    return _pcall(
        body, name="prologue", in_specs=[any_spec, any_spec, any_spec, vmem_spec, vmem_spec],
        out_specs=[any_spec] * 4,
        out_shape=[_sds((N_DEV,) + c8.shape, F32), _sds((N_DEV,) + w_in_tb.shape, BF16),
                   _sds((N_DEV,) + cw8.shape, F32), _sds((N_DEV, N_DEV, ncol), F32)],
        scratch_shapes=[pltpu.VMEM((N_DEV,) + c8.shape, F32), pltpu.VMEM((N_DEV, ncol), F32),
                        pltpu.SemaphoreType.DMA] + _exchange_sems(2) + _exchange_sems(1) + _exchange_sems(1),
        compiler_params=pltpu.CompilerParams(vmem_limit_bytes=VMEM_LIMIT),
    )(c8, w_in_tb, cw8, w_cols, b_cols)


def _adamw(w, g, m, v):
    m2 = ADAM_B1 * m + (1.0 - ADAM_B1) * g
    v2 = ADAM_B2 * v + (1.0 - ADAM_B2) * (g * g)
    m_hat = m2 / (1.0 - ADAM_B1 ** ADAM_STEP)
    v_hat = v2 / (1.0 - ADAM_B2 ** ADAM_STEP)
    delta = -ADAM_LR * (m_hat / (jnp.sqrt(v_hat) + ADAM_EPS) + ADAM_WD * w)
    return delta, m2, v2


def _sum_adamw(parts, w, m, v, name):
    rws, cols = w.shape
    tr = next((t for t in (256, 176, 128) if rws % t == 0), rws)

    def body(p_ref, w_ref, m_ref, v_ref, g_ref, d_ref, mo_ref, vo_ref):
        g = p_ref[0].astype(F32)
        for dev in range(1, N_DEV):
            g = g + p_ref[dev].astype(F32)
        g_ref[...] = g
        d_ref[...], mo_ref[...], vo_ref[...] = _adamw(w_ref[...], g, m_ref[...], v_ref[...])

    blk = pl.BlockSpec((tr, cols), lambda i: (i, 0))
    return _pcall(
        body, name=name, grid=(rws // tr,),
        in_specs=[pl.BlockSpec((N_DEV, tr, cols), lambda i: (0, i, 0)), blk, blk, blk],
        out_specs=[blk] * 4, out_shape=[_sds((rws, cols), F32)] * 4, compiler_params=_params(),
    )(parts, w, m, v)


def _wada_adamw(c_all, dmod_cols, w, m, v):
    rws, cols = w.shape
    tr = 256

    def body(c_ref, dm_ref, w_ref, m_ref, v_ref, g_ref, d_ref, mo_ref, vo_ref):
        cv = c_ref[...]
        sc = (cv * _sigmoid(cv)).astype(BF16)
        g = _dot_tn(sc, dm_ref[...].astype(BF16))
        g_ref[...] = g
        d_ref[...], mo_ref[...], vo_ref[...] = _adamw(w_ref[...], g, m_ref[...], v_ref[...])

    blk = pl.BlockSpec((tr, cols), lambda i: (i, 0))
    return _pcall(
        body, name="wada_adamw", grid=(rws // tr,),
        in_specs=[pl.BlockSpec((N_DEV, tr), lambda i: (0, i)), pl.BlockSpec((N_DEV, cols), lambda i: (0, 0)),
                  blk, blk, blk],
        out_specs=[blk] * 4, out_shape=[_sds((rws, cols), F32)] * 4, compiler_params=_params(),
    )(c_all, dmod_cols, w, m, v)


def _small_reduce(packs):
    def body(p_ref, o_ref):
        tot = p_ref[0]
        for dev in range(1, N_DEV):
            tot = tot + p_ref[dev]
        o_ref[...] = tot
        o_ref[16:17, :] = jnp.zeros((1, D), F32) + (0.5 / D) * jnp.sum(tot[16:17, :])

    return _pcall(body, name="small_reduce", out_shape=_sds((PACK_ROWS, D), F32))(packs)


def _adamw_many(ws, gs, ms, vs):
    k = len(ws)

    def body(*refs):
        for i in range(k):
            w_ref, g_ref, m_ref, v_ref = refs[i], refs[k + i], refs[2 * k + i], refs[3 * k + i]
            d_ref, mo_ref, vo_ref = refs[4 * k + i], refs[5 * k + i], refs[6 * k + i]
            d_ref[...], mo_ref[...], vo_ref[...] = _adamw(w_ref[...], g_ref[...], m_ref[...], v_ref[...])

    shp = [_sds(w.shape, F32) for w in ws]
    outs = _pcall(body, name="adamw_small", out_shape=shp * 3)(*ws, *gs, *ms, *vs)
    return outs[:k], outs[k:2 * k], outs[2 * k:]


def kernel(x, c, positions, w_ada, b_ada, norm1_w, w_in, conv_w, conv_b, dt_bias, a_log, d_skip, attn_sinks, ssm_norm_w, w_out, norm2_w, w_gate_up, w_down, final_norm_w, loss_target, m_w_ada, m_b_ada, m_norm1_w, m_w_in, m_conv_w, m_conv_b, m_dt_bias, m_a_log, m_d_skip, m_attn_sinks, m_ssm_norm_w, m_w_out, m_norm2_w, m_w_gate_up, m_w_down, m_final_norm_w, v_w_ada, v_b_ada, v_norm1_w, v_w_in, v_conv_w, v_conv_b, v_dt_bias, v_a_log, v_d_skip, v_attn_sinks, v_ssm_norm_w, v_w_out, v_norm2_w, v_w_gate_up, v_w_down, v_final_norm_w):
    s = x.shape[1]
    me = 4 * lax.axis_index("x") + 2 * lax.axis_index("y") + lax.axis_index("c")
    ada_cols = N_MOD * D // N_DEV

    c8 = jnp.pad(c, ((0, 7), (0, 0)))
    cw8 = jnp.pad(conv_w[0], ((0, 8 - CONVK), (0, 0)))
    w_in_t, m_w_in_t, v_w_in_t = jnp.transpose(w_in[0]), jnp.transpose(m_w_in[0]), jnp.transpose(v_w_in[0])
    w_gu_t, m_w_gu_t, v_w_gu_t = (jnp.transpose(w_gate_up[0]), jnp.transpose(m_w_gate_up[0]),
                                  jnp.transpose(v_w_gate_up[0]))
    b_cols = lax.dynamic_slice(b_ada, (0, me * ada_cols), (1, ada_cols))
    g_c, g_in, g_cw, g_mod = _prologue(c8, w_in_t.astype(BF16), cw8, w_ada[0], b_cols)
    c_all = g_c[:, 0, :]
    w_in_f = jnp.pad(g_in.reshape(IN_PROJ, D), ((0, INP - IN_PROJ), (0, 0)))
    conv_w8 = jnp.transpose(g_cw, (1, 0, 2)).reshape(8, D)
    mod = lax.dynamic_index_in_dim(g_mod, me, axis=1, keepdims=False).reshape(N_MOD, D)
    mod8 = jnp.pad(mod, ((0, 8 - N_MOD), (0, 0)))

    half = HD // 2
    inv_freq = ROPE_THETA ** (-jnp.arange(half, dtype=F32) / half)
    invf = jnp.tile(inv_freq, LB // half).reshape(1, LB)
    lanes = lambda a: jnp.pad(a, ((0, 0), (0, LB - a.shape[1])))
    ssm_p = jnp.pad(jnp.concatenate([lanes(dt_bias), lanes(a_log), lanes(d_skip)], axis=0), ((0, 5), (0, 0)))
    sinks8 = jnp.broadcast_to(attn_sinks.reshape(NQ, 1), (NQ, LB))

    xs, tgt, fnw = x[0], loss_target[0], final_norm_w.reshape(1, D)

    q, k, v, z, xbc, dtr, h1, cos, sin = _inproj_fwd(xs, positions[0].reshape(s, 1), invf, mod8, norm1_w, w_in_f)
    (attn, yn, y, hs, conv_u, dtv, acs, decay, probs, psink), (g_out, g_gu, g_down) = _mixer_fwd(
        q, k, v, sinks8, xbc, conv_w8, conv_b, dtr, ssm_p, z, ssm_norm_w,
        [w_out[0].astype(BF16), w_gu_t.astype(BF16), w_down[0].astype(BF16)])
    w_out_f = g_out.reshape(D, D)
    w_gu_f = g_gu.reshape(2 * DFF, D)
    w_down_f = g_down.reshape(DFF, D)
    x2, h2, mo, mix, gu, act, dx3, sm_f = _outproj_ffn_fwd_loss(attn, yn, xs, tgt, mod8, norm2_w, fnw, w_out_f, w_gu_f,
                                                                 w_down_f)

    dx2, dff, dgu, dmix, dattn, dyn, sm_b = _ffn_bwd(dx3, gu, x2, mo, mod8, norm2_w, w_gu_f, w_down_f, w_out_f)
    p_gu = _wgrad(dgu, h2, "wgrad_gate_up").reshape(N_DEV, 2 * DFF // N_DEV, D)
    p_down = _wgrad(act, dff, "wgrad_down").reshape(N_DEV, DFF // N_DEV, D)
    p_out = _wgrad(mix, dmix, "wgrad_out").reshape(N_DEV, D // N_DEV, D)
    dproj, dsink, sm_s, (r_gu, r_down, r_out) = _mixer_bwd(
        q, k, v, attn, dattn, cos, sin, probs, psink, dyn, y, z, xbc, conv_u, dtv, acs, decay, conv_w8, ssm_p,
        ssm_norm_w, hs, [p_gu, p_down, p_out])
    p_in = _wgrad(dproj, h1, "wgrad_in")[:IN_PROJ].reshape(N_DEV, IN_PROJ // N_DEV, D)
    gx, (r_in,), g_pack = _inproj_bwd(dproj, xs, dx2, mod8, norm1_w, w_in_f, [p_in], (sm_f, sm_b, sm_s, dsink))

    tot = _small_reduce(g_pack)
    loss = tot[16, 0]
    dmod_all = g_pack[:, 0:N_MOD, :].reshape(N_DEV, N_MOD * D)
    dmod_cols = lax.dynamic_slice(dmod_all, (0, me * ada_cols), (N_DEV, ada_cols))

    big = {
        "w_ada": _wada_adamw(c_all, dmod_cols, w_ada[0], m_w_ada[0], v_w_ada[0]),
        "w_in": [jnp.transpose(t) for t in _sum_adamw(r_in, w_in_t, m_w_in_t, v_w_in_t, "adamw_in")],
        "w_out": _sum_adamw(r_out, w_out[0], m_w_out[0], v_w_out[0], "adamw_out"),
        "w_gate_up": [jnp.transpose(t) for t in _sum_adamw(r_gu, w_gu_t, m_w_gu_t, v_w_gu_t, "adamw_gate_up")],
        "w_down": _sum_adamw(r_down, w_down[0], m_w_down[0], v_w_down[0], "adamw_down"),
    }
    small_names = ["b_ada", "norm1_w", "conv_w", "conv_b", "dt_bias", "a_log", "d_skip", "attn_sinks", "ssm_norm_w",
                   "norm2_w", "final_norm_w"]
    row15 = tot[15:16, :]
    small_g = {
        "b_ada": tot[0:N_MOD, :].reshape(1, N_MOD * D),
        "norm1_w": tot[6:7, :],
        "conv_w": lax.dynamic_slice(tot[10:14, :], (0, me * (D // N_DEV)), (CONVK, D // N_DEV)),
        "conv_b": tot[9:10, :],
        "dt_bias": row15[:, 0:NH],
        "a_log": row15[:, LB:LB + NH],
        "d_skip": row15[:, 2 * LB:2 * LB + NH],
        "attn_sinks": row15[:, 3 * LB:3 * LB + NQ],
        "ssm_norm_w": tot[14:15, 0:SW],
        "norm2_w": tot[7:8, :],
        "final_norm_w": tot[8:9, :],
    }
    small_w = {"b_ada": b_ada, "norm1_w": norm1_w, "conv_w": conv_w[0], "conv_b": conv_b, "dt_bias": dt_bias,
               "a_log": a_log, "d_skip": d_skip, "attn_sinks": attn_sinks, "ssm_norm_w": ssm_norm_w,
               "norm2_w": norm2_w, "final_norm_w": final_norm_w.reshape(1, D)}
    small_m = {"b_ada": m_b_ada, "norm1_w": m_norm1_w, "conv_w": m_conv_w[0], "conv_b": m_conv_b,
               "dt_bias": m_dt_bias, "a_log": m_a_log, "d_skip": m_d_skip, "attn_sinks": m_attn_sinks,
               "ssm_norm_w": m_ssm_norm_w, "norm2_w": m_norm2_w, "final_norm_w": m_final_norm_w.reshape(1, D)}
    small_v = {"b_ada": v_b_ada, "norm1_w": v_norm1_w, "conv_w": v_conv_w[0], "conv_b": v_conv_b,
               "dt_bias": v_dt_bias, "a_log": v_a_log, "d_skip": v_d_skip, "attn_sinks": v_attn_sinks,
               "ssm_norm_w": v_ssm_norm_w, "norm2_w": v_norm2_w, "final_norm_w": v_final_norm_w.reshape(1, D)}
    s_d, s_m, s_v = _adamw_many([small_w[k] for k in small_names], [small_g[k] for k in small_names],
                                [small_m[k] for k in small_names], [small_v[k] for k in small_names])

    order = ["w_ada", "b_ada", "norm1_w", "w_in", "conv_w", "conv_b", "dt_bias", "a_log", "d_skip", "attn_sinks",
             "ssm_norm_w", "w_out", "norm2_w", "w_gate_up", "w_down", "final_norm_w"]
    lead = {"w_ada", "w_in", "conv_w", "w_out", "w_gate_up", "w_down"}
    grads, deltas, new_m, new_v = [], [], [], []
    for name in order:
        if name in big:
            g, d, m2, v2 = big[name]
        else:
            i = small_names.index(name)
            g, d, m2, v2 = small_g[name], s_d[i], s_m[i], s_v[i]
        if name in lead:
            g, d, m2, v2 = g[None], d[None], m2[None], v2[None]
        if name == "final_norm_w":
            g, d, m2, v2 = g.reshape(D), d.reshape(D), m2.reshape(D), v2.reshape(D)
        grads.append(g)
        deltas.append(d)
        new_m.append(m2)
        new_v.append(v2)
    return (loss, gx[None], *grads, *deltas, *new_m, *new_v)
```

```python
import functools
import math

import jax
import jax.numpy as jnp
from jax import lax
from jax.experimental import pallas as pl
from jax.experimental.pallas import tpu as pltpu

F32 = jnp.float32
BF16 = jnp.bfloat16

N_DEV = 8
D = 1024
HD = 64
NQ = 8
AW = 512
KVW = 128
SW = 512
NST = 128
NH = 8
LB = 128
CONVK = 4
DFF = 2816
N_MOD = 6
IN_PROJ = 2312
INP = 2432
O_Q, O_K, O_V, O_Z, O_XBC, O_DT = 0, 512, 640, 768, 1280, 2304
ZXD = INP - O_Z
EPS = 1e-6
NEG = -1e30
ROPE_THETA = 10000.0
VMEM_LIMIT = 56 * 1024 * 1024

ADAM_LR = 0.001
ADAM_B1 = 0.9
ADAM_B2 = 0.999
ADAM_EPS = 1e-08
ADAM_WD = 0.01
ADAM_STEP = 10

NT_DIMS = (((1,), (1,)), ((), ()))
TN_DIMS = (((0,), (0,)), ((), ()))


def _pcall(body, **kw):
    return pl.pallas_call(body, **kw)


def _sds(shape, dtype):
    return jax.ShapeDtypeStruct(shape, dtype)


def _params(n_grid=1):
    return pltpu.CompilerParams(dimension_semantics=("arbitrary",) * n_grid, vmem_limit_bytes=VMEM_LIMIT)


def _const(shape):
    return pl.BlockSpec(shape, lambda *_: (0,) * len(shape), pipeline_mode=pl.Buffered(1))


def _largest_divisor(n, candidates):
    for cand in candidates:
        if n % cand == 0:
            return cand
    raise ValueError(f"no tile in {candidates} divides {n}")


def _rows(t, w):
    return pl.BlockSpec((t, w), lambda i: (i, 0))


def _dot(a, b):
    return jnp.dot(a, b, preferred_element_type=F32)


def _dot_nt(a, b):
    return lax.dot_general(a, b, NT_DIMS, preferred_element_type=F32)


def _dot_tn(a, b):
    return lax.dot_general(a, b, TN_DIMS, preferred_element_type=F32)


def _sigmoid(v):
    return 1.0 / (1.0 + jnp.exp(-v))


def _softplus(v):
    return jnp.maximum(v, 0.0) + jnp.log1p(jnp.exp(-jnp.abs(v)))


def _rope_sign_mask(shape):
    lane = lax.broadcasted_iota(jnp.int32, shape, 1)
    return (lane % HD) < (HD // 2)


def _rope(t, cs, sn, inverse):
    r_dn = pltpu.roll(t, HD // 2, 1)
    r_up = pltpu.roll(t, LB - HD // 2, 1)
    first = _rope_sign_mask(t.shape)
    if inverse:
        rot = jnp.where(first, r_up, -r_dn)
    else:
        rot = jnp.where(first, -r_up, r_dn)
    return t * cs + rot * sn


def _norm_mod_fwd(xv, nw, shift, scale):
    r = lax.rsqrt(jnp.mean(xv * xv, axis=-1, keepdims=True) + EPS)
    xh = xv * r
    return (xh * nw) * (1.0 + scale) + shift


def _norm_mod_bwd(xv, dh, nw, scale):
    r = lax.rsqrt(jnp.mean(xv * xv, axis=-1, keepdims=True) + EPS)
    xh = xv * r
    xn = xh * nw
    d_shift = jnp.sum(dh, axis=0, keepdims=True)
    d_scale = jnp.sum(dh * xn, axis=0, keepdims=True)
    dxn = dh * (1.0 + scale)
    d_w = jnp.sum(dxn * xh, axis=0, keepdims=True)
    dxh = dxn * nw
    dx = r * (dxh - xh * jnp.mean(dxh * xh, axis=-1, keepdims=True))
    return dx, d_shift, d_scale, d_w


def _inproj_fwd(x, pos, invf, mod8, n1w, w_in):
    s = x.shape[0]
    tt = min(512, s)

    def body(x_ref, pos_ref, invf_ref, mod_ref, nw_ref, w_ref,
             q_ref, k_ref, v_ref, z_ref, xbc_ref, dtr_ref, h1_ref, cos_ref, sin_ref):
        h = _norm_mod_fwd(x_ref[...], nw_ref[...], mod_ref[0:1, :], mod_ref[1:2, :])
        hb = h.astype(BF16)
        h1_ref[...] = hb
        proj = _dot_nt(hb, w_ref[...])
        ang = pos_ref[...].astype(F32) * invf_ref[...]
        cs = jnp.cos(ang)
        sn = jnp.sin(ang)
        cos_ref[...] = cs
        sin_ref[...] = sn
        for a in range(AW // LB):
            q_ref[:, a * LB:(a + 1) * LB] = _rope(proj[:, O_Q + a * LB:O_Q + (a + 1) * LB], cs, sn, False).astype(BF16)
        k_ref[...] = _rope(proj[:, O_K:O_V], cs, sn, False).astype(BF16)
        v_ref[...] = proj[:, O_V:O_Z].astype(BF16)
        z_ref[...] = proj[:, O_Z:O_XBC]
        xbc_ref[...] = proj[:, O_XBC:O_DT]
        dtr_ref[...] = proj[:, O_DT:INP]

    return _pcall(
        body, name="inproj_fwd", grid=(s // tt,),
        in_specs=[_rows(tt, D), _rows(tt, 1), _const((1, LB)), _const((8, D)), _const((1, D)), _const((INP, D))],
        out_specs=[_rows(tt, AW), _rows(tt, KVW), _rows(tt, KVW), _rows(tt, SW), _rows(tt, D), _rows(tt, LB),
                   _rows(tt, D), _rows(tt, LB), _rows(tt, LB)],
        out_shape=[_sds((s, AW), BF16), _sds((s, KVW), BF16), _sds((s, KVW), BF16), _sds((s, SW), F32),
                   _sds((s, D), F32), _sds((s, LB), F32), _sds((s, D), BF16), _sds((s, LB), F32), _sds((s, LB), F32)],
        compiler_params=_params(),
    )(x, pos, invf, mod8, n1w, w_in)


QPG = 4
ATT_SCALE = 1.0 / math.sqrt(HD)


def _stack_heads(val, g):
    return jnp.concatenate([val[:, (QPG * g + hh) * HD:(QPG * g + hh + 1) * HD] for hh in range(QPG)], axis=0)


def _unstack_heads(groups):
    pieces = [grp[hh * LB:(hh + 1) * LB, :] for grp in groups for hh in range(QPG)]
    return [jnp.concatenate(pieces[2 * a:2 * a + 2], axis=1) for a in range(NQ // 2)]


def _upper_mask():
    row = lax.broadcasted_iota(jnp.int32, (QPG * LB, LB), 0)
    col = lax.broadcasted_iota(jnp.int32, (QPG * LB, LB), 1)
    return col > (row % LB)


def _sink_wide(sinks, g):
    return jnp.concatenate([jnp.broadcast_to(sinks[QPG * g + hh:QPG * g + hh + 1, 0:1], (LB, LB))
                            for hh in range(QPG)], axis=0)


def _row_sums_wide(v, terms):
    return _dot_sel(v, jnp.ones((v.shape[1], LB), BF16), terms)


def _band(upper, prev_part, cur_part):
    return jnp.where(upper, prev_part, cur_part)


def _attn_scores(n, qg, kcat, upper):
    sp = _dot_nt(qg, kcat[0:LB, :]) * ATT_SCALE
    sc = _dot_nt(qg, kcat[LB:2 * LB, :]) * ATT_SCALE
    return _band(upper, jnp.where(n > 0, sp, NEG), sc)


def _attn_softmax(comb, sink):
    m = jnp.maximum(jnp.max(comb, axis=-1, keepdims=True), sink)
    p = jnp.exp(comb - m)
    es = jnp.exp(sink - m)
    return p, es, _row_sums_wide(p, 1) + es


def _attn_fwd_block(n, q_ref, kp_ref, kc_ref, vp_ref, vc_ref, sink_ref, o_ref, pr_ref, ps_ref):
    qv = q_ref[...]
    kcat = jnp.concatenate([kp_ref[...], kc_ref[...]], axis=0)
    vcat = jnp.concatenate([vp_ref[...], vc_ref[...]], axis=0)
    sinks = sink_ref[...]
    upper = _upper_mask()
    outs = []
    for g in range(NQ // QPG):
        sl = slice(g * HD, (g + 1) * HD)
        rows = slice(g * QPG * LB, (g + 1) * QPG * LB)
        p, es, denom = _attn_softmax(_attn_scores(n, _stack_heads(qv, g), kcat[:, sl], upper), _sink_wide(sinks, g))
        rden = 1.0 / denom
        pr_ref[0, rows, :] = (p * rden).astype(BF16)
        ps_ref[0, rows, :] = (es * rden).astype(BF16)
        outs.append((_dot(jnp.where(upper, p, 0.0).astype(BF16), vcat[0:LB, sl])
                     + _dot(jnp.where(upper, 0.0, p).astype(BF16), vcat[LB:2 * LB, sl])) * rden[:, 0:HD])
    for g, grp in enumerate(outs):
        for hh in range(QPG):
            h = QPG * g + hh
            o_ref[:, h * HD:(h + 1) * HD] = grp[hh * LB:(hh + 1) * LB, :].astype(BF16)


def _cumsum_rows(a, reverse):
    row = lax.broadcasted_iota(jnp.int32, a.shape, 0)
    step = 1
    while step < LB:
        if reverse:
            a = a + jnp.where(row < LB - step, pltpu.roll(a, LB - step, 0), 0.0)
        else:
            a = a + jnp.where(row >= step, pltpu.roll(a, step, 0), 0.0)
        step *= 2
    return a


SUB = 8


def _conv_shifts(tail, cur):
    row = lax.broadcasted_iota(jnp.int32, tail.shape, 0)
    out = [cur]
    for j in range(1, CONVK):
        rolled = pltpu.roll(cur, j, 0)
        top = jnp.where(row < j, pltpu.roll(tail, j, 0), rolled[0:SUB, :])
        out.append(jnp.concatenate([top, rolled[SUB:, :]], axis=0))
    return out


def _conv_advances(du, head):
    row = lax.broadcasted_iota(jnp.int32, head.shape, 0)
    out = []
    for j in range(1, CONVK):
        rolled = pltpu.roll(du, LB - j, 0)
        bottom = jnp.where(row >= SUB - j, pltpu.roll(head, SUB - j, 0), rolled[LB - SUB:, :])
        out.append(jnp.concatenate([rolled[:LB - SUB, :], bottom], axis=0))
    return out


def _split(v, terms):
    out = []
    for _ in range(terms - 1):
        t = v.astype(BF16)
        out.append(t)
        v = v - t.astype(F32)
    out.append(v.astype(BF16))
    return out


def _dot_sel(v, sel, terms):
    parts = [_dot(t, sel) for t in _split(v, terms)]
    return functools.reduce(lambda a, b: a + b, parts)


def _dot_nt_sel(v, sel, terms):
    parts = [_dot_nt(t, sel) for t in _split(v, terms)]
    return functools.reduce(lambda a, b: a + b, parts)


def _ssd_pre(xt_ref, xc_ref, cw_ref, cb_ref, dtr_ref, sp_ref, n):
    cur = xc_ref[...]
    tail = jnp.where(n > 0, xt_ref[...], 0.0)
    sh = _conv_shifts(tail, cur)
    u = cb_ref[...] + cw_ref[CONVK - 1:CONVK, :] * sh[0]
    for j in range(1, CONVK):
        u = u + cw_ref[CONVK - 1 - j:CONVK - j, :] * sh[j]
    dt = _softplus(dtr_ref[...] + sp_ref[0:1, :])
    acs = _cumsum_rows(dt * -jnp.exp(sp_ref[1:2, :]), False)
    return u, dt, acs


def _gated_norm_fwd(y, z, sgz, nw):
    yz = y * (z * sgz)
    parts = []
    for g in range(2):
        t = yz[:, g * 256:(g + 1) * 256]
        parts.append(t * lax.rsqrt(jnp.mean(t * t, axis=-1, keepdims=True) + EPS))
    return jnp.concatenate(parts, axis=1) * nw


HPG = 4
GW = HPG * HD


class _SsdChunk:
    def __init__(self, xc, dt, acs, spv, e64, e128, decay=None):
        self.e64, self.e128 = e64, e128
        alast = acs[LB - 1:LB, :]
        self.e_all = jnp.exp(acs)
        self.dte_all = jnp.exp(alast - acs)
        self.elast = jnp.exp(alast)
        wide = _dot_sel(jnp.concatenate([dt, self.e_all, self.dte_all], axis=0), e64, 2)
        self.dt_x, self.e_x, self.dte_x = wide[0:LB], wide[LB:2 * LB], wide[2 * LB:3 * LB]
        self.dsk_x = _dot_sel(spv, e64, 3)[2:3, :]
        if decay is None:
            acs_t = jnp.transpose(acs)
            ac_x = _dot_sel(acs, e128, 3)
            row = lax.broadcasted_iota(jnp.int32, (HPG * LB, LB), 0)
            col = lax.broadcasted_iota(jnp.int32, (HPG * LB, LB), 1)
            causal = (row % LB) >= col
        lane = lax.broadcasted_iota(jnp.int32, (LB, GW), 1)
        self.head_lanes = [(lane >= hh * HD) & (lane < (hh + 1) * HD) for hh in range(HPG)]
        self.xs, self.xdt, self.b, self.c, self.bb, self.cb16, self.cbm, self.dm_st, self.m_st = ([] for _ in range(9))
        for g in range(2):
            heads = range(HPG * g, HPG * (g + 1))
            if decay is None:
                ac_st = jnp.concatenate([ac_x[:, j * LB:(j + 1) * LB] for j in heads], axis=0)
                ar_st = jnp.concatenate([jnp.broadcast_to(acs_t[j:j + 1, :], (LB, LB)) for j in heads], axis=0)
                dm_st = jnp.exp(jnp.where(causal, ac_st - ar_st, NEG))
            else:
                dm_st = decay[g]
            bg = xc[:, SW + g * NST:SW + (g + 1) * NST]
            cg = xc[:, SW + 2 * NST + g * NST:SW + 2 * NST + (g + 1) * NST]
            bgb, cgb = bg.astype(BF16), cg.astype(BF16)
            cbm = _dot_nt(cgb, bgb)
            xs_g = xc[:, g * GW:(g + 1) * GW]
            self.xs.append(xs_g)
            self.xdt.append(xs_g * self.dt_x[:, g * GW:(g + 1) * GW])
            self.b.append(bg)
            self.c.append(cg)
            self.bb.append(bgb)
            self.cb16.append(cgb)
            self.cbm.append(cbm)
            self.dm_st.append(dm_st)
            self.m_st.append(jnp.concatenate([cbm] * HPG, axis=0) * dm_st)

    def elast_rows(self, g):
        return jnp.concatenate([jnp.broadcast_to(self.elast[:, j:j + 1], (HD, NST))
                                for j in range(HPG * g, HPG * (g + 1))], axis=0)

    def diag_blocks(self, stacked):
        out = stacked[(HPG - 1) * LB:HPG * LB, :]
        for hh in range(HPG - 2, -1, -1):
            out = jnp.where(self.head_lanes[hh], stacked[hh * LB:(hh + 1) * LB, :], out)
        return out

    def block_diag(self, v):
        return jnp.concatenate([jnp.where(self.head_lanes[hh], v, 0.0) for hh in range(HPG)], axis=0)


def _ssd_fwd_block(n, xt_ref, xc_ref, cw_ref, cb_ref, dtr_ref, sp_ref, z_ref, nw_ref, e64_ref, e128_ref,
                   yn_ref, y_ref, hs_ref, u_ref, dt_ref, acs_ref, dm_ref, h_scr):
    @pl.when(n == 0)
    def _():
        h_scr[...] = jnp.zeros_like(h_scr)

    h_all = h_scr[...]
    hs_ref[0] = h_all
    u, dt, acs = _ssd_pre(xt_ref, xc_ref, cw_ref, cb_ref, dtr_ref, sp_ref, n)
    u_ref[...] = u
    dt_ref[...] = dt
    acs_ref[...] = acs
    xc = u * _sigmoid(u)
    ck = _SsdChunk(xc, dt, acs, sp_ref[...], e64_ref[...], e128_ref[...])
    dm_ref[0] = jnp.concatenate(ck.dm_st, axis=0)
    ys, hn = [], []
    for g in range(2):
        gl = slice(g * GW, (g + 1) * GW)
        xdt = ck.xdt[g]
        hg = h_all[gl, :]
        y_diag = ck.diag_blocks(_dot(ck.m_st[g].astype(BF16), xdt.astype(BF16)))
        y_off = ck.e_x[:, gl] * _dot_nt(ck.cb16[g], hg.astype(BF16))
        ys.append(y_diag + y_off + ck.xs[g] * ck.dsk_x[:, gl])
        hn.append(hg * ck.elast_rows(g) + _dot_tn((xdt * ck.dte_x[:, gl]).astype(BF16), ck.bb[g]))
    h_scr[...] = jnp.concatenate(hn, axis=0)
    y = jnp.concatenate(ys, axis=1)
    y_ref[...] = y
    z = z_ref[...]
    yn_ref[...] = _gated_norm_fwd(y, z, _sigmoid(z), nw_ref[...]).astype(BF16)


def _mixer_fwd(q, k, v, sinks8, xbc, conv_w8, conv_b, dtr, ssm_p, z, nw, gathers):
    s = q.shape[0]
    nb = s // LB
    bps = _blocks_per_step(nb)
    nsteps = nb // bps
    tl = bps * LB
    cur = lambda n: (n, 0)
    prev = lambda n: (jnp.maximum(n * bps - 1, 0), 0)
    items, ex_shapes, n_g = _exchange_items(gathers, [])
    ne = len(items)

    n_in, n_out = 16, 10
    relay_step = (3 * (nsteps - 1)) // 4
    stack = pl.BlockSpec((bps, NH * LB, LB), lambda n: (n, 0, 0))
    e64, e128 = _head_expanders()

    def body(*refs):
        (q_ref, kp_ref, kc_ref, vp_ref, vc_ref, sink_ref, xt_ref, xc_ref, cw_ref, cb_ref, dtr_ref, sp_ref, z_ref,
         nw_ref, e64_ref, e128_ref) = refs[:n_in]
        ex_in = refs[n_in:n_in + ne]
        (o_ref, yn_ref, y_ref, hs_ref, u_ref, dt_ref, acs_ref, dm_ref, pr_ref,
         ps_ref) = refs[n_in + ne:n_in + n_out + ne]
        ex_out = refs[n_in + n_out + ne:n_in + n_out + 2 * ne]
        h_scr = refs[n_in + n_out + 2 * ne]
        sems = refs[n_in + n_out + 1 + 2 * ne:]
        n = pl.program_id(0)

        @pl.when(n == 0)
        def _():
            _Exchange(n_g, ex_in, ex_out, sems).two_level_start()

        for sub in range(bps):
            blk = n * bps + sub
            r = slice(sub * LB, (sub + 1) * LB)
            before = slice((sub - 1) * LB, sub * LB)
            one = slice(sub, sub + 1)
            _attn_fwd_block(blk, q_ref.at[r], kp_ref if sub == 0 else kc_ref.at[before], kc_ref.at[r],
                            vp_ref if sub == 0 else vc_ref.at[before], vc_ref.at[r], sink_ref,
                            o_ref.at[r], pr_ref.at[one], ps_ref.at[one])
            _ssd_fwd_block(blk, xt_ref if sub == 0 else xc_ref.at[sub * LB - SUB:sub * LB], xc_ref.at[r], cw_ref,
                           cb_ref, dtr_ref.at[r], sp_ref, z_ref.at[r], nw_ref, e64_ref, e128_ref,
                           yn_ref.at[r], y_ref.at[r], hs_ref.at[one], u_ref.at[r], dt_ref.at[r], acs_ref.at[r],
                           dm_ref.at[one], h_scr)

        @pl.when(n == relay_step)
        def _():
            _Exchange(n_g, ex_in, ex_out, sems).two_level_relay()

        @pl.when(n == nsteps - 1)
        def _():
            _Exchange(n_g, ex_in, ex_out, sems).two_level_finish()

    any_spec = pl.BlockSpec(memory_space=pl.ANY)
    tail = pl.BlockSpec((SUB, D), lambda n: (jnp.maximum(n * (tl // SUB) - 1, 0), 0))
    outs = _pcall(
        body, name="mixer_fwd", grid=(nsteps,),
        in_specs=[pl.BlockSpec((tl, AW), cur), pl.BlockSpec((LB, KVW), prev), pl.BlockSpec((tl, KVW), cur),
                  pl.BlockSpec((LB, KVW), prev), pl.BlockSpec((tl, KVW), cur), _const((8, LB)),
                  tail, pl.BlockSpec((tl, D), cur), _const((8, D)), _const((1, D)),
                  pl.BlockSpec((tl, LB), cur), _const((8, LB)), pl.BlockSpec((tl, SW), cur), _const((1, SW)),
                  _const(e64.shape), _const(e128.shape)]
        + [any_spec] * ne,
        out_specs=[pl.BlockSpec((tl, AW), cur), pl.BlockSpec((tl, SW), cur), pl.BlockSpec((tl, SW), cur),
                   pl.BlockSpec((bps, NH * HD, NST), lambda n: (n, 0, 0)), pl.BlockSpec((tl, D), cur),
                   pl.BlockSpec((tl, LB), cur), pl.BlockSpec((tl, LB), cur),
                   stack, stack, stack] + [any_spec] * ne,
        out_shape=[_sds((s, AW), BF16), _sds((s, SW), BF16), _sds((s, SW), F32), _sds((nb, NH * HD, NST), F32),
                   _sds((s, D), F32), _sds((s, LB), F32), _sds((s, LB), F32), _sds((nb, NH * LB, LB), F32),
                   _sds((nb, NH * LB, LB), BF16), _sds((nb, NH * LB, LB), BF16)]
        + ex_shapes,
        scratch_shapes=[pltpu.VMEM((NH * HD, NST), F32)] + _exchange_sems(ne),
        compiler_params=_params(),
    )(q, k, k, v, v, sinks8, xbc, xbc, conv_w8, conv_b, dtr, ssm_p, z, nw, e64, e128, *items)
    return outs[:n_out], outs[n_out:]


def _blocks_per_step(nb):
    return next(b for b in (4, 2, 1) if nb % b == 0)


def _head_expanders():
    j = lax.broadcasted_iota(jnp.int32, (LB, NH * HD), 0)
    e64 = (lax.broadcasted_iota(jnp.int32, (LB, NH * HD), 1) // HD == j).astype(BF16)
    j = lax.broadcasted_iota(jnp.int32, (LB, NH * LB), 0)
    e128 = (lax.broadcasted_iota(jnp.int32, (LB, NH * LB), 1) // LB == j).astype(BF16)
    return e64, e128


def _outproj_ffn_fwd_loss(attn, yn, x, tgt, mod8, n2w, fnw, w_out, w_gu_t, w_down):
    s = x.shape[0]
    tf = min(256, s)

    def body(a_ref, y_ref, x_ref, t_ref, mod_ref, nw_ref, fw_ref, wo_ref, wgu_ref, wd_ref,
             x2_ref, h2_ref, mo_ref, mix_ref, gu_ref, act_ref, dx3_ref, sm_ref):
        i = pl.program_id(0)

        @pl.when(i == 0)
        def _():
            sm_ref[...] = jnp.zeros_like(sm_ref)

        mix = jnp.concatenate([a_ref[...], y_ref[...]], axis=1)
        mix_ref[...] = mix
        mo = _dot(mix, wo_ref[...])
        mo_ref[...] = mo.astype(BF16)
        x2 = x_ref[...] + mod_ref[2:3, :] * mo
        x2_ref[...] = x2
        h2 = _norm_mod_fwd(x2, nw_ref[...], mod_ref[3:4, :], mod_ref[4:5, :]).astype(BF16)
        h2_ref[...] = h2
        gu = _dot_nt(h2, wgu_ref[...])
        gu_ref[...] = gu.astype(BF16)
        g = gu[:, :DFF]
        act = (g * _sigmoid(g) * gu[:, DFF:]).astype(BF16)
        act_ref[...] = act
        ff = _dot(act, wd_ref[...])
        x3 = x2 + mod_ref[5:6, :] * ff
        r = lax.rsqrt(jnp.mean(x3 * x3, axis=-1, keepdims=True) + EPS)
        xh = x3 * r
        fw = fw_ref[...]
        err = xh * fw - t_ref[...]
        dy = err * (1.0 / D)
        dxh = dy * fw
        dx3 = r * (dxh - xh * jnp.mean(dxh * xh, axis=-1, keepdims=True))
        dx3_ref[...] = dx3
        sm_ref[0:1, :] += jnp.sum(dx3 * ff, axis=0, keepdims=True)
        sm_ref[1:2, :] += jnp.sum(dy * xh, axis=0, keepdims=True)
        sm_ref[2:3, :] += jnp.sum(err * err, axis=0, keepdims=True)

    return _pcall(
        body, name="outproj_ffn_fwd_loss", grid=(s // tf,),
        in_specs=[_rows(tf, AW), _rows(tf, SW), _rows(tf, D), _rows(tf, D), _const((8, D)), _const((1, D)),
                  _const((1, D)), _const((D, D)), _const((2 * DFF, D)), _const((DFF, D))],
        out_specs=[_rows(tf, D), _rows(tf, D), _rows(tf, D), _rows(tf, D), _rows(tf, 2 * DFF), _rows(tf, DFF),
                   _rows(tf, D), pl.BlockSpec((8, D), lambda i: (0, 0))],
        out_shape=[_sds((s, D), F32), _sds((s, D), BF16), _sds((s, D), BF16), _sds((s, D), BF16),
                   _sds((s, 2 * DFF), BF16), _sds((s, DFF), BF16), _sds((s, D), F32), _sds((8, D), F32)],
        compiler_params=_params(),
    )(attn, yn, x, tgt, mod8, n2w, fnw, w_out, w_gu_t, w_down)


def _ffn_bwd(dx3, gu, x2, mixout, mod8, n2w, w_gu, w_down, w_out):
    s = x2.shape[0]
    tb = min(256, s)

    def body(dx3_ref, gu_ref, x2_ref, mo_ref, mod_ref, nw_ref, wgu_ref, wd_ref, wo_ref,
             dx2_ref, dff_ref, dgu_ref, dmix_ref, dattn_ref, dyn_ref, sm_ref):
        i = pl.program_id(0)

        @pl.when(i == 0)
        def _():
            sm_ref[...] = jnp.zeros_like(sm_ref)

        dx3 = dx3_ref[...]
        dff = (dx3 * mod_ref[5:6, :]).astype(BF16)
        dff_ref[...] = dff
        dact = _dot_nt(dff, wd_ref[...])
        g = gu_ref[:, :DFF].astype(F32)
        u = gu_ref[:, DFF:].astype(F32)
        sg = _sigmoid(g)
        dgu = jnp.concatenate([dact * u * sg * (1.0 + g * (1.0 - sg)), dact * g * sg], axis=1).astype(BF16)
        dgu_ref[...] = dgu
        dh2 = _dot(dgu, wgu_ref[...])
        dxn, d_shift, d_scale, d_w = _norm_mod_bwd(x2_ref[...], dh2, nw_ref[...], mod_ref[4:5, :])
        dx2 = dx3 + dxn
        dx2_ref[...] = dx2
        sm_ref[0:1, :] += d_shift
        sm_ref[1:2, :] += d_scale
        sm_ref[2:3, :] += d_w
        sm_ref[3:4, :] += jnp.sum(dx2 * mo_ref[...].astype(F32), axis=0, keepdims=True)
        dmix = (dx2 * mod_ref[2:3, :]).astype(BF16)
        dmix_ref[...] = dmix
        dmi = _dot_nt(dmix, wo_ref[...])
        dattn_ref[...] = dmi[:, :AW].astype(BF16)
        dyn_ref[...] = dmi[:, AW:]

    return _pcall(
        body, name="ffn_bwd", grid=(s // tb,),
        in_specs=[_rows(tb, D), _rows(tb, 2 * DFF), _rows(tb, D), _rows(tb, D), _const((8, D)), _const((1, D)),
                  _const((2 * DFF, D)), _const((DFF, D)), _const((D, D))],
        out_specs=[_rows(tb, D), _rows(tb, D), _rows(tb, 2 * DFF), _rows(tb, D), _rows(tb, AW), _rows(tb, SW),
                   pl.BlockSpec((8, D), lambda i: (0, 0))],
        out_shape=[_sds((s, D), F32), _sds((s, D), BF16), _sds((s, 2 * DFF), BF16), _sds((s, D), BF16),
                   _sds((s, AW), BF16), _sds((s, SW), F32), _sds((8, D), F32)],
        compiler_params=_params(),
    )(dx3, gu, x2, mixout, mod8, n2w, w_gu, w_down, w_out)


def _ssd_bwd_block(i, n, *refs):
    def run(dyn_ref, y_ref, z_ref, x_ref, u_ref, dt_ref, acs_ref, dm_ref, cw_ref, sp_ref, nw_ref,
            hs_ref, e64_ref, e128_ref, dzxd_ref, sm_ref, dh_scr, dun_scr):
        @pl.when(i == 0)
        def _():
            dh_scr[...] = jnp.zeros_like(dh_scr)
            dun_scr[...] = jnp.zeros_like(dun_scr)
            sm_ref[...] = jnp.zeros_like(sm_ref)

        u, dt, acs = u_ref[...], dt_ref[...], acs_ref[...]
        sg_u = _sigmoid(u)
        xc = u * sg_u
        a_neg = -jnp.exp(sp_ref[1:2, :])
        ck = _SsdChunk(xc, dt, acs, sp_ref[...], e64_ref[...], e128_ref[...],
                       decay=[dm_ref[0, g * HPG * LB:(g + 1) * HPG * LB, :] for g in range(2)])
        h_all = hs_ref[0]
        dh_all = dh_scr[...]
        riota = lax.broadcasted_iota(jnp.int32, (LB, LB), 0)
        lane1 = lax.broadcasted_iota(jnp.int32, (1, LB), 1)

        z = z_ref[...]
        y = y_ref[...]
        sgz = _sigmoid(z)
        sz = z * sgz
        yz = y * sz
        nwv = nw_ref[...]
        dyn_v = dyn_ref[...]
        dyhat = dyn_v * nwv
        yhat_parts, dyz_parts = [], []
        for g in range(2):
            gs = slice(g * 256, (g + 1) * 256)
            t = yz[:, gs]
            rg = lax.rsqrt(jnp.mean(t * t, axis=-1, keepdims=True) + EPS)
            yh = t * rg
            dyh = dyhat[:, gs]
            yhat_parts.append(yh)
            dyz_parts.append(rg * (dyh - yh * jnp.mean(dyh * yh, axis=-1, keepdims=True)))
        yhat = jnp.concatenate(yhat_parts, axis=1)
        dyz = jnp.concatenate(dyz_parts, axis=1)
        sm_ref[5:6, 0:SW] += jnp.sum(dyn_v * yhat, axis=0, keepdims=True)
        dy = dyz * sz
        dzxd_ref[:, 0:SW] = (dyz * y * sgz * (1.0 + z * (1.0 - sgz))).astype(BF16)

        cat = lambda parts: jnp.concatenate(parts, axis=1)
        dxs, dbs, dcs, dhp, g_cat, de_x, ddte_x, ddt_x, ddsk_x = ([] for _ in range(9))
        dacs_t = jnp.zeros((LB, LB), F32)
        hsum = jnp.zeros((1, LB), F32)
        for g in range(2):
            gl = slice(g * GW, (g + 1) * GW)
            xs_g, xdt, bgb, cgb = ck.xs[g], ck.xdt[g], ck.bb[g], ck.cb16[g]
            m_st, dm_st = ck.m_st[g], ck.dm_st[g]
            dt_x, e_x, dte_x = ck.dt_x[:, gl], ck.e_x[:, gl], ck.dte_x[:, gl]
            xdtb = xdt.astype(BF16)
            hg, dhn = h_all[gl, :], dh_all[gl, :]
            hb, dhnb = hg.astype(BF16), dhn.astype(BF16)
            dy_g = dy[:, gl]
            ddsk_x.append(jnp.sum(dy_g * xs_g, axis=0, keepdims=True))
            dy_bd = ck.block_diag(dy_g).astype(BF16)
            dm4 = _dot_nt(dy_bd, xdtb)
            dxdt = _dot_tn(m_st.astype(BF16), dy_bd)
            gmat = dm4 * m_st
            dcbm = dm4 * dm_st
            dcb = dcbm[0:LB] + dcbm[LB:2 * LB] + dcbm[2 * LB:3 * LB] + dcbm[3 * LB:4 * LB]
            g_cat.append(cat([gmat[hh * LB:(hh + 1) * LB, :] for hh in range(HPG)]))
            for hh in range(HPG):
                j = HPG * g + hh
                col_sum = jnp.sum(gmat[hh * LB:(hh + 1) * LB, :], axis=0, keepdims=True)
                dacs_t = dacs_t - jnp.where(riota == j, col_sum, 0.0)
                hsl = slice(hh * HD, (hh + 1) * HD)
                hsum = hsum + jnp.where(lane1 == j, jnp.sum(dhn[hsl, :] * hg[hsl, :]), 0.0)
            dchb = (dy_g * e_x).astype(BF16)
            dcg = _dot(dchb, hb)
            dh_prev = _dot_tn(dchb, cgb)
            de_x.append(dy_g * _dot_nt(cgb, hb))
            dxs_s = _dot_nt(bgb, dhnb)
            dbg = _dot((xdt * dte_x).astype(BF16), dhnb)
            dxdt = dxdt + dxs_s * dte_x
            ddte_x.append(dxs_s * xdt)
            dhp.append(dhn * ck.elast_rows(g) + dh_prev)
            dxs.append(dy_g * ck.dsk_x[:, gl] + dxdt * dt_x)
            ddt_x.append(dxdt * xs_g)
            dcbb = dcb.astype(BF16)
            dbs.append(dbg + _dot_tn(dcbb, cgb))
            dcs.append(dcg + _dot(dcbb, bgb))
        dh_scr[...] = jnp.concatenate(dhp, axis=0)
        red = _dot_nt_sel(jnp.concatenate([cat(de_x), cat(ddte_x), cat(ddt_x)], axis=0), ck.e64, 1)
        de_c, ddte_c, ddt_c = red[0:LB], red[LB:2 * LB], red[2 * LB:3 * LB]
        ddsk = _dot_nt_sel(jnp.broadcast_to(cat(ddsk_x), (SUB, NH * HD)), ck.e64, 2)[0:1, :]
        t1 = ddte_c * ck.dte_all
        dalast = jnp.sum(t1, axis=0, keepdims=True) + hsum * ck.elast
        dacs = (_dot_nt_sel(cat(g_cat), ck.e128, 2) + de_c * ck.e_all - t1 + jnp.transpose(dacs_t)
                + jnp.where(riota == LB - 1, dalast, 0.0))
        da = _cumsum_rows(dacs, True)
        ddt = ddt_c + da * a_neg
        da_log = jnp.sum(da * dt, axis=0, keepdims=True) * a_neg
        ddtr = ddt * (1.0 - jnp.exp(-dt))
        dzxd_ref[:, SW + D:ZXD] = ddtr.astype(BF16)
        sm_ref[6:7, 0:LB] += jnp.sum(ddtr, axis=0, keepdims=True)
        sm_ref[6:7, LB:2 * LB] += da_log
        sm_ref[6:7, 2 * LB:3 * LB] += ddsk

        du = cat(dxs + dbs + dcs) * (sg_u * (1.0 + u * (1.0 - sg_u)))
        xv = x_ref[...]
        adv = [du] + _conv_advances(du, dun_scr[...])
        sm_ref[0:1, :] += jnp.sum(du, axis=0, keepdims=True)
        dxbc = cw_ref[CONVK - 1:CONVK, :] * du
        for j in range(CONVK):
            sm_ref[CONVK - j:CONVK + 1 - j, :] += jnp.sum(adv[j] * xv, axis=0, keepdims=True)
            if j:
                dxbc = dxbc + cw_ref[CONVK - 1 - j:CONVK - j, :] * adv[j]
        dun_scr[...] = du[0:SUB, :]
        dzxd_ref[:, SW:SW + D] = dxbc.astype(BF16)

    run(*refs)


def _attn_bwd_block(i, n, q_ref, kp_ref, kc_ref, vp_ref, vc_ref, o_ref, do_ref, cos_ref, sin_ref, pr_ref, ps_ref,
                    dq_ref, dkv_ref, ds_ref, ck_scr, cv_scr):
    @pl.when(i == 0)
    def _():
        ds_ref[...] = jnp.zeros_like(ds_ref)
        ck_scr[...] = jnp.zeros_like(ck_scr)
        cv_scr[...] = jnp.zeros_like(cv_scr)

    qv, ov, dov = q_ref[...], o_ref[...], do_ref[...]
    kcat = jnp.concatenate([kp_ref[...], kc_ref[...]], axis=0)
    vcat = jnp.concatenate([vp_ref[...], vc_ref[...]], axis=0)
    upper = _upper_mask()
    srow = lax.broadcasted_iota(jnp.int32, (8, LB), 0)
    slane = lax.broadcasted_iota(jnp.int32, (8, LB), 1)
    dsink = jnp.zeros((8, LB), F32)
    dq_g, dk_g, dv_g = [], [], []
    for g in range(NQ // QPG):
        sl = slice(g * HD, (g + 1) * HD)
        qg = _stack_heads(qv, g)
        dog = _stack_heads(dov, g)
        rows = slice(g * QPG * LB, (g + 1) * QPG * LB)
        probs = pr_ref[0, rows, :].astype(F32)
        psink = ps_ref[0, rows, :].astype(F32)
        delta = _row_sums_wide(dog.astype(F32) * _stack_heads(ov, g).astype(F32), 2)
        dsc = probs * (_band(upper, _dot_nt(dog, vcat[0:LB, sl]), _dot_nt(dog, vcat[LB:2 * LB, sl])) - delta)
        sink_terms = (psink * delta)[:, 0:1]
        for hh in range(QPG):
            dsink = dsink - jnp.where((srow == QPG * g + hh) & (slane == 0),
                                      jnp.sum(sink_terms[hh * LB:(hh + 1) * LB, :]), 0.0)
        ds_p = jnp.where(upper, dsc, 0.0).astype(BF16)
        ds_c = jnp.where(upper, 0.0, dsc).astype(BF16)
        dq_g.append((_dot(ds_p, kcat[0:LB, sl]) + _dot(ds_c, kcat[LB:2 * LB, sl])) * ATT_SCALE)
        dk_g.append(jnp.concatenate([_dot_tn(ds_p, qg), _dot_tn(ds_c, qg)], axis=0) * ATT_SCALE)
        dv_g.append(jnp.concatenate([_dot_tn(jnp.where(upper, probs, 0.0).astype(BF16), dog),
                                     _dot_tn(jnp.where(upper, 0.0, probs).astype(BF16), dog)], axis=0))
    ds_ref[...] += dsink
    cs = cos_ref[...]
    sn = sin_ref[...]
    dk2 = jnp.concatenate(dk_g, axis=1)
    dv2 = jnp.concatenate(dv_g, axis=1)
    for a, tile in enumerate(_unstack_heads(dq_g)):
        dq_ref[:, a * LB:(a + 1) * LB] = _rope(tile, cs, sn, True).astype(BF16)
    dkv_ref[:, 0:KVW] = _rope(ck_scr[...] + dk2[LB:2 * LB, :], cs, sn, True).astype(BF16)
    dkv_ref[:, KVW:2 * KVW] = (cv_scr[...] + dv2[LB:2 * LB, :]).astype(BF16)
    ck_scr[...] = dk2[0:LB, :]
    cv_scr[...] = dv2[0:LB, :]


def _mixer_bwd(q, k, v, o, do, cos, sin, probs, psink, dyn, y, z, xbc, u, dtv, acs, decay, conv_w8, ssm_p, nw, hs,
               scatters):
    s = q.shape[0]
    nb = s // LB
    bps = _blocks_per_step(nb)
    nsteps = nb // bps
    tl = bps * LB
    cur = lambda i: (nsteps - 1 - i, 0)
    prev = lambda i: (jnp.maximum((nsteps - 1 - i) * bps - 1, 0), 0)
    n_in = 25
    items, ex_shapes, n_g = _exchange_items([], scatters)
    ne = len(items)
    e64, e128 = _head_expanders()
    stack = pl.BlockSpec((bps, NH * LB, LB), lambda i: (nsteps - 1 - i, 0, 0))

    def body(*refs):
        i = pl.program_id(0)
        (q_ref, kp_ref, kc_ref, vp_ref, vc_ref, o_ref, do_ref, cos_ref, sin_ref, pr_ref, ps_ref,
         dyn_ref, y_ref, z_ref, x_ref, u_ref, dt_ref, acs_ref, dm_ref, cw_ref, sp_ref, nw_ref,
         hs_ref, e64_ref, e128_ref) = refs[:n_in]
        ex_in = refs[n_in:n_in + ne]
        dp_ref, ds_ref, sm_ref = refs[n_in + ne:n_in + ne + 3]
        ex_out = refs[n_in + ne + 3:n_in + 2 * ne + 3]
        ck_scr, cv_scr, dh_scr, dun_scr = refs[n_in + 2 * ne + 3:n_in + 2 * ne + 7]
        sems = refs[n_in + 2 * ne + 7:]

        @pl.when(i == 0)
        def _():
            _Exchange(n_g, ex_in, ex_out, sems).start()

        for back in range(bps):
            sub = bps - 1 - back
            step = i * bps + back
            blk = (nsteps - 1 - i) * bps + sub
            r = slice(sub * LB, (sub + 1) * LB)
            before = slice((sub - 1) * LB, sub * LB)
            one = slice(sub, sub + 1)
            _attn_bwd_block(step, blk, q_ref.at[r], kp_ref if sub == 0 else kc_ref.at[before], kc_ref.at[r],
                            vp_ref if sub == 0 else vc_ref.at[before], vc_ref.at[r], o_ref.at[r], do_ref.at[r],
                            cos_ref.at[r], sin_ref.at[r], pr_ref.at[one], ps_ref.at[one],
                            dp_ref.at[r, O_Q:O_K], dp_ref.at[r, O_K:O_Z], ds_ref, ck_scr, cv_scr)
            _ssd_bwd_block(step, blk, dyn_ref.at[r], y_ref.at[r], z_ref.at[r], x_ref.at[r], u_ref.at[r], dt_ref.at[r],
                           acs_ref.at[r], dm_ref.at[one], cw_ref, sp_ref, nw_ref,
                           hs_ref.at[one], e64_ref, e128_ref, dp_ref.at[r, O_Z:INP], sm_ref, dh_scr, dun_scr)

        @pl.when(i == nsteps - 1)
        def _():
            _Exchange(n_g, ex_in, ex_out, sems).finish()

    any_spec = pl.BlockSpec(memory_space=pl.ANY)
    outs = _pcall(
        body, name="mixer_bwd", grid=(nsteps,),
        in_specs=[pl.BlockSpec((tl, AW), cur), pl.BlockSpec((LB, KVW), prev), pl.BlockSpec((tl, KVW), cur),
                  pl.BlockSpec((LB, KVW), prev), pl.BlockSpec((tl, KVW), cur), pl.BlockSpec((tl, AW), cur),
                  pl.BlockSpec((tl, AW), cur), pl.BlockSpec((tl, LB), cur), pl.BlockSpec((tl, LB), cur),
                  stack, stack,
                  pl.BlockSpec((tl, SW), cur), pl.BlockSpec((tl, SW), cur), pl.BlockSpec((tl, SW), cur),
                  pl.BlockSpec((tl, D), cur), pl.BlockSpec((tl, D), cur), pl.BlockSpec((tl, LB), cur),
                  pl.BlockSpec((tl, LB), cur), stack,
                  _const((8, D)), _const((8, LB)), _const((1, SW)),
                  pl.BlockSpec((bps, NH * HD, NST), lambda i: (nsteps - 1 - i, 0, 0)),
                  _const(e64.shape), _const(e128.shape)] + [any_spec] * ne,
        out_specs=[pl.BlockSpec((tl, INP), cur), pl.BlockSpec((8, LB), lambda i: (0, 0)),
                   pl.BlockSpec((8, D), lambda i: (0, 0))] + [any_spec] * ne,
        out_shape=[_sds((s, INP), BF16), _sds((8, LB), F32), _sds((8, D), F32)] + ex_shapes,
        scratch_shapes=[pltpu.VMEM((LB, KVW), F32), pltpu.VMEM((LB, KVW), F32),
                        pltpu.VMEM((NH * HD, NST), F32), pltpu.VMEM((SUB, D), F32)]
        + _exchange_sems(ne),
        compiler_params=_params(),
    )(q, k, k, v, v, o, do, cos, sin, probs, psink, dyn, y, z, xbc, u, dtv, acs, decay, conv_w8, ssm_p, nw, hs,
      e64, e128, *items)
    return outs[0], outs[1], outs[2], outs[3:]


def _inproj_bwd(dproj, x, dx2, mod8, n1w, w_in_t, scatters, smalls):
    s = x.shape[0]
    tt = min(512, s)
    nt = s // tt
    items, ex_shapes, n_g = _exchange_items([], scatters)
    ne = len(items)
    n_in = 10

    def body(*refs):
        dp_ref, x_ref, dx2_ref, mod_ref, nw_ref, w_ref, f_ref, b_ref, s_ref, k_ref = refs[:n_in]
        ex_in = refs[n_in:n_in + ne]
        gx_ref, sm_ref = refs[n_in + ne:n_in + 2 + ne]
        ex_out = refs[n_in + 2 + ne:n_in + 2 + 2 * ne]
        gpack_ref = refs[n_in + 2 + 2 * ne]
        pack_scr = refs[n_in + 3 + 2 * ne]
        sems = refs[n_in + 4 + 2 * ne:n_in + 7 + 2 * ne]
        pack_sems = refs[n_in + 7 + 2 * ne:]
        i = pl.program_id(0)

        @pl.when(i == 0)
        def _():
            sm_ref[...] = jnp.zeros_like(sm_ref)
            _Exchange(n_g, ex_in, ex_out, sems).start()

        w = w_ref[...]
        hr = tt // 2
        dh1 = [_dot(dp_ref[h * hr:(h + 1) * hr, :], w) for h in range(2)]
        sums = jnp.zeros((3, D), F32)
        for h in range(2):
            rows = slice(h * hr, (h + 1) * hr)
            dxn, d_shift, d_scale, d_w = _norm_mod_bwd(x_ref[rows, :], dh1[h], nw_ref[...], mod_ref[1:2, :])
            gx_ref[rows, :] = dx2_ref[rows, :] + dxn
            sums = sums + jnp.concatenate([d_shift, d_scale, d_w], axis=0)
        sm_ref[0:3, :] += sums

        @pl.when(i == nt - 1)
        def _():
            _pack_rows(f_ref, b_ref, s_ref, sm_ref, k_ref, pack_scr)
            small = _Exchange(1, [pack_scr], [gpack_ref], pack_sems)
            small.start()
            _Exchange(n_g, ex_in, ex_out, sems).finish()
            small.finish()

    any_spec = pl.BlockSpec(memory_space=pl.ANY)
    outs = _pcall(
        body, name="inproj_bwd", grid=(nt,),
        in_specs=[_rows(tt, INP), _rows(tt, D), _rows(tt, D), _const((8, D)), _const((1, D)), _const((INP, D)),
                  _const((8, D)), _const((8, D)), _const((8, D)), _const((8, LB))]
        + [any_spec] * ne,
        out_specs=[_rows(tt, D), pl.BlockSpec((8, D), lambda i: (0, 0))] + [any_spec] * (ne + 1),
        out_shape=[_sds((s, D), F32), _sds((8, D), F32)] + ex_shapes + [_sds((N_DEV, PACK_ROWS, D), F32)],
        scratch_shapes=[pltpu.VMEM((PACK_ROWS, D), F32)] + _exchange_sems(ne) + _exchange_sems(1),
        compiler_params=_params(),
    )(dproj, x, dx2, mod8, n1w, w_in_t, *smalls, *items)
    return outs[0], outs[2:2 + ne], outs[2 + ne]


def _wgrad(a, b, name):
    s, m = a.shape
    n = b.shape[1]
    tk = min(2048, s)
    wide = (1408, 1024, 512)
    tm = next((t for t in wide if m % t == 0), m)
    tn = n if n <= 2048 else _largest_divisor(n, wide)
    nk = s // tk

    def body(a_ref, b_ref, o_ref, acc):
        kk = pl.program_id(2)

        @pl.when(kk == 0)
        def _():
            acc[...] = jnp.zeros_like(acc)

        acc[...] += _dot_tn(a_ref[...], b_ref[...])

        @pl.when(kk == nk - 1)
        def _():
            o_ref[...] = acc[...].astype(BF16)

    return _pcall(
        body, name=name, grid=(m // tm, n // tn, nk),
        in_specs=[pl.BlockSpec((tk, tm), lambda i, j, kk: (kk, i)), pl.BlockSpec((tk, tn), lambda i, j, kk: (kk, j))],
        out_specs=pl.BlockSpec((tm, tn), lambda i, j, kk: (i, j)),
        out_shape=_sds((m, n), BF16),
        scratch_shapes=[pltpu.VMEM((tm, tn), F32)],
        compiler_params=_params(3),
    )(a, b)


PACK_ROWS = 24


def _pack_rows(f_ref, b_ref, s_ref, i_ref, k_ref, o_ref):
    o_ref[...] = jnp.zeros_like(o_ref)
    o_ref[0:2, :] = i_ref[0:2, :]
    o_ref[2:3, :] = b_ref[3:4, :]
    o_ref[3:5, :] = b_ref[0:2, :]
    o_ref[5:6, :] = f_ref[0:1, :]
    o_ref[6:7, :] = i_ref[2:3, :]
    o_ref[7:8, :] = b_ref[2:3, :]
    o_ref[8:9, :] = f_ref[1:2, :]
    o_ref[9:14, :] = s_ref[0:5, :]
    o_ref[14:15, :] = s_ref[5:6, :]
    o_ref[15:16, 0:3 * LB] = s_ref[6:7, 0:3 * LB]
    lane = lax.broadcasted_iota(jnp.int32, (1, LB), 1)
    sk = jnp.zeros((1, LB), F32)
    for h in range(NQ):
        sk = sk + jnp.where(lane == h, k_ref[h:h + 1, 0:1], 0.0)
    o_ref[15:16, 3 * LB:4 * LB] = sk
    o_ref[16:17, :] = f_ref[2:3, :]


def _exchange(gathers, scatters, name, two_level=False):
    items, shapes, n_g = _exchange_items(gathers, scatters)
    n = len(items)
    assert not (two_level and scatters)

    def body(*refs):
        ex = _Exchange(n_g, refs[:n], refs[n:2 * n], refs[2 * n:])
        if two_level:
            ex.gather_two_level()
        else:
            ex.start()
            ex.finish()

    any_spec = pl.BlockSpec(memory_space=pl.ANY)
    return _pcall(
        body, name=name, in_specs=[any_spec] * n, out_specs=[any_spec] * n, out_shape=shapes,
        scratch_shapes=_exchange_sems(n),
    )(*items)


def _exchange_items(gathers, scatters):
    items = list(gathers) + list(scatters)
    shapes = [_sds((N_DEV,) + a.shape, a.dtype) for a in gathers] + [_sds(a.shape, a.dtype) for a in scatters]
    return items, shapes, len(gathers)


def _exchange_sems(n):
    return [pltpu.SemaphoreType.DMA((n, N_DEV - 1)), pltpu.SemaphoreType.DMA((n, N_DEV - 1)),
            pltpu.SemaphoreType.DMA((n,))]


class _Exchange:
    def __init__(self, n_g, ins, outs, sems):
        self.n_g, self.ins, self.outs = n_g, ins, outs
        self.send_sems, self.recv_sems, self.loc_sems = sems
        xi, yi, ci = lax.axis_index("x"), lax.axis_index("y"), lax.axis_index("c")
        self.me = 4 * xi + 2 * yi + ci
        self.peers = []
        for r in range(1, N_DEV):
            px = 1 - xi if r & 4 else xi
            py = 1 - yi if r & 2 else yi
            pc = 1 - ci if r & 1 else ci
            self.peers.append(((px, py, pc), 4 * px + 2 * py + pc))

    def _copy(self, t, r, landing):
        dev, peer = self.peers[r]
        src = self.ins[t] if t < self.n_g else self.ins[t].at[peer]
        return pltpu.make_async_remote_copy(
            src_ref=src, dst_ref=self.outs[t].at[landing], send_sem=self.send_sems.at[t, r],
            recv_sem=self.recv_sems.at[t, r], device_id=dev, device_id_type=pl.DeviceIdType.MESH)

    def _local(self, t):
        src = self.ins[t] if t < self.n_g else self.ins[t].at[self.me]
        return pltpu.make_async_copy(src, self.outs[t].at[self.me], self.loc_sems.at[t])

    def start(self):
        for t in range(len(self.ins)):
            self._local(t).start()
            for r in range(N_DEV - 1):
                self._copy(t, r, self.me).start()

    def finish(self):
        n = len(self.ins)
        for t in range(n):
            for r in range(N_DEV - 1):
                self._copy(t, r, self.peers[r][1]).wait_recv()
        for t in range(n):
            for r in range(N_DEV - 1):
                self._copy(t, r, self.me).wait_send()
            self._local(t).wait()

    def gather_two_level(self):
        self.two_level_start()
        self.two_level_relay()
        self.two_level_finish()

    DIRECT = (0, 1, 3, 5)

    def two_level_start(self):
        for t in range(len(self.ins)):
            self._local(t).start()
            for r in self.DIRECT:
                self._copy(t, r, self.me).start()

    def _relay(self, t, r):
        peer = self.peers[r][1]
        return pltpu.make_async_remote_copy(
            src_ref=self.outs[t].at[peer], dst_ref=self.outs[t].at[peer], send_sem=self.send_sems.at[t, r + 1],
            recv_sem=self.recv_sems.at[t, r + 1], device_id=self.peers[0][0], device_id_type=pl.DeviceIdType.MESH)

    def two_level_relay(self):
        for t in range(len(self.ins)):
            for r in self.DIRECT[1:]:
                self._copy(t, r, self.peers[r][1]).wait_recv()
                self._relay(t, r).start()

    def two_level_finish(self):
        n = len(self.ins)
        for t in range(n):
            for r in (0, 2, 4, 6):
                self._copy(t, r, self.peers[r][1]).wait_recv()
        for t in range(n):
            for r in self.DIRECT:
                self._copy(t, r, self.me).wait_send()
            for r in self.DIRECT[1:]:
                self._relay(t, r).wait_send()
            self._local(t).wait()


def _prologue(c8, w_in_tb, cw8, w_cols, b_cols):
    ncol = w_cols.shape[1]

    def body(c_ref, win_ref, cw_ref, w_ref, b_ref, gc_ref, gin_ref, gcw_ref, gmod_ref, call_scr, mod_scr, loc_sem,
             *sems):
        big = _Exchange(2, [win_ref, cw_ref], [gin_ref, gcw_ref], sems[0:3])
        big.two_level_start()
        small = _Exchange(1, [c_ref], [gc_ref], sems[3:6])
        small.start()
        small.finish()
        landed = pltpu.make_async_copy(gc_ref, call_scr, loc_sem)
        landed.start()
        landed.wait()
        cv = call_scr[:, 0, :]
        sc = (cv * _sigmoid(cv)).astype(BF16)
        mod_scr[...] = _dot(sc, w_ref[...].astype(BF16)) + b_ref[...]
        mods = _Exchange(1, [mod_scr], [gmod_ref], sems[6:9])
        mods.start()
        mods.finish()
        big.two_level_relay()
        big.two_level_finish()

    any_spec = pl.BlockSpec(memory_space=pl.ANY)
    vmem_spec = pl.BlockSpec(memory_space=pltpu.VMEM)
    return _pcall(
        body, name="prologue", in_specs=[any_spec, any_spec, any_spec, vmem_spec, vmem_spec],
        out_specs=[any_spec] * 4,
        out_shape=[_sds((N_DEV,) + c8.shape, F32), _sds((N_DEV,) + w_in_tb.shape, BF16),
                   _sds((N_DEV,) + cw8.shape, F32), _sds((N_DEV, N_DEV, ncol), F32)],
        scratch_shapes=[pltpu.VMEM((N_DEV,) + c8.shape, F32), pltpu.VMEM((N_DEV, ncol), F32),
                        pltpu.SemaphoreType.DMA] + _exchange_sems(2) + _exchange_sems(1) + _exchange_sems(1),
        compiler_params=pltpu.CompilerParams(vmem_limit_bytes=VMEM_LIMIT),
    )(c8, w_in_tb, cw8, w_cols, b_cols)


def _adamw(w, g, m, v):
    m2 = ADAM_B1 * m + (1.0 - ADAM_B1) * g
    v2 = ADAM_B2 * v + (1.0 - ADAM_B2) * (g * g)
    m_hat = m2 / (1.0 - ADAM_B1 ** ADAM_STEP)
    v_hat = v2 / (1.0 - ADAM_B2 ** ADAM_STEP)
    delta = -ADAM_LR * (m_hat / (jnp.sqrt(v_hat) + ADAM_EPS) + ADAM_WD * w)
    return delta, m2, v2


def _sum_adamw(parts, w, m, v, name):
    rws, cols = w.shape
    tr = next((t for t in (256, 176, 128) if rws % t == 0), rws)

    def body(p_ref, w_ref, m_ref, v_ref, g_ref, d_ref, mo_ref, vo_ref):
        g = p_ref[0].astype(F32)
        for dev in range(1, N_DEV):
            g = g + p_ref[dev].astype(F32)
        g_ref[...] = g
        d_ref[...], mo_ref[...], vo_ref[...] = _adamw(w_ref[...], g, m_ref[...], v_ref[...])

    blk = pl.BlockSpec((tr, cols), lambda i: (i, 0))
    return _pcall(
        body, name=name, grid=(rws // tr,),
        in_specs=[pl.BlockSpec((N_DEV, tr, cols), lambda i: (0, i, 0)), blk, blk, blk],
        out_specs=[blk] * 4, out_shape=[_sds((rws, cols), F32)] * 4, compiler_params=_params(),
    )(parts, w, m, v)


def _wada_adamw(c_all, dmod_cols, w, m, v):
    rws, cols = w.shape
    tr = 256

    def body(c_ref, dm_ref, w_ref, m_ref, v_ref, g_ref, d_ref, mo_ref, vo_ref):
        cv = c_ref[...]
        sc = (cv * _sigmoid(cv)).astype(BF16)
        g = _dot_tn(sc, dm_ref[...].astype(BF16))
        g_ref[...] = g
        d_ref[...], mo_ref[...], vo_ref[...] = _adamw(w_ref[...], g, m_ref[...], v_ref[...])

    blk = pl.BlockSpec((tr, cols), lambda i: (i, 0))
    return _pcall(
        body, name="wada_adamw", grid=(rws // tr,),
        in_specs=[pl.BlockSpec((N_DEV, tr), lambda i: (0, i)), pl.BlockSpec((N_DEV, cols), lambda i: (0, 0)),
                  blk, blk, blk],
        out_specs=[blk] * 4, out_shape=[_sds((rws, cols), F32)] * 4, compiler_params=_params(),
    )(c_all, dmod_cols, w, m, v)


def _small_reduce(packs):
    def body(p_ref, o_ref):
        tot = p_ref[0]
        for dev in range(1, N_DEV):
            tot = tot + p_ref[dev]
        o_ref[...] = tot
        o_ref[16:17, :] = jnp.zeros((1, D), F32) + (0.5 / D) * jnp.sum(tot[16:17, :])

    return _pcall(body, name="small_reduce", out_shape=_sds((PACK_ROWS, D), F32))(packs)


def _adamw_many(ws, gs, ms, vs):
    k = len(ws)

    def body(*refs):
        for i in range(k):
            w_ref, g_ref, m_ref, v_ref = refs[i], refs[k + i], refs[2 * k + i], refs[3 * k + i]
            d_ref, mo_ref, vo_ref = refs[4 * k + i], refs[5 * k + i], refs[6 * k + i]
            d_ref[...], mo_ref[...], vo_ref[...] = _adamw(w_ref[...], g_ref[...], m_ref[...], v_ref[...])

    shp = [_sds(w.shape, F32) for w in ws]
    outs = _pcall(body, name="adamw_small", out_shape=shp * 3)(*ws, *gs, *ms, *vs)
    return outs[:k], outs[k:2 * k], outs[2 * k:]


def kernel(x, c, positions, w_ada, b_ada, norm1_w, w_in, conv_w, conv_b, dt_bias, a_log, d_skip, attn_sinks, ssm_norm_w, w_out, norm2_w, w_gate_up, w_down, final_norm_w, loss_target, m_w_ada, m_b_ada, m_norm1_w, m_w_in, m_conv_w, m_conv_b, m_dt_bias, m_a_log, m_d_skip, m_attn_sinks, m_ssm_norm_w, m_w_out, m_norm2_w, m_w_gate_up, m_w_down, m_final_norm_w, v_w_ada, v_b_ada, v_norm1_w, v_w_in, v_conv_w, v_conv_b, v_dt_bias, v_a_log, v_d_skip, v_attn_sinks, v_ssm_norm_w, v_w_out, v_norm2_w, v_w_gate_up, v_w_down, v_final_norm_w):
    s = x.shape[1]
    me = 4 * lax.axis_index("x") + 2 * lax.axis_index("y") + lax.axis_index("c")
    ada_cols = N_MOD * D // N_DEV

    c8 = jnp.pad(c, ((0, 7), (0, 0)))
    cw8 = jnp.pad(conv_w[0], ((0, 8 - CONVK), (0, 0)))
    w_in_t, m_w_in_t, v_w_in_t = jnp.transpose(w_in[0]), jnp.transpose(m_w_in[0]), jnp.transpose(v_w_in[0])
    w_gu_t, m_w_gu_t, v_w_gu_t = (jnp.transpose(w_gate_up[0]), jnp.transpose(m_w_gate_up[0]),
                                  jnp.transpose(v_w_gate_up[0]))
    b_cols = lax.dynamic_slice(b_ada, (0, me * ada_cols), (1, ada_cols))
    g_c, g_in, g_cw, g_mod = _prologue(c8, w_in_t.astype(BF16), cw8, w_ada[0], b_cols)
    c_all = g_c[:, 0, :]
    w_in_f = jnp.pad(g_in.reshape(IN_PROJ, D), ((0, INP - IN_PROJ), (0, 0)))
    conv_w8 = jnp.transpose(g_cw, (1, 0, 2)).reshape(8, D)
    mod = lax.dynamic_index_in_dim(g_mod, me, axis=1, keepdims=False).reshape(N_MOD, D)
    mod8 = jnp.pad(mod, ((0, 8 - N_MOD), (0, 0)))

    half = HD // 2
    inv_freq = ROPE_THETA ** (-jnp.arange(half, dtype=F32) / half)
    invf = jnp.tile(inv_freq, LB // half).reshape(1, LB)
    lanes = lambda a: jnp.pad(a, ((0, 0), (0, LB - a.shape[1])))
    ssm_p = jnp.pad(jnp.concatenate([lanes(dt_bias), lanes(a_log), lanes(d_skip)], axis=0), ((0, 5), (0, 0)))
    sinks8 = jnp.broadcast_to(attn_sinks.reshape(NQ, 1), (NQ, LB))

    xs, tgt, fnw = x[0], loss_target[0], final_norm_w.reshape(1, D)

    q, k, v, z, xbc, dtr, h1, cos, sin = _inproj_fwd(xs, positions[0].reshape(s, 1), invf, mod8, norm1_w, w_in_f)
    (attn, yn, y, hs, conv_u, dtv, acs, decay, probs, psink), (g_out, g_gu, g_down) = _mixer_fwd(
        q, k, v, sinks8, xbc, conv_w8, conv_b, dtr, ssm_p, z, ssm_norm_w,
        [w_out[0].astype(BF16), w_gu_t.astype(BF16), w_down[0].astype(BF16)])
    w_out_f = g_out.reshape(D, D)
    w_gu_f = g_gu.reshape(2 * DFF, D)
    w_down_f = g_down.reshape(DFF, D)
    x2, h2, mo, mix, gu, act, dx3, sm_f = _outproj_ffn_fwd_loss(attn, yn, xs, tgt, mod8, norm2_w, fnw, w_out_f, w_gu_f,
                                                                 w_down_f)

    dx2, dff, dgu, dmix, dattn, dyn, sm_b = _ffn_bwd(dx3, gu, x2, mo, mod8, norm2_w, w_gu_f, w_down_f, w_out_f)
    p_gu = _wgrad(dgu, h2, "wgrad_gate_up").reshape(N_DEV, 2 * DFF // N_DEV, D)
    p_down = _wgrad(act, dff, "wgrad_down").reshape(N_DEV, DFF // N_DEV, D)
    p_out = _wgrad(mix, dmix, "wgrad_out").reshape(N_DEV, D // N_DEV, D)
    dproj, dsink, sm_s, (r_gu, r_down) = _mixer_bwd(
        q, k, v, attn, dattn, cos, sin, probs, psink, dyn, y, z, xbc, conv_u, dtv, acs, decay, conv_w8, ssm_p,
        ssm_norm_w, hs, [p_gu, p_down])
    p_in = _wgrad(dproj, h1, "wgrad_in")[:IN_PROJ].reshape(N_DEV, IN_PROJ // N_DEV, D)
    gx, (r_in, r_out), g_pack = _inproj_bwd(dproj, xs, dx2, mod8, norm1_w, w_in_f, [p_in, p_out],
                                            (sm_f, sm_b, sm_s, dsink))

    tot = _small_reduce(g_pack)
    loss = tot[16, 0]
    dmod_all = g_pack[:, 0:N_MOD, :].reshape(N_DEV, N_MOD * D)
    dmod_cols = lax.dynamic_slice(dmod_all, (0, me * ada_cols), (N_DEV, ada_cols))

    big = {
        "w_ada": _wada_adamw(c_all, dmod_cols, w_ada[0], m_w_ada[0], v_w_ada[0]),
        "w_in": [jnp.transpose(t) for t in _sum_adamw(r_in, w_in_t, m_w_in_t, v_w_in_t, "adamw_in")],
        "w_out": _sum_adamw(r_out, w_out[0], m_w_out[0], v_w_out[0], "adamw_out"),
        "w_gate_up": [jnp.transpose(t) for t in _sum_adamw(r_gu, w_gu_t, m_w_gu_t, v_w_gu_t, "adamw_gate_up")],
        "w_down": _sum_adamw(r_down, w_down[0], m_w_down[0], v_w_down[0], "adamw_down"),
    }
    small_names = ["b_ada", "norm1_w", "conv_w", "conv_b", "dt_bias", "a_log", "d_skip", "attn_sinks", "ssm_norm_w",
                   "norm2_w", "final_norm_w"]
    row15 = tot[15:16, :]
    small_g = {
        "b_ada": tot[0:N_MOD, :].reshape(1, N_MOD * D),
        "norm1_w": tot[6:7, :],
        "conv_w": lax.dynamic_slice(tot[10:14, :], (0, me * (D // N_DEV)), (CONVK, D // N_DEV)),
        "conv_b": tot[9:10, :],
        "dt_bias": row15[:, 0:NH],
        "a_log": row15[:, LB:LB + NH],
        "d_skip": row15[:, 2 * LB:2 * LB + NH],
        "attn_sinks": row15[:, 3 * LB:3 * LB + NQ],
        "ssm_norm_w": tot[14:15, 0:SW],
        "norm2_w": tot[7:8, :],
        "final_norm_w": tot[8:9, :],
    }
    small_w = {"b_ada": b_ada, "norm1_w": norm1_w, "conv_w": conv_w[0], "conv_b": conv_b, "dt_bias": dt_bias,
               "a_log": a_log, "d_skip": d_skip, "attn_sinks": attn_sinks, "ssm_norm_w": ssm_norm_w,
               "norm2_w": norm2_w, "final_norm_w": final_norm_w.reshape(1, D)}
    small_m = {"b_ada": m_b_ada, "norm1_w": m_norm1_w, "conv_w": m_conv_w[0], "conv_b": m_conv_b,
               "dt_bias": m_dt_bias, "a_log": m_a_log, "d_skip": m_d_skip, "attn_sinks": m_attn_sinks,
               "ssm_norm_w": m_ssm_norm_w, "norm2_w": m_norm2_w, "final_norm_w": m_final_norm_w.reshape(1, D)}
    small_v = {"b_ada": v_b_ada, "norm1_w": v_norm1_w, "conv_w": v_conv_w[0], "conv_b": v_conv_b,
               "dt_bias": v_dt_bias, "a_log": v_a_log, "d_skip": v_d_skip, "attn_sinks": v_attn_sinks,
               "ssm_norm_w": v_ssm_norm_w, "norm2_w": v_norm2_w, "final_norm_w": v_final_norm_w.reshape(1, D)}
    s_d, s_m, s_v = _adamw_many([small_w[k] for k in small_names], [small_g[k] for k in small_names],
                                [small_m[k] for k in small_names], [small_v[k] for k in small_names])

    order = ["w_ada", "b_ada", "norm1_w", "w_in", "conv_w", "conv_b", "dt_bias", "a_log", "d_skip", "attn_sinks",
             "ssm_norm_w", "w_out", "norm2_w", "w_gate_up", "w_down", "final_norm_w"]
    lead = {"w_ada", "w_in", "conv_w", "w_out", "w_gate_up", "w_down"}
    grads, deltas, new_m, new_v = [], [], [], []
    for name in order:
        if name in big:
            g, d, m2, v2 = big[name]
        else:
            i = small_names.index(name)
            g, d, m2, v2 = small_g[name], s_d[i], s_m[i], s_v[i]
        if name in lead:
            g, d, m2, v2 = g[None], d[None], m2[None], v2[None]
        if name == "final_norm_w":
            g, d, m2, v2 = g.reshape(D), d.reshape(D), m2.reshape(D), v2.reshape(D)
        grads.append(g)
        deltas.append(d)
        new_m.append(m2)
        new_v.append(v2)
    return (loss, gx[None], *grads, *deltas, *new_m, *new_v)
```

```python
import functools
import math

import jax
import jax.numpy as jnp
from jax import lax
from jax.experimental import pallas as pl
from jax.experimental.pallas import tpu as pltpu

F32 = jnp.float32
BF16 = jnp.bfloat16

N_DEV = 8
D = 1024
HD = 64
NQ = 8
AW = 512
KVW = 128
SW = 512
NST = 128
NH = 8
LB = 128
CONVK = 4
DFF = 2816
N_MOD = 6
IN_PROJ = 2312
INP = 2432
O_Q, O_K, O_V, O_Z, O_XBC, O_DT = 0, 512, 640, 768, 1280, 2304
ZXD = INP - O_Z
EPS = 1e-6
NEG = -1e30
ROPE_THETA = 10000.0
VMEM_LIMIT = 56 * 1024 * 1024

ADAM_LR = 0.001
ADAM_B1 = 0.9
ADAM_B2 = 0.999
ADAM_EPS = 1e-08
ADAM_WD = 0.01
ADAM_STEP = 10

NT_DIMS = (((1,), (1,)), ((), ()))
TN_DIMS = (((0,), (0,)), ((), ()))


def _pcall(body, **kw):
    return pl.pallas_call(body, **kw)


def _sds(shape, dtype):
    return jax.ShapeDtypeStruct(shape, dtype)


def _params(n_grid=1):
    return pltpu.CompilerParams(dimension_semantics=("arbitrary",) * n_grid, vmem_limit_bytes=VMEM_LIMIT)


def _const(shape):
    return pl.BlockSpec(shape, lambda *_: (0,) * len(shape), pipeline_mode=pl.Buffered(1))


def _largest_divisor(n, candidates):
    for cand in candidates:
        if n % cand == 0:
            return cand
    raise ValueError(f"no tile in {candidates} divides {n}")


def _rows(t, w):
    return pl.BlockSpec((t, w), lambda i: (i, 0))


def _dot(a, b):
    return jnp.dot(a, b, preferred_element_type=F32)


def _dot_nt(a, b):
    return lax.dot_general(a, b, NT_DIMS, preferred_element_type=F32)


def _dot_tn(a, b):
    return lax.dot_general(a, b, TN_DIMS, preferred_element_type=F32)


def _sigmoid(v):
    return 1.0 / (1.0 + jnp.exp(-v))


def _softplus(v):
    return jnp.maximum(v, 0.0) + jnp.log1p(jnp.exp(-jnp.abs(v)))


def _rope_sign_mask(shape):
    lane = lax.broadcasted_iota(jnp.int32, shape, 1)
    return (lane % HD) < (HD // 2)


def _rope(t, cs, sn, inverse):
    r_dn = pltpu.roll(t, HD // 2, 1)
    r_up = pltpu.roll(t, LB - HD // 2, 1)
    first = _rope_sign_mask(t.shape)
    if inverse:
        rot = jnp.where(first, r_up, -r_dn)
    else:
        rot = jnp.where(first, -r_up, r_dn)
    return t * cs + rot * sn


def _norm_mod_fwd(xv, nw, shift, scale):
    r = lax.rsqrt(jnp.mean(xv * xv, axis=-1, keepdims=True) + EPS)
    xh = xv * r
    return (xh * nw) * (1.0 + scale) + shift


def _norm_mod_bwd(xv, dh, nw, scale):
    r = lax.rsqrt(jnp.mean(xv * xv, axis=-1, keepdims=True) + EPS)
    xh = xv * r
    xn = xh * nw
    d_shift = jnp.sum(dh, axis=0, keepdims=True)
    d_scale = jnp.sum(dh * xn, axis=0, keepdims=True)
    dxn = dh * (1.0 + scale)
    d_w = jnp.sum(dxn * xh, axis=0, keepdims=True)
    dxh = dxn * nw
    dx = r * (dxh - xh * jnp.mean(dxh * xh, axis=-1, keepdims=True))
    return dx, d_shift, d_scale, d_w


def _inproj_fwd(x, pos, invf, mod8, n1w, w_in):
    s = x.shape[0]
    tt = min(512, s)

    def body(x_ref, pos_ref, invf_ref, mod_ref, nw_ref, w_ref,
             q_ref, k_ref, v_ref, z_ref, xbc_ref, dtr_ref, h1_ref, cos_ref, sin_ref):
        h = _norm_mod_fwd(x_ref[...], nw_ref[...], mod_ref[0:1, :], mod_ref[1:2, :])
        hb = h.astype(BF16)
        h1_ref[...] = hb
        proj = _dot_nt(hb, w_ref[...])
        ang = pos_ref[...].astype(F32) * invf_ref[...]
        cs = jnp.cos(ang)
        sn = jnp.sin(ang)
        cos_ref[...] = cs
        sin_ref[...] = sn
        for a in range(AW // LB):
            q_ref[:, a * LB:(a + 1) * LB] = _rope(proj[:, O_Q + a * LB:O_Q + (a + 1) * LB], cs, sn, False).astype(BF16)
        k_ref[...] = _rope(proj[:, O_K:O_V], cs, sn, False).astype(BF16)
        v_ref[...] = proj[:, O_V:O_Z].astype(BF16)
        z_ref[...] = proj[:, O_Z:O_XBC]
        xbc_ref[...] = proj[:, O_XBC:O_DT]
        dtr_ref[...] = proj[:, O_DT:INP]

    return _pcall(
        body, name="inproj_fwd", grid=(s // tt,),
        in_specs=[_rows(tt, D), _rows(tt, 1), _const((1, LB)), _const((8, D)), _const((1, D)), _const((INP, D))],
        out_specs=[_rows(tt, AW), _rows(tt, KVW), _rows(tt, KVW), _rows(tt, SW), _rows(tt, D), _rows(tt, LB),
                   _rows(tt, D), _rows(tt, LB), _rows(tt, LB)],
        out_shape=[_sds((s, AW), BF16), _sds((s, KVW), BF16), _sds((s, KVW), BF16), _sds((s, SW), F32),
                   _sds((s, D), F32), _sds((s, LB), F32), _sds((s, D), BF16), _sds((s, LB), F32), _sds((s, LB), F32)],
        compiler_params=_params(),
    )(x, pos, invf, mod8, n1w, w_in)


QPG = 4
ATT_SCALE = 1.0 / math.sqrt(HD)


def _stack_heads(val, g):
    return jnp.concatenate([val[:, (QPG * g + hh) * HD:(QPG * g + hh + 1) * HD] for hh in range(QPG)], axis=0)


def _unstack_heads(groups):
    pieces = [grp[hh * LB:(hh + 1) * LB, :] for grp in groups for hh in range(QPG)]
    return [jnp.concatenate(pieces[2 * a:2 * a + 2], axis=1) for a in range(NQ // 2)]


def _upper_mask():
    row = lax.broadcasted_iota(jnp.int32, (QPG * LB, LB), 0)
    col = lax.broadcasted_iota(jnp.int32, (QPG * LB, LB), 1)
    return col > (row % LB)


def _sink_wide(sinks, g):
    return jnp.concatenate([jnp.broadcast_to(sinks[QPG * g + hh:QPG * g + hh + 1, 0:1], (LB, LB))
                            for hh in range(QPG)], axis=0)


def _row_sums_wide(v, terms):
    return _dot_sel(v, jnp.ones((v.shape[1], LB), BF16), terms)


def _band(upper, prev_part, cur_part):
    return jnp.where(upper, prev_part, cur_part)


def _attn_scores(n, qg, kcat, upper):
    sp = _dot_nt(qg, kcat[0:LB, :]) * ATT_SCALE
    sc = _dot_nt(qg, kcat[LB:2 * LB, :]) * ATT_SCALE
    return _band(upper, jnp.where(n > 0, sp, NEG), sc)


def _attn_softmax(comb, sink):
    m = jnp.maximum(jnp.max(comb, axis=-1, keepdims=True), sink)
    p = jnp.exp(comb - m)
    es = jnp.exp(sink - m)
    return p, es, _row_sums_wide(p, 1) + es


def _attn_fwd_block(n, q_ref, kp_ref, kc_ref, vp_ref, vc_ref, sink_ref, o_ref, pr_ref, ps_ref):
    qv = q_ref[...]
    kcat = jnp.concatenate([kp_ref[...], kc_ref[...]], axis=0)
    vcat = jnp.concatenate([vp_ref[...], vc_ref[...]], axis=0)
    sinks = sink_ref[...]
    upper = _upper_mask()
    outs = []
    for g in range(NQ // QPG):
        sl = slice(g * HD, (g + 1) * HD)
        rows = slice(g * QPG * LB, (g + 1) * QPG * LB)
        p, es, denom = _attn_softmax(_attn_scores(n, _stack_heads(qv, g), kcat[:, sl], upper), _sink_wide(sinks, g))
        rden = 1.0 / denom
        pr_ref[0, rows, :] = (p * rden).astype(BF16)
        ps_ref[0, rows, :] = (es * rden).astype(BF16)
        outs.append((_dot(jnp.where(upper, p, 0.0).astype(BF16), vcat[0:LB, sl])
                     + _dot(jnp.where(upper, 0.0, p).astype(BF16), vcat[LB:2 * LB, sl])) * rden[:, 0:HD])
    for g, grp in enumerate(outs):
        for hh in range(QPG):
            h = QPG * g + hh
            o_ref[:, h * HD:(h + 1) * HD] = grp[hh * LB:(hh + 1) * LB, :].astype(BF16)


def _cumsum_rows(a, reverse):
    row = lax.broadcasted_iota(jnp.int32, a.shape, 0)
    step = 1
    while step < LB:
        if reverse:
            a = a + jnp.where(row < LB - step, pltpu.roll(a, LB - step, 0), 0.0)
        else:
            a = a + jnp.where(row >= step, pltpu.roll(a, step, 0), 0.0)
        step *= 2
    return a


SUB = 8


def _conv_shifts(tail, cur):
    row = lax.broadcasted_iota(jnp.int32, tail.shape, 0)
    out = [cur]
    for j in range(1, CONVK):
        rolled = pltpu.roll(cur, j, 0)
        top = jnp.where(row < j, pltpu.roll(tail, j, 0), rolled[0:SUB, :])
        out.append(jnp.concatenate([top, rolled[SUB:, :]], axis=0))
    return out


def _conv_advances(du, head):
    row = lax.broadcasted_iota(jnp.int32, head.shape, 0)
    out = []
    for j in range(1, CONVK):
        rolled = pltpu.roll(du, LB - j, 0)
        bottom = jnp.where(row >= SUB - j, pltpu.roll(head, SUB - j, 0), rolled[LB - SUB:, :])
        out.append(jnp.concatenate([rolled[:LB - SUB, :], bottom], axis=0))
    return out


def _split(v, terms):
    out = []
    for _ in range(terms - 1):
        t = v.astype(BF16)
        out.append(t)
        v = v - t.astype(F32)
    out.append(v.astype(BF16))
    return out


def _dot_sel(v, sel, terms):
    parts = [_dot(t, sel) for t in _split(v, terms)]
    return functools.reduce(lambda a, b: a + b, parts)


def _dot_nt_sel(v, sel, terms):
    parts = [_dot_nt(t, sel) for t in _split(v, terms)]
    return functools.reduce(lambda a, b: a + b, parts)


def _ssd_pre(xt_ref, xc_ref, cw_ref, cb_ref, dtr_ref, sp_ref, n):
    cur = xc_ref[...]
    tail = jnp.where(n > 0, xt_ref[...], 0.0)
    sh = _conv_shifts(tail, cur)
    u = cb_ref[...] + cw_ref[CONVK - 1:CONVK, :] * sh[0]
    for j in range(1, CONVK):
        u = u + cw_ref[CONVK - 1 - j:CONVK - j, :] * sh[j]
    dt = _softplus(dtr_ref[...] + sp_ref[0:1, :])
    acs = _cumsum_rows(dt * -jnp.exp(sp_ref[1:2, :]), False)
    return u, dt, acs


def _gated_norm_fwd(y, z, sgz, nw):
    yz = y * (z * sgz)
    parts = []
    for g in range(2):
        t = yz[:, g * 256:(g + 1) * 256]
        parts.append(t * lax.rsqrt(jnp.mean(t * t, axis=-1, keepdims=True) + EPS))
    return jnp.concatenate(parts, axis=1) * nw


HPG = 4
GW = HPG * HD


class _SsdChunk:
    def __init__(self, xc, dt, acs, spv, e64, e128, decay=None):
        self.e64, self.e128 = e64, e128
        alast = acs[LB - 1:LB, :]
        self.e_all = jnp.exp(acs)
        self.dte_all = jnp.exp(alast - acs)
        self.elast = jnp.exp(alast)
        wide = _dot_sel(jnp.concatenate([dt, self.e_all, self.dte_all], axis=0), e64, 2)
        self.dt_x, self.e_x, self.dte_x = wide[0:LB], wide[LB:2 * LB], wide[2 * LB:3 * LB]
        self.dsk_x = _dot_sel(spv, e64, 3)[2:3, :]
        if decay is None:
            acs_t = jnp.transpose(acs)
            ac_x = _dot_sel(acs, e128, 3)
            row = lax.broadcasted_iota(jnp.int32, (HPG * LB, LB), 0)
            col = lax.broadcasted_iota(jnp.int32, (HPG * LB, LB), 1)
            causal = (row % LB) >= col
        lane = lax.broadcasted_iota(jnp.int32, (LB, GW), 1)
        self.head_lanes = [(lane >= hh * HD) & (lane < (hh + 1) * HD) for hh in range(HPG)]
        self.xs, self.xdt, self.b, self.c, self.bb, self.cb16, self.cbm, self.dm_st, self.m_st = ([] for _ in range(9))
        for g in range(2):
            heads = range(HPG * g, HPG * (g + 1))
            if decay is None:
                ac_st = jnp.concatenate([ac_x[:, j * LB:(j + 1) * LB] for j in heads], axis=0)
                ar_st = jnp.concatenate([jnp.broadcast_to(acs_t[j:j + 1, :], (LB, LB)) for j in heads], axis=0)
                dm_st = jnp.exp(jnp.where(causal, ac_st - ar_st, NEG))
            else:
                dm_st = decay[g]
            bg = xc[:, SW + g * NST:SW + (g + 1) * NST]
            cg = xc[:, SW + 2 * NST + g * NST:SW + 2 * NST + (g + 1) * NST]
            bgb, cgb = bg.astype(BF16), cg.astype(BF16)
            cbm = _dot_nt(cgb, bgb)
            xs_g = xc[:, g * GW:(g + 1) * GW]
            self.xs.append(xs_g)
            self.xdt.append(xs_g * self.dt_x[:, g * GW:(g + 1) * GW])
            self.b.append(bg)
            self.c.append(cg)
            self.bb.append(bgb)
            self.cb16.append(cgb)
            self.cbm.append(cbm)
            self.dm_st.append(dm_st)
            self.m_st.append(jnp.concatenate([cbm] * HPG, axis=0) * dm_st)

    def elast_rows(self, g):
        return jnp.concatenate([jnp.broadcast_to(self.elast[:, j:j + 1], (HD, NST))
                                for j in range(HPG * g, HPG * (g + 1))], axis=0)

    def diag_blocks(self, stacked):
        out = stacked[(HPG - 1) * LB:HPG * LB, :]
        for hh in range(HPG - 2, -1, -1):
            out = jnp.where(self.head_lanes[hh], stacked[hh * LB:(hh + 1) * LB, :], out)
        return out

    def block_diag(self, v):
        return jnp.concatenate([jnp.where(self.head_lanes[hh], v, 0.0) for hh in range(HPG)], axis=0)


def _ssd_fwd_block(n, xt_ref, xc_ref, cw_ref, cb_ref, dtr_ref, sp_ref, z_ref, nw_ref, e64_ref, e128_ref,
                   yn_ref, y_ref, hs_ref, u_ref, dt_ref, acs_ref, dm_ref, h_scr):
    @pl.when(n == 0)
    def _():
        h_scr[...] = jnp.zeros_like(h_scr)

    h_all = h_scr[...]
    hs_ref[0] = h_all
    u, dt, acs = _ssd_pre(xt_ref, xc_ref, cw_ref, cb_ref, dtr_ref, sp_ref, n)
    u_ref[...] = u
    dt_ref[...] = dt
    acs_ref[...] = acs
    xc = u * _sigmoid(u)
    ck = _SsdChunk(xc, dt, acs, sp_ref[...], e64_ref[...], e128_ref[...])
    dm_ref[0] = jnp.concatenate(ck.dm_st, axis=0)
    ys, hn = [], []
    for g in range(2):
        gl = slice(g * GW, (g + 1) * GW)
        xdt = ck.xdt[g]
        hg = h_all[gl, :]
        y_diag = ck.diag_blocks(_dot(ck.m_st[g].astype(BF16), xdt.astype(BF16)))
        y_off = ck.e_x[:, gl] * _dot_nt(ck.cb16[g], hg.astype(BF16))
        ys.append(y_diag + y_off + ck.xs[g] * ck.dsk_x[:, gl])
        hn.append(hg * ck.elast_rows(g) + _dot_tn((xdt * ck.dte_x[:, gl]).astype(BF16), ck.bb[g]))
    h_scr[...] = jnp.concatenate(hn, axis=0)
    y = jnp.concatenate(ys, axis=1)
    y_ref[...] = y
    z = z_ref[...]
    yn_ref[...] = _gated_norm_fwd(y, z, _sigmoid(z), nw_ref[...]).astype(BF16)


def _mixer_fwd(q, k, v, sinks8, xbc, conv_w8, conv_b, dtr, ssm_p, z, nw, gathers):
    s = q.shape[0]
    nb = s // LB
    bps = _blocks_per_step(nb)
    nsteps = nb // bps
    tl = bps * LB
    cur = lambda n: (n, 0)
    prev = lambda n: (jnp.maximum(n * bps - 1, 0), 0)
    items, ex_shapes, n_g = _exchange_items(gathers, [])
    ne = len(items)

    n_in, n_out = 16, 10
    relay_step = (3 * (nsteps - 1)) // 4
    stack = pl.BlockSpec((bps, NH * LB, LB), lambda n: (n, 0, 0))
    e64, e128 = _head_expanders()

    def body(*refs):
        (q_ref, kp_ref, kc_ref, vp_ref, vc_ref, sink_ref, xt_ref, xc_ref, cw_ref, cb_ref, dtr_ref, sp_ref, z_ref,
         nw_ref, e64_ref, e128_ref) = refs[:n_in]
        ex_in = refs[n_in:n_in + ne]
        (o_ref, yn_ref, y_ref, hs_ref, u_ref, dt_ref, acs_ref, dm_ref, pr_ref,
         ps_ref) = refs[n_in + ne:n_in + n_out + ne]
        ex_out = refs[n_in + n_out + ne:n_in + n_out + 2 * ne]
        h_scr = refs[n_in + n_out + 2 * ne]
        sems = refs[n_in + n_out + 1 + 2 * ne:]
        n = pl.program_id(0)

        @pl.when(n == 0)
        def _():
            _Exchange(n_g, ex_in, ex_out, sems).two_level_start()

        for sub in range(bps):
            blk = n * bps + sub
            r = slice(sub * LB, (sub + 1) * LB)
            before = slice((sub - 1) * LB, sub * LB)
            one = slice(sub, sub + 1)
            _attn_fwd_block(blk, q_ref.at[r], kp_ref if sub == 0 else kc_ref.at[before], kc_ref.at[r],
                            vp_ref if sub == 0 else vc_ref.at[before], vc_ref.at[r], sink_ref,
                            o_ref.at[r], pr_ref.at[one], ps_ref.at[one])
            _ssd_fwd_block(blk, xt_ref if sub == 0 else xc_ref.at[sub * LB - SUB:sub * LB], xc_ref.at[r], cw_ref,
                           cb_ref, dtr_ref.at[r], sp_ref, z_ref.at[r], nw_ref, e64_ref, e128_ref,
                           yn_ref.at[r], y_ref.at[r], hs_ref.at[one], u_ref.at[r], dt_ref.at[r], acs_ref.at[r],
                           dm_ref.at[one], h_scr)

        @pl.when(n == relay_step)
        def _():
            _Exchange(n_g, ex_in, ex_out, sems).two_level_relay()

        @pl.when(n == nsteps - 1)
        def _():
            _Exchange(n_g, ex_in, ex_out, sems).two_level_finish()

    any_spec = pl.BlockSpec(memory_space=pl.ANY)
    tail = pl.BlockSpec((SUB, D), lambda n: (jnp.maximum(n * (tl // SUB) - 1, 0), 0))
    outs = _pcall(
        body, name="mixer_fwd", grid=(nsteps,),
        in_specs=[pl.BlockSpec((tl, AW), cur), pl.BlockSpec((LB, KVW), prev), pl.BlockSpec((tl, KVW), cur),
                  pl.BlockSpec((LB, KVW), prev), pl.BlockSpec((tl, KVW), cur), _const((8, LB)),
                  tail, pl.BlockSpec((tl, D), cur), _const((8, D)), _const((1, D)),
                  pl.BlockSpec((tl, LB), cur), _const((8, LB)), pl.BlockSpec((tl, SW), cur), _const((1, SW)),
                  _const(e64.shape), _const(e128.shape)]
        + [any_spec] * ne,
        out_specs=[pl.BlockSpec((tl, AW), cur), pl.BlockSpec((tl, SW), cur), pl.BlockSpec((tl, SW), cur),
                   pl.BlockSpec((bps, NH * HD, NST), lambda n: (n, 0, 0)), pl.BlockSpec((tl, D), cur),
                   pl.BlockSpec((tl, LB), cur), pl.BlockSpec((tl, LB), cur),
                   stack, stack, stack] + [any_spec] * ne,
        out_shape=[_sds((s, AW), BF16), _sds((s, SW), BF16), _sds((s, SW), F32), _sds((nb, NH * HD, NST), F32),
                   _sds((s, D), F32), _sds((s, LB), F32), _sds((s, LB), F32), _sds((nb, NH * LB, LB), F32),
                   _sds((nb, NH * LB, LB), BF16), _sds((nb, NH * LB, LB), BF16)]
        + ex_shapes,
        scratch_shapes=[pltpu.VMEM((NH * HD, NST), F32)] + _exchange_sems(ne),
        compiler_params=_params(),
    )(q, k, k, v, v, sinks8, xbc, xbc, conv_w8, conv_b, dtr, ssm_p, z, nw, e64, e128, *items)
    return outs[:n_out], outs[n_out:]


def _blocks_per_step(nb):
    return next(b for b in (4, 2, 1) if nb % b == 0)


def _head_expanders():
    j = lax.broadcasted_iota(jnp.int32, (LB, NH * HD), 0)
    e64 = (lax.broadcasted_iota(jnp.int32, (LB, NH * HD), 1) // HD == j).astype(BF16)
    j = lax.broadcasted_iota(jnp.int32, (LB, NH * LB), 0)
    e128 = (lax.broadcasted_iota(jnp.int32, (LB, NH * LB), 1) // LB == j).astype(BF16)
    return e64, e128


def _outproj_ffn_fwd_loss(attn, yn, x, tgt, mod8, n2w, fnw, w_out, w_gu_t, w_down):
    s = x.shape[0]
    tf = min(256, s)

    def body(a_ref, y_ref, x_ref, t_ref, mod_ref, nw_ref, fw_ref, wo_ref, wgu_ref, wd_ref,
             x2_ref, h2_ref, mo_ref, mix_ref, gu_ref, act_ref, dx3_ref, sm_ref):
        i = pl.program_id(0)

        @pl.when(i == 0)
        def _():
            sm_ref[...] = jnp.zeros_like(sm_ref)

        mix = jnp.concatenate([a_ref[...], y_ref[...]], axis=1)
        mix_ref[...] = mix
        mo = _dot(mix, wo_ref[...])
        mo_ref[...] = mo.astype(BF16)
        x2 = x_ref[...] + mod_ref[2:3, :] * mo
        x2_ref[...] = x2
        h2 = _norm_mod_fwd(x2, nw_ref[...], mod_ref[3:4, :], mod_ref[4:5, :]).astype(BF16)
        h2_ref[...] = h2
        gu = _dot_nt(h2, wgu_ref[...])
        gu_ref[...] = gu.astype(BF16)
        g = gu[:, :DFF]
        act = (g * _sigmoid(g) * gu[:, DFF:]).astype(BF16)
        act_ref[...] = act
        ff = _dot(act, wd_ref[...])
        x3 = x2 + mod_ref[5:6, :] * ff
        r = lax.rsqrt(jnp.mean(x3 * x3, axis=-1, keepdims=True) + EPS)
        xh = x3 * r
        fw = fw_ref[...]
        err = xh * fw - t_ref[...]
        dy = err * (1.0 / D)
        dxh = dy * fw
        dx3 = r * (dxh - xh * jnp.mean(dxh * xh, axis=-1, keepdims=True))
        dx3_ref[...] = dx3
        sm_ref[0:1, :] += jnp.sum(dx3 * ff, axis=0, keepdims=True)
        sm_ref[1:2, :] += jnp.sum(dy * xh, axis=0, keepdims=True)
        sm_ref[2:3, :] += jnp.sum(err * err, axis=0, keepdims=True)

    return _pcall(
        body, name="outproj_ffn_fwd_loss", grid=(s // tf,),
        in_specs=[_rows(tf, AW), _rows(tf, SW), _rows(tf, D), _rows(tf, D), _const((8, D)), _const((1, D)),
                  _const((1, D)), _const((D, D)), _const((2 * DFF, D)), _const((DFF, D))],
        out_specs=[_rows(tf, D), _rows(tf, D), _rows(tf, D), _rows(tf, D), _rows(tf, 2 * DFF), _rows(tf, DFF),
                   _rows(tf, D), pl.BlockSpec((8, D), lambda i: (0, 0))],
        out_shape=[_sds((s, D), F32), _sds((s, D), BF16), _sds((s, D), BF16), _sds((s, D), BF16),
                   _sds((s, 2 * DFF), BF16), _sds((s, DFF), BF16), _sds((s, D), F32), _sds((8, D), F32)],
        compiler_params=_params(),
    )(attn, yn, x, tgt, mod8, n2w, fnw, w_out, w_gu_t, w_down)


def _ffn_bwd(dx3, gu, x2, mixout, mod8, n2w, w_gu, w_down, w_out):
    s = x2.shape[0]
    tb = min(256, s)

    def body(dx3_ref, gu_ref, x2_ref, mo_ref, mod_ref, nw_ref, wgu_ref, wd_ref, wo_ref,
             dx2_ref, dff_ref, dgu_ref, dmix_ref, dattn_ref, dyn_ref, sm_ref):
        i = pl.program_id(0)

        @pl.when(i == 0)
        def _():
            sm_ref[...] = jnp.zeros_like(sm_ref)

        dx3 = dx3_ref[...]
        dff = (dx3 * mod_ref[5:6, :]).astype(BF16)
        dff_ref[...] = dff
        dact = _dot_nt(dff, wd_ref[...])
        g = gu_ref[:, :DFF].astype(F32)
        u = gu_ref[:, DFF:].astype(F32)
        sg = _sigmoid(g)
        dgu = jnp.concatenate([dact * u * sg * (1.0 + g * (1.0 - sg)), dact * g * sg], axis=1).astype(BF16)
        dgu_ref[...] = dgu
        dh2 = _dot(dgu, wgu_ref[...])
        dxn, d_shift, d_scale, d_w = _norm_mod_bwd(x2_ref[...], dh2, nw_ref[...], mod_ref[4:5, :])
        dx2 = dx3 + dxn
        dx2_ref[...] = dx2
        sm_ref[0:1, :] += d_shift
        sm_ref[1:2, :] += d_scale
        sm_ref[2:3, :] += d_w
        sm_ref[3:4, :] += jnp.sum(dx2 * mo_ref[...].astype(F32), axis=0, keepdims=True)
        dmix = (dx2 * mod_ref[2:3, :]).astype(BF16)
        dmix_ref[...] = dmix
        dmi = _dot_nt(dmix, wo_ref[...])
        dattn_ref[...] = dmi[:, :AW].astype(BF16)
        dyn_ref[...] = dmi[:, AW:]

    return _pcall(
        body, name="ffn_bwd", grid=(s // tb,),
        in_specs=[_rows(tb, D), _rows(tb, 2 * DFF), _rows(tb, D), _rows(tb, D), _const((8, D)), _const((1, D)),
                  _const((2 * DFF, D)), _const((DFF, D)), _const((D, D))],
        out_specs=[_rows(tb, D), _rows(tb, D), _rows(tb, 2 * DFF), _rows(tb, D), _rows(tb, AW), _rows(tb, SW),
                   pl.BlockSpec((8, D), lambda i: (0, 0))],
        out_shape=[_sds((s, D), F32), _sds((s, D), BF16), _sds((s, 2 * DFF), BF16), _sds((s, D), BF16),
                   _sds((s, AW), BF16), _sds((s, SW), F32), _sds((8, D), F32)],
        compiler_params=_params(),
    )(dx3, gu, x2, mixout, mod8, n2w, w_gu, w_down, w_out)


def _ssd_bwd_block(i, *refs):
    def run(dyn_ref, y_ref, z_ref, x_ref, u_ref, dt_ref, acs_ref, dm_ref, cw_ref, sp_ref, nw_ref,
            hs_ref, e64_ref, e128_ref, dzxd_ref, sm_ref, dh_scr, dun_scr):
        @pl.when(i == 0)
        def _():
            dh_scr[...] = jnp.zeros_like(dh_scr)
            dun_scr[...] = jnp.zeros_like(dun_scr)
            sm_ref[...] = jnp.zeros_like(sm_ref)

        u, dt, acs = u_ref[...], dt_ref[...], acs_ref[...]
        sg_u = _sigmoid(u)
        xc = u * sg_u
        a_neg = -jnp.exp(sp_ref[1:2, :])
        ck = _SsdChunk(xc, dt, acs, sp_ref[...], e64_ref[...], e128_ref[...],
                       decay=[dm_ref[0, g * HPG * LB:(g + 1) * HPG * LB, :] for g in range(2)])
        h_all = hs_ref[0]
        dh_all = dh_scr[...]
        riota = lax.broadcasted_iota(jnp.int32, (LB, LB), 0)
        lane1 = lax.broadcasted_iota(jnp.int32, (1, LB), 1)

        z = z_ref[...]
        y = y_ref[...]
        sgz = _sigmoid(z)
        sz = z * sgz
        yz = y * sz
        nwv = nw_ref[...]
        dyn_v = dyn_ref[...]
        dyhat = dyn_v * nwv
        yhat_parts, dyz_parts = [], []
        for g in range(2):
            gs = slice(g * 256, (g + 1) * 256)
            t = yz[:, gs]
            rg = lax.rsqrt(jnp.mean(t * t, axis=-1, keepdims=True) + EPS)
            yh = t * rg
            dyh = dyhat[:, gs]
            yhat_parts.append(yh)
            dyz_parts.append(rg * (dyh - yh * jnp.mean(dyh * yh, axis=-1, keepdims=True)))
        yhat = jnp.concatenate(yhat_parts, axis=1)
        dyz = jnp.concatenate(dyz_parts, axis=1)
        sm_ref[5:6, 0:SW] += jnp.sum(dyn_v * yhat, axis=0, keepdims=True)
        dy = dyz * sz
        dzxd_ref[:, 0:SW] = (dyz * y * sgz * (1.0 + z * (1.0 - sgz))).astype(BF16)

        cat = lambda parts: jnp.concatenate(parts, axis=1)
        dxs, dbs, dcs, dhp, g_cat, de_x, ddte_x, ddt_x, ddsk_x = ([] for _ in range(9))
        dacs_t = jnp.zeros((LB, LB), F32)
        hsum = jnp.zeros((1, LB), F32)
        for g in range(2):
            gl = slice(g * GW, (g + 1) * GW)
            xs_g, xdt, bgb, cgb = ck.xs[g], ck.xdt[g], ck.bb[g], ck.cb16[g]
            m_st, dm_st = ck.m_st[g], ck.dm_st[g]
            dt_x, e_x, dte_x = ck.dt_x[:, gl], ck.e_x[:, gl], ck.dte_x[:, gl]
            xdtb = xdt.astype(BF16)
            hg, dhn = h_all[gl, :], dh_all[gl, :]
            hb, dhnb = hg.astype(BF16), dhn.astype(BF16)
            dy_g = dy[:, gl]
            ddsk_x.append(jnp.sum(dy_g * xs_g, axis=0, keepdims=True))
            dy_bd = ck.block_diag(dy_g).astype(BF16)
            dm4 = _dot_nt(dy_bd, xdtb)
            dxdt = _dot_tn(m_st.astype(BF16), dy_bd)
            gmat = dm4 * m_st
            dcbm = dm4 * dm_st
            dcb = dcbm[0:LB] + dcbm[LB:2 * LB] + dcbm[2 * LB:3 * LB] + dcbm[3 * LB:4 * LB]
            g_cat.append(cat([gmat[hh * LB:(hh + 1) * LB, :] for hh in range(HPG)]))
            for hh in range(HPG):
                j = HPG * g + hh
                col_sum = jnp.sum(gmat[hh * LB:(hh + 1) * LB, :], axis=0, keepdims=True)
                dacs_t = dacs_t - jnp.where(riota == j, col_sum, 0.0)
                hsl = slice(hh * HD, (hh + 1) * HD)
                hsum = hsum + jnp.where(lane1 == j, jnp.sum(dhn[hsl, :] * hg[hsl, :]), 0.0)
            dchb = (dy_g * e_x).astype(BF16)
            dcg = _dot(dchb, hb)
            dh_prev = _dot_tn(dchb, cgb)
            de_x.append(dy_g * _dot_nt(cgb, hb))
            dxs_s = _dot_nt(bgb, dhnb)
            dbg = _dot((xdt * dte_x).astype(BF16), dhnb)
            dxdt = dxdt + dxs_s * dte_x
            ddte_x.append(dxs_s * xdt)
            dhp.append(dhn * ck.elast_rows(g) + dh_prev)
            dxs.append(dy_g * ck.dsk_x[:, gl] + dxdt * dt_x)
            ddt_x.append(dxdt * xs_g)
            dcbb = dcb.astype(BF16)
            dbs.append(dbg + _dot_tn(dcbb, cgb))
            dcs.append(dcg + _dot(dcbb, bgb))
        dh_scr[...] = jnp.concatenate(dhp, axis=0)
        red = _dot_nt_sel(jnp.concatenate([cat(de_x), cat(ddte_x), cat(ddt_x)], axis=0), ck.e64, 1)
        de_c, ddte_c, ddt_c = red[0:LB], red[LB:2 * LB], red[2 * LB:3 * LB]
        ddsk = _dot_nt_sel(jnp.broadcast_to(cat(ddsk_x), (SUB, NH * HD)), ck.e64, 2)[0:1, :]
        t1 = ddte_c * ck.dte_all
        dalast = jnp.sum(t1, axis=0, keepdims=True) + hsum * ck.elast
        dacs = (_dot_nt_sel(cat(g_cat), ck.e128, 2) + de_c * ck.e_all - t1 + jnp.transpose(dacs_t)
                + jnp.where(riota == LB - 1, dalast, 0.0))
        da = _cumsum_rows(dacs, True)
        ddt = ddt_c + da * a_neg
        da_log = jnp.sum(da * dt, axis=0, keepdims=True) * a_neg
        ddtr = ddt * (1.0 - jnp.exp(-dt))
        dzxd_ref[:, SW + D:ZXD] = ddtr.astype(BF16)
        sm_ref[6:7, 0:LB] += jnp.sum(ddtr, axis=0, keepdims=True)
        sm_ref[6:7, LB:2 * LB] += da_log
        sm_ref[6:7, 2 * LB:3 * LB] += ddsk

        du = cat(dxs + dbs + dcs) * (sg_u * (1.0 + u * (1.0 - sg_u)))
        xv = x_ref[...]
        adv = [du] + _conv_advances(du, dun_scr[...])
        sm_ref[0:1, :] += jnp.sum(du, axis=0, keepdims=True)
        dxbc = cw_ref[CONVK - 1:CONVK, :] * du
        for j in range(CONVK):
            sm_ref[CONVK - j:CONVK + 1 - j, :] += jnp.sum(adv[j] * xv, axis=0, keepdims=True)
            if j:
                dxbc = dxbc + cw_ref[CONVK - 1 - j:CONVK - j, :] * adv[j]
        dun_scr[...] = du[0:SUB, :]
        dzxd_ref[:, SW:SW + D] = dxbc.astype(BF16)

    run(*refs)


def _attn_bwd_block(i, q_ref, kp_ref, kc_ref, vp_ref, vc_ref, o_ref, do_ref, cos_ref, sin_ref, pr_ref, ps_ref,
                    dq_ref, dkv_ref, ds_ref, ck_scr, cv_scr):
    @pl.when(i == 0)
    def _():
        ds_ref[...] = jnp.zeros_like(ds_ref)
        ck_scr[...] = jnp.zeros_like(ck_scr)
        cv_scr[...] = jnp.zeros_like(cv_scr)

    qv, ov, dov = q_ref[...], o_ref[...], do_ref[...]
    kcat = jnp.concatenate([kp_ref[...], kc_ref[...]], axis=0)
    vcat = jnp.concatenate([vp_ref[...], vc_ref[...]], axis=0)
    upper = _upper_mask()
    srow = lax.broadcasted_iota(jnp.int32, (8, LB), 0)
    slane = lax.broadcasted_iota(jnp.int32, (8, LB), 1)
    dsink = jnp.zeros((8, LB), F32)
    dq_g, dk_g, dv_g = [], [], []
    for g in range(NQ // QPG):
        sl = slice(g * HD, (g + 1) * HD)
        qg = _stack_heads(qv, g)
        dog = _stack_heads(dov, g)
        rows = slice(g * QPG * LB, (g + 1) * QPG * LB)
        probs = pr_ref[0, rows, :].astype(F32)
        psink = ps_ref[0, rows, :].astype(F32)
        delta = _row_sums_wide(dog.astype(F32) * _stack_heads(ov, g).astype(F32), 2)
        dsc = probs * (_band(upper, _dot_nt(dog, vcat[0:LB, sl]), _dot_nt(dog, vcat[LB:2 * LB, sl])) - delta)
        sink_terms = (psink * delta)[:, 0:1]
        for hh in range(QPG):
            dsink = dsink - jnp.where((srow == QPG * g + hh) & (slane == 0),
                                      jnp.sum(sink_terms[hh * LB:(hh + 1) * LB, :]), 0.0)
        ds_p = jnp.where(upper, dsc, 0.0).astype(BF16)
        ds_c = jnp.where(upper, 0.0, dsc).astype(BF16)
        dq_g.append((_dot(ds_p, kcat[0:LB, sl]) + _dot(ds_c, kcat[LB:2 * LB, sl])) * ATT_SCALE)
        dk_g.append(jnp.concatenate([_dot_tn(ds_p, qg), _dot_tn(ds_c, qg)], axis=0) * ATT_SCALE)
        dv_g.append(jnp.concatenate([_dot_tn(jnp.where(upper, probs, 0.0).astype(BF16), dog),
                                     _dot_tn(jnp.where(upper, 0.0, probs).astype(BF16), dog)], axis=0))
    ds_ref[...] += dsink
    cs = cos_ref[...]
    sn = sin_ref[...]
    dk2 = jnp.concatenate(dk_g, axis=1)
    dv2 = jnp.concatenate(dv_g, axis=1)
    for a, tile in enumerate(_unstack_heads(dq_g)):
        dq_ref[:, a * LB:(a + 1) * LB] = _rope(tile, cs, sn, True).astype(BF16)
    dkv_ref[:, 0:KVW] = _rope(ck_scr[...] + dk2[LB:2 * LB, :], cs, sn, True).astype(BF16)
    dkv_ref[:, KVW:2 * KVW] = (cv_scr[...] + dv2[LB:2 * LB, :]).astype(BF16)
    ck_scr[...] = dk2[0:LB, :]
    cv_scr[...] = dv2[0:LB, :]


def _mixer_bwd(q, k, v, o, do, cos, sin, probs, psink, dyn, y, z, xbc, u, dtv, acs, decay, conv_w8, ssm_p, nw, hs,
               scatters):
    s = q.shape[0]
    nb = s // LB
    bps = _blocks_per_step(nb)
    nsteps = nb // bps
    tl = bps * LB
    cur = lambda i: (nsteps - 1 - i, 0)
    prev = lambda i: (jnp.maximum((nsteps - 1 - i) * bps - 1, 0), 0)
    n_in = 25
    items, ex_shapes, n_g = _exchange_items([], scatters)
    ne = len(items)
    e64, e128 = _head_expanders()
    stack = pl.BlockSpec((bps, NH * LB, LB), lambda i: (nsteps - 1 - i, 0, 0))

    def body(*refs):
        i = pl.program_id(0)
        (q_ref, kp_ref, kc_ref, vp_ref, vc_ref, o_ref, do_ref, cos_ref, sin_ref, pr_ref, ps_ref,
         dyn_ref, y_ref, z_ref, x_ref, u_ref, dt_ref, acs_ref, dm_ref, cw_ref, sp_ref, nw_ref,
         hs_ref, e64_ref, e128_ref) = refs[:n_in]
        ex_in = refs[n_in:n_in + ne]
        dp_ref, ds_ref, sm_ref = refs[n_in + ne:n_in + ne + 3]
        ex_out = refs[n_in + ne + 3:n_in + 2 * ne + 3]
        ck_scr, cv_scr, dh_scr, dun_scr = refs[n_in + 2 * ne + 3:n_in + 2 * ne + 7]
        sems = refs[n_in + 2 * ne + 7:]

        @pl.when(i == 0)
        def _():
            _Exchange(n_g, ex_in, ex_out, sems).start()

        for back in range(bps):
            sub = bps - 1 - back
            step = i * bps + back
            r = slice(sub * LB, (sub + 1) * LB)
            before = slice((sub - 1) * LB, sub * LB)
            one = slice(sub, sub + 1)
            _attn_bwd_block(step, q_ref.at[r], kp_ref if sub == 0 else kc_ref.at[before], kc_ref.at[r],
                            vp_ref if sub == 0 else vc_ref.at[before], vc_ref.at[r], o_ref.at[r], do_ref.at[r],
                            cos_ref.at[r], sin_ref.at[r], pr_ref.at[one], ps_ref.at[one],
                            dp_ref.at[r, O_Q:O_K], dp_ref.at[r, O_K:O_Z], ds_ref, ck_scr, cv_scr)
            _ssd_bwd_block(step, dyn_ref.at[r], y_ref.at[r], z_ref.at[r], x_ref.at[r], u_ref.at[r], dt_ref.at[r],
                           acs_ref.at[r], dm_ref.at[one], cw_ref, sp_ref, nw_ref,
                           hs_ref.at[one], e64_ref, e128_ref, dp_ref.at[r, O_Z:INP], sm_ref, dh_scr, dun_scr)

        @pl.when(i == nsteps - 1)
        def _():
            _Exchange(n_g, ex_in, ex_out, sems).finish()

    any_spec = pl.BlockSpec(memory_space=pl.ANY)
    outs = _pcall(
        body, name="mixer_bwd", grid=(nsteps,),
        in_specs=[pl.BlockSpec((tl, AW), cur), pl.BlockSpec((LB, KVW), prev), pl.BlockSpec((tl, KVW), cur),
                  pl.BlockSpec((LB, KVW), prev), pl.BlockSpec((tl, KVW), cur), pl.BlockSpec((tl, AW), cur),
                  pl.BlockSpec((tl, AW), cur), pl.BlockSpec((tl, LB), cur), pl.BlockSpec((tl, LB), cur),
                  stack, stack,
                  pl.BlockSpec((tl, SW), cur), pl.BlockSpec((tl, SW), cur), pl.BlockSpec((tl, SW), cur),
                  pl.BlockSpec((tl, D), cur), pl.BlockSpec((tl, D), cur), pl.BlockSpec((tl, LB), cur),
                  pl.BlockSpec((tl, LB), cur), stack,
                  _const((8, D)), _const((8, LB)), _const((1, SW)),
                  pl.BlockSpec((bps, NH * HD, NST), lambda i: (nsteps - 1 - i, 0, 0)),
                  _const(e64.shape), _const(e128.shape)] + [any_spec] * ne,
        out_specs=[pl.BlockSpec((tl, INP), cur), pl.BlockSpec((8, LB), lambda i: (0, 0)),
                   pl.BlockSpec((8, D), lambda i: (0, 0))] + [any_spec] * ne,
        out_shape=[_sds((s, INP), BF16), _sds((8, LB), F32), _sds((8, D), F32)] + ex_shapes,
        scratch_shapes=[pltpu.VMEM((LB, KVW), F32), pltpu.VMEM((LB, KVW), F32),
                        pltpu.VMEM((NH * HD, NST), F32), pltpu.VMEM((SUB, D), F32)]
        + _exchange_sems(ne),
        compiler_params=_params(),
    )(q, k, k, v, v, o, do, cos, sin, probs, psink, dyn, y, z, xbc, u, dtv, acs, decay, conv_w8, ssm_p, nw, hs,
      e64, e128, *items)
    return outs[0], outs[1], outs[2], outs[3:]


def _inproj_bwd(dproj, x, dx2, mod8, n1w, w_in_t, scatters, smalls):
    s = x.shape[0]
    tt = min(512, s)
    nt = s // tt
    items, ex_shapes, n_g = _exchange_items([], scatters)
    ne = len(items)
    n_in = 10

    def body(*refs):
        dp_ref, x_ref, dx2_ref, mod_ref, nw_ref, w_ref, f_ref, b_ref, s_ref, k_ref = refs[:n_in]
        ex_in = refs[n_in:n_in + ne]
        gx_ref, sm_ref = refs[n_in + ne:n_in + 2 + ne]
        ex_out = refs[n_in + 2 + ne:n_in + 2 + 2 * ne]
        gpack_ref = refs[n_in + 2 + 2 * ne]
        pack_scr = refs[n_in + 3 + 2 * ne]
        sems = refs[n_in + 4 + 2 * ne:n_in + 7 + 2 * ne]
        pack_sems = refs[n_in + 7 + 2 * ne:]
        i = pl.program_id(0)

        @pl.when(i == 0)
        def _():
            sm_ref[...] = jnp.zeros_like(sm_ref)
            _Exchange(n_g, ex_in, ex_out, sems).start()

        w = w_ref[...]
        hr = tt // 2
        dh1 = [_dot(dp_ref[h * hr:(h + 1) * hr, :], w) for h in range(2)]
        sums = jnp.zeros((3, D), F32)
        for h in range(2):
            rows = slice(h * hr, (h + 1) * hr)
            dxn, d_shift, d_scale, d_w = _norm_mod_bwd(x_ref[rows, :], dh1[h], nw_ref[...], mod_ref[1:2, :])
            gx_ref[rows, :] = dx2_ref[rows, :] + dxn
            sums = sums + jnp.concatenate([d_shift, d_scale, d_w], axis=0)
        sm_ref[0:3, :] += sums

        @pl.when(i == nt - 1)
        def _():
            _pack_rows(f_ref, b_ref, s_ref, sm_ref, k_ref, pack_scr)
            small = _Exchange(1, [pack_scr], [gpack_ref], pack_sems)
            small.start()
            _Exchange(n_g, ex_in, ex_out, sems).finish()
            small.finish()

    any_spec = pl.BlockSpec(memory_space=pl.ANY)
    outs = _pcall(
        body, name="inproj_bwd", grid=(nt,),
        in_specs=[_rows(tt, INP), _rows(tt, D), _rows(tt, D), _const((8, D)), _const((1, D)), _const((INP, D)),
                  _const((8, D)), _const((8, D)), _const((8, D)), _const((8, LB))]
        + [any_spec] * ne,
        out_specs=[_rows(tt, D), pl.BlockSpec((8, D), lambda i: (0, 0))] + [any_spec] * (ne + 1),
        out_shape=[_sds((s, D), F32), _sds((8, D), F32)] + ex_shapes + [_sds((N_DEV, PACK_ROWS, D), F32)],
        scratch_shapes=[pltpu.VMEM((PACK_ROWS, D), F32)] + _exchange_sems(ne) + _exchange_sems(1),
        compiler_params=_params(),
    )(dproj, x, dx2, mod8, n1w, w_in_t, *smalls, *items)
    return outs[0], outs[2:2 + ne], outs[2 + ne]


def _wgrad(a, b, name):
    s, m = a.shape
    n = b.shape[1]
    tk = min(2048, s)
    wide = (1408, 1024, 512)
    tm = next((t for t in wide if m % t == 0), m)
    tn = n if n <= 2048 else _largest_divisor(n, wide)
    nk = s // tk

    def body(a_ref, b_ref, o_ref, acc):
        kk = pl.program_id(2)

        @pl.when(kk == 0)
        def _():
            acc[...] = jnp.zeros_like(acc)

        acc[...] += _dot_tn(a_ref[...], b_ref[...])

        @pl.when(kk == nk - 1)
        def _():
            o_ref[...] = acc[...].astype(BF16)

    return _pcall(
        body, name=name, grid=(m // tm, n // tn, nk),
        in_specs=[pl.BlockSpec((tk, tm), lambda i, j, kk: (kk, i)), pl.BlockSpec((tk, tn), lambda i, j, kk: (kk, j))],
        out_specs=pl.BlockSpec((tm, tn), lambda i, j, kk: (i, j)),
        out_shape=_sds((m, n), BF16),
        scratch_shapes=[pltpu.VMEM((tm, tn), F32)],
        compiler_params=_params(3),
    )(a, b)


PACK_ROWS = 24


def _pack_rows(f_ref, b_ref, s_ref, i_ref, k_ref, o_ref):
    o_ref[...] = jnp.zeros_like(o_ref)
    o_ref[0:2, :] = i_ref[0:2, :]
    o_ref[2:3, :] = b_ref[3:4, :]
    o_ref[3:5, :] = b_ref[0:2, :]
    o_ref[5:6, :] = f_ref[0:1, :]
    o_ref[6:7, :] = i_ref[2:3, :]
    o_ref[7:8, :] = b_ref[2:3, :]
    o_ref[8:9, :] = f_ref[1:2, :]
    o_ref[9:14, :] = s_ref[0:5, :]
    o_ref[14:15, :] = s_ref[5:6, :]
    o_ref[15:16, 0:3 * LB] = s_ref[6:7, 0:3 * LB]
    lane = lax.broadcasted_iota(jnp.int32, (1, LB), 1)
    sk = jnp.zeros((1, LB), F32)
    for h in range(NQ):
        sk = sk + jnp.where(lane == h, k_ref[h:h + 1, 0:1], 0.0)
    o_ref[15:16, 3 * LB:4 * LB] = sk
    o_ref[16:17, :] = f_ref[2:3, :]


def _exchange_items(gathers, scatters):
    items = list(gathers) + list(scatters)
    shapes = [_sds((N_DEV,) + a.shape, a.dtype) for a in gathers] + [_sds(a.shape, a.dtype) for a in scatters]
    return items, shapes, len(gathers)


def _exchange_sems(n):
    return [pltpu.SemaphoreType.DMA((n, N_DEV - 1)), pltpu.SemaphoreType.DMA((n, N_DEV - 1)),
            pltpu.SemaphoreType.DMA((n,))]


class _Exchange:
    def __init__(self, n_g, ins, outs, sems):
        self.n_g, self.ins, self.outs = n_g, ins, outs
        self.send_sems, self.recv_sems, self.loc_sems = sems
        xi, yi, ci = lax.axis_index("x"), lax.axis_index("y"), lax.axis_index("c")
        self.me = 4 * xi + 2 * yi + ci
        self.peers = []
        for r in range(1, N_DEV):
            px = 1 - xi if r & 4 else xi
            py = 1 - yi if r & 2 else yi
            pc = 1 - ci if r & 1 else ci
            self.peers.append(((px, py, pc), 4 * px + 2 * py + pc))

    def _copy(self, t, r, landing):
        dev, peer = self.peers[r]
        src = self.ins[t] if t < self.n_g else self.ins[t].at[peer]
        return pltpu.make_async_remote_copy(
            src_ref=src, dst_ref=self.outs[t].at[landing], send_sem=self.send_sems.at[t, r],
            recv_sem=self.recv_sems.at[t, r], device_id=dev, device_id_type=pl.DeviceIdType.MESH)

    def _local(self, t):
        src = self.ins[t] if t < self.n_g else self.ins[t].at[self.me]
        return pltpu.make_async_copy(src, self.outs[t].at[self.me], self.loc_sems.at[t])

    def start(self):
        for t in range(len(self.ins)):
            self._local(t).start()
            for r in range(N_DEV - 1):
                self._copy(t, r, self.me).start()

    def finish(self):
        n = len(self.ins)
        for t in range(n):
            for r in range(N_DEV - 1):
                self._copy(t, r, self.peers[r][1]).wait_recv()
        for t in range(n):
            for r in range(N_DEV - 1):
                self._copy(t, r, self.me).wait_send()
            self._local(t).wait()

    DIRECT = (0, 1, 3, 5)

    def two_level_start(self):
        for t in range(len(self.ins)):
            self._local(t).start()
            for r in self.DIRECT:
                self._copy(t, r, self.me).start()

    def _relay(self, t, r):
        peer = self.peers[r][1]
        return pltpu.make_async_remote_copy(
            src_ref=self.outs[t].at[peer], dst_ref=self.outs[t].at[peer], send_sem=self.send_sems.at[t, r + 1],
            recv_sem=self.recv_sems.at[t, r + 1], device_id=self.peers[0][0], device_id_type=pl.DeviceIdType.MESH)

    def two_level_relay(self):
        for t in range(len(self.ins)):
            for r in self.DIRECT[1:]:
                self._copy(t, r, self.peers[r][1]).wait_recv()
                self._relay(t, r).start()

    def two_level_finish(self):
        n = len(self.ins)
        for t in range(n):
            for r in (0, 2, 4, 6):
                self._copy(t, r, self.peers[r][1]).wait_recv()
        for t in range(n):
            for r in self.DIRECT:
                self._copy(t, r, self.me).wait_send()
            for r in self.DIRECT[1:]:
                self._relay(t, r).wait_send()
            self._local(t).wait()


def _prologue(c8, w_in_tb, cw8, w_cols, b_cols):
    ncol = w_cols.shape[1]

    def body(c_ref, win_ref, cw_ref, w_ref, b_ref, gc_ref, gin_ref, gcw_ref, gmod_ref, call_scr, mod_scr, loc_sem,
             *sems):
        big = _Exchange(2, [win_ref, cw_ref], [gin_ref, gcw_ref], sems[0:3])
        small = _Exchange(1, [c_ref], [gc_ref], sems[3:6])
        small.start()
        big.two_level_start()
        small.finish()
        landed = pltpu.make_async_copy(gc_ref, call_scr, loc_sem)
        landed.start()
        landed.wait()
        cv = call_scr[:, 0, :]
        sc = (cv * _sigmoid(cv)).astype(BF16)
        mod_scr[...] = _dot(sc, w_ref[...].astype(BF16)) + b_ref[...]
        mods = _Exchange(1, [mod_scr], [gmod_ref], sems[6:9])
        mods.start()
        mods.finish()
        big.two_level_relay()
        big.two_level_finish()

    any_spec = pl.BlockSpec(memory_space=pl.ANY)
    vmem_spec = pl.BlockSpec(memory_space=pltpu.VMEM)
    return _pcall(
        body, name="prologue", in_specs=[any_spec, any_spec, any_spec, vmem_spec, vmem_spec],
        out_specs=[any_spec] * 4,
        out_shape=[_sds((N_DEV,) + c8.shape, F32), _sds((N_DEV,) + w_in_tb.shape, BF16),
                   _sds((N_DEV,) + cw8.shape, F32), _sds((N_DEV, N_DEV, ncol), F32)],
        scratch_shapes=[pltpu.VMEM((N_DEV,) + c8.shape, F32), pltpu.VMEM((N_DEV, ncol), F32),
                        pltpu.SemaphoreType.DMA] + _exchange_sems(2) + _exchange_sems(1) + _exchange_sems(1),
        compiler_params=pltpu.CompilerParams(vmem_limit_bytes=VMEM_LIMIT),
    )(c8, w_in_tb, cw8, w_cols, b_cols)


def _adamw(w, g, m, v):
    m2 = ADAM_B1 * m + (1.0 - ADAM_B1) * g
    v2 = ADAM_B2 * v + (1.0 - ADAM_B2) * (g * g)
    m_hat = m2 / (1.0 - ADAM_B1 ** ADAM_STEP)
    v_hat = v2 / (1.0 - ADAM_B2 ** ADAM_STEP)
    delta = -ADAM_LR * (m_hat / (jnp.sqrt(v_hat) + ADAM_EPS) + ADAM_WD * w)
    return delta, m2, v2


def _sum_adamw(parts, w, m, v, name):
    rws, cols = w.shape
    tr = next((t for t in (256, 176, 128) if rws % t == 0), rws)

    def body(p_ref, w_ref, m_ref, v_ref, g_ref, d_ref, mo_ref, vo_ref):
        g = p_ref[0].astype(F32)
        for dev in range(1, N_DEV):
            g = g + p_ref[dev].astype(F32)
        g_ref[...] = g
        d_ref[...], mo_ref[...], vo_ref[...] = _adamw(w_ref[...], g, m_ref[...], v_ref[...])

    blk = pl.BlockSpec((tr, cols), lambda i: (i, 0))
    return _pcall(
        body, name=name, grid=(rws // tr,),
        in_specs=[pl.BlockSpec((N_DEV, tr, cols), lambda i: (0, i, 0)), blk, blk, blk],
        out_specs=[blk] * 4, out_shape=[_sds((rws, cols), F32)] * 4, compiler_params=_params(),
    )(parts, w, m, v)


def _wada_adamw(c_all, dmod_cols, w, m, v):
    rws, cols = w.shape
    tr = 256

    def body(c_ref, dm_ref, w_ref, m_ref, v_ref, g_ref, d_ref, mo_ref, vo_ref):
        cv = c_ref[...]
        sc = (cv * _sigmoid(cv)).astype(BF16)
        g = _dot_tn(sc, dm_ref[...].astype(BF16))
        g_ref[...] = g
        d_ref[...], mo_ref[...], vo_ref[...] = _adamw(w_ref[...], g, m_ref[...], v_ref[...])

    blk = pl.BlockSpec((tr, cols), lambda i: (i, 0))
    return _pcall(
        body, name="wada_adamw", grid=(rws // tr,),
        in_specs=[pl.BlockSpec((N_DEV, tr), lambda i: (0, i)), pl.BlockSpec((N_DEV, cols), lambda i: (0, 0)),
                  blk, blk, blk],
        out_specs=[blk] * 4, out_shape=[_sds((rws, cols), F32)] * 4, compiler_params=_params(),
    )(c_all, dmod_cols, w, m, v)


def _small_reduce(packs):
    def body(p_ref, o_ref):
        tot = p_ref[0]
        for dev in range(1, N_DEV):
            tot = tot + p_ref[dev]
        o_ref[...] = tot
        o_ref[16:17, :] = jnp.zeros((1, D), F32) + (0.5 / D) * jnp.sum(tot[16:17, :])

    return _pcall(body, name="small_reduce", out_shape=_sds((PACK_ROWS, D), F32))(packs)


def _adamw_many(ws, gs, ms, vs):
    k = len(ws)

    def body(*refs):
        for i in range(k):
            w_ref, g_ref, m_ref, v_ref = refs[i], refs[k + i], refs[2 * k + i], refs[3 * k + i]
            d_ref, mo_ref, vo_ref = refs[4 * k + i], refs[5 * k + i], refs[6 * k + i]
            d_ref[...], mo_ref[...], vo_ref[...] = _adamw(w_ref[...], g_ref[...], m_ref[...], v_ref[...])

    shp = [_sds(w.shape, F32) for w in ws]
    outs = _pcall(body, name="adamw_small", out_shape=shp * 3)(*ws, *gs, *ms, *vs)
    return outs[:k], outs[k:2 * k], outs[2 * k:]


def kernel(x, c, positions, w_ada, b_ada, norm1_w, w_in, conv_w, conv_b, dt_bias, a_log, d_skip, attn_sinks, ssm_norm_w, w_out, norm2_w, w_gate_up, w_down, final_norm_w, loss_target, m_w_ada, m_b_ada, m_norm1_w, m_w_in, m_conv_w, m_conv_b, m_dt_bias, m_a_log, m_d_skip, m_attn_sinks, m_ssm_norm_w, m_w_out, m_norm2_w, m_w_gate_up, m_w_down, m_final_norm_w, v_w_ada, v_b_ada, v_norm1_w, v_w_in, v_conv_w, v_conv_b, v_dt_bias, v_a_log, v_d_skip, v_attn_sinks, v_ssm_norm_w, v_w_out, v_norm2_w, v_w_gate_up, v_w_down, v_final_norm_w):
    s = x.shape[1]
    me = 4 * lax.axis_index("x") + 2 * lax.axis_index("y") + lax.axis_index("c")
    ada_cols = N_MOD * D // N_DEV

    c8 = jnp.pad(c, ((0, 7), (0, 0)))
    cw8 = jnp.pad(conv_w[0], ((0, 8 - CONVK), (0, 0)))
    w_in_t, m_w_in_t, v_w_in_t = jnp.transpose(w_in[0]), jnp.transpose(m_w_in[0]), jnp.transpose(v_w_in[0])
    w_gu_t, m_w_gu_t, v_w_gu_t = (jnp.transpose(w_gate_up[0]), jnp.transpose(m_w_gate_up[0]),
                                  jnp.transpose(v_w_gate_up[0]))
    b_cols = lax.dynamic_slice(b_ada, (0, me * ada_cols), (1, ada_cols))
    g_c, g_in, g_cw, g_mod = _prologue(c8, w_in_t.astype(BF16), cw8, w_ada[0], b_cols)
    c_all = g_c[:, 0, :]
    w_in_f = jnp.pad(g_in.reshape(IN_PROJ, D), ((0, INP - IN_PROJ), (0, 0)))
    conv_w8 = jnp.transpose(g_cw, (1, 0, 2)).reshape(8, D)
    mod = lax.dynamic_index_in_dim(g_mod, me, axis=1, keepdims=False).reshape(N_MOD, D)
    mod8 = jnp.pad(mod, ((0, 8 - N_MOD), (0, 0)))

    half = HD // 2
    inv_freq = ROPE_THETA ** (-jnp.arange(half, dtype=F32) / half)
    invf = jnp.tile(inv_freq, LB // half).reshape(1, LB)
    lanes = lambda a: jnp.pad(a, ((0, 0), (0, LB - a.shape[1])))
    ssm_p = jnp.pad(jnp.concatenate([lanes(dt_bias), lanes(a_log), lanes(d_skip)], axis=0), ((0, 5), (0, 0)))
    sinks8 = jnp.broadcast_to(attn_sinks.reshape(NQ, 1), (NQ, LB))

    xs, tgt, fnw = x[0], loss_target[0], final_norm_w.reshape(1, D)

    q, k, v, z, xbc, dtr, h1, cos, sin = _inproj_fwd(xs, positions[0].reshape(s, 1), invf, mod8, norm1_w, w_in_f)
    (attn, yn, y, hs, conv_u, dtv, acs, decay, probs, psink), (g_out, g_gu, g_down) = _mixer_fwd(
        q, k, v, sinks8, xbc, conv_w8, conv_b, dtr, ssm_p, z, ssm_norm_w,
        [w_out[0].astype(BF16), w_gu_t.astype(BF16), w_down[0].astype(BF16)])
    w_out_f = g_out.reshape(D, D)
    w_gu_f = g_gu.reshape(2 * DFF, D)
    w_down_f = g_down.reshape(DFF, D)
    x2, h2, mo, mix, gu, act, dx3, sm_f = _outproj_ffn_fwd_loss(attn, yn, xs, tgt, mod8, norm2_w, fnw, w_out_f, w_gu_f,
                                                                 w_down_f)

    dx2, dff, dgu, dmix, dattn, dyn, sm_b = _ffn_bwd(dx3, gu, x2, mo, mod8, norm2_w, w_gu_f, w_down_f, w_out_f)
    p_gu = _wgrad(dgu, h2, "wgrad_gate_up").reshape(N_DEV, 2 * DFF // N_DEV, D)
    p_down = _wgrad(act, dff, "wgrad_down").reshape(N_DEV, DFF // N_DEV, D)
    p_out = _wgrad(mix, dmix, "wgrad_out").reshape(N_DEV, D // N_DEV, D)
    dproj, dsink, sm_s, (r_gu, r_down, r_out) = _mixer_bwd(
        q, k, v, attn, dattn, cos, sin, probs, psink, dyn, y, z, xbc, conv_u, dtv, acs, decay, conv_w8, ssm_p,
        ssm_norm_w, hs, [p_gu, p_down, p_out])
    p_in = _wgrad(dproj, h1, "wgrad_in")[:IN_PROJ].reshape(N_DEV, IN_PROJ // N_DEV, D)
    gx, (r_in,), g_pack = _inproj_bwd(dproj, xs, dx2, mod8, norm1_w, w_in_f, [p_in], (sm_f, sm_b, sm_s, dsink))

    tot = _small_reduce(g_pack)
    loss = tot[16, 0]
    dmod_all = g_pack[:, 0:N_MOD, :].reshape(N_DEV, N_MOD * D)
    dmod_cols = lax.dynamic_slice(dmod_all, (0, me * ada_cols), (N_DEV, ada_cols))

    big = {
        "w_ada": _wada_adamw(c_all, dmod_cols, w_ada[0], m_w_ada[0], v_w_ada[0]),
        "w_in": [jnp.transpose(t) for t in _sum_adamw(r_in, w_in_t, m_w_in_t, v_w_in_t, "adamw_in")],
        "w_out": _sum_adamw(r_out, w_out[0], m_w_out[0], v_w_out[0], "adamw_out"),
        "w_gate_up": [jnp.transpose(t) for t in _sum_adamw(r_gu, w_gu_t, m_w_gu_t, v_w_gu_t, "adamw_gate_up")],
        "w_down": _sum_adamw(r_down, w_down[0], m_w_down[0], v_w_down[0], "adamw_down"),
    }
    small_names = ["b_ada", "norm1_w", "conv_w", "conv_b", "dt_bias", "a_log", "d_skip", "attn_sinks", "ssm_norm_w",
                   "norm2_w", "final_norm_w"]
    row15 = tot[15:16, :]
    small_g = {
        "b_ada": tot[0:N_MOD, :].reshape(1, N_MOD * D),
        "norm1_w": tot[6:7, :],
        "conv_w": lax.dynamic_slice(tot[10:14, :], (0, me * (D // N_DEV)), (CONVK, D // N_DEV)),
        "conv_b": tot[9:10, :],
        "dt_bias": row15[:, 0:NH],
        "a_log": row15[:, LB:LB + NH],
        "d_skip": row15[:, 2 * LB:2 * LB + NH],
        "attn_sinks": row15[:, 3 * LB:3 * LB + NQ],
        "ssm_norm_w": tot[14:15, 0:SW],
        "norm2_w": tot[7:8, :],
        "final_norm_w": tot[8:9, :],
    }
    small_w = {"b_ada": b_ada, "norm1_w": norm1_w, "conv_w": conv_w[0], "conv_b": conv_b, "dt_bias": dt_bias,
               "a_log": a_log, "d_skip": d_skip, "attn_sinks": attn_sinks, "ssm_norm_w": ssm_norm_w,
               "norm2_w": norm2_w, "final_norm_w": final_norm_w.reshape(1, D)}
    small_m = {"b_ada": m_b_ada, "norm1_w": m_norm1_w, "conv_w": m_conv_w[0], "conv_b": m_conv_b,
               "dt_bias": m_dt_bias, "a_log": m_a_log, "d_skip": m_d_skip, "attn_sinks": m_attn_sinks,
               "ssm_norm_w": m_ssm_norm_w, "norm2_w": m_norm2_w, "final_norm_w": m_final_norm_w.reshape(1, D)}
    small_v = {"b_ada": v_b_ada, "norm1_w": v_norm1_w, "conv_w": v_conv_w[0], "conv_b": v_conv_b,
               "dt_bias": v_dt_bias, "a_log": v_a_log, "d_skip": v_d_skip, "attn_sinks": v_attn_sinks,
               "ssm_norm_w": v_ssm_norm_w, "norm2_w": v_norm2_w, "final_norm_w": v_final_norm_w.reshape(1, D)}
    s_d, s_m, s_v = _adamw_many([small_w[k] for k in small_names], [small_g[k] for k in small_names],
                                [small_m[k] for k in small_names], [small_v[k] for k in small_names])

    order = ["w_ada", "b_ada", "norm1_w", "w_in", "conv_w", "conv_b", "dt_bias", "a_log", "d_skip", "attn_sinks",
             "ssm_norm_w", "w_out", "norm2_w", "w_gate_up", "w_down", "final_norm_w"]
    lead = {"w_ada", "w_in", "conv_w", "w_out", "w_gate_up", "w_down"}
    grads, deltas, new_m, new_v = [], [], [], []
    for name in order:
        if name in big:
            g, d, m2, v2 = big[name]
        else:
            i = small_names.index(name)
            g, d, m2, v2 = small_g[name], s_d[i], s_m[i], s_v[i]
        if name in lead:
            g, d, m2, v2 = g[None], d[None], m2[None], v2[None]
        if name == "final_norm_w":
            g, d, m2, v2 = g.reshape(D), d.reshape(D), m2.reshape(D), v2.reshape(D)
        grads.append(g)
        deltas.append(d)
        new_m.append(m2)
        new_v.append(v2)
    return (loss, gx[None], *grads, *deltas, *new_m, *new_v)
```

```python
import functools
import math

import jax
import jax.numpy as jnp
from jax import lax
from jax.experimental import pallas as pl
from jax.experimental.pallas import tpu as pltpu

F32 = jnp.float32
BF16 = jnp.bfloat16

N_DEV = 8
D = 1024
HD = 64
NQ = 8
AW = 512
KVW = 128
SW = 512
NST = 128
NH = 8
LB = 128
CONVK = 4
DFF = 2816
N_MOD = 6
IN_PROJ = 2312
INP = 2432
O_Q, O_K, O_V, O_Z, O_XBC, O_DT = 0, 512, 640, 768, 1280, 2304
ZXD = INP - O_Z
EPS = 1e-6
NEG = -1e30
ROPE_THETA = 10000.0
VMEM_LIMIT = 56 * 1024 * 1024

ADAM_LR = 0.001
ADAM_B1 = 0.9
ADAM_B2 = 0.999
ADAM_EPS = 1e-08
ADAM_WD = 0.01
ADAM_STEP = 10

NT_DIMS = (((1,), (1,)), ((), ()))
TN_DIMS = (((0,), (0,)), ((), ()))


def _pcall(body, **kw):
    return pl.pallas_call(body, **kw)


def _sds(shape, dtype):
    return jax.ShapeDtypeStruct(shape, dtype)


def _params(n_grid=1):
    return pltpu.CompilerParams(dimension_semantics=("arbitrary",) * n_grid, vmem_limit_bytes=VMEM_LIMIT)


def _const(shape):
    return pl.BlockSpec(shape, lambda *_: (0,) * len(shape), pipeline_mode=pl.Buffered(1))


def _largest_divisor(n, candidates):
    for cand in candidates:
        if n % cand == 0:
            return cand
    raise ValueError(f"no tile in {candidates} divides {n}")


def _rows(t, w):
    return pl.BlockSpec((t, w), lambda i: (i, 0))


def _dot(a, b):
    return jnp.dot(a, b, preferred_element_type=F32)


def _dot_nt(a, b):
    return lax.dot_general(a, b, NT_DIMS, preferred_element_type=F32)


def _dot_tn(a, b):
    return lax.dot_general(a, b, TN_DIMS, preferred_element_type=F32)


def _sigmoid(v):
    return 1.0 / (1.0 + jnp.exp(-v))


def _softplus(v):
    return jnp.maximum(v, 0.0) + jnp.log1p(jnp.exp(-jnp.abs(v)))


def _rope_sign_mask(shape):
    lane = lax.broadcasted_iota(jnp.int32, shape, 1)
    return (lane % HD) < (HD // 2)


def _rope(t, cs, sn, inverse):
    r_dn = pltpu.roll(t, HD // 2, 1)
    r_up = pltpu.roll(t, LB - HD // 2, 1)
    first = _rope_sign_mask(t.shape)
    if inverse:
        rot = jnp.where(first, r_up, -r_dn)
    else:
        rot = jnp.where(first, -r_up, r_dn)
    return t * cs + rot * sn


def _norm_mod_fwd(xv, nw, shift, scale):
    r = lax.rsqrt(jnp.mean(xv * xv, axis=-1, keepdims=True) + EPS)
    xh = xv * r
    return (xh * nw) * (1.0 + scale) + shift


def _norm_mod_bwd(xv, dh, nw, scale):
    r = lax.rsqrt(jnp.mean(xv * xv, axis=-1, keepdims=True) + EPS)
    xh = xv * r
    xn = xh * nw
    d_shift = jnp.sum(dh, axis=0, keepdims=True)
    d_scale = jnp.sum(dh * xn, axis=0, keepdims=True)
    dxn = dh * (1.0 + scale)
    d_w = jnp.sum(dxn * xh, axis=0, keepdims=True)
    dxh = dxn * nw
    dx = r * (dxh - xh * jnp.mean(dxh * xh, axis=-1, keepdims=True))
    return dx, d_shift, d_scale, d_w


def _inproj_fwd(x, pos, invf, mod8, n1w, w_in):
    s = x.shape[0]
    tt = min(512, s)

    def body(x_ref, pos_ref, invf_ref, mod_ref, nw_ref, w_ref,
             q_ref, k_ref, v_ref, z_ref, xbc_ref, dtr_ref, h1_ref, cos_ref, sin_ref):
        h = _norm_mod_fwd(x_ref[...], nw_ref[...], mod_ref[0:1, :], mod_ref[1:2, :])
        hb = h.astype(BF16)
        h1_ref[...] = hb
        proj = _dot_nt(hb, w_ref[...])
        ang = pos_ref[...].astype(F32) * invf_ref[...]
        cs = jnp.cos(ang)
        sn = jnp.sin(ang)
        cos_ref[...] = cs
        sin_ref[...] = sn
        for a in range(AW // LB):
            q_ref[:, a * LB:(a + 1) * LB] = _rope(proj[:, O_Q + a * LB:O_Q + (a + 1) * LB], cs, sn, False).astype(BF16)
        k_ref[...] = _rope(proj[:, O_K:O_V], cs, sn, False).astype(BF16)
        v_ref[...] = proj[:, O_V:O_Z].astype(BF16)
        z_ref[...] = proj[:, O_Z:O_XBC]
        xbc_ref[...] = proj[:, O_XBC:O_DT]
        dtr_ref[...] = proj[:, O_DT:INP]

    return _pcall(
        body, name="inproj_fwd", grid=(s // tt,),
        in_specs=[_rows(tt, D), _rows(tt, 1), _const((1, LB)), _const((8, D)), _const((1, D)), _const((INP, D))],
        out_specs=[_rows(tt, AW), _rows(tt, KVW), _rows(tt, KVW), _rows(tt, SW), _rows(tt, D), _rows(tt, LB),
                   _rows(tt, D), _rows(tt, LB), _rows(tt, LB)],
        out_shape=[_sds((s, AW), BF16), _sds((s, KVW), BF16), _sds((s, KVW), BF16), _sds((s, SW), F32),
                   _sds((s, D), F32), _sds((s, LB), F32), _sds((s, D), BF16), _sds((s, LB), F32), _sds((s, LB), F32)],
        compiler_params=_params(),
    )(x, pos, invf, mod8, n1w, w_in)


QPG = 4
ATT_SCALE = 1.0 / math.sqrt(HD)


def _stack_heads(val, g):
    return jnp.concatenate([val[:, (QPG * g + hh) * HD:(QPG * g + hh + 1) * HD] for hh in range(QPG)], axis=0)


def _unstack_heads(groups):
    pieces = [grp[hh * LB:(hh + 1) * LB, :] for grp in groups for hh in range(QPG)]
    return [jnp.concatenate(pieces[2 * a:2 * a + 2], axis=1) for a in range(NQ // 2)]


def _upper_mask():
    row = lax.broadcasted_iota(jnp.int32, (QPG * LB, LB), 0)
    col = lax.broadcasted_iota(jnp.int32, (QPG * LB, LB), 1)
    return col > (row % LB)


def _sink_wide(sinks, g):
    return jnp.concatenate([jnp.broadcast_to(sinks[QPG * g + hh:QPG * g + hh + 1, 0:1], (LB, LB))
                            for hh in range(QPG)], axis=0)


def _row_sums_wide(v, terms):
    return _dot_sel(v, jnp.ones((v.shape[1], LB), BF16), terms)


def _band(upper, prev_part, cur_part):
    return jnp.where(upper, prev_part, cur_part)


def _attn_scores(n, qg, kcat, upper):
    sp = _dot_nt(qg, kcat[0:LB, :]) * ATT_SCALE
    sc = _dot_nt(qg, kcat[LB:2 * LB, :]) * ATT_SCALE
    return _band(upper, jnp.where(n > 0, sp, NEG), sc)


def _attn_softmax(comb, sink):
    m = jnp.maximum(jnp.max(comb, axis=-1, keepdims=True), sink)
    p = jnp.exp(comb - m)
    es = jnp.exp(sink - m)
    return p, es, _row_sums_wide(p, 1) + es


def _attn_fwd_block(n, q_ref, kp_ref, kc_ref, vp_ref, vc_ref, sink_ref, o_ref, pr_ref, ps_ref):
    qv = q_ref[...]
    kcat = jnp.concatenate([kp_ref[...], kc_ref[...]], axis=0)
    vcat = jnp.concatenate([vp_ref[...], vc_ref[...]], axis=0)
    sinks = sink_ref[...]
    upper = _upper_mask()
    outs = []
    for g in range(NQ // QPG):
        sl = slice(g * HD, (g + 1) * HD)
        rows = slice(g * QPG * LB, (g + 1) * QPG * LB)
        p, es, denom = _attn_softmax(_attn_scores(n, _stack_heads(qv, g), kcat[:, sl], upper), _sink_wide(sinks, g))
        rden = 1.0 / denom
        pr_ref[0, rows, :] = (p * rden).astype(BF16)
        ps_ref[0, rows, :] = (es * rden).astype(BF16)
        outs.append((_dot(jnp.where(upper, p, 0.0).astype(BF16), vcat[0:LB, sl])
                     + _dot(jnp.where(upper, 0.0, p).astype(BF16), vcat[LB:2 * LB, sl])) * rden[:, 0:HD])
    for g, grp in enumerate(outs):
        for hh in range(QPG):
            h = QPG * g + hh
            o_ref[:, h * HD:(h + 1) * HD] = grp[hh * LB:(hh + 1) * LB, :].astype(BF16)


def _cumsum_rows(a, reverse):
    row = lax.broadcasted_iota(jnp.int32, a.shape, 0)
    step = 1
    while step < LB:
        if reverse:
            a = a + jnp.where(row < LB - step, pltpu.roll(a, LB - step, 0), 0.0)
        else:
            a = a + jnp.where(row >= step, pltpu.roll(a, step, 0), 0.0)
        step *= 2
    return a


SUB = 8


def _conv_shifts(tail, cur):
    row = lax.broadcasted_iota(jnp.int32, tail.shape, 0)
    out = [cur]
    for j in range(1, CONVK):
        rolled = pltpu.roll(cur, j, 0)
        top = jnp.where(row < j, pltpu.roll(tail, j, 0), rolled[0:SUB, :])
        out.append(jnp.concatenate([top, rolled[SUB:, :]], axis=0))
    return out


def _conv_advances(du, head):
    row = lax.broadcasted_iota(jnp.int32, head.shape, 0)
    out = []
    for j in range(1, CONVK):
        rolled = pltpu.roll(du, LB - j, 0)
        bottom = jnp.where(row >= SUB - j, pltpu.roll(head, SUB - j, 0), rolled[LB - SUB:, :])
        out.append(jnp.concatenate([rolled[:LB - SUB, :], bottom], axis=0))
    return out


def _split(v, terms):
    out = []
    for _ in range(terms - 1):
        t = v.astype(BF16)
        out.append(t)
        v = v - t.astype(F32)
    out.append(v.astype(BF16))
    return out


def _dot_sel(v, sel, terms):
    parts = [_dot(t, sel) for t in _split(v, terms)]
    return functools.reduce(lambda a, b: a + b, parts)


def _dot_nt_sel(v, sel, terms):
    parts = [_dot_nt(t, sel) for t in _split(v, terms)]
    return functools.reduce(lambda a, b: a + b, parts)


def _ssd_pre(xt_ref, xc_ref, cw_ref, cb_ref, dtr_ref, sp_ref, n):
    cur = xc_ref[...]
    tail = jnp.where(n > 0, xt_ref[...], 0.0)
    sh = _conv_shifts(tail, cur)
    u = cb_ref[...] + cw_ref[CONVK - 1:CONVK, :] * sh[0]
    for j in range(1, CONVK):
        u = u + cw_ref[CONVK - 1 - j:CONVK - j, :] * sh[j]
    dt = _softplus(dtr_ref[...] + sp_ref[0:1, :])
    acs = _cumsum_rows(dt * -jnp.exp(sp_ref[1:2, :]), False)
    return u, dt, acs


def _gated_norm_fwd(y, z, sgz, nw):
    yz = y * (z * sgz)
    parts = []
    for g in range(2):
        t = yz[:, g * 256:(g + 1) * 256]
        parts.append(t * lax.rsqrt(jnp.mean(t * t, axis=-1, keepdims=True) + EPS))
    return jnp.concatenate(parts, axis=1) * nw


HPG = 4
GW = HPG * HD


class _SsdChunk:
    def __init__(self, xc, dt, acs, spv, e64, e128, decay=None):
        self.e64, self.e128 = e64, e128
        alast = acs[LB - 1:LB, :]
        self.e_all = jnp.exp(acs)
        self.dte_all = jnp.exp(alast - acs)
        self.elast = jnp.exp(alast)
        wide = _dot_sel(jnp.concatenate([dt, self.e_all, self.dte_all], axis=0), e64, 2)
        self.dt_x, self.e_x, self.dte_x = wide[0:LB], wide[LB:2 * LB], wide[2 * LB:3 * LB]
        self.dsk_x = _dot_sel(spv, e64, 3)[2:3, :]
        if decay is None:
            acs_t = jnp.transpose(acs)
            ac_x = _dot_sel(acs, e128, 3)
            row = lax.broadcasted_iota(jnp.int32, (HPG * LB, LB), 0)
            col = lax.broadcasted_iota(jnp.int32, (HPG * LB, LB), 1)
            causal = (row % LB) >= col
        lane = lax.broadcasted_iota(jnp.int32, (LB, GW), 1)
        self.head_lanes = [(lane >= hh * HD) & (lane < (hh + 1) * HD) for hh in range(HPG)]
        self.xs, self.xdt, self.b, self.c, self.bb, self.cb16, self.cbm, self.dm_st, self.m_st = ([] for _ in range(9))
        for g in range(2):
            heads = range(HPG * g, HPG * (g + 1))
            if decay is None:
                ac_st = jnp.concatenate([ac_x[:, j * LB:(j + 1) * LB] for j in heads], axis=0)
                ar_st = jnp.concatenate([jnp.broadcast_to(acs_t[j:j + 1, :], (LB, LB)) for j in heads], axis=0)
                dm_st = jnp.exp(jnp.where(causal, ac_st - ar_st, NEG))
            else:
                dm_st = decay[g]
            bg = xc[:, SW + g * NST:SW + (g + 1) * NST]
            cg = xc[:, SW + 2 * NST + g * NST:SW + 2 * NST + (g + 1) * NST]
            bgb, cgb = bg.astype(BF16), cg.astype(BF16)
            cbm = _dot_nt(cgb, bgb)
            xs_g = xc[:, g * GW:(g + 1) * GW]
            self.xs.append(xs_g)
            self.xdt.append(xs_g * self.dt_x[:, g * GW:(g + 1) * GW])
            self.b.append(bg)
            self.c.append(cg)
            self.bb.append(bgb)
            self.cb16.append(cgb)
            self.cbm.append(cbm)
            self.dm_st.append(dm_st)
            self.m_st.append(jnp.concatenate([cbm] * HPG, axis=0) * dm_st)

    def elast_rows(self, g):
        return jnp.concatenate([jnp.broadcast_to(self.elast[:, j:j + 1], (HD, NST))
                                for j in range(HPG * g, HPG * (g + 1))], axis=0)

    def diag_blocks(self, stacked):
        out = stacked[(HPG - 1) * LB:HPG * LB, :]
        for hh in range(HPG - 2, -1, -1):
            out = jnp.where(self.head_lanes[hh], stacked[hh * LB:(hh + 1) * LB, :], out)
        return out

    def block_diag(self, v):
        return jnp.concatenate([jnp.where(self.head_lanes[hh], v, 0.0) for hh in range(HPG)], axis=0)


def _ssd_fwd_block(n, xt_ref, xc_ref, cw_ref, cb_ref, dtr_ref, sp_ref, z_ref, nw_ref, e64_ref, e128_ref,
                   yn_ref, y_ref, hs_ref, u_ref, dt_ref, acs_ref, dm_ref, h_scr):
    @pl.when(n == 0)
    def _():
        h_scr[...] = jnp.zeros_like(h_scr)

    h_all = h_scr[...]
    hs_ref[0] = h_all
    u, dt, acs = _ssd_pre(xt_ref, xc_ref, cw_ref, cb_ref, dtr_ref, sp_ref, n)
    u_ref[...] = u
    dt_ref[...] = dt
    acs_ref[...] = acs
    xc = u * _sigmoid(u)
    ck = _SsdChunk(xc, dt, acs, sp_ref[...], e64_ref[...], e128_ref[...])
    dm_ref[0] = jnp.concatenate(ck.dm_st, axis=0)
    ys, hn = [], []
    for g in range(2):
        gl = slice(g * GW, (g + 1) * GW)
        xdt = ck.xdt[g]
        hg = h_all[gl, :]
        y_diag = ck.diag_blocks(_dot(ck.m_st[g].astype(BF16), xdt.astype(BF16)))
        y_off = ck.e_x[:, gl] * _dot_nt(ck.cb16[g], hg.astype(BF16))
        ys.append(y_diag + y_off + ck.xs[g] * ck.dsk_x[:, gl])
        hn.append(hg * ck.elast_rows(g) + _dot_tn((xdt * ck.dte_x[:, gl]).astype(BF16), ck.bb[g]))
    h_scr[...] = jnp.concatenate(hn, axis=0)
    y = jnp.concatenate(ys, axis=1)
    y_ref[...] = y
    z = z_ref[...]
    yn_ref[...] = _gated_norm_fwd(y, z, _sigmoid(z), nw_ref[...]).astype(BF16)


def _mixer_fwd(q, k, v, sinks8, xbc, conv_w8, conv_b, dtr, ssm_p, z, nw, gathers):
    s = q.shape[0]
    nb = s // LB
    bps = _blocks_per_step(nb)
    nsteps = nb // bps
    tl = bps * LB
    cur = lambda n: (n, 0)
    prev = lambda n: (jnp.maximum(n * bps - 1, 0), 0)
    items, ex_shapes, n_g = _exchange_items(gathers, [])
    ne = len(items)

    n_in, n_out = 16, 10
    relay_step = (3 * (nsteps - 1)) // 4
    stack = pl.BlockSpec((bps, NH * LB, LB), lambda n: (n, 0, 0))
    e64, e128 = _head_expanders()

    def body(*refs):
        (q_ref, kp_ref, kc_ref, vp_ref, vc_ref, sink_ref, xt_ref, xc_ref, cw_ref, cb_ref, dtr_ref, sp_ref, z_ref,
         nw_ref, e64_ref, e128_ref) = refs[:n_in]
        ex_in = refs[n_in:n_in + ne]
        (o_ref, yn_ref, y_ref, hs_ref, u_ref, dt_ref, acs_ref, dm_ref, pr_ref,
         ps_ref) = refs[n_in + ne:n_in + n_out + ne]
        ex_out = refs[n_in + n_out + ne:n_in + n_out + 2 * ne]
        h_scr = refs[n_in + n_out + 2 * ne]
        sems = refs[n_in + n_out + 1 + 2 * ne:]
        n = pl.program_id(0)

        @pl.when(n == 0)
        def _():
            _Exchange(n_g, ex_in, ex_out, sems).two_level_start()

        for sub in range(bps):
            blk = n * bps + sub
            r = slice(sub * LB, (sub + 1) * LB)
            before = slice((sub - 1) * LB, sub * LB)
            one = slice(sub, sub + 1)
            _attn_fwd_block(blk, q_ref.at[r], kp_ref if sub == 0 else kc_ref.at[before], kc_ref.at[r],
                            vp_ref if sub == 0 else vc_ref.at[before], vc_ref.at[r], sink_ref,
                            o_ref.at[r], pr_ref.at[one], ps_ref.at[one])
            _ssd_fwd_block(blk, xt_ref if sub == 0 else xc_ref.at[sub * LB - SUB:sub * LB], xc_ref.at[r], cw_ref,
                           cb_ref, dtr_ref.at[r], sp_ref, z_ref.at[r], nw_ref, e64_ref, e128_ref,
                           yn_ref.at[r], y_ref.at[r], hs_ref.at[one], u_ref.at[r], dt_ref.at[r], acs_ref.at[r],
                           dm_ref.at[one], h_scr)

        @pl.when(n == relay_step)
        def _():
            _Exchange(n_g, ex_in, ex_out, sems).two_level_relay()

        @pl.when(n == nsteps - 1)
        def _():
            _Exchange(n_g, ex_in, ex_out, sems).two_level_finish()

    any_spec = pl.BlockSpec(memory_space=pl.ANY)
    tail = pl.BlockSpec((SUB, D), lambda n: (jnp.maximum(n * (tl // SUB) - 1, 0), 0))
    outs = _pcall(
        body, name="mixer_fwd", grid=(nsteps,),
        in_specs=[pl.BlockSpec((tl, AW), cur), pl.BlockSpec((LB, KVW), prev), pl.BlockSpec((tl, KVW), cur),
                  pl.BlockSpec((LB, KVW), prev), pl.BlockSpec((tl, KVW), cur), _const((8, LB)),
                  tail, pl.BlockSpec((tl, D), cur), _const((8, D)), _const((1, D)),
                  pl.BlockSpec((tl, LB), cur), _const((8, LB)), pl.BlockSpec((tl, SW), cur), _const((1, SW)),
                  _const(e64.shape), _const(e128.shape)]
        + [any_spec] * ne,
        out_specs=[pl.BlockSpec((tl, AW), cur), pl.BlockSpec((tl, SW), cur), pl.BlockSpec((tl, SW), cur),
                   pl.BlockSpec((bps, NH * HD, NST), lambda n: (n, 0, 0)), pl.BlockSpec((tl, D), cur),
                   pl.BlockSpec((tl, LB), cur), pl.BlockSpec((tl, LB), cur),
                   stack, stack, stack] + [any_spec] * ne,
        out_shape=[_sds((s, AW), BF16), _sds((s, SW), BF16), _sds((s, SW), F32), _sds((nb, NH * HD, NST), F32),
                   _sds((s, D), F32), _sds((s, LB), F32), _sds((s, LB), F32), _sds((nb, NH * LB, LB), F32),
                   _sds((nb, NH * LB, LB), BF16), _sds((nb, NH * LB, LB), BF16)]
        + ex_shapes,
        scratch_shapes=[pltpu.VMEM((NH * HD, NST), F32)] + _exchange_sems(ne),
        compiler_params=_params(),
    )(q, k, k, v, v, sinks8, xbc, xbc, conv_w8, conv_b, dtr, ssm_p, z, nw, e64, e128, *items)
    return outs[:n_out], outs[n_out:]


def _blocks_per_step(nb):
    return next(b for b in (4, 2, 1) if nb % b == 0)


def _head_expanders():
    j = lax.broadcasted_iota(jnp.int32, (LB, NH * HD), 0)
    e64 = (lax.broadcasted_iota(jnp.int32, (LB, NH * HD), 1) // HD == j).astype(BF16)
    j = lax.broadcasted_iota(jnp.int32, (LB, NH * LB), 0)
    e128 = (lax.broadcasted_iota(jnp.int32, (LB, NH * LB), 1) // LB == j).astype(BF16)
    return e64, e128


def _outproj_ffn_fwd_loss(attn, yn, x, tgt, mod8, n2w, fnw, w_out, w_gu_t, w_down):
    s = x.shape[0]
    tf = min(256, s)

    def body(a_ref, y_ref, x_ref, t_ref, mod_ref, nw_ref, fw_ref, wo_ref, wgu_ref, wd_ref,
             x2_ref, h2_ref, mo_ref, mix_ref, gu_ref, act_ref, dx3_ref, sm_ref):
        i = pl.program_id(0)

        @pl.when(i == 0)
        def _():
            sm_ref[...] = jnp.zeros_like(sm_ref)

        mix = jnp.concatenate([a_ref[...], y_ref[...]], axis=1)
        mix_ref[...] = mix
        mo = _dot(mix, wo_ref[...])
        mo_ref[...] = mo.astype(BF16)
        x2 = x_ref[...] + mod_ref[2:3, :] * mo
        x2_ref[...] = x2
        h2 = _norm_mod_fwd(x2, nw_ref[...], mod_ref[3:4, :], mod_ref[4:5, :]).astype(BF16)
        h2_ref[...] = h2
        gu = _dot_nt(h2, wgu_ref[...])
        gu_ref[...] = gu.astype(BF16)
        g = gu[:, :DFF]
        act = (g * _sigmoid(g) * gu[:, DFF:]).astype(BF16)
        act_ref[...] = act
        ff = _dot(act, wd_ref[...])
        x3 = x2 + mod_ref[5:6, :] * ff
        r = lax.rsqrt(jnp.mean(x3 * x3, axis=-1, keepdims=True) + EPS)
        xh = x3 * r
        fw = fw_ref[...]
        err = xh * fw - t_ref[...]
        dy = err * (1.0 / D)
        dxh = dy * fw
        dx3 = r * (dxh - xh * jnp.mean(dxh * xh, axis=-1, keepdims=True))
        dx3_ref[...] = dx3
        sm_ref[0:1, :] += jnp.sum(dx3 * ff, axis=0, keepdims=True)
        sm_ref[1:2, :] += jnp.sum(dy * xh, axis=0, keepdims=True)
        sm_ref[2:3, :] += jnp.sum(err * err, axis=0, keepdims=True)

    return _pcall(
        body, name="outproj_ffn_fwd_loss", grid=(s // tf,),
        in_specs=[_rows(tf, AW), _rows(tf, SW), _rows(tf, D), _rows(tf, D), _const((8, D)), _const((1, D)),
                  _const((1, D)), _const((D, D)), _const((2 * DFF, D)), _const((DFF, D))],
        out_specs=[_rows(tf, D), _rows(tf, D), _rows(tf, D), _rows(tf, D), _rows(tf, 2 * DFF), _rows(tf, DFF),
                   _rows(tf, D), pl.BlockSpec((8, D), lambda i: (0, 0))],
        out_shape=[_sds((s, D), F32), _sds((s, D), BF16), _sds((s, D), BF16), _sds((s, D), BF16),
                   _sds((s, 2 * DFF), BF16), _sds((s, DFF), BF16), _sds((s, D), F32), _sds((8, D), F32)],
        compiler_params=_params(),
    )(attn, yn, x, tgt, mod8, n2w, fnw, w_out, w_gu_t, w_down)


def _ffn_bwd(dx3, gu, x2, mixout, mod8, n2w, w_gu, w_down, w_out):
    s = x2.shape[0]
    tb = min(256, s)

    def body(dx3_ref, gu_ref, x2_ref, mo_ref, mod_ref, nw_ref, wgu_ref, wd_ref, wo_ref,
             dx2_ref, dff_ref, dgu_ref, dmix_ref, dattn_ref, dyn_ref, sm_ref):
        i = pl.program_id(0)

        @pl.when(i == 0)
        def _():
            sm_ref[...] = jnp.zeros_like(sm_ref)

        dx3 = dx3_ref[...]
        dff = (dx3 * mod_ref[5:6, :]).astype(BF16)
        dff_ref[...] = dff
        dact = _dot_nt(dff, wd_ref[...])
        g = gu_ref[:, :DFF].astype(F32)
        u = gu_ref[:, DFF:].astype(F32)
        sg = _sigmoid(g)
        dgu = jnp.concatenate([dact * u * sg * (1.0 + g * (1.0 - sg)), dact * g * sg], axis=1).astype(BF16)
        dgu_ref[...] = dgu
        dh2 = _dot(dgu, wgu_ref[...])
        dxn, d_shift, d_scale, d_w = _norm_mod_bwd(x2_ref[...], dh2, nw_ref[...], mod_ref[4:5, :])
        dx2 = dx3 + dxn
        dx2_ref[...] = dx2
        sm_ref[0:1, :] += d_shift
        sm_ref[1:2, :] += d_scale
        sm_ref[2:3, :] += d_w
        sm_ref[3:4, :] += jnp.sum(dx2 * mo_ref[...].astype(F32), axis=0, keepdims=True)
        dmix = (dx2 * mod_ref[2:3, :]).astype(BF16)
        dmix_ref[...] = dmix
        dmi = _dot_nt(dmix, wo_ref[...])
        dattn_ref[...] = dmi[:, :AW].astype(BF16)
        dyn_ref[...] = dmi[:, AW:]

    return _pcall(
        body, name="ffn_bwd", grid=(s // tb,),
        in_specs=[_rows(tb, D), _rows(tb, 2 * DFF), _rows(tb, D), _rows(tb, D), _const((8, D)), _const((1, D)),
                  _const((2 * DFF, D)), _const((DFF, D)), _const((D, D))],
        out_specs=[_rows(tb, D), _rows(tb, D), _rows(tb, 2 * DFF), _rows(tb, D), _rows(tb, AW), _rows(tb, SW),
                   pl.BlockSpec((8, D), lambda i: (0, 0))],
        out_shape=[_sds((s, D), F32), _sds((s, D), BF16), _sds((s, 2 * DFF), BF16), _sds((s, D), BF16),
                   _sds((s, AW), BF16), _sds((s, SW), F32), _sds((8, D), F32)],
        compiler_params=_params(),
    )(dx3, gu, x2, mixout, mod8, n2w, w_gu, w_down, w_out)


def _ssd_bwd_block(i, *refs):
    def run(dyn_ref, y_ref, z_ref, x_ref, u_ref, dt_ref, acs_ref, dm_ref, cw_ref, sp_ref, nw_ref,
            hs_ref, e64_ref, e128_ref, dzxd_ref, sm_ref, dh_scr, dun_scr):
        @pl.when(i == 0)
        def _():
            dh_scr[...] = jnp.zeros_like(dh_scr)
            dun_scr[...] = jnp.zeros_like(dun_scr)
            sm_ref[...] = jnp.zeros_like(sm_ref)

        u, dt, acs = u_ref[...], dt_ref[...], acs_ref[...]
        sg_u = _sigmoid(u)
        xc = u * sg_u
        a_neg = -jnp.exp(sp_ref[1:2, :])
        ck = _SsdChunk(xc, dt, acs, sp_ref[...], e64_ref[...], e128_ref[...],
                       decay=[dm_ref[0, g * HPG * LB:(g + 1) * HPG * LB, :] for g in range(2)])
        h_all = hs_ref[0]
        dh_all = dh_scr[...]
        riota = lax.broadcasted_iota(jnp.int32, (LB, LB), 0)
        lane1 = lax.broadcasted_iota(jnp.int32, (1, LB), 1)

        z = z_ref[...]
        y = y_ref[...]
        sgz = _sigmoid(z)
        sz = z * sgz
        yz = y * sz
        nwv = nw_ref[...]
        dyn_v = dyn_ref[...]
        dyhat = dyn_v * nwv
        yhat_parts, dyz_parts = [], []
        for g in range(2):
            gs = slice(g * 256, (g + 1) * 256)
            t = yz[:, gs]
            rg = lax.rsqrt(jnp.mean(t * t, axis=-1, keepdims=True) + EPS)
            yh = t * rg
            dyh = dyhat[:, gs]
            yhat_parts.append(yh)
            dyz_parts.append(rg * (dyh - yh * jnp.mean(dyh * yh, axis=-1, keepdims=True)))
        yhat = jnp.concatenate(yhat_parts, axis=1)
        dyz = jnp.concatenate(dyz_parts, axis=1)
        sm_ref[5:6, 0:SW] += jnp.sum(dyn_v * yhat, axis=0, keepdims=True)
        dy = dyz * sz
        dzxd_ref[:, 0:SW] = (dyz * y * sgz * (1.0 + z * (1.0 - sgz))).astype(BF16)

        cat = lambda parts: jnp.concatenate(parts, axis=1)
        dxs, dbs, dcs, dhp, g_cat, de_x, ddte_x, ddt_x, ddsk_x = ([] for _ in range(9))
        dacs_t = jnp.zeros((LB, LB), F32)
        hsum = jnp.zeros((1, LB), F32)
        for g in range(2):
            gl = slice(g * GW, (g + 1) * GW)
            xs_g, xdt, bgb, cgb = ck.xs[g], ck.xdt[g], ck.bb[g], ck.cb16[g]
            m_st, dm_st = ck.m_st[g], ck.dm_st[g]
            dt_x, e_x, dte_x = ck.dt_x[:, gl], ck.e_x[:, gl], ck.dte_x[:, gl]
            xdtb = xdt.astype(BF16)
            hg, dhn = h_all[gl, :], dh_all[gl, :]
            hb, dhnb = hg.astype(BF16), dhn.astype(BF16)
            dy_g = dy[:, gl]
            ddsk_x.append(jnp.sum(dy_g * xs_g, axis=0, keepdims=True))
            dy_bd = ck.block_diag(dy_g).astype(BF16)
            dm4 = _dot_nt(dy_bd, xdtb)
            dxdt = _dot_tn(m_st.astype(BF16), dy_bd)
            gmat = dm4 * m_st
            dcbm = dm4 * dm_st
            dcb = dcbm[0:LB] + dcbm[LB:2 * LB] + dcbm[2 * LB:3 * LB] + dcbm[3 * LB:4 * LB]
            g_cat.append(cat([gmat[hh * LB:(hh + 1) * LB, :] for hh in range(HPG)]))
            for hh in range(HPG):
                j = HPG * g + hh
                col_sum = jnp.sum(gmat[hh * LB:(hh + 1) * LB, :], axis=0, keepdims=True)
                dacs_t = dacs_t - jnp.where(riota == j, col_sum, 0.0)
                hsl = slice(hh * HD, (hh + 1) * HD)
                hsum = hsum + jnp.where(lane1 == j, jnp.sum(dhn[hsl, :] * hg[hsl, :]), 0.0)
            dchb = (dy_g * e_x).astype(BF16)
            dcg = _dot(dchb, hb)
            dh_prev = _dot_tn(dchb, cgb)
            de_x.append(dy_g * _dot_nt(cgb, hb))
            dxs_s = _dot_nt(bgb, dhnb)
            dbg = _dot((xdt * dte_x).astype(BF16), dhnb)
            dxdt = dxdt + dxs_s * dte_x
            ddte_x.append(dxs_s * xdt)
            dhp.append(dhn * ck.elast_rows(g) + dh_prev)
            dxs.append(dy_g * ck.dsk_x[:, gl] + dxdt * dt_x)
            ddt_x.append(dxdt * xs_g)
            dcbb = dcb.astype(BF16)
            dbs.append(dbg + _dot_tn(dcbb, cgb))
            dcs.append(dcg + _dot(dcbb, bgb))
        dh_scr[...] = jnp.concatenate(dhp, axis=0)
        red = _dot_nt_sel(jnp.concatenate([cat(de_x), cat(ddte_x), cat(ddt_x)], axis=0), ck.e64, 1)
        de_c, ddte_c, ddt_c = red[0:LB], red[LB:2 * LB], red[2 * LB:3 * LB]
        ddsk = _dot_nt_sel(jnp.broadcast_to(cat(ddsk_x), (SUB, NH * HD)), ck.e64, 2)[0:1, :]
        t1 = ddte_c * ck.dte_all
        dalast = jnp.sum(t1, axis=0, keepdims=True) + hsum * ck.elast
        dacs = (_dot_nt_sel(cat(g_cat), ck.e128, 2) + de_c * ck.e_all - t1 + jnp.transpose(dacs_t)
                + jnp.where(riota == LB - 1, dalast, 0.0))
        da = _cumsum_rows(dacs, True)
        ddt = ddt_c + da * a_neg
        da_log = jnp.sum(da * dt, axis=0, keepdims=True) * a_neg
        ddtr = ddt * (1.0 - jnp.exp(-dt))
        dzxd_ref[:, SW + D:ZXD] = ddtr.astype(BF16)
        sm_ref[6:7, 0:LB] += jnp.sum(ddtr, axis=0, keepdims=True)
        sm_ref[6:7, LB:2 * LB] += da_log
        sm_ref[6:7, 2 * LB:3 * LB] += ddsk

        du = cat(dxs + dbs + dcs) * (sg_u * (1.0 + u * (1.0 - sg_u)))
        xv = x_ref[...]
        adv = [du] + _conv_advances(du, dun_scr[...])
        sm_ref[0:1, :] += jnp.sum(du, axis=0, keepdims=True)
        dxbc = cw_ref[CONVK - 1:CONVK, :] * du
        for j in range(CONVK):
            sm_ref[CONVK - j:CONVK + 1 - j, :] += jnp.sum(adv[j] * xv, axis=0, keepdims=True)
            if j:
                dxbc = dxbc + cw_ref[CONVK - 1 - j:CONVK - j, :] * adv[j]
        dun_scr[...] = du[0:SUB, :]
        dzxd_ref[:, SW:SW + D] = dxbc.astype(BF16)

    run(*refs)


def _attn_bwd_block(i, q_ref, kp_ref, kc_ref, vp_ref, vc_ref, o_ref, do_ref, cos_ref, sin_ref, pr_ref, ps_ref,
                    dq_ref, dkv_ref, ds_ref, ck_scr, cv_scr):
    @pl.when(i == 0)
    def _():
        ds_ref[...] = jnp.zeros_like(ds_ref)
        ck_scr[...] = jnp.zeros_like(ck_scr)
        cv_scr[...] = jnp.zeros_like(cv_scr)

    qv, ov, dov = q_ref[...], o_ref[...], do_ref[...]
    kcat = jnp.concatenate([kp_ref[...], kc_ref[...]], axis=0)
    vcat = jnp.concatenate([vp_ref[...], vc_ref[...]], axis=0)
    upper = _upper_mask()
    srow = lax.broadcasted_iota(jnp.int32, (8, LB), 0)
    slane = lax.broadcasted_iota(jnp.int32, (8, LB), 1)
    dsink = jnp.zeros((8, LB), F32)
    dq_g, dk_g, dv_g = [], [], []
    for g in range(NQ // QPG):
        sl = slice(g * HD, (g + 1) * HD)
        qg = _stack_heads(qv, g)
        dog = _stack_heads(dov, g)
        rows = slice(g * QPG * LB, (g + 1) * QPG * LB)
        probs = pr_ref[0, rows, :].astype(F32)
        psink = ps_ref[0, rows, :].astype(F32)
        delta = _row_sums_wide(dog.astype(F32) * _stack_heads(ov, g).astype(F32), 2)
        dsc = probs * (_band(upper, _dot_nt(dog, vcat[0:LB, sl]), _dot_nt(dog, vcat[LB:2 * LB, sl])) - delta)
        sink_terms = (psink * delta)[:, 0:1]
        for hh in range(QPG):
            dsink = dsink - jnp.where((srow == QPG * g + hh) & (slane == 0),
                                      jnp.sum(sink_terms[hh * LB:(hh + 1) * LB, :]), 0.0)
        ds_p = jnp.where(upper, dsc, 0.0).astype(BF16)
        ds_c = jnp.where(upper, 0.0, dsc).astype(BF16)
        dq_g.append((_dot(ds_p, kcat[0:LB, sl]) + _dot(ds_c, kcat[LB:2 * LB, sl])) * ATT_SCALE)
        dk_g.append(jnp.concatenate([_dot_tn(ds_p, qg), _dot_tn(ds_c, qg)], axis=0) * ATT_SCALE)
        dv_g.append(jnp.concatenate([_dot_tn(jnp.where(upper, probs, 0.0).astype(BF16), dog),
                                     _dot_tn(jnp.where(upper, 0.0, probs).astype(BF16), dog)], axis=0))
    ds_ref[...] += dsink
    cs = cos_ref[...]
    sn = sin_ref[...]
    dk2 = jnp.concatenate(dk_g, axis=1)
    dv2 = jnp.concatenate(dv_g, axis=1)
    for a, tile in enumerate(_unstack_heads(dq_g)):
        dq_ref[:, a * LB:(a + 1) * LB] = _rope(tile, cs, sn, True).astype(BF16)
    dkv_ref[:, 0:KVW] = _rope(ck_scr[...] + dk2[LB:2 * LB, :], cs, sn, True).astype(BF16)
    dkv_ref[:, KVW:2 * KVW] = (cv_scr[...] + dv2[LB:2 * LB, :]).astype(BF16)
    ck_scr[...] = dk2[0:LB, :]
    cv_scr[...] = dv2[0:LB, :]


def _mixer_bwd(q, k, v, o, do, cos, sin, probs, psink, dyn, y, z, xbc, u, dtv, acs, decay, conv_w8, ssm_p, nw, hs,
               scatters):
    s = q.shape[0]
    nb = s // LB
    bps = _blocks_per_step(nb)
    nsteps = nb // bps
    tl = bps * LB
    cur = lambda i: (nsteps - 1 - i, 0)
    prev = lambda i: (jnp.maximum((nsteps - 1 - i) * bps - 1, 0), 0)
    n_in = 25
    items, ex_shapes, n_g = _exchange_items([], scatters)
    ne = len(items)
    e64, e128 = _head_expanders()
    stack = pl.BlockSpec((bps, NH * LB, LB), lambda i: (nsteps - 1 - i, 0, 0))

    def body(*refs):
        i = pl.program_id(0)
        (q_ref, kp_ref, kc_ref, vp_ref, vc_ref, o_ref, do_ref, cos_ref, sin_ref, pr_ref, ps_ref,
         dyn_ref, y_ref, z_ref, x_ref, u_ref, dt_ref, acs_ref, dm_ref, cw_ref, sp_ref, nw_ref,
         hs_ref, e64_ref, e128_ref) = refs[:n_in]
        ex_in = refs[n_in:n_in + ne]
        dp_ref, ds_ref, sm_ref = refs[n_in + ne:n_in + ne + 3]
        ex_out = refs[n_in + ne + 3:n_in + 2 * ne + 3]
        ck_scr, cv_scr, dh_scr, dun_scr = refs[n_in + 2 * ne + 3:n_in + 2 * ne + 7]
        sems = refs[n_in + 2 * ne + 7:]

        @pl.when(i == 0)
        def _():
            _Exchange(n_g, ex_in, ex_out, sems).start()

        for back in range(bps):
            sub = bps - 1 - back
            step = i * bps + back
            r = slice(sub * LB, (sub + 1) * LB)
            before = slice((sub - 1) * LB, sub * LB)
            one = slice(sub, sub + 1)
            _attn_bwd_block(step, q_ref.at[r], kp_ref if sub == 0 else kc_ref.at[before], kc_ref.at[r],
                            vp_ref if sub == 0 else vc_ref.at[before], vc_ref.at[r], o_ref.at[r], do_ref.at[r],
                            cos_ref.at[r], sin_ref.at[r], pr_ref.at[one], ps_ref.at[one],
                            dp_ref.at[r, O_Q:O_K], dp_ref.at[r, O_K:O_Z], ds_ref, ck_scr, cv_scr)
            _ssd_bwd_block(step, dyn_ref.at[r], y_ref.at[r], z_ref.at[r], x_ref.at[r], u_ref.at[r], dt_ref.at[r],
                           acs_ref.at[r], dm_ref.at[one], cw_ref, sp_ref, nw_ref,
                           hs_ref.at[one], e64_ref, e128_ref, dp_ref.at[r, O_Z:INP], sm_ref, dh_scr, dun_scr)

        @pl.when(i == nsteps - 1)
        def _():
            _Exchange(n_g, ex_in, ex_out, sems).finish()

    any_spec = pl.BlockSpec(memory_space=pl.ANY)
    outs = _pcall(
        body, name="mixer_bwd", grid=(nsteps,),
        in_specs=[pl.BlockSpec((tl, AW), cur), pl.BlockSpec((LB, KVW), prev), pl.BlockSpec((tl, KVW), cur),
                  pl.BlockSpec((LB, KVW), prev), pl.BlockSpec((tl, KVW), cur), pl.BlockSpec((tl, AW), cur),
                  pl.BlockSpec((tl, AW), cur), pl.BlockSpec((tl, LB), cur), pl.BlockSpec((tl, LB), cur),
                  stack, stack,
                  pl.BlockSpec((tl, SW), cur), pl.BlockSpec((tl, SW), cur), pl.BlockSpec((tl, SW), cur),
                  pl.BlockSpec((tl, D), cur), pl.BlockSpec((tl, D), cur), pl.BlockSpec((tl, LB), cur),
                  pl.BlockSpec((tl, LB), cur), stack,
                  _const((8, D)), _const((8, LB)), _const((1, SW)),
                  pl.BlockSpec((bps, NH * HD, NST), lambda i: (nsteps - 1 - i, 0, 0)),
                  _const(e64.shape), _const(e128.shape)] + [any_spec] * ne,
        out_specs=[pl.BlockSpec((tl, INP), cur), pl.BlockSpec((8, LB), lambda i: (0, 0)),
                   pl.BlockSpec((8, D), lambda i: (0, 0))] + [any_spec] * ne,
        out_shape=[_sds((s, INP), BF16), _sds((8, LB), F32), _sds((8, D), F32)] + ex_shapes,
        scratch_shapes=[pltpu.VMEM((LB, KVW), F32), pltpu.VMEM((LB, KVW), F32),
                        pltpu.VMEM((NH * HD, NST), F32), pltpu.VMEM((SUB, D), F32)]
        + _exchange_sems(ne),
        compiler_params=_params(),
    )(q, k, k, v, v, o, do, cos, sin, probs, psink, dyn, y, z, xbc, u, dtv, acs, decay, conv_w8, ssm_p, nw, hs,
      e64, e128, *items)
    return outs[0], outs[1], outs[2], outs[3:]


def _inproj_bwd(dproj, x, dx2, mod8, n1w, w_in_t, scatters, smalls):
    s = x.shape[0]
    tt = min(512, s)
    nt = s // tt
    items, ex_shapes, n_g = _exchange_items([], scatters)
    ne = len(items)
    n_in = 10

    def body(*refs):
        dp_ref, x_ref, dx2_ref, mod_ref, nw_ref, w_ref, f_ref, b_ref, s_ref, k_ref = refs[:n_in]
        ex_in = refs[n_in:n_in + ne]
        gx_ref, sm_ref = refs[n_in + ne:n_in + 2 + ne]
        ex_out = refs[n_in + 2 + ne:n_in + 2 + 2 * ne]
        gpack_ref = refs[n_in + 2 + 2 * ne]
        pack_scr = refs[n_in + 3 + 2 * ne]
        sems = refs[n_in + 4 + 2 * ne:n_in + 7 + 2 * ne]
        pack_sems = refs[n_in + 7 + 2 * ne:]
        i = pl.program_id(0)

        @pl.when(i == 0)
        def _():
            sm_ref[...] = jnp.zeros_like(sm_ref)
            _Exchange(n_g, ex_in, ex_out, sems).start()

        w = w_ref[...]
        hr = tt // 2
        dh1 = [_dot(dp_ref[h * hr:(h + 1) * hr, :], w) for h in range(2)]
        sums = jnp.zeros((3, D), F32)
        for h in range(2):
            rows = slice(h * hr, (h + 1) * hr)
            dxn, d_shift, d_scale, d_w = _norm_mod_bwd(x_ref[rows, :], dh1[h], nw_ref[...], mod_ref[1:2, :])
            gx_ref[rows, :] = dx2_ref[rows, :] + dxn
            sums = sums + jnp.concatenate([d_shift, d_scale, d_w], axis=0)
        sm_ref[0:3, :] += sums

        @pl.when(i == nt - 1)
        def _():
            _pack_rows(f_ref, b_ref, s_ref, sm_ref, k_ref, pack_scr)
            small = _Exchange(1, [pack_scr], [gpack_ref], pack_sems)
            small.start()
            _Exchange(n_g, ex_in, ex_out, sems).finish()
            small.finish()

    any_spec = pl.BlockSpec(memory_space=pl.ANY)
    outs = _pcall(
        body, name="inproj_bwd", grid=(nt,),
        in_specs=[_rows(tt, INP), _rows(tt, D), _rows(tt, D), _const((8, D)), _const((1, D)), _const((INP, D)),
                  _const((8, D)), _const((8, D)), _const((8, D)), _const((8, LB))]
        + [any_spec] * ne,
        out_specs=[_rows(tt, D), pl.BlockSpec((8, D), lambda i: (0, 0))] + [any_spec] * (ne + 1),
        out_shape=[_sds((s, D), F32), _sds((8, D), F32)] + ex_shapes + [_sds((N_DEV, PACK_ROWS, D), F32)],
        scratch_shapes=[pltpu.VMEM((PACK_ROWS, D), F32)] + _exchange_sems(ne) + _exchange_sems(1),
        compiler_params=_params(),
    )(dproj, x, dx2, mod8, n1w, w_in_t, *smalls, *items)
    return outs[0], outs[2:2 + ne], outs[2 + ne]


def _wgrad(a, b, name):
    s, m = a.shape
    n = b.shape[1]
    tk = min(2048, s)
    wide = (1408, 1024, 512)
    tm = next((t for t in wide if m % t == 0), m)
    tn = n if n <= 2048 else _largest_divisor(n, wide)
    nk = s // tk

    def body(a_ref, b_ref, o_ref, acc):
        kk = pl.program_id(2)

        @pl.when(kk == 0)
        def _():
            acc[...] = jnp.zeros_like(acc)

        acc[...] += _dot_tn(a_ref[...], b_ref[...])

        @pl.when(kk == nk - 1)
        def _():
            o_ref[...] = acc[...].astype(BF16)

    return _pcall(
        body, name=name, grid=(m // tm, n // tn, nk),
        in_specs=[pl.BlockSpec((tk, tm), lambda i, j, kk: (kk, i)), pl.BlockSpec((tk, tn), lambda i, j, kk: (kk, j))],
        out_specs=pl.BlockSpec((tm, tn), lambda i, j, kk: (i, j)),
        out_shape=_sds((m, n), BF16),
        scratch_shapes=[pltpu.VMEM((tm, tn), F32)],
        compiler_params=_params(3),
    )(a, b)


PACK_ROWS = 24


def _pack_rows(f_ref, b_ref, s_ref, i_ref, k_ref, o_ref):
    o_ref[...] = jnp.zeros_like(o_ref)
    o_ref[0:2, :] = i_ref[0:2, :]
    o_ref[2:3, :] = b_ref[3:4, :]
    o_ref[3:5, :] = b_ref[0:2, :]
    o_ref[5:6, :] = f_ref[0:1, :]
    o_ref[6:7, :] = i_ref[2:3, :]
    o_ref[7:8, :] = b_ref[2:3, :]
    o_ref[8:9, :] = f_ref[1:2, :]
    o_ref[9:14, :] = s_ref[0:5, :]
    o_ref[14:15, :] = s_ref[5:6, :]
    o_ref[15:16, 0:3 * LB] = s_ref[6:7, 0:3 * LB]
    lane = lax.broadcasted_iota(jnp.int32, (1, LB), 1)
    sk = jnp.zeros((1, LB), F32)
    for h in range(NQ):
        sk = sk + jnp.where(lane == h, k_ref[h:h + 1, 0:1], 0.0)
    o_ref[15:16, 3 * LB:4 * LB] = sk
    o_ref[16:17, :] = f_ref[2:3, :]


def _exchange_items(gathers, scatters):
    items = list(gathers) + list(scatters)
    shapes = [_sds((N_DEV,) + a.shape, a.dtype) for a in gathers] + [_sds(a.shape, a.dtype) for a in scatters]
    return items, shapes, len(gathers)


def _exchange_sems(n):
    return [pltpu.SemaphoreType.DMA((n, N_DEV - 1)), pltpu.SemaphoreType.DMA((n, N_DEV - 1)),
            pltpu.SemaphoreType.DMA((n,))]


class _Exchange:
    def __init__(self, n_g, ins, outs, sems):
        self.n_g, self.ins, self.outs = n_g, ins, outs
        self.send_sems, self.recv_sems, self.loc_sems = sems
        xi, yi, ci = lax.axis_index("x"), lax.axis_index("y"), lax.axis_index("c")
        self.me = 4 * xi + 2 * yi + ci
        self.peers = []
        for r in range(1, N_DEV):
            px = 1 - xi if r & 4 else xi
            py = 1 - yi if r & 2 else yi
            pc = 1 - ci if r & 1 else ci
            self.peers.append(((px, py, pc), 4 * px + 2 * py + pc))

    def _copy(self, t, r, landing):
        dev, peer = self.peers[r]
        src = self.ins[t] if t < self.n_g else self.ins[t].at[peer]
        return pltpu.make_async_remote_copy(
            src_ref=src, dst_ref=self.outs[t].at[landing], send_sem=self.send_sems.at[t, r],
            recv_sem=self.recv_sems.at[t, r], device_id=dev, device_id_type=pl.DeviceIdType.MESH)

    def _local(self, t):
        src = self.ins[t] if t < self.n_g else self.ins[t].at[self.me]
        return pltpu.make_async_copy(src, self.outs[t].at[self.me], self.loc_sems.at[t])

    def start(self):
        for t in range(len(self.ins)):
            self._local(t).start()
            for r in range(N_DEV - 1):
                self._copy(t, r, self.me).start()

    def finish(self):
        n = len(self.ins)
        for t in range(n):
            for r in range(N_DEV - 1):
                self._copy(t, r, self.peers[r][1]).wait_recv()
        for t in range(n):
            for r in range(N_DEV - 1):
                self._copy(t, r, self.me).wait_send()
            self._local(t).wait()

    DIRECT = (0, 1, 3, 5)

    def two_level_start(self):
        for t in range(len(self.ins)):
            self._local(t).start()
            for r in self.DIRECT:
                self._copy(t, r, self.me).start()

    def _relay(self, t, r):
        peer = self.peers[r][1]
        return pltpu.make_async_remote_copy(
            src_ref=self.outs[t].at[peer], dst_ref=self.outs[t].at[peer], send_sem=self.send_sems.at[t, r + 1],
            recv_sem=self.recv_sems.at[t, r + 1], device_id=self.peers[0][0], device_id_type=pl.DeviceIdType.MESH)

    def two_level_relay(self):
        for t in range(len(self.ins)):
            for r in self.DIRECT[1:]:
                self._copy(t, r, self.peers[r][1]).wait_recv()
                self._relay(t, r).start()

    def two_level_finish(self):
        n = len(self.ins)
        for t in range(n):
            for r in (0, 2, 4, 6):
                self._copy(t, r, self.peers[r][1]).wait_recv()
        for t in range(n):
            for r in self.DIRECT:
                self._copy(t, r, self.me).wait_send()
            for r in self.DIRECT[1:]:
                self._relay(t, r).wait_send()
            self._local(t).wait()


def _prologue(c8, w_in_tb, cw8, w_cols, b_cols):
    ncol = w_cols.shape[1]

    def body(c_ref, win_ref, cw_ref, w_ref, b_ref, gc_ref, gin_ref, gcw_ref, gmod_ref, call_scr, mod_scr, loc_sem,
             *sems):
        big = _Exchange(2, [win_ref, cw_ref], [gin_ref, gcw_ref], sems[0:3])
        small = _Exchange(1, [c_ref], [gc_ref], sems[3:6])
        small.start()
        big.two_level_start()
        small.finish()
        landed = pltpu.make_async_copy(gc_ref, call_scr, loc_sem)
        landed.start()
        landed.wait()
        cv = call_scr[:, 0, :]
        sc = (cv * _sigmoid(cv)).astype(BF16)
        mod_scr[...] = _dot(sc, w_ref[...].astype(BF16)) + b_ref[...]
        mods = _Exchange(1, [mod_scr], [gmod_ref], sems[6:9])
        mods.start()
        mods.finish()
        big.two_level_relay()
        big.two_level_finish()

    any_spec = pl.BlockSpec(memory_space=pl.ANY)
    vmem_spec = pl.BlockSpec(memory_space=pltpu.VMEM)
    return _pcall(
        body, name="prologue", in_specs=[any_spec, any_spec, any_spec, vmem_spec, vmem_spec],
        out_specs=[any_spec] * 4,
        out_shape=[_sds((N_DEV,) + c8.shape, F32), _sds((N_DEV,) + w_in_tb.shape, BF16),
                   _sds((N_DEV,) + cw8.shape, F32), _sds((N_DEV, N_DEV, ncol), F32)],
        scratch_shapes=[pltpu.VMEM((N_DEV,) + c8.shape, F32), pltpu.VMEM((N_DEV, ncol), F32),
                        pltpu.SemaphoreType.DMA] + _exchange_sems(2) + _exchange_sems(1) + _exchange_sems(1),
        compiler_params=pltpu.CompilerParams(vmem_limit_bytes=VMEM_LIMIT),
    )(c8, w_in_tb, cw8, w_cols, b_cols)


def _adamw(w, g, m, v):
    m2 = ADAM_B1 * m + (1.0 - ADAM_B1) * g
    v2 = ADAM_B2 * v + (1.0 - ADAM_B2) * (g * g)
    m_hat = m2 / (1.0 - ADAM_B1 ** ADAM_STEP)
    v_hat = v2 / (1.0 - ADAM_B2 ** ADAM_STEP)
    delta = -ADAM_LR * (m_hat / (jnp.sqrt(v_hat) + ADAM_EPS) + ADAM_WD * w)
    return delta, m2, v2


def _sum_adamw(parts, w, m, v, name):
    rws, cols = w.shape
    tr = next((t for t in (256, 176, 128) if rws % t == 0), rws)

    def body(p_ref, w_ref, m_ref, v_ref, g_ref, d_ref, mo_ref, vo_ref):
        g = p_ref[0].astype(F32)
        for dev in range(1, N_DEV):
            g = g + p_ref[dev].astype(F32)
        g_ref[...] = g
        d_ref[...], mo_ref[...], vo_ref[...] = _adamw(w_ref[...], g, m_ref[...], v_ref[...])

    blk = pl.BlockSpec((tr, cols), lambda i: (i, 0))
    return _pcall(
        body, name=name, grid=(rws // tr,),
        in_specs=[pl.BlockSpec((N_DEV, tr, cols), lambda i: (0, i, 0)), blk, blk, blk],
        out_specs=[blk] * 4, out_shape=[_sds((rws, cols), F32)] * 4, compiler_params=_params(),
    )(parts, w, m, v)


def _sum_adamw_rowwise(parts, w, m, v, name):
    rws, _, cols = w.shape

    def body(p_ref, w_ref, m_ref, v_ref, g_ref, d_ref, mo_ref, vo_ref, ins, outs, sems):
        loads = [pltpu.make_async_copy(src.at[:, 0, :], ins.at[i], sems.at[i])
                 for i, src in enumerate((w_ref, m_ref, v_ref))]
        for load in loads:
            load.start()
        g = p_ref[0].astype(F32)
        for dev in range(1, N_DEV):
            g = g + p_ref[dev].astype(F32)
        for load in loads:
            load.wait()
        outs[0] = g
        outs[1], outs[2], outs[3] = _adamw(ins[0], g, ins[1], ins[2])
        stores = [pltpu.make_async_copy(outs.at[i], dst.at[:, 0, :], sems.at[len(loads) + i])
                  for i, dst in enumerate((g_ref, d_ref, mo_ref, vo_ref))]
        for store in stores:
            store.start()
        for store in stores:
            store.wait()

    any_spec = pl.BlockSpec(memory_space=pl.ANY)
    return _pcall(
        body, name=name, grid=(1,),
        in_specs=[_const((N_DEV, rws, cols)), any_spec, any_spec, any_spec],
        out_specs=[any_spec] * 4, out_shape=[_sds((rws, 1, cols), F32)] * 4,
        scratch_shapes=[pltpu.VMEM((3, rws, cols), F32), pltpu.VMEM((4, rws, cols), F32),
                        pltpu.SemaphoreType.DMA((7,))],
        compiler_params=_params(),
    )(parts, w, m, v)


def _wada_adamw(c_all, dmod_cols, w, m, v):
    rws, cols = w.shape
    tr = 256

    def body(c_ref, dm_ref, w_ref, m_ref, v_ref, g_ref, d_ref, mo_ref, vo_ref):
        cv = c_ref[...]
        sc = (cv * _sigmoid(cv)).astype(BF16)
        g = _dot_tn(sc, dm_ref[...].astype(BF16))
        g_ref[...] = g
        d_ref[...], mo_ref[...], vo_ref[...] = _adamw(w_ref[...], g, m_ref[...], v_ref[...])

    blk = pl.BlockSpec((tr, cols), lambda i: (i, 0))
    return _pcall(
        body, name="wada_adamw", grid=(rws // tr,),
        in_specs=[pl.BlockSpec((N_DEV, tr), lambda i: (0, i)), pl.BlockSpec((N_DEV, cols), lambda i: (0, 0)),
                  blk, blk, blk],
        out_specs=[blk] * 4, out_shape=[_sds((rws, cols), F32)] * 4, compiler_params=_params(),
    )(c_all, dmod_cols, w, m, v)


def _small_reduce(packs):
    def body(p_ref, o_ref):
        tot = p_ref[0]
        for dev in range(1, N_DEV):
            tot = tot + p_ref[dev]
        o_ref[...] = tot
        o_ref[16:17, :] = jnp.zeros((1, D), F32) + (0.5 / D) * jnp.sum(tot[16:17, :])

    return _pcall(body, name="small_reduce", out_shape=_sds((PACK_ROWS, D), F32))(packs)


def _adamw_many(ws, gs, ms, vs):
    k = len(ws)

    def body(*refs):
        for i in range(k):
            w_ref, g_ref, m_ref, v_ref = refs[i], refs[k + i], refs[2 * k + i], refs[3 * k + i]
            d_ref, mo_ref, vo_ref = refs[4 * k + i], refs[5 * k + i], refs[6 * k + i]
            d_ref[...], mo_ref[...], vo_ref[...] = _adamw(w_ref[...], g_ref[...], m_ref[...], v_ref[...])

    shp = [_sds(w.shape, F32) for w in ws]
    outs = _pcall(body, name="adamw_small", out_shape=shp * 3)(*ws, *gs, *ms, *vs)
    return outs[:k], outs[k:2 * k], outs[2 * k:]


def kernel(x, c, positions, w_ada, b_ada, norm1_w, w_in, conv_w, conv_b, dt_bias, a_log, d_skip, attn_sinks, ssm_norm_w, w_out, norm2_w, w_gate_up, w_down, final_norm_w, loss_target, m_w_ada, m_b_ada, m_norm1_w, m_w_in, m_conv_w, m_conv_b, m_dt_bias, m_a_log, m_d_skip, m_attn_sinks, m_ssm_norm_w, m_w_out, m_norm2_w, m_w_gate_up, m_w_down, m_final_norm_w, v_w_ada, v_b_ada, v_norm1_w, v_w_in, v_conv_w, v_conv_b, v_dt_bias, v_a_log, v_d_skip, v_attn_sinks, v_ssm_norm_w, v_w_out, v_norm2_w, v_w_gate_up, v_w_down, v_final_norm_w):
    s = x.shape[1]
    me = 4 * lax.axis_index("x") + 2 * lax.axis_index("y") + lax.axis_index("c")
    ada_cols = N_MOD * D // N_DEV

    c8 = jnp.pad(c, ((0, 7), (0, 0)))
    cw8 = jnp.pad(conv_w[0], ((0, 8 - CONVK), (0, 0)))
    w_in_t = jnp.transpose(w_in[0])
    rowwise = lambda a: jnp.transpose(a, (2, 0, 1))
    w_gu_t, m_w_gu_t, v_w_gu_t = (jnp.transpose(w_gate_up[0]), jnp.transpose(m_w_gate_up[0]),
                                  jnp.transpose(v_w_gate_up[0]))
    b_cols = lax.dynamic_slice(b_ada, (0, me * ada_cols), (1, ada_cols))
    g_c, g_in, g_cw, g_mod = _prologue(c8, w_in_t.astype(BF16), cw8, w_ada[0], b_cols)
    c_all = g_c[:, 0, :]
    w_in_f = jnp.pad(g_in.reshape(IN_PROJ, D), ((0, INP - IN_PROJ), (0, 0)))
    conv_w8 = jnp.transpose(g_cw, (1, 0, 2)).reshape(8, D)
    mod = lax.dynamic_index_in_dim(g_mod, me, axis=1, keepdims=False).reshape(N_MOD, D)
    mod8 = jnp.pad(mod, ((0, 8 - N_MOD), (0, 0)))

    half = HD // 2
    inv_freq = ROPE_THETA ** (-jnp.arange(half, dtype=F32) / half)
    invf = jnp.tile(inv_freq, LB // half).reshape(1, LB)
    lanes = lambda a: jnp.pad(a, ((0, 0), (0, LB - a.shape[1])))
    ssm_p = jnp.pad(jnp.concatenate([lanes(dt_bias), lanes(a_log), lanes(d_skip)], axis=0), ((0, 5), (0, 0)))
    sinks8 = jnp.broadcast_to(attn_sinks.reshape(NQ, 1), (NQ, LB))

    xs, tgt, fnw = x[0], loss_target[0], final_norm_w.reshape(1, D)

    q, k, v, z, xbc, dtr, h1, cos, sin = _inproj_fwd(xs, positions[0].reshape(s, 1), invf, mod8, norm1_w, w_in_f)
    (attn, yn, y, hs, conv_u, dtv, acs, decay, probs, psink), (g_out, g_gu, g_down) = _mixer_fwd(
        q, k, v, sinks8, xbc, conv_w8, conv_b, dtr, ssm_p, z, ssm_norm_w,
        [w_out[0].astype(BF16), w_gu_t.astype(BF16), w_down[0].astype(BF16)])
    w_out_f = g_out.reshape(D, D)
    w_gu_f = g_gu.reshape(2 * DFF, D)
    w_down_f = g_down.reshape(DFF, D)
    x2, h2, mo, mix, gu, act, dx3, sm_f = _outproj_ffn_fwd_loss(attn, yn, xs, tgt, mod8, norm2_w, fnw, w_out_f, w_gu_f,
                                                                 w_down_f)

    dx2, dff, dgu, dmix, dattn, dyn, sm_b = _ffn_bwd(dx3, gu, x2, mo, mod8, norm2_w, w_gu_f, w_down_f, w_out_f)
    p_gu = _wgrad(dgu, h2, "wgrad_gate_up").reshape(N_DEV, 2 * DFF // N_DEV, D)
    p_down = _wgrad(act, dff, "wgrad_down").reshape(N_DEV, DFF // N_DEV, D)
    p_out = _wgrad(mix, dmix, "wgrad_out").reshape(N_DEV, D // N_DEV, D)
    dproj, dsink, sm_s, (r_gu, r_down, r_out) = _mixer_bwd(
        q, k, v, attn, dattn, cos, sin, probs, psink, dyn, y, z, xbc, conv_u, dtv, acs, decay, conv_w8, ssm_p,
        ssm_norm_w, hs, [p_gu, p_down, p_out])
    p_in = _wgrad(dproj, h1, "wgrad_in")[:IN_PROJ].reshape(N_DEV, IN_PROJ // N_DEV, D)
    gx, (r_in,), g_pack = _inproj_bwd(dproj, xs, dx2, mod8, norm1_w, w_in_f, [p_in], (sm_f, sm_b, sm_s, dsink))

    tot = _small_reduce(g_pack)
    loss = tot[16, 0]
    dmod_all = g_pack[:, 0:N_MOD, :].reshape(N_DEV, N_MOD * D)
    dmod_cols = lax.dynamic_slice(dmod_all, (0, me * ada_cols), (N_DEV, ada_cols))

    big = {
        "w_ada": _wada_adamw(c_all, dmod_cols, w_ada[0], m_w_ada[0], v_w_ada[0]),
        "w_in": [jnp.transpose(t, (1, 2, 0))[0] for t in
                 _sum_adamw_rowwise(r_in, rowwise(w_in), rowwise(m_w_in), rowwise(v_w_in), "adamw_in")],
        "w_out": _sum_adamw(r_out, w_out[0], m_w_out[0], v_w_out[0], "adamw_out"),
        "w_gate_up": [jnp.transpose(t) for t in _sum_adamw(r_gu, w_gu_t, m_w_gu_t, v_w_gu_t, "adamw_gate_up")],
        "w_down": _sum_adamw(r_down, w_down[0], m_w_down[0], v_w_down[0], "adamw_down"),
    }
    small_names = ["b_ada", "norm1_w", "conv_w", "conv_b", "dt_bias", "a_log", "d_skip", "attn_sinks", "ssm_norm_w",
                   "norm2_w", "final_norm_w"]
    row15 = tot[15:16, :]
    small_g = {
        "b_ada": tot[0:N_MOD, :].reshape(1, N_MOD * D),
        "norm1_w": tot[6:7, :],
        "conv_w": lax.dynamic_slice(tot[10:14, :], (0, me * (D // N_DEV)), (CONVK, D // N_DEV)),
        "conv_b": tot[9:10, :],
        "dt_bias": row15[:, 0:NH],
        "a_log": row15[:, LB:LB + NH],
        "d_skip": row15[:, 2 * LB:2 * LB + NH],
        "attn_sinks": row15[:, 3 * LB:3 * LB + NQ],
        "ssm_norm_w": tot[14:15, 0:SW],
        "norm2_w": tot[7:8, :],
        "final_norm_w": tot[8:9, :],
    }
    small_w = {"b_ada": b_ada, "norm1_w": norm1_w, "conv_w": conv_w[0], "conv_b": conv_b, "dt_bias": dt_bias,
               "a_log": a_log, "d_skip": d_skip, "attn_sinks": attn_sinks, "ssm_norm_w": ssm_norm_w,
               "norm2_w": norm2_w, "final_norm_w": final_norm_w.reshape(1, D)}
    small_m = {"b_ada": m_b_ada, "norm1_w": m_norm1_w, "conv_w": m_conv_w[0], "conv_b": m_conv_b,
               "dt_bias": m_dt_bias, "a_log": m_a_log, "d_skip": m_d_skip, "attn_sinks": m_attn_sinks,
               "ssm_norm_w": m_ssm_norm_w, "norm2_w": m_norm2_w, "final_norm_w": m_final_norm_w.reshape(1, D)}
    small_v = {"b_ada": v_b_ada, "norm1_w": v_norm1_w, "conv_w": v_conv_w[0], "conv_b": v_conv_b,
               "dt_bias": v_dt_bias, "a_log": v_a_log, "d_skip": v_d_skip, "attn_sinks": v_attn_sinks,
               "ssm_norm_w": v_ssm_norm_w, "norm2_w": v_norm2_w, "final_norm_w": v_final_norm_w.reshape(1, D)}
    s_d, s_m, s_v = _adamw_many([small_w[k] for k in small_names], [small_g[k] for k in small_names],
                                [small_m[k] for k in small_names], [small_v[k] for k in small_names])

    order = ["w_ada", "b_ada", "norm1_w", "w_in", "conv_w", "conv_b", "dt_bias", "a_log", "d_skip", "attn_sinks",
             "ssm_norm_w", "w_out", "norm2_w", "w_gate_up", "w_down", "final_norm_w"]
    lead = {"w_ada", "w_in", "conv_w", "w_out", "w_gate_up", "w_down"}
    grads, deltas, new_m, new_v = [], [], [], []
    for name in order:
        if name in big:
            g, d, m2, v2 = big[name]
        else:
            i = small_names.index(name)
            g, d, m2, v2 = small_g[name], s_d[i], s_m[i], s_v[i]
        if name in lead:
            g, d, m2, v2 = g[None], d[None], m2[None], v2[None]
        if name == "final_norm_w":
            g, d, m2, v2 = g.reshape(D), d.reshape(D), m2.reshape(D), v2.reshape(D)
        grads.append(g)
        deltas.append(d)
        new_m.append(m2)
        new_v.append(v2)
    return (loss, gx[None], *grads, *deltas, *new_m, *new_v)
```

```python
import functools
import math

import jax
import jax.numpy as jnp
from jax import lax
from jax.experimental import pallas as pl
from jax.experimental.pallas import tpu as pltpu

F32 = jnp.float32
BF16 = jnp.bfloat16

N_DEV = 8
D = 1024
HD = 64
NQ = 8
AW = 512
KVW = 128
SW = 512
NST = 128
NH = 8
LB = 128
CONVK = 4
DFF = 2816
N_MOD = 6
IN_PROJ = 2312
INP = 2432
O_Q, O_K, O_V, O_Z, O_XBC, O_DT = 0, 512, 640, 768, 1280, 2304
ZXD = INP - O_Z
EPS = 1e-6
NEG = -1e30
ROPE_THETA = 10000.0
VMEM_LIMIT = 56 * 1024 * 1024

ADAM_LR = 0.001
ADAM_B1 = 0.9
ADAM_B2 = 0.999
ADAM_EPS = 1e-08
ADAM_WD = 0.01
ADAM_STEP = 10

NT_DIMS = (((1,), (1,)), ((), ()))
TN_DIMS = (((0,), (0,)), ((), ()))


def _pcall(body, **kw):
    return pl.pallas_call(body, **kw)


def _sds(shape, dtype):
    return jax.ShapeDtypeStruct(shape, dtype)


def _params(n_grid=1):
    return pltpu.CompilerParams(dimension_semantics=("arbitrary",) * n_grid, vmem_limit_bytes=VMEM_LIMIT)


def _const(shape):
    return pl.BlockSpec(shape, lambda *_: (0,) * len(shape), pipeline_mode=pl.Buffered(1))


def _largest_divisor(n, candidates):
    for cand in candidates:
        if n % cand == 0:
            return cand
    raise ValueError(f"no tile in {candidates} divides {n}")


def _rows(t, w):
    return pl.BlockSpec((t, w), lambda i: (i, 0))


def _dot(a, b):
    return jnp.dot(a, b, preferred_element_type=F32)


def _dot_nt(a, b):
    return lax.dot_general(a, b, NT_DIMS, preferred_element_type=F32)


def _dot_tn(a, b):
    return lax.dot_general(a, b, TN_DIMS, preferred_element_type=F32)


def _sigmoid(v):
    return 1.0 / (1.0 + jnp.exp(-v))


def _softplus(v):
    return jnp.maximum(v, 0.0) + jnp.log1p(jnp.exp(-jnp.abs(v)))


def _rope_sign_mask(shape):
    lane = lax.broadcasted_iota(jnp.int32, shape, 1)
    return (lane % HD) < (HD // 2)


def _rope(t, cs, sn, inverse):
    r_dn = pltpu.roll(t, HD // 2, 1)
    r_up = pltpu.roll(t, LB - HD // 2, 1)
    first = _rope_sign_mask(t.shape)
    if inverse:
        rot = jnp.where(first, r_up, -r_dn)
    else:
        rot = jnp.where(first, -r_up, r_dn)
    return t * cs + rot * sn


def _norm_mod_fwd(xv, nw, shift, scale):
    r = lax.rsqrt(jnp.mean(xv * xv, axis=-1, keepdims=True) + EPS)
    xh = xv * r
    return (xh * nw) * (1.0 + scale) + shift


def _norm_mod_bwd(xv, dh, nw, scale):
    r = lax.rsqrt(jnp.mean(xv * xv, axis=-1, keepdims=True) + EPS)
    xh = xv * r
    xn = xh * nw
    d_shift = jnp.sum(dh, axis=0, keepdims=True)
    d_scale = jnp.sum(dh * xn, axis=0, keepdims=True)
    dxn = dh * (1.0 + scale)
    d_w = jnp.sum(dxn * xh, axis=0, keepdims=True)
    dxh = dxn * nw
    dx = r * (dxh - xh * jnp.mean(dxh * xh, axis=-1, keepdims=True))
    return dx, d_shift, d_scale, d_w


def _inproj_fwd(x, pos, invf, mod8, n1w, w_in):
    s = x.shape[0]
    tt = min(512, s)

    def body(x_ref, pos_ref, invf_ref, mod_ref, nw_ref, w_ref,
             q_ref, k_ref, v_ref, z_ref, xbc_ref, dtr_ref, h1_ref, cos_ref, sin_ref):
        h = _norm_mod_fwd(x_ref[...], nw_ref[...], mod_ref[0:1, :], mod_ref[1:2, :])
        hb = h.astype(BF16)
        h1_ref[...] = hb
        proj = _dot_nt(hb, w_ref[...])
        ang = pos_ref[...].astype(F32) * invf_ref[...]
        cs = jnp.cos(ang)
        sn = jnp.sin(ang)
        cos_ref[...] = cs
        sin_ref[...] = sn
        for a in range(AW // LB):
            q_ref[:, a * LB:(a + 1) * LB] = _rope(proj[:, O_Q + a * LB:O_Q + (a + 1) * LB], cs, sn, False).astype(BF16)
        k_ref[...] = _rope(proj[:, O_K:O_V], cs, sn, False).astype(BF16)
        v_ref[...] = proj[:, O_V:O_Z].astype(BF16)
        z_ref[...] = proj[:, O_Z:O_XBC]
        xbc_ref[...] = proj[:, O_XBC:O_DT]
        dtr_ref[...] = proj[:, O_DT:INP]

    return _pcall(
        body, name="inproj_fwd", grid=(s // tt,),
        in_specs=[_rows(tt, D), _rows(tt, 1), _const((1, LB)), _const((8, D)), _const((1, D)), _const((INP, D))],
        out_specs=[_rows(tt, AW), _rows(tt, KVW), _rows(tt, KVW), _rows(tt, SW), _rows(tt, D), _rows(tt, LB),
                   _rows(tt, D), _rows(tt, LB), _rows(tt, LB)],
        out_shape=[_sds((s, AW), BF16), _sds((s, KVW), BF16), _sds((s, KVW), BF16), _sds((s, SW), F32),
                   _sds((s, D), F32), _sds((s, LB), F32), _sds((s, D), BF16), _sds((s, LB), F32), _sds((s, LB), F32)],
        compiler_params=_params(),
    )(x, pos, invf, mod8, n1w, w_in)


QPG = 4
ATT_SCALE = 1.0 / math.sqrt(HD)


def _stack_heads(val, g):
    return jnp.concatenate([val[:, (QPG * g + hh) * HD:(QPG * g + hh + 1) * HD] for hh in range(QPG)], axis=0)


def _unstack_heads(groups):
    pieces = [grp[hh * LB:(hh + 1) * LB, :] for grp in groups for hh in range(QPG)]
    return [jnp.concatenate(pieces[2 * a:2 * a + 2], axis=1) for a in range(NQ // 2)]


def _upper_mask():
    row = lax.broadcasted_iota(jnp.int32, (QPG * LB, LB), 0)
    col = lax.broadcasted_iota(jnp.int32, (QPG * LB, LB), 1)
    return col > (row % LB)


def _sink_wide(sinks, g):
    return jnp.concatenate([jnp.broadcast_to(sinks[QPG * g + hh:QPG * g + hh + 1, 0:1], (LB, LB))
                            for hh in range(QPG)], axis=0)


def _row_sums_wide(v, terms):
    return _dot_sel(v, jnp.ones((v.shape[1], LB), BF16), terms)


def _band(upper, prev_part, cur_part):
    return jnp.where(upper, prev_part, cur_part)


def _attn_scores(n, qg, kcat, upper):
    sp = _dot_nt(qg, kcat[0:LB, :]) * ATT_SCALE
    sc = _dot_nt(qg, kcat[LB:2 * LB, :]) * ATT_SCALE
    return _band(upper, jnp.where(n > 0, sp, NEG), sc)


def _attn_softmax(comb, sink):
    m = jnp.maximum(jnp.max(comb, axis=-1, keepdims=True), sink)
    p = jnp.exp(comb - m)
    es = jnp.exp(sink - m)
    return p, es, _row_sums_wide(p, 1) + es


def _attn_fwd_block(n, q_ref, kp_ref, kc_ref, vp_ref, vc_ref, sink_ref, o_ref, pr_ref, ps_ref):
    qv = q_ref[...]
    kcat = jnp.concatenate([kp_ref[...], kc_ref[...]], axis=0)
    vcat = jnp.concatenate([vp_ref[...], vc_ref[...]], axis=0)
    sinks = sink_ref[...]
    upper = _upper_mask()
    outs = []
    for g in range(NQ // QPG):
        sl = slice(g * HD, (g + 1) * HD)
        rows = slice(g * QPG * LB, (g + 1) * QPG * LB)
        p, es, denom = _attn_softmax(_attn_scores(n, _stack_heads(qv, g), kcat[:, sl], upper), _sink_wide(sinks, g))
        rden = 1.0 / denom
        pr_ref[0, rows, :] = (p * rden).astype(BF16)
        ps_ref[0, rows, :] = (es * rden).astype(BF16)
        outs.append((_dot(jnp.where(upper, p, 0.0).astype(BF16), vcat[0:LB, sl])
                     + _dot(jnp.where(upper, 0.0, p).astype(BF16), vcat[LB:2 * LB, sl])) * rden[:, 0:HD])
    for g, grp in enumerate(outs):
        for hh in range(QPG):
            h = QPG * g + hh
            o_ref[:, h * HD:(h + 1) * HD] = grp[hh * LB:(hh + 1) * LB, :].astype(BF16)


def _cumsum_rows(a, reverse):
    row = lax.broadcasted_iota(jnp.int32, a.shape, 0)
    step = 1
    while step < LB:
        if reverse:
            a = a + jnp.where(row < LB - step, pltpu.roll(a, LB - step, 0), 0.0)
        else:
            a = a + jnp.where(row >= step, pltpu.roll(a, step, 0), 0.0)
        step *= 2
    return a


SUB = 8


def _conv_shifts(tail, cur):
    row = lax.broadcasted_iota(jnp.int32, tail.shape, 0)
    out = [cur]
    for j in range(1, CONVK):
        rolled = pltpu.roll(cur, j, 0)
        top = jnp.where(row < j, pltpu.roll(tail, j, 0), rolled[0:SUB, :])
        out.append(jnp.concatenate([top, rolled[SUB:, :]], axis=0))
    return out


def _conv_advances(du, head):
    row = lax.broadcasted_iota(jnp.int32, head.shape, 0)
    out = []
    for j in range(1, CONVK):
        rolled = pltpu.roll(du, LB - j, 0)
        bottom = jnp.where(row >= SUB - j, pltpu.roll(head, SUB - j, 0), rolled[LB - SUB:, :])
        out.append(jnp.concatenate([rolled[:LB - SUB, :], bottom], axis=0))
    return out


def _split(v, terms):
    out = []
    for _ in range(terms - 1):
        t = v.astype(BF16)
        out.append(t)
        v = v - t.astype(F32)
    out.append(v.astype(BF16))
    return out


def _dot_sel(v, sel, terms):
    parts = [_dot(t, sel) for t in _split(v, terms)]
    return functools.reduce(lambda a, b: a + b, parts)


def _dot_nt_sel(v, sel, terms):
    parts = [_dot_nt(t, sel) for t in _split(v, terms)]
    return functools.reduce(lambda a, b: a + b, parts)


def _ssd_pre(xt_ref, xc_ref, cw_ref, cb_ref, dtr_ref, sp_ref, n):
    cur = xc_ref[...]
    tail = jnp.where(n > 0, xt_ref[...], 0.0)
    sh = _conv_shifts(tail, cur)
    u = cb_ref[...] + cw_ref[CONVK - 1:CONVK, :] * sh[0]
    for j in range(1, CONVK):
        u = u + cw_ref[CONVK - 1 - j:CONVK - j, :] * sh[j]
    dt = _softplus(dtr_ref[...] + sp_ref[0:1, :])
    acs = _cumsum_rows(dt * -jnp.exp(sp_ref[1:2, :]), False)
    return u, dt, acs


def _gated_norm_fwd(y, z, sgz, nw):
    yz = y * (z * sgz)
    parts = []
    for g in range(2):
        t = yz[:, g * 256:(g + 1) * 256]
        parts.append(t * lax.rsqrt(jnp.mean(t * t, axis=-1, keepdims=True) + EPS))
    return jnp.concatenate(parts, axis=1) * nw


HPG = 4
GW = HPG * HD


class _SsdChunk:
    def __init__(self, xc, dt, acs, spv, e64, e128, decay=None):
        self.e64, self.e128 = e64, e128
        alast = acs[LB - 1:LB, :]
        self.e_all = jnp.exp(acs)
        self.dte_all = jnp.exp(alast - acs)
        self.elast = jnp.exp(alast)
        wide = _dot_sel(jnp.concatenate([dt, self.e_all, self.dte_all], axis=0), e64, 2)
        self.dt_x, self.e_x, self.dte_x = wide[0:LB], wide[LB:2 * LB], wide[2 * LB:3 * LB]
        self.dsk_x = _dot_sel(spv, e64, 3)[2:3, :]
        if decay is None:
            acs_t = jnp.transpose(acs)
            ac_x = _dot_sel(acs, e128, 3)
            row = lax.broadcasted_iota(jnp.int32, (HPG * LB, LB), 0)
            col = lax.broadcasted_iota(jnp.int32, (HPG * LB, LB), 1)
            causal = (row % LB) >= col
        lane = lax.broadcasted_iota(jnp.int32, (LB, GW), 1)
        self.head_lanes = [(lane >= hh * HD) & (lane < (hh + 1) * HD) for hh in range(HPG)]
        self.xs, self.xdt, self.b, self.c, self.bb, self.cb16, self.cbm, self.dm_st, self.m_st = ([] for _ in range(9))
        for g in range(2):
            heads = range(HPG * g, HPG * (g + 1))
            if decay is None:
                ac_st = jnp.concatenate([ac_x[:, j * LB:(j + 1) * LB] for j in heads], axis=0)
                ar_st = jnp.concatenate([jnp.broadcast_to(acs_t[j:j + 1, :], (LB, LB)) for j in heads], axis=0)
                dm_st = jnp.exp(jnp.where(causal, ac_st - ar_st, NEG))
            else:
                dm_st = decay[g]
            bg = xc[:, SW + g * NST:SW + (g + 1) * NST]
            cg = xc[:, SW + 2 * NST + g * NST:SW + 2 * NST + (g + 1) * NST]
            bgb, cgb = bg.astype(BF16), cg.astype(BF16)
            cbm = _dot_nt(cgb, bgb)
            xs_g = xc[:, g * GW:(g + 1) * GW]
            self.xs.append(xs_g)
            self.xdt.append(xs_g * self.dt_x[:, g * GW:(g + 1) * GW])
            self.b.append(bg)
            self.c.append(cg)
            self.bb.append(bgb)
            self.cb16.append(cgb)
            self.cbm.append(cbm)
            self.dm_st.append(dm_st)
            self.m_st.append(jnp.concatenate([cbm] * HPG, axis=0) * dm_st)

    def elast_rows(self, g):
        return jnp.concatenate([jnp.broadcast_to(self.elast[:, j:j + 1], (HD, NST))
                                for j in range(HPG * g, HPG * (g + 1))], axis=0)

    def diag_blocks(self, stacked):
        out = stacked[(HPG - 1) * LB:HPG * LB, :]
        for hh in range(HPG - 2, -1, -1):
            out = jnp.where(self.head_lanes[hh], stacked[hh * LB:(hh + 1) * LB, :], out)
        return out

    def block_diag(self, v):
        return jnp.concatenate([jnp.where(self.head_lanes[hh], v, 0.0) for hh in range(HPG)], axis=0)


def _ssd_fwd_block(n, xt_ref, xc_ref, cw_ref, cb_ref, dtr_ref, sp_ref, z_ref, nw_ref, e64_ref, e128_ref,
                   yn_ref, y_ref, hs_ref, u_ref, dt_ref, acs_ref, dm_ref, h_scr):
    @pl.when(n == 0)
    def _():
        h_scr[...] = jnp.zeros_like(h_scr)

    h_all = h_scr[...]
    hs_ref[0] = h_all
    u, dt, acs = _ssd_pre(xt_ref, xc_ref, cw_ref, cb_ref, dtr_ref, sp_ref, n)
    u_ref[...] = u
    dt_ref[...] = dt
    acs_ref[...] = acs
    xc = u * _sigmoid(u)
    ck = _SsdChunk(xc, dt, acs, sp_ref[...], e64_ref[...], e128_ref[...])
    dm_ref[0] = jnp.concatenate(ck.dm_st, axis=0)
    ys, hn = [], []
    for g in range(2):
        gl = slice(g * GW, (g + 1) * GW)
        xdt = ck.xdt[g]
        hg = h_all[gl, :]
        y_diag = ck.diag_blocks(_dot(ck.m_st[g].astype(BF16), xdt.astype(BF16)))
        y_off = ck.e_x[:, gl] * _dot_nt(ck.cb16[g], hg.astype(BF16))
        ys.append(y_diag + y_off + ck.xs[g] * ck.dsk_x[:, gl])
        hn.append(hg * ck.elast_rows(g) + _dot_tn((xdt * ck.dte_x[:, gl]).astype(BF16), ck.bb[g]))
    h_scr[...] = jnp.concatenate(hn, axis=0)
    y = jnp.concatenate(ys, axis=1)
    y_ref[...] = y
    z = z_ref[...]
    yn_ref[...] = _gated_norm_fwd(y, z, _sigmoid(z), nw_ref[...]).astype(BF16)


def _mixer_fwd(q, k, v, sinks8, xbc, conv_w8, conv_b, dtr, ssm_p, z, nw, gathers):
    s = q.shape[0]
    nb = s // LB
    bps = _blocks_per_step(nb)
    nsteps = nb // bps
    tl = bps * LB
    cur = lambda n: (n, 0)
    prev = lambda n: (jnp.maximum(n * bps - 1, 0), 0)
    items, ex_shapes, n_g = _exchange_items(gathers, [])
    ne = len(items)

    n_in, n_out = 16, 10
    relay_step = (3 * (nsteps - 1)) // 4
    stack = pl.BlockSpec((bps, NH * LB, LB), lambda n: (n, 0, 0))
    e64, e128 = _head_expanders()

    def body(*refs):
        (q_ref, kp_ref, kc_ref, vp_ref, vc_ref, sink_ref, xt_ref, xc_ref, cw_ref, cb_ref, dtr_ref, sp_ref, z_ref,
         nw_ref, e64_ref, e128_ref) = refs[:n_in]
        ex_in = refs[n_in:n_in + ne]
        (o_ref, yn_ref, y_ref, hs_ref, u_ref, dt_ref, acs_ref, dm_ref, pr_ref,
         ps_ref) = refs[n_in + ne:n_in + n_out + ne]
        ex_out = refs[n_in + n_out + ne:n_in + n_out + 2 * ne]
        h_scr = refs[n_in + n_out + 2 * ne]
        sems = refs[n_in + n_out + 1 + 2 * ne:]
        n = pl.program_id(0)

        @pl.when(n == 0)
        def _():
            _Exchange(n_g, ex_in, ex_out, sems).two_level_start()

        for sub in range(bps):
            blk = n * bps + sub
            r = slice(sub * LB, (sub + 1) * LB)
            before = slice((sub - 1) * LB, sub * LB)
            one = slice(sub, sub + 1)
            _attn_fwd_block(blk, q_ref.at[r], kp_ref if sub == 0 else kc_ref.at[before], kc_ref.at[r],
                            vp_ref if sub == 0 else vc_ref.at[before], vc_ref.at[r], sink_ref,
                            o_ref.at[r], pr_ref.at[one], ps_ref.at[one])
            _ssd_fwd_block(blk, xt_ref if sub == 0 else xc_ref.at[sub * LB - SUB:sub * LB], xc_ref.at[r], cw_ref,
                           cb_ref, dtr_ref.at[r], sp_ref, z_ref.at[r], nw_ref, e64_ref, e128_ref,
                           yn_ref.at[r], y_ref.at[r], hs_ref.at[one], u_ref.at[r], dt_ref.at[r], acs_ref.at[r],
                           dm_ref.at[one], h_scr)

        @pl.when(n == relay_step)
        def _():
            _Exchange(n_g, ex_in, ex_out, sems).two_level_relay()

        @pl.when(n == nsteps - 1)
        def _():
            _Exchange(n_g, ex_in, ex_out, sems).two_level_finish()

    any_spec = pl.BlockSpec(memory_space=pl.ANY)
    tail = pl.BlockSpec((SUB, D), lambda n: (jnp.maximum(n * (tl // SUB) - 1, 0), 0))
    outs = _pcall(
        body, name="mixer_fwd", grid=(nsteps,),
        in_specs=[pl.BlockSpec((tl, AW), cur), pl.BlockSpec((LB, KVW), prev), pl.BlockSpec((tl, KVW), cur),
                  pl.BlockSpec((LB, KVW), prev), pl.BlockSpec((tl, KVW), cur), _const((8, LB)),
                  tail, pl.BlockSpec((tl, D), cur), _const((8, D)), _const((1, D)),
                  pl.BlockSpec((tl, LB), cur), _const((8, LB)), pl.BlockSpec((tl, SW), cur), _const((1, SW)),
                  _const(e64.shape), _const(e128.shape)]
        + [any_spec] * ne,
        out_specs=[pl.BlockSpec((tl, AW), cur), pl.BlockSpec((tl, SW), cur), pl.BlockSpec((tl, SW), cur),
                   pl.BlockSpec((bps, NH * HD, NST), lambda n: (n, 0, 0)), pl.BlockSpec((tl, D), cur),
                   pl.BlockSpec((tl, LB), cur), pl.BlockSpec((tl, LB), cur),
                   stack, stack, stack] + [any_spec] * ne,
        out_shape=[_sds((s, AW), BF16), _sds((s, SW), BF16), _sds((s, SW), F32), _sds((nb, NH * HD, NST), F32),
                   _sds((s, D), F32), _sds((s, LB), F32), _sds((s, LB), F32), _sds((nb, NH * LB, LB), F32),
                   _sds((nb, NH * LB, LB), BF16), _sds((nb, NH * LB, LB), BF16)]
        + ex_shapes,
        scratch_shapes=[pltpu.VMEM((NH * HD, NST), F32)] + _exchange_sems(ne),
        compiler_params=_params(),
    )(q, k, k, v, v, sinks8, xbc, xbc, conv_w8, conv_b, dtr, ssm_p, z, nw, e64, e128, *items)
    return outs[:n_out], outs[n_out:]


def _blocks_per_step(nb):
    return next(b for b in (4, 2, 1) if nb % b == 0)


def _head_expanders():
    j = lax.broadcasted_iota(jnp.int32, (LB, NH * HD), 0)
    e64 = (lax.broadcasted_iota(jnp.int32, (LB, NH * HD), 1) // HD == j).astype(BF16)
    j = lax.broadcasted_iota(jnp.int32, (LB, NH * LB), 0)
    e128 = (lax.broadcasted_iota(jnp.int32, (LB, NH * LB), 1) // LB == j).astype(BF16)
    return e64, e128


def _outproj_ffn_fwd_loss(attn, yn, x, tgt, mod8, n2w, fnw, w_out, w_gu_t, w_down):
    s = x.shape[0]
    tf = min(256, s)

    def body(a_ref, y_ref, x_ref, t_ref, mod_ref, nw_ref, fw_ref, wo_ref, wgu_ref, wd_ref,
             x2_ref, h2_ref, mo_ref, mix_ref, gu_ref, act_ref, dx3_ref, sm_ref):
        i = pl.program_id(0)

        @pl.when(i == 0)
        def _():
            sm_ref[...] = jnp.zeros_like(sm_ref)

        mix = jnp.concatenate([a_ref[...], y_ref[...]], axis=1)
        mix_ref[...] = mix
        mo = _dot(mix, wo_ref[...])
        mo_ref[...] = mo.astype(BF16)
        x2 = x_ref[...] + mod_ref[2:3, :] * mo
        x2_ref[...] = x2
        h2 = _norm_mod_fwd(x2, nw_ref[...], mod_ref[3:4, :], mod_ref[4:5, :]).astype(BF16)
        h2_ref[...] = h2
        gu = _dot_nt(h2, wgu_ref[...])
        gu_ref[...] = gu.astype(BF16)
        g = gu[:, :DFF]
        act = (g * _sigmoid(g) * gu[:, DFF:]).astype(BF16)
        act_ref[...] = act
        ff = _dot(act, wd_ref[...])
        x3 = x2 + mod_ref[5:6, :] * ff
        r = lax.rsqrt(jnp.mean(x3 * x3, axis=-1, keepdims=True) + EPS)
        xh = x3 * r
        fw = fw_ref[...]
        err = xh * fw - t_ref[...]
        dy = err * (1.0 / D)
        dxh = dy * fw
        dx3 = r * (dxh - xh * jnp.mean(dxh * xh, axis=-1, keepdims=True))
        dx3_ref[...] = dx3
        sm_ref[0:1, :] += jnp.sum(dx3 * ff, axis=0, keepdims=True)
        sm_ref[1:2, :] += jnp.sum(dy * xh, axis=0, keepdims=True)
        sm_ref[2:3, :] += jnp.sum(err * err, axis=0, keepdims=True)

    return _pcall(
        body, name="outproj_ffn_fwd_loss", grid=(s // tf,),
        in_specs=[_rows(tf, AW), _rows(tf, SW), _rows(tf, D), _rows(tf, D), _const((8, D)), _const((1, D)),
                  _const((1, D)), _const((D, D)), _const((2 * DFF, D)), _const((DFF, D))],
        out_specs=[_rows(tf, D), _rows(tf, D), _rows(tf, D), _rows(tf, D), _rows(tf, 2 * DFF), _rows(tf, DFF),
                   _rows(tf, D), pl.BlockSpec((8, D), lambda i: (0, 0))],
        out_shape=[_sds((s, D), F32), _sds((s, D), BF16), _sds((s, D), BF16), _sds((s, D), BF16),
                   _sds((s, 2 * DFF), BF16), _sds((s, DFF), BF16), _sds((s, D), F32), _sds((8, D), F32)],
        compiler_params=_params(),
    )(attn, yn, x, tgt, mod8, n2w, fnw, w_out, w_gu_t, w_down)


def _ffn_bwd(dx3, gu, x2, mixout, mod8, n2w, w_gu, w_down, w_out):
    s = x2.shape[0]
    tb = min(256, s)

    def body(dx3_ref, gu_ref, x2_ref, mo_ref, mod_ref, nw_ref, wgu_ref, wd_ref, wo_ref,
             dx2_ref, dff_ref, dgu_ref, dmix_ref, dattn_ref, dyn_ref, sm_ref):
        i = pl.program_id(0)

        @pl.when(i == 0)
        def _():
            sm_ref[...] = jnp.zeros_like(sm_ref)

        dx3 = dx3_ref[...]
        dff = (dx3 * mod_ref[5:6, :]).astype(BF16)
        dff_ref[...] = dff
        dact = _dot_nt(dff, wd_ref[...])
        g = gu_ref[:, :DFF].astype(F32)
        u = gu_ref[:, DFF:].astype(F32)
        sg = _sigmoid(g)
        dgu = jnp.concatenate([dact * u * sg * (1.0 + g * (1.0 - sg)), dact * g * sg], axis=1).astype(BF16)
        dgu_ref[...] = dgu
        dh2 = _dot(dgu, wgu_ref[...])
        dxn, d_shift, d_scale, d_w = _norm_mod_bwd(x2_ref[...], dh2, nw_ref[...], mod_ref[4:5, :])
        dx2 = dx3 + dxn
        dx2_ref[...] = dx2
        sm_ref[0:1, :] += d_shift
        sm_ref[1:2, :] += d_scale
        sm_ref[2:3, :] += d_w
        sm_ref[3:4, :] += jnp.sum(dx2 * mo_ref[...].astype(F32), axis=0, keepdims=True)
        dmix = (dx2 * mod_ref[2:3, :]).astype(BF16)
        dmix_ref[...] = dmix
        dmi = _dot_nt(dmix, wo_ref[...])
        dattn_ref[...] = dmi[:, :AW].astype(BF16)
        dyn_ref[...] = dmi[:, AW:]

    return _pcall(
        body, name="ffn_bwd", grid=(s // tb,),
        in_specs=[_rows(tb, D), _rows(tb, 2 * DFF), _rows(tb, D), _rows(tb, D), _const((8, D)), _const((1, D)),
                  _const((2 * DFF, D)), _const((DFF, D)), _const((D, D))],
        out_specs=[_rows(tb, D), _rows(tb, D), _rows(tb, 2 * DFF), _rows(tb, D), _rows(tb, AW), _rows(tb, SW),
                   pl.BlockSpec((8, D), lambda i: (0, 0))],
        out_shape=[_sds((s, D), F32), _sds((s, D), BF16), _sds((s, 2 * DFF), BF16), _sds((s, D), BF16),
                   _sds((s, AW), BF16), _sds((s, SW), F32), _sds((8, D), F32)],
        compiler_params=_params(),
    )(dx3, gu, x2, mixout, mod8, n2w, w_gu, w_down, w_out)


def _ssd_bwd_block(i, *refs):
    def run(dyn_ref, y_ref, z_ref, x_ref, u_ref, dt_ref, acs_ref, dm_ref, cw_ref, sp_ref, nw_ref,
            hs_ref, e64_ref, e128_ref, dzxd_ref, sm_ref, dh_scr, dun_scr):
        @pl.when(i == 0)
        def _():
            dh_scr[...] = jnp.zeros_like(dh_scr)
            dun_scr[...] = jnp.zeros_like(dun_scr)
            sm_ref[...] = jnp.zeros_like(sm_ref)

        u, dt, acs = u_ref[...], dt_ref[...], acs_ref[...]
        sg_u = _sigmoid(u)
        xc = u * sg_u
        a_neg = -jnp.exp(sp_ref[1:2, :])
        ck = _SsdChunk(xc, dt, acs, sp_ref[...], e64_ref[...], e128_ref[...],
                       decay=[dm_ref[0, g * HPG * LB:(g + 1) * HPG * LB, :] for g in range(2)])
        h_all = hs_ref[0]
        dh_all = dh_scr[...]
        riota = lax.broadcasted_iota(jnp.int32, (LB, LB), 0)
        lane1 = lax.broadcasted_iota(jnp.int32, (1, LB), 1)

        z = z_ref[...]
        y = y_ref[...]
        sgz = _sigmoid(z)
        sz = z * sgz
        yz = y * sz
        nwv = nw_ref[...]
        dyn_v = dyn_ref[...]
        dyhat = dyn_v * nwv
        yhat_parts, dyz_parts = [], []
        for g in range(2):
            gs = slice(g * 256, (g + 1) * 256)
            t = yz[:, gs]
            rg = lax.rsqrt(jnp.mean(t * t, axis=-1, keepdims=True) + EPS)
            yh = t * rg
            dyh = dyhat[:, gs]
            yhat_parts.append(yh)
            dyz_parts.append(rg * (dyh - yh * jnp.mean(dyh * yh, axis=-1, keepdims=True)))
        yhat = jnp.concatenate(yhat_parts, axis=1)
        dyz = jnp.concatenate(dyz_parts, axis=1)
        sm_ref[5:6, 0:SW] += jnp.sum(dyn_v * yhat, axis=0, keepdims=True)
        dy = dyz * sz
        dzxd_ref[:, 0:SW] = (dyz * y * sgz * (1.0 + z * (1.0 - sgz))).astype(BF16)

        cat = lambda parts: jnp.concatenate(parts, axis=1)
        dxs, dbs, dcs, dhp, g_cat, de_x, ddte_x, ddt_x, ddsk_x = ([] for _ in range(9))
        dacs_t = jnp.zeros((LB, LB), F32)
        hsum = jnp.zeros((1, LB), F32)
        for g in range(2):
            gl = slice(g * GW, (g + 1) * GW)
            xs_g, xdt, bgb, cgb = ck.xs[g], ck.xdt[g], ck.bb[g], ck.cb16[g]
            m_st, dm_st = ck.m_st[g], ck.dm_st[g]
            dt_x, e_x, dte_x = ck.dt_x[:, gl], ck.e_x[:, gl], ck.dte_x[:, gl]
            xdtb = xdt.astype(BF16)
            hg, dhn = h_all[gl, :], dh_all[gl, :]
            hb, dhnb = hg.astype(BF16), dhn.astype(BF16)
            dy_g = dy[:, gl]
            ddsk_x.append(jnp.sum(dy_g * xs_g, axis=0, keepdims=True))
            dy_bd = ck.block_diag(dy_g).astype(BF16)
            dm4 = _dot_nt(dy_bd, xdtb)
            dxdt = _dot_tn(m_st.astype(BF16), dy_bd)
            gmat = dm4 * m_st
            dcbm = dm4 * dm_st
            dcb = dcbm[0:LB] + dcbm[LB:2 * LB] + dcbm[2 * LB:3 * LB] + dcbm[3 * LB:4 * LB]
            g_cat.append(cat([gmat[hh * LB:(hh + 1) * LB, :] for hh in range(HPG)]))
            for hh in range(HPG):
                j = HPG * g + hh
                col_sum = jnp.sum(gmat[hh * LB:(hh + 1) * LB, :], axis=0, keepdims=True)
                dacs_t = dacs_t - jnp.where(riota == j, col_sum, 0.0)
                hsl = slice(hh * HD, (hh + 1) * HD)
                hsum = hsum + jnp.where(lane1 == j, jnp.sum(dhn[hsl, :] * hg[hsl, :]), 0.0)
            dchb = (dy_g * e_x).astype(BF16)
            dcg = _dot(dchb, hb)
            dh_prev = _dot_tn(dchb, cgb)
            de_x.append(dy_g * _dot_nt(cgb, hb))
            dxs_s = _dot_nt(bgb, dhnb)
            dbg = _dot((xdt * dte_x).astype(BF16), dhnb)
            dxdt = dxdt + dxs_s * dte_x
            ddte_x.append(dxs_s * xdt)
            dhp.append(dhn * ck.elast_rows(g) + dh_prev)
            dxs.append(dy_g * ck.dsk_x[:, gl] + dxdt * dt_x)
            ddt_x.append(dxdt * xs_g)
            dcbb = dcb.astype(BF16)
            dbs.append(dbg + _dot_tn(dcbb, cgb))
            dcs.append(dcg + _dot(dcbb, bgb))
        dh_scr[...] = jnp.concatenate(dhp, axis=0)
        red = _dot_nt_sel(jnp.concatenate([cat(de_x), cat(ddte_x), cat(ddt_x)], axis=0), ck.e64, 1)
        de_c, ddte_c, ddt_c = red[0:LB], red[LB:2 * LB], red[2 * LB:3 * LB]
        ddsk = _dot_nt_sel(jnp.broadcast_to(cat(ddsk_x), (SUB, NH * HD)), ck.e64, 2)[0:1, :]
        t1 = ddte_c * ck.dte_all
        dalast = jnp.sum(t1, axis=0, keepdims=True) + hsum * ck.elast
        dacs = (_dot_nt_sel(cat(g_cat), ck.e128, 2) + de_c * ck.e_all - t1 + jnp.transpose(dacs_t)
                + jnp.where(riota == LB - 1, dalast, 0.0))
        da = _cumsum_rows(dacs, True)
        ddt = ddt_c + da * a_neg
        da_log = jnp.sum(da * dt, axis=0, keepdims=True) * a_neg
        ddtr = ddt * (1.0 - jnp.exp(-dt))
        dzxd_ref[:, SW + D:ZXD] = ddtr.astype(BF16)
        sm_ref[6:7, 0:LB] += jnp.sum(ddtr, axis=0, keepdims=True)
        sm_ref[6:7, LB:2 * LB] += da_log
        sm_ref[6:7, 2 * LB:3 * LB] += ddsk

        du = cat(dxs + dbs + dcs) * (sg_u * (1.0 + u * (1.0 - sg_u)))
        xv = x_ref[...]
        adv = [du] + _conv_advances(du, dun_scr[...])
        sm_ref[0:1, :] += jnp.sum(du, axis=0, keepdims=True)
        dxbc = cw_ref[CONVK - 1:CONVK, :] * du
        for j in range(CONVK):
            sm_ref[CONVK - j:CONVK + 1 - j, :] += jnp.sum(adv[j] * xv, axis=0, keepdims=True)
            if j:
                dxbc = dxbc + cw_ref[CONVK - 1 - j:CONVK - j, :] * adv[j]
        dun_scr[...] = du[0:SUB, :]
        dzxd_ref[:, SW:SW + D] = dxbc.astype(BF16)

    run(*refs)


def _attn_bwd_block(i, q_ref, kp_ref, kc_ref, vp_ref, vc_ref, o_ref, do_ref, cos_ref, sin_ref, pr_ref, ps_ref,
                    dq_ref, dkv_ref, ds_ref, ck_scr, cv_scr):
    @pl.when(i == 0)
    def _():
        ds_ref[...] = jnp.zeros_like(ds_ref)
        ck_scr[...] = jnp.zeros_like(ck_scr)
        cv_scr[...] = jnp.zeros_like(cv_scr)

    qv, ov, dov = q_ref[...], o_ref[...], do_ref[...]
    kcat = jnp.concatenate([kp_ref[...], kc_ref[...]], axis=0)
    vcat = jnp.concatenate([vp_ref[...], vc_ref[...]], axis=0)
    upper = _upper_mask()
    srow = lax.broadcasted_iota(jnp.int32, (8, LB), 0)
    slane = lax.broadcasted_iota(jnp.int32, (8, LB), 1)
    dsink = jnp.zeros((8, LB), F32)
    dq_g, dk_g, dv_g = [], [], []
    for g in range(NQ // QPG):
        sl = slice(g * HD, (g + 1) * HD)
        qg = _stack_heads(qv, g)
        dog = _stack_heads(dov, g)
        rows = slice(g * QPG * LB, (g + 1) * QPG * LB)
        probs = pr_ref[0, rows, :].astype(F32)
        psink = ps_ref[0, rows, :].astype(F32)
        delta = _row_sums_wide(dog.astype(F32) * _stack_heads(ov, g).astype(F32), 2)
        dsc = probs * (_band(upper, _dot_nt(dog, vcat[0:LB, sl]), _dot_nt(dog, vcat[LB:2 * LB, sl])) - delta)
        sink_terms = (psink * delta)[:, 0:1]
        for hh in range(QPG):
            dsink = dsink - jnp.where((srow == QPG * g + hh) & (slane == 0),
                                      jnp.sum(sink_terms[hh * LB:(hh + 1) * LB, :]), 0.0)
        ds_p = jnp.where(upper, dsc, 0.0).astype(BF16)
        ds_c = jnp.where(upper, 0.0, dsc).astype(BF16)
        dq_g.append((_dot(ds_p, kcat[0:LB, sl]) + _dot(ds_c, kcat[LB:2 * LB, sl])) * ATT_SCALE)
        dk_g.append(jnp.concatenate([_dot_tn(ds_p, qg), _dot_tn(ds_c, qg)], axis=0) * ATT_SCALE)
        dv_g.append(jnp.concatenate([_dot_tn(jnp.where(upper, probs, 0.0).astype(BF16), dog),
                                     _dot_tn(jnp.where(upper, 0.0, probs).astype(BF16), dog)], axis=0))
    ds_ref[...] += dsink
    cs = cos_ref[...]
    sn = sin_ref[...]
    dk2 = jnp.concatenate(dk_g, axis=1)
    dv2 = jnp.concatenate(dv_g, axis=1)
    for a, tile in enumerate(_unstack_heads(dq_g)):
        dq_ref[:, a * LB:(a + 1) * LB] = _rope(tile, cs, sn, True).astype(BF16)
    dkv_ref[:, 0:KVW] = _rope(ck_scr[...] + dk2[LB:2 * LB, :], cs, sn, True).astype(BF16)
    dkv_ref[:, KVW:2 * KVW] = (cv_scr[...] + dv2[LB:2 * LB, :]).astype(BF16)
    ck_scr[...] = dk2[0:LB, :]
    cv_scr[...] = dv2[0:LB, :]


def _mixer_bwd(q, k, v, o, do, cos, sin, probs, psink, dyn, y, z, xbc, u, dtv, acs, decay, conv_w8, ssm_p, nw, hs,
               scatters):
    s = q.shape[0]
    nb = s // LB
    bps = _blocks_per_step(nb)
    nsteps = nb // bps
    tl = bps * LB
    cur = lambda i: (nsteps - 1 - i, 0)
    prev = lambda i: (jnp.maximum((nsteps - 1 - i) * bps - 1, 0), 0)
    n_in = 25
    items, ex_shapes, n_g = _exchange_items([], scatters)
    ne = len(items)
    e64, e128 = _head_expanders()
    stack = pl.BlockSpec((bps, NH * LB, LB), lambda i: (nsteps - 1 - i, 0, 0))

    def body(*refs):
        i = pl.program_id(0)
        (q_ref, kp_ref, kc_ref, vp_ref, vc_ref, o_ref, do_ref, cos_ref, sin_ref, pr_ref, ps_ref,
         dyn_ref, y_ref, z_ref, x_ref, u_ref, dt_ref, acs_ref, dm_ref, cw_ref, sp_ref, nw_ref,
         hs_ref, e64_ref, e128_ref) = refs[:n_in]
        ex_in = refs[n_in:n_in + ne]
        dp_ref, ds_ref, sm_ref = refs[n_in + ne:n_in + ne + 3]
        ex_out = refs[n_in + ne + 3:n_in + 2 * ne + 3]
        ck_scr, cv_scr, dh_scr, dun_scr = refs[n_in + 2 * ne + 3:n_in + 2 * ne + 7]
        sems = refs[n_in + 2 * ne + 7:]

        @pl.when(i == 0)
        def _():
            _Exchange(n_g, ex_in, ex_out, sems).start()

        for back in range(bps):
            sub = bps - 1 - back
            step = i * bps + back
            r = slice(sub * LB, (sub + 1) * LB)
            before = slice((sub - 1) * LB, sub * LB)
            one = slice(sub, sub + 1)
            _attn_bwd_block(step, q_ref.at[r], kp_ref if sub == 0 else kc_ref.at[before], kc_ref.at[r],
                            vp_ref if sub == 0 else vc_ref.at[before], vc_ref.at[r], o_ref.at[r], do_ref.at[r],
                            cos_ref.at[r], sin_ref.at[r], pr_ref.at[one], ps_ref.at[one],
                            dp_ref.at[r, O_Q:O_K], dp_ref.at[r, O_K:O_Z], ds_ref, ck_scr, cv_scr)
            _ssd_bwd_block(step, dyn_ref.at[r], y_ref.at[r], z_ref.at[r], x_ref.at[r], u_ref.at[r], dt_ref.at[r],
                           acs_ref.at[r], dm_ref.at[one], cw_ref, sp_ref, nw_ref,
                           hs_ref.at[one], e64_ref, e128_ref, dp_ref.at[r, O_Z:INP], sm_ref, dh_scr, dun_scr)

        @pl.when(i == nsteps - 1)
        def _():
            _Exchange(n_g, ex_in, ex_out, sems).finish()

    any_spec = pl.BlockSpec(memory_space=pl.ANY)
    outs = _pcall(
        body, name="mixer_bwd", grid=(nsteps,),
        in_specs=[pl.BlockSpec((tl, AW), cur), pl.BlockSpec((LB, KVW), prev), pl.BlockSpec((tl, KVW), cur),
                  pl.BlockSpec((LB, KVW), prev), pl.BlockSpec((tl, KVW), cur), pl.BlockSpec((tl, AW), cur),
                  pl.BlockSpec((tl, AW), cur), pl.BlockSpec((tl, LB), cur), pl.BlockSpec((tl, LB), cur),
                  stack, stack,
                  pl.BlockSpec((tl, SW), cur), pl.BlockSpec((tl, SW), cur), pl.BlockSpec((tl, SW), cur),
                  pl.BlockSpec((tl, D), cur), pl.BlockSpec((tl, D), cur), pl.BlockSpec((tl, LB), cur),
                  pl.BlockSpec((tl, LB), cur), stack,
                  _const((8, D)), _const((8, LB)), _const((1, SW)),
                  pl.BlockSpec((bps, NH * HD, NST), lambda i: (nsteps - 1 - i, 0, 0)),
                  _const(e64.shape), _const(e128.shape)] + [any_spec] * ne,
        out_specs=[pl.BlockSpec((tl, INP), cur), pl.BlockSpec((8, LB), lambda i: (0, 0)),
                   pl.BlockSpec((8, D), lambda i: (0, 0))] + [any_spec] * ne,
        out_shape=[_sds((s, INP), BF16), _sds((8, LB), F32), _sds((8, D), F32)] + ex_shapes,
        scratch_shapes=[pltpu.VMEM((LB, KVW), F32), pltpu.VMEM((LB, KVW), F32),
                        pltpu.VMEM((NH * HD, NST), F32), pltpu.VMEM((SUB, D), F32)]
        + _exchange_sems(ne),
        compiler_params=_params(),
    )(q, k, k, v, v, o, do, cos, sin, probs, psink, dyn, y, z, xbc, u, dtv, acs, decay, conv_w8, ssm_p, nw, hs,
      e64, e128, *items)
    return outs[0], outs[1], outs[2], outs[3:]


def _inproj_bwd(dproj, x, dx2, mod8, n1w, w_in_t, scatters, smalls):
    s = x.shape[0]
    tt = min(512, s)
    nt = s // tt
    items, ex_shapes, n_g = _exchange_items([], scatters)
    ne = len(items)
    n_in = 10

    def body(*refs):
        dp_ref, x_ref, dx2_ref, mod_ref, nw_ref, w_ref, f_ref, b_ref, s_ref, k_ref = refs[:n_in]
        ex_in = refs[n_in:n_in + ne]
        gx_ref, sm_ref = refs[n_in + ne:n_in + 2 + ne]
        ex_out = refs[n_in + 2 + ne:n_in + 2 + 2 * ne]
        gpack_ref = refs[n_in + 2 + 2 * ne]
        pack_scr = refs[n_in + 3 + 2 * ne]
        sems = refs[n_in + 4 + 2 * ne:n_in + 7 + 2 * ne]
        pack_sems = refs[n_in + 7 + 2 * ne:]
        i = pl.program_id(0)

        @pl.when(i == 0)
        def _():
            sm_ref[...] = jnp.zeros_like(sm_ref)
            _Exchange(n_g, ex_in, ex_out, sems).start()

        w = w_ref[...]
        hr = tt // 2
        dh1 = [_dot(dp_ref[h * hr:(h + 1) * hr, :], w) for h in range(2)]
        sums = jnp.zeros((3, D), F32)
        for h in range(2):
            rows = slice(h * hr, (h + 1) * hr)
            dxn, d_shift, d_scale, d_w = _norm_mod_bwd(x_ref[rows, :], dh1[h], nw_ref[...], mod_ref[1:2, :])
            gx_ref[rows, :] = dx2_ref[rows, :] + dxn
            sums = sums + jnp.concatenate([d_shift, d_scale, d_w], axis=0)
        sm_ref[0:3, :] += sums

        @pl.when(i == nt - 1)
        def _():
            _pack_rows(f_ref, b_ref, s_ref, sm_ref, k_ref, pack_scr)
            small = _Exchange(1, [pack_scr], [gpack_ref], pack_sems)
            small.start()
            _Exchange(n_g, ex_in, ex_out, sems).finish()
            small.finish()

    any_spec = pl.BlockSpec(memory_space=pl.ANY)
    outs = _pcall(
        body, name="inproj_bwd", grid=(nt,),
        in_specs=[_rows(tt, INP), _rows(tt, D), _rows(tt, D), _const((8, D)), _const((1, D)), _const((INP, D)),
                  _const((8, D)), _const((8, D)), _const((8, D)), _const((8, LB))]
        + [any_spec] * ne,
        out_specs=[_rows(tt, D), pl.BlockSpec((8, D), lambda i: (0, 0))] + [any_spec] * (ne + 1),
        out_shape=[_sds((s, D), F32), _sds((8, D), F32)] + ex_shapes + [_sds((N_DEV, PACK_ROWS, D), F32)],
        scratch_shapes=[pltpu.VMEM((PACK_ROWS, D), F32)] + _exchange_sems(ne) + _exchange_sems(1),
        compiler_params=_params(),
    )(dproj, x, dx2, mod8, n1w, w_in_t, *smalls, *items)
    return outs[0], outs[2:2 + ne], outs[2 + ne]


def _wgrad(a, b, name):
    s, m = a.shape
    n = b.shape[1]
    tk = min(2048, s)
    wide = (1408, 1024, 512)
    tm = next((t for t in wide if m % t == 0), m)
    tn = n if n <= 2048 else _largest_divisor(n, wide)
    nk = s // tk

    def body(a_ref, b_ref, o_ref, acc):
        kk = pl.program_id(2)

        @pl.when(kk == 0)
        def _():
            acc[...] = jnp.zeros_like(acc)

        acc[...] += _dot_tn(a_ref[...], b_ref[...])

        @pl.when(kk == nk - 1)
        def _():
            o_ref[...] = acc[...].astype(BF16)

    return _pcall(
        body, name=name, grid=(m // tm, n // tn, nk),
        in_specs=[pl.BlockSpec((tk, tm), lambda i, j, kk: (kk, i)), pl.BlockSpec((tk, tn), lambda i, j, kk: (kk, j))],
        out_specs=pl.BlockSpec((tm, tn), lambda i, j, kk: (i, j)),
        out_shape=_sds((m, n), BF16),
        scratch_shapes=[pltpu.VMEM((tm, tn), F32)],
        compiler_params=_params(3),
    )(a, b)


PACK_ROWS = 24


def _pack_rows(f_ref, b_ref, s_ref, i_ref, k_ref, o_ref):
    o_ref[...] = jnp.zeros_like(o_ref)
    o_ref[0:2, :] = i_ref[0:2, :]
    o_ref[2:3, :] = b_ref[3:4, :]
    o_ref[3:5, :] = b_ref[0:2, :]
    o_ref[5:6, :] = f_ref[0:1, :]
    o_ref[6:7, :] = i_ref[2:3, :]
    o_ref[7:8, :] = b_ref[2:3, :]
    o_ref[8:9, :] = f_ref[1:2, :]
    o_ref[9:14, :] = s_ref[0:5, :]
    o_ref[14:15, :] = s_ref[5:6, :]
    o_ref[15:16, 0:3 * LB] = s_ref[6:7, 0:3 * LB]
    lane = lax.broadcasted_iota(jnp.int32, (1, LB), 1)
    sk = jnp.zeros((1, LB), F32)
    for h in range(NQ):
        sk = sk + jnp.where(lane == h, k_ref[h:h + 1, 0:1], 0.0)
    o_ref[15:16, 3 * LB:4 * LB] = sk
    o_ref[16:17, :] = f_ref[2:3, :]


def _exchange_items(gathers, scatters):
    items = list(gathers) + list(scatters)
    shapes = [_sds((N_DEV,) + a.shape, a.dtype) for a in gathers] + [_sds(a.shape, a.dtype) for a in scatters]
    return items, shapes, len(gathers)


def _exchange_sems(n):
    return [pltpu.SemaphoreType.DMA((n, N_DEV - 1)), pltpu.SemaphoreType.DMA((n, N_DEV - 1)),
            pltpu.SemaphoreType.DMA((n,))]


class _Exchange:
    def __init__(self, n_g, ins, outs, sems):
        self.n_g, self.ins, self.outs = n_g, ins, outs
        self.send_sems, self.recv_sems, self.loc_sems = sems
        xi, yi, ci = lax.axis_index("x"), lax.axis_index("y"), lax.axis_index("c")
        self.me = 4 * xi + 2 * yi + ci
        self.peers = []
        for r in range(1, N_DEV):
            px = 1 - xi if r & 4 else xi
            py = 1 - yi if r & 2 else yi
            pc = 1 - ci if r & 1 else ci
            self.peers.append(((px, py, pc), 4 * px + 2 * py + pc))

    def _copy(self, t, r, landing):
        dev, peer = self.peers[r]
        src = self.ins[t] if t < self.n_g else self.ins[t].at[peer]
        return pltpu.make_async_remote_copy(
            src_ref=src, dst_ref=self.outs[t].at[landing], send_sem=self.send_sems.at[t, r],
            recv_sem=self.recv_sems.at[t, r], device_id=dev, device_id_type=pl.DeviceIdType.MESH)

    def _local(self, t):
        src = self.ins[t] if t < self.n_g else self.ins[t].at[self.me]
        return pltpu.make_async_copy(src, self.outs[t].at[self.me], self.loc_sems.at[t])

    def start(self):
        for t in range(len(self.ins)):
            self._local(t).start()
            for r in range(N_DEV - 1):
                self._copy(t, r, self.me).start()

    def finish(self):
        n = len(self.ins)
        for t in range(n):
            for r in range(N_DEV - 1):
                self._copy(t, r, self.peers[r][1]).wait_recv()
        for t in range(n):
            for r in range(N_DEV - 1):
                self._copy(t, r, self.me).wait_send()
            self._local(t).wait()

    DIRECT = (0, 1, 3, 5)

    def two_level_start(self):
        for t in range(len(self.ins)):
            self._local(t).start()
            for r in self.DIRECT:
                self._copy(t, r, self.me).start()

    def _relay(self, t, r):
        peer = self.peers[r][1]
        return pltpu.make_async_remote_copy(
            src_ref=self.outs[t].at[peer], dst_ref=self.outs[t].at[peer], send_sem=self.send_sems.at[t, r + 1],
            recv_sem=self.recv_sems.at[t, r + 1], device_id=self.peers[0][0], device_id_type=pl.DeviceIdType.MESH)

    def two_level_relay(self):
        for t in range(len(self.ins)):
            for r in self.DIRECT[1:]:
                self._copy(t, r, self.peers[r][1]).wait_recv()
                self._relay(t, r).start()

    def two_level_finish(self):
        n = len(self.ins)
        for t in range(n):
            for r in (0, 2, 4, 6):
                self._copy(t, r, self.peers[r][1]).wait_recv()
        for t in range(n):
            for r in self.DIRECT:
                self._copy(t, r, self.me).wait_send()
            for r in self.DIRECT[1:]:
                self._relay(t, r).wait_send()
            self._local(t).wait()


def _prologue(c8, w_in_tb, cw8, w_cols, b_cols):
    ncol = w_cols.shape[1]

    def body(c_ref, win_ref, cw_ref, w_ref, b_ref, gc_ref, gin_ref, gcw_ref, gmod_ref, call_scr, mod_scr, loc_sem,
             *sems):
        big = _Exchange(2, [win_ref, cw_ref], [gin_ref, gcw_ref], sems[0:3])
        small = _Exchange(1, [c_ref], [gc_ref], sems[3:6])
        small.start()
        big.two_level_start()
        small.finish()
        landed = pltpu.make_async_copy(gc_ref, call_scr, loc_sem)
        landed.start()
        landed.wait()
        cv = call_scr[:, 0, :]
        sc = (cv * _sigmoid(cv)).astype(BF16)
        mod_scr[...] = _dot(sc, w_ref[...].astype(BF16)) + b_ref[...]
        mods = _Exchange(1, [mod_scr], [gmod_ref], sems[6:9])
        mods.start()
        mods.finish()
        big.two_level_relay()
        big.two_level_finish()

    any_spec = pl.BlockSpec(memory_space=pl.ANY)
    vmem_spec = pl.BlockSpec(memory_space=pltpu.VMEM)
    return _pcall(
        body, name="prologue", in_specs=[any_spec, any_spec, any_spec, vmem_spec, vmem_spec],
        out_specs=[any_spec] * 4,
        out_shape=[_sds((N_DEV,) + c8.shape, F32), _sds((N_DEV,) + w_in_tb.shape, BF16),
                   _sds((N_DEV,) + cw8.shape, F32), _sds((N_DEV, N_DEV, ncol), F32)],
        scratch_shapes=[pltpu.VMEM((N_DEV,) + c8.shape, F32), pltpu.VMEM((N_DEV, ncol), F32),
                        pltpu.SemaphoreType.DMA] + _exchange_sems(2) + _exchange_sems(1) + _exchange_sems(1),
        compiler_params=pltpu.CompilerParams(vmem_limit_bytes=VMEM_LIMIT),
    )(c8, w_in_tb, cw8, w_cols, b_cols)


def _adamw(w, g, m, v):
    m2 = ADAM_B1 * m + (1.0 - ADAM_B1) * g
    v2 = ADAM_B2 * v + (1.0 - ADAM_B2) * (g * g)
    m_hat = m2 / (1.0 - ADAM_B1 ** ADAM_STEP)
    v_hat = v2 / (1.0 - ADAM_B2 ** ADAM_STEP)
    delta = -ADAM_LR * (m_hat / (jnp.sqrt(v_hat) + ADAM_EPS) + ADAM_WD * w)
    return delta, m2, v2


def _sum_adamw(parts, w, m, v, name):
    rws, cols = w.shape
    tr = next((t for t in (256, 176, 128) if rws % t == 0), rws)

    def body(p_ref, w_ref, m_ref, v_ref, g_ref, d_ref, mo_ref, vo_ref):
        g = p_ref[0].astype(F32)
        for dev in range(1, N_DEV):
            g = g + p_ref[dev].astype(F32)
        g_ref[...] = g
        d_ref[...], mo_ref[...], vo_ref[...] = _adamw(w_ref[...], g, m_ref[...], v_ref[...])

    blk = pl.BlockSpec((tr, cols), lambda i: (i, 0))
    return _pcall(
        body, name=name, grid=(rws // tr,),
        in_specs=[pl.BlockSpec((N_DEV, tr, cols), lambda i: (0, i, 0)), blk, blk, blk],
        out_specs=[blk] * 4, out_shape=[_sds((rws, cols), F32)] * 4, compiler_params=_params(),
    )(parts, w, m, v)


def _sum_adamw_rowwise(parts, w, m, v, name):
    rws, _, cols = w.shape

    def body(p_ref, w_ref, m_ref, v_ref, g_ref, d_ref, mo_ref, vo_ref, ins, outs, sems):
        loads = [pltpu.make_async_copy(src.at[:, 0, :], ins.at[i], sems.at[i])
                 for i, src in enumerate((w_ref, m_ref, v_ref))]
        for load in loads:
            load.start()
        g = p_ref[0].astype(F32)
        for dev in range(1, N_DEV):
            g = g + p_ref[dev].astype(F32)
        for load in loads:
            load.wait()
        outs[0] = g
        outs[1], outs[2], outs[3] = _adamw(ins[0], g, ins[1], ins[2])
        stores = [pltpu.make_async_copy(outs.at[i], dst.at[:, 0, :], sems.at[len(loads) + i])
                  for i, dst in enumerate((g_ref, d_ref, mo_ref, vo_ref))]
        for store in stores:
            store.start()
        for store in stores:
            store.wait()

    any_spec = pl.BlockSpec(memory_space=pl.ANY)
    return _pcall(
        body, name=name, grid=(1,),
        in_specs=[_const((N_DEV, rws, cols)), any_spec, any_spec, any_spec],
        out_specs=[any_spec] * 4, out_shape=[_sds((rws, 1, cols), F32)] * 4,
        scratch_shapes=[pltpu.VMEM((3, rws, cols), F32), pltpu.VMEM((4, rws, cols), F32),
                        pltpu.SemaphoreType.DMA((7,))],
        compiler_params=_params(),
    )(parts, w, m, v)


def _wada_adamw(c_all, dmod_cols, w, m, v):
    rws, cols = w.shape
    tr = 256

    def body(c_ref, dm_ref, w_ref, m_ref, v_ref, g_ref, d_ref, mo_ref, vo_ref):
        cv = c_ref[...]
        sc = (cv * _sigmoid(cv)).astype(BF16)
        g = _dot_tn(sc, dm_ref[...].astype(BF16))
        g_ref[...] = g
        d_ref[...], mo_ref[...], vo_ref[...] = _adamw(w_ref[...], g, m_ref[...], v_ref[...])

    blk = pl.BlockSpec((tr, cols), lambda i: (i, 0))
    return _pcall(
        body, name="wada_adamw", grid=(rws // tr,),
        in_specs=[pl.BlockSpec((N_DEV, tr), lambda i: (0, i)), pl.BlockSpec((N_DEV, cols), lambda i: (0, 0)),
                  blk, blk, blk],
        out_specs=[blk] * 4, out_shape=[_sds((rws, cols), F32)] * 4, compiler_params=_params(),
    )(c_all, dmod_cols, w, m, v)


SMALL_NAMES = ("b_ada", "norm1_w", "conv_w", "conv_b", "dt_bias", "a_log", "d_skip", "attn_sinks", "ssm_norm_w",
               "norm2_w", "final_norm_w")


def _small_grads(tot, me):
    shard = D // N_DEV
    conv = tot[10:10 + CONVK, 0:shard]
    for dev in range(1, N_DEV):
        conv = jnp.where(me == dev, tot[10:10 + CONVK, dev * shard:(dev + 1) * shard], conv)
    return [
        jnp.concatenate([tot[r:r + 1, :] for r in range(N_MOD)], axis=1),
        tot[6:7, :], conv, tot[9:10, :],
        tot[15:16, 0:NH], tot[15:16, LB:LB + NH], tot[15:16, 2 * LB:2 * LB + NH], tot[15:16, 3 * LB:3 * LB + NQ],
        tot[14:15, 0:SW], tot[7:8, :], tot[8:9, :],
    ]


def _small_adamw(packs, ws, ms, vs):
    k = len(ws)

    def body(p_ref, *refs):
        w_refs, m_refs, v_refs = refs[:k], refs[k:2 * k], refs[2 * k:3 * k]
        loss_ref = refs[3 * k]
        g_refs, d_refs, mo_refs, vo_refs = (refs[3 * k + 1 + j * k:3 * k + 1 + (j + 1) * k] for j in range(4))
        tot_ref = refs[7 * k + 1]
        tot = p_ref[0]
        for dev in range(1, N_DEV):
            tot = tot + p_ref[dev]
        tot_ref[...] = tot
        loss_ref[...] = jnp.zeros((1, 1), F32) + (0.5 / D) * jnp.sum(tot[16:17, :])
        me = 4 * lax.axis_index("x") + 2 * lax.axis_index("y") + lax.axis_index("c")
        for i, g in enumerate(_small_grads(tot_ref, me)):
            g_refs[i][...] = g
            d_refs[i][...], mo_refs[i][...], vo_refs[i][...] = _adamw(w_refs[i][...], g, m_refs[i][...], v_refs[i][...])

    shp = [_sds(w.shape, F32) for w in ws]
    outs = _pcall(body, name="adamw_small", out_shape=[_sds((1, 1), F32)] + shp * 4,
                  scratch_shapes=[pltpu.VMEM((PACK_ROWS, D), F32)])(packs, *ws, *ms, *vs)
    return outs[0], outs[1:1 + k], outs[1 + k:1 + 2 * k], outs[1 + 2 * k:1 + 3 * k], outs[1 + 3 * k:]


def kernel(x, c, positions, w_ada, b_ada, norm1_w, w_in, conv_w, conv_b, dt_bias, a_log, d_skip, attn_sinks, ssm_norm_w, w_out, norm2_w, w_gate_up, w_down, final_norm_w, loss_target, m_w_ada, m_b_ada, m_norm1_w, m_w_in, m_conv_w, m_conv_b, m_dt_bias, m_a_log, m_d_skip, m_attn_sinks, m_ssm_norm_w, m_w_out, m_norm2_w, m_w_gate_up, m_w_down, m_final_norm_w, v_w_ada, v_b_ada, v_norm1_w, v_w_in, v_conv_w, v_conv_b, v_dt_bias, v_a_log, v_d_skip, v_attn_sinks, v_ssm_norm_w, v_w_out, v_norm2_w, v_w_gate_up, v_w_down, v_final_norm_w):
    s = x.shape[1]
    me = 4 * lax.axis_index("x") + 2 * lax.axis_index("y") + lax.axis_index("c")
    ada_cols = N_MOD * D // N_DEV

    c8 = jnp.pad(c, ((0, 7), (0, 0)))
    cw8 = jnp.pad(conv_w[0], ((0, 8 - CONVK), (0, 0)))
    w_in_t = jnp.transpose(w_in[0])
    rowwise = lambda a: jnp.transpose(a, (2, 0, 1))
    w_gu_t, m_w_gu_t, v_w_gu_t = (jnp.transpose(w_gate_up[0]), jnp.transpose(m_w_gate_up[0]),
                                  jnp.transpose(v_w_gate_up[0]))
    b_cols = lax.dynamic_slice(b_ada, (0, me * ada_cols), (1, ada_cols))
    g_c, g_in, g_cw, g_mod = _prologue(c8, w_in_t.astype(BF16), cw8, w_ada[0], b_cols)
    c_all = g_c[:, 0, :]
    w_in_f = jnp.pad(g_in.reshape(IN_PROJ, D), ((0, INP - IN_PROJ), (0, 0)))
    conv_w8 = jnp.transpose(g_cw, (1, 0, 2)).reshape(8, D)
    mod = lax.dynamic_index_in_dim(g_mod, me, axis=1, keepdims=False).reshape(N_MOD, D)
    mod8 = jnp.pad(mod, ((0, 8 - N_MOD), (0, 0)))

    half = HD // 2
    inv_freq = ROPE_THETA ** (-jnp.arange(half, dtype=F32) / half)
    invf = jnp.tile(inv_freq, LB // half).reshape(1, LB)
    lanes = lambda a: jnp.pad(a, ((0, 0), (0, LB - a.shape[1])))
    ssm_p = jnp.pad(jnp.concatenate([lanes(dt_bias), lanes(a_log), lanes(d_skip)], axis=0), ((0, 5), (0, 0)))
    sinks8 = jnp.broadcast_to(attn_sinks.reshape(NQ, 1), (NQ, LB))

    xs, tgt, fnw = x[0], loss_target[0], final_norm_w.reshape(1, D)

    q, k, v, z, xbc, dtr, h1, cos, sin = _inproj_fwd(xs, positions[0].reshape(s, 1), invf, mod8, norm1_w, w_in_f)
    (attn, yn, y, hs, conv_u, dtv, acs, decay, probs, psink), (g_out, g_gu, g_down) = _mixer_fwd(
        q, k, v, sinks8, xbc, conv_w8, conv_b, dtr, ssm_p, z, ssm_norm_w,
        [w_out[0].astype(BF16), w_gu_t.astype(BF16), w_down[0].astype(BF16)])
    w_out_f = g_out.reshape(D, D)
    w_gu_f = g_gu.reshape(2 * DFF, D)
    w_down_f = g_down.reshape(DFF, D)
    x2, h2, mo, mix, gu, act, dx3, sm_f = _outproj_ffn_fwd_loss(attn, yn, xs, tgt, mod8, norm2_w, fnw, w_out_f, w_gu_f,
                                                                 w_down_f)

    dx2, dff, dgu, dmix, dattn, dyn, sm_b = _ffn_bwd(dx3, gu, x2, mo, mod8, norm2_w, w_gu_f, w_down_f, w_out_f)
    p_gu = _wgrad(dgu, h2, "wgrad_gate_up").reshape(N_DEV, 2 * DFF // N_DEV, D)
    p_down = _wgrad(act, dff, "wgrad_down").reshape(N_DEV, DFF // N_DEV, D)
    p_out = _wgrad(mix, dmix, "wgrad_out").reshape(N_DEV, D // N_DEV, D)
    dproj, dsink, sm_s, (r_gu, r_down, r_out) = _mixer_bwd(
        q, k, v, attn, dattn, cos, sin, probs, psink, dyn, y, z, xbc, conv_u, dtv, acs, decay, conv_w8, ssm_p,
        ssm_norm_w, hs, [p_gu, p_down, p_out])
    p_in = _wgrad(dproj, h1, "wgrad_in")[:IN_PROJ].reshape(N_DEV, IN_PROJ // N_DEV, D)
    gx, (r_in,), g_pack = _inproj_bwd(dproj, xs, dx2, mod8, norm1_w, w_in_f, [p_in], (sm_f, sm_b, sm_s, dsink))

    dmod_all = g_pack[:, 0:N_MOD, :].reshape(N_DEV, N_MOD * D)
    dmod_cols = lax.dynamic_slice(dmod_all, (0, me * ada_cols), (N_DEV, ada_cols))

    big = {
        "w_ada": _wada_adamw(c_all, dmod_cols, w_ada[0], m_w_ada[0], v_w_ada[0]),
        "w_in": [jnp.transpose(t, (1, 2, 0))[0] for t in
                 _sum_adamw_rowwise(r_in, rowwise(w_in), rowwise(m_w_in), rowwise(v_w_in), "adamw_in")],
        "w_out": _sum_adamw(r_out, w_out[0], m_w_out[0], v_w_out[0], "adamw_out"),
        "w_gate_up": [jnp.transpose(t) for t in _sum_adamw(r_gu, w_gu_t, m_w_gu_t, v_w_gu_t, "adamw_gate_up")],
        "w_down": _sum_adamw(r_down, w_down[0], m_w_down[0], v_w_down[0], "adamw_down"),
    }
    small_w = {"b_ada": b_ada, "norm1_w": norm1_w, "conv_w": conv_w[0], "conv_b": conv_b, "dt_bias": dt_bias,
               "a_log": a_log, "d_skip": d_skip, "attn_sinks": attn_sinks, "ssm_norm_w": ssm_norm_w,
               "norm2_w": norm2_w, "final_norm_w": final_norm_w.reshape(1, D)}
    small_m = {"b_ada": m_b_ada, "norm1_w": m_norm1_w, "conv_w": m_conv_w[0], "conv_b": m_conv_b,
               "dt_bias": m_dt_bias, "a_log": m_a_log, "d_skip": m_d_skip, "attn_sinks": m_attn_sinks,
               "ssm_norm_w": m_ssm_norm_w, "norm2_w": m_norm2_w, "final_norm_w": m_final_norm_w.reshape(1, D)}
    small_v = {"b_ada": v_b_ada, "norm1_w": v_norm1_w, "conv_w": v_conv_w[0], "conv_b": v_conv_b,
               "dt_bias": v_dt_bias, "a_log": v_a_log, "d_skip": v_d_skip, "attn_sinks": v_attn_sinks,
               "ssm_norm_w": v_ssm_norm_w, "norm2_w": v_norm2_w, "final_norm_w": v_final_norm_w.reshape(1, D)}
    loss, s_g, s_d, s_m, s_v = _small_adamw(g_pack, [small_w[k] for k in SMALL_NAMES],
                                            [small_m[k] for k in SMALL_NAMES], [small_v[k] for k in SMALL_NAMES])

    order = ["w_ada", "b_ada", "norm1_w", "w_in", "conv_w", "conv_b", "dt_bias", "a_log", "d_skip", "attn_sinks",
             "ssm_norm_w", "w_out", "norm2_w", "w_gate_up", "w_down", "final_norm_w"]
    lead = {"w_ada", "w_in", "conv_w", "w_out", "w_gate_up", "w_down"}
    grads, deltas, new_m, new_v = [], [], [], []
    for name in order:
        if name in big:
            g, d, m2, v2 = big[name]
        else:
            i = SMALL_NAMES.index(name)
            g, d, m2, v2 = s_g[i], s_d[i], s_m[i], s_v[i]
        if name in lead:
            g, d, m2, v2 = g[None], d[None], m2[None], v2[None]
        if name == "final_norm_w":
            g, d, m2, v2 = g.reshape(D), d.reshape(D), m2.reshape(D), v2.reshape(D)
        grads.append(g)
        deltas.append(d)
        new_m.append(m2)
        new_v.append(v2)
    return (loss.reshape(()), gx[None], *grads, *deltas, *new_m, *new_v)
```

```python
import functools
import math

import jax
import jax.numpy as jnp
import numpy as np
from jax import lax
from jax.experimental import pallas as pl
from jax.experimental.pallas import tpu as pltpu

F32 = jnp.float32
BF16 = jnp.bfloat16

N_DEV = 8
D = 1024
HD = 64
NQ = 8
AW = 512
KVW = 128
SW = 512
NST = 128
NH = 8
LB = 128
CONVK = 4
DFF = 2816
N_MOD = 6
IN_PROJ = 2312
INP = 2432
O_Q, O_K, O_V, O_Z, O_XBC, O_DT = 0, 512, 640, 768, 1280, 2304
ZXD = INP - O_Z
EPS = 1e-6
NEG = -1e30
ROPE_THETA = 10000.0
VMEM_LIMIT = 56 * 1024 * 1024

ADAM_LR = 0.001
ADAM_B1 = 0.9
ADAM_B2 = 0.999
ADAM_EPS = 1e-08
ADAM_WD = 0.01
ADAM_STEP = 10

NT_DIMS = (((1,), (1,)), ((), ()))
TN_DIMS = (((0,), (0,)), ((), ()))


def _pcall(body, **kw):
    return pl.pallas_call(body, **kw)


def _sds(shape, dtype):
    return jax.ShapeDtypeStruct(shape, dtype)


def _params(n_grid=1):
    return pltpu.CompilerParams(dimension_semantics=("arbitrary",) * n_grid, vmem_limit_bytes=VMEM_LIMIT)


def _const(shape):
    return pl.BlockSpec(shape, lambda *_: (0,) * len(shape), pipeline_mode=pl.Buffered(1))


def _largest_divisor(n, candidates):
    for cand in candidates:
        if n % cand == 0:
            return cand
    raise ValueError(f"no tile in {candidates} divides {n}")


def _rows(t, w):
    return pl.BlockSpec((t, w), lambda i: (i, 0))


def _dot(a, b):
    return jnp.dot(a, b, preferred_element_type=F32)


def _dot_nt(a, b):
    return lax.dot_general(a, b, NT_DIMS, preferred_element_type=F32)


def _dot_tn(a, b):
    return lax.dot_general(a, b, TN_DIMS, preferred_element_type=F32)


def _sigmoid(v):
    return 1.0 / (1.0 + jnp.exp(-v))


def _softplus(v):
    return jnp.maximum(v, 0.0) + jnp.log1p(jnp.exp(-jnp.abs(v)))


def _rope_sign_mask(shape):
    lane = lax.broadcasted_iota(jnp.int32, shape, 1)
    return (lane % HD) < (HD // 2)


def _rope(t, cs, sn, inverse):
    r_dn = pltpu.roll(t, HD // 2, 1)
    r_up = pltpu.roll(t, LB - HD // 2, 1)
    first = _rope_sign_mask(t.shape)
    if inverse:
        rot = jnp.where(first, r_up, -r_dn)
    else:
        rot = jnp.where(first, -r_up, r_dn)
    return t * cs + rot * sn


def _norm_mod_fwd(xv, nw, shift, scale):
    r = lax.rsqrt(jnp.mean(xv * xv, axis=-1, keepdims=True) + EPS)
    xh = xv * r
    return (xh * nw) * (1.0 + scale) + shift


def _norm_mod_bwd(xv, dh, nw, scale):
    r = lax.rsqrt(jnp.mean(xv * xv, axis=-1, keepdims=True) + EPS)
    xh = xv * r
    xn = xh * nw
    d_shift = jnp.sum(dh, axis=0, keepdims=True)
    d_scale = jnp.sum(dh * xn, axis=0, keepdims=True)
    dxn = dh * (1.0 + scale)
    d_w = jnp.sum(dxn * xh, axis=0, keepdims=True)
    dxh = dxn * nw
    dx = r * (dxh - xh * jnp.mean(dxh * xh, axis=-1, keepdims=True))
    return dx, d_shift, d_scale, d_w


def _inproj_fwd(x, pos, invf, mod8, n1w, w_in):
    s = x.shape[0]
    tt = min(512, s)

    def body(x_ref, pos_ref, invf_ref, mod_ref, nw_ref, w_ref,
             q_ref, k_ref, v_ref, z_ref, xbc_ref, dtr_ref, h1_ref, cos_ref, sin_ref):
        h = _norm_mod_fwd(x_ref[...], nw_ref[...], mod_ref[0:1, :], mod_ref[1:2, :])
        hb = h.astype(BF16)
        h1_ref[...] = hb
        proj = _dot_nt(hb, w_ref[...])
        ang = pos_ref[...].astype(F32) * invf_ref[...]
        cs = jnp.cos(ang)
        sn = jnp.sin(ang)
        cos_ref[...] = cs
        sin_ref[...] = sn
        for a in range(AW // LB):
            q_ref[:, a * LB:(a + 1) * LB] = _rope(proj[:, O_Q + a * LB:O_Q + (a + 1) * LB], cs, sn, False).astype(BF16)
        k_ref[...] = _rope(proj[:, O_K:O_V], cs, sn, False).astype(BF16)
        v_ref[...] = proj[:, O_V:O_Z].astype(BF16)
        z_ref[...] = proj[:, O_Z:O_XBC]
        xbc_ref[...] = proj[:, O_XBC:O_DT]
        dtr_ref[...] = proj[:, O_DT:INP]

    return _pcall(
        body, name="inproj_fwd", grid=(s // tt,),
        in_specs=[_rows(tt, D), _rows(tt, 1), _const((1, LB)), _const((8, D)), _const((1, D)), _const((INP, D))],
        out_specs=[_rows(tt, AW), _rows(tt, KVW), _rows(tt, KVW), _rows(tt, SW), _rows(tt, D), _rows(tt, LB),
                   _rows(tt, D), _rows(tt, LB), _rows(tt, LB)],
        out_shape=[_sds((s, AW), BF16), _sds((s, KVW), BF16), _sds((s, KVW), BF16), _sds((s, SW), F32),
                   _sds((s, D), F32), _sds((s, LB), F32), _sds((s, D), BF16), _sds((s, LB), F32), _sds((s, LB), F32)],
        compiler_params=_params(),
    )(x, pos, invf, mod8, n1w, w_in)


QPG = 4
ATT_SCALE = 1.0 / math.sqrt(HD)


def _stack_heads(val, g):
    return jnp.concatenate([val[:, (QPG * g + hh) * HD:(QPG * g + hh + 1) * HD] for hh in range(QPG)], axis=0)


def _unstack_heads(groups):
    pieces = [grp[hh * LB:(hh + 1) * LB, :] for grp in groups for hh in range(QPG)]
    return [jnp.concatenate(pieces[2 * a:2 * a + 2], axis=1) for a in range(NQ // 2)]


def _upper_mask():
    row = lax.broadcasted_iota(jnp.int32, (QPG * LB, LB), 0)
    col = lax.broadcasted_iota(jnp.int32, (QPG * LB, LB), 1)
    return col > (row % LB)


def _sink_wide(sinks, g):
    return jnp.concatenate([jnp.broadcast_to(sinks[QPG * g + hh:QPG * g + hh + 1, 0:1], (LB, LB))
                            for hh in range(QPG)], axis=0)


def _row_sums_wide(v, terms):
    return _dot_sel(v, jnp.ones((v.shape[1], LB), BF16), terms)


def _band(upper, prev_part, cur_part):
    return jnp.where(upper, prev_part, cur_part)


def _attn_scores(n, qg, kcat, upper):
    sp = _dot_nt(qg, kcat[0:LB, :]) * ATT_SCALE
    sc = _dot_nt(qg, kcat[LB:2 * LB, :]) * ATT_SCALE
    return _band(upper, jnp.where(n > 0, sp, NEG), sc)


def _attn_softmax(comb, sink):
    m = jnp.maximum(jnp.max(comb, axis=-1, keepdims=True), sink)
    p = jnp.exp(comb - m)
    es = jnp.exp(sink - m)
    return p, es, _row_sums_wide(p, 1) + es


def _attn_fwd_block(n, q_ref, kp_ref, kc_ref, vp_ref, vc_ref, sink_ref, o_ref, pr_ref, ps_ref):
    qv = q_ref[...]
    kcat = jnp.concatenate([kp_ref[...], kc_ref[...]], axis=0)
    vcat = jnp.concatenate([vp_ref[...], vc_ref[...]], axis=0)
    sinks = sink_ref[...]
    upper = _upper_mask()
    outs = []
    for g in range(NQ // QPG):
        sl = slice(g * HD, (g + 1) * HD)
        rows = slice(g * QPG * LB, (g + 1) * QPG * LB)
        p, es, denom = _attn_softmax(_attn_scores(n, _stack_heads(qv, g), kcat[:, sl], upper), _sink_wide(sinks, g))
        rden = 1.0 / denom
        pr_ref[0, rows, :] = (p * rden).astype(BF16)
        ps_ref[0, rows, :] = (es * rden).astype(BF16)
        outs.append((_dot(jnp.where(upper, p, 0.0).astype(BF16), vcat[0:LB, sl])
                     + _dot(jnp.where(upper, 0.0, p).astype(BF16), vcat[LB:2 * LB, sl])) * rden[:, 0:HD])
    for g, grp in enumerate(outs):
        for hh in range(QPG):
            h = QPG * g + hh
            o_ref[:, h * HD:(h + 1) * HD] = grp[hh * LB:(hh + 1) * LB, :].astype(BF16)


def _cumsum_rows(a, reverse):
    row = lax.broadcasted_iota(jnp.int32, a.shape, 0)
    step = 1
    while step < LB:
        if reverse:
            a = a + jnp.where(row < LB - step, pltpu.roll(a, LB - step, 0), 0.0)
        else:
            a = a + jnp.where(row >= step, pltpu.roll(a, step, 0), 0.0)
        step *= 2
    return a


SUB = 8


def _conv_shifts(tail, cur):
    row = lax.broadcasted_iota(jnp.int32, tail.shape, 0)
    out = [cur]
    for j in range(1, CONVK):
        rolled = pltpu.roll(cur, j, 0)
        top = jnp.where(row < j, pltpu.roll(tail, j, 0), rolled[0:SUB, :])
        out.append(jnp.concatenate([top, rolled[SUB:, :]], axis=0))
    return out


def _conv_advances(du, head):
    row = lax.broadcasted_iota(jnp.int32, head.shape, 0)
    out = []
    for j in range(1, CONVK):
        rolled = pltpu.roll(du, LB - j, 0)
        bottom = jnp.where(row >= SUB - j, pltpu.roll(head, SUB - j, 0), rolled[LB - SUB:, :])
        out.append(jnp.concatenate([rolled[:LB - SUB, :], bottom], axis=0))
    return out


def _split(v, terms):
    out = []
    for _ in range(terms - 1):
        t = v.astype(BF16)
        out.append(t)
        v = v - t.astype(F32)
    out.append(v.astype(BF16))
    return out


def _dot_sel(v, sel, terms):
    parts = [_dot(t, sel) for t in _split(v, terms)]
    return functools.reduce(lambda a, b: a + b, parts)


def _dot_nt_sel(v, sel, terms):
    parts = [_dot_nt(t, sel) for t in _split(v, terms)]
    return functools.reduce(lambda a, b: a + b, parts)


def _ssd_pre(xt_ref, xc_ref, cw_ref, cb_ref, dtr_ref, sp_ref, n):
    cur = xc_ref[...]
    tail = jnp.where(n > 0, xt_ref[...], 0.0)
    sh = _conv_shifts(tail, cur)
    u = cb_ref[...] + cw_ref[CONVK - 1:CONVK, :] * sh[0]
    for j in range(1, CONVK):
        u = u + cw_ref[CONVK - 1 - j:CONVK - j, :] * sh[j]
    dt = _softplus(dtr_ref[...] + sp_ref[0:1, :])
    acs = _cumsum_rows(dt * -jnp.exp(sp_ref[1:2, :]), False)
    return u, dt, acs


def _gated_norm_fwd(y, z, sgz, nw):
    yz = y * (z * sgz)
    parts = []
    for g in range(2):
        t = yz[:, g * 256:(g + 1) * 256]
        parts.append(t * lax.rsqrt(jnp.mean(t * t, axis=-1, keepdims=True) + EPS))
    return jnp.concatenate(parts, axis=1) * nw


HPG = 4
GW = HPG * HD


class _SsdChunk:
    def __init__(self, xc, dt, acs, spv, e64, e128, decay=None):
        self.e64, self.e128 = e64, e128
        alast = acs[LB - 1:LB, :]
        self.e_all = jnp.exp(acs)
        self.dte_all = jnp.exp(alast - acs)
        self.elast = jnp.exp(alast)
        wide = _dot_sel(jnp.concatenate([dt, self.e_all, self.dte_all], axis=0), e64, 2)
        self.dt_x, self.e_x, self.dte_x = wide[0:LB], wide[LB:2 * LB], wide[2 * LB:3 * LB]
        self.dsk_x = _dot_sel(spv, e64, 3)[2:3, :]
        if decay is None:
            acs_t = jnp.transpose(acs)
            ac_x = _dot_sel(acs, e128, 3)
            row = lax.broadcasted_iota(jnp.int32, (HPG * LB, LB), 0)
            col = lax.broadcasted_iota(jnp.int32, (HPG * LB, LB), 1)
            causal = (row % LB) >= col
        lane = lax.broadcasted_iota(jnp.int32, (LB, GW), 1)
        self.head_lanes = [(lane >= hh * HD) & (lane < (hh + 1) * HD) for hh in range(HPG)]
        self.xs, self.xdt, self.b, self.c, self.bb, self.cb16, self.cbm, self.dm_st, self.m_st = ([] for _ in range(9))
        for g in range(2):
            heads = range(HPG * g, HPG * (g + 1))
            if decay is None:
                ac_st = jnp.concatenate([ac_x[:, j * LB:(j + 1) * LB] for j in heads], axis=0)
                ar_st = jnp.concatenate([jnp.broadcast_to(acs_t[j:j + 1, :], (LB, LB)) for j in heads], axis=0)
                dm_st = jnp.exp(jnp.where(causal, ac_st - ar_st, NEG))
            else:
                dm_st = decay[g]
            bg = xc[:, SW + g * NST:SW + (g + 1) * NST]
            cg = xc[:, SW + 2 * NST + g * NST:SW + 2 * NST + (g + 1) * NST]
            bgb, cgb = bg.astype(BF16), cg.astype(BF16)
            cbm = _dot_nt(cgb, bgb)
            xs_g = xc[:, g * GW:(g + 1) * GW]
            self.xs.append(xs_g)
            self.xdt.append(xs_g * self.dt_x[:, g * GW:(g + 1) * GW])
            self.b.append(bg)
            self.c.append(cg)
            self.bb.append(bgb)
            self.cb16.append(cgb)
            self.cbm.append(cbm)
            self.dm_st.append(dm_st)
            self.m_st.append(jnp.concatenate([cbm] * HPG, axis=0) * dm_st)

    def elast_rows(self, g):
        return jnp.concatenate([jnp.broadcast_to(self.elast[:, j:j + 1], (HD, NST))
                                for j in range(HPG * g, HPG * (g + 1))], axis=0)

    def diag_blocks(self, stacked):
        out = stacked[(HPG - 1) * LB:HPG * LB, :]
        for hh in range(HPG - 2, -1, -1):
            out = jnp.where(self.head_lanes[hh], stacked[hh * LB:(hh + 1) * LB, :], out)
        return out

    def block_diag(self, v):
        return jnp.concatenate([jnp.where(self.head_lanes[hh], v, 0.0) for hh in range(HPG)], axis=0)


def _ssd_fwd_block(n, xt_ref, xc_ref, cw_ref, cb_ref, dtr_ref, sp_ref, z_ref, nw_ref, e64_ref, e128_ref,
                   yn_ref, y_ref, hs_ref, u_ref, dt_ref, acs_ref, dm_ref, h_scr):
    @pl.when(n == 0)
    def _():
        h_scr[...] = jnp.zeros_like(h_scr)

    h_all = h_scr[...]
    hs_ref[0] = h_all
    u, dt, acs = _ssd_pre(xt_ref, xc_ref, cw_ref, cb_ref, dtr_ref, sp_ref, n)
    u_ref[...] = u
    dt_ref[...] = dt
    acs_ref[...] = acs
    xc = u * _sigmoid(u)
    ck = _SsdChunk(xc, dt, acs, sp_ref[...], e64_ref[...], e128_ref[...])
    dm_ref[0] = jnp.concatenate(ck.dm_st, axis=0)
    ys, hn = [], []
    for g in range(2):
        gl = slice(g * GW, (g + 1) * GW)
        xdt = ck.xdt[g]
        hg = h_all[gl, :]
        y_diag = ck.diag_blocks(_dot(ck.m_st[g].astype(BF16), xdt.astype(BF16)))
        y_off = ck.e_x[:, gl] * _dot_nt(ck.cb16[g], hg.astype(BF16))
        ys.append(y_diag + y_off + ck.xs[g] * ck.dsk_x[:, gl])
        hn.append(hg * ck.elast_rows(g) + _dot_tn((xdt * ck.dte_x[:, gl]).astype(BF16), ck.bb[g]))
    h_scr[...] = jnp.concatenate(hn, axis=0)
    y = jnp.concatenate(ys, axis=1)
    y_ref[...] = y
    z = z_ref[...]
    yn_ref[...] = _gated_norm_fwd(y, z, _sigmoid(z), nw_ref[...]).astype(BF16)


def _mixer_fwd(q, k, v, sinks8, xbc, conv_w8, conv_b, dtr, ssm_p, z, nw, gathers):
    s = q.shape[0]
    nb = s // LB
    bps = _blocks_per_step(nb)
    nsteps = nb // bps
    tl = bps * LB
    cur = lambda n: (n, 0)
    prev = lambda n: (jnp.maximum(n * bps - 1, 0), 0)
    items, ex_shapes, n_g = _exchange_items(gathers, [])
    ne = len(items)

    n_in, n_out = 16, 10
    relay_step = (3 * (nsteps - 1)) // 4
    stack = pl.BlockSpec((bps, NH * LB, LB), lambda n: (n, 0, 0))
    e64, e128 = _head_expanders()

    def body(*refs):
        (q_ref, kp_ref, kc_ref, vp_ref, vc_ref, sink_ref, xt_ref, xc_ref, cw_ref, cb_ref, dtr_ref, sp_ref, z_ref,
         nw_ref, e64_ref, e128_ref) = refs[:n_in]
        ex_in = refs[n_in:n_in + ne]
        (o_ref, yn_ref, y_ref, hs_ref, u_ref, dt_ref, acs_ref, dm_ref, pr_ref,
         ps_ref) = refs[n_in + ne:n_in + n_out + ne]
        ex_out = refs[n_in + n_out + ne:n_in + n_out + 2 * ne]
        h_scr = refs[n_in + n_out + 2 * ne]
        sems = refs[n_in + n_out + 1 + 2 * ne:]
        n = pl.program_id(0)

        @pl.when(n == 0)
        def _():
            _Exchange(n_g, ex_in, ex_out, sems).two_level_start()

        for sub in range(bps):
            blk = n * bps + sub
            r = slice(sub * LB, (sub + 1) * LB)
            before = slice((sub - 1) * LB, sub * LB)
            one = slice(sub, sub + 1)
            _attn_fwd_block(blk, q_ref.at[r], kp_ref if sub == 0 else kc_ref.at[before], kc_ref.at[r],
                            vp_ref if sub == 0 else vc_ref.at[before], vc_ref.at[r], sink_ref,
                            o_ref.at[r], pr_ref.at[one], ps_ref.at[one])
            _ssd_fwd_block(blk, xt_ref if sub == 0 else xc_ref.at[sub * LB - SUB:sub * LB], xc_ref.at[r], cw_ref,
                           cb_ref, dtr_ref.at[r], sp_ref, z_ref.at[r], nw_ref, e64_ref, e128_ref,
                           yn_ref.at[r], y_ref.at[r], hs_ref.at[one], u_ref.at[r], dt_ref.at[r], acs_ref.at[r],
                           dm_ref.at[one], h_scr)

        @pl.when(n == relay_step)
        def _():
            _Exchange(n_g, ex_in, ex_out, sems).two_level_relay()

        @pl.when(n == nsteps - 1)
        def _():
            _Exchange(n_g, ex_in, ex_out, sems).two_level_finish()

    any_spec = pl.BlockSpec(memory_space=pl.ANY)
    tail = pl.BlockSpec((SUB, D), lambda n: (jnp.maximum(n * (tl // SUB) - 1, 0), 0))
    outs = _pcall(
        body, name="mixer_fwd", grid=(nsteps,),
        in_specs=[pl.BlockSpec((tl, AW), cur), pl.BlockSpec((LB, KVW), prev), pl.BlockSpec((tl, KVW), cur),
                  pl.BlockSpec((LB, KVW), prev), pl.BlockSpec((tl, KVW), cur), _const((8, LB)),
                  tail, pl.BlockSpec((tl, D), cur), _const((8, D)), _const((1, D)),
                  pl.BlockSpec((tl, LB), cur), _const((8, LB)), pl.BlockSpec((tl, SW), cur), _const((1, SW)),
                  _const(e64.shape), _const(e128.shape)]
        + [any_spec] * ne,
        out_specs=[pl.BlockSpec((tl, AW), cur), pl.BlockSpec((tl, SW), cur), pl.BlockSpec((tl, SW), cur),
                   pl.BlockSpec((bps, NH * HD, NST), lambda n: (n, 0, 0)), pl.BlockSpec((tl, D), cur),
                   pl.BlockSpec((tl, LB), cur), pl.BlockSpec((tl, LB), cur),
                   stack, stack, stack] + [any_spec] * ne,
        out_shape=[_sds((s, AW), BF16), _sds((s, SW), BF16), _sds((s, SW), F32), _sds((nb, NH * HD, NST), F32),
                   _sds((s, D), F32), _sds((s, LB), F32), _sds((s, LB), F32), _sds((nb, NH * LB, LB), F32),
                   _sds((nb, NH * LB, LB), BF16), _sds((nb, NH * LB, LB), BF16)]
        + ex_shapes,
        scratch_shapes=[pltpu.VMEM((NH * HD, NST), F32)] + _exchange_sems(ne),
        compiler_params=_params(),
    )(q, k, k, v, v, sinks8, xbc, xbc, conv_w8, conv_b, dtr, ssm_p, z, nw, e64, e128, *items)
    return outs[:n_out], outs[n_out:]


def _blocks_per_step(nb):
    return next(b for b in (4, 2, 1) if nb % b == 0)


def _head_expanders():
    j = np.arange(LB)[:, None]
    e64 = (np.arange(NH * HD)[None, :] // HD == j).astype(BF16)
    e128 = (np.arange(NH * LB)[None, :] // LB == j).astype(BF16)
    return jnp.asarray(e64), jnp.asarray(e128)


def _outproj_ffn_fwd_loss(attn, yn, x, tgt, mod8, n2w, fnw, w_out, w_gu_t, w_down):
    s = x.shape[0]
    tf = min(256, s)

    def body(a_ref, y_ref, x_ref, t_ref, mod_ref, nw_ref, fw_ref, wo_ref, wgu_ref, wd_ref,
             x2_ref, h2_ref, mo_ref, mix_ref, gu_ref, act_ref, dx3_ref, sm_ref):
        i = pl.program_id(0)

        @pl.when(i == 0)
        def _():
            sm_ref[...] = jnp.zeros_like(sm_ref)

        mix = jnp.concatenate([a_ref[...], y_ref[...]], axis=1)
        mix_ref[...] = mix
        mo = _dot(mix, wo_ref[...])
        mo_ref[...] = mo.astype(BF16)
        x2 = x_ref[...] + mod_ref[2:3, :] * mo
        x2_ref[...] = x2
        h2 = _norm_mod_fwd(x2, nw_ref[...], mod_ref[3:4, :], mod_ref[4:5, :]).astype(BF16)
        h2_ref[...] = h2
        gu = _dot_nt(h2, wgu_ref[...])
        gu_ref[...] = gu.astype(BF16)
        g = gu[:, :DFF]
        act = (g * _sigmoid(g) * gu[:, DFF:]).astype(BF16)
        act_ref[...] = act
        ff = _dot(act, wd_ref[...])
        x3 = x2 + mod_ref[5:6, :] * ff
        r = lax.rsqrt(jnp.mean(x3 * x3, axis=-1, keepdims=True) + EPS)
        xh = x3 * r
        fw = fw_ref[...]
        err = xh * fw - t_ref[...]
        dy = err * (1.0 / D)
        dxh = dy * fw
        dx3 = r * (dxh - xh * jnp.mean(dxh * xh, axis=-1, keepdims=True))
        dx3_ref[...] = dx3
        sm_ref[0:1, :] += jnp.sum(dx3 * ff, axis=0, keepdims=True)
        sm_ref[1:2, :] += jnp.sum(dy * xh, axis=0, keepdims=True)
        sm_ref[2:3, :] += jnp.sum(err * err, axis=0, keepdims=True)

    return _pcall(
        body, name="outproj_ffn_fwd_loss", grid=(s // tf,),
        in_specs=[_rows(tf, AW), _rows(tf, SW), _rows(tf, D), _rows(tf, D), _const((8, D)), _const((1, D)),
                  _const((1, D)), _const((D, D)), _const((2 * DFF, D)), _const((DFF, D))],
        out_specs=[_rows(tf, D), _rows(tf, D), _rows(tf, D), _rows(tf, D), _rows(tf, 2 * DFF), _rows(tf, DFF),
                   _rows(tf, D), pl.BlockSpec((8, D), lambda i: (0, 0))],
        out_shape=[_sds((s, D), F32), _sds((s, D), BF16), _sds((s, D), BF16), _sds((s, D), BF16),
                   _sds((s, 2 * DFF), BF16), _sds((s, DFF), BF16), _sds((s, D), F32), _sds((8, D), F32)],
        compiler_params=_params(),
    )(attn, yn, x, tgt, mod8, n2w, fnw, w_out, w_gu_t, w_down)


def _ffn_bwd(dx3, gu, x2, mixout, mod8, n2w, w_gu, w_down, w_out):
    s = x2.shape[0]
    tb = min(256, s)

    def body(dx3_ref, gu_ref, x2_ref, mo_ref, mod_ref, nw_ref, wgu_ref, wd_ref, wo_ref,
             dx2_ref, dff_ref, dgu_ref, dmix_ref, dattn_ref, dyn_ref, sm_ref):
        i = pl.program_id(0)

        @pl.when(i == 0)
        def _():
            sm_ref[...] = jnp.zeros_like(sm_ref)

        dx3 = dx3_ref[...]
        dff = (dx3 * mod_ref[5:6, :]).astype(BF16)
        dff_ref[...] = dff
        dact = _dot_nt(dff, wd_ref[...])
        g = gu_ref[:, :DFF].astype(F32)
        u = gu_ref[:, DFF:].astype(F32)
        sg = _sigmoid(g)
        dgu = jnp.concatenate([dact * u * sg * (1.0 + g * (1.0 - sg)), dact * g * sg], axis=1).astype(BF16)
        dgu_ref[...] = dgu
        dh2 = _dot(dgu, wgu_ref[...])
        dxn, d_shift, d_scale, d_w = _norm_mod_bwd(x2_ref[...], dh2, nw_ref[...], mod_ref[4:5, :])
        dx2 = dx3 + dxn
        dx2_ref[...] = dx2
        sm_ref[0:1, :] += d_shift
        sm_ref[1:2, :] += d_scale
        sm_ref[2:3, :] += d_w
        sm_ref[3:4, :] += jnp.sum(dx2 * mo_ref[...].astype(F32), axis=0, keepdims=True)
        dmix = (dx2 * mod_ref[2:3, :]).astype(BF16)
        dmix_ref[...] = dmix
        dmi = _dot_nt(dmix, wo_ref[...])
        dattn_ref[...] = dmi[:, :AW].astype(BF16)
        dyn_ref[...] = dmi[:, AW:]

    return _pcall(
        body, name="ffn_bwd", grid=(s // tb,),
        in_specs=[_rows(tb, D), _rows(tb, 2 * DFF), _rows(tb, D), _rows(tb, D), _const((8, D)), _const((1, D)),
                  _const((2 * DFF, D)), _const((DFF, D)), _const((D, D))],
        out_specs=[_rows(tb, D), _rows(tb, D), _rows(tb, 2 * DFF), _rows(tb, D), _rows(tb, AW), _rows(tb, SW),
                   pl.BlockSpec((8, D), lambda i: (0, 0))],
        out_shape=[_sds((s, D), F32), _sds((s, D), BF16), _sds((s, 2 * DFF), BF16), _sds((s, D), BF16),
                   _sds((s, AW), BF16), _sds((s, SW), F32), _sds((8, D), F32)],
        compiler_params=_params(),
    )(dx3, gu, x2, mixout, mod8, n2w, w_gu, w_down, w_out)


def _ssd_bwd_block(i, *refs):
    def run(dyn_ref, y_ref, z_ref, x_ref, u_ref, dt_ref, acs_ref, dm_ref, cw_ref, sp_ref, nw_ref,
            hs_ref, e64_ref, e128_ref, dzxd_ref, sm_ref, dh_scr, dun_scr):
        @pl.when(i == 0)
        def _():
            dh_scr[...] = jnp.zeros_like(dh_scr)
            dun_scr[...] = jnp.zeros_like(dun_scr)
            sm_ref[...] = jnp.zeros_like(sm_ref)

        u, dt, acs = u_ref[...], dt_ref[...], acs_ref[...]
        sg_u = _sigmoid(u)
        xc = u * sg_u
        a_neg = -jnp.exp(sp_ref[1:2, :])
        ck = _SsdChunk(xc, dt, acs, sp_ref[...], e64_ref[...], e128_ref[...],
                       decay=[dm_ref[0, g * HPG * LB:(g + 1) * HPG * LB, :] for g in range(2)])
        h_all = hs_ref[0]
        dh_all = dh_scr[...]
        riota = lax.broadcasted_iota(jnp.int32, (LB, LB), 0)
        lane1 = lax.broadcasted_iota(jnp.int32, (1, LB), 1)

        z = z_ref[...]
        y = y_ref[...]
        sgz = _sigmoid(z)
        sz = z * sgz
        yz = y * sz
        nwv = nw_ref[...]
        dyn_v = dyn_ref[...]
        dyhat = dyn_v * nwv
        yhat_parts, dyz_parts = [], []
        for g in range(2):
            gs = slice(g * 256, (g + 1) * 256)
            t = yz[:, gs]
            rg = lax.rsqrt(jnp.mean(t * t, axis=-1, keepdims=True) + EPS)
            yh = t * rg
            dyh = dyhat[:, gs]
            yhat_parts.append(yh)
            dyz_parts.append(rg * (dyh - yh * jnp.mean(dyh * yh, axis=-1, keepdims=True)))
        yhat = jnp.concatenate(yhat_parts, axis=1)
        dyz = jnp.concatenate(dyz_parts, axis=1)
        sm_ref[5:6, 0:SW] += jnp.sum(dyn_v * yhat, axis=0, keepdims=True)
        dy = dyz * sz
        dzxd_ref[:, 0:SW] = (dyz * y * sgz * (1.0 + z * (1.0 - sgz))).astype(BF16)

        cat = lambda parts: jnp.concatenate(parts, axis=1)
        dxs, dbs, dcs, dhp, g_cat, de_x, ddte_x, ddt_x, ddsk_x = ([] for _ in range(9))
        dacs_t = jnp.zeros((LB, LB), F32)
        hsum = jnp.zeros((1, LB), F32)
        for g in range(2):
            gl = slice(g * GW, (g + 1) * GW)
            xs_g, xdt, bgb, cgb = ck.xs[g], ck.xdt[g], ck.bb[g], ck.cb16[g]
            m_st, dm_st = ck.m_st[g], ck.dm_st[g]
            dt_x, e_x, dte_x = ck.dt_x[:, gl], ck.e_x[:, gl], ck.dte_x[:, gl]
            xdtb = xdt.astype(BF16)
            hg, dhn = h_all[gl, :], dh_all[gl, :]
            hb, dhnb = hg.astype(BF16), dhn.astype(BF16)
            dy_g = dy[:, gl]
            ddsk_x.append(jnp.sum(dy_g * xs_g, axis=0, keepdims=True))
            dy_bd = ck.block_diag(dy_g).astype(BF16)
            dm4 = _dot_nt(dy_bd, xdtb)
            dxdt = _dot_tn(m_st.astype(BF16), dy_bd)
            gmat = dm4 * m_st
            dcbm = dm4 * dm_st
            dcb = dcbm[0:LB] + dcbm[LB:2 * LB] + dcbm[2 * LB:3 * LB] + dcbm[3 * LB:4 * LB]
            g_cat.append(cat([gmat[hh * LB:(hh + 1) * LB, :] for hh in range(HPG)]))
            for hh in range(HPG):
                j = HPG * g + hh
                col_sum = jnp.sum(gmat[hh * LB:(hh + 1) * LB, :], axis=0, keepdims=True)
                dacs_t = dacs_t - jnp.where(riota == j, col_sum, 0.0)
                hsl = slice(hh * HD, (hh + 1) * HD)
                hsum = hsum + jnp.where(lane1 == j, jnp.sum(dhn[hsl, :] * hg[hsl, :]), 0.0)
            dchb = (dy_g * e_x).astype(BF16)
            dcg = _dot(dchb, hb)
            dh_prev = _dot_tn(dchb, cgb)
            de_x.append(dy_g * _dot_nt(cgb, hb))
            dxs_s = _dot_nt(bgb, dhnb)
            dbg = _dot((xdt * dte_x).astype(BF16), dhnb)
            dxdt = dxdt + dxs_s * dte_x
            ddte_x.append(dxs_s * xdt)
            dhp.append(dhn * ck.elast_rows(g) + dh_prev)
            dxs.append(dy_g * ck.dsk_x[:, gl] + dxdt * dt_x)
            ddt_x.append(dxdt * xs_g)
            dcbb = dcb.astype(BF16)
            dbs.append(dbg + _dot_tn(dcbb, cgb))
            dcs.append(dcg + _dot(dcbb, bgb))
        dh_scr[...] = jnp.concatenate(dhp, axis=0)
        red = _dot_nt_sel(jnp.concatenate([cat(de_x), cat(ddte_x), cat(ddt_x)], axis=0), ck.e64, 1)
        de_c, ddte_c, ddt_c = red[0:LB], red[LB:2 * LB], red[2 * LB:3 * LB]
        ddsk = _dot_nt_sel(jnp.broadcast_to(cat(ddsk_x), (SUB, NH * HD)), ck.e64, 2)[0:1, :]
        t1 = ddte_c * ck.dte_all
        dalast = jnp.sum(t1, axis=0, keepdims=True) + hsum * ck.elast
        dacs = (_dot_nt_sel(cat(g_cat), ck.e128, 2) + de_c * ck.e_all - t1 + jnp.transpose(dacs_t)
                + jnp.where(riota == LB - 1, dalast, 0.0))
        da = _cumsum_rows(dacs, True)
        ddt = ddt_c + da * a_neg
        da_log = jnp.sum(da * dt, axis=0, keepdims=True) * a_neg
        ddtr = ddt * (1.0 - jnp.exp(-dt))
        dzxd_ref[:, SW + D:ZXD] = ddtr.astype(BF16)
        sm_ref[6:7, 0:LB] += jnp.sum(ddtr, axis=0, keepdims=True)
        sm_ref[6:7, LB:2 * LB] += da_log
        sm_ref[6:7, 2 * LB:3 * LB] += ddsk

        du = cat(dxs + dbs + dcs) * (sg_u * (1.0 + u * (1.0 - sg_u)))
        xv = x_ref[...]
        adv = [du] + _conv_advances(du, dun_scr[...])
        sm_ref[0:1, :] += jnp.sum(du, axis=0, keepdims=True)
        dxbc = cw_ref[CONVK - 1:CONVK, :] * du
        for j in range(CONVK):
            sm_ref[CONVK - j:CONVK + 1 - j, :] += jnp.sum(adv[j] * xv, axis=0, keepdims=True)
            if j:
                dxbc = dxbc + cw_ref[CONVK - 1 - j:CONVK - j, :] * adv[j]
        dun_scr[...] = du[0:SUB, :]
        dzxd_ref[:, SW:SW + D] = dxbc.astype(BF16)

    run(*refs)


def _attn_bwd_block(i, q_ref, kp_ref, kc_ref, vp_ref, vc_ref, o_ref, do_ref, cos_ref, sin_ref, pr_ref, ps_ref,
                    dq_ref, dkv_ref, ds_ref, ck_scr, cv_scr):
    @pl.when(i == 0)
    def _():
        ds_ref[...] = jnp.zeros_like(ds_ref)
        ck_scr[...] = jnp.zeros_like(ck_scr)
        cv_scr[...] = jnp.zeros_like(cv_scr)

    qv, ov, dov = q_ref[...], o_ref[...], do_ref[...]
    kcat = jnp.concatenate([kp_ref[...], kc_ref[...]], axis=0)
    vcat = jnp.concatenate([vp_ref[...], vc_ref[...]], axis=0)
    upper = _upper_mask()
    srow = lax.broadcasted_iota(jnp.int32, (8, LB), 0)
    slane = lax.broadcasted_iota(jnp.int32, (8, LB), 1)
    dsink = jnp.zeros((8, LB), F32)
    dq_g, dk_g, dv_g = [], [], []
    for g in range(NQ // QPG):
        sl = slice(g * HD, (g + 1) * HD)
        qg = _stack_heads(qv, g)
        dog = _stack_heads(dov, g)
        rows = slice(g * QPG * LB, (g + 1) * QPG * LB)
        probs = pr_ref[0, rows, :].astype(F32)
        psink = ps_ref[0, rows, :].astype(F32)
        delta = _row_sums_wide(dog.astype(F32) * _stack_heads(ov, g).astype(F32), 2)
        dsc = probs * (_band(upper, _dot_nt(dog, vcat[0:LB, sl]), _dot_nt(dog, vcat[LB:2 * LB, sl])) - delta)
        sink_terms = (psink * delta)[:, 0:1]
        for hh in range(QPG):
            dsink = dsink - jnp.where((srow == QPG * g + hh) & (slane == 0),
                                      jnp.sum(sink_terms[hh * LB:(hh + 1) * LB, :]), 0.0)
        ds_p = jnp.where(upper, dsc, 0.0).astype(BF16)
        ds_c = jnp.where(upper, 0.0, dsc).astype(BF16)
        dq_g.append((_dot(ds_p, kcat[0:LB, sl]) + _dot(ds_c, kcat[LB:2 * LB, sl])) * ATT_SCALE)
        dk_g.append(jnp.concatenate([_dot_tn(ds_p, qg), _dot_tn(ds_c, qg)], axis=0) * ATT_SCALE)
        dv_g.append(jnp.concatenate([_dot_tn(jnp.where(upper, probs, 0.0).astype(BF16), dog),
                                     _dot_tn(jnp.where(upper, 0.0, probs).astype(BF16), dog)], axis=0))
    ds_ref[...] += dsink
    cs = cos_ref[...]
    sn = sin_ref[...]
    dk2 = jnp.concatenate(dk_g, axis=1)
    dv2 = jnp.concatenate(dv_g, axis=1)
    for a, tile in enumerate(_unstack_heads(dq_g)):
        dq_ref[:, a * LB:(a + 1) * LB] = _rope(tile, cs, sn, True).astype(BF16)
    dkv_ref[:, 0:KVW] = _rope(ck_scr[...] + dk2[LB:2 * LB, :], cs, sn, True).astype(BF16)
    dkv_ref[:, KVW:2 * KVW] = (cv_scr[...] + dv2[LB:2 * LB, :]).astype(BF16)
    ck_scr[...] = dk2[0:LB, :]
    cv_scr[...] = dv2[0:LB, :]


def _mixer_bwd(q, k, v, o, do, cos, sin, probs, psink, dyn, y, z, xbc, u, dtv, acs, decay, conv_w8, ssm_p, nw, hs,
               scatters):
    s = q.shape[0]
    nb = s // LB
    bps = _blocks_per_step(nb)
    nsteps = nb // bps
    tl = bps * LB
    cur = lambda i: (nsteps - 1 - i, 0)
    prev = lambda i: (jnp.maximum((nsteps - 1 - i) * bps - 1, 0), 0)
    n_in = 25
    items, ex_shapes, n_g = _exchange_items([], scatters)
    ne = len(items)
    e64, e128 = _head_expanders()
    stack = pl.BlockSpec((bps, NH * LB, LB), lambda i: (nsteps - 1 - i, 0, 0))

    def body(*refs):
        i = pl.program_id(0)
        (q_ref, kp_ref, kc_ref, vp_ref, vc_ref, o_ref, do_ref, cos_ref, sin_ref, pr_ref, ps_ref,
         dyn_ref, y_ref, z_ref, x_ref, u_ref, dt_ref, acs_ref, dm_ref, cw_ref, sp_ref, nw_ref,
         hs_ref, e64_ref, e128_ref) = refs[:n_in]
        ex_in = refs[n_in:n_in + ne]
        dp_ref, ds_ref, sm_ref = refs[n_in + ne:n_in + ne + 3]
        ex_out = refs[n_in + ne + 3:n_in + 2 * ne + 3]
        ck_scr, cv_scr, dh_scr, dun_scr = refs[n_in + 2 * ne + 3:n_in + 2 * ne + 7]
        sems = refs[n_in + 2 * ne + 7:]

        @pl.when(i == 0)
        def _():
            _Exchange(n_g, ex_in, ex_out, sems).start()

        for back in range(bps):
            sub = bps - 1 - back
            step = i * bps + back
            r = slice(sub * LB, (sub + 1) * LB)
            before = slice((sub - 1) * LB, sub * LB)
            one = slice(sub, sub + 1)
            _attn_bwd_block(step, q_ref.at[r], kp_ref if sub == 0 else kc_ref.at[before], kc_ref.at[r],
                            vp_ref if sub == 0 else vc_ref.at[before], vc_ref.at[r], o_ref.at[r], do_ref.at[r],
                            cos_ref.at[r], sin_ref.at[r], pr_ref.at[one], ps_ref.at[one],
                            dp_ref.at[r, O_Q:O_K], dp_ref.at[r, O_K:O_Z], ds_ref, ck_scr, cv_scr)
            _ssd_bwd_block(step, dyn_ref.at[r], y_ref.at[r], z_ref.at[r], x_ref.at[r], u_ref.at[r], dt_ref.at[r],
                           acs_ref.at[r], dm_ref.at[one], cw_ref, sp_ref, nw_ref,
                           hs_ref.at[one], e64_ref, e128_ref, dp_ref.at[r, O_Z:INP], sm_ref, dh_scr, dun_scr)

        @pl.when(i == nsteps - 1)
        def _():
            _Exchange(n_g, ex_in, ex_out, sems).finish()

    any_spec = pl.BlockSpec(memory_space=pl.ANY)
    outs = _pcall(
        body, name="mixer_bwd", grid=(nsteps,),
        in_specs=[pl.BlockSpec((tl, AW), cur), pl.BlockSpec((LB, KVW), prev), pl.BlockSpec((tl, KVW), cur),
                  pl.BlockSpec((LB, KVW), prev), pl.BlockSpec((tl, KVW), cur), pl.BlockSpec((tl, AW), cur),
                  pl.BlockSpec((tl, AW), cur), pl.BlockSpec((tl, LB), cur), pl.BlockSpec((tl, LB), cur),
                  stack, stack,
                  pl.BlockSpec((tl, SW), cur), pl.BlockSpec((tl, SW), cur), pl.BlockSpec((tl, SW), cur),
                  pl.BlockSpec((tl, D), cur), pl.BlockSpec((tl, D), cur), pl.BlockSpec((tl, LB), cur),
                  pl.BlockSpec((tl, LB), cur), stack,
                  _const((8, D)), _const((8, LB)), _const((1, SW)),
                  pl.BlockSpec((bps, NH * HD, NST), lambda i: (nsteps - 1 - i, 0, 0)),
                  _const(e64.shape), _const(e128.shape)] + [any_spec] * ne,
        out_specs=[pl.BlockSpec((tl, INP), cur), pl.BlockSpec((8, LB), lambda i: (0, 0)),
                   pl.BlockSpec((8, D), lambda i: (0, 0))] + [any_spec] * ne,
        out_shape=[_sds((s, INP), BF16), _sds((8, LB), F32), _sds((8, D), F32)] + ex_shapes,
        scratch_shapes=[pltpu.VMEM((LB, KVW), F32), pltpu.VMEM((LB, KVW), F32),
                        pltpu.VMEM((NH * HD, NST), F32), pltpu.VMEM((SUB, D), F32)]
        + _exchange_sems(ne),
        compiler_params=_params(),
    )(q, k, k, v, v, o, do, cos, sin, probs, psink, dyn, y, z, xbc, u, dtv, acs, decay, conv_w8, ssm_p, nw, hs,
      e64, e128, *items)
    return outs[0], outs[1], outs[2], outs[3:]


def _inproj_bwd(dproj, x, dx2, mod8, n1w, w_in_t, scatters, smalls):
    s = x.shape[0]
    tt = min(512, s)
    nt = s // tt
    items, ex_shapes, n_g = _exchange_items([], scatters)
    ne = len(items)
    n_in = 10

    def body(*refs):
        dp_ref, x_ref, dx2_ref, mod_ref, nw_ref, w_ref, f_ref, b_ref, s_ref, k_ref = refs[:n_in]
        ex_in = refs[n_in:n_in + ne]
        gx_ref, sm_ref = refs[n_in + ne:n_in + 2 + ne]
        ex_out = refs[n_in + 2 + ne:n_in + 2 + 2 * ne]
        gpack_ref = refs[n_in + 2 + 2 * ne]
        pack_scr = refs[n_in + 3 + 2 * ne]
        sems = refs[n_in + 4 + 2 * ne:n_in + 7 + 2 * ne]
        pack_sems = refs[n_in + 7 + 2 * ne:]
        i = pl.program_id(0)

        @pl.when(i == 0)
        def _():
            sm_ref[...] = jnp.zeros_like(sm_ref)
            _Exchange(n_g, ex_in, ex_out, sems).start()

        w = w_ref[...]
        hr = tt // 2
        dh1 = [_dot(dp_ref[h * hr:(h + 1) * hr, :], w) for h in range(2)]
        sums = jnp.zeros((3, D), F32)
        for h in range(2):
            rows = slice(h * hr, (h + 1) * hr)
            dxn, d_shift, d_scale, d_w = _norm_mod_bwd(x_ref[rows, :], dh1[h], nw_ref[...], mod_ref[1:2, :])
            gx_ref[rows, :] = dx2_ref[rows, :] + dxn
            sums = sums + jnp.concatenate([d_shift, d_scale, d_w], axis=0)
        sm_ref[0:3, :] += sums

        @pl.when(i == nt - 1)
        def _():
            _pack_rows(f_ref, b_ref, s_ref, sm_ref, k_ref, pack_scr)
            small = _Exchange(1, [pack_scr], [gpack_ref], pack_sems)
            small.start()
            _Exchange(n_g, ex_in, ex_out, sems).finish()
            small.finish()

    any_spec = pl.BlockSpec(memory_space=pl.ANY)
    outs = _pcall(
        body, name="inproj_bwd", grid=(nt,),
        in_specs=[_rows(tt, INP), _rows(tt, D), _rows(tt, D), _const((8, D)), _const((1, D)), _const((INP, D)),
                  _const((8, D)), _const((8, D)), _const((8, D)), _const((8, LB))]
        + [any_spec] * ne,
        out_specs=[_rows(tt, D), pl.BlockSpec((8, D), lambda i: (0, 0))] + [any_spec] * (ne + 1),
        out_shape=[_sds((s, D), F32), _sds((8, D), F32)] + ex_shapes + [_sds((N_DEV, PACK_ROWS, D), F32)],
        scratch_shapes=[pltpu.VMEM((PACK_ROWS, D), F32)] + _exchange_sems(ne) + _exchange_sems(1),
        compiler_params=_params(),
    )(dproj, x, dx2, mod8, n1w, w_in_t, *smalls, *items)
    return outs[0], outs[2:2 + ne], outs[2 + ne]


def _wgrad(a, b, name):
    s, m = a.shape
    n = b.shape[1]
    tk = min(2048, s)
    wide = (1408, 1024, 512)
    tm = next((t for t in wide if m % t == 0), m)
    tn = n if n <= 2048 else _largest_divisor(n, wide)
    nk = s // tk

    def body(a_ref, b_ref, o_ref, acc):
        kk = pl.program_id(2)

        @pl.when(kk == 0)
        def _():
            acc[...] = jnp.zeros_like(acc)

        acc[...] += _dot_tn(a_ref[...], b_ref[...])

        @pl.when(kk == nk - 1)
        def _():
            o_ref[...] = acc[...].astype(BF16)

    return _pcall(
        body, name=name, grid=(m // tm, n // tn, nk),
        in_specs=[pl.BlockSpec((tk, tm), lambda i, j, kk: (kk, i)), pl.BlockSpec((tk, tn), lambda i, j, kk: (kk, j))],
        out_specs=pl.BlockSpec((tm, tn), lambda i, j, kk: (i, j)),
        out_shape=_sds((m, n), BF16),
        scratch_shapes=[pltpu.VMEM((tm, tn), F32)],
        compiler_params=_params(3),
    )(a, b)


PACK_ROWS = 24


def _pack_rows(f_ref, b_ref, s_ref, i_ref, k_ref, o_ref):
    o_ref[...] = jnp.zeros_like(o_ref)
    o_ref[0:2, :] = i_ref[0:2, :]
    o_ref[2:3, :] = b_ref[3:4, :]
    o_ref[3:5, :] = b_ref[0:2, :]
    o_ref[5:6, :] = f_ref[0:1, :]
    o_ref[6:7, :] = i_ref[2:3, :]
    o_ref[7:8, :] = b_ref[2:3, :]
    o_ref[8:9, :] = f_ref[1:2, :]
    o_ref[9:14, :] = s_ref[0:5, :]
    o_ref[14:15, :] = s_ref[5:6, :]
    o_ref[15:16, 0:3 * LB] = s_ref[6:7, 0:3 * LB]
    lane = lax.broadcasted_iota(jnp.int32, (1, LB), 1)
    sk = jnp.zeros((1, LB), F32)
    for h in range(NQ):
        sk = sk + jnp.where(lane == h, k_ref[h:h + 1, 0:1], 0.0)
    o_ref[15:16, 3 * LB:4 * LB] = sk
    o_ref[16:17, :] = f_ref[2:3, :]


def _exchange_items(gathers, scatters):
    items = list(gathers) + list(scatters)
    shapes = [_sds((N_DEV,) + a.shape, a.dtype) for a in gathers] + [_sds(a.shape, a.dtype) for a in scatters]
    return items, shapes, len(gathers)


def _exchange_sems(n):
    return [pltpu.SemaphoreType.DMA((n, N_DEV - 1)), pltpu.SemaphoreType.DMA((n, N_DEV - 1)),
            pltpu.SemaphoreType.DMA((n,))]


class _Exchange:
    def __init__(self, n_g, ins, outs, sems):
        self.n_g, self.ins, self.outs = n_g, ins, outs
        self.send_sems, self.recv_sems, self.loc_sems = sems
        xi, yi, ci = lax.axis_index("x"), lax.axis_index("y"), lax.axis_index("c")
        self.me = 4 * xi + 2 * yi + ci
        self.peers = []
        for r in range(1, N_DEV):
            px = 1 - xi if r & 4 else xi
            py = 1 - yi if r & 2 else yi
            pc = 1 - ci if r & 1 else ci
            self.peers.append(((px, py, pc), 4 * px + 2 * py + pc))

    def _copy(self, t, r, landing):
        dev, peer = self.peers[r]
        src = self.ins[t] if t < self.n_g else self.ins[t].at[peer]
        return pltpu.make_async_remote_copy(
            src_ref=src, dst_ref=self.outs[t].at[landing], send_sem=self.send_sems.at[t, r],
            recv_sem=self.recv_sems.at[t, r], device_id=dev, device_id_type=pl.DeviceIdType.MESH)

    def _local(self, t):
        src = self.ins[t] if t < self.n_g else self.ins[t].at[self.me]
        return pltpu.make_async_copy(src, self.outs[t].at[self.me], self.loc_sems.at[t])

    def start(self):
        for t in range(len(self.ins)):
            self._local(t).start()
            for r in range(N_DEV - 1):
                self._copy(t, r, self.me).start()

    def finish(self):
        n = len(self.ins)
        for t in range(n):
            for r in range(N_DEV - 1):
                self._copy(t, r, self.peers[r][1]).wait_recv()
        for t in range(n):
            for r in range(N_DEV - 1):
                self._copy(t, r, self.me).wait_send()
            self._local(t).wait()

    DIRECT = (0, 1, 3, 5)

    def two_level_start(self):
        for t in range(len(self.ins)):
            self._local(t).start()
            for r in self.DIRECT:
                self._copy(t, r, self.me).start()

    def _relay(self, t, r):
        peer = self.peers[r][1]
        return pltpu.make_async_remote_copy(
            src_ref=self.outs[t].at[peer], dst_ref=self.outs[t].at[peer], send_sem=self.send_sems.at[t, r + 1],
            recv_sem=self.recv_sems.at[t, r + 1], device_id=self.peers[0][0], device_id_type=pl.DeviceIdType.MESH)

    def two_level_relay(self):
        for t in range(len(self.ins)):
            for r in self.DIRECT[1:]:
                self._copy(t, r, self.peers[r][1]).wait_recv()
                self._relay(t, r).start()

    def two_level_finish(self):
        n = len(self.ins)
        for t in range(n):
            for r in (0, 2, 4, 6):
                self._copy(t, r, self.peers[r][1]).wait_recv()
        for t in range(n):
            for r in self.DIRECT:
                self._copy(t, r, self.me).wait_send()
            for r in self.DIRECT[1:]:
                self._relay(t, r).wait_send()
            self._local(t).wait()


def _prologue(c, w_in_tb, conv_w, w_cols, b_ada):
    ncol = w_cols.shape[1]
    c8_shape, cw8_shape = (8, c.shape[1]), (8, conv_w.shape[1])

    def body(c_ref, win_ref, cw_ref, w_ref, b_ref, gc_ref, gin_ref, gcw_ref, gmod_ref, c8_scr, cw8_scr, call_scr,
             mod_scr, loc_sem, *sems):
        c8_scr[...] = jnp.zeros_like(c8_scr)
        c8_scr[0:1, :] = c_ref[...]
        cw8_scr[...] = jnp.zeros_like(cw8_scr)
        cw8_scr[0:CONVK, :] = cw_ref[...]
        big = _Exchange(2, [win_ref, cw8_scr], [gin_ref, gcw_ref], sems[0:3])
        small = _Exchange(1, [c8_scr], [gc_ref], sems[3:6])
        small.start()
        big.two_level_start()
        small.finish()
        landed = pltpu.make_async_copy(gc_ref, call_scr, loc_sem)
        landed.start()
        landed.wait()
        cv = call_scr[:, 0, :]
        sc = (cv * _sigmoid(cv)).astype(BF16)
        bias = b_ref[:, 0:ncol]
        for dev in range(1, N_DEV):
            bias = jnp.where(small.me == dev, b_ref[:, dev * ncol:(dev + 1) * ncol], bias)
        mod_scr[...] = _dot(sc, w_ref[...].astype(BF16)) + bias
        mods = _Exchange(1, [mod_scr], [gmod_ref], sems[6:9])
        mods.start()
        mods.finish()
        big.two_level_relay()
        big.two_level_finish()

    any_spec = pl.BlockSpec(memory_space=pl.ANY)
    vmem_spec = pl.BlockSpec(memory_space=pltpu.VMEM)
    return _pcall(
        body, name="prologue", in_specs=[vmem_spec, any_spec, vmem_spec, vmem_spec, vmem_spec],
        out_specs=[any_spec] * 4,
        out_shape=[_sds((N_DEV,) + c8_shape, F32), _sds((N_DEV,) + w_in_tb.shape, BF16),
                   _sds((N_DEV,) + cw8_shape, F32), _sds((N_DEV, N_DEV, ncol), F32)],
        scratch_shapes=[pltpu.VMEM(c8_shape, F32), pltpu.VMEM(cw8_shape, F32),
                        pltpu.VMEM((N_DEV,) + c8_shape, F32), pltpu.VMEM((N_DEV, ncol), F32),
                        pltpu.SemaphoreType.DMA] + _exchange_sems(2) + _exchange_sems(1) + _exchange_sems(1),
        compiler_params=pltpu.CompilerParams(vmem_limit_bytes=VMEM_LIMIT),
    )(c, w_in_tb, conv_w, w_cols, b_ada)


def _adamw(w, g, m, v):
    m2 = ADAM_B1 * m + (1.0 - ADAM_B1) * g
    v2 = ADAM_B2 * v + (1.0 - ADAM_B2) * (g * g)
    m_hat = m2 / (1.0 - ADAM_B1 ** ADAM_STEP)
    v_hat = v2 / (1.0 - ADAM_B2 ** ADAM_STEP)
    delta = -ADAM_LR * (m_hat / (jnp.sqrt(v_hat) + ADAM_EPS) + ADAM_WD * w)
    return delta, m2, v2


def _sum_adamw(parts, w, m, v, name):
    rws, cols = w.shape
    tr = next((t for t in (256, 176, 128) if rws % t == 0), rws)

    def body(p_ref, w_ref, m_ref, v_ref, g_ref, d_ref, mo_ref, vo_ref):
        g = p_ref[0].astype(F32)
        for dev in range(1, N_DEV):
            g = g + p_ref[dev].astype(F32)
        g_ref[...] = g
        d_ref[...], mo_ref[...], vo_ref[...] = _adamw(w_ref[...], g, m_ref[...], v_ref[...])

    blk = pl.BlockSpec((tr, cols), lambda i: (i, 0))
    return _pcall(
        body, name=name, grid=(rws // tr,),
        in_specs=[pl.BlockSpec((N_DEV, tr, cols), lambda i: (0, i, 0)), blk, blk, blk],
        out_specs=[blk] * 4, out_shape=[_sds((rws, cols), F32)] * 4, compiler_params=_params(),
    )(parts, w, m, v)


def _sum_adamw_rowwise(parts, w, m, v, name):
    rws, _, cols = w.shape

    def body(p_ref, w_ref, m_ref, v_ref, g_ref, d_ref, mo_ref, vo_ref, ins, outs, sems):
        loads = [pltpu.make_async_copy(src.at[:, 0, :], ins.at[i], sems.at[i])
                 for i, src in enumerate((w_ref, m_ref, v_ref))]
        for load in loads:
            load.start()
        g = p_ref[0].astype(F32)
        for dev in range(1, N_DEV):
            g = g + p_ref[dev].astype(F32)
        for load in loads:
            load.wait()
        outs[0] = g
        outs[1], outs[2], outs[3] = _adamw(ins[0], g, ins[1], ins[2])
        stores = [pltpu.make_async_copy(outs.at[i], dst.at[:, 0, :], sems.at[len(loads) + i])
                  for i, dst in enumerate((g_ref, d_ref, mo_ref, vo_ref))]
        for store in stores:
            store.start()
        for store in stores:
            store.wait()

    any_spec = pl.BlockSpec(memory_space=pl.ANY)
    return _pcall(
        body, name=name, grid=(1,),
        in_specs=[_const((N_DEV, rws, cols)), any_spec, any_spec, any_spec],
        out_specs=[any_spec] * 4, out_shape=[_sds((rws, 1, cols), F32)] * 4,
        scratch_shapes=[pltpu.VMEM((3, rws, cols), F32), pltpu.VMEM((4, rws, cols), F32),
                        pltpu.SemaphoreType.DMA((7,))],
        compiler_params=_params(),
    )(parts, w, m, v)


def _wada_adamw(c_all, dmod_cols, w, m, v):
    rws, cols = w.shape
    tr = 256

    def body(c_ref, dm_ref, w_ref, m_ref, v_ref, g_ref, d_ref, mo_ref, vo_ref):
        cv = c_ref[...]
        sc = (cv * _sigmoid(cv)).astype(BF16)
        g = _dot_tn(sc, dm_ref[...].astype(BF16))
        g_ref[...] = g
        d_ref[...], mo_ref[...], vo_ref[...] = _adamw(w_ref[...], g, m_ref[...], v_ref[...])

    blk = pl.BlockSpec((tr, cols), lambda i: (i, 0))
    return _pcall(
        body, name="wada_adamw", grid=(rws // tr,),
        in_specs=[pl.BlockSpec((N_DEV, tr), lambda i: (0, i)), pl.BlockSpec((N_DEV, cols), lambda i: (0, 0)),
                  blk, blk, blk],
        out_specs=[blk] * 4, out_shape=[_sds((rws, cols), F32)] * 4, compiler_params=_params(),
    )(c_all, dmod_cols, w, m, v)


SMALL_NAMES = ("b_ada", "norm1_w", "conv_w", "conv_b", "dt_bias", "a_log", "d_skip", "attn_sinks", "ssm_norm_w",
               "norm2_w", "final_norm_w")


def _small_grads(tot, me):
    shard = D // N_DEV
    conv = tot[10:10 + CONVK, 0:shard]
    for dev in range(1, N_DEV):
        conv = jnp.where(me == dev, tot[10:10 + CONVK, dev * shard:(dev + 1) * shard], conv)
    return [
        jnp.concatenate([tot[r:r + 1, :] for r in range(N_MOD)], axis=1),
        tot[6:7, :], conv, tot[9:10, :],
        tot[15:16, 0:NH], tot[15:16, LB:LB + NH], tot[15:16, 2 * LB:2 * LB + NH], tot[15:16, 3 * LB:3 * LB + NQ],
        tot[14:15, 0:SW], tot[7:8, :], tot[8:9, :],
    ]


def _small_adamw(packs, ws, ms, vs):
    k = len(ws)

    def body(p_ref, *refs):
        w_refs, m_refs, v_refs = refs[:k], refs[k:2 * k], refs[2 * k:3 * k]
        loss_ref = refs[3 * k]
        g_refs, d_refs, mo_refs, vo_refs = (refs[3 * k + 1 + j * k:3 * k + 1 + (j + 1) * k] for j in range(4))
        tot_ref = refs[7 * k + 1]
        tot = p_ref[0]
        for dev in range(1, N_DEV):
            tot = tot + p_ref[dev]
        tot_ref[...] = tot
        loss_ref[...] = jnp.zeros((1, 1), F32) + (0.5 / D) * jnp.sum(tot[16:17, :])
        me = 4 * lax.axis_index("x") + 2 * lax.axis_index("y") + lax.axis_index("c")
        for i, g in enumerate(_small_grads(tot_ref, me)):
            g_refs[i][...] = g
            d_refs[i][...], mo_refs[i][...], vo_refs[i][...] = _adamw(w_refs[i][...], g, m_refs[i][...], v_refs[i][...])

    shp = [_sds(w.shape, F32) for w in ws]
    outs = _pcall(body, name="adamw_small", out_shape=[_sds((1, 1), F32)] + shp * 4,
                  scratch_shapes=[pltpu.VMEM((PACK_ROWS, D), F32)])(packs, *ws, *ms, *vs)
    return outs[0], outs[1:1 + k], outs[1 + k:1 + 2 * k], outs[1 + 2 * k:1 + 3 * k], outs[1 + 3 * k:]


def kernel(x, c, positions, w_ada, b_ada, norm1_w, w_in, conv_w, conv_b, dt_bias, a_log, d_skip, attn_sinks, ssm_norm_w, w_out, norm2_w, w_gate_up, w_down, final_norm_w, loss_target, m_w_ada, m_b_ada, m_norm1_w, m_w_in, m_conv_w, m_conv_b, m_dt_bias, m_a_log, m_d_skip, m_attn_sinks, m_ssm_norm_w, m_w_out, m_norm2_w, m_w_gate_up, m_w_down, m_final_norm_w, v_w_ada, v_b_ada, v_norm1_w, v_w_in, v_conv_w, v_conv_b, v_dt_bias, v_a_log, v_d_skip, v_attn_sinks, v_ssm_norm_w, v_w_out, v_norm2_w, v_w_gate_up, v_w_down, v_final_norm_w):
    s = x.shape[1]
    me = 4 * lax.axis_index("x") + 2 * lax.axis_index("y") + lax.axis_index("c")
    ada_cols = N_MOD * D // N_DEV

    w_in_t = jnp.transpose(w_in[0])
    rowwise = lambda a: jnp.transpose(a, (2, 0, 1))
    w_gu_t, m_w_gu_t, v_w_gu_t = (jnp.transpose(w_gate_up[0]), jnp.transpose(m_w_gate_up[0]),
                                  jnp.transpose(v_w_gate_up[0]))
    g_c, g_in, g_cw, g_mod = _prologue(c, w_in_t.astype(BF16), conv_w[0], w_ada[0], b_ada)
    c_all = g_c[:, 0, :]
    w_in_f = jnp.pad(g_in.reshape(IN_PROJ, D), ((0, INP - IN_PROJ), (0, 0)))
    conv_w8 = jnp.transpose(g_cw, (1, 0, 2)).reshape(8, D)
    mod = lax.dynamic_index_in_dim(g_mod, me, axis=1, keepdims=False).reshape(N_MOD, D)
    mod8 = jnp.pad(mod, ((0, 8 - N_MOD), (0, 0)))

    half = HD // 2
    inv_freq = ROPE_THETA ** (-jnp.arange(half, dtype=F32) / half)
    invf = jnp.tile(inv_freq, LB // half).reshape(1, LB)
    lanes = lambda a: jnp.pad(a, ((0, 0), (0, LB - a.shape[1])))
    ssm_p = jnp.pad(jnp.concatenate([lanes(dt_bias), lanes(a_log), lanes(d_skip)], axis=0), ((0, 5), (0, 0)))
    sinks8 = jnp.broadcast_to(attn_sinks.reshape(NQ, 1), (NQ, LB))

    xs, tgt, fnw = x[0], loss_target[0], final_norm_w.reshape(1, D)

    q, k, v, z, xbc, dtr, h1, cos, sin = _inproj_fwd(xs, positions[0].reshape(s, 1), invf, mod8, norm1_w, w_in_f)
    (attn, yn, y, hs, conv_u, dtv, acs, decay, probs, psink), (g_out, g_gu, g_down) = _mixer_fwd(
        q, k, v, sinks8, xbc, conv_w8, conv_b, dtr, ssm_p, z, ssm_norm_w,
        [w_out[0].astype(BF16), w_gu_t.astype(BF16), w_down[0].astype(BF16)])
    w_out_f = g_out.reshape(D, D)
    w_gu_f = g_gu.reshape(2 * DFF, D)
    w_down_f = g_down.reshape(DFF, D)
    x2, h2, mo, mix, gu, act, dx3, sm_f = _outproj_ffn_fwd_loss(attn, yn, xs, tgt, mod8, norm2_w, fnw, w_out_f, w_gu_f,
                                                                 w_down_f)

    dx2, dff, dgu, dmix, dattn, dyn, sm_b = _ffn_bwd(dx3, gu, x2, mo, mod8, norm2_w, w_gu_f, w_down_f, w_out_f)
    p_gu = _wgrad(dgu, h2, "wgrad_gate_up").reshape(N_DEV, 2 * DFF // N_DEV, D)
    p_down = _wgrad(act, dff, "wgrad_down").reshape(N_DEV, DFF // N_DEV, D)
    p_out = _wgrad(mix, dmix, "wgrad_out").reshape(N_DEV, D // N_DEV, D)
    dproj, dsink, sm_s, (r_gu, r_down, r_out) = _mixer_bwd(
        q, k, v, attn, dattn, cos, sin, probs, psink, dyn, y, z, xbc, conv_u, dtv, acs, decay, conv_w8, ssm_p,
        ssm_norm_w, hs, [p_gu, p_down, p_out])
    p_in = _wgrad(dproj, h1, "wgrad_in")[:IN_PROJ].reshape(N_DEV, IN_PROJ // N_DEV, D)
    gx, (r_in,), g_pack = _inproj_bwd(dproj, xs, dx2, mod8, norm1_w, w_in_f, [p_in], (sm_f, sm_b, sm_s, dsink))

    dmod_all = g_pack[:, 0:N_MOD, :].reshape(N_DEV, N_MOD * D)
    dmod_cols = lax.dynamic_slice(dmod_all, (0, me * ada_cols), (N_DEV, ada_cols))

    big = {
        "w_ada": _wada_adamw(c_all, dmod_cols, w_ada[0], m_w_ada[0], v_w_ada[0]),
        "w_in": [jnp.transpose(t, (1, 2, 0))[0] for t in
                 _sum_adamw_rowwise(r_in, rowwise(w_in), rowwise(m_w_in), rowwise(v_w_in), "adamw_in")],
        "w_out": _sum_adamw(r_out, w_out[0], m_w_out[0], v_w_out[0], "adamw_out"),
        "w_gate_up": [jnp.transpose(t) for t in _sum_adamw(r_gu, w_gu_t, m_w_gu_t, v_w_gu_t, "adamw_gate_up")],
        "w_down": _sum_adamw(r_down, w_down[0], m_w_down[0], v_w_down[0], "adamw_down"),
    }
    small_w = {"b_ada": b_ada, "norm1_w": norm1_w, "conv_w": conv_w[0], "conv_b": conv_b, "dt_bias": dt_bias,
               "a_log": a_log, "d_skip": d_skip, "attn_sinks": attn_sinks, "ssm_norm_w": ssm_norm_w,
               "norm2_w": norm2_w, "final_norm_w": final_norm_w.reshape(1, D)}
    small_m = {"b_ada": m_b_ada, "norm1_w": m_norm1_w, "conv_w": m_conv_w[0], "conv_b": m_conv_b,
               "dt_bias": m_dt_bias, "a_log": m_a_log, "d_skip": m_d_skip, "attn_sinks": m_attn_sinks,
               "ssm_norm_w": m_ssm_norm_w, "norm2_w": m_norm2_w, "final_norm_w": m_final_norm_w.reshape(1, D)}
    small_v = {"b_ada": v_b_ada, "norm1_w": v_norm1_w, "conv_w": v_conv_w[0], "conv_b": v_conv_b,
               "dt_bias": v_dt_bias, "a_log": v_a_log, "d_skip": v_d_skip, "attn_sinks": v_attn_sinks,
               "ssm_norm_w": v_ssm_norm_w, "norm2_w": v_norm2_w, "final_norm_w": v_final_norm_w.reshape(1, D)}
    loss, s_g, s_d, s_m, s_v = _small_adamw(g_pack, [small_w[k] for k in SMALL_NAMES],
                                            [small_m[k] for k in SMALL_NAMES], [small_v[k] for k in SMALL_NAMES])

    order = ["w_ada", "b_ada", "norm1_w", "w_in", "conv_w", "conv_b", "dt_bias", "a_log", "d_skip", "attn_sinks",
             "ssm_norm_w", "w_out", "norm2_w", "w_gate_up", "w_down", "final_norm_w"]
    lead = {"w_ada", "w_in", "conv_w", "w_out", "w_gate_up", "w_down"}
    grads, deltas, new_m, new_v = [], [], [], []
    for name in order:
        if name in big:
            g, d, m2, v2 = big[name]
        else:
            i = SMALL_NAMES.index(name)
            g, d, m2, v2 = s_g[i], s_d[i], s_m[i], s_v[i]
        if name in lead:
            g, d, m2, v2 = g[None], d[None], m2[None], v2[None]
        if name == "final_norm_w":
            g, d, m2, v2 = g.reshape(D), d.reshape(D), m2.reshape(D), v2.reshape(D)
        grads.append(g)
        deltas.append(d)
        new_m.append(m2)
        new_v.append(v2)
    return (loss.reshape(()), gx[None], *grads, *deltas, *new_m, *new_v)
```

```python
import functools
import math

import jax
import jax.numpy as jnp
import numpy as np
from jax import lax
from jax.experimental import pallas as pl
from jax.experimental.pallas import tpu as pltpu

F32 = jnp.float32
BF16 = jnp.bfloat16

N_DEV = 8
D = 1024
HD = 64
NQ = 8
AW = 512
KVW = 128
SW = 512
NST = 128
NH = 8
LB = 128
CONVK = 4
DFF = 2816
N_MOD = 6
IN_PROJ = 2312
INP = 2432
O_Q, O_K, O_V, O_Z, O_XBC, O_DT = 0, 512, 640, 768, 1280, 2304
ZXD = INP - O_Z
EPS = 1e-6
NEG = -1e30
ROPE_THETA = 10000.0
VMEM_LIMIT = 56 * 1024 * 1024

ADAM_LR = 0.001
ADAM_B1 = 0.9
ADAM_B2 = 0.999
ADAM_EPS = 1e-08
ADAM_WD = 0.01
ADAM_STEP = 10

NT_DIMS = (((1,), (1,)), ((), ()))
TN_DIMS = (((0,), (0,)), ((), ()))


def _pcall(body, **kw):
    return pl.pallas_call(body, **kw)


def _sds(shape, dtype):
    return jax.ShapeDtypeStruct(shape, dtype)


def _params(n_grid=1):
    return pltpu.CompilerParams(dimension_semantics=("arbitrary",) * n_grid, vmem_limit_bytes=VMEM_LIMIT)


def _const(shape):
    return pl.BlockSpec(shape, lambda *_: (0,) * len(shape), pipeline_mode=pl.Buffered(1))


def _largest_divisor(n, candidates):
    for cand in candidates:
        if n % cand == 0:
            return cand
    raise ValueError(f"no tile in {candidates} divides {n}")


def _rows(t, w):
    return pl.BlockSpec((t, w), lambda i: (i, 0))


def _dot(a, b):
    return jnp.dot(a, b, preferred_element_type=F32)


def _dot_nt(a, b):
    return lax.dot_general(a, b, NT_DIMS, preferred_element_type=F32)


def _dot_tn(a, b):
    return lax.dot_general(a, b, TN_DIMS, preferred_element_type=F32)


def _sigmoid(v):
    return 1.0 / (1.0 + jnp.exp(-v))


def _softplus(v):
    return jnp.maximum(v, 0.0) + jnp.log1p(jnp.exp(-jnp.abs(v)))


def _rope_sign_mask(shape):
    lane = lax.broadcasted_iota(jnp.int32, shape, 1)
    return (lane % HD) < (HD // 2)


def _rope(t, cs, sn, inverse):
    r_dn = pltpu.roll(t, HD // 2, 1)
    r_up = pltpu.roll(t, LB - HD // 2, 1)
    first = _rope_sign_mask(t.shape)
    if inverse:
        rot = jnp.where(first, r_up, -r_dn)
    else:
        rot = jnp.where(first, -r_up, r_dn)
    return t * cs + rot * sn


def _norm_mod_fwd(xv, nw, shift, scale):
    r = lax.rsqrt(jnp.mean(xv * xv, axis=-1, keepdims=True) + EPS)
    xh = xv * r
    return (xh * nw) * (1.0 + scale) + shift


def _norm_mod_bwd(xv, dh, nw, scale):
    r = lax.rsqrt(jnp.mean(xv * xv, axis=-1, keepdims=True) + EPS)
    xh = xv * r
    xn = xh * nw
    d_shift = jnp.sum(dh, axis=0, keepdims=True)
    d_scale = jnp.sum(dh * xn, axis=0, keepdims=True)
    dxn = dh * (1.0 + scale)
    d_w = jnp.sum(dxn * xh, axis=0, keepdims=True)
    dxh = dxn * nw
    dx = r * (dxh - xh * jnp.mean(dxh * xh, axis=-1, keepdims=True))
    return dx, d_shift, d_scale, d_w


def _inproj_fwd(x, pos, invf, mod8, n1w, w_in):
    s = x.shape[0]
    tt = min(512, s)

    def body(x_ref, pos_ref, invf_ref, mod_ref, nw_ref, w_ref,
             q_ref, k_ref, v_ref, z_ref, xbc_ref, dtr_ref, h1_ref, cos_ref, sin_ref):
        h = _norm_mod_fwd(x_ref[...], nw_ref[...], mod_ref[0:1, :], mod_ref[1:2, :])
        hb = h.astype(BF16)
        h1_ref[...] = hb
        proj = _dot_nt(hb, w_ref[...])
        ang = pos_ref[...].astype(F32) * invf_ref[...]
        cs = jnp.cos(ang)
        sn = jnp.sin(ang)
        cos_ref[...] = cs
        sin_ref[...] = sn
        for a in range(AW // LB):
            q_ref[:, a * LB:(a + 1) * LB] = _rope(proj[:, O_Q + a * LB:O_Q + (a + 1) * LB], cs, sn, False).astype(BF16)
        k_ref[...] = _rope(proj[:, O_K:O_V], cs, sn, False).astype(BF16)
        v_ref[...] = proj[:, O_V:O_Z].astype(BF16)
        z_ref[...] = proj[:, O_Z:O_XBC]
        xbc_ref[...] = proj[:, O_XBC:O_DT]
        dtr_ref[...] = proj[:, O_DT:INP]

    return _pcall(
        body, name="inproj_fwd", grid=(s // tt,),
        in_specs=[_rows(tt, D), _rows(tt, 1), _const((1, LB)), _const((8, D)), _const((1, D)), _const((INP, D))],
        out_specs=[_rows(tt, AW), _rows(tt, KVW), _rows(tt, KVW), _rows(tt, SW), _rows(tt, D), _rows(tt, LB),
                   _rows(tt, D), _rows(tt, LB), _rows(tt, LB)],
        out_shape=[_sds((s, AW), BF16), _sds((s, KVW), BF16), _sds((s, KVW), BF16), _sds((s, SW), F32),
                   _sds((s, D), F32), _sds((s, LB), F32), _sds((s, D), BF16), _sds((s, LB), F32), _sds((s, LB), F32)],
        compiler_params=_params(),
    )(x, pos, invf, mod8, n1w, w_in)


QPG = 4
ATT_SCALE = 1.0 / math.sqrt(HD)


def _stack_heads(val, g):
    return jnp.concatenate([val[:, (QPG * g + hh) * HD:(QPG * g + hh + 1) * HD] for hh in range(QPG)], axis=0)


def _unstack_heads(groups):
    pieces = [grp[hh * LB:(hh + 1) * LB, :] for grp in groups for hh in range(QPG)]
    return [jnp.concatenate(pieces[2 * a:2 * a + 2], axis=1) for a in range(NQ // 2)]


def _upper_mask():
    row = lax.broadcasted_iota(jnp.int32, (QPG * LB, LB), 0)
    col = lax.broadcasted_iota(jnp.int32, (QPG * LB, LB), 1)
    return col > (row % LB)


def _sink_wide(sinks, g):
    return jnp.concatenate([jnp.broadcast_to(sinks[QPG * g + hh:QPG * g + hh + 1, 0:1], (LB, LB))
                            for hh in range(QPG)], axis=0)


def _row_sums_wide(v, terms):
    return _dot_sel(v, jnp.ones((v.shape[1], LB), BF16), terms)


def _band(upper, prev_part, cur_part):
    return jnp.where(upper, prev_part, cur_part)


def _attn_scores(n, qg, kcat, upper):
    sp = _dot_nt(qg, kcat[0:LB, :]) * ATT_SCALE
    sc = _dot_nt(qg, kcat[LB:2 * LB, :]) * ATT_SCALE
    return _band(upper, jnp.where(n > 0, sp, NEG), sc)


def _attn_softmax(comb, sink):
    m = jnp.maximum(jnp.max(comb, axis=-1, keepdims=True), sink)
    p = jnp.exp(comb - m)
    es = jnp.exp(sink - m)
    return p, es, _row_sums_wide(p, 1) + es


def _attn_fwd_block(n, q_ref, kp_ref, kc_ref, vp_ref, vc_ref, sink_ref, o_ref, pr_ref, ps_ref):
    qv = q_ref[...]
    kcat = jnp.concatenate([kp_ref[...], kc_ref[...]], axis=0)
    vcat = jnp.concatenate([vp_ref[...], vc_ref[...]], axis=0)
    sinks = sink_ref[...]
    upper = _upper_mask()
    outs = []
    for g in range(NQ // QPG):
        sl = slice(g * HD, (g + 1) * HD)
        rows = slice(g * QPG * LB, (g + 1) * QPG * LB)
        p, es, denom = _attn_softmax(_attn_scores(n, _stack_heads(qv, g), kcat[:, sl], upper), _sink_wide(sinks, g))
        rden = 1.0 / denom
        pr_ref[0, rows, :] = (p * rden).astype(BF16)
        ps_ref[0, rows, :] = (es * rden).astype(BF16)
        outs.append((_dot(jnp.where(upper, p, 0.0).astype(BF16), vcat[0:LB, sl])
                     + _dot(jnp.where(upper, 0.0, p).astype(BF16), vcat[LB:2 * LB, sl])) * rden[:, 0:HD])
    for g, grp in enumerate(outs):
        for hh in range(QPG):
            h = QPG * g + hh
            o_ref[:, h * HD:(h + 1) * HD] = grp[hh * LB:(hh + 1) * LB, :].astype(BF16)


def _cumsum_rows(a, reverse):
    row = lax.broadcasted_iota(jnp.int32, a.shape, 0)
    step = 1
    while step < LB:
        if reverse:
            a = a + jnp.where(row < LB - step, pltpu.roll(a, LB - step, 0), 0.0)
        else:
            a = a + jnp.where(row >= step, pltpu.roll(a, step, 0), 0.0)
        step *= 2
    return a


SUB = 8


def _conv_shifts(tail, cur):
    row = lax.broadcasted_iota(jnp.int32, tail.shape, 0)
    out = [cur]
    for j in range(1, CONVK):
        rolled = pltpu.roll(cur, j, 0)
        top = jnp.where(row < j, pltpu.roll(tail, j, 0), rolled[0:SUB, :])
        out.append(jnp.concatenate([top, rolled[SUB:, :]], axis=0))
    return out


def _conv_advances(du, head):
    row = lax.broadcasted_iota(jnp.int32, head.shape, 0)
    out = []
    for j in range(1, CONVK):
        rolled = pltpu.roll(du, LB - j, 0)
        bottom = jnp.where(row >= SUB - j, pltpu.roll(head, SUB - j, 0), rolled[LB - SUB:, :])
        out.append(jnp.concatenate([rolled[:LB - SUB, :], bottom], axis=0))
    return out


def _split(v, terms):
    out = []
    for _ in range(terms - 1):
        t = v.astype(BF16)
        out.append(t)
        v = v - t.astype(F32)
    out.append(v.astype(BF16))
    return out


def _dot_sel(v, sel, terms):
    parts = [_dot(t, sel) for t in _split(v, terms)]
    return functools.reduce(lambda a, b: a + b, parts)


def _dot_nt_sel(v, sel, terms):
    parts = [_dot_nt(t, sel) for t in _split(v, terms)]
    return functools.reduce(lambda a, b: a + b, parts)


def _ssd_pre(xt_ref, xc_ref, cw_ref, cb_ref, dtr_ref, sp_ref, n):
    cur = xc_ref[...]
    tail = jnp.where(n > 0, xt_ref[...], 0.0)
    sh = _conv_shifts(tail, cur)
    u = cb_ref[...] + cw_ref[CONVK - 1:CONVK, :] * sh[0]
    for j in range(1, CONVK):
        u = u + cw_ref[CONVK - 1 - j:CONVK - j, :] * sh[j]
    dt = _softplus(dtr_ref[...] + sp_ref[0:1, :])
    acs = _cumsum_rows(dt * -jnp.exp(sp_ref[1:2, :]), False)
    return u, dt, acs


def _gated_norm_fwd(y, z, sgz, nw):
    yz = y * (z * sgz)
    parts = []
    for g in range(2):
        t = yz[:, g * 256:(g + 1) * 256]
        parts.append(t * lax.rsqrt(jnp.mean(t * t, axis=-1, keepdims=True) + EPS))
    return jnp.concatenate(parts, axis=1) * nw


HPG = 4
GW = HPG * HD


class _SsdChunk:
    def __init__(self, xc, dt, acs, spv, e64, e128, decay=None):
        self.e64, self.e128 = e64, e128
        alast = acs[LB - 1:LB, :]
        self.e_all = jnp.exp(acs)
        self.dte_all = jnp.exp(alast - acs)
        self.elast = jnp.exp(alast)
        wide = _dot_sel(jnp.concatenate([dt, self.e_all, self.dte_all], axis=0), e64, 2)
        self.dt_x, self.e_x, self.dte_x = wide[0:LB], wide[LB:2 * LB], wide[2 * LB:3 * LB]
        self.dsk_x = _dot_sel(spv, e64, 3)[2:3, :]
        if decay is None:
            acs_t = jnp.transpose(acs)
            ac_x = _dot_sel(acs, e128, 3)
            row = lax.broadcasted_iota(jnp.int32, (HPG * LB, LB), 0)
            col = lax.broadcasted_iota(jnp.int32, (HPG * LB, LB), 1)
            causal = (row % LB) >= col
        lane = lax.broadcasted_iota(jnp.int32, (LB, GW), 1)
        self.head_lanes = [(lane >= hh * HD) & (lane < (hh + 1) * HD) for hh in range(HPG)]
        self.xs, self.xdt, self.b, self.c, self.bb, self.cb16, self.cbm, self.dm_st, self.m_st = ([] for _ in range(9))
        for g in range(2):
            heads = range(HPG * g, HPG * (g + 1))
            if decay is None:
                ac_st = jnp.concatenate([ac_x[:, j * LB:(j + 1) * LB] for j in heads], axis=0)
                ar_st = jnp.concatenate([jnp.broadcast_to(acs_t[j:j + 1, :], (LB, LB)) for j in heads], axis=0)
                dm_st = jnp.exp(jnp.where(causal, ac_st - ar_st, NEG))
            else:
                dm_st = decay[g]
            bg = xc[:, SW + g * NST:SW + (g + 1) * NST]
            cg = xc[:, SW + 2 * NST + g * NST:SW + 2 * NST + (g + 1) * NST]
            bgb, cgb = bg.astype(BF16), cg.astype(BF16)
            cbm = _dot_nt(cgb, bgb)
            xs_g = xc[:, g * GW:(g + 1) * GW]
            self.xs.append(xs_g)
            self.xdt.append(xs_g * self.dt_x[:, g * GW:(g + 1) * GW])
            self.b.append(bg)
            self.c.append(cg)
            self.bb.append(bgb)
            self.cb16.append(cgb)
            self.cbm.append(cbm)
            self.dm_st.append(dm_st)
            self.m_st.append(jnp.concatenate([cbm] * HPG, axis=0) * dm_st)

    def elast_rows(self, g):
        return jnp.concatenate([jnp.broadcast_to(self.elast[:, j:j + 1], (HD, NST))
                                for j in range(HPG * g, HPG * (g + 1))], axis=0)

    def diag_blocks(self, stacked):
        out = stacked[(HPG - 1) * LB:HPG * LB, :]
        for hh in range(HPG - 2, -1, -1):
            out = jnp.where(self.head_lanes[hh], stacked[hh * LB:(hh + 1) * LB, :], out)
        return out

    def block_diag(self, v):
        return jnp.concatenate([jnp.where(self.head_lanes[hh], v, 0.0) for hh in range(HPG)], axis=0)


def _ssd_fwd_block(n, xt_ref, xc_ref, cw_ref, cb_ref, dtr_ref, sp_ref, z_ref, nw_ref, e64_ref, e128_ref,
                   yn_ref, y_ref, hs_ref, u_ref, dt_ref, acs_ref, dm_ref, h_scr):
    @pl.when(n == 0)
    def _():
        h_scr[...] = jnp.zeros_like(h_scr)

    h_all = h_scr[...]
    hs_ref[0] = h_all
    u, dt, acs = _ssd_pre(xt_ref, xc_ref, cw_ref, cb_ref, dtr_ref, sp_ref, n)
    u_ref[...] = u
    dt_ref[...] = dt
    acs_ref[...] = acs
    xc = u * _sigmoid(u)
    ck = _SsdChunk(xc, dt, acs, sp_ref[...], e64_ref[...], e128_ref[...])
    dm_ref[0] = jnp.concatenate(ck.dm_st, axis=0)
    ys, hn = [], []
    for g in range(2):
        gl = slice(g * GW, (g + 1) * GW)
        xdt = ck.xdt[g]
        hg = h_all[gl, :]
        y_diag = ck.diag_blocks(_dot(ck.m_st[g].astype(BF16), xdt.astype(BF16)))
        y_off = ck.e_x[:, gl] * _dot_nt(ck.cb16[g], hg.astype(BF16))
        ys.append(y_diag + y_off + ck.xs[g] * ck.dsk_x[:, gl])
        hn.append(hg * ck.elast_rows(g) + _dot_tn((xdt * ck.dte_x[:, gl]).astype(BF16), ck.bb[g]))
    h_scr[...] = jnp.concatenate(hn, axis=0)
    y = jnp.concatenate(ys, axis=1)
    y_ref[...] = y
    z = z_ref[...]
    yn_ref[...] = _gated_norm_fwd(y, z, _sigmoid(z), nw_ref[...]).astype(BF16)


def _mixer_fwd(q, k, v, sinks8, xbc, conv_w8, conv_b, dtr, ssm_p, z, nw, gathers):
    s = q.shape[0]
    nb = s // LB
    bps = _blocks_per_step(nb)
    nsteps = nb // bps
    tl = bps * LB
    cur = lambda n: (n, 0)
    prev = lambda n: (jnp.maximum(n * bps - 1, 0), 0)
    items, ex_shapes, n_g = _exchange_items(gathers, [])
    ne = len(items)

    n_in, n_out = 16, 10
    relay_step = (3 * (nsteps - 1)) // 4
    stack = pl.BlockSpec((bps, NH * LB, LB), lambda n: (n, 0, 0))
    e64, e128 = _head_expanders()

    def body(*refs):
        (q_ref, kp_ref, kc_ref, vp_ref, vc_ref, sink_ref, xt_ref, xc_ref, cw_ref, cb_ref, dtr_ref, sp_ref, z_ref,
         nw_ref, e64_ref, e128_ref) = refs[:n_in]
        ex_in = refs[n_in:n_in + ne]
        (o_ref, yn_ref, y_ref, hs_ref, u_ref, dt_ref, acs_ref, dm_ref, pr_ref,
         ps_ref) = refs[n_in + ne:n_in + n_out + ne]
        ex_out = refs[n_in + n_out + ne:n_in + n_out + 2 * ne]
        h_scr = refs[n_in + n_out + 2 * ne]
        sems = refs[n_in + n_out + 1 + 2 * ne:]
        n = pl.program_id(0)

        @pl.when(n == 0)
        def _():
            _Exchange(n_g, ex_in, ex_out, sems).two_level_start()

        for sub in range(bps):
            blk = n * bps + sub
            r = slice(sub * LB, (sub + 1) * LB)
            before = slice((sub - 1) * LB, sub * LB)
            one = slice(sub, sub + 1)
            _attn_fwd_block(blk, q_ref.at[r], kp_ref if sub == 0 else kc_ref.at[before], kc_ref.at[r],
                            vp_ref if sub == 0 else vc_ref.at[before], vc_ref.at[r], sink_ref,
                            o_ref.at[r], pr_ref.at[one], ps_ref.at[one])
            _ssd_fwd_block(blk, xt_ref if sub == 0 else xc_ref.at[sub * LB - SUB:sub * LB], xc_ref.at[r], cw_ref,
                           cb_ref, dtr_ref.at[r], sp_ref, z_ref.at[r], nw_ref, e64_ref, e128_ref,
                           yn_ref.at[r], y_ref.at[r], hs_ref.at[one], u_ref.at[r], dt_ref.at[r], acs_ref.at[r],
                           dm_ref.at[one], h_scr)

        @pl.when(n == relay_step)
        def _():
            _Exchange(n_g, ex_in, ex_out, sems).two_level_relay()

        @pl.when(n == nsteps - 1)
        def _():
            _Exchange(n_g, ex_in, ex_out, sems).two_level_finish()

    any_spec = pl.BlockSpec(memory_space=pl.ANY)
    tail = pl.BlockSpec((SUB, D), lambda n: (jnp.maximum(n * (tl // SUB) - 1, 0), 0))
    outs = _pcall(
        body, name="mixer_fwd", grid=(nsteps,),
        in_specs=[pl.BlockSpec((tl, AW), cur), pl.BlockSpec((LB, KVW), prev), pl.BlockSpec((tl, KVW), cur),
                  pl.BlockSpec((LB, KVW), prev), pl.BlockSpec((tl, KVW), cur), _const((8, LB)),
                  tail, pl.BlockSpec((tl, D), cur), _const((8, D)), _const((1, D)),
                  pl.BlockSpec((tl, LB), cur), _const((8, LB)), pl.BlockSpec((tl, SW), cur), _const((1, SW)),
                  _const(e64.shape), _const(e128.shape)]
        + [any_spec] * ne,
        out_specs=[pl.BlockSpec((tl, AW), cur), pl.BlockSpec((tl, SW), cur), pl.BlockSpec((tl, SW), cur),
                   pl.BlockSpec((bps, NH * HD, NST), lambda n: (n, 0, 0)), pl.BlockSpec((tl, D), cur),
                   pl.BlockSpec((tl, LB), cur), pl.BlockSpec((tl, LB), cur),
                   stack, stack, stack] + [any_spec] * ne,
        out_shape=[_sds((s, AW), BF16), _sds((s, SW), BF16), _sds((s, SW), F32), _sds((nb, NH * HD, NST), F32),
                   _sds((s, D), F32), _sds((s, LB), F32), _sds((s, LB), F32), _sds((nb, NH * LB, LB), F32),
                   _sds((nb, NH * LB, LB), BF16), _sds((nb, NH * LB, LB), BF16)]
        + ex_shapes,
        scratch_shapes=[pltpu.VMEM((NH * HD, NST), F32)] + _exchange_sems(ne),
        compiler_params=_params(),
    )(q, k, k, v, v, sinks8, xbc, xbc, conv_w8, conv_b, dtr, ssm_p, z, nw, e64, e128, *items)
    return outs[:n_out], outs[n_out:]


def _blocks_per_step(nb):
    return next(b for b in (4, 2, 1) if nb % b == 0)


def _head_expanders():
    j = np.arange(LB)[:, None]
    e64 = (np.arange(NH * HD)[None, :] // HD == j).astype(BF16)
    e128 = (np.arange(NH * LB)[None, :] // LB == j).astype(BF16)
    return jnp.asarray(e64), jnp.asarray(e128)


def _outproj_ffn_fwd_loss(attn, yn, x, tgt, mod8, n2w, fnw, w_out, w_gu_t, w_down):
    s = x.shape[0]
    tf = min(256, s)

    def body(a_ref, y_ref, x_ref, t_ref, mod_ref, nw_ref, fw_ref, wo_ref, wgu_ref, wd_ref,
             x2_ref, h2_ref, mo_ref, mix_ref, gu_ref, act_ref, dx3_ref, sm_ref):
        i = pl.program_id(0)

        @pl.when(i == 0)
        def _():
            sm_ref[...] = jnp.zeros_like(sm_ref)

        mix = jnp.concatenate([a_ref[...], y_ref[...]], axis=1)
        mix_ref[...] = mix
        mo = _dot(mix, wo_ref[...])
        mo_ref[...] = mo.astype(BF16)
        x2 = x_ref[...] + mod_ref[2:3, :] * mo
        x2_ref[...] = x2
        h2 = _norm_mod_fwd(x2, nw_ref[...], mod_ref[3:4, :], mod_ref[4:5, :]).astype(BF16)
        h2_ref[...] = h2
        gu = _dot_nt(h2, wgu_ref[...])
        gu_ref[...] = gu.astype(BF16)
        g = gu[:, :DFF]
        act = (g * _sigmoid(g) * gu[:, DFF:]).astype(BF16)
        act_ref[...] = act
        ff = _dot(act, wd_ref[...])
        x3 = x2 + mod_ref[5:6, :] * ff
        r = lax.rsqrt(jnp.mean(x3 * x3, axis=-1, keepdims=True) + EPS)
        xh = x3 * r
        fw = fw_ref[...]
        err = xh * fw - t_ref[...]
        dy = err * (1.0 / D)
        dxh = dy * fw
        dx3 = r * (dxh - xh * jnp.mean(dxh * xh, axis=-1, keepdims=True))
        dx3_ref[...] = dx3
        sm_ref[0:1, :] += jnp.sum(dx3 * ff, axis=0, keepdims=True)
        sm_ref[1:2, :] += jnp.sum(dy * xh, axis=0, keepdims=True)
        sm_ref[2:3, :] += jnp.sum(err * err, axis=0, keepdims=True)

    return _pcall(
        body, name="outproj_ffn_fwd_loss", grid=(s // tf,),
        in_specs=[_rows(tf, AW), _rows(tf, SW), _rows(tf, D), _rows(tf, D), _const((8, D)), _const((1, D)),
                  _const((1, D)), _const((D, D)), _const((2 * DFF, D)), _const((DFF, D))],
        out_specs=[_rows(tf, D), _rows(tf, D), _rows(tf, D), _rows(tf, D), _rows(tf, 2 * DFF), _rows(tf, DFF),
                   _rows(tf, D), pl.BlockSpec((8, D), lambda i: (0, 0))],
        out_shape=[_sds((s, D), F32), _sds((s, D), BF16), _sds((s, D), BF16), _sds((s, D), BF16),
                   _sds((s, 2 * DFF), BF16), _sds((s, DFF), BF16), _sds((s, D), F32), _sds((8, D), F32)],
        compiler_params=_params(),
    )(attn, yn, x, tgt, mod8, n2w, fnw, w_out, w_gu_t, w_down)


def _ffn_bwd(dx3, gu, x2, mixout, mod8, n2w, w_gu, w_down, w_out):
    s = x2.shape[0]
    tb = min(256, s)

    def body(dx3_ref, gu_ref, x2_ref, mo_ref, mod_ref, nw_ref, wgu_ref, wd_ref, wo_ref,
             dx2_ref, dff_ref, dgu_ref, dmix_ref, dattn_ref, dyn_ref, sm_ref):
        i = pl.program_id(0)

        @pl.when(i == 0)
        def _():
            sm_ref[...] = jnp.zeros_like(sm_ref)

        dx3 = dx3_ref[...]
        dff = (dx3 * mod_ref[5:6, :]).astype(BF16)
        dff_ref[...] = dff
        dact = _dot_nt(dff, wd_ref[...])
        g = gu_ref[:, :DFF].astype(F32)
        u = gu_ref[:, DFF:].astype(F32)
        sg = _sigmoid(g)
        dgu = jnp.concatenate([dact * u * sg * (1.0 + g * (1.0 - sg)), dact * g * sg], axis=1).astype(BF16)
        dgu_ref[...] = dgu
        dh2 = _dot(dgu, wgu_ref[...])
        dxn, d_shift, d_scale, d_w = _norm_mod_bwd(x2_ref[...], dh2, nw_ref[...], mod_ref[4:5, :])
        dx2 = dx3 + dxn
        dx2_ref[...] = dx2
        sm_ref[0:1, :] += d_shift
        sm_ref[1:2, :] += d_scale
        sm_ref[2:3, :] += d_w
        sm_ref[3:4, :] += jnp.sum(dx2 * mo_ref[...].astype(F32), axis=0, keepdims=True)
        dmix = (dx2 * mod_ref[2:3, :]).astype(BF16)
        dmix_ref[...] = dmix
        dmi = _dot_nt(dmix, wo_ref[...])
        dattn_ref[...] = dmi[:, :AW].astype(BF16)
        dyn_ref[...] = dmi[:, AW:]

    return _pcall(
        body, name="ffn_bwd", grid=(s // tb,),
        in_specs=[_rows(tb, D), _rows(tb, 2 * DFF), _rows(tb, D), _rows(tb, D), _const((8, D)), _const((1, D)),
                  _const((2 * DFF, D)), _const((DFF, D)), _const((D, D))],
        out_specs=[_rows(tb, D), _rows(tb, D), _rows(tb, 2 * DFF), _rows(tb, D), _rows(tb, AW), _rows(tb, SW),
                   pl.BlockSpec((8, D), lambda i: (0, 0))],
        out_shape=[_sds((s, D), F32), _sds((s, D), BF16), _sds((s, 2 * DFF), BF16), _sds((s, D), BF16),
                   _sds((s, AW), BF16), _sds((s, SW), F32), _sds((8, D), F32)],
        compiler_params=_params(),
    )(dx3, gu, x2, mixout, mod8, n2w, w_gu, w_down, w_out)


def _ssd_bwd_block(i, *refs):
    def run(dyn_ref, y_ref, z_ref, x_ref, u_ref, dt_ref, acs_ref, dm_ref, cw_ref, sp_ref, nw_ref,
            hs_ref, e64_ref, e128_ref, dzxd_ref, sm_ref, dh_scr, dun_scr):
        @pl.when(i == 0)
        def _():
            dh_scr[...] = jnp.zeros_like(dh_scr)
            dun_scr[...] = jnp.zeros_like(dun_scr)
            sm_ref[...] = jnp.zeros_like(sm_ref)

        u, dt, acs = u_ref[...], dt_ref[...], acs_ref[...]
        sg_u = _sigmoid(u)
        xc = u * sg_u
        a_neg = -jnp.exp(sp_ref[1:2, :])
        ck = _SsdChunk(xc, dt, acs, sp_ref[...], e64_ref[...], e128_ref[...],
                       decay=[dm_ref[0, g * HPG * LB:(g + 1) * HPG * LB, :] for g in range(2)])
        h_all = hs_ref[0]
        dh_all = dh_scr[...]
        riota = lax.broadcasted_iota(jnp.int32, (LB, LB), 0)
        lane1 = lax.broadcasted_iota(jnp.int32, (1, LB), 1)

        z = z_ref[...]
        y = y_ref[...]
        sgz = _sigmoid(z)
        sz = z * sgz
        yz = y * sz
        nwv = nw_ref[...]
        dyn_v = dyn_ref[...]
        dyhat = dyn_v * nwv
        yhat_parts, dyz_parts = [], []
        for g in range(2):
            gs = slice(g * 256, (g + 1) * 256)
            t = yz[:, gs]
            rg = lax.rsqrt(jnp.mean(t * t, axis=-1, keepdims=True) + EPS)
            yh = t * rg
            dyh = dyhat[:, gs]
            yhat_parts.append(yh)
            dyz_parts.append(rg * (dyh - yh * jnp.mean(dyh * yh, axis=-1, keepdims=True)))
        yhat = jnp.concatenate(yhat_parts, axis=1)
        dyz = jnp.concatenate(dyz_parts, axis=1)
        sm_ref[5:6, 0:SW] += jnp.sum(dyn_v * yhat, axis=0, keepdims=True)
        dy = dyz * sz
        dzxd_ref[:, 0:SW] = (dyz * y * sgz * (1.0 + z * (1.0 - sgz))).astype(BF16)

        cat = lambda parts: jnp.concatenate(parts, axis=1)
        dxs, dbs, dcs, dhp, g_cat, de_x, ddte_x, ddt_x, ddsk_x = ([] for _ in range(9))
        dacs_t = jnp.zeros((LB, LB), F32)
        hsum = jnp.zeros((1, LB), F32)
        for g in range(2):
            gl = slice(g * GW, (g + 1) * GW)
            xs_g, xdt, bgb, cgb = ck.xs[g], ck.xdt[g], ck.bb[g], ck.cb16[g]
            m_st, dm_st = ck.m_st[g], ck.dm_st[g]
            dt_x, e_x, dte_x = ck.dt_x[:, gl], ck.e_x[:, gl], ck.dte_x[:, gl]
            xdtb = xdt.astype(BF16)
            hg, dhn = h_all[gl, :], dh_all[gl, :]
            hb, dhnb = hg.astype(BF16), dhn.astype(BF16)
            dy_g = dy[:, gl]
            ddsk_x.append(jnp.sum(dy_g * xs_g, axis=0, keepdims=True))
            dy_bd = ck.block_diag(dy_g).astype(BF16)
            dm4 = _dot_nt(dy_bd, xdtb)
            dxdt = _dot_tn(m_st.astype(BF16), dy_bd)
            gmat = dm4 * m_st
            dcbm = dm4 * dm_st
            dcb = dcbm[0:LB] + dcbm[LB:2 * LB] + dcbm[2 * LB:3 * LB] + dcbm[3 * LB:4 * LB]
            g_cat.append(cat([gmat[hh * LB:(hh + 1) * LB, :] for hh in range(HPG)]))
            for hh in range(HPG):
                j = HPG * g + hh
                col_sum = jnp.sum(gmat[hh * LB:(hh + 1) * LB, :], axis=0, keepdims=True)
                dacs_t = dacs_t - jnp.where(riota == j, col_sum, 0.0)
                hsl = slice(hh * HD, (hh + 1) * HD)
                hsum = hsum + jnp.where(lane1 == j, jnp.sum(dhn[hsl, :] * hg[hsl, :]), 0.0)
            dchb = (dy_g * e_x).astype(BF16)
            dcg = _dot(dchb, hb)
            dh_prev = _dot_tn(dchb, cgb)
            de_x.append(dy_g * _dot_nt(cgb, hb))
            dxs_s = _dot_nt(bgb, dhnb)
            dbg = _dot((xdt * dte_x).astype(BF16), dhnb)
            dxdt = dxdt + dxs_s * dte_x
            ddte_x.append(dxs_s * xdt)
            dhp.append(dhn * ck.elast_rows(g) + dh_prev)
            dxs.append(dy_g * ck.dsk_x[:, gl] + dxdt * dt_x)
            ddt_x.append(dxdt * xs_g)
            dcbb = dcb.astype(BF16)
            dbs.append(dbg + _dot_tn(dcbb, cgb))
            dcs.append(dcg + _dot(dcbb, bgb))
        dh_scr[...] = jnp.concatenate(dhp, axis=0)
        red = _dot_nt_sel(jnp.concatenate([cat(de_x), cat(ddte_x), cat(ddt_x)], axis=0), ck.e64, 1)
        de_c, ddte_c, ddt_c = red[0:LB], red[LB:2 * LB], red[2 * LB:3 * LB]
        ddsk = _dot_nt_sel(jnp.broadcast_to(cat(ddsk_x), (SUB, NH * HD)), ck.e64, 2)[0:1, :]
        t1 = ddte_c * ck.dte_all
        dalast = jnp.sum(t1, axis=0, keepdims=True) + hsum * ck.elast
        dacs = (_dot_nt_sel(cat(g_cat), ck.e128, 2) + de_c * ck.e_all - t1 + jnp.transpose(dacs_t)
                + jnp.where(riota == LB - 1, dalast, 0.0))
        da = _cumsum_rows(dacs, True)
        ddt = ddt_c + da * a_neg
        da_log = jnp.sum(da * dt, axis=0, keepdims=True) * a_neg
        ddtr = ddt * (1.0 - jnp.exp(-dt))
        dzxd_ref[:, SW + D:ZXD] = ddtr.astype(BF16)
        sm_ref[6:7, 0:LB] += jnp.sum(ddtr, axis=0, keepdims=True)
        sm_ref[6:7, LB:2 * LB] += da_log
        sm_ref[6:7, 2 * LB:3 * LB] += ddsk

        du = cat(dxs + dbs + dcs) * (sg_u * (1.0 + u * (1.0 - sg_u)))
        xv = x_ref[...]
        adv = [du] + _conv_advances(du, dun_scr[...])
        sm_ref[0:1, :] += jnp.sum(du, axis=0, keepdims=True)
        dxbc = cw_ref[CONVK - 1:CONVK, :] * du
        for j in range(CONVK):
            sm_ref[CONVK - j:CONVK + 1 - j, :] += jnp.sum(adv[j] * xv, axis=0, keepdims=True)
            if j:
                dxbc = dxbc + cw_ref[CONVK - 1 - j:CONVK - j, :] * adv[j]
        dun_scr[...] = du[0:SUB, :]
        dzxd_ref[:, SW:SW + D] = dxbc.astype(BF16)

    run(*refs)


def _attn_bwd_block(i, q_ref, kp_ref, kc_ref, vp_ref, vc_ref, o_ref, do_ref, cos_ref, sin_ref, pr_ref, ps_ref,
                    dq_ref, dkv_ref, ds_ref, ck_scr, cv_scr):
    @pl.when(i == 0)
    def _():
        ds_ref[...] = jnp.zeros_like(ds_ref)
        ck_scr[...] = jnp.zeros_like(ck_scr)
        cv_scr[...] = jnp.zeros_like(cv_scr)

    qv, ov, dov = q_ref[...], o_ref[...], do_ref[...]
    kcat = jnp.concatenate([kp_ref[...], kc_ref[...]], axis=0)
    vcat = jnp.concatenate([vp_ref[...], vc_ref[...]], axis=0)
    upper = _upper_mask()
    srow = lax.broadcasted_iota(jnp.int32, (8, LB), 0)
    slane = lax.broadcasted_iota(jnp.int32, (8, LB), 1)
    dsink = jnp.zeros((8, LB), F32)
    dq_g, dk_g, dv_g = [], [], []
    for g in range(NQ // QPG):
        sl = slice(g * HD, (g + 1) * HD)
        qg = _stack_heads(qv, g)
        dog = _stack_heads(dov, g)
        rows = slice(g * QPG * LB, (g + 1) * QPG * LB)
        probs = pr_ref[0, rows, :].astype(F32)
        psink = ps_ref[0, rows, :].astype(F32)
        delta = _row_sums_wide(dog.astype(F32) * _stack_heads(ov, g).astype(F32), 2)
        dsc = probs * (_band(upper, _dot_nt(dog, vcat[0:LB, sl]), _dot_nt(dog, vcat[LB:2 * LB, sl])) - delta)
        sink_terms = (psink * delta)[:, 0:1]
        for hh in range(QPG):
            dsink = dsink - jnp.where((srow == QPG * g + hh) & (slane == 0),
                                      jnp.sum(sink_terms[hh * LB:(hh + 1) * LB, :]), 0.0)
        ds_p = jnp.where(upper, dsc, 0.0).astype(BF16)
        ds_c = jnp.where(upper, 0.0, dsc).astype(BF16)
        dq_g.append((_dot(ds_p, kcat[0:LB, sl]) + _dot(ds_c, kcat[LB:2 * LB, sl])) * ATT_SCALE)
        dk_g.append(jnp.concatenate([_dot_tn(ds_p, qg), _dot_tn(ds_c, qg)], axis=0) * ATT_SCALE)
        dv_g.append(jnp.concatenate([_dot_tn(jnp.where(upper, probs, 0.0).astype(BF16), dog),
                                     _dot_tn(jnp.where(upper, 0.0, probs).astype(BF16), dog)], axis=0))
    ds_ref[...] += dsink
    cs = cos_ref[...]
    sn = sin_ref[...]
    dk2 = jnp.concatenate(dk_g, axis=1)
    dv2 = jnp.concatenate(dv_g, axis=1)
    for a, tile in enumerate(_unstack_heads(dq_g)):
        dq_ref[:, a * LB:(a + 1) * LB] = _rope(tile, cs, sn, True).astype(BF16)
    dkv_ref[:, 0:KVW] = _rope(ck_scr[...] + dk2[LB:2 * LB, :], cs, sn, True).astype(BF16)
    dkv_ref[:, KVW:2 * KVW] = (cv_scr[...] + dv2[LB:2 * LB, :]).astype(BF16)
    ck_scr[...] = dk2[0:LB, :]
    cv_scr[...] = dv2[0:LB, :]


def _mixer_bwd(q, k, v, o, do, cos, sin, probs, psink, dyn, y, z, xbc, u, dtv, acs, decay, conv_w8, ssm_p, nw, hs,
               scatters):
    s = q.shape[0]
    nb = s // LB
    bps = _blocks_per_step(nb)
    nsteps = nb // bps
    tl = bps * LB
    cur = lambda i: (nsteps - 1 - i, 0)
    prev = lambda i: (jnp.maximum((nsteps - 1 - i) * bps - 1, 0), 0)
    n_in = 25
    items, ex_shapes, n_g = _exchange_items([], scatters)
    ne = len(items)
    e64, e128 = _head_expanders()
    stack = pl.BlockSpec((bps, NH * LB, LB), lambda i: (nsteps - 1 - i, 0, 0))

    def body(*refs):
        i = pl.program_id(0)
        (q_ref, kp_ref, kc_ref, vp_ref, vc_ref, o_ref, do_ref, cos_ref, sin_ref, pr_ref, ps_ref,
         dyn_ref, y_ref, z_ref, x_ref, u_ref, dt_ref, acs_ref, dm_ref, cw_ref, sp_ref, nw_ref,
         hs_ref, e64_ref, e128_ref) = refs[:n_in]
        ex_in = refs[n_in:n_in + ne]
        dp_ref, ds_ref, sm_ref = refs[n_in + ne:n_in + ne + 3]
        ex_out = refs[n_in + ne + 3:n_in + 2 * ne + 3]
        ck_scr, cv_scr, dh_scr, dun_scr = refs[n_in + 2 * ne + 3:n_in + 2 * ne + 7]
        sems = refs[n_in + 2 * ne + 7:]

        @pl.when(i == 0)
        def _():
            _Exchange(n_g, ex_in, ex_out, sems).start()

        for back in range(bps):
            sub = bps - 1 - back
            step = i * bps + back
            r = slice(sub * LB, (sub + 1) * LB)
            before = slice((sub - 1) * LB, sub * LB)
            one = slice(sub, sub + 1)
            _attn_bwd_block(step, q_ref.at[r], kp_ref if sub == 0 else kc_ref.at[before], kc_ref.at[r],
                            vp_ref if sub == 0 else vc_ref.at[before], vc_ref.at[r], o_ref.at[r], do_ref.at[r],
                            cos_ref.at[r], sin_ref.at[r], pr_ref.at[one], ps_ref.at[one],
                            dp_ref.at[r, O_Q:O_K], dp_ref.at[r, O_K:O_Z], ds_ref, ck_scr, cv_scr)
            _ssd_bwd_block(step, dyn_ref.at[r], y_ref.at[r], z_ref.at[r], x_ref.at[r], u_ref.at[r], dt_ref.at[r],
                           acs_ref.at[r], dm_ref.at[one], cw_ref, sp_ref, nw_ref,
                           hs_ref.at[one], e64_ref, e128_ref, dp_ref.at[r, O_Z:INP], sm_ref, dh_scr, dun_scr)

        @pl.when(i == nsteps - 1)
        def _():
            _Exchange(n_g, ex_in, ex_out, sems).finish()

    any_spec = pl.BlockSpec(memory_space=pl.ANY)
    outs = _pcall(
        body, name="mixer_bwd", grid=(nsteps,),
        in_specs=[pl.BlockSpec((tl, AW), cur), pl.BlockSpec((LB, KVW), prev), pl.BlockSpec((tl, KVW), cur),
                  pl.BlockSpec((LB, KVW), prev), pl.BlockSpec((tl, KVW), cur), pl.BlockSpec((tl, AW), cur),
                  pl.BlockSpec((tl, AW), cur), pl.BlockSpec((tl, LB), cur), pl.BlockSpec((tl, LB), cur),
                  stack, stack,
                  pl.BlockSpec((tl, SW), cur), pl.BlockSpec((tl, SW), cur), pl.BlockSpec((tl, SW), cur),
                  pl.BlockSpec((tl, D), cur), pl.BlockSpec((tl, D), cur), pl.BlockSpec((tl, LB), cur),
                  pl.BlockSpec((tl, LB), cur), stack,
                  _const((8, D)), _const((8, LB)), _const((1, SW)),
                  pl.BlockSpec((bps, NH * HD, NST), lambda i: (nsteps - 1 - i, 0, 0)),
                  _const(e64.shape), _const(e128.shape)] + [any_spec] * ne,
        out_specs=[pl.BlockSpec((tl, INP), cur), pl.BlockSpec((8, LB), lambda i: (0, 0)),
                   pl.BlockSpec((8, D), lambda i: (0, 0))] + [any_spec] * ne,
        out_shape=[_sds((s, INP), BF16), _sds((8, LB), F32), _sds((8, D), F32)] + ex_shapes,
        scratch_shapes=[pltpu.VMEM((LB, KVW), F32), pltpu.VMEM((LB, KVW), F32),
                        pltpu.VMEM((NH * HD, NST), F32), pltpu.VMEM((SUB, D), F32)]
        + _exchange_sems(ne),
        compiler_params=_params(),
    )(q, k, k, v, v, o, do, cos, sin, probs, psink, dyn, y, z, xbc, u, dtv, acs, decay, conv_w8, ssm_p, nw, hs,
      e64, e128, *items)
    return outs[0], outs[1], outs[2], outs[3:]


def _inproj_bwd(dproj, x, dx2, mod8, n1w, w_in_t, scatters, smalls):
    s = x.shape[0]
    tt = min(512, s)
    nt = s // tt
    items, ex_shapes, n_g = _exchange_items([], scatters)
    ne = len(items)
    n_in = 10

    def body(*refs):
        dp_ref, x_ref, dx2_ref, mod_ref, nw_ref, w_ref, f_ref, b_ref, s_ref, k_ref = refs[:n_in]
        ex_in = refs[n_in:n_in + ne]
        gx_ref, sm_ref = refs[n_in + ne:n_in + 2 + ne]
        ex_out = refs[n_in + 2 + ne:n_in + 2 + 2 * ne]
        gpack_ref = refs[n_in + 2 + 2 * ne]
        pack_scr = refs[n_in + 3 + 2 * ne]
        sems = refs[n_in + 4 + 2 * ne:n_in + 7 + 2 * ne]
        pack_sems = refs[n_in + 7 + 2 * ne:]
        i = pl.program_id(0)

        @pl.when(i == 0)
        def _():
            sm_ref[...] = jnp.zeros_like(sm_ref)
            _Exchange(n_g, ex_in, ex_out, sems).start()

        w = w_ref[...]
        hr = tt // 2
        dh1 = [_dot(dp_ref[h * hr:(h + 1) * hr, :], w) for h in range(2)]
        sums = jnp.zeros((3, D), F32)
        for h in range(2):
            rows = slice(h * hr, (h + 1) * hr)
            dxn, d_shift, d_scale, d_w = _norm_mod_bwd(x_ref[rows, :], dh1[h], nw_ref[...], mod_ref[1:2, :])
            gx_ref[rows, :] = dx2_ref[rows, :] + dxn
            sums = sums + jnp.concatenate([d_shift, d_scale, d_w], axis=0)
        sm_ref[0:3, :] += sums

        @pl.when(i == nt - 1)
        def _():
            _pack_rows(f_ref, b_ref, s_ref, sm_ref, k_ref, pack_scr)
            small = _Exchange(1, [pack_scr], [gpack_ref], pack_sems)
            small.start()
            _Exchange(n_g, ex_in, ex_out, sems).finish()
            small.finish()

    any_spec = pl.BlockSpec(memory_space=pl.ANY)
    outs = _pcall(
        body, name="inproj_bwd", grid=(nt,),
        in_specs=[_rows(tt, INP), _rows(tt, D), _rows(tt, D), _const((8, D)), _const((1, D)), _const((INP, D)),
                  _const((8, D)), _const((8, D)), _const((8, D)), _const((8, LB))]
        + [any_spec] * ne,
        out_specs=[_rows(tt, D), pl.BlockSpec((8, D), lambda i: (0, 0))] + [any_spec] * (ne + 1),
        out_shape=[_sds((s, D), F32), _sds((8, D), F32)] + ex_shapes + [_sds((N_DEV, PACK_ROWS, D), F32)],
        scratch_shapes=[pltpu.VMEM((PACK_ROWS, D), F32)] + _exchange_sems(ne) + _exchange_sems(1),
        compiler_params=_params(),
    )(dproj, x, dx2, mod8, n1w, w_in_t, *smalls, *items)
    return outs[0], outs[2:2 + ne], outs[2 + ne]


def _wgrad(a, b, name):
    s, m = a.shape
    n = b.shape[1]
    tk = min(2048, s)
    wide = (1408, 1024, 512)
    tm = next((t for t in wide if m % t == 0), m)
    tn = n if n <= 2048 else _largest_divisor(n, wide)
    nk = s // tk

    def body(a_ref, b_ref, o_ref, acc):
        kk = pl.program_id(2)

        @pl.when(kk == 0)
        def _():
            acc[...] = jnp.zeros_like(acc)

        acc[...] += _dot_tn(a_ref[...], b_ref[...])

        @pl.when(kk == nk - 1)
        def _():
            o_ref[...] = acc[...].astype(BF16)

    return _pcall(
        body, name=name, grid=(m // tm, n // tn, nk),
        in_specs=[pl.BlockSpec((tk, tm), lambda i, j, kk: (kk, i)), pl.BlockSpec((tk, tn), lambda i, j, kk: (kk, j))],
        out_specs=pl.BlockSpec((tm, tn), lambda i, j, kk: (i, j)),
        out_shape=_sds((m, n), BF16),
        scratch_shapes=[pltpu.VMEM((tm, tn), F32)],
        compiler_params=_params(3),
    )(a, b)


PACK_ROWS = 24


def _pack_rows(f_ref, b_ref, s_ref, i_ref, k_ref, o_ref):
    o_ref[...] = jnp.zeros_like(o_ref)
    o_ref[0:2, :] = i_ref[0:2, :]
    o_ref[2:3, :] = b_ref[3:4, :]
    o_ref[3:5, :] = b_ref[0:2, :]
    o_ref[5:6, :] = f_ref[0:1, :]
    o_ref[6:7, :] = i_ref[2:3, :]
    o_ref[7:8, :] = b_ref[2:3, :]
    o_ref[8:9, :] = f_ref[1:2, :]
    o_ref[9:14, :] = s_ref[0:5, :]
    o_ref[14:15, :] = s_ref[5:6, :]
    o_ref[15:16, 0:3 * LB] = s_ref[6:7, 0:3 * LB]
    lane = lax.broadcasted_iota(jnp.int32, (1, LB), 1)
    sk = jnp.zeros((1, LB), F32)
    for h in range(NQ):
        sk = sk + jnp.where(lane == h, k_ref[h:h + 1, 0:1], 0.0)
    o_ref[15:16, 3 * LB:4 * LB] = sk
    o_ref[16:17, :] = f_ref[2:3, :]


def _exchange_items(gathers, scatters):
    items = list(gathers) + list(scatters)
    shapes = [_sds((N_DEV,) + a.shape, a.dtype) for a in gathers] + [_sds(a.shape, a.dtype) for a in scatters]
    return items, shapes, len(gathers)


def _exchange_sems(n):
    return [pltpu.SemaphoreType.DMA((n, N_DEV - 1)), pltpu.SemaphoreType.DMA((n, N_DEV - 1)),
            pltpu.SemaphoreType.DMA((n,))]


class _Exchange:
    def __init__(self, n_g, ins, outs, sems):
        self.n_g, self.ins, self.outs = n_g, ins, outs
        self.send_sems, self.recv_sems, self.loc_sems = sems
        xi, yi, ci = lax.axis_index("x"), lax.axis_index("y"), lax.axis_index("c")
        self.me = 4 * xi + 2 * yi + ci
        self.peers = []
        for r in range(1, N_DEV):
            px = 1 - xi if r & 4 else xi
            py = 1 - yi if r & 2 else yi
            pc = 1 - ci if r & 1 else ci
            self.peers.append(((px, py, pc), 4 * px + 2 * py + pc))

    def _copy(self, t, r, landing):
        dev, peer = self.peers[r]
        src = self.ins[t] if t < self.n_g else self.ins[t].at[peer]
        return pltpu.make_async_remote_copy(
            src_ref=src, dst_ref=self.outs[t].at[landing], send_sem=self.send_sems.at[t, r],
            recv_sem=self.recv_sems.at[t, r], device_id=dev, device_id_type=pl.DeviceIdType.MESH)

    def _local(self, t):
        src = self.ins[t] if t < self.n_g else self.ins[t].at[self.me]
        return pltpu.make_async_copy(src, self.outs[t].at[self.me], self.loc_sems.at[t])

    def start(self):
        for t in range(len(self.ins)):
            self._local(t).start()
            for r in range(N_DEV - 1):
                self._copy(t, r, self.me).start()

    def finish(self):
        n = len(self.ins)
        for t in range(n):
            for r in range(N_DEV - 1):
                self._copy(t, r, self.peers[r][1]).wait_recv()
        for t in range(n):
            for r in range(N_DEV - 1):
                self._copy(t, r, self.me).wait_send()
            self._local(t).wait()

    DIRECT = (0, 1, 3, 5)

    def two_level_start(self):
        for t in range(len(self.ins)):
            self._local(t).start()
            for r in self.DIRECT:
                self._copy(t, r, self.me).start()

    def _relay(self, t, r):
        peer = self.peers[r][1]
        return pltpu.make_async_remote_copy(
            src_ref=self.outs[t].at[peer], dst_ref=self.outs[t].at[peer], send_sem=self.send_sems.at[t, r + 1],
            recv_sem=self.recv_sems.at[t, r + 1], device_id=self.peers[0][0], device_id_type=pl.DeviceIdType.MESH)

    def two_level_relay(self):
        for t in range(len(self.ins)):
            for r in self.DIRECT[1:]:
                self._copy(t, r, self.peers[r][1]).wait_recv()
                self._relay(t, r).start()

    def two_level_finish(self):
        n = len(self.ins)
        for t in range(n):
            for r in (0, 2, 4, 6):
                self._copy(t, r, self.peers[r][1]).wait_recv()
        for t in range(n):
            for r in self.DIRECT:
                self._copy(t, r, self.me).wait_send()
            for r in self.DIRECT[1:]:
                self._relay(t, r).wait_send()
            self._local(t).wait()


def _prologue(c, w_in_tb, conv_w, w_cols, b_ada):
    ncol = w_cols.shape[1]
    c8_shape, cw8_shape = (8, c.shape[1]), (8, conv_w.shape[1])

    def body(c_ref, win_ref, cw_ref, w_ref, b_ref, gc_ref, gin_ref, gcw_ref, gmod_ref, c8_scr, cw8_scr, call_scr,
             mod_scr, loc_sem, *sems):
        c8_scr[...] = jnp.zeros_like(c8_scr)
        c8_scr[0:1, :] = c_ref[...]
        cw8_scr[...] = jnp.zeros_like(cw8_scr)
        cw8_scr[0:CONVK, :] = cw_ref[...]
        big = _Exchange(2, [win_ref, cw8_scr], [gin_ref, gcw_ref], sems[0:3])
        small = _Exchange(1, [c8_scr], [gc_ref], sems[3:6])
        small.start()
        big.two_level_start()
        small.finish()
        landed = pltpu.make_async_copy(gc_ref, call_scr, loc_sem)
        landed.start()
        landed.wait()
        cv = call_scr[:, 0, :]
        sc = (cv * _sigmoid(cv)).astype(BF16)
        bias = b_ref[:, 0:ncol]
        for dev in range(1, N_DEV):
            bias = jnp.where(small.me == dev, b_ref[:, dev * ncol:(dev + 1) * ncol], bias)
        mod_scr[...] = _dot(sc, w_ref[...].astype(BF16)) + bias
        mods = _Exchange(1, [mod_scr], [gmod_ref], sems[6:9])
        mods.start()
        mods.finish()
        big.two_level_relay()
        big.two_level_finish()

    any_spec = pl.BlockSpec(memory_space=pl.ANY)
    vmem_spec = pl.BlockSpec(memory_space=pltpu.VMEM)
    return _pcall(
        body, name="prologue", in_specs=[vmem_spec, any_spec, vmem_spec, vmem_spec, vmem_spec],
        out_specs=[any_spec] * 4,
        out_shape=[_sds((N_DEV,) + c8_shape, F32), _sds((N_DEV,) + w_in_tb.shape, BF16),
                   _sds((N_DEV,) + cw8_shape, F32), _sds((N_DEV, N_DEV, ncol), F32)],
        scratch_shapes=[pltpu.VMEM(c8_shape, F32), pltpu.VMEM(cw8_shape, F32),
                        pltpu.VMEM((N_DEV,) + c8_shape, F32), pltpu.VMEM((N_DEV, ncol), F32),
                        pltpu.SemaphoreType.DMA] + _exchange_sems(2) + _exchange_sems(1) + _exchange_sems(1),
        compiler_params=pltpu.CompilerParams(vmem_limit_bytes=VMEM_LIMIT),
    )(c, w_in_tb, conv_w, w_cols, b_ada)


def _adamw(w, g, m, v):
    m2 = ADAM_B1 * m + (1.0 - ADAM_B1) * g
    v2 = ADAM_B2 * v + (1.0 - ADAM_B2) * (g * g)
    m_hat = m2 / (1.0 - ADAM_B1 ** ADAM_STEP)
    v_hat = v2 / (1.0 - ADAM_B2 ** ADAM_STEP)
    delta = -ADAM_LR * (m_hat / (jnp.sqrt(v_hat) + ADAM_EPS) + ADAM_WD * w)
    return delta, m2, v2


def _sum_adamw(parts, w, m, v, name):
    rws, cols = w.shape
    tr = next((t for t in (256, 176, 128) if rws % t == 0), rws)

    def body(p_ref, w_ref, m_ref, v_ref, g_ref, d_ref, mo_ref, vo_ref):
        g = p_ref[0].astype(F32)
        for dev in range(1, N_DEV):
            g = g + p_ref[dev].astype(F32)
        g_ref[...] = g
        d_ref[...], mo_ref[...], vo_ref[...] = _adamw(w_ref[...], g, m_ref[...], v_ref[...])

    blk = pl.BlockSpec((tr, cols), lambda i: (i, 0))
    return _pcall(
        body, name=name, grid=(rws // tr,),
        in_specs=[pl.BlockSpec((N_DEV, tr, cols), lambda i: (0, i, 0)), blk, blk, blk],
        out_specs=[blk] * 4, out_shape=[_sds((rws, cols), F32)] * 4, compiler_params=_params(),
    )(parts, w, m, v)


def _sum_adamw_rowwise(parts, w, m, v, name):
    rws, _, cols = w.shape

    def body(p_ref, w_ref, m_ref, v_ref, g_ref, d_ref, mo_ref, vo_ref, ins, outs, sems):
        loads = [pltpu.make_async_copy(src.at[:, 0, :], ins.at[i], sems.at[i])
                 for i, src in enumerate((w_ref, m_ref, v_ref))]
        for load in loads:
            load.start()
        g = p_ref[0].astype(F32)
        for dev in range(1, N_DEV):
            g = g + p_ref[dev].astype(F32)
        for load in loads:
            load.wait()
        outs[0] = g
        outs[1], outs[2], outs[3] = _adamw(ins[0], g, ins[1], ins[2])
        stores = [pltpu.make_async_copy(outs.at[i], dst.at[:, 0, :], sems.at[len(loads) + i])
                  for i, dst in enumerate((g_ref, d_ref, mo_ref, vo_ref))]
        for store in stores:
            store.start()
        for store in stores:
            store.wait()

    any_spec = pl.BlockSpec(memory_space=pl.ANY)
    return _pcall(
        body, name=name, grid=(1,),
        in_specs=[_const((N_DEV, rws, cols)), any_spec, any_spec, any_spec],
        out_specs=[any_spec] * 4, out_shape=[_sds((rws, 1, cols), F32)] * 4,
        scratch_shapes=[pltpu.VMEM((3, rws, cols), F32), pltpu.VMEM((4, rws, cols), F32),
                        pltpu.SemaphoreType.DMA((7,))],
        compiler_params=_params(),
    )(parts, w, m, v)


def _wada_adamw(c_all, packs, w, m, v):
    rws, cols = w.shape
    tr = 256

    def body(c_ref, p_ref, w_ref, m_ref, v_ref, g_ref, d_ref, mo_ref, vo_ref, dm_scr):
        @pl.when(pl.program_id(0) == 0)
        def _():
            me = 4 * lax.axis_index("x") + 2 * lax.axis_index("y") + lax.axis_index("c")
            for b in range(N_DEV):
                flat = jnp.concatenate([p_ref[b, r:r + 1, :] for r in range(N_MOD)], axis=1)
                row = flat[:, 0:cols]
                for dev in range(1, N_DEV):
                    row = jnp.where(me == dev, flat[:, dev * cols:(dev + 1) * cols], row)
                dm_scr[b:b + 1, :] = row

        cv = c_ref[...]
        sc = (cv * _sigmoid(cv)).astype(BF16)
        g = _dot_tn(sc, dm_scr[...].astype(BF16))
        g_ref[...] = g
        d_ref[...], mo_ref[...], vo_ref[...] = _adamw(w_ref[...], g, m_ref[...], v_ref[...])

    blk = pl.BlockSpec((tr, cols), lambda i: (i, 0))
    return _pcall(
        body, name="wada_adamw", grid=(rws // tr,),
        in_specs=[pl.BlockSpec((N_DEV, tr), lambda i: (0, i)), _const(packs.shape), blk, blk, blk],
        out_specs=[blk] * 4, out_shape=[_sds((rws, cols), F32)] * 4,
        scratch_shapes=[pltpu.VMEM((N_DEV, cols), F32)], compiler_params=_params(),
    )(c_all, packs, w, m, v)


SMALL_NAMES = ("b_ada", "norm1_w", "conv_w", "conv_b", "dt_bias", "a_log", "d_skip", "attn_sinks", "ssm_norm_w",
               "norm2_w", "final_norm_w")


def _small_grads(tot, me):
    shard = D // N_DEV
    conv = tot[10:10 + CONVK, 0:shard]
    for dev in range(1, N_DEV):
        conv = jnp.where(me == dev, tot[10:10 + CONVK, dev * shard:(dev + 1) * shard], conv)
    return [
        jnp.concatenate([tot[r:r + 1, :] for r in range(N_MOD)], axis=1),
        tot[6:7, :], conv, tot[9:10, :],
        tot[15:16, 0:NH], tot[15:16, LB:LB + NH], tot[15:16, 2 * LB:2 * LB + NH], tot[15:16, 3 * LB:3 * LB + NQ],
        tot[14:15, 0:SW], tot[7:8, :], tot[8:9, :],
    ]


def _small_adamw(packs, ws, ms, vs):
    k = len(ws)

    def body(p_ref, *refs):
        w_refs, m_refs, v_refs = refs[:k], refs[k:2 * k], refs[2 * k:3 * k]
        loss_ref = refs[3 * k]
        g_refs, d_refs, mo_refs, vo_refs = (refs[3 * k + 1 + j * k:3 * k + 1 + (j + 1) * k] for j in range(4))
        tot_ref = refs[7 * k + 1]
        tot = p_ref[0]
        for dev in range(1, N_DEV):
            tot = tot + p_ref[dev]
        tot_ref[...] = tot
        loss_ref[...] = jnp.zeros((1, 1), F32) + (0.5 / D) * jnp.sum(tot[16:17, :])
        me = 4 * lax.axis_index("x") + 2 * lax.axis_index("y") + lax.axis_index("c")
        for i, g in enumerate(_small_grads(tot_ref, me)):
            g_refs[i][...] = g
            d_refs[i][...], mo_refs[i][...], vo_refs[i][...] = _adamw(w_refs[i][...], g, m_refs[i][...], v_refs[i][...])

    shp = [_sds(w.shape, F32) for w in ws]
    outs = _pcall(body, name="adamw_small", out_shape=[_sds((1, 1), F32)] + shp * 4,
                  scratch_shapes=[pltpu.VMEM((PACK_ROWS, D), F32)])(packs, *ws, *ms, *vs)
    return outs[0], outs[1:1 + k], outs[1 + k:1 + 2 * k], outs[1 + 2 * k:1 + 3 * k], outs[1 + 3 * k:]


def kernel(x, c, positions, w_ada, b_ada, norm1_w, w_in, conv_w, conv_b, dt_bias, a_log, d_skip, attn_sinks, ssm_norm_w, w_out, norm2_w, w_gate_up, w_down, final_norm_w, loss_target, m_w_ada, m_b_ada, m_norm1_w, m_w_in, m_conv_w, m_conv_b, m_dt_bias, m_a_log, m_d_skip, m_attn_sinks, m_ssm_norm_w, m_w_out, m_norm2_w, m_w_gate_up, m_w_down, m_final_norm_w, v_w_ada, v_b_ada, v_norm1_w, v_w_in, v_conv_w, v_conv_b, v_dt_bias, v_a_log, v_d_skip, v_attn_sinks, v_ssm_norm_w, v_w_out, v_norm2_w, v_w_gate_up, v_w_down, v_final_norm_w):
    s = x.shape[1]
    me = 4 * lax.axis_index("x") + 2 * lax.axis_index("y") + lax.axis_index("c")

    w_in_t = jnp.transpose(w_in[0])
    rowwise = lambda a: jnp.transpose(a, (2, 0, 1))
    w_gu_t, m_w_gu_t, v_w_gu_t = (jnp.transpose(w_gate_up[0]), jnp.transpose(m_w_gate_up[0]),
                                  jnp.transpose(v_w_gate_up[0]))
    g_c, g_in, g_cw, g_mod = _prologue(c, w_in_t.astype(BF16), conv_w[0], w_ada[0], b_ada)
    c_all = g_c[:, 0, :]
    w_in_f = jnp.pad(g_in.reshape(IN_PROJ, D), ((0, INP - IN_PROJ), (0, 0)))
    conv_w8 = jnp.transpose(g_cw, (1, 0, 2)).reshape(8, D)
    mod = lax.dynamic_index_in_dim(g_mod, me, axis=1, keepdims=False).reshape(N_MOD, D)
    mod8 = jnp.pad(mod, ((0, 8 - N_MOD), (0, 0)))

    half = HD // 2
    inv_freq = ROPE_THETA ** (-jnp.arange(half, dtype=F32) / half)
    invf = jnp.tile(inv_freq, LB // half).reshape(1, LB)
    lanes = lambda a: jnp.pad(a, ((0, 0), (0, LB - a.shape[1])))
    ssm_p = jnp.pad(jnp.concatenate([lanes(dt_bias), lanes(a_log), lanes(d_skip)], axis=0), ((0, 5), (0, 0)))
    sinks8 = jnp.broadcast_to(attn_sinks.reshape(NQ, 1), (NQ, LB))

    xs, tgt, fnw = x[0], loss_target[0], final_norm_w.reshape(1, D)

    q, k, v, z, xbc, dtr, h1, cos, sin = _inproj_fwd(xs, positions[0].reshape(s, 1), invf, mod8, norm1_w, w_in_f)
    (attn, yn, y, hs, conv_u, dtv, acs, decay, probs, psink), (g_out, g_gu, g_down) = _mixer_fwd(
        q, k, v, sinks8, xbc, conv_w8, conv_b, dtr, ssm_p, z, ssm_norm_w,
        [w_out[0].astype(BF16), w_gu_t.astype(BF16), w_down[0].astype(BF16)])
    w_out_f = g_out.reshape(D, D)
    w_gu_f = g_gu.reshape(2 * DFF, D)
    w_down_f = g_down.reshape(DFF, D)
    x2, h2, mo, mix, gu, act, dx3, sm_f = _outproj_ffn_fwd_loss(attn, yn, xs, tgt, mod8, norm2_w, fnw, w_out_f, w_gu_f,
                                                                 w_down_f)

    dx2, dff, dgu, dmix, dattn, dyn, sm_b = _ffn_bwd(dx3, gu, x2, mo, mod8, norm2_w, w_gu_f, w_down_f, w_out_f)
    p_gu = _wgrad(dgu, h2, "wgrad_gate_up").reshape(N_DEV, 2 * DFF // N_DEV, D)
    p_down = _wgrad(act, dff, "wgrad_down").reshape(N_DEV, DFF // N_DEV, D)
    p_out = _wgrad(mix, dmix, "wgrad_out").reshape(N_DEV, D // N_DEV, D)
    dproj, dsink, sm_s, (r_gu, r_down, r_out) = _mixer_bwd(
        q, k, v, attn, dattn, cos, sin, probs, psink, dyn, y, z, xbc, conv_u, dtv, acs, decay, conv_w8, ssm_p,
        ssm_norm_w, hs, [p_gu, p_down, p_out])
    p_in = _wgrad(dproj, h1, "wgrad_in")[:IN_PROJ].reshape(N_DEV, IN_PROJ // N_DEV, D)
    gx, (r_in,), g_pack = _inproj_bwd(dproj, xs, dx2, mod8, norm1_w, w_in_f, [p_in], (sm_f, sm_b, sm_s, dsink))

    big = {
        "w_ada": _wada_adamw(c_all, g_pack, w_ada[0], m_w_ada[0], v_w_ada[0]),
        "w_in": [jnp.transpose(t, (1, 2, 0))[0] for t in
                 _sum_adamw_rowwise(r_in, rowwise(w_in), rowwise(m_w_in), rowwise(v_w_in), "adamw_in")],
        "w_out": _sum_adamw(r_out, w_out[0], m_w_out[0], v_w_out[0], "adamw_out"),
        "w_gate_up": [jnp.transpose(t) for t in _sum_adamw(r_gu, w_gu_t, m_w_gu_t, v_w_gu_t, "adamw_gate_up")],
        "w_down": _sum_adamw(r_down, w_down[0], m_w_down[0], v_w_down[0], "adamw_down"),
    }
    small_w = {"b_ada": b_ada, "norm1_w": norm1_w, "conv_w": conv_w[0], "conv_b": conv_b, "dt_bias": dt_bias,
               "a_log": a_log, "d_skip": d_skip, "attn_sinks": attn_sinks, "ssm_norm_w": ssm_norm_w,
               "norm2_w": norm2_w, "final_norm_w": final_norm_w.reshape(1, D)}
    small_m = {"b_ada": m_b_ada, "norm1_w": m_norm1_w, "conv_w": m_conv_w[0], "conv_b": m_conv_b,
               "dt_bias": m_dt_bias, "a_log": m_a_log, "d_skip": m_d_skip, "attn_sinks": m_attn_sinks,
               "ssm_norm_w": m_ssm_norm_w, "norm2_w": m_norm2_w, "final_norm_w": m_final_norm_w.reshape(1, D)}
    small_v = {"b_ada": v_b_ada, "norm1_w": v_norm1_w, "conv_w": v_conv_w[0], "conv_b": v_conv_b,
               "dt_bias": v_dt_bias, "a_log": v_a_log, "d_skip": v_d_skip, "attn_sinks": v_attn_sinks,
               "ssm_norm_w": v_ssm_norm_w, "norm2_w": v_norm2_w, "final_norm_w": v_final_norm_w.reshape(1, D)}
    loss, s_g, s_d, s_m, s_v = _small_adamw(g_pack, [small_w[k] for k in SMALL_NAMES],
                                            [small_m[k] for k in SMALL_NAMES], [small_v[k] for k in SMALL_NAMES])

    order = ["w_ada", "b_ada", "norm1_w", "w_in", "conv_w", "conv_b", "dt_bias", "a_log", "d_skip", "attn_sinks",
             "ssm_norm_w", "w_out", "norm2_w", "w_gate_up", "w_down", "final_norm_w"]
    lead = {"w_ada", "w_in", "conv_w", "w_out", "w_gate_up", "w_down"}
    grads, deltas, new_m, new_v = [], [], [], []
    for name in order:
        if name in big:
            g, d, m2, v2 = big[name]
        else:
            i = SMALL_NAMES.index(name)
            g, d, m2, v2 = s_g[i], s_d[i], s_m[i], s_v[i]
        if name in lead:
            g, d, m2, v2 = g[None], d[None], m2[None], v2[None]
        if name == "final_norm_w":
            g, d, m2, v2 = g.reshape(D), d.reshape(D), m2.reshape(D), v2.reshape(D)
        grads.append(g)
        deltas.append(d)
        new_m.append(m2)
        new_v.append(v2)
    return (loss.reshape(()), gx[None], *grads, *deltas, *new_m, *new_v)
```

```python
import functools
import math

import jax
import jax.numpy as jnp
import numpy as np
from jax import lax
from jax.experimental import pallas as pl
from jax.experimental.pallas import tpu as pltpu

F32 = jnp.float32
BF16 = jnp.bfloat16

N_DEV = 8
D = 1024
HD = 64
NQ = 8
AW = 512
KVW = 128
SW = 512
NST = 128
NH = 8
LB = 128
CONVK = 4
DFF = 2816
N_MOD = 6
IN_PROJ = 2312
INP = 2432
O_Q, O_K, O_V, O_Z, O_XBC, O_DT = 0, 512, 640, 768, 1280, 2304
ZXD = INP - O_Z
EPS = 1e-6
NEG = -1e30
ROPE_THETA = 10000.0
VMEM_LIMIT = 56 * 1024 * 1024

ADAM_LR = 0.001
ADAM_B1 = 0.9
ADAM_B2 = 0.999
ADAM_EPS = 1e-08
ADAM_WD = 0.01
ADAM_STEP = 10

NT_DIMS = (((1,), (1,)), ((), ()))
TN_DIMS = (((0,), (0,)), ((), ()))


def _pcall(body, **kw):
    return pl.pallas_call(body, **kw)


def _sds(shape, dtype):
    return jax.ShapeDtypeStruct(shape, dtype)


def _params(n_grid=1):
    return pltpu.CompilerParams(dimension_semantics=("arbitrary",) * n_grid, vmem_limit_bytes=VMEM_LIMIT)


def _const(shape):
    return pl.BlockSpec(shape, lambda *_: (0,) * len(shape), pipeline_mode=pl.Buffered(1))


def _largest_divisor(n, candidates):
    for cand in candidates:
        if n % cand == 0:
            return cand
    raise ValueError(f"no tile in {candidates} divides {n}")


def _rows(t, w):
    return pl.BlockSpec((t, w), lambda i: (i, 0))


def _dot(a, b):
    return jnp.dot(a, b, preferred_element_type=F32)


def _dot_nt(a, b):
    return lax.dot_general(a, b, NT_DIMS, preferred_element_type=F32)


def _dot_tn(a, b):
    return lax.dot_general(a, b, TN_DIMS, preferred_element_type=F32)


def _sigmoid(v):
    return 1.0 / (1.0 + jnp.exp(-v))


def _softplus(v):
    return jnp.maximum(v, 0.0) + jnp.log1p(jnp.exp(-jnp.abs(v)))


def _rope_sign_mask(shape):
    lane = lax.broadcasted_iota(jnp.int32, shape, 1)
    return (lane % HD) < (HD // 2)


def _rope(t, cs, sn, inverse):
    r_dn = pltpu.roll(t, HD // 2, 1)
    r_up = pltpu.roll(t, LB - HD // 2, 1)
    first = _rope_sign_mask(t.shape)
    if inverse:
        rot = jnp.where(first, r_up, -r_dn)
    else:
        rot = jnp.where(first, -r_up, r_dn)
    return t * cs + rot * sn


def _norm_mod_fwd(xv, nw, shift, scale):
    r = lax.rsqrt(jnp.mean(xv * xv, axis=-1, keepdims=True) + EPS)
    xh = xv * r
    return (xh * nw) * (1.0 + scale) + shift


def _norm_mod_bwd(xv, dh, nw, scale):
    r = lax.rsqrt(jnp.mean(xv * xv, axis=-1, keepdims=True) + EPS)
    xh = xv * r
    xn = xh * nw
    d_shift = jnp.sum(dh, axis=0, keepdims=True)
    d_scale = jnp.sum(dh * xn, axis=0, keepdims=True)
    dxn = dh * (1.0 + scale)
    d_w = jnp.sum(dxn * xh, axis=0, keepdims=True)
    dxh = dxn * nw
    dx = r * (dxh - xh * jnp.mean(dxh * xh, axis=-1, keepdims=True))
    return dx, d_shift, d_scale, d_w


def _inproj_fwd(x, pos, invf, mod8, n1w, w_in):
    s = x.shape[0]
    tt = min(512, s)

    def body(x_ref, pos_ref, invf_ref, mod_ref, nw_ref, w_ref,
             q_ref, k_ref, v_ref, z_ref, xbc_ref, dtr_ref, h1_ref, cos_ref, sin_ref):
        h = _norm_mod_fwd(x_ref[...], nw_ref[...], mod_ref[0:1, :], mod_ref[1:2, :])
        hb = h.astype(BF16)
        h1_ref[...] = hb
        proj = _dot_nt(hb, w_ref[...])
        ang = pos_ref[...].astype(F32) * invf_ref[...]
        cs = jnp.cos(ang)
        sn = jnp.sin(ang)
        cos_ref[...] = cs
        sin_ref[...] = sn
        for a in range(AW // LB):
            q_ref[:, a * LB:(a + 1) * LB] = _rope(proj[:, O_Q + a * LB:O_Q + (a + 1) * LB], cs, sn, False).astype(BF16)
        k_ref[...] = _rope(proj[:, O_K:O_V], cs, sn, False).astype(BF16)
        v_ref[...] = proj[:, O_V:O_Z].astype(BF16)
        z_ref[...] = proj[:, O_Z:O_XBC]
        xbc_ref[...] = proj[:, O_XBC:O_DT]
        dtr_ref[...] = proj[:, O_DT:INP]

    return _pcall(
        body, name="inproj_fwd", grid=(s // tt,),
        in_specs=[_rows(tt, D), _rows(tt, 1), _const((1, LB)), _const((8, D)), _const((1, D)), _const((INP, D))],
        out_specs=[_rows(tt, AW), _rows(tt, KVW), _rows(tt, KVW), _rows(tt, SW), _rows(tt, D), _rows(tt, LB),
                   _rows(tt, D), _rows(tt, LB), _rows(tt, LB)],
        out_shape=[_sds((s, AW), BF16), _sds((s, KVW), BF16), _sds((s, KVW), BF16), _sds((s, SW), F32),
                   _sds((s, D), F32), _sds((s, LB), F32), _sds((s, D), BF16), _sds((s, LB), F32), _sds((s, LB), F32)],
        compiler_params=_params(),
    )(x, pos, invf, mod8, n1w, w_in)


QPG = 4
ATT_SCALE = 1.0 / math.sqrt(HD)


def _stack_heads(val, g):
    return jnp.concatenate([val[:, (QPG * g + hh) * HD:(QPG * g + hh + 1) * HD] for hh in range(QPG)], axis=0)


def _unstack_heads(groups):
    pieces = [grp[hh * LB:(hh + 1) * LB, :] for grp in groups for hh in range(QPG)]
    return [jnp.concatenate(pieces[2 * a:2 * a + 2], axis=1) for a in range(NQ // 2)]


def _upper_mask():
    row = lax.broadcasted_iota(jnp.int32, (QPG * LB, LB), 0)
    col = lax.broadcasted_iota(jnp.int32, (QPG * LB, LB), 1)
    return col > (row % LB)


def _sink_wide(sinks, g):
    return jnp.concatenate([jnp.broadcast_to(sinks[QPG * g + hh:QPG * g + hh + 1, 0:1], (LB, LB))
                            for hh in range(QPG)], axis=0)


def _row_sums_wide(v, terms):
    return _dot_sel(v, jnp.ones((v.shape[1], LB), BF16), terms)


def _band(upper, prev_part, cur_part):
    return jnp.where(upper, prev_part, cur_part)


def _attn_scores(n, qg, kcat, upper):
    sp = _dot_nt(qg, kcat[0:LB, :]) * ATT_SCALE
    sc = _dot_nt(qg, kcat[LB:2 * LB, :]) * ATT_SCALE
    return _band(upper, jnp.where(n > 0, sp, NEG), sc)


def _attn_softmax(comb, sink):
    m = jnp.maximum(jnp.max(comb, axis=-1, keepdims=True), sink)
    p = jnp.exp(comb - m)
    es = jnp.exp(sink - m)
    return p, es, _row_sums_wide(p, 1) + es


def _attn_fwd_block(n, q_ref, kp_ref, kc_ref, vp_ref, vc_ref, sink_ref, o_ref, pr_ref, ps_ref):
    qv = q_ref[...]
    kcat = jnp.concatenate([kp_ref[...], kc_ref[...]], axis=0)
    vcat = jnp.concatenate([vp_ref[...], vc_ref[...]], axis=0)
    sinks = sink_ref[...]
    upper = _upper_mask()
    outs = []
    for g in range(NQ // QPG):
        sl = slice(g * HD, (g + 1) * HD)
        rows = slice(g * QPG * LB, (g + 1) * QPG * LB)
        p, es, denom = _attn_softmax(_attn_scores(n, _stack_heads(qv, g), kcat[:, sl], upper), _sink_wide(sinks, g))
        rden = 1.0 / denom
        pr_ref[0, rows, :] = (p * rden).astype(BF16)
        ps_ref[0, rows, :] = (es * rden).astype(BF16)
        outs.append((_dot(jnp.where(upper, p, 0.0).astype(BF16), vcat[0:LB, sl])
                     + _dot(jnp.where(upper, 0.0, p).astype(BF16), vcat[LB:2 * LB, sl])) * rden[:, 0:HD])
    for g, grp in enumerate(outs):
        for hh in range(QPG):
            h = QPG * g + hh
            o_ref[:, h * HD:(h + 1) * HD] = grp[hh * LB:(hh + 1) * LB, :].astype(BF16)


def _cumsum_rows(a, reverse):
    row = lax.broadcasted_iota(jnp.int32, a.shape, 0)
    step = 1
    while step < LB:
        if reverse:
            a = a + jnp.where(row < LB - step, pltpu.roll(a, LB - step, 0), 0.0)
        else:
            a = a + jnp.where(row >= step, pltpu.roll(a, step, 0), 0.0)
        step *= 2
    return a


SUB = 8


def _conv_shifts(tail, cur):
    row = lax.broadcasted_iota(jnp.int32, tail.shape, 0)
    out = [cur]
    for j in range(1, CONVK):
        rolled = pltpu.roll(cur, j, 0)
        top = jnp.where(row < j, pltpu.roll(tail, j, 0), rolled[0:SUB, :])
        out.append(jnp.concatenate([top, rolled[SUB:, :]], axis=0))
    return out


def _conv_advances(du, head):
    row = lax.broadcasted_iota(jnp.int32, head.shape, 0)
    out = []
    for j in range(1, CONVK):
        rolled = pltpu.roll(du, LB - j, 0)
        bottom = jnp.where(row >= SUB - j, pltpu.roll(head, SUB - j, 0), rolled[LB - SUB:, :])
        out.append(jnp.concatenate([rolled[:LB - SUB, :], bottom], axis=0))
    return out


def _split(v, terms):
    out = []
    for _ in range(terms - 1):
        t = v.astype(BF16)
        out.append(t)
        v = v - t.astype(F32)
    out.append(v.astype(BF16))
    return out


def _dot_sel(v, sel, terms):
    parts = [_dot(t, sel) for t in _split(v, terms)]
    return functools.reduce(lambda a, b: a + b, parts)


def _dot_nt_sel(v, sel, terms):
    parts = [_dot_nt(t, sel) for t in _split(v, terms)]
    return functools.reduce(lambda a, b: a + b, parts)


def _ssd_pre(xt_ref, xc_ref, cw_ref, cb_ref, dtr_ref, sp_ref, n):
    cur = xc_ref[...]
    tail = jnp.where(n > 0, xt_ref[...], 0.0)
    sh = _conv_shifts(tail, cur)
    u = cb_ref[...] + cw_ref[CONVK - 1:CONVK, :] * sh[0]
    for j in range(1, CONVK):
        u = u + cw_ref[CONVK - 1 - j:CONVK - j, :] * sh[j]
    dt = _softplus(dtr_ref[...] + sp_ref[0:1, :])
    acs = _cumsum_rows(dt * -jnp.exp(sp_ref[1:2, :]), False)
    return u, dt, acs


def _gated_norm_fwd(y, z, sgz, nw):
    yz = y * (z * sgz)
    parts = []
    for g in range(2):
        t = yz[:, g * 256:(g + 1) * 256]
        parts.append(t * lax.rsqrt(jnp.mean(t * t, axis=-1, keepdims=True) + EPS))
    return jnp.concatenate(parts, axis=1) * nw


HPG = 4
GW = HPG * HD


class _SsdChunk:
    def __init__(self, xc, dt, acs, spv, e64, e128, decay=None):
        self.e64, self.e128 = e64, e128
        alast = acs[LB - 1:LB, :]
        self.e_all = jnp.exp(acs)
        self.dte_all = jnp.exp(alast - acs)
        self.elast = jnp.exp(alast)
        wide = _dot_sel(jnp.concatenate([dt, self.e_all, self.dte_all], axis=0), e64, 2)
        self.dt_x, self.e_x, self.dte_x = wide[0:LB], wide[LB:2 * LB], wide[2 * LB:3 * LB]
        self.dsk_x = _dot_sel(spv, e64, 3)[2:3, :]
        if decay is None:
            acs_t = jnp.transpose(acs)
            ac_x = _dot_sel(acs, e128, 3)
            row = lax.broadcasted_iota(jnp.int32, (HPG * LB, LB), 0)
            col = lax.broadcasted_iota(jnp.int32, (HPG * LB, LB), 1)
            causal = (row % LB) >= col
        lane = lax.broadcasted_iota(jnp.int32, (LB, GW), 1)
        self.head_lanes = [(lane >= hh * HD) & (lane < (hh + 1) * HD) for hh in range(HPG)]
        self.xs, self.xdt, self.b, self.c, self.bb, self.cb16, self.cbm, self.dm_st, self.m_st = ([] for _ in range(9))
        for g in range(2):
            heads = range(HPG * g, HPG * (g + 1))
            if decay is None:
                ac_st = jnp.concatenate([ac_x[:, j * LB:(j + 1) * LB] for j in heads], axis=0)
                ar_st = jnp.concatenate([jnp.broadcast_to(acs_t[j:j + 1, :], (LB, LB)) for j in heads], axis=0)
                dm_st = jnp.exp(jnp.where(causal, ac_st - ar_st, NEG))
            else:
                dm_st = decay[g]
            bg = xc[:, SW + g * NST:SW + (g + 1) * NST]
            cg = xc[:, SW + 2 * NST + g * NST:SW + 2 * NST + (g + 1) * NST]
            bgb, cgb = bg.astype(BF16), cg.astype(BF16)
            cbm = _dot_nt(cgb, bgb)
            xs_g = xc[:, g * GW:(g + 1) * GW]
            self.xs.append(xs_g)
            self.xdt.append(xs_g * self.dt_x[:, g * GW:(g + 1) * GW])
            self.b.append(bg)
            self.c.append(cg)
            self.bb.append(bgb)
            self.cb16.append(cgb)
            self.cbm.append(cbm)
            self.dm_st.append(dm_st)
            self.m_st.append(jnp.concatenate([cbm] * HPG, axis=0) * dm_st)

    def elast_rows(self, g):
        return jnp.concatenate([jnp.broadcast_to(self.elast[:, j:j + 1], (HD, NST))
                                for j in range(HPG * g, HPG * (g + 1))], axis=0)

    def diag_blocks(self, stacked):
        out = stacked[(HPG - 1) * LB:HPG * LB, :]
        for hh in range(HPG - 2, -1, -1):
            out = jnp.where(self.head_lanes[hh], stacked[hh * LB:(hh + 1) * LB, :], out)
        return out

    def block_diag(self, v):
        return jnp.concatenate([jnp.where(self.head_lanes[hh], v, 0.0) for hh in range(HPG)], axis=0)


def _ssd_fwd_block(n, xt_ref, xc_ref, cw_ref, cb_ref, dtr_ref, sp_ref, z_ref, nw_ref, e64_ref, e128_ref,
                   yn_ref, y_ref, hs_ref, u_ref, dt_ref, acs_ref, dm_ref, h_scr):
    @pl.when(n == 0)
    def _():
        h_scr[...] = jnp.zeros_like(h_scr)

    h_all = h_scr[...]
    hs_ref[0] = h_all
    u, dt, acs = _ssd_pre(xt_ref, xc_ref, cw_ref, cb_ref, dtr_ref, sp_ref, n)
    u_ref[...] = u
    dt_ref[...] = dt
    acs_ref[...] = acs
    xc = u * _sigmoid(u)
    ck = _SsdChunk(xc, dt, acs, sp_ref[...], e64_ref[...], e128_ref[...])
    dm_ref[0] = jnp.concatenate(ck.dm_st, axis=0)
    ys, hn = [], []
    for g in range(2):
        gl = slice(g * GW, (g + 1) * GW)
        xdt = ck.xdt[g]
        hg = h_all[gl, :]
        y_diag = ck.diag_blocks(_dot(ck.m_st[g].astype(BF16), xdt.astype(BF16)))
        y_off = ck.e_x[:, gl] * _dot_nt(ck.cb16[g], hg.astype(BF16))
        ys.append(y_diag + y_off + ck.xs[g] * ck.dsk_x[:, gl])
        hn.append(hg * ck.elast_rows(g) + _dot_tn((xdt * ck.dte_x[:, gl]).astype(BF16), ck.bb[g]))
    h_scr[...] = jnp.concatenate(hn, axis=0)
    y = jnp.concatenate(ys, axis=1)
    y_ref[...] = y
    z = z_ref[...]
    yn_ref[...] = _gated_norm_fwd(y, z, _sigmoid(z), nw_ref[...]).astype(BF16)


def _mixer_fwd(q, k, v, sinks8, xbc, conv_w8, conv_b, dtr, ssm_p, z, nw, gathers):
    s = q.shape[0]
    nb = s // LB
    bps = _blocks_per_step(nb)
    nsteps = nb // bps
    tl = bps * LB
    cur = lambda n: (n, 0)
    prev = lambda n: (jnp.maximum(n * bps - 1, 0), 0)
    items, ex_shapes, n_g = _exchange_items(gathers, [])
    ne = len(items)

    n_in, n_out = 16, 10
    relay_step = (3 * (nsteps - 1)) // 4
    stack = pl.BlockSpec((bps, NH * LB, LB), lambda n: (n, 0, 0))
    e64, e128 = _head_expanders()

    def body(*refs):
        (q_ref, kp_ref, kc_ref, vp_ref, vc_ref, sink_ref, xt_ref, xc_ref, cw_ref, cb_ref, dtr_ref, sp_ref, z_ref,
         nw_ref, e64_ref, e128_ref) = refs[:n_in]
        ex_in = refs[n_in:n_in + ne]
        (o_ref, yn_ref, y_ref, hs_ref, u_ref, dt_ref, acs_ref, dm_ref, pr_ref,
         ps_ref) = refs[n_in + ne:n_in + n_out + ne]
        ex_out = refs[n_in + n_out + ne:n_in + n_out + 2 * ne]
        h_scr = refs[n_in + n_out + 2 * ne]
        sems = refs[n_in + n_out + 1 + 2 * ne:]
        n = pl.program_id(0)

        @pl.when(n == 0)
        def _():
            _Exchange(n_g, ex_in, ex_out, sems).two_level_start()

        for sub in range(bps):
            blk = n * bps + sub
            r = slice(sub * LB, (sub + 1) * LB)
            before = slice((sub - 1) * LB, sub * LB)
            one = slice(sub, sub + 1)
            _attn_fwd_block(blk, q_ref.at[r], kp_ref if sub == 0 else kc_ref.at[before], kc_ref.at[r],
                            vp_ref if sub == 0 else vc_ref.at[before], vc_ref.at[r], sink_ref,
                            o_ref.at[r], pr_ref.at[one], ps_ref.at[one])
            _ssd_fwd_block(blk, xt_ref if sub == 0 else xc_ref.at[sub * LB - SUB:sub * LB], xc_ref.at[r], cw_ref,
                           cb_ref, dtr_ref.at[r], sp_ref, z_ref.at[r], nw_ref, e64_ref, e128_ref,
                           yn_ref.at[r], y_ref.at[r], hs_ref.at[one], u_ref.at[r], dt_ref.at[r], acs_ref.at[r],
                           dm_ref.at[one], h_scr)

        @pl.when(n == relay_step)
        def _():
            _Exchange(n_g, ex_in, ex_out, sems).two_level_relay()

        @pl.when(n == nsteps - 1)
        def _():
            _Exchange(n_g, ex_in, ex_out, sems).two_level_finish()

    any_spec = pl.BlockSpec(memory_space=pl.ANY)
    tail = pl.BlockSpec((SUB, D), lambda n: (jnp.maximum(n * (tl // SUB) - 1, 0), 0))
    outs = _pcall(
        body, name="mixer_fwd", grid=(nsteps,),
        in_specs=[pl.BlockSpec((tl, AW), cur), pl.BlockSpec((LB, KVW), prev), pl.BlockSpec((tl, KVW), cur),
                  pl.BlockSpec((LB, KVW), prev), pl.BlockSpec((tl, KVW), cur), _const((8, LB)),
                  tail, pl.BlockSpec((tl, D), cur), _const((8, D)), _const((1, D)),
                  pl.BlockSpec((tl, LB), cur), _const((8, LB)), pl.BlockSpec((tl, SW), cur), _const((1, SW)),
                  _const(e64.shape), _const(e128.shape)]
        + [any_spec] * ne,
        out_specs=[pl.BlockSpec((tl, AW), cur), pl.BlockSpec((tl, SW), cur), pl.BlockSpec((tl, SW), cur),
                   pl.BlockSpec((bps, NH * HD, NST), lambda n: (n, 0, 0)), pl.BlockSpec((tl, D), cur),
                   pl.BlockSpec((tl, LB), cur), pl.BlockSpec((tl, LB), cur),
                   stack, stack, stack] + [any_spec] * ne,
        out_shape=[_sds((s, AW), BF16), _sds((s, SW), BF16), _sds((s, SW), F32), _sds((nb, NH * HD, NST), F32),
                   _sds((s, D), F32), _sds((s, LB), F32), _sds((s, LB), F32), _sds((nb, NH * LB, LB), F32),
                   _sds((nb, NH * LB, LB), BF16), _sds((nb, NH * LB, LB), BF16)]
        + ex_shapes,
        scratch_shapes=[pltpu.VMEM((NH * HD, NST), F32)] + _exchange_sems(ne),
        compiler_params=_params(),
    )(q, k, k, v, v, sinks8, xbc, xbc, conv_w8, conv_b, dtr, ssm_p, z, nw, e64, e128, *items)
    return outs[:n_out], outs[n_out:]


def _blocks_per_step(nb):
    return next(b for b in (4, 2, 1) if nb % b == 0)


def _head_expanders():
    j = np.arange(LB)[:, None]
    e64 = (np.arange(NH * HD)[None, :] // HD == j).astype(BF16)
    e128 = (np.arange(NH * LB)[None, :] // LB == j).astype(BF16)
    return jnp.asarray(e64), jnp.asarray(e128)


def _outproj_ffn_fwd_loss(attn, yn, x, tgt, mod8, n2w, fnw, w_out, w_gu_t, w_down):
    s = x.shape[0]
    tf = min(256, s)

    def body(a_ref, y_ref, x_ref, t_ref, mod_ref, nw_ref, fw_ref, wo_ref, wgu_ref, wd_ref,
             x2_ref, h2_ref, mo_ref, mix_ref, gu_ref, act_ref, dx3_ref, sm_ref):
        i = pl.program_id(0)

        @pl.when(i == 0)
        def _():
            sm_ref[...] = jnp.zeros_like(sm_ref)

        mix = jnp.concatenate([a_ref[...], y_ref[...]], axis=1)
        mix_ref[...] = mix
        mo = _dot(mix, wo_ref[...])
        mo_ref[...] = mo.astype(BF16)
        x2 = x_ref[...] + mod_ref[2:3, :] * mo
        x2_ref[...] = x2
        h2 = _norm_mod_fwd(x2, nw_ref[...], mod_ref[3:4, :], mod_ref[4:5, :]).astype(BF16)
        h2_ref[...] = h2
        gu = _dot_nt(h2, wgu_ref[...])
        gu_ref[...] = gu.astype(BF16)
        g = gu[:, :DFF]
        act = (g * _sigmoid(g) * gu[:, DFF:]).astype(BF16)
        act_ref[...] = act
        ff = _dot(act, wd_ref[...])
        x3 = x2 + mod_ref[5:6, :] * ff
        r = lax.rsqrt(jnp.mean(x3 * x3, axis=-1, keepdims=True) + EPS)
        xh = x3 * r
        fw = fw_ref[...]
        err = xh * fw - t_ref[...]
        dy = err * (1.0 / D)
        dxh = dy * fw
        dx3 = r * (dxh - xh * jnp.mean(dxh * xh, axis=-1, keepdims=True))
        dx3_ref[...] = dx3
        sm_ref[0:1, :] += jnp.sum(dx3 * ff, axis=0, keepdims=True)
        sm_ref[1:2, :] += jnp.sum(dy * xh, axis=0, keepdims=True)
        sm_ref[2:3, :] += jnp.sum(err * err, axis=0, keepdims=True)

    return _pcall(
        body, name="outproj_ffn_fwd_loss", grid=(s // tf,),
        in_specs=[_rows(tf, AW), _rows(tf, SW), _rows(tf, D), _rows(tf, D), _const((8, D)), _const((1, D)),
                  _const((1, D)), _const((D, D)), _const((2 * DFF, D)), _const((DFF, D))],
        out_specs=[_rows(tf, D), _rows(tf, D), _rows(tf, D), _rows(tf, D), _rows(tf, 2 * DFF), _rows(tf, DFF),
                   _rows(tf, D), pl.BlockSpec((8, D), lambda i: (0, 0))],
        out_shape=[_sds((s, D), F32), _sds((s, D), BF16), _sds((s, D), BF16), _sds((s, D), BF16),
                   _sds((s, 2 * DFF), BF16), _sds((s, DFF), BF16), _sds((s, D), F32), _sds((8, D), F32)],
        compiler_params=_params(),
    )(attn, yn, x, tgt, mod8, n2w, fnw, w_out, w_gu_t, w_down)


def _ffn_bwd(dx3, gu, x2, mixout, mod8, n2w, w_gu, w_down, w_out):
    s = x2.shape[0]
    tb = min(256, s)

    def body(dx3_ref, gu_ref, x2_ref, mo_ref, mod_ref, nw_ref, wgu_ref, wd_ref, wo_ref,
             dx2_ref, dff_ref, dgu_ref, dmix_ref, dattn_ref, dyn_ref, sm_ref):
        i = pl.program_id(0)

        @pl.when(i == 0)
        def _():
            sm_ref[...] = jnp.zeros_like(sm_ref)

        dx3 = dx3_ref[...]
        dff = (dx3 * mod_ref[5:6, :]).astype(BF16)
        dff_ref[...] = dff
        dact = _dot_nt(dff, wd_ref[...])
        g = gu_ref[:, :DFF].astype(F32)
        u = gu_ref[:, DFF:].astype(F32)
        sg = _sigmoid(g)
        dgu = jnp.concatenate([dact * u * sg * (1.0 + g * (1.0 - sg)), dact * g * sg], axis=1).astype(BF16)
        dgu_ref[...] = dgu
        dh2 = _dot(dgu, wgu_ref[...])
        dxn, d_shift, d_scale, d_w = _norm_mod_bwd(x2_ref[...], dh2, nw_ref[...], mod_ref[4:5, :])
        dx2 = dx3 + dxn
        dx2_ref[...] = dx2
        sm_ref[0:1, :] += d_shift
        sm_ref[1:2, :] += d_scale
        sm_ref[2:3, :] += d_w
        sm_ref[3:4, :] += jnp.sum(dx2 * mo_ref[...].astype(F32), axis=0, keepdims=True)
        dmix = (dx2 * mod_ref[2:3, :]).astype(BF16)
        dmix_ref[...] = dmix
        dmi = _dot_nt(dmix, wo_ref[...])
        dattn_ref[...] = dmi[:, :AW].astype(BF16)
        dyn_ref[...] = dmi[:, AW:]

    return _pcall(
        body, name="ffn_bwd", grid=(s // tb,),
        in_specs=[_rows(tb, D), _rows(tb, 2 * DFF), _rows(tb, D), _rows(tb, D), _const((8, D)), _const((1, D)),
                  _const((2 * DFF, D)), _const((DFF, D)), _const((D, D))],
        out_specs=[_rows(tb, D), _rows(tb, D), _rows(tb, 2 * DFF), _rows(tb, D), _rows(tb, AW), _rows(tb, SW),
                   pl.BlockSpec((8, D), lambda i: (0, 0))],
        out_shape=[_sds((s, D), F32), _sds((s, D), BF16), _sds((s, 2 * DFF), BF16), _sds((s, D), BF16),
                   _sds((s, AW), BF16), _sds((s, SW), F32), _sds((8, D), F32)],
        compiler_params=_params(),
    )(dx3, gu, x2, mixout, mod8, n2w, w_gu, w_down, w_out)


def _ssd_bwd_block(i, *refs):
    def run(dyn_ref, y_ref, z_ref, x_ref, u_ref, dt_ref, acs_ref, dm_ref, cw_ref, sp_ref, nw_ref,
            hs_ref, e64_ref, e128_ref, dzxd_ref, sm_ref, dh_scr, dun_scr):
        @pl.when(i == 0)
        def _():
            dh_scr[...] = jnp.zeros_like(dh_scr)
            dun_scr[...] = jnp.zeros_like(dun_scr)
            sm_ref[...] = jnp.zeros_like(sm_ref)

        u, dt, acs = u_ref[...], dt_ref[...], acs_ref[...]
        sg_u = _sigmoid(u)
        xc = u * sg_u
        a_neg = -jnp.exp(sp_ref[1:2, :])
        ck = _SsdChunk(xc, dt, acs, sp_ref[...], e64_ref[...], e128_ref[...],
                       decay=[dm_ref[0, g * HPG * LB:(g + 1) * HPG * LB, :] for g in range(2)])
        h_all = hs_ref[0]
        dh_all = dh_scr[...]
        riota = lax.broadcasted_iota(jnp.int32, (LB, LB), 0)
        lane1 = lax.broadcasted_iota(jnp.int32, (1, LB), 1)

        z = z_ref[...]
        y = y_ref[...]
        sgz = _sigmoid(z)
        sz = z * sgz
        yz = y * sz
        nwv = nw_ref[...]
        dyn_v = dyn_ref[...]
        dyhat = dyn_v * nwv
        yhat_parts, dyz_parts = [], []
        for g in range(2):
            gs = slice(g * 256, (g + 1) * 256)
            t = yz[:, gs]
            rg = lax.rsqrt(jnp.mean(t * t, axis=-1, keepdims=True) + EPS)
            yh = t * rg
            dyh = dyhat[:, gs]
            yhat_parts.append(yh)
            dyz_parts.append(rg * (dyh - yh * jnp.mean(dyh * yh, axis=-1, keepdims=True)))
        yhat = jnp.concatenate(yhat_parts, axis=1)
        dyz = jnp.concatenate(dyz_parts, axis=1)
        sm_ref[5:6, 0:SW] += jnp.sum(dyn_v * yhat, axis=0, keepdims=True)
        dy = dyz * sz
        dzxd_ref[:, 0:SW] = (dyz * y * sgz * (1.0 + z * (1.0 - sgz))).astype(BF16)

        cat = lambda parts: jnp.concatenate(parts, axis=1)
        dxs, dbs, dcs, dhp, g_cat, de_x, ddte_x, ddt_x, ddsk_x = ([] for _ in range(9))
        dacs_t = jnp.zeros((LB, LB), F32)
        hsum = jnp.zeros((1, LB), F32)
        for g in range(2):
            gl = slice(g * GW, (g + 1) * GW)
            xs_g, xdt, bgb, cgb = ck.xs[g], ck.xdt[g], ck.bb[g], ck.cb16[g]
            m_st, dm_st = ck.m_st[g], ck.dm_st[g]
            dt_x, e_x, dte_x = ck.dt_x[:, gl], ck.e_x[:, gl], ck.dte_x[:, gl]
            xdtb = xdt.astype(BF16)
            hg, dhn = h_all[gl, :], dh_all[gl, :]
            hb, dhnb = hg.astype(BF16), dhn.astype(BF16)
            dy_g = dy[:, gl]
            ddsk_x.append(jnp.sum(dy_g * xs_g, axis=0, keepdims=True))
            dy_bd = ck.block_diag(dy_g).astype(BF16)
            dm4 = _dot_nt(dy_bd, xdtb)
            dxdt = _dot_tn(m_st.astype(BF16), dy_bd)
            gmat = dm4 * m_st
            dcbm = dm4 * dm_st
            dcb = dcbm[0:LB] + dcbm[LB:2 * LB] + dcbm[2 * LB:3 * LB] + dcbm[3 * LB:4 * LB]
            g_cat.append(cat([gmat[hh * LB:(hh + 1) * LB, :] for hh in range(HPG)]))
            for hh in range(HPG):
                j = HPG * g + hh
                col_sum = jnp.sum(gmat[hh * LB:(hh + 1) * LB, :], axis=0, keepdims=True)
                dacs_t = dacs_t - jnp.where(riota == j, col_sum, 0.0)
                hsl = slice(hh * HD, (hh + 1) * HD)
                hsum = hsum + jnp.where(lane1 == j, jnp.sum(dhn[hsl, :] * hg[hsl, :]), 0.0)
            dchb = (dy_g * e_x).astype(BF16)
            dcg = _dot(dchb, hb)
            dh_prev = _dot_tn(dchb, cgb)
            de_x.append(dy_g * _dot_nt(cgb, hb))
            dxs_s = _dot_nt(bgb, dhnb)
            dbg = _dot((xdt * dte_x).astype(BF16), dhnb)
            dxdt = dxdt + dxs_s * dte_x
            ddte_x.append(dxs_s * xdt)
            dhp.append(dhn * ck.elast_rows(g) + dh_prev)
            dxs.append(dy_g * ck.dsk_x[:, gl] + dxdt * dt_x)
            ddt_x.append(dxdt * xs_g)
            dcbb = dcb.astype(BF16)
            dbs.append(dbg + _dot_tn(dcbb, cgb))
            dcs.append(dcg + _dot(dcbb, bgb))
        dh_scr[...] = jnp.concatenate(dhp, axis=0)
        red = _dot_nt_sel(jnp.concatenate([cat(de_x), cat(ddte_x), cat(ddt_x)], axis=0), ck.e64, 1)
        de_c, ddte_c, ddt_c = red[0:LB], red[LB:2 * LB], red[2 * LB:3 * LB]
        ddsk = _dot_nt_sel(jnp.broadcast_to(cat(ddsk_x), (SUB, NH * HD)), ck.e64, 2)[0:1, :]
        t1 = ddte_c * ck.dte_all
        dalast = jnp.sum(t1, axis=0, keepdims=True) + hsum * ck.elast
        dacs = (_dot_nt_sel(cat(g_cat), ck.e128, 2) + de_c * ck.e_all - t1 + jnp.transpose(dacs_t)
                + jnp.where(riota == LB - 1, dalast, 0.0))
        da = _cumsum_rows(dacs, True)
        ddt = ddt_c + da * a_neg
        da_log = jnp.sum(da * dt, axis=0, keepdims=True) * a_neg
        ddtr = ddt * (1.0 - jnp.exp(-dt))
        dzxd_ref[:, SW + D:ZXD] = ddtr.astype(BF16)
        sm_ref[6:7, 0:LB] += jnp.sum(ddtr, axis=0, keepdims=True)
        sm_ref[6:7, LB:2 * LB] += da_log
        sm_ref[6:7, 2 * LB:3 * LB] += ddsk

        du = cat(dxs + dbs + dcs) * (sg_u * (1.0 + u * (1.0 - sg_u)))
        xv = x_ref[...]
        adv = [du] + _conv_advances(du, dun_scr[...])
        sm_ref[0:1, :] += jnp.sum(du, axis=0, keepdims=True)
        dxbc = cw_ref[CONVK - 1:CONVK, :] * du
        for j in range(CONVK):
            sm_ref[CONVK - j:CONVK + 1 - j, :] += jnp.sum(adv[j] * xv, axis=0, keepdims=True)
            if j:
                dxbc = dxbc + cw_ref[CONVK - 1 - j:CONVK - j, :] * adv[j]
        dun_scr[...] = du[0:SUB, :]
        dzxd_ref[:, SW:SW + D] = dxbc.astype(BF16)

    run(*refs)


def _attn_bwd_block(i, q_ref, kp_ref, kc_ref, vp_ref, vc_ref, o_ref, do_ref, cos_ref, sin_ref, pr_ref, ps_ref,
                    dq_ref, dkv_ref, ds_ref, ck_scr, cv_scr):
    @pl.when(i == 0)
    def _():
        ds_ref[...] = jnp.zeros_like(ds_ref)
        ck_scr[...] = jnp.zeros_like(ck_scr)
        cv_scr[...] = jnp.zeros_like(cv_scr)

    qv, ov, dov = q_ref[...], o_ref[...], do_ref[...]
    kcat = jnp.concatenate([kp_ref[...], kc_ref[...]], axis=0)
    vcat = jnp.concatenate([vp_ref[...], vc_ref[...]], axis=0)
    upper = _upper_mask()
    srow = lax.broadcasted_iota(jnp.int32, (8, LB), 0)
    slane = lax.broadcasted_iota(jnp.int32, (8, LB), 1)
    dsink = jnp.zeros((8, LB), F32)
    dq_g, dk_g, dv_g = [], [], []
    for g in range(NQ // QPG):
        sl = slice(g * HD, (g + 1) * HD)
        qg = _stack_heads(qv, g)
        dog = _stack_heads(dov, g)
        rows = slice(g * QPG * LB, (g + 1) * QPG * LB)
        probs = pr_ref[0, rows, :].astype(F32)
        psink = ps_ref[0, rows, :].astype(F32)
        delta = _row_sums_wide(dog.astype(F32) * _stack_heads(ov, g).astype(F32), 2)
        dsc = probs * (_band(upper, _dot_nt(dog, vcat[0:LB, sl]), _dot_nt(dog, vcat[LB:2 * LB, sl])) - delta)
        sink_terms = (psink * delta)[:, 0:1]
        for hh in range(QPG):
            dsink = dsink - jnp.where((srow == QPG * g + hh) & (slane == 0),
                                      jnp.sum(sink_terms[hh * LB:(hh + 1) * LB, :]), 0.0)
        ds_p = jnp.where(upper, dsc, 0.0).astype(BF16)
        ds_c = jnp.where(upper, 0.0, dsc).astype(BF16)
        dq_g.append((_dot(ds_p, kcat[0:LB, sl]) + _dot(ds_c, kcat[LB:2 * LB, sl])) * ATT_SCALE)
        dk_g.append(jnp.concatenate([_dot_tn(ds_p, qg), _dot_tn(ds_c, qg)], axis=0) * ATT_SCALE)
        dv_g.append(jnp.concatenate([_dot_tn(jnp.where(upper, probs, 0.0).astype(BF16), dog),
                                     _dot_tn(jnp.where(upper, 0.0, probs).astype(BF16), dog)], axis=0))
    ds_ref[...] += dsink
    cs = cos_ref[...]
    sn = sin_ref[...]
    dk2 = jnp.concatenate(dk_g, axis=1)
    dv2 = jnp.concatenate(dv_g, axis=1)
    for a, tile in enumerate(_unstack_heads(dq_g)):
        dq_ref[:, a * LB:(a + 1) * LB] = _rope(tile, cs, sn, True).astype(BF16)
    dkv_ref[:, 0:KVW] = _rope(ck_scr[...] + dk2[LB:2 * LB, :], cs, sn, True).astype(BF16)
    dkv_ref[:, KVW:2 * KVW] = (cv_scr[...] + dv2[LB:2 * LB, :]).astype(BF16)
    ck_scr[...] = dk2[0:LB, :]
    cv_scr[...] = dv2[0:LB, :]


def _mixer_bwd(q, k, v, o, do, cos, sin, probs, psink, dyn, y, z, xbc, u, dtv, acs, decay, conv_w8, ssm_p, nw, hs,
               scatters):
    s = q.shape[0]
    nb = s // LB
    bps = _blocks_per_step(nb)
    nsteps = nb // bps
    tl = bps * LB
    cur = lambda i: (nsteps - 1 - i, 0)
    prev = lambda i: (jnp.maximum((nsteps - 1 - i) * bps - 1, 0), 0)
    n_in = 25
    items, ex_shapes, n_g = _exchange_items([], scatters)
    ne = len(items)
    e64, e128 = _head_expanders()
    stack = pl.BlockSpec((bps, NH * LB, LB), lambda i: (nsteps - 1 - i, 0, 0))

    def body(*refs):
        i = pl.program_id(0)
        (q_ref, kp_ref, kc_ref, vp_ref, vc_ref, o_ref, do_ref, cos_ref, sin_ref, pr_ref, ps_ref,
         dyn_ref, y_ref, z_ref, x_ref, u_ref, dt_ref, acs_ref, dm_ref, cw_ref, sp_ref, nw_ref,
         hs_ref, e64_ref, e128_ref) = refs[:n_in]
        ex_in = refs[n_in:n_in + ne]
        dp_ref, ds_ref, sm_ref = refs[n_in + ne:n_in + ne + 3]
        ex_out = refs[n_in + ne + 3:n_in + 2 * ne + 3]
        ck_scr, cv_scr, dh_scr, dun_scr = refs[n_in + 2 * ne + 3:n_in + 2 * ne + 7]
        sems = refs[n_in + 2 * ne + 7:]

        @pl.when(i == 0)
        def _():
            _Exchange(n_g, ex_in, ex_out, sems).start()

        for back in range(bps):
            sub = bps - 1 - back
            step = i * bps + back
            r = slice(sub * LB, (sub + 1) * LB)
            before = slice((sub - 1) * LB, sub * LB)
            one = slice(sub, sub + 1)
            _attn_bwd_block(step, q_ref.at[r], kp_ref if sub == 0 else kc_ref.at[before], kc_ref.at[r],
                            vp_ref if sub == 0 else vc_ref.at[before], vc_ref.at[r], o_ref.at[r], do_ref.at[r],
                            cos_ref.at[r], sin_ref.at[r], pr_ref.at[one], ps_ref.at[one],
                            dp_ref.at[r, O_Q:O_K], dp_ref.at[r, O_K:O_Z], ds_ref, ck_scr, cv_scr)
            _ssd_bwd_block(step, dyn_ref.at[r], y_ref.at[r], z_ref.at[r], x_ref.at[r], u_ref.at[r], dt_ref.at[r],
                           acs_ref.at[r], dm_ref.at[one], cw_ref, sp_ref, nw_ref,
                           hs_ref.at[one], e64_ref, e128_ref, dp_ref.at[r, O_Z:INP], sm_ref, dh_scr, dun_scr)

        @pl.when(i == nsteps - 1)
        def _():
            _Exchange(n_g, ex_in, ex_out, sems).finish()

    any_spec = pl.BlockSpec(memory_space=pl.ANY)
    outs = _pcall(
        body, name="mixer_bwd", grid=(nsteps,),
        in_specs=[pl.BlockSpec((tl, AW), cur), pl.BlockSpec((LB, KVW), prev), pl.BlockSpec((tl, KVW), cur),
                  pl.BlockSpec((LB, KVW), prev), pl.BlockSpec((tl, KVW), cur), pl.BlockSpec((tl, AW), cur),
                  pl.BlockSpec((tl, AW), cur), pl.BlockSpec((tl, LB), cur), pl.BlockSpec((tl, LB), cur),
                  stack, stack,
                  pl.BlockSpec((tl, SW), cur), pl.BlockSpec((tl, SW), cur), pl.BlockSpec((tl, SW), cur),
                  pl.BlockSpec((tl, D), cur), pl.BlockSpec((tl, D), cur), pl.BlockSpec((tl, LB), cur),
                  pl.BlockSpec((tl, LB), cur), stack,
                  _const((8, D)), _const((8, LB)), _const((1, SW)),
                  pl.BlockSpec((bps, NH * HD, NST), lambda i: (nsteps - 1 - i, 0, 0)),
                  _const(e64.shape), _const(e128.shape)] + [any_spec] * ne,
        out_specs=[pl.BlockSpec((tl, INP), cur), pl.BlockSpec((8, LB), lambda i: (0, 0)),
                   pl.BlockSpec((8, D), lambda i: (0, 0))] + [any_spec] * ne,
        out_shape=[_sds((s, INP), BF16), _sds((8, LB), F32), _sds((8, D), F32)] + ex_shapes,
        scratch_shapes=[pltpu.VMEM((LB, KVW), F32), pltpu.VMEM((LB, KVW), F32),
                        pltpu.VMEM((NH * HD, NST), F32), pltpu.VMEM((SUB, D), F32)]
        + _exchange_sems(ne),
        compiler_params=_params(),
    )(q, k, k, v, v, o, do, cos, sin, probs, psink, dyn, y, z, xbc, u, dtv, acs, decay, conv_w8, ssm_p, nw, hs,
      e64, e128, *items)
    return outs[0], outs[1], outs[2], outs[3:]


def _inproj_bwd(dproj, x, dx2, mod8, n1w, w_in_t, scatters, smalls):
    s = x.shape[0]
    tt = min(512, s)
    nt = s // tt
    items, ex_shapes, n_g = _exchange_items([], scatters)
    ne = len(items)
    n_in = 10

    def body(*refs):
        dp_ref, x_ref, dx2_ref, mod_ref, nw_ref, w_ref, f_ref, b_ref, s_ref, k_ref = refs[:n_in]
        ex_in = refs[n_in:n_in + ne]
        gx_ref, sm_ref = refs[n_in + ne:n_in + 2 + ne]
        ex_out = refs[n_in + 2 + ne:n_in + 2 + 2 * ne]
        gpack_ref = refs[n_in + 2 + 2 * ne]
        pack_scr = refs[n_in + 3 + 2 * ne]
        sems = refs[n_in + 4 + 2 * ne:n_in + 7 + 2 * ne]
        pack_sems = refs[n_in + 7 + 2 * ne:]
        i = pl.program_id(0)

        @pl.when(i == 0)
        def _():
            sm_ref[...] = jnp.zeros_like(sm_ref)
            _Exchange(n_g, ex_in, ex_out, sems).start()

        w = w_ref[...]
        hr = tt // 2
        dh1 = [_dot(dp_ref[h * hr:(h + 1) * hr, :], w) for h in range(2)]
        sums = jnp.zeros((3, D), F32)
        for h in range(2):
            rows = slice(h * hr, (h + 1) * hr)
            dxn, d_shift, d_scale, d_w = _norm_mod_bwd(x_ref[rows, :], dh1[h], nw_ref[...], mod_ref[1:2, :])
            gx_ref[rows, :] = dx2_ref[rows, :] + dxn
            sums = sums + jnp.concatenate([d_shift, d_scale, d_w], axis=0)
        sm_ref[0:3, :] += sums

        @pl.when(i == nt - 1)
        def _():
            _pack_rows(f_ref, b_ref, s_ref, sm_ref, k_ref, pack_scr)
            small = _Exchange(1, [pack_scr], [gpack_ref], pack_sems)
            small.start()
            _Exchange(n_g, ex_in, ex_out, sems).finish()
            small.finish()

    any_spec = pl.BlockSpec(memory_space=pl.ANY)
    outs = _pcall(
        body, name="inproj_bwd", grid=(nt,),
        in_specs=[_rows(tt, INP), _rows(tt, D), _rows(tt, D), _const((8, D)), _const((1, D)), _const((INP, D)),
                  _const((8, D)), _const((8, D)), _const((8, D)), _const((8, LB))]
        + [any_spec] * ne,
        out_specs=[_rows(tt, D), pl.BlockSpec((8, D), lambda i: (0, 0))] + [any_spec] * (ne + 1),
        out_shape=[_sds((s, D), F32), _sds((8, D), F32)] + ex_shapes + [_sds((N_DEV, PACK_ROWS, D), F32)],
        scratch_shapes=[pltpu.VMEM((PACK_ROWS, D), F32)] + _exchange_sems(ne) + _exchange_sems(1),
        compiler_params=_params(),
    )(dproj, x, dx2, mod8, n1w, w_in_t, *smalls, *items)
    return outs[0], outs[2:2 + ne], outs[2 + ne]


def _wgrad(a, b, name):
    s, m = a.shape
    n = b.shape[1]
    tk = min(2048, s)
    wide = (1408, 1024, 512)
    tm = next((t for t in wide if m % t == 0), m)
    tn = n if n <= 2048 else _largest_divisor(n, wide)
    nk = s // tk

    def body(a_ref, b_ref, o_ref, acc):
        kk = pl.program_id(2)

        @pl.when(kk == 0)
        def _():
            acc[...] = jnp.zeros_like(acc)

        acc[...] += _dot_tn(a_ref[...], b_ref[...])

        @pl.when(kk == nk - 1)
        def _():
            o_ref[...] = acc[...].astype(BF16)

    return _pcall(
        body, name=name, grid=(m // tm, n // tn, nk),
        in_specs=[pl.BlockSpec((tk, tm), lambda i, j, kk: (kk, i)), pl.BlockSpec((tk, tn), lambda i, j, kk: (kk, j))],
        out_specs=pl.BlockSpec((tm, tn), lambda i, j, kk: (i, j)),
        out_shape=_sds((m, n), BF16),
        scratch_shapes=[pltpu.VMEM((tm, tn), F32)],
        compiler_params=_params(3),
    )(a, b)


PACK_ROWS = 24


def _pack_rows(f_ref, b_ref, s_ref, i_ref, k_ref, o_ref):
    o_ref[...] = jnp.zeros_like(o_ref)
    o_ref[0:2, :] = i_ref[0:2, :]
    o_ref[2:3, :] = b_ref[3:4, :]
    o_ref[3:5, :] = b_ref[0:2, :]
    o_ref[5:6, :] = f_ref[0:1, :]
    o_ref[6:7, :] = i_ref[2:3, :]
    o_ref[7:8, :] = b_ref[2:3, :]
    o_ref[8:9, :] = f_ref[1:2, :]
    o_ref[9:14, :] = s_ref[0:5, :]
    o_ref[14:15, :] = s_ref[5:6, :]
    o_ref[15:16, 0:3 * LB] = s_ref[6:7, 0:3 * LB]
    lane = lax.broadcasted_iota(jnp.int32, (1, LB), 1)
    sk = jnp.zeros((1, LB), F32)
    for h in range(NQ):
        sk = sk + jnp.where(lane == h, k_ref[h:h + 1, 0:1], 0.0)
    o_ref[15:16, 3 * LB:4 * LB] = sk
    o_ref[16:17, :] = f_ref[2:3, :]


def _exchange_items(gathers, scatters):
    items = list(gathers) + list(scatters)
    shapes = [_sds((N_DEV,) + a.shape, a.dtype) for a in gathers] + [_sds(a.shape, a.dtype) for a in scatters]
    return items, shapes, len(gathers)


def _exchange_sems(n):
    return [pltpu.SemaphoreType.DMA((n, N_DEV - 1)), pltpu.SemaphoreType.DMA((n, N_DEV - 1)),
            pltpu.SemaphoreType.DMA((n,))]


class _Exchange:
    def __init__(self, n_g, ins, outs, sems):
        self.n_g, self.ins, self.outs = n_g, ins, outs
        self.send_sems, self.recv_sems, self.loc_sems = sems
        xi, yi, ci = lax.axis_index("x"), lax.axis_index("y"), lax.axis_index("c")
        self.me = 4 * xi + 2 * yi + ci
        self.peers = []
        for r in range(1, N_DEV):
            px = 1 - xi if r & 4 else xi
            py = 1 - yi if r & 2 else yi
            pc = 1 - ci if r & 1 else ci
            self.peers.append(((px, py, pc), 4 * px + 2 * py + pc))

    def _copy(self, t, r, landing):
        dev, peer = self.peers[r]
        src = self.ins[t] if t < self.n_g else self.ins[t].at[peer]
        return pltpu.make_async_remote_copy(
            src_ref=src, dst_ref=self.outs[t].at[landing], send_sem=self.send_sems.at[t, r],
            recv_sem=self.recv_sems.at[t, r], device_id=dev, device_id_type=pl.DeviceIdType.MESH)

    def _local(self, t):
        src = self.ins[t] if t < self.n_g else self.ins[t].at[self.me]
        return pltpu.make_async_copy(src, self.outs[t].at[self.me], self.loc_sems.at[t])

    def start(self):
        for t in range(len(self.ins)):
            self._local(t).start()
            for r in range(N_DEV - 1):
                self._copy(t, r, self.me).start()

    def finish(self):
        n = len(self.ins)
        for t in range(n):
            for r in range(N_DEV - 1):
                self._copy(t, r, self.peers[r][1]).wait_recv()
        for t in range(n):
            for r in range(N_DEV - 1):
                self._copy(t, r, self.me).wait_send()
            self._local(t).wait()

    DIRECT = (0, 1, 3, 5)

    def two_level_start(self):
        for t in range(len(self.ins)):
            self._local(t).start()
            for r in self.DIRECT:
                self._copy(t, r, self.me).start()

    def _relay(self, t, r):
        peer = self.peers[r][1]
        return pltpu.make_async_remote_copy(
            src_ref=self.outs[t].at[peer], dst_ref=self.outs[t].at[peer], send_sem=self.send_sems.at[t, r + 1],
            recv_sem=self.recv_sems.at[t, r + 1], device_id=self.peers[0][0], device_id_type=pl.DeviceIdType.MESH)

    def two_level_relay(self):
        for t in range(len(self.ins)):
            for r in self.DIRECT[1:]:
                self._copy(t, r, self.peers[r][1]).wait_recv()
                self._relay(t, r).start()

    def two_level_finish(self):
        n = len(self.ins)
        for t in range(n):
            for r in (0, 2, 4, 6):
                self._copy(t, r, self.peers[r][1]).wait_recv()
        for t in range(n):
            for r in self.DIRECT:
                self._copy(t, r, self.me).wait_send()
            for r in self.DIRECT[1:]:
                self._relay(t, r).wait_send()
            self._local(t).wait()


def _prologue(c, w_in_tb, conv_w, w_cols, b_ada):
    ncol = w_cols.shape[1]
    c8_shape, cw8_shape = (8, c.shape[1]), (8, conv_w.shape[1])

    def body(c_ref, win_ref, cw_ref, w_ref, b_ref, gc_ref, gin_ref, gcw_ref, gmod_ref, c8_scr, cw8_scr, call_scr,
             mod_scr, loc_sem, *sems):
        c8_scr[...] = jnp.zeros_like(c8_scr)
        c8_scr[0:1, :] = c_ref[...]
        cw8_scr[...] = jnp.zeros_like(cw8_scr)
        cw8_scr[0:CONVK, :] = cw_ref[...]
        big = _Exchange(2, [win_ref, cw8_scr], [gin_ref, gcw_ref], sems[0:3])
        small = _Exchange(1, [c8_scr], [gc_ref], sems[3:6])
        small.start()
        big.two_level_start()
        small.finish()
        landed = pltpu.make_async_copy(gc_ref, call_scr, loc_sem)
        landed.start()
        landed.wait()
        cv = call_scr[:, 0, :]
        sc = (cv * _sigmoid(cv)).astype(BF16)
        bias = b_ref[:, 0:ncol]
        for dev in range(1, N_DEV):
            bias = jnp.where(small.me == dev, b_ref[:, dev * ncol:(dev + 1) * ncol], bias)
        mod_scr[...] = _dot(sc, w_ref[...].astype(BF16)) + bias
        mods = _Exchange(1, [mod_scr], [gmod_ref], sems[6:9])
        mods.start()
        mods.finish()
        big.two_level_relay()
        big.two_level_finish()

    any_spec = pl.BlockSpec(memory_space=pl.ANY)
    vmem_spec = pl.BlockSpec(memory_space=pltpu.VMEM)
    return _pcall(
        body, name="prologue", in_specs=[vmem_spec, any_spec, vmem_spec, vmem_spec, vmem_spec],
        out_specs=[any_spec] * 4,
        out_shape=[_sds((N_DEV,) + c8_shape, F32), _sds((N_DEV,) + w_in_tb.shape, BF16),
                   _sds((N_DEV,) + cw8_shape, F32), _sds((N_DEV, N_DEV, ncol), F32)],
        scratch_shapes=[pltpu.VMEM(c8_shape, F32), pltpu.VMEM(cw8_shape, F32),
                        pltpu.VMEM((N_DEV,) + c8_shape, F32), pltpu.VMEM((N_DEV, ncol), F32),
                        pltpu.SemaphoreType.DMA] + _exchange_sems(2) + _exchange_sems(1) + _exchange_sems(1),
        compiler_params=pltpu.CompilerParams(vmem_limit_bytes=VMEM_LIMIT),
    )(c, w_in_tb, conv_w, w_cols, b_ada)


def _adamw(w, g, m, v):
    m2 = ADAM_B1 * m + (1.0 - ADAM_B1) * g
    v2 = ADAM_B2 * v + (1.0 - ADAM_B2) * (g * g)
    m_hat = m2 / (1.0 - ADAM_B1 ** ADAM_STEP)
    v_hat = v2 / (1.0 - ADAM_B2 ** ADAM_STEP)
    delta = -ADAM_LR * (m_hat / (jnp.sqrt(v_hat) + ADAM_EPS) + ADAM_WD * w)
    return delta, m2, v2


def _sum_adamw(parts, w, m, v, name):
    rws, cols = w.shape
    tr = next((t for t in (256, 176, 128) if rws % t == 0), rws)

    def body(p_ref, w_ref, m_ref, v_ref, g_ref, d_ref, mo_ref, vo_ref):
        g = p_ref[0].astype(F32)
        for dev in range(1, N_DEV):
            g = g + p_ref[dev].astype(F32)
        g_ref[...] = g
        d_ref[...], mo_ref[...], vo_ref[...] = _adamw(w_ref[...], g, m_ref[...], v_ref[...])

    blk = pl.BlockSpec((tr, cols), lambda i: (i, 0))
    return _pcall(
        body, name=name, grid=(rws // tr,),
        in_specs=[pl.BlockSpec((N_DEV, tr, cols), lambda i: (0, i, 0)), blk, blk, blk],
        out_specs=[blk] * 4, out_shape=[_sds((rws, cols), F32)] * 4, compiler_params=_params(),
    )(parts, w, m, v)


def _sum_adamw_rowwise(parts, w, m, v, name):
    rws, _, cols = w.shape
    step = 80
    chunks = [(lo, min(lo + step, rws)) for lo in range(0, rws, step)]
    n_in, n_out = 3, 4

    def body(p_ref, w_ref, m_ref, v_ref, g_ref, d_ref, mo_ref, vo_ref, ins, outs, sems):
        def load(k, i):
            lo, hi = chunks[k]
            src = (w_ref, m_ref, v_ref)[i]
            return pltpu.make_async_copy(src.at[lo:hi, 0, :], ins.at[i, lo:hi], sems.at[k * n_in + i])

        def store(k, i):
            lo, hi = chunks[k]
            dst = (g_ref, d_ref, mo_ref, vo_ref)[i]
            return pltpu.make_async_copy(outs.at[i, lo:hi], dst.at[lo:hi, 0, :],
                                         sems.at[len(chunks) * n_in + k * n_out + i])

        for k in range(len(chunks)):
            for i in range(n_in):
                load(k, i).start(priority=(k * n_in + i) % 2)
        for k, (lo, hi) in enumerate(chunks):
            g = p_ref[0, lo:hi, :].astype(F32)
            for dev in range(1, N_DEV):
                g = g + p_ref[dev, lo:hi, :].astype(F32)
            for i in range(n_in):
                load(k, i).wait()
            outs[0, lo:hi] = g
            outs[1, lo:hi], outs[2, lo:hi], outs[3, lo:hi] = _adamw(ins[0, lo:hi], g, ins[1, lo:hi], ins[2, lo:hi])
            for i in range(n_out):
                store(k, i).start(priority=(k * n_out + i) % 2)
        for k in range(len(chunks)):
            for i in range(n_out):
                store(k, i).wait()

    any_spec = pl.BlockSpec(memory_space=pl.ANY)
    return _pcall(
        body, name=name, grid=(1,),
        in_specs=[_const((N_DEV, rws, cols)), any_spec, any_spec, any_spec],
        out_specs=[any_spec] * 4, out_shape=[_sds((rws, 1, cols), F32)] * 4,
        scratch_shapes=[pltpu.VMEM((n_in, rws, cols), F32), pltpu.VMEM((n_out, rws, cols), F32),
                        pltpu.SemaphoreType.DMA((len(chunks) * (n_in + n_out),))],
        compiler_params=_params(),
    )(parts, w, m, v)


def _wada_adamw(c_all, dmod_cols, w, m, v):
    rws, cols = w.shape
    tr = 256

    def body(c_ref, dm_ref, w_ref, m_ref, v_ref, g_ref, d_ref, mo_ref, vo_ref):
        cv = c_ref[...]
        sc = (cv * _sigmoid(cv)).astype(BF16)
        g = _dot_tn(sc, dm_ref[...].astype(BF16))
        g_ref[...] = g
        d_ref[...], mo_ref[...], vo_ref[...] = _adamw(w_ref[...], g, m_ref[...], v_ref[...])

    blk = pl.BlockSpec((tr, cols), lambda i: (i, 0))
    return _pcall(
        body, name="wada_adamw", grid=(rws // tr,),
        in_specs=[pl.BlockSpec((N_DEV, tr), lambda i: (0, i)), pl.BlockSpec((N_DEV, cols), lambda i: (0, 0)),
                  blk, blk, blk],
        out_specs=[blk] * 4, out_shape=[_sds((rws, cols), F32)] * 4, compiler_params=_params(),
    )(c_all, dmod_cols, w, m, v)


SMALL_NAMES = ("b_ada", "norm1_w", "conv_w", "conv_b", "dt_bias", "a_log", "d_skip", "attn_sinks", "ssm_norm_w",
               "norm2_w", "final_norm_w")


def _small_grads(tot, me):
    shard = D // N_DEV
    conv = tot[10:10 + CONVK, 0:shard]
    for dev in range(1, N_DEV):
        conv = jnp.where(me == dev, tot[10:10 + CONVK, dev * shard:(dev + 1) * shard], conv)
    return [
        jnp.concatenate([tot[r:r + 1, :] for r in range(N_MOD)], axis=1),
        tot[6:7, :], conv, tot[9:10, :],
        tot[15:16, 0:NH], tot[15:16, LB:LB + NH], tot[15:16, 2 * LB:2 * LB + NH], tot[15:16, 3 * LB:3 * LB + NQ],
        tot[14:15, 0:SW], tot[7:8, :], tot[8:9, :],
    ]


def _small_adamw(packs, ws, ms, vs):
    k = len(ws)

    def body(p_ref, *refs):
        w_refs, m_refs, v_refs = refs[:k], refs[k:2 * k], refs[2 * k:3 * k]
        loss_ref = refs[3 * k]
        g_refs, d_refs, mo_refs, vo_refs = (refs[3 * k + 1 + j * k:3 * k + 1 + (j + 1) * k] for j in range(4))
        tot_ref = refs[7 * k + 1]
        tot = p_ref[0]
        for dev in range(1, N_DEV):
            tot = tot + p_ref[dev]
        tot_ref[...] = tot
        loss_ref[...] = jnp.zeros((1, 1), F32) + (0.5 / D) * jnp.sum(tot[16:17, :])
        me = 4 * lax.axis_index("x") + 2 * lax.axis_index("y") + lax.axis_index("c")
        for i, g in enumerate(_small_grads(tot_ref, me)):
            g_refs[i][...] = g
            d_refs[i][...], mo_refs[i][...], vo_refs[i][...] = _adamw(w_refs[i][...], g, m_refs[i][...], v_refs[i][...])

    shp = [_sds(w.shape, F32) for w in ws]
    outs = _pcall(body, name="adamw_small", out_shape=[_sds((1, 1), F32)] + shp * 4,
                  scratch_shapes=[pltpu.VMEM((PACK_ROWS, D), F32)])(packs, *ws, *ms, *vs)
    return outs[0], outs[1:1 + k], outs[1 + k:1 + 2 * k], outs[1 + 2 * k:1 + 3 * k], outs[1 + 3 * k:]


def kernel(x, c, positions, w_ada, b_ada, norm1_w, w_in, conv_w, conv_b, dt_bias, a_log, d_skip, attn_sinks, ssm_norm_w, w_out, norm2_w, w_gate_up, w_down, final_norm_w, loss_target, m_w_ada, m_b_ada, m_norm1_w, m_w_in, m_conv_w, m_conv_b, m_dt_bias, m_a_log, m_d_skip, m_attn_sinks, m_ssm_norm_w, m_w_out, m_norm2_w, m_w_gate_up, m_w_down, m_final_norm_w, v_w_ada, v_b_ada, v_norm1_w, v_w_in, v_conv_w, v_conv_b, v_dt_bias, v_a_log, v_d_skip, v_attn_sinks, v_ssm_norm_w, v_w_out, v_norm2_w, v_w_gate_up, v_w_down, v_final_norm_w):
    s = x.shape[1]
    me = 4 * lax.axis_index("x") + 2 * lax.axis_index("y") + lax.axis_index("c")
    ada_cols = N_MOD * D // N_DEV

    w_in_t = jnp.transpose(w_in[0])
    rowwise = lambda a: jnp.transpose(a, (2, 0, 1))
    w_gu_t, m_w_gu_t, v_w_gu_t = (jnp.transpose(w_gate_up[0]), jnp.transpose(m_w_gate_up[0]),
                                  jnp.transpose(v_w_gate_up[0]))
    g_c, g_in, g_cw, g_mod = _prologue(c, w_in_t.astype(BF16), conv_w[0], w_ada[0], b_ada)
    c_all = g_c[:, 0, :]
    w_in_f = jnp.pad(g_in.reshape(IN_PROJ, D), ((0, INP - IN_PROJ), (0, 0)))
    conv_w8 = jnp.transpose(g_cw, (1, 0, 2)).reshape(8, D)
    mod = lax.dynamic_index_in_dim(g_mod, me, axis=1, keepdims=False).reshape(N_MOD, D)
    mod8 = jnp.pad(mod, ((0, 8 - N_MOD), (0, 0)))

    half = HD // 2
    inv_freq = ROPE_THETA ** (-jnp.arange(half, dtype=F32) / half)
    invf = jnp.tile(inv_freq, LB // half).reshape(1, LB)
    lanes = lambda a: jnp.pad(a, ((0, 0), (0, LB - a.shape[1])))
    ssm_p = jnp.pad(jnp.concatenate([lanes(dt_bias), lanes(a_log), lanes(d_skip)], axis=0), ((0, 5), (0, 0)))
    sinks8 = jnp.broadcast_to(attn_sinks.reshape(NQ, 1), (NQ, LB))

    xs, tgt, fnw = x[0], loss_target[0], final_norm_w.reshape(1, D)

    q, k, v, z, xbc, dtr, h1, cos, sin = _inproj_fwd(xs, positions[0].reshape(s, 1), invf, mod8, norm1_w, w_in_f)
    (attn, yn, y, hs, conv_u, dtv, acs, decay, probs, psink), (g_out, g_gu, g_down) = _mixer_fwd(
        q, k, v, sinks8, xbc, conv_w8, conv_b, dtr, ssm_p, z, ssm_norm_w,
        [w_out[0].astype(BF16), w_gu_t.astype(BF16), w_down[0].astype(BF16)])
    w_out_f = g_out.reshape(D, D)
    w_gu_f = g_gu.reshape(2 * DFF, D)
    w_down_f = g_down.reshape(DFF, D)
    x2, h2, mo, mix, gu, act, dx3, sm_f = _outproj_ffn_fwd_loss(attn, yn, xs, tgt, mod8, norm2_w, fnw, w_out_f, w_gu_f,
                                                                 w_down_f)

    dx2, dff, dgu, dmix, dattn, dyn, sm_b = _ffn_bwd(dx3, gu, x2, mo, mod8, norm2_w, w_gu_f, w_down_f, w_out_f)
    p_gu = _wgrad(dgu, h2, "wgrad_gate_up").reshape(N_DEV, 2 * DFF // N_DEV, D)
    p_down = _wgrad(act, dff, "wgrad_down").reshape(N_DEV, DFF // N_DEV, D)
    p_out = _wgrad(mix, dmix, "wgrad_out").reshape(N_DEV, D // N_DEV, D)
    dproj, dsink, sm_s, (r_gu, r_down, r_out) = _mixer_bwd(
        q, k, v, attn, dattn, cos, sin, probs, psink, dyn, y, z, xbc, conv_u, dtv, acs, decay, conv_w8, ssm_p,
        ssm_norm_w, hs, [p_gu, p_down, p_out])
    p_in = _wgrad(dproj, h1, "wgrad_in")[:IN_PROJ].reshape(N_DEV, IN_PROJ // N_DEV, D)
    gx, (r_in,), g_pack = _inproj_bwd(dproj, xs, dx2, mod8, norm1_w, w_in_f, [p_in], (sm_f, sm_b, sm_s, dsink))

    dmod_all = g_pack[:, 0:N_MOD, :].reshape(N_DEV, N_MOD * D)
    dmod_cols = lax.dynamic_slice(dmod_all, (0, me * ada_cols), (N_DEV, ada_cols))

    big = {
        "w_ada": _wada_adamw(c_all, dmod_cols, w_ada[0], m_w_ada[0], v_w_ada[0]),
        "w_in": [jnp.transpose(t, (1, 2, 0))[0] for t in
                 _sum_adamw_rowwise(r_in, rowwise(w_in), rowwise(m_w_in), rowwise(v_w_in), "adamw_in")],
        "w_out": _sum_adamw(r_out, w_out[0], m_w_out[0], v_w_out[0], "adamw_out"),
        "w_gate_up": [jnp.transpose(t) for t in _sum_adamw(r_gu, w_gu_t, m_w_gu_t, v_w_gu_t, "adamw_gate_up")],
        "w_down": _sum_adamw(r_down, w_down[0], m_w_down[0], v_w_down[0], "adamw_down"),
    }
    small_w = {"b_ada": b_ada, "norm1_w": norm1_w, "conv_w": conv_w[0], "conv_b": conv_b, "dt_bias": dt_bias,
               "a_log": a_log, "d_skip": d_skip, "attn_sinks": attn_sinks, "ssm_norm_w": ssm_norm_w,
               "norm2_w": norm2_w, "final_norm_w": final_norm_w.reshape(1, D)}
    small_m = {"b_ada": m_b_ada, "norm1_w": m_norm1_w, "conv_w": m_conv_w[0], "conv_b": m_conv_b,
               "dt_bias": m_dt_bias, "a_log": m_a_log, "d_skip": m_d_skip, "attn_sinks": m_attn_sinks,
               "ssm_norm_w": m_ssm_norm_w, "norm2_w": m_norm2_w, "final_norm_w": m_final_norm_w.reshape(1, D)}
    small_v = {"b_ada": v_b_ada, "norm1_w": v_norm1_w, "conv_w": v_conv_w[0], "conv_b": v_conv_b,
               "dt_bias": v_dt_bias, "a_log": v_a_log, "d_skip": v_d_skip, "attn_sinks": v_attn_sinks,
               "ssm_norm_w": v_ssm_norm_w, "norm2_w": v_norm2_w, "final_norm_w": v_final_norm_w.reshape(1, D)}
    loss, s_g, s_d, s_m, s_v = _small_adamw(g_pack, [small_w[k] for k in SMALL_NAMES],
                                            [small_m[k] for k in SMALL_NAMES], [small_v[k] for k in SMALL_NAMES])

    order = ["w_ada", "b_ada", "norm1_w", "w_in", "conv_w", "conv_b", "dt_bias", "a_log", "d_skip", "attn_sinks",
             "ssm_norm_w", "w_out", "norm2_w", "w_gate_up", "w_down", "final_norm_w"]
    lead = {"w_ada", "w_in", "conv_w", "w_out", "w_gate_up", "w_down"}
    grads, deltas, new_m, new_v = [], [], [], []
    for name in order:
        if name in big:
            g, d, m2, v2 = big[name]
        else:
            i = SMALL_NAMES.index(name)
            g, d, m2, v2 = s_g[i], s_d[i], s_m[i], s_v[i]
        if name in lead:
            g, d, m2, v2 = g[None], d[None], m2[None], v2[None]
        if name == "final_norm_w":
            g, d, m2, v2 = g.reshape(D), d.reshape(D), m2.reshape(D), v2.reshape(D)
        grads.append(g)
        deltas.append(d)
        new_m.append(m2)
        new_v.append(v2)
    return (loss.reshape(()), gx[None], *grads, *deltas, *new_m, *new_v)
```

```python
import functools
import math

import jax
import jax.numpy as jnp
import numpy as np
from jax import lax
from jax.experimental import pallas as pl
from jax.experimental.pallas import tpu as pltpu

F32 = jnp.float32
BF16 = jnp.bfloat16

N_DEV = 8
D = 1024
HD = 64
NQ = 8
AW = 512
KVW = 128
SW = 512
NST = 128
NH = 8
LB = 128
CONVK = 4
DFF = 2816
N_MOD = 6
IN_PROJ = 2312
INP = 2432
O_Q, O_K, O_V, O_Z, O_XBC, O_DT = 0, 512, 640, 768, 1280, 2304
ZXD = INP - O_Z
EPS = 1e-6
NEG = -1e30
ROPE_THETA = 10000.0
VMEM_LIMIT = 56 * 1024 * 1024

ADAM_LR = 0.001
ADAM_B1 = 0.9
ADAM_B2 = 0.999
ADAM_EPS = 1e-08
ADAM_WD = 0.01
ADAM_STEP = 10

NT_DIMS = (((1,), (1,)), ((), ()))
TN_DIMS = (((0,), (0,)), ((), ()))


def _pcall(body, **kw):
    return pl.pallas_call(body, **kw)


def _sds(shape, dtype):
    return jax.ShapeDtypeStruct(shape, dtype)


def _params(n_grid=1):
    return pltpu.CompilerParams(dimension_semantics=("arbitrary",) * n_grid, vmem_limit_bytes=VMEM_LIMIT)


def _const(shape):
    return pl.BlockSpec(shape, lambda *_: (0,) * len(shape), pipeline_mode=pl.Buffered(1))


def _largest_divisor(n, candidates):
    for cand in candidates:
        if n % cand == 0:
            return cand
    raise ValueError(f"no tile in {candidates} divides {n}")


def _rows(t, w):
    return pl.BlockSpec((t, w), lambda i: (i, 0))


def _dot(a, b):
    return jnp.dot(a, b, preferred_element_type=F32)


def _dot_nt(a, b):
    return lax.dot_general(a, b, NT_DIMS, preferred_element_type=F32)


def _dot_tn(a, b):
    return lax.dot_general(a, b, TN_DIMS, preferred_element_type=F32)


def _sigmoid(v):
    return 1.0 / (1.0 + jnp.exp(-v))


def _softplus(v):
    return jnp.maximum(v, 0.0) + jnp.log1p(jnp.exp(-jnp.abs(v)))


def _rope_sign_mask(shape):
    lane = lax.broadcasted_iota(jnp.int32, shape, 1)
    return (lane % HD) < (HD // 2)


def _rope(t, cs, sn, inverse):
    r_dn = pltpu.roll(t, HD // 2, 1)
    r_up = pltpu.roll(t, LB - HD // 2, 1)
    first = _rope_sign_mask(t.shape)
    if inverse:
        rot = jnp.where(first, r_up, -r_dn)
    else:
        rot = jnp.where(first, -r_up, r_dn)
    return t * cs + rot * sn


def _norm_mod_fwd(xv, nw, shift, scale):
    r = lax.rsqrt(jnp.mean(xv * xv, axis=-1, keepdims=True) + EPS)
    xh = xv * r
    return (xh * nw) * (1.0 + scale) + shift


def _norm_mod_bwd(xv, dh, nw, scale):
    r = lax.rsqrt(jnp.mean(xv * xv, axis=-1, keepdims=True) + EPS)
    xh = xv * r
    xn = xh * nw
    d_shift = jnp.sum(dh, axis=0, keepdims=True)
    d_scale = jnp.sum(dh * xn, axis=0, keepdims=True)
    dxn = dh * (1.0 + scale)
    d_w = jnp.sum(dxn * xh, axis=0, keepdims=True)
    dxh = dxn * nw
    dx = r * (dxh - xh * jnp.mean(dxh * xh, axis=-1, keepdims=True))
    return dx, d_shift, d_scale, d_w


def _inproj_fwd(x, pos, invf, mod8, n1w, w_in):
    s = x.shape[0]
    tt = min(512, s)

    def body(x_ref, pos_ref, invf_ref, mod_ref, nw_ref, w_ref,
             q_ref, k_ref, v_ref, z_ref, xbc_ref, dtr_ref, h1_ref, cos_ref, sin_ref):
        h = _norm_mod_fwd(x_ref[...], nw_ref[...], mod_ref[0:1, :], mod_ref[1:2, :])
        hb = h.astype(BF16)
        h1_ref[...] = hb
        proj = _dot_nt(hb, w_ref[...])
        i = pl.program_id(0)
        per8 = 8 // (tt // LB)
        rows8 = pos_ref[pl.ds(pl.multiple_of((i // per8) * 8, 8), 8), :].astype(F32)
        cols = jnp.transpose(jnp.concatenate([rows8] * (LB // 8), axis=0))
        first = (i % per8) * (tt // LB)
        angs = []
        for j in range(tt // LB):
            col = cols[:, j:j + 1]
            for o in range(1, per8):
                col = jnp.where(first == o * (tt // LB), cols[:, o * (tt // LB) + j:o * (tt // LB) + j + 1], col)
            angs.append(col * invf_ref[...])
        ang = jnp.concatenate(angs, axis=0)
        cs = jnp.cos(ang)
        sn = jnp.sin(ang)
        cos_ref[...] = cs
        sin_ref[...] = sn
        for a in range(AW // LB):
            q_ref[:, a * LB:(a + 1) * LB] = _rope(proj[:, O_Q + a * LB:O_Q + (a + 1) * LB], cs, sn, False).astype(BF16)
        k_ref[...] = _rope(proj[:, O_K:O_V], cs, sn, False).astype(BF16)
        v_ref[...] = proj[:, O_V:O_Z].astype(BF16)
        z_ref[...] = proj[:, O_Z:O_XBC]
        xbc_ref[...] = proj[:, O_XBC:O_DT]
        dtr_ref[...] = proj[:, O_DT:INP]

    return _pcall(
        body, name="inproj_fwd", grid=(s // tt,),
        in_specs=[_rows(tt, D), _const(pos.shape), _const((1, LB)), _const((8, D)), _const((1, D)), _const((INP, D))],
        out_specs=[_rows(tt, AW), _rows(tt, KVW), _rows(tt, KVW), _rows(tt, SW), _rows(tt, D), _rows(tt, LB),
                   _rows(tt, D), _rows(tt, LB), _rows(tt, LB)],
        out_shape=[_sds((s, AW), BF16), _sds((s, KVW), BF16), _sds((s, KVW), BF16), _sds((s, SW), F32),
                   _sds((s, D), F32), _sds((s, LB), F32), _sds((s, D), BF16), _sds((s, LB), F32), _sds((s, LB), F32)],
        compiler_params=_params(),
    )(x, pos, invf, mod8, n1w, w_in)


QPG = 4
ATT_SCALE = 1.0 / math.sqrt(HD)


def _stack_heads(val, g):
    return jnp.concatenate([val[:, (QPG * g + hh) * HD:(QPG * g + hh + 1) * HD] for hh in range(QPG)], axis=0)


def _unstack_heads(groups):
    pieces = [grp[hh * LB:(hh + 1) * LB, :] for grp in groups for hh in range(QPG)]
    return [jnp.concatenate(pieces[2 * a:2 * a + 2], axis=1) for a in range(NQ // 2)]


def _upper_mask():
    row = lax.broadcasted_iota(jnp.int32, (QPG * LB, LB), 0)
    col = lax.broadcasted_iota(jnp.int32, (QPG * LB, LB), 1)
    return col > (row % LB)


def _sink_wide(sinks, g):
    return jnp.concatenate([jnp.broadcast_to(sinks[QPG * g + hh:QPG * g + hh + 1, 0:1], (LB, LB))
                            for hh in range(QPG)], axis=0)


def _row_sums_wide(v, terms):
    return _dot_sel(v, jnp.ones((v.shape[1], LB), BF16), terms)


def _band(upper, prev_part, cur_part):
    return jnp.where(upper, prev_part, cur_part)


def _attn_scores(n, qg, kcat, upper):
    sp = _dot_nt(qg, kcat[0:LB, :]) * ATT_SCALE
    sc = _dot_nt(qg, kcat[LB:2 * LB, :]) * ATT_SCALE
    return _band(upper, jnp.where(n > 0, sp, NEG), sc)


def _attn_softmax(comb, sink):
    m = jnp.maximum(jnp.max(comb, axis=-1, keepdims=True), sink)
    p = jnp.exp(comb - m)
    es = jnp.exp(sink - m)
    return p, es, _row_sums_wide(p, 1) + es


def _attn_fwd_block(n, q_ref, kp_ref, kc_ref, vp_ref, vc_ref, sink_ref, o_ref, pr_ref, ps_ref):
    qv = q_ref[...]
    kcat = jnp.concatenate([kp_ref[...], kc_ref[...]], axis=0)
    vcat = jnp.concatenate([vp_ref[...], vc_ref[...]], axis=0)
    sinks = sink_ref[...]
    upper = _upper_mask()
    outs = []
    for g in range(NQ // QPG):
        sl = slice(g * HD, (g + 1) * HD)
        rows = slice(g * QPG * LB, (g + 1) * QPG * LB)
        p, es, denom = _attn_softmax(_attn_scores(n, _stack_heads(qv, g), kcat[:, sl], upper), _sink_wide(sinks, g))
        rden = 1.0 / denom
        pr_ref[0, rows, :] = (p * rden).astype(BF16)
        ps_ref[0, rows, :] = (es * rden).astype(BF16)
        outs.append((_dot(jnp.where(upper, p, 0.0).astype(BF16), vcat[0:LB, sl])
                     + _dot(jnp.where(upper, 0.0, p).astype(BF16), vcat[LB:2 * LB, sl])) * rden[:, 0:HD])
    for g, grp in enumerate(outs):
        for hh in range(QPG):
            h = QPG * g + hh
            o_ref[:, h * HD:(h + 1) * HD] = grp[hh * LB:(hh + 1) * LB, :].astype(BF16)


def _cumsum_rows(a, reverse):
    row = lax.broadcasted_iota(jnp.int32, a.shape, 0)
    step = 1
    while step < LB:
        if reverse:
            a = a + jnp.where(row < LB - step, pltpu.roll(a, LB - step, 0), 0.0)
        else:
            a = a + jnp.where(row >= step, pltpu.roll(a, step, 0), 0.0)
        step *= 2
    return a


SUB = 8


def _conv_shifts(tail, cur):
    row = lax.broadcasted_iota(jnp.int32, tail.shape, 0)
    out = [cur]
    for j in range(1, CONVK):
        rolled = pltpu.roll(cur, j, 0)
        top = jnp.where(row < j, pltpu.roll(tail, j, 0), rolled[0:SUB, :])
        out.append(jnp.concatenate([top, rolled[SUB:, :]], axis=0))
    return out


def _conv_advances(du, head):
    row = lax.broadcasted_iota(jnp.int32, head.shape, 0)
    out = []
    for j in range(1, CONVK):
        rolled = pltpu.roll(du, LB - j, 0)
        bottom = jnp.where(row >= SUB - j, pltpu.roll(head, SUB - j, 0), rolled[LB - SUB:, :])
        out.append(jnp.concatenate([rolled[:LB - SUB, :], bottom], axis=0))
    return out


def _split(v, terms):
    out = []
    for _ in range(terms - 1):
        t = v.astype(BF16)
        out.append(t)
        v = v - t.astype(F32)
    out.append(v.astype(BF16))
    return out


def _dot_sel(v, sel, terms):
    parts = [_dot(t, sel) for t in _split(v, terms)]
    return functools.reduce(lambda a, b: a + b, parts)


def _dot_nt_sel(v, sel, terms):
    parts = [_dot_nt(t, sel) for t in _split(v, terms)]
    return functools.reduce(lambda a, b: a + b, parts)


def _ssd_pre(xt_ref, xc_ref, cw_ref, cb_ref, dtr_ref, sp_ref, n):
    cur = xc_ref[...]
    tail = jnp.where(n > 0, xt_ref[...], 0.0)
    sh = _conv_shifts(tail, cur)
    u = cb_ref[...] + cw_ref[CONVK - 1:CONVK, :] * sh[0]
    for j in range(1, CONVK):
        u = u + cw_ref[CONVK - 1 - j:CONVK - j, :] * sh[j]
    dt = _softplus(dtr_ref[...] + sp_ref[0:1, :])
    acs = _cumsum_rows(dt * -jnp.exp(sp_ref[1:2, :]), False)
    return u, dt, acs


def _gated_norm_fwd(y, z, sgz, nw):
    yz = y * (z * sgz)
    parts = []
    for g in range(2):
        t = yz[:, g * 256:(g + 1) * 256]
        parts.append(t * lax.rsqrt(jnp.mean(t * t, axis=-1, keepdims=True) + EPS))
    return jnp.concatenate(parts, axis=1) * nw


HPG = 4
GW = HPG * HD


class _SsdChunk:
    def __init__(self, xc, dt, acs, spv, e64, e128, decay=None):
        self.e64, self.e128 = e64, e128
        alast = acs[LB - 1:LB, :]
        self.e_all = jnp.exp(acs)
        self.dte_all = jnp.exp(alast - acs)
        self.elast = jnp.exp(alast)
        wide = _dot_sel(jnp.concatenate([dt, self.e_all, self.dte_all], axis=0), e64, 2)
        self.dt_x, self.e_x, self.dte_x = wide[0:LB], wide[LB:2 * LB], wide[2 * LB:3 * LB]
        self.dsk_x = _dot_sel(spv, e64, 3)[2:3, :]
        if decay is None:
            acs_t = jnp.transpose(acs)
            ac_x = _dot_sel(acs, e128, 3)
            row = lax.broadcasted_iota(jnp.int32, (HPG * LB, LB), 0)
            col = lax.broadcasted_iota(jnp.int32, (HPG * LB, LB), 1)
            causal = (row % LB) >= col
        lane = lax.broadcasted_iota(jnp.int32, (LB, GW), 1)
        self.head_lanes = [(lane >= hh * HD) & (lane < (hh + 1) * HD) for hh in range(HPG)]
        self.xs, self.xdt, self.b, self.c, self.bb, self.cb16, self.cbm, self.dm_st, self.m_st = ([] for _ in range(9))
        for g in range(2):
            heads = range(HPG * g, HPG * (g + 1))
            if decay is None:
                ac_st = jnp.concatenate([ac_x[:, j * LB:(j + 1) * LB] for j in heads], axis=0)
                ar_st = jnp.concatenate([jnp.broadcast_to(acs_t[j:j + 1, :], (LB, LB)) for j in heads], axis=0)
                dm_st = jnp.exp(jnp.where(causal, ac_st - ar_st, NEG))
            else:
                dm_st = decay[g]
            bg = xc[:, SW + g * NST:SW + (g + 1) * NST]
            cg = xc[:, SW + 2 * NST + g * NST:SW + 2 * NST + (g + 1) * NST]
            bgb, cgb = bg.astype(BF16), cg.astype(BF16)
            cbm = _dot_nt(cgb, bgb)
            xs_g = xc[:, g * GW:(g + 1) * GW]
            self.xs.append(xs_g)
            self.xdt.append(xs_g * self.dt_x[:, g * GW:(g + 1) * GW])
            self.b.append(bg)
            self.c.append(cg)
            self.bb.append(bgb)
            self.cb16.append(cgb)
            self.cbm.append(cbm)
            self.dm_st.append(dm_st)
            self.m_st.append(jnp.concatenate([cbm] * HPG, axis=0) * dm_st)

    def elast_rows(self, g):
        return jnp.concatenate([jnp.broadcast_to(self.elast[:, j:j + 1], (HD, NST))
                                for j in range(HPG * g, HPG * (g + 1))], axis=0)

    def diag_blocks(self, stacked):
        out = stacked[(HPG - 1) * LB:HPG * LB, :]
        for hh in range(HPG - 2, -1, -1):
            out = jnp.where(self.head_lanes[hh], stacked[hh * LB:(hh + 1) * LB, :], out)
        return out

    def block_diag(self, v):
        return jnp.concatenate([jnp.where(self.head_lanes[hh], v, 0.0) for hh in range(HPG)], axis=0)


def _ssd_fwd_block(n, xt_ref, xc_ref, cw_ref, cb_ref, dtr_ref, sp_ref, z_ref, nw_ref, e64_ref, e128_ref,
                   yn_ref, y_ref, hs_ref, u_ref, dt_ref, acs_ref, dm_ref, h_scr):
    @pl.when(n == 0)
    def _():
        h_scr[...] = jnp.zeros_like(h_scr)

    h_all = h_scr[...]
    hs_ref[0] = h_all
    u, dt, acs = _ssd_pre(xt_ref, xc_ref, cw_ref, cb_ref, dtr_ref, sp_ref, n)
    u_ref[...] = u
    dt_ref[...] = dt
    acs_ref[...] = acs
    xc = u * _sigmoid(u)
    ck = _SsdChunk(xc, dt, acs, sp_ref[...], e64_ref[...], e128_ref[...])
    dm_ref[0] = jnp.concatenate(ck.dm_st, axis=0)
    ys, hn = [], []
    for g in range(2):
        gl = slice(g * GW, (g + 1) * GW)
        xdt = ck.xdt[g]
        hg = h_all[gl, :]
        y_diag = ck.diag_blocks(_dot(ck.m_st[g].astype(BF16), xdt.astype(BF16)))
        y_off = ck.e_x[:, gl] * _dot_nt(ck.cb16[g], hg.astype(BF16))
        ys.append(y_diag + y_off + ck.xs[g] * ck.dsk_x[:, gl])
        hn.append(hg * ck.elast_rows(g) + _dot_tn((xdt * ck.dte_x[:, gl]).astype(BF16), ck.bb[g]))
    h_scr[...] = jnp.concatenate(hn, axis=0)
    y = jnp.concatenate(ys, axis=1)
    y_ref[...] = y
    z = z_ref[...]
    yn_ref[...] = _gated_norm_fwd(y, z, _sigmoid(z), nw_ref[...]).astype(BF16)


def _mixer_fwd(q, k, v, sinks8, xbc, conv_w8, conv_b, dtr, ssm_p, z, nw, gathers):
    s = q.shape[0]
    nb = s // LB
    bps = _blocks_per_step(nb)
    nsteps = nb // bps
    tl = bps * LB
    cur = lambda n: (n, 0)
    prev = lambda n: (jnp.maximum(n * bps - 1, 0), 0)
    items, ex_shapes, n_g = _exchange_items(gathers, [])
    ne = len(items)

    n_in, n_out = 16, 10
    relay_step = (3 * (nsteps - 1)) // 4
    stack = pl.BlockSpec((bps, NH * LB, LB), lambda n: (n, 0, 0))
    e64, e128 = _head_expanders()

    def body(*refs):
        (q_ref, kp_ref, kc_ref, vp_ref, vc_ref, sink_ref, xt_ref, xc_ref, cw_ref, cb_ref, dtr_ref, sp_ref, z_ref,
         nw_ref, e64_ref, e128_ref) = refs[:n_in]
        ex_in = refs[n_in:n_in + ne]
        (o_ref, yn_ref, y_ref, hs_ref, u_ref, dt_ref, acs_ref, dm_ref, pr_ref,
         ps_ref) = refs[n_in + ne:n_in + n_out + ne]
        ex_out = refs[n_in + n_out + ne:n_in + n_out + 2 * ne]
        h_scr = refs[n_in + n_out + 2 * ne]
        sems = refs[n_in + n_out + 1 + 2 * ne:]
        n = pl.program_id(0)

        @pl.when(n == 0)
        def _():
            _Exchange(n_g, ex_in, ex_out, sems).two_level_start()

        for sub in range(bps):
            blk = n * bps + sub
            r = slice(sub * LB, (sub + 1) * LB)
            before = slice((sub - 1) * LB, sub * LB)
            one = slice(sub, sub + 1)
            _attn_fwd_block(blk, q_ref.at[r], kp_ref if sub == 0 else kc_ref.at[before], kc_ref.at[r],
                            vp_ref if sub == 0 else vc_ref.at[before], vc_ref.at[r], sink_ref,
                            o_ref.at[r], pr_ref.at[one], ps_ref.at[one])
            _ssd_fwd_block(blk, xt_ref if sub == 0 else xc_ref.at[sub * LB - SUB:sub * LB], xc_ref.at[r], cw_ref,
                           cb_ref, dtr_ref.at[r], sp_ref, z_ref.at[r], nw_ref, e64_ref, e128_ref,
                           yn_ref.at[r], y_ref.at[r], hs_ref.at[one], u_ref.at[r], dt_ref.at[r], acs_ref.at[r],
                           dm_ref.at[one], h_scr)

        @pl.when(n == relay_step)
        def _():
            _Exchange(n_g, ex_in, ex_out, sems).two_level_relay()

        @pl.when(n == nsteps - 1)
        def _():
            _Exchange(n_g, ex_in, ex_out, sems).two_level_finish()

    any_spec = pl.BlockSpec(memory_space=pl.ANY)
    tail = pl.BlockSpec((SUB, D), lambda n: (jnp.maximum(n * (tl // SUB) - 1, 0), 0))
    outs = _pcall(
        body, name="mixer_fwd", grid=(nsteps,),
        in_specs=[pl.BlockSpec((tl, AW), cur), pl.BlockSpec((LB, KVW), prev), pl.BlockSpec((tl, KVW), cur),
                  pl.BlockSpec((LB, KVW), prev), pl.BlockSpec((tl, KVW), cur), _const((8, LB)),
                  tail, pl.BlockSpec((tl, D), cur), _const((8, D)), _const((1, D)),
                  pl.BlockSpec((tl, LB), cur), _const((8, LB)), pl.BlockSpec((tl, SW), cur), _const((1, SW)),
                  _const(e64.shape), _const(e128.shape)]
        + [any_spec] * ne,
        out_specs=[pl.BlockSpec((tl, AW), cur), pl.BlockSpec((tl, SW), cur), pl.BlockSpec((tl, SW), cur),
                   pl.BlockSpec((bps, NH * HD, NST), lambda n: (n, 0, 0)), pl.BlockSpec((tl, D), cur),
                   pl.BlockSpec((tl, LB), cur), pl.BlockSpec((tl, LB), cur),
                   stack, stack, stack] + [any_spec] * ne,
        out_shape=[_sds((s, AW), BF16), _sds((s, SW), BF16), _sds((s, SW), F32), _sds((nb, NH * HD, NST), F32),
                   _sds((s, D), F32), _sds((s, LB), F32), _sds((s, LB), F32), _sds((nb, NH * LB, LB), F32),
                   _sds((nb, NH * LB, LB), BF16), _sds((nb, NH * LB, LB), BF16)]
        + ex_shapes,
        scratch_shapes=[pltpu.VMEM((NH * HD, NST), F32)] + _exchange_sems(ne),
        compiler_params=_params(),
    )(q, k, k, v, v, sinks8, xbc, xbc, conv_w8, conv_b, dtr, ssm_p, z, nw, e64, e128, *items)
    return outs[:n_out], outs[n_out:]


def _blocks_per_step(nb):
    return next(b for b in (4, 2, 1) if nb % b == 0)


def _head_expanders():
    j = np.arange(LB)[:, None]
    e64 = (np.arange(NH * HD)[None, :] // HD == j).astype(BF16)
    e128 = (np.arange(NH * LB)[None, :] // LB == j).astype(BF16)
    return jnp.asarray(e64), jnp.asarray(e128)


def _outproj_ffn_fwd_loss(attn, yn, x, tgt, mod8, n2w, fnw, w_out, w_gu_t, w_down):
    s = x.shape[0]
    tf = min(256, s)

    def body(a_ref, y_ref, x_ref, t_ref, mod_ref, nw_ref, fw_ref, wo_ref, wgu_ref, wd_ref,
             x2_ref, h2_ref, mo_ref, mix_ref, gu_ref, act_ref, dx3_ref, sm_ref):
        i = pl.program_id(0)

        @pl.when(i == 0)
        def _():
            sm_ref[...] = jnp.zeros_like(sm_ref)

        mix = jnp.concatenate([a_ref[...], y_ref[...]], axis=1)
        mix_ref[...] = mix
        mo = _dot(mix, wo_ref[...])
        mo_ref[...] = mo.astype(BF16)
        x2 = x_ref[...] + mod_ref[2:3, :] * mo
        x2_ref[...] = x2
        h2 = _norm_mod_fwd(x2, nw_ref[...], mod_ref[3:4, :], mod_ref[4:5, :]).astype(BF16)
        h2_ref[...] = h2
        gu = _dot_nt(h2, wgu_ref[...])
        gu_ref[...] = gu.astype(BF16)
        g = gu[:, :DFF]
        act = (g * _sigmoid(g) * gu[:, DFF:]).astype(BF16)
        act_ref[...] = act
        ff = _dot(act, wd_ref[...])
        x3 = x2 + mod_ref[5:6, :] * ff
        r = lax.rsqrt(jnp.mean(x3 * x3, axis=-1, keepdims=True) + EPS)
        xh = x3 * r
        fw = fw_ref[...]
        err = xh * fw - t_ref[...]
        dy = err * (1.0 / D)
        dxh = dy * fw
        dx3 = r * (dxh - xh * jnp.mean(dxh * xh, axis=-1, keepdims=True))
        dx3_ref[...] = dx3
        sm_ref[0:1, :] += jnp.sum(dx3 * ff, axis=0, keepdims=True)
        sm_ref[1:2, :] += jnp.sum(dy * xh, axis=0, keepdims=True)
        sm_ref[2:3, :] += jnp.sum(err * err, axis=0, keepdims=True)

    return _pcall(
        body, name="outproj_ffn_fwd_loss", grid=(s // tf,),
        in_specs=[_rows(tf, AW), _rows(tf, SW), _rows(tf, D), _rows(tf, D), _const((8, D)), _const((1, D)),
                  _const((1, D)), _const((D, D)), _const((2 * DFF, D)), _const((DFF, D))],
        out_specs=[_rows(tf, D), _rows(tf, D), _rows(tf, D), _rows(tf, D), _rows(tf, 2 * DFF), _rows(tf, DFF),
                   _rows(tf, D), pl.BlockSpec((8, D), lambda i: (0, 0))],
        out_shape=[_sds((s, D), F32), _sds((s, D), BF16), _sds((s, D), BF16), _sds((s, D), BF16),
                   _sds((s, 2 * DFF), BF16), _sds((s, DFF), BF16), _sds((s, D), F32), _sds((8, D), F32)],
        compiler_params=_params(),
    )(attn, yn, x, tgt, mod8, n2w, fnw, w_out, w_gu_t, w_down)


def _ffn_bwd(dx3, gu, x2, mixout, mod8, n2w, w_gu, w_down, w_out):
    s = x2.shape[0]
    tb = min(256, s)

    def body(dx3_ref, gu_ref, x2_ref, mo_ref, mod_ref, nw_ref, wgu_ref, wd_ref, wo_ref,
             dx2_ref, dff_ref, dgu_ref, dmix_ref, dattn_ref, dyn_ref, sm_ref):
        i = pl.program_id(0)

        @pl.when(i == 0)
        def _():
            sm_ref[...] = jnp.zeros_like(sm_ref)

        dx3 = dx3_ref[...]
        dff = (dx3 * mod_ref[5:6, :]).astype(BF16)
        dff_ref[...] = dff
        dact = _dot_nt(dff, wd_ref[...])
        g = gu_ref[:, :DFF].astype(F32)
        u = gu_ref[:, DFF:].astype(F32)
        sg = _sigmoid(g)
        dgu = jnp.concatenate([dact * u * sg * (1.0 + g * (1.0 - sg)), dact * g * sg], axis=1).astype(BF16)
        dgu_ref[...] = dgu
        dh2 = _dot(dgu, wgu_ref[...])
        dxn, d_shift, d_scale, d_w = _norm_mod_bwd(x2_ref[...], dh2, nw_ref[...], mod_ref[4:5, :])
        dx2 = dx3 + dxn
        dx2_ref[...] = dx2
        sm_ref[0:1, :] += d_shift
        sm_ref[1:2, :] += d_scale
        sm_ref[2:3, :] += d_w
        sm_ref[3:4, :] += jnp.sum(dx2 * mo_ref[...].astype(F32), axis=0, keepdims=True)
        dmix = (dx2 * mod_ref[2:3, :]).astype(BF16)
        dmix_ref[...] = dmix
        dmi = _dot_nt(dmix, wo_ref[...])
        dattn_ref[...] = dmi[:, :AW].astype(BF16)
        dyn_ref[...] = dmi[:, AW:]

    return _pcall(
        body, name="ffn_bwd", grid=(s // tb,),
        in_specs=[_rows(tb, D), _rows(tb, 2 * DFF), _rows(tb, D), _rows(tb, D), _const((8, D)), _const((1, D)),
                  _const((2 * DFF, D)), _const((DFF, D)), _const((D, D))],
        out_specs=[_rows(tb, D), _rows(tb, D), _rows(tb, 2 * DFF), _rows(tb, D), _rows(tb, AW), _rows(tb, SW),
                   pl.BlockSpec((8, D), lambda i: (0, 0))],
        out_shape=[_sds((s, D), F32), _sds((s, D), BF16), _sds((s, 2 * DFF), BF16), _sds((s, D), BF16),
                   _sds((s, AW), BF16), _sds((s, SW), F32), _sds((8, D), F32)],
        compiler_params=_params(),
    )(dx3, gu, x2, mixout, mod8, n2w, w_gu, w_down, w_out)


def _ssd_bwd_block(i, *refs):
    def run(dyn_ref, y_ref, z_ref, x_ref, u_ref, dt_ref, acs_ref, dm_ref, cw_ref, sp_ref, nw_ref,
            hs_ref, e64_ref, e128_ref, dzxd_ref, sm_ref, dh_scr, dun_scr):
        @pl.when(i == 0)
        def _():
            dh_scr[...] = jnp.zeros_like(dh_scr)
            dun_scr[...] = jnp.zeros_like(dun_scr)
            sm_ref[...] = jnp.zeros_like(sm_ref)

        u, dt, acs = u_ref[...], dt_ref[...], acs_ref[...]
        sg_u = _sigmoid(u)
        xc = u * sg_u
        a_neg = -jnp.exp(sp_ref[1:2, :])
        ck = _SsdChunk(xc, dt, acs, sp_ref[...], e64_ref[...], e128_ref[...],
                       decay=[dm_ref[0, g * HPG * LB:(g + 1) * HPG * LB, :] for g in range(2)])
        h_all = hs_ref[0]
        dh_all = dh_scr[...]
        riota = lax.broadcasted_iota(jnp.int32, (LB, LB), 0)
        lane1 = lax.broadcasted_iota(jnp.int32, (1, LB), 1)

        z = z_ref[...]
        y = y_ref[...]
        sgz = _sigmoid(z)
        sz = z * sgz
        yz = y * sz
        nwv = nw_ref[...]
        dyn_v = dyn_ref[...]
        dyhat = dyn_v * nwv
        yhat_parts, dyz_parts = [], []
        for g in range(2):
            gs = slice(g * 256, (g + 1) * 256)
            t = yz[:, gs]
            rg = lax.rsqrt(jnp.mean(t * t, axis=-1, keepdims=True) + EPS)
            yh = t * rg
            dyh = dyhat[:, gs]
            yhat_parts.append(yh)
            dyz_parts.append(rg * (dyh - yh * jnp.mean(dyh * yh, axis=-1, keepdims=True)))
        yhat = jnp.concatenate(yhat_parts, axis=1)
        dyz = jnp.concatenate(dyz_parts, axis=1)
        sm_ref[5:6, 0:SW] += jnp.sum(dyn_v * yhat, axis=0, keepdims=True)
        dy = dyz * sz
        dzxd_ref[:, 0:SW] = (dyz * y * sgz * (1.0 + z * (1.0 - sgz))).astype(BF16)

        cat = lambda parts: jnp.concatenate(parts, axis=1)
        dxs, dbs, dcs, dhp, g_cat, de_x, ddte_x, ddt_x, ddsk_x = ([] for _ in range(9))
        dacs_t = jnp.zeros((LB, LB), F32)
        hsum = jnp.zeros((1, LB), F32)
        for g in range(2):
            gl = slice(g * GW, (g + 1) * GW)
            xs_g, xdt, bgb, cgb = ck.xs[g], ck.xdt[g], ck.bb[g], ck.cb16[g]
            m_st, dm_st = ck.m_st[g], ck.dm_st[g]
            dt_x, e_x, dte_x = ck.dt_x[:, gl], ck.e_x[:, gl], ck.dte_x[:, gl]
            xdtb = xdt.astype(BF16)
            hg, dhn = h_all[gl, :], dh_all[gl, :]
            hb, dhnb = hg.astype(BF16), dhn.astype(BF16)
            dy_g = dy[:, gl]
            ddsk_x.append(jnp.sum(dy_g * xs_g, axis=0, keepdims=True))
            dy_bd = ck.block_diag(dy_g).astype(BF16)
            dm4 = _dot_nt(dy_bd, xdtb)
            dxdt = _dot_tn(m_st.astype(BF16), dy_bd)
            gmat = dm4 * m_st
            dcbm = dm4 * dm_st
            dcb = dcbm[0:LB] + dcbm[LB:2 * LB] + dcbm[2 * LB:3 * LB] + dcbm[3 * LB:4 * LB]
            g_cat.append(cat([gmat[hh * LB:(hh + 1) * LB, :] for hh in range(HPG)]))
            for hh in range(HPG):
                j = HPG * g + hh
                col_sum = jnp.sum(gmat[hh * LB:(hh + 1) * LB, :], axis=0, keepdims=True)
                dacs_t = dacs_t - jnp.where(riota == j, col_sum, 0.0)
                hsl = slice(hh * HD, (hh + 1) * HD)
                hsum = hsum + jnp.where(lane1 == j, jnp.sum(dhn[hsl, :] * hg[hsl, :]), 0.0)
            dchb = (dy_g * e_x).astype(BF16)
            dcg = _dot(dchb, hb)
            dh_prev = _dot_tn(dchb, cgb)
            de_x.append(dy_g * _dot_nt(cgb, hb))
            dxs_s = _dot_nt(bgb, dhnb)
            dbg = _dot((xdt * dte_x).astype(BF16), dhnb)
            dxdt = dxdt + dxs_s * dte_x
            ddte_x.append(dxs_s * xdt)
            dhp.append(dhn * ck.elast_rows(g) + dh_prev)
            dxs.append(dy_g * ck.dsk_x[:, gl] + dxdt * dt_x)
            ddt_x.append(dxdt * xs_g)
            dcbb = dcb.astype(BF16)
            dbs.append(dbg + _dot_tn(dcbb, cgb))
            dcs.append(dcg + _dot(dcbb, bgb))
        dh_scr[...] = jnp.concatenate(dhp, axis=0)
        red = _dot_nt_sel(jnp.concatenate([cat(de_x), cat(ddte_x), cat(ddt_x)], axis=0), ck.e64, 1)
        de_c, ddte_c, ddt_c = red[0:LB], red[LB:2 * LB], red[2 * LB:3 * LB]
        ddsk = _dot_nt_sel(jnp.broadcast_to(cat(ddsk_x), (SUB, NH * HD)), ck.e64, 2)[0:1, :]
        t1 = ddte_c * ck.dte_all
        dalast = jnp.sum(t1, axis=0, keepdims=True) + hsum * ck.elast
        dacs = (_dot_nt_sel(cat(g_cat), ck.e128, 2) + de_c * ck.e_all - t1 + jnp.transpose(dacs_t)
                + jnp.where(riota == LB - 1, dalast, 0.0))
        da = _cumsum_rows(dacs, True)
        ddt = ddt_c + da * a_neg
        da_log = jnp.sum(da * dt, axis=0, keepdims=True) * a_neg
        ddtr = ddt * (1.0 - jnp.exp(-dt))
        dzxd_ref[:, SW + D:ZXD] = ddtr.astype(BF16)
        sm_ref[6:7, 0:LB] += jnp.sum(ddtr, axis=0, keepdims=True)
        sm_ref[6:7, LB:2 * LB] += da_log
        sm_ref[6:7, 2 * LB:3 * LB] += ddsk

        du = cat(dxs + dbs + dcs) * (sg_u * (1.0 + u * (1.0 - sg_u)))
        xv = x_ref[...]
        adv = [du] + _conv_advances(du, dun_scr[...])
        sm_ref[0:1, :] += jnp.sum(du, axis=0, keepdims=True)
        dxbc = cw_ref[CONVK - 1:CONVK, :] * du
        for j in range(CONVK):
            sm_ref[CONVK - j:CONVK + 1 - j, :] += jnp.sum(adv[j] * xv, axis=0, keepdims=True)
            if j:
                dxbc = dxbc + cw_ref[CONVK - 1 - j:CONVK - j, :] * adv[j]
        dun_scr[...] = du[0:SUB, :]
        dzxd_ref[:, SW:SW + D] = dxbc.astype(BF16)

    run(*refs)


def _attn_bwd_block(i, q_ref, kp_ref, kc_ref, vp_ref, vc_ref, o_ref, do_ref, cos_ref, sin_ref, pr_ref, ps_ref,
                    dq_ref, dkv_ref, ds_ref, ck_scr, cv_scr):
    @pl.when(i == 0)
    def _():
        ds_ref[...] = jnp.zeros_like(ds_ref)
        ck_scr[...] = jnp.zeros_like(ck_scr)
        cv_scr[...] = jnp.zeros_like(cv_scr)

    qv, ov, dov = q_ref[...], o_ref[...], do_ref[...]
    kcat = jnp.concatenate([kp_ref[...], kc_ref[...]], axis=0)
    vcat = jnp.concatenate([vp_ref[...], vc_ref[...]], axis=0)
    upper = _upper_mask()
    srow = lax.broadcasted_iota(jnp.int32, (8, LB), 0)
    slane = lax.broadcasted_iota(jnp.int32, (8, LB), 1)
    dsink = jnp.zeros((8, LB), F32)
    dq_g, dk_g, dv_g = [], [], []
    for g in range(NQ // QPG):
        sl = slice(g * HD, (g + 1) * HD)
        qg = _stack_heads(qv, g)
        dog = _stack_heads(dov, g)
        rows = slice(g * QPG * LB, (g + 1) * QPG * LB)
        probs = pr_ref[0, rows, :].astype(F32)
        psink = ps_ref[0, rows, :].astype(F32)
        delta = _row_sums_wide(dog.astype(F32) * _stack_heads(ov, g).astype(F32), 2)
        dsc = probs * (_band(upper, _dot_nt(dog, vcat[0:LB, sl]), _dot_nt(dog, vcat[LB:2 * LB, sl])) - delta)
        sink_terms = (psink * delta)[:, 0:1]
        for hh in range(QPG):
            dsink = dsink - jnp.where((srow == QPG * g + hh) & (slane == 0),
                                      jnp.sum(sink_terms[hh * LB:(hh + 1) * LB, :]), 0.0)
        ds_p = jnp.where(upper, dsc, 0.0).astype(BF16)
        ds_c = jnp.where(upper, 0.0, dsc).astype(BF16)
        dq_g.append((_dot(ds_p, kcat[0:LB, sl]) + _dot(ds_c, kcat[LB:2 * LB, sl])) * ATT_SCALE)
        dk_g.append(jnp.concatenate([_dot_tn(ds_p, qg), _dot_tn(ds_c, qg)], axis=0) * ATT_SCALE)
        dv_g.append(jnp.concatenate([_dot_tn(jnp.where(upper, probs, 0.0).astype(BF16), dog),
                                     _dot_tn(jnp.where(upper, 0.0, probs).astype(BF16), dog)], axis=0))
    ds_ref[...] += dsink
    cs = cos_ref[...]
    sn = sin_ref[...]
    dk2 = jnp.concatenate(dk_g, axis=1)
    dv2 = jnp.concatenate(dv_g, axis=1)
    for a, tile in enumerate(_unstack_heads(dq_g)):
        dq_ref[:, a * LB:(a + 1) * LB] = _rope(tile, cs, sn, True).astype(BF16)
    dkv_ref[:, 0:KVW] = _rope(ck_scr[...] + dk2[LB:2 * LB, :], cs, sn, True).astype(BF16)
    dkv_ref[:, KVW:2 * KVW] = (cv_scr[...] + dv2[LB:2 * LB, :]).astype(BF16)
    ck_scr[...] = dk2[0:LB, :]
    cv_scr[...] = dv2[0:LB, :]


def _mixer_bwd(q, k, v, o, do, cos, sin, probs, psink, dyn, y, z, xbc, u, dtv, acs, decay, conv_w8, ssm_p, nw, hs,
               scatters):
    s = q.shape[0]
    nb = s // LB
    bps = _blocks_per_step(nb)
    nsteps = nb // bps
    tl = bps * LB
    cur = lambda i: (nsteps - 1 - i, 0)
    prev = lambda i: (jnp.maximum((nsteps - 1 - i) * bps - 1, 0), 0)
    n_in = 25
    items, ex_shapes, n_g = _exchange_items([], scatters)
    ne = len(items)
    e64, e128 = _head_expanders()
    stack = pl.BlockSpec((bps, NH * LB, LB), lambda i: (nsteps - 1 - i, 0, 0))

    def body(*refs):
        i = pl.program_id(0)
        (q_ref, kp_ref, kc_ref, vp_ref, vc_ref, o_ref, do_ref, cos_ref, sin_ref, pr_ref, ps_ref,
         dyn_ref, y_ref, z_ref, x_ref, u_ref, dt_ref, acs_ref, dm_ref, cw_ref, sp_ref, nw_ref,
         hs_ref, e64_ref, e128_ref) = refs[:n_in]
        ex_in = refs[n_in:n_in + ne]
        dp_ref, ds_ref, sm_ref = refs[n_in + ne:n_in + ne + 3]
        ex_out = refs[n_in + ne + 3:n_in + 2 * ne + 3]
        ck_scr, cv_scr, dh_scr, dun_scr = refs[n_in + 2 * ne + 3:n_in + 2 * ne + 7]
        sems = refs[n_in + 2 * ne + 7:]

        @pl.when(i == 0)
        def _():
            _Exchange(n_g, ex_in, ex_out, sems).start()

        for back in range(bps):
            sub = bps - 1 - back
            step = i * bps + back
            r = slice(sub * LB, (sub + 1) * LB)
            before = slice((sub - 1) * LB, sub * LB)
            one = slice(sub, sub + 1)
            _attn_bwd_block(step, q_ref.at[r], kp_ref if sub == 0 else kc_ref.at[before], kc_ref.at[r],
                            vp_ref if sub == 0 else vc_ref.at[before], vc_ref.at[r], o_ref.at[r], do_ref.at[r],
                            cos_ref.at[r], sin_ref.at[r], pr_ref.at[one], ps_ref.at[one],
                            dp_ref.at[r, O_Q:O_K], dp_ref.at[r, O_K:O_Z], ds_ref, ck_scr, cv_scr)
            _ssd_bwd_block(step, dyn_ref.at[r], y_ref.at[r], z_ref.at[r], x_ref.at[r], u_ref.at[r], dt_ref.at[r],
                           acs_ref.at[r], dm_ref.at[one], cw_ref, sp_ref, nw_ref,
                           hs_ref.at[one], e64_ref, e128_ref, dp_ref.at[r, O_Z:INP], sm_ref, dh_scr, dun_scr)

        @pl.when(i == nsteps - 1)
        def _():
            _Exchange(n_g, ex_in, ex_out, sems).finish()

    any_spec = pl.BlockSpec(memory_space=pl.ANY)
    outs = _pcall(
        body, name="mixer_bwd", grid=(nsteps,),
        in_specs=[pl.BlockSpec((tl, AW), cur), pl.BlockSpec((LB, KVW), prev), pl.BlockSpec((tl, KVW), cur),
                  pl.BlockSpec((LB, KVW), prev), pl.BlockSpec((tl, KVW), cur), pl.BlockSpec((tl, AW), cur),
                  pl.BlockSpec((tl, AW), cur), pl.BlockSpec((tl, LB), cur), pl.BlockSpec((tl, LB), cur),
                  stack, stack,
                  pl.BlockSpec((tl, SW), cur), pl.BlockSpec((tl, SW), cur), pl.BlockSpec((tl, SW), cur),
                  pl.BlockSpec((tl, D), cur), pl.BlockSpec((tl, D), cur), pl.BlockSpec((tl, LB), cur),
                  pl.BlockSpec((tl, LB), cur), stack,
                  _const((8, D)), _const((8, LB)), _const((1, SW)),
                  pl.BlockSpec((bps, NH * HD, NST), lambda i: (nsteps - 1 - i, 0, 0)),
                  _const(e64.shape), _const(e128.shape)] + [any_spec] * ne,
        out_specs=[pl.BlockSpec((tl, INP), cur), pl.BlockSpec((8, LB), lambda i: (0, 0)),
                   pl.BlockSpec((8, D), lambda i: (0, 0))] + [any_spec] * ne,
        out_shape=[_sds((s, INP), BF16), _sds((8, LB), F32), _sds((8, D), F32)] + ex_shapes,
        scratch_shapes=[pltpu.VMEM((LB, KVW), F32), pltpu.VMEM((LB, KVW), F32),
                        pltpu.VMEM((NH * HD, NST), F32), pltpu.VMEM((SUB, D), F32)]
        + _exchange_sems(ne),
        compiler_params=_params(),
    )(q, k, k, v, v, o, do, cos, sin, probs, psink, dyn, y, z, xbc, u, dtv, acs, decay, conv_w8, ssm_p, nw, hs,
      e64, e128, *items)
    return outs[0], outs[1], outs[2], outs[3:]


def _inproj_bwd(dproj, x, dx2, mod8, n1w, w_in_t, scatters, smalls):
    s = x.shape[0]
    tt = min(512, s)
    nt = s // tt
    items, ex_shapes, n_g = _exchange_items([], scatters)
    ne = len(items)
    n_in = 10

    def body(*refs):
        dp_ref, x_ref, dx2_ref, mod_ref, nw_ref, w_ref, f_ref, b_ref, s_ref, k_ref = refs[:n_in]
        ex_in = refs[n_in:n_in + ne]
        gx_ref, sm_ref = refs[n_in + ne:n_in + 2 + ne]
        ex_out = refs[n_in + 2 + ne:n_in + 2 + 2 * ne]
        gpack_ref = refs[n_in + 2 + 2 * ne]
        pack_scr = refs[n_in + 3 + 2 * ne]
        sems = refs[n_in + 4 + 2 * ne:n_in + 7 + 2 * ne]
        pack_sems = refs[n_in + 7 + 2 * ne:]
        i = pl.program_id(0)

        @pl.when(i == 0)
        def _():
            sm_ref[...] = jnp.zeros_like(sm_ref)
            _Exchange(n_g, ex_in, ex_out, sems).start()

        w = w_ref[...]
        hr = tt // 2
        dh1 = [_dot(dp_ref[h * hr:(h + 1) * hr, :], w) for h in range(2)]
        sums = jnp.zeros((3, D), F32)
        for h in range(2):
            rows = slice(h * hr, (h + 1) * hr)
            dxn, d_shift, d_scale, d_w = _norm_mod_bwd(x_ref[rows, :], dh1[h], nw_ref[...], mod_ref[1:2, :])
            gx_ref[rows, :] = dx2_ref[rows, :] + dxn
            sums = sums + jnp.concatenate([d_shift, d_scale, d_w], axis=0)
        sm_ref[0:3, :] += sums

        @pl.when(i == nt - 1)
        def _():
            _pack_rows(f_ref, b_ref, s_ref, sm_ref, k_ref, pack_scr)
            small = _Exchange(1, [pack_scr], [gpack_ref], pack_sems)
            small.start()
            _Exchange(n_g, ex_in, ex_out, sems).finish()
            small.finish()

    any_spec = pl.BlockSpec(memory_space=pl.ANY)
    outs = _pcall(
        body, name="inproj_bwd", grid=(nt,),
        in_specs=[_rows(tt, INP), _rows(tt, D), _rows(tt, D), _const((8, D)), _const((1, D)), _const((INP, D)),
                  _const((8, D)), _const((8, D)), _const((8, D)), _const((8, LB))]
        + [any_spec] * ne,
        out_specs=[_rows(tt, D), pl.BlockSpec((8, D), lambda i: (0, 0))] + [any_spec] * (ne + 1),
        out_shape=[_sds((s, D), F32), _sds((8, D), F32)] + ex_shapes + [_sds((N_DEV, PACK_ROWS, D), F32)],
        scratch_shapes=[pltpu.VMEM((PACK_ROWS, D), F32)] + _exchange_sems(ne) + _exchange_sems(1),
        compiler_params=_params(),
    )(dproj, x, dx2, mod8, n1w, w_in_t, *smalls, *items)
    return outs[0], outs[2:2 + ne], outs[2 + ne]


def _wgrad(a, b, name):
    s, m = a.shape
    n = b.shape[1]
    tk = min(2048, s)
    wide = (1408, 1024, 512)
    tm = next((t for t in wide if m % t == 0), m)
    tn = n if n <= 2048 else _largest_divisor(n, wide)
    nk = s // tk

    def body(a_ref, b_ref, o_ref, acc):
        kk = pl.program_id(2)

        @pl.when(kk == 0)
        def _():
            acc[...] = jnp.zeros_like(acc)

        acc[...] += _dot_tn(a_ref[...], b_ref[...])

        @pl.when(kk == nk - 1)
        def _():
            o_ref[...] = acc[...].astype(BF16)

    return _pcall(
        body, name=name, grid=(m // tm, n // tn, nk),
        in_specs=[pl.BlockSpec((tk, tm), lambda i, j, kk: (kk, i)), pl.BlockSpec((tk, tn), lambda i, j, kk: (kk, j))],
        out_specs=pl.BlockSpec((tm, tn), lambda i, j, kk: (i, j)),
        out_shape=_sds((m, n), BF16),
        scratch_shapes=[pltpu.VMEM((tm, tn), F32)],
        compiler_params=_params(3),
    )(a, b)


PACK_ROWS = 24


def _pack_rows(f_ref, b_ref, s_ref, i_ref, k_ref, o_ref):
    o_ref[...] = jnp.zeros_like(o_ref)
    o_ref[0:2, :] = i_ref[0:2, :]
    o_ref[2:3, :] = b_ref[3:4, :]
    o_ref[3:5, :] = b_ref[0:2, :]
    o_ref[5:6, :] = f_ref[0:1, :]
    o_ref[6:7, :] = i_ref[2:3, :]
    o_ref[7:8, :] = b_ref[2:3, :]
    o_ref[8:9, :] = f_ref[1:2, :]
    o_ref[9:14, :] = s_ref[0:5, :]
    o_ref[14:15, :] = s_ref[5:6, :]
    o_ref[15:16, 0:3 * LB] = s_ref[6:7, 0:3 * LB]
    lane = lax.broadcasted_iota(jnp.int32, (1, LB), 1)
    sk = jnp.zeros((1, LB), F32)
    for h in range(NQ):
        sk = sk + jnp.where(lane == h, k_ref[h:h + 1, 0:1], 0.0)
    o_ref[15:16, 3 * LB:4 * LB] = sk
    o_ref[16:17, :] = f_ref[2:3, :]


def _exchange_items(gathers, scatters):
    items = list(gathers) + list(scatters)
    shapes = [_sds((N_DEV,) + a.shape, a.dtype) for a in gathers] + [_sds(a.shape, a.dtype) for a in scatters]
    return items, shapes, len(gathers)


def _exchange_sems(n):
    return [pltpu.SemaphoreType.DMA((n, N_DEV - 1)), pltpu.SemaphoreType.DMA((n, N_DEV - 1)),
            pltpu.SemaphoreType.DMA((n,))]


class _Exchange:
    def __init__(self, n_g, ins, outs, sems):
        self.n_g, self.ins, self.outs = n_g, ins, outs
        self.send_sems, self.recv_sems, self.loc_sems = sems
        xi, yi, ci = lax.axis_index("x"), lax.axis_index("y"), lax.axis_index("c")
        self.me = 4 * xi + 2 * yi + ci
        self.peers = []
        for r in range(1, N_DEV):
            px = 1 - xi if r & 4 else xi
            py = 1 - yi if r & 2 else yi
            pc = 1 - ci if r & 1 else ci
            self.peers.append(((px, py, pc), 4 * px + 2 * py + pc))

    def _copy(self, t, r, landing):
        dev, peer = self.peers[r]
        src = self.ins[t] if t < self.n_g else self.ins[t].at[peer]
        return pltpu.make_async_remote_copy(
            src_ref=src, dst_ref=self.outs[t].at[landing], send_sem=self.send_sems.at[t, r],
            recv_sem=self.recv_sems.at[t, r], device_id=dev, device_id_type=pl.DeviceIdType.MESH)

    def _local(self, t):
        src = self.ins[t] if t < self.n_g else self.ins[t].at[self.me]
        return pltpu.make_async_copy(src, self.outs[t].at[self.me], self.loc_sems.at[t])

    def start(self):
        for t in range(len(self.ins)):
            self._local(t).start()
            for r in range(N_DEV - 1):
                self._copy(t, r, self.me).start()

    def finish(self):
        n = len(self.ins)
        for t in range(n):
            for r in range(N_DEV - 1):
                self._copy(t, r, self.peers[r][1]).wait_recv()
        for t in range(n):
            for r in range(N_DEV - 1):
                self._copy(t, r, self.me).wait_send()
            self._local(t).wait()

    DIRECT = (0, 1, 3, 5)

    def two_level_start(self):
        for t in range(len(self.ins)):
            self._local(t).start()
            for r in self.DIRECT:
                self._copy(t, r, self.me).start()

    def _relay(self, t, r):
        peer = self.peers[r][1]
        return pltpu.make_async_remote_copy(
            src_ref=self.outs[t].at[peer], dst_ref=self.outs[t].at[peer], send_sem=self.send_sems.at[t, r + 1],
            recv_sem=self.recv_sems.at[t, r + 1], device_id=self.peers[0][0], device_id_type=pl.DeviceIdType.MESH)

    def two_level_relay(self):
        for t in range(len(self.ins)):
            for r in self.DIRECT[1:]:
                self._copy(t, r, self.peers[r][1]).wait_recv()
                self._relay(t, r).start()

    def two_level_finish(self):
        n = len(self.ins)
        for t in range(n):
            for r in (0, 2, 4, 6):
                self._copy(t, r, self.peers[r][1]).wait_recv()
        for t in range(n):
            for r in self.DIRECT:
                self._copy(t, r, self.me).wait_send()
            for r in self.DIRECT[1:]:
                self._relay(t, r).wait_send()
            self._local(t).wait()


def _prologue(c, w_in_tb, conv_w, w_cols, b_ada):
    ncol = w_cols.shape[1]
    c8_shape, cw8_shape = (8, c.shape[1]), (8, conv_w.shape[1])

    def body(c_ref, win_ref, cw_ref, w_ref, b_ref, gc_ref, gin_ref, gcw_ref, gmod_ref, c8_scr, cw8_scr, call_scr,
             mod_scr, loc_sem, *sems):
        c8_scr[...] = jnp.zeros_like(c8_scr)
        c8_scr[0:1, :] = c_ref[...]
        cw8_scr[...] = jnp.zeros_like(cw8_scr)
        cw8_scr[0:CONVK, :] = cw_ref[...]
        big = _Exchange(2, [win_ref, cw8_scr], [gin_ref, gcw_ref], sems[0:3])
        small = _Exchange(1, [c8_scr], [gc_ref], sems[3:6])
        small.start()
        big.two_level_start()
        small.finish()
        landed = pltpu.make_async_copy(gc_ref, call_scr, loc_sem)
        landed.start()
        landed.wait()
        cv = call_scr[:, 0, :]
        sc = (cv * _sigmoid(cv)).astype(BF16)
        bias = b_ref[:, 0:ncol]
        for dev in range(1, N_DEV):
            bias = jnp.where(small.me == dev, b_ref[:, dev * ncol:(dev + 1) * ncol], bias)
        mod_scr[...] = _dot(sc, w_ref[...].astype(BF16)) + bias
        mods = _Exchange(1, [mod_scr], [gmod_ref], sems[6:9])
        mods.start()
        mods.finish()
        big.two_level_relay()
        big.two_level_finish()

    any_spec = pl.BlockSpec(memory_space=pl.ANY)
    vmem_spec = pl.BlockSpec(memory_space=pltpu.VMEM)
    return _pcall(
        body, name="prologue", in_specs=[vmem_spec, any_spec, vmem_spec, vmem_spec, vmem_spec],
        out_specs=[any_spec] * 4,
        out_shape=[_sds((N_DEV,) + c8_shape, F32), _sds((N_DEV,) + w_in_tb.shape, BF16),
                   _sds((N_DEV,) + cw8_shape, F32), _sds((N_DEV, N_DEV, ncol), F32)],
        scratch_shapes=[pltpu.VMEM(c8_shape, F32), pltpu.VMEM(cw8_shape, F32),
                        pltpu.VMEM((N_DEV,) + c8_shape, F32), pltpu.VMEM((N_DEV, ncol), F32),
                        pltpu.SemaphoreType.DMA] + _exchange_sems(2) + _exchange_sems(1) + _exchange_sems(1),
        compiler_params=pltpu.CompilerParams(vmem_limit_bytes=VMEM_LIMIT),
    )(c, w_in_tb, conv_w, w_cols, b_ada)


def _adamw(w, g, m, v):
    m2 = ADAM_B1 * m + (1.0 - ADAM_B1) * g
    v2 = ADAM_B2 * v + (1.0 - ADAM_B2) * (g * g)
    m_hat = m2 / (1.0 - ADAM_B1 ** ADAM_STEP)
    v_hat = v2 / (1.0 - ADAM_B2 ** ADAM_STEP)
    delta = -ADAM_LR * (m_hat / (jnp.sqrt(v_hat) + ADAM_EPS) + ADAM_WD * w)
    return delta, m2, v2


def _sum_adamw(parts, w, m, v, name):
    rws, cols = w.shape
    tr = next((t for t in (256, 176, 128) if rws % t == 0), rws)

    def body(p_ref, w_ref, m_ref, v_ref, g_ref, d_ref, mo_ref, vo_ref):
        g = p_ref[0].astype(F32)
        for dev in range(1, N_DEV):
            g = g + p_ref[dev].astype(F32)
        g_ref[...] = g
        d_ref[...], mo_ref[...], vo_ref[...] = _adamw(w_ref[...], g, m_ref[...], v_ref[...])

    blk = pl.BlockSpec((tr, cols), lambda i: (i, 0))
    return _pcall(
        body, name=name, grid=(rws // tr,),
        in_specs=[pl.BlockSpec((N_DEV, tr, cols), lambda i: (0, i, 0)), blk, blk, blk],
        out_specs=[blk] * 4, out_shape=[_sds((rws, cols), F32)] * 4, compiler_params=_params(),
    )(parts, w, m, v)


def _sum_adamw_rowwise(parts, w, m, v, name):
    rws, _, cols = w.shape

    def body(p_ref, w_ref, m_ref, v_ref, g_ref, d_ref, mo_ref, vo_ref, ins, outs, sems):
        loads = [pltpu.make_async_copy(src.at[:, 0, :], ins.at[i], sems.at[i])
                 for i, src in enumerate((w_ref, m_ref, v_ref))]
        for load in loads:
            load.start()
        g = p_ref[0].astype(F32)
        for dev in range(1, N_DEV):
            g = g + p_ref[dev].astype(F32)
        for load in loads:
            load.wait()
        outs[0] = g
        outs[1], outs[2], outs[3] = _adamw(ins[0], g, ins[1], ins[2])
        stores = [pltpu.make_async_copy(outs.at[i], dst.at[:, 0, :], sems.at[len(loads) + i])
                  for i, dst in enumerate((g_ref, d_ref, mo_ref, vo_ref))]
        for store in stores:
            store.start()
        for store in stores:
            store.wait()

    any_spec = pl.BlockSpec(memory_space=pl.ANY)
    return _pcall(
        body, name=name, grid=(1,),
        in_specs=[_const((N_DEV, rws, cols)), any_spec, any_spec, any_spec],
        out_specs=[any_spec] * 4, out_shape=[_sds((rws, 1, cols), F32)] * 4,
        scratch_shapes=[pltpu.VMEM((3, rws, cols), F32), pltpu.VMEM((4, rws, cols), F32),
                        pltpu.SemaphoreType.DMA((7,))],
        compiler_params=_params(),
    )(parts, w, m, v)


def _wada_adamw(c_all, dmod_cols, w, m, v):
    rws, cols = w.shape
    tr = 256

    def body(c_ref, dm_ref, w_ref, m_ref, v_ref, g_ref, d_ref, mo_ref, vo_ref):
        cv = c_ref[...]
        sc = (cv * _sigmoid(cv)).astype(BF16)
        g = _dot_tn(sc, dm_ref[...].astype(BF16))
        g_ref[...] = g
        d_ref[...], mo_ref[...], vo_ref[...] = _adamw(w_ref[...], g, m_ref[...], v_ref[...])

    blk = pl.BlockSpec((tr, cols), lambda i: (i, 0))
    return _pcall(
        body, name="wada_adamw", grid=(rws // tr,),
        in_specs=[pl.BlockSpec((N_DEV, tr), lambda i: (0, i)), pl.BlockSpec((N_DEV, cols), lambda i: (0, 0)),
                  blk, blk, blk],
        out_specs=[blk] * 4, out_shape=[_sds((rws, cols), F32)] * 4, compiler_params=_params(),
    )(c_all, dmod_cols, w, m, v)


SMALL_NAMES = ("b_ada", "norm1_w", "conv_w", "conv_b", "dt_bias", "a_log", "d_skip", "attn_sinks", "ssm_norm_w",
               "norm2_w", "final_norm_w")


def _small_grads(tot, me):
    shard = D // N_DEV
    conv = tot[10:10 + CONVK, 0:shard]
    for dev in range(1, N_DEV):
        conv = jnp.where(me == dev, tot[10:10 + CONVK, dev * shard:(dev + 1) * shard], conv)
    return [
        jnp.concatenate([tot[r:r + 1, :] for r in range(N_MOD)], axis=1),
        tot[6:7, :], conv, tot[9:10, :],
        tot[15:16, 0:NH], tot[15:16, LB:LB + NH], tot[15:16, 2 * LB:2 * LB + NH], tot[15:16, 3 * LB:3 * LB + NQ],
        tot[14:15, 0:SW], tot[7:8, :], tot[8:9, :],
    ]


def _small_adamw(packs, ws, ms, vs):
    k = len(ws)

    def body(p_ref, *refs):
        w_refs, m_refs, v_refs = refs[:k], refs[k:2 * k], refs[2 * k:3 * k]
        loss_ref = refs[3 * k]
        g_refs, d_refs, mo_refs, vo_refs = (refs[3 * k + 1 + j * k:3 * k + 1 + (j + 1) * k] for j in range(4))
        tot_ref = refs[7 * k + 1]
        tot = p_ref[0]
        for dev in range(1, N_DEV):
            tot = tot + p_ref[dev]
        tot_ref[...] = tot
        loss_ref[...] = jnp.zeros((1, 1), F32) + (0.5 / D) * jnp.sum(tot[16:17, :])
        me = 4 * lax.axis_index("x") + 2 * lax.axis_index("y") + lax.axis_index("c")
        for i, g in enumerate(_small_grads(tot_ref, me)):
            g_refs[i][...] = g
            d_refs[i][...], mo_refs[i][...], vo_refs[i][...] = _adamw(w_refs[i][...], g, m_refs[i][...], v_refs[i][...])

    shp = [_sds(w.shape, F32) for w in ws]
    outs = _pcall(body, name="adamw_small", out_shape=[_sds((1, 1), F32)] + shp * 4,
                  scratch_shapes=[pltpu.VMEM((PACK_ROWS, D), F32)])(packs, *ws, *ms, *vs)
    return outs[0], outs[1:1 + k], outs[1 + k:1 + 2 * k], outs[1 + 2 * k:1 + 3 * k], outs[1 + 3 * k:]


def kernel(x, c, positions, w_ada, b_ada, norm1_w, w_in, conv_w, conv_b, dt_bias, a_log, d_skip, attn_sinks, ssm_norm_w, w_out, norm2_w, w_gate_up, w_down, final_norm_w, loss_target, m_w_ada, m_b_ada, m_norm1_w, m_w_in, m_conv_w, m_conv_b, m_dt_bias, m_a_log, m_d_skip, m_attn_sinks, m_ssm_norm_w, m_w_out, m_norm2_w, m_w_gate_up, m_w_down, m_final_norm_w, v_w_ada, v_b_ada, v_norm1_w, v_w_in, v_conv_w, v_conv_b, v_dt_bias, v_a_log, v_d_skip, v_attn_sinks, v_ssm_norm_w, v_w_out, v_norm2_w, v_w_gate_up, v_w_down, v_final_norm_w):
    s = x.shape[1]
    me = 4 * lax.axis_index("x") + 2 * lax.axis_index("y") + lax.axis_index("c")
    ada_cols = N_MOD * D // N_DEV

    w_in_t = jnp.transpose(w_in[0])
    rowwise = lambda a: jnp.transpose(a, (2, 0, 1))
    w_gu_t, m_w_gu_t, v_w_gu_t = (jnp.transpose(w_gate_up[0]), jnp.transpose(m_w_gate_up[0]),
                                  jnp.transpose(v_w_gate_up[0]))
    g_c, g_in, g_cw, g_mod = _prologue(c, w_in_t.astype(BF16), conv_w[0], w_ada[0], b_ada)
    c_all = g_c[:, 0, :]
    w_in_f = jnp.pad(g_in.reshape(IN_PROJ, D), ((0, INP - IN_PROJ), (0, 0)))
    conv_w8 = jnp.transpose(g_cw, (1, 0, 2)).reshape(8, D)
    mod = lax.dynamic_index_in_dim(g_mod, me, axis=1, keepdims=False).reshape(N_MOD, D)
    mod8 = jnp.pad(mod, ((0, 8 - N_MOD), (0, 0)))

    half = HD // 2
    inv_freq = ROPE_THETA ** (-jnp.arange(half, dtype=F32) / half)
    invf = jnp.tile(inv_freq, LB // half).reshape(1, LB)
    lanes = lambda a: jnp.pad(a, ((0, 0), (0, LB - a.shape[1])))
    ssm_p = jnp.pad(jnp.concatenate([lanes(dt_bias), lanes(a_log), lanes(d_skip)], axis=0), ((0, 5), (0, 0)))
    sinks8 = jnp.broadcast_to(attn_sinks.reshape(NQ, 1), (NQ, LB))

    xs, tgt, fnw = x[0], loss_target[0], final_norm_w.reshape(1, D)

    q, k, v, z, xbc, dtr, h1, cos, sin = _inproj_fwd(xs, positions[0].reshape(s // LB, LB), invf, mod8, norm1_w, w_in_f)
    (attn, yn, y, hs, conv_u, dtv, acs, decay, probs, psink), (g_out, g_gu, g_down) = _mixer_fwd(
        q, k, v, sinks8, xbc, conv_w8, conv_b, dtr, ssm_p, z, ssm_norm_w,
        [w_out[0].astype(BF16), w_gu_t.astype(BF16), w_down[0].astype(BF16)])
    w_out_f = g_out.reshape(D, D)
    w_gu_f = g_gu.reshape(2 * DFF, D)
    w_down_f = g_down.reshape(DFF, D)
    x2, h2, mo, mix, gu, act, dx3, sm_f = _outproj_ffn_fwd_loss(attn, yn, xs, tgt, mod8, norm2_w, fnw, w_out_f, w_gu_f,
                                                                 w_down_f)

    dx2, dff, dgu, dmix, dattn, dyn, sm_b = _ffn_bwd(dx3, gu, x2, mo, mod8, norm2_w, w_gu_f, w_down_f, w_out_f)
    p_gu = _wgrad(dgu, h2, "wgrad_gate_up").reshape(N_DEV, 2 * DFF // N_DEV, D)
    p_down = _wgrad(act, dff, "wgrad_down").reshape(N_DEV, DFF // N_DEV, D)
    p_out = _wgrad(mix, dmix, "wgrad_out").reshape(N_DEV, D // N_DEV, D)
    dproj, dsink, sm_s, (r_gu, r_down, r_out) = _mixer_bwd(
        q, k, v, attn, dattn, cos, sin, probs, psink, dyn, y, z, xbc, conv_u, dtv, acs, decay, conv_w8, ssm_p,
        ssm_norm_w, hs, [p_gu, p_down, p_out])
    p_in = _wgrad(dproj, h1, "wgrad_in")[:IN_PROJ].reshape(N_DEV, IN_PROJ // N_DEV, D)
    gx, (r_in,), g_pack = _inproj_bwd(dproj, xs, dx2, mod8, norm1_w, w_in_f, [p_in], (sm_f, sm_b, sm_s, dsink))

    dmod_all = g_pack[:, 0:N_MOD, :].reshape(N_DEV, N_MOD * D)
    dmod_cols = lax.dynamic_slice(dmod_all, (0, me * ada_cols), (N_DEV, ada_cols))

    big = {
        "w_ada": _wada_adamw(c_all, dmod_cols, w_ada[0], m_w_ada[0], v_w_ada[0]),
        "w_in": [jnp.transpose(t, (1, 2, 0))[0] for t in
                 _sum_adamw_rowwise(r_in, rowwise(w_in), rowwise(m_w_in), rowwise(v_w_in), "adamw_in")],
        "w_out": _sum_adamw(r_out, w_out[0], m_w_out[0], v_w_out[0], "adamw_out"),
        "w_gate_up": [jnp.transpose(t) for t in _sum_adamw(r_gu, w_gu_t, m_w_gu_t, v_w_gu_t, "adamw_gate_up")],
        "w_down": _sum_adamw(r_down, w_down[0], m_w_down[0], v_w_down[0], "adamw_down"),
    }
    small_w = {"b_ada": b_ada, "norm1_w": norm1_w, "conv_w": conv_w[0], "conv_b": conv_b, "dt_bias": dt_bias,
               "a_log": a_log, "d_skip": d_skip, "attn_sinks": attn_sinks, "ssm_norm_w": ssm_norm_w,
               "norm2_w": norm2_w, "final_norm_w": final_norm_w.reshape(1, D)}
    small_m = {"b_ada": m_b_ada, "norm1_w": m_norm1_w, "conv_w": m_conv_w[0], "conv_b": m_conv_b,
               "dt_bias": m_dt_bias, "a_log": m_a_log, "d_skip": m_d_skip, "attn_sinks": m_attn_sinks,
               "ssm_norm_w": m_ssm_norm_w, "norm2_w": m_norm2_w, "final_norm_w": m_final_norm_w.reshape(1, D)}
    small_v = {"b_ada": v_b_ada, "norm1_w": v_norm1_w, "conv_w": v_conv_w[0], "conv_b": v_conv_b,
               "dt_bias": v_dt_bias, "a_log": v_a_log, "d_skip": v_d_skip, "attn_sinks": v_attn_sinks,
               "ssm_norm_w": v_ssm_norm_w, "norm2_w": v_norm2_w, "final_norm_w": v_final_norm_w.reshape(1, D)}
    loss, s_g, s_d, s_m, s_v = _small_adamw(g_pack, [small_w[k] for k in SMALL_NAMES],
                                            [small_m[k] for k in SMALL_NAMES], [small_v[k] for k in SMALL_NAMES])

    order = ["w_ada", "b_ada", "norm1_w", "w_in", "conv_w", "conv_b", "dt_bias", "a_log", "d_skip", "attn_sinks",
             "ssm_norm_w", "w_out", "norm2_w", "w_gate_up", "w_down", "final_norm_w"]
    lead = {"w_ada", "w_in", "conv_w", "w_out", "w_gate_up", "w_down"}
    grads, deltas, new_m, new_v = [], [], [], []
    for name in order:
        if name in big:
            g, d, m2, v2 = big[name]
        else:
            i = SMALL_NAMES.index(name)
            g, d, m2, v2 = s_g[i], s_d[i], s_m[i], s_v[i]
        if name in lead:
            g, d, m2, v2 = g[None], d[None], m2[None], v2[None]
        if name == "final_norm_w":
            g, d, m2, v2 = g.reshape(D), d.reshape(D), m2.reshape(D), v2.reshape(D)
        grads.append(g)
        deltas.append(d)
        new_m.append(m2)
        new_v.append(v2)
    return (loss.reshape(()), gx[None], *grads, *deltas, *new_m, *new_v)
```
